```python
import math
import jax, jax.numpy as jnp
from jax import lax
import numpy as np

D_MODEL = 1024
BATCH = 8
SEQ = 8192
DEPTH = 1

MIX_WIDTH = D_MODEL
HEAD_DIM = 64
ATTN_WIDTH = MIX_WIDTH // 2
CONV_WIDTH = MIX_WIDTH - ATTN_WIDTH
N_HEADS = ATTN_WIDTH // HEAD_DIM
N_KV_HEADS = 2
GROUP = N_HEADS // N_KV_HEADS
KV_WIDTH = N_KV_HEADS * HEAD_DIM
CONV_GROUPS = CONV_WIDTH // HEAD_DIM
CONV_K = 3
WINDOW = 128
BLOCK = 128
ROPE_THETA = 500000.0
ROT_DIM = HEAD_DIM // 4
D_FF = 4 * D_MODEL
NORM_EPS = 1e-6
NEG_INF = -1e30
SPLIT_SIZES = (ATTN_WIDTH, KV_WIDTH, KV_WIDTH, CONV_WIDTH, CONV_WIDTH, CONV_WIDTH)
IN_COLS = sum(SPLIT_SIZES)

kernel_name = "hymba_swa_sink_shortconv_sqrelu_sandwich"


def _rmsnorm(x, gain):
    x32 = x.astype(jnp.float32)
    y = x32 * lax.rsqrt(jnp.mean(x32 * x32, axis=-1, keepdims=True) + NORM_EPS)
    return (y * gain.astype(jnp.float32)).astype(x.dtype)


def _partial_rope(t, cos, sin):
    half = ROT_DIM // 2
    t1 = t[..., :half]
    t2 = t[..., half:ROT_DIM]
    rot = jnp.concatenate([t1 * cos - t2 * sin, t2 * cos + t1 * sin], axis=-1)
    return jnp.concatenate([rot, t[..., ROT_DIM:]], axis=-1)


def _rope_tables(seq, dtype):
    pos = jnp.arange(seq, dtype=jnp.float32)
    inv_freq = ROPE_THETA ** (-jnp.arange(0, ROT_DIM, 2, dtype=jnp.float32) / ROT_DIM)
    ang = pos[:, None] * inv_freq[None, :]
    cos = jnp.cos(ang)[None, :, None, :].astype(dtype)
    sin = jnp.sin(ang)[None, :, None, :].astype(dtype)
    return cos, sin


def _sliding_window_attention(q, k, v, sinks):
    b, s = q.shape[0], q.shape[1]
    nb = s // BLOCK
    qb = q.reshape(b, nb, BLOCK, N_KV_HEADS, GROUP, HEAD_DIM)

    def band(t):
        tp = jnp.pad(t, ((0, 0), (BLOCK, 0), (0, 0), (0, 0)))
        tp = tp.reshape(b, nb + 1, BLOCK, N_KV_HEADS, HEAD_DIM)
        return jnp.concatenate([tp[:, :-1], tp[:, 1:]], axis=2)

    kb, vb = band(k), band(v)
    scale = 1.0 / math.sqrt(HEAD_DIM)
    scores = jnp.einsum('bnqkgd,bnskd->bnkgqs', qb, kb).astype(jnp.float32) * scale

    qi = jnp.arange(BLOCK)[:, None]
    kj = jnp.arange(2 * BLOCK)[None, :]
    dist = BLOCK + qi - kj
    in_win = (dist >= 0) & (dist < WINDOW)
    key_abs = jnp.arange(nb)[:, None, None] * BLOCK - BLOCK + kj[None]
    valid = in_win[None] & (key_abs >= 0)
    scores = jnp.where(valid[None, :, None, None], scores, NEG_INF)

    sink = jnp.broadcast_to(sinks.astype(jnp.float32).reshape(1, 1, N_KV_HEADS, GROUP, 1, 1),
                            scores.shape[:-1] + (1,))
    probs = jax.nn.softmax(jnp.concatenate([scores, sink], axis=-1), axis=-1)[..., :-1]
    out = jnp.einsum('bnkgqs,bnskd->bnqkgd', probs.astype(v.dtype), vb)
    return out.reshape(b, s, N_HEADS * HEAD_DIM)


def _short_conv(u, w):
    s = u.shape[1]
    up = jnp.pad(u, ((0, 0), (CONV_K - 1, 0), (0, 0)))
    return sum(up[:, j:j + s, :] * w[j] for j in range(CONV_K))


def _fwd_setup_inputs(seed: int = 0) -> dict:
    key = jax.random.key(seed)
    ks = jax.random.split(key, 16)
    f32 = jnp.float32

    def gain(k, n):
        return 1.0 + 0.02 * jax.random.normal(k, (DEPTH, n), f32)

    return {
        "x": jax.random.normal(ks[0], (BATCH, SEQ, D_MODEL), f32),
        "pre_mix_norm": gain(ks[1], D_MODEL),
        "w_in": jax.random.normal(ks[2], (DEPTH, D_MODEL, IN_COLS), f32) * D_MODEL ** -0.5,
        "conv_w": jax.random.normal(ks[3], (DEPTH, CONV_K, CONV_WIDTH), f32) * CONV_K ** -0.5,
        "attn_sinks": 0.5 * jax.random.normal(ks[4], (DEPTH, N_HEADS), f32),
        "attn_group_norm": gain(ks[5], ATTN_WIDTH),
        "conv_group_norm": gain(ks[6], CONV_WIDTH),
        "w_out": jax.random.normal(ks[7], (DEPTH, MIX_WIDTH, D_MODEL), f32) * MIX_WIDTH ** -0.5,
        "post_mix_norm": gain(ks[8], D_MODEL),
        "pre_mlp_norm": gain(ks[9], D_MODEL),
        "w_up": jax.random.normal(ks[10], (DEPTH, D_MODEL, D_FF), f32) * D_MODEL ** -0.5,
        "w_down": jax.random.normal(ks[11], (DEPTH, D_FF, D_MODEL), f32) * D_FF ** -0.5,
        "post_mlp_norm": gain(ks[12], D_MODEL),
    }


def _fwd_reference(x, pre_mix_norm, w_in, conv_w, attn_sinks, attn_group_norm, conv_group_norm,
              w_out, post_mix_norm, pre_mlp_norm, w_up, w_down, post_mlp_norm):
    b, s, _ = x.shape
    cos, sin = _rope_tables(s, x.dtype)
    split_idx = list(np.cumsum(SPLIT_SIZES)[:-1])
    h = x
    for l in range(DEPTH):
        hn = _rmsnorm(h, pre_mix_norm[l])
        proj = jnp.einsum('bsd,de->bse', hn, w_in[l])
        q, k, v, gb, gc, xin = jnp.split(proj, split_idx, axis=-1)

        q = _partial_rope(q.reshape(b, s, N_HEADS, HEAD_DIM), cos, sin)
        k = _partial_rope(k.reshape(b, s, N_KV_HEADS, HEAD_DIM), cos, sin)
        v = v.reshape(b, s, N_KV_HEADS, HEAD_DIM)
        attn = _sliding_window_attention(q, k, v, attn_sinks[l])

        conv = gb * _short_conv(gc * xin, conv_w[l])

        mixed = jnp.concatenate([_rmsnorm(attn, attn_group_norm[l]),
                                 _rmsnorm(conv, conv_group_norm[l])], axis=-1)
        mix_out = jnp.einsum('bse,ed->bsd', mixed, w_out[l])
        h = h + _rmsnorm(mix_out, post_mix_norm[l])

        hn = _rmsnorm(h, pre_mlp_norm[l])
        up = jax.nn.relu(jnp.einsum('bsd,df->bsf', hn, w_up[l]))
        mlp_out = jnp.einsum('bsf,fd->bsd', up * up, w_down[l])
        h = h + _rmsnorm(mlp_out, post_mlp_norm[l])
    return h


import jax as _jax
import jax.numpy as _jnp

TWIN_FORMAT = 'train_step'
FWD_PARAMS = ['x', 'pre_mix_norm', 'w_in', 'conv_w', 'attn_sinks', 'attn_group_norm', 'conv_group_norm', 'w_out', 'post_mix_norm', 'pre_mlp_norm', 'w_up', 'w_down', 'post_mlp_norm']
TWIN_WEIGHTS = ['pre_mix_norm', 'w_in', 'conv_w', 'attn_sinks', 'attn_group_norm', 'conv_group_norm', 'w_out', 'post_mix_norm', 'pre_mlp_norm', 'w_up', 'w_down', 'post_mlp_norm']
TWIN_DIFF_INPUT = 'x'
TWIN_INPUTS = ['x', 'pre_mix_norm', 'w_in', 'conv_w', 'attn_sinks', 'attn_group_norm', 'conv_group_norm', 'w_out', 'post_mix_norm', 'pre_mlp_norm', 'w_up', 'w_down', 'post_mlp_norm', 'loss_target', 'm_pre_mix_norm', 'm_w_in', 'm_conv_w', 'm_attn_sinks', 'm_attn_group_norm', 'm_conv_group_norm', 'm_w_out', 'm_post_mix_norm', 'm_pre_mlp_norm', 'm_w_up', 'm_w_down', 'm_post_mlp_norm', 'v_pre_mix_norm', 'v_w_in', 'v_conv_w', 'v_attn_sinks', 'v_attn_group_norm', 'v_conv_group_norm', 'v_w_out', 'v_post_mix_norm', 'v_pre_mlp_norm', 'v_w_up', 'v_w_down', 'v_post_mlp_norm']
TWIN_OUTPUTS = ['loss', 'grad_x', 'grad_pre_mix_norm', 'grad_w_in', 'grad_conv_w', 'grad_attn_sinks', 'grad_attn_group_norm', 'grad_conv_group_norm', 'grad_w_out', 'grad_post_mix_norm', 'grad_pre_mlp_norm', 'grad_w_up', 'grad_w_down', 'grad_post_mlp_norm', 'delta_pre_mix_norm', 'delta_w_in', 'delta_conv_w', 'delta_attn_sinks', 'delta_attn_group_norm', 'delta_conv_group_norm', 'delta_w_out', 'delta_post_mix_norm', 'delta_pre_mlp_norm', 'delta_w_up', 'delta_w_down', 'delta_post_mlp_norm', 'new_m_pre_mix_norm', 'new_m_w_in', 'new_m_conv_w', 'new_m_attn_sinks', 'new_m_attn_group_norm', 'new_m_conv_group_norm', 'new_m_w_out', 'new_m_post_mix_norm', 'new_m_pre_mlp_norm', 'new_m_w_up', 'new_m_w_down', 'new_m_post_mlp_norm', 'new_v_pre_mix_norm', 'new_v_w_in', 'new_v_conv_w', 'new_v_attn_sinks', 'new_v_attn_group_norm', 'new_v_conv_group_norm', 'new_v_w_out', 'new_v_post_mix_norm', 'new_v_pre_mlp_norm', 'new_v_w_up', 'new_v_w_down', 'new_v_post_mlp_norm']
TWIN_LEAF_KINDS = {'loss': 'loss', 'grad_x': 'grad_x', 'grad_pre_mix_norm': 'grad_w', 'grad_w_in': 'grad_w', 'grad_conv_w': 'grad_w', 'grad_attn_sinks': 'grad_w', 'grad_attn_group_norm': 'grad_w', 'grad_conv_group_norm': 'grad_w', 'grad_w_out': 'grad_w', 'grad_post_mix_norm': 'grad_w', 'grad_pre_mlp_norm': 'grad_w', 'grad_w_up': 'grad_w', 'grad_w_down': 'grad_w', 'grad_post_mlp_norm': 'grad_w', 'delta_pre_mix_norm': 'delta_w', 'delta_w_in': 'delta_w', 'delta_conv_w': 'delta_w', 'delta_attn_sinks': 'delta_w', 'delta_attn_group_norm': 'delta_w', 'delta_conv_group_norm': 'delta_w', 'delta_w_out': 'delta_w', 'delta_post_mix_norm': 'delta_w', 'delta_pre_mlp_norm': 'delta_w', 'delta_w_up': 'delta_w', 'delta_w_down': 'delta_w', 'delta_post_mlp_norm': 'delta_w', 'new_m_pre_mix_norm': 'new_m', 'new_m_w_in': 'new_m', 'new_m_conv_w': 'new_m', 'new_m_attn_sinks': 'new_m', 'new_m_attn_group_norm': 'new_m', 'new_m_conv_group_norm': 'new_m', 'new_m_w_out': 'new_m', 'new_m_post_mix_norm': 'new_m', 'new_m_pre_mlp_norm': 'new_m', 'new_m_w_up': 'new_m', 'new_m_w_down': 'new_m', 'new_m_post_mlp_norm': 'new_m', 'new_v_pre_mix_norm': 'new_v', 'new_v_w_in': 'new_v', 'new_v_conv_w': 'new_v', 'new_v_attn_sinks': 'new_v', 'new_v_attn_group_norm': 'new_v', 'new_v_conv_group_norm': 'new_v', 'new_v_w_out': 'new_v', 'new_v_post_mix_norm': 'new_v', 'new_v_pre_mlp_norm': 'new_v', 'new_v_w_up': 'new_v', 'new_v_w_down': 'new_v', 'new_v_post_mlp_norm': 'new_v'}


def _forward(args):
    return _fwd_reference(*[args[k] for k in FWD_PARAMS])


def _output_shape():
    out = _jax.eval_shape(lambda: _forward(_fwd_setup_inputs(0)))
    return out.shape, out.dtype

N_MICROBATCH = 1
ADAM_LR = 0.001
ADAM_B1 = 0.9
ADAM_B2 = 0.999
ADAM_EPS = 1e-08
ADAM_WD = 0.01
ADAM_STEP = 10
PER_EXAMPLE_BATCH_AXIS = {'x': 0, 'loss_target': 0}
SHARED_INPUTS = []
_WEIGHT_DTYPES = {'pre_mix_norm': _jnp.float32, 'w_in': _jnp.float32, 'conv_w': _jnp.float32, 'attn_sinks': _jnp.float32, 'attn_group_norm': _jnp.float32, 'conv_group_norm': _jnp.float32, 'w_out': _jnp.float32, 'post_mix_norm': _jnp.float32, 'pre_mlp_norm': _jnp.float32, 'w_up': _jnp.float32, 'w_down': _jnp.float32, 'post_mlp_norm': _jnp.float32}
MOMENT_SCALE = {'pre_mix_norm': 1.352778e+00, 'w_in': 8.109687e-01, 'conv_w': 7.520126e-01, 'attn_sinks': 2.005270e-01, 'attn_group_norm': 1.476387e+00, 'conv_group_norm': 1.242251e+00, 'w_out': 1.101342e+00, 'post_mix_norm': 6.356116e+01, 'pre_mlp_norm': 1.243251e+00, 'w_up': 6.062290e-01, 'w_down': 1.255111e+00, 'post_mlp_norm': 6.617089e+01}


def _to_microbatches(a, axis):
    t = _jnp.moveaxis(a, axis, 0)
    t = t.reshape((N_MICROBATCH, t.shape[0] // N_MICROBATCH) + t.shape[1:])
    return _jnp.moveaxis(t, 1, axis + 1)


def setup_inputs(seed: int = 0) -> dict:
    inp = _fwd_setup_inputs(seed)
    key = _jax.random.fold_in(_jax.random.key(seed), 7919)
    shape, _ = _output_shape()
    out = dict(inp)
    out["loss_target"] = _jax.random.normal(_jax.random.fold_in(key, 0), shape, _jnp.float32)
    for i, name in enumerate(TWIN_WEIGHTS):
        w = inp[name].astype(_jnp.float32)
        if MOMENT_SCALE is None:
            s = _jnp.sqrt(_jnp.mean(_jnp.square(w)) + 1e-30)
        else:
            s = MOMENT_SCALE[name]
        km, kv = _jax.random.split(_jax.random.fold_in(key, i + 1))
        out[name] = w
        out["m_" + name] = s * _jax.random.normal(km, w.shape, _jnp.float32)
        out["v_" + name] = (s * s) * _jax.random.uniform(kv, w.shape, _jnp.float32, 0.5, 1.5)
    if N_MICROBATCH > 1:
        for name, axis in PER_EXAMPLE_BATCH_AXIS.items():
            out[name] = _to_microbatches(out[name], axis)
    return {'x': out['x'], 'pre_mix_norm': out['pre_mix_norm'], 'w_in': out['w_in'], 'conv_w': out['conv_w'], 'attn_sinks': out['attn_sinks'], 'attn_group_norm': out['attn_group_norm'], 'conv_group_norm': out['conv_group_norm'], 'w_out': out['w_out'], 'post_mix_norm': out['post_mix_norm'], 'pre_mlp_norm': out['pre_mlp_norm'], 'w_up': out['w_up'], 'w_down': out['w_down'], 'post_mlp_norm': out['post_mlp_norm'], 'loss_target': out['loss_target'], 'm_pre_mix_norm': out['m_pre_mix_norm'], 'm_w_in': out['m_w_in'], 'm_conv_w': out['m_conv_w'], 'm_attn_sinks': out['m_attn_sinks'], 'm_attn_group_norm': out['m_attn_group_norm'], 'm_conv_group_norm': out['m_conv_group_norm'], 'm_w_out': out['m_w_out'], 'm_post_mix_norm': out['m_post_mix_norm'], 'm_pre_mlp_norm': out['m_pre_mlp_norm'], 'm_w_up': out['m_w_up'], 'm_w_down': out['m_w_down'], 'm_post_mlp_norm': out['m_post_mlp_norm'], 'v_pre_mix_norm': out['v_pre_mix_norm'], 'v_w_in': out['v_w_in'], 'v_conv_w': out['v_conv_w'], 'v_attn_sinks': out['v_attn_sinks'], 'v_attn_group_norm': out['v_attn_group_norm'], 'v_conv_group_norm': out['v_conv_group_norm'], 'v_w_out': out['v_w_out'], 'v_post_mix_norm': out['v_post_mix_norm'], 'v_pre_mlp_norm': out['v_pre_mlp_norm'], 'v_w_up': out['v_w_up'], 'v_w_down': out['v_w_down'], 'v_post_mlp_norm': out['v_post_mlp_norm']}


def _loss(weights, diff, rest, loss_target):
    with _jax.named_scope("forward"):
        args = {**rest, TWIN_DIFF_INPUT: diff, **{k: w.astype(_WEIGHT_DTYPES[k]) for k, w in weights.items()}}
        y = _forward(args)
    with _jax.named_scope("loss_head"):
        err = _jnp.square(y.astype(_jnp.float32) - loss_target)
        return 0.5 * _jnp.sum(_jnp.mean(err, axis=-1)) if err.ndim else 0.5 * err


def _adamw(w, g, m, v):
    m = ADAM_B1 * m + (1.0 - ADAM_B1) * g
    v = ADAM_B2 * v + (1.0 - ADAM_B2) * _jnp.square(g)
    m_hat = m / (1.0 - ADAM_B1 ** ADAM_STEP)
    v_hat = v / (1.0 - ADAM_B2 ** ADAM_STEP)
    delta = -ADAM_LR * (m_hat / (_jnp.sqrt(v_hat) + ADAM_EPS) + ADAM_WD * w)
    return delta, m, v


def reference(x, pre_mix_norm, w_in, conv_w, attn_sinks, attn_group_norm, conv_group_norm, w_out, post_mix_norm, pre_mlp_norm, w_up, w_down, post_mlp_norm, loss_target, m_pre_mix_norm, m_w_in, m_conv_w, m_attn_sinks, m_attn_group_norm, m_conv_group_norm, m_w_out, m_post_mix_norm, m_pre_mlp_norm, m_w_up, m_w_down, m_post_mlp_norm, v_pre_mix_norm, v_w_in, v_conv_w, v_attn_sinks, v_attn_group_norm, v_conv_group_norm, v_w_out, v_post_mix_norm, v_pre_mlp_norm, v_w_up, v_w_down, v_post_mlp_norm):
    given = dict(x=x, pre_mix_norm=pre_mix_norm, w_in=w_in, conv_w=conv_w, attn_sinks=attn_sinks, attn_group_norm=attn_group_norm, conv_group_norm=conv_group_norm, w_out=w_out, post_mix_norm=post_mix_norm, pre_mlp_norm=pre_mlp_norm, w_up=w_up, w_down=w_down, post_mlp_norm=post_mlp_norm, loss_target=loss_target, m_pre_mix_norm=m_pre_mix_norm, m_w_in=m_w_in, m_conv_w=m_conv_w, m_attn_sinks=m_attn_sinks, m_attn_group_norm=m_attn_group_norm, m_conv_group_norm=m_conv_group_norm, m_w_out=m_w_out, m_post_mix_norm=m_post_mix_norm, m_pre_mlp_norm=m_pre_mlp_norm, m_w_up=m_w_up, m_w_down=m_w_down, m_post_mlp_norm=m_post_mlp_norm, v_pre_mix_norm=v_pre_mix_norm, v_w_in=v_w_in, v_conv_w=v_conv_w, v_attn_sinks=v_attn_sinks, v_attn_group_norm=v_attn_group_norm, v_conv_group_norm=v_conv_group_norm, v_w_out=v_w_out, v_post_mix_norm=v_post_mix_norm, v_pre_mlp_norm=v_pre_mlp_norm, v_w_up=v_w_up, v_w_down=v_w_down, v_post_mlp_norm=v_post_mlp_norm)
    weights = {n: given[n] for n in TWIN_WEIGHTS}
    shared = {n: given[n] for n in SHARED_INPUTS}
    per_example = {n: given[n] for n in ['x']}
    grad_fn = _jax.value_and_grad(_loss, argnums=(0, 1))

    def one_microbatch(ex, loss_target):
        ex = dict(ex)
        diff = ex.pop(TWIN_DIFF_INPUT)
        return grad_fn(weights, diff, {**shared, **ex}, loss_target)

    if N_MICROBATCH == 1:
        loss, (grad_w, grad_x) = one_microbatch(per_example, given["loss_target"])
    else:
        def body(carry, xs):
            loss_sum, grad_sum = carry
            l_k, (gw_k, gx_k) = one_microbatch(xs[0], xs[1])
            with _jax.named_scope("update"):
                return (loss_sum + l_k, _jax.tree.map(_jnp.add, grad_sum, gw_k)), gx_k

        init = (_jnp.zeros((), _jnp.float32), _jax.tree.map(_jnp.zeros_like, weights))
        (loss, grad_w), grad_x = _jax.lax.scan(body, init, (per_example, given["loss_target"]))
    with _jax.named_scope("update"):
        delta_w, new_m, new_v = {}, {}, {}
        for n in TWIN_WEIGHTS:
            delta_w[n], new_m[n], new_v[n] = _adamw(weights[n], grad_w[n], given["m_" + n], given["v_" + n])
    return (loss, grad_x, *[grad_w[n] for n in TWIN_WEIGHTS], *[delta_w[n] for n in TWIN_WEIGHTS],
            *[new_m[n] for n in TWIN_WEIGHTS], *[new_v[n] for n in TWIN_WEIGHTS])
```

```python
import functools
import math

import jax
import jax.numpy as jnp
from jax import lax
from jax.experimental import pallas as pl
from jax.experimental.pallas import tpu as pltpu

F32 = jnp.float32
BF16 = jnp.bfloat16

D_MODEL = 1024
HEAD_DIM = 64
Q_WIDTH = 512
KV_WIDTH = 128
CONV_WIDTH = 512
CONV_K = 3
D_FF = 4096
IN_COLS = 2304
QBLOCK = 128
ROT_DIM = 16
ROPE_THETA = 500000.0
NORM_EPS = 1e-6
NEG_INF = -1e30
N_CHIPS = 4

ADAM_LR = 0.001
ADAM_B1 = 0.9
ADAM_B2 = 0.999
ADAM_EPS = 1e-08
ADAM_WD = 0.01
ADAM_STEP = 10

H_UP, H_DOWN, H_OUT, H_IN = 512, 512, 128, 288
OFF_UP, OFF_DOWN, OFF_OUT, OFF_IN = 0, 512, 1024, 1152
H_ROWS = H_UP + H_DOWN + H_OUT + H_IN

TOKEN_TILE = 512
MLP_BWD_TOKEN_TILE = 256
WGRAD_TOKEN_TILE = 1024
VMEM_LIMIT_V7X = 56 * 1024 * 1024

MESH = pl.DeviceIdType.MESH
ANY = pl.BlockSpec(memory_space=pl.ANY)


def _params(sem=None, vmem=VMEM_LIMIT_V7X):
    return pltpu.CompilerParams(dimension_semantics=sem, vmem_limit_bytes=vmem)


def _resident(shape, index):
    return pl.BlockSpec(shape, lambda *_: index, pipeline_mode=pl.Buffered(1))


def _rms(v):
    return lax.rsqrt(jnp.mean(v * v, axis=-1, keepdims=True) + NORM_EPS)


def _norm_bwd(dy, gain, vhat, rstd):
    t = dy * gain
    return rstd * (t - vhat * jnp.mean(t * vhat, axis=-1, keepdims=True))


def _colsum(v):
    return jnp.sum(v, axis=0, keepdims=True)


def _dot_nt(a, b):
    return lax.dot_general(a, b, (((1,), (1,)), ((), ())), preferred_element_type=F32)


def _dot_tn(a, b):
    return lax.dot_general(a, b, (((0,), (0,)), ((), ())), preferred_element_type=F32)


def _dot(a, b):
    return jnp.dot(a, b, preferred_element_type=F32)


def _lane_lt64(shape):
    return lax.broadcasted_iota(jnp.int32, shape, 1) < HEAD_DIM


def _rope_tables(seq):
    half = ROT_DIM // 2
    pos = jnp.arange(seq, dtype=F32)
    inv_freq = ROPE_THETA ** (-jnp.arange(0, ROT_DIM, 2, dtype=F32) / ROT_DIM)
    ang = pos[:, None] * inv_freq[None, :]
    cos, sin = jnp.cos(ang), jnp.sin(ang)
    ones = jnp.ones((seq, HEAD_DIM - ROT_DIM), F32)
    zeros = jnp.zeros((seq, HEAD_DIM - half), F32)
    c_head = jnp.concatenate([cos, cos, ones], axis=1)
    sa_head = jnp.concatenate([-sin, zeros], axis=1)
    sb_head = jnp.concatenate([jnp.zeros((seq, half), F32), sin, jnp.zeros((seq, HEAD_DIM - ROT_DIM), F32)], axis=1)
    two = lambda t: jnp.concatenate([t, t], axis=1)
    return two(c_head), two(sa_head), two(sb_head)


def _rope(t, c, sa, sb):
    half = ROT_DIM // 2
    return t * c + pltpu.roll(t, 128 - half, 1) * sa + pltpu.roll(t, half, 1) * sb


def _rope_transposed(dt, c, sa, sb):
    half = ROT_DIM // 2
    return dt * c + pltpu.roll(dt * sa, half, 1) + pltpu.roll(dt * sb, 128 - half, 1)


def _in_proj(x, g_pre, wfull, rope):
    seq = x.shape[0]
    tb = TOKEN_TILE

    def body(x_ref, g_ref, w_ref, c_ref, sa_ref, sb_ref,
             q_ref, kd0_ref, kd1_ref, vd0_ref, vd1_ref, gb_ref, gc_ref, xin_ref, hn_ref):
        xv = x_ref[...]
        hn = (xv * _rms(xv) * g_ref[...]).astype(BF16)
        hn_ref[...] = hn
        proj = _dot_nt(hn, w_ref[...].reshape(IN_COLS, D_MODEL))
        c, sa, sb = c_ref[...], sa_ref[...], sb_ref[...]
        scale = 1.0 / math.sqrt(HEAD_DIM)
        for p in range(Q_WIDTH // 128):
            q_ref[:, 128 * p:128 * (p + 1)] = (_rope(proj[:, 128 * p:128 * (p + 1)], c, sa, sb) * scale).astype(BF16)
        k = _rope(proj[:, Q_WIDTH:Q_WIDTH + KV_WIDTH], c, sa, sb)
        v = proj[:, Q_WIDTH + KV_WIDTH:Q_WIDTH + 2 * KV_WIDTH]
        low = _lane_lt64(k.shape)
        k_sw, v_sw = pltpu.roll(k, HEAD_DIM, 1), pltpu.roll(v, HEAD_DIM, 1)
        kd0_ref[...] = jnp.where(low, k, k_sw).astype(BF16)
        kd1_ref[...] = jnp.where(low, k_sw, k).astype(BF16)
        vd0_ref[...] = jnp.where(low, v, v_sw).astype(BF16)
        vd1_ref[...] = jnp.where(low, v_sw, v).astype(BF16)
        base = Q_WIDTH + 2 * KV_WIDTH
        gb_ref[...] = proj[:, base:base + CONV_WIDTH]
        gc_ref[...] = proj[:, base + CONV_WIDTH:base + 2 * CONV_WIDTH]
        xin_ref[...] = proj[:, base + 2 * CONV_WIDTH:base + 3 * CONV_WIDTH]

    tile = lambda w: pl.BlockSpec((tb, w), lambda i: (i, 0))
    sds = jax.ShapeDtypeStruct
    return pl.pallas_call(
        body, name="in_proj", grid=(seq // tb,),
        in_specs=[tile(D_MODEL), _resident((1, D_MODEL), (0, 0)),
                  _resident((2 * N_CHIPS, H_IN, D_MODEL), (0, OFF_IN // H_IN, 0)),
                  tile(128), tile(128), tile(128)],
        out_specs=[tile(Q_WIDTH), tile(128), tile(128), tile(128), tile(128),
                   tile(CONV_WIDTH), tile(CONV_WIDTH), tile(CONV_WIDTH), tile(D_MODEL)],
        out_shape=[sds((seq, Q_WIDTH), BF16)] + [sds((seq, 128), BF16)] * 4
                  + [sds((seq, CONV_WIDTH), F32)] * 3 + [sds((seq, D_MODEL), BF16)],
        compiler_params=_params(("arbitrary",)),
    )(x, g_pre, wfull, *rope)


def _attn_masks(i):
    shape = (4 * QBLOCK, 2 * QBLOCK)
    row = lax.broadcasted_iota(jnp.int32, shape, 0)
    col = lax.broadcasted_iota(jnp.int32, shape, 1)
    qi = row & (QBLOCK - 1)
    valid = (col > qi) & (col <= qi + QBLOCK) & ((col >= QBLOCK) | (i > 0))
    return valid


def _stack_heads(pair0, pair1):
    low = _lane_lt64(pair0.shape)
    zero = jnp.zeros_like(pair0)
    return jnp.concatenate([jnp.where(low, pair0, zero), jnp.where(low, zero, pair0),
                            jnp.where(low, pair1, zero), jnp.where(low, zero, pair1)], axis=0)


def _unstack_heads(stacked):
    low = _lane_lt64((QBLOCK, 128))
    pair0 = jnp.where(low, stacked[0:QBLOCK], stacked[QBLOCK:2 * QBLOCK])
    pair1 = jnp.where(low, stacked[2 * QBLOCK:3 * QBLOCK], stacked[3 * QBLOCK:4 * QBLOCK])
    return pair0, pair1


def _sink_column(sink_ref, kv_head):
    row = lax.broadcasted_iota(jnp.int32, (4 * QBLOCK, 1), 0)
    s = [sink_ref[0, 4 * kv_head + j] for j in range(4)]
    return jnp.where(row < QBLOCK, s[0], jnp.where(row < 2 * QBLOCK, s[1], jnp.where(row < 3 * QBLOCK, s[2], s[3])))


def _band(ref, i):
    prev = pl.multiple_of(jnp.maximum(i - 1, 0) * QBLOCK, QBLOCK)
    own = pl.multiple_of(i * QBLOCK, QBLOCK)
    return jnp.concatenate([ref[pl.ds(prev, QBLOCK), :], ref[pl.ds(own, QBLOCK), :]], axis=0), prev, own


def _softmax_with_sink(s, sink_col):
    m = jnp.maximum(jnp.max(s, axis=-1, keepdims=True), sink_col)
    p = jnp.exp(s - m)
    e_sink = jnp.exp(sink_col - m)
    inv_l = 1.0 / (jnp.sum(p, axis=-1, keepdims=True) + e_sink)
    return p, e_sink, inv_l


def _attention_fwd(q, kd0, kd1, vd0, vd1, sinks):
    seq = q.shape[0]

    def body(sink_ref, q_ref, kd0_ref, kd1_ref, vd0_ref, vd1_ref, o_ref):
        i = pl.program_id(0)
        valid = _attn_masks(i)
        for kv_head, (k_ref, v_ref) in enumerate(((kd0_ref, vd0_ref), (kd1_ref, vd1_ref))):
            kband, _, _ = _band(k_ref, i)
            vband, _, _ = _band(v_ref, i)
            base = 256 * kv_head
            qm = _stack_heads(q_ref[:, base:base + 128], q_ref[:, base + 128:base + 256])
            s = jnp.where(valid, _dot_nt(qm, kband), NEG_INF)
            p, _, inv_l = _softmax_with_sink(s, _sink_column(sink_ref, kv_head))
            o = _dot(p.astype(BF16), vband) * inv_l
            pair0, pair1 = _unstack_heads(o)
            o_ref[:, base:base + 128] = pair0
            o_ref[:, base + 128:base + 256] = pair1

    blk = pl.BlockSpec((QBLOCK, Q_WIDTH), lambda i: (i, 0))
    full = _resident((seq, 128), (0, 0))
    return pl.pallas_call(
        body, name="attention_fwd", grid=(seq // QBLOCK,),
        in_specs=[pl.BlockSpec(memory_space=pltpu.SMEM), blk, full, full, full, full],
        out_specs=blk,
        out_shape=jax.ShapeDtypeStruct((seq, Q_WIDTH), F32),
        compiler_params=_params(("arbitrary",)),
    )(sinks, q, kd0, kd1, vd0, vd1)


def _conv_parts(gb, gc, xin, gc_halo, xin_halo, conv_w, first):
    tb = gb.shape[0]
    u = gc * xin
    u_halo = jnp.where(first, 0.0, gc_halo * xin_halo)
    ext = jnp.concatenate([u_halo, u], axis=0)
    u1 = pltpu.roll(ext, 1, 0)[8:8 + tb]
    u2 = pltpu.roll(ext, 2, 0)[8:8 + tb]
    y = conv_w[0:1, :] * u2 + conv_w[1:2, :] * u1 + conv_w[2:3, :] * u
    return u, u1, u2, y


def _halo_prev(tb, w):
    return pl.BlockSpec((8, w), lambda i: (jnp.maximum(i * (tb // 8) - 1, 0), 0))


def _mix_out(x, attn, gb, gc, xin, conv_w, g_attn, g_conv, g_post_mix, wfull):
    seq = x.shape[0]
    tb = TOKEN_TILE

    def body(x_ref, a_ref, gb_ref, gc_ref, xin_ref, gch_ref, xinh_ref, cw_ref, ga_ref, gcn_ref, gpm_ref, w_ref,
             h_ref, mix_ref, mixed_ref):
        first = pl.program_id(0) == 0
        _, _, _, y = _conv_parts(gb_ref[...], gc_ref[...], xin_ref[...], gch_ref[...], xinh_ref[...], cw_ref[...], first)
        conv = gb_ref[...] * y
        a = a_ref[...]
        mixed_ref[:, 0:Q_WIDTH] = (a * _rms(a) * ga_ref[...]).astype(BF16)
        mixed_ref[:, Q_WIDTH:] = (conv * _rms(conv) * gcn_ref[...]).astype(BF16)
        mix = _dot(mixed_ref[...], w_ref[...].reshape(D_MODEL, D_MODEL))
        mix_ref[...] = mix
        h_ref[...] = x_ref[...] + mix * _rms(mix) * gpm_ref[...]

    tile = lambda w: pl.BlockSpec((tb, w), lambda i: (i, 0))
    sds = jax.ShapeDtypeStruct
    return pl.pallas_call(
        body, name="mix_out", grid=(seq // tb,),
        in_specs=[tile(D_MODEL), tile(Q_WIDTH), tile(CONV_WIDTH), tile(CONV_WIDTH), tile(CONV_WIDTH),
                  _halo_prev(tb, CONV_WIDTH), _halo_prev(tb, CONV_WIDTH),
                  _resident((CONV_K, CONV_WIDTH), (0, 0)), _resident((1, Q_WIDTH), (0, 0)),
                  _resident((1, CONV_WIDTH), (0, 0)), _resident((1, D_MODEL), (0, 0)),
                  _resident((2 * N_CHIPS, H_OUT, D_MODEL), (0, OFF_OUT // H_OUT, 0))],
        out_specs=[tile(D_MODEL), tile(D_MODEL), tile(D_MODEL)],
        out_shape=[sds((seq, D_MODEL), F32), sds((seq, D_MODEL), F32), sds((seq, D_MODEL), BF16)],
        compiler_params=_params(("arbitrary",)),
    )(x, attn, gb, gc, xin, gc, xin, conv_w, g_attn, g_conv, g_post_mix, wfull)


def _mlp_loss(h, target, g_pre_mlp, g_post_mlp, wfull):
    seq = h.shape[0]
    tb = TOKEN_TILE

    def body(h_ref, t_ref, g2_ref, g4_ref, wup_ref, wdown_ref,
             up_ref, hn2_ref, dout_ref, dmlp_ref, loss_ref, dg4_ref, act_ref):
        @pl.when(pl.program_id(0) == 0)
        def _():
            loss_ref[...] = jnp.zeros_like(loss_ref)
            dg4_ref[...] = jnp.zeros_like(dg4_ref)

        hv = h_ref[...]
        hn2 = (hv * _rms(hv) * g2_ref[...]).astype(BF16)
        hn2_ref[...] = hn2
        for j in range(N_CHIPS):
            up = _dot(hn2[:, :H_UP], wup_ref[2 * j]) + _dot(hn2[:, H_UP:], wup_ref[2 * j + 1])
            up = jnp.maximum(up, 0.0)
            up_ref[:, 1024 * j:1024 * (j + 1)] = up.astype(BF16)
            act_ref[:, 1024 * j:1024 * (j + 1)] = (up * up).astype(BF16)
        mlp = _dot(act_ref[...], wdown_ref[...].reshape(D_FF, D_MODEL))
        rstd = _rms(mlp)
        zhat = mlp * rstd
        diff = hv + zhat * g4_ref[...] - t_ref[...]
        loss_ref[...] += jnp.sum(jnp.sum(diff * diff, axis=1, keepdims=True), axis=0, keepdims=True)
        dout = diff * (1.0 / D_MODEL)
        dout_ref[...] = dout
        dg4_ref[...] += _colsum(dout * zhat)
        dmlp_ref[...] = _norm_bwd(dout, g4_ref[...], zhat, rstd).astype(BF16)

    tile = lambda w: pl.BlockSpec((tb, w), lambda i: (i, 0))
    sds = jax.ShapeDtypeStruct
    return pl.pallas_call(
        body, name="mlp_loss", grid=(seq // tb,),
        in_specs=[tile(D_MODEL), tile(D_MODEL), _resident((1, D_MODEL), (0, 0)), _resident((1, D_MODEL), (0, 0)),
                  _resident((2 * N_CHIPS, H_UP, D_MODEL), (0, OFF_UP // H_UP, 0)),
                  _resident((2 * N_CHIPS, H_DOWN, D_MODEL), (0, OFF_DOWN // H_DOWN, 0))],
        out_specs=[tile(D_FF), tile(D_MODEL), tile(D_MODEL), tile(D_MODEL),
                   pl.BlockSpec((1, 1), lambda i: (0, 0)), pl.BlockSpec((1, D_MODEL), lambda i: (0, 0))],
        out_shape=[sds((seq, D_FF), BF16), sds((seq, D_MODEL), BF16), sds((seq, D_MODEL), F32),
                   sds((seq, D_MODEL), BF16), sds((1, 1), F32), sds((1, D_MODEL), F32)],
        scratch_shapes=[pltpu.VMEM((tb, D_FF), BF16)],
        compiler_params=_params(("arbitrary",)),
    )(h, target, g_pre_mlp, g_post_mlp, wfull, wfull)


def _mlp_bwd(dmlp, up, h, dout, mix, g_pre_mlp, g_post_mix, wfull):
    seq = h.shape[0]
    tb = MLP_BWD_TOKEN_TILE

    def body(dmlp_ref, up_ref, h_ref, dout_ref, mix_ref, g2_ref, gpm_ref, wup_ref, wdown_ref,
             dup_ref, dh_ref, dmix_ref, dg2_ref, dgpm_ref):
        @pl.when(pl.program_id(0) == 0)
        def _():
            dg2_ref[...] = jnp.zeros_like(dg2_ref)
            dgpm_ref[...] = jnp.zeros_like(dgpm_ref)

        dmlp_v = dmlp_ref[...]
        halves = [None, None]
        for j in range(N_CHIPS):
            cols = slice(1024 * j, 1024 * (j + 1))
            dact = jnp.concatenate([_dot_nt(dmlp_v, wdown_ref[2 * j]), _dot_nt(dmlp_v, wdown_ref[2 * j + 1])], axis=1)
            dup = (dact * (2.0 * up_ref[:, cols].astype(F32))).astype(BF16)
            dup_ref[:, cols] = dup
            for half in range(2):
                part = _dot_nt(dup, wup_ref[2 * j + half])
                halves[half] = part if j == 0 else halves[half] + part
        dhn2 = jnp.concatenate(halves, axis=1)
        hv = h_ref[...]
        r2 = _rms(hv)
        hhat = hv * r2
        dg2_ref[...] += _colsum(dhn2 * hhat)
        dh = dout_ref[...] + _norm_bwd(dhn2, g2_ref[...], hhat, r2)
        dh_ref[...] = dh
        mix_v = mix_ref[...]
        rz = _rms(mix_v)
        zhat = mix_v * rz
        dgpm_ref[...] += _colsum(dh * zhat)
        dmix_ref[...] = _norm_bwd(dh, gpm_ref[...], zhat, rz).astype(BF16)

    tile = lambda w: pl.BlockSpec((tb, w), lambda i: (i, 0))
    vec = pl.BlockSpec((1, D_MODEL), lambda i: (0, 0))
    sds = jax.ShapeDtypeStruct
    return pl.pallas_call(
        body, name="mlp_bwd", grid=(seq // tb,),
        in_specs=[tile(D_MODEL), tile(D_FF), tile(D_MODEL), tile(D_MODEL), tile(D_MODEL),
                  _resident((1, D_MODEL), (0, 0)), _resident((1, D_MODEL), (0, 0)),
                  _resident((2 * N_CHIPS, H_UP, D_MODEL), (0, OFF_UP // H_UP, 0)),
                  _resident((2 * N_CHIPS, H_DOWN, D_MODEL), (0, OFF_DOWN // H_DOWN, 0))],
        out_specs=[tile(D_FF), tile(D_MODEL), tile(D_MODEL), vec, vec],
        out_shape=[sds((seq, D_FF), BF16), sds((seq, D_MODEL), F32), sds((seq, D_MODEL), BF16),
                   sds((1, D_MODEL), F32), sds((1, D_MODEL), F32)],
        compiler_params=_params(("arbitrary",)),
    )(dmlp, up, h, dout, mix, g_pre_mlp, g_post_mix, wfull, wfull)


def _mix_bwd(dmix, attn, gb, gc, xin, conv_w, g_attn, g_conv, wfull):
    seq = attn.shape[0]
    tb = TOKEN_TILE

    def body(dmix_ref, a_ref, gb_ref, gc_ref, xin_ref, gch_ref, xinh_ref, cw_ref, ga_ref, gcn_ref, w_ref,
             dattn_ref, dgb_ref, dy_ref, dga_ref, dgcn_ref, dcw_ref):
        first = pl.program_id(0) == 0

        @pl.when(first)
        def _():
            dga_ref[...] = jnp.zeros_like(dga_ref)
            dgcn_ref[...] = jnp.zeros_like(dgcn_ref)
            dcw_ref[...] = jnp.zeros_like(dcw_ref)

        dmixed = _dot_nt(dmix_ref[...], w_ref[...].reshape(D_MODEL, D_MODEL))
        a = a_ref[...]
        ra = _rms(a)
        ahat = a * ra
        dan = dmixed[:, 0:Q_WIDTH]
        dga_ref[...] += _colsum(dan * ahat)
        dattn_ref[...] = _norm_bwd(dan, ga_ref[...], ahat, ra).astype(BF16)
        gbv = gb_ref[...]
        u, u1, u2, y = _conv_parts(gbv, gc_ref[...], xin_ref[...], gch_ref[...], xinh_ref[...], cw_ref[...], first)
        conv = gbv * y
        rc = _rms(conv)
        chat = conv * rc
        dcn = dmixed[:, Q_WIDTH:]
        dgcn_ref[...] += _colsum(dcn * chat)
        dconv = _norm_bwd(dcn, gcn_ref[...], chat, rc)
        dgb_ref[...] = dconv * y
        dy = dconv * gbv
        dy_ref[...] = dy
        dcw_ref[0:1, :] += _colsum(dy * u2)
        dcw_ref[1:2, :] += _colsum(dy * u1)
        dcw_ref[2:3, :] += _colsum(dy * u)

    tile = lambda w: pl.BlockSpec((tb, w), lambda i: (i, 0))
    sds = jax.ShapeDtypeStruct
    return pl.pallas_call(
        body, name="mix_bwd", grid=(seq // tb,),
        in_specs=[tile(D_MODEL), tile(Q_WIDTH), tile(CONV_WIDTH), tile(CONV_WIDTH), tile(CONV_WIDTH),
                  _halo_prev(tb, CONV_WIDTH), _halo_prev(tb, CONV_WIDTH),
                  _resident((CONV_K, CONV_WIDTH), (0, 0)), _resident((1, Q_WIDTH), (0, 0)),
                  _resident((1, CONV_WIDTH), (0, 0)),
                  _resident((2 * N_CHIPS, H_OUT, D_MODEL), (0, OFF_OUT // H_OUT, 0))],
        out_specs=[tile(Q_WIDTH), tile(CONV_WIDTH), tile(CONV_WIDTH),
                   pl.BlockSpec((1, Q_WIDTH), lambda i: (0, 0)), pl.BlockSpec((1, CONV_WIDTH), lambda i: (0, 0)),
                   pl.BlockSpec((CONV_K, CONV_WIDTH), lambda i: (0, 0))],
        out_shape=[sds((seq, Q_WIDTH), BF16), sds((seq, CONV_WIDTH), F32), sds((seq, CONV_WIDTH), F32),
                   sds((1, Q_WIDTH), F32), sds((1, CONV_WIDTH), F32), sds((CONV_K, CONV_WIDTH), F32)],
        compiler_params=_params(("arbitrary",)),
    )(dmix, attn, gb, gc, xin, gc, xin, conv_w, g_attn, g_conv, wfull)


def _attention_bwd(q, dattn, attn, kd0, kd1, vd0, vd1, sinks):
    seq = q.shape[0]

    def body(sink_ref, q_ref, do_ref, o_ref, kd0_ref, kd1_ref, vd0_ref, vd1_ref,
             dq_ref, dk0_ref, dk1_ref, dv0_ref, dv1_ref, dsink_ref):
        i = pl.program_id(0)

        @pl.when(i == 0)
        def _():
            for r in (dk0_ref, dk1_ref, dv0_ref, dv1_ref, dsink_ref):
                r[...] = jnp.zeros_like(r)

        valid = _attn_masks(i)
        lane = lax.broadcasted_iota(jnp.int32, (1, 128), 1)
        dsink = jnp.zeros((1, 128), F32)
        for kv_head, (k_ref, v_ref, dk_ref, dv_ref) in enumerate(
                ((kd0_ref, vd0_ref, dk0_ref, dv0_ref), (kd1_ref, vd1_ref, dk1_ref, dv1_ref))):
            kband, prev, own = _band(k_ref, i)
            vband, _, _ = _band(v_ref, i)
            base = 256 * kv_head
            qm = _stack_heads(q_ref[:, base:base + 128], q_ref[:, base + 128:base + 256])
            dom = _stack_heads(do_ref[:, base:base + 128], do_ref[:, base + 128:base + 256])
            om = _stack_heads(o_ref[:, base:base + 128], o_ref[:, base + 128:base + 256])
            s = jnp.where(valid, _dot_nt(qm, kband), NEG_INF)
            p, e_sink, inv_l = _softmax_with_sink(s, _sink_column(sink_ref, kv_head))
            p = p * inv_l
            delta = jnp.sum(dom.astype(F32) * om, axis=-1, keepdims=True)
            ds = (p * (_dot_nt(dom, vband) - delta)).astype(BF16)
            sink_term = -(e_sink * inv_l) * delta
            for j in range(4):
                part = jnp.sum(sink_term[QBLOCK * j:QBLOCK * (j + 1)], axis=0, keepdims=True)
                dsink = dsink + jnp.where(lane == 4 * kv_head + j, part, 0.0)
            pair0, pair1 = _unstack_heads(_dot(ds, kband))
            dq_ref[:, base:base + 128] = pair0
            dq_ref[:, base + 128:base + 256] = pair1
            dkd = _dot_tn(ds, qm)
            dkd = dkd + pltpu.roll(dkd, HEAD_DIM, 1)
            dvd = _dot_tn(p.astype(BF16), dom)
            dvd = dvd + pltpu.roll(dvd, HEAD_DIM, 1)
            dk_ref[pl.ds(prev, QBLOCK), :] += dkd[0:QBLOCK]
            dk_ref[pl.ds(own, QBLOCK), :] += dkd[QBLOCK:]
            dv_ref[pl.ds(prev, QBLOCK), :] += dvd[0:QBLOCK]
            dv_ref[pl.ds(own, QBLOCK), :] += dvd[QBLOCK:]
        dsink_ref[...] += dsink

    blk = pl.BlockSpec((QBLOCK, Q_WIDTH), lambda i: (i, 0))
    full = _resident((seq, 128), (0, 0))
    acc = pl.BlockSpec((seq, 128), lambda i: (0, 0))
    sds = jax.ShapeDtypeStruct
    return pl.pallas_call(
        body, name="attention_bwd", grid=(seq // QBLOCK,),
        in_specs=[pl.BlockSpec(memory_space=pltpu.SMEM), blk, blk, blk, full, full, full, full],
        out_specs=[blk, acc, acc, acc, acc, pl.BlockSpec((1, 128), lambda i: (0, 0))],
        out_shape=[sds((seq, Q_WIDTH), F32)] + [sds((seq, 128), F32)] * 4 + [sds((1, 128), F32)],
        compiler_params=_params(("arbitrary",)),
    )(sinks, q, dattn, attn, kd0, kd1, vd0, vd1)


def _in_proj_bwd(dq, dk0, dk1, dv0, dv1, dgb, dy, gc, xin, conv_w, x, dh, g_pre, wfull, rope):
    seq = x.shape[0]
    tb = TOKEN_TILE
    n_tiles = seq // tb

    def body(dq_ref, dk0_ref, dk1_ref, dv0_ref, dv1_ref, dgb_ref, dy_ref, dyh_ref, gc_ref, xin_ref, cw_ref,
             x_ref, dh_ref, g_ref, w_ref, c_ref, sa_ref, sb_ref,
             dproj_ref, gx_ref, dg_ref):
        i = pl.program_id(0)

        @pl.when(i == 0)
        def _():
            dg_ref[...] = jnp.zeros_like(dg_ref)

        c, sa, sb = c_ref[...], sa_ref[...], sb_ref[...]
        scale = 1.0 / math.sqrt(HEAD_DIM)
        for p in range(Q_WIDTH // 128):
            dproj_ref[:, 128 * p:128 * (p + 1)] = _rope_transposed(
                dq_ref[:, 128 * p:128 * (p + 1)] * scale, c, sa, sb).astype(BF16)
        low = _lane_lt64((tb, 128))
        dk = jnp.where(low, dk0_ref[...], dk1_ref[...])
        dproj_ref[:, Q_WIDTH:Q_WIDTH + KV_WIDTH] = _rope_transposed(dk, c, sa, sb).astype(BF16)
        dproj_ref[:, Q_WIDTH + KV_WIDTH:Q_WIDTH + 2 * KV_WIDTH] = jnp.where(low, dv0_ref[...], dv1_ref[...]).astype(BF16)
        base = Q_WIDTH + 2 * KV_WIDTH
        dproj_ref[:, base:base + CONV_WIDTH] = dgb_ref[...].astype(BF16)
        dy = dy_ref[...]
        ext = jnp.concatenate([dy, jnp.where(i == n_tiles - 1, 0.0, dyh_ref[...])], axis=0)
        dy1 = pltpu.roll(ext, tb + 8 - 1, 0)[0:tb]
        dy2 = pltpu.roll(ext, tb + 8 - 2, 0)[0:tb]
        cw = cw_ref[...]
        du = cw[2:3, :] * dy + cw[1:2, :] * dy1 + cw[0:1, :] * dy2
        dproj_ref[:, base + CONV_WIDTH:base + 2 * CONV_WIDTH] = (du * xin_ref[...]).astype(BF16)
        dproj_ref[:, base + 2 * CONV_WIDTH:] = (du * gc_ref[...]).astype(BF16)
        dhn = _dot(dproj_ref[...], w_ref[...].reshape(IN_COLS, D_MODEL))
        xv = x_ref[...]
        r = _rms(xv)
        xhat = xv * r
        dg_ref[...] += _colsum(dhn * xhat)
        gx_ref[...] = dh_ref[...] + _norm_bwd(dhn, g_ref[...], xhat, r)

    tile = lambda w: pl.BlockSpec((tb, w), lambda i: (i, 0))
    halo_next = pl.BlockSpec((8, CONV_WIDTH), lambda i: (jnp.minimum((i + 1) * (tb // 8), seq // 8 - 1), 0))
    sds = jax.ShapeDtypeStruct
    return pl.pallas_call(
        body, name="in_proj_bwd", grid=(n_tiles,),
        in_specs=[tile(Q_WIDTH), tile(128), tile(128), tile(128), tile(128), tile(CONV_WIDTH), tile(CONV_WIDTH), halo_next,
                  tile(CONV_WIDTH), tile(CONV_WIDTH), _resident((CONV_K, CONV_WIDTH), (0, 0)),
                  tile(D_MODEL), tile(D_MODEL), _resident((1, D_MODEL), (0, 0)),
                  _resident((2 * N_CHIPS, H_IN, D_MODEL), (0, OFF_IN // H_IN, 0)),
                  tile(128), tile(128), tile(128)],
        out_specs=[tile(IN_COLS), tile(D_MODEL), pl.BlockSpec((1, D_MODEL), lambda i: (0, 0))],
        out_shape=[sds((seq, IN_COLS), BF16), sds((seq, D_MODEL), F32), sds((1, D_MODEL), F32)],
        compiler_params=_params(("arbitrary",)),
    )(dq, dk0, dk1, dv0, dv1, dgb, dy, dy, gc, xin, conv_w, x, dh, g_pre, wfull, *rope)


G_SHAPE = (N_CHIPS, 2, H_ROWS, D_MODEL)


def _wgrad(name, a, b, *, a_cols, per_chip, h_rows, row_off, square_a, grads):
    seq = a.shape[0]
    bt = WGRAD_TOKEN_TILE
    n_k = seq // bt
    chips_per_step = 1 if per_chip else N_CHIPS
    m = chips_per_step * 2 * h_rows
    a_wide = per_chip and a.shape[1] > a_cols
    b_wide = per_chip and b.shape[1] > D_MODEL

    def body(*refs):
        a_ref, b_ref = refs[0], refs[1]
        g_ref, acc_ref = refs[-2], refs[-1]
        k = pl.program_id(1)

        @pl.when(k == 0)
        def _():
            acc_ref[...] = jnp.zeros_like(acc_ref)

        av = a_ref[...]
        if square_a:
            av = (av.astype(F32) * av.astype(F32)).astype(BF16)
        acc_ref[...] += _dot_tn(av, b_ref[...])

        @pl.when(k == n_k - 1)
        def _():
            for cidx in range(chips_per_step):
                for half in range(2):
                    r0 = (2 * cidx + half) * h_rows
                    g_ref[cidx, half] = acc_ref[r0:r0 + h_rows, :]

    a_spec = pl.BlockSpec((bt, a_cols), (lambda j, k: (k, j)) if a_wide else (lambda j, k: (k, 0)))
    b_spec = pl.BlockSpec((bt, D_MODEL), (lambda j, k: (k, j)) if b_wide else (lambda j, k: (k, 0)))
    g_spec = pl.BlockSpec((chips_per_step, 2, h_rows, D_MODEL), lambda j, k: (j, 0, row_off // h_rows, 0))
    in_specs, operands, aliases = [a_spec, b_spec], [a, b], {}
    if grads is not None:
        in_specs.append(ANY)
        operands.append(grads)
        aliases = {2: 0}
    return pl.pallas_call(
        body, name=name, grid=(N_CHIPS if per_chip else 1, n_k),
        in_specs=in_specs, out_specs=g_spec,
        out_shape=jax.ShapeDtypeStruct(G_SHAPE, F32),
        scratch_shapes=[pltpu.VMEM((m, D_MODEL), F32)],
        input_output_aliases=aliases,
        compiler_params=_params(("arbitrary", "arbitrary")),
    )(*operands)


def _local_step(x, target, wfull, conv_w, sinks, g_pre_mix, g_attn, g_conv, g_post_mix, g_pre_mlp, g_post_mlp):
    rope = _rope_tables(x.shape[0])
    q, kd0, kd1, vd0, vd1, gb, gc, xin, hn = _in_proj(x, g_pre_mix, wfull, rope)
    attn = _attention_fwd(q, kd0, kd1, vd0, vd1, sinks)
    h, mix, mixed = _mix_out(x, attn, gb, gc, xin, conv_w, g_attn, g_conv, g_post_mix, wfull)
    up, hn2, dout, dmlp, loss_sum, dg_post_mlp = _mlp_loss(h, target, g_pre_mlp, g_post_mlp, wfull)
    dup, dh, dmix, dg_pre_mlp, dg_post_mix = _mlp_bwd(dmlp, up, h, dout, mix, g_pre_mlp, g_post_mix, wfull)
    grads = _wgrad("wgrad_down", up, dmlp, a_cols=1024, per_chip=True, h_rows=H_DOWN, row_off=OFF_DOWN,
                   square_a=True, grads=None)
    grads = _wgrad("wgrad_up", hn2, dup, a_cols=1024, per_chip=True, h_rows=H_UP, row_off=OFF_UP,
                   square_a=False, grads=grads)
    dattn, dgb, dy, dg_attn, dg_conv, dconv_w = _mix_bwd(dmix, attn, gb, gc, xin, conv_w, g_attn, g_conv, wfull)
    grads = _wgrad("wgrad_out", mixed, dmix, a_cols=D_MODEL, per_chip=False, h_rows=H_OUT, row_off=OFF_OUT,
                   square_a=False, grads=grads)
    dq, dk0, dk1, dv0, dv1, dsink = _attention_bwd(q, dattn, attn, kd0, kd1, vd0, vd1, sinks)
    dproj, grad_x, dg_pre_mix = _in_proj_bwd(dq, dk0, dk1, dv0, dv1, dgb, dy, gc, xin, conv_w, x, dh, g_pre_mix,
                                             wfull, rope)
    grads = _wgrad("wgrad_in", dproj, hn, a_cols=IN_COLS, per_chip=False, h_rows=H_IN, row_off=OFF_IN,
                   square_a=False, grads=grads)
    small = dict(pre_mix_norm=dg_pre_mix, conv_w=dconv_w, attn_sinks=dsink[:, :8], attn_group_norm=dg_attn,
                 conv_group_norm=dg_conv, post_mix_norm=dg_post_mix, pre_mlp_norm=dg_pre_mlp,
                 post_mlp_norm=dg_post_mlp)
    return loss_sum, grad_x, grads, small


def _place():
    return lax.axis_index("x"), lax.axis_index("y"), lax.axis_index("c")


def _other_chips(x, y):
    return [(1 - x, y), (x, 1 - y), (1 - x, 1 - y)]


def _cast_pack(core, w_up, w_down, w_out, w_in_t):
    def body(core_ref, up_ref, down_ref, out_ref, in_ref, o_ref):
        o_ref[OFF_UP:OFF_UP + H_UP, :] = up_ref[...].astype(BF16)
        o_ref[OFF_DOWN:OFF_DOWN + H_DOWN, :] = down_ref[...].astype(BF16)
        o_ref[OFF_OUT:OFF_OUT + H_OUT, :] = out_ref[...].astype(BF16)
        o_ref[OFF_IN:OFF_IN + H_IN, :] = in_ref[...].astype(BF16)

    half = lambda rows: pl.BlockSpec((rows, D_MODEL), lambda i, core_ref: (core_ref[0], 0))
    return pl.pallas_call(
        body, name="cast_pack",
        grid_spec=pltpu.PrefetchScalarGridSpec(
            num_scalar_prefetch=1, grid=(1,),
            in_specs=[half(H_UP), half(H_DOWN), half(H_OUT), half(H_IN)],
            out_specs=pl.BlockSpec((H_ROWS, D_MODEL), lambda i, core_ref: (0, 0))),
        out_shape=jax.ShapeDtypeStruct((H_ROWS, D_MODEL), BF16),
        compiler_params=_params(("arbitrary",)),
    )(core, w_up, w_down, w_out, w_in_t)


def _all_gather_weights(half_block, conv_w_pad):
    def body(blk_ref, cw_ref, out_ref, cw_out_ref, send_sems, recv_sems, local_sems):
        x, y, c = _place()
        me, sibling = (x, y, c), (x, y, 1 - c)
        chips = _other_chips(x, y)

        def rows(px, py, pc):
            return out_ref.at[4 * px + 2 * py + pc]

        def copy(k, block, to, src=None):
            return pltpu.make_async_remote_copy(
                src_ref=rows(*block) if src is None else src, dst_ref=rows(*block),
                send_sem=send_sems.at[k], recv_sem=recv_sems.at[k], device_id=to, device_id_type=MESH)

        def cw_copy(k, chip, to):
            return pltpu.make_async_remote_copy(
                src_ref=cw_ref, dst_ref=cw_out_ref.at[2 * chip[0] + chip[1]],
                send_sem=send_sems.at[7 + k], recv_sem=recv_sems.at[7 + k], device_id=to, device_id_type=MESH)

        mine = pltpu.make_async_copy(blk_ref, rows(*me), local_sems.at[0])
        mine_cw = pltpu.make_async_copy(cw_ref, cw_out_ref.at[2 * x + y], local_sems.at[1])
        mine.start()
        mine_cw.start()
        first = [copy(0, me, sibling, src=blk_ref)]
        first += [copy(1 + j, me, (*chip, c), src=blk_ref) for j, chip in enumerate(chips)]
        first += [cw_copy(j, (x, y), (*chip, c)) for j, chip in enumerate(chips)]
        for cp in first:
            cp.start()
        passed = [copy(4 + j, (*chip, c), sibling) for j, chip in enumerate(chips)]
        for j, chip in enumerate(chips):
            copy(1 + j, (*chip, c), me).wait_recv()
            passed[j].start()
        copy(0, sibling, me).wait_recv()
        for j, chip in enumerate(chips):
            copy(4 + j, (*chip, 1 - c), me).wait_recv()
            cw_copy(j, chip, me).wait_recv()
        for cp in first + passed:
            cp.wait_send()
        mine.wait()
        mine_cw.wait()

    return pl.pallas_call(
        body, name="all_gather_weights",
        in_specs=[ANY, ANY], out_specs=[ANY, ANY],
        out_shape=[jax.ShapeDtypeStruct((2 * N_CHIPS, H_ROWS, D_MODEL), BF16),
                   jax.ShapeDtypeStruct((N_CHIPS,) + conv_w_pad.shape, F32)],
        scratch_shapes=[pltpu.SemaphoreType.DMA((10,)), pltpu.SemaphoreType.DMA((10,)), pltpu.SemaphoreType.DMA((2,))],
    )(half_block, conv_w_pad)


def _send_other_half(grads):
    def body(g_ref, recv_ref, send_sems, recv_sems):
        x, y, c = _place()
        copies = [pltpu.make_async_remote_copy(
            src_ref=g_ref.at[j, 1 - c], dst_ref=recv_ref.at[j], send_sem=send_sems.at[j], recv_sem=recv_sems.at[j],
            device_id=(x, y, 1 - c), device_id_type=MESH) for j in range(N_CHIPS)]
        for cp in copies:
            cp.start()
        for cp in copies:
            cp.wait_recv()
        for cp in copies:
            cp.wait_send()

    return pl.pallas_call(
        body, name="send_other_half", in_specs=[ANY], out_specs=ANY,
        out_shape=jax.ShapeDtypeStruct((N_CHIPS, H_ROWS, D_MODEL), F32),
        scratch_shapes=[pltpu.SemaphoreType.DMA((N_CHIPS,)), pltpu.SemaphoreType.DMA((N_CHIPS,))],
    )(grads)


SUM_ROWS = 480


def _pair_sum(core, grads, received):
    def body(core_ref, g_ref, r_ref, o_ref):
        o_ref[...] = g_ref[0] + r_ref[...]

    return pl.pallas_call(
        body, name="pair_sum",
        grid_spec=pltpu.PrefetchScalarGridSpec(
            num_scalar_prefetch=1, grid=(N_CHIPS, H_ROWS // SUM_ROWS),
            in_specs=[pl.BlockSpec((1, 1, SUM_ROWS, D_MODEL), lambda j, r, core_ref: (j, core_ref[0], r, 0)),
                      pl.BlockSpec((1, SUM_ROWS, D_MODEL), lambda j, r, core_ref: (j, r, 0))],
            out_specs=pl.BlockSpec((1, SUM_ROWS, D_MODEL), lambda j, r, core_ref: (j, r, 0))),
        out_shape=jax.ShapeDtypeStruct((N_CHIPS, H_ROWS, D_MODEL), F32),
        compiler_params=_params(("arbitrary", "arbitrary")),
    )(core, grads, received)


def _exchange_between_chips(partial):
    def body(p_ref, recv_ref, send_sems, recv_sems, local_sem):
        x, y, c = _place()
        my_chip = 2 * x + y
        chips = _other_chips(x, y)
        mine = pltpu.make_async_copy(p_ref.at[my_chip], recv_ref.at[my_chip], local_sem)
        mine.start()
        copies = [pltpu.make_async_remote_copy(
            src_ref=p_ref.at[2 * chip[0] + chip[1]], dst_ref=recv_ref.at[my_chip],
            send_sem=send_sems.at[j], recv_sem=recv_sems.at[j], device_id=(*chip, c), device_id_type=MESH)
            for j, chip in enumerate(chips)]
        for cp in copies:
            cp.start()
        for j, chip in enumerate(chips):
            pltpu.make_async_remote_copy(
                src_ref=p_ref.at[my_chip], dst_ref=recv_ref.at[2 * chip[0] + chip[1]],
                send_sem=send_sems.at[j], recv_sem=recv_sems.at[j], device_id=(*chip, c), device_id_type=MESH).wait_recv()
        for cp in copies:
            cp.wait_send()
        mine.wait()

    return pl.pallas_call(
        body, name="exchange_between_chips", in_specs=[ANY], out_specs=ANY,
        out_shape=jax.ShapeDtypeStruct((N_CHIPS, H_ROWS, D_MODEL), F32),
        scratch_shapes=[pltpu.SemaphoreType.DMA((3,)), pltpu.SemaphoreType.DMA((3,)), pltpu.SemaphoreType.DMA],
    )(partial)


def _chip_sum(received):
    def body(r_ref, o_ref):
        o_ref[...] = (r_ref[0] + r_ref[1]) + (r_ref[2] + r_ref[3])

    return pl.pallas_call(
        body, name="chip_sum", grid=(H_ROWS // SUM_ROWS,),
        in_specs=[pl.BlockSpec((N_CHIPS, SUM_ROWS, D_MODEL), lambda r: (0, r, 0))],
        out_specs=pl.BlockSpec((SUM_ROWS, D_MODEL), lambda r: (r, 0)),
        out_shape=jax.ShapeDtypeStruct((H_ROWS, D_MODEL), F32),
        compiler_params=_params(("arbitrary",)),
    )(received)


def _share_with_sibling(reduced_half):
    def body(h_ref, out_ref, send_sem, recv_sem, local_sem):
        x, y, c = _place()
        mine = pltpu.make_async_copy(h_ref, out_ref.at[c], local_sem)
        mine.start()
        cp = pltpu.make_async_remote_copy(src_ref=h_ref, dst_ref=out_ref.at[c], send_sem=send_sem, recv_sem=recv_sem,
                                          device_id=(x, y, 1 - c), device_id_type=MESH)
        cp.start()
        pltpu.make_async_remote_copy(src_ref=h_ref, dst_ref=out_ref.at[1 - c], send_sem=send_sem, recv_sem=recv_sem,
                                     device_id=(x, y, 1 - c), device_id_type=MESH).wait_recv()
        cp.wait_send()
        mine.wait()

    return pl.pallas_call(
        body, name="share_with_sibling", in_specs=[ANY], out_specs=ANY,
        out_shape=jax.ShapeDtypeStruct((2, H_ROWS, D_MODEL), F32),
        scratch_shapes=[pltpu.SemaphoreType.DMA, pltpu.SemaphoreType.DMA, pltpu.SemaphoreType.DMA],
    )(reduced_half)


SMALL_ROWS = 8


def _all_reduce_small(packed):
    def body(v_ref, o_ref, buf_ref, send_sems, recv_sems):
        x, y, c = _place()
        me = 4 * x + 2 * y + c
        buf_ref[me] = v_ref[...]
        copies = []
        for mask in range(1, 8):
            mx, my, mc = mask >> 2, (mask >> 1) & 1, mask & 1
            peer = (x ^ mx, y ^ my, c ^ mc)
            copies.append(pltpu.make_async_remote_copy(
                src_ref=v_ref, dst_ref=buf_ref.at[me], send_sem=send_sems.at[mask - 1], recv_sem=recv_sems.at[mask - 1],
                device_id=peer, device_id_type=MESH))
        for cp in copies:
            cp.start()
        for cp in copies:
            cp.wait_recv()
        for cp in copies:
            cp.wait_send()
        total = buf_ref[0]
        for d in range(1, 8):
            total = total + buf_ref[d]
        o_ref[...] = total

    vm = pl.BlockSpec(memory_space=pltpu.VMEM)
    return pl.pallas_call(
        body, name="all_reduce_small", in_specs=[vm], out_specs=vm,
        out_shape=jax.ShapeDtypeStruct(packed.shape, F32),
        scratch_shapes=[pltpu.VMEM((8,) + packed.shape, F32), pltpu.SemaphoreType.DMA((7,)), pltpu.SemaphoreType.DMA((7,))],
    )(packed)


def _adamw_math(w, g, m, v):
    m = ADAM_B1 * m + (1.0 - ADAM_B1) * g
    v = ADAM_B2 * v + (1.0 - ADAM_B2) * (g * g)
    m_hat = m / (1.0 - ADAM_B1 ** ADAM_STEP)
    v_hat = v / (1.0 - ADAM_B2 ** ADAM_STEP)
    delta = -ADAM_LR * (m_hat / (jnp.sqrt(v_hat) + ADAM_EPS) + ADAM_WD * w)
    return delta, m, v


def _adamw_rows(name, reduced, w, m, v, h_rows, row_off, rt):
    per_half = h_rows // rt

    def body(r_ref, w_ref, m_ref, v_ref, g_out, d_out, m_out, v_out):
        g = r_ref[0]
        g_out[...] = g
        d_out[...], m_out[...], v_out[...] = _adamw_math(w_ref[...], g, m_ref[...], v_ref[...])

    blk = pl.BlockSpec((rt, D_MODEL), lambda h, r: (h * per_half + r, 0))
    sds = jax.ShapeDtypeStruct(w.shape, F32)
    return pl.pallas_call(
        body, name=name, grid=(2, per_half),
        in_specs=[pl.BlockSpec((1, rt, D_MODEL), lambda h, r: (h, row_off // rt + r, 0)), blk, blk, blk],
        out_specs=[blk, blk, blk, blk], out_shape=[sds, sds, sds, sds],
        compiler_params=_params(("arbitrary", "arbitrary")),
    )(reduced, w, m, v)


def _adamw_small(w, g, m, v):
    def body(w_ref, g_ref, m_ref, v_ref, d_out, m_out, v_out):
        d_out[...], m_out[...], v_out[...] = _adamw_math(w_ref[...], g_ref[...], m_ref[...], v_ref[...])

    vm = pl.BlockSpec(memory_space=pltpu.VMEM)
    sds = jax.ShapeDtypeStruct(w.shape, F32)
    return pl.pallas_call(body, name="adamw_small", in_specs=[vm] * 4, out_specs=[vm] * 3, out_shape=[sds] * 3)(w, g, m, v)


SMALL_VECTORS = ("pre_mix_norm", "post_mix_norm", "pre_mlp_norm", "post_mlp_norm")


def _pack_small(p):
    rows = [p[n].reshape(1, D_MODEL) for n in SMALL_VECTORS]
    rows.append(jnp.concatenate([p["attn_group_norm"].reshape(1, -1), p["conv_group_norm"].reshape(1, -1)], axis=1))
    cw = p["conv_w"].reshape(CONV_K, -1)
    width = cw.shape[1]
    cw = jnp.pad(cw, ((0, 1), (0, CONV_WIDTH - width))).reshape(2, D_MODEL)
    rows.append(cw)
    rows.append(jnp.pad(p["attn_sinks"].reshape(1, -1), ((0, 0), (0, D_MODEL - 8))))
    return jnp.concatenate(rows, axis=0)


def _unpack_small(packed, conv_width):
    out = {n: packed[i:i + 1] for i, n in enumerate(SMALL_VECTORS)}
    out["attn_group_norm"] = packed[4:5, :Q_WIDTH]
    out["conv_group_norm"] = packed[4:5, Q_WIDTH:]
    out["conv_w"] = packed[5:7].reshape(4, CONV_WIDTH)[:CONV_K, :conv_width].reshape(1, CONV_K, conv_width)
    out["attn_sinks"] = packed[7:8, :8]
    return out


WEIGHT_ORDER = ("pre_mix_norm", "w_in", "conv_w", "attn_sinks", "attn_group_norm", "conv_group_norm", "w_out",
                "post_mix_norm", "pre_mlp_norm", "w_up", "w_down", "post_mlp_norm")


def kernel(x, pre_mix_norm, w_in, conv_w, attn_sinks, attn_group_norm, conv_group_norm, w_out, post_mix_norm, pre_mlp_norm, w_up, w_down, post_mlp_norm, loss_target, m_pre_mix_norm, m_w_in, m_conv_w, m_attn_sinks, m_attn_group_norm, m_conv_group_norm, m_w_out, m_post_mix_norm, m_pre_mlp_norm, m_w_up, m_w_down, m_post_mlp_norm, v_pre_mix_norm, v_w_in, v_conv_w, v_attn_sinks, v_attn_group_norm, v_conv_group_norm, v_w_out, v_post_mix_norm, v_pre_mlp_norm, v_w_up, v_w_down, v_post_mlp_norm):
    w = dict(pre_mix_norm=pre_mix_norm, w_in=w_in, conv_w=conv_w, attn_sinks=attn_sinks, attn_group_norm=attn_group_norm,
             conv_group_norm=conv_group_norm, w_out=w_out, post_mix_norm=post_mix_norm, pre_mlp_norm=pre_mlp_norm,
             w_up=w_up, w_down=w_down, post_mlp_norm=post_mlp_norm)
    m = dict(pre_mix_norm=m_pre_mix_norm, w_in=m_w_in, conv_w=m_conv_w, attn_sinks=m_attn_sinks,
             attn_group_norm=m_attn_group_norm, conv_group_norm=m_conv_group_norm, w_out=m_w_out,
             post_mix_norm=m_post_mix_norm, pre_mlp_norm=m_pre_mlp_norm, w_up=m_w_up, w_down=m_w_down,
             post_mlp_norm=m_post_mlp_norm)
    v = dict(pre_mix_norm=v_pre_mix_norm, w_in=v_w_in, conv_w=v_conv_w, attn_sinks=v_attn_sinks,
             attn_group_norm=v_attn_group_norm, conv_group_norm=v_conv_group_norm, w_out=v_w_out,
             post_mix_norm=v_post_mix_norm, pre_mlp_norm=v_pre_mlp_norm, w_up=v_w_up, w_down=v_w_down,
             post_mlp_norm=v_post_mlp_norm)
    core = lax.axis_index("c").astype(jnp.int32).reshape(1)
    chip = 2 * lax.axis_index("x") + lax.axis_index("y")
    local_conv = conv_w.shape[2]

    half_block = _cast_pack(core, w_up[0], w_down[0], w_out[0], w_in[0].T)
    wfull, conv_all = _all_gather_weights(half_block, jnp.pad(conv_w[0], ((0, 8 - CONV_K), (0, 0))))
    conv_full = conv_all[:, :CONV_K, :].transpose(1, 0, 2).reshape(CONV_K, CONV_WIDTH)

    loss_sum, grad_x, grads, small = _local_step(
        x[0], loss_target[0], wfull, conv_full, attn_sinks, pre_mix_norm, attn_group_norm, conv_group_norm,
        post_mix_norm, pre_mlp_norm, post_mlp_norm)
    loss = lax.psum(loss_sum[0, 0] * (0.5 / D_MODEL), ("x", "y", "c"))

    partial = _pair_sum(core, grads, _send_other_half(grads))
    reduced = _share_with_sibling(_chip_sum(_exchange_between_chips(partial)))

    out_g, out_d, out_m, out_v = {}, {}, {}, {}
    out_g["w_up"], out_d["w_up"], out_m["w_up"], out_v["w_up"] = _adamw_rows(
        "adamw_up", reduced, w_up[0], m_w_up[0], v_w_up[0], H_UP, OFF_UP, 256)
    out_g["w_down"], out_d["w_down"], out_m["w_down"], out_v["w_down"] = _adamw_rows(
        "adamw_down", reduced, w_down[0], m_w_down[0], v_w_down[0], H_DOWN, OFF_DOWN, 256)
    out_g["w_out"], out_d["w_out"], out_m["w_out"], out_v["w_out"] = _adamw_rows(
        "adamw_out", reduced, w_out[0], m_w_out[0], v_w_out[0], H_OUT, OFF_OUT, H_OUT)
    in_t = _adamw_rows("adamw_in", reduced, w_in[0].T, m_w_in[0].T, v_w_in[0].T, H_IN, OFF_IN, H_IN)
    out_g["w_in"], out_d["w_in"], out_m["w_in"], out_v["w_in"] = [t.T for t in in_t]

    small_sum = _unpack_small(_all_reduce_small(_pack_small(small)), CONV_WIDTH)
    small_sum["conv_w"] = lax.dynamic_slice_in_dim(small_sum["conv_w"], chip * local_conv, local_conv, axis=2)
    names = SMALL_VECTORS + ("attn_group_norm", "conv_group_norm", "conv_w", "attn_sinks")
    packed = [_pack_small({n: t[n] for n in names}) for t in (w, small_sum, m, v)]
    small_d, small_m, small_v = [_unpack_small(t, local_conv) for t in _adamw_small(*packed)]
    for n in names:
        out_g[n], out_d[n], out_m[n], out_v[n] = small_sum[n], small_d[n], small_m[n], small_v[n]

    def shaped(d):
        return [d[n].reshape(w[n].shape) for n in WEIGHT_ORDER]

    return (loss, grad_x[None], *shaped(out_g), *shaped(out_d), *shaped(out_m), *shaped(out_v))
```

```python
import math
from typing import Callable, NamedTuple

import jax
import jax.numpy as jnp
from jax import lax
from jax.experimental import pallas as pl
from jax.experimental.pallas import tpu as pltpu

F32 = jnp.float32
BF16 = jnp.bfloat16

D_MODEL = 1024
HEAD_DIM = 64
Q_WIDTH = 512
KV_WIDTH = 128
CONV_WIDTH = 512
CONV_K = 3
D_FF = 4096
IN_COLS = 2304
QBLOCK = 128
ROT_DIM = 16
ROPE_THETA = 500000.0
NORM_EPS = 1e-6
NEG_INF = -1e30
N_CHIPS = 4

ADAM_LR = 0.001
ADAM_B1 = 0.9
ADAM_B2 = 0.999
ADAM_EPS = 1e-08
ADAM_WD = 0.01
ADAM_STEP = 10

H_UP, H_DOWN, H_OUT, H_IN = 512, 512, 128, 288

TOKEN_TILE = 512
MLP_BWD_TOKEN_TILE = 256
WGRAD_TOKEN_TILE = 1024
VMEM_LIMIT_V7X = 56 * 1024 * 1024

MESH = pl.DeviceIdType.MESH
ANY = pl.BlockSpec(memory_space=pl.ANY)
VMEM_WHOLE = pl.BlockSpec(memory_space=pltpu.VMEM)
SDS = jax.ShapeDtypeStruct


def _resident(shape):
    zeros = (0,) * len(shape)
    return pl.BlockSpec(shape, lambda *_: zeros, pipeline_mode=pl.Buffered(1))


def _rms(v):
    return lax.rsqrt(jnp.mean(v * v, axis=-1, keepdims=True) + NORM_EPS)


def _norm_bwd(dy, gain, vhat, rstd):
    t = dy * gain
    return rstd * (t - vhat * jnp.mean(t * vhat, axis=-1, keepdims=True))


def _colsum(v):
    return jnp.sum(v, axis=0, keepdims=True)


def _dot_nt(a, b):
    return lax.dot_general(a, b, (((1,), (1,)), ((), ())), preferred_element_type=F32)


def _dot_tn(a, b):
    return lax.dot_general(a, b, (((0,), (0,)), ((), ())), preferred_element_type=F32)


def _dot(a, b):
    return jnp.dot(a, b, preferred_element_type=F32)


def _lane_lt64(shape):
    return lax.broadcasted_iota(jnp.int32, shape, 1) < HEAD_DIM


class _Comm(NamedTuple):
    operands: tuple
    out_shapes: tuple
    aliases: dict
    n_remote: int
    n_local: int
    plan: Callable


def _merge(*comms):
    operands, out_shapes, aliases, parts = [], [], {}, []
    n_remote = n_local = 0
    for cm in comms:
        parts.append((len(operands), len(out_shapes), n_remote, n_local, cm))
        for k, v in cm.aliases.items():
            aliases[len(operands) + k] = len(out_shapes) + v
        operands += cm.operands
        out_shapes += cm.out_shapes
        n_remote += cm.n_remote
        n_local += cm.n_local

    def plan(ins, outs, send, recv, loc):
        sends, recvs, locs = [], [], []
        for i0, o0, r0, l0, cm in parts:
            s, r, l = cm.plan(ins[i0:i0 + len(cm.operands)], outs[o0:o0 + len(cm.out_shapes)],
                              lambda k, r0=r0: send(r0 + k), lambda k, r0=r0: recv(r0 + k), lambda k, l0=l0: loc(l0 + k))
            sends, recvs, locs = sends + s, recvs + r, locs + l
        return sends, recvs, locs

    return _Comm(tuple(operands), tuple(out_shapes), aliases, n_remote, n_local, plan)


def _sem_scratch(comm):
    return [pltpu.SemaphoreType.DMA((max(comm.n_remote, 1),)), pltpu.SemaphoreType.DMA((max(comm.n_remote, 1),)),
            pltpu.SemaphoreType.DMA((max(comm.n_local, 1),))]


def _pallas(body, *, name, grid, in_specs, out_specs, out_shape, operands, scratch=(), comm=None):
    params = pltpu.CompilerParams(dimension_semantics=("arbitrary",) * len(grid), vmem_limit_bytes=VMEM_LIMIT_V7X)
    if comm is None:
        return pl.pallas_call(body, name=name, grid=grid, in_specs=in_specs, out_specs=out_specs, out_shape=out_shape,
                              scratch_shapes=list(scratch), compiler_params=params)(*operands)
    n_in, n_out, n_scr = len(in_specs), len(out_specs), len(scratch)
    c_in, c_out = len(comm.operands), len(comm.out_shapes)

    def with_comm(*refs):
        ins, c_ins = refs[:n_in], refs[n_in:n_in + c_in]
        o0 = n_in + c_in
        outs, c_outs = refs[o0:o0 + n_out], refs[o0 + n_out:o0 + n_out + c_out]
        s0 = o0 + n_out + c_out
        scr = refs[s0:s0 + n_scr]
        send_sems, recv_sems, local_sems = refs[s0 + n_scr:]
        first = last = None
        for axis, size in enumerate(grid):
            at_start, at_end = pl.program_id(axis) == 0, pl.program_id(axis) == size - 1
            first = at_start if first is None else jnp.logical_and(first, at_start)
            last = at_end if last is None else jnp.logical_and(last, at_end)

        def copies():
            return comm.plan(c_ins, c_outs, lambda k: send_sems.at[k], lambda k: recv_sems.at[k],
                             lambda k: local_sems.at[k])

        @pl.when(first)
        def _():
            sends, _, locs = copies()
            for cp in sends + locs:
                cp.start()

        body(*ins, *outs, *scr)

        @pl.when(last)
        def _():
            sends, recvs, locs = copies()
            for cp in recvs:
                cp.wait_recv()
            for cp in sends:
                cp.wait_send()
            for cp in locs:
                cp.wait()

    return pl.pallas_call(
        with_comm, name=name, grid=grid,
        in_specs=list(in_specs) + [ANY] * c_in, out_specs=list(out_specs) + [ANY] * c_out,
        out_shape=list(out_shape) + list(comm.out_shapes),
        scratch_shapes=list(scratch) + _sem_scratch(comm),
        input_output_aliases={n_in + k: n_out + v for k, v in comm.aliases.items()},
        compiler_params=params)(*operands, *comm.operands)


def _comm_only(name, comm):
    c_in, c_out = len(comm.operands), len(comm.out_shapes)

    def body(*refs):
        send_sems, recv_sems, local_sems = refs[c_in + c_out:]
        sends, recvs, locs = comm.plan(refs[:c_in], refs[c_in:c_in + c_out], lambda k: send_sems.at[k],
                                       lambda k: recv_sems.at[k], lambda k: local_sems.at[k])
        for cp in sends + locs:
            cp.start()
        for cp in recvs:
            cp.wait_recv()
        for cp in sends:
            cp.wait_send()
        for cp in locs:
            cp.wait()

    return pl.pallas_call(
        body, name=name, in_specs=[ANY] * c_in, out_specs=[ANY] * c_out, out_shape=list(comm.out_shapes),
        scratch_shapes=_sem_scratch(comm),
        input_output_aliases=dict(comm.aliases))(*comm.operands)


def _place():
    return lax.axis_index("x"), lax.axis_index("y"), lax.axis_index("c")


def _other_chips(x, y):
    return [(1 - x, y), (x, 1 - y), (1 - x, 1 - y)]


def _slot(px, py, pc):
    return 4 * px + 2 * py + pc


def _remote(src, dst, send_sem, recv_sem, to):
    return pltpu.make_async_remote_copy(src_ref=src, dst_ref=dst, send_sem=send_sem, recv_sem=recv_sem,
                                        device_id=to, device_id_type=MESH)


def _gather_first(half_block):
    def plan(ins, outs, send, recv, loc):
        (blk,), (full,) = ins, outs
        x, y, c = _place()
        chips = _other_chips(x, y)
        mine = full.at[_slot(x, y, c)]
        sends = [_remote(blk, mine, send(0), recv(0), (x, y, 1 - c))]
        sends += [_remote(blk, mine, send(1 + j), recv(1 + j), (*chip, c)) for j, chip in enumerate(chips)]
        recvs = [_remote(blk, full.at[_slot(x, y, 1 - c)], send(0), recv(0), (x, y, 1 - c))]
        recvs += [_remote(blk, full.at[_slot(*chip, c)], send(1 + j), recv(1 + j), (*chip, c))
                  for j, chip in enumerate(chips)]
        return sends, recvs, [pltpu.make_async_copy(blk, mine, loc(0))]

    return _Comm((half_block,), (SDS((2 * N_CHIPS,) + half_block.shape, half_block.dtype),), {}, 4, 1, plan)


def _gather_second(partly_gathered):
    def plan(ins, outs, send, recv, loc):
        (src,), (full,) = ins, outs
        x, y, c = _place()
        chips = _other_chips(x, y)
        sends = [_remote(src.at[_slot(*chip, c)], full.at[_slot(*chip, c)], send(j), recv(j), (x, y, 1 - c))
                 for j, chip in enumerate(chips)]
        recvs = [_remote(src.at[_slot(*chip, 1 - c)], full.at[_slot(*chip, 1 - c)], send(j), recv(j), (x, y, 1 - c))
                 for j, chip in enumerate(chips)]
        return sends, recvs, []

    return _Comm((partly_gathered,), (SDS(partly_gathered.shape, partly_gathered.dtype),), {0: 0}, 3, 0, plan)


def _gather_small(block):
    def plan(ins, outs, send, recv, loc):
        (blk,), (full,) = ins, outs
        x, y, c = _place()
        chips = _other_chips(x, y)
        sends = [_remote(blk, full.at[2 * x + y], send(j), recv(j), (*chip, c)) for j, chip in enumerate(chips)]
        recvs = [_remote(blk, full.at[2 * chip[0] + chip[1]], send(j), recv(j), (*chip, c))
                 for j, chip in enumerate(chips)]
        return sends, recvs, [pltpu.make_async_copy(blk, full.at[2 * x + y], loc(0))]

    return _Comm((block,), (SDS((N_CHIPS,) + block.shape, block.dtype),), {}, 3, 1, plan)


def _pair_send(grads):
    def plan(ins, outs, send, recv, loc):
        (g,), (got,) = ins, outs
        x, y, c = _place()
        copies = [_remote(g.at[j, 1 - c], got.at[j], send(j), recv(j), (x, y, 1 - c)) for j in range(N_CHIPS)]
        return copies, copies, []

    shape = (grads.shape[0],) + grads.shape[2:]
    return _Comm((grads,), (SDS(shape, grads.dtype),), {}, N_CHIPS, 0, plan)


def _chip_exchange(partial):
    def plan(ins, outs, send, recv, loc):
        (p,), (got,) = ins, outs
        x, y, c = _place()
        my_chip = 2 * x + y
        chips = _other_chips(x, y)
        sends = [_remote(p.at[2 * chip[0] + chip[1]], got.at[my_chip], send(j), recv(j), (*chip, c))
                 for j, chip in enumerate(chips)]
        recvs = [_remote(p.at[my_chip], got.at[2 * chip[0] + chip[1]], send(j), recv(j), (*chip, c))
                 for j, chip in enumerate(chips)]
        return sends, recvs, [pltpu.make_async_copy(p.at[my_chip], got.at[my_chip], loc(0))]

    return _Comm((partial,), (SDS(partial.shape, partial.dtype),), {}, 3, 1, plan)


def _pair_sum(name, core, grads, received):
    h = grads.shape[2]

    def body(core_ref, g_ref, r_ref, o_ref):
        o_ref[...] = (g_ref[0] + r_ref[...]).astype(BF16)

    return pl.pallas_call(
        body, name=name,
        grid_spec=pltpu.PrefetchScalarGridSpec(
            num_scalar_prefetch=1, grid=(N_CHIPS,),
            in_specs=[pl.BlockSpec((1, 1, h, D_MODEL), lambda j, core_ref: (j, core_ref[0], 0, 0)),
                      pl.BlockSpec((1, h, D_MODEL), lambda j, core_ref: (j, 0, 0))],
            out_specs=pl.BlockSpec((1, h, D_MODEL), lambda j, core_ref: (j, 0, 0))),
        out_shape=SDS((N_CHIPS, h, D_MODEL), BF16),
        compiler_params=pltpu.CompilerParams(dimension_semantics=("arbitrary",), vmem_limit_bytes=VMEM_LIMIT_V7X),
    )(core, grads, received)


def _finish_reduce(exchanged):
    n = len(exchanged)

    def body(*refs):
        got, out = refs[:n], refs[n:2 * n]
        halves = refs[2 * n:3 * n]
        send_sems, recv_sems, local_sems = refs[3 * n:]
        x, y, c = _place()
        sends, recvs, locs = [], [], []
        for k in range(n):
            g = got[k]
            halves[k][...] = ((g[0].astype(F32) + g[1].astype(F32)) + (g[2].astype(F32) + g[3].astype(F32)))
            locs.append(pltpu.make_async_copy(halves[k], out[k].at[c], local_sems.at[k]))
            sends.append(_remote(halves[k], out[k].at[c], send_sems.at[k], recv_sems.at[k], (x, y, 1 - c)))
            recvs.append(_remote(halves[k], out[k].at[1 - c], send_sems.at[k], recv_sems.at[k], (x, y, 1 - c)))
            locs[-1].start()
            sends[-1].start()
        for cp in recvs:
            cp.wait_recv()
        for cp in sends:
            cp.wait_send()
        for cp in locs:
            cp.wait()

    return pl.pallas_call(
        body, name="finish_reduce", in_specs=[VMEM_WHOLE] * n, out_specs=[ANY] * n,
        out_shape=[SDS((2,) + e.shape[1:], F32) for e in exchanged],
        scratch_shapes=[pltpu.VMEM(e.shape[1:], F32) for e in exchanged]
                       + [pltpu.SemaphoreType.DMA((n,)), pltpu.SemaphoreType.DMA((n,)), pltpu.SemaphoreType.DMA((n,))],
        compiler_params=pltpu.CompilerParams(vmem_limit_bytes=VMEM_LIMIT_V7X),
    )(*exchanged)


SMALL_ROWS = 8


def _all_reduce_small(packed):
    def body(v_ref, o_ref, buf_ref, send_sems, recv_sems):
        x, y, c = _place()
        me = _slot(x, y, c)
        buf_ref[me] = v_ref[...]
        copies = []
        for mask in range(1, 8):
            peer = (x ^ (mask >> 2), y ^ ((mask >> 1) & 1), c ^ (mask & 1))
            copies.append(_remote(v_ref, buf_ref.at[me], send_sems.at[mask - 1], recv_sems.at[mask - 1], peer))
        for cp in copies:
            cp.start()
        for cp in copies:
            cp.wait_recv()
        for cp in copies:
            cp.wait_send()
        total = buf_ref[0]
        for d in range(1, 8):
            total = total + buf_ref[d]
        o_ref[...] = total

    return pl.pallas_call(
        body, name="all_reduce_small", in_specs=[VMEM_WHOLE], out_specs=VMEM_WHOLE,
        out_shape=SDS(packed.shape, F32),
        scratch_shapes=[pltpu.VMEM((8,) + packed.shape, F32), pltpu.SemaphoreType.DMA((7,)), pltpu.SemaphoreType.DMA((7,))],
    )(packed)


def _rope_tables(seq):
    half = ROT_DIM // 2
    pos = jnp.arange(seq, dtype=F32)
    inv_freq = ROPE_THETA ** (-jnp.arange(0, ROT_DIM, 2, dtype=F32) / ROT_DIM)
    ang = pos[:, None] * inv_freq[None, :]
    cos, sin = jnp.cos(ang), jnp.sin(ang)
    ones = jnp.ones((seq, HEAD_DIM - ROT_DIM), F32)
    zeros = jnp.zeros((seq, HEAD_DIM - half), F32)
    c_head = jnp.concatenate([cos, cos, ones], axis=1)
    sa_head = jnp.concatenate([-sin, zeros], axis=1)
    sb_head = jnp.concatenate([jnp.zeros((seq, half), F32), sin, jnp.zeros((seq, HEAD_DIM - ROT_DIM), F32)], axis=1)
    two = lambda t: jnp.concatenate([t, t], axis=1)
    return two(c_head), two(sa_head), two(sb_head)


def _rope(t, c, sa, sb):
    half = ROT_DIM // 2
    return t * c + pltpu.roll(t, 128 - half, 1) * sa + pltpu.roll(t, half, 1) * sb


def _rope_transposed(dt, c, sa, sb):
    half = ROT_DIM // 2
    return dt * c + pltpu.roll(dt * sa, half, 1) + pltpu.roll(dt * sb, 128 - half, 1)


def _cast_halves(core, w_up, w_down, w_out, w_in_t):
    def body(core_ref, up_ref, down_ref, out_ref, in_ref, up_o, down_o, out_o, in_o):
        up_o[...] = up_ref[...].astype(BF16)
        down_o[...] = down_ref[...].astype(BF16)
        out_o[...] = out_ref[...].astype(BF16)
        in_o[...] = in_ref[...].astype(BF16)

    half = lambda rows: pl.BlockSpec((rows, D_MODEL), lambda i, core_ref: (core_ref[0], 0))
    whole = lambda rows: pl.BlockSpec((rows, D_MODEL), lambda i, core_ref: (0, 0))
    rows = (H_UP, H_DOWN, H_OUT, H_IN)
    return pl.pallas_call(
        body, name="cast_halves",
        grid_spec=pltpu.PrefetchScalarGridSpec(
            num_scalar_prefetch=1, grid=(1,), in_specs=[half(r) for r in rows], out_specs=[whole(r) for r in rows]),
        out_shape=[SDS((r, D_MODEL), BF16) for r in rows],
        compiler_params=pltpu.CompilerParams(dimension_semantics=("arbitrary",), vmem_limit_bytes=VMEM_LIMIT_V7X),
    )(core, w_up, w_down, w_out, w_in_t)


def _in_proj(x, g_pre, w_in_t, rope, comm=None):
    seq = x.shape[0]
    tb = TOKEN_TILE

    def body(x_ref, g_ref, w_ref, c_ref, sa_ref, sb_ref,
             q_ref, kd0_ref, kd1_ref, vd0_ref, vd1_ref, gb_ref, gc_ref, xin_ref, hn_ref):
        xv = x_ref[...]
        hn = (xv * _rms(xv) * g_ref[...]).astype(BF16)
        hn_ref[...] = hn
        proj = _dot_nt(hn, w_ref[...].reshape(IN_COLS, D_MODEL))
        c, sa, sb = c_ref[...], sa_ref[...], sb_ref[...]
        scale = 1.0 / math.sqrt(HEAD_DIM)
        for p in range(Q_WIDTH // 128):
            q_ref[:, 128 * p:128 * (p + 1)] = (_rope(proj[:, 128 * p:128 * (p + 1)], c, sa, sb) * scale).astype(BF16)
        k = _rope(proj[:, Q_WIDTH:Q_WIDTH + KV_WIDTH], c, sa, sb)
        v = proj[:, Q_WIDTH + KV_WIDTH:Q_WIDTH + 2 * KV_WIDTH]
        low = _lane_lt64(k.shape)
        k_sw, v_sw = pltpu.roll(k, HEAD_DIM, 1), pltpu.roll(v, HEAD_DIM, 1)
        kd0_ref[...] = jnp.where(low, k, k_sw).astype(BF16)
        kd1_ref[...] = jnp.where(low, k_sw, k).astype(BF16)
        vd0_ref[...] = jnp.where(low, v, v_sw).astype(BF16)
        vd1_ref[...] = jnp.where(low, v_sw, v).astype(BF16)
        base = Q_WIDTH + 2 * KV_WIDTH
        gb_ref[...] = proj[:, base:base + CONV_WIDTH]
        gc_ref[...] = proj[:, base + CONV_WIDTH:base + 2 * CONV_WIDTH]
        xin_ref[...] = proj[:, base + 2 * CONV_WIDTH:base + 3 * CONV_WIDTH]

    tile = lambda w: pl.BlockSpec((tb, w), lambda i: (i, 0))
    return _pallas(
        body, name="in_proj", grid=(seq // tb,),
        in_specs=[tile(D_MODEL), _resident((1, D_MODEL)), _resident(w_in_t.shape), tile(128), tile(128), tile(128)],
        out_specs=[tile(Q_WIDTH), tile(128), tile(128), tile(128), tile(128),
                   tile(CONV_WIDTH), tile(CONV_WIDTH), tile(CONV_WIDTH), tile(D_MODEL)],
        out_shape=[SDS((seq, Q_WIDTH), BF16)] + [SDS((seq, 128), BF16)] * 4
                  + [SDS((seq, CONV_WIDTH), F32)] * 3 + [SDS((seq, D_MODEL), BF16)],
        operands=(x, g_pre, w_in_t, *rope), comm=comm)


def _attn_valid(i):
    shape = (4 * QBLOCK, 2 * QBLOCK)
    row = lax.broadcasted_iota(jnp.int32, shape, 0)
    col = lax.broadcasted_iota(jnp.int32, shape, 1)
    qi = row & (QBLOCK - 1)
    return (col > qi) & (col <= qi + QBLOCK) & ((col >= QBLOCK) | (i > 0))


def _stack_heads(pair0, pair1):
    low = _lane_lt64(pair0.shape)
    zero = jnp.zeros_like(pair0)
    return jnp.concatenate([jnp.where(low, pair0, zero), jnp.where(low, zero, pair0),
                            jnp.where(low, pair1, zero), jnp.where(low, zero, pair1)], axis=0)


def _unstack_heads(stacked):
    low = _lane_lt64((QBLOCK, 128))
    pair0 = jnp.where(low, stacked[0:QBLOCK], stacked[QBLOCK:2 * QBLOCK])
    pair1 = jnp.where(low, stacked[2 * QBLOCK:3 * QBLOCK], stacked[3 * QBLOCK:4 * QBLOCK])
    return pair0, pair1


def _sink_column(sink_ref, kv_head):
    row = lax.broadcasted_iota(jnp.int32, (4 * QBLOCK, 1), 0)
    s = [sink_ref[0, 4 * kv_head + j] for j in range(4)]
    return jnp.where(row < QBLOCK, s[0], jnp.where(row < 2 * QBLOCK, s[1], jnp.where(row < 3 * QBLOCK, s[2], s[3])))


def _band(ref, i):
    prev = pl.multiple_of(jnp.maximum(i - 1, 0) * QBLOCK, QBLOCK)
    own = pl.multiple_of(i * QBLOCK, QBLOCK)
    return jnp.concatenate([ref[pl.ds(prev, QBLOCK), :], ref[pl.ds(own, QBLOCK), :]], axis=0), prev, own


def _softmax_with_sink(s, sink_col):
    m = jnp.maximum(jnp.max(s, axis=-1, keepdims=True), sink_col)
    p = jnp.exp(s - m)
    e_sink = jnp.exp(sink_col - m)
    inv_l = 1.0 / (jnp.sum(p, axis=-1, keepdims=True) + e_sink)
    return p, e_sink, inv_l


def _attention_fwd(q, kd0, kd1, vd0, vd1, sinks, comm=None):
    seq = q.shape[0]

    def body(sink_ref, q_ref, kd0_ref, kd1_ref, vd0_ref, vd1_ref, o_ref):
        i = pl.program_id(0)
        valid = _attn_valid(i)
        for kv_head, (k_ref, v_ref) in enumerate(((kd0_ref, vd0_ref), (kd1_ref, vd1_ref))):
            kband, _, _ = _band(k_ref, i)
            vband, _, _ = _band(v_ref, i)
            base = 256 * kv_head
            qm = _stack_heads(q_ref[:, base:base + 128], q_ref[:, base + 128:base + 256])
            s = jnp.where(valid, _dot_nt(qm, kband), NEG_INF)
            p, _, inv_l = _softmax_with_sink(s, _sink_column(sink_ref, kv_head))
            o = _dot(p.astype(BF16), vband) * inv_l
            pair0, pair1 = _unstack_heads(o)
            o_ref[:, base:base + 128] = pair0
            o_ref[:, base + 128:base + 256] = pair1

    blk = pl.BlockSpec((QBLOCK, Q_WIDTH), lambda i: (i, 0))
    full = _resident((seq, 128))
    return _pallas(
        body, name="attention_fwd", grid=(seq // QBLOCK,),
        in_specs=[pl.BlockSpec(memory_space=pltpu.SMEM), blk, full, full, full, full],
        out_specs=[blk], out_shape=[SDS((seq, Q_WIDTH), F32)],
        operands=(sinks, q, kd0, kd1, vd0, vd1), comm=comm)


def _conv_parts(gb, gc, xin, gc_halo, xin_halo, conv_w, first):
    tb = gb.shape[0]
    u = gc * xin
    u_halo = jnp.where(first, 0.0, gc_halo * xin_halo)
    ext = jnp.concatenate([u_halo, u], axis=0)
    u1 = pltpu.roll(ext, 1, 0)[8:8 + tb]
    u2 = pltpu.roll(ext, 2, 0)[8:8 + tb]
    y = conv_w[0:1, :] * u2 + conv_w[1:2, :] * u1 + conv_w[2:3, :] * u
    return u, u1, u2, y


def _halo_prev(tb, w):
    return pl.BlockSpec((8, w), lambda i: (jnp.maximum(i * (tb // 8) - 1, 0), 0))


def _mix_out(x, attn, gb, gc, xin, conv_w, g_attn, g_conv, g_post_mix, w_out, comm=None):
    seq = x.shape[0]
    tb = TOKEN_TILE

    def body(x_ref, a_ref, gb_ref, gc_ref, xin_ref, gch_ref, xinh_ref, cw_ref, ga_ref, gcn_ref, gpm_ref, w_ref,
             h_ref, mix_ref, mixed_ref):
        first = pl.program_id(0) == 0
        _, _, _, y = _conv_parts(gb_ref[...], gc_ref[...], xin_ref[...], gch_ref[...], xinh_ref[...], cw_ref[...], first)
        conv = gb_ref[...] * y
        a = a_ref[...]
        mixed_ref[:, 0:Q_WIDTH] = (a * _rms(a) * ga_ref[...]).astype(BF16)
        mixed_ref[:, Q_WIDTH:] = (conv * _rms(conv) * gcn_ref[...]).astype(BF16)
        mix = _dot(mixed_ref[...], w_ref[...].reshape(D_MODEL, D_MODEL))
        mix_ref[...] = mix
        h_ref[...] = x_ref[...] + mix * _rms(mix) * gpm_ref[...]

    tile = lambda w: pl.BlockSpec((tb, w), lambda i: (i, 0))
    return _pallas(
        body, name="mix_out", grid=(seq // tb,),
        in_specs=[tile(D_MODEL), tile(Q_WIDTH), tile(CONV_WIDTH), tile(CONV_WIDTH), tile(CONV_WIDTH),
                  _halo_prev(tb, CONV_WIDTH), _halo_prev(tb, CONV_WIDTH),
                  _resident((CONV_K, CONV_WIDTH)), _resident((1, Q_WIDTH)), _resident((1, CONV_WIDTH)),
                  _resident((1, D_MODEL)), _resident(w_out.shape)],
        out_specs=[tile(D_MODEL), tile(D_MODEL), tile(D_MODEL)],
        out_shape=[SDS((seq, D_MODEL), F32), SDS((seq, D_MODEL), F32), SDS((seq, D_MODEL), BF16)],
        operands=(x, attn, gb, gc, xin, gc, xin, conv_w, g_attn, g_conv, g_post_mix, w_out), comm=comm)


def _mlp_loss(h, target, g_pre_mlp, g_post_mlp, w_up, w_down):
    seq = h.shape[0]
    tb = TOKEN_TILE

    def body(h_ref, t_ref, g2_ref, g4_ref, wup_ref, wdown_ref,
             up_ref, hn2_ref, dout_ref, dmlp_ref, loss_ref, dg4_ref, act_ref):
        @pl.when(pl.program_id(0) == 0)
        def _():
            loss_ref[...] = jnp.zeros_like(loss_ref)
            dg4_ref[...] = jnp.zeros_like(dg4_ref)

        hv = h_ref[...]
        hn2 = (hv * _rms(hv) * g2_ref[...]).astype(BF16)
        hn2_ref[...] = hn2
        for j in range(N_CHIPS):
            up = _dot(hn2[:, :H_UP], wup_ref[2 * j]) + _dot(hn2[:, H_UP:], wup_ref[2 * j + 1])
            up = jnp.maximum(up, 0.0)
            up_ref[:, 1024 * j:1024 * (j + 1)] = up.astype(BF16)
            act_ref[:, 1024 * j:1024 * (j + 1)] = (up * up).astype(BF16)
        mlp = _dot(act_ref[...], wdown_ref[...].reshape(D_FF, D_MODEL))
        rstd = _rms(mlp)
        zhat = mlp * rstd
        diff = hv + zhat * g4_ref[...] - t_ref[...]
        loss_ref[...] += jnp.sum(jnp.sum(diff * diff, axis=1, keepdims=True), axis=0, keepdims=True)
        dout = diff * (1.0 / D_MODEL)
        dout_ref[...] = dout
        dg4_ref[...] += _colsum(dout * zhat)
        dmlp_ref[...] = _norm_bwd(dout, g4_ref[...], zhat, rstd).astype(BF16)

    tile = lambda w: pl.BlockSpec((tb, w), lambda i: (i, 0))
    return _pallas(
        body, name="mlp_loss", grid=(seq // tb,),
        in_specs=[tile(D_MODEL), tile(D_MODEL), _resident((1, D_MODEL)), _resident((1, D_MODEL)),
                  _resident(w_up.shape), _resident(w_down.shape)],
        out_specs=[tile(D_FF), tile(D_MODEL), tile(D_MODEL), tile(D_MODEL),
                   pl.BlockSpec((1, 1), lambda i: (0, 0)), pl.BlockSpec((1, D_MODEL), lambda i: (0, 0))],
        out_shape=[SDS((seq, D_FF), BF16), SDS((seq, D_MODEL), BF16), SDS((seq, D_MODEL), F32),
                   SDS((seq, D_MODEL), BF16), SDS((1, 1), F32), SDS((1, D_MODEL), F32)],
        scratch=[pltpu.VMEM((tb, D_FF), BF16)],
        operands=(h, target, g_pre_mlp, g_post_mlp, w_up, w_down))


def _mlp_bwd(dmlp, up, h, dout, mix, g_pre_mlp, g_post_mix, w_up, w_down):
    seq = h.shape[0]
    tb = MLP_BWD_TOKEN_TILE

    def body(dmlp_ref, up_ref, h_ref, dout_ref, mix_ref, g2_ref, gpm_ref, wup_ref, wdown_ref,
             dup_ref, dh_ref, dmix_ref, dg2_ref, dgpm_ref):
        @pl.when(pl.program_id(0) == 0)
        def _():
            dg2_ref[...] = jnp.zeros_like(dg2_ref)
            dgpm_ref[...] = jnp.zeros_like(dgpm_ref)

        dmlp_v = dmlp_ref[...]
        halves = [None, None]
        for j in range(N_CHIPS):
            cols = slice(1024 * j, 1024 * (j + 1))
            dact = jnp.concatenate([_dot_nt(dmlp_v, wdown_ref[2 * j]), _dot_nt(dmlp_v, wdown_ref[2 * j + 1])], axis=1)
            dup = (dact * (2.0 * up_ref[:, cols].astype(F32))).astype(BF16)
            dup_ref[:, cols] = dup
            for half in range(2):
                part = _dot_nt(dup, wup_ref[2 * j + half])
                halves[half] = part if j == 0 else halves[half] + part
        dhn2 = jnp.concatenate(halves, axis=1)
        hv = h_ref[...]
        r2 = _rms(hv)
        hhat = hv * r2
        dg2_ref[...] += _colsum(dhn2 * hhat)
        dh = dout_ref[...] + _norm_bwd(dhn2, g2_ref[...], hhat, r2)
        dh_ref[...] = dh
        mix_v = mix_ref[...]
        rz = _rms(mix_v)
        zhat = mix_v * rz
        dgpm_ref[...] += _colsum(dh * zhat)
        dmix_ref[...] = _norm_bwd(dh, gpm_ref[...], zhat, rz).astype(BF16)

    tile = lambda w: pl.BlockSpec((tb, w), lambda i: (i, 0))
    vec = pl.BlockSpec((1, D_MODEL), lambda i: (0, 0))
    return _pallas(
        body, name="mlp_bwd", grid=(seq // tb,),
        in_specs=[tile(D_MODEL), tile(D_FF), tile(D_MODEL), tile(D_MODEL), tile(D_MODEL),
                  _resident((1, D_MODEL)), _resident((1, D_MODEL)), _resident(w_up.shape), _resident(w_down.shape)],
        out_specs=[tile(D_FF), tile(D_MODEL), tile(D_MODEL), vec, vec],
        out_shape=[SDS((seq, D_FF), BF16), SDS((seq, D_MODEL), F32), SDS((seq, D_MODEL), BF16),
                   SDS((1, D_MODEL), F32), SDS((1, D_MODEL), F32)],
        operands=(dmlp, up, h, dout, mix, g_pre_mlp, g_post_mix, w_up, w_down))


def _mix_bwd(dmix, attn, gb, gc, xin, conv_w, g_attn, g_conv, w_out, comm=None):
    seq = attn.shape[0]
    tb = TOKEN_TILE

    def body(dmix_ref, a_ref, gb_ref, gc_ref, xin_ref, gch_ref, xinh_ref, cw_ref, ga_ref, gcn_ref, w_ref,
             dattn_ref, dgb_ref, dy_ref, dga_ref, dgcn_ref, dcw_ref):
        first = pl.program_id(0) == 0

        @pl.when(first)
        def _():
            dga_ref[...] = jnp.zeros_like(dga_ref)
            dgcn_ref[...] = jnp.zeros_like(dgcn_ref)
            dcw_ref[...] = jnp.zeros_like(dcw_ref)

        dmixed = _dot_nt(dmix_ref[...], w_ref[...].reshape(D_MODEL, D_MODEL))
        a = a_ref[...]
        ra = _rms(a)
        ahat = a * ra
        dan = dmixed[:, 0:Q_WIDTH]
        dga_ref[...] += _colsum(dan * ahat)
        dattn_ref[...] = _norm_bwd(dan, ga_ref[...], ahat, ra).astype(BF16)
        gbv = gb_ref[...]
        u, u1, u2, y = _conv_parts(gbv, gc_ref[...], xin_ref[...], gch_ref[...], xinh_ref[...], cw_ref[...], first)
        conv = gbv * y
        rc = _rms(conv)
        chat = conv * rc
        dcn = dmixed[:, Q_WIDTH:]
        dgcn_ref[...] += _colsum(dcn * chat)
        dconv = _norm_bwd(dcn, gcn_ref[...], chat, rc)
        dgb_ref[...] = dconv * y
        dy = dconv * gbv
        dy_ref[...] = dy
        dcw_ref[0:1, :] += _colsum(dy * u2)
        dcw_ref[1:2, :] += _colsum(dy * u1)
        dcw_ref[2:3, :] += _colsum(dy * u)

    tile = lambda w: pl.BlockSpec((tb, w), lambda i: (i, 0))
    return _pallas(
        body, name="mix_bwd", grid=(seq // tb,),
        in_specs=[tile(D_MODEL), tile(Q_WIDTH), tile(CONV_WIDTH), tile(CONV_WIDTH), tile(CONV_WIDTH),
                  _halo_prev(tb, CONV_WIDTH), _halo_prev(tb, CONV_WIDTH),
                  _resident((CONV_K, CONV_WIDTH)), _resident((1, Q_WIDTH)), _resident((1, CONV_WIDTH)),
                  _resident(w_out.shape)],
        out_specs=[tile(Q_WIDTH), tile(CONV_WIDTH), tile(CONV_WIDTH),
                   pl.BlockSpec((1, Q_WIDTH), lambda i: (0, 0)), pl.BlockSpec((1, CONV_WIDTH), lambda i: (0, 0)),
                   pl.BlockSpec((CONV_K, CONV_WIDTH), lambda i: (0, 0))],
        out_shape=[SDS((seq, Q_WIDTH), BF16), SDS((seq, CONV_WIDTH), F32), SDS((seq, CONV_WIDTH), F32),
                   SDS((1, Q_WIDTH), F32), SDS((1, CONV_WIDTH), F32), SDS((CONV_K, CONV_WIDTH), F32)],
        operands=(dmix, attn, gb, gc, xin, gc, xin, conv_w, g_attn, g_conv, w_out), comm=comm)


def _attention_bwd(q, dattn, attn, kd0, kd1, vd0, vd1, sinks, comm=None):
    seq = q.shape[0]

    def body(sink_ref, q_ref, do_ref, o_ref, kd0_ref, kd1_ref, vd0_ref, vd1_ref,
             dq_ref, dk0_ref, dk1_ref, dv0_ref, dv1_ref, dsink_ref):
        i = pl.program_id(0)

        @pl.when(i == 0)
        def _():
            for r in (dk0_ref, dk1_ref, dv0_ref, dv1_ref, dsink_ref):
                r[...] = jnp.zeros_like(r)

        valid = _attn_valid(i)
        lane = lax.broadcasted_iota(jnp.int32, (1, 128), 1)
        dsink = jnp.zeros((1, 128), F32)
        for kv_head, (k_ref, v_ref, dk_ref, dv_ref) in enumerate(
                ((kd0_ref, vd0_ref, dk0_ref, dv0_ref), (kd1_ref, vd1_ref, dk1_ref, dv1_ref))):
            kband, prev, own = _band(k_ref, i)
            vband, _, _ = _band(v_ref, i)
            base = 256 * kv_head
            qm = _stack_heads(q_ref[:, base:base + 128], q_ref[:, base + 128:base + 256])
            dom = _stack_heads(do_ref[:, base:base + 128], do_ref[:, base + 128:base + 256])
            om = _stack_heads(o_ref[:, base:base + 128], o_ref[:, base + 128:base + 256])
            s = jnp.where(valid, _dot_nt(qm, kband), NEG_INF)
            p, e_sink, inv_l = _softmax_with_sink(s, _sink_column(sink_ref, kv_head))
            p = p * inv_l
            delta = jnp.sum(dom.astype(F32) * om, axis=-1, keepdims=True)
            ds = (p * (_dot_nt(dom, vband) - delta)).astype(BF16)
            sink_term = -(e_sink * inv_l) * delta
            for j in range(4):
                part = jnp.sum(sink_term[QBLOCK * j:QBLOCK * (j + 1)], axis=0, keepdims=True)
                dsink = dsink + jnp.where(lane == 4 * kv_head + j, part, 0.0)
            pair0, pair1 = _unstack_heads(_dot(ds, kband))
            dq_ref[:, base:base + 128] = pair0
            dq_ref[:, base + 128:base + 256] = pair1
            dkd = _dot_tn(ds, qm)
            dkd = dkd + pltpu.roll(dkd, HEAD_DIM, 1)
            dvd = _dot_tn(p.astype(BF16), dom)
            dvd = dvd + pltpu.roll(dvd, HEAD_DIM, 1)
            dk_ref[pl.ds(prev, QBLOCK), :] += dkd[0:QBLOCK]
            dk_ref[pl.ds(own, QBLOCK), :] += dkd[QBLOCK:]
            dv_ref[pl.ds(prev, QBLOCK), :] += dvd[0:QBLOCK]
            dv_ref[pl.ds(own, QBLOCK), :] += dvd[QBLOCK:]
        dsink_ref[...] += dsink

    blk = pl.BlockSpec((QBLOCK, Q_WIDTH), lambda i: (i, 0))
    full = _resident((seq, 128))
    acc = pl.BlockSpec((seq, 128), lambda i: (0, 0))
    return _pallas(
        body, name="attention_bwd", grid=(seq // QBLOCK,),
        in_specs=[pl.BlockSpec(memory_space=pltpu.SMEM), blk, blk, blk, full, full, full, full],
        out_specs=[blk, acc, acc, acc, acc, pl.BlockSpec((1, 128), lambda i: (0, 0))],
        out_shape=[SDS((seq, Q_WIDTH), F32)] + [SDS((seq, 128), F32)] * 4 + [SDS((1, 128), F32)],
        operands=(sinks, q, dattn, attn, kd0, kd1, vd0, vd1), comm=comm)


def _in_proj_bwd(dq, dk0, dk1, dv0, dv1, dgb, dy, gc, xin, conv_w, x, dh, g_pre, w_in_t, rope):
    seq = x.shape[0]
    tb = TOKEN_TILE
    n_tiles = seq // tb

    def body(dq_ref, dk0_ref, dk1_ref, dv0_ref, dv1_ref, dgb_ref, dy_ref, dyh_ref, gc_ref, xin_ref, cw_ref,
             x_ref, dh_ref, g_ref, w_ref, c_ref, sa_ref, sb_ref,
             dproj_ref, gx_ref, dg_ref):
        i = pl.program_id(0)

        @pl.when(i == 0)
        def _():
            dg_ref[...] = jnp.zeros_like(dg_ref)

        c, sa, sb = c_ref[...], sa_ref[...], sb_ref[...]
        scale = 1.0 / math.sqrt(HEAD_DIM)
        for p in range(Q_WIDTH // 128):
            dproj_ref[:, 128 * p:128 * (p + 1)] = _rope_transposed(
                dq_ref[:, 128 * p:128 * (p + 1)] * scale, c, sa, sb).astype(BF16)
        low = _lane_lt64((tb, 128))
        dk = jnp.where(low, dk0_ref[...], dk1_ref[...])
        dproj_ref[:, Q_WIDTH:Q_WIDTH + KV_WIDTH] = _rope_transposed(dk, c, sa, sb).astype(BF16)
        dproj_ref[:, Q_WIDTH + KV_WIDTH:Q_WIDTH + 2 * KV_WIDTH] = jnp.where(low, dv0_ref[...], dv1_ref[...]).astype(BF16)
        base = Q_WIDTH + 2 * KV_WIDTH
        dproj_ref[:, base:base + CONV_WIDTH] = dgb_ref[...].astype(BF16)
        dy = dy_ref[...]
        ext = jnp.concatenate([dy, jnp.where(i == n_tiles - 1, 0.0, dyh_ref[...])], axis=0)
        dy1 = pltpu.roll(ext, tb + 8 - 1, 0)[0:tb]
        dy2 = pltpu.roll(ext, tb + 8 - 2, 0)[0:tb]
        cw = cw_ref[...]
        du = cw[2:3, :] * dy + cw[1:2, :] * dy1 + cw[0:1, :] * dy2
        dproj_ref[:, base + CONV_WIDTH:base + 2 * CONV_WIDTH] = (du * xin_ref[...]).astype(BF16)
        dproj_ref[:, base + 2 * CONV_WIDTH:] = (du * gc_ref[...]).astype(BF16)
        dhn = _dot(dproj_ref[...], w_ref[...].reshape(IN_COLS, D_MODEL))
        xv = x_ref[...]
        r = _rms(xv)
        xhat = xv * r
        dg_ref[...] += _colsum(dhn * xhat)
        gx_ref[...] = dh_ref[...] + _norm_bwd(dhn, g_ref[...], xhat, r)

    tile = lambda w: pl.BlockSpec((tb, w), lambda i: (i, 0))
    halo_next = pl.BlockSpec((8, CONV_WIDTH), lambda i: (jnp.minimum((i + 1) * (tb // 8), seq // 8 - 1), 0))
    return _pallas(
        body, name="in_proj_bwd", grid=(n_tiles,),
        in_specs=[tile(Q_WIDTH), tile(128), tile(128), tile(128), tile(128), tile(CONV_WIDTH), tile(CONV_WIDTH), halo_next,
                  tile(CONV_WIDTH), tile(CONV_WIDTH), _resident((CONV_K, CONV_WIDTH)),
                  tile(D_MODEL), tile(D_MODEL), _resident((1, D_MODEL)), _resident(w_in_t.shape),
                  tile(128), tile(128), tile(128)],
        out_specs=[tile(IN_COLS), tile(D_MODEL), pl.BlockSpec((1, D_MODEL), lambda i: (0, 0))],
        out_shape=[SDS((seq, IN_COLS), BF16), SDS((seq, D_MODEL), F32), SDS((1, D_MODEL), F32)],
        operands=(dq, dk0, dk1, dv0, dv1, dgb, dy, dy, gc, xin, conv_w, x, dh, g_pre, w_in_t, *rope))


def _wgrad(name, a, b, *, per_chip, h_rows, square_a=False, comm=None):
    seq = a.shape[0]
    bt = WGRAD_TOKEN_TILE
    n_k = seq // bt
    chips_per_step = 1 if per_chip else N_CHIPS
    m = chips_per_step * 2 * h_rows
    a_cols = m if per_chip else a.shape[1]
    a_wide = a.shape[1] > a_cols
    b_wide = b.shape[1] > D_MODEL

    def body(a_ref, b_ref, g_ref, acc_ref):
        k = pl.program_id(1)

        @pl.when(k == 0)
        def _():
            acc_ref[...] = jnp.zeros_like(acc_ref)

        av = a_ref[...]
        if square_a:
            av = (av.astype(F32) * av.astype(F32)).astype(BF16)
        acc_ref[...] += _dot_tn(av, b_ref[...])

        @pl.when(k == n_k - 1)
        def _():
            for cidx in range(chips_per_step):
                for half in range(2):
                    r0 = (2 * cidx + half) * h_rows
                    g_ref[cidx, half] = acc_ref[r0:r0 + h_rows, :]

    a_spec = pl.BlockSpec((bt, a_cols), (lambda j, k: (k, j)) if a_wide else (lambda j, k: (k, 0)))
    b_spec = pl.BlockSpec((bt, D_MODEL), (lambda j, k: (k, j)) if b_wide else (lambda j, k: (k, 0)))
    g_spec = pl.BlockSpec((chips_per_step, 2, h_rows, D_MODEL), lambda j, k: (j, 0, 0, 0))
    return _pallas(
        body, name=name, grid=(N_CHIPS if per_chip else 1, n_k),
        in_specs=[a_spec, b_spec], out_specs=[g_spec], out_shape=[SDS((N_CHIPS, 2, h_rows, D_MODEL), F32)],
        scratch=[pltpu.VMEM((m, D_MODEL), F32)], operands=(a, b), comm=comm)


def _adamw_math(w, g, m, v):
    m = ADAM_B1 * m + (1.0 - ADAM_B1) * g
    v = ADAM_B2 * v + (1.0 - ADAM_B2) * (g * g)
    m_hat = m / (1.0 - ADAM_B1 ** ADAM_STEP)
    v_hat = v / (1.0 - ADAM_B2 ** ADAM_STEP)
    delta = -ADAM_LR * (m_hat / (jnp.sqrt(v_hat) + ADAM_EPS) + ADAM_WD * w)
    return delta, m, v


def _adamw_rows(name, reduced, w, m, v, rt):
    per_half = reduced.shape[1] // rt

    def body(r_ref, w_ref, m_ref, v_ref, g_out, d_out, m_out, v_out):
        g = r_ref[0]
        g_out[...] = g
        d_out[...], m_out[...], v_out[...] = _adamw_math(w_ref[...], g, m_ref[...], v_ref[...])

    blk = pl.BlockSpec((rt, D_MODEL), lambda h, r: (h * per_half + r, 0))
    return _pallas(
        body, name=name, grid=(2, per_half),
        in_specs=[pl.BlockSpec((1, rt, D_MODEL), lambda h, r: (h, r, 0)), blk, blk, blk],
        out_specs=[blk, blk, blk, blk], out_shape=[SDS(w.shape, F32)] * 4, operands=(reduced, w, m, v))


def _adamw_small(w, g, m, v):
    def body(w_ref, g_ref, m_ref, v_ref, d_out, m_out, v_out):
        d_out[...], m_out[...], v_out[...] = _adamw_math(w_ref[...], g_ref[...], m_ref[...], v_ref[...])

    return pl.pallas_call(body, name="adamw_small", in_specs=[VMEM_WHOLE] * 4, out_specs=[VMEM_WHOLE] * 3,
                          out_shape=[SDS(w.shape, F32)] * 3)(w, g, m, v)


SMALL_VECTORS = ("pre_mix_norm", "post_mix_norm", "pre_mlp_norm", "post_mlp_norm")
SMALL_NAMES = SMALL_VECTORS + ("attn_group_norm", "conv_group_norm", "conv_w", "attn_sinks")


def _pack_small(p):
    rows = [p[n].reshape(1, D_MODEL) for n in SMALL_VECTORS]
    rows.append(jnp.concatenate([p["attn_group_norm"].reshape(1, -1), p["conv_group_norm"].reshape(1, -1)], axis=1))
    cw = p["conv_w"].reshape(CONV_K, -1)
    rows.append(jnp.pad(cw, ((0, 1), (0, CONV_WIDTH - cw.shape[1]))).reshape(2, D_MODEL))
    rows.append(jnp.pad(p["attn_sinks"].reshape(1, -1), ((0, 0), (0, D_MODEL - 8))))
    return jnp.concatenate(rows, axis=0)


def _unpack_small(packed, conv_width):
    out = {n: packed[i:i + 1] for i, n in enumerate(SMALL_VECTORS)}
    out["attn_group_norm"] = packed[4:5, :Q_WIDTH]
    out["conv_group_norm"] = packed[4:5, Q_WIDTH:]
    out["conv_w"] = packed[5:7].reshape(4, CONV_WIDTH)[:CONV_K, :conv_width].reshape(1, CONV_K, conv_width)
    out["attn_sinks"] = packed[7:8, :8]
    return out


WEIGHT_ORDER = ("pre_mix_norm", "w_in", "conv_w", "attn_sinks", "attn_group_norm", "conv_group_norm", "w_out",
                "post_mix_norm", "pre_mlp_norm", "w_up", "w_down", "post_mlp_norm")


def kernel(x, pre_mix_norm, w_in, conv_w, attn_sinks, attn_group_norm, conv_group_norm, w_out, post_mix_norm, pre_mlp_norm, w_up, w_down, post_mlp_norm, loss_target, m_pre_mix_norm, m_w_in, m_conv_w, m_attn_sinks, m_attn_group_norm, m_conv_group_norm, m_w_out, m_post_mix_norm, m_pre_mlp_norm, m_w_up, m_w_down, m_post_mlp_norm, v_pre_mix_norm, v_w_in, v_conv_w, v_attn_sinks, v_attn_group_norm, v_conv_group_norm, v_w_out, v_post_mix_norm, v_pre_mlp_norm, v_w_up, v_w_down, v_post_mlp_norm):
    w = dict(pre_mix_norm=pre_mix_norm, w_in=w_in, conv_w=conv_w, attn_sinks=attn_sinks, attn_group_norm=attn_group_norm,
             conv_group_norm=conv_group_norm, w_out=w_out, post_mix_norm=post_mix_norm, pre_mlp_norm=pre_mlp_norm,
             w_up=w_up, w_down=w_down, post_mlp_norm=post_mlp_norm)
    m = dict(pre_mix_norm=m_pre_mix_norm, w_in=m_w_in, conv_w=m_conv_w, attn_sinks=m_attn_sinks,
             attn_group_norm=m_attn_group_norm, conv_group_norm=m_conv_group_norm, w_out=m_w_out,
             post_mix_norm=m_post_mix_norm, pre_mlp_norm=m_pre_mlp_norm, w_up=m_w_up, w_down=m_w_down,
             post_mlp_norm=m_post_mlp_norm)
    v = dict(pre_mix_norm=v_pre_mix_norm, w_in=v_w_in, conv_w=v_conv_w, attn_sinks=v_attn_sinks,
             attn_group_norm=v_attn_group_norm, conv_group_norm=v_conv_group_norm, w_out=v_w_out,
             post_mix_norm=v_post_mix_norm, pre_mlp_norm=v_pre_mlp_norm, w_up=v_w_up, w_down=v_w_down,
             post_mlp_norm=v_post_mlp_norm)
    core = lax.axis_index("c").astype(jnp.int32).reshape(1)
    chip = 2 * lax.axis_index("x") + lax.axis_index("y")
    local_conv = conv_w.shape[2]
    xs, target = x[0], loss_target[0]
    rope = _rope_tables(xs.shape[0])

    hb_up, hb_down, hb_out, hb_in = _cast_halves(core, w_up[0], w_down[0], w_out[0], w_in[0].T)
    gather_in = _gather_first(hb_in)
    conv_pad = jnp.pad(conv_w[0], ((0, 8 - CONV_K), (0, 0)))
    wf_in, conv_all = _comm_only("gather_in_first", _merge(gather_in, _gather_small(conv_pad)))
    wf_in, = _comm_only("gather_in_second", _gather_second(wf_in))
    conv_full = conv_all[:, :CONV_K, :].transpose(1, 0, 2).reshape(CONV_K, CONV_WIDTH)

    *proj, wf_up, wf_out = _in_proj(xs, pre_mix_norm, wf_in, rope, comm=_merge(_gather_first(hb_up), _gather_first(hb_out)))
    q, kd0, kd1, vd0, vd1, gb, gc, xin, hn = proj
    attn, wf_up, wf_out, wf_down = _attention_fwd(
        q, kd0, kd1, vd0, vd1, attn_sinks,
        comm=_merge(_gather_second(wf_up), _gather_second(wf_out), _gather_first(hb_down)))
    h, mix, mixed, wf_down = _mix_out(xs, attn, gb, gc, xin, conv_full, attn_group_norm, conv_group_norm, post_mix_norm,
                                      wf_out, comm=_gather_second(wf_down))
    up, hn2, dout, dmlp, loss_sum, dg_post_mlp = _mlp_loss(h, target, pre_mlp_norm, post_mlp_norm, wf_up, wf_down)
    loss = lax.psum(loss_sum[0, 0] * (0.5 / D_MODEL), ("x", "y", "c"))

    dup, dh, dmix, dg_pre_mlp, dg_post_mix = _mlp_bwd(dmlp, up, h, dout, mix, pre_mlp_norm, post_mix_norm, wf_up, wf_down)
    g_down, = _wgrad("wgrad_down", up, dmlp, per_chip=True, h_rows=H_DOWN, square_a=True)
    g_up, got_down = _wgrad("wgrad_up", hn2, dup, per_chip=True, h_rows=H_UP, comm=_pair_send(g_down))
    p_down = _pair_sum("pair_sum_down", core, g_down, got_down)
    dattn, dgb, dy, dg_attn, dg_conv, dconv_w, ex_down, got_up = _mix_bwd(
        dmix, attn, gb, gc, xin, conv_full, attn_group_norm, conv_group_norm, wf_out,
        comm=_merge(_chip_exchange(p_down), _pair_send(g_up)))
    p_up = _pair_sum("pair_sum_up", core, g_up, got_up)
    g_out, = _wgrad("wgrad_out", mixed, dmix, per_chip=False, h_rows=H_OUT)
    dq, dk0, dk1, dv0, dv1, dsink, ex_up, got_out = _attention_bwd(
        q, dattn, attn, kd0, kd1, vd0, vd1, attn_sinks, comm=_merge(_chip_exchange(p_up), _pair_send(g_out)))
    p_out = _pair_sum("pair_sum_out", core, g_out, got_out)
    dproj, grad_x, dg_pre_mix = _in_proj_bwd(dq, dk0, dk1, dv0, dv1, dgb, dy, gc, xin, conv_full, xs, dh, pre_mix_norm,
                                             wf_in, rope)
    g_in, ex_out = _wgrad("wgrad_in", dproj, hn, per_chip=False, h_rows=H_IN, comm=_chip_exchange(p_out))
    got_in, = _comm_only("pair_send_in", _pair_send(g_in))
    ex_in, = _comm_only("chip_exchange_in", _chip_exchange(_pair_sum("pair_sum_in", core, g_in, got_in)))
    r_down, r_up, r_out, r_in = _finish_reduce([ex_down, ex_up, ex_out, ex_in])

    out_g, out_d, out_m, out_v = {}, {}, {}, {}
    out_g["w_up"], out_d["w_up"], out_m["w_up"], out_v["w_up"] = _adamw_rows(
        "adamw_up", r_up, w_up[0], m_w_up[0], v_w_up[0], 256)
    out_g["w_down"], out_d["w_down"], out_m["w_down"], out_v["w_down"] = _adamw_rows(
        "adamw_down", r_down, w_down[0], m_w_down[0], v_w_down[0], 256)
    out_g["w_out"], out_d["w_out"], out_m["w_out"], out_v["w_out"] = _adamw_rows(
        "adamw_out", r_out, w_out[0], m_w_out[0], v_w_out[0], H_OUT)
    in_t = _adamw_rows("adamw_in", r_in, w_in[0].T, m_w_in[0].T, v_w_in[0].T, H_IN)
    out_g["w_in"], out_d["w_in"], out_m["w_in"], out_v["w_in"] = [t.T for t in in_t]

    small = dict(pre_mix_norm=dg_pre_mix, conv_w=dconv_w, attn_sinks=dsink[:, :8], attn_group_norm=dg_attn,
                 conv_group_norm=dg_conv, post_mix_norm=dg_post_mix, pre_mlp_norm=dg_pre_mlp, post_mlp_norm=dg_post_mlp)
    small_sum = _unpack_small(_all_reduce_small(_pack_small(small)), CONV_WIDTH)
    small_sum["conv_w"] = lax.dynamic_slice_in_dim(small_sum["conv_w"], chip * local_conv, local_conv, axis=2)
    packed = [_pack_small({n: t[n] for n in SMALL_NAMES}) for t in (w, small_sum, m, v)]
    small_d, small_m, small_v = [_unpack_small(t, local_conv) for t in _adamw_small(*packed)]
    for n in SMALL_NAMES:
        out_g[n], out_d[n], out_m[n], out_v[n] = small_sum[n], small_d[n], small_m[n], small_v[n]

    def shaped(d):
        return [d[n].reshape(w[n].shape) for n in WEIGHT_ORDER]

    return (loss, grad_x[None], *shaped(out_g), *shaped(out_d), *shaped(out_m), *shaped(out_v))
```

```python
import math
from typing import Callable, NamedTuple

import jax
import jax.numpy as jnp
import numpy as np
from jax import lax
from jax.experimental import pallas as pl
from jax.experimental.pallas import tpu as pltpu

F32 = jnp.float32
BF16 = jnp.bfloat16

D_MODEL = 1024
HEAD_DIM = 64
Q_WIDTH = 512
KV_WIDTH = 128
CONV_WIDTH = 512
CONV_K = 3
D_FF = 4096
IN_COLS = 2304
QBLOCK = 128
ROT_DIM = 16
ROPE_THETA = 500000.0
NORM_EPS = 1e-6
NEG_INF = -1e30
N_CHIPS = 4

ADAM_LR = 0.001
ADAM_B1 = 0.9
ADAM_B2 = 0.999
ADAM_EPS = 1e-08
ADAM_WD = 0.01
ADAM_STEP = 10

H_UP, H_DOWN, H_OUT, H_IN = 512, 512, 128, 288

TOKEN_TILE = 512
MLP_BWD_TOKEN_TILE = 256
ATTN_FWD_BLOCKS = 4
ATTN_BWD_BLOCKS = 2
WGRAD_TOKEN_TILE = 1024
VMEM_LIMIT_V7X = 56 * 1024 * 1024

MESH = pl.DeviceIdType.MESH
ANY = pl.BlockSpec(memory_space=pl.ANY)
VMEM_WHOLE = pl.BlockSpec(memory_space=pltpu.VMEM)
SDS = jax.ShapeDtypeStruct


def _resident(shape):
    zeros = (0,) * len(shape)
    return pl.BlockSpec(shape, lambda *_: zeros, pipeline_mode=pl.Buffered(1))


def _rms(v):
    return lax.rsqrt(jnp.mean(v * v, axis=-1, keepdims=True) + NORM_EPS)


def _norm_bwd(dy, gain, vhat, rstd):
    t = dy * gain
    return rstd * (t - vhat * jnp.mean(t * vhat, axis=-1, keepdims=True))


def _colsum(v):
    return jnp.sum(v, axis=0, keepdims=True)


def _dot_nt(a, b):
    return lax.dot_general(a, b, (((1,), (1,)), ((), ())), preferred_element_type=F32)


def _dot_tn(a, b):
    return lax.dot_general(a, b, (((0,), (0,)), ((), ())), preferred_element_type=F32)


def _dot(a, b):
    return jnp.dot(a, b, preferred_element_type=F32)


def _lane_lt64(shape):
    return lax.broadcasted_iota(jnp.int32, shape, 1) < HEAD_DIM


class _Comm(NamedTuple):
    operands: tuple
    out_shapes: tuple
    aliases: dict
    n_remote: int
    n_local: int
    plan: Callable


def _merge(*comms):
    operands, out_shapes, aliases, parts = [], [], {}, []
    n_remote = n_local = 0
    for cm in comms:
        parts.append((len(operands), len(out_shapes), n_remote, n_local, cm))
        for k, v in cm.aliases.items():
            aliases[len(operands) + k] = len(out_shapes) + v
        operands += cm.operands
        out_shapes += cm.out_shapes
        n_remote += cm.n_remote
        n_local += cm.n_local

    def plan(ins, outs, send, recv, loc):
        sends, recvs, locs = [], [], []
        for i0, o0, r0, l0, cm in parts:
            s, r, l = cm.plan(ins[i0:i0 + len(cm.operands)], outs[o0:o0 + len(cm.out_shapes)],
                              lambda k, r0=r0: send(r0 + k), lambda k, r0=r0: recv(r0 + k), lambda k, l0=l0: loc(l0 + k))
            sends, recvs, locs = sends + s, recvs + r, locs + l
        return sends, recvs, locs

    return _Comm(tuple(operands), tuple(out_shapes), aliases, n_remote, n_local, plan)


def _sem_scratch(comm):
    return [pltpu.SemaphoreType.DMA((max(comm.n_remote, 1),)), pltpu.SemaphoreType.DMA((max(comm.n_remote, 1),)),
            pltpu.SemaphoreType.DMA((max(comm.n_local, 1),))]


def _pallas(body, *, name, grid, in_specs, out_specs, out_shape, operands, scratch=(), comm=None):
    params = pltpu.CompilerParams(dimension_semantics=("arbitrary",) * len(grid), vmem_limit_bytes=VMEM_LIMIT_V7X)
    if comm is None:
        return pl.pallas_call(body, name=name, grid=grid, in_specs=in_specs, out_specs=out_specs, out_shape=out_shape,
                              scratch_shapes=list(scratch), compiler_params=params)(*operands)
    n_in, n_out, n_scr = len(in_specs), len(out_specs), len(scratch)
    c_in, c_out = len(comm.operands), len(comm.out_shapes)

    def with_comm(*refs):
        ins, c_ins = refs[:n_in], refs[n_in:n_in + c_in]
        o0 = n_in + c_in
        outs, c_outs = refs[o0:o0 + n_out], refs[o0 + n_out:o0 + n_out + c_out]
        s0 = o0 + n_out + c_out
        scr = refs[s0:s0 + n_scr]
        send_sems, recv_sems, local_sems = refs[s0 + n_scr:]
        first = last = None
        for axis, size in enumerate(grid):
            at_start, at_end = pl.program_id(axis) == 0, pl.program_id(axis) == size - 1
            first = at_start if first is None else jnp.logical_and(first, at_start)
            last = at_end if last is None else jnp.logical_and(last, at_end)

        def copies():
            return comm.plan(c_ins, c_outs, lambda k: send_sems.at[k], lambda k: recv_sems.at[k],
                             lambda k: local_sems.at[k])

        @pl.when(first)
        def _():
            sends, _, locs = copies()
            for cp in sends + locs:
                cp.start()

        body(*ins, *outs, *scr)

        @pl.when(last)
        def _():
            sends, recvs, locs = copies()
            for cp in recvs:
                cp.wait_recv()
            for cp in sends:
                cp.wait_send()
            for cp in locs:
                cp.wait()

    return pl.pallas_call(
        with_comm, name=name, grid=grid,
        in_specs=list(in_specs) + [ANY] * c_in, out_specs=list(out_specs) + [ANY] * c_out,
        out_shape=list(out_shape) + list(comm.out_shapes),
        scratch_shapes=list(scratch) + _sem_scratch(comm),
        input_output_aliases={n_in + k: n_out + v for k, v in comm.aliases.items()},
        compiler_params=params)(*operands, *comm.operands)


def _comm_only(name, comm):
    c_in, c_out = len(comm.operands), len(comm.out_shapes)

    def body(*refs):
        send_sems, recv_sems, local_sems = refs[c_in + c_out:]
        sends, recvs, locs = comm.plan(refs[:c_in], refs[c_in:c_in + c_out], lambda k: send_sems.at[k],
                                       lambda k: recv_sems.at[k], lambda k: local_sems.at[k])
        for cp in sends + locs:
            cp.start()
        for cp in recvs:
            cp.wait_recv()
        for cp in sends:
            cp.wait_send()
        for cp in locs:
            cp.wait()

    return pl.pallas_call(
        body, name=name, in_specs=[ANY] * c_in, out_specs=[ANY] * c_out, out_shape=list(comm.out_shapes),
        scratch_shapes=_sem_scratch(comm),
        input_output_aliases=dict(comm.aliases))(*comm.operands)


def _place():
    return lax.axis_index("x"), lax.axis_index("y"), lax.axis_index("c")


def _other_chips(x, y):
    return [(1 - x, y), (x, 1 - y), (1 - x, 1 - y)]


def _slot(px, py, pc):
    return 4 * px + 2 * py + pc


def _remote(src, dst, send_sem, recv_sem, to):
    return pltpu.make_async_remote_copy(src_ref=src, dst_ref=dst, send_sem=send_sem, recv_sem=recv_sem,
                                        device_id=to, device_id_type=MESH)


def _gather_first(half_block):
    def plan(ins, outs, send, recv, loc):
        (blk,), (full,) = ins, outs
        x, y, c = _place()
        chips = _other_chips(x, y)
        mine = full.at[_slot(x, y, c)]
        sends = [_remote(blk, mine, send(0), recv(0), (x, y, 1 - c))]
        sends += [_remote(blk, mine, send(1 + j), recv(1 + j), (*chip, c)) for j, chip in enumerate(chips)]
        recvs = [_remote(blk, full.at[_slot(x, y, 1 - c)], send(0), recv(0), (x, y, 1 - c))]
        recvs += [_remote(blk, full.at[_slot(*chip, c)], send(1 + j), recv(1 + j), (*chip, c))
                  for j, chip in enumerate(chips)]
        return sends, recvs, [pltpu.make_async_copy(blk, mine, loc(0))]

    return _Comm((half_block,), (SDS((2 * N_CHIPS,) + half_block.shape, half_block.dtype),), {}, 4, 1, plan)


def _gather_second(partly_gathered):
    def plan(ins, outs, send, recv, loc):
        (src,), (full,) = ins, outs
        x, y, c = _place()
        chips = _other_chips(x, y)
        sends = [_remote(src.at[_slot(*chip, c)], full.at[_slot(*chip, c)], send(j), recv(j), (x, y, 1 - c))
                 for j, chip in enumerate(chips)]
        recvs = [_remote(src.at[_slot(*chip, 1 - c)], full.at[_slot(*chip, 1 - c)], send(j), recv(j), (x, y, 1 - c))
                 for j, chip in enumerate(chips)]
        return sends, recvs, []

    return _Comm((partly_gathered,), (SDS(partly_gathered.shape, partly_gathered.dtype),), {0: 0}, 3, 0, plan)


def _gather_small(block):
    def plan(ins, outs, send, recv, loc):
        (blk,), (full,) = ins, outs
        x, y, c = _place()
        chips = _other_chips(x, y)
        sends = [_remote(blk, full.at[2 * x + y], send(j), recv(j), (*chip, c)) for j, chip in enumerate(chips)]
        recvs = [_remote(blk, full.at[2 * chip[0] + chip[1]], send(j), recv(j), (*chip, c))
                 for j, chip in enumerate(chips)]
        return sends, recvs, [pltpu.make_async_copy(blk, full.at[2 * x + y], loc(0))]

    return _Comm((block,), (SDS((N_CHIPS,) + block.shape, block.dtype),), {}, 3, 1, plan)


def _pair_send(grads):
    def plan(ins, outs, send, recv, loc):
        (g,), (got,) = ins, outs
        x, y, c = _place()
        copies = [_remote(g.at[j, 1 - c], got.at[j], send(j), recv(j), (x, y, 1 - c)) for j in range(N_CHIPS)]
        return copies, copies, []

    shape = (grads.shape[0],) + grads.shape[2:]
    return _Comm((grads,), (SDS(shape, grads.dtype),), {}, N_CHIPS, 0, plan)


def _chip_exchange(partial):
    def plan(ins, outs, send, recv, loc):
        (p,), (got,) = ins, outs
        x, y, c = _place()
        my_chip = 2 * x + y
        chips = _other_chips(x, y)
        sends = [_remote(p.at[2 * chip[0] + chip[1]], got.at[my_chip], send(j), recv(j), (*chip, c))
                 for j, chip in enumerate(chips)]
        recvs = [_remote(p.at[my_chip], got.at[2 * chip[0] + chip[1]], send(j), recv(j), (*chip, c))
                 for j, chip in enumerate(chips)]
        return sends, recvs, [pltpu.make_async_copy(p.at[my_chip], got.at[my_chip], loc(0))]

    return _Comm((partial,), (SDS(partial.shape, partial.dtype),), {}, 3, 1, plan)


def _pair_sum(name, core, grads, received):
    h = grads.shape[2]

    def body(core_ref, g_ref, r_ref, o_ref):
        o_ref[...] = (g_ref[0] + r_ref[...]).astype(BF16)

    return pl.pallas_call(
        body, name=name,
        grid_spec=pltpu.PrefetchScalarGridSpec(
            num_scalar_prefetch=1, grid=(N_CHIPS,),
            in_specs=[pl.BlockSpec((1, 1, h, D_MODEL), lambda j, core_ref: (j, core_ref[0], 0, 0)),
                      pl.BlockSpec((1, h, D_MODEL), lambda j, core_ref: (j, 0, 0))],
            out_specs=pl.BlockSpec((1, h, D_MODEL), lambda j, core_ref: (j, 0, 0))),
        out_shape=SDS((N_CHIPS, h, D_MODEL), BF16),
        compiler_params=pltpu.CompilerParams(dimension_semantics=("arbitrary",), vmem_limit_bytes=VMEM_LIMIT_V7X),
    )(core, grads, received)


def _finish_reduce(exchanged):
    n = len(exchanged)

    def body(*refs):
        got, out = refs[:n], refs[n:2 * n]
        halves = refs[2 * n:3 * n]
        send_sems, recv_sems, local_sems = refs[3 * n:]
        x, y, c = _place()
        sends, recvs, locs = [], [], []
        for k in range(n):
            g = got[k]
            halves[k][...] = ((g[0].astype(F32) + g[1].astype(F32)) + (g[2].astype(F32) + g[3].astype(F32)))
            locs.append(pltpu.make_async_copy(halves[k], out[k].at[c], local_sems.at[k]))
            sends.append(_remote(halves[k], out[k].at[c], send_sems.at[k], recv_sems.at[k], (x, y, 1 - c)))
            recvs.append(_remote(halves[k], out[k].at[1 - c], send_sems.at[k], recv_sems.at[k], (x, y, 1 - c)))
            locs[-1].start()
            sends[-1].start()
        for cp in recvs:
            cp.wait_recv()
        for cp in sends:
            cp.wait_send()
        for cp in locs:
            cp.wait()

    return pl.pallas_call(
        body, name="finish_reduce", in_specs=[VMEM_WHOLE] * n, out_specs=[ANY] * n,
        out_shape=[SDS((2,) + e.shape[1:], F32) for e in exchanged],
        scratch_shapes=[pltpu.VMEM(e.shape[1:], F32) for e in exchanged]
                       + [pltpu.SemaphoreType.DMA((n,)), pltpu.SemaphoreType.DMA((n,)), pltpu.SemaphoreType.DMA((n,))],
        compiler_params=pltpu.CompilerParams(vmem_limit_bytes=VMEM_LIMIT_V7X),
    )(*exchanged)


SMALL_ROWS = 8


def _all_reduce_small(packed):
    def body(v_ref, o_ref, buf_ref, send_sems, recv_sems):
        x, y, c = _place()
        me = _slot(x, y, c)
        buf_ref[me] = v_ref[...]
        copies = []
        for mask in range(1, 8):
            peer = (x ^ (mask >> 2), y ^ ((mask >> 1) & 1), c ^ (mask & 1))
            copies.append(_remote(v_ref, buf_ref.at[me], send_sems.at[mask - 1], recv_sems.at[mask - 1], peer))
        for cp in copies:
            cp.start()
        for cp in copies:
            cp.wait_recv()
        for cp in copies:
            cp.wait_send()
        total = buf_ref[0]
        for d in range(1, 8):
            total = total + buf_ref[d]
        o_ref[...] = total

    return pl.pallas_call(
        body, name="all_reduce_small", in_specs=[VMEM_WHOLE], out_specs=VMEM_WHOLE,
        out_shape=SDS(packed.shape, F32),
        scratch_shapes=[pltpu.VMEM((8,) + packed.shape, F32), pltpu.SemaphoreType.DMA((7,)), pltpu.SemaphoreType.DMA((7,))],
    )(packed)


def _rope_expansion():
    half = ROT_DIM // 2
    expand = np.zeros((2 * half, 3 * 128), np.float32)
    const = np.zeros((1, 3 * 128), np.float32)
    for lane in range(128):
        d = lane % HEAD_DIM
        if d < ROT_DIM:
            expand[d % half, lane] = 1.0
        else:
            const[0, lane] = 1.0
        if d < half:
            expand[half + d, 128 + lane] = -1.0
        elif d < ROT_DIM:
            expand[half + d - half, 256 + lane] = 1.0
    return expand, const


def _rope_tables(seq):
    pos = jnp.arange(seq, dtype=F32)
    inv_freq = ROPE_THETA ** (-jnp.arange(0, ROT_DIM, 2, dtype=F32) / ROT_DIM)
    ang_t = inv_freq[:, None] * pos[None, :]
    cs_t = jnp.concatenate([jnp.cos(ang_t), jnp.sin(ang_t)], axis=0)
    expand, const = _rope_expansion()
    return lax.dot_general(cs_t, jnp.asarray(expand), (((0,), (0,)), ((), ())),
                           precision=lax.Precision.HIGHEST, preferred_element_type=F32) + jnp.asarray(const)


def _rope_specs(tb):
    return [pl.BlockSpec((tb, 128), lambda i, k=k: (i, k)) for k in range(3)]


def _rope(t, c, sa, sb):
    half = ROT_DIM // 2
    return t * c + pltpu.roll(t, 128 - half, 1) * sa + pltpu.roll(t, half, 1) * sb


def _rope_transposed(dt, c, sa, sb):
    half = ROT_DIM // 2
    return dt * c + pltpu.roll(dt * sa, half, 1) + pltpu.roll(dt * sb, 128 - half, 1)


def _cast_halves(core, w_up, w_down, w_out, w_in_t):
    def body(core_ref, up_ref, down_ref, out_ref, in_ref, up_o, down_o, out_o, in_o):
        up_o[...] = up_ref[...].astype(BF16)
        down_o[...] = down_ref[...].astype(BF16)
        out_o[...] = out_ref[...].astype(BF16)
        in_o[...] = in_ref[...].astype(BF16)

    half = lambda rows: pl.BlockSpec((rows, D_MODEL), lambda i, core_ref: (core_ref[0], 0))
    whole = lambda rows: pl.BlockSpec((rows, D_MODEL), lambda i, core_ref: (0, 0))
    rows = (H_UP, H_DOWN, H_OUT, H_IN)
    return pl.pallas_call(
        body, name="cast_halves",
        grid_spec=pltpu.PrefetchScalarGridSpec(
            num_scalar_prefetch=1, grid=(1,), in_specs=[half(r) for r in rows], out_specs=[whole(r) for r in rows]),
        out_shape=[SDS((r, D_MODEL), BF16) for r in rows],
        compiler_params=pltpu.CompilerParams(dimension_semantics=("arbitrary",), vmem_limit_bytes=VMEM_LIMIT_V7X),
    )(core, w_up, w_down, w_out, w_in_t)


def _in_proj(x, g_pre, w_in_t, rope, comm=None):
    seq = x.shape[0]
    tb = TOKEN_TILE

    def body(x_ref, g_ref, w_ref, c_ref, sa_ref, sb_ref,
             q_ref, kd0_ref, kd1_ref, vd0_ref, vd1_ref, gb_ref, gc_ref, xin_ref, hn_ref):
        xv = x_ref[...]
        hn = (xv * _rms(xv) * g_ref[...]).astype(BF16)
        hn_ref[...] = hn
        proj = _dot_nt(hn, w_ref[...].reshape(IN_COLS, D_MODEL))
        c, sa, sb = c_ref[...], sa_ref[...], sb_ref[...]
        scale = 1.0 / math.sqrt(HEAD_DIM)
        for p in range(Q_WIDTH // 128):
            q_ref[:, 128 * p:128 * (p + 1)] = (_rope(proj[:, 128 * p:128 * (p + 1)], c, sa, sb) * scale).astype(BF16)
        k = _rope(proj[:, Q_WIDTH:Q_WIDTH + KV_WIDTH], c, sa, sb)
        v = proj[:, Q_WIDTH + KV_WIDTH:Q_WIDTH + 2 * KV_WIDTH]
        low = _lane_lt64(k.shape)
        k_sw, v_sw = pltpu.roll(k, HEAD_DIM, 1), pltpu.roll(v, HEAD_DIM, 1)
        kd0_ref[...] = jnp.where(low, k, k_sw).astype(BF16)
        kd1_ref[...] = jnp.where(low, k_sw, k).astype(BF16)
        vd0_ref[...] = jnp.where(low, v, v_sw).astype(BF16)
        vd1_ref[...] = jnp.where(low, v_sw, v).astype(BF16)
        base = Q_WIDTH + 2 * KV_WIDTH
        gb_ref[...] = proj[:, base:base + CONV_WIDTH]
        gc_ref[...] = proj[:, base + CONV_WIDTH:base + 2 * CONV_WIDTH]
        xin_ref[...] = proj[:, base + 2 * CONV_WIDTH:base + 3 * CONV_WIDTH]

    tile = lambda w: pl.BlockSpec((tb, w), lambda i: (i, 0))
    return _pallas(
        body, name="in_proj", grid=(seq // tb,),
        in_specs=[tile(D_MODEL), _resident((1, D_MODEL)), _resident(w_in_t.shape), *_rope_specs(tb)],
        out_specs=[tile(Q_WIDTH), tile(128), tile(128), tile(128), tile(128),
                   tile(CONV_WIDTH), tile(CONV_WIDTH), tile(CONV_WIDTH), tile(D_MODEL)],
        out_shape=[SDS((seq, Q_WIDTH), BF16)] + [SDS((seq, 128), BF16)] * 4
                  + [SDS((seq, CONV_WIDTH), F32)] * 3 + [SDS((seq, D_MODEL), BF16)],
        operands=(x, g_pre, w_in_t, rope, rope, rope), comm=comm)


def _attn_valid(i):
    shape = (4 * QBLOCK, 2 * QBLOCK)
    row = lax.broadcasted_iota(jnp.int32, shape, 0)
    col = lax.broadcasted_iota(jnp.int32, shape, 1)
    qi = row & (QBLOCK - 1)
    return (col > qi) & (col <= qi + QBLOCK) & ((col >= QBLOCK) | (i > 0))


def _stack_heads(pair0, pair1):
    low = _lane_lt64(pair0.shape)
    zero = jnp.zeros_like(pair0)
    return jnp.concatenate([jnp.where(low, pair0, zero), jnp.where(low, zero, pair0),
                            jnp.where(low, pair1, zero), jnp.where(low, zero, pair1)], axis=0)


def _unstack_heads(stacked):
    low = _lane_lt64((QBLOCK, 128))
    pair0 = jnp.where(low, stacked[0:QBLOCK], stacked[QBLOCK:2 * QBLOCK])
    pair1 = jnp.where(low, stacked[2 * QBLOCK:3 * QBLOCK], stacked[3 * QBLOCK:4 * QBLOCK])
    return pair0, pair1


def _sink_column(sink_ref, kv_head):
    row = lax.broadcasted_iota(jnp.int32, (4 * QBLOCK, 1), 0)
    s = [sink_ref[0, 4 * kv_head + j] for j in range(4)]
    return jnp.where(row < QBLOCK, s[0], jnp.where(row < 2 * QBLOCK, s[1], jnp.where(row < 3 * QBLOCK, s[2], s[3])))


def _band(ref, i):
    prev = pl.multiple_of(jnp.maximum(i - 1, 0) * QBLOCK, QBLOCK)
    own = pl.multiple_of(i * QBLOCK, QBLOCK)
    return jnp.concatenate([ref[pl.ds(prev, QBLOCK), :], ref[pl.ds(own, QBLOCK), :]], axis=0), prev, own


def _softmax_with_sink(s, sink_col):
    m = jnp.maximum(jnp.max(s, axis=-1, keepdims=True), sink_col)
    p = jnp.exp(s - m)
    e_sink = jnp.exp(sink_col - m)
    inv_l = 1.0 / (jnp.sum(p, axis=-1, keepdims=True) + e_sink)
    return p, e_sink, inv_l


def _attention_fwd(q, kd0, kd1, vd0, vd1, sinks, comm=None):
    seq = q.shape[0]

    nb = ATTN_FWD_BLOCKS

    def body(sink_ref, q_ref, kd0_ref, kd1_ref, vd0_ref, vd1_ref, o_ref):
        for b in range(nb):
            i = pl.program_id(0) * nb + b
            rows = slice(QBLOCK * b, QBLOCK * (b + 1))
            valid = _attn_valid(i)
            for kv_head, (k_ref, v_ref) in enumerate(((kd0_ref, vd0_ref), (kd1_ref, vd1_ref))):
                kband, _, _ = _band(k_ref, i)
                vband, _, _ = _band(v_ref, i)
                base = 256 * kv_head
                qm = _stack_heads(q_ref[rows, base:base + 128], q_ref[rows, base + 128:base + 256])
                s = jnp.where(valid, _dot_nt(qm, kband), NEG_INF)
                p, _, inv_l = _softmax_with_sink(s, _sink_column(sink_ref, kv_head))
                o = _dot(p.astype(BF16), vband) * inv_l
                pair0, pair1 = _unstack_heads(o)
                o_ref[rows, base:base + 128] = pair0
                o_ref[rows, base + 128:base + 256] = pair1

    blk = pl.BlockSpec((nb * QBLOCK, Q_WIDTH), lambda i: (i, 0))
    full = _resident((seq, 128))
    return _pallas(
        body, name="attention_fwd", grid=(seq // (nb * QBLOCK),),
        in_specs=[pl.BlockSpec(memory_space=pltpu.SMEM), blk, full, full, full, full],
        out_specs=[blk], out_shape=[SDS((seq, Q_WIDTH), F32)],
        operands=(sinks, q, kd0, kd1, vd0, vd1), comm=comm)


def _conv_parts(gb, gc, xin, gc_halo, xin_halo, conv_w, first):
    tb = gb.shape[0]
    u = gc * xin
    u_halo = jnp.where(first, 0.0, gc_halo * xin_halo)
    ext = jnp.concatenate([u_halo, u], axis=0)
    u1 = pltpu.roll(ext, 1, 0)[8:8 + tb]
    u2 = pltpu.roll(ext, 2, 0)[8:8 + tb]
    y = conv_w[0:1, :] * u2 + conv_w[1:2, :] * u1 + conv_w[2:3, :] * u
    return u, u1, u2, y


def _halo_prev(tb, w):
    return pl.BlockSpec((8, w), lambda i: (jnp.maximum(i * (tb // 8) - 1, 0), 0))


def _mix_out(x, attn, gb, gc, xin, conv_w, g_attn, g_conv, g_post_mix, w_out, comm=None):
    seq = x.shape[0]
    tb = TOKEN_TILE

    def body(x_ref, a_ref, gb_ref, gc_ref, xin_ref, gch_ref, xinh_ref, cw_ref, ga_ref, gcn_ref, gpm_ref, w_ref,
             h_ref, mix_ref, mixed_ref):
        first = pl.program_id(0) == 0
        _, _, _, y = _conv_parts(gb_ref[...], gc_ref[...], xin_ref[...], gch_ref[...], xinh_ref[...], cw_ref[...], first)
        conv = gb_ref[...] * y
        a = a_ref[...]
        mixed_ref[:, 0:Q_WIDTH] = (a * _rms(a) * ga_ref[...]).astype(BF16)
        mixed_ref[:, Q_WIDTH:] = (conv * _rms(conv) * gcn_ref[...]).astype(BF16)
        mix = _dot(mixed_ref[...], w_ref[...].reshape(D_MODEL, D_MODEL))
        mix_ref[...] = mix
        h_ref[...] = x_ref[...] + mix * _rms(mix) * gpm_ref[...]

    tile = lambda w: pl.BlockSpec((tb, w), lambda i: (i, 0))
    return _pallas(
        body, name="mix_out", grid=(seq // tb,),
        in_specs=[tile(D_MODEL), tile(Q_WIDTH), tile(CONV_WIDTH), tile(CONV_WIDTH), tile(CONV_WIDTH),
                  _halo_prev(tb, CONV_WIDTH), _halo_prev(tb, CONV_WIDTH),
                  _resident((CONV_K, CONV_WIDTH)), _resident((1, Q_WIDTH)), _resident((1, CONV_WIDTH)),
                  _resident((1, D_MODEL)), _resident(w_out.shape)],
        out_specs=[tile(D_MODEL), tile(D_MODEL), tile(D_MODEL)],
        out_shape=[SDS((seq, D_MODEL), F32), SDS((seq, D_MODEL), F32), SDS((seq, D_MODEL), BF16)],
        operands=(x, attn, gb, gc, xin, gc, xin, conv_w, g_attn, g_conv, g_post_mix, w_out), comm=comm)


def _mlp_loss(h, target, g_pre_mlp, g_post_mlp, w_up, w_down):
    seq = h.shape[0]
    tb = TOKEN_TILE

    def body(h_ref, t_ref, g2_ref, g4_ref, wup_ref, wdown_ref,
             up_ref, hn2_ref, dout_ref, dmlp_ref, loss_ref, dg4_ref, act_ref):
        @pl.when(pl.program_id(0) == 0)
        def _():
            loss_ref[...] = jnp.zeros_like(loss_ref)
            dg4_ref[...] = jnp.zeros_like(dg4_ref)

        hv = h_ref[...]
        hn2 = (hv * _rms(hv) * g2_ref[...]).astype(BF16)
        hn2_ref[...] = hn2
        for j in range(N_CHIPS):
            up = _dot(hn2[:, :H_UP], wup_ref[2 * j]) + _dot(hn2[:, H_UP:], wup_ref[2 * j + 1])
            up = jnp.maximum(up, 0.0)
            up_ref[:, 1024 * j:1024 * (j + 1)] = up.astype(BF16)
            act_ref[:, 1024 * j:1024 * (j + 1)] = (up * up).astype(BF16)
        mlp = _dot(act_ref[...], wdown_ref[...].reshape(D_FF, D_MODEL))
        rstd = _rms(mlp)
        zhat = mlp * rstd
        diff = hv + zhat * g4_ref[...] - t_ref[...]
        loss_ref[...] += jnp.sum(jnp.sum(diff * diff, axis=1, keepdims=True), axis=0, keepdims=True)
        dout = diff * (1.0 / D_MODEL)
        dout_ref[...] = dout
        dg4_ref[...] += _colsum(dout * zhat)
        dmlp_ref[...] = _norm_bwd(dout, g4_ref[...], zhat, rstd).astype(BF16)

    tile = lambda w: pl.BlockSpec((tb, w), lambda i: (i, 0))
    return _pallas(
        body, name="mlp_loss", grid=(seq // tb,),
        in_specs=[tile(D_MODEL), tile(D_MODEL), _resident((1, D_MODEL)), _resident((1, D_MODEL)),
                  _resident(w_up.shape), _resident(w_down.shape)],
        out_specs=[tile(D_FF), tile(D_MODEL), tile(D_MODEL), tile(D_MODEL),
                   pl.BlockSpec((1, 1), lambda i: (0, 0)), pl.BlockSpec((1, D_MODEL), lambda i: (0, 0))],
        out_shape=[SDS((seq, D_FF), BF16), SDS((seq, D_MODEL), BF16), SDS((seq, D_MODEL), F32),
                   SDS((seq, D_MODEL), BF16), SDS((1, 1), F32), SDS((1, D_MODEL), F32)],
        scratch=[pltpu.VMEM((tb, D_FF), BF16)],
        operands=(h, target, g_pre_mlp, g_post_mlp, w_up, w_down))


def _mlp_bwd(dmlp, up, h, dout, mix, g_pre_mlp, g_post_mix, w_up, w_down):
    seq = h.shape[0]
    tb = MLP_BWD_TOKEN_TILE

    def body(dmlp_ref, up_ref, h_ref, dout_ref, mix_ref, g2_ref, gpm_ref, wup_ref, wdown_ref,
             dup_ref, dh_ref, dmix_ref, dg2_ref, dgpm_ref):
        @pl.when(pl.program_id(0) == 0)
        def _():
            dg2_ref[...] = jnp.zeros_like(dg2_ref)
            dgpm_ref[...] = jnp.zeros_like(dgpm_ref)

        dmlp_v = dmlp_ref[...]
        halves = [None, None]
        for j in range(N_CHIPS):
            cols = slice(1024 * j, 1024 * (j + 1))
            dact = jnp.concatenate([_dot_nt(dmlp_v, wdown_ref[2 * j]), _dot_nt(dmlp_v, wdown_ref[2 * j + 1])], axis=1)
            dup = (dact * (2.0 * up_ref[:, cols].astype(F32))).astype(BF16)
            dup_ref[:, cols] = dup
            for half in range(2):
                part = _dot_nt(dup, wup_ref[2 * j + half])
                halves[half] = part if j == 0 else halves[half] + part
        dhn2 = jnp.concatenate(halves, axis=1)
        hv = h_ref[...]
        r2 = _rms(hv)
        hhat = hv * r2
        dg2_ref[...] += _colsum(dhn2 * hhat)
        dh = dout_ref[...] + _norm_bwd(dhn2, g2_ref[...], hhat, r2)
        dh_ref[...] = dh
        mix_v = mix_ref[...]
        rz = _rms(mix_v)
        zhat = mix_v * rz
        dgpm_ref[...] += _colsum(dh * zhat)
        dmix_ref[...] = _norm_bwd(dh, gpm_ref[...], zhat, rz).astype(BF16)

    tile = lambda w: pl.BlockSpec((tb, w), lambda i: (i, 0))
    vec = pl.BlockSpec((1, D_MODEL), lambda i: (0, 0))
    return _pallas(
        body, name="mlp_bwd", grid=(seq // tb,),
        in_specs=[tile(D_MODEL), tile(D_FF), tile(D_MODEL), tile(D_MODEL), tile(D_MODEL),
                  _resident((1, D_MODEL)), _resident((1, D_MODEL)), _resident(w_up.shape), _resident(w_down.shape)],
        out_specs=[tile(D_FF), tile(D_MODEL), tile(D_MODEL), vec, vec],
        out_shape=[SDS((seq, D_FF), BF16), SDS((seq, D_MODEL), F32), SDS((seq, D_MODEL), BF16),
                   SDS((1, D_MODEL), F32), SDS((1, D_MODEL), F32)],
        operands=(dmlp, up, h, dout, mix, g_pre_mlp, g_post_mix, w_up, w_down))


def _mix_bwd(dmix, attn, gb, gc, xin, conv_w, g_attn, g_conv, w_out, comm=None):
    seq = attn.shape[0]
    tb = TOKEN_TILE

    def body(dmix_ref, a_ref, gb_ref, gc_ref, xin_ref, gch_ref, xinh_ref, cw_ref, ga_ref, gcn_ref, w_ref,
             dattn_ref, dgb_ref, dy_ref, dga_ref, dgcn_ref, dcw_ref):
        first = pl.program_id(0) == 0

        @pl.when(first)
        def _():
            dga_ref[...] = jnp.zeros_like(dga_ref)
            dgcn_ref[...] = jnp.zeros_like(dgcn_ref)
            dcw_ref[...] = jnp.zeros_like(dcw_ref)

        dmixed = _dot_nt(dmix_ref[...], w_ref[...].reshape(D_MODEL, D_MODEL))
        a = a_ref[...]
        ra = _rms(a)
        ahat = a * ra
        dan = dmixed[:, 0:Q_WIDTH]
        dga_ref[...] += _colsum(dan * ahat)
        dattn_ref[...] = _norm_bwd(dan, ga_ref[...], ahat, ra).astype(BF16)
        gbv = gb_ref[...]
        u, u1, u2, y = _conv_parts(gbv, gc_ref[...], xin_ref[...], gch_ref[...], xinh_ref[...], cw_ref[...], first)
        conv = gbv * y
        rc = _rms(conv)
        chat = conv * rc
        dcn = dmixed[:, Q_WIDTH:]
        dgcn_ref[...] += _colsum(dcn * chat)
        dconv = _norm_bwd(dcn, gcn_ref[...], chat, rc)
        dgb_ref[...] = dconv * y
        dy = dconv * gbv
        dy_ref[...] = dy
        dcw_ref[0:1, :] += _colsum(dy * u2)
        dcw_ref[1:2, :] += _colsum(dy * u1)
        dcw_ref[2:3, :] += _colsum(dy * u)

    tile = lambda w: pl.BlockSpec((tb, w), lambda i: (i, 0))
    return _pallas(
        body, name="mix_bwd", grid=(seq // tb,),
        in_specs=[tile(D_MODEL), tile(Q_WIDTH), tile(CONV_WIDTH), tile(CONV_WIDTH), tile(CONV_WIDTH),
                  _halo_prev(tb, CONV_WIDTH), _halo_prev(tb, CONV_WIDTH),
                  _resident((CONV_K, CONV_WIDTH)), _resident((1, Q_WIDTH)), _resident((1, CONV_WIDTH)),
                  _resident(w_out.shape)],
        out_specs=[tile(Q_WIDTH), tile(CONV_WIDTH), tile(CONV_WIDTH),
                   pl.BlockSpec((1, Q_WIDTH), lambda i: (0, 0)), pl.BlockSpec((1, CONV_WIDTH), lambda i: (0, 0)),
                   pl.BlockSpec((CONV_K, CONV_WIDTH), lambda i: (0, 0))],
        out_shape=[SDS((seq, Q_WIDTH), BF16), SDS((seq, CONV_WIDTH), F32), SDS((seq, CONV_WIDTH), F32),
                   SDS((1, Q_WIDTH), F32), SDS((1, CONV_WIDTH), F32), SDS((CONV_K, CONV_WIDTH), F32)],
        operands=(dmix, attn, gb, gc, xin, gc, xin, conv_w, g_attn, g_conv, w_out), comm=comm)


def _attention_bwd(q, dattn, attn, kd0, kd1, vd0, vd1, sinks, comm=None):
    seq = q.shape[0]
    nb = ATTN_BWD_BLOCKS

    def body(sink_ref, q_ref, do_ref, o_ref, kd0_ref, kd1_ref, vd0_ref, vd1_ref,
             dq_ref, dk0_ref, dk1_ref, dv0_ref, dv1_ref, dsink_ref):
        @pl.when(pl.program_id(0) == 0)
        def _():
            for r in (dk0_ref, dk1_ref, dv0_ref, dv1_ref, dsink_ref):
                r[...] = jnp.zeros_like(r)

        lane = lax.broadcasted_iota(jnp.int32, (1, 128), 1)
        dsink = jnp.zeros((1, 128), F32)
        for b in range(nb):
            i = pl.program_id(0) * nb + b
            rows = slice(QBLOCK * b, QBLOCK * (b + 1))
            valid = _attn_valid(i)
            for kv_head, (k_ref, v_ref, dk_ref, dv_ref) in enumerate(
                    ((kd0_ref, vd0_ref, dk0_ref, dv0_ref), (kd1_ref, vd1_ref, dk1_ref, dv1_ref))):
                kband, prev, own = _band(k_ref, i)
                vband, _, _ = _band(v_ref, i)
                base = 256 * kv_head
                qm = _stack_heads(q_ref[rows, base:base + 128], q_ref[rows, base + 128:base + 256])
                dom = _stack_heads(do_ref[rows, base:base + 128], do_ref[rows, base + 128:base + 256])
                om = _stack_heads(o_ref[rows, base:base + 128], o_ref[rows, base + 128:base + 256])
                s = jnp.where(valid, _dot_nt(qm, kband), NEG_INF)
                p, e_sink, inv_l = _softmax_with_sink(s, _sink_column(sink_ref, kv_head))
                p = p * inv_l
                delta = jnp.sum(dom.astype(F32) * om, axis=-1, keepdims=True)
                ds = (p * (_dot_nt(dom, vband) - delta)).astype(BF16)
                sink_term = -(e_sink * inv_l) * delta
                for j in range(4):
                    part = jnp.sum(sink_term[QBLOCK * j:QBLOCK * (j + 1)], axis=0, keepdims=True)
                    dsink = dsink + jnp.where(lane == 4 * kv_head + j, part, 0.0)
                pair0, pair1 = _unstack_heads(_dot(ds, kband))
                dq_ref[rows, base:base + 128] = pair0
                dq_ref[rows, base + 128:base + 256] = pair1
                dkd = _dot_tn(ds, qm)
                dkd = dkd + pltpu.roll(dkd, HEAD_DIM, 1)
                dvd = _dot_tn(p.astype(BF16), dom)
                dvd = dvd + pltpu.roll(dvd, HEAD_DIM, 1)
                dk_ref[pl.ds(prev, QBLOCK), :] += dkd[0:QBLOCK]
                dk_ref[pl.ds(own, QBLOCK), :] += dkd[QBLOCK:]
                dv_ref[pl.ds(prev, QBLOCK), :] += dvd[0:QBLOCK]
                dv_ref[pl.ds(own, QBLOCK), :] += dvd[QBLOCK:]
        dsink_ref[...] += dsink

    blk = pl.BlockSpec((nb * QBLOCK, Q_WIDTH), lambda i: (i, 0))
    full = _resident((seq, 128))
    acc = pl.BlockSpec((seq, 128), lambda i: (0, 0))
    return _pallas(
        body, name="attention_bwd", grid=(seq // (nb * QBLOCK),),
        in_specs=[pl.BlockSpec(memory_space=pltpu.SMEM), blk, blk, blk, full, full, full, full],
        out_specs=[blk, acc, acc, acc, acc, pl.BlockSpec((1, 128), lambda i: (0, 0))],
        out_shape=[SDS((seq, Q_WIDTH), F32)] + [SDS((seq, 128), F32)] * 4 + [SDS((1, 128), F32)],
        operands=(sinks, q, dattn, attn, kd0, kd1, vd0, vd1), comm=comm)


def _in_proj_bwd(dq, dk0, dk1, dv0, dv1, dgb, dy, gc, xin, conv_w, x, dh, g_pre, w_in_t, rope):
    seq = x.shape[0]
    tb = TOKEN_TILE
    n_tiles = seq // tb

    def body(dq_ref, dk0_ref, dk1_ref, dv0_ref, dv1_ref, dgb_ref, dy_ref, dyh_ref, gc_ref, xin_ref, cw_ref,
             x_ref, dh_ref, g_ref, w_ref, c_ref, sa_ref, sb_ref,
             dproj_ref, gx_ref, dg_ref):
        i = pl.program_id(0)

        @pl.when(i == 0)
        def _():
            dg_ref[...] = jnp.zeros_like(dg_ref)

        c, sa, sb = c_ref[...], sa_ref[...], sb_ref[...]
        scale = 1.0 / math.sqrt(HEAD_DIM)
        for p in range(Q_WIDTH // 128):
            dproj_ref[:, 128 * p:128 * (p + 1)] = _rope_transposed(
                dq_ref[:, 128 * p:128 * (p + 1)] * scale, c, sa, sb).astype(BF16)
        low = _lane_lt64((tb, 128))
        dk = jnp.where(low, dk0_ref[...], dk1_ref[...])
        dproj_ref[:, Q_WIDTH:Q_WIDTH + KV_WIDTH] = _rope_transposed(dk, c, sa, sb).astype(BF16)
        dproj_ref[:, Q_WIDTH + KV_WIDTH:Q_WIDTH + 2 * KV_WIDTH] = jnp.where(low, dv0_ref[...], dv1_ref[...]).astype(BF16)
        base = Q_WIDTH + 2 * KV_WIDTH
        dproj_ref[:, base:base + CONV_WIDTH] = dgb_ref[...].astype(BF16)
        dy = dy_ref[...]
        ext = jnp.concatenate([dy, jnp.where(i == n_tiles - 1, 0.0, dyh_ref[...])], axis=0)
        dy1 = pltpu.roll(ext, tb + 8 - 1, 0)[0:tb]
        dy2 = pltpu.roll(ext, tb + 8 - 2, 0)[0:tb]
        cw = cw_ref[...]
        du = cw[2:3, :] * dy + cw[1:2, :] * dy1 + cw[0:1, :] * dy2
        dproj_ref[:, base + CONV_WIDTH:base + 2 * CONV_WIDTH] = (du * xin_ref[...]).astype(BF16)
        dproj_ref[:, base + 2 * CONV_WIDTH:] = (du * gc_ref[...]).astype(BF16)
        dhn = _dot(dproj_ref[...], w_ref[...].reshape(IN_COLS, D_MODEL))
        xv = x_ref[...]
        r = _rms(xv)
        xhat = xv * r
        dg_ref[...] += _colsum(dhn * xhat)
        gx_ref[...] = dh_ref[...] + _norm_bwd(dhn, g_ref[...], xhat, r)

    tile = lambda w: pl.BlockSpec((tb, w), lambda i: (i, 0))
    halo_next = pl.BlockSpec((8, CONV_WIDTH), lambda i: (jnp.minimum((i + 1) * (tb // 8), seq // 8 - 1), 0))
    return _pallas(
        body, name="in_proj_bwd", grid=(n_tiles,),
        in_specs=[tile(Q_WIDTH), tile(128), tile(128), tile(128), tile(128), tile(CONV_WIDTH), tile(CONV_WIDTH), halo_next,
                  tile(CONV_WIDTH), tile(CONV_WIDTH), _resident((CONV_K, CONV_WIDTH)),
                  tile(D_MODEL), tile(D_MODEL), _resident((1, D_MODEL)), _resident(w_in_t.shape), *_rope_specs(tb)],
        out_specs=[tile(IN_COLS), tile(D_MODEL), pl.BlockSpec((1, D_MODEL), lambda i: (0, 0))],
        out_shape=[SDS((seq, IN_COLS), BF16), SDS((seq, D_MODEL), F32), SDS((1, D_MODEL), F32)],
        operands=(dq, dk0, dk1, dv0, dv1, dgb, dy, dy, gc, xin, conv_w, x, dh, g_pre, w_in_t, rope, rope, rope))


def _wgrad(name, a, b, *, per_chip, h_rows, square_a=False, comm=None):
    seq = a.shape[0]
    bt = WGRAD_TOKEN_TILE
    n_k = seq // bt
    chips_per_step = 1 if per_chip else N_CHIPS
    m = chips_per_step * 2 * h_rows
    a_cols = m if per_chip else a.shape[1]
    a_wide = a.shape[1] > a_cols
    b_wide = b.shape[1] > D_MODEL

    def body(a_ref, b_ref, g_ref, acc_ref):
        k = pl.program_id(1)

        @pl.when(k == 0)
        def _():
            acc_ref[...] = jnp.zeros_like(acc_ref)

        av = a_ref[...]
        if square_a:
            av = (av.astype(F32) * av.astype(F32)).astype(BF16)
        acc_ref[...] += _dot_tn(av, b_ref[...])

        @pl.when(k == n_k - 1)
        def _():
            for cidx in range(chips_per_step):
                for half in range(2):
                    r0 = (2 * cidx + half) * h_rows
                    g_ref[cidx, half] = acc_ref[r0:r0 + h_rows, :]

    a_spec = pl.BlockSpec((bt, a_cols), (lambda j, k: (k, j)) if a_wide else (lambda j, k: (k, 0)))
    b_spec = pl.BlockSpec((bt, D_MODEL), (lambda j, k: (k, j)) if b_wide else (lambda j, k: (k, 0)))
    g_spec = pl.BlockSpec((chips_per_step, 2, h_rows, D_MODEL), lambda j, k: (j, 0, 0, 0))
    return _pallas(
        body, name=name, grid=(N_CHIPS if per_chip else 1, n_k),
        in_specs=[a_spec, b_spec], out_specs=[g_spec], out_shape=[SDS((N_CHIPS, 2, h_rows, D_MODEL), F32)],
        scratch=[pltpu.VMEM((m, D_MODEL), F32)], operands=(a, b), comm=comm)


def _adamw_math(w, g, m, v):
    m = ADAM_B1 * m + (1.0 - ADAM_B1) * g
    v = ADAM_B2 * v + (1.0 - ADAM_B2) * (g * g)
    m_hat = m / (1.0 - ADAM_B1 ** ADAM_STEP)
    v_hat = v / (1.0 - ADAM_B2 ** ADAM_STEP)
    delta = -ADAM_LR * (m_hat / (jnp.sqrt(v_hat) + ADAM_EPS) + ADAM_WD * w)
    return delta, m, v


def _adamw_rows(name, reduced, w, m, v, rt):
    per_half = reduced.shape[1] // rt

    def body(r_ref, w_ref, m_ref, v_ref, g_out, d_out, m_out, v_out):
        g = r_ref[0]
        g_out[...] = g
        d_out[...], m_out[...], v_out[...] = _adamw_math(w_ref[...], g, m_ref[...], v_ref[...])

    blk = pl.BlockSpec((rt, D_MODEL), lambda h, r: (h * per_half + r, 0))
    return _pallas(
        body, name=name, grid=(2, per_half),
        in_specs=[pl.BlockSpec((1, rt, D_MODEL), lambda h, r: (h, r, 0)), blk, blk, blk],
        out_specs=[blk, blk, blk, blk], out_shape=[SDS(w.shape, F32)] * 4, operands=(reduced, w, m, v))


def _adamw_small(w, g, m, v):
    def body(w_ref, g_ref, m_ref, v_ref, d_out, m_out, v_out):
        d_out[...], m_out[...], v_out[...] = _adamw_math(w_ref[...], g_ref[...], m_ref[...], v_ref[...])

    return pl.pallas_call(body, name="adamw_small", in_specs=[VMEM_WHOLE] * 4, out_specs=[VMEM_WHOLE] * 3,
                          out_shape=[SDS(w.shape, F32)] * 3)(w, g, m, v)


SMALL_VECTORS = ("pre_mix_norm", "post_mix_norm", "pre_mlp_norm", "post_mlp_norm")
SMALL_NAMES = SMALL_VECTORS + ("attn_group_norm", "conv_group_norm", "conv_w", "attn_sinks")


def _pack_small(p):
    rows = [p[n].reshape(1, D_MODEL) for n in SMALL_VECTORS]
    rows.append(jnp.concatenate([p["attn_group_norm"].reshape(1, -1), p["conv_group_norm"].reshape(1, -1)], axis=1))
    cw = p["conv_w"].reshape(CONV_K, -1)
    rows.append(jnp.pad(cw, ((0, 1), (0, CONV_WIDTH - cw.shape[1]))).reshape(2, D_MODEL))
    last = jnp.concatenate([p["attn_sinks"].reshape(1, 8), p.get("loss_sum", jnp.zeros((1, 1), F32))], axis=1)
    rows.append(jnp.pad(last, ((0, 0), (0, D_MODEL - 9))))
    return jnp.concatenate(rows, axis=0)


def _unpack_small(packed, conv_width):
    out = {n: packed[i:i + 1] for i, n in enumerate(SMALL_VECTORS)}
    out["attn_group_norm"] = packed[4:5, :Q_WIDTH]
    out["conv_group_norm"] = packed[4:5, Q_WIDTH:]
    out["conv_w"] = packed[5:7].reshape(4, CONV_WIDTH)[:CONV_K, :conv_width].reshape(1, CONV_K, conv_width)
    out["attn_sinks"] = packed[7:8, :8]
    out["loss_sum"] = packed[7, 8]
    return out


WEIGHT_ORDER = ("pre_mix_norm", "w_in", "conv_w", "attn_sinks", "attn_group_norm", "conv_group_norm", "w_out",
                "post_mix_norm", "pre_mlp_norm", "w_up", "w_down", "post_mlp_norm")


def kernel(x, pre_mix_norm, w_in, conv_w, attn_sinks, attn_group_norm, conv_group_norm, w_out, post_mix_norm, pre_mlp_norm, w_up, w_down, post_mlp_norm, loss_target, m_pre_mix_norm, m_w_in, m_conv_w, m_attn_sinks, m_attn_group_norm, m_conv_group_norm, m_w_out, m_post_mix_norm, m_pre_mlp_norm, m_w_up, m_w_down, m_post_mlp_norm, v_pre_mix_norm, v_w_in, v_conv_w, v_attn_sinks, v_attn_group_norm, v_conv_group_norm, v_w_out, v_post_mix_norm, v_pre_mlp_norm, v_w_up, v_w_down, v_post_mlp_norm):
    w = dict(pre_mix_norm=pre_mix_norm, w_in=w_in, conv_w=conv_w, attn_sinks=attn_sinks, attn_group_norm=attn_group_norm,
             conv_group_norm=conv_group_norm, w_out=w_out, post_mix_norm=post_mix_norm, pre_mlp_norm=pre_mlp_norm,
             w_up=w_up, w_down=w_down, post_mlp_norm=post_mlp_norm)
    m = dict(pre_mix_norm=m_pre_mix_norm, w_in=m_w_in, conv_w=m_conv_w, attn_sinks=m_attn_sinks,
             attn_group_norm=m_attn_group_norm, conv_group_norm=m_conv_group_norm, w_out=m_w_out,
             post_mix_norm=m_post_mix_norm, pre_mlp_norm=m_pre_mlp_norm, w_up=m_w_up, w_down=m_w_down,
             post_mlp_norm=m_post_mlp_norm)
    v = dict(pre_mix_norm=v_pre_mix_norm, w_in=v_w_in, conv_w=v_conv_w, attn_sinks=v_attn_sinks,
             attn_group_norm=v_attn_group_norm, conv_group_norm=v_conv_group_norm, w_out=v_w_out,
             post_mix_norm=v_post_mix_norm, pre_mlp_norm=v_pre_mlp_norm, w_up=v_w_up, w_down=v_w_down,
             post_mlp_norm=v_post_mlp_norm)
    core = lax.axis_index("c").astype(jnp.int32).reshape(1)
    chip = 2 * lax.axis_index("x") + lax.axis_index("y")
    local_conv = conv_w.shape[2]
    xs, target = x[0], loss_target[0]
    rope = _rope_tables(xs.shape[0])

    hb_up, hb_down, hb_out, hb_in = _cast_halves(core, w_up[0], w_down[0], w_out[0], w_in[0].T)
    gather_in = _gather_first(hb_in)
    conv_pad = jnp.pad(conv_w[0], ((0, 8 - CONV_K), (0, 0)))
    wf_in, conv_all = _comm_only("gather_in_first", _merge(gather_in, _gather_small(conv_pad)))
    wf_in, = _comm_only("gather_in_second", _gather_second(wf_in))
    conv_full = conv_all[:, :CONV_K, :].transpose(1, 0, 2).reshape(CONV_K, CONV_WIDTH)

    *proj, wf_up, wf_out = _in_proj(xs, pre_mix_norm, wf_in, rope, comm=_merge(_gather_first(hb_up), _gather_first(hb_out)))
    q, kd0, kd1, vd0, vd1, gb, gc, xin, hn = proj
    attn, wf_up, wf_out, wf_down = _attention_fwd(
        q, kd0, kd1, vd0, vd1, attn_sinks,
        comm=_merge(_gather_second(wf_up), _gather_second(wf_out), _gather_first(hb_down)))
    h, mix, mixed, wf_down = _mix_out(xs, attn, gb, gc, xin, conv_full, attn_group_norm, conv_group_norm, post_mix_norm,
                                      wf_out, comm=_gather_second(wf_down))
    up, hn2, dout, dmlp, loss_sum, dg_post_mlp = _mlp_loss(h, target, pre_mlp_norm, post_mlp_norm, wf_up, wf_down)

    dup, dh, dmix, dg_pre_mlp, dg_post_mix = _mlp_bwd(dmlp, up, h, dout, mix, pre_mlp_norm, post_mix_norm, wf_up, wf_down)
    g_down, = _wgrad("wgrad_down", up, dmlp, per_chip=True, h_rows=H_DOWN, square_a=True)
    g_up, got_down = _wgrad("wgrad_up", hn2, dup, per_chip=True, h_rows=H_UP, comm=_pair_send(g_down))
    p_down = _pair_sum("pair_sum_down", core, g_down, got_down)
    dattn, dgb, dy, dg_attn, dg_conv, dconv_w, ex_down, got_up = _mix_bwd(
        dmix, attn, gb, gc, xin, conv_full, attn_group_norm, conv_group_norm, wf_out,
        comm=_merge(_chip_exchange(p_down), _pair_send(g_up)))
    p_up = _pair_sum("pair_sum_up", core, g_up, got_up)
    g_out, = _wgrad("wgrad_out", mixed, dmix, per_chip=False, h_rows=H_OUT)
    dq, dk0, dk1, dv0, dv1, dsink, ex_up, got_out = _attention_bwd(
        q, dattn, attn, kd0, kd1, vd0, vd1, attn_sinks, comm=_merge(_chip_exchange(p_up), _pair_send(g_out)))
    p_out = _pair_sum("pair_sum_out", core, g_out, got_out)
    dproj, grad_x, dg_pre_mix = _in_proj_bwd(dq, dk0, dk1, dv0, dv1, dgb, dy, gc, xin, conv_full, xs, dh, pre_mix_norm,
                                             wf_in, rope)
    g_in, ex_out = _wgrad("wgrad_in", dproj, hn, per_chip=False, h_rows=H_IN, comm=_chip_exchange(p_out))
    got_in, = _comm_only("pair_send_in", _pair_send(g_in))
    ex_in, = _comm_only("chip_exchange_in", _chip_exchange(_pair_sum("pair_sum_in", core, g_in, got_in)))
    r_down, r_up, r_out, r_in = _finish_reduce([ex_down, ex_up, ex_out, ex_in])

    out_g, out_d, out_m, out_v = {}, {}, {}, {}
    out_g["w_up"], out_d["w_up"], out_m["w_up"], out_v["w_up"] = _adamw_rows(
        "adamw_up", r_up, w_up[0], m_w_up[0], v_w_up[0], 256)
    out_g["w_down"], out_d["w_down"], out_m["w_down"], out_v["w_down"] = _adamw_rows(
        "adamw_down", r_down, w_down[0], m_w_down[0], v_w_down[0], 256)
    out_g["w_out"], out_d["w_out"], out_m["w_out"], out_v["w_out"] = _adamw_rows(
        "adamw_out", r_out, w_out[0], m_w_out[0], v_w_out[0], H_OUT)
    in_t = _adamw_rows("adamw_in", r_in, w_in[0].T, m_w_in[0].T, v_w_in[0].T, H_IN)
    out_g["w_in"], out_d["w_in"], out_m["w_in"], out_v["w_in"] = [t.T for t in in_t]

    small = dict(pre_mix_norm=dg_pre_mix, conv_w=dconv_w, attn_sinks=dsink[:, :8], attn_group_norm=dg_attn,
                 conv_group_norm=dg_conv, post_mix_norm=dg_post_mix, pre_mlp_norm=dg_pre_mlp, post_mlp_norm=dg_post_mlp,
                 loss_sum=loss_sum)
    small_sum = _unpack_small(_all_reduce_small(_pack_small(small)), CONV_WIDTH)
    loss = small_sum["loss_sum"] * (0.5 / D_MODEL)
    small_sum["conv_w"] = lax.dynamic_slice_in_dim(small_sum["conv_w"], chip * local_conv, local_conv, axis=2)
    packed = [_pack_small({n: t[n] for n in SMALL_NAMES}) for t in (w, small_sum, m, v)]
    small_d, small_m, small_v = [_unpack_small(t, local_conv) for t in _adamw_small(*packed)]
    for n in SMALL_NAMES:
        out_g[n], out_d[n], out_m[n], out_v[n] = small_sum[n], small_d[n], small_m[n], small_v[n]

    def shaped(d):
        return [d[n].reshape(w[n].shape) for n in WEIGHT_ORDER]

    return (loss, grad_x[None], *shaped(out_g), *shaped(out_d), *shaped(out_m), *shaped(out_v))
```

```python
import math
from typing import Callable, NamedTuple

import jax
import jax.numpy as jnp
import numpy as np
from jax import lax
from jax.experimental import pallas as pl
from jax.experimental.pallas import tpu as pltpu

F32 = jnp.float32
BF16 = jnp.bfloat16

D_MODEL = 1024
HEAD_DIM = 64
Q_WIDTH = 512
KV_WIDTH = 128
CONV_WIDTH = 512
CONV_K = 3
D_FF = 4096
IN_COLS = 2304
QBLOCK = 128
ROT_DIM = 16
ROPE_THETA = 500000.0
NORM_EPS = 1e-6
NEG_INF = -1e30
N_CHIPS = 4

ADAM_LR = 0.001
ADAM_B1 = 0.9
ADAM_B2 = 0.999
ADAM_EPS = 1e-08
ADAM_WD = 0.01
ADAM_STEP = 10

H_UP, H_DOWN, H_OUT, H_IN = 512, 512, 128, 288

TOKEN_TILE = 512
MLP_BWD_TOKEN_TILE = 256
ATTN_FWD_BLOCKS = 4
ATTN_BWD_BLOCKS = 2
WGRAD_TOKEN_TILE = 1024
VMEM_LIMIT_V7X = 56 * 1024 * 1024

MESH = pl.DeviceIdType.MESH
ANY = pl.BlockSpec(memory_space=pl.ANY)
VMEM_WHOLE = pl.BlockSpec(memory_space=pltpu.VMEM)
SDS = jax.ShapeDtypeStruct


def _resident(shape):
    zeros = (0,) * len(shape)
    return pl.BlockSpec(shape, lambda *_: zeros, pipeline_mode=pl.Buffered(1))


def _rms(v):
    return lax.rsqrt(jnp.mean(v * v, axis=-1, keepdims=True) + NORM_EPS)


def _norm_bwd(dy, gain, vhat, rstd):
    t = dy * gain
    return rstd * (t - vhat * jnp.mean(t * vhat, axis=-1, keepdims=True))


def _colsum(v):
    return jnp.sum(v, axis=0, keepdims=True)


def _dot_nt(a, b):
    return lax.dot_general(a, b, (((1,), (1,)), ((), ())), preferred_element_type=F32)


def _dot_tn(a, b):
    return lax.dot_general(a, b, (((0,), (0,)), ((), ())), preferred_element_type=F32)


def _dot(a, b):
    return jnp.dot(a, b, preferred_element_type=F32)


def _lane_lt64(shape):
    return lax.broadcasted_iota(jnp.int32, shape, 1) < HEAD_DIM


class _Comm(NamedTuple):
    operands: tuple
    out_shapes: tuple
    aliases: dict
    n_remote: int
    n_local: int
    plan: Callable


def _merge(*comms):
    operands, out_shapes, aliases, parts = [], [], {}, []
    n_remote = n_local = 0
    for cm in comms:
        parts.append((len(operands), len(out_shapes), n_remote, n_local, cm))
        for k, v in cm.aliases.items():
            aliases[len(operands) + k] = len(out_shapes) + v
        operands += cm.operands
        out_shapes += cm.out_shapes
        n_remote += cm.n_remote
        n_local += cm.n_local

    def plan(ins, outs, send, recv, loc):
        sends, recvs, locs = [], [], []
        for i0, o0, r0, l0, cm in parts:
            s, r, l = cm.plan(ins[i0:i0 + len(cm.operands)], outs[o0:o0 + len(cm.out_shapes)],
                              lambda k, r0=r0: send(r0 + k), lambda k, r0=r0: recv(r0 + k), lambda k, l0=l0: loc(l0 + k))
            sends, recvs, locs = sends + s, recvs + r, locs + l
        return sends, recvs, locs

    return _Comm(tuple(operands), tuple(out_shapes), aliases, n_remote, n_local, plan)


def _sem_scratch(comm):
    return [pltpu.SemaphoreType.DMA((max(comm.n_remote, 1),)), pltpu.SemaphoreType.DMA((max(comm.n_remote, 1),)),
            pltpu.SemaphoreType.DMA((max(comm.n_local, 1),))]


def _pallas(body, *, name, grid, in_specs, out_specs, out_shape, operands, scratch=(), comm=None):
    params = pltpu.CompilerParams(dimension_semantics=("arbitrary",) * len(grid), vmem_limit_bytes=VMEM_LIMIT_V7X)
    if comm is None:
        return pl.pallas_call(body, name=name, grid=grid, in_specs=in_specs, out_specs=out_specs, out_shape=out_shape,
                              scratch_shapes=list(scratch), compiler_params=params)(*operands)
    n_in, n_out, n_scr = len(in_specs), len(out_specs), len(scratch)
    c_in, c_out = len(comm.operands), len(comm.out_shapes)

    def with_comm(*refs):
        ins, c_ins = refs[:n_in], refs[n_in:n_in + c_in]
        o0 = n_in + c_in
        outs, c_outs = refs[o0:o0 + n_out], refs[o0 + n_out:o0 + n_out + c_out]
        s0 = o0 + n_out + c_out
        scr = refs[s0:s0 + n_scr]
        send_sems, recv_sems, local_sems = refs[s0 + n_scr:]
        first = last = None
        for axis, size in enumerate(grid):
            at_start, at_end = pl.program_id(axis) == 0, pl.program_id(axis) == size - 1
            first = at_start if first is None else jnp.logical_and(first, at_start)
            last = at_end if last is None else jnp.logical_and(last, at_end)

        def copies():
            return comm.plan(c_ins, c_outs, lambda k: send_sems.at[k], lambda k: recv_sems.at[k],
                             lambda k: local_sems.at[k])

        @pl.when(first)
        def _():
            sends, _, locs = copies()
            for cp in sends + locs:
                cp.start()

        body(*ins, *outs, *scr)

        @pl.when(last)
        def _():
            sends, recvs, locs = copies()
            for cp in recvs:
                cp.wait_recv()
            for cp in sends:
                cp.wait_send()
            for cp in locs:
                cp.wait()

    return pl.pallas_call(
        with_comm, name=name, grid=grid,
        in_specs=list(in_specs) + [ANY] * c_in, out_specs=list(out_specs) + [ANY] * c_out,
        out_shape=list(out_shape) + list(comm.out_shapes),
        scratch_shapes=list(scratch) + _sem_scratch(comm),
        input_output_aliases={n_in + k: n_out + v for k, v in comm.aliases.items()},
        compiler_params=params)(*operands, *comm.operands)


def _comm_only(name, comm):
    c_in, c_out = len(comm.operands), len(comm.out_shapes)

    def body(*refs):
        send_sems, recv_sems, local_sems = refs[c_in + c_out:]
        sends, recvs, locs = comm.plan(refs[:c_in], refs[c_in:c_in + c_out], lambda k: send_sems.at[k],
                                       lambda k: recv_sems.at[k], lambda k: local_sems.at[k])
        for cp in sends + locs:
            cp.start()
        for cp in recvs:
            cp.wait_recv()
        for cp in sends:
            cp.wait_send()
        for cp in locs:
            cp.wait()

    return pl.pallas_call(
        body, name=name, in_specs=[ANY] * c_in, out_specs=[ANY] * c_out, out_shape=list(comm.out_shapes),
        scratch_shapes=_sem_scratch(comm),
        input_output_aliases=dict(comm.aliases))(*comm.operands)


def _place():
    return lax.axis_index("x"), lax.axis_index("y"), lax.axis_index("c")


def _other_chips(x, y):
    return [(1 - x, y), (x, 1 - y), (1 - x, 1 - y)]


def _slot(px, py, pc):
    return 4 * px + 2 * py + pc


def _remote(src, dst, send_sem, recv_sem, to):
    return pltpu.make_async_remote_copy(src_ref=src, dst_ref=dst, send_sem=send_sem, recv_sem=recv_sem,
                                        device_id=to, device_id_type=MESH)


def _gather_first(half_block):
    def plan(ins, outs, send, recv, loc):
        (blk,), (full,) = ins, outs
        x, y, c = _place()
        chips = _other_chips(x, y)
        mine = full.at[_slot(x, y, c)]
        sends = [_remote(blk, mine, send(0), recv(0), (x, y, 1 - c))]
        sends += [_remote(blk, mine, send(1 + j), recv(1 + j), (*chip, c)) for j, chip in enumerate(chips)]
        recvs = [_remote(blk, full.at[_slot(x, y, 1 - c)], send(0), recv(0), (x, y, 1 - c))]
        recvs += [_remote(blk, full.at[_slot(*chip, c)], send(1 + j), recv(1 + j), (*chip, c))
                  for j, chip in enumerate(chips)]
        return sends, recvs, [pltpu.make_async_copy(blk, mine, loc(0))]

    return _Comm((half_block,), (SDS((2 * N_CHIPS,) + half_block.shape, half_block.dtype),), {}, 4, 1, plan)


def _gather_second(partly_gathered):
    def plan(ins, outs, send, recv, loc):
        (src,), (full,) = ins, outs
        x, y, c = _place()
        chips = _other_chips(x, y)
        sends = [_remote(src.at[_slot(*chip, c)], full.at[_slot(*chip, c)], send(j), recv(j), (x, y, 1 - c))
                 for j, chip in enumerate(chips)]
        recvs = [_remote(src.at[_slot(*chip, 1 - c)], full.at[_slot(*chip, 1 - c)], send(j), recv(j), (x, y, 1 - c))
                 for j, chip in enumerate(chips)]
        return sends, recvs, []

    return _Comm((partly_gathered,), (SDS(partly_gathered.shape, partly_gathered.dtype),), {0: 0}, 3, 0, plan)


def _gather_small(block):
    def plan(ins, outs, send, recv, loc):
        (blk,), (full,) = ins, outs
        x, y, c = _place()
        chips = _other_chips(x, y)
        sends = [_remote(blk, full.at[2 * x + y], send(j), recv(j), (*chip, c)) for j, chip in enumerate(chips)]
        recvs = [_remote(blk, full.at[2 * chip[0] + chip[1]], send(j), recv(j), (*chip, c))
                 for j, chip in enumerate(chips)]
        return sends, recvs, [pltpu.make_async_copy(blk, full.at[2 * x + y], loc(0))]

    return _Comm((block,), (SDS((N_CHIPS,) + block.shape, block.dtype),), {}, 3, 1, plan)


def _pair_send(grads):
    def plan(ins, outs, send, recv, loc):
        (g,), (got,) = ins, outs
        x, y, c = _place()
        copies = [_remote(g.at[j, 1 - c], got.at[j], send(j), recv(j), (x, y, 1 - c)) for j in range(N_CHIPS)]
        return copies, copies, []

    shape = (grads.shape[0],) + grads.shape[2:]
    return _Comm((grads,), (SDS(shape, grads.dtype),), {}, N_CHIPS, 0, plan)


def _chip_exchange(partial):
    def plan(ins, outs, send, recv, loc):
        (p,), (got,) = ins, outs
        x, y, c = _place()
        my_chip = 2 * x + y
        chips = _other_chips(x, y)
        sends = [_remote(p.at[2 * chip[0] + chip[1]], got.at[my_chip], send(j), recv(j), (*chip, c))
                 for j, chip in enumerate(chips)]
        recvs = [_remote(p.at[my_chip], got.at[2 * chip[0] + chip[1]], send(j), recv(j), (*chip, c))
                 for j, chip in enumerate(chips)]
        return sends, recvs, [pltpu.make_async_copy(p.at[my_chip], got.at[my_chip], loc(0))]

    return _Comm((partial,), (SDS(partial.shape, partial.dtype),), {}, 3, 1, plan)


def _pair_sum(name, core, grads, received):
    h = grads.shape[2]

    def body(core_ref, g_ref, r_ref, o_ref):
        o_ref[...] = (g_ref[0] + r_ref[...]).astype(BF16)

    return pl.pallas_call(
        body, name=name,
        grid_spec=pltpu.PrefetchScalarGridSpec(
            num_scalar_prefetch=1, grid=(N_CHIPS,),
            in_specs=[pl.BlockSpec((1, 1, h, D_MODEL), lambda j, core_ref: (j, core_ref[0], 0, 0)),
                      pl.BlockSpec((1, h, D_MODEL), lambda j, core_ref: (j, 0, 0))],
            out_specs=pl.BlockSpec((1, h, D_MODEL), lambda j, core_ref: (j, 0, 0))),
        out_shape=SDS((N_CHIPS, h, D_MODEL), BF16),
        compiler_params=pltpu.CompilerParams(dimension_semantics=("arbitrary",), vmem_limit_bytes=VMEM_LIMIT_V7X),
    )(core, grads, received)


def _finish_reduce(exchanged):
    n = len(exchanged)

    def body(*refs):
        got, out = refs[:n], refs[n:2 * n]
        halves = refs[2 * n:3 * n]
        send_sems, recv_sems, local_sems = refs[3 * n:]
        x, y, c = _place()
        sends, recvs, locs = [], [], []
        for k in range(n):
            g = got[k]
            halves[k][...] = ((g[0].astype(F32) + g[1].astype(F32)) + (g[2].astype(F32) + g[3].astype(F32)))
            locs.append(pltpu.make_async_copy(halves[k], out[k].at[c], local_sems.at[k]))
            sends.append(_remote(halves[k], out[k].at[c], send_sems.at[k], recv_sems.at[k], (x, y, 1 - c)))
            recvs.append(_remote(halves[k], out[k].at[1 - c], send_sems.at[k], recv_sems.at[k], (x, y, 1 - c)))
            locs[-1].start()
            sends[-1].start()
        for cp in recvs:
            cp.wait_recv()
        for cp in sends:
            cp.wait_send()
        for cp in locs:
            cp.wait()

    return pl.pallas_call(
        body, name="finish_reduce", in_specs=[VMEM_WHOLE] * n, out_specs=[ANY] * n,
        out_shape=[SDS((2,) + e.shape[1:], F32) for e in exchanged],
        scratch_shapes=[pltpu.VMEM(e.shape[1:], F32) for e in exchanged]
                       + [pltpu.SemaphoreType.DMA((n,)), pltpu.SemaphoreType.DMA((n,)), pltpu.SemaphoreType.DMA((n,))],
        compiler_params=pltpu.CompilerParams(vmem_limit_bytes=VMEM_LIMIT_V7X),
    )(*exchanged)


SMALL_ROWS = 8


def _all_reduce_small(packed):
    def body(v_ref, o_ref, buf_ref, send_sems, recv_sems):
        x, y, c = _place()
        me = _slot(x, y, c)
        buf_ref[me] = v_ref[...]
        copies = []
        for mask in range(1, 8):
            peer = (x ^ (mask >> 2), y ^ ((mask >> 1) & 1), c ^ (mask & 1))
            copies.append(_remote(v_ref, buf_ref.at[me], send_sems.at[mask - 1], recv_sems.at[mask - 1], peer))
        for cp in copies:
            cp.start()
        for cp in copies:
            cp.wait_recv()
        for cp in copies:
            cp.wait_send()
        total = buf_ref[0]
        for d in range(1, 8):
            total = total + buf_ref[d]
        o_ref[...] = total

    return pl.pallas_call(
        body, name="all_reduce_small", in_specs=[VMEM_WHOLE], out_specs=VMEM_WHOLE,
        out_shape=SDS(packed.shape, F32),
        scratch_shapes=[pltpu.VMEM((8,) + packed.shape, F32), pltpu.SemaphoreType.DMA((7,)), pltpu.SemaphoreType.DMA((7,))],
    )(packed)


def _rope_expansion():
    half = ROT_DIM // 2
    expand = np.zeros((2 * half, 3 * 128), np.float32)
    const = np.zeros((1, 3 * 128), np.float32)
    for lane in range(128):
        d = lane % HEAD_DIM
        if d < ROT_DIM:
            expand[d % half, lane] = 1.0
        else:
            const[0, lane] = 1.0
        if d < half:
            expand[half + d, 128 + lane] = -1.0
        elif d < ROT_DIM:
            expand[half + d - half, 256 + lane] = 1.0
    return expand, const


def _rope_tables(seq):
    pos = jnp.arange(seq, dtype=F32)
    inv_freq = ROPE_THETA ** (-jnp.arange(0, ROT_DIM, 2, dtype=F32) / ROT_DIM)
    ang_t = inv_freq[:, None] * pos[None, :]
    cs_t = jnp.concatenate([jnp.cos(ang_t), jnp.sin(ang_t)], axis=0)
    hi = lax.reduce_precision(cs_t, 8, 7)
    mid = lax.reduce_precision(cs_t - hi, 8, 7)
    low = cs_t - hi - mid
    expand, const = _rope_expansion()
    pieces = jnp.concatenate([hi, mid, low], axis=0).astype(BF16)
    expand3 = jnp.asarray(np.concatenate([expand] * 3, axis=0), BF16)
    return lax.dot_general(pieces, expand3, (((0,), (0,)), ((), ())), preferred_element_type=F32) + jnp.asarray(const)


def _rope_specs(tb):
    return [pl.BlockSpec((tb, 128), lambda i, k=k: (i, k)) for k in range(3)]


def _rope(t, c, sa, sb):
    half = ROT_DIM // 2
    return t * c + pltpu.roll(t, 128 - half, 1) * sa + pltpu.roll(t, half, 1) * sb


def _rope_transposed(dt, c, sa, sb):
    half = ROT_DIM // 2
    return dt * c + pltpu.roll(dt * sa, half, 1) + pltpu.roll(dt * sb, 128 - half, 1)


def _cast_halves(core, w_up, w_down, w_out, w_in_t):
    def body(core_ref, up_ref, down_ref, out_ref, in_ref, up_o, down_o, out_o, in_o):
        up_o[...] = up_ref[...].astype(BF16)
        down_o[...] = down_ref[...].astype(BF16)
        out_o[...] = out_ref[...].astype(BF16)
        in_o[...] = in_ref[...].astype(BF16)

    half = lambda rows: pl.BlockSpec((rows, D_MODEL), lambda i, core_ref: (core_ref[0], 0))
    whole = lambda rows: pl.BlockSpec((rows, D_MODEL), lambda i, core_ref: (0, 0))
    rows = (H_UP, H_DOWN, H_OUT, H_IN)
    return pl.pallas_call(
        body, name="cast_halves",
        grid_spec=pltpu.PrefetchScalarGridSpec(
            num_scalar_prefetch=1, grid=(1,), in_specs=[half(r) for r in rows], out_specs=[whole(r) for r in rows]),
        out_shape=[SDS((r, D_MODEL), BF16) for r in rows],
        compiler_params=pltpu.CompilerParams(dimension_semantics=("arbitrary",), vmem_limit_bytes=VMEM_LIMIT_V7X),
    )(core, w_up, w_down, w_out, w_in_t)


def _in_proj(x, g_pre, w_in_t, rope, comm=None):
    seq = x.shape[0]
    tb = TOKEN_TILE

    def body(x_ref, g_ref, w_ref, c_ref, sa_ref, sb_ref,
             q_ref, kd0_ref, kd1_ref, vd0_ref, vd1_ref, gb_ref, gc_ref, xin_ref, hn_ref):
        xv = x_ref[...]
        hn = (xv * _rms(xv) * g_ref[...]).astype(BF16)
        hn_ref[...] = hn
        proj = _dot_nt(hn, w_ref[...].reshape(IN_COLS, D_MODEL))
        c, sa, sb = c_ref[...], sa_ref[...], sb_ref[...]
        scale = 1.0 / math.sqrt(HEAD_DIM)
        for p in range(Q_WIDTH // 128):
            q_ref[:, 128 * p:128 * (p + 1)] = (_rope(proj[:, 128 * p:128 * (p + 1)], c, sa, sb) * scale).astype(BF16)
        k = _rope(proj[:, Q_WIDTH:Q_WIDTH + KV_WIDTH], c, sa, sb)
        v = proj[:, Q_WIDTH + KV_WIDTH:Q_WIDTH + 2 * KV_WIDTH]
        low = _lane_lt64(k.shape)
        k_sw, v_sw = pltpu.roll(k, HEAD_DIM, 1), pltpu.roll(v, HEAD_DIM, 1)
        kd0_ref[...] = jnp.where(low, k, k_sw).astype(BF16)
        kd1_ref[...] = jnp.where(low, k_sw, k).astype(BF16)
        vd0_ref[...] = jnp.where(low, v, v_sw).astype(BF16)
        vd1_ref[...] = jnp.where(low, v_sw, v).astype(BF16)
        base = Q_WIDTH + 2 * KV_WIDTH
        gb_ref[...] = proj[:, base:base + CONV_WIDTH].astype(BF16)
        gc_ref[...] = proj[:, base + CONV_WIDTH:base + 2 * CONV_WIDTH].astype(BF16)
        xin_ref[...] = proj[:, base + 2 * CONV_WIDTH:base + 3 * CONV_WIDTH].astype(BF16)

    tile = lambda w: pl.BlockSpec((tb, w), lambda i: (i, 0))
    return _pallas(
        body, name="in_proj", grid=(seq // tb,),
        in_specs=[tile(D_MODEL), _resident((1, D_MODEL)), _resident(w_in_t.shape), *_rope_specs(tb)],
        out_specs=[tile(Q_WIDTH), tile(128), tile(128), tile(128), tile(128),
                   tile(CONV_WIDTH), tile(CONV_WIDTH), tile(CONV_WIDTH), tile(D_MODEL)],
        out_shape=[SDS((seq, Q_WIDTH), BF16)] + [SDS((seq, 128), BF16)] * 4
                  + [SDS((seq, CONV_WIDTH), BF16)] * 3 + [SDS((seq, D_MODEL), BF16)],
        operands=(x, g_pre, w_in_t, rope, rope, rope), comm=comm)


def _attn_valid(i):
    shape = (4 * QBLOCK, 2 * QBLOCK)
    row = lax.broadcasted_iota(jnp.int32, shape, 0)
    col = lax.broadcasted_iota(jnp.int32, shape, 1)
    qi = row & (QBLOCK - 1)
    return (col > qi) & (col <= qi + QBLOCK) & ((col >= QBLOCK) | (i > 0))


def _stack_heads(pair0, pair1):
    low = _lane_lt64(pair0.shape)
    zero = jnp.zeros_like(pair0)
    return jnp.concatenate([jnp.where(low, pair0, zero), jnp.where(low, zero, pair0),
                            jnp.where(low, pair1, zero), jnp.where(low, zero, pair1)], axis=0)


def _unstack_heads(stacked):
    low = _lane_lt64((QBLOCK, 128))
    pair0 = jnp.where(low, stacked[0:QBLOCK], stacked[QBLOCK:2 * QBLOCK])
    pair1 = jnp.where(low, stacked[2 * QBLOCK:3 * QBLOCK], stacked[3 * QBLOCK:4 * QBLOCK])
    return pair0, pair1


def _sink_column(sink_ref, kv_head):
    row = lax.broadcasted_iota(jnp.int32, (4 * QBLOCK, 1), 0)
    s = [sink_ref[0, 4 * kv_head + j] for j in range(4)]
    return jnp.where(row < QBLOCK, s[0], jnp.where(row < 2 * QBLOCK, s[1], jnp.where(row < 3 * QBLOCK, s[2], s[3])))


def _band(ref, i):
    prev = pl.multiple_of(jnp.maximum(i - 1, 0) * QBLOCK, QBLOCK)
    own = pl.multiple_of(i * QBLOCK, QBLOCK)
    return jnp.concatenate([ref[pl.ds(prev, QBLOCK), :], ref[pl.ds(own, QBLOCK), :]], axis=0), prev, own


def _softmax_with_sink(s, sink_col):
    m = jnp.maximum(jnp.max(s, axis=-1, keepdims=True), sink_col)
    p = jnp.exp(s - m)
    e_sink = jnp.exp(sink_col - m)
    inv_l = 1.0 / (jnp.sum(p, axis=-1, keepdims=True) + e_sink)
    return p, e_sink, inv_l


def _attention_fwd(q, kd0, kd1, vd0, vd1, sinks, comm=None):
    seq = q.shape[0]

    nb = ATTN_FWD_BLOCKS

    def body(sink_ref, q_ref, kd0_ref, kd1_ref, vd0_ref, vd1_ref, o_ref):
        for b in range(nb):
            i = pl.program_id(0) * nb + b
            rows = slice(QBLOCK * b, QBLOCK * (b + 1))
            valid = _attn_valid(i)
            for kv_head, (k_ref, v_ref) in enumerate(((kd0_ref, vd0_ref), (kd1_ref, vd1_ref))):
                kband, _, _ = _band(k_ref, i)
                vband, _, _ = _band(v_ref, i)
                base = 256 * kv_head
                qm = _stack_heads(q_ref[rows, base:base + 128], q_ref[rows, base + 128:base + 256])
                s = jnp.where(valid, _dot_nt(qm, kband), NEG_INF)
                p, _, inv_l = _softmax_with_sink(s, _sink_column(sink_ref, kv_head))
                o = _dot(p.astype(BF16), vband) * inv_l
                pair0, pair1 = _unstack_heads(o)
                o_ref[rows, base:base + 128] = pair0.astype(BF16)
                o_ref[rows, base + 128:base + 256] = pair1.astype(BF16)

    blk = pl.BlockSpec((nb * QBLOCK, Q_WIDTH), lambda i: (i, 0))
    full = _resident((seq, 128))
    return _pallas(
        body, name="attention_fwd", grid=(seq // (nb * QBLOCK),),
        in_specs=[pl.BlockSpec(memory_space=pltpu.SMEM), blk, full, full, full, full],
        out_specs=[blk], out_shape=[SDS((seq, Q_WIDTH), BF16)],
        operands=(sinks, q, kd0, kd1, vd0, vd1), comm=comm)


HALO = 16


def _conv_parts(gc, xin, gc_halo, xin_halo, conv_w, first):
    tb = gc.shape[0]
    u = gc.astype(F32) * xin.astype(F32)
    u_halo = jnp.where(first, 0.0, gc_halo.astype(F32) * xin_halo.astype(F32))
    ext = jnp.concatenate([u_halo, u], axis=0)
    u1 = pltpu.roll(ext, 1, 0)[HALO:HALO + tb]
    u2 = pltpu.roll(ext, 2, 0)[HALO:HALO + tb]
    y = conv_w[0:1, :] * u2 + conv_w[1:2, :] * u1 + conv_w[2:3, :] * u
    return u, u1, u2, y


def _halo_prev(tb, w):
    return pl.BlockSpec((HALO, w), lambda i: (jnp.maximum(i * (tb // HALO) - 1, 0), 0))


def _residual_mid(x, mix, g_post_mix):
    mix_f = mix.astype(F32)
    return x + mix_f * _rms(mix_f) * g_post_mix


def _mix_out(attn, gb, gc, xin, conv_w, g_attn, g_conv, w_out, comm=None):
    seq = attn.shape[0]
    tb = TOKEN_TILE

    def body(a_ref, gb_ref, gc_ref, xin_ref, gch_ref, xinh_ref, cw_ref, ga_ref, gcn_ref, w_ref, mix_ref, mixed_ref):
        first = pl.program_id(0) == 0
        _, _, _, y = _conv_parts(gc_ref[...], xin_ref[...], gch_ref[...], xinh_ref[...], cw_ref[...], first)
        conv = gb_ref[...].astype(F32) * y
        a = a_ref[...].astype(F32)
        mixed_ref[:, 0:Q_WIDTH] = (a * _rms(a) * ga_ref[...]).astype(BF16)
        mixed_ref[:, Q_WIDTH:] = (conv * _rms(conv) * gcn_ref[...]).astype(BF16)
        mix_ref[...] = _dot(mixed_ref[...], w_ref[...].reshape(D_MODEL, D_MODEL)).astype(BF16)

    tile = lambda w: pl.BlockSpec((tb, w), lambda i: (i, 0))
    return _pallas(
        body, name="mix_out", grid=(seq // tb,),
        in_specs=[tile(Q_WIDTH), tile(CONV_WIDTH), tile(CONV_WIDTH), tile(CONV_WIDTH),
                  _halo_prev(tb, CONV_WIDTH), _halo_prev(tb, CONV_WIDTH),
                  _resident((CONV_K, CONV_WIDTH)), _resident((1, Q_WIDTH)), _resident((1, CONV_WIDTH)),
                  _resident(w_out.shape)],
        out_specs=[tile(D_MODEL), tile(D_MODEL)],
        out_shape=[SDS((seq, D_MODEL), BF16), SDS((seq, D_MODEL), BF16)],
        operands=(attn, gb, gc, xin, gc, xin, conv_w, g_attn, g_conv, w_out), comm=comm)


def _mlp_loss(x, mix, target, g_post_mix, g_pre_mlp, g_post_mlp, w_up, w_down):
    seq = x.shape[0]
    tb = TOKEN_TILE

    def body(x_ref, mix_ref, t_ref, gpm_ref, g2_ref, g4_ref, wup_ref, wdown_ref,
             up_ref, hn2_ref, dout_ref, dmlp_ref, loss_ref, dg4_ref, act_ref):
        @pl.when(pl.program_id(0) == 0)
        def _():
            loss_ref[...] = jnp.zeros_like(loss_ref)
            dg4_ref[...] = jnp.zeros_like(dg4_ref)

        hv = _residual_mid(x_ref[...], mix_ref[...], gpm_ref[...])
        hn2 = (hv * _rms(hv) * g2_ref[...]).astype(BF16)
        hn2_ref[...] = hn2
        for j in range(N_CHIPS):
            up = _dot(hn2[:, :H_UP], wup_ref[2 * j]) + _dot(hn2[:, H_UP:], wup_ref[2 * j + 1])
            up = jnp.maximum(up, 0.0)
            up_ref[:, 1024 * j:1024 * (j + 1)] = up.astype(BF16)
            act_ref[:, 1024 * j:1024 * (j + 1)] = (up * up).astype(BF16)
        mlp = _dot(act_ref[...], wdown_ref[...].reshape(D_FF, D_MODEL))
        rstd = _rms(mlp)
        zhat = mlp * rstd
        diff = hv + zhat * g4_ref[...] - t_ref[...]
        loss_ref[...] += jnp.sum(jnp.sum(diff * diff, axis=1, keepdims=True), axis=0, keepdims=True)
        dout = diff * (1.0 / D_MODEL)
        dout_ref[...] = dout
        dg4_ref[...] += _colsum(dout * zhat)
        dmlp_ref[...] = _norm_bwd(dout, g4_ref[...], zhat, rstd).astype(BF16)

    tile = lambda w: pl.BlockSpec((tb, w), lambda i: (i, 0))
    return _pallas(
        body, name="mlp_loss", grid=(seq // tb,),
        in_specs=[tile(D_MODEL), tile(D_MODEL), tile(D_MODEL), _resident((1, D_MODEL)), _resident((1, D_MODEL)),
                  _resident((1, D_MODEL)), _resident(w_up.shape), _resident(w_down.shape)],
        out_specs=[tile(D_FF), tile(D_MODEL), tile(D_MODEL), tile(D_MODEL),
                   pl.BlockSpec((1, 1), lambda i: (0, 0)), pl.BlockSpec((1, D_MODEL), lambda i: (0, 0))],
        out_shape=[SDS((seq, D_FF), BF16), SDS((seq, D_MODEL), BF16), SDS((seq, D_MODEL), F32),
                   SDS((seq, D_MODEL), BF16), SDS((1, 1), F32), SDS((1, D_MODEL), F32)],
        scratch=[pltpu.VMEM((tb, D_FF), BF16)],
        operands=(x, mix, target, g_post_mix, g_pre_mlp, g_post_mlp, w_up, w_down))


def _mlp_bwd(dmlp, up, x, dout, mix, g_pre_mlp, g_post_mix, w_up, w_down):
    seq = x.shape[0]
    tb = MLP_BWD_TOKEN_TILE

    def body(dmlp_ref, up_ref, x_ref, dout_ref, mix_ref, g2_ref, gpm_ref, wup_ref, wdown_ref,
             dup_ref, dh_ref, dmix_ref, dg2_ref, dgpm_ref):
        @pl.when(pl.program_id(0) == 0)
        def _():
            dg2_ref[...] = jnp.zeros_like(dg2_ref)
            dgpm_ref[...] = jnp.zeros_like(dgpm_ref)

        dmlp_v = dmlp_ref[...]
        halves = [None, None]
        for j in range(N_CHIPS):
            cols = slice(1024 * j, 1024 * (j + 1))
            dact = jnp.concatenate([_dot_nt(dmlp_v, wdown_ref[2 * j]), _dot_nt(dmlp_v, wdown_ref[2 * j + 1])], axis=1)
            dup = (dact * (2.0 * up_ref[:, cols].astype(F32))).astype(BF16)
            dup_ref[:, cols] = dup
            for half in range(2):
                part = _dot_nt(dup, wup_ref[2 * j + half])
                halves[half] = part if j == 0 else halves[half] + part
        dhn2 = jnp.concatenate(halves, axis=1)
        mix_v = mix_ref[...].astype(F32)
        hv = _residual_mid(x_ref[...], mix_ref[...], gpm_ref[...])
        r2 = _rms(hv)
        hhat = hv * r2
        dg2_ref[...] += _colsum(dhn2 * hhat)
        dh = dout_ref[...] + _norm_bwd(dhn2, g2_ref[...], hhat, r2)
        dh_ref[...] = dh.astype(BF16)
        rz = _rms(mix_v)
        zhat = mix_v * rz
        dgpm_ref[...] += _colsum(dh * zhat)
        dmix_ref[...] = _norm_bwd(dh, gpm_ref[...], zhat, rz).astype(BF16)

    tile = lambda w: pl.BlockSpec((tb, w), lambda i: (i, 0))
    vec = pl.BlockSpec((1, D_MODEL), lambda i: (0, 0))
    return _pallas(
        body, name="mlp_bwd", grid=(seq // tb,),
        in_specs=[tile(D_MODEL), tile(D_FF), tile(D_MODEL), tile(D_MODEL), tile(D_MODEL),
                  _resident((1, D_MODEL)), _resident((1, D_MODEL)), _resident(w_up.shape), _resident(w_down.shape)],
        out_specs=[tile(D_FF), tile(D_MODEL), tile(D_MODEL), vec, vec],
        out_shape=[SDS((seq, D_FF), BF16), SDS((seq, D_MODEL), BF16), SDS((seq, D_MODEL), BF16),
                   SDS((1, D_MODEL), F32), SDS((1, D_MODEL), F32)],
        operands=(dmlp, up, x, dout, mix, g_pre_mlp, g_post_mix, w_up, w_down))


def _mix_bwd(dmix, attn, gb, gc, xin, conv_w, g_attn, g_conv, w_out, comm=None):
    seq = attn.shape[0]
    tb = TOKEN_TILE

    def body(dmix_ref, a_ref, gb_ref, gc_ref, xin_ref, gch_ref, xinh_ref, cw_ref, ga_ref, gcn_ref, w_ref,
             dattn_ref, dgb_ref, dy_ref, dga_ref, dgcn_ref, dcw_ref):
        first = pl.program_id(0) == 0

        @pl.when(first)
        def _():
            dga_ref[...] = jnp.zeros_like(dga_ref)
            dgcn_ref[...] = jnp.zeros_like(dgcn_ref)
            dcw_ref[...] = jnp.zeros_like(dcw_ref)

        dmixed = _dot_nt(dmix_ref[...], w_ref[...].reshape(D_MODEL, D_MODEL))
        a = a_ref[...].astype(F32)
        ra = _rms(a)
        ahat = a * ra
        dan = dmixed[:, 0:Q_WIDTH]
        dga_ref[...] += _colsum(dan * ahat)
        dattn_ref[...] = _norm_bwd(dan, ga_ref[...], ahat, ra).astype(BF16)
        gbv = gb_ref[...].astype(F32)
        u, u1, u2, y = _conv_parts(gc_ref[...], xin_ref[...], gch_ref[...], xinh_ref[...], cw_ref[...], first)
        conv = gbv * y
        rc = _rms(conv)
        chat = conv * rc
        dcn = dmixed[:, Q_WIDTH:]
        dgcn_ref[...] += _colsum(dcn * chat)
        dconv = _norm_bwd(dcn, gcn_ref[...], chat, rc)
        dgb_ref[...] = (dconv * y).astype(BF16)
        dy = dconv * gbv
        dy_ref[...] = dy.astype(BF16)
        dcw_ref[0:1, :] += _colsum(dy * u2)
        dcw_ref[1:2, :] += _colsum(dy * u1)
        dcw_ref[2:3, :] += _colsum(dy * u)

    tile = lambda w: pl.BlockSpec((tb, w), lambda i: (i, 0))
    return _pallas(
        body, name="mix_bwd", grid=(seq // tb,),
        in_specs=[tile(D_MODEL), tile(Q_WIDTH), tile(CONV_WIDTH), tile(CONV_WIDTH), tile(CONV_WIDTH),
                  _halo_prev(tb, CONV_WIDTH), _halo_prev(tb, CONV_WIDTH),
                  _resident((CONV_K, CONV_WIDTH)), _resident((1, Q_WIDTH)), _resident((1, CONV_WIDTH)),
                  _resident(w_out.shape)],
        out_specs=[tile(Q_WIDTH), tile(CONV_WIDTH), tile(CONV_WIDTH),
                   pl.BlockSpec((1, Q_WIDTH), lambda i: (0, 0)), pl.BlockSpec((1, CONV_WIDTH), lambda i: (0, 0)),
                   pl.BlockSpec((CONV_K, CONV_WIDTH), lambda i: (0, 0))],
        out_shape=[SDS((seq, Q_WIDTH), BF16), SDS((seq, CONV_WIDTH), BF16), SDS((seq, CONV_WIDTH), BF16),
                   SDS((1, Q_WIDTH), F32), SDS((1, CONV_WIDTH), F32), SDS((CONV_K, CONV_WIDTH), F32)],
        operands=(dmix, attn, gb, gc, xin, gc, xin, conv_w, g_attn, g_conv, w_out), comm=comm)


def _attention_bwd(q, dattn, attn, kd0, kd1, vd0, vd1, sinks, comm=None):
    seq = q.shape[0]
    nb = ATTN_BWD_BLOCKS

    def body(sink_ref, q_ref, do_ref, o_ref, kd0_ref, kd1_ref, vd0_ref, vd1_ref,
             dq_ref, dk0_ref, dk1_ref, dv0_ref, dv1_ref, dsink_ref):
        @pl.when(pl.program_id(0) == 0)
        def _():
            for r in (dk0_ref, dk1_ref, dv0_ref, dv1_ref, dsink_ref):
                r[...] = jnp.zeros_like(r)

        lane = lax.broadcasted_iota(jnp.int32, (1, 128), 1)
        dsink = jnp.zeros((1, 128), F32)
        for b in range(nb):
            i = pl.program_id(0) * nb + b
            rows = slice(QBLOCK * b, QBLOCK * (b + 1))
            valid = _attn_valid(i)
            for kv_head, (k_ref, v_ref, dk_ref, dv_ref) in enumerate(
                    ((kd0_ref, vd0_ref, dk0_ref, dv0_ref), (kd1_ref, vd1_ref, dk1_ref, dv1_ref))):
                kband, prev, own = _band(k_ref, i)
                vband, _, _ = _band(v_ref, i)
                base = 256 * kv_head
                qm = _stack_heads(q_ref[rows, base:base + 128], q_ref[rows, base + 128:base + 256])
                dom = _stack_heads(do_ref[rows, base:base + 128], do_ref[rows, base + 128:base + 256])
                om = _stack_heads(o_ref[rows, base:base + 128], o_ref[rows, base + 128:base + 256])
                s = jnp.where(valid, _dot_nt(qm, kband), NEG_INF)
                p, e_sink, inv_l = _softmax_with_sink(s, _sink_column(sink_ref, kv_head))
                p = p * inv_l
                delta = jnp.sum(dom.astype(F32) * om.astype(F32), axis=-1, keepdims=True)
                ds = (p * (_dot_nt(dom, vband) - delta)).astype(BF16)
                sink_term = -(e_sink * inv_l) * delta
                for j in range(4):
                    part = jnp.sum(sink_term[QBLOCK * j:QBLOCK * (j + 1)], axis=0, keepdims=True)
                    dsink = dsink + jnp.where(lane == 4 * kv_head + j, part, 0.0)
                pair0, pair1 = _unstack_heads(_dot(ds, kband))
                dq_ref[rows, base:base + 128] = pair0.astype(BF16)
                dq_ref[rows, base + 128:base + 256] = pair1.astype(BF16)
                dkd = _dot_tn(ds, qm)
                dkd = dkd + pltpu.roll(dkd, HEAD_DIM, 1)
                dvd = _dot_tn(p.astype(BF16), dom)
                dvd = dvd + pltpu.roll(dvd, HEAD_DIM, 1)
                dk_ref[pl.ds(prev, QBLOCK), :] += dkd[0:QBLOCK]
                dk_ref[pl.ds(own, QBLOCK), :] += dkd[QBLOCK:]
                dv_ref[pl.ds(prev, QBLOCK), :] += dvd[0:QBLOCK]
                dv_ref[pl.ds(own, QBLOCK), :] += dvd[QBLOCK:]
        dsink_ref[...] += dsink

    blk = pl.BlockSpec((nb * QBLOCK, Q_WIDTH), lambda i: (i, 0))
    full = _resident((seq, 128))
    acc = pl.BlockSpec((seq, 128), lambda i: (0, 0))
    return _pallas(
        body, name="attention_bwd", grid=(seq // (nb * QBLOCK),),
        in_specs=[pl.BlockSpec(memory_space=pltpu.SMEM), blk, blk, blk, full, full, full, full],
        out_specs=[blk, acc, acc, acc, acc, pl.BlockSpec((1, 128), lambda i: (0, 0))],
        out_shape=[SDS((seq, Q_WIDTH), BF16)] + [SDS((seq, 128), F32)] * 4 + [SDS((1, 128), F32)],
        operands=(sinks, q, dattn, attn, kd0, kd1, vd0, vd1), comm=comm)


def _in_proj_bwd(dq, dk0, dk1, dv0, dv1, dgb, dy, gc, xin, conv_w, x, dh, g_pre, w_in_t, rope):
    seq = x.shape[0]
    tb = TOKEN_TILE
    n_tiles = seq // tb

    def body(dq_ref, dk0_ref, dk1_ref, dv0_ref, dv1_ref, dgb_ref, dy_ref, dyh_ref, gc_ref, xin_ref, cw_ref,
             x_ref, dh_ref, g_ref, w_ref, c_ref, sa_ref, sb_ref,
             dproj_ref, gx_ref, dg_ref):
        i = pl.program_id(0)

        @pl.when(i == 0)
        def _():
            dg_ref[...] = jnp.zeros_like(dg_ref)

        c, sa, sb = c_ref[...], sa_ref[...], sb_ref[...]
        scale = 1.0 / math.sqrt(HEAD_DIM)
        for p in range(Q_WIDTH // 128):
            dproj_ref[:, 128 * p:128 * (p + 1)] = _rope_transposed(
                dq_ref[:, 128 * p:128 * (p + 1)].astype(F32) * scale, c, sa, sb).astype(BF16)
        low = _lane_lt64((tb, 128))
        dk = jnp.where(low, dk0_ref[...], dk1_ref[...])
        dproj_ref[:, Q_WIDTH:Q_WIDTH + KV_WIDTH] = _rope_transposed(dk, c, sa, sb).astype(BF16)
        dproj_ref[:, Q_WIDTH + KV_WIDTH:Q_WIDTH + 2 * KV_WIDTH] = jnp.where(low, dv0_ref[...], dv1_ref[...]).astype(BF16)
        base = Q_WIDTH + 2 * KV_WIDTH
        dproj_ref[:, base:base + CONV_WIDTH] = dgb_ref[...]
        dy = dy_ref[...].astype(F32)
        ext = jnp.concatenate([dy, jnp.where(i == n_tiles - 1, 0.0, dyh_ref[...].astype(F32))], axis=0)
        dy1 = pltpu.roll(ext, tb + HALO - 1, 0)[0:tb]
        dy2 = pltpu.roll(ext, tb + HALO - 2, 0)[0:tb]
        cw = cw_ref[...]
        du = cw[2:3, :] * dy + cw[1:2, :] * dy1 + cw[0:1, :] * dy2
        dproj_ref[:, base + CONV_WIDTH:base + 2 * CONV_WIDTH] = (du * xin_ref[...].astype(F32)).astype(BF16)
        dproj_ref[:, base + 2 * CONV_WIDTH:] = (du * gc_ref[...].astype(F32)).astype(BF16)
        dhn = _dot(dproj_ref[...], w_ref[...].reshape(IN_COLS, D_MODEL))
        xv = x_ref[...]
        r = _rms(xv)
        xhat = xv * r
        dg_ref[...] += _colsum(dhn * xhat)
        gx_ref[...] = dh_ref[...].astype(F32) + _norm_bwd(dhn, g_ref[...], xhat, r)

    tile = lambda w: pl.BlockSpec((tb, w), lambda i: (i, 0))
    halo_next = pl.BlockSpec((HALO, CONV_WIDTH), lambda i: (jnp.minimum((i + 1) * (tb // HALO), seq // HALO - 1), 0))
    return _pallas(
        body, name="in_proj_bwd", grid=(n_tiles,),
        in_specs=[tile(Q_WIDTH), tile(128), tile(128), tile(128), tile(128), tile(CONV_WIDTH), tile(CONV_WIDTH), halo_next,
                  tile(CONV_WIDTH), tile(CONV_WIDTH), _resident((CONV_K, CONV_WIDTH)),
                  tile(D_MODEL), tile(D_MODEL), _resident((1, D_MODEL)), _resident(w_in_t.shape), *_rope_specs(tb)],
        out_specs=[tile(IN_COLS), tile(D_MODEL), pl.BlockSpec((1, D_MODEL), lambda i: (0, 0))],
        out_shape=[SDS((seq, IN_COLS), BF16), SDS((seq, D_MODEL), F32), SDS((1, D_MODEL), F32)],
        operands=(dq, dk0, dk1, dv0, dv1, dgb, dy, dy, gc, xin, conv_w, x, dh, g_pre, w_in_t, rope, rope, rope))


def _wgrad(name, a, b, *, per_chip, h_rows, square_a=False, comm=None):
    seq = a.shape[0]
    bt = WGRAD_TOKEN_TILE
    n_k = seq // bt
    chips_per_step = 1 if per_chip else N_CHIPS
    m = chips_per_step * 2 * h_rows
    a_cols = m if per_chip else a.shape[1]
    a_wide = a.shape[1] > a_cols
    b_wide = b.shape[1] > D_MODEL

    def body(a_ref, b_ref, g_ref, acc_ref):
        k = pl.program_id(1)

        @pl.when(k == 0)
        def _():
            acc_ref[...] = jnp.zeros_like(acc_ref)

        av = a_ref[...]
        if square_a:
            av = (av.astype(F32) * av.astype(F32)).astype(BF16)
        acc_ref[...] += _dot_tn(av, b_ref[...])

        @pl.when(k == n_k - 1)
        def _():
            for cidx in range(chips_per_step):
                for half in range(2):
                    r0 = (2 * cidx + half) * h_rows
                    g_ref[cidx, half] = acc_ref[r0:r0 + h_rows, :]

    a_spec = pl.BlockSpec((bt, a_cols), (lambda j, k: (k, j)) if a_wide else (lambda j, k: (k, 0)))
    b_spec = pl.BlockSpec((bt, D_MODEL), (lambda j, k: (k, j)) if b_wide else (lambda j, k: (k, 0)))
    g_spec = pl.BlockSpec((chips_per_step, 2, h_rows, D_MODEL), lambda j, k: (j, 0, 0, 0))
    return _pallas(
        body, name=name, grid=(N_CHIPS if per_chip else 1, n_k),
        in_specs=[a_spec, b_spec], out_specs=[g_spec], out_shape=[SDS((N_CHIPS, 2, h_rows, D_MODEL), F32)],
        scratch=[pltpu.VMEM((m, D_MODEL), F32)], operands=(a, b), comm=comm)


def _adamw_math(w, g, m, v):
    m = ADAM_B1 * m + (1.0 - ADAM_B1) * g
    v = ADAM_B2 * v + (1.0 - ADAM_B2) * (g * g)
    m_hat = m / (1.0 - ADAM_B1 ** ADAM_STEP)
    v_hat = v / (1.0 - ADAM_B2 ** ADAM_STEP)
    delta = -ADAM_LR * (m_hat / (jnp.sqrt(v_hat) + ADAM_EPS) + ADAM_WD * w)
    return delta, m, v


def _adamw_rows(name, reduced, w, m, v, rt):
    per_half = reduced.shape[1] // rt

    def body(r_ref, w_ref, m_ref, v_ref, g_out, d_out, m_out, v_out):
        g = r_ref[0]
        g_out[...] = g
        d_out[...], m_out[...], v_out[...] = _adamw_math(w_ref[...], g, m_ref[...], v_ref[...])

    blk = pl.BlockSpec((rt, D_MODEL), lambda h, r: (h * per_half + r, 0))
    return _pallas(
        body, name=name, grid=(2, per_half),
        in_specs=[pl.BlockSpec((1, rt, D_MODEL), lambda h, r: (h, r, 0)), blk, blk, blk],
        out_specs=[blk, blk, blk, blk], out_shape=[SDS(w.shape, F32)] * 4, operands=(reduced, w, m, v))


def _adamw_small(w, g, m, v):
    def body(w_ref, g_ref, m_ref, v_ref, d_out, m_out, v_out):
        d_out[...], m_out[...], v_out[...] = _adamw_math(w_ref[...], g_ref[...], m_ref[...], v_ref[...])

    return pl.pallas_call(body, name="adamw_small", in_specs=[VMEM_WHOLE] * 4, out_specs=[VMEM_WHOLE] * 3,
                          out_shape=[SDS(w.shape, F32)] * 3)(w, g, m, v)


SMALL_VECTORS = ("pre_mix_norm", "post_mix_norm", "pre_mlp_norm", "post_mlp_norm")
SMALL_NAMES = SMALL_VECTORS + ("attn_group_norm", "conv_group_norm", "conv_w", "attn_sinks")


def _pack_small(p):
    rows = [p[n].reshape(1, D_MODEL) for n in SMALL_VECTORS]
    rows.append(jnp.concatenate([p["attn_group_norm"].reshape(1, -1), p["conv_group_norm"].reshape(1, -1)], axis=1))
    cw = p["conv_w"].reshape(CONV_K, -1)
    rows.append(jnp.pad(cw, ((0, 1), (0, CONV_WIDTH - cw.shape[1]))).reshape(2, D_MODEL))
    last = jnp.concatenate([p["attn_sinks"].reshape(1, 8), p.get("loss_sum", jnp.zeros((1, 1), F32))], axis=1)
    rows.append(jnp.pad(last, ((0, 0), (0, D_MODEL - 9))))
    return jnp.concatenate(rows, axis=0)


def _unpack_small(packed, conv_width):
    out = {n: packed[i:i + 1] for i, n in enumerate(SMALL_VECTORS)}
    out["attn_group_norm"] = packed[4:5, :Q_WIDTH]
    out["conv_group_norm"] = packed[4:5, Q_WIDTH:]
    out["conv_w"] = packed[5:7].reshape(4, CONV_WIDTH)[:CONV_K, :conv_width].reshape(1, CONV_K, conv_width)
    out["attn_sinks"] = packed[7:8, :8]
    out["loss_sum"] = packed[7, 8]
    return out


WEIGHT_ORDER = ("pre_mix_norm", "w_in", "conv_w", "attn_sinks", "attn_group_norm", "conv_group_norm", "w_out",
                "post_mix_norm", "pre_mlp_norm", "w_up", "w_down", "post_mlp_norm")


def kernel(x, pre_mix_norm, w_in, conv_w, attn_sinks, attn_group_norm, conv_group_norm, w_out, post_mix_norm, pre_mlp_norm, w_up, w_down, post_mlp_norm, loss_target, m_pre_mix_norm, m_w_in, m_conv_w, m_attn_sinks, m_attn_group_norm, m_conv_group_norm, m_w_out, m_post_mix_norm, m_pre_mlp_norm, m_w_up, m_w_down, m_post_mlp_norm, v_pre_mix_norm, v_w_in, v_conv_w, v_attn_sinks, v_attn_group_norm, v_conv_group_norm, v_w_out, v_post_mix_norm, v_pre_mlp_norm, v_w_up, v_w_down, v_post_mlp_norm):
    w = dict(pre_mix_norm=pre_mix_norm, w_in=w_in, conv_w=conv_w, attn_sinks=attn_sinks, attn_group_norm=attn_group_norm,
             conv_group_norm=conv_group_norm, w_out=w_out, post_mix_norm=post_mix_norm, pre_mlp_norm=pre_mlp_norm,
             w_up=w_up, w_down=w_down, post_mlp_norm=post_mlp_norm)
    m = dict(pre_mix_norm=m_pre_mix_norm, w_in=m_w_in, conv_w=m_conv_w, attn_sinks=m_attn_sinks,
             attn_group_norm=m_attn_group_norm, conv_group_norm=m_conv_group_norm, w_out=m_w_out,
             post_mix_norm=m_post_mix_norm, pre_mlp_norm=m_pre_mlp_norm, w_up=m_w_up, w_down=m_w_down,
             post_mlp_norm=m_post_mlp_norm)
    v = dict(pre_mix_norm=v_pre_mix_norm, w_in=v_w_in, conv_w=v_conv_w, attn_sinks=v_attn_sinks,
             attn_group_norm=v_attn_group_norm, conv_group_norm=v_conv_group_norm, w_out=v_w_out,
             post_mix_norm=v_post_mix_norm, pre_mlp_norm=v_pre_mlp_norm, w_up=v_w_up, w_down=v_w_down,
             post_mlp_norm=v_post_mlp_norm)
    core = lax.axis_index("c").astype(jnp.int32).reshape(1)
    chip = 2 * lax.axis_index("x") + lax.axis_index("y")
    local_conv = conv_w.shape[2]
    xs, target = x[0], loss_target[0]
    rope = _rope_tables(xs.shape[0])

    hb_up, hb_down, hb_out, hb_in = _cast_halves(core, w_up[0], w_down[0], w_out[0], w_in[0].T)
    gather_in = _gather_first(hb_in)
    conv_pad = jnp.pad(conv_w[0], ((0, 8 - CONV_K), (0, 0)))
    wf_in, conv_all = _comm_only("gather_in_first", _merge(gather_in, _gather_small(conv_pad)))
    wf_in, = _comm_only("gather_in_second", _gather_second(wf_in))
    conv_full = conv_all[:, :CONV_K, :].transpose(1, 0, 2).reshape(CONV_K, CONV_WIDTH)

    *proj, wf_up, wf_out = _in_proj(xs, pre_mix_norm, wf_in, rope, comm=_merge(_gather_first(hb_up), _gather_first(hb_out)))
    q, kd0, kd1, vd0, vd1, gb, gc, xin, hn = proj
    attn, wf_up, wf_out, wf_down = _attention_fwd(
        q, kd0, kd1, vd0, vd1, attn_sinks,
        comm=_merge(_gather_second(wf_up), _gather_second(wf_out), _gather_first(hb_down)))
    mix, mixed, wf_down = _mix_out(attn, gb, gc, xin, conv_full, attn_group_norm, conv_group_norm, wf_out,
                                   comm=_gather_second(wf_down))
    up, hn2, dout, dmlp, loss_sum, dg_post_mlp = _mlp_loss(xs, mix, target, post_mix_norm, pre_mlp_norm, post_mlp_norm,
                                                           wf_up, wf_down)

    dup, dh, dmix, dg_pre_mlp, dg_post_mix = _mlp_bwd(dmlp, up, xs, dout, mix, pre_mlp_norm, post_mix_norm, wf_up, wf_down)
    g_down, = _wgrad("wgrad_down", up, dmlp, per_chip=True, h_rows=H_DOWN, square_a=True)
    g_up, got_down = _wgrad("wgrad_up", hn2, dup, per_chip=True, h_rows=H_UP, comm=_pair_send(g_down))
    p_down = _pair_sum("pair_sum_down", core, g_down, got_down)
    dattn, dgb, dy, dg_attn, dg_conv, dconv_w, ex_down, got_up = _mix_bwd(
        dmix, attn, gb, gc, xin, conv_full, attn_group_norm, conv_group_norm, wf_out,
        comm=_merge(_chip_exchange(p_down), _pair_send(g_up)))
    p_up = _pair_sum("pair_sum_up", core, g_up, got_up)
    g_out, = _wgrad("wgrad_out", mixed, dmix, per_chip=False, h_rows=H_OUT)
    dq, dk0, dk1, dv0, dv1, dsink, ex_up, got_out = _attention_bwd(
        q, dattn, attn, kd0, kd1, vd0, vd1, attn_sinks, comm=_merge(_chip_exchange(p_up), _pair_send(g_out)))
    p_out = _pair_sum("pair_sum_out", core, g_out, got_out)
    dproj, grad_x, dg_pre_mix = _in_proj_bwd(dq, dk0, dk1, dv0, dv1, dgb, dy, gc, xin, conv_full, xs, dh, pre_mix_norm,
                                             wf_in, rope)
    g_in, ex_out = _wgrad("wgrad_in", dproj, hn, per_chip=False, h_rows=H_IN, comm=_chip_exchange(p_out))
    got_in, = _comm_only("pair_send_in", _pair_send(g_in))
    ex_in, = _comm_only("chip_exchange_in", _chip_exchange(_pair_sum("pair_sum_in", core, g_in, got_in)))
    r_down, r_up, r_out, r_in = _finish_reduce([ex_down, ex_up, ex_out, ex_in])

    out_g, out_d, out_m, out_v = {}, {}, {}, {}
    out_g["w_up"], out_d["w_up"], out_m["w_up"], out_v["w_up"] = _adamw_rows(
        "adamw_up", r_up, w_up[0], m_w_up[0], v_w_up[0], 256)
    out_g["w_down"], out_d["w_down"], out_m["w_down"], out_v["w_down"] = _adamw_rows(
        "adamw_down", r_down, w_down[0], m_w_down[0], v_w_down[0], 256)
    out_g["w_out"], out_d["w_out"], out_m["w_out"], out_v["w_out"] = _adamw_rows(
        "adamw_out", r_out, w_out[0], m_w_out[0], v_w_out[0], H_OUT)
    in_t = _adamw_rows("adamw_in", r_in, w_in[0].T, m_w_in[0].T, v_w_in[0].T, H_IN)
    out_g["w_in"], out_d["w_in"], out_m["w_in"], out_v["w_in"] = [t.T for t in in_t]

    small = dict(pre_mix_norm=dg_pre_mix, conv_w=dconv_w, attn_sinks=dsink[:, :8], attn_group_norm=dg_attn,
                 conv_group_norm=dg_conv, post_mix_norm=dg_post_mix, pre_mlp_norm=dg_pre_mlp, post_mlp_norm=dg_post_mlp,
                 loss_sum=loss_sum)
    small_sum = _unpack_small(_all_reduce_small(_pack_small(small)), CONV_WIDTH)
    loss = small_sum["loss_sum"] * (0.5 / D_MODEL)
    small_sum["conv_w"] = lax.dynamic_slice_in_dim(small_sum["conv_w"], chip * local_conv, local_conv, axis=2)
    packed = [_pack_small({n: t[n] for n in SMALL_NAMES}) for t in (w, small_sum, m, v)]
    small_d, small_m, small_v = [_unpack_small(t, local_conv) for t in _adamw_small(*packed)]
    for n in SMALL_NAMES:
        out_g[n], out_d[n], out_m[n], out_v[n] = small_sum[n], small_d[n], small_m[n], small_v[n]

    def shaped(d):
        return [d[n].reshape(w[n].shape) for n in WEIGHT_ORDER]

    return (loss, grad_x[None], *shaped(out_g), *shaped(out_d), *shaped(out_m), *shaped(out_v))
```

```python
import math
from typing import Callable, NamedTuple

import jax
import jax.numpy as jnp
import numpy as np
from jax import lax
from jax.experimental import pallas as pl
from jax.experimental.pallas import tpu as pltpu

F32 = jnp.float32
BF16 = jnp.bfloat16

D_MODEL = 1024
HEAD_DIM = 64
Q_WIDTH = 512
KV_WIDTH = 128
CONV_WIDTH = 512
CONV_K = 3
D_FF = 4096
IN_COLS = 2304
QBLOCK = 128
ROT_DIM = 16
ROPE_THETA = 500000.0
NORM_EPS = 1e-6
NEG_INF = -1e30
N_CHIPS = 4

ADAM_LR = 0.001
ADAM_B1 = 0.9
ADAM_B2 = 0.999
ADAM_EPS = 1e-08
ADAM_WD = 0.01
ADAM_STEP = 10

H_UP, H_DOWN, H_OUT, H_IN = 512, 512, 128, 288

TOKEN_TILE = 512
MLP_BWD_TOKEN_TILE = 256
ATTN_FWD_BLOCKS = 4
ATTN_BWD_BLOCKS = 2
WGRAD_TOKEN_TILE = 2048
VMEM_LIMIT_V7X = 56 * 1024 * 1024

MESH = pl.DeviceIdType.MESH
ANY = pl.BlockSpec(memory_space=pl.ANY)
VMEM_WHOLE = pl.BlockSpec(memory_space=pltpu.VMEM)
SDS = jax.ShapeDtypeStruct


def _resident(shape):
    zeros = (0,) * len(shape)
    return pl.BlockSpec(shape, lambda *_: zeros, pipeline_mode=pl.Buffered(1))


def _rms(v):
    return lax.rsqrt(jnp.mean(v * v, axis=-1, keepdims=True) + NORM_EPS)


def _norm_bwd(dy, gain, vhat, rstd):
    t = dy * gain
    return rstd * (t - vhat * jnp.mean(t * vhat, axis=-1, keepdims=True))


def _colsum(v):
    return jnp.sum(v, axis=0, keepdims=True)


def _dot_nt(a, b):
    return lax.dot_general(a, b, (((1,), (1,)), ((), ())), preferred_element_type=F32)


def _dot_tn(a, b):
    return lax.dot_general(a, b, (((0,), (0,)), ((), ())), preferred_element_type=F32)


def _dot(a, b):
    return jnp.dot(a, b, preferred_element_type=F32)


def _lane_lt64(shape):
    return lax.broadcasted_iota(jnp.int32, shape, 1) < HEAD_DIM


class _Comm(NamedTuple):
    operands: tuple
    out_shapes: tuple
    aliases: dict
    n_remote: int
    n_local: int
    plan: Callable


def _merge(*comms):
    operands, out_shapes, aliases, parts = [], [], {}, []
    n_remote = n_local = 0
    for cm in comms:
        parts.append((len(operands), len(out_shapes), n_remote, n_local, cm))
        for k, v in cm.aliases.items():
            aliases[len(operands) + k] = len(out_shapes) + v
        operands += cm.operands
        out_shapes += cm.out_shapes
        n_remote += cm.n_remote
        n_local += cm.n_local

    def plan(ins, outs, send, recv, loc):
        sends, recvs, locs = [], [], []
        for i0, o0, r0, l0, cm in parts:
            s, r, l = cm.plan(ins[i0:i0 + len(cm.operands)], outs[o0:o0 + len(cm.out_shapes)],
                              lambda k, r0=r0: send(r0 + k), lambda k, r0=r0: recv(r0 + k), lambda k, l0=l0: loc(l0 + k))
            sends, recvs, locs = sends + s, recvs + r, locs + l
        return sends, recvs, locs

    return _Comm(tuple(operands), tuple(out_shapes), aliases, n_remote, n_local, plan)


def _sem_scratch(comm):
    return [pltpu.SemaphoreType.DMA((max(comm.n_remote, 1),)), pltpu.SemaphoreType.DMA((max(comm.n_remote, 1),)),
            pltpu.SemaphoreType.DMA((max(comm.n_local, 1),))]


def _pallas(body, *, name, grid, in_specs, out_specs, out_shape, operands, scratch=(), comm=None):
    params = pltpu.CompilerParams(dimension_semantics=("arbitrary",) * len(grid), vmem_limit_bytes=VMEM_LIMIT_V7X)
    if comm is None:
        return pl.pallas_call(body, name=name, grid=grid, in_specs=in_specs, out_specs=out_specs, out_shape=out_shape,
                              scratch_shapes=list(scratch), compiler_params=params)(*operands)
    n_in, n_out, n_scr = len(in_specs), len(out_specs), len(scratch)
    c_in, c_out = len(comm.operands), len(comm.out_shapes)

    def with_comm(*refs):
        ins, c_ins = refs[:n_in], refs[n_in:n_in + c_in]
        o0 = n_in + c_in
        outs, c_outs = refs[o0:o0 + n_out], refs[o0 + n_out:o0 + n_out + c_out]
        s0 = o0 + n_out + c_out
        scr = refs[s0:s0 + n_scr]
        send_sems, recv_sems, local_sems = refs[s0 + n_scr:]
        first = last = None
        for axis, size in enumerate(grid):
            at_start, at_end = pl.program_id(axis) == 0, pl.program_id(axis) == size - 1
            first = at_start if first is None else jnp.logical_and(first, at_start)
            last = at_end if last is None else jnp.logical_and(last, at_end)

        def copies():
            return comm.plan(c_ins, c_outs, lambda k: send_sems.at[k], lambda k: recv_sems.at[k],
                             lambda k: local_sems.at[k])

        @pl.when(first)
        def _():
            sends, _, locs = copies()
            for cp in sends + locs:
                cp.start()

        body(*ins, *outs, *scr)

        @pl.when(last)
        def _():
            sends, recvs, locs = copies()
            for cp in recvs:
                cp.wait_recv()
            for cp in sends:
                cp.wait_send()
            for cp in locs:
                cp.wait()

    return pl.pallas_call(
        with_comm, name=name, grid=grid,
        in_specs=list(in_specs) + [ANY] * c_in, out_specs=list(out_specs) + [ANY] * c_out,
        out_shape=list(out_shape) + list(comm.out_shapes),
        scratch_shapes=list(scratch) + _sem_scratch(comm),
        input_output_aliases={n_in + k: n_out + v for k, v in comm.aliases.items()},
        compiler_params=params)(*operands, *comm.operands)


def _comm_only(name, comm):
    c_in, c_out = len(comm.operands), len(comm.out_shapes)

    def body(*refs):
        send_sems, recv_sems, local_sems = refs[c_in + c_out:]
        sends, recvs, locs = comm.plan(refs[:c_in], refs[c_in:c_in + c_out], lambda k: send_sems.at[k],
                                       lambda k: recv_sems.at[k], lambda k: local_sems.at[k])
        for cp in sends + locs:
            cp.start()
        for cp in recvs:
            cp.wait_recv()
        for cp in sends:
            cp.wait_send()
        for cp in locs:
            cp.wait()

    return pl.pallas_call(
        body, name=name, in_specs=[ANY] * c_in, out_specs=[ANY] * c_out, out_shape=list(comm.out_shapes),
        scratch_shapes=_sem_scratch(comm),
        input_output_aliases=dict(comm.aliases))(*comm.operands)


def _place():
    return lax.axis_index("x"), lax.axis_index("y"), lax.axis_index("c")


def _other_chips(x, y):
    return [(1 - x, y), (x, 1 - y), (1 - x, 1 - y)]


def _slot(px, py, pc):
    return 4 * px + 2 * py + pc


def _remote(src, dst, send_sem, recv_sem, to):
    return pltpu.make_async_remote_copy(src_ref=src, dst_ref=dst, send_sem=send_sem, recv_sem=recv_sem,
                                        device_id=to, device_id_type=MESH)


def _gather_first(half_block):
    def plan(ins, outs, send, recv, loc):
        (blk,), (full,) = ins, outs
        x, y, c = _place()
        chips = _other_chips(x, y)
        mine = full.at[_slot(x, y, c)]
        sends = [_remote(blk, mine, send(0), recv(0), (x, y, 1 - c))]
        sends += [_remote(blk, mine, send(1 + j), recv(1 + j), (*chip, c)) for j, chip in enumerate(chips)]
        recvs = [_remote(blk, full.at[_slot(x, y, 1 - c)], send(0), recv(0), (x, y, 1 - c))]
        recvs += [_remote(blk, full.at[_slot(*chip, c)], send(1 + j), recv(1 + j), (*chip, c))
                  for j, chip in enumerate(chips)]
        return sends, recvs, [pltpu.make_async_copy(blk, mine, loc(0))]

    return _Comm((half_block,), (SDS((2 * N_CHIPS,) + half_block.shape, half_block.dtype),), {}, 4, 1, plan)


def _gather_second(partly_gathered):
    def plan(ins, outs, send, recv, loc):
        (src,), (full,) = ins, outs
        x, y, c = _place()
        chips = _other_chips(x, y)
        sends = [_remote(src.at[_slot(*chip, c)], full.at[_slot(*chip, c)], send(j), recv(j), (x, y, 1 - c))
                 for j, chip in enumerate(chips)]
        recvs = [_remote(src.at[_slot(*chip, 1 - c)], full.at[_slot(*chip, 1 - c)], send(j), recv(j), (x, y, 1 - c))
                 for j, chip in enumerate(chips)]
        return sends, recvs, []

    return _Comm((partly_gathered,), (SDS(partly_gathered.shape, partly_gathered.dtype),), {0: 0}, 3, 0, plan)


def _gather_small(block):
    def plan(ins, outs, send, recv, loc):
        (blk,), (full,) = ins, outs
        x, y, c = _place()
        chips = _other_chips(x, y)
        sends = [_remote(blk, full.at[2 * x + y], send(j), recv(j), (*chip, c)) for j, chip in enumerate(chips)]
        recvs = [_remote(blk, full.at[2 * chip[0] + chip[1]], send(j), recv(j), (*chip, c))
                 for j, chip in enumerate(chips)]
        return sends, recvs, [pltpu.make_async_copy(blk, full.at[2 * x + y], loc(0))]

    return _Comm((block,), (SDS((N_CHIPS,) + block.shape, block.dtype),), {}, 3, 1, plan)


def _pair_send(grads):
    def plan(ins, outs, send, recv, loc):
        (g,), (got,) = ins, outs
        x, y, c = _place()
        copies = [_remote(g.at[j, 1 - c], got.at[j], send(j), recv(j), (x, y, 1 - c)) for j in range(N_CHIPS)]
        return copies, copies, []

    shape = (grads.shape[0],) + grads.shape[2:]
    return _Comm((grads,), (SDS(shape, grads.dtype),), {}, N_CHIPS, 0, plan)


def _chip_exchange(partial):
    def plan(ins, outs, send, recv, loc):
        (p,), (got,) = ins, outs
        x, y, c = _place()
        my_chip = 2 * x + y
        chips = _other_chips(x, y)
        sends = [_remote(p.at[2 * chip[0] + chip[1]], got.at[my_chip], send(j), recv(j), (*chip, c))
                 for j, chip in enumerate(chips)]
        recvs = [_remote(p.at[my_chip], got.at[2 * chip[0] + chip[1]], send(j), recv(j), (*chip, c))
                 for j, chip in enumerate(chips)]
        return sends, recvs, [pltpu.make_async_copy(p.at[my_chip], got.at[my_chip], loc(0))]

    return _Comm((partial,), (SDS(partial.shape, partial.dtype),), {}, 3, 1, plan)


def _pair_sum(name, core, grads, received):
    h = grads.shape[2]

    def body(core_ref, g_ref, r_ref, o_ref):
        o_ref[...] = (g_ref[0] + r_ref[...]).astype(BF16)

    return pl.pallas_call(
        body, name=name,
        grid_spec=pltpu.PrefetchScalarGridSpec(
            num_scalar_prefetch=1, grid=(N_CHIPS,),
            in_specs=[pl.BlockSpec((1, 1, h, D_MODEL), lambda j, core_ref: (j, core_ref[0], 0, 0)),
                      pl.BlockSpec((1, h, D_MODEL), lambda j, core_ref: (j, 0, 0))],
            out_specs=pl.BlockSpec((1, h, D_MODEL), lambda j, core_ref: (j, 0, 0))),
        out_shape=SDS((N_CHIPS, h, D_MODEL), BF16),
        compiler_params=pltpu.CompilerParams(dimension_semantics=("arbitrary",), vmem_limit_bytes=VMEM_LIMIT_V7X),
    )(core, grads, received)


def _finish_reduce(exchanged):
    n = len(exchanged)

    def body(*refs):
        got, out = refs[:n], refs[n:2 * n]
        halves = refs[2 * n:3 * n]
        send_sems, recv_sems, local_sems = refs[3 * n:]
        x, y, c = _place()
        sends, recvs, locs = [], [], []
        for k in range(n):
            g = got[k]
            halves[k][...] = ((g[0].astype(F32) + g[1].astype(F32)) + (g[2].astype(F32) + g[3].astype(F32)))
            locs.append(pltpu.make_async_copy(halves[k], out[k].at[c], local_sems.at[k]))
            sends.append(_remote(halves[k], out[k].at[c], send_sems.at[k], recv_sems.at[k], (x, y, 1 - c)))
            recvs.append(_remote(halves[k], out[k].at[1 - c], send_sems.at[k], recv_sems.at[k], (x, y, 1 - c)))
            locs[-1].start()
            sends[-1].start()
        for cp in recvs:
            cp.wait_recv()
        for cp in sends:
            cp.wait_send()
        for cp in locs:
            cp.wait()

    return pl.pallas_call(
        body, name="finish_reduce", in_specs=[VMEM_WHOLE] * n, out_specs=[ANY] * n,
        out_shape=[SDS((2,) + e.shape[1:], F32) for e in exchanged],
        scratch_shapes=[pltpu.VMEM(e.shape[1:], F32) for e in exchanged]
                       + [pltpu.SemaphoreType.DMA((n,)), pltpu.SemaphoreType.DMA((n,)), pltpu.SemaphoreType.DMA((n,))],
        compiler_params=pltpu.CompilerParams(vmem_limit_bytes=VMEM_LIMIT_V7X),
    )(*exchanged)


SMALL_ROWS = 8


def _all_reduce_small(packed):
    def body(v_ref, o_ref, buf_ref, send_sems, recv_sems):
        x, y, c = _place()
        me = _slot(x, y, c)
        buf_ref[me] = v_ref[...]
        copies = []
        for mask in range(1, 8):
            peer = (x ^ (mask >> 2), y ^ ((mask >> 1) & 1), c ^ (mask & 1))
            copies.append(_remote(v_ref, buf_ref.at[me], send_sems.at[mask - 1], recv_sems.at[mask - 1], peer))
        for cp in copies:
            cp.start()
        for cp in copies:
            cp.wait_recv()
        for cp in copies:
            cp.wait_send()
        total = buf_ref[0]
        for d in range(1, 8):
            total = total + buf_ref[d]
        o_ref[...] = total

    return pl.pallas_call(
        body, name="all_reduce_small", in_specs=[VMEM_WHOLE], out_specs=VMEM_WHOLE,
        out_shape=SDS(packed.shape, F32),
        scratch_shapes=[pltpu.VMEM((8,) + packed.shape, F32), pltpu.SemaphoreType.DMA((7,)), pltpu.SemaphoreType.DMA((7,))],
    )(packed)


def _rope_expansion():
    half = ROT_DIM // 2
    expand = np.zeros((2 * half, 3 * 128), np.float32)
    const = np.zeros((1, 3 * 128), np.float32)
    for lane in range(128):
        d = lane % HEAD_DIM
        if d < ROT_DIM:
            expand[d % half, lane] = 1.0
        else:
            const[0, lane] = 1.0
        if d < half:
            expand[half + d, 128 + lane] = -1.0
        elif d < ROT_DIM:
            expand[half + d - half, 256 + lane] = 1.0
    return expand, const


ROPE_PIECES = 3 * ROT_DIM


def _rope_inputs(seq):
    pos = jnp.arange(seq, dtype=F32)
    inv_freq = ROPE_THETA ** (-jnp.arange(0, ROT_DIM, 2, dtype=F32) / ROT_DIM)
    ang = pos[:, None] * inv_freq[None, :]
    cs = jnp.concatenate([jnp.cos(ang), jnp.sin(ang)], axis=1)
    hi = lax.reduce_precision(cs, 8, 7)
    mid = lax.reduce_precision(cs - hi, 8, 7)
    low = cs - hi - mid
    expand, const = _rope_expansion()
    pieces = jnp.concatenate([hi, mid, low], axis=1).astype(BF16)
    return pieces, jnp.asarray(np.concatenate([expand] * 3, axis=0), BF16), jnp.asarray(const)


def _rope_specs(tb):
    return [pl.BlockSpec((tb, ROPE_PIECES), lambda i: (i, 0)), _resident((ROPE_PIECES, 3 * 128)), _resident((1, 3 * 128))]


def _rope_tile(pieces_ref, expand_ref, const_ref):
    tables = _dot(pieces_ref[...], expand_ref[...]) + const_ref[...]
    return tables[:, 0:128], tables[:, 128:256], tables[:, 256:384]


def _rope(t, c, sa, sb):
    half = ROT_DIM // 2
    return t * c + pltpu.roll(t, 128 - half, 1) * sa + pltpu.roll(t, half, 1) * sb


def _rope_transposed(dt, c, sa, sb):
    half = ROT_DIM // 2
    return dt * c + pltpu.roll(dt * sa, half, 1) + pltpu.roll(dt * sb, 128 - half, 1)


def _cast_halves(core, w_up, w_down, w_out, w_in_t):
    def body(core_ref, up_ref, down_ref, out_ref, in_ref, up_o, down_o, out_o, in_o):
        up_o[...] = up_ref[...].astype(BF16)
        down_o[...] = down_ref[...].astype(BF16)
        out_o[...] = out_ref[...].astype(BF16)
        in_o[...] = in_ref[...].astype(BF16)

    half = lambda rows: pl.BlockSpec((rows, D_MODEL), lambda i, core_ref: (core_ref[0], 0))
    whole = lambda rows: pl.BlockSpec((rows, D_MODEL), lambda i, core_ref: (0, 0))
    rows = (H_UP, H_DOWN, H_OUT, H_IN)
    return pl.pallas_call(
        body, name="cast_halves",
        grid_spec=pltpu.PrefetchScalarGridSpec(
            num_scalar_prefetch=1, grid=(1,), in_specs=[half(r) for r in rows], out_specs=[whole(r) for r in rows]),
        out_shape=[SDS((r, D_MODEL), BF16) for r in rows],
        compiler_params=pltpu.CompilerParams(dimension_semantics=("arbitrary",), vmem_limit_bytes=VMEM_LIMIT_V7X),
    )(core, w_up, w_down, w_out, w_in_t)


def _in_proj(x, g_pre, w_in_t, rope, comm=None):
    seq = x.shape[0]
    tb = TOKEN_TILE

    def body(x_ref, g_ref, w_ref, c_ref, sa_ref, sb_ref,
             q_ref, kd0_ref, kd1_ref, vd0_ref, vd1_ref, gb_ref, gc_ref, xin_ref, hn_ref):
        xv = x_ref[...]
        hn = (xv * _rms(xv) * g_ref[...]).astype(BF16)
        hn_ref[...] = hn
        proj = _dot_nt(hn, w_ref[...].reshape(IN_COLS, D_MODEL))
        c, sa, sb = _rope_tile(c_ref, sa_ref, sb_ref)
        scale = 1.0 / math.sqrt(HEAD_DIM)
        for p in range(Q_WIDTH // 128):
            q_ref[:, 128 * p:128 * (p + 1)] = (_rope(proj[:, 128 * p:128 * (p + 1)], c, sa, sb) * scale).astype(BF16)
        k = _rope(proj[:, Q_WIDTH:Q_WIDTH + KV_WIDTH], c, sa, sb)
        v = proj[:, Q_WIDTH + KV_WIDTH:Q_WIDTH + 2 * KV_WIDTH]
        low = _lane_lt64(k.shape)
        k_sw, v_sw = pltpu.roll(k, HEAD_DIM, 1), pltpu.roll(v, HEAD_DIM, 1)
        kd0_ref[...] = jnp.where(low, k, k_sw).astype(BF16)
        kd1_ref[...] = jnp.where(low, k_sw, k).astype(BF16)
        vd0_ref[...] = jnp.where(low, v, v_sw).astype(BF16)
        vd1_ref[...] = jnp.where(low, v_sw, v).astype(BF16)
        base = Q_WIDTH + 2 * KV_WIDTH
        gb_ref[...] = proj[:, base:base + CONV_WIDTH].astype(BF16)
        gc_ref[...] = proj[:, base + CONV_WIDTH:base + 2 * CONV_WIDTH].astype(BF16)
        xin_ref[...] = proj[:, base + 2 * CONV_WIDTH:base + 3 * CONV_WIDTH].astype(BF16)

    tile = lambda w: pl.BlockSpec((tb, w), lambda i: (i, 0))
    return _pallas(
        body, name="in_proj", grid=(seq // tb,),
        in_specs=[tile(D_MODEL), _resident((1, D_MODEL)), _resident(w_in_t.shape), *_rope_specs(tb)],
        out_specs=[tile(Q_WIDTH), tile(128), tile(128), tile(128), tile(128),
                   tile(CONV_WIDTH), tile(CONV_WIDTH), tile(CONV_WIDTH), tile(D_MODEL)],
        out_shape=[SDS((seq, Q_WIDTH), BF16)] + [SDS((seq, 128), BF16)] * 4
                  + [SDS((seq, CONV_WIDTH), BF16)] * 3 + [SDS((seq, D_MODEL), BF16)],
        operands=(x, g_pre, w_in_t, *rope), comm=comm)


def _attn_valid(i):
    shape = (4 * QBLOCK, 2 * QBLOCK)
    row = lax.broadcasted_iota(jnp.int32, shape, 0)
    col = lax.broadcasted_iota(jnp.int32, shape, 1)
    qi = row & (QBLOCK - 1)
    return (col > qi) & (col <= qi + QBLOCK) & ((col >= QBLOCK) | (i > 0))


def _stack_heads(pair0, pair1):
    low = _lane_lt64(pair0.shape)
    zero = jnp.zeros_like(pair0)
    return jnp.concatenate([jnp.where(low, pair0, zero), jnp.where(low, zero, pair0),
                            jnp.where(low, pair1, zero), jnp.where(low, zero, pair1)], axis=0)


def _unstack_heads(stacked):
    low = _lane_lt64((QBLOCK, 128))
    pair0 = jnp.where(low, stacked[0:QBLOCK], stacked[QBLOCK:2 * QBLOCK])
    pair1 = jnp.where(low, stacked[2 * QBLOCK:3 * QBLOCK], stacked[3 * QBLOCK:4 * QBLOCK])
    return pair0, pair1


def _sink_column(sink_ref, kv_head):
    row = lax.broadcasted_iota(jnp.int32, (4 * QBLOCK, 1), 0)
    s = [sink_ref[0, 4 * kv_head + j] for j in range(4)]
    return jnp.where(row < QBLOCK, s[0], jnp.where(row < 2 * QBLOCK, s[1], jnp.where(row < 3 * QBLOCK, s[2], s[3])))


def _band(ref, i):
    prev = pl.multiple_of(jnp.maximum(i - 1, 0) * QBLOCK, QBLOCK)
    own = pl.multiple_of(i * QBLOCK, QBLOCK)
    return jnp.concatenate([ref[pl.ds(prev, QBLOCK), :], ref[pl.ds(own, QBLOCK), :]], axis=0), prev, own


def _softmax_with_sink(s, sink_col):
    m = jnp.maximum(jnp.max(s, axis=-1, keepdims=True), sink_col)
    p = jnp.exp(s - m)
    e_sink = jnp.exp(sink_col - m)
    inv_l = 1.0 / (jnp.sum(p, axis=-1, keepdims=True) + e_sink)
    return p, e_sink, inv_l


def _attention_fwd(q, kd0, kd1, vd0, vd1, sinks, comm=None):
    seq = q.shape[0]

    nb = ATTN_FWD_BLOCKS

    def body(sink_ref, q_ref, kd0_ref, kd1_ref, vd0_ref, vd1_ref, o_ref):
        for b in range(nb):
            i = pl.program_id(0) * nb + b
            rows = slice(QBLOCK * b, QBLOCK * (b + 1))
            valid = _attn_valid(i)
            for kv_head, (k_ref, v_ref) in enumerate(((kd0_ref, vd0_ref), (kd1_ref, vd1_ref))):
                kband, _, _ = _band(k_ref, i)
                vband, _, _ = _band(v_ref, i)
                base = 256 * kv_head
                qm = _stack_heads(q_ref[rows, base:base + 128], q_ref[rows, base + 128:base + 256])
                s = jnp.where(valid, _dot_nt(qm, kband), NEG_INF)
                p, _, inv_l = _softmax_with_sink(s, _sink_column(sink_ref, kv_head))
                o = _dot(p.astype(BF16), vband) * inv_l
                pair0, pair1 = _unstack_heads(o)
                o_ref[rows, base:base + 128] = pair0.astype(BF16)
                o_ref[rows, base + 128:base + 256] = pair1.astype(BF16)

    blk = pl.BlockSpec((nb * QBLOCK, Q_WIDTH), lambda i: (i, 0))
    full = _resident((seq, 128))
    return _pallas(
        body, name="attention_fwd", grid=(seq // (nb * QBLOCK),),
        in_specs=[pl.BlockSpec(memory_space=pltpu.SMEM), blk, full, full, full, full],
        out_specs=[blk], out_shape=[SDS((seq, Q_WIDTH), BF16)],
        operands=(sinks, q, kd0, kd1, vd0, vd1), comm=comm)


HALO = 16


def _conv_parts(gc, xin, gc_halo, xin_halo, conv_w, first):
    tb = gc.shape[0]
    u = gc.astype(F32) * xin.astype(F32)
    u_halo = jnp.where(first, 0.0, gc_halo.astype(F32) * xin_halo.astype(F32))
    ext = jnp.concatenate([u_halo, u], axis=0)
    u1 = pltpu.roll(ext, 1, 0)[HALO:HALO + tb]
    u2 = pltpu.roll(ext, 2, 0)[HALO:HALO + tb]
    y = conv_w[0:1, :] * u2 + conv_w[1:2, :] * u1 + conv_w[2:3, :] * u
    return u, u1, u2, y


def _halo_prev(tb, w):
    return pl.BlockSpec((HALO, w), lambda i: (jnp.maximum(i * (tb // HALO) - 1, 0), 0))


def _residual_mid(x, mix, g_post_mix):
    mix_f = mix.astype(F32)
    return x + mix_f * _rms(mix_f) * g_post_mix


def _mix_out(attn, gb, gc, xin, conv_w, g_attn, g_conv, w_out, comm=None):
    seq = attn.shape[0]
    tb = TOKEN_TILE

    def body(a_ref, gb_ref, gc_ref, xin_ref, gch_ref, xinh_ref, cw_ref, ga_ref, gcn_ref, w_ref, mix_ref, mixed_ref):
        first = pl.program_id(0) == 0
        _, _, _, y = _conv_parts(gc_ref[...], xin_ref[...], gch_ref[...], xinh_ref[...], cw_ref[...], first)
        conv = gb_ref[...].astype(F32) * y
        a = a_ref[...].astype(F32)
        mixed_ref[:, 0:Q_WIDTH] = (a * _rms(a) * ga_ref[...]).astype(BF16)
        mixed_ref[:, Q_WIDTH:] = (conv * _rms(conv) * gcn_ref[...]).astype(BF16)
        mix_ref[...] = _dot(mixed_ref[...], w_ref[...].reshape(D_MODEL, D_MODEL)).astype(BF16)

    tile = lambda w: pl.BlockSpec((tb, w), lambda i: (i, 0))
    return _pallas(
        body, name="mix_out", grid=(seq // tb,),
        in_specs=[tile(Q_WIDTH), tile(CONV_WIDTH), tile(CONV_WIDTH), tile(CONV_WIDTH),
                  _halo_prev(tb, CONV_WIDTH), _halo_prev(tb, CONV_WIDTH),
                  _resident((CONV_K, CONV_WIDTH)), _resident((1, Q_WIDTH)), _resident((1, CONV_WIDTH)),
                  _resident(w_out.shape)],
        out_specs=[tile(D_MODEL), tile(D_MODEL)],
        out_shape=[SDS((seq, D_MODEL), BF16), SDS((seq, D_MODEL), BF16)],
        operands=(attn, gb, gc, xin, gc, xin, conv_w, g_attn, g_conv, w_out), comm=comm)


def _mlp_loss(x, mix, target, g_post_mix, g_pre_mlp, g_post_mlp, w_up, w_down):
    seq = x.shape[0]
    tb = TOKEN_TILE

    def body(x_ref, mix_ref, t_ref, gpm_ref, g2_ref, g4_ref, wup_ref, wdown_ref,
             up_ref, hn2_ref, dout_ref, dmlp_ref, loss_ref, dg4_ref, act_ref):
        @pl.when(pl.program_id(0) == 0)
        def _():
            loss_ref[...] = jnp.zeros_like(loss_ref)
            dg4_ref[...] = jnp.zeros_like(dg4_ref)

        hv = _residual_mid(x_ref[...], mix_ref[...], gpm_ref[...])
        hn2 = (hv * _rms(hv) * g2_ref[...]).astype(BF16)
        hn2_ref[...] = hn2
        for j in range(N_CHIPS):
            up = _dot(hn2[:, :H_UP], wup_ref[2 * j]) + _dot(hn2[:, H_UP:], wup_ref[2 * j + 1])
            up = jnp.maximum(up, 0.0)
            up_ref[:, 1024 * j:1024 * (j + 1)] = up.astype(BF16)
            act_ref[:, 1024 * j:1024 * (j + 1)] = (up * up).astype(BF16)
        mlp = _dot(act_ref[...], wdown_ref[...].reshape(D_FF, D_MODEL))
        rstd = _rms(mlp)
        zhat = mlp * rstd
        diff = hv + zhat * g4_ref[...] - t_ref[...]
        loss_ref[...] += jnp.sum(jnp.sum(diff * diff, axis=1, keepdims=True), axis=0, keepdims=True)
        dout = diff * (1.0 / D_MODEL)
        dout_ref[...] = dout
        dg4_ref[...] += _colsum(dout * zhat)
        dmlp_ref[...] = _norm_bwd(dout, g4_ref[...], zhat, rstd).astype(BF16)

    tile = lambda w: pl.BlockSpec((tb, w), lambda i: (i, 0))
    return _pallas(
        body, name="mlp_loss", grid=(seq // tb,),
        in_specs=[tile(D_MODEL), tile(D_MODEL), tile(D_MODEL), _resident((1, D_MODEL)), _resident((1, D_MODEL)),
                  _resident((1, D_MODEL)), _resident(w_up.shape), _resident(w_down.shape)],
        out_specs=[tile(D_FF), tile(D_MODEL), tile(D_MODEL), tile(D_MODEL),
                   pl.BlockSpec((1, 1), lambda i: (0, 0)), pl.BlockSpec((1, D_MODEL), lambda i: (0, 0))],
        out_shape=[SDS((seq, D_FF), BF16), SDS((seq, D_MODEL), BF16), SDS((seq, D_MODEL), F32),
                   SDS((seq, D_MODEL), BF16), SDS((1, 1), F32), SDS((1, D_MODEL), F32)],
        scratch=[pltpu.VMEM((tb, D_FF), BF16)],
        operands=(x, mix, target, g_post_mix, g_pre_mlp, g_post_mlp, w_up, w_down))


def _mlp_bwd(dmlp, up, x, dout, mix, g_pre_mlp, g_post_mix, w_up, w_down):
    seq = x.shape[0]
    tb = MLP_BWD_TOKEN_TILE

    def body(dmlp_ref, up_ref, x_ref, dout_ref, mix_ref, g2_ref, gpm_ref, wup_ref, wdown_ref,
             dup_ref, dh_ref, dmix_ref, dg2_ref, dgpm_ref):
        @pl.when(pl.program_id(0) == 0)
        def _():
            dg2_ref[...] = jnp.zeros_like(dg2_ref)
            dgpm_ref[...] = jnp.zeros_like(dgpm_ref)

        dmlp_v = dmlp_ref[...]
        halves = [None, None]
        for j in range(N_CHIPS):
            cols = slice(1024 * j, 1024 * (j + 1))
            dact = jnp.concatenate([_dot_nt(dmlp_v, wdown_ref[2 * j]), _dot_nt(dmlp_v, wdown_ref[2 * j + 1])], axis=1)
            dup = (dact * (2.0 * up_ref[:, cols].astype(F32))).astype(BF16)
            dup_ref[:, cols] = dup
            for half in range(2):
                part = _dot_nt(dup, wup_ref[2 * j + half])
                halves[half] = part if j == 0 else halves[half] + part
        dhn2 = jnp.concatenate(halves, axis=1)
        mix_v = mix_ref[...].astype(F32)
        hv = _residual_mid(x_ref[...], mix_ref[...], gpm_ref[...])
        r2 = _rms(hv)
        hhat = hv * r2
        dg2_ref[...] += _colsum(dhn2 * hhat)
        dh = dout_ref[...] + _norm_bwd(dhn2, g2_ref[...], hhat, r2)
        dh_ref[...] = dh.astype(BF16)
        rz = _rms(mix_v)
        zhat = mix_v * rz
        dgpm_ref[...] += _colsum(dh * zhat)
        dmix_ref[...] = _norm_bwd(dh, gpm_ref[...], zhat, rz).astype(BF16)

    tile = lambda w: pl.BlockSpec((tb, w), lambda i: (i, 0))
    vec = pl.BlockSpec((1, D_MODEL), lambda i: (0, 0))
    return _pallas(
        body, name="mlp_bwd", grid=(seq // tb,),
        in_specs=[tile(D_MODEL), tile(D_FF), tile(D_MODEL), tile(D_MODEL), tile(D_MODEL),
                  _resident((1, D_MODEL)), _resident((1, D_MODEL)), _resident(w_up.shape), _resident(w_down.shape)],
        out_specs=[tile(D_FF), tile(D_MODEL), tile(D_MODEL), vec, vec],
        out_shape=[SDS((seq, D_FF), BF16), SDS((seq, D_MODEL), BF16), SDS((seq, D_MODEL), BF16),
                   SDS((1, D_MODEL), F32), SDS((1, D_MODEL), F32)],
        operands=(dmlp, up, x, dout, mix, g_pre_mlp, g_post_mix, w_up, w_down))


def _mix_bwd(dmix, attn, gb, gc, xin, conv_w, g_attn, g_conv, w_out, comm=None):
    seq = attn.shape[0]
    tb = TOKEN_TILE

    def body(dmix_ref, a_ref, gb_ref, gc_ref, xin_ref, gch_ref, xinh_ref, cw_ref, ga_ref, gcn_ref, w_ref,
             dattn_ref, dgb_ref, dy_ref, dga_ref, dgcn_ref, dcw_ref):
        first = pl.program_id(0) == 0

        @pl.when(first)
        def _():
            dga_ref[...] = jnp.zeros_like(dga_ref)
            dgcn_ref[...] = jnp.zeros_like(dgcn_ref)
            dcw_ref[...] = jnp.zeros_like(dcw_ref)

        dmixed = _dot_nt(dmix_ref[...], w_ref[...].reshape(D_MODEL, D_MODEL))
        a = a_ref[...].astype(F32)
        ra = _rms(a)
        ahat = a * ra
        dan = dmixed[:, 0:Q_WIDTH]
        dga_ref[...] += _colsum(dan * ahat)
        dattn_ref[...] = _norm_bwd(dan, ga_ref[...], ahat, ra).astype(BF16)
        gbv = gb_ref[...].astype(F32)
        u, u1, u2, y = _conv_parts(gc_ref[...], xin_ref[...], gch_ref[...], xinh_ref[...], cw_ref[...], first)
        conv = gbv * y
        rc = _rms(conv)
        chat = conv * rc
        dcn = dmixed[:, Q_WIDTH:]
        dgcn_ref[...] += _colsum(dcn * chat)
        dconv = _norm_bwd(dcn, gcn_ref[...], chat, rc)
        dgb_ref[...] = (dconv * y).astype(BF16)
        dy = dconv * gbv
        dy_ref[...] = dy.astype(BF16)
        dcw_ref[0:1, :] += _colsum(dy * u2)
        dcw_ref[1:2, :] += _colsum(dy * u1)
        dcw_ref[2:3, :] += _colsum(dy * u)

    tile = lambda w: pl.BlockSpec((tb, w), lambda i: (i, 0))
    return _pallas(
        body, name="mix_bwd", grid=(seq // tb,),
        in_specs=[tile(D_MODEL), tile(Q_WIDTH), tile(CONV_WIDTH), tile(CONV_WIDTH), tile(CONV_WIDTH),
                  _halo_prev(tb, CONV_WIDTH), _halo_prev(tb, CONV_WIDTH),
                  _resident((CONV_K, CONV_WIDTH)), _resident((1, Q_WIDTH)), _resident((1, CONV_WIDTH)),
                  _resident(w_out.shape)],
        out_specs=[tile(Q_WIDTH), tile(CONV_WIDTH), tile(CONV_WIDTH),
                   pl.BlockSpec((1, Q_WIDTH), lambda i: (0, 0)), pl.BlockSpec((1, CONV_WIDTH), lambda i: (0, 0)),
                   pl.BlockSpec((CONV_K, CONV_WIDTH), lambda i: (0, 0))],
        out_shape=[SDS((seq, Q_WIDTH), BF16), SDS((seq, CONV_WIDTH), BF16), SDS((seq, CONV_WIDTH), BF16),
                   SDS((1, Q_WIDTH), F32), SDS((1, CONV_WIDTH), F32), SDS((CONV_K, CONV_WIDTH), F32)],
        operands=(dmix, attn, gb, gc, xin, gc, xin, conv_w, g_attn, g_conv, w_out), comm=comm)


def _attention_bwd(q, dattn, attn, kd0, kd1, vd0, vd1, sinks, comm=None):
    seq = q.shape[0]
    nb = ATTN_BWD_BLOCKS

    def body(sink_ref, q_ref, do_ref, o_ref, kd0_ref, kd1_ref, vd0_ref, vd1_ref,
             dq_ref, dk0_ref, dk1_ref, dv0_ref, dv1_ref, dsink_ref):
        @pl.when(pl.program_id(0) == 0)
        def _():
            for r in (dk0_ref, dk1_ref, dv0_ref, dv1_ref, dsink_ref):
                r[...] = jnp.zeros_like(r)

        lane = lax.broadcasted_iota(jnp.int32, (1, 128), 1)
        dsink = jnp.zeros((1, 128), F32)
        for b in range(nb):
            i = pl.program_id(0) * nb + b
            rows = slice(QBLOCK * b, QBLOCK * (b + 1))
            valid = _attn_valid(i)
            for kv_head, (k_ref, v_ref, dk_ref, dv_ref) in enumerate(
                    ((kd0_ref, vd0_ref, dk0_ref, dv0_ref), (kd1_ref, vd1_ref, dk1_ref, dv1_ref))):
                kband, prev, own = _band(k_ref, i)
                vband, _, _ = _band(v_ref, i)
                base = 256 * kv_head
                qm = _stack_heads(q_ref[rows, base:base + 128], q_ref[rows, base + 128:base + 256])
                dom = _stack_heads(do_ref[rows, base:base + 128], do_ref[rows, base + 128:base + 256])
                om = _stack_heads(o_ref[rows, base:base + 128], o_ref[rows, base + 128:base + 256])
                s = jnp.where(valid, _dot_nt(qm, kband), NEG_INF)
                p, e_sink, inv_l = _softmax_with_sink(s, _sink_column(sink_ref, kv_head))
                p = p * inv_l
                delta = jnp.sum(dom.astype(F32) * om.astype(F32), axis=-1, keepdims=True)
                ds = (p * (_dot_nt(dom, vband) - delta)).astype(BF16)
                sink_term = -(e_sink * inv_l) * delta
                for j in range(4):
                    part = jnp.sum(sink_term[QBLOCK * j:QBLOCK * (j + 1)], axis=0, keepdims=True)
                    dsink = dsink + jnp.where(lane == 4 * kv_head + j, part, 0.0)
                pair0, pair1 = _unstack_heads(_dot(ds, kband))
                dq_ref[rows, base:base + 128] = pair0.astype(BF16)
                dq_ref[rows, base + 128:base + 256] = pair1.astype(BF16)
                dkd = _dot_tn(ds, qm)
                dkd = dkd + pltpu.roll(dkd, HEAD_DIM, 1)
                dvd = _dot_tn(p.astype(BF16), dom)
                dvd = dvd + pltpu.roll(dvd, HEAD_DIM, 1)
                dk_ref[pl.ds(prev, QBLOCK), :] += dkd[0:QBLOCK]
                dk_ref[pl.ds(own, QBLOCK), :] += dkd[QBLOCK:]
                dv_ref[pl.ds(prev, QBLOCK), :] += dvd[0:QBLOCK]
                dv_ref[pl.ds(own, QBLOCK), :] += dvd[QBLOCK:]
        dsink_ref[...] += dsink

    blk = pl.BlockSpec((nb * QBLOCK, Q_WIDTH), lambda i: (i, 0))
    full = _resident((seq, 128))
    acc = pl.BlockSpec((seq, 128), lambda i: (0, 0))
    return _pallas(
        body, name="attention_bwd", grid=(seq // (nb * QBLOCK),),
        in_specs=[pl.BlockSpec(memory_space=pltpu.SMEM), blk, blk, blk, full, full, full, full],
        out_specs=[blk, acc, acc, acc, acc, pl.BlockSpec((1, 128), lambda i: (0, 0))],
        out_shape=[SDS((seq, Q_WIDTH), BF16)] + [SDS((seq, 128), F32)] * 4 + [SDS((1, 128), F32)],
        operands=(sinks, q, dattn, attn, kd0, kd1, vd0, vd1), comm=comm)


def _in_proj_bwd(dq, dk0, dk1, dv0, dv1, dgb, dy, gc, xin, conv_w, x, dh, g_pre, w_in_t, rope):
    seq = x.shape[0]
    tb = TOKEN_TILE
    n_tiles = seq // tb

    def body(dq_ref, dk0_ref, dk1_ref, dv0_ref, dv1_ref, dgb_ref, dy_ref, dyh_ref, gc_ref, xin_ref, cw_ref,
             x_ref, dh_ref, g_ref, w_ref, c_ref, sa_ref, sb_ref,
             dproj_ref, gx_ref, dg_ref):
        i = pl.program_id(0)

        @pl.when(i == 0)
        def _():
            dg_ref[...] = jnp.zeros_like(dg_ref)

        c, sa, sb = _rope_tile(c_ref, sa_ref, sb_ref)
        scale = 1.0 / math.sqrt(HEAD_DIM)
        for p in range(Q_WIDTH // 128):
            dproj_ref[:, 128 * p:128 * (p + 1)] = _rope_transposed(
                dq_ref[:, 128 * p:128 * (p + 1)].astype(F32) * scale, c, sa, sb).astype(BF16)
        low = _lane_lt64((tb, 128))
        dk = jnp.where(low, dk0_ref[...], dk1_ref[...])
        dproj_ref[:, Q_WIDTH:Q_WIDTH + KV_WIDTH] = _rope_transposed(dk, c, sa, sb).astype(BF16)
        dproj_ref[:, Q_WIDTH + KV_WIDTH:Q_WIDTH + 2 * KV_WIDTH] = jnp.where(low, dv0_ref[...], dv1_ref[...]).astype(BF16)
        base = Q_WIDTH + 2 * KV_WIDTH
        dproj_ref[:, base:base + CONV_WIDTH] = dgb_ref[...]
        dy = dy_ref[...].astype(F32)
        ext = jnp.concatenate([dy, jnp.where(i == n_tiles - 1, 0.0, dyh_ref[...].astype(F32))], axis=0)
        dy1 = pltpu.roll(ext, tb + HALO - 1, 0)[0:tb]
        dy2 = pltpu.roll(ext, tb + HALO - 2, 0)[0:tb]
        cw = cw_ref[...]
        du = cw[2:3, :] * dy + cw[1:2, :] * dy1 + cw[0:1, :] * dy2
        dproj_ref[:, base + CONV_WIDTH:base + 2 * CONV_WIDTH] = (du * xin_ref[...].astype(F32)).astype(BF16)
        dproj_ref[:, base + 2 * CONV_WIDTH:] = (du * gc_ref[...].astype(F32)).astype(BF16)
        dhn = _dot(dproj_ref[...], w_ref[...].reshape(IN_COLS, D_MODEL))
        xv = x_ref[...]
        r = _rms(xv)
        xhat = xv * r
        dg_ref[...] += _colsum(dhn * xhat)
        gx_ref[...] = dh_ref[...].astype(F32) + _norm_bwd(dhn, g_ref[...], xhat, r)

    tile = lambda w: pl.BlockSpec((tb, w), lambda i: (i, 0))
    halo_next = pl.BlockSpec((HALO, CONV_WIDTH), lambda i: (jnp.minimum((i + 1) * (tb // HALO), seq // HALO - 1), 0))
    return _pallas(
        body, name="in_proj_bwd", grid=(n_tiles,),
        in_specs=[tile(Q_WIDTH), tile(128), tile(128), tile(128), tile(128), tile(CONV_WIDTH), tile(CONV_WIDTH), halo_next,
                  tile(CONV_WIDTH), tile(CONV_WIDTH), _resident((CONV_K, CONV_WIDTH)),
                  tile(D_MODEL), tile(D_MODEL), _resident((1, D_MODEL)), _resident(w_in_t.shape), *_rope_specs(tb)],
        out_specs=[tile(IN_COLS), tile(D_MODEL), pl.BlockSpec((1, D_MODEL), lambda i: (0, 0))],
        out_shape=[SDS((seq, IN_COLS), BF16), SDS((seq, D_MODEL), F32), SDS((1, D_MODEL), F32)],
        operands=(dq, dk0, dk1, dv0, dv1, dgb, dy, dy, gc, xin, conv_w, x, dh, g_pre, w_in_t, *rope))


def _wgrad(name, a, b, *, per_chip, h_rows, square_a=False, comm=None):
    seq = a.shape[0]
    chips_per_step = 1 if per_chip else N_CHIPS
    m = chips_per_step * 2 * h_rows
    bt = min(seq, WGRAD_TOKEN_TILE if m <= 1024 else WGRAD_TOKEN_TILE // 2)
    n_k = seq // bt
    a_cols = m if per_chip else a.shape[1]
    a_wide = a.shape[1] > a_cols
    b_wide = b.shape[1] > D_MODEL

    def body(a_ref, b_ref, g_ref, acc_ref):
        k = pl.program_id(1)

        @pl.when(k == 0)
        def _():
            acc_ref[...] = jnp.zeros_like(acc_ref)

        av = a_ref[...]
        if square_a:
            av = (av.astype(F32) * av.astype(F32)).astype(BF16)
        acc_ref[...] += _dot_tn(av, b_ref[...])

        @pl.when(k == n_k - 1)
        def _():
            for cidx in range(chips_per_step):
                for half in range(2):
                    r0 = (2 * cidx + half) * h_rows
                    g_ref[cidx, half] = acc_ref[r0:r0 + h_rows, :]

    a_spec = pl.BlockSpec((bt, a_cols), (lambda j, k: (k, j)) if a_wide else (lambda j, k: (k, 0)))
    b_spec = pl.BlockSpec((bt, D_MODEL), (lambda j, k: (k, j)) if b_wide else (lambda j, k: (k, 0)))
    g_spec = pl.BlockSpec((chips_per_step, 2, h_rows, D_MODEL), lambda j, k: (j, 0, 0, 0))
    return _pallas(
        body, name=name, grid=(N_CHIPS if per_chip else 1, n_k),
        in_specs=[a_spec, b_spec], out_specs=[g_spec], out_shape=[SDS((N_CHIPS, 2, h_rows, D_MODEL), F32)],
        scratch=[pltpu.VMEM((m, D_MODEL), F32)], operands=(a, b), comm=comm)


def _adamw_math(w, g, m, v):
    m = ADAM_B1 * m + (1.0 - ADAM_B1) * g
    v = ADAM_B2 * v + (1.0 - ADAM_B2) * (g * g)
    m_hat = m / (1.0 - ADAM_B1 ** ADAM_STEP)
    v_hat = v / (1.0 - ADAM_B2 ** ADAM_STEP)
    delta = -ADAM_LR * (m_hat / (jnp.sqrt(v_hat) + ADAM_EPS) + ADAM_WD * w)
    return delta, m, v


def _adamw_rows(name, reduced, w, m, v, rt):
    per_half = reduced.shape[1] // rt

    def body(r_ref, w_ref, m_ref, v_ref, g_out, d_out, m_out, v_out):
        g = r_ref[0]
        g_out[...] = g
        d_out[...], m_out[...], v_out[...] = _adamw_math(w_ref[...], g, m_ref[...], v_ref[...])

    blk = pl.BlockSpec((rt, D_MODEL), lambda h, r: (h * per_half + r, 0))
    return _pallas(
        body, name=name, grid=(2, per_half),
        in_specs=[pl.BlockSpec((1, rt, D_MODEL), lambda h, r: (h, r, 0)), blk, blk, blk],
        out_specs=[blk, blk, blk, blk], out_shape=[SDS(w.shape, F32)] * 4, operands=(reduced, w, m, v))


def _adamw_small(w, g, m, v):
    def body(w_ref, g_ref, m_ref, v_ref, d_out, m_out, v_out):
        d_out[...], m_out[...], v_out[...] = _adamw_math(w_ref[...], g_ref[...], m_ref[...], v_ref[...])

    return pl.pallas_call(body, name="adamw_small", in_specs=[VMEM_WHOLE] * 4, out_specs=[VMEM_WHOLE] * 3,
                          out_shape=[SDS(w.shape, F32)] * 3)(w, g, m, v)


SMALL_VECTORS = ("pre_mix_norm", "post_mix_norm", "pre_mlp_norm", "post_mlp_norm")
SMALL_NAMES = SMALL_VECTORS + ("attn_group_norm", "conv_group_norm", "conv_w", "attn_sinks")


def _pack_small(p):
    rows = [p[n].reshape(1, D_MODEL) for n in SMALL_VECTORS]
    rows.append(jnp.concatenate([p["attn_group_norm"].reshape(1, -1), p["conv_group_norm"].reshape(1, -1)], axis=1))
    cw = p["conv_w"].reshape(CONV_K, -1)
    rows.append(jnp.pad(cw, ((0, 1), (0, CONV_WIDTH - cw.shape[1]))).reshape(2, D_MODEL))
    last = jnp.concatenate([p["attn_sinks"].reshape(1, 8), p.get("loss_sum", jnp.zeros((1, 1), F32))], axis=1)
    rows.append(jnp.pad(last, ((0, 0), (0, D_MODEL - 9))))
    return jnp.concatenate(rows, axis=0)


def _unpack_small(packed, conv_width):
    out = {n: packed[i:i + 1] for i, n in enumerate(SMALL_VECTORS)}
    out["attn_group_norm"] = packed[4:5, :Q_WIDTH]
    out["conv_group_norm"] = packed[4:5, Q_WIDTH:]
    out["conv_w"] = packed[5:7].reshape(4, CONV_WIDTH)[:CONV_K, :conv_width].reshape(1, CONV_K, conv_width)
    out["attn_sinks"] = packed[7:8, :8]
    out["loss_sum"] = packed[7, 8]
    return out


WEIGHT_ORDER = ("pre_mix_norm", "w_in", "conv_w", "attn_sinks", "attn_group_norm", "conv_group_norm", "w_out",
                "post_mix_norm", "pre_mlp_norm", "w_up", "w_down", "post_mlp_norm")


def kernel(x, pre_mix_norm, w_in, conv_w, attn_sinks, attn_group_norm, conv_group_norm, w_out, post_mix_norm, pre_mlp_norm, w_up, w_down, post_mlp_norm, loss_target, m_pre_mix_norm, m_w_in, m_conv_w, m_attn_sinks, m_attn_group_norm, m_conv_group_norm, m_w_out, m_post_mix_norm, m_pre_mlp_norm, m_w_up, m_w_down, m_post_mlp_norm, v_pre_mix_norm, v_w_in, v_conv_w, v_attn_sinks, v_attn_group_norm, v_conv_group_norm, v_w_out, v_post_mix_norm, v_pre_mlp_norm, v_w_up, v_w_down, v_post_mlp_norm):
    w = dict(pre_mix_norm=pre_mix_norm, w_in=w_in, conv_w=conv_w, attn_sinks=attn_sinks, attn_group_norm=attn_group_norm,
             conv_group_norm=conv_group_norm, w_out=w_out, post_mix_norm=post_mix_norm, pre_mlp_norm=pre_mlp_norm,
             w_up=w_up, w_down=w_down, post_mlp_norm=post_mlp_norm)
    m = dict(pre_mix_norm=m_pre_mix_norm, w_in=m_w_in, conv_w=m_conv_w, attn_sinks=m_attn_sinks,
             attn_group_norm=m_attn_group_norm, conv_group_norm=m_conv_group_norm, w_out=m_w_out,
             post_mix_norm=m_post_mix_norm, pre_mlp_norm=m_pre_mlp_norm, w_up=m_w_up, w_down=m_w_down,
             post_mlp_norm=m_post_mlp_norm)
    v = dict(pre_mix_norm=v_pre_mix_norm, w_in=v_w_in, conv_w=v_conv_w, attn_sinks=v_attn_sinks,
             attn_group_norm=v_attn_group_norm, conv_group_norm=v_conv_group_norm, w_out=v_w_out,
             post_mix_norm=v_post_mix_norm, pre_mlp_norm=v_pre_mlp_norm, w_up=v_w_up, w_down=v_w_down,
             post_mlp_norm=v_post_mlp_norm)
    core = lax.axis_index("c").astype(jnp.int32).reshape(1)
    chip = 2 * lax.axis_index("x") + lax.axis_index("y")
    local_conv = conv_w.shape[2]
    xs, target = x[0], loss_target[0]
    rope = _rope_inputs(xs.shape[0])

    hb_up, hb_down, hb_out, hb_in = _cast_halves(core, w_up[0], w_down[0], w_out[0], w_in[0].T)
    gather_in = _gather_first(hb_in)
    conv_pad = jnp.pad(conv_w[0], ((0, 8 - CONV_K), (0, 0)))
    wf_in, conv_all = _comm_only("gather_in_first", _merge(gather_in, _gather_small(conv_pad)))
    wf_in, = _comm_only("gather_in_second", _gather_second(wf_in))
    conv_full = conv_all[:, :CONV_K, :].transpose(1, 0, 2).reshape(CONV_K, CONV_WIDTH)

    *proj, wf_up, wf_out = _in_proj(xs, pre_mix_norm, wf_in, rope, comm=_merge(_gather_first(hb_up), _gather_first(hb_out)))
    q, kd0, kd1, vd0, vd1, gb, gc, xin, hn = proj
    attn, wf_up, wf_out, wf_down = _attention_fwd(
        q, kd0, kd1, vd0, vd1, attn_sinks,
        comm=_merge(_gather_second(wf_up), _gather_second(wf_out), _gather_first(hb_down)))
    mix, mixed, wf_down = _mix_out(attn, gb, gc, xin, conv_full, attn_group_norm, conv_group_norm, wf_out,
                                   comm=_gather_second(wf_down))
    up, hn2, dout, dmlp, loss_sum, dg_post_mlp = _mlp_loss(xs, mix, target, post_mix_norm, pre_mlp_norm, post_mlp_norm,
                                                           wf_up, wf_down)

    dup, dh, dmix, dg_pre_mlp, dg_post_mix = _mlp_bwd(dmlp, up, xs, dout, mix, pre_mlp_norm, post_mix_norm, wf_up, wf_down)
    g_down, = _wgrad("wgrad_down", up, dmlp, per_chip=True, h_rows=H_DOWN, square_a=True)
    g_up, got_down = _wgrad("wgrad_up", hn2, dup, per_chip=True, h_rows=H_UP, comm=_pair_send(g_down))
    p_down = _pair_sum("pair_sum_down", core, g_down, got_down)
    dattn, dgb, dy, dg_attn, dg_conv, dconv_w, ex_down, got_up = _mix_bwd(
        dmix, attn, gb, gc, xin, conv_full, attn_group_norm, conv_group_norm, wf_out,
        comm=_merge(_chip_exchange(p_down), _pair_send(g_up)))
    p_up = _pair_sum("pair_sum_up", core, g_up, got_up)
    g_out, = _wgrad("wgrad_out", mixed, dmix, per_chip=False, h_rows=H_OUT)
    dq, dk0, dk1, dv0, dv1, dsink, ex_up, got_out = _attention_bwd(
        q, dattn, attn, kd0, kd1, vd0, vd1, attn_sinks, comm=_merge(_chip_exchange(p_up), _pair_send(g_out)))
    p_out = _pair_sum("pair_sum_out", core, g_out, got_out)
    dproj, grad_x, dg_pre_mix = _in_proj_bwd(dq, dk0, dk1, dv0, dv1, dgb, dy, gc, xin, conv_full, xs, dh, pre_mix_norm,
                                             wf_in, rope)
    g_in, ex_out = _wgrad("wgrad_in", dproj, hn, per_chip=False, h_rows=H_IN, comm=_chip_exchange(p_out))
    got_in, = _comm_only("pair_send_in", _pair_send(g_in))
    ex_in, = _comm_only("chip_exchange_in", _chip_exchange(_pair_sum("pair_sum_in", core, g_in, got_in)))
    r_down, r_up, r_out, r_in = _finish_reduce([ex_down, ex_up, ex_out, ex_in])

    out_g, out_d, out_m, out_v = {}, {}, {}, {}
    out_g["w_up"], out_d["w_up"], out_m["w_up"], out_v["w_up"] = _adamw_rows(
        "adamw_up", r_up, w_up[0], m_w_up[0], v_w_up[0], 256)
    out_g["w_down"], out_d["w_down"], out_m["w_down"], out_v["w_down"] = _adamw_rows(
        "adamw_down", r_down, w_down[0], m_w_down[0], v_w_down[0], 256)
    out_g["w_out"], out_d["w_out"], out_m["w_out"], out_v["w_out"] = _adamw_rows(
        "adamw_out", r_out, w_out[0], m_w_out[0], v_w_out[0], H_OUT)
    in_t = _adamw_rows("adamw_in", r_in, w_in[0].T, m_w_in[0].T, v_w_in[0].T, H_IN)
    out_g["w_in"], out_d["w_in"], out_m["w_in"], out_v["w_in"] = [t.T for t in in_t]

    small = dict(pre_mix_norm=dg_pre_mix, conv_w=dconv_w, attn_sinks=dsink[:, :8], attn_group_norm=dg_attn,
                 conv_group_norm=dg_conv, post_mix_norm=dg_post_mix, pre_mlp_norm=dg_pre_mlp, post_mlp_norm=dg_post_mlp,
                 loss_sum=loss_sum)
    small_sum = _unpack_small(_all_reduce_small(_pack_small(small)), CONV_WIDTH)
    loss = small_sum["loss_sum"] * (0.5 / D_MODEL)
    small_sum["conv_w"] = lax.dynamic_slice_in_dim(small_sum["conv_w"], chip * local_conv, local_conv, axis=2)
    packed = [_pack_small({n: t[n] for n in SMALL_NAMES}) for t in (w, small_sum, m, v)]
    small_d, small_m, small_v = [_unpack_small(t, local_conv) for t in _adamw_small(*packed)]
    for n in SMALL_NAMES:
        out_g[n], out_d[n], out_m[n], out_v[n] = small_sum[n], small_d[n], small_m[n], small_v[n]

    def shaped(d):
        return [d[n].reshape(w[n].shape) for n in WEIGHT_ORDER]

    return (loss, grad_x[None], *shaped(out_g), *shaped(out_d), *shaped(out_m), *shaped(out_v))
```

```python
import math
from typing import Callable, NamedTuple

import jax
import jax.numpy as jnp
import numpy as np
from jax import lax
from jax.experimental import pallas as pl
from jax.experimental.pallas import tpu as pltpu

F32 = jnp.float32
BF16 = jnp.bfloat16

D_MODEL = 1024
HEAD_DIM = 64
Q_WIDTH = 512
KV_WIDTH = 128
CONV_WIDTH = 512
CONV_K = 3
D_FF = 4096
IN_COLS = 2304
QBLOCK = 128
ROT_DIM = 16
ROPE_THETA = 500000.0
NORM_EPS = 1e-6
NEG_INF = -1e30
N_CHIPS = 4

ADAM_LR = 0.001
ADAM_B1 = 0.9
ADAM_B2 = 0.999
ADAM_EPS = 1e-08
ADAM_WD = 0.01
ADAM_STEP = 10

H_UP, H_DOWN, H_OUT, H_IN = 512, 512, 128, 288

TOKEN_TILE = 512
MLP_BWD_TOKEN_TILE = 256
ATTN_FWD_BLOCKS = 4
ATTN_BWD_BLOCKS = 2
WGRAD_TOKEN_TILE = 2048
VMEM_LIMIT_V7X = 56 * 1024 * 1024

MESH = pl.DeviceIdType.MESH
ANY = pl.BlockSpec(memory_space=pl.ANY)
VMEM_WHOLE = pl.BlockSpec(memory_space=pltpu.VMEM)
SDS = jax.ShapeDtypeStruct


def _resident(shape):
    zeros = (0,) * len(shape)
    return pl.BlockSpec(shape, lambda *_: zeros, pipeline_mode=pl.Buffered(1))


def _rms(v):
    return lax.rsqrt(jnp.mean(v * v, axis=-1, keepdims=True) + NORM_EPS)


def _norm_bwd(dy, gain, vhat, rstd):
    t = dy * gain
    return rstd * (t - vhat * jnp.mean(t * vhat, axis=-1, keepdims=True))


def _colsum(v):
    return jnp.sum(v, axis=0, keepdims=True)


def _dot_nt(a, b):
    return lax.dot_general(a, b, (((1,), (1,)), ((), ())), preferred_element_type=F32)


def _dot_tn(a, b):
    return lax.dot_general(a, b, (((0,), (0,)), ((), ())), preferred_element_type=F32)


def _dot(a, b):
    return jnp.dot(a, b, preferred_element_type=F32)


def _lane_lt64(shape):
    return lax.broadcasted_iota(jnp.int32, shape, 1) < HEAD_DIM


class _Comm(NamedTuple):
    operands: tuple
    out_shapes: tuple
    aliases: dict
    n_remote: int
    n_local: int
    plan: Callable


def _merge(*comms):
    operands, out_shapes, aliases, parts = [], [], {}, []
    n_remote = n_local = 0
    for cm in comms:
        parts.append((len(operands), len(out_shapes), n_remote, n_local, cm))
        for k, v in cm.aliases.items():
            aliases[len(operands) + k] = len(out_shapes) + v
        operands += cm.operands
        out_shapes += cm.out_shapes
        n_remote += cm.n_remote
        n_local += cm.n_local

    def plan(ins, outs, send, recv, loc):
        sends, recvs, locs = [], [], []
        for i0, o0, r0, l0, cm in parts:
            s, r, l = cm.plan(ins[i0:i0 + len(cm.operands)], outs[o0:o0 + len(cm.out_shapes)],
                              lambda k, r0=r0: send(r0 + k), lambda k, r0=r0: recv(r0 + k), lambda k, l0=l0: loc(l0 + k))
            sends, recvs, locs = sends + s, recvs + r, locs + l
        return sends, recvs, locs

    return _Comm(tuple(operands), tuple(out_shapes), aliases, n_remote, n_local, plan)


def _sem_scratch(comm):
    return [pltpu.SemaphoreType.DMA((max(comm.n_remote, 1),)), pltpu.SemaphoreType.DMA((max(comm.n_remote, 1),)),
            pltpu.SemaphoreType.DMA((max(comm.n_local, 1),))]


def _pallas(body, *, name, grid, in_specs, out_specs, out_shape, operands, scratch=(), comm=None):
    params = pltpu.CompilerParams(dimension_semantics=("arbitrary",) * len(grid), vmem_limit_bytes=VMEM_LIMIT_V7X)
    if comm is None:
        return pl.pallas_call(body, name=name, grid=grid, in_specs=in_specs, out_specs=out_specs, out_shape=out_shape,
                              scratch_shapes=list(scratch), compiler_params=params)(*operands)
    n_in, n_out, n_scr = len(in_specs), len(out_specs), len(scratch)
    c_in, c_out = len(comm.operands), len(comm.out_shapes)

    def with_comm(*refs):
        ins, c_ins = refs[:n_in], refs[n_in:n_in + c_in]
        o0 = n_in + c_in
        outs, c_outs = refs[o0:o0 + n_out], refs[o0 + n_out:o0 + n_out + c_out]
        s0 = o0 + n_out + c_out
        scr = refs[s0:s0 + n_scr]
        send_sems, recv_sems, local_sems = refs[s0 + n_scr:]
        first = last = None
        for axis, size in enumerate(grid):
            at_start, at_end = pl.program_id(axis) == 0, pl.program_id(axis) == size - 1
            first = at_start if first is None else jnp.logical_and(first, at_start)
            last = at_end if last is None else jnp.logical_and(last, at_end)

        def copies():
            return comm.plan(c_ins, c_outs, lambda k: send_sems.at[k], lambda k: recv_sems.at[k],
                             lambda k: local_sems.at[k])

        @pl.when(first)
        def _():
            sends, _, locs = copies()
            for cp in sends + locs:
                cp.start()

        body(*ins, *outs, *scr)

        @pl.when(last)
        def _():
            sends, recvs, locs = copies()
            for cp in recvs:
                cp.wait_recv()
            for cp in sends:
                cp.wait_send()
            for cp in locs:
                cp.wait()

    return pl.pallas_call(
        with_comm, name=name, grid=grid,
        in_specs=list(in_specs) + [ANY] * c_in, out_specs=list(out_specs) + [ANY] * c_out,
        out_shape=list(out_shape) + list(comm.out_shapes),
        scratch_shapes=list(scratch) + _sem_scratch(comm),
        input_output_aliases={n_in + k: n_out + v for k, v in comm.aliases.items()},
        compiler_params=params)(*operands, *comm.operands)


def _place():
    return lax.axis_index("x"), lax.axis_index("y"), lax.axis_index("c")


def _other_chips(x, y):
    return [(1 - x, y), (x, 1 - y), (1 - x, 1 - y)]


def _slot(px, py, pc):
    return 4 * px + 2 * py + pc


def _remote(src, dst, send_sem, recv_sem, to):
    return pltpu.make_async_remote_copy(src_ref=src, dst_ref=dst, send_sem=send_sem, recv_sem=recv_sem,
                                        device_id=to, device_id_type=MESH)


def _gather_first(half_block):
    def plan(ins, outs, send, recv, loc):
        (blk,), (full,) = ins, outs
        x, y, c = _place()
        chips = _other_chips(x, y)
        mine = full.at[_slot(x, y, c)]
        sends = [_remote(blk, mine, send(0), recv(0), (x, y, 1 - c))]
        sends += [_remote(blk, mine, send(1 + j), recv(1 + j), (*chip, c)) for j, chip in enumerate(chips)]
        recvs = [_remote(blk, full.at[_slot(x, y, 1 - c)], send(0), recv(0), (x, y, 1 - c))]
        recvs += [_remote(blk, full.at[_slot(*chip, c)], send(1 + j), recv(1 + j), (*chip, c))
                  for j, chip in enumerate(chips)]
        return sends, recvs, [pltpu.make_async_copy(blk, mine, loc(0))]

    return _Comm((half_block,), (SDS((2 * N_CHIPS,) + half_block.shape, half_block.dtype),), {}, 4, 1, plan)


def _gather_second(partly_gathered):
    def plan(ins, outs, send, recv, loc):
        (src,), (full,) = ins, outs
        x, y, c = _place()
        chips = _other_chips(x, y)
        sends = [_remote(src.at[_slot(*chip, c)], full.at[_slot(*chip, c)], send(j), recv(j), (x, y, 1 - c))
                 for j, chip in enumerate(chips)]
        recvs = [_remote(src.at[_slot(*chip, 1 - c)], full.at[_slot(*chip, 1 - c)], send(j), recv(j), (x, y, 1 - c))
                 for j, chip in enumerate(chips)]
        return sends, recvs, []

    return _Comm((partly_gathered,), (SDS(partly_gathered.shape, partly_gathered.dtype),), {0: 0}, 3, 0, plan)


def _gather_whole(half_block, small_block):
    def body(blk_ref, small_ref, out_ref, small_out_ref, send_sems, recv_sems, local_sems):
        x, y, c = _place()
        me, sibling = (x, y, c), (x, y, 1 - c)
        chips = _other_chips(x, y)

        def copy(k, block, to, src=None):
            return _remote(out_ref.at[_slot(*block)] if src is None else src, out_ref.at[_slot(*block)],
                           send_sems.at[k], recv_sems.at[k], to)

        def small_copy(k, chip, to):
            return _remote(small_ref, small_out_ref.at[2 * chip[0] + chip[1]], send_sems.at[7 + k], recv_sems.at[7 + k], to)

        mine = pltpu.make_async_copy(blk_ref, out_ref.at[_slot(*me)], local_sems.at[0])
        mine_small = pltpu.make_async_copy(small_ref, small_out_ref.at[2 * x + y], local_sems.at[1])
        mine.start()
        mine_small.start()
        first = [copy(0, me, sibling, src=blk_ref)]
        first += [copy(1 + j, me, (*chip, c), src=blk_ref) for j, chip in enumerate(chips)]
        first += [small_copy(j, (x, y), (*chip, c)) for j, chip in enumerate(chips)]
        for cp in first:
            cp.start()
        passed = [copy(4 + j, (*chip, c), sibling) for j, chip in enumerate(chips)]
        for j, chip in enumerate(chips):
            copy(1 + j, (*chip, c), me).wait_recv()
            passed[j].start()
        copy(0, sibling, me).wait_recv()
        for j, chip in enumerate(chips):
            copy(4 + j, (*chip, 1 - c), me).wait_recv()
            small_copy(j, chip, me).wait_recv()
        for cp in first + passed:
            cp.wait_send()
        mine.wait()
        mine_small.wait()

    return pl.pallas_call(
        body, name="gather_whole", in_specs=[ANY, ANY], out_specs=[ANY, ANY],
        out_shape=[SDS((2 * N_CHIPS,) + half_block.shape, half_block.dtype),
                   SDS((N_CHIPS,) + small_block.shape, small_block.dtype)],
        scratch_shapes=[pltpu.SemaphoreType.DMA((10,)), pltpu.SemaphoreType.DMA((10,)), pltpu.SemaphoreType.DMA((2,))],
    )(half_block, small_block)


def _pair_send(grads):
    def plan(ins, outs, send, recv, loc):
        (g,), (got,) = ins, outs
        x, y, c = _place()
        copies = [_remote(g.at[j, 1 - c], got.at[j], send(j), recv(j), (x, y, 1 - c)) for j in range(N_CHIPS)]
        return copies, copies, []

    shape = (grads.shape[0],) + grads.shape[2:]
    return _Comm((grads,), (SDS(shape, grads.dtype),), {}, N_CHIPS, 0, plan)


def _chip_exchange(partial):
    def plan(ins, outs, send, recv, loc):
        (p,), (got,) = ins, outs
        x, y, c = _place()
        my_chip = 2 * x + y
        chips = _other_chips(x, y)
        sends = [_remote(p.at[2 * chip[0] + chip[1]], got.at[my_chip], send(j), recv(j), (*chip, c))
                 for j, chip in enumerate(chips)]
        recvs = [_remote(p.at[my_chip], got.at[2 * chip[0] + chip[1]], send(j), recv(j), (*chip, c))
                 for j, chip in enumerate(chips)]
        return sends, recvs, [pltpu.make_async_copy(p.at[my_chip], got.at[my_chip], loc(0))]

    return _Comm((partial,), (SDS(partial.shape, partial.dtype),), {}, 3, 1, plan)


def _pair_sum(name, core, grads, received):
    h = grads.shape[2]

    def body(core_ref, g_ref, r_ref, o_ref):
        o_ref[...] = (g_ref[0] + r_ref[...]).astype(BF16)

    return pl.pallas_call(
        body, name=name,
        grid_spec=pltpu.PrefetchScalarGridSpec(
            num_scalar_prefetch=1, grid=(N_CHIPS,),
            in_specs=[pl.BlockSpec((1, 1, h, D_MODEL), lambda j, core_ref: (j, core_ref[0], 0, 0)),
                      pl.BlockSpec((1, h, D_MODEL), lambda j, core_ref: (j, 0, 0))],
            out_specs=pl.BlockSpec((1, h, D_MODEL), lambda j, core_ref: (j, 0, 0))),
        out_shape=SDS((N_CHIPS, h, D_MODEL), BF16),
        compiler_params=pltpu.CompilerParams(dimension_semantics=("arbitrary",), vmem_limit_bytes=VMEM_LIMIT_V7X),
    )(core, grads, received)


SMALL_ROWS = 8


def _sum_blocks(ref):
    return (ref[0].astype(F32) + ref[1].astype(F32)) + (ref[2].astype(F32) + ref[3].astype(F32))


def _tail_reduce(last_grads, exchanged, small):
    n = len(exchanged)
    h = last_grads.shape[2]

    def body(*refs):
        g_ref, ex, small_ref = refs[0], refs[1:1 + n], refs[1 + n]
        o0 = 2 + n
        out, out_last, small_out = refs[o0:o0 + n], refs[o0 + n], refs[o0 + n + 1]
        s0 = o0 + n + 2
        halves, half_last = refs[s0:s0 + n], refs[s0 + n]
        own, got, part, exch, small_buf = refs[s0 + n + 1:s0 + n + 6]
        pair_send, pair_recv, chip_send, chip_recv, share_send, share_recv, small_send, small_recv, local_sems = refs[s0 + n + 6:]
        x, y, c = _place()
        sibling = (x, y, 1 - c)
        my_chip, me = 2 * x + y, _slot(x, y, c)
        chips = _other_chips(x, y)

        to_sibling = [_remote(g_ref.at[j, 1 - c], got.at[j], pair_send.at[j], pair_recv.at[j], sibling)
                      for j in range(N_CHIPS)]
        load_own = [pltpu.make_async_copy(g_ref.at[j, c], own.at[j], local_sems.at[j]) for j in range(N_CHIPS)]
        for cp in to_sibling + load_own:
            cp.start()

        small_buf[me] = small_ref[...]
        small_copies = []
        for mask in range(1, 8):
            peer = (x ^ (mask >> 2), y ^ ((mask >> 1) & 1), c ^ (mask & 1))
            small_copies.append(_remote(small_ref, small_buf.at[me], small_send.at[mask - 1], small_recv.at[mask - 1], peer))
        for cp in small_copies:
            cp.start()

        def share(k, half_ref, out_ref):
            keep = pltpu.make_async_copy(half_ref, out_ref.at[c], local_sems.at[N_CHIPS + k])
            give = _remote(half_ref, out_ref.at[c], share_send.at[k], share_recv.at[k], sibling)
            take = _remote(half_ref, out_ref.at[1 - c], share_send.at[k], share_recv.at[k], sibling)
            keep.start()
            give.start()
            return keep, give, take

        shares = []
        for k in range(n):
            halves[k][...] = _sum_blocks(ex[k])
            shares.append(share(k, halves[k], out[k]))

        for cp in to_sibling:
            cp.wait_recv()
        for cp in load_own:
            cp.wait()
        part[...] = (own[...] + got[...]).astype(BF16)
        exch[my_chip] = part[my_chip]
        to_chips = [_remote(part.at[2 * chip[0] + chip[1]], exch.at[my_chip], chip_send.at[j], chip_recv.at[j], (*chip, c))
                    for j, chip in enumerate(chips)]
        from_chips = [_remote(part.at[my_chip], exch.at[2 * chip[0] + chip[1]], chip_send.at[j], chip_recv.at[j], (*chip, c))
                      for j, chip in enumerate(chips)]
        for cp in to_chips:
            cp.start()

        for cp in small_copies:
            cp.wait_recv()
        total = small_buf[0]
        for d in range(1, 8):
            total = total + small_buf[d]
        small_out[...] = total

        for cp in from_chips:
            cp.wait_recv()
        half_last[...] = _sum_blocks(exch)
        shares.append(share(n, half_last, out_last))

        for keep, give, take in shares:
            take.wait_recv()
            give.wait_send()
            keep.wait()
        for cp in to_sibling + to_chips + small_copies:
            cp.wait_send()

    blocks = (N_CHIPS, h, D_MODEL)
    return pl.pallas_call(
        body, name="tail_reduce",
        in_specs=[ANY] + [VMEM_WHOLE] * (n + 1), out_specs=[ANY] * (n + 1) + [VMEM_WHOLE],
        out_shape=[SDS((2,) + e.shape[1:], F32) for e in exchanged] + [SDS((2, h, D_MODEL), F32), SDS(small.shape, F32)],
        scratch_shapes=[pltpu.VMEM(e.shape[1:], F32) for e in exchanged] + [pltpu.VMEM((h, D_MODEL), F32)]
                       + [pltpu.VMEM(blocks, F32), pltpu.VMEM(blocks, F32), pltpu.VMEM(blocks, BF16), pltpu.VMEM(blocks, BF16),
                          pltpu.VMEM((8,) + small.shape, F32)]
                       + [pltpu.SemaphoreType.DMA((N_CHIPS,)), pltpu.SemaphoreType.DMA((N_CHIPS,)),
                          pltpu.SemaphoreType.DMA((3,)), pltpu.SemaphoreType.DMA((3,)),
                          pltpu.SemaphoreType.DMA((n + 1,)), pltpu.SemaphoreType.DMA((n + 1,)),
                          pltpu.SemaphoreType.DMA((7,)), pltpu.SemaphoreType.DMA((7,)),
                          pltpu.SemaphoreType.DMA((N_CHIPS + n + 1,))],
        compiler_params=pltpu.CompilerParams(vmem_limit_bytes=VMEM_LIMIT_V7X),
    )(last_grads, *exchanged, small)


def _rope_expansion():
    half = ROT_DIM // 2
    expand = np.zeros((2 * half, 3 * 128), np.float32)
    const = np.zeros((1, 3 * 128), np.float32)
    for lane in range(128):
        d = lane % HEAD_DIM
        if d < ROT_DIM:
            expand[d % half, lane] = 1.0
        else:
            const[0, lane] = 1.0
        if d < half:
            expand[half + d, 128 + lane] = -1.0
        elif d < ROT_DIM:
            expand[half + d - half, 256 + lane] = 1.0
    return expand, const


ROPE_PIECES = 3 * ROT_DIM


def _rope_inputs(seq):
    pos = jnp.arange(seq, dtype=F32)
    inv_freq = ROPE_THETA ** (-jnp.arange(0, ROT_DIM, 2, dtype=F32) / ROT_DIM)
    ang = pos[:, None] * inv_freq[None, :]
    cs = jnp.concatenate([jnp.cos(ang), jnp.sin(ang)], axis=1)
    hi = lax.reduce_precision(cs, 8, 7)
    mid = lax.reduce_precision(cs - hi, 8, 7)
    low = cs - hi - mid
    expand, const = _rope_expansion()
    pieces = jnp.concatenate([hi, mid, low], axis=1).astype(BF16)
    return pieces, jnp.asarray(np.concatenate([expand] * 3, axis=0), BF16), jnp.asarray(const)


def _rope_specs(tb):
    return [pl.BlockSpec((tb, ROPE_PIECES), lambda i: (i, 0)), _resident((ROPE_PIECES, 3 * 128)), _resident((1, 3 * 128))]


def _rope_tile(pieces_ref, expand_ref, const_ref):
    tables = _dot(pieces_ref[...], expand_ref[...]) + const_ref[...]
    return tables[:, 0:128], tables[:, 128:256], tables[:, 256:384]


def _rope(t, c, sa, sb):
    half = ROT_DIM // 2
    return t * c + pltpu.roll(t, 128 - half, 1) * sa + pltpu.roll(t, half, 1) * sb


def _rope_transposed(dt, c, sa, sb):
    half = ROT_DIM // 2
    return dt * c + pltpu.roll(dt * sa, half, 1) + pltpu.roll(dt * sb, 128 - half, 1)


def _cast_halves(core, w_up, w_down, w_out, w_in_t):
    def body(core_ref, up_ref, down_ref, out_ref, in_ref, up_o, down_o, out_o, in_o):
        up_o[...] = up_ref[...].astype(BF16)
        down_o[...] = down_ref[...].astype(BF16)
        out_o[...] = out_ref[...].astype(BF16)
        in_o[...] = in_ref[...].astype(BF16)

    half = lambda rows: pl.BlockSpec((rows, D_MODEL), lambda i, core_ref: (core_ref[0], 0))
    whole = lambda rows: pl.BlockSpec((rows, D_MODEL), lambda i, core_ref: (0, 0))
    rows = (H_UP, H_DOWN, H_OUT, H_IN)
    return pl.pallas_call(
        body, name="cast_halves",
        grid_spec=pltpu.PrefetchScalarGridSpec(
            num_scalar_prefetch=1, grid=(1,), in_specs=[half(r) for r in rows], out_specs=[whole(r) for r in rows]),
        out_shape=[SDS((r, D_MODEL), BF16) for r in rows],
        compiler_params=pltpu.CompilerParams(dimension_semantics=("arbitrary",), vmem_limit_bytes=VMEM_LIMIT_V7X),
    )(core, w_up, w_down, w_out, w_in_t)


def _in_proj(x, g_pre, w_in_t, rope, comm=None):
    seq = x.shape[0]
    tb = TOKEN_TILE

    def body(x_ref, g_ref, w_ref, c_ref, sa_ref, sb_ref,
             q_ref, kd0_ref, kd1_ref, vd0_ref, vd1_ref, gb_ref, gc_ref, xin_ref, hn_ref):
        xv = x_ref[...]
        hn = (xv * _rms(xv) * g_ref[...]).astype(BF16)
        hn_ref[...] = hn
        proj = _dot_nt(hn, w_ref[...].reshape(IN_COLS, D_MODEL))
        c, sa, sb = _rope_tile(c_ref, sa_ref, sb_ref)
        scale = 1.0 / math.sqrt(HEAD_DIM)
        for p in range(Q_WIDTH // 128):
            q_ref[:, 128 * p:128 * (p + 1)] = (_rope(proj[:, 128 * p:128 * (p + 1)], c, sa, sb) * scale).astype(BF16)
        k = _rope(proj[:, Q_WIDTH:Q_WIDTH + KV_WIDTH], c, sa, sb)
        v = proj[:, Q_WIDTH + KV_WIDTH:Q_WIDTH + 2 * KV_WIDTH]
        low = _lane_lt64(k.shape)
        k_sw, v_sw = pltpu.roll(k, HEAD_DIM, 1), pltpu.roll(v, HEAD_DIM, 1)
        kd0_ref[...] = jnp.where(low, k, k_sw).astype(BF16)
        kd1_ref[...] = jnp.where(low, k_sw, k).astype(BF16)
        vd0_ref[...] = jnp.where(low, v, v_sw).astype(BF16)
        vd1_ref[...] = jnp.where(low, v_sw, v).astype(BF16)
        base = Q_WIDTH + 2 * KV_WIDTH
        gb_ref[...] = proj[:, base:base + CONV_WIDTH].astype(BF16)
        gc_ref[...] = proj[:, base + CONV_WIDTH:base + 2 * CONV_WIDTH].astype(BF16)
        xin_ref[...] = proj[:, base + 2 * CONV_WIDTH:base + 3 * CONV_WIDTH].astype(BF16)

    tile = lambda w: pl.BlockSpec((tb, w), lambda i: (i, 0))
    return _pallas(
        body, name="in_proj", grid=(seq // tb,),
        in_specs=[tile(D_MODEL), _resident((1, D_MODEL)), _resident(w_in_t.shape), *_rope_specs(tb)],
        out_specs=[tile(Q_WIDTH), tile(128), tile(128), tile(128), tile(128),
                   tile(CONV_WIDTH), tile(CONV_WIDTH), tile(CONV_WIDTH), tile(D_MODEL)],
        out_shape=[SDS((seq, Q_WIDTH), BF16)] + [SDS((seq, 128), BF16)] * 4
                  + [SDS((seq, CONV_WIDTH), BF16)] * 3 + [SDS((seq, D_MODEL), BF16)],
        operands=(x, g_pre, w_in_t, *rope), comm=comm)


def _attn_valid(i):
    shape = (4 * QBLOCK, 2 * QBLOCK)
    row = lax.broadcasted_iota(jnp.int32, shape, 0)
    col = lax.broadcasted_iota(jnp.int32, shape, 1)
    qi = row & (QBLOCK - 1)
    return (col > qi) & (col <= qi + QBLOCK) & ((col >= QBLOCK) | (i > 0))


def _stack_heads(pair0, pair1):
    low = _lane_lt64(pair0.shape)
    zero = jnp.zeros_like(pair0)
    return jnp.concatenate([jnp.where(low, pair0, zero), jnp.where(low, zero, pair0),
                            jnp.where(low, pair1, zero), jnp.where(low, zero, pair1)], axis=0)


def _unstack_heads(stacked):
    low = _lane_lt64((QBLOCK, 128))
    pair0 = jnp.where(low, stacked[0:QBLOCK], stacked[QBLOCK:2 * QBLOCK])
    pair1 = jnp.where(low, stacked[2 * QBLOCK:3 * QBLOCK], stacked[3 * QBLOCK:4 * QBLOCK])
    return pair0, pair1


def _sink_column(sink_ref, kv_head):
    row = lax.broadcasted_iota(jnp.int32, (4 * QBLOCK, 1), 0)
    s = [sink_ref[0, 4 * kv_head + j] for j in range(4)]
    return jnp.where(row < QBLOCK, s[0], jnp.where(row < 2 * QBLOCK, s[1], jnp.where(row < 3 * QBLOCK, s[2], s[3])))


def _band(ref, i):
    prev = pl.multiple_of(jnp.maximum(i - 1, 0) * QBLOCK, QBLOCK)
    own = pl.multiple_of(i * QBLOCK, QBLOCK)
    return jnp.concatenate([ref[pl.ds(prev, QBLOCK), :], ref[pl.ds(own, QBLOCK), :]], axis=0), prev, own


def _softmax_with_sink(s, sink_col):
    m = jnp.maximum(jnp.max(s, axis=-1, keepdims=True), sink_col)
    p = jnp.exp(s - m)
    e_sink = jnp.exp(sink_col - m)
    inv_l = 1.0 / (jnp.sum(p, axis=-1, keepdims=True) + e_sink)
    return p, e_sink, inv_l


def _attention_fwd(q, kd0, kd1, vd0, vd1, sinks, comm=None):
    seq = q.shape[0]

    nb = ATTN_FWD_BLOCKS

    def body(sink_ref, q_ref, kd0_ref, kd1_ref, vd0_ref, vd1_ref, o_ref):
        for b in range(nb):
            i = pl.program_id(0) * nb + b
            rows = slice(QBLOCK * b, QBLOCK * (b + 1))
            valid = _attn_valid(i)
            for kv_head, (k_ref, v_ref) in enumerate(((kd0_ref, vd0_ref), (kd1_ref, vd1_ref))):
                kband, _, _ = _band(k_ref, i)
                vband, _, _ = _band(v_ref, i)
                base = 256 * kv_head
                qm = _stack_heads(q_ref[rows, base:base + 128], q_ref[rows, base + 128:base + 256])
                s = jnp.where(valid, _dot_nt(qm, kband), NEG_INF)
                p, _, inv_l = _softmax_with_sink(s, _sink_column(sink_ref, kv_head))
                o = _dot(p.astype(BF16), vband) * inv_l
                pair0, pair1 = _unstack_heads(o)
                o_ref[rows, base:base + 128] = pair0.astype(BF16)
                o_ref[rows, base + 128:base + 256] = pair1.astype(BF16)

    blk = pl.BlockSpec((nb * QBLOCK, Q_WIDTH), lambda i: (i, 0))
    full = _resident((seq, 128))
    return _pallas(
        body, name="attention_fwd", grid=(seq // (nb * QBLOCK),),
        in_specs=[pl.BlockSpec(memory_space=pltpu.SMEM), blk, full, full, full, full],
        out_specs=[blk], out_shape=[SDS((seq, Q_WIDTH), BF16)],
        operands=(sinks, q, kd0, kd1, vd0, vd1), comm=comm)


HALO = 16


def _conv_parts(gc, xin, gc_halo, xin_halo, conv_w, first):
    tb = gc.shape[0]
    u = gc.astype(F32) * xin.astype(F32)
    u_halo = jnp.where(first, 0.0, gc_halo.astype(F32) * xin_halo.astype(F32))
    ext = jnp.concatenate([u_halo, u], axis=0)
    u1 = pltpu.roll(ext, 1, 0)[HALO:HALO + tb]
    u2 = pltpu.roll(ext, 2, 0)[HALO:HALO + tb]
    y = conv_w[0:1, :] * u2 + conv_w[1:2, :] * u1 + conv_w[2:3, :] * u
    return u, u1, u2, y


def _halo_prev(tb, w):
    return pl.BlockSpec((HALO, w), lambda i: (jnp.maximum(i * (tb // HALO) - 1, 0), 0))


def _residual_mid(x, mix, g_post_mix):
    mix_f = mix.astype(F32)
    return x + mix_f * _rms(mix_f) * g_post_mix


def _mix_out(attn, gb, gc, xin, conv_w, g_attn, g_conv, w_out, comm=None):
    seq = attn.shape[0]
    tb = TOKEN_TILE

    def body(a_ref, gb_ref, gc_ref, xin_ref, gch_ref, xinh_ref, cw_ref, ga_ref, gcn_ref, w_ref, mix_ref, mixed_ref):
        first = pl.program_id(0) == 0
        _, _, _, y = _conv_parts(gc_ref[...], xin_ref[...], gch_ref[...], xinh_ref[...], cw_ref[...], first)
        conv = gb_ref[...].astype(F32) * y
        a = a_ref[...].astype(F32)
        mixed_ref[:, 0:Q_WIDTH] = (a * _rms(a) * ga_ref[...]).astype(BF16)
        mixed_ref[:, Q_WIDTH:] = (conv * _rms(conv) * gcn_ref[...]).astype(BF16)
        mix_ref[...] = _dot(mixed_ref[...], w_ref[...].reshape(D_MODEL, D_MODEL)).astype(BF16)

    tile = lambda w: pl.BlockSpec((tb, w), lambda i: (i, 0))
    return _pallas(
        body, name="mix_out", grid=(seq // tb,),
        in_specs=[tile(Q_WIDTH), tile(CONV_WIDTH), tile(CONV_WIDTH), tile(CONV_WIDTH),
                  _halo_prev(tb, CONV_WIDTH), _halo_prev(tb, CONV_WIDTH),
                  _resident((CONV_K, CONV_WIDTH)), _resident((1, Q_WIDTH)), _resident((1, CONV_WIDTH)),
                  _resident(w_out.shape)],
        out_specs=[tile(D_MODEL), tile(D_MODEL)],
        out_shape=[SDS((seq, D_MODEL), BF16), SDS((seq, D_MODEL), BF16)],
        operands=(attn, gb, gc, xin, gc, xin, conv_w, g_attn, g_conv, w_out), comm=comm)


def _mlp_loss(x, mix, target, g_post_mix, g_pre_mlp, g_post_mlp, w_up, w_down):
    seq = x.shape[0]
    tb = TOKEN_TILE

    def body(x_ref, mix_ref, t_ref, gpm_ref, g2_ref, g4_ref, wup_ref, wdown_ref,
             up_ref, hn2_ref, dout_ref, dmlp_ref, loss_ref, dg4_ref, act_ref):
        @pl.when(pl.program_id(0) == 0)
        def _():
            loss_ref[...] = jnp.zeros_like(loss_ref)
            dg4_ref[...] = jnp.zeros_like(dg4_ref)

        hv = _residual_mid(x_ref[...], mix_ref[...], gpm_ref[...])
        hn2 = (hv * _rms(hv) * g2_ref[...]).astype(BF16)
        hn2_ref[...] = hn2
        for j in range(N_CHIPS):
            up = _dot(hn2[:, :H_UP], wup_ref[2 * j]) + _dot(hn2[:, H_UP:], wup_ref[2 * j + 1])
            up = jnp.maximum(up, 0.0)
            up_ref[:, 1024 * j:1024 * (j + 1)] = up.astype(BF16)
            act_ref[:, 1024 * j:1024 * (j + 1)] = (up * up).astype(BF16)
        mlp = _dot(act_ref[...], wdown_ref[...].reshape(D_FF, D_MODEL))
        rstd = _rms(mlp)
        zhat = mlp * rstd
        diff = hv + zhat * g4_ref[...] - t_ref[...]
        loss_ref[...] += jnp.sum(jnp.sum(diff * diff, axis=1, keepdims=True), axis=0, keepdims=True)
        dout = diff * (1.0 / D_MODEL)
        dout_ref[...] = dout
        dg4_ref[...] += _colsum(dout * zhat)
        dmlp_ref[...] = _norm_bwd(dout, g4_ref[...], zhat, rstd).astype(BF16)

    tile = lambda w: pl.BlockSpec((tb, w), lambda i: (i, 0))
    return _pallas(
        body, name="mlp_loss", grid=(seq // tb,),
        in_specs=[tile(D_MODEL), tile(D_MODEL), tile(D_MODEL), _resident((1, D_MODEL)), _resident((1, D_MODEL)),
                  _resident((1, D_MODEL)), _resident(w_up.shape), _resident(w_down.shape)],
        out_specs=[tile(D_FF), tile(D_MODEL), tile(D_MODEL), tile(D_MODEL),
                   pl.BlockSpec((1, 1), lambda i: (0, 0)), pl.BlockSpec((1, D_MODEL), lambda i: (0, 0))],
        out_shape=[SDS((seq, D_FF), BF16), SDS((seq, D_MODEL), BF16), SDS((seq, D_MODEL), F32),
                   SDS((seq, D_MODEL), BF16), SDS((1, 1), F32), SDS((1, D_MODEL), F32)],
        scratch=[pltpu.VMEM((tb, D_FF), BF16)],
        operands=(x, mix, target, g_post_mix, g_pre_mlp, g_post_mlp, w_up, w_down))


def _mlp_bwd(dmlp, up, x, dout, mix, g_pre_mlp, g_post_mix, w_up, w_down):
    seq = x.shape[0]
    tb = MLP_BWD_TOKEN_TILE

    def body(dmlp_ref, up_ref, x_ref, dout_ref, mix_ref, g2_ref, gpm_ref, wup_ref, wdown_ref,
             dup_ref, dh_ref, dmix_ref, dg2_ref, dgpm_ref):
        @pl.when(pl.program_id(0) == 0)
        def _():
            dg2_ref[...] = jnp.zeros_like(dg2_ref)
            dgpm_ref[...] = jnp.zeros_like(dgpm_ref)

        dmlp_v = dmlp_ref[...]
        halves = [None, None]
        for j in range(N_CHIPS):
            cols = slice(1024 * j, 1024 * (j + 1))
            dact = jnp.concatenate([_dot_nt(dmlp_v, wdown_ref[2 * j]), _dot_nt(dmlp_v, wdown_ref[2 * j + 1])], axis=1)
            dup = (dact * (2.0 * up_ref[:, cols].astype(F32))).astype(BF16)
            dup_ref[:, cols] = dup
            for half in range(2):
                part = _dot_nt(dup, wup_ref[2 * j + half])
                halves[half] = part if j == 0 else halves[half] + part
        dhn2 = jnp.concatenate(halves, axis=1)
        mix_v = mix_ref[...].astype(F32)
        hv = _residual_mid(x_ref[...], mix_ref[...], gpm_ref[...])
        r2 = _rms(hv)
        hhat = hv * r2
        dg2_ref[...] += _colsum(dhn2 * hhat)
        dh = dout_ref[...] + _norm_bwd(dhn2, g2_ref[...], hhat, r2)
        dh_ref[...] = dh.astype(BF16)
        rz = _rms(mix_v)
        zhat = mix_v * rz
        dgpm_ref[...] += _colsum(dh * zhat)
        dmix_ref[...] = _norm_bwd(dh, gpm_ref[...], zhat, rz).astype(BF16)

    tile = lambda w: pl.BlockSpec((tb, w), lambda i: (i, 0))
    vec = pl.BlockSpec((1, D_MODEL), lambda i: (0, 0))
    return _pallas(
        body, name="mlp_bwd", grid=(seq // tb,),
        in_specs=[tile(D_MODEL), tile(D_FF), tile(D_MODEL), tile(D_MODEL), tile(D_MODEL),
                  _resident((1, D_MODEL)), _resident((1, D_MODEL)), _resident(w_up.shape), _resident(w_down.shape)],
        out_specs=[tile(D_FF), tile(D_MODEL), tile(D_MODEL), vec, vec],
        out_shape=[SDS((seq, D_FF), BF16), SDS((seq, D_MODEL), BF16), SDS((seq, D_MODEL), BF16),
                   SDS((1, D_MODEL), F32), SDS((1, D_MODEL), F32)],
        operands=(dmlp, up, x, dout, mix, g_pre_mlp, g_post_mix, w_up, w_down))


def _mix_bwd(dmix, attn, gb, gc, xin, conv_w, g_attn, g_conv, w_out, comm=None):
    seq = attn.shape[0]
    tb = TOKEN_TILE

    def body(dmix_ref, a_ref, gb_ref, gc_ref, xin_ref, gch_ref, xinh_ref, cw_ref, ga_ref, gcn_ref, w_ref,
             dattn_ref, dgb_ref, dy_ref, dga_ref, dgcn_ref, dcw_ref):
        first = pl.program_id(0) == 0

        @pl.when(first)
        def _():
            dga_ref[...] = jnp.zeros_like(dga_ref)
            dgcn_ref[...] = jnp.zeros_like(dgcn_ref)
            dcw_ref[...] = jnp.zeros_like(dcw_ref)

        dmixed = _dot_nt(dmix_ref[...], w_ref[...].reshape(D_MODEL, D_MODEL))
        a = a_ref[...].astype(F32)
        ra = _rms(a)
        ahat = a * ra
        dan = dmixed[:, 0:Q_WIDTH]
        dga_ref[...] += _colsum(dan * ahat)
        dattn_ref[...] = _norm_bwd(dan, ga_ref[...], ahat, ra).astype(BF16)
        gbv = gb_ref[...].astype(F32)
        u, u1, u2, y = _conv_parts(gc_ref[...], xin_ref[...], gch_ref[...], xinh_ref[...], cw_ref[...], first)
        conv = gbv * y
        rc = _rms(conv)
        chat = conv * rc
        dcn = dmixed[:, Q_WIDTH:]
        dgcn_ref[...] += _colsum(dcn * chat)
        dconv = _norm_bwd(dcn, gcn_ref[...], chat, rc)
        dgb_ref[...] = (dconv * y).astype(BF16)
        dy = dconv * gbv
        dy_ref[...] = dy.astype(BF16)
        dcw_ref[0:1, :] += _colsum(dy * u2)
        dcw_ref[1:2, :] += _colsum(dy * u1)
        dcw_ref[2:3, :] += _colsum(dy * u)

    tile = lambda w: pl.BlockSpec((tb, w), lambda i: (i, 0))
    return _pallas(
        body, name="mix_bwd", grid=(seq // tb,),
        in_specs=[tile(D_MODEL), tile(Q_WIDTH), tile(CONV_WIDTH), tile(CONV_WIDTH), tile(CONV_WIDTH),
                  _halo_prev(tb, CONV_WIDTH), _halo_prev(tb, CONV_WIDTH),
                  _resident((CONV_K, CONV_WIDTH)), _resident((1, Q_WIDTH)), _resident((1, CONV_WIDTH)),
                  _resident(w_out.shape)],
        out_specs=[tile(Q_WIDTH), tile(CONV_WIDTH), tile(CONV_WIDTH),
                   pl.BlockSpec((1, Q_WIDTH), lambda i: (0, 0)), pl.BlockSpec((1, CONV_WIDTH), lambda i: (0, 0)),
                   pl.BlockSpec((CONV_K, CONV_WIDTH), lambda i: (0, 0))],
        out_shape=[SDS((seq, Q_WIDTH), BF16), SDS((seq, CONV_WIDTH), BF16), SDS((seq, CONV_WIDTH), BF16),
                   SDS((1, Q_WIDTH), F32), SDS((1, CONV_WIDTH), F32), SDS((CONV_K, CONV_WIDTH), F32)],
        operands=(dmix, attn, gb, gc, xin, gc, xin, conv_w, g_attn, g_conv, w_out), comm=comm)


def _attention_bwd(q, dattn, attn, kd0, kd1, vd0, vd1, sinks, comm=None):
    seq = q.shape[0]
    nb = ATTN_BWD_BLOCKS

    def body(sink_ref, q_ref, do_ref, o_ref, kd0_ref, kd1_ref, vd0_ref, vd1_ref,
             dq_ref, dk0_ref, dk1_ref, dv0_ref, dv1_ref, dsink_ref):
        @pl.when(pl.program_id(0) == 0)
        def _():
            for r in (dk0_ref, dk1_ref, dv0_ref, dv1_ref, dsink_ref):
                r[...] = jnp.zeros_like(r)

        lane = lax.broadcasted_iota(jnp.int32, (1, 128), 1)
        dsink = jnp.zeros((1, 128), F32)
        for b in range(nb):
            i = pl.program_id(0) * nb + b
            rows = slice(QBLOCK * b, QBLOCK * (b + 1))
            valid = _attn_valid(i)
            for kv_head, (k_ref, v_ref, dk_ref, dv_ref) in enumerate(
                    ((kd0_ref, vd0_ref, dk0_ref, dv0_ref), (kd1_ref, vd1_ref, dk1_ref, dv1_ref))):
                kband, prev, own = _band(k_ref, i)
                vband, _, _ = _band(v_ref, i)
                base = 256 * kv_head
                qm = _stack_heads(q_ref[rows, base:base + 128], q_ref[rows, base + 128:base + 256])
                dom = _stack_heads(do_ref[rows, base:base + 128], do_ref[rows, base + 128:base + 256])
                om = _stack_heads(o_ref[rows, base:base + 128], o_ref[rows, base + 128:base + 256])
                s = jnp.where(valid, _dot_nt(qm, kband), NEG_INF)
                p, e_sink, inv_l = _softmax_with_sink(s, _sink_column(sink_ref, kv_head))
                p = p * inv_l
                delta = jnp.sum(dom.astype(F32) * om.astype(F32), axis=-1, keepdims=True)
                ds = (p * (_dot_nt(dom, vband) - delta)).astype(BF16)
                sink_term = -(e_sink * inv_l) * delta
                for j in range(4):
                    part = jnp.sum(sink_term[QBLOCK * j:QBLOCK * (j + 1)], axis=0, keepdims=True)
                    dsink = dsink + jnp.where(lane == 4 * kv_head + j, part, 0.0)
                pair0, pair1 = _unstack_heads(_dot(ds, kband))
                dq_ref[rows, base:base + 128] = pair0.astype(BF16)
                dq_ref[rows, base + 128:base + 256] = pair1.astype(BF16)
                dkd = _dot_tn(ds, qm)
                dkd = dkd + pltpu.roll(dkd, HEAD_DIM, 1)
                dvd = _dot_tn(p.astype(BF16), dom)
                dvd = dvd + pltpu.roll(dvd, HEAD_DIM, 1)
                dk_ref[pl.ds(prev, QBLOCK), :] += dkd[0:QBLOCK]
                dk_ref[pl.ds(own, QBLOCK), :] += dkd[QBLOCK:]
                dv_ref[pl.ds(prev, QBLOCK), :] += dvd[0:QBLOCK]
                dv_ref[pl.ds(own, QBLOCK), :] += dvd[QBLOCK:]
        dsink_ref[...] += dsink

    blk = pl.BlockSpec((nb * QBLOCK, Q_WIDTH), lambda i: (i, 0))
    full = _resident((seq, 128))
    acc = pl.BlockSpec((seq, 128), lambda i: (0, 0))
    return _pallas(
        body, name="attention_bwd", grid=(seq // (nb * QBLOCK),),
        in_specs=[pl.BlockSpec(memory_space=pltpu.SMEM), blk, blk, blk, full, full, full, full],
        out_specs=[blk, acc, acc, acc, acc, pl.BlockSpec((1, 128), lambda i: (0, 0))],
        out_shape=[SDS((seq, Q_WIDTH), BF16)] + [SDS((seq, 128), F32)] * 4 + [SDS((1, 128), F32)],
        operands=(sinks, q, dattn, attn, kd0, kd1, vd0, vd1), comm=comm)


def _in_proj_bwd(dq, dk0, dk1, dv0, dv1, dgb, dy, gc, xin, conv_w, x, dh, g_pre, w_in_t, rope):
    seq = x.shape[0]
    tb = TOKEN_TILE
    n_tiles = seq // tb

    def body(dq_ref, dk0_ref, dk1_ref, dv0_ref, dv1_ref, dgb_ref, dy_ref, dyh_ref, gc_ref, xin_ref, cw_ref,
             x_ref, dh_ref, g_ref, w_ref, c_ref, sa_ref, sb_ref,
             dproj_ref, gx_ref, dg_ref):
        i = pl.program_id(0)

        @pl.when(i == 0)
        def _():
            dg_ref[...] = jnp.zeros_like(dg_ref)

        c, sa, sb = _rope_tile(c_ref, sa_ref, sb_ref)
        scale = 1.0 / math.sqrt(HEAD_DIM)
        for p in range(Q_WIDTH // 128):
            dproj_ref[:, 128 * p:128 * (p + 1)] = _rope_transposed(
                dq_ref[:, 128 * p:128 * (p + 1)].astype(F32) * scale, c, sa, sb).astype(BF16)
        low = _lane_lt64((tb, 128))
        dk = jnp.where(low, dk0_ref[...], dk1_ref[...])
        dproj_ref[:, Q_WIDTH:Q_WIDTH + KV_WIDTH] = _rope_transposed(dk, c, sa, sb).astype(BF16)
        dproj_ref[:, Q_WIDTH + KV_WIDTH:Q_WIDTH + 2 * KV_WIDTH] = jnp.where(low, dv0_ref[...], dv1_ref[...]).astype(BF16)
        base = Q_WIDTH + 2 * KV_WIDTH
        dproj_ref[:, base:base + CONV_WIDTH] = dgb_ref[...]
        dy = dy_ref[...].astype(F32)
        ext = jnp.concatenate([dy, jnp.where(i == n_tiles - 1, 0.0, dyh_ref[...].astype(F32))], axis=0)
        dy1 = pltpu.roll(ext, tb + HALO - 1, 0)[0:tb]
        dy2 = pltpu.roll(ext, tb + HALO - 2, 0)[0:tb]
        cw = cw_ref[...]
        du = cw[2:3, :] * dy + cw[1:2, :] * dy1 + cw[0:1, :] * dy2
        dproj_ref[:, base + CONV_WIDTH:base + 2 * CONV_WIDTH] = (du * xin_ref[...].astype(F32)).astype(BF16)
        dproj_ref[:, base + 2 * CONV_WIDTH:] = (du * gc_ref[...].astype(F32)).astype(BF16)
        dhn = _dot(dproj_ref[...], w_ref[...].reshape(IN_COLS, D_MODEL))
        xv = x_ref[...]
        r = _rms(xv)
        xhat = xv * r
        dg_ref[...] += _colsum(dhn * xhat)
        gx_ref[...] = dh_ref[...].astype(F32) + _norm_bwd(dhn, g_ref[...], xhat, r)

    tile = lambda w: pl.BlockSpec((tb, w), lambda i: (i, 0))
    halo_next = pl.BlockSpec((HALO, CONV_WIDTH), lambda i: (jnp.minimum((i + 1) * (tb // HALO), seq // HALO - 1), 0))
    return _pallas(
        body, name="in_proj_bwd", grid=(n_tiles,),
        in_specs=[tile(Q_WIDTH), tile(128), tile(128), tile(128), tile(128), tile(CONV_WIDTH), tile(CONV_WIDTH), halo_next,
                  tile(CONV_WIDTH), tile(CONV_WIDTH), _resident((CONV_K, CONV_WIDTH)),
                  tile(D_MODEL), tile(D_MODEL), _resident((1, D_MODEL)), _resident(w_in_t.shape), *_rope_specs(tb)],
        out_specs=[tile(IN_COLS), tile(D_MODEL), pl.BlockSpec((1, D_MODEL), lambda i: (0, 0))],
        out_shape=[SDS((seq, IN_COLS), BF16), SDS((seq, D_MODEL), F32), SDS((1, D_MODEL), F32)],
        operands=(dq, dk0, dk1, dv0, dv1, dgb, dy, dy, gc, xin, conv_w, x, dh, g_pre, w_in_t, *rope))


def _wgrad(name, a, b, *, per_chip, h_rows, square_a=False, comm=None):
    seq = a.shape[0]
    chips_per_step = 1 if per_chip else N_CHIPS
    m = chips_per_step * 2 * h_rows
    bt = min(seq, WGRAD_TOKEN_TILE if m <= 1024 else WGRAD_TOKEN_TILE // 2)
    n_k = seq // bt
    a_cols = m if per_chip else a.shape[1]
    a_wide = a.shape[1] > a_cols
    b_wide = b.shape[1] > D_MODEL

    def body(a_ref, b_ref, g_ref, acc_ref):
        k = pl.program_id(1)

        @pl.when(k == 0)
        def _():
            acc_ref[...] = jnp.zeros_like(acc_ref)

        av = a_ref[...]
        if square_a:
            av = (av.astype(F32) * av.astype(F32)).astype(BF16)
        acc_ref[...] += _dot_tn(av, b_ref[...])

        @pl.when(k == n_k - 1)
        def _():
            for cidx in range(chips_per_step):
                for half in range(2):
                    r0 = (2 * cidx + half) * h_rows
                    g_ref[cidx, half] = acc_ref[r0:r0 + h_rows, :]

    a_spec = pl.BlockSpec((bt, a_cols), (lambda j, k: (k, j)) if a_wide else (lambda j, k: (k, 0)))
    b_spec = pl.BlockSpec((bt, D_MODEL), (lambda j, k: (k, j)) if b_wide else (lambda j, k: (k, 0)))
    g_spec = pl.BlockSpec((chips_per_step, 2, h_rows, D_MODEL), lambda j, k: (j, 0, 0, 0))
    return _pallas(
        body, name=name, grid=(N_CHIPS if per_chip else 1, n_k),
        in_specs=[a_spec, b_spec], out_specs=[g_spec], out_shape=[SDS((N_CHIPS, 2, h_rows, D_MODEL), F32)],
        scratch=[pltpu.VMEM((m, D_MODEL), F32)], operands=(a, b), comm=comm)


def _adamw_math(w, g, m, v):
    m = ADAM_B1 * m + (1.0 - ADAM_B1) * g
    v = ADAM_B2 * v + (1.0 - ADAM_B2) * (g * g)
    m_hat = m / (1.0 - ADAM_B1 ** ADAM_STEP)
    v_hat = v / (1.0 - ADAM_B2 ** ADAM_STEP)
    delta = -ADAM_LR * (m_hat / (jnp.sqrt(v_hat) + ADAM_EPS) + ADAM_WD * w)
    return delta, m, v


def _adamw_rows(name, reduced, w, m, v, rt):
    per_half = reduced.shape[1] // rt

    def body(r_ref, w_ref, m_ref, v_ref, g_out, d_out, m_out, v_out):
        g = r_ref[0]
        g_out[...] = g
        d_out[...], m_out[...], v_out[...] = _adamw_math(w_ref[...], g, m_ref[...], v_ref[...])

    blk = pl.BlockSpec((rt, D_MODEL), lambda h, r: (h * per_half + r, 0))
    return _pallas(
        body, name=name, grid=(2, per_half),
        in_specs=[pl.BlockSpec((1, rt, D_MODEL), lambda h, r: (h, r, 0)), blk, blk, blk],
        out_specs=[blk, blk, blk, blk], out_shape=[SDS(w.shape, F32)] * 4, operands=(reduced, w, m, v))


def _adamw_small(w, g, m, v):
    def body(w_ref, g_ref, m_ref, v_ref, d_out, m_out, v_out):
        d_out[...], m_out[...], v_out[...] = _adamw_math(w_ref[...], g_ref[...], m_ref[...], v_ref[...])

    return pl.pallas_call(body, name="adamw_small", in_specs=[VMEM_WHOLE] * 4, out_specs=[VMEM_WHOLE] * 3,
                          out_shape=[SDS(w.shape, F32)] * 3)(w, g, m, v)


SMALL_VECTORS = ("pre_mix_norm", "post_mix_norm", "pre_mlp_norm", "post_mlp_norm")
SMALL_NAMES = SMALL_VECTORS + ("attn_group_norm", "conv_group_norm", "conv_w", "attn_sinks")


def _pack_small(p):
    rows = [p[n].reshape(1, D_MODEL) for n in SMALL_VECTORS]
    rows.append(jnp.concatenate([p["attn_group_norm"].reshape(1, -1), p["conv_group_norm"].reshape(1, -1)], axis=1))
    cw = p["conv_w"].reshape(CONV_K, -1)
    rows.append(jnp.pad(cw, ((0, 1), (0, CONV_WIDTH - cw.shape[1]))).reshape(2, D_MODEL))
    last = jnp.concatenate([p["attn_sinks"].reshape(1, 8), p.get("loss_sum", jnp.zeros((1, 1), F32))], axis=1)
    rows.append(jnp.pad(last, ((0, 0), (0, D_MODEL - 9))))
    return jnp.concatenate(rows, axis=0)


def _unpack_small(packed, conv_width):
    out = {n: packed[i:i + 1] for i, n in enumerate(SMALL_VECTORS)}
    out["attn_group_norm"] = packed[4:5, :Q_WIDTH]
    out["conv_group_norm"] = packed[4:5, Q_WIDTH:]
    out["conv_w"] = packed[5:7].reshape(4, CONV_WIDTH)[:CONV_K, :conv_width].reshape(1, CONV_K, conv_width)
    out["attn_sinks"] = packed[7:8, :8]
    out["loss_sum"] = packed[7, 8]
    return out


WEIGHT_ORDER = ("pre_mix_norm", "w_in", "conv_w", "attn_sinks", "attn_group_norm", "conv_group_norm", "w_out",
                "post_mix_norm", "pre_mlp_norm", "w_up", "w_down", "post_mlp_norm")


def kernel(x, pre_mix_norm, w_in, conv_w, attn_sinks, attn_group_norm, conv_group_norm, w_out, post_mix_norm, pre_mlp_norm, w_up, w_down, post_mlp_norm, loss_target, m_pre_mix_norm, m_w_in, m_conv_w, m_attn_sinks, m_attn_group_norm, m_conv_group_norm, m_w_out, m_post_mix_norm, m_pre_mlp_norm, m_w_up, m_w_down, m_post_mlp_norm, v_pre_mix_norm, v_w_in, v_conv_w, v_attn_sinks, v_attn_group_norm, v_conv_group_norm, v_w_out, v_post_mix_norm, v_pre_mlp_norm, v_w_up, v_w_down, v_post_mlp_norm):
    w = dict(pre_mix_norm=pre_mix_norm, w_in=w_in, conv_w=conv_w, attn_sinks=attn_sinks, attn_group_norm=attn_group_norm,
             conv_group_norm=conv_group_norm, w_out=w_out, post_mix_norm=post_mix_norm, pre_mlp_norm=pre_mlp_norm,
             w_up=w_up, w_down=w_down, post_mlp_norm=post_mlp_norm)
    m = dict(pre_mix_norm=m_pre_mix_norm, w_in=m_w_in, conv_w=m_conv_w, attn_sinks=m_attn_sinks,
             attn_group_norm=m_attn_group_norm, conv_group_norm=m_conv_group_norm, w_out=m_w_out,
             post_mix_norm=m_post_mix_norm, pre_mlp_norm=m_pre_mlp_norm, w_up=m_w_up, w_down=m_w_down,
             post_mlp_norm=m_post_mlp_norm)
    v = dict(pre_mix_norm=v_pre_mix_norm, w_in=v_w_in, conv_w=v_conv_w, attn_sinks=v_attn_sinks,
             attn_group_norm=v_attn_group_norm, conv_group_norm=v_conv_group_norm, w_out=v_w_out,
             post_mix_norm=v_post_mix_norm, pre_mlp_norm=v_pre_mlp_norm, w_up=v_w_up, w_down=v_w_down,
             post_mlp_norm=v_post_mlp_norm)
    core = lax.axis_index("c").astype(jnp.int32).reshape(1)
    chip = 2 * lax.axis_index("x") + lax.axis_index("y")
    local_conv = conv_w.shape[2]
    xs, target = x[0], loss_target[0]
    rope = _rope_inputs(xs.shape[0])

    hb_up, hb_down, hb_out, hb_in = _cast_halves(core, w_up[0], w_down[0], w_out[0], w_in[0].T)
    conv_pad = jnp.pad(conv_w[0], ((0, 8 - CONV_K), (0, 0)))
    wf_in, conv_all = _gather_whole(hb_in, conv_pad)
    conv_full = conv_all[:, :CONV_K, :].transpose(1, 0, 2).reshape(CONV_K, CONV_WIDTH)

    *proj, wf_up, wf_out = _in_proj(xs, pre_mix_norm, wf_in, rope, comm=_merge(_gather_first(hb_up), _gather_first(hb_out)))
    q, kd0, kd1, vd0, vd1, gb, gc, xin, hn = proj
    attn, wf_up, wf_out, wf_down = _attention_fwd(
        q, kd0, kd1, vd0, vd1, attn_sinks,
        comm=_merge(_gather_second(wf_up), _gather_second(wf_out), _gather_first(hb_down)))
    mix, mixed, wf_down = _mix_out(attn, gb, gc, xin, conv_full, attn_group_norm, conv_group_norm, wf_out,
                                   comm=_gather_second(wf_down))
    up, hn2, dout, dmlp, loss_sum, dg_post_mlp = _mlp_loss(xs, mix, target, post_mix_norm, pre_mlp_norm, post_mlp_norm,
                                                           wf_up, wf_down)

    dup, dh, dmix, dg_pre_mlp, dg_post_mix = _mlp_bwd(dmlp, up, xs, dout, mix, pre_mlp_norm, post_mix_norm, wf_up, wf_down)
    g_down, = _wgrad("wgrad_down", up, dmlp, per_chip=True, h_rows=H_DOWN, square_a=True)
    g_up, got_down = _wgrad("wgrad_up", hn2, dup, per_chip=True, h_rows=H_UP, comm=_pair_send(g_down))
    p_down = _pair_sum("pair_sum_down", core, g_down, got_down)
    dattn, dgb, dy, dg_attn, dg_conv, dconv_w, ex_down, got_up = _mix_bwd(
        dmix, attn, gb, gc, xin, conv_full, attn_group_norm, conv_group_norm, wf_out,
        comm=_merge(_chip_exchange(p_down), _pair_send(g_up)))
    p_up = _pair_sum("pair_sum_up", core, g_up, got_up)
    g_out, = _wgrad("wgrad_out", mixed, dmix, per_chip=False, h_rows=H_OUT)
    dq, dk0, dk1, dv0, dv1, dsink, ex_up, got_out = _attention_bwd(
        q, dattn, attn, kd0, kd1, vd0, vd1, attn_sinks, comm=_merge(_chip_exchange(p_up), _pair_send(g_out)))
    p_out = _pair_sum("pair_sum_out", core, g_out, got_out)
    dproj, grad_x, dg_pre_mix = _in_proj_bwd(dq, dk0, dk1, dv0, dv1, dgb, dy, gc, xin, conv_full, xs, dh, pre_mix_norm,
                                             wf_in, rope)
    g_in, ex_out = _wgrad("wgrad_in", dproj, hn, per_chip=False, h_rows=H_IN, comm=_chip_exchange(p_out))
    small = dict(pre_mix_norm=dg_pre_mix, conv_w=dconv_w, attn_sinks=dsink[:, :8], attn_group_norm=dg_attn,
                 conv_group_norm=dg_conv, post_mix_norm=dg_post_mix, pre_mlp_norm=dg_pre_mlp, post_mlp_norm=dg_post_mlp,
                 loss_sum=loss_sum)
    r_down, r_up, r_out, r_in, small_total = _tail_reduce(g_in, [ex_down, ex_up, ex_out], _pack_small(small))

    out_g, out_d, out_m, out_v = {}, {}, {}, {}
    out_g["w_up"], out_d["w_up"], out_m["w_up"], out_v["w_up"] = _adamw_rows(
        "adamw_up", r_up, w_up[0], m_w_up[0], v_w_up[0], 256)
    out_g["w_down"], out_d["w_down"], out_m["w_down"], out_v["w_down"] = _adamw_rows(
        "adamw_down", r_down, w_down[0], m_w_down[0], v_w_down[0], 256)
    out_g["w_out"], out_d["w_out"], out_m["w_out"], out_v["w_out"] = _adamw_rows(
        "adamw_out", r_out, w_out[0], m_w_out[0], v_w_out[0], H_OUT)
    in_t = _adamw_rows("adamw_in", r_in, w_in[0].T, m_w_in[0].T, v_w_in[0].T, H_IN)
    out_g["w_in"], out_d["w_in"], out_m["w_in"], out_v["w_in"] = [t.T for t in in_t]

    small_sum = _unpack_small(small_total, CONV_WIDTH)
    loss = small_sum["loss_sum"] * (0.5 / D_MODEL)
    small_sum["conv_w"] = lax.dynamic_slice_in_dim(small_sum["conv_w"], chip * local_conv, local_conv, axis=2)
    packed = [_pack_small({n: t[n] for n in SMALL_NAMES}) for t in (w, small_sum, m, v)]
    small_d, small_m, small_v = [_unpack_small(t, local_conv) for t in _adamw_small(*packed)]
    for n in SMALL_NAMES:
        out_g[n], out_d[n], out_m[n], out_v[n] = small_sum[n], small_d[n], small_m[n], small_v[n]

    def shaped(d):
        return [d[n].reshape(w[n].shape) for n in WEIGHT_ORDER]

    return (loss, grad_x[None], *shaped(out_g), *shaped(out_d), *shaped(out_m), *shaped(out_v))
```

```python
import math
from typing import Callable, NamedTuple

import jax
import jax.numpy as jnp
import numpy as np
from jax import lax
from jax.experimental import pallas as pl
from jax.experimental.pallas import tpu as pltpu

F32 = jnp.float32
BF16 = jnp.bfloat16

D_MODEL = 1024
HEAD_DIM = 64
Q_WIDTH = 512
KV_WIDTH = 128
CONV_WIDTH = 512
CONV_K = 3
D_FF = 4096
IN_COLS = 2304
QBLOCK = 128
ROT_DIM = 16
ROPE_THETA = 500000.0
NORM_EPS = 1e-6
NEG_INF = -1e30
N_CHIPS = 4

ADAM_LR = 0.001
ADAM_B1 = 0.9
ADAM_B2 = 0.999
ADAM_EPS = 1e-08
ADAM_WD = 0.01
ADAM_STEP = 10

H_UP, H_DOWN, H_OUT, H_IN = 512, 512, 128, 288

TOKEN_TILE = 512
MLP_BWD_TOKEN_TILE = 512
MLP_BWD_SUB_TILE = 256
ATTN_FWD_BLOCKS = 4
ATTN_BWD_BLOCKS = 2
WGRAD_TOKEN_TILE = 2048
VMEM_LIMIT_V7X = 56 * 1024 * 1024

MESH = pl.DeviceIdType.MESH
ANY = pl.BlockSpec(memory_space=pl.ANY)
VMEM_WHOLE = pl.BlockSpec(memory_space=pltpu.VMEM)
SDS = jax.ShapeDtypeStruct


def _resident(shape):
    zeros = (0,) * len(shape)
    return pl.BlockSpec(shape, lambda *_: zeros, pipeline_mode=pl.Buffered(1))


def _rms(v):
    return lax.rsqrt(jnp.mean(v * v, axis=-1, keepdims=True) + NORM_EPS)


def _norm_bwd(dy, gain, vhat, rstd):
    t = dy * gain
    return rstd * (t - vhat * jnp.mean(t * vhat, axis=-1, keepdims=True))


def _colsum(v):
    return jnp.sum(v, axis=0, keepdims=True)


def _dot_nt(a, b):
    return lax.dot_general(a, b, (((1,), (1,)), ((), ())), preferred_element_type=F32)


def _dot_tn(a, b):
    return lax.dot_general(a, b, (((0,), (0,)), ((), ())), preferred_element_type=F32)


def _dot(a, b):
    return jnp.dot(a, b, preferred_element_type=F32)


def _chip_block(w_ref, chip):
    both = w_ref[pl.ds(2 * chip, 2)]
    return both.reshape(2 * both.shape[1], both.shape[2])


def _lane_lt64(shape):
    return lax.broadcasted_iota(jnp.int32, shape, 1) < HEAD_DIM


class _Comm(NamedTuple):
    operands: tuple
    out_shapes: tuple
    aliases: dict
    n_remote: int
    n_local: int
    plan: Callable


def _merge(*comms):
    operands, out_shapes, aliases, parts = [], [], {}, []
    n_remote = n_local = 0
    for cm in comms:
        parts.append((len(operands), len(out_shapes), n_remote, n_local, cm))
        for k, v in cm.aliases.items():
            aliases[len(operands) + k] = len(out_shapes) + v
        operands += cm.operands
        out_shapes += cm.out_shapes
        n_remote += cm.n_remote
        n_local += cm.n_local

    def plan(ins, outs, send, recv, loc):
        sends, recvs, locs = [], [], []
        for i0, o0, r0, l0, cm in parts:
            s, r, l = cm.plan(ins[i0:i0 + len(cm.operands)], outs[o0:o0 + len(cm.out_shapes)],
                              lambda k, r0=r0: send(r0 + k), lambda k, r0=r0: recv(r0 + k), lambda k, l0=l0: loc(l0 + k))
            sends, recvs, locs = sends + s, recvs + r, locs + l
        return sends, recvs, locs

    return _Comm(tuple(operands), tuple(out_shapes), aliases, n_remote, n_local, plan)


def _sem_scratch(comm):
    return [pltpu.SemaphoreType.DMA((max(comm.n_remote, 1),)), pltpu.SemaphoreType.DMA((max(comm.n_remote, 1),)),
            pltpu.SemaphoreType.DMA((max(comm.n_local, 1),))]


def _pallas(body, *, name, grid, in_specs, out_specs, out_shape, operands, scratch=(), comm=None):
    params = pltpu.CompilerParams(dimension_semantics=("arbitrary",) * len(grid), vmem_limit_bytes=VMEM_LIMIT_V7X)
    if comm is None:
        return pl.pallas_call(body, name=name, grid=grid, in_specs=in_specs, out_specs=out_specs, out_shape=out_shape,
                              scratch_shapes=list(scratch), compiler_params=params)(*operands)
    n_in, n_out, n_scr = len(in_specs), len(out_specs), len(scratch)
    c_in, c_out = len(comm.operands), len(comm.out_shapes)

    def with_comm(*refs):
        ins, c_ins = refs[:n_in], refs[n_in:n_in + c_in]
        o0 = n_in + c_in
        outs, c_outs = refs[o0:o0 + n_out], refs[o0 + n_out:o0 + n_out + c_out]
        s0 = o0 + n_out + c_out
        scr = refs[s0:s0 + n_scr]
        send_sems, recv_sems, local_sems = refs[s0 + n_scr:]
        first = last = None
        for axis, size in enumerate(grid):
            at_start, at_end = pl.program_id(axis) == 0, pl.program_id(axis) == size - 1
            first = at_start if first is None else jnp.logical_and(first, at_start)
            last = at_end if last is None else jnp.logical_and(last, at_end)

        def copies():
            return comm.plan(c_ins, c_outs, lambda k: send_sems.at[k], lambda k: recv_sems.at[k],
                             lambda k: local_sems.at[k])

        @pl.when(first)
        def _():
            sends, _, locs = copies()
            for cp in sends + locs:
                cp.start()

        body(*ins, *outs, *scr)

        @pl.when(last)
        def _():
            sends, recvs, locs = copies()
            for cp in recvs:
                cp.wait_recv()
            for cp in sends:
                cp.wait_send()
            for cp in locs:
                cp.wait()

    return pl.pallas_call(
        with_comm, name=name, grid=grid,
        in_specs=list(in_specs) + [ANY] * c_in, out_specs=list(out_specs) + [ANY] * c_out,
        out_shape=list(out_shape) + list(comm.out_shapes),
        scratch_shapes=list(scratch) + _sem_scratch(comm),
        input_output_aliases={n_in + k: n_out + v for k, v in comm.aliases.items()},
        compiler_params=params)(*operands, *comm.operands)


def _place():
    return lax.axis_index("x"), lax.axis_index("y"), lax.axis_index("c")


def _other_chips(x, y):
    return [(1 - x, y), (x, 1 - y), (1 - x, 1 - y)]


def _slot(px, py, pc):
    return 4 * px + 2 * py + pc


def _remote(src, dst, send_sem, recv_sem, to):
    return pltpu.make_async_remote_copy(src_ref=src, dst_ref=dst, send_sem=send_sem, recv_sem=recv_sem,
                                        device_id=to, device_id_type=MESH)


def _gather_first(half_block):
    def plan(ins, outs, send, recv, loc):
        (blk,), (full,) = ins, outs
        x, y, c = _place()
        chips = _other_chips(x, y)
        mine = full.at[_slot(x, y, c)]
        sends = [_remote(blk, mine, send(0), recv(0), (x, y, 1 - c))]
        sends += [_remote(blk, mine, send(1 + j), recv(1 + j), (*chip, c)) for j, chip in enumerate(chips)]
        recvs = [_remote(blk, full.at[_slot(x, y, 1 - c)], send(0), recv(0), (x, y, 1 - c))]
        recvs += [_remote(blk, full.at[_slot(*chip, c)], send(1 + j), recv(1 + j), (*chip, c))
                  for j, chip in enumerate(chips)]
        return sends, recvs, [pltpu.make_async_copy(blk, mine, loc(0))]

    return _Comm((half_block,), (SDS((2 * N_CHIPS,) + half_block.shape, half_block.dtype),), {}, 4, 1, plan)


def _gather_second(partly_gathered):
    def plan(ins, outs, send, recv, loc):
        (src,), (full,) = ins, outs
        x, y, c = _place()
        chips = _other_chips(x, y)
        sends = [_remote(src.at[_slot(*chip, c)], full.at[_slot(*chip, c)], send(j), recv(j), (x, y, 1 - c))
                 for j, chip in enumerate(chips)]
        recvs = [_remote(src.at[_slot(*chip, 1 - c)], full.at[_slot(*chip, 1 - c)], send(j), recv(j), (x, y, 1 - c))
                 for j, chip in enumerate(chips)]
        return sends, recvs, []

    return _Comm((partly_gathered,), (SDS(partly_gathered.shape, partly_gathered.dtype),), {0: 0}, 3, 0, plan)


def _gather_whole(half_block, small_block):
    def body(blk_ref, small_ref, out_ref, small_out_ref, send_sems, recv_sems, local_sems):
        x, y, c = _place()
        me, sibling = (x, y, c), (x, y, 1 - c)
        chips = _other_chips(x, y)

        def copy(k, block, to, src=None):
            return _remote(out_ref.at[_slot(*block)] if src is None else src, out_ref.at[_slot(*block)],
                           send_sems.at[k], recv_sems.at[k], to)

        def small_copy(k, chip, to):
            return _remote(small_ref, small_out_ref.at[2 * chip[0] + chip[1]], send_sems.at[7 + k], recv_sems.at[7 + k], to)

        mine = pltpu.make_async_copy(blk_ref, out_ref.at[_slot(*me)], local_sems.at[0])
        mine_small = pltpu.make_async_copy(small_ref, small_out_ref.at[2 * x + y], local_sems.at[1])
        mine.start()
        mine_small.start()
        first = [copy(0, me, sibling, src=blk_ref)]
        first += [copy(1 + j, me, (*chip, c), src=blk_ref) for j, chip in enumerate(chips)]
        first += [small_copy(j, (x, y), (*chip, c)) for j, chip in enumerate(chips)]
        for cp in first:
            cp.start()
        passed = [copy(4 + j, (*chip, c), sibling) for j, chip in enumerate(chips)]
        for j, chip in enumerate(chips):
            copy(1 + j, (*chip, c), me).wait_recv()
            passed[j].start()
        copy(0, sibling, me).wait_recv()
        for j, chip in enumerate(chips):
            copy(4 + j, (*chip, 1 - c), me).wait_recv()
            small_copy(j, chip, me).wait_recv()
        for cp in first + passed:
            cp.wait_send()
        mine.wait()
        mine_small.wait()

    return pl.pallas_call(
        body, name="gather_whole", in_specs=[ANY, ANY], out_specs=[ANY, ANY],
        out_shape=[SDS((2 * N_CHIPS,) + half_block.shape, half_block.dtype),
                   SDS((N_CHIPS,) + small_block.shape, small_block.dtype)],
        scratch_shapes=[pltpu.SemaphoreType.DMA((10,)), pltpu.SemaphoreType.DMA((10,)), pltpu.SemaphoreType.DMA((2,))],
    )(half_block, small_block)


def _pair_send(grads):
    def plan(ins, outs, send, recv, loc):
        (g,), (got,) = ins, outs
        x, y, c = _place()
        copies = [_remote(g.at[j, 1 - c], got.at[j], send(j), recv(j), (x, y, 1 - c)) for j in range(N_CHIPS)]
        return copies, copies, []

    shape = (grads.shape[0],) + grads.shape[2:]
    return _Comm((grads,), (SDS(shape, grads.dtype),), {}, N_CHIPS, 0, plan)


def _chip_exchange(partial):
    def plan(ins, outs, send, recv, loc):
        (p,), (got,) = ins, outs
        x, y, c = _place()
        my_chip = 2 * x + y
        chips = _other_chips(x, y)
        sends = [_remote(p.at[2 * chip[0] + chip[1]], got.at[my_chip], send(j), recv(j), (*chip, c))
                 for j, chip in enumerate(chips)]
        recvs = [_remote(p.at[my_chip], got.at[2 * chip[0] + chip[1]], send(j), recv(j), (*chip, c))
                 for j, chip in enumerate(chips)]
        return sends, recvs, [pltpu.make_async_copy(p.at[my_chip], got.at[my_chip], loc(0))]

    return _Comm((partial,), (SDS(partial.shape, partial.dtype),), {}, 3, 1, plan)


def _pair_sum(name, core, grads, received):
    h = grads.shape[2]

    def body(core_ref, g_ref, r_ref, o_ref):
        o_ref[...] = (g_ref[0] + r_ref[...]).astype(BF16)

    return pl.pallas_call(
        body, name=name,
        grid_spec=pltpu.PrefetchScalarGridSpec(
            num_scalar_prefetch=1, grid=(N_CHIPS,),
            in_specs=[pl.BlockSpec((1, 1, h, D_MODEL), lambda j, core_ref: (j, core_ref[0], 0, 0)),
                      pl.BlockSpec((1, h, D_MODEL), lambda j, core_ref: (j, 0, 0))],
            out_specs=pl.BlockSpec((1, h, D_MODEL), lambda j, core_ref: (j, 0, 0))),
        out_shape=SDS((N_CHIPS, h, D_MODEL), BF16),
        compiler_params=pltpu.CompilerParams(dimension_semantics=("arbitrary",), vmem_limit_bytes=VMEM_LIMIT_V7X),
    )(core, grads, received)


SMALL_ROWS = 8


def _sum_blocks(ref):
    return (ref[0].astype(F32) + ref[1].astype(F32)) + (ref[2].astype(F32) + ref[3].astype(F32))


def _tail_reduce(last_grads, exchanged, small):
    n = len(exchanged)
    h = last_grads.shape[2]

    def body(*refs):
        g_ref, ex, small_ref = refs[0], refs[1:1 + n], refs[1 + n]
        o0 = 2 + n
        out, out_last, small_out = refs[o0:o0 + n], refs[o0 + n], refs[o0 + n + 1]
        s0 = o0 + n + 2
        halves, half_last = refs[s0:s0 + n], refs[s0 + n]
        own, got, part, exch, small_buf = refs[s0 + n + 1:s0 + n + 6]
        pair_send, pair_recv, chip_send, chip_recv, share_send, share_recv, small_send, small_recv, local_sems = refs[s0 + n + 6:]
        x, y, c = _place()
        sibling = (x, y, 1 - c)
        my_chip, me = 2 * x + y, _slot(x, y, c)
        chips = _other_chips(x, y)

        to_sibling = [_remote(g_ref.at[j, 1 - c], got.at[j], pair_send.at[j], pair_recv.at[j], sibling)
                      for j in range(N_CHIPS)]
        load_own = [pltpu.make_async_copy(g_ref.at[j, c], own.at[j], local_sems.at[j]) for j in range(N_CHIPS)]
        for cp in to_sibling + load_own:
            cp.start()

        small_buf[me] = small_ref[...]
        small_copies = []
        for mask in range(1, 8):
            peer = (x ^ (mask >> 2), y ^ ((mask >> 1) & 1), c ^ (mask & 1))
            small_copies.append(_remote(small_ref, small_buf.at[me], small_send.at[mask - 1], small_recv.at[mask - 1], peer))
        for cp in small_copies:
            cp.start()

        def share(k, half_ref, out_ref):
            keep = pltpu.make_async_copy(half_ref, out_ref.at[c], local_sems.at[N_CHIPS + k])
            give = _remote(half_ref, out_ref.at[c], share_send.at[k], share_recv.at[k], sibling)
            take = _remote(half_ref, out_ref.at[1 - c], share_send.at[k], share_recv.at[k], sibling)
            keep.start()
            give.start()
            return keep, give, take

        shares = []
        for k in range(n):
            halves[k][...] = _sum_blocks(ex[k])
            shares.append(share(k, halves[k], out[k]))

        for cp in to_sibling:
            cp.wait_recv()
        for cp in load_own:
            cp.wait()
        part[...] = (own[...] + got[...]).astype(BF16)
        exch[my_chip] = part[my_chip]
        to_chips = [_remote(part.at[2 * chip[0] + chip[1]], exch.at[my_chip], chip_send.at[j], chip_recv.at[j], (*chip, c))
                    for j, chip in enumerate(chips)]
        from_chips = [_remote(part.at[my_chip], exch.at[2 * chip[0] + chip[1]], chip_send.at[j], chip_recv.at[j], (*chip, c))
                      for j, chip in enumerate(chips)]
        for cp in to_chips:
            cp.start()

        for cp in small_copies:
            cp.wait_recv()
        total = small_buf[0]
        for d in range(1, 8):
            total = total + small_buf[d]
        small_out[...] = total

        for cp in from_chips:
            cp.wait_recv()
        half_last[...] = _sum_blocks(exch)
        shares.append(share(n, half_last, out_last))

        for keep, give, take in shares:
            take.wait_recv()
            give.wait_send()
            keep.wait()
        for cp in to_sibling + to_chips + small_copies:
            cp.wait_send()

    blocks = (N_CHIPS, h, D_MODEL)
    return pl.pallas_call(
        body, name="tail_reduce",
        in_specs=[ANY] + [VMEM_WHOLE] * (n + 1), out_specs=[ANY] * (n + 1) + [VMEM_WHOLE],
        out_shape=[SDS((2,) + e.shape[1:], F32) for e in exchanged] + [SDS((2, h, D_MODEL), F32), SDS(small.shape, F32)],
        scratch_shapes=[pltpu.VMEM(e.shape[1:], F32) for e in exchanged] + [pltpu.VMEM((h, D_MODEL), F32)]
                       + [pltpu.VMEM(blocks, F32), pltpu.VMEM(blocks, F32), pltpu.VMEM(blocks, BF16), pltpu.VMEM(blocks, BF16),
                          pltpu.VMEM((8,) + small.shape, F32)]
                       + [pltpu.SemaphoreType.DMA((N_CHIPS,)), pltpu.SemaphoreType.DMA((N_CHIPS,)),
                          pltpu.SemaphoreType.DMA((3,)), pltpu.SemaphoreType.DMA((3,)),
                          pltpu.SemaphoreType.DMA((n + 1,)), pltpu.SemaphoreType.DMA((n + 1,)),
                          pltpu.SemaphoreType.DMA((7,)), pltpu.SemaphoreType.DMA((7,)),
                          pltpu.SemaphoreType.DMA((N_CHIPS + n + 1,))],
        compiler_params=pltpu.CompilerParams(vmem_limit_bytes=VMEM_LIMIT_V7X),
    )(last_grads, *exchanged, small)


def _rope_expansion():
    half = ROT_DIM // 2
    expand = np.zeros((2 * half, 3 * 128), np.float32)
    const = np.zeros((1, 3 * 128), np.float32)
    for lane in range(128):
        d = lane % HEAD_DIM
        if d < ROT_DIM:
            expand[d % half, lane] = 1.0
        else:
            const[0, lane] = 1.0
        if d < half:
            expand[half + d, 128 + lane] = -1.0
        elif d < ROT_DIM:
            expand[half + d - half, 256 + lane] = 1.0
    return expand, const


ROPE_PIECES = 3 * ROT_DIM


def _rope_inputs(seq):
    pos = jnp.arange(seq, dtype=F32)
    inv_freq = ROPE_THETA ** (-jnp.arange(0, ROT_DIM, 2, dtype=F32) / ROT_DIM)
    ang = pos[:, None] * inv_freq[None, :]
    cs = jnp.concatenate([jnp.cos(ang), jnp.sin(ang)], axis=1)
    hi = lax.reduce_precision(cs, 8, 7)
    mid = lax.reduce_precision(cs - hi, 8, 7)
    low = cs - hi - mid
    expand, const = _rope_expansion()
    pieces = jnp.concatenate([hi, mid, low], axis=1).astype(BF16)
    return pieces, jnp.asarray(np.concatenate([expand] * 3, axis=0), BF16), jnp.asarray(const)


def _rope_specs(tb):
    return [pl.BlockSpec((tb, ROPE_PIECES), lambda i: (i, 0)), _resident((ROPE_PIECES, 3 * 128)), _resident((1, 3 * 128))]


def _rope_tile(pieces_ref, expand_ref, const_ref):
    tables = _dot(pieces_ref[...], expand_ref[...]) + const_ref[...]
    return tables[:, 0:128], tables[:, 128:256], tables[:, 256:384]


def _rope(t, c, sa, sb):
    half = ROT_DIM // 2
    return t * c + pltpu.roll(t, 128 - half, 1) * sa + pltpu.roll(t, half, 1) * sb


def _rope_transposed(dt, c, sa, sb):
    half = ROT_DIM // 2
    return dt * c + pltpu.roll(dt * sa, half, 1) + pltpu.roll(dt * sb, 128 - half, 1)


def _cast_halves(core, w_up, w_down, w_out, w_in_t):
    def body(core_ref, up_ref, down_ref, out_ref, in_ref, up_o, down_o, out_o, in_o):
        up_o[...] = up_ref[...].astype(BF16)
        down_o[...] = down_ref[...].astype(BF16)
        out_o[...] = out_ref[...].astype(BF16)
        in_o[...] = in_ref[...].astype(BF16)

    half = lambda rows: pl.BlockSpec((rows, D_MODEL), lambda i, core_ref: (core_ref[0], 0))
    whole = lambda rows: pl.BlockSpec((rows, D_MODEL), lambda i, core_ref: (0, 0))
    rows = (H_UP, H_DOWN, H_OUT, H_IN)
    return pl.pallas_call(
        body, name="cast_halves",
        grid_spec=pltpu.PrefetchScalarGridSpec(
            num_scalar_prefetch=1, grid=(1,), in_specs=[half(r) for r in rows], out_specs=[whole(r) for r in rows]),
        out_shape=[SDS((r, D_MODEL), BF16) for r in rows],
        compiler_params=pltpu.CompilerParams(dimension_semantics=("arbitrary",), vmem_limit_bytes=VMEM_LIMIT_V7X),
    )(core, w_up, w_down, w_out, w_in_t)


def _in_proj(x, g_pre, w_in_t, rope, comm=None):
    seq = x.shape[0]
    tb = TOKEN_TILE

    def body(x_ref, g_ref, w_ref, c_ref, sa_ref, sb_ref,
             q_ref, kd0_ref, kd1_ref, vd0_ref, vd1_ref, gb_ref, gc_ref, xin_ref, hn_ref):
        xv = x_ref[...]
        hn = (xv * _rms(xv) * g_ref[...]).astype(BF16)
        hn_ref[...] = hn
        proj = _dot_nt(hn, w_ref[...].reshape(IN_COLS, D_MODEL))
        c, sa, sb = _rope_tile(c_ref, sa_ref, sb_ref)
        scale = 1.0 / math.sqrt(HEAD_DIM)
        for p in range(Q_WIDTH // 128):
            q_ref[:, 128 * p:128 * (p + 1)] = (_rope(proj[:, 128 * p:128 * (p + 1)], c, sa, sb) * scale).astype(BF16)
        k = _rope(proj[:, Q_WIDTH:Q_WIDTH + KV_WIDTH], c, sa, sb)
        v = proj[:, Q_WIDTH + KV_WIDTH:Q_WIDTH + 2 * KV_WIDTH]
        low = _lane_lt64(k.shape)
        k_sw, v_sw = pltpu.roll(k, HEAD_DIM, 1), pltpu.roll(v, HEAD_DIM, 1)
        kd0_ref[...] = jnp.where(low, k, k_sw).astype(BF16)
        kd1_ref[...] = jnp.where(low, k_sw, k).astype(BF16)
        vd0_ref[...] = jnp.where(low, v, v_sw).astype(BF16)
        vd1_ref[...] = jnp.where(low, v_sw, v).astype(BF16)
        base = Q_WIDTH + 2 * KV_WIDTH
        gb_ref[...] = proj[:, base:base + CONV_WIDTH].astype(BF16)
        gc_ref[...] = proj[:, base + CONV_WIDTH:base + 2 * CONV_WIDTH].astype(BF16)
        xin_ref[...] = proj[:, base + 2 * CONV_WIDTH:base + 3 * CONV_WIDTH].astype(BF16)

    tile = lambda w: pl.BlockSpec((tb, w), lambda i: (i, 0))
    return _pallas(
        body, name="in_proj", grid=(seq // tb,),
        in_specs=[tile(D_MODEL), _resident((1, D_MODEL)), _resident(w_in_t.shape), *_rope_specs(tb)],
        out_specs=[tile(Q_WIDTH), tile(128), tile(128), tile(128), tile(128),
                   tile(CONV_WIDTH), tile(CONV_WIDTH), tile(CONV_WIDTH), tile(D_MODEL)],
        out_shape=[SDS((seq, Q_WIDTH), BF16)] + [SDS((seq, 128), BF16)] * 4
                  + [SDS((seq, CONV_WIDTH), BF16)] * 3 + [SDS((seq, D_MODEL), BF16)],
        operands=(x, g_pre, w_in_t, *rope), comm=comm)


def _attn_valid(i):
    shape = (4 * QBLOCK, 2 * QBLOCK)
    row = lax.broadcasted_iota(jnp.int32, shape, 0)
    col = lax.broadcasted_iota(jnp.int32, shape, 1)
    qi = row & (QBLOCK - 1)
    return (col > qi) & (col <= qi + QBLOCK) & ((col >= QBLOCK) | (i > 0))


def _stack_heads(pair0, pair1):
    low = _lane_lt64(pair0.shape)
    zero = jnp.zeros_like(pair0)
    return jnp.concatenate([jnp.where(low, pair0, zero), jnp.where(low, zero, pair0),
                            jnp.where(low, pair1, zero), jnp.where(low, zero, pair1)], axis=0)


def _unstack_heads(stacked):
    low = _lane_lt64((QBLOCK, 128))
    pair0 = jnp.where(low, stacked[0:QBLOCK], stacked[QBLOCK:2 * QBLOCK])
    pair1 = jnp.where(low, stacked[2 * QBLOCK:3 * QBLOCK], stacked[3 * QBLOCK:4 * QBLOCK])
    return pair0, pair1


def _sink_column(sink_ref, kv_head):
    row = lax.broadcasted_iota(jnp.int32, (4 * QBLOCK, 1), 0)
    s = [sink_ref[0, 4 * kv_head + j] for j in range(4)]
    return jnp.where(row < QBLOCK, s[0], jnp.where(row < 2 * QBLOCK, s[1], jnp.where(row < 3 * QBLOCK, s[2], s[3])))


def _band(ref, i):
    prev = pl.multiple_of(jnp.maximum(i - 1, 0) * QBLOCK, QBLOCK)
    own = pl.multiple_of(i * QBLOCK, QBLOCK)
    return jnp.concatenate([ref[pl.ds(prev, QBLOCK), :], ref[pl.ds(own, QBLOCK), :]], axis=0), prev, own


def _softmax_with_sink(s, sink_col):
    m = jnp.maximum(jnp.max(s, axis=-1, keepdims=True), sink_col)
    p = jnp.exp(s - m)
    e_sink = jnp.exp(sink_col - m)
    inv_l = 1.0 / (jnp.sum(p, axis=-1, keepdims=True) + e_sink)
    return p, e_sink, inv_l


def _attention_fwd(q, kd0, kd1, vd0, vd1, sinks, comm=None):
    seq = q.shape[0]

    nb = ATTN_FWD_BLOCKS

    def body(sink_ref, q_ref, kd0_ref, kd1_ref, vd0_ref, vd1_ref, o_ref):
        for b in range(nb):
            i = pl.program_id(0) * nb + b
            rows = slice(QBLOCK * b, QBLOCK * (b + 1))
            valid = _attn_valid(i)
            for kv_head, (k_ref, v_ref) in enumerate(((kd0_ref, vd0_ref), (kd1_ref, vd1_ref))):
                kband, _, _ = _band(k_ref, i)
                vband, _, _ = _band(v_ref, i)
                base = 256 * kv_head
                qm = _stack_heads(q_ref[rows, base:base + 128], q_ref[rows, base + 128:base + 256])
                s = jnp.where(valid, _dot_nt(qm, kband), NEG_INF)
                p, _, inv_l = _softmax_with_sink(s, _sink_column(sink_ref, kv_head))
                o = _dot(p.astype(BF16), vband) * inv_l
                pair0, pair1 = _unstack_heads(o)
                o_ref[rows, base:base + 128] = pair0.astype(BF16)
                o_ref[rows, base + 128:base + 256] = pair1.astype(BF16)

    blk = pl.BlockSpec((nb * QBLOCK, Q_WIDTH), lambda i: (i, 0))
    full = _resident((seq, 128))
    return _pallas(
        body, name="attention_fwd", grid=(seq // (nb * QBLOCK),),
        in_specs=[pl.BlockSpec(memory_space=pltpu.SMEM), blk, full, full, full, full],
        out_specs=[blk], out_shape=[SDS((seq, Q_WIDTH), BF16)],
        operands=(sinks, q, kd0, kd1, vd0, vd1), comm=comm)


HALO = 16


def _conv_parts(gc, xin, gc_halo, xin_halo, conv_w, first):
    tb = gc.shape[0]
    u = gc.astype(F32) * xin.astype(F32)
    u_halo = jnp.where(first, 0.0, gc_halo.astype(F32) * xin_halo.astype(F32))
    ext = jnp.concatenate([u_halo, u], axis=0)
    u1 = pltpu.roll(ext, 1, 0)[HALO:HALO + tb]
    u2 = pltpu.roll(ext, 2, 0)[HALO:HALO + tb]
    y = conv_w[0:1, :] * u2 + conv_w[1:2, :] * u1 + conv_w[2:3, :] * u
    return u, u1, u2, y


def _halo_prev(tb, w):
    return pl.BlockSpec((HALO, w), lambda i: (jnp.maximum(i * (tb // HALO) - 1, 0), 0))


def _residual_mid(x, mix, g_post_mix):
    mix_f = mix.astype(F32)
    return x + mix_f * _rms(mix_f) * g_post_mix


def _mix_out(attn, gb, gc, xin, conv_w, g_attn, g_conv, w_out, comm=None):
    seq = attn.shape[0]
    tb = TOKEN_TILE

    def body(a_ref, gb_ref, gc_ref, xin_ref, gch_ref, xinh_ref, cw_ref, ga_ref, gcn_ref, w_ref, mix_ref, mixed_ref):
        first = pl.program_id(0) == 0
        _, _, _, y = _conv_parts(gc_ref[...], xin_ref[...], gch_ref[...], xinh_ref[...], cw_ref[...], first)
        conv = gb_ref[...].astype(F32) * y
        a = a_ref[...].astype(F32)
        mixed_ref[:, 0:Q_WIDTH] = (a * _rms(a) * ga_ref[...]).astype(BF16)
        mixed_ref[:, Q_WIDTH:] = (conv * _rms(conv) * gcn_ref[...]).astype(BF16)
        mix_ref[...] = _dot(mixed_ref[...], w_ref[...].reshape(D_MODEL, D_MODEL)).astype(BF16)

    tile = lambda w: pl.BlockSpec((tb, w), lambda i: (i, 0))
    return _pallas(
        body, name="mix_out", grid=(seq // tb,),
        in_specs=[tile(Q_WIDTH), tile(CONV_WIDTH), tile(CONV_WIDTH), tile(CONV_WIDTH),
                  _halo_prev(tb, CONV_WIDTH), _halo_prev(tb, CONV_WIDTH),
                  _resident((CONV_K, CONV_WIDTH)), _resident((1, Q_WIDTH)), _resident((1, CONV_WIDTH)),
                  _resident(w_out.shape)],
        out_specs=[tile(D_MODEL), tile(D_MODEL)],
        out_shape=[SDS((seq, D_MODEL), BF16), SDS((seq, D_MODEL), BF16)],
        operands=(attn, gb, gc, xin, gc, xin, conv_w, g_attn, g_conv, w_out), comm=comm)


def _mlp_loss(x, mix, target, g_post_mix, g_pre_mlp, g_post_mlp, w_up, w_down):
    seq = x.shape[0]
    tb = TOKEN_TILE

    def body(x_ref, mix_ref, t_ref, gpm_ref, g2_ref, g4_ref, wup_ref, wdown_ref,
             up_ref, hn2_ref, dout_ref, dmlp_ref, loss_ref, dg4_ref, act_ref):
        @pl.when(pl.program_id(0) == 0)
        def _():
            loss_ref[...] = jnp.zeros_like(loss_ref)
            dg4_ref[...] = jnp.zeros_like(dg4_ref)

        halves = [slice(0, tb // 2), slice(tb // 2, tb)]
        hv, hn2 = [], []
        for rows in halves:
            hv.append(_residual_mid(x_ref[rows, :], mix_ref[rows, :], gpm_ref[...]))
            hn2.append((hv[-1] * _rms(hv[-1]) * g2_ref[...]).astype(BF16))
            hn2_ref[rows, :] = hn2[-1]
        for k, rows in enumerate(halves):
            for j in range(N_CHIPS):
                up = _dot(hn2[k], _chip_block(wup_ref, j))
                up = jnp.maximum(up, 0.0)
                up_ref[rows, 1024 * j:1024 * (j + 1)] = up.astype(BF16)
                act_ref[rows, 1024 * j:1024 * (j + 1)] = (up * up).astype(BF16)
        w_down_all = wdown_ref[...].reshape(D_FF, D_MODEL)
        loss = jnp.zeros((1, 1), F32)
        dg4 = jnp.zeros((1, D_MODEL), F32)
        for k, rows in enumerate(halves):
            mlp = _dot(act_ref[rows, :], w_down_all)
            rstd = _rms(mlp)
            zhat = mlp * rstd
            diff = hv[k] + zhat * g4_ref[...] - t_ref[rows, :]
            loss = loss + jnp.sum(jnp.sum(diff * diff, axis=1, keepdims=True), axis=0, keepdims=True)
            dout = diff * (1.0 / D_MODEL)
            dout_ref[rows, :] = dout
            dg4 = dg4 + _colsum(dout * zhat)
            dmlp_ref[rows, :] = _norm_bwd(dout, g4_ref[...], zhat, rstd).astype(BF16)
        loss_ref[...] += loss
        dg4_ref[...] += dg4

    tile = lambda w: pl.BlockSpec((tb, w), lambda i: (i, 0))
    return _pallas(
        body, name="mlp_loss", grid=(seq // tb,),
        in_specs=[tile(D_MODEL), tile(D_MODEL), tile(D_MODEL), _resident((1, D_MODEL)), _resident((1, D_MODEL)),
                  _resident((1, D_MODEL)), _resident(w_up.shape), _resident(w_down.shape)],
        out_specs=[tile(D_FF), tile(D_MODEL), tile(D_MODEL), tile(D_MODEL),
                   pl.BlockSpec((1, 1), lambda i: (0, 0)), pl.BlockSpec((1, D_MODEL), lambda i: (0, 0))],
        out_shape=[SDS((seq, D_FF), BF16), SDS((seq, D_MODEL), BF16), SDS((seq, D_MODEL), F32),
                   SDS((seq, D_MODEL), BF16), SDS((1, 1), F32), SDS((1, D_MODEL), F32)],
        scratch=[pltpu.VMEM((tb, D_FF), BF16)],
        operands=(x, mix, target, g_post_mix, g_pre_mlp, g_post_mlp, w_up, w_down))


def _mlp_bwd(dmlp, up, x, dout, mix, g_pre_mlp, g_post_mix, w_up, w_down):
    seq = x.shape[0]
    tb = MLP_BWD_TOKEN_TILE

    def body(dmlp_ref, up_ref, x_ref, dout_ref, mix_ref, g2_ref, gpm_ref, wup_ref, wdown_ref,
             dup_ref, dh_ref, dmix_ref, dg2_ref, dgpm_ref):
        @pl.when(pl.program_id(0) == 0)
        def _():
            dg2_ref[...] = jnp.zeros_like(dg2_ref)
            dgpm_ref[...] = jnp.zeros_like(dgpm_ref)

        subs = [slice(k * MLP_BWD_SUB_TILE, (k + 1) * MLP_BWD_SUB_TILE) for k in range(tb // MLP_BWD_SUB_TILE)]
        dhn2 = []
        for rows in subs:
            dmlp_v = dmlp_ref[rows, :]
            acc = None
            for j in range(N_CHIPS):
                cols = slice(1024 * j, 1024 * (j + 1))
                dact = _dot_nt(dmlp_v, _chip_block(wdown_ref, j))
                dup = (dact * (2.0 * up_ref[rows, cols].astype(F32))).astype(BF16)
                dup_ref[rows, cols] = dup
                part = _dot_nt(dup, _chip_block(wup_ref, j))
                acc = part if acc is None else acc + part
            dhn2.append(acc)
        dg2 = jnp.zeros((1, D_MODEL), F32)
        dgpm = jnp.zeros((1, D_MODEL), F32)
        for k, rows in enumerate(subs):
            mix_v = mix_ref[rows, :].astype(F32)
            hv = _residual_mid(x_ref[rows, :], mix_ref[rows, :], gpm_ref[...])
            r2 = _rms(hv)
            hhat = hv * r2
            dg2 = dg2 + _colsum(dhn2[k] * hhat)
            dh = dout_ref[rows, :] + _norm_bwd(dhn2[k], g2_ref[...], hhat, r2)
            dh_ref[rows, :] = dh.astype(BF16)
            rz = _rms(mix_v)
            zhat = mix_v * rz
            dgpm = dgpm + _colsum(dh * zhat)
            dmix_ref[rows, :] = _norm_bwd(dh, gpm_ref[...], zhat, rz).astype(BF16)
        dg2_ref[...] += dg2
        dgpm_ref[...] += dgpm

    tile = lambda w: pl.BlockSpec((tb, w), lambda i: (i, 0))
    vec = pl.BlockSpec((1, D_MODEL), lambda i: (0, 0))
    return _pallas(
        body, name="mlp_bwd", grid=(seq // tb,),
        in_specs=[tile(D_MODEL), tile(D_FF), tile(D_MODEL), tile(D_MODEL), tile(D_MODEL),
                  _resident((1, D_MODEL)), _resident((1, D_MODEL)), _resident(w_up.shape), _resident(w_down.shape)],
        out_specs=[tile(D_FF), tile(D_MODEL), tile(D_MODEL), vec, vec],
        out_shape=[SDS((seq, D_FF), BF16), SDS((seq, D_MODEL), BF16), SDS((seq, D_MODEL), BF16),
                   SDS((1, D_MODEL), F32), SDS((1, D_MODEL), F32)],
        operands=(dmlp, up, x, dout, mix, g_pre_mlp, g_post_mix, w_up, w_down))


def _mix_bwd(dmix, attn, gb, gc, xin, conv_w, g_attn, g_conv, w_out, comm=None):
    seq = attn.shape[0]
    tb = TOKEN_TILE

    def body(dmix_ref, a_ref, gb_ref, gc_ref, xin_ref, gch_ref, xinh_ref, cw_ref, ga_ref, gcn_ref, w_ref,
             dattn_ref, dgb_ref, dy_ref, dga_ref, dgcn_ref, dcw_ref):
        first = pl.program_id(0) == 0

        @pl.when(first)
        def _():
            dga_ref[...] = jnp.zeros_like(dga_ref)
            dgcn_ref[...] = jnp.zeros_like(dgcn_ref)
            dcw_ref[...] = jnp.zeros_like(dcw_ref)

        dmixed = _dot_nt(dmix_ref[...], w_ref[...].reshape(D_MODEL, D_MODEL))
        a = a_ref[...].astype(F32)
        ra = _rms(a)
        ahat = a * ra
        dan = dmixed[:, 0:Q_WIDTH]
        dga_ref[...] += _colsum(dan * ahat)
        dattn_ref[...] = _norm_bwd(dan, ga_ref[...], ahat, ra).astype(BF16)
        gbv = gb_ref[...].astype(F32)
        u, u1, u2, y = _conv_parts(gc_ref[...], xin_ref[...], gch_ref[...], xinh_ref[...], cw_ref[...], first)
        conv = gbv * y
        rc = _rms(conv)
        chat = conv * rc
        dcn = dmixed[:, Q_WIDTH:]
        dgcn_ref[...] += _colsum(dcn * chat)
        dconv = _norm_bwd(dcn, gcn_ref[...], chat, rc)
        dgb_ref[...] = (dconv * y).astype(BF16)
        dy = dconv * gbv
        dy_ref[...] = dy.astype(BF16)
        dcw_ref[0:1, :] += _colsum(dy * u2)
        dcw_ref[1:2, :] += _colsum(dy * u1)
        dcw_ref[2:3, :] += _colsum(dy * u)

    tile = lambda w: pl.BlockSpec((tb, w), lambda i: (i, 0))
    return _pallas(
        body, name="mix_bwd", grid=(seq // tb,),
        in_specs=[tile(D_MODEL), tile(Q_WIDTH), tile(CONV_WIDTH), tile(CONV_WIDTH), tile(CONV_WIDTH),
                  _halo_prev(tb, CONV_WIDTH), _halo_prev(tb, CONV_WIDTH),
                  _resident((CONV_K, CONV_WIDTH)), _resident((1, Q_WIDTH)), _resident((1, CONV_WIDTH)),
                  _resident(w_out.shape)],
        out_specs=[tile(Q_WIDTH), tile(CONV_WIDTH), tile(CONV_WIDTH),
                   pl.BlockSpec((1, Q_WIDTH), lambda i: (0, 0)), pl.BlockSpec((1, CONV_WIDTH), lambda i: (0, 0)),
                   pl.BlockSpec((CONV_K, CONV_WIDTH), lambda i: (0, 0))],
        out_shape=[SDS((seq, Q_WIDTH), BF16), SDS((seq, CONV_WIDTH), BF16), SDS((seq, CONV_WIDTH), BF16),
                   SDS((1, Q_WIDTH), F32), SDS((1, CONV_WIDTH), F32), SDS((CONV_K, CONV_WIDTH), F32)],
        operands=(dmix, attn, gb, gc, xin, gc, xin, conv_w, g_attn, g_conv, w_out), comm=comm)


def _attention_bwd(q, dattn, attn, kd0, kd1, vd0, vd1, sinks, comm=None):
    seq = q.shape[0]
    nb = ATTN_BWD_BLOCKS

    def body(sink_ref, q_ref, do_ref, o_ref, kd0_ref, kd1_ref, vd0_ref, vd1_ref,
             dq_ref, dk0_ref, dk1_ref, dv0_ref, dv1_ref, dsink_ref):
        @pl.when(pl.program_id(0) == 0)
        def _():
            for r in (dk0_ref, dk1_ref, dv0_ref, dv1_ref, dsink_ref):
                r[...] = jnp.zeros_like(r)

        lane = lax.broadcasted_iota(jnp.int32, (1, 128), 1)
        dsink = jnp.zeros((1, 128), F32)
        for b in range(nb):
            i = pl.program_id(0) * nb + b
            rows = slice(QBLOCK * b, QBLOCK * (b + 1))
            valid = _attn_valid(i)
            for kv_head, (k_ref, v_ref, dk_ref, dv_ref) in enumerate(
                    ((kd0_ref, vd0_ref, dk0_ref, dv0_ref), (kd1_ref, vd1_ref, dk1_ref, dv1_ref))):
                kband, prev, own = _band(k_ref, i)
                vband, _, _ = _band(v_ref, i)
                base = 256 * kv_head
                qm = _stack_heads(q_ref[rows, base:base + 128], q_ref[rows, base + 128:base + 256])
                dom = _stack_heads(do_ref[rows, base:base + 128], do_ref[rows, base + 128:base + 256])
                om = _stack_heads(o_ref[rows, base:base + 128], o_ref[rows, base + 128:base + 256])
                s = jnp.where(valid, _dot_nt(qm, kband), NEG_INF)
                p, e_sink, inv_l = _softmax_with_sink(s, _sink_column(sink_ref, kv_head))
                p = p * inv_l
                delta = jnp.sum(dom.astype(F32) * om.astype(F32), axis=-1, keepdims=True)
                ds = (p * (_dot_nt(dom, vband) - delta)).astype(BF16)
                sink_term = -(e_sink * inv_l) * delta
                for j in range(4):
                    part = jnp.sum(sink_term[QBLOCK * j:QBLOCK * (j + 1)], axis=0, keepdims=True)
                    dsink = dsink + jnp.where(lane == 4 * kv_head + j, part, 0.0)
                pair0, pair1 = _unstack_heads(_dot(ds, kband))
                dq_ref[rows, base:base + 128] = pair0.astype(BF16)
                dq_ref[rows, base + 128:base + 256] = pair1.astype(BF16)
                dkd = _dot_tn(ds, qm)
                dkd = dkd + pltpu.roll(dkd, HEAD_DIM, 1)
                dvd = _dot_tn(p.astype(BF16), dom)
                dvd = dvd + pltpu.roll(dvd, HEAD_DIM, 1)
                dk_ref[pl.ds(prev, QBLOCK), :] += dkd[0:QBLOCK]
                dk_ref[pl.ds(own, QBLOCK), :] += dkd[QBLOCK:]
                dv_ref[pl.ds(prev, QBLOCK), :] += dvd[0:QBLOCK]
                dv_ref[pl.ds(own, QBLOCK), :] += dvd[QBLOCK:]
        dsink_ref[...] += dsink

    blk = pl.BlockSpec((nb * QBLOCK, Q_WIDTH), lambda i: (i, 0))
    full = _resident((seq, 128))
    acc = pl.BlockSpec((seq, 128), lambda i: (0, 0))
    return _pallas(
        body, name="attention_bwd", grid=(seq // (nb * QBLOCK),),
        in_specs=[pl.BlockSpec(memory_space=pltpu.SMEM), blk, blk, blk, full, full, full, full],
        out_specs=[blk, acc, acc, acc, acc, pl.BlockSpec((1, 128), lambda i: (0, 0))],
        out_shape=[SDS((seq, Q_WIDTH), BF16)] + [SDS((seq, 128), F32)] * 4 + [SDS((1, 128), F32)],
        operands=(sinks, q, dattn, attn, kd0, kd1, vd0, vd1), comm=comm)


def _in_proj_bwd(dq, dk0, dk1, dv0, dv1, dgb, dy, gc, xin, conv_w, x, dh, g_pre, w_in_t, rope):
    seq = x.shape[0]
    tb = TOKEN_TILE
    n_tiles = seq // tb

    def body(dq_ref, dk0_ref, dk1_ref, dv0_ref, dv1_ref, dgb_ref, dy_ref, dyh_ref, gc_ref, xin_ref, cw_ref,
             x_ref, dh_ref, g_ref, w_ref, c_ref, sa_ref, sb_ref,
             dproj_ref, gx_ref, dg_ref):
        i = pl.program_id(0)

        @pl.when(i == 0)
        def _():
            dg_ref[...] = jnp.zeros_like(dg_ref)

        dy = dy_ref[...].astype(F32)
        ext = jnp.concatenate([dy, jnp.where(i == n_tiles - 1, 0.0, dyh_ref[...].astype(F32))], axis=0)
        dy1 = pltpu.roll(ext, tb + HALO - 1, 0)[0:tb]
        dy2 = pltpu.roll(ext, tb + HALO - 2, 0)[0:tb]
        cw = cw_ref[...]
        du = cw[2:3, :] * dy + cw[1:2, :] * dy1 + cw[0:1, :] * dy2
        scale = 1.0 / math.sqrt(HEAD_DIM)
        base = Q_WIDTH + 2 * KV_WIDTH
        halves = [slice(0, tb // 2), slice(tb // 2, tb)]
        low = _lane_lt64((tb // 2, 128))
        for rows in halves:
            c, sa, sb = _rope_tile(c_ref.at[rows, :], sa_ref, sb_ref)
            for p in range(Q_WIDTH // 128):
                dproj_ref[rows, 128 * p:128 * (p + 1)] = _rope_transposed(
                    dq_ref[rows, 128 * p:128 * (p + 1)].astype(F32) * scale, c, sa, sb).astype(BF16)
            dk = jnp.where(low, dk0_ref[rows, :], dk1_ref[rows, :])
            dproj_ref[rows, Q_WIDTH:Q_WIDTH + KV_WIDTH] = _rope_transposed(dk, c, sa, sb).astype(BF16)
            dproj_ref[rows, Q_WIDTH + KV_WIDTH:base] = jnp.where(low, dv0_ref[rows, :], dv1_ref[rows, :]).astype(BF16)
            dproj_ref[rows, base:base + CONV_WIDTH] = dgb_ref[rows, :]
            dproj_ref[rows, base + CONV_WIDTH:base + 2 * CONV_WIDTH] = (du[rows] * xin_ref[rows, :].astype(F32)).astype(BF16)
            dproj_ref[rows, base + 2 * CONV_WIDTH:] = (du[rows] * gc_ref[rows, :].astype(F32)).astype(BF16)
        w_all = w_ref[...].reshape(IN_COLS, D_MODEL)
        dhn = [_dot(dproj_ref[rows, :], w_all) for rows in halves]
        dg = jnp.zeros((1, D_MODEL), F32)
        for k, rows in enumerate(halves):
            xv = x_ref[rows, :]
            r = _rms(xv)
            xhat = xv * r
            dg = dg + _colsum(dhn[k] * xhat)
            gx_ref[rows, :] = dh_ref[rows, :].astype(F32) + _norm_bwd(dhn[k], g_ref[...], xhat, r)
        dg_ref[...] += dg

    tile = lambda w: pl.BlockSpec((tb, w), lambda i: (i, 0))
    halo_next = pl.BlockSpec((HALO, CONV_WIDTH), lambda i: (jnp.minimum((i + 1) * (tb // HALO), seq // HALO - 1), 0))
    return _pallas(
        body, name="in_proj_bwd", grid=(n_tiles,),
        in_specs=[tile(Q_WIDTH), tile(128), tile(128), tile(128), tile(128), tile(CONV_WIDTH), tile(CONV_WIDTH), halo_next,
                  tile(CONV_WIDTH), tile(CONV_WIDTH), _resident((CONV_K, CONV_WIDTH)),
                  tile(D_MODEL), tile(D_MODEL), _resident((1, D_MODEL)), _resident(w_in_t.shape), *_rope_specs(tb)],
        out_specs=[tile(IN_COLS), tile(D_MODEL), pl.BlockSpec((1, D_MODEL), lambda i: (0, 0))],
        out_shape=[SDS((seq, IN_COLS), BF16), SDS((seq, D_MODEL), F32), SDS((1, D_MODEL), F32)],
        operands=(dq, dk0, dk1, dv0, dv1, dgb, dy, dy, gc, xin, conv_w, x, dh, g_pre, w_in_t, *rope))


def _wgrad(name, a, b, *, per_chip, h_rows, square_a=False, comm=None):
    seq = a.shape[0]
    chips_per_step = 1 if per_chip else N_CHIPS
    m = chips_per_step * 2 * h_rows
    bt = min(seq, WGRAD_TOKEN_TILE if m <= 1024 else WGRAD_TOKEN_TILE // 2)
    n_k = seq // bt
    a_cols = m if per_chip else a.shape[1]
    a_wide = a.shape[1] > a_cols
    b_wide = b.shape[1] > D_MODEL

    def body(a_ref, b_ref, g_ref, acc_ref):
        k = pl.program_id(1)

        @pl.when(k == 0)
        def _():
            acc_ref[...] = jnp.zeros_like(acc_ref)

        av = a_ref[...]
        if square_a:
            av = (av.astype(F32) * av.astype(F32)).astype(BF16)
        acc_ref[...] += _dot_tn(av, b_ref[...])

        @pl.when(k == n_k - 1)
        def _():
            for cidx in range(chips_per_step):
                for half in range(2):
                    r0 = (2 * cidx + half) * h_rows
                    g_ref[cidx, half] = acc_ref[r0:r0 + h_rows, :]

    a_spec = pl.BlockSpec((bt, a_cols), (lambda j, k: (k, j)) if a_wide else (lambda j, k: (k, 0)))
    b_spec = pl.BlockSpec((bt, D_MODEL), (lambda j, k: (k, j)) if b_wide else (lambda j, k: (k, 0)))
    g_spec = pl.BlockSpec((chips_per_step, 2, h_rows, D_MODEL), lambda j, k: (j, 0, 0, 0))
    return _pallas(
        body, name=name, grid=(N_CHIPS if per_chip else 1, n_k),
        in_specs=[a_spec, b_spec], out_specs=[g_spec], out_shape=[SDS((N_CHIPS, 2, h_rows, D_MODEL), F32)],
        scratch=[pltpu.VMEM((m, D_MODEL), F32)], operands=(a, b), comm=comm)


def _adamw_math(w, g, m, v):
    m = ADAM_B1 * m + (1.0 - ADAM_B1) * g
    v = ADAM_B2 * v + (1.0 - ADAM_B2) * (g * g)
    m_hat = m / (1.0 - ADAM_B1 ** ADAM_STEP)
    v_hat = v / (1.0 - ADAM_B2 ** ADAM_STEP)
    delta = -ADAM_LR * (m_hat / (jnp.sqrt(v_hat) + ADAM_EPS) + ADAM_WD * w)
    return delta, m, v


def _adamw_rows(name, reduced, w, m, v, rt):
    per_half = reduced.shape[1] // rt

    def body(r_ref, w_ref, m_ref, v_ref, g_out, d_out, m_out, v_out):
        g = r_ref[0]
        g_out[...] = g
        d_out[...], m_out[...], v_out[...] = _adamw_math(w_ref[...], g, m_ref[...], v_ref[...])

    blk = pl.BlockSpec((rt, D_MODEL), lambda h, r: (h * per_half + r, 0))
    return _pallas(
        body, name=name, grid=(2, per_half),
        in_specs=[pl.BlockSpec((1, rt, D_MODEL), lambda h, r: (h, r, 0)), blk, blk, blk],
        out_specs=[blk, blk, blk, blk], out_shape=[SDS(w.shape, F32)] * 4, operands=(reduced, w, m, v))


def _adamw_small(w, g, m, v):
    def body(w_ref, g_ref, m_ref, v_ref, d_out, m_out, v_out):
        d_out[...], m_out[...], v_out[...] = _adamw_math(w_ref[...], g_ref[...], m_ref[...], v_ref[...])

    return pl.pallas_call(body, name="adamw_small", in_specs=[VMEM_WHOLE] * 4, out_specs=[VMEM_WHOLE] * 3,
                          out_shape=[SDS(w.shape, F32)] * 3)(w, g, m, v)


SMALL_VECTORS = ("pre_mix_norm", "post_mix_norm", "pre_mlp_norm", "post_mlp_norm")
SMALL_NAMES = SMALL_VECTORS + ("attn_group_norm", "conv_group_norm", "conv_w", "attn_sinks")


def _pack_small(p):
    rows = [p[n].reshape(1, D_MODEL) for n in SMALL_VECTORS]
    rows.append(jnp.concatenate([p["attn_group_norm"].reshape(1, -1), p["conv_group_norm"].reshape(1, -1)], axis=1))
    cw = p["conv_w"].reshape(CONV_K, -1)
    rows.append(jnp.pad(cw, ((0, 1), (0, CONV_WIDTH - cw.shape[1]))).reshape(2, D_MODEL))
    last = jnp.concatenate([p["attn_sinks"].reshape(1, 8), p.get("loss_sum", jnp.zeros((1, 1), F32))], axis=1)
    rows.append(jnp.pad(last, ((0, 0), (0, D_MODEL - 9))))
    return jnp.concatenate(rows, axis=0)


def _unpack_small(packed, conv_width):
    out = {n: packed[i:i + 1] for i, n in enumerate(SMALL_VECTORS)}
    out["attn_group_norm"] = packed[4:5, :Q_WIDTH]
    out["conv_group_norm"] = packed[4:5, Q_WIDTH:]
    out["conv_w"] = packed[5:7].reshape(4, CONV_WIDTH)[:CONV_K, :conv_width].reshape(1, CONV_K, conv_width)
    out["attn_sinks"] = packed[7:8, :8]
    out["loss_sum"] = packed[7, 8]
    return out


WEIGHT_ORDER = ("pre_mix_norm", "w_in", "conv_w", "attn_sinks", "attn_group_norm", "conv_group_norm", "w_out",
                "post_mix_norm", "pre_mlp_norm", "w_up", "w_down", "post_mlp_norm")


def kernel(x, pre_mix_norm, w_in, conv_w, attn_sinks, attn_group_norm, conv_group_norm, w_out, post_mix_norm, pre_mlp_norm, w_up, w_down, post_mlp_norm, loss_target, m_pre_mix_norm, m_w_in, m_conv_w, m_attn_sinks, m_attn_group_norm, m_conv_group_norm, m_w_out, m_post_mix_norm, m_pre_mlp_norm, m_w_up, m_w_down, m_post_mlp_norm, v_pre_mix_norm, v_w_in, v_conv_w, v_attn_sinks, v_attn_group_norm, v_conv_group_norm, v_w_out, v_post_mix_norm, v_pre_mlp_norm, v_w_up, v_w_down, v_post_mlp_norm):
    w = dict(pre_mix_norm=pre_mix_norm, w_in=w_in, conv_w=conv_w, attn_sinks=attn_sinks, attn_group_norm=attn_group_norm,
             conv_group_norm=conv_group_norm, w_out=w_out, post_mix_norm=post_mix_norm, pre_mlp_norm=pre_mlp_norm,
             w_up=w_up, w_down=w_down, post_mlp_norm=post_mlp_norm)
    m = dict(pre_mix_norm=m_pre_mix_norm, w_in=m_w_in, conv_w=m_conv_w, attn_sinks=m_attn_sinks,
             attn_group_norm=m_attn_group_norm, conv_group_norm=m_conv_group_norm, w_out=m_w_out,
             post_mix_norm=m_post_mix_norm, pre_mlp_norm=m_pre_mlp_norm, w_up=m_w_up, w_down=m_w_down,
             post_mlp_norm=m_post_mlp_norm)
    v = dict(pre_mix_norm=v_pre_mix_norm, w_in=v_w_in, conv_w=v_conv_w, attn_sinks=v_attn_sinks,
             attn_group_norm=v_attn_group_norm, conv_group_norm=v_conv_group_norm, w_out=v_w_out,
             post_mix_norm=v_post_mix_norm, pre_mlp_norm=v_pre_mlp_norm, w_up=v_w_up, w_down=v_w_down,
             post_mlp_norm=v_post_mlp_norm)
    core = lax.axis_index("c").astype(jnp.int32).reshape(1)
    chip = 2 * lax.axis_index("x") + lax.axis_index("y")
    local_conv = conv_w.shape[2]
    xs, target = x[0], loss_target[0]
    rope = _rope_inputs(xs.shape[0])

    hb_up, hb_down, hb_out, hb_in = _cast_halves(core, w_up[0], w_down[0], w_out[0], w_in[0].T)
    conv_pad = jnp.pad(conv_w[0], ((0, 8 - CONV_K), (0, 0)))
    wf_in, conv_all = _gather_whole(hb_in, conv_pad)
    conv_full = conv_all[:, :CONV_K, :].transpose(1, 0, 2).reshape(CONV_K, CONV_WIDTH)

    *proj, wf_up, wf_out = _in_proj(xs, pre_mix_norm, wf_in, rope, comm=_merge(_gather_first(hb_up), _gather_first(hb_out)))
    q, kd0, kd1, vd0, vd1, gb, gc, xin, hn = proj
    attn, wf_up, wf_out, wf_down = _attention_fwd(
        q, kd0, kd1, vd0, vd1, attn_sinks,
        comm=_merge(_gather_second(wf_up), _gather_second(wf_out), _gather_first(hb_down)))
    mix, mixed, wf_down = _mix_out(attn, gb, gc, xin, conv_full, attn_group_norm, conv_group_norm, wf_out,
                                   comm=_gather_second(wf_down))
    up, hn2, dout, dmlp, loss_sum, dg_post_mlp = _mlp_loss(xs, mix, target, post_mix_norm, pre_mlp_norm, post_mlp_norm,
                                                           wf_up, wf_down)

    dup, dh, dmix, dg_pre_mlp, dg_post_mix = _mlp_bwd(dmlp, up, xs, dout, mix, pre_mlp_norm, post_mix_norm, wf_up, wf_down)
    g_down, = _wgrad("wgrad_down", up, dmlp, per_chip=True, h_rows=H_DOWN, square_a=True)
    g_up, got_down = _wgrad("wgrad_up", hn2, dup, per_chip=True, h_rows=H_UP, comm=_pair_send(g_down))
    p_down = _pair_sum("pair_sum_down", core, g_down, got_down)
    dattn, dgb, dy, dg_attn, dg_conv, dconv_w, ex_down, got_up = _mix_bwd(
        dmix, attn, gb, gc, xin, conv_full, attn_group_norm, conv_group_norm, wf_out,
        comm=_merge(_chip_exchange(p_down), _pair_send(g_up)))
    p_up = _pair_sum("pair_sum_up", core, g_up, got_up)
    g_out, = _wgrad("wgrad_out", mixed, dmix, per_chip=False, h_rows=H_OUT)
    dq, dk0, dk1, dv0, dv1, dsink, ex_up, got_out = _attention_bwd(
        q, dattn, attn, kd0, kd1, vd0, vd1, attn_sinks, comm=_merge(_chip_exchange(p_up), _pair_send(g_out)))
    p_out = _pair_sum("pair_sum_out", core, g_out, got_out)
    dproj, grad_x, dg_pre_mix = _in_proj_bwd(dq, dk0, dk1, dv0, dv1, dgb, dy, gc, xin, conv_full, xs, dh, pre_mix_norm,
                                             wf_in, rope)
    g_in, ex_out = _wgrad("wgrad_in", dproj, hn, per_chip=False, h_rows=H_IN, comm=_chip_exchange(p_out))
    small = dict(pre_mix_norm=dg_pre_mix, conv_w=dconv_w, attn_sinks=dsink[:, :8], attn_group_norm=dg_attn,
                 conv_group_norm=dg_conv, post_mix_norm=dg_post_mix, pre_mlp_norm=dg_pre_mlp, post_mlp_norm=dg_post_mlp,
                 loss_sum=loss_sum)
    r_down, r_up, r_out, r_in, small_total = _tail_reduce(g_in, [ex_down, ex_up, ex_out], _pack_small(small))

    out_g, out_d, out_m, out_v = {}, {}, {}, {}
    out_g["w_up"], out_d["w_up"], out_m["w_up"], out_v["w_up"] = _adamw_rows(
        "adamw_up", r_up, w_up[0], m_w_up[0], v_w_up[0], 256)
    out_g["w_down"], out_d["w_down"], out_m["w_down"], out_v["w_down"] = _adamw_rows(
        "adamw_down", r_down, w_down[0], m_w_down[0], v_w_down[0], 256)
    out_g["w_out"], out_d["w_out"], out_m["w_out"], out_v["w_out"] = _adamw_rows(
        "adamw_out", r_out, w_out[0], m_w_out[0], v_w_out[0], H_OUT)
    in_t = _adamw_rows("adamw_in", r_in, w_in[0].T, m_w_in[0].T, v_w_in[0].T, H_IN)
    out_g["w_in"], out_d["w_in"], out_m["w_in"], out_v["w_in"] = [t.T for t in in_t]

    small_sum = _unpack_small(small_total, CONV_WIDTH)
    loss = small_sum["loss_sum"] * (0.5 / D_MODEL)
    small_sum["conv_w"] = lax.dynamic_slice_in_dim(small_sum["conv_w"], chip * local_conv, local_conv, axis=2)
    packed = [_pack_small({n: t[n] for n in SMALL_NAMES}) for t in (w, small_sum, m, v)]
    small_d, small_m, small_v = [_unpack_small(t, local_conv) for t in _adamw_small(*packed)]
    for n in SMALL_NAMES:
        out_g[n], out_d[n], out_m[n], out_v[n] = small_sum[n], small_d[n], small_m[n], small_v[n]

    def shaped(d):
        return [d[n].reshape(w[n].shape) for n in WEIGHT_ORDER]

    return (loss, grad_x[None], *shaped(out_g), *shaped(out_d), *shaped(out_m), *shaped(out_v))
```

```python
import math
from typing import Callable, NamedTuple

import jax
import jax.numpy as jnp
import numpy as np
from jax import lax
from jax.experimental import pallas as pl
from jax.experimental.pallas import tpu as pltpu

F32 = jnp.float32
BF16 = jnp.bfloat16

D_MODEL = 1024
HEAD_DIM = 64
Q_WIDTH = 512
KV_WIDTH = 128
CONV_WIDTH = 512
CONV_K = 3
D_FF = 4096
IN_COLS = 2304
QBLOCK = 128
ROT_DIM = 16
ROPE_THETA = 500000.0
NORM_EPS = 1e-6
NEG_INF = -1e30
N_CHIPS = 4

ADAM_LR = 0.001
ADAM_B1 = 0.9
ADAM_B2 = 0.999
ADAM_EPS = 1e-08
ADAM_WD = 0.01
ADAM_STEP = 10

H_UP, H_DOWN, H_OUT, H_IN = 512, 512, 128, 288

TOKEN_TILE = 512
MLP_BWD_TOKEN_TILE = 512
MLP_BWD_SUB_TILE = 256
ATTN_FWD_BLOCKS = 4
ATTN_BWD_BLOCKS = 2
WGRAD_TOKEN_TILE = 2048
VMEM_LIMIT_V7X = 56 * 1024 * 1024

MESH = pl.DeviceIdType.MESH
ANY = pl.BlockSpec(memory_space=pl.ANY)
VMEM_WHOLE = pl.BlockSpec(memory_space=pltpu.VMEM)
SDS = jax.ShapeDtypeStruct


def _resident(shape):
    zeros = (0,) * len(shape)
    return pl.BlockSpec(shape, lambda *_: zeros, pipeline_mode=pl.Buffered(1))


def _rms(v):
    return lax.rsqrt(jnp.mean(v * v, axis=-1, keepdims=True) + NORM_EPS)


def _norm_bwd(dy, gain, vhat, rstd):
    t = dy * gain
    return rstd * (t - vhat * jnp.mean(t * vhat, axis=-1, keepdims=True))


def _colsum(v):
    return jnp.sum(v, axis=0, keepdims=True)


def _dot_nt(a, b):
    return lax.dot_general(a, b, (((1,), (1,)), ((), ())), preferred_element_type=F32)


def _dot_tn(a, b):
    return lax.dot_general(a, b, (((0,), (0,)), ((), ())), preferred_element_type=F32)


def _dot(a, b):
    return jnp.dot(a, b, preferred_element_type=F32)


def _chip_block(w_ref, chip):
    both = w_ref[pl.ds(2 * chip, 2)]
    return both.reshape(2 * both.shape[1], both.shape[2])


def _lane_lt64(shape):
    return lax.broadcasted_iota(jnp.int32, shape, 1) < HEAD_DIM


class _Comm(NamedTuple):
    operands: tuple
    out_shapes: tuple
    aliases: dict
    n_remote: int
    n_local: int
    plan: Callable


def _merge(*comms):
    operands, out_shapes, aliases, parts = [], [], {}, []
    n_remote = n_local = 0
    for cm in comms:
        parts.append((len(operands), len(out_shapes), n_remote, n_local, cm))
        for k, v in cm.aliases.items():
            aliases[len(operands) + k] = len(out_shapes) + v
        operands += cm.operands
        out_shapes += cm.out_shapes
        n_remote += cm.n_remote
        n_local += cm.n_local

    def plan(ins, outs, send, recv, loc):
        sends, recvs, locs = [], [], []
        for i0, o0, r0, l0, cm in parts:
            s, r, l = cm.plan(ins[i0:i0 + len(cm.operands)], outs[o0:o0 + len(cm.out_shapes)],
                              lambda k, r0=r0: send(r0 + k), lambda k, r0=r0: recv(r0 + k), lambda k, l0=l0: loc(l0 + k))
            sends, recvs, locs = sends + s, recvs + r, locs + l
        return sends, recvs, locs

    return _Comm(tuple(operands), tuple(out_shapes), aliases, n_remote, n_local, plan)


def _sem_scratch(comm):
    return [pltpu.SemaphoreType.DMA((max(comm.n_remote, 1),)), pltpu.SemaphoreType.DMA((max(comm.n_remote, 1),)),
            pltpu.SemaphoreType.DMA((max(comm.n_local, 1),))]


def _pallas(body, *, name, grid, in_specs, out_specs, out_shape, operands, scratch=(), comm=None):
    params = pltpu.CompilerParams(dimension_semantics=("arbitrary",) * len(grid), vmem_limit_bytes=VMEM_LIMIT_V7X)
    if comm is None:
        return pl.pallas_call(body, name=name, grid=grid, in_specs=in_specs, out_specs=out_specs, out_shape=out_shape,
                              scratch_shapes=list(scratch), compiler_params=params)(*operands)
    n_in, n_out, n_scr = len(in_specs), len(out_specs), len(scratch)
    c_in, c_out = len(comm.operands), len(comm.out_shapes)

    def with_comm(*refs):
        ins, c_ins = refs[:n_in], refs[n_in:n_in + c_in]
        o0 = n_in + c_in
        outs, c_outs = refs[o0:o0 + n_out], refs[o0 + n_out:o0 + n_out + c_out]
        s0 = o0 + n_out + c_out
        scr = refs[s0:s0 + n_scr]
        send_sems, recv_sems, local_sems = refs[s0 + n_scr:]
        first = last = None
        for axis, size in enumerate(grid):
            at_start, at_end = pl.program_id(axis) == 0, pl.program_id(axis) == size - 1
            first = at_start if first is None else jnp.logical_and(first, at_start)
            last = at_end if last is None else jnp.logical_and(last, at_end)

        def copies():
            return comm.plan(c_ins, c_outs, lambda k: send_sems.at[k], lambda k: recv_sems.at[k],
                             lambda k: local_sems.at[k])

        @pl.when(first)
        def _():
            sends, _, locs = copies()
            for cp in sends + locs:
                cp.start()

        body(*ins, *outs, *scr)

        @pl.when(last)
        def _():
            sends, recvs, locs = copies()
            for cp in recvs:
                cp.wait_recv()
            for cp in sends:
                cp.wait_send()
            for cp in locs:
                cp.wait()

    return pl.pallas_call(
        with_comm, name=name, grid=grid,
        in_specs=list(in_specs) + [ANY] * c_in, out_specs=list(out_specs) + [ANY] * c_out,
        out_shape=list(out_shape) + list(comm.out_shapes),
        scratch_shapes=list(scratch) + _sem_scratch(comm),
        input_output_aliases={n_in + k: n_out + v for k, v in comm.aliases.items()},
        compiler_params=params)(*operands, *comm.operands)


def _place():
    return lax.axis_index("x"), lax.axis_index("y"), lax.axis_index("c")


def _other_chips(x, y):
    return [(1 - x, y), (x, 1 - y), (1 - x, 1 - y)]


def _slot(px, py, pc):
    return 4 * px + 2 * py + pc


def _remote(src, dst, send_sem, recv_sem, to):
    return pltpu.make_async_remote_copy(src_ref=src, dst_ref=dst, send_sem=send_sem, recv_sem=recv_sem,
                                        device_id=to, device_id_type=MESH)


def _gather_first(half_block):
    def plan(ins, outs, send, recv, loc):
        (blk,), (full,) = ins, outs
        x, y, c = _place()
        chips = _other_chips(x, y)
        mine = full.at[_slot(x, y, c)]
        sends = [_remote(blk, mine, send(0), recv(0), (x, y, 1 - c))]
        sends += [_remote(blk, mine, send(1 + j), recv(1 + j), (*chip, c)) for j, chip in enumerate(chips)]
        recvs = [_remote(blk, full.at[_slot(x, y, 1 - c)], send(0), recv(0), (x, y, 1 - c))]
        recvs += [_remote(blk, full.at[_slot(*chip, c)], send(1 + j), recv(1 + j), (*chip, c))
                  for j, chip in enumerate(chips)]
        return sends, recvs, [pltpu.make_async_copy(blk, mine, loc(0))]

    return _Comm((half_block,), (SDS((2 * N_CHIPS,) + half_block.shape, half_block.dtype),), {}, 4, 1, plan)


def _gather_second(partly_gathered):
    def plan(ins, outs, send, recv, loc):
        (src,), (full,) = ins, outs
        x, y, c = _place()
        chips = _other_chips(x, y)
        sends = [_remote(src.at[_slot(*chip, c)], full.at[_slot(*chip, c)], send(j), recv(j), (x, y, 1 - c))
                 for j, chip in enumerate(chips)]
        recvs = [_remote(src.at[_slot(*chip, 1 - c)], full.at[_slot(*chip, 1 - c)], send(j), recv(j), (x, y, 1 - c))
                 for j, chip in enumerate(chips)]
        return sends, recvs, []

    return _Comm((partly_gathered,), (SDS(partly_gathered.shape, partly_gathered.dtype),), {0: 0}, 3, 0, plan)


def _gather_whole(half_block, small_block):
    def body(blk_ref, small_ref, out_ref, small_out_ref, send_sems, recv_sems, local_sems):
        x, y, c = _place()
        me, sibling = (x, y, c), (x, y, 1 - c)
        chips = _other_chips(x, y)

        def copy(k, block, to, src=None):
            return _remote(out_ref.at[_slot(*block)] if src is None else src, out_ref.at[_slot(*block)],
                           send_sems.at[k], recv_sems.at[k], to)

        def small_copy(k, chip, to):
            return _remote(small_ref, small_out_ref.at[2 * chip[0] + chip[1]], send_sems.at[7 + k], recv_sems.at[7 + k], to)

        mine = pltpu.make_async_copy(blk_ref, out_ref.at[_slot(*me)], local_sems.at[0])
        mine_small = pltpu.make_async_copy(small_ref, small_out_ref.at[2 * x + y], local_sems.at[1])
        mine.start()
        mine_small.start()
        first = [copy(0, me, sibling, src=blk_ref)]
        first += [copy(1 + j, me, (*chip, c), src=blk_ref) for j, chip in enumerate(chips)]
        first += [small_copy(j, (x, y), (*chip, c)) for j, chip in enumerate(chips)]
        for cp in first:
            cp.start()
        passed = [copy(4 + j, (*chip, c), sibling) for j, chip in enumerate(chips)]
        for j, chip in enumerate(chips):
            copy(1 + j, (*chip, c), me).wait_recv()
            passed[j].start()
        copy(0, sibling, me).wait_recv()
        for j, chip in enumerate(chips):
            copy(4 + j, (*chip, 1 - c), me).wait_recv()
            small_copy(j, chip, me).wait_recv()
        for cp in first + passed:
            cp.wait_send()
        mine.wait()
        mine_small.wait()

    return pl.pallas_call(
        body, name="gather_whole", in_specs=[ANY, ANY], out_specs=[ANY, ANY],
        out_shape=[SDS((2 * N_CHIPS,) + half_block.shape, half_block.dtype),
                   SDS((N_CHIPS,) + small_block.shape, small_block.dtype)],
        scratch_shapes=[pltpu.SemaphoreType.DMA((10,)), pltpu.SemaphoreType.DMA((10,)), pltpu.SemaphoreType.DMA((2,))],
    )(half_block, small_block)


def _pair_send(grads):
    def plan(ins, outs, send, recv, loc):
        (g,), (got,) = ins, outs
        x, y, c = _place()
        copies = [_remote(g.at[j, 1 - c], got.at[j], send(j), recv(j), (x, y, 1 - c)) for j in range(N_CHIPS)]
        return copies, copies, []

    shape = (grads.shape[0],) + grads.shape[2:]
    return _Comm((grads,), (SDS(shape, grads.dtype),), {}, N_CHIPS, 0, plan)


def _chip_exchange(partial):
    def plan(ins, outs, send, recv, loc):
        (p,), (got,) = ins, outs
        x, y, c = _place()
        my_chip = 2 * x + y
        chips = _other_chips(x, y)
        sends = [_remote(p.at[2 * chip[0] + chip[1]], got.at[my_chip], send(j), recv(j), (*chip, c))
                 for j, chip in enumerate(chips)]
        recvs = [_remote(p.at[my_chip], got.at[2 * chip[0] + chip[1]], send(j), recv(j), (*chip, c))
                 for j, chip in enumerate(chips)]
        return sends, recvs, [pltpu.make_async_copy(p.at[my_chip], got.at[my_chip], loc(0))]

    return _Comm((partial,), (SDS(partial.shape, partial.dtype),), {}, 3, 1, plan)


def _pair_sum(name, core, grads, received):
    h = grads.shape[2]

    def body(core_ref, g_ref, r_ref, o_ref):
        o_ref[...] = (g_ref[0] + r_ref[...]).astype(BF16)

    return pl.pallas_call(
        body, name=name,
        grid_spec=pltpu.PrefetchScalarGridSpec(
            num_scalar_prefetch=1, grid=(N_CHIPS,),
            in_specs=[pl.BlockSpec((1, 1, h, D_MODEL), lambda j, core_ref: (j, core_ref[0], 0, 0)),
                      pl.BlockSpec((1, h, D_MODEL), lambda j, core_ref: (j, 0, 0))],
            out_specs=pl.BlockSpec((1, h, D_MODEL), lambda j, core_ref: (j, 0, 0))),
        out_shape=SDS((N_CHIPS, h, D_MODEL), BF16),
        compiler_params=pltpu.CompilerParams(dimension_semantics=("arbitrary",), vmem_limit_bytes=VMEM_LIMIT_V7X),
    )(core, grads, received)


SMALL_ROWS = 8


def _sum_blocks(ref):
    return (ref[0].astype(F32) + ref[1].astype(F32)) + (ref[2].astype(F32) + ref[3].astype(F32))


def _tail_reduce(last_grads, exchanged, small):
    n = len(exchanged)
    h = last_grads.shape[2]

    def body(*refs):
        g_ref, ex, small_ref = refs[0], refs[1:1 + n], refs[1 + n]
        o0 = 2 + n
        out, out_last, small_out = refs[o0:o0 + n], refs[o0 + n], refs[o0 + n + 1]
        s0 = o0 + n + 2
        halves, half_last = refs[s0:s0 + n], refs[s0 + n]
        own, got, part, exch, small_buf = refs[s0 + n + 1:s0 + n + 6]
        pair_send, pair_recv, chip_send, chip_recv, share_send, share_recv, small_send, small_recv, local_sems = refs[s0 + n + 6:]
        x, y, c = _place()
        sibling = (x, y, 1 - c)
        my_chip, me = 2 * x + y, _slot(x, y, c)
        chips = _other_chips(x, y)

        to_sibling = [_remote(g_ref.at[j, 1 - c], got.at[j], pair_send.at[j], pair_recv.at[j], sibling)
                      for j in range(N_CHIPS)]
        load_own = [pltpu.make_async_copy(g_ref.at[j, c], own.at[j], local_sems.at[j]) for j in range(N_CHIPS)]
        for cp in to_sibling + load_own:
            cp.start()

        small_buf[me] = small_ref[...]
        small_copies = []
        for mask in range(1, 8):
            peer = (x ^ (mask >> 2), y ^ ((mask >> 1) & 1), c ^ (mask & 1))
            small_copies.append(_remote(small_ref, small_buf.at[me], small_send.at[mask - 1], small_recv.at[mask - 1], peer))
        for cp in small_copies:
            cp.start()

        def share(k, half_ref, out_ref):
            keep = pltpu.make_async_copy(half_ref, out_ref.at[c], local_sems.at[N_CHIPS + k])
            give = _remote(half_ref, out_ref.at[c], share_send.at[k], share_recv.at[k], sibling)
            take = _remote(half_ref, out_ref.at[1 - c], share_send.at[k], share_recv.at[k], sibling)
            keep.start()
            give.start()
            return keep, give, take

        shares = []
        for k in range(n):
            halves[k][...] = _sum_blocks(ex[k])
            shares.append(share(k, halves[k], out[k]))

        for cp in to_sibling:
            cp.wait_recv()
        for cp in load_own:
            cp.wait()
        part[...] = (own[...] + got[...]).astype(BF16)
        exch[my_chip] = part[my_chip]
        to_chips = [_remote(part.at[2 * chip[0] + chip[1]], exch.at[my_chip], chip_send.at[j], chip_recv.at[j], (*chip, c))
                    for j, chip in enumerate(chips)]
        from_chips = [_remote(part.at[my_chip], exch.at[2 * chip[0] + chip[1]], chip_send.at[j], chip_recv.at[j], (*chip, c))
                      for j, chip in enumerate(chips)]
        for cp in to_chips:
            cp.start()

        for cp in small_copies:
            cp.wait_recv()
        total = small_buf[0]
        for d in range(1, 8):
            total = total + small_buf[d]
        small_out[...] = total

        for cp in from_chips:
            cp.wait_recv()
        half_last[...] = _sum_blocks(exch)
        shares.append(share(n, half_last, out_last))

        for keep, give, take in shares:
            take.wait_recv()
            give.wait_send()
            keep.wait()
        for cp in to_sibling + to_chips + small_copies:
            cp.wait_send()

    blocks = (N_CHIPS, h, D_MODEL)
    return pl.pallas_call(
        body, name="tail_reduce",
        in_specs=[ANY] + [VMEM_WHOLE] * (n + 1), out_specs=[ANY] * (n + 1) + [VMEM_WHOLE],
        out_shape=[SDS((2,) + e.shape[1:], F32) for e in exchanged] + [SDS((2, h, D_MODEL), F32), SDS(small.shape, F32)],
        scratch_shapes=[pltpu.VMEM(e.shape[1:], F32) for e in exchanged] + [pltpu.VMEM((h, D_MODEL), F32)]
                       + [pltpu.VMEM(blocks, F32), pltpu.VMEM(blocks, F32), pltpu.VMEM(blocks, BF16), pltpu.VMEM(blocks, BF16),
                          pltpu.VMEM((8,) + small.shape, F32)]
                       + [pltpu.SemaphoreType.DMA((N_CHIPS,)), pltpu.SemaphoreType.DMA((N_CHIPS,)),
                          pltpu.SemaphoreType.DMA((3,)), pltpu.SemaphoreType.DMA((3,)),
                          pltpu.SemaphoreType.DMA((n + 1,)), pltpu.SemaphoreType.DMA((n + 1,)),
                          pltpu.SemaphoreType.DMA((7,)), pltpu.SemaphoreType.DMA((7,)),
                          pltpu.SemaphoreType.DMA((N_CHIPS + n + 1,))],
        compiler_params=pltpu.CompilerParams(vmem_limit_bytes=VMEM_LIMIT_V7X),
    )(last_grads, *exchanged, small)


def _rope_expansion():
    half = ROT_DIM // 2
    expand = np.zeros((2 * half, 3 * 128), np.float32)
    const = np.zeros((1, 3 * 128), np.float32)
    for lane in range(128):
        d = lane % HEAD_DIM
        if d < ROT_DIM:
            expand[d % half, lane] = 1.0
        else:
            const[0, lane] = 1.0
        if d < half:
            expand[half + d, 128 + lane] = -1.0
        elif d < ROT_DIM:
            expand[half + d - half, 256 + lane] = 1.0
    return expand, const


ROPE_PIECES = 3 * ROT_DIM


def _rope_inputs(seq):
    pos = jnp.arange(seq, dtype=F32)
    inv_freq = ROPE_THETA ** (-jnp.arange(0, ROT_DIM, 2, dtype=F32) / ROT_DIM)
    ang = pos[:, None] * inv_freq[None, :]
    cs = jnp.concatenate([jnp.cos(ang), jnp.sin(ang)], axis=1)
    hi = lax.reduce_precision(cs, 8, 7)
    mid = lax.reduce_precision(cs - hi, 8, 7)
    low = cs - hi - mid
    expand, const = _rope_expansion()
    pieces = jnp.concatenate([hi, mid, low], axis=1).astype(BF16)
    return pieces, jnp.asarray(np.concatenate([expand] * 3, axis=0), BF16), jnp.asarray(const)


def _rope_specs(tb):
    return [pl.BlockSpec((tb, ROPE_PIECES), lambda i: (i, 0)), _resident((ROPE_PIECES, 3 * 128)), _resident((1, 3 * 128))]


def _rope_tile(pieces_ref, expand_ref, const_ref):
    tables = _dot(pieces_ref[...], expand_ref[...]) + const_ref[...]
    return tables[:, 0:128], tables[:, 128:256], tables[:, 256:384]


def _rope(t, c, sa, sb):
    half = ROT_DIM // 2
    return t * c + pltpu.roll(t, 128 - half, 1) * sa + pltpu.roll(t, half, 1) * sb


def _rope_transposed(dt, c, sa, sb):
    half = ROT_DIM // 2
    return dt * c + pltpu.roll(dt * sa, half, 1) + pltpu.roll(dt * sb, 128 - half, 1)


def _cast_halves(core, w_up, w_down, w_out, w_in_t):
    def body(core_ref, up_ref, down_ref, out_ref, in_ref, up_o, down_o, out_o, in_o):
        up_o[...] = up_ref[...].astype(BF16)
        down_o[...] = down_ref[...].astype(BF16)
        out_o[...] = out_ref[...].astype(BF16)
        in_o[...] = in_ref[...].astype(BF16)

    half = lambda rows: pl.BlockSpec((rows, D_MODEL), lambda i, core_ref: (core_ref[0], 0))
    whole = lambda rows: pl.BlockSpec((rows, D_MODEL), lambda i, core_ref: (0, 0))
    rows = (H_UP, H_DOWN, H_OUT, H_IN)
    return pl.pallas_call(
        body, name="cast_halves",
        grid_spec=pltpu.PrefetchScalarGridSpec(
            num_scalar_prefetch=1, grid=(1,), in_specs=[half(r) for r in rows], out_specs=[whole(r) for r in rows]),
        out_shape=[SDS((r, D_MODEL), BF16) for r in rows],
        compiler_params=pltpu.CompilerParams(dimension_semantics=("arbitrary",), vmem_limit_bytes=VMEM_LIMIT_V7X),
    )(core, w_up, w_down, w_out, w_in_t)


def _in_proj(x, g_pre, w_in_t, rope, comm=None):
    seq = x.shape[0]
    tb = TOKEN_TILE

    def body(x_ref, g_ref, w_ref, c_ref, sa_ref, sb_ref,
             q_ref, kd0_ref, kd1_ref, vd0_ref, vd1_ref, gb_ref, gc_ref, xin_ref, hn_ref):
        xv = x_ref[...]
        hn = (xv * _rms(xv) * g_ref[...]).astype(BF16)
        hn_ref[...] = hn
        proj = _dot_nt(hn, w_ref[...].reshape(IN_COLS, D_MODEL))
        c, sa, sb = _rope_tile(c_ref, sa_ref, sb_ref)
        scale = 1.0 / math.sqrt(HEAD_DIM)
        for p in range(Q_WIDTH // 128):
            q_ref[:, 128 * p:128 * (p + 1)] = (_rope(proj[:, 128 * p:128 * (p + 1)], c, sa, sb) * scale).astype(BF16)
        k = _rope(proj[:, Q_WIDTH:Q_WIDTH + KV_WIDTH], c, sa, sb)
        v = proj[:, Q_WIDTH + KV_WIDTH:Q_WIDTH + 2 * KV_WIDTH]
        low = _lane_lt64(k.shape)
        k_sw, v_sw = pltpu.roll(k, HEAD_DIM, 1), pltpu.roll(v, HEAD_DIM, 1)
        kd0_ref[...] = jnp.where(low, k, k_sw).astype(BF16)
        kd1_ref[...] = jnp.where(low, k_sw, k).astype(BF16)
        vd0_ref[...] = jnp.where(low, v, v_sw).astype(BF16)
        vd1_ref[...] = jnp.where(low, v_sw, v).astype(BF16)
        base = Q_WIDTH + 2 * KV_WIDTH
        gb_ref[...] = proj[:, base:base + CONV_WIDTH].astype(BF16)
        gc_ref[...] = proj[:, base + CONV_WIDTH:base + 2 * CONV_WIDTH].astype(BF16)
        xin_ref[...] = proj[:, base + 2 * CONV_WIDTH:base + 3 * CONV_WIDTH].astype(BF16)

    tile = lambda w: pl.BlockSpec((tb, w), lambda i: (i, 0))
    return _pallas(
        body, name="in_proj", grid=(seq // tb,),
        in_specs=[tile(D_MODEL), _resident((1, D_MODEL)), _resident(w_in_t.shape), *_rope_specs(tb)],
        out_specs=[tile(Q_WIDTH), tile(128), tile(128), tile(128), tile(128),
                   tile(CONV_WIDTH), tile(CONV_WIDTH), tile(CONV_WIDTH), tile(D_MODEL)],
        out_shape=[SDS((seq, Q_WIDTH), BF16)] + [SDS((seq, 128), BF16)] * 4
                  + [SDS((seq, CONV_WIDTH), BF16)] * 3 + [SDS((seq, D_MODEL), BF16)],
        operands=(x, g_pre, w_in_t, *rope), comm=comm)


def _attn_valid(i):
    shape = (4 * QBLOCK, 2 * QBLOCK)
    row = lax.broadcasted_iota(jnp.int32, shape, 0)
    col = lax.broadcasted_iota(jnp.int32, shape, 1)
    qi = row & (QBLOCK - 1)
    return (col > qi) & (col <= qi + QBLOCK) & ((col >= QBLOCK) | (i > 0))


def _stack_heads(pair0, pair1):
    low = _lane_lt64(pair0.shape)
    zero = jnp.zeros_like(pair0)
    return jnp.concatenate([jnp.where(low, pair0, zero), jnp.where(low, zero, pair0),
                            jnp.where(low, pair1, zero), jnp.where(low, zero, pair1)], axis=0)


def _unstack_heads(stacked):
    low = _lane_lt64((QBLOCK, 128))
    pair0 = jnp.where(low, stacked[0:QBLOCK], stacked[QBLOCK:2 * QBLOCK])
    pair1 = jnp.where(low, stacked[2 * QBLOCK:3 * QBLOCK], stacked[3 * QBLOCK:4 * QBLOCK])
    return pair0, pair1


def _sink_column(sink_ref, kv_head):
    row = lax.broadcasted_iota(jnp.int32, (4 * QBLOCK, 1), 0)
    s = [sink_ref[0, 4 * kv_head + j] for j in range(4)]
    return jnp.where(row < QBLOCK, s[0], jnp.where(row < 2 * QBLOCK, s[1], jnp.where(row < 3 * QBLOCK, s[2], s[3])))


def _band(ref, i):
    prev = pl.multiple_of(jnp.maximum(i - 1, 0) * QBLOCK, QBLOCK)
    own = pl.multiple_of(i * QBLOCK, QBLOCK)
    return jnp.concatenate([ref[pl.ds(prev, QBLOCK), :], ref[pl.ds(own, QBLOCK), :]], axis=0), prev, own


def _softmax_with_sink(s, sink_col):
    m = jnp.maximum(jnp.max(s, axis=-1, keepdims=True), sink_col)
    p = jnp.exp(s - m)
    e_sink = jnp.exp(sink_col - m)
    inv_l = 1.0 / (jnp.sum(p, axis=-1, keepdims=True) + e_sink)
    return p, e_sink, inv_l


def _attention_fwd(q, kd0, kd1, vd0, vd1, sinks, comm=None):
    seq = q.shape[0]

    nb = ATTN_FWD_BLOCKS

    def body(sink_ref, q_ref, kd0_ref, kd1_ref, vd0_ref, vd1_ref, o_ref):
        for b in range(nb):
            i = pl.program_id(0) * nb + b
            rows = slice(QBLOCK * b, QBLOCK * (b + 1))
            valid = _attn_valid(i)
            for kv_head, (k_ref, v_ref) in enumerate(((kd0_ref, vd0_ref), (kd1_ref, vd1_ref))):
                kband, _, _ = _band(k_ref, i)
                vband, _, _ = _band(v_ref, i)
                base = 256 * kv_head
                qm = _stack_heads(q_ref[rows, base:base + 128], q_ref[rows, base + 128:base + 256])
                s = jnp.where(valid, _dot_nt(qm, kband), NEG_INF)
                p, _, inv_l = _softmax_with_sink(s, _sink_column(sink_ref, kv_head))
                o = _dot(p.astype(BF16), vband) * inv_l
                pair0, pair1 = _unstack_heads(o)
                o_ref[rows, base:base + 128] = pair0.astype(BF16)
                o_ref[rows, base + 128:base + 256] = pair1.astype(BF16)

    blk = pl.BlockSpec((nb * QBLOCK, Q_WIDTH), lambda i: (i, 0))
    full = _resident((seq, 128))
    return _pallas(
        body, name="attention_fwd", grid=(seq // (nb * QBLOCK),),
        in_specs=[pl.BlockSpec(memory_space=pltpu.SMEM), blk, full, full, full, full],
        out_specs=[blk], out_shape=[SDS((seq, Q_WIDTH), BF16)],
        operands=(sinks, q, kd0, kd1, vd0, vd1), comm=comm)


HALO = 16


def _conv_parts(gc, xin, gc_halo, xin_halo, conv_w, first):
    tb = gc.shape[0]
    u = gc.astype(F32) * xin.astype(F32)
    u_halo = jnp.where(first, 0.0, gc_halo.astype(F32) * xin_halo.astype(F32))
    ext = jnp.concatenate([u_halo, u], axis=0)
    u1 = pltpu.roll(ext, 1, 0)[HALO:HALO + tb]
    u2 = pltpu.roll(ext, 2, 0)[HALO:HALO + tb]
    y = conv_w[0:1, :] * u2 + conv_w[1:2, :] * u1 + conv_w[2:3, :] * u
    return u, u1, u2, y


def _halo_prev(tb, w):
    return pl.BlockSpec((HALO, w), lambda i: (jnp.maximum(i * (tb // HALO) - 1, 0), 0))


def _residual_mid(x, mix, g_post_mix):
    mix_f = mix.astype(F32)
    return x + mix_f * _rms(mix_f) * g_post_mix


def _mix_out(attn, gb, gc, xin, conv_w, g_attn, g_conv, w_out, comm=None):
    seq = attn.shape[0]
    tb = TOKEN_TILE

    def body(a_ref, gb_ref, gc_ref, xin_ref, gch_ref, xinh_ref, cw_ref, ga_ref, gcn_ref, w_ref, mix_ref, mixed_ref):
        first = pl.program_id(0) == 0
        _, _, _, y = _conv_parts(gc_ref[...], xin_ref[...], gch_ref[...], xinh_ref[...], cw_ref[...], first)
        conv = gb_ref[...].astype(F32) * y
        a = a_ref[...].astype(F32)
        mixed_ref[:, 0:Q_WIDTH] = (a * _rms(a) * ga_ref[...]).astype(BF16)
        mixed_ref[:, Q_WIDTH:] = (conv * _rms(conv) * gcn_ref[...]).astype(BF16)
        mix_ref[...] = _dot(mixed_ref[...], w_ref[...].reshape(D_MODEL, D_MODEL)).astype(BF16)

    tile = lambda w: pl.BlockSpec((tb, w), lambda i: (i, 0))
    return _pallas(
        body, name="mix_out", grid=(seq // tb,),
        in_specs=[tile(Q_WIDTH), tile(CONV_WIDTH), tile(CONV_WIDTH), tile(CONV_WIDTH),
                  _halo_prev(tb, CONV_WIDTH), _halo_prev(tb, CONV_WIDTH),
                  _resident((CONV_K, CONV_WIDTH)), _resident((1, Q_WIDTH)), _resident((1, CONV_WIDTH)),
                  _resident(w_out.shape)],
        out_specs=[tile(D_MODEL), tile(D_MODEL)],
        out_shape=[SDS((seq, D_MODEL), BF16), SDS((seq, D_MODEL), BF16)],
        operands=(attn, gb, gc, xin, gc, xin, conv_w, g_attn, g_conv, w_out), comm=comm)


def _mlp_loss(x, mix, target, g_post_mix, g_pre_mlp, g_post_mlp, w_up, w_down):
    seq = x.shape[0]
    tb = TOKEN_TILE

    def body(x_ref, mix_ref, t_ref, gpm_ref, g2_ref, g4_ref, wup_ref, wdown_ref,
             up_ref, hn2_ref, dout_ref, dmlp_ref, loss_ref, dg4_ref, act_ref):
        @pl.when(pl.program_id(0) == 0)
        def _():
            loss_ref[...] = jnp.zeros_like(loss_ref)
            dg4_ref[...] = jnp.zeros_like(dg4_ref)

        halves = [slice(0, tb // 2), slice(tb // 2, tb)]
        hv, hn2 = [], []
        for rows in halves:
            hv.append(_residual_mid(x_ref[rows, :], mix_ref[rows, :], gpm_ref[...]))
            hn2.append((hv[-1] * _rms(hv[-1]) * g2_ref[...]).astype(BF16))
            hn2_ref[rows, :] = hn2[-1]
        for k, rows in enumerate(halves):
            for j in range(N_CHIPS):
                up = _dot(hn2[k], _chip_block(wup_ref, j))
                up = jnp.maximum(up, 0.0)
                up_ref[rows, 1024 * j:1024 * (j + 1)] = up.astype(BF16)
                act_ref[rows, 1024 * j:1024 * (j + 1)] = (up * up).astype(BF16)
        w_down_all = wdown_ref[...].reshape(D_FF, D_MODEL)
        loss = jnp.zeros((1, 1), F32)
        dg4 = jnp.zeros((1, D_MODEL), F32)
        for k, rows in enumerate(halves):
            mlp = _dot(act_ref[rows, :], w_down_all)
            rstd = _rms(mlp)
            zhat = mlp * rstd
            diff = hv[k] + zhat * g4_ref[...] - t_ref[rows, :]
            loss = loss + jnp.sum(jnp.sum(diff * diff, axis=1, keepdims=True), axis=0, keepdims=True)
            dout = diff * (1.0 / D_MODEL)
            dout_ref[rows, :] = dout
            dg4 = dg4 + _colsum(dout * zhat)
            dmlp_ref[rows, :] = _norm_bwd(dout, g4_ref[...], zhat, rstd).astype(BF16)
        loss_ref[...] += loss
        dg4_ref[...] += dg4

    tile = lambda w: pl.BlockSpec((tb, w), lambda i: (i, 0))
    return _pallas(
        body, name="mlp_loss", grid=(seq // tb,),
        in_specs=[tile(D_MODEL), tile(D_MODEL), tile(D_MODEL), _resident((1, D_MODEL)), _resident((1, D_MODEL)),
                  _resident((1, D_MODEL)), _resident(w_up.shape), _resident(w_down.shape)],
        out_specs=[tile(D_FF), tile(D_MODEL), tile(D_MODEL), tile(D_MODEL),
                   pl.BlockSpec((1, 1), lambda i: (0, 0)), pl.BlockSpec((1, D_MODEL), lambda i: (0, 0))],
        out_shape=[SDS((seq, D_FF), BF16), SDS((seq, D_MODEL), BF16), SDS((seq, D_MODEL), F32),
                   SDS((seq, D_MODEL), BF16), SDS((1, 1), F32), SDS((1, D_MODEL), F32)],
        scratch=[pltpu.VMEM((tb, D_FF), BF16)],
        operands=(x, mix, target, g_post_mix, g_pre_mlp, g_post_mlp, w_up, w_down))


def _mlp_bwd(dmlp, up, x, dout, mix, g_pre_mlp, g_post_mix, w_up, w_down):
    seq = x.shape[0]
    tb = MLP_BWD_TOKEN_TILE

    def body(dmlp_ref, up_ref, x_ref, dout_ref, mix_ref, g2_ref, gpm_ref, wup_ref, wdown_ref,
             dup_ref, dh_ref, dmix_ref, dg2_ref, dgpm_ref):
        @pl.when(pl.program_id(0) == 0)
        def _():
            dg2_ref[...] = jnp.zeros_like(dg2_ref)
            dgpm_ref[...] = jnp.zeros_like(dgpm_ref)

        subs = [slice(k * MLP_BWD_SUB_TILE, (k + 1) * MLP_BWD_SUB_TILE) for k in range(tb // MLP_BWD_SUB_TILE)]
        dhn2 = []
        for rows in subs:
            dmlp_v = dmlp_ref[rows, :]
            acc = None
            for j in range(N_CHIPS):
                cols = slice(1024 * j, 1024 * (j + 1))
                dact = _dot_nt(dmlp_v, _chip_block(wdown_ref, j))
                dup = (dact * (2.0 * up_ref[rows, cols].astype(F32))).astype(BF16)
                dup_ref[rows, cols] = dup
                part = _dot_nt(dup, _chip_block(wup_ref, j))
                acc = part if acc is None else acc + part
            dhn2.append(acc)
        dg2 = jnp.zeros((1, D_MODEL), F32)
        dgpm = jnp.zeros((1, D_MODEL), F32)
        for k, rows in enumerate(subs):
            mix_v = mix_ref[rows, :].astype(F32)
            hv = _residual_mid(x_ref[rows, :], mix_ref[rows, :], gpm_ref[...])
            r2 = _rms(hv)
            hhat = hv * r2
            dg2 = dg2 + _colsum(dhn2[k] * hhat)
            dh = dout_ref[rows, :] + _norm_bwd(dhn2[k], g2_ref[...], hhat, r2)
            dh_ref[rows, :] = dh.astype(BF16)
            rz = _rms(mix_v)
            zhat = mix_v * rz
            dgpm = dgpm + _colsum(dh * zhat)
            dmix_ref[rows, :] = _norm_bwd(dh, gpm_ref[...], zhat, rz).astype(BF16)
        dg2_ref[...] += dg2
        dgpm_ref[...] += dgpm

    tile = lambda w: pl.BlockSpec((tb, w), lambda i: (i, 0))
    vec = pl.BlockSpec((1, D_MODEL), lambda i: (0, 0))
    return _pallas(
        body, name="mlp_bwd", grid=(seq // tb,),
        in_specs=[tile(D_MODEL), tile(D_FF), tile(D_MODEL), tile(D_MODEL), tile(D_MODEL),
                  _resident((1, D_MODEL)), _resident((1, D_MODEL)), _resident(w_up.shape), _resident(w_down.shape)],
        out_specs=[tile(D_FF), tile(D_MODEL), tile(D_MODEL), vec, vec],
        out_shape=[SDS((seq, D_FF), BF16), SDS((seq, D_MODEL), BF16), SDS((seq, D_MODEL), BF16),
                   SDS((1, D_MODEL), F32), SDS((1, D_MODEL), F32)],
        operands=(dmlp, up, x, dout, mix, g_pre_mlp, g_post_mix, w_up, w_down))


class _Rider(NamedTuple):
    body: Callable
    in_specs: list
    out_specs: list
    out_shape: list
    operands: tuple


def _mix_bwd(dmix, attn, gb, gc, xin, conv_w, g_attn, g_conv, w_out, n_k):
    seq = attn.shape[0]
    tb = seq // (N_CHIPS * n_k)

    def body(first, dmix_ref, a_ref, gb_ref, gc_ref, xin_ref, gch_ref, xinh_ref, cw_ref, ga_ref, gcn_ref, w_ref,
             dattn_ref, dgb_ref, dy_ref, dga_ref, dgcn_ref, dcw_ref):
        @pl.when(first)
        def _():
            dga_ref[...] = jnp.zeros_like(dga_ref)
            dgcn_ref[...] = jnp.zeros_like(dgcn_ref)
            dcw_ref[...] = jnp.zeros_like(dcw_ref)

        dmixed = _dot_nt(dmix_ref[...], w_ref[...].reshape(D_MODEL, D_MODEL))
        a = a_ref[...].astype(F32)
        ra = _rms(a)
        ahat = a * ra
        dan = dmixed[:, 0:Q_WIDTH]
        dga_ref[...] += _colsum(dan * ahat)
        dattn_ref[...] = _norm_bwd(dan, ga_ref[...], ahat, ra).astype(BF16)
        gbv = gb_ref[...].astype(F32)
        u, u1, u2, y = _conv_parts(gc_ref[...], xin_ref[...], gch_ref[...], xinh_ref[...], cw_ref[...], first)
        conv = gbv * y
        rc = _rms(conv)
        chat = conv * rc
        dcn = dmixed[:, Q_WIDTH:]
        dgcn_ref[...] += _colsum(dcn * chat)
        dconv = _norm_bwd(dcn, gcn_ref[...], chat, rc)
        dgb_ref[...] = (dconv * y).astype(BF16)
        dy = dconv * gbv
        dy_ref[...] = dy.astype(BF16)
        dcw_ref[0:1, :] += _colsum(dy * u2)
        dcw_ref[1:2, :] += _colsum(dy * u1)
        dcw_ref[2:3, :] += _colsum(dy * u)

    tile = lambda w: pl.BlockSpec((tb, w), lambda j, k: (j * n_k + k, 0))
    halo = lambda w: pl.BlockSpec((HALO, w), lambda j, k: (jnp.maximum((j * n_k + k) * (tb // HALO) - 1, 0), 0))
    whole = lambda shape: pl.BlockSpec(shape, lambda j, k: (0,) * len(shape))
    return _Rider(
        body,
        in_specs=[tile(D_MODEL), tile(Q_WIDTH), tile(CONV_WIDTH), tile(CONV_WIDTH), tile(CONV_WIDTH),
                  halo(CONV_WIDTH), halo(CONV_WIDTH),
                  _resident((CONV_K, CONV_WIDTH)), _resident((1, Q_WIDTH)), _resident((1, CONV_WIDTH)),
                  _resident(w_out.shape)],
        out_specs=[tile(Q_WIDTH), tile(CONV_WIDTH), tile(CONV_WIDTH),
                   whole((1, Q_WIDTH)), whole((1, CONV_WIDTH)), whole((CONV_K, CONV_WIDTH))],
        out_shape=[SDS((seq, Q_WIDTH), BF16), SDS((seq, CONV_WIDTH), BF16), SDS((seq, CONV_WIDTH), BF16),
                   SDS((1, Q_WIDTH), F32), SDS((1, CONV_WIDTH), F32), SDS((CONV_K, CONV_WIDTH), F32)],
        operands=(dmix, attn, gb, gc, xin, gc, xin, conv_w, g_attn, g_conv, w_out))


def _attention_bwd(q, dattn, attn, kd0, kd1, vd0, vd1, sinks, comm=None):
    seq = q.shape[0]
    nb = ATTN_BWD_BLOCKS

    def body(sink_ref, q_ref, do_ref, o_ref, kd0_ref, kd1_ref, vd0_ref, vd1_ref,
             dq_ref, dk0_ref, dk1_ref, dv0_ref, dv1_ref, dsink_ref):
        @pl.when(pl.program_id(0) == 0)
        def _():
            for r in (dk0_ref, dk1_ref, dv0_ref, dv1_ref, dsink_ref):
                r[...] = jnp.zeros_like(r)

        lane = lax.broadcasted_iota(jnp.int32, (1, 128), 1)
        dsink = jnp.zeros((1, 128), F32)
        for b in range(nb):
            i = pl.program_id(0) * nb + b
            rows = slice(QBLOCK * b, QBLOCK * (b + 1))
            valid = _attn_valid(i)
            for kv_head, (k_ref, v_ref, dk_ref, dv_ref) in enumerate(
                    ((kd0_ref, vd0_ref, dk0_ref, dv0_ref), (kd1_ref, vd1_ref, dk1_ref, dv1_ref))):
                kband, prev, own = _band(k_ref, i)
                vband, _, _ = _band(v_ref, i)
                base = 256 * kv_head
                qm = _stack_heads(q_ref[rows, base:base + 128], q_ref[rows, base + 128:base + 256])
                dom = _stack_heads(do_ref[rows, base:base + 128], do_ref[rows, base + 128:base + 256])
                om = _stack_heads(o_ref[rows, base:base + 128], o_ref[rows, base + 128:base + 256])
                s = jnp.where(valid, _dot_nt(qm, kband), NEG_INF)
                p, e_sink, inv_l = _softmax_with_sink(s, _sink_column(sink_ref, kv_head))
                p = p * inv_l
                delta = jnp.sum(dom.astype(F32) * om.astype(F32), axis=-1, keepdims=True)
                ds = (p * (_dot_nt(dom, vband) - delta)).astype(BF16)
                sink_term = -(e_sink * inv_l) * delta
                for j in range(4):
                    part = jnp.sum(sink_term[QBLOCK * j:QBLOCK * (j + 1)], axis=0, keepdims=True)
                    dsink = dsink + jnp.where(lane == 4 * kv_head + j, part, 0.0)
                pair0, pair1 = _unstack_heads(_dot(ds, kband))
                dq_ref[rows, base:base + 128] = pair0.astype(BF16)
                dq_ref[rows, base + 128:base + 256] = pair1.astype(BF16)
                dkd = _dot_tn(ds, qm)
                dkd = dkd + pltpu.roll(dkd, HEAD_DIM, 1)
                dvd = _dot_tn(p.astype(BF16), dom)
                dvd = dvd + pltpu.roll(dvd, HEAD_DIM, 1)
                dk_ref[pl.ds(prev, QBLOCK), :] += dkd[0:QBLOCK]
                dk_ref[pl.ds(own, QBLOCK), :] += dkd[QBLOCK:]
                dv_ref[pl.ds(prev, QBLOCK), :] += dvd[0:QBLOCK]
                dv_ref[pl.ds(own, QBLOCK), :] += dvd[QBLOCK:]
        dsink_ref[...] += dsink

    blk = pl.BlockSpec((nb * QBLOCK, Q_WIDTH), lambda i: (i, 0))
    full = _resident((seq, 128))
    acc = pl.BlockSpec((seq, 128), lambda i: (0, 0))
    return _pallas(
        body, name="attention_bwd", grid=(seq // (nb * QBLOCK),),
        in_specs=[pl.BlockSpec(memory_space=pltpu.SMEM), blk, blk, blk, full, full, full, full],
        out_specs=[blk, acc, acc, acc, acc, pl.BlockSpec((1, 128), lambda i: (0, 0))],
        out_shape=[SDS((seq, Q_WIDTH), BF16)] + [SDS((seq, 128), F32)] * 4 + [SDS((1, 128), F32)],
        operands=(sinks, q, dattn, attn, kd0, kd1, vd0, vd1), comm=comm)


def _in_proj_bwd(dq, dk0, dk1, dv0, dv1, dgb, dy, gc, xin, conv_w, x, dh, g_pre, w_in_t, rope):
    seq = x.shape[0]
    tb = TOKEN_TILE
    n_tiles = seq // tb

    def body(dq_ref, dk0_ref, dk1_ref, dv0_ref, dv1_ref, dgb_ref, dy_ref, dyh_ref, gc_ref, xin_ref, cw_ref,
             x_ref, dh_ref, g_ref, w_ref, c_ref, sa_ref, sb_ref,
             dproj_ref, gx_ref, dg_ref):
        i = pl.program_id(0)

        @pl.when(i == 0)
        def _():
            dg_ref[...] = jnp.zeros_like(dg_ref)

        dy = dy_ref[...].astype(F32)
        ext = jnp.concatenate([dy, jnp.where(i == n_tiles - 1, 0.0, dyh_ref[...].astype(F32))], axis=0)
        dy1 = pltpu.roll(ext, tb + HALO - 1, 0)[0:tb]
        dy2 = pltpu.roll(ext, tb + HALO - 2, 0)[0:tb]
        cw = cw_ref[...]
        du = cw[2:3, :] * dy + cw[1:2, :] * dy1 + cw[0:1, :] * dy2
        scale = 1.0 / math.sqrt(HEAD_DIM)
        base = Q_WIDTH + 2 * KV_WIDTH
        halves = [slice(0, tb // 2), slice(tb // 2, tb)]
        low = _lane_lt64((tb // 2, 128))
        for rows in halves:
            c, sa, sb = _rope_tile(c_ref.at[rows, :], sa_ref, sb_ref)
            for p in range(Q_WIDTH // 128):
                dproj_ref[rows, 128 * p:128 * (p + 1)] = _rope_transposed(
                    dq_ref[rows, 128 * p:128 * (p + 1)].astype(F32) * scale, c, sa, sb).astype(BF16)
            dk = jnp.where(low, dk0_ref[rows, :], dk1_ref[rows, :])
            dproj_ref[rows, Q_WIDTH:Q_WIDTH + KV_WIDTH] = _rope_transposed(dk, c, sa, sb).astype(BF16)
            dproj_ref[rows, Q_WIDTH + KV_WIDTH:base] = jnp.where(low, dv0_ref[rows, :], dv1_ref[rows, :]).astype(BF16)
            dproj_ref[rows, base:base + CONV_WIDTH] = dgb_ref[rows, :]
            dproj_ref[rows, base + CONV_WIDTH:base + 2 * CONV_WIDTH] = (du[rows] * xin_ref[rows, :].astype(F32)).astype(BF16)
            dproj_ref[rows, base + 2 * CONV_WIDTH:] = (du[rows] * gc_ref[rows, :].astype(F32)).astype(BF16)
        w_all = w_ref[...].reshape(IN_COLS, D_MODEL)
        dhn = [_dot(dproj_ref[rows, :], w_all) for rows in halves]
        dg = jnp.zeros((1, D_MODEL), F32)
        for k, rows in enumerate(halves):
            xv = x_ref[rows, :]
            r = _rms(xv)
            xhat = xv * r
            dg = dg + _colsum(dhn[k] * xhat)
            gx_ref[rows, :] = dh_ref[rows, :].astype(F32) + _norm_bwd(dhn[k], g_ref[...], xhat, r)
        dg_ref[...] += dg

    tile = lambda w: pl.BlockSpec((tb, w), lambda i: (i, 0))
    halo_next = pl.BlockSpec((HALO, CONV_WIDTH), lambda i: (jnp.minimum((i + 1) * (tb // HALO), seq // HALO - 1), 0))
    return _pallas(
        body, name="in_proj_bwd", grid=(n_tiles,),
        in_specs=[tile(Q_WIDTH), tile(128), tile(128), tile(128), tile(128), tile(CONV_WIDTH), tile(CONV_WIDTH), halo_next,
                  tile(CONV_WIDTH), tile(CONV_WIDTH), _resident((CONV_K, CONV_WIDTH)),
                  tile(D_MODEL), tile(D_MODEL), _resident((1, D_MODEL)), _resident(w_in_t.shape), *_rope_specs(tb)],
        out_specs=[tile(IN_COLS), tile(D_MODEL), pl.BlockSpec((1, D_MODEL), lambda i: (0, 0))],
        out_shape=[SDS((seq, IN_COLS), BF16), SDS((seq, D_MODEL), F32), SDS((1, D_MODEL), F32)],
        operands=(dq, dk0, dk1, dv0, dv1, dgb, dy, dy, gc, xin, conv_w, x, dh, g_pre, w_in_t, *rope))


def _wgrad_grid(seq, per_chip, h_rows):
    chips_per_step = 1 if per_chip else N_CHIPS
    m = chips_per_step * 2 * h_rows
    bt = min(seq, WGRAD_TOKEN_TILE if m <= 1024 else WGRAD_TOKEN_TILE // 2)
    return chips_per_step, m, bt, seq // bt


def _wgrad(name, a, b, *, per_chip, h_rows, square_a=False, comm=None, rider=None):
    seq = a.shape[0]
    chips_per_step, m, bt, n_k = _wgrad_grid(seq, per_chip, h_rows)
    a_cols = m if per_chip else a.shape[1]
    a_wide = a.shape[1] > a_cols
    b_wide = b.shape[1] > D_MODEL
    n_ride_in = len(rider.in_specs) if rider else 0
    n_ride_out = len(rider.out_specs) if rider else 0

    def body(a_ref, b_ref, *rest):
        ride_in, g_ref = rest[:n_ride_in], rest[n_ride_in]
        ride_out, acc_ref = rest[n_ride_in + 1:n_ride_in + 1 + n_ride_out], rest[-1]
        k = pl.program_id(1)

        @pl.when(k == 0)
        def _():
            acc_ref[...] = jnp.zeros_like(acc_ref)

        av = a_ref[...]
        if square_a:
            av = (av.astype(F32) * av.astype(F32)).astype(BF16)
        acc_ref[...] += _dot_tn(av, b_ref[...])

        @pl.when(k == n_k - 1)
        def _():
            for cidx in range(chips_per_step):
                for half in range(2):
                    r0 = (2 * cidx + half) * h_rows
                    g_ref[cidx, half] = acc_ref[r0:r0 + h_rows, :]

        if rider:
            rider.body(jnp.logical_and(pl.program_id(0) == 0, k == 0), *ride_in, *ride_out)

    a_spec = pl.BlockSpec((bt, a_cols), (lambda j, k: (k, j)) if a_wide else (lambda j, k: (k, 0)))
    b_spec = pl.BlockSpec((bt, D_MODEL), (lambda j, k: (k, j)) if b_wide else (lambda j, k: (k, 0)))
    g_spec = pl.BlockSpec((chips_per_step, 2, h_rows, D_MODEL), lambda j, k: (j, 0, 0, 0))
    return _pallas(
        body, name=name, grid=(N_CHIPS if per_chip else 1, n_k),
        in_specs=[a_spec, b_spec] + (rider.in_specs if rider else []),
        out_specs=[g_spec] + (rider.out_specs if rider else []),
        out_shape=[SDS((N_CHIPS, 2, h_rows, D_MODEL), F32)] + (rider.out_shape if rider else []),
        scratch=[pltpu.VMEM((m, D_MODEL), F32)], operands=(a, b) + (rider.operands if rider else ()), comm=comm)


def _adamw_math(w, g, m, v):
    m = ADAM_B1 * m + (1.0 - ADAM_B1) * g
    v = ADAM_B2 * v + (1.0 - ADAM_B2) * (g * g)
    m_hat = m / (1.0 - ADAM_B1 ** ADAM_STEP)
    v_hat = v / (1.0 - ADAM_B2 ** ADAM_STEP)
    delta = -ADAM_LR * (m_hat / (jnp.sqrt(v_hat) + ADAM_EPS) + ADAM_WD * w)
    return delta, m, v


def _adamw_rows(name, reduced, w, m, v, rt):
    per_half = reduced.shape[1] // rt

    def body(r_ref, w_ref, m_ref, v_ref, g_out, d_out, m_out, v_out):
        g = r_ref[0]
        g_out[...] = g
        d_out[...], m_out[...], v_out[...] = _adamw_math(w_ref[...], g, m_ref[...], v_ref[...])

    blk = pl.BlockSpec((rt, D_MODEL), lambda h, r: (h * per_half + r, 0))
    return _pallas(
        body, name=name, grid=(2, per_half),
        in_specs=[pl.BlockSpec((1, rt, D_MODEL), lambda h, r: (h, r, 0)), blk, blk, blk],
        out_specs=[blk, blk, blk, blk], out_shape=[SDS(w.shape, F32)] * 4, operands=(reduced, w, m, v))


def _adamw_small(w, g, m, v):
    def body(w_ref, g_ref, m_ref, v_ref, d_out, m_out, v_out):
        d_out[...], m_out[...], v_out[...] = _adamw_math(w_ref[...], g_ref[...], m_ref[...], v_ref[...])

    return pl.pallas_call(body, name="adamw_small", in_specs=[VMEM_WHOLE] * 4, out_specs=[VMEM_WHOLE] * 3,
                          out_shape=[SDS(w.shape, F32)] * 3)(w, g, m, v)


SMALL_VECTORS = ("pre_mix_norm", "post_mix_norm", "pre_mlp_norm", "post_mlp_norm")
SMALL_NAMES = SMALL_VECTORS + ("attn_group_norm", "conv_group_norm", "conv_w", "attn_sinks")


def _pack_small(p):
    rows = [p[n].reshape(1, D_MODEL) for n in SMALL_VECTORS]
    rows.append(jnp.concatenate([p["attn_group_norm"].reshape(1, -1), p["conv_group_norm"].reshape(1, -1)], axis=1))
    cw = p["conv_w"].reshape(CONV_K, -1)
    rows.append(jnp.pad(cw, ((0, 1), (0, CONV_WIDTH - cw.shape[1]))).reshape(2, D_MODEL))
    last = jnp.concatenate([p["attn_sinks"].reshape(1, 8), p.get("loss_sum", jnp.zeros((1, 1), F32))], axis=1)
    rows.append(jnp.pad(last, ((0, 0), (0, D_MODEL - 9))))
    return jnp.concatenate(rows, axis=0)


def _unpack_small(packed, conv_width):
    out = {n: packed[i:i + 1] for i, n in enumerate(SMALL_VECTORS)}
    out["attn_group_norm"] = packed[4:5, :Q_WIDTH]
    out["conv_group_norm"] = packed[4:5, Q_WIDTH:]
    out["conv_w"] = packed[5:7].reshape(4, CONV_WIDTH)[:CONV_K, :conv_width].reshape(1, CONV_K, conv_width)
    out["attn_sinks"] = packed[7:8, :8]
    out["loss_sum"] = packed[7, 8]
    return out


WEIGHT_ORDER = ("pre_mix_norm", "w_in", "conv_w", "attn_sinks", "attn_group_norm", "conv_group_norm", "w_out",
                "post_mix_norm", "pre_mlp_norm", "w_up", "w_down", "post_mlp_norm")


def kernel(x, pre_mix_norm, w_in, conv_w, attn_sinks, attn_group_norm, conv_group_norm, w_out, post_mix_norm, pre_mlp_norm, w_up, w_down, post_mlp_norm, loss_target, m_pre_mix_norm, m_w_in, m_conv_w, m_attn_sinks, m_attn_group_norm, m_conv_group_norm, m_w_out, m_post_mix_norm, m_pre_mlp_norm, m_w_up, m_w_down, m_post_mlp_norm, v_pre_mix_norm, v_w_in, v_conv_w, v_attn_sinks, v_attn_group_norm, v_conv_group_norm, v_w_out, v_post_mix_norm, v_pre_mlp_norm, v_w_up, v_w_down, v_post_mlp_norm):
    w = dict(pre_mix_norm=pre_mix_norm, w_in=w_in, conv_w=conv_w, attn_sinks=attn_sinks, attn_group_norm=attn_group_norm,
             conv_group_norm=conv_group_norm, w_out=w_out, post_mix_norm=post_mix_norm, pre_mlp_norm=pre_mlp_norm,
             w_up=w_up, w_down=w_down, post_mlp_norm=post_mlp_norm)
    m = dict(pre_mix_norm=m_pre_mix_norm, w_in=m_w_in, conv_w=m_conv_w, attn_sinks=m_attn_sinks,
             attn_group_norm=m_attn_group_norm, conv_group_norm=m_conv_group_norm, w_out=m_w_out,
             post_mix_norm=m_post_mix_norm, pre_mlp_norm=m_pre_mlp_norm, w_up=m_w_up, w_down=m_w_down,
             post_mlp_norm=m_post_mlp_norm)
    v = dict(pre_mix_norm=v_pre_mix_norm, w_in=v_w_in, conv_w=v_conv_w, attn_sinks=v_attn_sinks,
             attn_group_norm=v_attn_group_norm, conv_group_norm=v_conv_group_norm, w_out=v_w_out,
             post_mix_norm=v_post_mix_norm, pre_mlp_norm=v_pre_mlp_norm, w_up=v_w_up, w_down=v_w_down,
             post_mlp_norm=v_post_mlp_norm)
    core = lax.axis_index("c").astype(jnp.int32).reshape(1)
    chip = 2 * lax.axis_index("x") + lax.axis_index("y")
    local_conv = conv_w.shape[2]
    xs, target = x[0], loss_target[0]
    rope = _rope_inputs(xs.shape[0])

    hb_up, hb_down, hb_out, hb_in = _cast_halves(core, w_up[0], w_down[0], w_out[0], w_in[0].T)
    conv_pad = jnp.pad(conv_w[0], ((0, 8 - CONV_K), (0, 0)))
    wf_in, conv_all = _gather_whole(hb_in, conv_pad)
    conv_full = conv_all[:, :CONV_K, :].transpose(1, 0, 2).reshape(CONV_K, CONV_WIDTH)

    *proj, wf_up, wf_out = _in_proj(xs, pre_mix_norm, wf_in, rope, comm=_merge(_gather_first(hb_up), _gather_first(hb_out)))
    q, kd0, kd1, vd0, vd1, gb, gc, xin, hn = proj
    attn, wf_up, wf_out, wf_down = _attention_fwd(
        q, kd0, kd1, vd0, vd1, attn_sinks,
        comm=_merge(_gather_second(wf_up), _gather_second(wf_out), _gather_first(hb_down)))
    mix, mixed, wf_down = _mix_out(attn, gb, gc, xin, conv_full, attn_group_norm, conv_group_norm, wf_out,
                                   comm=_gather_second(wf_down))
    up, hn2, dout, dmlp, loss_sum, dg_post_mlp = _mlp_loss(xs, mix, target, post_mix_norm, pre_mlp_norm, post_mlp_norm,
                                                           wf_up, wf_down)

    dup, dh, dmix, dg_pre_mlp, dg_post_mix = _mlp_bwd(dmlp, up, xs, dout, mix, pre_mlp_norm, post_mix_norm, wf_up, wf_down)
    n_k = _wgrad_grid(xs.shape[0], True, H_DOWN)[3]
    g_down, dattn, dgb, dy, dg_attn, dg_conv, dconv_w = _wgrad(
        "wgrad_down", up, dmlp, per_chip=True, h_rows=H_DOWN, square_a=True,
        rider=_mix_bwd(dmix, attn, gb, gc, xin, conv_full, attn_group_norm, conv_group_norm, wf_out, n_k))
    g_up, got_down = _wgrad("wgrad_up", hn2, dup, per_chip=True, h_rows=H_UP, comm=_pair_send(g_down))
    p_down = _pair_sum("pair_sum_down", core, g_down, got_down)
    g_out, got_up = _wgrad("wgrad_out", mixed, dmix, per_chip=False, h_rows=H_OUT, comm=_pair_send(g_up))
    p_up = _pair_sum("pair_sum_up", core, g_up, got_up)
    dq, dk0, dk1, dv0, dv1, dsink, ex_down, ex_up, got_out = _attention_bwd(
        q, dattn, attn, kd0, kd1, vd0, vd1, attn_sinks,
        comm=_merge(_chip_exchange(p_down), _chip_exchange(p_up), _pair_send(g_out)))
    p_out = _pair_sum("pair_sum_out", core, g_out, got_out)
    dproj, grad_x, dg_pre_mix = _in_proj_bwd(dq, dk0, dk1, dv0, dv1, dgb, dy, gc, xin, conv_full, xs, dh, pre_mix_norm,
                                             wf_in, rope)
    g_in, ex_out = _wgrad("wgrad_in", dproj, hn, per_chip=False, h_rows=H_IN, comm=_chip_exchange(p_out))
    small = dict(pre_mix_norm=dg_pre_mix, conv_w=dconv_w, attn_sinks=dsink[:, :8], attn_group_norm=dg_attn,
                 conv_group_norm=dg_conv, post_mix_norm=dg_post_mix, pre_mlp_norm=dg_pre_mlp, post_mlp_norm=dg_post_mlp,
                 loss_sum=loss_sum)
    r_down, r_up, r_out, r_in, small_total = _tail_reduce(g_in, [ex_down, ex_up, ex_out], _pack_small(small))

    out_g, out_d, out_m, out_v = {}, {}, {}, {}
    out_g["w_up"], out_d["w_up"], out_m["w_up"], out_v["w_up"] = _adamw_rows(
        "adamw_up", r_up, w_up[0], m_w_up[0], v_w_up[0], 256)
    out_g["w_down"], out_d["w_down"], out_m["w_down"], out_v["w_down"] = _adamw_rows(
        "adamw_down", r_down, w_down[0], m_w_down[0], v_w_down[0], 256)
    out_g["w_out"], out_d["w_out"], out_m["w_out"], out_v["w_out"] = _adamw_rows(
        "adamw_out", r_out, w_out[0], m_w_out[0], v_w_out[0], H_OUT)
    in_t = _adamw_rows("adamw_in", r_in, w_in[0].T, m_w_in[0].T, v_w_in[0].T, H_IN)
    out_g["w_in"], out_d["w_in"], out_m["w_in"], out_v["w_in"] = [t.T for t in in_t]

    small_sum = _unpack_small(small_total, CONV_WIDTH)
    loss = small_sum["loss_sum"] * (0.5 / D_MODEL)
    small_sum["conv_w"] = lax.dynamic_slice_in_dim(small_sum["conv_w"], chip * local_conv, local_conv, axis=2)
    packed = [_pack_small({n: t[n] for n in SMALL_NAMES}) for t in (w, small_sum, m, v)]
    small_d, small_m, small_v = [_unpack_small(t, local_conv) for t in _adamw_small(*packed)]
    for n in SMALL_NAMES:
        out_g[n], out_d[n], out_m[n], out_v[n] = small_sum[n], small_d[n], small_m[n], small_v[n]

    def shaped(d):
        return [d[n].reshape(w[n].shape) for n in WEIGHT_ORDER]

    return (loss, grad_x[None], *shaped(out_g), *shaped(out_d), *shaped(out_m), *shaped(out_v))
```

```python
import math
from typing import Callable, NamedTuple

import jax
import jax.numpy as jnp
import numpy as np
from jax import lax
from jax.experimental import pallas as pl
from jax.experimental.pallas import tpu as pltpu

F32 = jnp.float32
BF16 = jnp.bfloat16

D_MODEL = 1024
HEAD_DIM = 64
Q_WIDTH = 512
KV_WIDTH = 128
CONV_WIDTH = 512
CONV_K = 3
D_FF = 4096
IN_COLS = 2304
QBLOCK = 128
ROT_DIM = 16
ROPE_THETA = 500000.0
NORM_EPS = 1e-6
NEG_INF = -1e30
N_CHIPS = 4

ADAM_LR = 0.001
ADAM_B1 = 0.9
ADAM_B2 = 0.999
ADAM_EPS = 1e-08
ADAM_WD = 0.01
ADAM_STEP = 10

H_UP, H_DOWN, H_OUT, H_IN = 512, 512, 128, 288
UP_SPLIT = 384

TOKEN_TILE = 512
MLP_BWD_TOKEN_TILE = 512
MLP_BWD_SUB_TILE = 256
ATTN_FWD_BLOCKS = 4
ATTN_BWD_BLOCKS = 2
WGRAD_TOKEN_TILE = 2048
VMEM_LIMIT_V7X = 56 * 1024 * 1024

MESH = pl.DeviceIdType.MESH
ANY = pl.BlockSpec(memory_space=pl.ANY)
VMEM_WHOLE = pl.BlockSpec(memory_space=pltpu.VMEM)
SDS = jax.ShapeDtypeStruct


def _resident(shape):
    zeros = (0,) * len(shape)
    return pl.BlockSpec(shape, lambda *_: zeros, pipeline_mode=pl.Buffered(1))


def _rms(v):
    return lax.rsqrt(jnp.mean(v * v, axis=-1, keepdims=True) + NORM_EPS)


def _norm_bwd(dy, gain, vhat, rstd):
    t = dy * gain
    return rstd * (t - vhat * jnp.mean(t * vhat, axis=-1, keepdims=True))


def _colsum(v):
    return jnp.sum(v, axis=0, keepdims=True)


def _dot_nt(a, b):
    return lax.dot_general(a, b, (((1,), (1,)), ((), ())), preferred_element_type=F32)


def _dot_tn(a, b):
    return lax.dot_general(a, b, (((0,), (0,)), ((), ())), preferred_element_type=F32)


def _dot(a, b):
    return jnp.dot(a, b, preferred_element_type=F32)


def _chip_block(w_ref, chip):
    both = w_ref[pl.ds(2 * chip, 2)]
    return both.reshape(2 * both.shape[1], both.shape[2])


def _lane_lt64(shape):
    return lax.broadcasted_iota(jnp.int32, shape, 1) < HEAD_DIM


class _Comm(NamedTuple):
    operands: tuple
    out_shapes: tuple
    aliases: dict
    n_remote: int
    n_local: int
    plan: Callable
    after: Callable = None


def _merge(*comms):
    assert all(cm.after is None for cm in comms)
    operands, out_shapes, aliases, parts = [], [], {}, []
    n_remote = n_local = 0
    for cm in comms:
        parts.append((len(operands), len(out_shapes), n_remote, n_local, cm))
        for k, v in cm.aliases.items():
            aliases[len(operands) + k] = len(out_shapes) + v
        operands += cm.operands
        out_shapes += cm.out_shapes
        n_remote += cm.n_remote
        n_local += cm.n_local

    def plan(ins, outs, send, recv, loc):
        sends, recvs, locs = [], [], []
        for i0, o0, r0, l0, cm in parts:
            s, r, l = cm.plan(ins[i0:i0 + len(cm.operands)], outs[o0:o0 + len(cm.out_shapes)],
                              lambda k, r0=r0: send(r0 + k), lambda k, r0=r0: recv(r0 + k), lambda k, l0=l0: loc(l0 + k))
            sends, recvs, locs = sends + s, recvs + r, locs + l
        return sends, recvs, locs

    return _Comm(tuple(operands), tuple(out_shapes), aliases, n_remote, n_local, plan)


def _sem_scratch(comm):
    return [pltpu.SemaphoreType.DMA((max(comm.n_remote, 1),)), pltpu.SemaphoreType.DMA((max(comm.n_remote, 1),)),
            pltpu.SemaphoreType.DMA((max(comm.n_local, 1),))]


def _pallas(body, *, name, grid, in_specs, out_specs, out_shape, operands, scratch=(), comm=None):
    params = pltpu.CompilerParams(dimension_semantics=("arbitrary",) * len(grid), vmem_limit_bytes=VMEM_LIMIT_V7X)
    if comm is None:
        return pl.pallas_call(body, name=name, grid=grid, in_specs=in_specs, out_specs=out_specs, out_shape=out_shape,
                              scratch_shapes=list(scratch), compiler_params=params)(*operands)
    n_in, n_out, n_scr = len(in_specs), len(out_specs), len(scratch)
    c_in, c_out = len(comm.operands), len(comm.out_shapes)

    def with_comm(*refs):
        ins, c_ins = refs[:n_in], refs[n_in:n_in + c_in]
        o0 = n_in + c_in
        outs, c_outs = refs[o0:o0 + n_out], refs[o0 + n_out:o0 + n_out + c_out]
        s0 = o0 + n_out + c_out
        scr = refs[s0:s0 + n_scr]
        send_sems, recv_sems, local_sems = refs[s0 + n_scr:]
        first = last = None
        for axis, size in enumerate(grid):
            at_start, at_end = pl.program_id(axis) == 0, pl.program_id(axis) == size - 1
            first = at_start if first is None else jnp.logical_and(first, at_start)
            last = at_end if last is None else jnp.logical_and(last, at_end)

        def copies():
            return comm.plan(c_ins, c_outs, lambda k: send_sems.at[k], lambda k: recv_sems.at[k],
                             lambda k: local_sems.at[k])

        @pl.when(first)
        def _():
            sends, _, locs = copies()
            for cp in sends + locs:
                cp.start()

        body(*ins, *outs, *scr)

        @pl.when(last)
        def _():
            sends, recvs, locs = copies()
            for cp in recvs:
                cp.wait_recv()
            for cp in sends:
                cp.wait_send()
            for cp in locs:
                cp.wait()
            if comm.after is not None:
                sends, recvs, _ = comm.after(c_ins, c_outs, lambda k: send_sems.at[k], lambda k: recv_sems.at[k],
                                             lambda k: local_sems.at[k])
                for cp in sends:
                    cp.start()
                for cp in recvs:
                    cp.wait_recv()
                for cp in sends:
                    cp.wait_send()

    return pl.pallas_call(
        with_comm, name=name, grid=grid,
        in_specs=list(in_specs) + [ANY] * c_in, out_specs=list(out_specs) + [ANY] * c_out,
        out_shape=list(out_shape) + list(comm.out_shapes),
        scratch_shapes=list(scratch) + _sem_scratch(comm),
        input_output_aliases={n_in + k: n_out + v for k, v in comm.aliases.items()},
        compiler_params=params)(*operands, *comm.operands)


def _place():
    return lax.axis_index("x"), lax.axis_index("y"), lax.axis_index("c")


def _other_chips(x, y):
    return [(1 - x, y), (x, 1 - y), (1 - x, 1 - y)]


def _slot(px, py, pc):
    return 4 * px + 2 * py + pc


def _remote(src, dst, send_sem, recv_sem, to):
    return pltpu.make_async_remote_copy(src_ref=src, dst_ref=dst, send_sem=send_sem, recv_sem=recv_sem,
                                        device_id=to, device_id_type=MESH)


def _gather_legs(half_block, so_far, first=None, second=None, second_after=None):
    has_block, has_buffer = half_block is not None, so_far is not None
    shape = so_far.shape if has_buffer else (2 * N_CHIPS,) + half_block.shape
    dtype = so_far.dtype if has_buffer else half_block.dtype

    def forward(rows, base, ins, outs, send, recv):
        src = ins[-1] if has_buffer else outs[0]
        full = outs[0]
        x, y, c = _place()
        chips = _other_chips(x, y)
        span = pl.ds(*rows)
        sends = [_remote(src.at[_slot(*chip, c), span], full.at[_slot(*chip, c), span], send(base + j), recv(base + j),
                         (x, y, 1 - c)) for j, chip in enumerate(chips)]
        recvs = [_remote(src.at[_slot(*chip, 1 - c), span], full.at[_slot(*chip, 1 - c), span], send(base + j),
                         recv(base + j), (x, y, 1 - c)) for j, chip in enumerate(chips)]
        return sends, recvs

    def plan(ins, outs, send, recv, loc):
        sends, recvs, locs = [], [], []
        x, y, c = _place()
        if first is not None:
            blk, full, span = ins[0].at[pl.ds(*first)], outs[0], pl.ds(*first)
            chips = _other_chips(x, y)
            mine = full.at[_slot(x, y, c), span]
            sends += [_remote(blk, mine, send(0), recv(0), (x, y, 1 - c))]
            sends += [_remote(blk, mine, send(1 + j), recv(1 + j), (*chip, c)) for j, chip in enumerate(chips)]
            recvs += [_remote(blk, full.at[_slot(x, y, 1 - c), span], send(0), recv(0), (x, y, 1 - c))]
            recvs += [_remote(blk, full.at[_slot(*chip, c), span], send(1 + j), recv(1 + j), (*chip, c))
                      for j, chip in enumerate(chips)]
            locs += [pltpu.make_async_copy(blk, mine, loc(0))]
        if second is not None:
            s, r = forward(second, 4, ins, outs, send, recv)
            sends, recvs = sends + s, recvs + r
        return sends, recvs, locs

    def after(ins, outs, send, recv, loc):
        s, r = forward(second_after, 7, ins, outs, send, recv)
        return s, r, []

    operands = ((half_block,) if has_block else ()) + ((so_far,) if has_buffer else ())
    return _Comm(operands, (SDS(shape, dtype),), {len(operands) - 1: 0} if has_buffer else {}, 10, 1, plan,
                 after if second_after is not None else None)


def _gather_small(block):
    def plan(ins, outs, send, recv, loc):
        (blk,), (full,) = ins, outs
        x, y, c = _place()
        chips = _other_chips(x, y)
        sends = [_remote(blk, full.at[2 * x + y], send(j), recv(j), (*chip, c)) for j, chip in enumerate(chips)]
        recvs = [_remote(blk, full.at[2 * chip[0] + chip[1]], send(j), recv(j), (*chip, c))
                 for j, chip in enumerate(chips)]
        return sends, recvs, [pltpu.make_async_copy(blk, full.at[2 * x + y], loc(0))]

    return _Comm((block,), (SDS((N_CHIPS,) + block.shape, block.dtype),), {}, 3, 1, plan)


def _pair_send(grads):
    def plan(ins, outs, send, recv, loc):
        (g,), (got,) = ins, outs
        x, y, c = _place()
        copies = [_remote(g.at[j, 1 - c], got.at[j], send(j), recv(j), (x, y, 1 - c)) for j in range(N_CHIPS)]
        return copies, copies, []

    shape = (grads.shape[0],) + grads.shape[2:]
    return _Comm((grads,), (SDS(shape, grads.dtype),), {}, N_CHIPS, 0, plan)


def _chip_exchange(partial):
    def plan(ins, outs, send, recv, loc):
        (p,), (got,) = ins, outs
        x, y, c = _place()
        my_chip = 2 * x + y
        chips = _other_chips(x, y)
        sends = [_remote(p.at[2 * chip[0] + chip[1]], got.at[my_chip], send(j), recv(j), (*chip, c))
                 for j, chip in enumerate(chips)]
        recvs = [_remote(p.at[my_chip], got.at[2 * chip[0] + chip[1]], send(j), recv(j), (*chip, c))
                 for j, chip in enumerate(chips)]
        return sends, recvs, [pltpu.make_async_copy(p.at[my_chip], got.at[my_chip], loc(0))]

    return _Comm((partial,), (SDS(partial.shape, partial.dtype),), {}, 3, 1, plan)


def _pair_sum(name, core, grads, received):
    h = grads.shape[2]

    def body(core_ref, g_ref, r_ref, o_ref):
        o_ref[...] = (g_ref[0] + r_ref[...]).astype(BF16)

    return pl.pallas_call(
        body, name=name,
        grid_spec=pltpu.PrefetchScalarGridSpec(
            num_scalar_prefetch=1, grid=(N_CHIPS,),
            in_specs=[pl.BlockSpec((1, 1, h, D_MODEL), lambda j, core_ref: (j, core_ref[0], 0, 0)),
                      pl.BlockSpec((1, h, D_MODEL), lambda j, core_ref: (j, 0, 0))],
            out_specs=pl.BlockSpec((1, h, D_MODEL), lambda j, core_ref: (j, 0, 0))),
        out_shape=SDS((N_CHIPS, h, D_MODEL), BF16),
        compiler_params=pltpu.CompilerParams(dimension_semantics=("arbitrary",), vmem_limit_bytes=VMEM_LIMIT_V7X),
    )(core, grads, received)


SMALL_ROWS = 8


def _sum_blocks(ref):
    return (ref[0].astype(F32) + ref[1].astype(F32)) + (ref[2].astype(F32) + ref[3].astype(F32))


def _tail_reduce(last_grads, exchanged, small):
    n = len(exchanged)
    h = last_grads.shape[2]

    def body(*refs):
        g_ref, ex, small_ref = refs[0], refs[1:1 + n], refs[1 + n]
        o0 = 2 + n
        out, out_last, small_out = refs[o0:o0 + n], refs[o0 + n], refs[o0 + n + 1]
        s0 = o0 + n + 2
        halves, half_last = refs[s0:s0 + n], refs[s0 + n]
        own, got, part, exch, small_buf = refs[s0 + n + 1:s0 + n + 6]
        pair_send, pair_recv, chip_send, chip_recv, share_send, share_recv, small_send, small_recv, local_sems = refs[s0 + n + 6:]
        x, y, c = _place()
        sibling = (x, y, 1 - c)
        my_chip, me = 2 * x + y, _slot(x, y, c)
        chips = _other_chips(x, y)

        to_sibling = [_remote(g_ref.at[j, 1 - c], got.at[j], pair_send.at[j], pair_recv.at[j], sibling)
                      for j in range(N_CHIPS)]
        load_own = [pltpu.make_async_copy(g_ref.at[j, c], own.at[j], local_sems.at[j]) for j in range(N_CHIPS)]
        for cp in to_sibling + load_own:
            cp.start()

        small_buf[me] = small_ref[...]
        small_copies = []
        for mask in range(1, 8):
            peer = (x ^ (mask >> 2), y ^ ((mask >> 1) & 1), c ^ (mask & 1))
            small_copies.append(_remote(small_ref, small_buf.at[me], small_send.at[mask - 1], small_recv.at[mask - 1], peer))
        for cp in small_copies:
            cp.start()

        def share(k, half_ref, out_ref):
            keep = pltpu.make_async_copy(half_ref, out_ref.at[c], local_sems.at[N_CHIPS + k])
            give = _remote(half_ref, out_ref.at[c], share_send.at[k], share_recv.at[k], sibling)
            take = _remote(half_ref, out_ref.at[1 - c], share_send.at[k], share_recv.at[k], sibling)
            keep.start()
            give.start()
            return keep, give, take

        shares = []
        for k in range(n):
            halves[k][...] = _sum_blocks(ex[k])
            shares.append(share(k, halves[k], out[k]))

        for cp in to_sibling:
            cp.wait_recv()
        for cp in load_own:
            cp.wait()
        part[...] = (own[...] + got[...]).astype(BF16)
        exch[my_chip] = part[my_chip]
        to_chips = [_remote(part.at[2 * chip[0] + chip[1]], exch.at[my_chip], chip_send.at[j], chip_recv.at[j], (*chip, c))
                    for j, chip in enumerate(chips)]
        from_chips = [_remote(part.at[my_chip], exch.at[2 * chip[0] + chip[1]], chip_send.at[j], chip_recv.at[j], (*chip, c))
                      for j, chip in enumerate(chips)]
        for cp in to_chips:
            cp.start()

        for cp in small_copies:
            cp.wait_recv()
        total = small_buf[0]
        for d in range(1, 8):
            total = total + small_buf[d]
        small_out[...] = total

        for cp in from_chips:
            cp.wait_recv()
        half_last[...] = _sum_blocks(exch)
        shares.append(share(n, half_last, out_last))

        for keep, give, take in shares:
            take.wait_recv()
            give.wait_send()
            keep.wait()
        for cp in to_sibling + to_chips + small_copies:
            cp.wait_send()

    blocks = (N_CHIPS, h, D_MODEL)
    return pl.pallas_call(
        body, name="tail_reduce",
        in_specs=[ANY] + [VMEM_WHOLE] * (n + 1), out_specs=[ANY] * (n + 1) + [VMEM_WHOLE],
        out_shape=[SDS((2,) + e.shape[1:], F32) for e in exchanged] + [SDS((2, h, D_MODEL), F32), SDS(small.shape, F32)],
        scratch_shapes=[pltpu.VMEM(e.shape[1:], F32) for e in exchanged] + [pltpu.VMEM((h, D_MODEL), F32)]
                       + [pltpu.VMEM(blocks, F32), pltpu.VMEM(blocks, F32), pltpu.VMEM(blocks, BF16), pltpu.VMEM(blocks, BF16),
                          pltpu.VMEM((8,) + small.shape, F32)]
                       + [pltpu.SemaphoreType.DMA((N_CHIPS,)), pltpu.SemaphoreType.DMA((N_CHIPS,)),
                          pltpu.SemaphoreType.DMA((3,)), pltpu.SemaphoreType.DMA((3,)),
                          pltpu.SemaphoreType.DMA((n + 1,)), pltpu.SemaphoreType.DMA((n + 1,)),
                          pltpu.SemaphoreType.DMA((7,)), pltpu.SemaphoreType.DMA((7,)),
                          pltpu.SemaphoreType.DMA((N_CHIPS + n + 1,))],
        compiler_params=pltpu.CompilerParams(vmem_limit_bytes=VMEM_LIMIT_V7X),
    )(last_grads, *exchanged, small)


def _rope_expansion():
    half = ROT_DIM // 2
    expand = np.zeros((2 * half, 3 * 128), np.float32)
    const = np.zeros((1, 3 * 128), np.float32)
    for lane in range(128):
        d = lane % HEAD_DIM
        if d < ROT_DIM:
            expand[d % half, lane] = 1.0
        else:
            const[0, lane] = 1.0
        if d < half:
            expand[half + d, 128 + lane] = -1.0
        elif d < ROT_DIM:
            expand[half + d - half, 256 + lane] = 1.0
    return expand, const


ROPE_PIECES = 3 * ROT_DIM


def _rope_inputs(seq):
    pos = jnp.arange(seq, dtype=F32)
    inv_freq = ROPE_THETA ** (-jnp.arange(0, ROT_DIM, 2, dtype=F32) / ROT_DIM)
    ang = pos[:, None] * inv_freq[None, :]
    cs = jnp.concatenate([jnp.cos(ang), jnp.sin(ang)], axis=1)
    hi = lax.reduce_precision(cs, 8, 7)
    mid = lax.reduce_precision(cs - hi, 8, 7)
    low = cs - hi - mid
    expand, const = _rope_expansion()
    pieces = jnp.concatenate([hi, mid, low], axis=1).astype(BF16)
    return pieces, jnp.asarray(np.concatenate([expand] * 3, axis=0), BF16), jnp.asarray(const)


def _rope_specs(tb):
    return [pl.BlockSpec((tb, ROPE_PIECES), lambda i: (i, 0)), _resident((ROPE_PIECES, 3 * 128)), _resident((1, 3 * 128))]


def _rope_tile(pieces_ref, expand_ref, const_ref):
    tables = _dot(pieces_ref[...], expand_ref[...]) + const_ref[...]
    return tables[:, 0:128], tables[:, 128:256], tables[:, 256:384]


def _rope(t, c, sa, sb):
    half = ROT_DIM // 2
    return t * c + pltpu.roll(t, 128 - half, 1) * sa + pltpu.roll(t, half, 1) * sb


def _rope_transposed(dt, c, sa, sb):
    half = ROT_DIM // 2
    return dt * c + pltpu.roll(dt * sa, half, 1) + pltpu.roll(dt * sb, 128 - half, 1)


def _cast_halves(core, w_up, w_down, w_out, w_in_t):
    def body(core_ref, up_ref, down_ref, out_ref, in_ref, up_o, down_o, out_o, in_o):
        up_o[...] = up_ref[...].astype(BF16)
        down_o[...] = down_ref[...].astype(BF16)
        out_o[...] = out_ref[...].astype(BF16)
        in_o[...] = in_ref[...].astype(BF16)

    half = lambda rows: pl.BlockSpec((rows, D_MODEL), lambda i, core_ref: (core_ref[0], 0))
    whole = lambda rows: pl.BlockSpec((rows, D_MODEL), lambda i, core_ref: (0, 0))
    rows = (H_UP, H_DOWN, H_OUT, H_IN)
    return pl.pallas_call(
        body, name="cast_halves",
        grid_spec=pltpu.PrefetchScalarGridSpec(
            num_scalar_prefetch=1, grid=(1,), in_specs=[half(r) for r in rows], out_specs=[whole(r) for r in rows]),
        out_shape=[SDS((r, D_MODEL), BF16) for r in rows],
        compiler_params=pltpu.CompilerParams(dimension_semantics=("arbitrary",), vmem_limit_bytes=VMEM_LIMIT_V7X),
    )(core, w_up, w_down, w_out, w_in_t)


def _pre_norm(x, g_pre, comm=None):
    seq = x.shape[0]
    tb = TOKEN_TILE

    def body(x_ref, g_ref, hn_ref):
        xv = x_ref[...]
        hn_ref[...] = (xv * _rms(xv) * g_ref[...]).astype(BF16)

    tile = pl.BlockSpec((tb, D_MODEL), lambda i: (i, 0))
    return _pallas(body, name="pre_norm", grid=(seq // tb,), in_specs=[tile, _resident((1, D_MODEL))], out_specs=[tile],
                   out_shape=[SDS((seq, D_MODEL), BF16)], operands=(x, g_pre), comm=comm)


def _in_proj(hn, w_in_t, rope, comm=None):
    seq = hn.shape[0]
    tb = TOKEN_TILE

    def body(hn_ref, w_ref, c_ref, sa_ref, sb_ref,
             q_ref, kd0_ref, kd1_ref, vd0_ref, vd1_ref, gb_ref, gc_ref, xin_ref):
        proj = _dot_nt(hn_ref[...], w_ref[...].reshape(IN_COLS, D_MODEL))
        c, sa, sb = _rope_tile(c_ref, sa_ref, sb_ref)
        scale = 1.0 / math.sqrt(HEAD_DIM)
        for p in range(Q_WIDTH // 128):
            q_ref[:, 128 * p:128 * (p + 1)] = (_rope(proj[:, 128 * p:128 * (p + 1)], c, sa, sb) * scale).astype(BF16)
        k = _rope(proj[:, Q_WIDTH:Q_WIDTH + KV_WIDTH], c, sa, sb)
        v = proj[:, Q_WIDTH + KV_WIDTH:Q_WIDTH + 2 * KV_WIDTH]
        low = _lane_lt64(k.shape)
        k_sw, v_sw = pltpu.roll(k, HEAD_DIM, 1), pltpu.roll(v, HEAD_DIM, 1)
        kd0_ref[...] = jnp.where(low, k, k_sw).astype(BF16)
        kd1_ref[...] = jnp.where(low, k_sw, k).astype(BF16)
        vd0_ref[...] = jnp.where(low, v, v_sw).astype(BF16)
        vd1_ref[...] = jnp.where(low, v_sw, v).astype(BF16)
        base = Q_WIDTH + 2 * KV_WIDTH
        gb_ref[...] = proj[:, base:base + CONV_WIDTH].astype(BF16)
        gc_ref[...] = proj[:, base + CONV_WIDTH:base + 2 * CONV_WIDTH].astype(BF16)
        xin_ref[...] = proj[:, base + 2 * CONV_WIDTH:base + 3 * CONV_WIDTH].astype(BF16)

    tile = lambda w: pl.BlockSpec((tb, w), lambda i: (i, 0))
    return _pallas(
        body, name="in_proj", grid=(seq // tb,),
        in_specs=[tile(D_MODEL), _resident(w_in_t.shape), *_rope_specs(tb)],
        out_specs=[tile(Q_WIDTH), tile(128), tile(128), tile(128), tile(128),
                   tile(CONV_WIDTH), tile(CONV_WIDTH), tile(CONV_WIDTH)],
        out_shape=[SDS((seq, Q_WIDTH), BF16)] + [SDS((seq, 128), BF16)] * 4 + [SDS((seq, CONV_WIDTH), BF16)] * 3,
        operands=(hn, w_in_t, *rope), comm=comm)


def _attn_valid(i):
    shape = (4 * QBLOCK, 2 * QBLOCK)
    row = lax.broadcasted_iota(jnp.int32, shape, 0)
    col = lax.broadcasted_iota(jnp.int32, shape, 1)
    qi = row & (QBLOCK - 1)
    return (col > qi) & (col <= qi + QBLOCK) & ((col >= QBLOCK) | (i > 0))


def _stack_heads(pair0, pair1):
    low = _lane_lt64(pair0.shape)
    zero = jnp.zeros_like(pair0)
    return jnp.concatenate([jnp.where(low, pair0, zero), jnp.where(low, zero, pair0),
                            jnp.where(low, pair1, zero), jnp.where(low, zero, pair1)], axis=0)


def _unstack_heads(stacked):
    low = _lane_lt64((QBLOCK, 128))
    pair0 = jnp.where(low, stacked[0:QBLOCK], stacked[QBLOCK:2 * QBLOCK])
    pair1 = jnp.where(low, stacked[2 * QBLOCK:3 * QBLOCK], stacked[3 * QBLOCK:4 * QBLOCK])
    return pair0, pair1


def _sink_column(sink_ref, kv_head):
    row = lax.broadcasted_iota(jnp.int32, (4 * QBLOCK, 1), 0)
    s = [sink_ref[0, 4 * kv_head + j] for j in range(4)]
    return jnp.where(row < QBLOCK, s[0], jnp.where(row < 2 * QBLOCK, s[1], jnp.where(row < 3 * QBLOCK, s[2], s[3])))


def _band(ref, i):
    prev = pl.multiple_of(jnp.maximum(i - 1, 0) * QBLOCK, QBLOCK)
    own = pl.multiple_of(i * QBLOCK, QBLOCK)
    return jnp.concatenate([ref[pl.ds(prev, QBLOCK), :], ref[pl.ds(own, QBLOCK), :]], axis=0), prev, own


def _softmax_with_sink(s, sink_col):
    m = jnp.maximum(jnp.max(s, axis=-1, keepdims=True), sink_col)
    p = jnp.exp(s - m)
    e_sink = jnp.exp(sink_col - m)
    inv_l = 1.0 / (jnp.sum(p, axis=-1, keepdims=True) + e_sink)
    return p, e_sink, inv_l


def _attention_fwd(q, kd0, kd1, vd0, vd1, sinks, comm=None):
    seq = q.shape[0]

    nb = ATTN_FWD_BLOCKS

    def body(sink_ref, q_ref, kd0_ref, kd1_ref, vd0_ref, vd1_ref, o_ref):
        for b in range(nb):
            i = pl.program_id(0) * nb + b
            rows = slice(QBLOCK * b, QBLOCK * (b + 1))
            valid = _attn_valid(i)
            for kv_head, (k_ref, v_ref) in enumerate(((kd0_ref, vd0_ref), (kd1_ref, vd1_ref))):
                kband, _, _ = _band(k_ref, i)
                vband, _, _ = _band(v_ref, i)
                base = 256 * kv_head
                qm = _stack_heads(q_ref[rows, base:base + 128], q_ref[rows, base + 128:base + 256])
                s = jnp.where(valid, _dot_nt(qm, kband), NEG_INF)
                p, _, inv_l = _softmax_with_sink(s, _sink_column(sink_ref, kv_head))
                o = _dot(p.astype(BF16), vband) * inv_l
                pair0, pair1 = _unstack_heads(o)
                o_ref[rows, base:base + 128] = pair0.astype(BF16)
                o_ref[rows, base + 128:base + 256] = pair1.astype(BF16)

    blk = pl.BlockSpec((nb * QBLOCK, Q_WIDTH), lambda i: (i, 0))
    full = _resident((seq, 128))
    return _pallas(
        body, name="attention_fwd", grid=(seq // (nb * QBLOCK),),
        in_specs=[pl.BlockSpec(memory_space=pltpu.SMEM), blk, full, full, full, full],
        out_specs=[blk], out_shape=[SDS((seq, Q_WIDTH), BF16)],
        operands=(sinks, q, kd0, kd1, vd0, vd1), comm=comm)


HALO = 16


def _conv_parts(gc, xin, gc_halo, xin_halo, conv_w, first):
    tb = gc.shape[0]
    u = gc.astype(F32) * xin.astype(F32)
    u_halo = jnp.where(first, 0.0, gc_halo.astype(F32) * xin_halo.astype(F32))
    ext = jnp.concatenate([u_halo, u], axis=0)
    u1 = pltpu.roll(ext, 1, 0)[HALO:HALO + tb]
    u2 = pltpu.roll(ext, 2, 0)[HALO:HALO + tb]
    y = conv_w[0:1, :] * u2 + conv_w[1:2, :] * u1 + conv_w[2:3, :] * u
    return u, u1, u2, y


def _halo_prev(tb, w):
    return pl.BlockSpec((HALO, w), lambda i: (jnp.maximum(i * (tb // HALO) - 1, 0), 0))


def _residual_mid(x, mix, g_post_mix):
    mix_f = mix.astype(F32)
    return x + mix_f * _rms(mix_f) * g_post_mix


def _mix_out(attn, gb, gc, xin, conv_w, g_attn, g_conv, w_out, comm=None):
    seq = attn.shape[0]
    tb = TOKEN_TILE

    def body(a_ref, gb_ref, gc_ref, xin_ref, gch_ref, xinh_ref, cw_ref, ga_ref, gcn_ref, w_ref, mix_ref, mixed_ref):
        first = pl.program_id(0) == 0
        _, _, _, y = _conv_parts(gc_ref[...], xin_ref[...], gch_ref[...], xinh_ref[...], cw_ref[...], first)
        conv = gb_ref[...].astype(F32) * y
        a = a_ref[...].astype(F32)
        mixed_ref[:, 0:Q_WIDTH] = (a * _rms(a) * ga_ref[...]).astype(BF16)
        mixed_ref[:, Q_WIDTH:] = (conv * _rms(conv) * gcn_ref[...]).astype(BF16)
        mix_ref[...] = _dot(mixed_ref[...], w_ref[...].reshape(D_MODEL, D_MODEL)).astype(BF16)

    tile = lambda w: pl.BlockSpec((tb, w), lambda i: (i, 0))
    return _pallas(
        body, name="mix_out", grid=(seq // tb,),
        in_specs=[tile(Q_WIDTH), tile(CONV_WIDTH), tile(CONV_WIDTH), tile(CONV_WIDTH),
                  _halo_prev(tb, CONV_WIDTH), _halo_prev(tb, CONV_WIDTH),
                  _resident((CONV_K, CONV_WIDTH)), _resident((1, Q_WIDTH)), _resident((1, CONV_WIDTH)),
                  _resident(w_out.shape)],
        out_specs=[tile(D_MODEL), tile(D_MODEL)],
        out_shape=[SDS((seq, D_MODEL), BF16), SDS((seq, D_MODEL), BF16)],
        operands=(attn, gb, gc, xin, gc, xin, conv_w, g_attn, g_conv, w_out), comm=comm)


def _mlp_loss(x, mix, target, g_post_mix, g_pre_mlp, g_post_mlp, w_up, w_down):
    seq = x.shape[0]
    tb = TOKEN_TILE

    def body(x_ref, mix_ref, t_ref, gpm_ref, g2_ref, g4_ref, wup_ref, wdown_ref,
             up_ref, hn2_ref, dout_ref, dmlp_ref, loss_ref, dg4_ref, act_ref):
        @pl.when(pl.program_id(0) == 0)
        def _():
            loss_ref[...] = jnp.zeros_like(loss_ref)
            dg4_ref[...] = jnp.zeros_like(dg4_ref)

        halves = [slice(0, tb // 2), slice(tb // 2, tb)]
        hv, hn2 = [], []
        for rows in halves:
            hv.append(_residual_mid(x_ref[rows, :], mix_ref[rows, :], gpm_ref[...]))
            hn2.append((hv[-1] * _rms(hv[-1]) * g2_ref[...]).astype(BF16))
            hn2_ref[rows, :] = hn2[-1]
        for k, rows in enumerate(halves):
            for j in range(N_CHIPS):
                up = _dot(hn2[k], _chip_block(wup_ref, j))
                up = jnp.maximum(up, 0.0)
                up_ref[rows, 1024 * j:1024 * (j + 1)] = up.astype(BF16)
                act_ref[rows, 1024 * j:1024 * (j + 1)] = (up * up).astype(BF16)
        w_down_all = wdown_ref[...].reshape(D_FF, D_MODEL)
        loss = jnp.zeros((1, 1), F32)
        dg4 = jnp.zeros((1, D_MODEL), F32)
        for k, rows in enumerate(halves):
            mlp = _dot(act_ref[rows, :], w_down_all)
            rstd = _rms(mlp)
            zhat = mlp * rstd
            diff = hv[k] + zhat * g4_ref[...] - t_ref[rows, :]
            loss = loss + jnp.sum(jnp.sum(diff * diff, axis=1, keepdims=True), axis=0, keepdims=True)
            dout = diff * (1.0 / D_MODEL)
            dout_ref[rows, :] = dout
            dg4 = dg4 + _colsum(dout * zhat)
            dmlp_ref[rows, :] = _norm_bwd(dout, g4_ref[...], zhat, rstd).astype(BF16)
        loss_ref[...] += loss
        dg4_ref[...] += dg4

    tile = lambda w: pl.BlockSpec((tb, w), lambda i: (i, 0))
    return _pallas(
        body, name="mlp_loss", grid=(seq // tb,),
        in_specs=[tile(D_MODEL), tile(D_MODEL), tile(D_MODEL), _resident((1, D_MODEL)), _resident((1, D_MODEL)),
                  _resident((1, D_MODEL)), _resident(w_up.shape), _resident(w_down.shape)],
        out_specs=[tile(D_FF), tile(D_MODEL), tile(D_MODEL), tile(D_MODEL),
                   pl.BlockSpec((1, 1), lambda i: (0, 0)), pl.BlockSpec((1, D_MODEL), lambda i: (0, 0))],
        out_shape=[SDS((seq, D_FF), BF16), SDS((seq, D_MODEL), BF16), SDS((seq, D_MODEL), F32),
                   SDS((seq, D_MODEL), BF16), SDS((1, 1), F32), SDS((1, D_MODEL), F32)],
        scratch=[pltpu.VMEM((tb, D_FF), BF16)],
        operands=(x, mix, target, g_post_mix, g_pre_mlp, g_post_mlp, w_up, w_down))


def _mlp_bwd(dmlp, up, x, dout, mix, g_pre_mlp, g_post_mix, w_up, w_down):
    seq = x.shape[0]
    tb = MLP_BWD_TOKEN_TILE

    def body(dmlp_ref, up_ref, x_ref, dout_ref, mix_ref, g2_ref, gpm_ref, wup_ref, wdown_ref,
             dup_ref, dh_ref, dmix_ref, dg2_ref, dgpm_ref):
        @pl.when(pl.program_id(0) == 0)
        def _():
            dg2_ref[...] = jnp.zeros_like(dg2_ref)
            dgpm_ref[...] = jnp.zeros_like(dgpm_ref)

        subs = [slice(k * MLP_BWD_SUB_TILE, (k + 1) * MLP_BWD_SUB_TILE) for k in range(tb // MLP_BWD_SUB_TILE)]
        dhn2 = []
        for rows in subs:
            dmlp_v = dmlp_ref[rows, :]
            acc = None
            for j in range(N_CHIPS):
                cols = slice(1024 * j, 1024 * (j + 1))
                dact = _dot_nt(dmlp_v, _chip_block(wdown_ref, j))
                dup = (dact * (2.0 * up_ref[rows, cols].astype(F32))).astype(BF16)
                dup_ref[rows, cols] = dup
                part = _dot_nt(dup, _chip_block(wup_ref, j))
                acc = part if acc is None else acc + part
            dhn2.append(acc)
        dg2 = jnp.zeros((1, D_MODEL), F32)
        dgpm = jnp.zeros((1, D_MODEL), F32)
        for k, rows in enumerate(subs):
            mix_v = mix_ref[rows, :].astype(F32)
            hv = _residual_mid(x_ref[rows, :], mix_ref[rows, :], gpm_ref[...])
            r2 = _rms(hv)
            hhat = hv * r2
            dg2 = dg2 + _colsum(dhn2[k] * hhat)
            dh = dout_ref[rows, :] + _norm_bwd(dhn2[k], g2_ref[...], hhat, r2)
            dh_ref[rows, :] = dh.astype(BF16)
            rz = _rms(mix_v)
            zhat = mix_v * rz
            dgpm = dgpm + _colsum(dh * zhat)
            dmix_ref[rows, :] = _norm_bwd(dh, gpm_ref[...], zhat, rz).astype(BF16)
        dg2_ref[...] += dg2
        dgpm_ref[...] += dgpm

    tile = lambda w: pl.BlockSpec((tb, w), lambda i: (i, 0))
    vec = pl.BlockSpec((1, D_MODEL), lambda i: (0, 0))
    return _pallas(
        body, name="mlp_bwd", grid=(seq // tb,),
        in_specs=[tile(D_MODEL), tile(D_FF), tile(D_MODEL), tile(D_MODEL), tile(D_MODEL),
                  _resident((1, D_MODEL)), _resident((1, D_MODEL)), _resident(w_up.shape), _resident(w_down.shape)],
        out_specs=[tile(D_FF), tile(D_MODEL), tile(D_MODEL), vec, vec],
        out_shape=[SDS((seq, D_FF), BF16), SDS((seq, D_MODEL), BF16), SDS((seq, D_MODEL), BF16),
                   SDS((1, D_MODEL), F32), SDS((1, D_MODEL), F32)],
        operands=(dmlp, up, x, dout, mix, g_pre_mlp, g_post_mix, w_up, w_down))


class _Rider(NamedTuple):
    body: Callable
    in_specs: list
    out_specs: list
    out_shape: list
    operands: tuple


def _mix_bwd(dmix, attn, gb, gc, xin, conv_w, g_attn, g_conv, w_out, n_k):
    seq = attn.shape[0]
    tb = seq // (N_CHIPS * n_k)

    def body(first, dmix_ref, a_ref, gb_ref, gc_ref, xin_ref, gch_ref, xinh_ref, cw_ref, ga_ref, gcn_ref, w_ref,
             dattn_ref, dgb_ref, dy_ref, dga_ref, dgcn_ref, dcw_ref):
        @pl.when(first)
        def _():
            dga_ref[...] = jnp.zeros_like(dga_ref)
            dgcn_ref[...] = jnp.zeros_like(dgcn_ref)
            dcw_ref[...] = jnp.zeros_like(dcw_ref)

        dmixed = _dot_nt(dmix_ref[...], w_ref[...].reshape(D_MODEL, D_MODEL))
        a = a_ref[...].astype(F32)
        ra = _rms(a)
        ahat = a * ra
        dan = dmixed[:, 0:Q_WIDTH]
        dga_ref[...] += _colsum(dan * ahat)
        dattn_ref[...] = _norm_bwd(dan, ga_ref[...], ahat, ra).astype(BF16)
        gbv = gb_ref[...].astype(F32)
        u, u1, u2, y = _conv_parts(gc_ref[...], xin_ref[...], gch_ref[...], xinh_ref[...], cw_ref[...], first)
        conv = gbv * y
        rc = _rms(conv)
        chat = conv * rc
        dcn = dmixed[:, Q_WIDTH:]
        dgcn_ref[...] += _colsum(dcn * chat)
        dconv = _norm_bwd(dcn, gcn_ref[...], chat, rc)
        dgb_ref[...] = (dconv * y).astype(BF16)
        dy = dconv * gbv
        dy_ref[...] = dy.astype(BF16)
        dcw_ref[0:1, :] += _colsum(dy * u2)
        dcw_ref[1:2, :] += _colsum(dy * u1)
        dcw_ref[2:3, :] += _colsum(dy * u)

    tile = lambda w: pl.BlockSpec((tb, w), lambda j, k: (j * n_k + k, 0))
    halo = lambda w: pl.BlockSpec((HALO, w), lambda j, k: (jnp.maximum((j * n_k + k) * (tb // HALO) - 1, 0), 0))
    whole = lambda shape: pl.BlockSpec(shape, lambda j, k: (0,) * len(shape))
    return _Rider(
        body,
        in_specs=[tile(D_MODEL), tile(Q_WIDTH), tile(CONV_WIDTH), tile(CONV_WIDTH), tile(CONV_WIDTH),
                  halo(CONV_WIDTH), halo(CONV_WIDTH),
                  _resident((CONV_K, CONV_WIDTH)), _resident((1, Q_WIDTH)), _resident((1, CONV_WIDTH)),
                  _resident(w_out.shape)],
        out_specs=[tile(Q_WIDTH), tile(CONV_WIDTH), tile(CONV_WIDTH),
                   whole((1, Q_WIDTH)), whole((1, CONV_WIDTH)), whole((CONV_K, CONV_WIDTH))],
        out_shape=[SDS((seq, Q_WIDTH), BF16), SDS((seq, CONV_WIDTH), BF16), SDS((seq, CONV_WIDTH), BF16),
                   SDS((1, Q_WIDTH), F32), SDS((1, CONV_WIDTH), F32), SDS((CONV_K, CONV_WIDTH), F32)],
        operands=(dmix, attn, gb, gc, xin, gc, xin, conv_w, g_attn, g_conv, w_out))


def _attention_bwd(q, dattn, attn, kd0, kd1, vd0, vd1, sinks, comm=None):
    seq = q.shape[0]
    nb = ATTN_BWD_BLOCKS

    def body(sink_ref, q_ref, do_ref, o_ref, kd0_ref, kd1_ref, vd0_ref, vd1_ref,
             dq_ref, dk0_ref, dk1_ref, dv0_ref, dv1_ref, dsink_ref):
        @pl.when(pl.program_id(0) == 0)
        def _():
            for r in (dk0_ref, dk1_ref, dv0_ref, dv1_ref, dsink_ref):
                r[...] = jnp.zeros_like(r)

        lane = lax.broadcasted_iota(jnp.int32, (1, 128), 1)
        dsink = jnp.zeros((1, 128), F32)
        for b in range(nb):
            i = pl.program_id(0) * nb + b
            rows = slice(QBLOCK * b, QBLOCK * (b + 1))
            valid = _attn_valid(i)
            for kv_head, (k_ref, v_ref, dk_ref, dv_ref) in enumerate(
                    ((kd0_ref, vd0_ref, dk0_ref, dv0_ref), (kd1_ref, vd1_ref, dk1_ref, dv1_ref))):
                kband, prev, own = _band(k_ref, i)
                vband, _, _ = _band(v_ref, i)
                base = 256 * kv_head
                qm = _stack_heads(q_ref[rows, base:base + 128], q_ref[rows, base + 128:base + 256])
                dom = _stack_heads(do_ref[rows, base:base + 128], do_ref[rows, base + 128:base + 256])
                om = _stack_heads(o_ref[rows, base:base + 128], o_ref[rows, base + 128:base + 256])
                s = jnp.where(valid, _dot_nt(qm, kband), NEG_INF)
                p, e_sink, inv_l = _softmax_with_sink(s, _sink_column(sink_ref, kv_head))
                p = p * inv_l
                delta = jnp.sum(dom.astype(F32) * om.astype(F32), axis=-1, keepdims=True)
                ds = (p * (_dot_nt(dom, vband) - delta)).astype(BF16)
                sink_term = -(e_sink * inv_l) * delta
                for j in range(4):
                    part = jnp.sum(sink_term[QBLOCK * j:QBLOCK * (j + 1)], axis=0, keepdims=True)
                    dsink = dsink + jnp.where(lane == 4 * kv_head + j, part, 0.0)
                pair0, pair1 = _unstack_heads(_dot(ds, kband))
                dq_ref[rows, base:base + 128] = pair0.astype(BF16)
                dq_ref[rows, base + 128:base + 256] = pair1.astype(BF16)
                dkd = _dot_tn(ds, qm)
                dkd = dkd + pltpu.roll(dkd, HEAD_DIM, 1)
                dvd = _dot_tn(p.astype(BF16), dom)
                dvd = dvd + pltpu.roll(dvd, HEAD_DIM, 1)
                dk_ref[pl.ds(prev, QBLOCK), :] += dkd[0:QBLOCK]
                dk_ref[pl.ds(own, QBLOCK), :] += dkd[QBLOCK:]
                dv_ref[pl.ds(prev, QBLOCK), :] += dvd[0:QBLOCK]
                dv_ref[pl.ds(own, QBLOCK), :] += dvd[QBLOCK:]
        dsink_ref[...] += dsink

    blk = pl.BlockSpec((nb * QBLOCK, Q_WIDTH), lambda i: (i, 0))
    full = _resident((seq, 128))
    acc = pl.BlockSpec((seq, 128), lambda i: (0, 0))
    return _pallas(
        body, name="attention_bwd", grid=(seq // (nb * QBLOCK),),
        in_specs=[pl.BlockSpec(memory_space=pltpu.SMEM), blk, blk, blk, full, full, full, full],
        out_specs=[blk, acc, acc, acc, acc, pl.BlockSpec((1, 128), lambda i: (0, 0))],
        out_shape=[SDS((seq, Q_WIDTH), BF16)] + [SDS((seq, 128), F32)] * 4 + [SDS((1, 128), F32)],
        operands=(sinks, q, dattn, attn, kd0, kd1, vd0, vd1), comm=comm)


def _in_proj_bwd(dq, dk0, dk1, dv0, dv1, dgb, dy, gc, xin, conv_w, x, dh, g_pre, w_in_t, rope):
    seq = x.shape[0]
    tb = TOKEN_TILE
    n_tiles = seq // tb

    def body(dq_ref, dk0_ref, dk1_ref, dv0_ref, dv1_ref, dgb_ref, dy_ref, dyh_ref, gc_ref, xin_ref, cw_ref,
             x_ref, dh_ref, g_ref, w_ref, c_ref, sa_ref, sb_ref,
             dproj_ref, gx_ref, dg_ref):
        i = pl.program_id(0)

        @pl.when(i == 0)
        def _():
            dg_ref[...] = jnp.zeros_like(dg_ref)

        dy = dy_ref[...].astype(F32)
        ext = jnp.concatenate([dy, jnp.where(i == n_tiles - 1, 0.0, dyh_ref[...].astype(F32))], axis=0)
        dy1 = pltpu.roll(ext, tb + HALO - 1, 0)[0:tb]
        dy2 = pltpu.roll(ext, tb + HALO - 2, 0)[0:tb]
        cw = cw_ref[...]
        du = cw[2:3, :] * dy + cw[1:2, :] * dy1 + cw[0:1, :] * dy2
        scale = 1.0 / math.sqrt(HEAD_DIM)
        base = Q_WIDTH + 2 * KV_WIDTH
        halves = [slice(0, tb // 2), slice(tb // 2, tb)]
        low = _lane_lt64((tb // 2, 128))
        for rows in halves:
            c, sa, sb = _rope_tile(c_ref.at[rows, :], sa_ref, sb_ref)
            for p in range(Q_WIDTH // 128):
                dproj_ref[rows, 128 * p:128 * (p + 1)] = _rope_transposed(
                    dq_ref[rows, 128 * p:128 * (p + 1)].astype(F32) * scale, c, sa, sb).astype(BF16)
            dk = jnp.where(low, dk0_ref[rows, :], dk1_ref[rows, :])
            dproj_ref[rows, Q_WIDTH:Q_WIDTH + KV_WIDTH] = _rope_transposed(dk, c, sa, sb).astype(BF16)
            dproj_ref[rows, Q_WIDTH + KV_WIDTH:base] = jnp.where(low, dv0_ref[rows, :], dv1_ref[rows, :]).astype(BF16)
            dproj_ref[rows, base:base + CONV_WIDTH] = dgb_ref[rows, :]
            dproj_ref[rows, base + CONV_WIDTH:base + 2 * CONV_WIDTH] = (du[rows] * xin_ref[rows, :].astype(F32)).astype(BF16)
            dproj_ref[rows, base + 2 * CONV_WIDTH:] = (du[rows] * gc_ref[rows, :].astype(F32)).astype(BF16)
        w_all = w_ref[...].reshape(IN_COLS, D_MODEL)
        dhn = [_dot(dproj_ref[rows, :], w_all) for rows in halves]
        dg = jnp.zeros((1, D_MODEL), F32)
        for k, rows in enumerate(halves):
            xv = x_ref[rows, :]
            r = _rms(xv)
            xhat = xv * r
            dg = dg + _colsum(dhn[k] * xhat)
            gx_ref[rows, :] = dh_ref[rows, :].astype(F32) + _norm_bwd(dhn[k], g_ref[...], xhat, r)
        dg_ref[...] += dg

    tile = lambda w: pl.BlockSpec((tb, w), lambda i: (i, 0))
    halo_next = pl.BlockSpec((HALO, CONV_WIDTH), lambda i: (jnp.minimum((i + 1) * (tb // HALO), seq // HALO - 1), 0))
    return _pallas(
        body, name="in_proj_bwd", grid=(n_tiles,),
        in_specs=[tile(Q_WIDTH), tile(128), tile(128), tile(128), tile(128), tile(CONV_WIDTH), tile(CONV_WIDTH), halo_next,
                  tile(CONV_WIDTH), tile(CONV_WIDTH), _resident((CONV_K, CONV_WIDTH)),
                  tile(D_MODEL), tile(D_MODEL), _resident((1, D_MODEL)), _resident(w_in_t.shape), *_rope_specs(tb)],
        out_specs=[tile(IN_COLS), tile(D_MODEL), pl.BlockSpec((1, D_MODEL), lambda i: (0, 0))],
        out_shape=[SDS((seq, IN_COLS), BF16), SDS((seq, D_MODEL), F32), SDS((1, D_MODEL), F32)],
        operands=(dq, dk0, dk1, dv0, dv1, dgb, dy, dy, gc, xin, conv_w, x, dh, g_pre, w_in_t, *rope))


def _wgrad_grid(seq, per_chip, h_rows):
    chips_per_step = 1 if per_chip else N_CHIPS
    m = chips_per_step * 2 * h_rows
    bt = min(seq, WGRAD_TOKEN_TILE if m <= 1024 else WGRAD_TOKEN_TILE // 2)
    return chips_per_step, m, bt, seq // bt


def _wgrad(name, a, b, *, per_chip, h_rows, square_a=False, comm=None, rider=None):
    seq = a.shape[0]
    chips_per_step, m, bt, n_k = _wgrad_grid(seq, per_chip, h_rows)
    a_cols = m if per_chip else a.shape[1]
    a_wide = a.shape[1] > a_cols
    b_wide = b.shape[1] > D_MODEL
    n_ride_in = len(rider.in_specs) if rider else 0
    n_ride_out = len(rider.out_specs) if rider else 0

    def body(a_ref, b_ref, *rest):
        ride_in, g_ref = rest[:n_ride_in], rest[n_ride_in]
        ride_out, acc_ref = rest[n_ride_in + 1:n_ride_in + 1 + n_ride_out], rest[-1]
        k = pl.program_id(1)

        @pl.when(k == 0)
        def _():
            acc_ref[...] = jnp.zeros_like(acc_ref)

        av = a_ref[...]
        if square_a:
            av = (av.astype(F32) * av.astype(F32)).astype(BF16)
        acc_ref[...] += _dot_tn(av, b_ref[...])

        @pl.when(k == n_k - 1)
        def _():
            for cidx in range(chips_per_step):
                for half in range(2):
                    r0 = (2 * cidx + half) * h_rows
                    g_ref[cidx, half] = acc_ref[r0:r0 + h_rows, :]

        if rider:
            rider.body(jnp.logical_and(pl.program_id(0) == 0, k == 0), *ride_in, *ride_out)

    a_spec = pl.BlockSpec((bt, a_cols), (lambda j, k: (k, j)) if a_wide else (lambda j, k: (k, 0)))
    b_spec = pl.BlockSpec((bt, D_MODEL), (lambda j, k: (k, j)) if b_wide else (lambda j, k: (k, 0)))
    g_spec = pl.BlockSpec((chips_per_step, 2, h_rows, D_MODEL), lambda j, k: (j, 0, 0, 0))
    return _pallas(
        body, name=name, grid=(N_CHIPS if per_chip else 1, n_k),
        in_specs=[a_spec, b_spec] + (rider.in_specs if rider else []),
        out_specs=[g_spec] + (rider.out_specs if rider else []),
        out_shape=[SDS((N_CHIPS, 2, h_rows, D_MODEL), F32)] + (rider.out_shape if rider else []),
        scratch=[pltpu.VMEM((m, D_MODEL), F32)], operands=(a, b) + (rider.operands if rider else ()), comm=comm)


def _adamw_math(w, g, m, v):
    m = ADAM_B1 * m + (1.0 - ADAM_B1) * g
    v = ADAM_B2 * v + (1.0 - ADAM_B2) * (g * g)
    m_hat = m / (1.0 - ADAM_B1 ** ADAM_STEP)
    v_hat = v / (1.0 - ADAM_B2 ** ADAM_STEP)
    delta = -ADAM_LR * (m_hat / (jnp.sqrt(v_hat) + ADAM_EPS) + ADAM_WD * w)
    return delta, m, v


def _adamw_rows(name, reduced, w, m, v, rt):
    per_half = reduced.shape[1] // rt

    def body(r_ref, w_ref, m_ref, v_ref, g_out, d_out, m_out, v_out):
        g = r_ref[0]
        g_out[...] = g
        d_out[...], m_out[...], v_out[...] = _adamw_math(w_ref[...], g, m_ref[...], v_ref[...])

    blk = pl.BlockSpec((rt, D_MODEL), lambda h, r: (h * per_half + r, 0))
    return _pallas(
        body, name=name, grid=(2, per_half),
        in_specs=[pl.BlockSpec((1, rt, D_MODEL), lambda h, r: (h, r, 0)), blk, blk, blk],
        out_specs=[blk, blk, blk, blk], out_shape=[SDS(w.shape, F32)] * 4, operands=(reduced, w, m, v))


def _adamw_small(w, g, m, v):
    def body(w_ref, g_ref, m_ref, v_ref, d_out, m_out, v_out):
        d_out[...], m_out[...], v_out[...] = _adamw_math(w_ref[...], g_ref[...], m_ref[...], v_ref[...])

    return pl.pallas_call(body, name="adamw_small", in_specs=[VMEM_WHOLE] * 4, out_specs=[VMEM_WHOLE] * 3,
                          out_shape=[SDS(w.shape, F32)] * 3)(w, g, m, v)


SMALL_VECTORS = ("pre_mix_norm", "post_mix_norm", "pre_mlp_norm", "post_mlp_norm")
SMALL_NAMES = SMALL_VECTORS + ("attn_group_norm", "conv_group_norm", "conv_w", "attn_sinks")


def _pack_small(p):
    rows = [p[n].reshape(1, D_MODEL) for n in SMALL_VECTORS]
    rows.append(jnp.concatenate([p["attn_group_norm"].reshape(1, -1), p["conv_group_norm"].reshape(1, -1)], axis=1))
    cw = p["conv_w"].reshape(CONV_K, -1)
    rows.append(jnp.pad(cw, ((0, 1), (0, CONV_WIDTH - cw.shape[1]))).reshape(2, D_MODEL))
    last = jnp.concatenate([p["attn_sinks"].reshape(1, 8), p.get("loss_sum", jnp.zeros((1, 1), F32))], axis=1)
    rows.append(jnp.pad(last, ((0, 0), (0, D_MODEL - 9))))
    return jnp.concatenate(rows, axis=0)


def _unpack_small(packed, conv_width):
    out = {n: packed[i:i + 1] for i, n in enumerate(SMALL_VECTORS)}
    out["attn_group_norm"] = packed[4:5, :Q_WIDTH]
    out["conv_group_norm"] = packed[4:5, Q_WIDTH:]
    out["conv_w"] = packed[5:7].reshape(4, CONV_WIDTH)[:CONV_K, :conv_width].reshape(1, CONV_K, conv_width)
    out["attn_sinks"] = packed[7:8, :8]
    out["loss_sum"] = packed[7, 8]
    return out


WEIGHT_ORDER = ("pre_mix_norm", "w_in", "conv_w", "attn_sinks", "attn_group_norm", "conv_group_norm", "w_out",
                "post_mix_norm", "pre_mlp_norm", "w_up", "w_down", "post_mlp_norm")


def kernel(x, pre_mix_norm, w_in, conv_w, attn_sinks, attn_group_norm, conv_group_norm, w_out, post_mix_norm, pre_mlp_norm, w_up, w_down, post_mlp_norm, loss_target, m_pre_mix_norm, m_w_in, m_conv_w, m_attn_sinks, m_attn_group_norm, m_conv_group_norm, m_w_out, m_post_mix_norm, m_pre_mlp_norm, m_w_up, m_w_down, m_post_mlp_norm, v_pre_mix_norm, v_w_in, v_conv_w, v_attn_sinks, v_attn_group_norm, v_conv_group_norm, v_w_out, v_post_mix_norm, v_pre_mlp_norm, v_w_up, v_w_down, v_post_mlp_norm):
    w = dict(pre_mix_norm=pre_mix_norm, w_in=w_in, conv_w=conv_w, attn_sinks=attn_sinks, attn_group_norm=attn_group_norm,
             conv_group_norm=conv_group_norm, w_out=w_out, post_mix_norm=post_mix_norm, pre_mlp_norm=pre_mlp_norm,
             w_up=w_up, w_down=w_down, post_mlp_norm=post_mlp_norm)
    m = dict(pre_mix_norm=m_pre_mix_norm, w_in=m_w_in, conv_w=m_conv_w, attn_sinks=m_attn_sinks,
             attn_group_norm=m_attn_group_norm, conv_group_norm=m_conv_group_norm, w_out=m_w_out,
             post_mix_norm=m_post_mix_norm, pre_mlp_norm=m_pre_mlp_norm, w_up=m_w_up, w_down=m_w_down,
             post_mlp_norm=m_post_mlp_norm)
    v = dict(pre_mix_norm=v_pre_mix_norm, w_in=v_w_in, conv_w=v_conv_w, attn_sinks=v_attn_sinks,
             attn_group_norm=v_attn_group_norm, conv_group_norm=v_conv_group_norm, w_out=v_w_out,
             post_mix_norm=v_post_mix_norm, pre_mlp_norm=v_pre_mlp_norm, w_up=v_w_up, w_down=v_w_down,
             post_mlp_norm=v_post_mlp_norm)
    core = lax.axis_index("c").astype(jnp.int32).reshape(1)
    chip = 2 * lax.axis_index("x") + lax.axis_index("y")
    local_conv = conv_w.shape[2]
    xs, target = x[0], loss_target[0]
    rope = _rope_inputs(xs.shape[0])

    hb_up, hb_down, hb_out, hb_in = _cast_halves(core, w_up[0], w_down[0], w_out[0], w_in[0].T)
    conv_pad = jnp.pad(conv_w[0], ((0, 8 - CONV_K), (0, 0)))
    hn, wf_in = _pre_norm(xs, pre_mix_norm, comm=_gather_legs(hb_in, None, first=(0, H_IN), second_after=(0, H_IN)))
    *proj, wf_out, wf_up, conv_all = _in_proj(
        hn, wf_in, rope, comm=_merge(_gather_legs(hb_out, None, first=(0, H_OUT)),
                                     _gather_legs(hb_up, None, first=(0, UP_SPLIT)), _gather_small(conv_pad)))
    conv_full = conv_all[:, :CONV_K, :].transpose(1, 0, 2).reshape(CONV_K, CONV_WIDTH)
    q, kd0, kd1, vd0, vd1, gb, gc, xin = proj
    attn, wf_out, wf_up, wf_down = _attention_fwd(
        q, kd0, kd1, vd0, vd1, attn_sinks,
        comm=_merge(_gather_legs(None, wf_out, second=(0, H_OUT)),
                    _gather_legs(hb_up, wf_up, first=(UP_SPLIT, H_UP - UP_SPLIT), second=(0, UP_SPLIT)),
                    _gather_legs(hb_down, None, first=(0, H_DOWN))))
    mix, mixed, wf_up, wf_down = _mix_out(
        attn, gb, gc, xin, conv_full, attn_group_norm, conv_group_norm, wf_out,
        comm=_merge(_gather_legs(None, wf_up, second=(UP_SPLIT, H_UP - UP_SPLIT)),
                    _gather_legs(None, wf_down, second=(0, H_DOWN))))
    up, hn2, dout, dmlp, loss_sum, dg_post_mlp = _mlp_loss(xs, mix, target, post_mix_norm, pre_mlp_norm, post_mlp_norm,
                                                           wf_up, wf_down)

    dup, dh, dmix, dg_pre_mlp, dg_post_mix = _mlp_bwd(dmlp, up, xs, dout, mix, pre_mlp_norm, post_mix_norm, wf_up, wf_down)
    n_k = _wgrad_grid(xs.shape[0], True, H_DOWN)[3]
    g_down, dattn, dgb, dy, dg_attn, dg_conv, dconv_w = _wgrad(
        "wgrad_down", up, dmlp, per_chip=True, h_rows=H_DOWN, square_a=True,
        rider=_mix_bwd(dmix, attn, gb, gc, xin, conv_full, attn_group_norm, conv_group_norm, wf_out, n_k))
    g_up, got_down = _wgrad("wgrad_up", hn2, dup, per_chip=True, h_rows=H_UP, comm=_pair_send(g_down))
    p_down = _pair_sum("pair_sum_down", core, g_down, got_down)
    g_out, got_up = _wgrad("wgrad_out", mixed, dmix, per_chip=False, h_rows=H_OUT, comm=_pair_send(g_up))
    p_up = _pair_sum("pair_sum_up", core, g_up, got_up)
    dq, dk0, dk1, dv0, dv1, dsink, ex_down, ex_up, got_out = _attention_bwd(
        q, dattn, attn, kd0, kd1, vd0, vd1, attn_sinks,
        comm=_merge(_chip_exchange(p_down), _chip_exchange(p_up), _pair_send(g_out)))
    p_out = _pair_sum("pair_sum_out", core, g_out, got_out)
    dproj, grad_x, dg_pre_mix = _in_proj_bwd(dq, dk0, dk1, dv0, dv1, dgb, dy, gc, xin, conv_full, xs, dh, pre_mix_norm,
                                             wf_in, rope)
    g_in, ex_out = _wgrad("wgrad_in", dproj, hn, per_chip=False, h_rows=H_IN, comm=_chip_exchange(p_out))
    small = dict(pre_mix_norm=dg_pre_mix, conv_w=dconv_w, attn_sinks=dsink[:, :8], attn_group_norm=dg_attn,
                 conv_group_norm=dg_conv, post_mix_norm=dg_post_mix, pre_mlp_norm=dg_pre_mlp, post_mlp_norm=dg_post_mlp,
                 loss_sum=loss_sum)
    r_down, r_up, r_out, r_in, small_total = _tail_reduce(g_in, [ex_down, ex_up, ex_out], _pack_small(small))

    out_g, out_d, out_m, out_v = {}, {}, {}, {}
    out_g["w_up"], out_d["w_up"], out_m["w_up"], out_v["w_up"] = _adamw_rows(
        "adamw_up", r_up, w_up[0], m_w_up[0], v_w_up[0], 256)
    out_g["w_down"], out_d["w_down"], out_m["w_down"], out_v["w_down"] = _adamw_rows(
        "adamw_down", r_down, w_down[0], m_w_down[0], v_w_down[0], 256)
    out_g["w_out"], out_d["w_out"], out_m["w_out"], out_v["w_out"] = _adamw_rows(
        "adamw_out", r_out, w_out[0], m_w_out[0], v_w_out[0], H_OUT)
    in_t = _adamw_rows("adamw_in", r_in, w_in[0].T, m_w_in[0].T, v_w_in[0].T, H_IN)
    out_g["w_in"], out_d["w_in"], out_m["w_in"], out_v["w_in"] = [t.T for t in in_t]

    small_sum = _unpack_small(small_total, CONV_WIDTH)
    loss = small_sum["loss_sum"] * (0.5 / D_MODEL)
    small_sum["conv_w"] = lax.dynamic_slice_in_dim(small_sum["conv_w"], chip * local_conv, local_conv, axis=2)
    packed = [_pack_small({n: t[n] for n in SMALL_NAMES}) for t in (w, small_sum, m, v)]
    small_d, small_m, small_v = [_unpack_small(t, local_conv) for t in _adamw_small(*packed)]
    for n in SMALL_NAMES:
        out_g[n], out_d[n], out_m[n], out_v[n] = small_sum[n], small_d[n], small_m[n], small_v[n]

    def shaped(d):
        return [d[n].reshape(w[n].shape) for n in WEIGHT_ORDER]

    return (loss, grad_x[None], *shaped(out_g), *shaped(out_d), *shaped(out_m), *shaped(out_v))
```

```python
import math
from typing import Callable, NamedTuple

import jax
import jax.numpy as jnp
import numpy as np
from jax import lax
from jax.experimental import pallas as pl
from jax.experimental.pallas import tpu as pltpu

F32 = jnp.float32
BF16 = jnp.bfloat16

D_MODEL = 1024
HEAD_DIM = 64
Q_WIDTH = 512
KV_WIDTH = 128
CONV_WIDTH = 512
CONV_K = 3
D_FF = 4096
IN_COLS = 2304
QBLOCK = 128
ROT_DIM = 16
ROPE_THETA = 500000.0
NORM_EPS = 1e-6
NEG_INF = -1e30
N_CHIPS = 4

ADAM_LR = 0.001
ADAM_B1 = 0.9
ADAM_B2 = 0.999
ADAM_EPS = 1e-08
ADAM_WD = 0.01
ADAM_STEP = 10

H_UP, H_DOWN, H_OUT, H_IN = 512, 512, 128, 288
UP_SPLIT = 320
DOWN_SPLIT = 320

TOKEN_TILE = 512
MLP_BWD_TOKEN_TILE = 512
MLP_BWD_SUB_TILE = 256
ATTN_FWD_BLOCKS = 4
ATTN_BWD_BLOCKS = 2
WGRAD_TOKEN_TILE = 2048
VMEM_LIMIT_V7X = 56 * 1024 * 1024

MESH = pl.DeviceIdType.MESH
ANY = pl.BlockSpec(memory_space=pl.ANY)
VMEM_WHOLE = pl.BlockSpec(memory_space=pltpu.VMEM)
SDS = jax.ShapeDtypeStruct


def _resident(shape):
    zeros = (0,) * len(shape)
    return pl.BlockSpec(shape, lambda *_: zeros, pipeline_mode=pl.Buffered(1))


def _rms(v):
    return lax.rsqrt(jnp.mean(v * v, axis=-1, keepdims=True) + NORM_EPS)


def _norm_bwd(dy, gain, vhat, rstd):
    t = dy * gain
    return rstd * (t - vhat * jnp.mean(t * vhat, axis=-1, keepdims=True))


def _colsum(v):
    return jnp.sum(v, axis=0, keepdims=True)


def _dot_nt(a, b):
    return lax.dot_general(a, b, (((1,), (1,)), ((), ())), preferred_element_type=F32)


def _dot_tn(a, b):
    return lax.dot_general(a, b, (((0,), (0,)), ((), ())), preferred_element_type=F32)


def _dot(a, b):
    return jnp.dot(a, b, preferred_element_type=F32)


def _chip_block(w_ref, chip):
    both = w_ref[pl.ds(2 * chip, 2)]
    return both.reshape(2 * both.shape[1], both.shape[2])


def _lane_lt64(shape):
    return lax.broadcasted_iota(jnp.int32, shape, 1) < HEAD_DIM


class _Comm(NamedTuple):
    operands: tuple
    out_shapes: tuple
    aliases: dict
    n_remote: int
    n_local: int
    plan: Callable
    after: Callable = None


def _merge(*comms):
    operands, out_shapes, aliases, parts = [], [], {}, []
    n_remote = n_local = 0
    for cm in comms:
        parts.append((len(operands), len(out_shapes), n_remote, n_local, cm))
        for k, v in cm.aliases.items():
            aliases[len(operands) + k] = len(out_shapes) + v
        operands += cm.operands
        out_shapes += cm.out_shapes
        n_remote += cm.n_remote
        n_local += cm.n_local

    def run(which, ins, outs, send, recv, loc):
        sends, recvs, locs = [], [], []
        for i0, o0, r0, l0, cm in parts:
            fn = getattr(cm, which)
            if fn is not None:
                s, r, l = fn(ins[i0:i0 + len(cm.operands)], outs[o0:o0 + len(cm.out_shapes)],
                             lambda k, r0=r0: send(r0 + k), lambda k, r0=r0: recv(r0 + k), lambda k, l0=l0: loc(l0 + k))
                sends, recvs, locs = sends + s, recvs + r, locs + l
        return sends, recvs, locs

    def plan(*args):
        return run("plan", *args)

    def after(*args):
        return run("after", *args)

    return _Comm(tuple(operands), tuple(out_shapes), aliases, n_remote, n_local, plan,
                 after if any(cm.after is not None for cm in comms) else None)


def _sem_scratch(comm):
    return [pltpu.SemaphoreType.DMA((max(comm.n_remote, 1),)), pltpu.SemaphoreType.DMA((max(comm.n_remote, 1),)),
            pltpu.SemaphoreType.DMA((max(comm.n_local, 1),))]


def _pallas(body, *, name, grid, in_specs, out_specs, out_shape, operands, scratch=(), comm=None):
    params = pltpu.CompilerParams(dimension_semantics=("arbitrary",) * len(grid), vmem_limit_bytes=VMEM_LIMIT_V7X)
    if comm is None:
        return pl.pallas_call(body, name=name, grid=grid, in_specs=in_specs, out_specs=out_specs, out_shape=out_shape,
                              scratch_shapes=list(scratch), compiler_params=params)(*operands)
    n_in, n_out, n_scr = len(in_specs), len(out_specs), len(scratch)
    c_in, c_out = len(comm.operands), len(comm.out_shapes)

    def with_comm(*refs):
        ins, c_ins = refs[:n_in], refs[n_in:n_in + c_in]
        o0 = n_in + c_in
        outs, c_outs = refs[o0:o0 + n_out], refs[o0 + n_out:o0 + n_out + c_out]
        s0 = o0 + n_out + c_out
        scr = refs[s0:s0 + n_scr]
        send_sems, recv_sems, local_sems = refs[s0 + n_scr:]
        first = last = None
        for axis, size in enumerate(grid):
            at_start, at_end = pl.program_id(axis) == 0, pl.program_id(axis) == size - 1
            first = at_start if first is None else jnp.logical_and(first, at_start)
            last = at_end if last is None else jnp.logical_and(last, at_end)

        def copies():
            return comm.plan(c_ins, c_outs, lambda k: send_sems.at[k], lambda k: recv_sems.at[k],
                             lambda k: local_sems.at[k])

        @pl.when(first)
        def _():
            sends, _, locs = copies()
            for cp in sends + locs:
                cp.start()

        body(*ins, *outs, *scr)

        @pl.when(last)
        def _():
            sends, recvs, locs = copies()
            for cp in recvs:
                cp.wait_recv()
            for cp in sends:
                cp.wait_send()
            for cp in locs:
                cp.wait()
            if comm.after is not None:
                sends, recvs, _ = comm.after(c_ins, c_outs, lambda k: send_sems.at[k], lambda k: recv_sems.at[k],
                                             lambda k: local_sems.at[k])
                for cp in sends:
                    cp.start()
                for cp in recvs:
                    cp.wait_recv()
                for cp in sends:
                    cp.wait_send()

    return pl.pallas_call(
        with_comm, name=name, grid=grid,
        in_specs=list(in_specs) + [ANY] * c_in, out_specs=list(out_specs) + [ANY] * c_out,
        out_shape=list(out_shape) + list(comm.out_shapes),
        scratch_shapes=list(scratch) + _sem_scratch(comm),
        input_output_aliases={n_in + k: n_out + v for k, v in comm.aliases.items()},
        compiler_params=params)(*operands, *comm.operands)


def _place():
    return lax.axis_index("x"), lax.axis_index("y"), lax.axis_index("c")


def _other_chips(x, y):
    return [(1 - x, y), (x, 1 - y), (1 - x, 1 - y)]


def _slot(px, py, pc):
    return 4 * px + 2 * py + pc


def _remote(src, dst, send_sem, recv_sem, to):
    return pltpu.make_async_remote_copy(src_ref=src, dst_ref=dst, send_sem=send_sem, recv_sem=recv_sem,
                                        device_id=to, device_id_type=MESH)


def _gather_legs(half_block, so_far, first=None, second=None, second_after=None):
    has_block, has_buffer = half_block is not None, so_far is not None
    shape = so_far.shape if has_buffer else (2 * N_CHIPS,) + half_block.shape
    dtype = so_far.dtype if has_buffer else half_block.dtype

    def forward(rows, base, ins, outs, send, recv):
        src = ins[-1] if has_buffer else outs[0]
        full = outs[0]
        x, y, c = _place()
        chips = _other_chips(x, y)
        span = pl.ds(*rows)
        sends = [_remote(src.at[_slot(*chip, c), span], full.at[_slot(*chip, c), span], send(base + j), recv(base + j),
                         (x, y, 1 - c)) for j, chip in enumerate(chips)]
        recvs = [_remote(src.at[_slot(*chip, 1 - c), span], full.at[_slot(*chip, 1 - c), span], send(base + j),
                         recv(base + j), (x, y, 1 - c)) for j, chip in enumerate(chips)]
        return sends, recvs

    def plan(ins, outs, send, recv, loc):
        sends, recvs, locs = [], [], []
        x, y, c = _place()
        if first is not None:
            blk, full, span = ins[0].at[pl.ds(*first)], outs[0], pl.ds(*first)
            chips = _other_chips(x, y)
            mine = full.at[_slot(x, y, c), span]
            sends += [_remote(blk, mine, send(0), recv(0), (x, y, 1 - c))]
            sends += [_remote(blk, mine, send(1 + j), recv(1 + j), (*chip, c)) for j, chip in enumerate(chips)]
            recvs += [_remote(blk, full.at[_slot(x, y, 1 - c), span], send(0), recv(0), (x, y, 1 - c))]
            recvs += [_remote(blk, full.at[_slot(*chip, c), span], send(1 + j), recv(1 + j), (*chip, c))
                      for j, chip in enumerate(chips)]
            locs += [pltpu.make_async_copy(blk, mine, loc(0))]
        if second is not None:
            s, r = forward(second, 4, ins, outs, send, recv)
            sends, recvs = sends + s, recvs + r
        return sends, recvs, locs

    def after(ins, outs, send, recv, loc):
        s, r = forward(second_after, 7, ins, outs, send, recv)
        return s, r, []

    operands = ((half_block,) if has_block else ()) + ((so_far,) if has_buffer else ())
    return _Comm(operands, (SDS(shape, dtype),), {len(operands) - 1: 0} if has_buffer else {}, 10, 1, plan,
                 after if second_after is not None else None)


def _gather_small(block):
    def plan(ins, outs, send, recv, loc):
        (blk,), (full,) = ins, outs
        x, y, c = _place()
        chips = _other_chips(x, y)
        sends = [_remote(blk, full.at[2 * x + y], send(j), recv(j), (*chip, c)) for j, chip in enumerate(chips)]
        recvs = [_remote(blk, full.at[2 * chip[0] + chip[1]], send(j), recv(j), (*chip, c))
                 for j, chip in enumerate(chips)]
        return sends, recvs, [pltpu.make_async_copy(blk, full.at[2 * x + y], loc(0))]

    return _Comm((block,), (SDS((N_CHIPS,) + block.shape, block.dtype),), {}, 3, 1, plan)


def _pair_send(grads):
    def plan(ins, outs, send, recv, loc):
        (g,), (got,) = ins, outs
        x, y, c = _place()
        copies = [_remote(g.at[j, 1 - c], got.at[j], send(j), recv(j), (x, y, 1 - c)) for j in range(N_CHIPS)]
        return copies, copies, []

    shape = (grads.shape[0],) + grads.shape[2:]
    return _Comm((grads,), (SDS(shape, grads.dtype),), {}, N_CHIPS, 0, plan)


def _chip_exchange(partial):
    def plan(ins, outs, send, recv, loc):
        (p,), (got,) = ins, outs
        x, y, c = _place()
        my_chip = 2 * x + y
        chips = _other_chips(x, y)
        sends = [_remote(p.at[2 * chip[0] + chip[1]], got.at[my_chip], send(j), recv(j), (*chip, c))
                 for j, chip in enumerate(chips)]
        recvs = [_remote(p.at[my_chip], got.at[2 * chip[0] + chip[1]], send(j), recv(j), (*chip, c))
                 for j, chip in enumerate(chips)]
        return sends, recvs, [pltpu.make_async_copy(p.at[my_chip], got.at[my_chip], loc(0))]

    return _Comm((partial,), (SDS(partial.shape, partial.dtype),), {}, 3, 1, plan)


def _pair_sum(name, core, grads, received):
    h = grads.shape[2]

    def body(core_ref, g_ref, r_ref, o_ref):
        o_ref[...] = (g_ref[0] + r_ref[...]).astype(BF16)

    return pl.pallas_call(
        body, name=name,
        grid_spec=pltpu.PrefetchScalarGridSpec(
            num_scalar_prefetch=1, grid=(N_CHIPS,),
            in_specs=[pl.BlockSpec((1, 1, h, D_MODEL), lambda j, core_ref: (j, core_ref[0], 0, 0)),
                      pl.BlockSpec((1, h, D_MODEL), lambda j, core_ref: (j, 0, 0))],
            out_specs=pl.BlockSpec((1, h, D_MODEL), lambda j, core_ref: (j, 0, 0))),
        out_shape=SDS((N_CHIPS, h, D_MODEL), BF16),
        compiler_params=pltpu.CompilerParams(dimension_semantics=("arbitrary",), vmem_limit_bytes=VMEM_LIMIT_V7X),
    )(core, grads, received)


SMALL_ROWS = 8


def _sum_blocks(ref):
    return (ref[0].astype(F32) + ref[1].astype(F32)) + (ref[2].astype(F32) + ref[3].astype(F32))


def _tail_reduce(last_grads, exchanged, small):
    n = len(exchanged)
    h = last_grads.shape[2]

    def body(*refs):
        g_ref, ex, small_ref = refs[0], refs[1:1 + n], refs[1 + n]
        o0 = 2 + n
        out, out_last, small_out = refs[o0:o0 + n], refs[o0 + n], refs[o0 + n + 1]
        s0 = o0 + n + 2
        halves, half_last = refs[s0:s0 + n], refs[s0 + n]
        own, got, part, exch, small_buf = refs[s0 + n + 1:s0 + n + 6]
        pair_send, pair_recv, chip_send, chip_recv, share_send, share_recv, small_send, small_recv, local_sems = refs[s0 + n + 6:]
        x, y, c = _place()
        sibling = (x, y, 1 - c)
        my_chip, me = 2 * x + y, _slot(x, y, c)
        chips = _other_chips(x, y)

        to_sibling = [_remote(g_ref.at[j, 1 - c], got.at[j], pair_send.at[j], pair_recv.at[j], sibling)
                      for j in range(N_CHIPS)]
        load_own = [pltpu.make_async_copy(g_ref.at[j, c], own.at[j], local_sems.at[j]) for j in range(N_CHIPS)]
        for cp in to_sibling + load_own:
            cp.start()

        small_buf[me] = small_ref[...]
        small_copies = []
        for mask in range(1, 8):
            peer = (x ^ (mask >> 2), y ^ ((mask >> 1) & 1), c ^ (mask & 1))
            small_copies.append(_remote(small_ref, small_buf.at[me], small_send.at[mask - 1], small_recv.at[mask - 1], peer))
        for cp in small_copies:
            cp.start()

        def share(k, half_ref, out_ref):
            keep = pltpu.make_async_copy(half_ref, out_ref.at[c], local_sems.at[N_CHIPS + k])
            give = _remote(half_ref, out_ref.at[c], share_send.at[k], share_recv.at[k], sibling)
            take = _remote(half_ref, out_ref.at[1 - c], share_send.at[k], share_recv.at[k], sibling)
            keep.start()
            give.start()
            return keep, give, take

        shares = []
        for k in range(n):
            halves[k][...] = _sum_blocks(ex[k])
            shares.append(share(k, halves[k], out[k]))

        for cp in to_sibling:
            cp.wait_recv()
        for cp in load_own:
            cp.wait()
        part[...] = (own[...] + got[...]).astype(BF16)
        exch[my_chip] = part[my_chip]
        to_chips = [_remote(part.at[2 * chip[0] + chip[1]], exch.at[my_chip], chip_send.at[j], chip_recv.at[j], (*chip, c))
                    for j, chip in enumerate(chips)]
        from_chips = [_remote(part.at[my_chip], exch.at[2 * chip[0] + chip[1]], chip_send.at[j], chip_recv.at[j], (*chip, c))
                      for j, chip in enumerate(chips)]
        for cp in to_chips:
            cp.start()

        for cp in small_copies:
            cp.wait_recv()
        total = small_buf[0]
        for d in range(1, 8):
            total = total + small_buf[d]
        small_out[...] = total

        for cp in from_chips:
            cp.wait_recv()
        half_last[...] = _sum_blocks(exch)
        shares.append(share(n, half_last, out_last))

        for keep, give, take in shares:
            take.wait_recv()
            give.wait_send()
            keep.wait()
        for cp in to_sibling + to_chips + small_copies:
            cp.wait_send()

    blocks = (N_CHIPS, h, D_MODEL)
    return pl.pallas_call(
        body, name="tail_reduce",
        in_specs=[ANY] + [VMEM_WHOLE] * (n + 1), out_specs=[ANY] * (n + 1) + [VMEM_WHOLE],
        out_shape=[SDS((2,) + e.shape[1:], F32) for e in exchanged] + [SDS((2, h, D_MODEL), F32), SDS(small.shape, F32)],
        scratch_shapes=[pltpu.VMEM(e.shape[1:], F32) for e in exchanged] + [pltpu.VMEM((h, D_MODEL), F32)]
                       + [pltpu.VMEM(blocks, F32), pltpu.VMEM(blocks, F32), pltpu.VMEM(blocks, BF16), pltpu.VMEM(blocks, BF16),
                          pltpu.VMEM((8,) + small.shape, F32)]
                       + [pltpu.SemaphoreType.DMA((N_CHIPS,)), pltpu.SemaphoreType.DMA((N_CHIPS,)),
                          pltpu.SemaphoreType.DMA((3,)), pltpu.SemaphoreType.DMA((3,)),
                          pltpu.SemaphoreType.DMA((n + 1,)), pltpu.SemaphoreType.DMA((n + 1,)),
                          pltpu.SemaphoreType.DMA((7,)), pltpu.SemaphoreType.DMA((7,)),
                          pltpu.SemaphoreType.DMA((N_CHIPS + n + 1,))],
        compiler_params=pltpu.CompilerParams(vmem_limit_bytes=VMEM_LIMIT_V7X),
    )(last_grads, *exchanged, small)


def _rope_expansion():
    half = ROT_DIM // 2
    expand = np.zeros((2 * half, 3 * 128), np.float32)
    const = np.zeros((1, 3 * 128), np.float32)
    for lane in range(128):
        d = lane % HEAD_DIM
        if d < ROT_DIM:
            expand[d % half, lane] = 1.0
        else:
            const[0, lane] = 1.0
        if d < half:
            expand[half + d, 128 + lane] = -1.0
        elif d < ROT_DIM:
            expand[half + d - half, 256 + lane] = 1.0
    return expand, const


ROPE_PIECES = 3 * ROT_DIM


def _rope_inputs(seq):
    pos = jnp.arange(seq, dtype=F32)
    inv_freq = ROPE_THETA ** (-jnp.arange(0, ROT_DIM, 2, dtype=F32) / ROT_DIM)
    ang = pos[:, None] * inv_freq[None, :]
    cs = jnp.concatenate([jnp.cos(ang), jnp.sin(ang)], axis=1)
    hi = lax.reduce_precision(cs, 8, 7)
    mid = lax.reduce_precision(cs - hi, 8, 7)
    low = cs - hi - mid
    expand, const = _rope_expansion()
    pieces = jnp.concatenate([hi, mid, low], axis=1).astype(BF16)
    return pieces, jnp.asarray(np.concatenate([expand] * 3, axis=0), BF16), jnp.asarray(const)


def _rope_specs(tb):
    return [pl.BlockSpec((tb, ROPE_PIECES), lambda i: (i, 0)), _resident((ROPE_PIECES, 3 * 128)), _resident((1, 3 * 128))]


def _rope_tile(pieces_ref, expand_ref, const_ref):
    tables = _dot(pieces_ref[...], expand_ref[...]) + const_ref[...]
    return tables[:, 0:128], tables[:, 128:256], tables[:, 256:384]


def _rope(t, c, sa, sb):
    half = ROT_DIM // 2
    return t * c + pltpu.roll(t, 128 - half, 1) * sa + pltpu.roll(t, half, 1) * sb


def _rope_transposed(dt, c, sa, sb):
    half = ROT_DIM // 2
    return dt * c + pltpu.roll(dt * sa, half, 1) + pltpu.roll(dt * sb, 128 - half, 1)


def _cast_halves(core, w_up, w_down, w_out, w_in_t):
    def body(core_ref, up_ref, down_ref, out_ref, in_ref, up_o, down_o, out_o, in_o):
        up_o[...] = up_ref[...].astype(BF16)
        down_o[...] = down_ref[...].astype(BF16)
        out_o[...] = out_ref[...].astype(BF16)
        in_o[...] = in_ref[...].astype(BF16)

    half = lambda rows: pl.BlockSpec((rows, D_MODEL), lambda i, core_ref: (core_ref[0], 0))
    whole = lambda rows: pl.BlockSpec((rows, D_MODEL), lambda i, core_ref: (0, 0))
    rows = (H_UP, H_DOWN, H_OUT, H_IN)
    return pl.pallas_call(
        body, name="cast_halves",
        grid_spec=pltpu.PrefetchScalarGridSpec(
            num_scalar_prefetch=1, grid=(1,), in_specs=[half(r) for r in rows], out_specs=[whole(r) for r in rows]),
        out_shape=[SDS((r, D_MODEL), BF16) for r in rows],
        compiler_params=pltpu.CompilerParams(dimension_semantics=("arbitrary",), vmem_limit_bytes=VMEM_LIMIT_V7X),
    )(core, w_up, w_down, w_out, w_in_t)


def _pre_norm(x, g_pre, comm=None):
    seq = x.shape[0]
    tb = TOKEN_TILE

    def body(x_ref, g_ref, hn_ref):
        xv = x_ref[...]
        hn_ref[...] = (xv * _rms(xv) * g_ref[...]).astype(BF16)

    tile = pl.BlockSpec((tb, D_MODEL), lambda i: (i, 0))
    return _pallas(body, name="pre_norm", grid=(seq // tb,), in_specs=[tile, _resident((1, D_MODEL))], out_specs=[tile],
                   out_shape=[SDS((seq, D_MODEL), BF16)], operands=(x, g_pre), comm=comm)


def _in_proj(hn, w_in_t, rope, comm=None):
    seq = hn.shape[0]
    tb = TOKEN_TILE

    def body(hn_ref, w_ref, c_ref, sa_ref, sb_ref,
             q_ref, kd0_ref, kd1_ref, vd0_ref, vd1_ref, gb_ref, gc_ref, xin_ref):
        proj = _dot_nt(hn_ref[...], w_ref[...].reshape(IN_COLS, D_MODEL))
        c, sa, sb = _rope_tile(c_ref, sa_ref, sb_ref)
        scale = 1.0 / math.sqrt(HEAD_DIM)
        for p in range(Q_WIDTH // 128):
            q_ref[:, 128 * p:128 * (p + 1)] = (_rope(proj[:, 128 * p:128 * (p + 1)], c, sa, sb) * scale).astype(BF16)
        k = _rope(proj[:, Q_WIDTH:Q_WIDTH + KV_WIDTH], c, sa, sb)
        v = proj[:, Q_WIDTH + KV_WIDTH:Q_WIDTH + 2 * KV_WIDTH]
        low = _lane_lt64(k.shape)
        k_sw, v_sw = pltpu.roll(k, HEAD_DIM, 1), pltpu.roll(v, HEAD_DIM, 1)
        kd0_ref[...] = jnp.where(low, k, k_sw).astype(BF16)
        kd1_ref[...] = jnp.where(low, k_sw, k).astype(BF16)
        vd0_ref[...] = jnp.where(low, v, v_sw).astype(BF16)
        vd1_ref[...] = jnp.where(low, v_sw, v).astype(BF16)
        base = Q_WIDTH + 2 * KV_WIDTH
        gb_ref[...] = proj[:, base:base + CONV_WIDTH].astype(BF16)
        gc_ref[...] = proj[:, base + CONV_WIDTH:base + 2 * CONV_WIDTH].astype(BF16)
        xin_ref[...] = proj[:, base + 2 * CONV_WIDTH:base + 3 * CONV_WIDTH].astype(BF16)

    tile = lambda w: pl.BlockSpec((tb, w), lambda i: (i, 0))
    return _pallas(
        body, name="in_proj", grid=(seq // tb,),
        in_specs=[tile(D_MODEL), _resident(w_in_t.shape), *_rope_specs(tb)],
        out_specs=[tile(Q_WIDTH), tile(128), tile(128), tile(128), tile(128),
                   tile(CONV_WIDTH), tile(CONV_WIDTH), tile(CONV_WIDTH)],
        out_shape=[SDS((seq, Q_WIDTH), BF16)] + [SDS((seq, 128), BF16)] * 4 + [SDS((seq, CONV_WIDTH), BF16)] * 3,
        operands=(hn, w_in_t, *rope), comm=comm)


def _attn_valid(i):
    shape = (4 * QBLOCK, 2 * QBLOCK)
    row = lax.broadcasted_iota(jnp.int32, shape, 0)
    col = lax.broadcasted_iota(jnp.int32, shape, 1)
    qi = row & (QBLOCK - 1)
    return (col > qi) & (col <= qi + QBLOCK) & ((col >= QBLOCK) | (i > 0))


def _stack_heads(pair0, pair1):
    low = _lane_lt64(pair0.shape)
    zero = jnp.zeros_like(pair0)
    return jnp.concatenate([jnp.where(low, pair0, zero), jnp.where(low, zero, pair0),
                            jnp.where(low, pair1, zero), jnp.where(low, zero, pair1)], axis=0)


def _unstack_heads(stacked):
    low = _lane_lt64((QBLOCK, 128))
    pair0 = jnp.where(low, stacked[0:QBLOCK], stacked[QBLOCK:2 * QBLOCK])
    pair1 = jnp.where(low, stacked[2 * QBLOCK:3 * QBLOCK], stacked[3 * QBLOCK:4 * QBLOCK])
    return pair0, pair1


def _sink_column(sink_ref, kv_head):
    row = lax.broadcasted_iota(jnp.int32, (4 * QBLOCK, 1), 0)
    s = [sink_ref[0, 4 * kv_head + j] for j in range(4)]
    return jnp.where(row < QBLOCK, s[0], jnp.where(row < 2 * QBLOCK, s[1], jnp.where(row < 3 * QBLOCK, s[2], s[3])))


def _band(ref, i):
    prev = pl.multiple_of(jnp.maximum(i - 1, 0) * QBLOCK, QBLOCK)
    own = pl.multiple_of(i * QBLOCK, QBLOCK)
    return jnp.concatenate([ref[pl.ds(prev, QBLOCK), :], ref[pl.ds(own, QBLOCK), :]], axis=0), prev, own


def _softmax_with_sink(s, sink_col):
    m = jnp.maximum(jnp.max(s, axis=-1, keepdims=True), sink_col)
    p = jnp.exp(s - m)
    e_sink = jnp.exp(sink_col - m)
    inv_l = 1.0 / (jnp.sum(p, axis=-1, keepdims=True) + e_sink)
    return p, e_sink, inv_l


def _attention_fwd(q, kd0, kd1, vd0, vd1, sinks, comm=None):
    seq = q.shape[0]

    nb = ATTN_FWD_BLOCKS

    def body(sink_ref, q_ref, kd0_ref, kd1_ref, vd0_ref, vd1_ref, o_ref):
        for b in range(nb):
            i = pl.program_id(0) * nb + b
            rows = slice(QBLOCK * b, QBLOCK * (b + 1))
            valid = _attn_valid(i)
            for kv_head, (k_ref, v_ref) in enumerate(((kd0_ref, vd0_ref), (kd1_ref, vd1_ref))):
                kband, _, _ = _band(k_ref, i)
                vband, _, _ = _band(v_ref, i)
                base = 256 * kv_head
                qm = _stack_heads(q_ref[rows, base:base + 128], q_ref[rows, base + 128:base + 256])
                s = jnp.where(valid, _dot_nt(qm, kband), NEG_INF)
                p, _, inv_l = _softmax_with_sink(s, _sink_column(sink_ref, kv_head))
                o = _dot(p.astype(BF16), vband) * inv_l
                pair0, pair1 = _unstack_heads(o)
                o_ref[rows, base:base + 128] = pair0.astype(BF16)
                o_ref[rows, base + 128:base + 256] = pair1.astype(BF16)

    blk = pl.BlockSpec((nb * QBLOCK, Q_WIDTH), lambda i: (i, 0))
    full = _resident((seq, 128))
    return _pallas(
        body, name="attention_fwd", grid=(seq // (nb * QBLOCK),),
        in_specs=[pl.BlockSpec(memory_space=pltpu.SMEM), blk, full, full, full, full],
        out_specs=[blk], out_shape=[SDS((seq, Q_WIDTH), BF16)],
        operands=(sinks, q, kd0, kd1, vd0, vd1), comm=comm)


HALO = 16


def _conv_parts(gc, xin, gc_halo, xin_halo, conv_w, first):
    tb = gc.shape[0]
    u = gc.astype(F32) * xin.astype(F32)
    u_halo = jnp.where(first, 0.0, gc_halo.astype(F32) * xin_halo.astype(F32))
    ext = jnp.concatenate([u_halo, u], axis=0)
    u1 = pltpu.roll(ext, 1, 0)[HALO:HALO + tb]
    u2 = pltpu.roll(ext, 2, 0)[HALO:HALO + tb]
    y = conv_w[0:1, :] * u2 + conv_w[1:2, :] * u1 + conv_w[2:3, :] * u
    return u, u1, u2, y


def _halo_prev(tb, w):
    return pl.BlockSpec((HALO, w), lambda i: (jnp.maximum(i * (tb // HALO) - 1, 0), 0))


def _residual_mid(x, mix, g_post_mix):
    mix_f = mix.astype(F32)
    return x + mix_f * _rms(mix_f) * g_post_mix


def _mix_out(attn, gb, gc, xin, conv_w, g_attn, g_conv, w_out, comm=None):
    seq = attn.shape[0]
    tb = TOKEN_TILE

    def body(a_ref, gb_ref, gc_ref, xin_ref, gch_ref, xinh_ref, cw_ref, ga_ref, gcn_ref, w_ref, mix_ref, mixed_ref):
        first = pl.program_id(0) == 0
        _, _, _, y = _conv_parts(gc_ref[...], xin_ref[...], gch_ref[...], xinh_ref[...], cw_ref[...], first)
        conv = gb_ref[...].astype(F32) * y
        a = a_ref[...].astype(F32)
        mixed_ref[:, 0:Q_WIDTH] = (a * _rms(a) * ga_ref[...]).astype(BF16)
        mixed_ref[:, Q_WIDTH:] = (conv * _rms(conv) * gcn_ref[...]).astype(BF16)
        mix_ref[...] = _dot(mixed_ref[...], w_ref[...].reshape(D_MODEL, D_MODEL)).astype(BF16)

    tile = lambda w: pl.BlockSpec((tb, w), lambda i: (i, 0))
    return _pallas(
        body, name="mix_out", grid=(seq // tb,),
        in_specs=[tile(Q_WIDTH), tile(CONV_WIDTH), tile(CONV_WIDTH), tile(CONV_WIDTH),
                  _halo_prev(tb, CONV_WIDTH), _halo_prev(tb, CONV_WIDTH),
                  _resident((CONV_K, CONV_WIDTH)), _resident((1, Q_WIDTH)), _resident((1, CONV_WIDTH)),
                  _resident(w_out.shape)],
        out_specs=[tile(D_MODEL), tile(D_MODEL)],
        out_shape=[SDS((seq, D_MODEL), BF16), SDS((seq, D_MODEL), BF16)],
        operands=(attn, gb, gc, xin, gc, xin, conv_w, g_attn, g_conv, w_out), comm=comm)


def _mlp_loss(x, mix, target, g_post_mix, g_pre_mlp, g_post_mlp, w_up, w_down):
    seq = x.shape[0]
    tb = TOKEN_TILE

    def body(x_ref, mix_ref, t_ref, gpm_ref, g2_ref, g4_ref, wup_ref, wdown_ref,
             up_ref, hn2_ref, dout_ref, dmlp_ref, loss_ref, dg4_ref, act_ref):
        @pl.when(pl.program_id(0) == 0)
        def _():
            loss_ref[...] = jnp.zeros_like(loss_ref)
            dg4_ref[...] = jnp.zeros_like(dg4_ref)

        halves = [slice(0, tb // 2), slice(tb // 2, tb)]
        hv, hn2 = [], []
        for rows in halves:
            hv.append(_residual_mid(x_ref[rows, :], mix_ref[rows, :], gpm_ref[...]))
            hn2.append((hv[-1] * _rms(hv[-1]) * g2_ref[...]).astype(BF16))
            hn2_ref[rows, :] = hn2[-1]
        for k, rows in enumerate(halves):
            for j in range(N_CHIPS):
                up = _dot(hn2[k], _chip_block(wup_ref, j))
                up = jnp.maximum(up, 0.0)
                up_ref[rows, 1024 * j:1024 * (j + 1)] = up.astype(BF16)
                act_ref[rows, 1024 * j:1024 * (j + 1)] = (up * up).astype(BF16)
        w_down_all = wdown_ref[...].reshape(D_FF, D_MODEL)
        loss = jnp.zeros((1, 1), F32)
        dg4 = jnp.zeros((1, D_MODEL), F32)
        for k, rows in enumerate(halves):
            mlp = _dot(act_ref[rows, :], w_down_all)
            rstd = _rms(mlp)
            zhat = mlp * rstd
            diff = hv[k] + zhat * g4_ref[...] - t_ref[rows, :]
            loss = loss + jnp.sum(jnp.sum(diff * diff, axis=1, keepdims=True), axis=0, keepdims=True)
            dout = diff * (1.0 / D_MODEL)
            dout_ref[rows, :] = dout
            dg4 = dg4 + _colsum(dout * zhat)
            dmlp_ref[rows, :] = _norm_bwd(dout, g4_ref[...], zhat, rstd).astype(BF16)
        loss_ref[...] += loss
        dg4_ref[...] += dg4

    tile = lambda w: pl.BlockSpec((tb, w), lambda i: (i, 0))
    return _pallas(
        body, name="mlp_loss", grid=(seq // tb,),
        in_specs=[tile(D_MODEL), tile(D_MODEL), tile(D_MODEL), _resident((1, D_MODEL)), _resident((1, D_MODEL)),
                  _resident((1, D_MODEL)), _resident(w_up.shape), _resident(w_down.shape)],
        out_specs=[tile(D_FF), tile(D_MODEL), tile(D_MODEL), tile(D_MODEL),
                   pl.BlockSpec((1, 1), lambda i: (0, 0)), pl.BlockSpec((1, D_MODEL), lambda i: (0, 0))],
        out_shape=[SDS((seq, D_FF), BF16), SDS((seq, D_MODEL), BF16), SDS((seq, D_MODEL), F32),
                   SDS((seq, D_MODEL), BF16), SDS((1, 1), F32), SDS((1, D_MODEL), F32)],
        scratch=[pltpu.VMEM((tb, D_FF), BF16)],
        operands=(x, mix, target, g_post_mix, g_pre_mlp, g_post_mlp, w_up, w_down))


def _mlp_bwd(dmlp, up, x, dout, mix, g_pre_mlp, g_post_mix, w_up, w_down):
    seq = x.shape[0]
    tb = MLP_BWD_TOKEN_TILE

    def body(dmlp_ref, up_ref, x_ref, dout_ref, mix_ref, g2_ref, gpm_ref, wup_ref, wdown_ref,
             dup_ref, dh_ref, dmix_ref, dg2_ref, dgpm_ref):
        @pl.when(pl.program_id(0) == 0)
        def _():
            dg2_ref[...] = jnp.zeros_like(dg2_ref)
            dgpm_ref[...] = jnp.zeros_like(dgpm_ref)

        subs = [slice(k * MLP_BWD_SUB_TILE, (k + 1) * MLP_BWD_SUB_TILE) for k in range(tb // MLP_BWD_SUB_TILE)]
        dhn2 = []
        for rows in subs:
            dmlp_v = dmlp_ref[rows, :]
            acc = None
            for j in range(N_CHIPS):
                cols = slice(1024 * j, 1024 * (j + 1))
                dact = _dot_nt(dmlp_v, _chip_block(wdown_ref, j))
                dup = (dact * (2.0 * up_ref[rows, cols].astype(F32))).astype(BF16)
                dup_ref[rows, cols] = dup
                part = _dot_nt(dup, _chip_block(wup_ref, j))
                acc = part if acc is None else acc + part
            dhn2.append(acc)
        dg2 = jnp.zeros((1, D_MODEL), F32)
        dgpm = jnp.zeros((1, D_MODEL), F32)
        for k, rows in enumerate(subs):
            mix_v = mix_ref[rows, :].astype(F32)
            hv = _residual_mid(x_ref[rows, :], mix_ref[rows, :], gpm_ref[...])
            r2 = _rms(hv)
            hhat = hv * r2
            dg2 = dg2 + _colsum(dhn2[k] * hhat)
            dh = dout_ref[rows, :] + _norm_bwd(dhn2[k], g2_ref[...], hhat, r2)
            dh_ref[rows, :] = dh.astype(BF16)
            rz = _rms(mix_v)
            zhat = mix_v * rz
            dgpm = dgpm + _colsum(dh * zhat)
            dmix_ref[rows, :] = _norm_bwd(dh, gpm_ref[...], zhat, rz).astype(BF16)
        dg2_ref[...] += dg2
        dgpm_ref[...] += dgpm

    tile = lambda w: pl.BlockSpec((tb, w), lambda i: (i, 0))
    vec = pl.BlockSpec((1, D_MODEL), lambda i: (0, 0))
    return _pallas(
        body, name="mlp_bwd", grid=(seq // tb,),
        in_specs=[tile(D_MODEL), tile(D_FF), tile(D_MODEL), tile(D_MODEL), tile(D_MODEL),
                  _resident((1, D_MODEL)), _resident((1, D_MODEL)), _resident(w_up.shape), _resident(w_down.shape)],
        out_specs=[tile(D_FF), tile(D_MODEL), tile(D_MODEL), vec, vec],
        out_shape=[SDS((seq, D_FF), BF16), SDS((seq, D_MODEL), BF16), SDS((seq, D_MODEL), BF16),
                   SDS((1, D_MODEL), F32), SDS((1, D_MODEL), F32)],
        operands=(dmlp, up, x, dout, mix, g_pre_mlp, g_post_mix, w_up, w_down))


class _Rider(NamedTuple):
    body: Callable
    in_specs: list
    out_specs: list
    out_shape: list
    operands: tuple


def _mix_bwd(dmix, attn, gb, gc, xin, conv_w, g_attn, g_conv, w_out, n_k):
    seq = attn.shape[0]
    tb = seq // (N_CHIPS * n_k)

    def body(first, dmix_ref, a_ref, gb_ref, gc_ref, xin_ref, gch_ref, xinh_ref, cw_ref, ga_ref, gcn_ref, w_ref,
             dattn_ref, dgb_ref, dy_ref, dga_ref, dgcn_ref, dcw_ref):
        @pl.when(first)
        def _():
            dga_ref[...] = jnp.zeros_like(dga_ref)
            dgcn_ref[...] = jnp.zeros_like(dgcn_ref)
            dcw_ref[...] = jnp.zeros_like(dcw_ref)

        dmixed = _dot_nt(dmix_ref[...], w_ref[...].reshape(D_MODEL, D_MODEL))
        a = a_ref[...].astype(F32)
        ra = _rms(a)
        ahat = a * ra
        dan = dmixed[:, 0:Q_WIDTH]
        dga_ref[...] += _colsum(dan * ahat)
        dattn_ref[...] = _norm_bwd(dan, ga_ref[...], ahat, ra).astype(BF16)
        gbv = gb_ref[...].astype(F32)
        u, u1, u2, y = _conv_parts(gc_ref[...], xin_ref[...], gch_ref[...], xinh_ref[...], cw_ref[...], first)
        conv = gbv * y
        rc = _rms(conv)
        chat = conv * rc
        dcn = dmixed[:, Q_WIDTH:]
        dgcn_ref[...] += _colsum(dcn * chat)
        dconv = _norm_bwd(dcn, gcn_ref[...], chat, rc)
        dgb_ref[...] = (dconv * y).astype(BF16)
        dy = dconv * gbv
        dy_ref[...] = dy.astype(BF16)
        dcw_ref[0:1, :] += _colsum(dy * u2)
        dcw_ref[1:2, :] += _colsum(dy * u1)
        dcw_ref[2:3, :] += _colsum(dy * u)

    tile = lambda w: pl.BlockSpec((tb, w), lambda j, k: (j * n_k + k, 0))
    halo = lambda w: pl.BlockSpec((HALO, w), lambda j, k: (jnp.maximum((j * n_k + k) * (tb // HALO) - 1, 0), 0))
    whole = lambda shape: pl.BlockSpec(shape, lambda j, k: (0,) * len(shape))
    return _Rider(
        body,
        in_specs=[tile(D_MODEL), tile(Q_WIDTH), tile(CONV_WIDTH), tile(CONV_WIDTH), tile(CONV_WIDTH),
                  halo(CONV_WIDTH), halo(CONV_WIDTH),
                  _resident((CONV_K, CONV_WIDTH)), _resident((1, Q_WIDTH)), _resident((1, CONV_WIDTH)),
                  _resident(w_out.shape)],
        out_specs=[tile(Q_WIDTH), tile(CONV_WIDTH), tile(CONV_WIDTH),
                   whole((1, Q_WIDTH)), whole((1, CONV_WIDTH)), whole((CONV_K, CONV_WIDTH))],
        out_shape=[SDS((seq, Q_WIDTH), BF16), SDS((seq, CONV_WIDTH), BF16), SDS((seq, CONV_WIDTH), BF16),
                   SDS((1, Q_WIDTH), F32), SDS((1, CONV_WIDTH), F32), SDS((CONV_K, CONV_WIDTH), F32)],
        operands=(dmix, attn, gb, gc, xin, gc, xin, conv_w, g_attn, g_conv, w_out))


def _attention_bwd(q, dattn, attn, kd0, kd1, vd0, vd1, sinks, comm=None):
    seq = q.shape[0]
    nb = ATTN_BWD_BLOCKS

    def body(sink_ref, q_ref, do_ref, o_ref, kd0_ref, kd1_ref, vd0_ref, vd1_ref,
             dq_ref, dk0_ref, dk1_ref, dv0_ref, dv1_ref, dsink_ref):
        @pl.when(pl.program_id(0) == 0)
        def _():
            for r in (dk0_ref, dk1_ref, dv0_ref, dv1_ref, dsink_ref):
                r[...] = jnp.zeros_like(r)

        lane = lax.broadcasted_iota(jnp.int32, (1, 128), 1)
        dsink = jnp.zeros((1, 128), F32)
        for b in range(nb):
            i = pl.program_id(0) * nb + b
            rows = slice(QBLOCK * b, QBLOCK * (b + 1))
            valid = _attn_valid(i)
            for kv_head, (k_ref, v_ref, dk_ref, dv_ref) in enumerate(
                    ((kd0_ref, vd0_ref, dk0_ref, dv0_ref), (kd1_ref, vd1_ref, dk1_ref, dv1_ref))):
                kband, prev, own = _band(k_ref, i)
                vband, _, _ = _band(v_ref, i)
                base = 256 * kv_head
                qm = _stack_heads(q_ref[rows, base:base + 128], q_ref[rows, base + 128:base + 256])
                dom = _stack_heads(do_ref[rows, base:base + 128], do_ref[rows, base + 128:base + 256])
                om = _stack_heads(o_ref[rows, base:base + 128], o_ref[rows, base + 128:base + 256])
                s = jnp.where(valid, _dot_nt(qm, kband), NEG_INF)
                p, e_sink, inv_l = _softmax_with_sink(s, _sink_column(sink_ref, kv_head))
                p = p * inv_l
                delta = jnp.sum(dom.astype(F32) * om.astype(F32), axis=-1, keepdims=True)
                ds = (p * (_dot_nt(dom, vband) - delta)).astype(BF16)
                sink_term = -(e_sink * inv_l) * delta
                for j in range(4):
                    part = jnp.sum(sink_term[QBLOCK * j:QBLOCK * (j + 1)], axis=0, keepdims=True)
                    dsink = dsink + jnp.where(lane == 4 * kv_head + j, part, 0.0)
                pair0, pair1 = _unstack_heads(_dot(ds, kband))
                dq_ref[rows, base:base + 128] = pair0.astype(BF16)
                dq_ref[rows, base + 128:base + 256] = pair1.astype(BF16)
                dkd = _dot_tn(ds, qm)
                dkd = dkd + pltpu.roll(dkd, HEAD_DIM, 1)
                dvd = _dot_tn(p.astype(BF16), dom)
                dvd = dvd + pltpu.roll(dvd, HEAD_DIM, 1)
                dk_ref[pl.ds(prev, QBLOCK), :] += dkd[0:QBLOCK]
                dk_ref[pl.ds(own, QBLOCK), :] += dkd[QBLOCK:]
                dv_ref[pl.ds(prev, QBLOCK), :] += dvd[0:QBLOCK]
                dv_ref[pl.ds(own, QBLOCK), :] += dvd[QBLOCK:]
        dsink_ref[...] += dsink

    blk = pl.BlockSpec((nb * QBLOCK, Q_WIDTH), lambda i: (i, 0))
    full = _resident((seq, 128))
    acc = pl.BlockSpec((seq, 128), lambda i: (0, 0))
    return _pallas(
        body, name="attention_bwd", grid=(seq // (nb * QBLOCK),),
        in_specs=[pl.BlockSpec(memory_space=pltpu.SMEM), blk, blk, blk, full, full, full, full],
        out_specs=[blk, acc, acc, acc, acc, pl.BlockSpec((1, 128), lambda i: (0, 0))],
        out_shape=[SDS((seq, Q_WIDTH), BF16)] + [SDS((seq, 128), F32)] * 4 + [SDS((1, 128), F32)],
        operands=(sinks, q, dattn, attn, kd0, kd1, vd0, vd1), comm=comm)


def _in_proj_bwd(dq, dk0, dk1, dv0, dv1, dgb, dy, gc, xin, conv_w, x, dh, g_pre, w_in_t, rope):
    seq = x.shape[0]
    tb = TOKEN_TILE
    n_tiles = seq // tb

    def body(dq_ref, dk0_ref, dk1_ref, dv0_ref, dv1_ref, dgb_ref, dy_ref, dyh_ref, gc_ref, xin_ref, cw_ref,
             x_ref, dh_ref, g_ref, w_ref, c_ref, sa_ref, sb_ref,
             dproj_ref, gx_ref, dg_ref):
        i = pl.program_id(0)

        @pl.when(i == 0)
        def _():
            dg_ref[...] = jnp.zeros_like(dg_ref)

        dy = dy_ref[...].astype(F32)
        ext = jnp.concatenate([dy, jnp.where(i == n_tiles - 1, 0.0, dyh_ref[...].astype(F32))], axis=0)
        dy1 = pltpu.roll(ext, tb + HALO - 1, 0)[0:tb]
        dy2 = pltpu.roll(ext, tb + HALO - 2, 0)[0:tb]
        cw = cw_ref[...]
        du = cw[2:3, :] * dy + cw[1:2, :] * dy1 + cw[0:1, :] * dy2
        scale = 1.0 / math.sqrt(HEAD_DIM)
        base = Q_WIDTH + 2 * KV_WIDTH
        halves = [slice(0, tb // 2), slice(tb // 2, tb)]
        low = _lane_lt64((tb // 2, 128))
        for rows in halves:
            c, sa, sb = _rope_tile(c_ref.at[rows, :], sa_ref, sb_ref)
            for p in range(Q_WIDTH // 128):
                dproj_ref[rows, 128 * p:128 * (p + 1)] = _rope_transposed(
                    dq_ref[rows, 128 * p:128 * (p + 1)].astype(F32) * scale, c, sa, sb).astype(BF16)
            dk = jnp.where(low, dk0_ref[rows, :], dk1_ref[rows, :])
            dproj_ref[rows, Q_WIDTH:Q_WIDTH + KV_WIDTH] = _rope_transposed(dk, c, sa, sb).astype(BF16)
            dproj_ref[rows, Q_WIDTH + KV_WIDTH:base] = jnp.where(low, dv0_ref[rows, :], dv1_ref[rows, :]).astype(BF16)
            dproj_ref[rows, base:base + CONV_WIDTH] = dgb_ref[rows, :]
            dproj_ref[rows, base + CONV_WIDTH:base + 2 * CONV_WIDTH] = (du[rows] * xin_ref[rows, :].astype(F32)).astype(BF16)
            dproj_ref[rows, base + 2 * CONV_WIDTH:] = (du[rows] * gc_ref[rows, :].astype(F32)).astype(BF16)
        w_all = w_ref[...].reshape(IN_COLS, D_MODEL)
        dhn = [_dot(dproj_ref[rows, :], w_all) for rows in halves]
        dg = jnp.zeros((1, D_MODEL), F32)
        for k, rows in enumerate(halves):
            xv = x_ref[rows, :]
            r = _rms(xv)
            xhat = xv * r
            dg = dg + _colsum(dhn[k] * xhat)
            gx_ref[rows, :] = dh_ref[rows, :].astype(F32) + _norm_bwd(dhn[k], g_ref[...], xhat, r)
        dg_ref[...] += dg

    tile = lambda w: pl.BlockSpec((tb, w), lambda i: (i, 0))
    halo_next = pl.BlockSpec((HALO, CONV_WIDTH), lambda i: (jnp.minimum((i + 1) * (tb // HALO), seq // HALO - 1), 0))
    return _pallas(
        body, name="in_proj_bwd", grid=(n_tiles,),
        in_specs=[tile(Q_WIDTH), tile(128), tile(128), tile(128), tile(128), tile(CONV_WIDTH), tile(CONV_WIDTH), halo_next,
                  tile(CONV_WIDTH), tile(CONV_WIDTH), _resident((CONV_K, CONV_WIDTH)),
                  tile(D_MODEL), tile(D_MODEL), _resident((1, D_MODEL)), _resident(w_in_t.shape), *_rope_specs(tb)],
        out_specs=[tile(IN_COLS), tile(D_MODEL), pl.BlockSpec((1, D_MODEL), lambda i: (0, 0))],
        out_shape=[SDS((seq, IN_COLS), BF16), SDS((seq, D_MODEL), F32), SDS((1, D_MODEL), F32)],
        operands=(dq, dk0, dk1, dv0, dv1, dgb, dy, dy, gc, xin, conv_w, x, dh, g_pre, w_in_t, *rope))


def _wgrad_grid(seq, per_chip, h_rows):
    chips_per_step = 1 if per_chip else N_CHIPS
    m = chips_per_step * 2 * h_rows
    bt = min(seq, WGRAD_TOKEN_TILE if m <= 1024 else WGRAD_TOKEN_TILE // 2)
    return chips_per_step, m, bt, seq // bt


def _wgrad(name, a, b, *, per_chip, h_rows, square_a=False, comm=None, rider=None):
    seq = a.shape[0]
    chips_per_step, m, bt, n_k = _wgrad_grid(seq, per_chip, h_rows)
    a_cols = m if per_chip else a.shape[1]
    a_wide = a.shape[1] > a_cols
    b_wide = b.shape[1] > D_MODEL
    n_ride_in = len(rider.in_specs) if rider else 0
    n_ride_out = len(rider.out_specs) if rider else 0

    def body(a_ref, b_ref, *rest):
        ride_in, g_ref = rest[:n_ride_in], rest[n_ride_in]
        ride_out, acc_ref = rest[n_ride_in + 1:n_ride_in + 1 + n_ride_out], rest[-1]
        k = pl.program_id(1)

        @pl.when(k == 0)
        def _():
            acc_ref[...] = jnp.zeros_like(acc_ref)

        av = a_ref[...]
        if square_a:
            av = (av.astype(F32) * av.astype(F32)).astype(BF16)
        acc_ref[...] += _dot_tn(av, b_ref[...])

        @pl.when(k == n_k - 1)
        def _():
            for cidx in range(chips_per_step):
                for half in range(2):
                    r0 = (2 * cidx + half) * h_rows
                    g_ref[cidx, half] = acc_ref[r0:r0 + h_rows, :]

        if rider:
            rider.body(jnp.logical_and(pl.program_id(0) == 0, k == 0), *ride_in, *ride_out)

    a_spec = pl.BlockSpec((bt, a_cols), (lambda j, k: (k, j)) if a_wide else (lambda j, k: (k, 0)))
    b_spec = pl.BlockSpec((bt, D_MODEL), (lambda j, k: (k, j)) if b_wide else (lambda j, k: (k, 0)))
    g_spec = pl.BlockSpec((chips_per_step, 2, h_rows, D_MODEL), lambda j, k: (j, 0, 0, 0))
    return _pallas(
        body, name=name, grid=(N_CHIPS if per_chip else 1, n_k),
        in_specs=[a_spec, b_spec] + (rider.in_specs if rider else []),
        out_specs=[g_spec] + (rider.out_specs if rider else []),
        out_shape=[SDS((N_CHIPS, 2, h_rows, D_MODEL), F32)] + (rider.out_shape if rider else []),
        scratch=[pltpu.VMEM((m, D_MODEL), F32)], operands=(a, b) + (rider.operands if rider else ()), comm=comm)


def _adamw_math(w, g, m, v):
    m = ADAM_B1 * m + (1.0 - ADAM_B1) * g
    v = ADAM_B2 * v + (1.0 - ADAM_B2) * (g * g)
    m_hat = m / (1.0 - ADAM_B1 ** ADAM_STEP)
    v_hat = v / (1.0 - ADAM_B2 ** ADAM_STEP)
    delta = -ADAM_LR * (m_hat / (jnp.sqrt(v_hat) + ADAM_EPS) + ADAM_WD * w)
    return delta, m, v


def _adamw_rows(name, reduced, w, m, v, rt):
    per_half = reduced.shape[1] // rt

    def body(r_ref, w_ref, m_ref, v_ref, g_out, d_out, m_out, v_out):
        g = r_ref[0]
        g_out[...] = g
        d_out[...], m_out[...], v_out[...] = _adamw_math(w_ref[...], g, m_ref[...], v_ref[...])

    blk = pl.BlockSpec((rt, D_MODEL), lambda h, r: (h * per_half + r, 0))
    return _pallas(
        body, name=name, grid=(2, per_half),
        in_specs=[pl.BlockSpec((1, rt, D_MODEL), lambda h, r: (h, r, 0)), blk, blk, blk],
        out_specs=[blk, blk, blk, blk], out_shape=[SDS(w.shape, F32)] * 4, operands=(reduced, w, m, v))


def _adamw_small(w, g, m, v):
    def body(w_ref, g_ref, m_ref, v_ref, d_out, m_out, v_out):
        d_out[...], m_out[...], v_out[...] = _adamw_math(w_ref[...], g_ref[...], m_ref[...], v_ref[...])

    return pl.pallas_call(body, name="adamw_small", in_specs=[VMEM_WHOLE] * 4, out_specs=[VMEM_WHOLE] * 3,
                          out_shape=[SDS(w.shape, F32)] * 3)(w, g, m, v)


SMALL_VECTORS = ("pre_mix_norm", "post_mix_norm", "pre_mlp_norm", "post_mlp_norm")
SMALL_NAMES = SMALL_VECTORS + ("attn_group_norm", "conv_group_norm", "conv_w", "attn_sinks")


def _pack_small(p):
    rows = [p[n].reshape(1, D_MODEL) for n in SMALL_VECTORS]
    rows.append(jnp.concatenate([p["attn_group_norm"].reshape(1, -1), p["conv_group_norm"].reshape(1, -1)], axis=1))
    cw = p["conv_w"].reshape(CONV_K, -1)
    rows.append(jnp.pad(cw, ((0, 1), (0, CONV_WIDTH - cw.shape[1]))).reshape(2, D_MODEL))
    last = jnp.concatenate([p["attn_sinks"].reshape(1, 8), p.get("loss_sum", jnp.zeros((1, 1), F32))], axis=1)
    rows.append(jnp.pad(last, ((0, 0), (0, D_MODEL - 9))))
    return jnp.concatenate(rows, axis=0)


def _unpack_small(packed, conv_width):
    out = {n: packed[i:i + 1] for i, n in enumerate(SMALL_VECTORS)}
    out["attn_group_norm"] = packed[4:5, :Q_WIDTH]
    out["conv_group_norm"] = packed[4:5, Q_WIDTH:]
    out["conv_w"] = packed[5:7].reshape(4, CONV_WIDTH)[:CONV_K, :conv_width].reshape(1, CONV_K, conv_width)
    out["attn_sinks"] = packed[7:8, :8]
    out["loss_sum"] = packed[7, 8]
    return out


WEIGHT_ORDER = ("pre_mix_norm", "w_in", "conv_w", "attn_sinks", "attn_group_norm", "conv_group_norm", "w_out",
                "post_mix_norm", "pre_mlp_norm", "w_up", "w_down", "post_mlp_norm")


def kernel(x, pre_mix_norm, w_in, conv_w, attn_sinks, attn_group_norm, conv_group_norm, w_out, post_mix_norm, pre_mlp_norm, w_up, w_down, post_mlp_norm, loss_target, m_pre_mix_norm, m_w_in, m_conv_w, m_attn_sinks, m_attn_group_norm, m_conv_group_norm, m_w_out, m_post_mix_norm, m_pre_mlp_norm, m_w_up, m_w_down, m_post_mlp_norm, v_pre_mix_norm, v_w_in, v_conv_w, v_attn_sinks, v_attn_group_norm, v_conv_group_norm, v_w_out, v_post_mix_norm, v_pre_mlp_norm, v_w_up, v_w_down, v_post_mlp_norm):
    w = dict(pre_mix_norm=pre_mix_norm, w_in=w_in, conv_w=conv_w, attn_sinks=attn_sinks, attn_group_norm=attn_group_norm,
             conv_group_norm=conv_group_norm, w_out=w_out, post_mix_norm=post_mix_norm, pre_mlp_norm=pre_mlp_norm,
             w_up=w_up, w_down=w_down, post_mlp_norm=post_mlp_norm)
    m = dict(pre_mix_norm=m_pre_mix_norm, w_in=m_w_in, conv_w=m_conv_w, attn_sinks=m_attn_sinks,
             attn_group_norm=m_attn_group_norm, conv_group_norm=m_conv_group_norm, w_out=m_w_out,
             post_mix_norm=m_post_mix_norm, pre_mlp_norm=m_pre_mlp_norm, w_up=m_w_up, w_down=m_w_down,
             post_mlp_norm=m_post_mlp_norm)
    v = dict(pre_mix_norm=v_pre_mix_norm, w_in=v_w_in, conv_w=v_conv_w, attn_sinks=v_attn_sinks,
             attn_group_norm=v_attn_group_norm, conv_group_norm=v_conv_group_norm, w_out=v_w_out,
             post_mix_norm=v_post_mix_norm, pre_mlp_norm=v_pre_mlp_norm, w_up=v_w_up, w_down=v_w_down,
             post_mlp_norm=v_post_mlp_norm)
    core = lax.axis_index("c").astype(jnp.int32).reshape(1)
    chip = 2 * lax.axis_index("x") + lax.axis_index("y")
    local_conv = conv_w.shape[2]
    xs, target = x[0], loss_target[0]
    rope = _rope_inputs(xs.shape[0])

    hb_up, hb_down, hb_out, hb_in = _cast_halves(core, w_up[0], w_down[0], w_out[0], w_in[0].T)
    conv_pad = jnp.pad(conv_w[0], ((0, 8 - CONV_K), (0, 0)))
    hn, wf_in = _pre_norm(xs, pre_mix_norm, comm=_gather_legs(hb_in, None, first=(0, H_IN), second_after=(0, H_IN)))
    *proj, wf_out, wf_up, conv_all = _in_proj(
        hn, wf_in, rope, comm=_merge(_gather_legs(hb_out, None, first=(0, H_OUT)),
                                     _gather_legs(hb_up, None, first=(0, UP_SPLIT)), _gather_small(conv_pad)))
    conv_full = conv_all[:, :CONV_K, :].transpose(1, 0, 2).reshape(CONV_K, CONV_WIDTH)
    q, kd0, kd1, vd0, vd1, gb, gc, xin = proj
    attn, wf_out, wf_up, wf_down = _attention_fwd(
        q, kd0, kd1, vd0, vd1, attn_sinks,
        comm=_merge(_gather_legs(None, wf_out, second=(0, H_OUT)),
                    _gather_legs(hb_up, wf_up, first=(UP_SPLIT, H_UP - UP_SPLIT), second=(0, UP_SPLIT)),
                    _gather_legs(hb_down, None, first=(0, DOWN_SPLIT))))
    down_rest = (DOWN_SPLIT, H_DOWN - DOWN_SPLIT)
    mix, mixed, wf_up, wf_down = _mix_out(
        attn, gb, gc, xin, conv_full, attn_group_norm, conv_group_norm, wf_out,
        comm=_merge(_gather_legs(None, wf_up, second=(UP_SPLIT, H_UP - UP_SPLIT)),
                    _gather_legs(hb_down, wf_down, first=down_rest, second=(0, DOWN_SPLIT), second_after=down_rest)))
    up, hn2, dout, dmlp, loss_sum, dg_post_mlp = _mlp_loss(xs, mix, target, post_mix_norm, pre_mlp_norm, post_mlp_norm,
                                                           wf_up, wf_down)

    dup, dh, dmix, dg_pre_mlp, dg_post_mix = _mlp_bwd(dmlp, up, xs, dout, mix, pre_mlp_norm, post_mix_norm, wf_up, wf_down)
    n_k = _wgrad_grid(xs.shape[0], True, H_DOWN)[3]
    g_down, dattn, dgb, dy, dg_attn, dg_conv, dconv_w = _wgrad(
        "wgrad_down", up, dmlp, per_chip=True, h_rows=H_DOWN, square_a=True,
        rider=_mix_bwd(dmix, attn, gb, gc, xin, conv_full, attn_group_norm, conv_group_norm, wf_out, n_k))
    g_up, got_down = _wgrad("wgrad_up", hn2, dup, per_chip=True, h_rows=H_UP, comm=_pair_send(g_down))
    p_down = _pair_sum("pair_sum_down", core, g_down, got_down)
    g_out, got_up = _wgrad("wgrad_out", mixed, dmix, per_chip=False, h_rows=H_OUT, comm=_pair_send(g_up))
    p_up = _pair_sum("pair_sum_up", core, g_up, got_up)
    dq, dk0, dk1, dv0, dv1, dsink, ex_down, ex_up, got_out = _attention_bwd(
        q, dattn, attn, kd0, kd1, vd0, vd1, attn_sinks,
        comm=_merge(_chip_exchange(p_down), _chip_exchange(p_up), _pair_send(g_out)))
    p_out = _pair_sum("pair_sum_out", core, g_out, got_out)
    dproj, grad_x, dg_pre_mix = _in_proj_bwd(dq, dk0, dk1, dv0, dv1, dgb, dy, gc, xin, conv_full, xs, dh, pre_mix_norm,
                                             wf_in, rope)
    g_in, ex_out = _wgrad("wgrad_in", dproj, hn, per_chip=False, h_rows=H_IN, comm=_chip_exchange(p_out))
    small = dict(pre_mix_norm=dg_pre_mix, conv_w=dconv_w, attn_sinks=dsink[:, :8], attn_group_norm=dg_attn,
                 conv_group_norm=dg_conv, post_mix_norm=dg_post_mix, pre_mlp_norm=dg_pre_mlp, post_mlp_norm=dg_post_mlp,
                 loss_sum=loss_sum)
    r_down, r_up, r_out, r_in, small_total = _tail_reduce(g_in, [ex_down, ex_up, ex_out], _pack_small(small))

    out_g, out_d, out_m, out_v = {}, {}, {}, {}
    out_g["w_up"], out_d["w_up"], out_m["w_up"], out_v["w_up"] = _adamw_rows(
        "adamw_up", r_up, w_up[0], m_w_up[0], v_w_up[0], 256)
    out_g["w_down"], out_d["w_down"], out_m["w_down"], out_v["w_down"] = _adamw_rows(
        "adamw_down", r_down, w_down[0], m_w_down[0], v_w_down[0], 256)
    out_g["w_out"], out_d["w_out"], out_m["w_out"], out_v["w_out"] = _adamw_rows(
        "adamw_out", r_out, w_out[0], m_w_out[0], v_w_out[0], H_OUT)
    in_t = _adamw_rows("adamw_in", r_in, w_in[0].T, m_w_in[0].T, v_w_in[0].T, H_IN)
    out_g["w_in"], out_d["w_in"], out_m["w_in"], out_v["w_in"] = [t.T for t in in_t]

    small_sum = _unpack_small(small_total, CONV_WIDTH)
    loss = small_sum["loss_sum"] * (0.5 / D_MODEL)
    small_sum["conv_w"] = lax.dynamic_slice_in_dim(small_sum["conv_w"], chip * local_conv, local_conv, axis=2)
    packed = [_pack_small({n: t[n] for n in SMALL_NAMES}) for t in (w, small_sum, m, v)]
    small_d, small_m, small_v = [_unpack_small(t, local_conv) for t in _adamw_small(*packed)]
    for n in SMALL_NAMES:
        out_g[n], out_d[n], out_m[n], out_v[n] = small_sum[n], small_d[n], small_m[n], small_v[n]

    def shaped(d):
        return [d[n].reshape(w[n].shape) for n in WEIGHT_ORDER]

    return (loss, grad_x[None], *shaped(out_g), *shaped(out_d), *shaped(out_m), *shaped(out_v))
```

```python
import math
from typing import Callable, NamedTuple

import jax
import jax.numpy as jnp
import numpy as np
from jax import lax
from jax.experimental import pallas as pl
from jax.experimental.pallas import tpu as pltpu

F32 = jnp.float32
BF16 = jnp.bfloat16

D_MODEL = 1024
HEAD_DIM = 64
Q_WIDTH = 512
KV_WIDTH = 128
CONV_WIDTH = 512
CONV_K = 3
D_FF = 4096
IN_COLS = 2304
QBLOCK = 128
ROT_DIM = 16
ROPE_THETA = 500000.0
NORM_EPS = 1e-6
NEG_INF = -1e30
N_CHIPS = 4

ADAM_LR = 0.001
ADAM_B1 = 0.9
ADAM_B2 = 0.999
ADAM_EPS = 1e-08
ADAM_WD = 0.01
ADAM_STEP = 10

H_UP, H_DOWN, H_OUT, H_IN = 512, 512, 128, 288
UP_SPLIT = 256
DOWN_SPLIT = 192

TOKEN_TILE = 512
MLP_BWD_TOKEN_TILE = 512
MLP_BWD_SUB_TILE = 256
ATTN_FWD_BLOCKS = 4
ATTN_BWD_BLOCKS = 2
WGRAD_TOKEN_TILE = 2048
VMEM_LIMIT_V7X = 56 * 1024 * 1024

MESH = pl.DeviceIdType.MESH
ANY = pl.BlockSpec(memory_space=pl.ANY)
VMEM_WHOLE = pl.BlockSpec(memory_space=pltpu.VMEM)
SDS = jax.ShapeDtypeStruct


def _resident(shape):
    zeros = (0,) * len(shape)
    return pl.BlockSpec(shape, lambda *_: zeros, pipeline_mode=pl.Buffered(1))


def _rms(v):
    return lax.rsqrt(jnp.mean(v * v, axis=-1, keepdims=True) + NORM_EPS)


def _norm_bwd(dy, gain, vhat, rstd):
    t = dy * gain
    return rstd * (t - vhat * jnp.mean(t * vhat, axis=-1, keepdims=True))


def _colsum(v):
    return jnp.sum(v, axis=0, keepdims=True)


def _dot_nt(a, b):
    return lax.dot_general(a, b, (((1,), (1,)), ((), ())), preferred_element_type=F32)


def _dot_tn(a, b):
    return lax.dot_general(a, b, (((0,), (0,)), ((), ())), preferred_element_type=F32)


def _dot(a, b):
    return jnp.dot(a, b, preferred_element_type=F32)


def _chip_block(w_ref, chip):
    both = w_ref[pl.ds(2 * chip, 2)]
    return both.reshape(2 * both.shape[1], both.shape[2])


def _lane_lt64(shape):
    return lax.broadcasted_iota(jnp.int32, shape, 1) < HEAD_DIM


class _Comm(NamedTuple):
    operands: tuple
    out_shapes: tuple
    aliases: dict
    n_remote: int
    n_local: int
    plan: Callable
    after: Callable = None


def _merge(*comms):
    operands, out_shapes, aliases, parts = [], [], {}, []
    n_remote = n_local = 0
    for cm in comms:
        parts.append((len(operands), len(out_shapes), n_remote, n_local, cm))
        for k, v in cm.aliases.items():
            aliases[len(operands) + k] = len(out_shapes) + v
        operands += cm.operands
        out_shapes += cm.out_shapes
        n_remote += cm.n_remote
        n_local += cm.n_local

    def run(which, ins, outs, send, recv, loc):
        sends, recvs, locs = [], [], []
        for i0, o0, r0, l0, cm in parts:
            fn = getattr(cm, which)
            if fn is not None:
                s, r, l = fn(ins[i0:i0 + len(cm.operands)], outs[o0:o0 + len(cm.out_shapes)],
                             lambda k, r0=r0: send(r0 + k), lambda k, r0=r0: recv(r0 + k), lambda k, l0=l0: loc(l0 + k))
                sends, recvs, locs = sends + s, recvs + r, locs + l
        return sends, recvs, locs

    def plan(*args):
        return run("plan", *args)

    def after(*args):
        return run("after", *args)

    return _Comm(tuple(operands), tuple(out_shapes), aliases, n_remote, n_local, plan,
                 after if any(cm.after is not None for cm in comms) else None)


def _sem_scratch(comm):
    return [pltpu.SemaphoreType.DMA((max(comm.n_remote, 1),)), pltpu.SemaphoreType.DMA((max(comm.n_remote, 1),)),
            pltpu.SemaphoreType.DMA((max(comm.n_local, 1),))]


def _pallas(body, *, name, grid, in_specs, out_specs, out_shape, operands, scratch=(), comm=None):
    params = pltpu.CompilerParams(dimension_semantics=("arbitrary",) * len(grid), vmem_limit_bytes=VMEM_LIMIT_V7X)
    if comm is None:
        return pl.pallas_call(body, name=name, grid=grid, in_specs=in_specs, out_specs=out_specs, out_shape=out_shape,
                              scratch_shapes=list(scratch), compiler_params=params)(*operands)
    n_in, n_out, n_scr = len(in_specs), len(out_specs), len(scratch)
    c_in, c_out = len(comm.operands), len(comm.out_shapes)

    def with_comm(*refs):
        ins, c_ins = refs[:n_in], refs[n_in:n_in + c_in]
        o0 = n_in + c_in
        outs, c_outs = refs[o0:o0 + n_out], refs[o0 + n_out:o0 + n_out + c_out]
        s0 = o0 + n_out + c_out
        scr = refs[s0:s0 + n_scr]
        send_sems, recv_sems, local_sems = refs[s0 + n_scr:]
        first = last = None
        for axis, size in enumerate(grid):
            at_start, at_end = pl.program_id(axis) == 0, pl.program_id(axis) == size - 1
            first = at_start if first is None else jnp.logical_and(first, at_start)
            last = at_end if last is None else jnp.logical_and(last, at_end)

        def copies():
            return comm.plan(c_ins, c_outs, lambda k: send_sems.at[k], lambda k: recv_sems.at[k],
                             lambda k: local_sems.at[k])

        @pl.when(first)
        def _():
            sends, _, locs = copies()
            for cp in sends + locs:
                cp.start()

        body(*ins, *outs, *scr)

        @pl.when(last)
        def _():
            sends, recvs, locs = copies()
            for cp in recvs:
                cp.wait_recv()
            for cp in sends:
                cp.wait_send()
            for cp in locs:
                cp.wait()
            if comm.after is not None:
                sends, recvs, _ = comm.after(c_ins, c_outs, lambda k: send_sems.at[k], lambda k: recv_sems.at[k],
                                             lambda k: local_sems.at[k])
                for cp in sends:
                    cp.start()
                for cp in recvs:
                    cp.wait_recv()
                for cp in sends:
                    cp.wait_send()

    return pl.pallas_call(
        with_comm, name=name, grid=grid,
        in_specs=list(in_specs) + [ANY] * c_in, out_specs=list(out_specs) + [ANY] * c_out,
        out_shape=list(out_shape) + list(comm.out_shapes),
        scratch_shapes=list(scratch) + _sem_scratch(comm),
        input_output_aliases={n_in + k: n_out + v for k, v in comm.aliases.items()},
        compiler_params=params)(*operands, *comm.operands)


def _place():
    return lax.axis_index("x"), lax.axis_index("y"), lax.axis_index("c")


def _other_chips(x, y):
    return [(1 - x, y), (x, 1 - y), (1 - x, 1 - y)]


def _slot(px, py, pc):
    return 4 * px + 2 * py + pc


def _remote(src, dst, send_sem, recv_sem, to):
    return pltpu.make_async_remote_copy(src_ref=src, dst_ref=dst, send_sem=send_sem, recv_sem=recv_sem,
                                        device_id=to, device_id_type=MESH)


def _gather_legs(half_block, so_far, first=None, second=None, second_after=None):
    has_block, has_buffer = half_block is not None, so_far is not None
    shape = so_far.shape if has_buffer else (2 * N_CHIPS,) + half_block.shape
    dtype = so_far.dtype if has_buffer else half_block.dtype

    def forward(rows, base, ins, outs, send, recv):
        src = ins[-1] if has_buffer else outs[0]
        full = outs[0]
        x, y, c = _place()
        chips = _other_chips(x, y)
        span = pl.ds(*rows)
        sends = [_remote(src.at[_slot(*chip, c), span], full.at[_slot(*chip, c), span], send(base + j), recv(base + j),
                         (x, y, 1 - c)) for j, chip in enumerate(chips)]
        recvs = [_remote(src.at[_slot(*chip, 1 - c), span], full.at[_slot(*chip, 1 - c), span], send(base + j),
                         recv(base + j), (x, y, 1 - c)) for j, chip in enumerate(chips)]
        return sends, recvs

    def plan(ins, outs, send, recv, loc):
        sends, recvs, locs = [], [], []
        x, y, c = _place()
        if first is not None:
            blk, full, span = ins[0].at[pl.ds(*first)], outs[0], pl.ds(*first)
            chips = _other_chips(x, y)
            mine = full.at[_slot(x, y, c), span]
            sends += [_remote(blk, mine, send(0), recv(0), (x, y, 1 - c))]
            sends += [_remote(blk, mine, send(1 + j), recv(1 + j), (*chip, c)) for j, chip in enumerate(chips)]
            recvs += [_remote(blk, full.at[_slot(x, y, 1 - c), span], send(0), recv(0), (x, y, 1 - c))]
            recvs += [_remote(blk, full.at[_slot(*chip, c), span], send(1 + j), recv(1 + j), (*chip, c))
                      for j, chip in enumerate(chips)]
            locs += [pltpu.make_async_copy(blk, mine, loc(0))]
        if second is not None:
            s, r = forward(second, 4, ins, outs, send, recv)
            sends, recvs = sends + s, recvs + r
        return sends, recvs, locs

    def after(ins, outs, send, recv, loc):
        s, r = forward(second_after, 7, ins, outs, send, recv)
        return s, r, []

    operands = ((half_block,) if has_block else ()) + ((so_far,) if has_buffer else ())
    return _Comm(operands, (SDS(shape, dtype),), {len(operands) - 1: 0} if has_buffer else {}, 10, 1, plan,
                 after if second_after is not None else None)


def _gather_small(block):
    def plan(ins, outs, send, recv, loc):
        (blk,), (full,) = ins, outs
        x, y, c = _place()
        chips = _other_chips(x, y)
        sends = [_remote(blk, full.at[2 * x + y], send(j), recv(j), (*chip, c)) for j, chip in enumerate(chips)]
        recvs = [_remote(blk, full.at[2 * chip[0] + chip[1]], send(j), recv(j), (*chip, c))
                 for j, chip in enumerate(chips)]
        return sends, recvs, [pltpu.make_async_copy(blk, full.at[2 * x + y], loc(0))]

    return _Comm((block,), (SDS((N_CHIPS,) + block.shape, block.dtype),), {}, 3, 1, plan)


def _pair_send(grads):
    def plan(ins, outs, send, recv, loc):
        (g,), (got,) = ins, outs
        x, y, c = _place()
        copies = [_remote(g.at[j, 1 - c], got.at[j], send(j), recv(j), (x, y, 1 - c)) for j in range(N_CHIPS)]
        return copies, copies, []

    shape = (grads.shape[0],) + grads.shape[2:]
    return _Comm((grads,), (SDS(shape, grads.dtype),), {}, N_CHIPS, 0, plan)


def _chip_exchange(partial):
    def plan(ins, outs, send, recv, loc):
        (p,), (got,) = ins, outs
        x, y, c = _place()
        my_chip = 2 * x + y
        chips = _other_chips(x, y)
        sends = [_remote(p.at[2 * chip[0] + chip[1]], got.at[my_chip], send(j), recv(j), (*chip, c))
                 for j, chip in enumerate(chips)]
        recvs = [_remote(p.at[my_chip], got.at[2 * chip[0] + chip[1]], send(j), recv(j), (*chip, c))
                 for j, chip in enumerate(chips)]
        return sends, recvs, [pltpu.make_async_copy(p.at[my_chip], got.at[my_chip], loc(0))]

    return _Comm((partial,), (SDS(partial.shape, partial.dtype),), {}, 3, 1, plan)


def _pair_sum(name, core, grads, received):
    h = grads.shape[2]

    def body(core_ref, g_ref, r_ref, o_ref):
        o_ref[...] = (g_ref[0] + r_ref[...]).astype(BF16)

    return pl.pallas_call(
        body, name=name,
        grid_spec=pltpu.PrefetchScalarGridSpec(
            num_scalar_prefetch=1, grid=(N_CHIPS,),
            in_specs=[pl.BlockSpec((1, 1, h, D_MODEL), lambda j, core_ref: (j, core_ref[0], 0, 0)),
                      pl.BlockSpec((1, h, D_MODEL), lambda j, core_ref: (j, 0, 0))],
            out_specs=pl.BlockSpec((1, h, D_MODEL), lambda j, core_ref: (j, 0, 0))),
        out_shape=SDS((N_CHIPS, h, D_MODEL), BF16),
        compiler_params=pltpu.CompilerParams(dimension_semantics=("arbitrary",), vmem_limit_bytes=VMEM_LIMIT_V7X),
    )(core, grads, received)


SMALL_ROWS = 8


def _sum_blocks(ref):
    return (ref[0].astype(F32) + ref[1].astype(F32)) + (ref[2].astype(F32) + ref[3].astype(F32))


def _tail_reduce(last_grads, exchanged, small):
    n = len(exchanged)
    h = last_grads.shape[2]

    def body(*refs):
        g_ref, ex, small_ref = refs[0], refs[1:1 + n], refs[1 + n]
        o0 = 2 + n
        out, out_last, small_out = refs[o0:o0 + n], refs[o0 + n], refs[o0 + n + 1]
        s0 = o0 + n + 2
        halves, half_last = refs[s0:s0 + n], refs[s0 + n]
        own, got, part, exch, small_buf = refs[s0 + n + 1:s0 + n + 6]
        pair_send, pair_recv, chip_send, chip_recv, share_send, share_recv, small_send, small_recv, local_sems = refs[s0 + n + 6:]
        x, y, c = _place()
        sibling = (x, y, 1 - c)
        my_chip, me = 2 * x + y, _slot(x, y, c)
        chips = _other_chips(x, y)

        to_sibling = [_remote(g_ref.at[j, 1 - c], got.at[j], pair_send.at[j], pair_recv.at[j], sibling)
                      for j in range(N_CHIPS)]
        load_own = [pltpu.make_async_copy(g_ref.at[j, c], own.at[j], local_sems.at[j]) for j in range(N_CHIPS)]
        for cp in to_sibling + load_own:
            cp.start()

        small_buf[me] = small_ref[...]
        small_copies = []
        for mask in range(1, 8):
            peer = (x ^ (mask >> 2), y ^ ((mask >> 1) & 1), c ^ (mask & 1))
            small_copies.append(_remote(small_ref, small_buf.at[me], small_send.at[mask - 1], small_recv.at[mask - 1], peer))
        for cp in small_copies:
            cp.start()

        def share(k, half_ref, out_ref):
            keep = pltpu.make_async_copy(half_ref, out_ref.at[c], local_sems.at[N_CHIPS + k])
            give = _remote(half_ref, out_ref.at[c], share_send.at[k], share_recv.at[k], sibling)
            take = _remote(half_ref, out_ref.at[1 - c], share_send.at[k], share_recv.at[k], sibling)
            keep.start()
            give.start()
            return keep, give, take

        shares = []
        for k in range(n):
            halves[k][...] = _sum_blocks(ex[k])
            shares.append(share(k, halves[k], out[k]))

        for cp in to_sibling:
            cp.wait_recv()
        for cp in load_own:
            cp.wait()
        part[...] = (own[...] + got[...]).astype(BF16)
        exch[my_chip] = part[my_chip]
        to_chips = [_remote(part.at[2 * chip[0] + chip[1]], exch.at[my_chip], chip_send.at[j], chip_recv.at[j], (*chip, c))
                    for j, chip in enumerate(chips)]
        from_chips = [_remote(part.at[my_chip], exch.at[2 * chip[0] + chip[1]], chip_send.at[j], chip_recv.at[j], (*chip, c))
                      for j, chip in enumerate(chips)]
        for cp in to_chips:
            cp.start()

        for cp in small_copies:
            cp.wait_recv()
        total = small_buf[0]
        for d in range(1, 8):
            total = total + small_buf[d]
        small_out[...] = total

        for cp in from_chips:
            cp.wait_recv()
        half_last[...] = _sum_blocks(exch)
        shares.append(share(n, half_last, out_last))

        for keep, give, take in shares:
            take.wait_recv()
            give.wait_send()
            keep.wait()
        for cp in to_sibling + to_chips + small_copies:
            cp.wait_send()

    blocks = (N_CHIPS, h, D_MODEL)
    return pl.pallas_call(
        body, name="tail_reduce",
        in_specs=[ANY] + [VMEM_WHOLE] * (n + 1), out_specs=[ANY] * (n + 1) + [VMEM_WHOLE],
        out_shape=[SDS((2,) + e.shape[1:], F32) for e in exchanged] + [SDS((2, h, D_MODEL), F32), SDS(small.shape, F32)],
        scratch_shapes=[pltpu.VMEM(e.shape[1:], F32) for e in exchanged] + [pltpu.VMEM((h, D_MODEL), F32)]
                       + [pltpu.VMEM(blocks, F32), pltpu.VMEM(blocks, F32), pltpu.VMEM(blocks, BF16), pltpu.VMEM(blocks, BF16),
                          pltpu.VMEM((8,) + small.shape, F32)]
                       + [pltpu.SemaphoreType.DMA((N_CHIPS,)), pltpu.SemaphoreType.DMA((N_CHIPS,)),
                          pltpu.SemaphoreType.DMA((3,)), pltpu.SemaphoreType.DMA((3,)),
                          pltpu.SemaphoreType.DMA((n + 1,)), pltpu.SemaphoreType.DMA((n + 1,)),
                          pltpu.SemaphoreType.DMA((7,)), pltpu.SemaphoreType.DMA((7,)),
                          pltpu.SemaphoreType.DMA((N_CHIPS + n + 1,))],
        compiler_params=pltpu.CompilerParams(vmem_limit_bytes=VMEM_LIMIT_V7X),
    )(last_grads, *exchanged, small)


def _rope_expansion():
    half = ROT_DIM // 2
    expand = np.zeros((2 * half, 3 * 128), np.float32)
    const = np.zeros((1, 3 * 128), np.float32)
    for lane in range(128):
        d = lane % HEAD_DIM
        if d < ROT_DIM:
            expand[d % half, lane] = 1.0
        else:
            const[0, lane] = 1.0
        if d < half:
            expand[half + d, 128 + lane] = -1.0
        elif d < ROT_DIM:
            expand[half + d - half, 256 + lane] = 1.0
    return expand, const


ROPE_PIECES = 3 * ROT_DIM


def _rope_inputs(seq):
    pos = jnp.arange(seq, dtype=F32)
    inv_freq = ROPE_THETA ** (-jnp.arange(0, ROT_DIM, 2, dtype=F32) / ROT_DIM)
    ang = pos[:, None] * inv_freq[None, :]
    cs = jnp.concatenate([jnp.cos(ang), jnp.sin(ang)], axis=1)
    hi = lax.reduce_precision(cs, 8, 7)
    mid = lax.reduce_precision(cs - hi, 8, 7)
    low = cs - hi - mid
    expand, const = _rope_expansion()
    pieces = jnp.concatenate([hi, mid, low], axis=1).astype(BF16)
    return pieces, jnp.asarray(np.concatenate([expand] * 3, axis=0), BF16), jnp.asarray(const)


def _rope_specs(tb):
    return [pl.BlockSpec((tb, ROPE_PIECES), lambda i: (i, 0)), _resident((ROPE_PIECES, 3 * 128)), _resident((1, 3 * 128))]


def _rope_tile(pieces_ref, expand_ref, const_ref):
    tables = _dot(pieces_ref[...], expand_ref[...]) + const_ref[...]
    return tables[:, 0:128], tables[:, 128:256], tables[:, 256:384]


def _rope(t, c, sa, sb):
    half = ROT_DIM // 2
    return t * c + pltpu.roll(t, 128 - half, 1) * sa + pltpu.roll(t, half, 1) * sb


def _rope_transposed(dt, c, sa, sb):
    half = ROT_DIM // 2
    return dt * c + pltpu.roll(dt * sa, half, 1) + pltpu.roll(dt * sb, 128 - half, 1)


def _cast_halves(core, w_up, w_down, w_out, w_in_t):
    def body(core_ref, up_ref, down_ref, out_ref, in_ref, up_o, down_o, out_o, in_o):
        up_o[...] = up_ref[...].astype(BF16)
        down_o[...] = down_ref[...].astype(BF16)
        out_o[...] = out_ref[...].astype(BF16)
        in_o[...] = in_ref[...].astype(BF16)

    half = lambda rows: pl.BlockSpec((rows, D_MODEL), lambda i, core_ref: (core_ref[0], 0))
    whole = lambda rows: pl.BlockSpec((rows, D_MODEL), lambda i, core_ref: (0, 0))
    rows = (H_UP, H_DOWN, H_OUT, H_IN)
    return pl.pallas_call(
        body, name="cast_halves",
        grid_spec=pltpu.PrefetchScalarGridSpec(
            num_scalar_prefetch=1, grid=(1,), in_specs=[half(r) for r in rows], out_specs=[whole(r) for r in rows]),
        out_shape=[SDS((r, D_MODEL), BF16) for r in rows],
        compiler_params=pltpu.CompilerParams(dimension_semantics=("arbitrary",), vmem_limit_bytes=VMEM_LIMIT_V7X),
    )(core, w_up, w_down, w_out, w_in_t)


def _pre_norm(x, g_pre, comm=None):
    seq = x.shape[0]
    tb = TOKEN_TILE

    def body(x_ref, g_ref, hn_ref):
        xv = x_ref[...]
        hn_ref[...] = (xv * _rms(xv) * g_ref[...]).astype(BF16)

    tile = pl.BlockSpec((tb, D_MODEL), lambda i: (i, 0))
    return _pallas(body, name="pre_norm", grid=(seq // tb,), in_specs=[tile, _resident((1, D_MODEL))], out_specs=[tile],
                   out_shape=[SDS((seq, D_MODEL), BF16)], operands=(x, g_pre), comm=comm)


def _in_proj(hn, w_in_t, rope, comm=None):
    seq = hn.shape[0]
    tb = TOKEN_TILE

    def body(hn_ref, w_ref, c_ref, sa_ref, sb_ref,
             q_ref, kd0_ref, kd1_ref, vd0_ref, vd1_ref, gb_ref, gc_ref, xin_ref):
        proj = _dot_nt(hn_ref[...], w_ref[...].reshape(IN_COLS, D_MODEL))
        c, sa, sb = _rope_tile(c_ref, sa_ref, sb_ref)
        scale = 1.0 / math.sqrt(HEAD_DIM)
        for p in range(Q_WIDTH // 128):
            q_ref[:, 128 * p:128 * (p + 1)] = (_rope(proj[:, 128 * p:128 * (p + 1)], c, sa, sb) * scale).astype(BF16)
        k = _rope(proj[:, Q_WIDTH:Q_WIDTH + KV_WIDTH], c, sa, sb)
        v = proj[:, Q_WIDTH + KV_WIDTH:Q_WIDTH + 2 * KV_WIDTH]
        low = _lane_lt64(k.shape)
        k_sw, v_sw = pltpu.roll(k, HEAD_DIM, 1), pltpu.roll(v, HEAD_DIM, 1)
        kd0_ref[...] = jnp.where(low, k, k_sw).astype(BF16)
        kd1_ref[...] = jnp.where(low, k_sw, k).astype(BF16)
        vd0_ref[...] = jnp.where(low, v, v_sw).astype(BF16)
        vd1_ref[...] = jnp.where(low, v_sw, v).astype(BF16)
        base = Q_WIDTH + 2 * KV_WIDTH
        gb_ref[...] = proj[:, base:base + CONV_WIDTH].astype(BF16)
        gc_ref[...] = proj[:, base + CONV_WIDTH:base + 2 * CONV_WIDTH].astype(BF16)
        xin_ref[...] = proj[:, base + 2 * CONV_WIDTH:base + 3 * CONV_WIDTH].astype(BF16)

    tile = lambda w: pl.BlockSpec((tb, w), lambda i: (i, 0))
    return _pallas(
        body, name="in_proj", grid=(seq // tb,),
        in_specs=[tile(D_MODEL), _resident(w_in_t.shape), *_rope_specs(tb)],
        out_specs=[tile(Q_WIDTH), tile(128), tile(128), tile(128), tile(128),
                   tile(CONV_WIDTH), tile(CONV_WIDTH), tile(CONV_WIDTH)],
        out_shape=[SDS((seq, Q_WIDTH), BF16)] + [SDS((seq, 128), BF16)] * 4 + [SDS((seq, CONV_WIDTH), BF16)] * 3,
        operands=(hn, w_in_t, *rope), comm=comm)


def _attn_valid(i):
    shape = (4 * QBLOCK, 2 * QBLOCK)
    row = lax.broadcasted_iota(jnp.int32, shape, 0)
    col = lax.broadcasted_iota(jnp.int32, shape, 1)
    qi = row & (QBLOCK - 1)
    return (col > qi) & (col <= qi + QBLOCK) & ((col >= QBLOCK) | (i > 0))


def _stack_heads(pair0, pair1):
    low = _lane_lt64(pair0.shape)
    zero = jnp.zeros_like(pair0)
    return jnp.concatenate([jnp.where(low, pair0, zero), jnp.where(low, zero, pair0),
                            jnp.where(low, pair1, zero), jnp.where(low, zero, pair1)], axis=0)


def _unstack_heads(stacked):
    low = _lane_lt64((QBLOCK, 128))
    pair0 = jnp.where(low, stacked[0:QBLOCK], stacked[QBLOCK:2 * QBLOCK])
    pair1 = jnp.where(low, stacked[2 * QBLOCK:3 * QBLOCK], stacked[3 * QBLOCK:4 * QBLOCK])
    return pair0, pair1


def _sink_column(sink_ref, kv_head):
    row = lax.broadcasted_iota(jnp.int32, (4 * QBLOCK, 1), 0)
    s = [sink_ref[0, 4 * kv_head + j] for j in range(4)]
    return jnp.where(row < QBLOCK, s[0], jnp.where(row < 2 * QBLOCK, s[1], jnp.where(row < 3 * QBLOCK, s[2], s[3])))


def _band(ref, i):
    prev = pl.multiple_of(jnp.maximum(i - 1, 0) * QBLOCK, QBLOCK)
    own = pl.multiple_of(i * QBLOCK, QBLOCK)
    return jnp.concatenate([ref[pl.ds(prev, QBLOCK), :], ref[pl.ds(own, QBLOCK), :]], axis=0), prev, own


def _softmax_with_sink(s, sink_col):
    m = jnp.maximum(jnp.max(s, axis=-1, keepdims=True), sink_col)
    p = jnp.exp(s - m)
    e_sink = jnp.exp(sink_col - m)
    inv_l = 1.0 / (jnp.sum(p, axis=-1, keepdims=True) + e_sink)
    return p, e_sink, inv_l


def _attention_fwd(q, kd0, kd1, vd0, vd1, sinks, comm=None):
    seq = q.shape[0]

    nb = ATTN_FWD_BLOCKS

    def body(sink_ref, q_ref, kd0_ref, kd1_ref, vd0_ref, vd1_ref, o_ref):
        for b in range(nb):
            i = pl.program_id(0) * nb + b
            rows = slice(QBLOCK * b, QBLOCK * (b + 1))
            valid = _attn_valid(i)
            for kv_head, (k_ref, v_ref) in enumerate(((kd0_ref, vd0_ref), (kd1_ref, vd1_ref))):
                kband, _, _ = _band(k_ref, i)
                vband, _, _ = _band(v_ref, i)
                base = 256 * kv_head
                qm = _stack_heads(q_ref[rows, base:base + 128], q_ref[rows, base + 128:base + 256])
                s = jnp.where(valid, _dot_nt(qm, kband), NEG_INF)
                p, _, inv_l = _softmax_with_sink(s, _sink_column(sink_ref, kv_head))
                o = _dot(p.astype(BF16), vband) * inv_l
                pair0, pair1 = _unstack_heads(o)
                o_ref[rows, base:base + 128] = pair0.astype(BF16)
                o_ref[rows, base + 128:base + 256] = pair1.astype(BF16)

    blk = pl.BlockSpec((nb * QBLOCK, Q_WIDTH), lambda i: (i, 0))
    full = _resident((seq, 128))
    return _pallas(
        body, name="attention_fwd", grid=(seq // (nb * QBLOCK),),
        in_specs=[pl.BlockSpec(memory_space=pltpu.SMEM), blk, full, full, full, full],
        out_specs=[blk], out_shape=[SDS((seq, Q_WIDTH), BF16)],
        operands=(sinks, q, kd0, kd1, vd0, vd1), comm=comm)


HALO = 16


def _conv_parts(gc, xin, gc_halo, xin_halo, conv_w, first):
    tb = gc.shape[0]
    u = gc.astype(F32) * xin.astype(F32)
    u_halo = jnp.where(first, 0.0, gc_halo.astype(F32) * xin_halo.astype(F32))
    ext = jnp.concatenate([u_halo, u], axis=0)
    u1 = pltpu.roll(ext, 1, 0)[HALO:HALO + tb]
    u2 = pltpu.roll(ext, 2, 0)[HALO:HALO + tb]
    y = conv_w[0:1, :] * u2 + conv_w[1:2, :] * u1 + conv_w[2:3, :] * u
    return u, u1, u2, y


def _halo_prev(tb, w):
    return pl.BlockSpec((HALO, w), lambda i: (jnp.maximum(i * (tb // HALO) - 1, 0), 0))


def _residual_mid(x, mix, g_post_mix):
    mix_f = mix.astype(F32)
    return x + mix_f * _rms(mix_f) * g_post_mix


def _mix_out(attn, gb, gc, xin, conv_w, g_attn, g_conv, w_out, comm=None):
    seq = attn.shape[0]
    tb = TOKEN_TILE

    def body(a_ref, gb_ref, gc_ref, xin_ref, gch_ref, xinh_ref, cw_ref, ga_ref, gcn_ref, w_ref, mix_ref, mixed_ref):
        first = pl.program_id(0) == 0
        _, _, _, y = _conv_parts(gc_ref[...], xin_ref[...], gch_ref[...], xinh_ref[...], cw_ref[...], first)
        conv = gb_ref[...].astype(F32) * y
        a = a_ref[...].astype(F32)
        mixed_ref[:, 0:Q_WIDTH] = (a * _rms(a) * ga_ref[...]).astype(BF16)
        mixed_ref[:, Q_WIDTH:] = (conv * _rms(conv) * gcn_ref[...]).astype(BF16)
        mix_ref[...] = _dot(mixed_ref[...], w_ref[...].reshape(D_MODEL, D_MODEL)).astype(BF16)

    tile = lambda w: pl.BlockSpec((tb, w), lambda i: (i, 0))
    return _pallas(
        body, name="mix_out", grid=(seq // tb,),
        in_specs=[tile(Q_WIDTH), tile(CONV_WIDTH), tile(CONV_WIDTH), tile(CONV_WIDTH),
                  _halo_prev(tb, CONV_WIDTH), _halo_prev(tb, CONV_WIDTH),
                  _resident((CONV_K, CONV_WIDTH)), _resident((1, Q_WIDTH)), _resident((1, CONV_WIDTH)),
                  _resident(w_out.shape)],
        out_specs=[tile(D_MODEL), tile(D_MODEL)],
        out_shape=[SDS((seq, D_MODEL), BF16), SDS((seq, D_MODEL), BF16)],
        operands=(attn, gb, gc, xin, gc, xin, conv_w, g_attn, g_conv, w_out), comm=comm)


def _mlp_up(x, mix, g_post_mix, g_pre_mlp, w_up, comm=None):
    seq = x.shape[0]
    tb = TOKEN_TILE

    def body(x_ref, mix_ref, gpm_ref, g2_ref, wup_ref, up_ref, hn2_ref):
        halves = [slice(0, tb // 2), slice(tb // 2, tb)]
        hn2 = []
        for rows in halves:
            hv = _residual_mid(x_ref[rows, :], mix_ref[rows, :], gpm_ref[...])
            hn2.append((hv * _rms(hv) * g2_ref[...]).astype(BF16))
            hn2_ref[rows, :] = hn2[-1]
        for k, rows in enumerate(halves):
            for j in range(N_CHIPS):
                up = jnp.maximum(_dot(hn2[k], _chip_block(wup_ref, j)), 0.0)
                up_ref[rows, 1024 * j:1024 * (j + 1)] = up.astype(BF16)

    tile = lambda w: pl.BlockSpec((tb, w), lambda i: (i, 0))
    return _pallas(
        body, name="mlp_up", grid=(seq // tb,),
        in_specs=[tile(D_MODEL), tile(D_MODEL), _resident((1, D_MODEL)), _resident((1, D_MODEL)), _resident(w_up.shape)],
        out_specs=[tile(D_FF), tile(D_MODEL)],
        out_shape=[SDS((seq, D_FF), BF16), SDS((seq, D_MODEL), BF16)],
        operands=(x, mix, g_post_mix, g_pre_mlp, w_up), comm=comm)


def _mlp_down_loss(up, x, mix, target, g_post_mix, g_post_mlp, w_down):
    seq = x.shape[0]
    tb = TOKEN_TILE

    def body(up_ref, x_ref, mix_ref, t_ref, gpm_ref, g4_ref, wdown_ref, dout_ref, dmlp_ref, loss_ref, dg4_ref):
        @pl.when(pl.program_id(0) == 0)
        def _():
            loss_ref[...] = jnp.zeros_like(loss_ref)
            dg4_ref[...] = jnp.zeros_like(dg4_ref)

        halves = [slice(0, tb // 2), slice(tb // 2, tb)]
        w_down_all = wdown_ref[...].reshape(D_FF, D_MODEL)
        loss = jnp.zeros((1, 1), F32)
        dg4 = jnp.zeros((1, D_MODEL), F32)
        for rows in halves:
            up = up_ref[rows, :].astype(F32)
            mlp = _dot((up * up).astype(BF16), w_down_all)
            rstd = _rms(mlp)
            zhat = mlp * rstd
            hv = _residual_mid(x_ref[rows, :], mix_ref[rows, :], gpm_ref[...])
            diff = hv + zhat * g4_ref[...] - t_ref[rows, :]
            loss = loss + jnp.sum(jnp.sum(diff * diff, axis=1, keepdims=True), axis=0, keepdims=True)
            dout = diff * (1.0 / D_MODEL)
            dout_ref[rows, :] = dout
            dg4 = dg4 + _colsum(dout * zhat)
            dmlp_ref[rows, :] = _norm_bwd(dout, g4_ref[...], zhat, rstd).astype(BF16)
        loss_ref[...] += loss
        dg4_ref[...] += dg4

    tile = lambda w: pl.BlockSpec((tb, w), lambda i: (i, 0))
    return _pallas(
        body, name="mlp_down_loss", grid=(seq // tb,),
        in_specs=[tile(D_FF), tile(D_MODEL), tile(D_MODEL), tile(D_MODEL), _resident((1, D_MODEL)), _resident((1, D_MODEL)),
                  _resident(w_down.shape)],
        out_specs=[tile(D_MODEL), tile(D_MODEL),
                   pl.BlockSpec((1, 1), lambda i: (0, 0)), pl.BlockSpec((1, D_MODEL), lambda i: (0, 0))],
        out_shape=[SDS((seq, D_MODEL), F32), SDS((seq, D_MODEL), BF16), SDS((1, 1), F32), SDS((1, D_MODEL), F32)],
        operands=(up, x, mix, target, g_post_mix, g_post_mlp, w_down))


def _mlp_bwd(dmlp, up, x, dout, mix, g_pre_mlp, g_post_mix, w_up, w_down):
    seq = x.shape[0]
    tb = MLP_BWD_TOKEN_TILE

    def body(dmlp_ref, up_ref, x_ref, dout_ref, mix_ref, g2_ref, gpm_ref, wup_ref, wdown_ref,
             dup_ref, dh_ref, dmix_ref, dg2_ref, dgpm_ref):
        @pl.when(pl.program_id(0) == 0)
        def _():
            dg2_ref[...] = jnp.zeros_like(dg2_ref)
            dgpm_ref[...] = jnp.zeros_like(dgpm_ref)

        subs = [slice(k * MLP_BWD_SUB_TILE, (k + 1) * MLP_BWD_SUB_TILE) for k in range(tb // MLP_BWD_SUB_TILE)]
        dhn2 = []
        for rows in subs:
            dmlp_v = dmlp_ref[rows, :]
            acc = None
            for j in range(N_CHIPS):
                cols = slice(1024 * j, 1024 * (j + 1))
                dact = _dot_nt(dmlp_v, _chip_block(wdown_ref, j))
                dup = (dact * (2.0 * up_ref[rows, cols].astype(F32))).astype(BF16)
                dup_ref[rows, cols] = dup
                part = _dot_nt(dup, _chip_block(wup_ref, j))
                acc = part if acc is None else acc + part
            dhn2.append(acc)
        dg2 = jnp.zeros((1, D_MODEL), F32)
        dgpm = jnp.zeros((1, D_MODEL), F32)
        for k, rows in enumerate(subs):
            mix_v = mix_ref[rows, :].astype(F32)
            hv = _residual_mid(x_ref[rows, :], mix_ref[rows, :], gpm_ref[...])
            r2 = _rms(hv)
            hhat = hv * r2
            dg2 = dg2 + _colsum(dhn2[k] * hhat)
            dh = dout_ref[rows, :] + _norm_bwd(dhn2[k], g2_ref[...], hhat, r2)
            dh_ref[rows, :] = dh.astype(BF16)
            rz = _rms(mix_v)
            zhat = mix_v * rz
            dgpm = dgpm + _colsum(dh * zhat)
            dmix_ref[rows, :] = _norm_bwd(dh, gpm_ref[...], zhat, rz).astype(BF16)
        dg2_ref[...] += dg2
        dgpm_ref[...] += dgpm

    tile = lambda w: pl.BlockSpec((tb, w), lambda i: (i, 0))
    vec = pl.BlockSpec((1, D_MODEL), lambda i: (0, 0))
    return _pallas(
        body, name="mlp_bwd", grid=(seq // tb,),
        in_specs=[tile(D_MODEL), tile(D_FF), tile(D_MODEL), tile(D_MODEL), tile(D_MODEL),
                  _resident((1, D_MODEL)), _resident((1, D_MODEL)), _resident(w_up.shape), _resident(w_down.shape)],
        out_specs=[tile(D_FF), tile(D_MODEL), tile(D_MODEL), vec, vec],
        out_shape=[SDS((seq, D_FF), BF16), SDS((seq, D_MODEL), BF16), SDS((seq, D_MODEL), BF16),
                   SDS((1, D_MODEL), F32), SDS((1, D_MODEL), F32)],
        operands=(dmlp, up, x, dout, mix, g_pre_mlp, g_post_mix, w_up, w_down))


class _Rider(NamedTuple):
    body: Callable
    in_specs: list
    out_specs: list
    out_shape: list
    operands: tuple


def _mix_bwd(dmix, attn, gb, gc, xin, conv_w, g_attn, g_conv, w_out, n_k):
    seq = attn.shape[0]
    tb = seq // (N_CHIPS * n_k)

    def body(first, dmix_ref, a_ref, gb_ref, gc_ref, xin_ref, gch_ref, xinh_ref, cw_ref, ga_ref, gcn_ref, w_ref,
             dattn_ref, dgb_ref, dy_ref, dga_ref, dgcn_ref, dcw_ref):
        @pl.when(first)
        def _():
            dga_ref[...] = jnp.zeros_like(dga_ref)
            dgcn_ref[...] = jnp.zeros_like(dgcn_ref)
            dcw_ref[...] = jnp.zeros_like(dcw_ref)

        dmixed = _dot_nt(dmix_ref[...], w_ref[...].reshape(D_MODEL, D_MODEL))
        a = a_ref[...].astype(F32)
        ra = _rms(a)
        ahat = a * ra
        dan = dmixed[:, 0:Q_WIDTH]
        dga_ref[...] += _colsum(dan * ahat)
        dattn_ref[...] = _norm_bwd(dan, ga_ref[...], ahat, ra).astype(BF16)
        gbv = gb_ref[...].astype(F32)
        u, u1, u2, y = _conv_parts(gc_ref[...], xin_ref[...], gch_ref[...], xinh_ref[...], cw_ref[...], first)
        conv = gbv * y
        rc = _rms(conv)
        chat = conv * rc
        dcn = dmixed[:, Q_WIDTH:]
        dgcn_ref[...] += _colsum(dcn * chat)
        dconv = _norm_bwd(dcn, gcn_ref[...], chat, rc)
        dgb_ref[...] = (dconv * y).astype(BF16)
        dy = dconv * gbv
        dy_ref[...] = dy.astype(BF16)
        dcw_ref[0:1, :] += _colsum(dy * u2)
        dcw_ref[1:2, :] += _colsum(dy * u1)
        dcw_ref[2:3, :] += _colsum(dy * u)

    tile = lambda w: pl.BlockSpec((tb, w), lambda j, k: (j * n_k + k, 0))
    halo = lambda w: pl.BlockSpec((HALO, w), lambda j, k: (jnp.maximum((j * n_k + k) * (tb // HALO) - 1, 0), 0))
    whole = lambda shape: pl.BlockSpec(shape, lambda j, k: (0,) * len(shape))
    return _Rider(
        body,
        in_specs=[tile(D_MODEL), tile(Q_WIDTH), tile(CONV_WIDTH), tile(CONV_WIDTH), tile(CONV_WIDTH),
                  halo(CONV_WIDTH), halo(CONV_WIDTH),
                  _resident((CONV_K, CONV_WIDTH)), _resident((1, Q_WIDTH)), _resident((1, CONV_WIDTH)),
                  _resident(w_out.shape)],
        out_specs=[tile(Q_WIDTH), tile(CONV_WIDTH), tile(CONV_WIDTH),
                   whole((1, Q_WIDTH)), whole((1, CONV_WIDTH)), whole((CONV_K, CONV_WIDTH))],
        out_shape=[SDS((seq, Q_WIDTH), BF16), SDS((seq, CONV_WIDTH), BF16), SDS((seq, CONV_WIDTH), BF16),
                   SDS((1, Q_WIDTH), F32), SDS((1, CONV_WIDTH), F32), SDS((CONV_K, CONV_WIDTH), F32)],
        operands=(dmix, attn, gb, gc, xin, gc, xin, conv_w, g_attn, g_conv, w_out))


def _attention_bwd(q, dattn, attn, kd0, kd1, vd0, vd1, sinks, comm=None):
    seq = q.shape[0]
    nb = ATTN_BWD_BLOCKS

    def body(sink_ref, q_ref, do_ref, o_ref, kd0_ref, kd1_ref, vd0_ref, vd1_ref,
             dq_ref, dk0_ref, dk1_ref, dv0_ref, dv1_ref, dsink_ref):
        @pl.when(pl.program_id(0) == 0)
        def _():
            for r in (dk0_ref, dk1_ref, dv0_ref, dv1_ref, dsink_ref):
                r[...] = jnp.zeros_like(r)

        lane = lax.broadcasted_iota(jnp.int32, (1, 128), 1)
        dsink = jnp.zeros((1, 128), F32)
        for b in range(nb):
            i = pl.program_id(0) * nb + b
            rows = slice(QBLOCK * b, QBLOCK * (b + 1))
            valid = _attn_valid(i)
            for kv_head, (k_ref, v_ref, dk_ref, dv_ref) in enumerate(
                    ((kd0_ref, vd0_ref, dk0_ref, dv0_ref), (kd1_ref, vd1_ref, dk1_ref, dv1_ref))):
                kband, prev, own = _band(k_ref, i)
                vband, _, _ = _band(v_ref, i)
                base = 256 * kv_head
                qm = _stack_heads(q_ref[rows, base:base + 128], q_ref[rows, base + 128:base + 256])
                dom = _stack_heads(do_ref[rows, base:base + 128], do_ref[rows, base + 128:base + 256])
                om = _stack_heads(o_ref[rows, base:base + 128], o_ref[rows, base + 128:base + 256])
                s = jnp.where(valid, _dot_nt(qm, kband), NEG_INF)
                p, e_sink, inv_l = _softmax_with_sink(s, _sink_column(sink_ref, kv_head))
                p = p * inv_l
                delta = jnp.sum(dom.astype(F32) * om.astype(F32), axis=-1, keepdims=True)
                ds = (p * (_dot_nt(dom, vband) - delta)).astype(BF16)
                sink_term = -(e_sink * inv_l) * delta
                for j in range(4):
                    part = jnp.sum(sink_term[QBLOCK * j:QBLOCK * (j + 1)], axis=0, keepdims=True)
                    dsink = dsink + jnp.where(lane == 4 * kv_head + j, part, 0.0)
                pair0, pair1 = _unstack_heads(_dot(ds, kband))
                dq_ref[rows, base:base + 128] = pair0.astype(BF16)
                dq_ref[rows, base + 128:base + 256] = pair1.astype(BF16)
                dkd = _dot_tn(ds, qm)
                dkd = dkd + pltpu.roll(dkd, HEAD_DIM, 1)
                dvd = _dot_tn(p.astype(BF16), dom)
                dvd = dvd + pltpu.roll(dvd, HEAD_DIM, 1)
                dk_ref[pl.ds(prev, QBLOCK), :] += dkd[0:QBLOCK]
                dk_ref[pl.ds(own, QBLOCK), :] += dkd[QBLOCK:]
                dv_ref[pl.ds(prev, QBLOCK), :] += dvd[0:QBLOCK]
                dv_ref[pl.ds(own, QBLOCK), :] += dvd[QBLOCK:]
        dsink_ref[...] += dsink

    blk = pl.BlockSpec((nb * QBLOCK, Q_WIDTH), lambda i: (i, 0))
    full = _resident((seq, 128))
    acc = pl.BlockSpec((seq, 128), lambda i: (0, 0))
    return _pallas(
        body, name="attention_bwd", grid=(seq // (nb * QBLOCK),),
        in_specs=[pl.BlockSpec(memory_space=pltpu.SMEM), blk, blk, blk, full, full, full, full],
        out_specs=[blk, acc, acc, acc, acc, pl.BlockSpec((1, 128), lambda i: (0, 0))],
        out_shape=[SDS((seq, Q_WIDTH), BF16)] + [SDS((seq, 128), F32)] * 4 + [SDS((1, 128), F32)],
        operands=(sinks, q, dattn, attn, kd0, kd1, vd0, vd1), comm=comm)


def _in_proj_bwd(dq, dk0, dk1, dv0, dv1, dgb, dy, gc, xin, conv_w, x, dh, g_pre, w_in_t, rope):
    seq = x.shape[0]
    tb = TOKEN_TILE
    n_tiles = seq // tb

    def body(dq_ref, dk0_ref, dk1_ref, dv0_ref, dv1_ref, dgb_ref, dy_ref, dyh_ref, gc_ref, xin_ref, cw_ref,
             x_ref, dh_ref, g_ref, w_ref, c_ref, sa_ref, sb_ref,
             dproj_ref, gx_ref, dg_ref):
        i = pl.program_id(0)

        @pl.when(i == 0)
        def _():
            dg_ref[...] = jnp.zeros_like(dg_ref)

        dy = dy_ref[...].astype(F32)
        ext = jnp.concatenate([dy, jnp.where(i == n_tiles - 1, 0.0, dyh_ref[...].astype(F32))], axis=0)
        dy1 = pltpu.roll(ext, tb + HALO - 1, 0)[0:tb]
        dy2 = pltpu.roll(ext, tb + HALO - 2, 0)[0:tb]
        cw = cw_ref[...]
        du = cw[2:3, :] * dy + cw[1:2, :] * dy1 + cw[0:1, :] * dy2
        scale = 1.0 / math.sqrt(HEAD_DIM)
        base = Q_WIDTH + 2 * KV_WIDTH
        halves = [slice(0, tb // 2), slice(tb // 2, tb)]
        low = _lane_lt64((tb // 2, 128))
        for rows in halves:
            c, sa, sb = _rope_tile(c_ref.at[rows, :], sa_ref, sb_ref)
            for p in range(Q_WIDTH // 128):
                dproj_ref[rows, 128 * p:128 * (p + 1)] = _rope_transposed(
                    dq_ref[rows, 128 * p:128 * (p + 1)].astype(F32) * scale, c, sa, sb).astype(BF16)
            dk = jnp.where(low, dk0_ref[rows, :], dk1_ref[rows, :])
            dproj_ref[rows, Q_WIDTH:Q_WIDTH + KV_WIDTH] = _rope_transposed(dk, c, sa, sb).astype(BF16)
            dproj_ref[rows, Q_WIDTH + KV_WIDTH:base] = jnp.where(low, dv0_ref[rows, :], dv1_ref[rows, :]).astype(BF16)
            dproj_ref[rows, base:base + CONV_WIDTH] = dgb_ref[rows, :]
            dproj_ref[rows, base + CONV_WIDTH:base + 2 * CONV_WIDTH] = (du[rows] * xin_ref[rows, :].astype(F32)).astype(BF16)
            dproj_ref[rows, base + 2 * CONV_WIDTH:] = (du[rows] * gc_ref[rows, :].astype(F32)).astype(BF16)
        w_all = w_ref[...].reshape(IN_COLS, D_MODEL)
        dhn = [_dot(dproj_ref[rows, :], w_all) for rows in halves]
        dg = jnp.zeros((1, D_MODEL), F32)
        for k, rows in enumerate(halves):
            xv = x_ref[rows, :]
            r = _rms(xv)
            xhat = xv * r
            dg = dg + _colsum(dhn[k] * xhat)
            gx_ref[rows, :] = dh_ref[rows, :].astype(F32) + _norm_bwd(dhn[k], g_ref[...], xhat, r)
        dg_ref[...] += dg

    tile = lambda w: pl.BlockSpec((tb, w), lambda i: (i, 0))
    halo_next = pl.BlockSpec((HALO, CONV_WIDTH), lambda i: (jnp.minimum((i + 1) * (tb // HALO), seq // HALO - 1), 0))
    return _pallas(
        body, name="in_proj_bwd", grid=(n_tiles,),
        in_specs=[tile(Q_WIDTH), tile(128), tile(128), tile(128), tile(128), tile(CONV_WIDTH), tile(CONV_WIDTH), halo_next,
                  tile(CONV_WIDTH), tile(CONV_WIDTH), _resident((CONV_K, CONV_WIDTH)),
                  tile(D_MODEL), tile(D_MODEL), _resident((1, D_MODEL)), _resident(w_in_t.shape), *_rope_specs(tb)],
        out_specs=[tile(IN_COLS), tile(D_MODEL), pl.BlockSpec((1, D_MODEL), lambda i: (0, 0))],
        out_shape=[SDS((seq, IN_COLS), BF16), SDS((seq, D_MODEL), F32), SDS((1, D_MODEL), F32)],
        operands=(dq, dk0, dk1, dv0, dv1, dgb, dy, dy, gc, xin, conv_w, x, dh, g_pre, w_in_t, *rope))


def _wgrad_grid(seq, per_chip, h_rows):
    chips_per_step = 1 if per_chip else N_CHIPS
    m = chips_per_step * 2 * h_rows
    bt = min(seq, WGRAD_TOKEN_TILE if m <= 1024 else WGRAD_TOKEN_TILE // 2)
    return chips_per_step, m, bt, seq // bt


def _wgrad(name, a, b, *, per_chip, h_rows, square_a=False, comm=None, rider=None):
    seq = a.shape[0]
    chips_per_step, m, bt, n_k = _wgrad_grid(seq, per_chip, h_rows)
    a_cols = m if per_chip else a.shape[1]
    a_wide = a.shape[1] > a_cols
    b_wide = b.shape[1] > D_MODEL
    n_ride_in = len(rider.in_specs) if rider else 0
    n_ride_out = len(rider.out_specs) if rider else 0

    def body(a_ref, b_ref, *rest):
        ride_in, g_ref = rest[:n_ride_in], rest[n_ride_in]
        ride_out, acc_ref = rest[n_ride_in + 1:n_ride_in + 1 + n_ride_out], rest[-1]
        k = pl.program_id(1)

        @pl.when(k == 0)
        def _():
            acc_ref[...] = jnp.zeros_like(acc_ref)

        av = a_ref[...]
        if square_a:
            av = (av.astype(F32) * av.astype(F32)).astype(BF16)
        acc_ref[...] += _dot_tn(av, b_ref[...])

        @pl.when(k == n_k - 1)
        def _():
            for cidx in range(chips_per_step):
                for half in range(2):
                    r0 = (2 * cidx + half) * h_rows
                    g_ref[cidx, half] = acc_ref[r0:r0 + h_rows, :]

        if rider:
            rider.body(jnp.logical_and(pl.program_id(0) == 0, k == 0), *ride_in, *ride_out)

    a_spec = pl.BlockSpec((bt, a_cols), (lambda j, k: (k, j)) if a_wide else (lambda j, k: (k, 0)))
    b_spec = pl.BlockSpec((bt, D_MODEL), (lambda j, k: (k, j)) if b_wide else (lambda j, k: (k, 0)))
    g_spec = pl.BlockSpec((chips_per_step, 2, h_rows, D_MODEL), lambda j, k: (j, 0, 0, 0))
    return _pallas(
        body, name=name, grid=(N_CHIPS if per_chip else 1, n_k),
        in_specs=[a_spec, b_spec] + (rider.in_specs if rider else []),
        out_specs=[g_spec] + (rider.out_specs if rider else []),
        out_shape=[SDS((N_CHIPS, 2, h_rows, D_MODEL), F32)] + (rider.out_shape if rider else []),
        scratch=[pltpu.VMEM((m, D_MODEL), F32)], operands=(a, b) + (rider.operands if rider else ()), comm=comm)


def _adamw_math(w, g, m, v):
    m = ADAM_B1 * m + (1.0 - ADAM_B1) * g
    v = ADAM_B2 * v + (1.0 - ADAM_B2) * (g * g)
    m_hat = m / (1.0 - ADAM_B1 ** ADAM_STEP)
    v_hat = v / (1.0 - ADAM_B2 ** ADAM_STEP)
    delta = -ADAM_LR * (m_hat / (jnp.sqrt(v_hat) + ADAM_EPS) + ADAM_WD * w)
    return delta, m, v


def _adamw_rows(name, reduced, w, m, v, rt):
    per_half = reduced.shape[1] // rt

    def body(r_ref, w_ref, m_ref, v_ref, g_out, d_out, m_out, v_out):
        g = r_ref[0]
        g_out[...] = g
        d_out[...], m_out[...], v_out[...] = _adamw_math(w_ref[...], g, m_ref[...], v_ref[...])

    blk = pl.BlockSpec((rt, D_MODEL), lambda h, r: (h * per_half + r, 0))
    return _pallas(
        body, name=name, grid=(2, per_half),
        in_specs=[pl.BlockSpec((1, rt, D_MODEL), lambda h, r: (h, r, 0)), blk, blk, blk],
        out_specs=[blk, blk, blk, blk], out_shape=[SDS(w.shape, F32)] * 4, operands=(reduced, w, m, v))


def _adamw_small(w, g, m, v):
    def body(w_ref, g_ref, m_ref, v_ref, d_out, m_out, v_out):
        d_out[...], m_out[...], v_out[...] = _adamw_math(w_ref[...], g_ref[...], m_ref[...], v_ref[...])

    return pl.pallas_call(body, name="adamw_small", in_specs=[VMEM_WHOLE] * 4, out_specs=[VMEM_WHOLE] * 3,
                          out_shape=[SDS(w.shape, F32)] * 3)(w, g, m, v)


SMALL_VECTORS = ("pre_mix_norm", "post_mix_norm", "pre_mlp_norm", "post_mlp_norm")
SMALL_NAMES = SMALL_VECTORS + ("attn_group_norm", "conv_group_norm", "conv_w", "attn_sinks")


def _pack_small(p):
    rows = [p[n].reshape(1, D_MODEL) for n in SMALL_VECTORS]
    rows.append(jnp.concatenate([p["attn_group_norm"].reshape(1, -1), p["conv_group_norm"].reshape(1, -1)], axis=1))
    cw = p["conv_w"].reshape(CONV_K, -1)
    rows.append(jnp.pad(cw, ((0, 1), (0, CONV_WIDTH - cw.shape[1]))).reshape(2, D_MODEL))
    last = jnp.concatenate([p["attn_sinks"].reshape(1, 8), p.get("loss_sum", jnp.zeros((1, 1), F32))], axis=1)
    rows.append(jnp.pad(last, ((0, 0), (0, D_MODEL - 9))))
    return jnp.concatenate(rows, axis=0)


def _unpack_small(packed, conv_width):
    out = {n: packed[i:i + 1] for i, n in enumerate(SMALL_VECTORS)}
    out["attn_group_norm"] = packed[4:5, :Q_WIDTH]
    out["conv_group_norm"] = packed[4:5, Q_WIDTH:]
    out["conv_w"] = packed[5:7].reshape(4, CONV_WIDTH)[:CONV_K, :conv_width].reshape(1, CONV_K, conv_width)
    out["attn_sinks"] = packed[7:8, :8]
    out["loss_sum"] = packed[7, 8]
    return out


WEIGHT_ORDER = ("pre_mix_norm", "w_in", "conv_w", "attn_sinks", "attn_group_norm", "conv_group_norm", "w_out",
                "post_mix_norm", "pre_mlp_norm", "w_up", "w_down", "post_mlp_norm")


def kernel(x, pre_mix_norm, w_in, conv_w, attn_sinks, attn_group_norm, conv_group_norm, w_out, post_mix_norm, pre_mlp_norm, w_up, w_down, post_mlp_norm, loss_target, m_pre_mix_norm, m_w_in, m_conv_w, m_attn_sinks, m_attn_group_norm, m_conv_group_norm, m_w_out, m_post_mix_norm, m_pre_mlp_norm, m_w_up, m_w_down, m_post_mlp_norm, v_pre_mix_norm, v_w_in, v_conv_w, v_attn_sinks, v_attn_group_norm, v_conv_group_norm, v_w_out, v_post_mix_norm, v_pre_mlp_norm, v_w_up, v_w_down, v_post_mlp_norm):
    w = dict(pre_mix_norm=pre_mix_norm, w_in=w_in, conv_w=conv_w, attn_sinks=attn_sinks, attn_group_norm=attn_group_norm,
             conv_group_norm=conv_group_norm, w_out=w_out, post_mix_norm=post_mix_norm, pre_mlp_norm=pre_mlp_norm,
             w_up=w_up, w_down=w_down, post_mlp_norm=post_mlp_norm)
    m = dict(pre_mix_norm=m_pre_mix_norm, w_in=m_w_in, conv_w=m_conv_w, attn_sinks=m_attn_sinks,
             attn_group_norm=m_attn_group_norm, conv_group_norm=m_conv_group_norm, w_out=m_w_out,
             post_mix_norm=m_post_mix_norm, pre_mlp_norm=m_pre_mlp_norm, w_up=m_w_up, w_down=m_w_down,
             post_mlp_norm=m_post_mlp_norm)
    v = dict(pre_mix_norm=v_pre_mix_norm, w_in=v_w_in, conv_w=v_conv_w, attn_sinks=v_attn_sinks,
             attn_group_norm=v_attn_group_norm, conv_group_norm=v_conv_group_norm, w_out=v_w_out,
             post_mix_norm=v_post_mix_norm, pre_mlp_norm=v_pre_mlp_norm, w_up=v_w_up, w_down=v_w_down,
             post_mlp_norm=v_post_mlp_norm)
    core = lax.axis_index("c").astype(jnp.int32).reshape(1)
    chip = 2 * lax.axis_index("x") + lax.axis_index("y")
    local_conv = conv_w.shape[2]
    xs, target = x[0], loss_target[0]
    rope = _rope_inputs(xs.shape[0])

    hb_up, hb_down, hb_out, hb_in = _cast_halves(core, w_up[0], w_down[0], w_out[0], w_in[0].T)
    conv_pad = jnp.pad(conv_w[0], ((0, 8 - CONV_K), (0, 0)))
    hn, wf_in = _pre_norm(xs, pre_mix_norm, comm=_gather_legs(hb_in, None, first=(0, H_IN), second_after=(0, H_IN)))
    *proj, wf_out, wf_up, conv_all = _in_proj(
        hn, wf_in, rope, comm=_merge(_gather_legs(hb_out, None, first=(0, H_OUT)),
                                     _gather_legs(hb_up, None, first=(0, UP_SPLIT)), _gather_small(conv_pad)))
    conv_full = conv_all[:, :CONV_K, :].transpose(1, 0, 2).reshape(CONV_K, CONV_WIDTH)
    q, kd0, kd1, vd0, vd1, gb, gc, xin = proj
    attn, wf_out, wf_up, wf_down = _attention_fwd(
        q, kd0, kd1, vd0, vd1, attn_sinks,
        comm=_merge(_gather_legs(None, wf_out, second=(0, H_OUT)),
                    _gather_legs(hb_up, wf_up, first=(UP_SPLIT, H_UP - UP_SPLIT), second=(0, UP_SPLIT)),
                    _gather_legs(hb_down, None, first=(0, DOWN_SPLIT))))
    down_rest = (DOWN_SPLIT, H_DOWN - DOWN_SPLIT)
    mix, mixed, wf_up, wf_down = _mix_out(
        attn, gb, gc, xin, conv_full, attn_group_norm, conv_group_norm, wf_out,
        comm=_merge(_gather_legs(None, wf_up, second=(UP_SPLIT, H_UP - UP_SPLIT)),
                    _gather_legs(hb_down, wf_down, first=down_rest, second=(0, DOWN_SPLIT))))
    up, hn2, wf_down = _mlp_up(xs, mix, post_mix_norm, pre_mlp_norm, wf_up, comm=_gather_legs(None, wf_down, second=down_rest))
    dout, dmlp, loss_sum, dg_post_mlp = _mlp_down_loss(up, xs, mix, target, post_mix_norm, post_mlp_norm, wf_down)

    dup, dh, dmix, dg_pre_mlp, dg_post_mix = _mlp_bwd(dmlp, up, xs, dout, mix, pre_mlp_norm, post_mix_norm, wf_up, wf_down)
    n_k = _wgrad_grid(xs.shape[0], True, H_DOWN)[3]
    g_down, dattn, dgb, dy, dg_attn, dg_conv, dconv_w = _wgrad(
        "wgrad_down", up, dmlp, per_chip=True, h_rows=H_DOWN, square_a=True,
        rider=_mix_bwd(dmix, attn, gb, gc, xin, conv_full, attn_group_norm, conv_group_norm, wf_out, n_k))
    g_up, got_down = _wgrad("wgrad_up", hn2, dup, per_chip=True, h_rows=H_UP, comm=_pair_send(g_down))
    p_down = _pair_sum("pair_sum_down", core, g_down, got_down)
    g_out, got_up = _wgrad("wgrad_out", mixed, dmix, per_chip=False, h_rows=H_OUT, comm=_pair_send(g_up))
    p_up = _pair_sum("pair_sum_up", core, g_up, got_up)
    dq, dk0, dk1, dv0, dv1, dsink, ex_down, ex_up, got_out = _attention_bwd(
        q, dattn, attn, kd0, kd1, vd0, vd1, attn_sinks,
        comm=_merge(_chip_exchange(p_down), _chip_exchange(p_up), _pair_send(g_out)))
    p_out = _pair_sum("pair_sum_out", core, g_out, got_out)
    dproj, grad_x, dg_pre_mix = _in_proj_bwd(dq, dk0, dk1, dv0, dv1, dgb, dy, gc, xin, conv_full, xs, dh, pre_mix_norm,
                                             wf_in, rope)
    g_in, ex_out = _wgrad("wgrad_in", dproj, hn, per_chip=False, h_rows=H_IN, comm=_chip_exchange(p_out))
    small = dict(pre_mix_norm=dg_pre_mix, conv_w=dconv_w, attn_sinks=dsink[:, :8], attn_group_norm=dg_attn,
                 conv_group_norm=dg_conv, post_mix_norm=dg_post_mix, pre_mlp_norm=dg_pre_mlp, post_mlp_norm=dg_post_mlp,
                 loss_sum=loss_sum)
    r_down, r_up, r_out, r_in, small_total = _tail_reduce(g_in, [ex_down, ex_up, ex_out], _pack_small(small))

    out_g, out_d, out_m, out_v = {}, {}, {}, {}
    out_g["w_up"], out_d["w_up"], out_m["w_up"], out_v["w_up"] = _adamw_rows(
        "adamw_up", r_up, w_up[0], m_w_up[0], v_w_up[0], 256)
    out_g["w_down"], out_d["w_down"], out_m["w_down"], out_v["w_down"] = _adamw_rows(
        "adamw_down", r_down, w_down[0], m_w_down[0], v_w_down[0], 256)
    out_g["w_out"], out_d["w_out"], out_m["w_out"], out_v["w_out"] = _adamw_rows(
        "adamw_out", r_out, w_out[0], m_w_out[0], v_w_out[0], H_OUT)
    in_t = _adamw_rows("adamw_in", r_in, w_in[0].T, m_w_in[0].T, v_w_in[0].T, H_IN)
    out_g["w_in"], out_d["w_in"], out_m["w_in"], out_v["w_in"] = [t.T for t in in_t]

    small_sum = _unpack_small(small_total, CONV_WIDTH)
    loss = small_sum["loss_sum"] * (0.5 / D_MODEL)
    small_sum["conv_w"] = lax.dynamic_slice_in_dim(small_sum["conv_w"], chip * local_conv, local_conv, axis=2)
    packed = [_pack_small({n: t[n] for n in SMALL_NAMES}) for t in (w, small_sum, m, v)]
    small_d, small_m, small_v = [_unpack_small(t, local_conv) for t in _adamw_small(*packed)]
    for n in SMALL_NAMES:
        out_g[n], out_d[n], out_m[n], out_v[n] = small_sum[n], small_d[n], small_m[n], small_v[n]

    def shaped(d):
        return [d[n].reshape(w[n].shape) for n in WEIGHT_ORDER]

    return (loss, grad_x[None], *shaped(out_g), *shaped(out_d), *shaped(out_m), *shaped(out_v))
```

```python
import math
from typing import Callable, NamedTuple

import jax
import jax.numpy as jnp
import numpy as np
from jax import lax
from jax.experimental import pallas as pl
from jax.experimental.pallas import tpu as pltpu

F32 = jnp.float32
BF16 = jnp.bfloat16

D_MODEL = 1024
HEAD_DIM = 64
Q_WIDTH = 512
KV_WIDTH = 128
CONV_WIDTH = 512
CONV_K = 3
D_FF = 4096
IN_COLS = 2304
QBLOCK = 128
ROT_DIM = 16
ROPE_THETA = 500000.0
NORM_EPS = 1e-6
NEG_INF = -1e30
N_CHIPS = 4

ADAM_LR = 0.001
ADAM_B1 = 0.9
ADAM_B2 = 0.999
ADAM_EPS = 1e-08
ADAM_WD = 0.01
ADAM_STEP = 10

H_UP, H_DOWN, H_OUT, H_IN = 512, 512, 128, 288

TOKEN_TILE = 512
MLP_BWD_TOKEN_TILE = 512
MLP_BWD_SUB_TILE = 256
ATTN_FWD_BLOCKS = 4
ATTN_BWD_BLOCKS = 2
WGRAD_TOKEN_TILE = 2048
VMEM_LIMIT_V7X = 56 * 1024 * 1024

MESH = pl.DeviceIdType.MESH
ANY = pl.BlockSpec(memory_space=pl.ANY)
VMEM_WHOLE = pl.BlockSpec(memory_space=pltpu.VMEM)
SDS = jax.ShapeDtypeStruct


def _resident(shape):
    zeros = (0,) * len(shape)
    return pl.BlockSpec(shape, lambda *_: zeros, pipeline_mode=pl.Buffered(1))


def _rms(v):
    return lax.rsqrt(jnp.mean(v * v, axis=-1, keepdims=True) + NORM_EPS)


def _norm_bwd(dy, gain, vhat, rstd):
    t = dy * gain
    return rstd * (t - vhat * jnp.mean(t * vhat, axis=-1, keepdims=True))


def _colsum(v):
    return jnp.sum(v, axis=0, keepdims=True)


def _dot_nt(a, b):
    return lax.dot_general(a, b, (((1,), (1,)), ((), ())), preferred_element_type=F32)


def _dot_tn(a, b):
    return lax.dot_general(a, b, (((0,), (0,)), ((), ())), preferred_element_type=F32)


def _dot(a, b):
    return jnp.dot(a, b, preferred_element_type=F32)


def _chip_block(w_ref, chip):
    both = w_ref[pl.ds(2 * chip, 2)]
    return both.reshape(2 * both.shape[1], both.shape[2])


def _lane_lt64(shape):
    return lax.broadcasted_iota(jnp.int32, shape, 1) < HEAD_DIM


class _Comm(NamedTuple):
    operands: tuple
    out_shapes: tuple
    aliases: dict
    n_remote: int
    n_local: int
    plan: Callable


def _merge(*comms):
    operands, out_shapes, aliases, parts = [], [], {}, []
    n_remote = n_local = 0
    for cm in comms:
        parts.append((len(operands), len(out_shapes), n_remote, n_local, cm))
        for k, v in cm.aliases.items():
            aliases[len(operands) + k] = len(out_shapes) + v
        operands += cm.operands
        out_shapes += cm.out_shapes
        n_remote += cm.n_remote
        n_local += cm.n_local

    def plan(ins, outs, send, recv, loc):
        sends, recvs, locs = [], [], []
        for i0, o0, r0, l0, cm in parts:
            s, r, l = cm.plan(ins[i0:i0 + len(cm.operands)], outs[o0:o0 + len(cm.out_shapes)],
                              lambda k, r0=r0: send(r0 + k), lambda k, r0=r0: recv(r0 + k), lambda k, l0=l0: loc(l0 + k))
            sends, recvs, locs = sends + s, recvs + r, locs + l
        return sends, recvs, locs

    return _Comm(tuple(operands), tuple(out_shapes), aliases, n_remote, n_local, plan)


def _sem_scratch(comm):
    return [pltpu.SemaphoreType.DMA((max(comm.n_remote, 1),)), pltpu.SemaphoreType.DMA((max(comm.n_remote, 1),)),
            pltpu.SemaphoreType.DMA((max(comm.n_local, 1),))]


def _pallas(body, *, name, grid, in_specs, out_specs, out_shape, operands, scratch=(), comm=None):
    params = pltpu.CompilerParams(dimension_semantics=("arbitrary",) * len(grid), vmem_limit_bytes=VMEM_LIMIT_V7X)
    if comm is None:
        return pl.pallas_call(body, name=name, grid=grid, in_specs=in_specs, out_specs=out_specs, out_shape=out_shape,
                              scratch_shapes=list(scratch), compiler_params=params)(*operands)
    n_in, n_out, n_scr = len(in_specs), len(out_specs), len(scratch)
    c_in, c_out = len(comm.operands), len(comm.out_shapes)

    def with_comm(*refs):
        ins, c_ins = refs[:n_in], refs[n_in:n_in + c_in]
        o0 = n_in + c_in
        outs, c_outs = refs[o0:o0 + n_out], refs[o0 + n_out:o0 + n_out + c_out]
        s0 = o0 + n_out + c_out
        scr = refs[s0:s0 + n_scr]
        send_sems, recv_sems, local_sems = refs[s0 + n_scr:]
        first = last = None
        for axis, size in enumerate(grid):
            at_start, at_end = pl.program_id(axis) == 0, pl.program_id(axis) == size - 1
            first = at_start if first is None else jnp.logical_and(first, at_start)
            last = at_end if last is None else jnp.logical_and(last, at_end)

        def copies():
            return comm.plan(c_ins, c_outs, lambda k: send_sems.at[k], lambda k: recv_sems.at[k],
                             lambda k: local_sems.at[k])

        @pl.when(first)
        def _():
            sends, _, locs = copies()
            for cp in sends + locs:
                cp.start()

        body(*ins, *outs, *scr)

        @pl.when(last)
        def _():
            sends, recvs, locs = copies()
            for cp in recvs:
                cp.wait_recv()
            for cp in sends:
                cp.wait_send()
            for cp in locs:
                cp.wait()

    return pl.pallas_call(
        with_comm, name=name, grid=grid,
        in_specs=list(in_specs) + [ANY] * c_in, out_specs=list(out_specs) + [ANY] * c_out,
        out_shape=list(out_shape) + list(comm.out_shapes),
        scratch_shapes=list(scratch) + _sem_scratch(comm),
        input_output_aliases={n_in + k: n_out + v for k, v in comm.aliases.items()},
        compiler_params=params)(*operands, *comm.operands)


def _place():
    return lax.axis_index("x"), lax.axis_index("y"), lax.axis_index("c")


def _other_chips(x, y):
    return [(1 - x, y), (x, 1 - y), (1 - x, 1 - y)]


def _slot(px, py, pc):
    return 4 * px + 2 * py + pc


def _remote(src, dst, send_sem, recv_sem, to):
    return pltpu.make_async_remote_copy(src_ref=src, dst_ref=dst, send_sem=send_sem, recv_sem=recv_sem,
                                        device_id=to, device_id_type=MESH)


def _gather_first(half_block):
    def plan(ins, outs, send, recv, loc):
        (blk,), (full,) = ins, outs
        x, y, c = _place()
        chips = _other_chips(x, y)
        mine = full.at[_slot(x, y, c)]
        sends = [_remote(blk, mine, send(0), recv(0), (x, y, 1 - c))]
        sends += [_remote(blk, mine, send(1 + j), recv(1 + j), (*chip, c)) for j, chip in enumerate(chips)]
        recvs = [_remote(blk, full.at[_slot(x, y, 1 - c)], send(0), recv(0), (x, y, 1 - c))]
        recvs += [_remote(blk, full.at[_slot(*chip, c)], send(1 + j), recv(1 + j), (*chip, c))
                  for j, chip in enumerate(chips)]
        return sends, recvs, [pltpu.make_async_copy(blk, mine, loc(0))]

    return _Comm((half_block,), (SDS((2 * N_CHIPS,) + half_block.shape, half_block.dtype),), {}, 4, 1, plan)


def _gather_second(partly_gathered):
    def plan(ins, outs, send, recv, loc):
        (src,), (full,) = ins, outs
        x, y, c = _place()
        chips = _other_chips(x, y)
        sends = [_remote(src.at[_slot(*chip, c)], full.at[_slot(*chip, c)], send(j), recv(j), (x, y, 1 - c))
                 for j, chip in enumerate(chips)]
        recvs = [_remote(src.at[_slot(*chip, 1 - c)], full.at[_slot(*chip, 1 - c)], send(j), recv(j), (x, y, 1 - c))
                 for j, chip in enumerate(chips)]
        return sends, recvs, []

    return _Comm((partly_gathered,), (SDS(partly_gathered.shape, partly_gathered.dtype),), {0: 0}, 3, 0, plan)


def _gather_whole(half_block, small_block):
    def body(blk_ref, small_ref, out_ref, small_out_ref, send_sems, recv_sems, local_sems):
        x, y, c = _place()
        me, sibling = (x, y, c), (x, y, 1 - c)
        chips = _other_chips(x, y)

        def copy(k, block, to, src=None):
            return _remote(out_ref.at[_slot(*block)] if src is None else src, out_ref.at[_slot(*block)],
                           send_sems.at[k], recv_sems.at[k], to)

        def small_copy(k, chip, to):
            return _remote(small_ref, small_out_ref.at[2 * chip[0] + chip[1]], send_sems.at[7 + k], recv_sems.at[7 + k], to)

        mine = pltpu.make_async_copy(blk_ref, out_ref.at[_slot(*me)], local_sems.at[0])
        mine_small = pltpu.make_async_copy(small_ref, small_out_ref.at[2 * x + y], local_sems.at[1])
        mine.start()
        mine_small.start()
        first = [copy(0, me, sibling, src=blk_ref)]
        first += [copy(1 + j, me, (*chip, c), src=blk_ref) for j, chip in enumerate(chips)]
        first += [small_copy(j, (x, y), (*chip, c)) for j, chip in enumerate(chips)]
        for cp in first:
            cp.start()
        passed = [copy(4 + j, (*chip, c), sibling) for j, chip in enumerate(chips)]
        for j, chip in enumerate(chips):
            copy(1 + j, (*chip, c), me).wait_recv()
            passed[j].start()
        copy(0, sibling, me).wait_recv()
        for j, chip in enumerate(chips):
            copy(4 + j, (*chip, 1 - c), me).wait_recv()
            small_copy(j, chip, me).wait_recv()
        for cp in first + passed:
            cp.wait_send()
        mine.wait()
        mine_small.wait()

    return pl.pallas_call(
        body, name="gather_whole", in_specs=[ANY, ANY], out_specs=[ANY, ANY],
        out_shape=[SDS((2 * N_CHIPS,) + half_block.shape, half_block.dtype),
                   SDS((N_CHIPS,) + small_block.shape, small_block.dtype)],
        scratch_shapes=[pltpu.SemaphoreType.DMA((10,)), pltpu.SemaphoreType.DMA((10,)), pltpu.SemaphoreType.DMA((2,))],
    )(half_block, small_block)


def _pair_send(grads):
    def plan(ins, outs, send, recv, loc):
        (g,), (got,) = ins, outs
        x, y, c = _place()
        copies = [_remote(g.at[j, 1 - c], got.at[j], send(j), recv(j), (x, y, 1 - c)) for j in range(N_CHIPS)]
        return copies, copies, []

    shape = (grads.shape[0],) + grads.shape[2:]
    return _Comm((grads,), (SDS(shape, grads.dtype),), {}, N_CHIPS, 0, plan)


def _chip_exchange(partial):
    def plan(ins, outs, send, recv, loc):
        (p,), (got,) = ins, outs
        x, y, c = _place()
        my_chip = 2 * x + y
        chips = _other_chips(x, y)
        sends = [_remote(p.at[2 * chip[0] + chip[1]], got.at[my_chip], send(j), recv(j), (*chip, c))
                 for j, chip in enumerate(chips)]
        recvs = [_remote(p.at[my_chip], got.at[2 * chip[0] + chip[1]], send(j), recv(j), (*chip, c))
                 for j, chip in enumerate(chips)]
        return sends, recvs, [pltpu.make_async_copy(p.at[my_chip], got.at[my_chip], loc(0))]

    return _Comm((partial,), (SDS(partial.shape, partial.dtype),), {}, 3, 1, plan)


def _pair_sum(name, core, grads, received):
    h = grads.shape[2]

    def body(core_ref, g_ref, r_ref, o_ref):
        o_ref[...] = (g_ref[0] + r_ref[...]).astype(BF16)

    return pl.pallas_call(
        body, name=name,
        grid_spec=pltpu.PrefetchScalarGridSpec(
            num_scalar_prefetch=1, grid=(N_CHIPS,),
            in_specs=[pl.BlockSpec((1, 1, h, D_MODEL), lambda j, core_ref: (j, core_ref[0], 0, 0)),
                      pl.BlockSpec((1, h, D_MODEL), lambda j, core_ref: (j, 0, 0))],
            out_specs=pl.BlockSpec((1, h, D_MODEL), lambda j, core_ref: (j, 0, 0))),
        out_shape=SDS((N_CHIPS, h, D_MODEL), BF16),
        compiler_params=pltpu.CompilerParams(dimension_semantics=("arbitrary",), vmem_limit_bytes=VMEM_LIMIT_V7X),
    )(core, grads, received)


SMALL_ROWS = 8


def _sum_blocks(ref):
    return (ref[0].astype(F32) + ref[1].astype(F32)) + (ref[2].astype(F32) + ref[3].astype(F32))


def _tail_reduce(last_grads, exchanged, small):
    n = len(exchanged)
    h = last_grads.shape[2]

    def body(*refs):
        g_ref, ex, small_ref = refs[0], refs[1:1 + n], refs[1 + n]
        o0 = 2 + n
        out, out_last, small_out = refs[o0:o0 + n], refs[o0 + n], refs[o0 + n + 1]
        s0 = o0 + n + 2
        halves, half_last = refs[s0:s0 + n], refs[s0 + n]
        own, got, part, exch, small_buf = refs[s0 + n + 1:s0 + n + 6]
        pair_send, pair_recv, chip_send, chip_recv, share_send, share_recv, small_send, small_recv, local_sems = refs[s0 + n + 6:]
        x, y, c = _place()
        sibling = (x, y, 1 - c)
        my_chip, me = 2 * x + y, _slot(x, y, c)
        chips = _other_chips(x, y)

        to_sibling = [_remote(g_ref.at[j, 1 - c], got.at[j], pair_send.at[j], pair_recv.at[j], sibling)
                      for j in range(N_CHIPS)]
        load_own = [pltpu.make_async_copy(g_ref.at[j, c], own.at[j], local_sems.at[j]) for j in range(N_CHIPS)]
        for cp in to_sibling + load_own:
            cp.start()

        small_buf[me] = small_ref[...]
        small_copies = []
        for mask in range(1, 8):
            peer = (x ^ (mask >> 2), y ^ ((mask >> 1) & 1), c ^ (mask & 1))
            small_copies.append(_remote(small_ref, small_buf.at[me], small_send.at[mask - 1], small_recv.at[mask - 1], peer))
        for cp in small_copies:
            cp.start()

        def share(k, half_ref, out_ref):
            keep = pltpu.make_async_copy(half_ref, out_ref.at[c], local_sems.at[N_CHIPS + k])
            give = _remote(half_ref, out_ref.at[c], share_send.at[k], share_recv.at[k], sibling)
            take = _remote(half_ref, out_ref.at[1 - c], share_send.at[k], share_recv.at[k], sibling)
            keep.start()
            give.start()
            return keep, give, take

        shares = []
        for k in range(n):
            halves[k][...] = _sum_blocks(ex[k])
            shares.append(share(k, halves[k], out[k]))

        for cp in to_sibling:
            cp.wait_recv()
        for cp in load_own:
            cp.wait()
        part[...] = (own[...] + got[...]).astype(BF16)
        exch[my_chip] = part[my_chip]
        to_chips = [_remote(part.at[2 * chip[0] + chip[1]], exch.at[my_chip], chip_send.at[j], chip_recv.at[j], (*chip, c))
                    for j, chip in enumerate(chips)]
        from_chips = [_remote(part.at[my_chip], exch.at[2 * chip[0] + chip[1]], chip_send.at[j], chip_recv.at[j], (*chip, c))
                      for j, chip in enumerate(chips)]
        for cp in to_chips:
            cp.start()

        for cp in small_copies:
            cp.wait_recv()
        total = small_buf[0]
        for d in range(1, 8):
            total = total + small_buf[d]
        small_out[...] = total

        for cp in from_chips:
            cp.wait_recv()
        half_last[...] = _sum_blocks(exch)
        shares.append(share(n, half_last, out_last))

        for keep, give, take in shares:
            take.wait_recv()
            give.wait_send()
            keep.wait()
        for cp in to_sibling + to_chips + small_copies:
            cp.wait_send()

    blocks = (N_CHIPS, h, D_MODEL)
    return pl.pallas_call(
        body, name="tail_reduce",
        in_specs=[ANY] + [VMEM_WHOLE] * (n + 1), out_specs=[ANY] * (n + 1) + [VMEM_WHOLE],
        out_shape=[SDS((2,) + e.shape[1:], F32) for e in exchanged] + [SDS((2, h, D_MODEL), F32), SDS(small.shape, F32)],
        scratch_shapes=[pltpu.VMEM(e.shape[1:], F32) for e in exchanged] + [pltpu.VMEM((h, D_MODEL), F32)]
                       + [pltpu.VMEM(blocks, F32), pltpu.VMEM(blocks, F32), pltpu.VMEM(blocks, BF16), pltpu.VMEM(blocks, BF16),
                          pltpu.VMEM((8,) + small.shape, F32)]
                       + [pltpu.SemaphoreType.DMA((N_CHIPS,)), pltpu.SemaphoreType.DMA((N_CHIPS,)),
                          pltpu.SemaphoreType.DMA((3,)), pltpu.SemaphoreType.DMA((3,)),
                          pltpu.SemaphoreType.DMA((n + 1,)), pltpu.SemaphoreType.DMA((n + 1,)),
                          pltpu.SemaphoreType.DMA((7,)), pltpu.SemaphoreType.DMA((7,)),
                          pltpu.SemaphoreType.DMA((N_CHIPS + n + 1,))],
        compiler_params=pltpu.CompilerParams(vmem_limit_bytes=VMEM_LIMIT_V7X),
    )(last_grads, *exchanged, small)


def _rope_expansion():
    half = ROT_DIM // 2
    expand = np.zeros((2 * half, 3 * 128), np.float32)
    const = np.zeros((1, 3 * 128), np.float32)
    for lane in range(128):
        d = lane % HEAD_DIM
        if d < ROT_DIM:
            expand[d % half, lane] = 1.0
        else:
            const[0, lane] = 1.0
        if d < half:
            expand[half + d, 128 + lane] = -1.0
        elif d < ROT_DIM:
            expand[half + d - half, 256 + lane] = 1.0
    return expand, const


ROPE_PIECES = 3 * ROT_DIM


def _rope_inputs(seq):
    pos = jnp.arange(seq, dtype=F32)
    inv_freq = ROPE_THETA ** (-jnp.arange(0, ROT_DIM, 2, dtype=F32) / ROT_DIM)
    ang = pos[:, None] * inv_freq[None, :]
    cs = jnp.concatenate([jnp.cos(ang), jnp.sin(ang)], axis=1)
    hi = lax.reduce_precision(cs, 8, 7)
    mid = lax.reduce_precision(cs - hi, 8, 7)
    low = cs - hi - mid
    expand, const = _rope_expansion()
    pieces = jnp.concatenate([hi, mid, low], axis=1).astype(BF16)
    return pieces, jnp.asarray(np.concatenate([expand] * 3, axis=0), BF16), jnp.asarray(const)


def _rope_specs(tb):
    return [pl.BlockSpec((tb, ROPE_PIECES), lambda i: (i, 0)), _resident((ROPE_PIECES, 3 * 128)), _resident((1, 3 * 128))]


def _rope_tile(pieces_ref, expand_ref, const_ref):
    tables = _dot(pieces_ref[...], expand_ref[...]) + const_ref[...]
    return tables[:, 0:128], tables[:, 128:256], tables[:, 256:384]


def _rope(t, c, sa, sb):
    half = ROT_DIM // 2
    return t * c + pltpu.roll(t, 128 - half, 1) * sa + pltpu.roll(t, half, 1) * sb


def _rope_transposed(dt, c, sa, sb):
    half = ROT_DIM // 2
    return dt * c + pltpu.roll(dt * sa, half, 1) + pltpu.roll(dt * sb, 128 - half, 1)


def _cast_halves(core, w_up, w_down, w_out, w_in_t):
    def body(core_ref, up_ref, down_ref, out_ref, in_ref, up_o, down_o, out_o, in_o):
        up_o[...] = up_ref[...].astype(BF16)
        down_o[...] = down_ref[...].astype(BF16)
        out_o[...] = out_ref[...].astype(BF16)
        in_o[...] = in_ref[...].astype(BF16)

    half = lambda rows: pl.BlockSpec((rows, D_MODEL), lambda i, core_ref: (core_ref[0], 0))
    whole = lambda rows: pl.BlockSpec((rows, D_MODEL), lambda i, core_ref: (0, 0))
    rows = (H_UP, H_DOWN, H_OUT, H_IN)
    return pl.pallas_call(
        body, name="cast_halves",
        grid_spec=pltpu.PrefetchScalarGridSpec(
            num_scalar_prefetch=1, grid=(1,), in_specs=[half(r) for r in rows], out_specs=[whole(r) for r in rows]),
        out_shape=[SDS((r, D_MODEL), BF16) for r in rows],
        compiler_params=pltpu.CompilerParams(dimension_semantics=("arbitrary",), vmem_limit_bytes=VMEM_LIMIT_V7X),
    )(core, w_up, w_down, w_out, w_in_t)


def _in_proj(x, g_pre, w_in_t, rope, comm=None):
    seq = x.shape[0]
    tb = TOKEN_TILE

    def body(x_ref, g_ref, w_ref, c_ref, sa_ref, sb_ref,
             q_ref, kd0_ref, kd1_ref, vd0_ref, vd1_ref, gb_ref, gc_ref, xin_ref, hn_ref):
        xv = x_ref[...]
        hn = (xv * _rms(xv) * g_ref[...]).astype(BF16)
        hn_ref[...] = hn
        proj = _dot_nt(hn, w_ref[...].reshape(IN_COLS, D_MODEL))
        c, sa, sb = _rope_tile(c_ref, sa_ref, sb_ref)
        scale = 1.0 / math.sqrt(HEAD_DIM)
        for p in range(Q_WIDTH // 128):
            q_ref[:, 128 * p:128 * (p + 1)] = (_rope(proj[:, 128 * p:128 * (p + 1)], c, sa, sb) * scale).astype(BF16)
        k = _rope(proj[:, Q_WIDTH:Q_WIDTH + KV_WIDTH], c, sa, sb)
        v = proj[:, Q_WIDTH + KV_WIDTH:Q_WIDTH + 2 * KV_WIDTH]
        low = _lane_lt64(k.shape)
        k_sw, v_sw = pltpu.roll(k, HEAD_DIM, 1), pltpu.roll(v, HEAD_DIM, 1)
        kd0_ref[...] = jnp.where(low, k, k_sw).astype(BF16)
        kd1_ref[...] = jnp.where(low, k_sw, k).astype(BF16)
        vd0_ref[...] = jnp.where(low, v, v_sw).astype(BF16)
        vd1_ref[...] = jnp.where(low, v_sw, v).astype(BF16)
        base = Q_WIDTH + 2 * KV_WIDTH
        gb_ref[...] = proj[:, base:base + CONV_WIDTH].astype(BF16)
        gc_ref[...] = proj[:, base + CONV_WIDTH:base + 2 * CONV_WIDTH].astype(BF16)
        xin_ref[...] = proj[:, base + 2 * CONV_WIDTH:base + 3 * CONV_WIDTH].astype(BF16)

    tile = lambda w: pl.BlockSpec((tb, w), lambda i: (i, 0))
    return _pallas(
        body, name="in_proj", grid=(seq // tb,),
        in_specs=[tile(D_MODEL), _resident((1, D_MODEL)), _resident(w_in_t.shape), *_rope_specs(tb)],
        out_specs=[tile(Q_WIDTH), tile(128), tile(128), tile(128), tile(128),
                   tile(CONV_WIDTH), tile(CONV_WIDTH), tile(CONV_WIDTH), tile(D_MODEL)],
        out_shape=[SDS((seq, Q_WIDTH), BF16)] + [SDS((seq, 128), BF16)] * 4
                  + [SDS((seq, CONV_WIDTH), BF16)] * 3 + [SDS((seq, D_MODEL), BF16)],
        operands=(x, g_pre, w_in_t, *rope), comm=comm)


def _attn_valid(i):
    shape = (4 * QBLOCK, 2 * QBLOCK)
    row = lax.broadcasted_iota(jnp.int32, shape, 0)
    col = lax.broadcasted_iota(jnp.int32, shape, 1)
    qi = row & (QBLOCK - 1)
    return (col > qi) & (col <= qi + QBLOCK) & ((col >= QBLOCK) | (i > 0))


def _stack_heads(pair0, pair1):
    low = _lane_lt64(pair0.shape)
    zero = jnp.zeros_like(pair0)
    return jnp.concatenate([jnp.where(low, pair0, zero), jnp.where(low, zero, pair0),
                            jnp.where(low, pair1, zero), jnp.where(low, zero, pair1)], axis=0)


def _unstack_heads(stacked):
    low = _lane_lt64((QBLOCK, 128))
    pair0 = jnp.where(low, stacked[0:QBLOCK], stacked[QBLOCK:2 * QBLOCK])
    pair1 = jnp.where(low, stacked[2 * QBLOCK:3 * QBLOCK], stacked[3 * QBLOCK:4 * QBLOCK])
    return pair0, pair1


def _sink_column(sink_ref, kv_head):
    row = lax.broadcasted_iota(jnp.int32, (4 * QBLOCK, 1), 0)
    s = [sink_ref[0, 4 * kv_head + j] for j in range(4)]
    return jnp.where(row < QBLOCK, s[0], jnp.where(row < 2 * QBLOCK, s[1], jnp.where(row < 3 * QBLOCK, s[2], s[3])))


def _band(ref, i):
    prev = pl.multiple_of(jnp.maximum(i - 1, 0) * QBLOCK, QBLOCK)
    own = pl.multiple_of(i * QBLOCK, QBLOCK)
    return jnp.concatenate([ref[pl.ds(prev, QBLOCK), :], ref[pl.ds(own, QBLOCK), :]], axis=0), prev, own


def _softmax_with_sink(s, sink_col):
    m = jnp.maximum(jnp.max(s, axis=-1, keepdims=True), sink_col)
    p = jnp.exp(s - m)
    e_sink = jnp.exp(sink_col - m)
    inv_l = 1.0 / (jnp.sum(p, axis=-1, keepdims=True) + e_sink)
    return p, e_sink, inv_l


def _attention_fwd(q, kd0, kd1, vd0, vd1, sinks, comm=None):
    seq = q.shape[0]

    nb = ATTN_FWD_BLOCKS

    def body(sink_ref, q_ref, kd0_ref, kd1_ref, vd0_ref, vd1_ref, o_ref):
        for b in range(nb):
            i = pl.program_id(0) * nb + b
            rows = slice(QBLOCK * b, QBLOCK * (b + 1))
            valid = _attn_valid(i)
            for kv_head, (k_ref, v_ref) in enumerate(((kd0_ref, vd0_ref), (kd1_ref, vd1_ref))):
                kband, _, _ = _band(k_ref, i)
                vband, _, _ = _band(v_ref, i)
                base = 256 * kv_head
                qm = _stack_heads(q_ref[rows, base:base + 128], q_ref[rows, base + 128:base + 256])
                s = jnp.where(valid, _dot_nt(qm, kband), NEG_INF)
                p, _, inv_l = _softmax_with_sink(s, _sink_column(sink_ref, kv_head))
                o = _dot(p.astype(BF16), vband) * inv_l
                pair0, pair1 = _unstack_heads(o)
                o_ref[rows, base:base + 128] = pair0.astype(BF16)
                o_ref[rows, base + 128:base + 256] = pair1.astype(BF16)

    blk = pl.BlockSpec((nb * QBLOCK, Q_WIDTH), lambda i: (i, 0))
    full = _resident((seq, 128))
    return _pallas(
        body, name="attention_fwd", grid=(seq // (nb * QBLOCK),),
        in_specs=[pl.BlockSpec(memory_space=pltpu.SMEM), blk, full, full, full, full],
        out_specs=[blk], out_shape=[SDS((seq, Q_WIDTH), BF16)],
        operands=(sinks, q, kd0, kd1, vd0, vd1), comm=comm)


HALO = 16


def _conv_parts(gc, xin, gc_halo, xin_halo, conv_w, first):
    tb = gc.shape[0]
    u = gc.astype(F32) * xin.astype(F32)
    u_halo = jnp.where(first, 0.0, gc_halo.astype(F32) * xin_halo.astype(F32))
    ext = jnp.concatenate([u_halo, u], axis=0)
    u1 = pltpu.roll(ext, 1, 0)[HALO:HALO + tb]
    u2 = pltpu.roll(ext, 2, 0)[HALO:HALO + tb]
    y = conv_w[0:1, :] * u2 + conv_w[1:2, :] * u1 + conv_w[2:3, :] * u
    return u, u1, u2, y


def _halo_prev(tb, w):
    return pl.BlockSpec((HALO, w), lambda i: (jnp.maximum(i * (tb // HALO) - 1, 0), 0))


def _residual_mid(x, mix, g_post_mix):
    mix_f = mix.astype(F32)
    return x + mix_f * _rms(mix_f) * g_post_mix


def _mix_out(attn, gb, gc, xin, conv_w, g_attn, g_conv, w_out, comm=None):
    seq = attn.shape[0]
    tb = TOKEN_TILE

    def body(a_ref, gb_ref, gc_ref, xin_ref, gch_ref, xinh_ref, cw_ref, ga_ref, gcn_ref, w_ref, mix_ref, mixed_ref):
        first = pl.program_id(0) == 0
        _, _, _, y = _conv_parts(gc_ref[...], xin_ref[...], gch_ref[...], xinh_ref[...], cw_ref[...], first)
        conv = gb_ref[...].astype(F32) * y
        a = a_ref[...].astype(F32)
        mixed_ref[:, 0:Q_WIDTH] = (a * _rms(a) * ga_ref[...]).astype(BF16)
        mixed_ref[:, Q_WIDTH:] = (conv * _rms(conv) * gcn_ref[...]).astype(BF16)
        mix_ref[...] = _dot(mixed_ref[...], w_ref[...].reshape(D_MODEL, D_MODEL)).astype(BF16)

    tile = lambda w: pl.BlockSpec((tb, w), lambda i: (i, 0))
    return _pallas(
        body, name="mix_out", grid=(seq // tb,),
        in_specs=[tile(Q_WIDTH), tile(CONV_WIDTH), tile(CONV_WIDTH), tile(CONV_WIDTH),
                  _halo_prev(tb, CONV_WIDTH), _halo_prev(tb, CONV_WIDTH),
                  _resident((CONV_K, CONV_WIDTH)), _resident((1, Q_WIDTH)), _resident((1, CONV_WIDTH)),
                  _resident(w_out.shape)],
        out_specs=[tile(D_MODEL), tile(D_MODEL)],
        out_shape=[SDS((seq, D_MODEL), BF16), SDS((seq, D_MODEL), BF16)],
        operands=(attn, gb, gc, xin, gc, xin, conv_w, g_attn, g_conv, w_out), comm=comm)


def _mlp_loss(x, mix, target, g_post_mix, g_pre_mlp, g_post_mlp, w_up, w_down):
    seq = x.shape[0]
    tb = TOKEN_TILE

    def body(x_ref, mix_ref, t_ref, gpm_ref, g2_ref, g4_ref, wup_ref, wdown_ref,
             up_ref, hn2_ref, dout_ref, dmlp_ref, loss_ref, dg4_ref, act_ref):
        @pl.when(pl.program_id(0) == 0)
        def _():
            loss_ref[...] = jnp.zeros_like(loss_ref)
            dg4_ref[...] = jnp.zeros_like(dg4_ref)

        halves = [slice(0, tb // 2), slice(tb // 2, tb)]
        hv, hn2 = [], []
        for rows in halves:
            hv.append(_residual_mid(x_ref[rows, :], mix_ref[rows, :], gpm_ref[...]))
            hn2.append((hv[-1] * _rms(hv[-1]) * g2_ref[...]).astype(BF16))
            hn2_ref[rows, :] = hn2[-1]
        for k, rows in enumerate(halves):
            for j in range(N_CHIPS):
                up = _dot(hn2[k], _chip_block(wup_ref, j))
                up = jnp.maximum(up, 0.0)
                up_ref[rows, 1024 * j:1024 * (j + 1)] = up.astype(BF16)
                act_ref[rows, 1024 * j:1024 * (j + 1)] = (up * up).astype(BF16)
        w_down_all = wdown_ref[...].reshape(D_FF, D_MODEL)
        loss = jnp.zeros((1, 1), F32)
        dg4 = jnp.zeros((1, D_MODEL), F32)
        for k, rows in enumerate(halves):
            mlp = _dot(act_ref[rows, :], w_down_all)
            rstd = _rms(mlp)
            zhat = mlp * rstd
            diff = hv[k] + zhat * g4_ref[...] - t_ref[rows, :]
            loss = loss + jnp.sum(jnp.sum(diff * diff, axis=1, keepdims=True), axis=0, keepdims=True)
            dout = diff * (1.0 / D_MODEL)
            dout_ref[rows, :] = dout
            dg4 = dg4 + _colsum(dout * zhat)
            dmlp_ref[rows, :] = _norm_bwd(dout, g4_ref[...], zhat, rstd).astype(BF16)
        loss_ref[...] += loss
        dg4_ref[...] += dg4

    tile = lambda w: pl.BlockSpec((tb, w), lambda i: (i, 0))
    return _pallas(
        body, name="mlp_loss", grid=(seq // tb,),
        in_specs=[tile(D_MODEL), tile(D_MODEL), tile(D_MODEL), _resident((1, D_MODEL)), _resident((1, D_MODEL)),
                  _resident((1, D_MODEL)), _resident(w_up.shape), _resident(w_down.shape)],
        out_specs=[tile(D_FF), tile(D_MODEL), tile(D_MODEL), tile(D_MODEL),
                   pl.BlockSpec((1, 1), lambda i: (0, 0)), pl.BlockSpec((1, D_MODEL), lambda i: (0, 0))],
        out_shape=[SDS((seq, D_FF), BF16), SDS((seq, D_MODEL), BF16), SDS((seq, D_MODEL), F32),
                   SDS((seq, D_MODEL), BF16), SDS((1, 1), F32), SDS((1, D_MODEL), F32)],
        scratch=[pltpu.VMEM((tb, D_FF), BF16)],
        operands=(x, mix, target, g_post_mix, g_pre_mlp, g_post_mlp, w_up, w_down))


def _mlp_bwd(dmlp, up, x, dout, mix, g_pre_mlp, g_post_mix, w_up, w_down):
    seq = x.shape[0]
    tb = MLP_BWD_TOKEN_TILE

    def body(dmlp_ref, up_ref, x_ref, dout_ref, mix_ref, g2_ref, gpm_ref, wup_ref, wdown_ref,
             dup_ref, dh_ref, dmix_ref, dg2_ref, dgpm_ref):
        @pl.when(pl.program_id(0) == 0)
        def _():
            dg2_ref[...] = jnp.zeros_like(dg2_ref)
            dgpm_ref[...] = jnp.zeros_like(dgpm_ref)

        subs = [slice(k * MLP_BWD_SUB_TILE, (k + 1) * MLP_BWD_SUB_TILE) for k in range(tb // MLP_BWD_SUB_TILE)]
        dhn2 = []
        for rows in subs:
            dmlp_v = dmlp_ref[rows, :]
            acc = None
            for j in range(N_CHIPS):
                cols = slice(1024 * j, 1024 * (j + 1))
                dact = _dot_nt(dmlp_v, _chip_block(wdown_ref, j))
                dup = (dact * (2.0 * up_ref[rows, cols].astype(F32))).astype(BF16)
                dup_ref[rows, cols] = dup
                part = _dot_nt(dup, _chip_block(wup_ref, j))
                acc = part if acc is None else acc + part
            dhn2.append(acc)
        dg2 = jnp.zeros((1, D_MODEL), F32)
        dgpm = jnp.zeros((1, D_MODEL), F32)
        for k, rows in enumerate(subs):
            mix_v = mix_ref[rows, :].astype(F32)
            hv = _residual_mid(x_ref[rows, :], mix_ref[rows, :], gpm_ref[...])
            r2 = _rms(hv)
            hhat = hv * r2
            dg2 = dg2 + _colsum(dhn2[k] * hhat)
            dh = dout_ref[rows, :] + _norm_bwd(dhn2[k], g2_ref[...], hhat, r2)
            dh_ref[rows, :] = dh.astype(BF16)
            rz = _rms(mix_v)
            zhat = mix_v * rz
            dgpm = dgpm + _colsum(dh * zhat)
            dmix_ref[rows, :] = _norm_bwd(dh, gpm_ref[...], zhat, rz).astype(BF16)
        dg2_ref[...] += dg2
        dgpm_ref[...] += dgpm

    tile = lambda w: pl.BlockSpec((tb, w), lambda i: (i, 0))
    vec = pl.BlockSpec((1, D_MODEL), lambda i: (0, 0))
    return _pallas(
        body, name="mlp_bwd", grid=(seq // tb,),
        in_specs=[tile(D_MODEL), tile(D_FF), tile(D_MODEL), tile(D_MODEL), tile(D_MODEL),
                  _resident((1, D_MODEL)), _resident((1, D_MODEL)), _resident(w_up.shape), _resident(w_down.shape)],
        out_specs=[tile(D_FF), tile(D_MODEL), tile(D_MODEL), vec, vec],
        out_shape=[SDS((seq, D_FF), BF16), SDS((seq, D_MODEL), BF16), SDS((seq, D_MODEL), BF16),
                   SDS((1, D_MODEL), F32), SDS((1, D_MODEL), F32)],
        operands=(dmlp, up, x, dout, mix, g_pre_mlp, g_post_mix, w_up, w_down))


class _Rider(NamedTuple):
    body: Callable
    in_specs: list
    out_specs: list
    out_shape: list
    operands: tuple


def _mix_bwd(dmix, attn, gb, gc, xin, conv_w, g_attn, g_conv, w_out, n_k):
    seq = attn.shape[0]
    tb = seq // (N_CHIPS * n_k)

    def body(first, dmix_ref, a_ref, gb_ref, gc_ref, xin_ref, gch_ref, xinh_ref, cw_ref, ga_ref, gcn_ref, w_ref,
             dattn_ref, dgb_ref, dy_ref, dga_ref, dgcn_ref, dcw_ref):
        @pl.when(first)
        def _():
            dga_ref[...] = jnp.zeros_like(dga_ref)
            dgcn_ref[...] = jnp.zeros_like(dgcn_ref)
            dcw_ref[...] = jnp.zeros_like(dcw_ref)

        dmixed = _dot_nt(dmix_ref[...], w_ref[...].reshape(D_MODEL, D_MODEL))
        a = a_ref[...].astype(F32)
        ra = _rms(a)
        ahat = a * ra
        dan = dmixed[:, 0:Q_WIDTH]
        dga_ref[...] += _colsum(dan * ahat)
        dattn_ref[...] = _norm_bwd(dan, ga_ref[...], ahat, ra).astype(BF16)
        gbv = gb_ref[...].astype(F32)
        u, u1, u2, y = _conv_parts(gc_ref[...], xin_ref[...], gch_ref[...], xinh_ref[...], cw_ref[...], first)
        conv = gbv * y
        rc = _rms(conv)
        chat = conv * rc
        dcn = dmixed[:, Q_WIDTH:]
        dgcn_ref[...] += _colsum(dcn * chat)
        dconv = _norm_bwd(dcn, gcn_ref[...], chat, rc)
        dgb_ref[...] = (dconv * y).astype(BF16)
        dy = dconv * gbv
        dy_ref[...] = dy.astype(BF16)
        dcw_ref[0:1, :] += _colsum(dy * u2)
        dcw_ref[1:2, :] += _colsum(dy * u1)
        dcw_ref[2:3, :] += _colsum(dy * u)

    tile = lambda w: pl.BlockSpec((tb, w), lambda j, k: (j * n_k + k, 0))
    halo = lambda w: pl.BlockSpec((HALO, w), lambda j, k: (jnp.maximum((j * n_k + k) * (tb // HALO) - 1, 0), 0))
    whole = lambda shape: pl.BlockSpec(shape, lambda j, k: (0,) * len(shape))
    return _Rider(
        body,
        in_specs=[tile(D_MODEL), tile(Q_WIDTH), tile(CONV_WIDTH), tile(CONV_WIDTH), tile(CONV_WIDTH),
                  halo(CONV_WIDTH), halo(CONV_WIDTH),
                  _resident((CONV_K, CONV_WIDTH)), _resident((1, Q_WIDTH)), _resident((1, CONV_WIDTH)),
                  _resident(w_out.shape)],
        out_specs=[tile(Q_WIDTH), tile(CONV_WIDTH), tile(CONV_WIDTH),
                   whole((1, Q_WIDTH)), whole((1, CONV_WIDTH)), whole((CONV_K, CONV_WIDTH))],
        out_shape=[SDS((seq, Q_WIDTH), BF16), SDS((seq, CONV_WIDTH), BF16), SDS((seq, CONV_WIDTH), BF16),
                   SDS((1, Q_WIDTH), F32), SDS((1, CONV_WIDTH), F32), SDS((CONV_K, CONV_WIDTH), F32)],
        operands=(dmix, attn, gb, gc, xin, gc, xin, conv_w, g_attn, g_conv, w_out))


def _attention_bwd(q, dattn, attn, kd0, kd1, vd0, vd1, sinks, comm=None):
    seq = q.shape[0]
    nb = ATTN_BWD_BLOCKS

    def body(sink_ref, q_ref, do_ref, o_ref, kd0_ref, kd1_ref, vd0_ref, vd1_ref,
             dq_ref, dk0_ref, dk1_ref, dv0_ref, dv1_ref, dsink_ref):
        @pl.when(pl.program_id(0) == 0)
        def _():
            for r in (dk0_ref, dk1_ref, dv0_ref, dv1_ref, dsink_ref):
                r[...] = jnp.zeros_like(r)

        lane = lax.broadcasted_iota(jnp.int32, (1, 128), 1)
        dsink = jnp.zeros((1, 128), F32)
        for b in range(nb):
            i = pl.program_id(0) * nb + b
            rows = slice(QBLOCK * b, QBLOCK * (b + 1))
            valid = _attn_valid(i)
            for kv_head, (k_ref, v_ref, dk_ref, dv_ref) in enumerate(
                    ((kd0_ref, vd0_ref, dk0_ref, dv0_ref), (kd1_ref, vd1_ref, dk1_ref, dv1_ref))):
                kband, prev, own = _band(k_ref, i)
                vband, _, _ = _band(v_ref, i)
                base = 256 * kv_head
                qm = _stack_heads(q_ref[rows, base:base + 128], q_ref[rows, base + 128:base + 256])
                dom = _stack_heads(do_ref[rows, base:base + 128], do_ref[rows, base + 128:base + 256])
                om = _stack_heads(o_ref[rows, base:base + 128], o_ref[rows, base + 128:base + 256])
                s = jnp.where(valid, _dot_nt(qm, kband), NEG_INF)
                p, e_sink, inv_l = _softmax_with_sink(s, _sink_column(sink_ref, kv_head))
                p = p * inv_l
                delta = jnp.sum(dom.astype(F32) * om.astype(F32), axis=-1, keepdims=True)
                ds = (p * (_dot_nt(dom, vband) - delta)).astype(BF16)
                sink_term = -(e_sink * inv_l) * delta
                for j in range(4):
                    part = jnp.sum(sink_term[QBLOCK * j:QBLOCK * (j + 1)], axis=0, keepdims=True)
                    dsink = dsink + jnp.where(lane == 4 * kv_head + j, part, 0.0)
                pair0, pair1 = _unstack_heads(_dot(ds, kband))
                dq_ref[rows, base:base + 128] = pair0.astype(BF16)
                dq_ref[rows, base + 128:base + 256] = pair1.astype(BF16)
                dkd = _dot_tn(ds, qm)
                dkd = dkd + pltpu.roll(dkd, HEAD_DIM, 1)
                dvd = _dot_tn(p.astype(BF16), dom)
                dvd = dvd + pltpu.roll(dvd, HEAD_DIM, 1)
                dk_ref[pl.ds(prev, QBLOCK), :] += dkd[0:QBLOCK]
                dk_ref[pl.ds(own, QBLOCK), :] += dkd[QBLOCK:]
                dv_ref[pl.ds(prev, QBLOCK), :] += dvd[0:QBLOCK]
                dv_ref[pl.ds(own, QBLOCK), :] += dvd[QBLOCK:]
        dsink_ref[...] += dsink

    blk = pl.BlockSpec((nb * QBLOCK, Q_WIDTH), lambda i: (i, 0))
    full = _resident((seq, 128))
    acc = pl.BlockSpec((seq, 128), lambda i: (0, 0))
    return _pallas(
        body, name="attention_bwd", grid=(seq // (nb * QBLOCK),),
        in_specs=[pl.BlockSpec(memory_space=pltpu.SMEM), blk, blk, blk, full, full, full, full],
        out_specs=[blk, acc, acc, acc, acc, pl.BlockSpec((1, 128), lambda i: (0, 0))],
        out_shape=[SDS((seq, Q_WIDTH), BF16)] + [SDS((seq, 128), F32)] * 4 + [SDS((1, 128), F32)],
        operands=(sinks, q, dattn, attn, kd0, kd1, vd0, vd1), comm=comm)


def _in_proj_bwd(dq, dk0, dk1, dv0, dv1, dgb, dy, gc, xin, conv_w, x, dh, g_pre, w_in_t, rope):
    seq = x.shape[0]
    tb = TOKEN_TILE
    n_tiles = seq // tb

    def body(dq_ref, dk0_ref, dk1_ref, dv0_ref, dv1_ref, dgb_ref, dy_ref, dyh_ref, gc_ref, xin_ref, cw_ref,
             x_ref, dh_ref, g_ref, w_ref, c_ref, sa_ref, sb_ref,
             dproj_ref, gx_ref, dg_ref):
        i = pl.program_id(0)

        @pl.when(i == 0)
        def _():
            dg_ref[...] = jnp.zeros_like(dg_ref)

        dy = dy_ref[...].astype(F32)
        ext = jnp.concatenate([dy, jnp.where(i == n_tiles - 1, 0.0, dyh_ref[...].astype(F32))], axis=0)
        dy1 = pltpu.roll(ext, tb + HALO - 1, 0)[0:tb]
        dy2 = pltpu.roll(ext, tb + HALO - 2, 0)[0:tb]
        cw = cw_ref[...]
        du = cw[2:3, :] * dy + cw[1:2, :] * dy1 + cw[0:1, :] * dy2
        scale = 1.0 / math.sqrt(HEAD_DIM)
        base = Q_WIDTH + 2 * KV_WIDTH
        halves = [slice(0, tb // 2), slice(tb // 2, tb)]
        low = _lane_lt64((tb // 2, 128))
        for rows in halves:
            c, sa, sb = _rope_tile(c_ref.at[rows, :], sa_ref, sb_ref)
            for p in range(Q_WIDTH // 128):
                dproj_ref[rows, 128 * p:128 * (p + 1)] = _rope_transposed(
                    dq_ref[rows, 128 * p:128 * (p + 1)].astype(F32) * scale, c, sa, sb).astype(BF16)
            dk = jnp.where(low, dk0_ref[rows, :], dk1_ref[rows, :])
            dproj_ref[rows, Q_WIDTH:Q_WIDTH + KV_WIDTH] = _rope_transposed(dk, c, sa, sb).astype(BF16)
            dproj_ref[rows, Q_WIDTH + KV_WIDTH:base] = jnp.where(low, dv0_ref[rows, :], dv1_ref[rows, :]).astype(BF16)
            dproj_ref[rows, base:base + CONV_WIDTH] = dgb_ref[rows, :]
            dproj_ref[rows, base + CONV_WIDTH:base + 2 * CONV_WIDTH] = (du[rows] * xin_ref[rows, :].astype(F32)).astype(BF16)
            dproj_ref[rows, base + 2 * CONV_WIDTH:] = (du[rows] * gc_ref[rows, :].astype(F32)).astype(BF16)
        w_all = w_ref[...].reshape(IN_COLS, D_MODEL)
        dhn = [_dot(dproj_ref[rows, :], w_all) for rows in halves]
        dg = jnp.zeros((1, D_MODEL), F32)
        for k, rows in enumerate(halves):
            xv = x_ref[rows, :]
            r = _rms(xv)
            xhat = xv * r
            dg = dg + _colsum(dhn[k] * xhat)
            gx_ref[rows, :] = dh_ref[rows, :].astype(F32) + _norm_bwd(dhn[k], g_ref[...], xhat, r)
        dg_ref[...] += dg

    tile = lambda w: pl.BlockSpec((tb, w), lambda i: (i, 0))
    halo_next = pl.BlockSpec((HALO, CONV_WIDTH), lambda i: (jnp.minimum((i + 1) * (tb // HALO), seq // HALO - 1), 0))
    return _pallas(
        body, name="in_proj_bwd", grid=(n_tiles,),
        in_specs=[tile(Q_WIDTH), tile(128), tile(128), tile(128), tile(128), tile(CONV_WIDTH), tile(CONV_WIDTH), halo_next,
                  tile(CONV_WIDTH), tile(CONV_WIDTH), _resident((CONV_K, CONV_WIDTH)),
                  tile(D_MODEL), tile(D_MODEL), _resident((1, D_MODEL)), _resident(w_in_t.shape), *_rope_specs(tb)],
        out_specs=[tile(IN_COLS), tile(D_MODEL), pl.BlockSpec((1, D_MODEL), lambda i: (0, 0))],
        out_shape=[SDS((seq, IN_COLS), BF16), SDS((seq, D_MODEL), F32), SDS((1, D_MODEL), F32)],
        operands=(dq, dk0, dk1, dv0, dv1, dgb, dy, dy, gc, xin, conv_w, x, dh, g_pre, w_in_t, *rope))


def _wgrad_grid(seq, per_chip, h_rows):
    chips_per_step = 1 if per_chip else N_CHIPS
    m = chips_per_step * 2 * h_rows
    bt = min(seq, WGRAD_TOKEN_TILE)
    return chips_per_step, m, bt, seq // bt


def _wgrad(name, a, b, *, per_chip, h_rows, square_a=False, comm=None, rider=None):
    seq = a.shape[0]
    chips_per_step, m, bt, n_k = _wgrad_grid(seq, per_chip, h_rows)
    a_cols = m if per_chip else a.shape[1]
    a_wide = a.shape[1] > a_cols
    b_wide = b.shape[1] > D_MODEL
    n_ride_in = len(rider.in_specs) if rider else 0
    n_ride_out = len(rider.out_specs) if rider else 0

    def body(a_ref, b_ref, *rest):
        ride_in, g_ref = rest[:n_ride_in], rest[n_ride_in]
        ride_out = rest[n_ride_in + 1:n_ride_in + 1 + n_ride_out]
        k = pl.program_id(1)

        @pl.when(k == 0)
        def _():
            g_ref[...] = jnp.zeros_like(g_ref)

        av = a_ref[...]
        if square_a:
            av = (av.astype(F32) * av.astype(F32)).astype(BF16)
        g_ref[...] += _dot_tn(av, b_ref[...]).reshape(g_ref.shape)

        if rider:
            rider.body(jnp.logical_and(pl.program_id(0) == 0, k == 0), *ride_in, *ride_out)

    a_spec = pl.BlockSpec((bt, a_cols), (lambda j, k: (k, j)) if a_wide else (lambda j, k: (k, 0)))
    b_spec = pl.BlockSpec((bt, D_MODEL), (lambda j, k: (k, j)) if b_wide else (lambda j, k: (k, 0)))
    g_spec = pl.BlockSpec((chips_per_step, 2, h_rows, D_MODEL), lambda j, k: (j, 0, 0, 0),
                          pipeline_mode=None if per_chip else pl.Buffered(1))
    return _pallas(
        body, name=name, grid=(N_CHIPS if per_chip else 1, n_k),
        in_specs=[a_spec, b_spec] + (rider.in_specs if rider else []),
        out_specs=[g_spec] + (rider.out_specs if rider else []),
        out_shape=[SDS((N_CHIPS, 2, h_rows, D_MODEL), F32)] + (rider.out_shape if rider else []),
        operands=(a, b) + (rider.operands if rider else ()), comm=comm)


def _adamw_math(w, g, m, v):
    m = ADAM_B1 * m + (1.0 - ADAM_B1) * g
    v = ADAM_B2 * v + (1.0 - ADAM_B2) * (g * g)
    m_hat = m / (1.0 - ADAM_B1 ** ADAM_STEP)
    v_hat = v / (1.0 - ADAM_B2 ** ADAM_STEP)
    delta = -ADAM_LR * (m_hat / (jnp.sqrt(v_hat) + ADAM_EPS) + ADAM_WD * w)
    return delta, m, v


def _adamw_rows(name, reduced, w, m, v, rt):
    per_half = reduced.shape[1] // rt

    def body(r_ref, w_ref, m_ref, v_ref, g_out, d_out, m_out, v_out):
        g = r_ref[0]
        g_out[...] = g
        d_out[...], m_out[...], v_out[...] = _adamw_math(w_ref[...], g, m_ref[...], v_ref[...])

    blk = pl.BlockSpec((rt, D_MODEL), lambda h, r: (h * per_half + r, 0))
    return _pallas(
        body, name=name, grid=(2, per_half),
        in_specs=[pl.BlockSpec((1, rt, D_MODEL), lambda h, r: (h, r, 0)), blk, blk, blk],
        out_specs=[blk, blk, blk, blk], out_shape=[SDS(w.shape, F32)] * 4, operands=(reduced, w, m, v))


def _adamw_small(w, g, m, v):
    def body(w_ref, g_ref, m_ref, v_ref, d_out, m_out, v_out):
        d_out[...], m_out[...], v_out[...] = _adamw_math(w_ref[...], g_ref[...], m_ref[...], v_ref[...])

    return pl.pallas_call(body, name="adamw_small", in_specs=[VMEM_WHOLE] * 4, out_specs=[VMEM_WHOLE] * 3,
                          out_shape=[SDS(w.shape, F32)] * 3)(w, g, m, v)


SMALL_VECTORS = ("pre_mix_norm", "post_mix_norm", "pre_mlp_norm", "post_mlp_norm")
SMALL_NAMES = SMALL_VECTORS + ("attn_group_norm", "conv_group_norm", "conv_w", "attn_sinks")


def _pack_small(p):
    rows = [p[n].reshape(1, D_MODEL) for n in SMALL_VECTORS]
    rows.append(jnp.concatenate([p["attn_group_norm"].reshape(1, -1), p["conv_group_norm"].reshape(1, -1)], axis=1))
    cw = p["conv_w"].reshape(CONV_K, -1)
    rows.append(jnp.pad(cw, ((0, 1), (0, CONV_WIDTH - cw.shape[1]))).reshape(2, D_MODEL))
    last = jnp.concatenate([p["attn_sinks"].reshape(1, 8), p.get("loss_sum", jnp.zeros((1, 1), F32))], axis=1)
    rows.append(jnp.pad(last, ((0, 0), (0, D_MODEL - 9))))
    return jnp.concatenate(rows, axis=0)


def _unpack_small(packed, conv_width):
    out = {n: packed[i:i + 1] for i, n in enumerate(SMALL_VECTORS)}
    out["attn_group_norm"] = packed[4:5, :Q_WIDTH]
    out["conv_group_norm"] = packed[4:5, Q_WIDTH:]
    out["conv_w"] = packed[5:7].reshape(4, CONV_WIDTH)[:CONV_K, :conv_width].reshape(1, CONV_K, conv_width)
    out["attn_sinks"] = packed[7:8, :8]
    out["loss_sum"] = packed[7, 8]
    return out


WEIGHT_ORDER = ("pre_mix_norm", "w_in", "conv_w", "attn_sinks", "attn_group_norm", "conv_group_norm", "w_out",
                "post_mix_norm", "pre_mlp_norm", "w_up", "w_down", "post_mlp_norm")


def kernel(x, pre_mix_norm, w_in, conv_w, attn_sinks, attn_group_norm, conv_group_norm, w_out, post_mix_norm, pre_mlp_norm, w_up, w_down, post_mlp_norm, loss_target, m_pre_mix_norm, m_w_in, m_conv_w, m_attn_sinks, m_attn_group_norm, m_conv_group_norm, m_w_out, m_post_mix_norm, m_pre_mlp_norm, m_w_up, m_w_down, m_post_mlp_norm, v_pre_mix_norm, v_w_in, v_conv_w, v_attn_sinks, v_attn_group_norm, v_conv_group_norm, v_w_out, v_post_mix_norm, v_pre_mlp_norm, v_w_up, v_w_down, v_post_mlp_norm):
    w = dict(pre_mix_norm=pre_mix_norm, w_in=w_in, conv_w=conv_w, attn_sinks=attn_sinks, attn_group_norm=attn_group_norm,
             conv_group_norm=conv_group_norm, w_out=w_out, post_mix_norm=post_mix_norm, pre_mlp_norm=pre_mlp_norm,
             w_up=w_up, w_down=w_down, post_mlp_norm=post_mlp_norm)
    m = dict(pre_mix_norm=m_pre_mix_norm, w_in=m_w_in, conv_w=m_conv_w, attn_sinks=m_attn_sinks,
             attn_group_norm=m_attn_group_norm, conv_group_norm=m_conv_group_norm, w_out=m_w_out,
             post_mix_norm=m_post_mix_norm, pre_mlp_norm=m_pre_mlp_norm, w_up=m_w_up, w_down=m_w_down,
             post_mlp_norm=m_post_mlp_norm)
    v = dict(pre_mix_norm=v_pre_mix_norm, w_in=v_w_in, conv_w=v_conv_w, attn_sinks=v_attn_sinks,
             attn_group_norm=v_attn_group_norm, conv_group_norm=v_conv_group_norm, w_out=v_w_out,
             post_mix_norm=v_post_mix_norm, pre_mlp_norm=v_pre_mlp_norm, w_up=v_w_up, w_down=v_w_down,
             post_mlp_norm=v_post_mlp_norm)
    core = lax.axis_index("c").astype(jnp.int32).reshape(1)
    chip = 2 * lax.axis_index("x") + lax.axis_index("y")
    local_conv = conv_w.shape[2]
    xs, target = x[0], loss_target[0]
    rope = _rope_inputs(xs.shape[0])

    hb_up, hb_down, hb_out, hb_in = _cast_halves(core, w_up[0], w_down[0], w_out[0], w_in[0].T)
    conv_pad = jnp.pad(conv_w[0], ((0, 8 - CONV_K), (0, 0)))
    wf_in, conv_all = _gather_whole(hb_in, conv_pad)
    conv_full = conv_all[:, :CONV_K, :].transpose(1, 0, 2).reshape(CONV_K, CONV_WIDTH)

    *proj, wf_up, wf_out = _in_proj(xs, pre_mix_norm, wf_in, rope, comm=_merge(_gather_first(hb_up), _gather_first(hb_out)))
    q, kd0, kd1, vd0, vd1, gb, gc, xin, hn = proj
    attn, wf_up, wf_out, wf_down = _attention_fwd(
        q, kd0, kd1, vd0, vd1, attn_sinks,
        comm=_merge(_gather_second(wf_up), _gather_second(wf_out), _gather_first(hb_down)))
    mix, mixed, wf_down = _mix_out(attn, gb, gc, xin, conv_full, attn_group_norm, conv_group_norm, wf_out,
                                   comm=_gather_second(wf_down))
    up, hn2, dout, dmlp, loss_sum, dg_post_mlp = _mlp_loss(xs, mix, target, post_mix_norm, pre_mlp_norm, post_mlp_norm,
                                                           wf_up, wf_down)

    dup, dh, dmix, dg_pre_mlp, dg_post_mix = _mlp_bwd(dmlp, up, xs, dout, mix, pre_mlp_norm, post_mix_norm, wf_up, wf_down)
    n_k = _wgrad_grid(xs.shape[0], True, H_DOWN)[3]
    g_down, dattn, dgb, dy, dg_attn, dg_conv, dconv_w = _wgrad(
        "wgrad_down", up, dmlp, per_chip=True, h_rows=H_DOWN, square_a=True,
        rider=_mix_bwd(dmix, attn, gb, gc, xin, conv_full, attn_group_norm, conv_group_norm, wf_out, n_k))
    g_up, got_down = _wgrad("wgrad_up", hn2, dup, per_chip=True, h_rows=H_UP, comm=_pair_send(g_down))
    p_down = _pair_sum("pair_sum_down", core, g_down, got_down)
    g_out, got_up = _wgrad("wgrad_out", mixed, dmix, per_chip=False, h_rows=H_OUT, comm=_pair_send(g_up))
    p_up = _pair_sum("pair_sum_up", core, g_up, got_up)
    dq, dk0, dk1, dv0, dv1, dsink, ex_down, ex_up, got_out = _attention_bwd(
        q, dattn, attn, kd0, kd1, vd0, vd1, attn_sinks,
        comm=_merge(_chip_exchange(p_down), _chip_exchange(p_up), _pair_send(g_out)))
    p_out = _pair_sum("pair_sum_out", core, g_out, got_out)
    dproj, grad_x, dg_pre_mix = _in_proj_bwd(dq, dk0, dk1, dv0, dv1, dgb, dy, gc, xin, conv_full, xs, dh, pre_mix_norm,
                                             wf_in, rope)
    g_in, ex_out = _wgrad("wgrad_in", dproj, hn, per_chip=False, h_rows=H_IN, comm=_chip_exchange(p_out))
    small = dict(pre_mix_norm=dg_pre_mix, conv_w=dconv_w, attn_sinks=dsink[:, :8], attn_group_norm=dg_attn,
                 conv_group_norm=dg_conv, post_mix_norm=dg_post_mix, pre_mlp_norm=dg_pre_mlp, post_mlp_norm=dg_post_mlp,
                 loss_sum=loss_sum)
    r_down, r_up, r_out, r_in, small_total = _tail_reduce(g_in, [ex_down, ex_up, ex_out], _pack_small(small))

    out_g, out_d, out_m, out_v = {}, {}, {}, {}
    out_g["w_up"], out_d["w_up"], out_m["w_up"], out_v["w_up"] = _adamw_rows(
        "adamw_up", r_up, w_up[0], m_w_up[0], v_w_up[0], 256)
    out_g["w_down"], out_d["w_down"], out_m["w_down"], out_v["w_down"] = _adamw_rows(
        "adamw_down", r_down, w_down[0], m_w_down[0], v_w_down[0], 256)
    out_g["w_out"], out_d["w_out"], out_m["w_out"], out_v["w_out"] = _adamw_rows(
        "adamw_out", r_out, w_out[0], m_w_out[0], v_w_out[0], H_OUT)
    in_t = _adamw_rows("adamw_in", r_in, w_in[0].T, m_w_in[0].T, v_w_in[0].T, H_IN)
    out_g["w_in"], out_d["w_in"], out_m["w_in"], out_v["w_in"] = [t.T for t in in_t]

    small_sum = _unpack_small(small_total, CONV_WIDTH)
    loss = small_sum["loss_sum"] * (0.5 / D_MODEL)
    small_sum["conv_w"] = lax.dynamic_slice_in_dim(small_sum["conv_w"], chip * local_conv, local_conv, axis=2)
    packed = [_pack_small({n: t[n] for n in SMALL_NAMES}) for t in (w, small_sum, m, v)]
    small_d, small_m, small_v = [_unpack_small(t, local_conv) for t in _adamw_small(*packed)]
    for n in SMALL_NAMES:
        out_g[n], out_d[n], out_m[n], out_v[n] = small_sum[n], small_d[n], small_m[n], small_v[n]

    def shaped(d):
        return [d[n].reshape(w[n].shape) for n in WEIGHT_ORDER]

    return (loss, grad_x[None], *shaped(out_g), *shaped(out_d), *shaped(out_m), *shaped(out_v))
```

```python
import math
from typing import Callable, NamedTuple

import jax
import jax.numpy as jnp
import numpy as np
from jax import lax
from jax.experimental import pallas as pl
from jax.experimental.pallas import tpu as pltpu

F32 = jnp.float32
BF16 = jnp.bfloat16

D_MODEL = 1024
HEAD_DIM = 64
Q_WIDTH = 512
KV_WIDTH = 128
CONV_WIDTH = 512
CONV_K = 3
D_FF = 4096
IN_COLS = 2304
QBLOCK = 128
ROT_DIM = 16
ROPE_THETA = 500000.0
NORM_EPS = 1e-6
NEG_INF = -1e30
N_CHIPS = 4

ADAM_LR = 0.001
ADAM_B1 = 0.9
ADAM_B2 = 0.999
ADAM_EPS = 1e-08
ADAM_WD = 0.01
ADAM_STEP = 10

H_UP, H_DOWN, H_OUT, H_IN = 512, 512, 128, 288

TOKEN_TILE = 512
MLP_BWD_TOKEN_TILE = 512
MLP_BWD_SUB_TILE = 256
ATTN_FWD_BLOCKS = 4
ATTN_BWD_BLOCKS = 2
WGRAD_TOKEN_TILE = 2048
VMEM_LIMIT_V7X = 56 * 1024 * 1024

MESH = pl.DeviceIdType.MESH
ANY = pl.BlockSpec(memory_space=pl.ANY)
VMEM_WHOLE = pl.BlockSpec(memory_space=pltpu.VMEM)
SDS = jax.ShapeDtypeStruct


def _resident(shape):
    zeros = (0,) * len(shape)
    return pl.BlockSpec(shape, lambda *_: zeros, pipeline_mode=pl.Buffered(1))


def _rms(v):
    return lax.rsqrt(jnp.mean(v * v, axis=-1, keepdims=True) + NORM_EPS)


def _norm_bwd(dy, gain, vhat, rstd):
    t = dy * gain
    return rstd * (t - vhat * jnp.mean(t * vhat, axis=-1, keepdims=True))


def _colsum(v):
    return jnp.sum(v, axis=0, keepdims=True)


def _dot_nt(a, b):
    return lax.dot_general(a, b, (((1,), (1,)), ((), ())), preferred_element_type=F32)


def _dot_tn(a, b):
    return lax.dot_general(a, b, (((0,), (0,)), ((), ())), preferred_element_type=F32)


def _dot(a, b):
    return jnp.dot(a, b, preferred_element_type=F32)


def _chip_block(w_ref, chip):
    both = w_ref[pl.ds(2 * chip, 2)]
    return both.reshape(2 * both.shape[1], both.shape[2])


def _lane_lt64(shape):
    return lax.broadcasted_iota(jnp.int32, shape, 1) < HEAD_DIM


class _Comm(NamedTuple):
    operands: tuple
    out_shapes: tuple
    aliases: dict
    n_remote: int
    n_local: int
    plan: Callable


def _merge(*comms):
    operands, out_shapes, aliases, parts = [], [], {}, []
    n_remote = n_local = 0
    for cm in comms:
        parts.append((len(operands), len(out_shapes), n_remote, n_local, cm))
        for k, v in cm.aliases.items():
            aliases[len(operands) + k] = len(out_shapes) + v
        operands += cm.operands
        out_shapes += cm.out_shapes
        n_remote += cm.n_remote
        n_local += cm.n_local

    def plan(ins, outs, send, recv, loc):
        sends, recvs, locs = [], [], []
        for i0, o0, r0, l0, cm in parts:
            s, r, l = cm.plan(ins[i0:i0 + len(cm.operands)], outs[o0:o0 + len(cm.out_shapes)],
                              lambda k, r0=r0: send(r0 + k), lambda k, r0=r0: recv(r0 + k), lambda k, l0=l0: loc(l0 + k))
            sends, recvs, locs = sends + s, recvs + r, locs + l
        return sends, recvs, locs

    return _Comm(tuple(operands), tuple(out_shapes), aliases, n_remote, n_local, plan)


def _sem_scratch(comm):
    return [pltpu.SemaphoreType.DMA((max(comm.n_remote, 1),)), pltpu.SemaphoreType.DMA((max(comm.n_remote, 1),)),
            pltpu.SemaphoreType.DMA((max(comm.n_local, 1),))]


class _Rider(NamedTuple):
    body: Callable
    in_specs: list
    out_specs: list
    out_shape: list
    operands: tuple


def _pallas(body, *, name, grid, in_specs, out_specs, out_shape, operands, scratch=(), comm=None, rider=None):
    params = pltpu.CompilerParams(dimension_semantics=("arbitrary",) * len(grid), vmem_limit_bytes=VMEM_LIMIT_V7X)
    if rider is not None:
        own_in, own_out, ride_in, ride_out = len(in_specs), len(out_specs), len(rider.in_specs), len(rider.out_specs)
        own_body = body

        def body(*refs):
            o0 = own_in + ride_in
            s0 = o0 + own_out + ride_out
            own_body(*refs[:own_in], *refs[o0:o0 + own_out], *refs[s0:])
            first = None
            for axis in range(len(grid)):
                at_start = pl.program_id(axis) == 0
                first = at_start if first is None else jnp.logical_and(first, at_start)
            rider.body(first, *refs[own_in:o0], *refs[o0 + own_out:s0])

        in_specs, out_specs = list(in_specs) + rider.in_specs, list(out_specs) + rider.out_specs
        out_shape, operands = list(out_shape) + rider.out_shape, tuple(operands) + tuple(rider.operands)
    if comm is None:
        return pl.pallas_call(body, name=name, grid=grid, in_specs=in_specs, out_specs=out_specs, out_shape=out_shape,
                              scratch_shapes=list(scratch), compiler_params=params)(*operands)
    n_in, n_out, n_scr = len(in_specs), len(out_specs), len(scratch)
    c_in, c_out = len(comm.operands), len(comm.out_shapes)

    def with_comm(*refs):
        ins, c_ins = refs[:n_in], refs[n_in:n_in + c_in]
        o0 = n_in + c_in
        outs, c_outs = refs[o0:o0 + n_out], refs[o0 + n_out:o0 + n_out + c_out]
        s0 = o0 + n_out + c_out
        scr = refs[s0:s0 + n_scr]
        send_sems, recv_sems, local_sems = refs[s0 + n_scr:]
        first = last = None
        for axis, size in enumerate(grid):
            at_start, at_end = pl.program_id(axis) == 0, pl.program_id(axis) == size - 1
            first = at_start if first is None else jnp.logical_and(first, at_start)
            last = at_end if last is None else jnp.logical_and(last, at_end)

        def copies():
            return comm.plan(c_ins, c_outs, lambda k: send_sems.at[k], lambda k: recv_sems.at[k],
                             lambda k: local_sems.at[k])

        @pl.when(first)
        def _():
            sends, _, locs = copies()
            for cp in sends + locs:
                cp.start()

        body(*ins, *outs, *scr)

        @pl.when(last)
        def _():
            sends, recvs, locs = copies()
            for cp in recvs:
                cp.wait_recv()
            for cp in sends:
                cp.wait_send()
            for cp in locs:
                cp.wait()

    return pl.pallas_call(
        with_comm, name=name, grid=grid,
        in_specs=list(in_specs) + [ANY] * c_in, out_specs=list(out_specs) + [ANY] * c_out,
        out_shape=list(out_shape) + list(comm.out_shapes),
        scratch_shapes=list(scratch) + _sem_scratch(comm),
        input_output_aliases={n_in + k: n_out + v for k, v in comm.aliases.items()},
        compiler_params=params)(*operands, *comm.operands)


def _place():
    return lax.axis_index("x"), lax.axis_index("y"), lax.axis_index("c")


def _other_chips(x, y):
    return [(1 - x, y), (x, 1 - y), (1 - x, 1 - y)]


def _slot(px, py, pc):
    return 4 * px + 2 * py + pc


def _remote(src, dst, send_sem, recv_sem, to):
    return pltpu.make_async_remote_copy(src_ref=src, dst_ref=dst, send_sem=send_sem, recv_sem=recv_sem,
                                        device_id=to, device_id_type=MESH)


def _gather_first(half_block):
    def plan(ins, outs, send, recv, loc):
        (blk,), (full,) = ins, outs
        x, y, c = _place()
        chips = _other_chips(x, y)
        mine = full.at[_slot(x, y, c)]
        sends = [_remote(blk, mine, send(0), recv(0), (x, y, 1 - c))]
        sends += [_remote(blk, mine, send(1 + j), recv(1 + j), (*chip, c)) for j, chip in enumerate(chips)]
        recvs = [_remote(blk, full.at[_slot(x, y, 1 - c)], send(0), recv(0), (x, y, 1 - c))]
        recvs += [_remote(blk, full.at[_slot(*chip, c)], send(1 + j), recv(1 + j), (*chip, c))
                  for j, chip in enumerate(chips)]
        return sends, recvs, [pltpu.make_async_copy(blk, mine, loc(0))]

    return _Comm((half_block,), (SDS((2 * N_CHIPS,) + half_block.shape, half_block.dtype),), {}, 4, 1, plan)


def _gather_second(partly_gathered):
    def plan(ins, outs, send, recv, loc):
        (src,), (full,) = ins, outs
        x, y, c = _place()
        chips = _other_chips(x, y)
        sends = [_remote(src.at[_slot(*chip, c)], full.at[_slot(*chip, c)], send(j), recv(j), (x, y, 1 - c))
                 for j, chip in enumerate(chips)]
        recvs = [_remote(src.at[_slot(*chip, 1 - c)], full.at[_slot(*chip, 1 - c)], send(j), recv(j), (x, y, 1 - c))
                 for j, chip in enumerate(chips)]
        return sends, recvs, []

    return _Comm((partly_gathered,), (SDS(partly_gathered.shape, partly_gathered.dtype),), {0: 0}, 3, 0, plan)


def _gather_whole(half_block, small_block):
    def body(blk_ref, small_ref, out_ref, small_out_ref, send_sems, recv_sems, local_sems):
        x, y, c = _place()
        me, sibling = (x, y, c), (x, y, 1 - c)
        chips = _other_chips(x, y)

        def copy(k, block, to, src=None):
            return _remote(out_ref.at[_slot(*block)] if src is None else src, out_ref.at[_slot(*block)],
                           send_sems.at[k], recv_sems.at[k], to)

        def small_copy(k, chip, to):
            return _remote(small_ref, small_out_ref.at[2 * chip[0] + chip[1]], send_sems.at[7 + k], recv_sems.at[7 + k], to)

        mine = pltpu.make_async_copy(blk_ref, out_ref.at[_slot(*me)], local_sems.at[0])
        mine_small = pltpu.make_async_copy(small_ref, small_out_ref.at[2 * x + y], local_sems.at[1])
        mine.start()
        mine_small.start()
        first = [copy(0, me, sibling, src=blk_ref)]
        first += [copy(1 + j, me, (*chip, c), src=blk_ref) for j, chip in enumerate(chips)]
        first += [small_copy(j, (x, y), (*chip, c)) for j, chip in enumerate(chips)]
        for cp in first:
            cp.start()
        passed = [copy(4 + j, (*chip, c), sibling) for j, chip in enumerate(chips)]
        for j, chip in enumerate(chips):
            copy(1 + j, (*chip, c), me).wait_recv()
            passed[j].start()
        copy(0, sibling, me).wait_recv()
        for j, chip in enumerate(chips):
            copy(4 + j, (*chip, 1 - c), me).wait_recv()
            small_copy(j, chip, me).wait_recv()
        for cp in first + passed:
            cp.wait_send()
        mine.wait()
        mine_small.wait()

    return pl.pallas_call(
        body, name="gather_whole", in_specs=[ANY, ANY], out_specs=[ANY, ANY],
        out_shape=[SDS((2 * N_CHIPS,) + half_block.shape, half_block.dtype),
                   SDS((N_CHIPS,) + small_block.shape, small_block.dtype)],
        scratch_shapes=[pltpu.SemaphoreType.DMA((10,)), pltpu.SemaphoreType.DMA((10,)), pltpu.SemaphoreType.DMA((2,))],
    )(half_block, small_block)


def _pair_send(grads):
    def plan(ins, outs, send, recv, loc):
        (g,), (got,) = ins, outs
        x, y, c = _place()
        copies = [_remote(g.at[j, 1 - c], got.at[j], send(j), recv(j), (x, y, 1 - c)) for j in range(N_CHIPS)]
        return copies, copies, []

    shape = (grads.shape[0],) + grads.shape[2:]
    return _Comm((grads,), (SDS(shape, grads.dtype),), {}, N_CHIPS, 0, plan)


def _chip_exchange(partial):
    def plan(ins, outs, send, recv, loc):
        (p,), (got,) = ins, outs
        x, y, c = _place()
        my_chip = 2 * x + y
        chips = _other_chips(x, y)
        sends = [_remote(p.at[2 * chip[0] + chip[1]], got.at[my_chip], send(j), recv(j), (*chip, c))
                 for j, chip in enumerate(chips)]
        recvs = [_remote(p.at[my_chip], got.at[2 * chip[0] + chip[1]], send(j), recv(j), (*chip, c))
                 for j, chip in enumerate(chips)]
        return sends, recvs, [pltpu.make_async_copy(p.at[my_chip], got.at[my_chip], loc(0))]

    return _Comm((partial,), (SDS(partial.shape, partial.dtype),), {}, 3, 1, plan)


def _pair_sum(grads, received, grid):
    n_chips, _, h, _ = grads.shape
    steps = math.prod(grid)
    per_chip = max(steps // n_chips, 1)
    chips, rows = max(n_chips // steps, 1), h // per_chip

    def where(*ids):
        t = 0
        for i, size in zip(ids, grid):
            t = t * size + i
        return t // per_chip, t % per_chip

    def body(first, g_ref, r_ref, o_ref):
        mine = jnp.where(lax.axis_index("c") == 0, g_ref[:, 0], g_ref[:, 1])
        o_ref[...] = (mine + r_ref[...]).astype(BF16)

    block = pl.BlockSpec((chips, rows, D_MODEL), lambda *ids: (*where(*ids), 0))
    return _Rider(
        body,
        in_specs=[pl.BlockSpec((chips, 2, rows, D_MODEL), lambda *ids: (where(*ids)[0], 0, where(*ids)[1], 0)), block],
        out_specs=[block], out_shape=[SDS((n_chips, h, D_MODEL), BF16)], operands=(grads, received))


SMALL_ROWS = 8


def _sum_blocks(ref):
    return (ref[0].astype(F32) + ref[1].astype(F32)) + (ref[2].astype(F32) + ref[3].astype(F32))


def _tail_reduce(last_grads, exchanged, small):
    n = len(exchanged)
    h = last_grads.shape[2]

    def body(*refs):
        g_ref, ex, small_ref = refs[0], refs[1:1 + n], refs[1 + n]
        o0 = 2 + n
        out, out_last, small_out = refs[o0:o0 + n], refs[o0 + n], refs[o0 + n + 1]
        s0 = o0 + n + 2
        halves, half_last = refs[s0:s0 + n], refs[s0 + n]
        own, got, part, exch, small_buf = refs[s0 + n + 1:s0 + n + 6]
        pair_send, pair_recv, chip_send, chip_recv, share_send, share_recv, small_send, small_recv, local_sems = refs[s0 + n + 6:]
        x, y, c = _place()
        sibling = (x, y, 1 - c)
        my_chip, me = 2 * x + y, _slot(x, y, c)
        chips = _other_chips(x, y)

        to_sibling = [_remote(g_ref.at[j, 1 - c], got.at[j], pair_send.at[j], pair_recv.at[j], sibling)
                      for j in range(N_CHIPS)]
        load_own = [pltpu.make_async_copy(g_ref.at[j, c], own.at[j], local_sems.at[j]) for j in range(N_CHIPS)]
        for cp in to_sibling + load_own:
            cp.start()

        small_buf[me] = small_ref[...]
        small_copies = []
        for mask in range(1, 8):
            peer = (x ^ (mask >> 2), y ^ ((mask >> 1) & 1), c ^ (mask & 1))
            small_copies.append(_remote(small_ref, small_buf.at[me], small_send.at[mask - 1], small_recv.at[mask - 1], peer))
        for cp in small_copies:
            cp.start()

        def share(k, half_ref, out_ref):
            keep = pltpu.make_async_copy(half_ref, out_ref.at[c], local_sems.at[N_CHIPS + k])
            give = _remote(half_ref, out_ref.at[c], share_send.at[k], share_recv.at[k], sibling)
            take = _remote(half_ref, out_ref.at[1 - c], share_send.at[k], share_recv.at[k], sibling)
            keep.start()
            give.start()
            return keep, give, take

        shares = []
        for k in range(n):
            halves[k][...] = _sum_blocks(ex[k])
            shares.append(share(k, halves[k], out[k]))

        for cp in to_sibling:
            cp.wait_recv()
        for cp in load_own:
            cp.wait()
        part[...] = (own[...] + got[...]).astype(BF16)
        exch[my_chip] = part[my_chip]
        to_chips = [_remote(part.at[2 * chip[0] + chip[1]], exch.at[my_chip], chip_send.at[j], chip_recv.at[j], (*chip, c))
                    for j, chip in enumerate(chips)]
        from_chips = [_remote(part.at[my_chip], exch.at[2 * chip[0] + chip[1]], chip_send.at[j], chip_recv.at[j], (*chip, c))
                      for j, chip in enumerate(chips)]
        for cp in to_chips:
            cp.start()

        for cp in small_copies:
            cp.wait_recv()
        total = small_buf[0]
        for d in range(1, 8):
            total = total + small_buf[d]
        small_out[...] = total

        for cp in from_chips:
            cp.wait_recv()
        half_last[...] = _sum_blocks(exch)
        shares.append(share(n, half_last, out_last))

        for keep, give, take in shares:
            take.wait_recv()
            give.wait_send()
            keep.wait()
        for cp in to_sibling + to_chips + small_copies:
            cp.wait_send()

    blocks = (N_CHIPS, h, D_MODEL)
    return pl.pallas_call(
        body, name="tail_reduce",
        in_specs=[ANY] + [VMEM_WHOLE] * (n + 1), out_specs=[ANY] * (n + 1) + [VMEM_WHOLE],
        out_shape=[SDS((2,) + e.shape[1:], F32) for e in exchanged] + [SDS((2, h, D_MODEL), F32), SDS(small.shape, F32)],
        scratch_shapes=[pltpu.VMEM(e.shape[1:], F32) for e in exchanged] + [pltpu.VMEM((h, D_MODEL), F32)]
                       + [pltpu.VMEM(blocks, F32), pltpu.VMEM(blocks, F32), pltpu.VMEM(blocks, BF16), pltpu.VMEM(blocks, BF16),
                          pltpu.VMEM((8,) + small.shape, F32)]
                       + [pltpu.SemaphoreType.DMA((N_CHIPS,)), pltpu.SemaphoreType.DMA((N_CHIPS,)),
                          pltpu.SemaphoreType.DMA((3,)), pltpu.SemaphoreType.DMA((3,)),
                          pltpu.SemaphoreType.DMA((n + 1,)), pltpu.SemaphoreType.DMA((n + 1,)),
                          pltpu.SemaphoreType.DMA((7,)), pltpu.SemaphoreType.DMA((7,)),
                          pltpu.SemaphoreType.DMA((N_CHIPS + n + 1,))],
        compiler_params=pltpu.CompilerParams(vmem_limit_bytes=VMEM_LIMIT_V7X),
    )(last_grads, *exchanged, small)


def _rope_expansion():
    half = ROT_DIM // 2
    expand = np.zeros((2 * half, 3 * 128), np.float32)
    const = np.zeros((1, 3 * 128), np.float32)
    for lane in range(128):
        d = lane % HEAD_DIM
        if d < ROT_DIM:
            expand[d % half, lane] = 1.0
        else:
            const[0, lane] = 1.0
        if d < half:
            expand[half + d, 128 + lane] = -1.0
        elif d < ROT_DIM:
            expand[half + d - half, 256 + lane] = 1.0
    return expand, const


ROPE_PIECES = 3 * ROT_DIM


def _rope_inputs(seq):
    pos = jnp.arange(seq, dtype=F32)
    inv_freq = ROPE_THETA ** (-jnp.arange(0, ROT_DIM, 2, dtype=F32) / ROT_DIM)
    ang = pos[:, None] * inv_freq[None, :]
    cs = jnp.concatenate([jnp.cos(ang), jnp.sin(ang)], axis=1)
    hi = lax.reduce_precision(cs, 8, 7)
    mid = lax.reduce_precision(cs - hi, 8, 7)
    low = cs - hi - mid
    expand, const = _rope_expansion()
    pieces = jnp.concatenate([hi, mid, low], axis=1).astype(BF16)
    return pieces, jnp.asarray(np.concatenate([expand] * 3, axis=0), BF16), jnp.asarray(const)


def _rope_specs(tb):
    return [pl.BlockSpec((tb, ROPE_PIECES), lambda i: (i, 0)), _resident((ROPE_PIECES, 3 * 128)), _resident((1, 3 * 128))]


def _rope_tile(pieces_ref, expand_ref, const_ref):
    tables = _dot(pieces_ref[...], expand_ref[...]) + const_ref[...]
    return tables[:, 0:128], tables[:, 128:256], tables[:, 256:384]


def _rope(t, c, sa, sb):
    half = ROT_DIM // 2
    return t * c + pltpu.roll(t, 128 - half, 1) * sa + pltpu.roll(t, half, 1) * sb


def _rope_transposed(dt, c, sa, sb):
    half = ROT_DIM // 2
    return dt * c + pltpu.roll(dt * sa, half, 1) + pltpu.roll(dt * sb, 128 - half, 1)


def _cast_halves(core, w_up, w_down, w_out, w_in_t):
    def body(core_ref, up_ref, down_ref, out_ref, in_ref, up_o, down_o, out_o, in_o):
        up_o[...] = up_ref[...].astype(BF16)
        down_o[...] = down_ref[...].astype(BF16)
        out_o[...] = out_ref[...].astype(BF16)
        in_o[...] = in_ref[...].astype(BF16)

    half = lambda rows: pl.BlockSpec((rows, D_MODEL), lambda i, core_ref: (core_ref[0], 0))
    whole = lambda rows: pl.BlockSpec((rows, D_MODEL), lambda i, core_ref: (0, 0))
    rows = (H_UP, H_DOWN, H_OUT, H_IN)
    return pl.pallas_call(
        body, name="cast_halves",
        grid_spec=pltpu.PrefetchScalarGridSpec(
            num_scalar_prefetch=1, grid=(1,), in_specs=[half(r) for r in rows], out_specs=[whole(r) for r in rows]),
        out_shape=[SDS((r, D_MODEL), BF16) for r in rows],
        compiler_params=pltpu.CompilerParams(dimension_semantics=("arbitrary",), vmem_limit_bytes=VMEM_LIMIT_V7X),
    )(core, w_up, w_down, w_out, w_in_t)


def _in_proj(x, g_pre, w_in_t, rope, comm=None):
    seq = x.shape[0]
    tb = TOKEN_TILE

    def body(x_ref, g_ref, w_ref, c_ref, sa_ref, sb_ref,
             q_ref, kd0_ref, kd1_ref, vd0_ref, vd1_ref, gb_ref, gc_ref, xin_ref, hn_ref):
        xv = x_ref[...]
        hn = (xv * _rms(xv) * g_ref[...]).astype(BF16)
        hn_ref[...] = hn
        proj = _dot_nt(hn, w_ref[...].reshape(IN_COLS, D_MODEL))
        c, sa, sb = _rope_tile(c_ref, sa_ref, sb_ref)
        scale = 1.0 / math.sqrt(HEAD_DIM)
        for p in range(Q_WIDTH // 128):
            q_ref[:, 128 * p:128 * (p + 1)] = (_rope(proj[:, 128 * p:128 * (p + 1)], c, sa, sb) * scale).astype(BF16)
        k = _rope(proj[:, Q_WIDTH:Q_WIDTH + KV_WIDTH], c, sa, sb)
        v = proj[:, Q_WIDTH + KV_WIDTH:Q_WIDTH + 2 * KV_WIDTH]
        low = _lane_lt64(k.shape)
        k_sw, v_sw = pltpu.roll(k, HEAD_DIM, 1), pltpu.roll(v, HEAD_DIM, 1)
        kd0_ref[...] = jnp.where(low, k, k_sw).astype(BF16)
        kd1_ref[...] = jnp.where(low, k_sw, k).astype(BF16)
        vd0_ref[...] = jnp.where(low, v, v_sw).astype(BF16)
        vd1_ref[...] = jnp.where(low, v_sw, v).astype(BF16)
        base = Q_WIDTH + 2 * KV_WIDTH
        gb_ref[...] = proj[:, base:base + CONV_WIDTH].astype(BF16)
        gc_ref[...] = proj[:, base + CONV_WIDTH:base + 2 * CONV_WIDTH].astype(BF16)
        xin_ref[...] = proj[:, base + 2 * CONV_WIDTH:base + 3 * CONV_WIDTH].astype(BF16)

    tile = lambda w: pl.BlockSpec((tb, w), lambda i: (i, 0))
    return _pallas(
        body, name="in_proj", grid=(seq // tb,),
        in_specs=[tile(D_MODEL), _resident((1, D_MODEL)), _resident(w_in_t.shape), *_rope_specs(tb)],
        out_specs=[tile(Q_WIDTH), tile(128), tile(128), tile(128), tile(128),
                   tile(CONV_WIDTH), tile(CONV_WIDTH), tile(CONV_WIDTH), tile(D_MODEL)],
        out_shape=[SDS((seq, Q_WIDTH), BF16)] + [SDS((seq, 128), BF16)] * 4
                  + [SDS((seq, CONV_WIDTH), BF16)] * 3 + [SDS((seq, D_MODEL), BF16)],
        operands=(x, g_pre, w_in_t, *rope), comm=comm)


def _attn_valid(i):
    shape = (4 * QBLOCK, 2 * QBLOCK)
    row = lax.broadcasted_iota(jnp.int32, shape, 0)
    col = lax.broadcasted_iota(jnp.int32, shape, 1)
    qi = row & (QBLOCK - 1)
    return (col > qi) & (col <= qi + QBLOCK) & ((col >= QBLOCK) | (i > 0))


def _stack_heads(pair0, pair1):
    low = _lane_lt64(pair0.shape)
    zero = jnp.zeros_like(pair0)
    return jnp.concatenate([jnp.where(low, pair0, zero), jnp.where(low, zero, pair0),
                            jnp.where(low, pair1, zero), jnp.where(low, zero, pair1)], axis=0)


def _unstack_heads(stacked):
    low = _lane_lt64((QBLOCK, 128))
    pair0 = jnp.where(low, stacked[0:QBLOCK], stacked[QBLOCK:2 * QBLOCK])
    pair1 = jnp.where(low, stacked[2 * QBLOCK:3 * QBLOCK], stacked[3 * QBLOCK:4 * QBLOCK])
    return pair0, pair1


def _sink_column(sink_ref, kv_head):
    row = lax.broadcasted_iota(jnp.int32, (4 * QBLOCK, 1), 0)
    s = [sink_ref[0, 4 * kv_head + j] for j in range(4)]
    return jnp.where(row < QBLOCK, s[0], jnp.where(row < 2 * QBLOCK, s[1], jnp.where(row < 3 * QBLOCK, s[2], s[3])))


def _band(ref, i):
    prev = pl.multiple_of(jnp.maximum(i - 1, 0) * QBLOCK, QBLOCK)
    own = pl.multiple_of(i * QBLOCK, QBLOCK)
    return jnp.concatenate([ref[pl.ds(prev, QBLOCK), :], ref[pl.ds(own, QBLOCK), :]], axis=0), prev, own


def _softmax_with_sink(s, sink_col):
    m = jnp.maximum(jnp.max(s, axis=-1, keepdims=True), sink_col)
    p = jnp.exp(s - m)
    e_sink = jnp.exp(sink_col - m)
    inv_l = 1.0 / (jnp.sum(p, axis=-1, keepdims=True) + e_sink)
    return p, e_sink, inv_l


def _attention_fwd(q, kd0, kd1, vd0, vd1, sinks, comm=None):
    seq = q.shape[0]

    nb = ATTN_FWD_BLOCKS

    def body(sink_ref, q_ref, kd0_ref, kd1_ref, vd0_ref, vd1_ref, o_ref):
        for b in range(nb):
            i = pl.program_id(0) * nb + b
            rows = slice(QBLOCK * b, QBLOCK * (b + 1))
            valid = _attn_valid(i)
            for kv_head, (k_ref, v_ref) in enumerate(((kd0_ref, vd0_ref), (kd1_ref, vd1_ref))):
                kband, _, _ = _band(k_ref, i)
                vband, _, _ = _band(v_ref, i)
                base = 256 * kv_head
                qm = _stack_heads(q_ref[rows, base:base + 128], q_ref[rows, base + 128:base + 256])
                s = jnp.where(valid, _dot_nt(qm, kband), NEG_INF)
                p, _, inv_l = _softmax_with_sink(s, _sink_column(sink_ref, kv_head))
                o = _dot(p.astype(BF16), vband) * inv_l
                pair0, pair1 = _unstack_heads(o)
                o_ref[rows, base:base + 128] = pair0.astype(BF16)
                o_ref[rows, base + 128:base + 256] = pair1.astype(BF16)

    blk = pl.BlockSpec((nb * QBLOCK, Q_WIDTH), lambda i: (i, 0))
    full = _resident((seq, 128))
    return _pallas(
        body, name="attention_fwd", grid=(seq // (nb * QBLOCK),),
        in_specs=[pl.BlockSpec(memory_space=pltpu.SMEM), blk, full, full, full, full],
        out_specs=[blk], out_shape=[SDS((seq, Q_WIDTH), BF16)],
        operands=(sinks, q, kd0, kd1, vd0, vd1), comm=comm)


HALO = 16


def _conv_parts(gc, xin, gc_halo, xin_halo, conv_w, first):
    tb = gc.shape[0]
    u = gc.astype(F32) * xin.astype(F32)
    u_halo = jnp.where(first, 0.0, gc_halo.astype(F32) * xin_halo.astype(F32))
    ext = jnp.concatenate([u_halo, u], axis=0)
    u1 = pltpu.roll(ext, 1, 0)[HALO:HALO + tb]
    u2 = pltpu.roll(ext, 2, 0)[HALO:HALO + tb]
    y = conv_w[0:1, :] * u2 + conv_w[1:2, :] * u1 + conv_w[2:3, :] * u
    return u, u1, u2, y


def _halo_prev(tb, w):
    return pl.BlockSpec((HALO, w), lambda i: (jnp.maximum(i * (tb // HALO) - 1, 0), 0))


def _residual_mid(x, mix, g_post_mix):
    mix_f = mix.astype(F32)
    return x + mix_f * _rms(mix_f) * g_post_mix


def _mix_out(attn, gb, gc, xin, conv_w, g_attn, g_conv, w_out, comm=None):
    seq = attn.shape[0]
    tb = TOKEN_TILE

    def body(a_ref, gb_ref, gc_ref, xin_ref, gch_ref, xinh_ref, cw_ref, ga_ref, gcn_ref, w_ref, mix_ref, mixed_ref):
        first = pl.program_id(0) == 0
        _, _, _, y = _conv_parts(gc_ref[...], xin_ref[...], gch_ref[...], xinh_ref[...], cw_ref[...], first)
        conv = gb_ref[...].astype(F32) * y
        a = a_ref[...].astype(F32)
        mixed_ref[:, 0:Q_WIDTH] = (a * _rms(a) * ga_ref[...]).astype(BF16)
        mixed_ref[:, Q_WIDTH:] = (conv * _rms(conv) * gcn_ref[...]).astype(BF16)
        mix_ref[...] = _dot(mixed_ref[...], w_ref[...].reshape(D_MODEL, D_MODEL)).astype(BF16)

    tile = lambda w: pl.BlockSpec((tb, w), lambda i: (i, 0))
    return _pallas(
        body, name="mix_out", grid=(seq // tb,),
        in_specs=[tile(Q_WIDTH), tile(CONV_WIDTH), tile(CONV_WIDTH), tile(CONV_WIDTH),
                  _halo_prev(tb, CONV_WIDTH), _halo_prev(tb, CONV_WIDTH),
                  _resident((CONV_K, CONV_WIDTH)), _resident((1, Q_WIDTH)), _resident((1, CONV_WIDTH)),
                  _resident(w_out.shape)],
        out_specs=[tile(D_MODEL), tile(D_MODEL)],
        out_shape=[SDS((seq, D_MODEL), BF16), SDS((seq, D_MODEL), BF16)],
        operands=(attn, gb, gc, xin, gc, xin, conv_w, g_attn, g_conv, w_out), comm=comm)


def _mlp_loss(x, mix, target, g_post_mix, g_pre_mlp, g_post_mlp, w_up, w_down):
    seq = x.shape[0]
    tb = TOKEN_TILE

    def body(x_ref, mix_ref, t_ref, gpm_ref, g2_ref, g4_ref, wup_ref, wdown_ref,
             up_ref, hn2_ref, dout_ref, dmlp_ref, loss_ref, dg4_ref, act_ref):
        @pl.when(pl.program_id(0) == 0)
        def _():
            loss_ref[...] = jnp.zeros_like(loss_ref)
            dg4_ref[...] = jnp.zeros_like(dg4_ref)

        halves = [slice(0, tb // 2), slice(tb // 2, tb)]
        hv, hn2 = [], []
        for rows in halves:
            hv.append(_residual_mid(x_ref[rows, :], mix_ref[rows, :], gpm_ref[...]))
            hn2.append((hv[-1] * _rms(hv[-1]) * g2_ref[...]).astype(BF16))
            hn2_ref[rows, :] = hn2[-1]
        for k, rows in enumerate(halves):
            for j in range(N_CHIPS):
                up = _dot(hn2[k], _chip_block(wup_ref, j))
                up = jnp.maximum(up, 0.0)
                up_ref[rows, 1024 * j:1024 * (j + 1)] = up.astype(BF16)
                act_ref[rows, 1024 * j:1024 * (j + 1)] = (up * up).astype(BF16)
        w_down_all = wdown_ref[...].reshape(D_FF, D_MODEL)
        loss = jnp.zeros((1, 1), F32)
        dg4 = jnp.zeros((1, D_MODEL), F32)
        for k, rows in enumerate(halves):
            mlp = _dot(act_ref[rows, :], w_down_all)
            rstd = _rms(mlp)
            zhat = mlp * rstd
            diff = hv[k] + zhat * g4_ref[...] - t_ref[rows, :]
            loss = loss + jnp.sum(jnp.sum(diff * diff, axis=1, keepdims=True), axis=0, keepdims=True)
            dout = diff * (1.0 / D_MODEL)
            dout_ref[rows, :] = dout
            dg4 = dg4 + _colsum(dout * zhat)
            dmlp_ref[rows, :] = _norm_bwd(dout, g4_ref[...], zhat, rstd).astype(BF16)
        loss_ref[...] += loss
        dg4_ref[...] += dg4

    tile = lambda w: pl.BlockSpec((tb, w), lambda i: (i, 0))
    return _pallas(
        body, name="mlp_loss", grid=(seq // tb,),
        in_specs=[tile(D_MODEL), tile(D_MODEL), tile(D_MODEL), _resident((1, D_MODEL)), _resident((1, D_MODEL)),
                  _resident((1, D_MODEL)), _resident(w_up.shape), _resident(w_down.shape)],
        out_specs=[tile(D_FF), tile(D_MODEL), tile(D_MODEL), tile(D_MODEL),
                   pl.BlockSpec((1, 1), lambda i: (0, 0)), pl.BlockSpec((1, D_MODEL), lambda i: (0, 0))],
        out_shape=[SDS((seq, D_FF), BF16), SDS((seq, D_MODEL), BF16), SDS((seq, D_MODEL), F32),
                   SDS((seq, D_MODEL), BF16), SDS((1, 1), F32), SDS((1, D_MODEL), F32)],
        scratch=[pltpu.VMEM((tb, D_FF), BF16)],
        operands=(x, mix, target, g_post_mix, g_pre_mlp, g_post_mlp, w_up, w_down))


def _mlp_bwd(dmlp, up, x, dout, mix, g_pre_mlp, g_post_mix, w_up, w_down):
    seq = x.shape[0]
    tb = MLP_BWD_TOKEN_TILE

    def body(dmlp_ref, up_ref, x_ref, dout_ref, mix_ref, g2_ref, gpm_ref, wup_ref, wdown_ref,
             dup_ref, dh_ref, dmix_ref, dg2_ref, dgpm_ref):
        @pl.when(pl.program_id(0) == 0)
        def _():
            dg2_ref[...] = jnp.zeros_like(dg2_ref)
            dgpm_ref[...] = jnp.zeros_like(dgpm_ref)

        subs = [slice(k * MLP_BWD_SUB_TILE, (k + 1) * MLP_BWD_SUB_TILE) for k in range(tb // MLP_BWD_SUB_TILE)]
        dhn2 = []
        for rows in subs:
            dmlp_v = dmlp_ref[rows, :]
            acc = None
            for j in range(N_CHIPS):
                cols = slice(1024 * j, 1024 * (j + 1))
                dact = _dot_nt(dmlp_v, _chip_block(wdown_ref, j))
                dup = (dact * (2.0 * up_ref[rows, cols].astype(F32))).astype(BF16)
                dup_ref[rows, cols] = dup
                part = _dot_nt(dup, _chip_block(wup_ref, j))
                acc = part if acc is None else acc + part
            dhn2.append(acc)
        dg2 = jnp.zeros((1, D_MODEL), F32)
        dgpm = jnp.zeros((1, D_MODEL), F32)
        for k, rows in enumerate(subs):
            mix_v = mix_ref[rows, :].astype(F32)
            hv = _residual_mid(x_ref[rows, :], mix_ref[rows, :], gpm_ref[...])
            r2 = _rms(hv)
            hhat = hv * r2
            dg2 = dg2 + _colsum(dhn2[k] * hhat)
            dh = dout_ref[rows, :] + _norm_bwd(dhn2[k], g2_ref[...], hhat, r2)
            dh_ref[rows, :] = dh.astype(BF16)
            rz = _rms(mix_v)
            zhat = mix_v * rz
            dgpm = dgpm + _colsum(dh * zhat)
            dmix_ref[rows, :] = _norm_bwd(dh, gpm_ref[...], zhat, rz).astype(BF16)
        dg2_ref[...] += dg2
        dgpm_ref[...] += dgpm

    tile = lambda w: pl.BlockSpec((tb, w), lambda i: (i, 0))
    vec = pl.BlockSpec((1, D_MODEL), lambda i: (0, 0))
    return _pallas(
        body, name="mlp_bwd", grid=(seq // tb,),
        in_specs=[tile(D_MODEL), tile(D_FF), tile(D_MODEL), tile(D_MODEL), tile(D_MODEL),
                  _resident((1, D_MODEL)), _resident((1, D_MODEL)), _resident(w_up.shape), _resident(w_down.shape)],
        out_specs=[tile(D_FF), tile(D_MODEL), tile(D_MODEL), vec, vec],
        out_shape=[SDS((seq, D_FF), BF16), SDS((seq, D_MODEL), BF16), SDS((seq, D_MODEL), BF16),
                   SDS((1, D_MODEL), F32), SDS((1, D_MODEL), F32)],
        operands=(dmlp, up, x, dout, mix, g_pre_mlp, g_post_mix, w_up, w_down))


def _mix_bwd(dmix, attn, gb, gc, xin, conv_w, g_attn, g_conv, w_out, n_k):
    seq = attn.shape[0]
    tb = seq // (N_CHIPS * n_k)

    def body(first, dmix_ref, a_ref, gb_ref, gc_ref, xin_ref, gch_ref, xinh_ref, cw_ref, ga_ref, gcn_ref, w_ref,
             dattn_ref, dgb_ref, dy_ref, dga_ref, dgcn_ref, dcw_ref):
        @pl.when(first)
        def _():
            dga_ref[...] = jnp.zeros_like(dga_ref)
            dgcn_ref[...] = jnp.zeros_like(dgcn_ref)
            dcw_ref[...] = jnp.zeros_like(dcw_ref)

        dmixed = _dot_nt(dmix_ref[...], w_ref[...].reshape(D_MODEL, D_MODEL))
        a = a_ref[...].astype(F32)
        ra = _rms(a)
        ahat = a * ra
        dan = dmixed[:, 0:Q_WIDTH]
        dga_ref[...] += _colsum(dan * ahat)
        dattn_ref[...] = _norm_bwd(dan, ga_ref[...], ahat, ra).astype(BF16)
        gbv = gb_ref[...].astype(F32)
        u, u1, u2, y = _conv_parts(gc_ref[...], xin_ref[...], gch_ref[...], xinh_ref[...], cw_ref[...], first)
        conv = gbv * y
        rc = _rms(conv)
        chat = conv * rc
        dcn = dmixed[:, Q_WIDTH:]
        dgcn_ref[...] += _colsum(dcn * chat)
        dconv = _norm_bwd(dcn, gcn_ref[...], chat, rc)
        dgb_ref[...] = (dconv * y).astype(BF16)
        dy = dconv * gbv
        dy_ref[...] = dy.astype(BF16)
        dcw_ref[0:1, :] += _colsum(dy * u2)
        dcw_ref[1:2, :] += _colsum(dy * u1)
        dcw_ref[2:3, :] += _colsum(dy * u)

    tile = lambda w: pl.BlockSpec((tb, w), lambda j, k: (j * n_k + k, 0))
    halo = lambda w: pl.BlockSpec((HALO, w), lambda j, k: (jnp.maximum((j * n_k + k) * (tb // HALO) - 1, 0), 0))
    whole = lambda shape: pl.BlockSpec(shape, lambda j, k: (0,) * len(shape))
    return _Rider(
        body,
        in_specs=[tile(D_MODEL), tile(Q_WIDTH), tile(CONV_WIDTH), tile(CONV_WIDTH), tile(CONV_WIDTH),
                  halo(CONV_WIDTH), halo(CONV_WIDTH),
                  _resident((CONV_K, CONV_WIDTH)), _resident((1, Q_WIDTH)), _resident((1, CONV_WIDTH)),
                  _resident(w_out.shape)],
        out_specs=[tile(Q_WIDTH), tile(CONV_WIDTH), tile(CONV_WIDTH),
                   whole((1, Q_WIDTH)), whole((1, CONV_WIDTH)), whole((CONV_K, CONV_WIDTH))],
        out_shape=[SDS((seq, Q_WIDTH), BF16), SDS((seq, CONV_WIDTH), BF16), SDS((seq, CONV_WIDTH), BF16),
                   SDS((1, Q_WIDTH), F32), SDS((1, CONV_WIDTH), F32), SDS((CONV_K, CONV_WIDTH), F32)],
        operands=(dmix, attn, gb, gc, xin, gc, xin, conv_w, g_attn, g_conv, w_out))


def _attention_bwd(q, dattn, attn, kd0, kd1, vd0, vd1, sinks, comm=None, rider=None):
    seq = q.shape[0]
    nb = ATTN_BWD_BLOCKS

    def body(sink_ref, q_ref, do_ref, o_ref, kd0_ref, kd1_ref, vd0_ref, vd1_ref,
             dq_ref, dk0_ref, dk1_ref, dv0_ref, dv1_ref, dsink_ref):
        @pl.when(pl.program_id(0) == 0)
        def _():
            for r in (dk0_ref, dk1_ref, dv0_ref, dv1_ref, dsink_ref):
                r[...] = jnp.zeros_like(r)

        lane = lax.broadcasted_iota(jnp.int32, (1, 128), 1)
        dsink = jnp.zeros((1, 128), F32)
        for b in range(nb):
            i = pl.program_id(0) * nb + b
            rows = slice(QBLOCK * b, QBLOCK * (b + 1))
            valid = _attn_valid(i)
            for kv_head, (k_ref, v_ref, dk_ref, dv_ref) in enumerate(
                    ((kd0_ref, vd0_ref, dk0_ref, dv0_ref), (kd1_ref, vd1_ref, dk1_ref, dv1_ref))):
                kband, prev, own = _band(k_ref, i)
                vband, _, _ = _band(v_ref, i)
                base = 256 * kv_head
                qm = _stack_heads(q_ref[rows, base:base + 128], q_ref[rows, base + 128:base + 256])
                dom = _stack_heads(do_ref[rows, base:base + 128], do_ref[rows, base + 128:base + 256])
                om = _stack_heads(o_ref[rows, base:base + 128], o_ref[rows, base + 128:base + 256])
                s = jnp.where(valid, _dot_nt(qm, kband), NEG_INF)
                p, e_sink, inv_l = _softmax_with_sink(s, _sink_column(sink_ref, kv_head))
                p = p * inv_l
                delta = jnp.sum(dom.astype(F32) * om.astype(F32), axis=-1, keepdims=True)
                ds = (p * (_dot_nt(dom, vband) - delta)).astype(BF16)
                sink_term = -(e_sink * inv_l) * delta
                for j in range(4):
                    part = jnp.sum(sink_term[QBLOCK * j:QBLOCK * (j + 1)], axis=0, keepdims=True)
                    dsink = dsink + jnp.where(lane == 4 * kv_head + j, part, 0.0)
                pair0, pair1 = _unstack_heads(_dot(ds, kband))
                dq_ref[rows, base:base + 128] = pair0.astype(BF16)
                dq_ref[rows, base + 128:base + 256] = pair1.astype(BF16)
                dkd = _dot_tn(ds, qm)
                dkd = dkd + pltpu.roll(dkd, HEAD_DIM, 1)
                dvd = _dot_tn(p.astype(BF16), dom)
                dvd = dvd + pltpu.roll(dvd, HEAD_DIM, 1)
                dk_ref[pl.ds(prev, QBLOCK), :] += dkd[0:QBLOCK]
                dk_ref[pl.ds(own, QBLOCK), :] += dkd[QBLOCK:]
                dv_ref[pl.ds(prev, QBLOCK), :] += dvd[0:QBLOCK]
                dv_ref[pl.ds(own, QBLOCK), :] += dvd[QBLOCK:]
        dsink_ref[...] += dsink

    blk = pl.BlockSpec((nb * QBLOCK, Q_WIDTH), lambda i: (i, 0))
    full = _resident((seq, 128))
    acc = pl.BlockSpec((seq, 128), lambda i: (0, 0))
    return _pallas(
        body, name="attention_bwd", grid=(seq // (nb * QBLOCK),),
        in_specs=[pl.BlockSpec(memory_space=pltpu.SMEM), blk, blk, blk, full, full, full, full],
        out_specs=[blk, acc, acc, acc, acc, pl.BlockSpec((1, 128), lambda i: (0, 0))],
        out_shape=[SDS((seq, Q_WIDTH), BF16)] + [SDS((seq, 128), F32)] * 4 + [SDS((1, 128), F32)],
        operands=(sinks, q, dattn, attn, kd0, kd1, vd0, vd1), comm=comm, rider=rider)


def _in_proj_bwd(dq, dk0, dk1, dv0, dv1, dgb, dy, gc, xin, conv_w, x, dh, g_pre, w_in_t, rope, rider=None):
    seq = x.shape[0]
    tb = TOKEN_TILE
    n_tiles = seq // tb

    def body(dq_ref, dk0_ref, dk1_ref, dv0_ref, dv1_ref, dgb_ref, dy_ref, dyh_ref, gc_ref, xin_ref, cw_ref,
             x_ref, dh_ref, g_ref, w_ref, c_ref, sa_ref, sb_ref,
             dproj_ref, gx_ref, dg_ref):
        i = pl.program_id(0)

        @pl.when(i == 0)
        def _():
            dg_ref[...] = jnp.zeros_like(dg_ref)

        dy = dy_ref[...].astype(F32)
        ext = jnp.concatenate([dy, jnp.where(i == n_tiles - 1, 0.0, dyh_ref[...].astype(F32))], axis=0)
        dy1 = pltpu.roll(ext, tb + HALO - 1, 0)[0:tb]
        dy2 = pltpu.roll(ext, tb + HALO - 2, 0)[0:tb]
        cw = cw_ref[...]
        du = cw[2:3, :] * dy + cw[1:2, :] * dy1 + cw[0:1, :] * dy2
        scale = 1.0 / math.sqrt(HEAD_DIM)
        base = Q_WIDTH + 2 * KV_WIDTH
        halves = [slice(0, tb // 2), slice(tb // 2, tb)]
        low = _lane_lt64((tb // 2, 128))
        for rows in halves:
            c, sa, sb = _rope_tile(c_ref.at[rows, :], sa_ref, sb_ref)
            for p in range(Q_WIDTH // 128):
                dproj_ref[rows, 128 * p:128 * (p + 1)] = _rope_transposed(
                    dq_ref[rows, 128 * p:128 * (p + 1)].astype(F32) * scale, c, sa, sb).astype(BF16)
            dk = jnp.where(low, dk0_ref[rows, :], dk1_ref[rows, :])
            dproj_ref[rows, Q_WIDTH:Q_WIDTH + KV_WIDTH] = _rope_transposed(dk, c, sa, sb).astype(BF16)
            dproj_ref[rows, Q_WIDTH + KV_WIDTH:base] = jnp.where(low, dv0_ref[rows, :], dv1_ref[rows, :]).astype(BF16)
            dproj_ref[rows, base:base + CONV_WIDTH] = dgb_ref[rows, :]
            dproj_ref[rows, base + CONV_WIDTH:base + 2 * CONV_WIDTH] = (du[rows] * xin_ref[rows, :].astype(F32)).astype(BF16)
            dproj_ref[rows, base + 2 * CONV_WIDTH:] = (du[rows] * gc_ref[rows, :].astype(F32)).astype(BF16)
        w_all = w_ref[...].reshape(IN_COLS, D_MODEL)
        dhn = [_dot(dproj_ref[rows, :], w_all) for rows in halves]
        dg = jnp.zeros((1, D_MODEL), F32)
        for k, rows in enumerate(halves):
            xv = x_ref[rows, :]
            r = _rms(xv)
            xhat = xv * r
            dg = dg + _colsum(dhn[k] * xhat)
            gx_ref[rows, :] = dh_ref[rows, :].astype(F32) + _norm_bwd(dhn[k], g_ref[...], xhat, r)
        dg_ref[...] += dg

    tile = lambda w: pl.BlockSpec((tb, w), lambda i: (i, 0))
    halo_next = pl.BlockSpec((HALO, CONV_WIDTH), lambda i: (jnp.minimum((i + 1) * (tb // HALO), seq // HALO - 1), 0))
    return _pallas(
        body, name="in_proj_bwd", grid=(n_tiles,),
        in_specs=[tile(Q_WIDTH), tile(128), tile(128), tile(128), tile(128), tile(CONV_WIDTH), tile(CONV_WIDTH), halo_next,
                  tile(CONV_WIDTH), tile(CONV_WIDTH), _resident((CONV_K, CONV_WIDTH)),
                  tile(D_MODEL), tile(D_MODEL), _resident((1, D_MODEL)), _resident(w_in_t.shape), *_rope_specs(tb)],
        out_specs=[tile(IN_COLS), tile(D_MODEL), pl.BlockSpec((1, D_MODEL), lambda i: (0, 0))],
        out_shape=[SDS((seq, IN_COLS), BF16), SDS((seq, D_MODEL), F32), SDS((1, D_MODEL), F32)],
        operands=(dq, dk0, dk1, dv0, dv1, dgb, dy, dy, gc, xin, conv_w, x, dh, g_pre, w_in_t, *rope), rider=rider)


def _wgrad_grid(seq, per_chip, h_rows):
    chips_per_step = 1 if per_chip else N_CHIPS
    m = chips_per_step * 2 * h_rows
    bt = min(seq, WGRAD_TOKEN_TILE)
    return chips_per_step, m, bt, seq // bt


def _wgrad(name, a, b, *, per_chip, h_rows, square_a=False, comm=None, rider=None):
    seq = a.shape[0]
    chips_per_step, m, bt, n_k = _wgrad_grid(seq, per_chip, h_rows)
    a_cols = m if per_chip else a.shape[1]
    a_wide = a.shape[1] > a_cols
    b_wide = b.shape[1] > D_MODEL

    def body(a_ref, b_ref, g_ref):
        @pl.when(pl.program_id(1) == 0)
        def _():
            g_ref[...] = jnp.zeros_like(g_ref)

        av = a_ref[...]
        if square_a:
            av = (av.astype(F32) * av.astype(F32)).astype(BF16)
        g_ref[...] += _dot_tn(av, b_ref[...]).reshape(g_ref.shape)

    a_spec = pl.BlockSpec((bt, a_cols), (lambda j, k: (k, j)) if a_wide else (lambda j, k: (k, 0)))
    b_spec = pl.BlockSpec((bt, D_MODEL), (lambda j, k: (k, j)) if b_wide else (lambda j, k: (k, 0)))
    g_spec = pl.BlockSpec((chips_per_step, 2, h_rows, D_MODEL), lambda j, k: (j, 0, 0, 0),
                          pipeline_mode=None if per_chip else pl.Buffered(1))
    return _pallas(
        body, name=name, grid=(N_CHIPS if per_chip else 1, n_k),
        in_specs=[a_spec, b_spec], out_specs=[g_spec], out_shape=[SDS((N_CHIPS, 2, h_rows, D_MODEL), F32)],
        operands=(a, b), comm=comm, rider=rider)


def _adamw_math(w, g, m, v):
    m = ADAM_B1 * m + (1.0 - ADAM_B1) * g
    v = ADAM_B2 * v + (1.0 - ADAM_B2) * (g * g)
    m_hat = m / (1.0 - ADAM_B1 ** ADAM_STEP)
    v_hat = v / (1.0 - ADAM_B2 ** ADAM_STEP)
    delta = -ADAM_LR * (m_hat / (jnp.sqrt(v_hat) + ADAM_EPS) + ADAM_WD * w)
    return delta, m, v


def _adamw_rows(name, reduced, w, m, v, rt):
    per_half = reduced.shape[1] // rt

    def body(r_ref, w_ref, m_ref, v_ref, g_out, d_out, m_out, v_out):
        g = r_ref[0]
        g_out[...] = g
        d_out[...], m_out[...], v_out[...] = _adamw_math(w_ref[...], g, m_ref[...], v_ref[...])

    blk = pl.BlockSpec((rt, D_MODEL), lambda h, r: (h * per_half + r, 0))
    return _pallas(
        body, name=name, grid=(2, per_half),
        in_specs=[pl.BlockSpec((1, rt, D_MODEL), lambda h, r: (h, r, 0)), blk, blk, blk],
        out_specs=[blk, blk, blk, blk], out_shape=[SDS(w.shape, F32)] * 4, operands=(reduced, w, m, v))


def _adamw_small(w, g, m, v):
    def body(w_ref, g_ref, m_ref, v_ref, d_out, m_out, v_out):
        d_out[...], m_out[...], v_out[...] = _adamw_math(w_ref[...], g_ref[...], m_ref[...], v_ref[...])

    return pl.pallas_call(body, name="adamw_small", in_specs=[VMEM_WHOLE] * 4, out_specs=[VMEM_WHOLE] * 3,
                          out_shape=[SDS(w.shape, F32)] * 3)(w, g, m, v)


SMALL_VECTORS = ("pre_mix_norm", "post_mix_norm", "pre_mlp_norm", "post_mlp_norm")
SMALL_NAMES = SMALL_VECTORS + ("attn_group_norm", "conv_group_norm", "conv_w", "attn_sinks")


def _pack_small(p):
    rows = [p[n].reshape(1, D_MODEL) for n in SMALL_VECTORS]
    rows.append(jnp.concatenate([p["attn_group_norm"].reshape(1, -1), p["conv_group_norm"].reshape(1, -1)], axis=1))
    cw = p["conv_w"].reshape(CONV_K, -1)
    rows.append(jnp.pad(cw, ((0, 1), (0, CONV_WIDTH - cw.shape[1]))).reshape(2, D_MODEL))
    last = jnp.concatenate([p["attn_sinks"].reshape(1, 8), p.get("loss_sum", jnp.zeros((1, 1), F32))], axis=1)
    rows.append(jnp.pad(last, ((0, 0), (0, D_MODEL - 9))))
    return jnp.concatenate(rows, axis=0)


def _unpack_small(packed, conv_width):
    out = {n: packed[i:i + 1] for i, n in enumerate(SMALL_VECTORS)}
    out["attn_group_norm"] = packed[4:5, :Q_WIDTH]
    out["conv_group_norm"] = packed[4:5, Q_WIDTH:]
    out["conv_w"] = packed[5:7].reshape(4, CONV_WIDTH)[:CONV_K, :conv_width].reshape(1, CONV_K, conv_width)
    out["attn_sinks"] = packed[7:8, :8]
    out["loss_sum"] = packed[7, 8]
    return out


WEIGHT_ORDER = ("pre_mix_norm", "w_in", "conv_w", "attn_sinks", "attn_group_norm", "conv_group_norm", "w_out",
                "post_mix_norm", "pre_mlp_norm", "w_up", "w_down", "post_mlp_norm")


def kernel(x, pre_mix_norm, w_in, conv_w, attn_sinks, attn_group_norm, conv_group_norm, w_out, post_mix_norm, pre_mlp_norm, w_up, w_down, post_mlp_norm, loss_target, m_pre_mix_norm, m_w_in, m_conv_w, m_attn_sinks, m_attn_group_norm, m_conv_group_norm, m_w_out, m_post_mix_norm, m_pre_mlp_norm, m_w_up, m_w_down, m_post_mlp_norm, v_pre_mix_norm, v_w_in, v_conv_w, v_attn_sinks, v_attn_group_norm, v_conv_group_norm, v_w_out, v_post_mix_norm, v_pre_mlp_norm, v_w_up, v_w_down, v_post_mlp_norm):
    w = dict(pre_mix_norm=pre_mix_norm, w_in=w_in, conv_w=conv_w, attn_sinks=attn_sinks, attn_group_norm=attn_group_norm,
             conv_group_norm=conv_group_norm, w_out=w_out, post_mix_norm=post_mix_norm, pre_mlp_norm=pre_mlp_norm,
             w_up=w_up, w_down=w_down, post_mlp_norm=post_mlp_norm)
    m = dict(pre_mix_norm=m_pre_mix_norm, w_in=m_w_in, conv_w=m_conv_w, attn_sinks=m_attn_sinks,
             attn_group_norm=m_attn_group_norm, conv_group_norm=m_conv_group_norm, w_out=m_w_out,
             post_mix_norm=m_post_mix_norm, pre_mlp_norm=m_pre_mlp_norm, w_up=m_w_up, w_down=m_w_down,
             post_mlp_norm=m_post_mlp_norm)
    v = dict(pre_mix_norm=v_pre_mix_norm, w_in=v_w_in, conv_w=v_conv_w, attn_sinks=v_attn_sinks,
             attn_group_norm=v_attn_group_norm, conv_group_norm=v_conv_group_norm, w_out=v_w_out,
             post_mix_norm=v_post_mix_norm, pre_mlp_norm=v_pre_mlp_norm, w_up=v_w_up, w_down=v_w_down,
             post_mlp_norm=v_post_mlp_norm)
    core = lax.axis_index("c").astype(jnp.int32).reshape(1)
    chip = 2 * lax.axis_index("x") + lax.axis_index("y")
    local_conv = conv_w.shape[2]
    xs, target = x[0], loss_target[0]
    rope = _rope_inputs(xs.shape[0])

    hb_up, hb_down, hb_out, hb_in = _cast_halves(core, w_up[0], w_down[0], w_out[0], w_in[0].T)
    conv_pad = jnp.pad(conv_w[0], ((0, 8 - CONV_K), (0, 0)))
    wf_in, conv_all = _gather_whole(hb_in, conv_pad)
    conv_full = conv_all[:, :CONV_K, :].transpose(1, 0, 2).reshape(CONV_K, CONV_WIDTH)

    *proj, wf_up, wf_out = _in_proj(xs, pre_mix_norm, wf_in, rope, comm=_merge(_gather_first(hb_up), _gather_first(hb_out)))
    q, kd0, kd1, vd0, vd1, gb, gc, xin, hn = proj
    attn, wf_up, wf_out, wf_down = _attention_fwd(
        q, kd0, kd1, vd0, vd1, attn_sinks,
        comm=_merge(_gather_second(wf_up), _gather_second(wf_out), _gather_first(hb_down)))
    mix, mixed, wf_down = _mix_out(attn, gb, gc, xin, conv_full, attn_group_norm, conv_group_norm, wf_out,
                                   comm=_gather_second(wf_down))
    up, hn2, dout, dmlp, loss_sum, dg_post_mlp = _mlp_loss(xs, mix, target, post_mix_norm, pre_mlp_norm, post_mlp_norm,
                                                           wf_up, wf_down)

    dup, dh, dmix, dg_pre_mlp, dg_post_mix = _mlp_bwd(dmlp, up, xs, dout, mix, pre_mlp_norm, post_mix_norm, wf_up, wf_down)
    n_k = _wgrad_grid(xs.shape[0], True, H_DOWN)[3]
    g_down, dattn, dgb, dy, dg_attn, dg_conv, dconv_w = _wgrad(
        "wgrad_down", up, dmlp, per_chip=True, h_rows=H_DOWN, square_a=True,
        rider=_mix_bwd(dmix, attn, gb, gc, xin, conv_full, attn_group_norm, conv_group_norm, wf_out, n_k))
    seq = xs.shape[0]
    g_up, got_down = _wgrad("wgrad_up", hn2, dup, per_chip=True, h_rows=H_UP, comm=_pair_send(g_down))
    g_out, p_down, got_up = _wgrad("wgrad_out", mixed, dmix, per_chip=False, h_rows=H_OUT, comm=_pair_send(g_up),
                                   rider=_pair_sum(g_down, got_down, (1, _wgrad_grid(seq, False, H_OUT)[3])))
    dq, dk0, dk1, dv0, dv1, dsink, p_up, ex_down, got_out = _attention_bwd(
        q, dattn, attn, kd0, kd1, vd0, vd1, attn_sinks, comm=_merge(_chip_exchange(p_down), _pair_send(g_out)),
        rider=_pair_sum(g_up, got_up, (seq // (ATTN_BWD_BLOCKS * QBLOCK),)))
    dproj, grad_x, dg_pre_mix, p_out = _in_proj_bwd(
        dq, dk0, dk1, dv0, dv1, dgb, dy, gc, xin, conv_full, xs, dh, pre_mix_norm, wf_in, rope,
        rider=_pair_sum(g_out, got_out, (seq // TOKEN_TILE,)))
    g_in, ex_up, ex_out = _wgrad("wgrad_in", dproj, hn, per_chip=False, h_rows=H_IN,
                                 comm=_merge(_chip_exchange(p_up), _chip_exchange(p_out)))
    small = dict(pre_mix_norm=dg_pre_mix, conv_w=dconv_w, attn_sinks=dsink[:, :8], attn_group_norm=dg_attn,
                 conv_group_norm=dg_conv, post_mix_norm=dg_post_mix, pre_mlp_norm=dg_pre_mlp, post_mlp_norm=dg_post_mlp,
                 loss_sum=loss_sum)
    r_down, r_up, r_out, r_in, small_total = _tail_reduce(g_in, [ex_down, ex_up, ex_out], _pack_small(small))

    out_g, out_d, out_m, out_v = {}, {}, {}, {}
    out_g["w_up"], out_d["w_up"], out_m["w_up"], out_v["w_up"] = _adamw_rows(
        "adamw_up", r_up, w_up[0], m_w_up[0], v_w_up[0], 256)
    out_g["w_down"], out_d["w_down"], out_m["w_down"], out_v["w_down"] = _adamw_rows(
        "adamw_down", r_down, w_down[0], m_w_down[0], v_w_down[0], 256)
    out_g["w_out"], out_d["w_out"], out_m["w_out"], out_v["w_out"] = _adamw_rows(
        "adamw_out", r_out, w_out[0], m_w_out[0], v_w_out[0], H_OUT)
    in_t = _adamw_rows("adamw_in", r_in, w_in[0].T, m_w_in[0].T, v_w_in[0].T, H_IN)
    out_g["w_in"], out_d["w_in"], out_m["w_in"], out_v["w_in"] = [t.T for t in in_t]

    small_sum = _unpack_small(small_total, CONV_WIDTH)
    loss = small_sum["loss_sum"] * (0.5 / D_MODEL)
    small_sum["conv_w"] = lax.dynamic_slice_in_dim(small_sum["conv_w"], chip * local_conv, local_conv, axis=2)
    packed = [_pack_small({n: t[n] for n in SMALL_NAMES}) for t in (w, small_sum, m, v)]
    small_d, small_m, small_v = [_unpack_small(t, local_conv) for t in _adamw_small(*packed)]
    for n in SMALL_NAMES:
        out_g[n], out_d[n], out_m[n], out_v[n] = small_sum[n], small_d[n], small_m[n], small_v[n]

    def shaped(d):
        return [d[n].reshape(w[n].shape) for n in WEIGHT_ORDER]

    return (loss, grad_x[None], *shaped(out_g), *shaped(out_d), *shaped(out_m), *shaped(out_v))
```

```python
import math
from typing import Callable, NamedTuple

import jax
import jax.numpy as jnp
import numpy as np
from jax import lax
from jax.experimental import pallas as pl
from jax.experimental.pallas import tpu as pltpu

F32 = jnp.float32
BF16 = jnp.bfloat16

D_MODEL = 1024
HEAD_DIM = 64
Q_WIDTH = 512
KV_WIDTH = 128
CONV_WIDTH = 512
CONV_K = 3
D_FF = 4096
IN_COLS = 2304
QBLOCK = 128
ROT_DIM = 16
ROPE_THETA = 500000.0
NORM_EPS = 1e-6
NEG_INF = -1e30
N_CHIPS = 4

ADAM_LR = 0.001
ADAM_B1 = 0.9
ADAM_B2 = 0.999
ADAM_EPS = 1e-08
ADAM_WD = 0.01
ADAM_STEP = 10

H_UP, H_DOWN, H_OUT, H_IN = 512, 512, 128, 288

TOKEN_TILE = 512
MLP_BWD_TOKEN_TILE = 512
MLP_BWD_SUB_TILE = 256
ATTN_FWD_BLOCKS = 4
ATTN_BWD_BLOCKS = 2
WGRAD_TOKEN_TILE = 2048
VMEM_LIMIT_V7X = 56 * 1024 * 1024

MESH = pl.DeviceIdType.MESH
ANY = pl.BlockSpec(memory_space=pl.ANY)
VMEM_WHOLE = pl.BlockSpec(memory_space=pltpu.VMEM)
SDS = jax.ShapeDtypeStruct


def _resident(shape):
    zeros = (0,) * len(shape)
    return pl.BlockSpec(shape, lambda *_: zeros, pipeline_mode=pl.Buffered(1))


def _rms(v):
    return lax.rsqrt(jnp.mean(v * v, axis=-1, keepdims=True) + NORM_EPS)


def _norm_bwd(dy, gain, vhat, rstd):
    t = dy * gain
    return rstd * (t - vhat * jnp.mean(t * vhat, axis=-1, keepdims=True))


def _colsum(v):
    return jnp.sum(v, axis=0, keepdims=True)


def _dot_nt(a, b):
    return lax.dot_general(a, b, (((1,), (1,)), ((), ())), preferred_element_type=F32)


def _dot_tn(a, b):
    return lax.dot_general(a, b, (((0,), (0,)), ((), ())), preferred_element_type=F32)


def _dot(a, b):
    return jnp.dot(a, b, preferred_element_type=F32)


def _chip_block(w_ref, chip):
    both = w_ref[pl.ds(2 * chip, 2)]
    return both.reshape(2 * both.shape[1], both.shape[2])


def _lane_lt64(shape):
    return lax.broadcasted_iota(jnp.int32, shape, 1) < HEAD_DIM


class _Comm(NamedTuple):
    operands: tuple
    out_shapes: tuple
    aliases: dict
    n_remote: int
    n_local: int
    plan: Callable


def _merge(*comms):
    operands, out_shapes, aliases, parts = [], [], {}, []
    n_remote = n_local = 0
    for cm in comms:
        parts.append((len(operands), len(out_shapes), n_remote, n_local, cm))
        for k, v in cm.aliases.items():
            aliases[len(operands) + k] = len(out_shapes) + v
        operands += cm.operands
        out_shapes += cm.out_shapes
        n_remote += cm.n_remote
        n_local += cm.n_local

    def plan(ins, outs, send, recv, loc):
        sends, recvs, locs = [], [], []
        for i0, o0, r0, l0, cm in parts:
            s, r, l = cm.plan(ins[i0:i0 + len(cm.operands)], outs[o0:o0 + len(cm.out_shapes)],
                              lambda k, r0=r0: send(r0 + k), lambda k, r0=r0: recv(r0 + k), lambda k, l0=l0: loc(l0 + k))
            sends, recvs, locs = sends + s, recvs + r, locs + l
        return sends, recvs, locs

    return _Comm(tuple(operands), tuple(out_shapes), aliases, n_remote, n_local, plan)


def _sem_scratch(comm):
    return [pltpu.SemaphoreType.DMA((max(comm.n_remote, 1),)), pltpu.SemaphoreType.DMA((max(comm.n_remote, 1),)),
            pltpu.SemaphoreType.DMA((max(comm.n_local, 1),))]


class _Rider(NamedTuple):
    body: Callable
    in_specs: list
    out_specs: list
    out_shape: list
    operands: tuple


def _pallas(body, *, name, grid, in_specs, out_specs, out_shape, operands, scratch=(), comm=None, rider=None):
    params = pltpu.CompilerParams(dimension_semantics=("arbitrary",) * len(grid), vmem_limit_bytes=VMEM_LIMIT_V7X)
    if rider is not None:
        own_in, own_out, ride_in, ride_out = len(in_specs), len(out_specs), len(rider.in_specs), len(rider.out_specs)
        own_body = body

        def body(*refs):
            o0 = own_in + ride_in
            s0 = o0 + own_out + ride_out
            own_body(*refs[:own_in], *refs[o0:o0 + own_out], *refs[s0:])
            first = None
            for axis in range(len(grid)):
                at_start = pl.program_id(axis) == 0
                first = at_start if first is None else jnp.logical_and(first, at_start)
            rider.body(first, *refs[own_in:o0], *refs[o0 + own_out:s0])

        in_specs, out_specs = list(in_specs) + rider.in_specs, list(out_specs) + rider.out_specs
        out_shape, operands = list(out_shape) + rider.out_shape, tuple(operands) + tuple(rider.operands)
    if comm is None:
        return pl.pallas_call(body, name=name, grid=grid, in_specs=in_specs, out_specs=out_specs, out_shape=out_shape,
                              scratch_shapes=list(scratch), compiler_params=params)(*operands)
    n_in, n_out, n_scr = len(in_specs), len(out_specs), len(scratch)
    c_in, c_out = len(comm.operands), len(comm.out_shapes)

    def with_comm(*refs):
        ins, c_ins = refs[:n_in], refs[n_in:n_in + c_in]
        o0 = n_in + c_in
        outs, c_outs = refs[o0:o0 + n_out], refs[o0 + n_out:o0 + n_out + c_out]
        s0 = o0 + n_out + c_out
        scr = refs[s0:s0 + n_scr]
        send_sems, recv_sems, local_sems = refs[s0 + n_scr:]
        first = last = None
        for axis, size in enumerate(grid):
            at_start, at_end = pl.program_id(axis) == 0, pl.program_id(axis) == size - 1
            first = at_start if first is None else jnp.logical_and(first, at_start)
            last = at_end if last is None else jnp.logical_and(last, at_end)

        def copies():
            return comm.plan(c_ins, c_outs, lambda k: send_sems.at[k], lambda k: recv_sems.at[k],
                             lambda k: local_sems.at[k])

        @pl.when(first)
        def _():
            sends, _, locs = copies()
            for cp in sends + locs:
                cp.start()

        body(*ins, *outs, *scr)

        @pl.when(last)
        def _():
            sends, recvs, locs = copies()
            for cp in recvs:
                cp.wait_recv()
            for cp in sends:
                cp.wait_send()
            for cp in locs:
                cp.wait()

    return pl.pallas_call(
        with_comm, name=name, grid=grid,
        in_specs=list(in_specs) + [ANY] * c_in, out_specs=list(out_specs) + [ANY] * c_out,
        out_shape=list(out_shape) + list(comm.out_shapes),
        scratch_shapes=list(scratch) + _sem_scratch(comm),
        input_output_aliases={n_in + k: n_out + v for k, v in comm.aliases.items()},
        compiler_params=params)(*operands, *comm.operands)


def _place():
    return lax.axis_index("x"), lax.axis_index("y"), lax.axis_index("c")


def _other_chips(x, y):
    return [(1 - x, y), (x, 1 - y), (1 - x, 1 - y)]


def _slot(px, py, pc):
    return 4 * px + 2 * py + pc


def _remote(src, dst, send_sem, recv_sem, to):
    return pltpu.make_async_remote_copy(src_ref=src, dst_ref=dst, send_sem=send_sem, recv_sem=recv_sem,
                                        device_id=to, device_id_type=MESH)


def _gather_first(half_block):
    def plan(ins, outs, send, recv, loc):
        (blk,), (full,) = ins, outs
        x, y, c = _place()
        chips = _other_chips(x, y)
        mine = full.at[_slot(x, y, c)]
        sends = [_remote(blk, mine, send(0), recv(0), (x, y, 1 - c))]
        sends += [_remote(blk, mine, send(1 + j), recv(1 + j), (*chip, c)) for j, chip in enumerate(chips)]
        recvs = [_remote(blk, full.at[_slot(x, y, 1 - c)], send(0), recv(0), (x, y, 1 - c))]
        recvs += [_remote(blk, full.at[_slot(*chip, c)], send(1 + j), recv(1 + j), (*chip, c))
                  for j, chip in enumerate(chips)]
        return sends, recvs, [pltpu.make_async_copy(blk, mine, loc(0))]

    return _Comm((half_block,), (SDS((2 * N_CHIPS,) + half_block.shape, half_block.dtype),), {}, 4, 1, plan)


def _gather_second(partly_gathered):
    def plan(ins, outs, send, recv, loc):
        (src,), (full,) = ins, outs
        x, y, c = _place()
        chips = _other_chips(x, y)
        sends = [_remote(src.at[_slot(*chip, c)], full.at[_slot(*chip, c)], send(j), recv(j), (x, y, 1 - c))
                 for j, chip in enumerate(chips)]
        recvs = [_remote(src.at[_slot(*chip, 1 - c)], full.at[_slot(*chip, 1 - c)], send(j), recv(j), (x, y, 1 - c))
                 for j, chip in enumerate(chips)]
        return sends, recvs, []

    return _Comm((partly_gathered,), (SDS(partly_gathered.shape, partly_gathered.dtype),), {0: 0}, 3, 0, plan)


def _gather_whole(half_block, small_block):
    def body(blk_ref, small_ref, out_ref, small_out_ref, send_sems, recv_sems, local_sems):
        x, y, c = _place()
        me, sibling = (x, y, c), (x, y, 1 - c)
        chips = _other_chips(x, y)

        def copy(k, block, to, src=None):
            return _remote(out_ref.at[_slot(*block)] if src is None else src, out_ref.at[_slot(*block)],
                           send_sems.at[k], recv_sems.at[k], to)

        def small_copy(k, chip, to):
            return _remote(small_ref, small_out_ref.at[2 * chip[0] + chip[1]], send_sems.at[7 + k], recv_sems.at[7 + k], to)

        mine = pltpu.make_async_copy(blk_ref, out_ref.at[_slot(*me)], local_sems.at[0])
        mine_small = pltpu.make_async_copy(small_ref, small_out_ref.at[2 * x + y], local_sems.at[1])
        mine.start()
        mine_small.start()
        first = [copy(0, me, sibling, src=blk_ref)]
        first += [copy(1 + j, me, (*chip, c), src=blk_ref) for j, chip in enumerate(chips)]
        first += [small_copy(j, (x, y), (*chip, c)) for j, chip in enumerate(chips)]
        for cp in first:
            cp.start()
        passed = [copy(4 + j, (*chip, c), sibling) for j, chip in enumerate(chips)]
        for j, chip in enumerate(chips):
            copy(1 + j, (*chip, c), me).wait_recv()
            passed[j].start()
        copy(0, sibling, me).wait_recv()
        for j, chip in enumerate(chips):
            copy(4 + j, (*chip, 1 - c), me).wait_recv()
            small_copy(j, chip, me).wait_recv()
        for cp in first + passed:
            cp.wait_send()
        mine.wait()
        mine_small.wait()

    return pl.pallas_call(
        body, name="gather_whole", in_specs=[ANY, ANY], out_specs=[ANY, ANY],
        out_shape=[SDS((2 * N_CHIPS,) + half_block.shape, half_block.dtype),
                   SDS((N_CHIPS,) + small_block.shape, small_block.dtype)],
        scratch_shapes=[pltpu.SemaphoreType.DMA((10,)), pltpu.SemaphoreType.DMA((10,)), pltpu.SemaphoreType.DMA((2,))],
    )(half_block, small_block)


def _pair_send(grads):
    def plan(ins, outs, send, recv, loc):
        (g,), (got,) = ins, outs
        x, y, c = _place()
        copies = [_remote(g.at[j, 1 - c], got.at[j], send(j), recv(j), (x, y, 1 - c)) for j in range(N_CHIPS)]
        return copies, copies, []

    shape = (grads.shape[0],) + grads.shape[2:]
    return _Comm((grads,), (SDS(shape, grads.dtype),), {}, N_CHIPS, 0, plan)


def _chip_exchange(partial):
    def plan(ins, outs, send, recv, loc):
        (p,), (got,) = ins, outs
        x, y, c = _place()
        my_chip = 2 * x + y
        chips = _other_chips(x, y)
        sends = [_remote(p.at[2 * chip[0] + chip[1]], got.at[my_chip], send(j), recv(j), (*chip, c))
                 for j, chip in enumerate(chips)]
        recvs = [_remote(p.at[my_chip], got.at[2 * chip[0] + chip[1]], send(j), recv(j), (*chip, c))
                 for j, chip in enumerate(chips)]
        return sends, recvs, [pltpu.make_async_copy(p.at[my_chip], got.at[my_chip], loc(0))]

    return _Comm((partial,), (SDS(partial.shape, partial.dtype),), {}, 3, 1, plan)


def _pair_sum(name, core, grads, received):
    h = grads.shape[2]

    def body(core_ref, g_ref, r_ref, o_ref):
        o_ref[...] = (g_ref[0] + r_ref[...]).astype(BF16)

    return pl.pallas_call(
        body, name=name,
        grid_spec=pltpu.PrefetchScalarGridSpec(
            num_scalar_prefetch=1, grid=(N_CHIPS,),
            in_specs=[pl.BlockSpec((1, 1, h, D_MODEL), lambda j, core_ref: (j, core_ref[0], 0, 0)),
                      pl.BlockSpec((1, h, D_MODEL), lambda j, core_ref: (j, 0, 0))],
            out_specs=pl.BlockSpec((1, h, D_MODEL), lambda j, core_ref: (j, 0, 0))),
        out_shape=SDS((N_CHIPS, h, D_MODEL), BF16),
        compiler_params=pltpu.CompilerParams(dimension_semantics=("arbitrary",), vmem_limit_bytes=VMEM_LIMIT_V7X),
    )(core, grads, received)


SMALL_ROWS = 8


def _sum_blocks(ref):
    return (ref[0].astype(F32) + ref[1].astype(F32)) + (ref[2].astype(F32) + ref[3].astype(F32))


def _tail_reduce(last_grads, exchanged, small):
    n = len(exchanged)
    h = last_grads.shape[2]

    def body(*refs):
        g_ref, ex, small_ref = refs[0], refs[1:1 + n], refs[1 + n]
        o0 = 2 + n
        out, out_last, small_out = refs[o0:o0 + n], refs[o0 + n], refs[o0 + n + 1]
        s0 = o0 + n + 2
        halves, half_last = refs[s0:s0 + n], refs[s0 + n]
        own, got, part, exch, small_buf = refs[s0 + n + 1:s0 + n + 6]
        pair_send, pair_recv, chip_send, chip_recv, share_send, share_recv, small_send, small_recv, local_sems = refs[s0 + n + 6:]
        x, y, c = _place()
        sibling = (x, y, 1 - c)
        my_chip, me = 2 * x + y, _slot(x, y, c)
        chips = _other_chips(x, y)

        to_sibling = [_remote(g_ref.at[j, 1 - c], got.at[j], pair_send.at[j], pair_recv.at[j], sibling)
                      for j in range(N_CHIPS)]
        load_own = [pltpu.make_async_copy(g_ref.at[j, c], own.at[j], local_sems.at[j]) for j in range(N_CHIPS)]
        for cp in to_sibling + load_own:
            cp.start()

        small_buf[me] = small_ref[...]
        small_copies = []
        for mask in range(1, 8):
            peer = (x ^ (mask >> 2), y ^ ((mask >> 1) & 1), c ^ (mask & 1))
            small_copies.append(_remote(small_ref, small_buf.at[me], small_send.at[mask - 1], small_recv.at[mask - 1], peer))
        for cp in small_copies:
            cp.start()

        def share(k, half_ref, out_ref):
            keep = pltpu.make_async_copy(half_ref, out_ref.at[c], local_sems.at[N_CHIPS + k])
            give = _remote(half_ref, out_ref.at[c], share_send.at[k], share_recv.at[k], sibling)
            take = _remote(half_ref, out_ref.at[1 - c], share_send.at[k], share_recv.at[k], sibling)
            keep.start()
            give.start()
            return keep, give, take

        shares = []
        for k in range(n):
            halves[k][...] = _sum_blocks(ex[k])
            shares.append(share(k, halves[k], out[k]))

        for cp in to_sibling:
            cp.wait_recv()
        for cp in load_own:
            cp.wait()
        part[...] = (own[...] + got[...]).astype(BF16)
        exch[my_chip] = part[my_chip]
        to_chips = [_remote(part.at[2 * chip[0] + chip[1]], exch.at[my_chip], chip_send.at[j], chip_recv.at[j], (*chip, c))
                    for j, chip in enumerate(chips)]
        from_chips = [_remote(part.at[my_chip], exch.at[2 * chip[0] + chip[1]], chip_send.at[j], chip_recv.at[j], (*chip, c))
                      for j, chip in enumerate(chips)]
        for cp in to_chips:
            cp.start()

        for cp in small_copies:
            cp.wait_recv()
        total = small_buf[0]
        for d in range(1, 8):
            total = total + small_buf[d]
        small_out[...] = total

        for cp in from_chips:
            cp.wait_recv()
        half_last[...] = _sum_blocks(exch)
        shares.append(share(n, half_last, out_last))

        for keep, give, take in shares:
            take.wait_recv()
            give.wait_send()
            keep.wait()
        for cp in to_sibling + to_chips + small_copies:
            cp.wait_send()

    blocks = (N_CHIPS, h, D_MODEL)
    return pl.pallas_call(
        body, name="tail_reduce",
        in_specs=[ANY] + [VMEM_WHOLE] * (n + 1), out_specs=[ANY] * (n + 1) + [VMEM_WHOLE],
        out_shape=[SDS((2,) + e.shape[1:], F32) for e in exchanged] + [SDS((2, h, D_MODEL), F32), SDS(small.shape, F32)],
        scratch_shapes=[pltpu.VMEM(e.shape[1:], F32) for e in exchanged] + [pltpu.VMEM((h, D_MODEL), F32)]
                       + [pltpu.VMEM(blocks, F32), pltpu.VMEM(blocks, F32), pltpu.VMEM(blocks, BF16), pltpu.VMEM(blocks, BF16),
                          pltpu.VMEM((8,) + small.shape, F32)]
                       + [pltpu.SemaphoreType.DMA((N_CHIPS,)), pltpu.SemaphoreType.DMA((N_CHIPS,)),
                          pltpu.SemaphoreType.DMA((3,)), pltpu.SemaphoreType.DMA((3,)),
                          pltpu.SemaphoreType.DMA((n + 1,)), pltpu.SemaphoreType.DMA((n + 1,)),
                          pltpu.SemaphoreType.DMA((7,)), pltpu.SemaphoreType.DMA((7,)),
                          pltpu.SemaphoreType.DMA((N_CHIPS + n + 1,))],
        compiler_params=pltpu.CompilerParams(vmem_limit_bytes=VMEM_LIMIT_V7X),
    )(last_grads, *exchanged, small)


def _rope_expansion():
    half = ROT_DIM // 2
    expand = np.zeros((2 * half, 3 * 128), np.float32)
    const = np.zeros((1, 3 * 128), np.float32)
    for lane in range(128):
        d = lane % HEAD_DIM
        if d < ROT_DIM:
            expand[d % half, lane] = 1.0
        else:
            const[0, lane] = 1.0
        if d < half:
            expand[half + d, 128 + lane] = -1.0
        elif d < ROT_DIM:
            expand[half + d - half, 256 + lane] = 1.0
    return expand, const


ROPE_PIECES = 3 * ROT_DIM


def _rope_inputs(seq):
    pos = jnp.arange(seq, dtype=F32)
    inv_freq = ROPE_THETA ** (-jnp.arange(0, ROT_DIM, 2, dtype=F32) / ROT_DIM)
    ang = pos[:, None] * inv_freq[None, :]
    cs = jnp.concatenate([jnp.cos(ang), jnp.sin(ang)], axis=1)
    hi = lax.reduce_precision(cs, 8, 7)
    mid = lax.reduce_precision(cs - hi, 8, 7)
    low = cs - hi - mid
    expand, const = _rope_expansion()
    pieces = jnp.concatenate([hi, mid, low], axis=1).astype(BF16)
    return pieces, jnp.asarray(np.concatenate([expand] * 3, axis=0), BF16), jnp.asarray(const)


def _rope_specs(tb):
    return [pl.BlockSpec((tb, ROPE_PIECES), lambda i: (i, 0)), _resident((ROPE_PIECES, 3 * 128)), _resident((1, 3 * 128))]


def _rope_tile(pieces_ref, expand_ref, const_ref):
    tables = _dot(pieces_ref[...], expand_ref[...]) + const_ref[...]
    return tables[:, 0:128], tables[:, 128:256], tables[:, 256:384]


def _rope(t, c, sa, sb):
    half = ROT_DIM // 2
    return t * c + pltpu.roll(t, 128 - half, 1) * sa + pltpu.roll(t, half, 1) * sb


def _rope_transposed(dt, c, sa, sb):
    half = ROT_DIM // 2
    return dt * c + pltpu.roll(dt * sa, half, 1) + pltpu.roll(dt * sb, 128 - half, 1)


def _cast_halves(core, w_up, w_down, w_out, w_in_t):
    def body(core_ref, up_ref, down_ref, out_ref, in_ref, up_o, down_o, out_o, in_o):
        up_o[...] = up_ref[...].astype(BF16)
        down_o[...] = down_ref[...].astype(BF16)
        out_o[...] = out_ref[...].astype(BF16)
        in_o[...] = in_ref[...].astype(BF16)

    half = lambda rows: pl.BlockSpec((rows, D_MODEL), lambda i, core_ref: (core_ref[0], 0))
    whole = lambda rows: pl.BlockSpec((rows, D_MODEL), lambda i, core_ref: (0, 0))
    rows = (H_UP, H_DOWN, H_OUT, H_IN)
    return pl.pallas_call(
        body, name="cast_halves",
        grid_spec=pltpu.PrefetchScalarGridSpec(
            num_scalar_prefetch=1, grid=(1,), in_specs=[half(r) for r in rows], out_specs=[whole(r) for r in rows]),
        out_shape=[SDS((r, D_MODEL), BF16) for r in rows],
        compiler_params=pltpu.CompilerParams(dimension_semantics=("arbitrary",), vmem_limit_bytes=VMEM_LIMIT_V7X),
    )(core, w_up, w_down, w_out, w_in_t)


def _in_proj(x, g_pre, w_in_t, rope, comm=None):
    seq = x.shape[0]
    tb = TOKEN_TILE

    def body(x_ref, g_ref, w_ref, c_ref, sa_ref, sb_ref,
             q_ref, kd0_ref, kd1_ref, vd0_ref, vd1_ref, gb_ref, gc_ref, xin_ref, hn_ref):
        xv = x_ref[...]
        hn = (xv * _rms(xv) * g_ref[...]).astype(BF16)
        hn_ref[...] = hn
        proj = _dot_nt(hn, w_ref[...].reshape(IN_COLS, D_MODEL))
        c, sa, sb = _rope_tile(c_ref, sa_ref, sb_ref)
        scale = 1.0 / math.sqrt(HEAD_DIM)
        for p in range(Q_WIDTH // 128):
            q_ref[:, 128 * p:128 * (p + 1)] = (_rope(proj[:, 128 * p:128 * (p + 1)], c, sa, sb) * scale).astype(BF16)
        k = _rope(proj[:, Q_WIDTH:Q_WIDTH + KV_WIDTH], c, sa, sb)
        v = proj[:, Q_WIDTH + KV_WIDTH:Q_WIDTH + 2 * KV_WIDTH]
        low = _lane_lt64(k.shape)
        k_sw, v_sw = pltpu.roll(k, HEAD_DIM, 1), pltpu.roll(v, HEAD_DIM, 1)
        kd0_ref[...] = jnp.where(low, k, k_sw).astype(BF16)
        kd1_ref[...] = jnp.where(low, k_sw, k).astype(BF16)
        vd0_ref[...] = jnp.where(low, v, v_sw).astype(BF16)
        vd1_ref[...] = jnp.where(low, v_sw, v).astype(BF16)
        base = Q_WIDTH + 2 * KV_WIDTH
        gb_ref[...] = proj[:, base:base + CONV_WIDTH].astype(BF16)
        gc_ref[...] = proj[:, base + CONV_WIDTH:base + 2 * CONV_WIDTH].astype(BF16)
        xin_ref[...] = proj[:, base + 2 * CONV_WIDTH:base + 3 * CONV_WIDTH].astype(BF16)

    tile = lambda w: pl.BlockSpec((tb, w), lambda i: (i, 0))
    return _pallas(
        body, name="in_proj", grid=(seq // tb,),
        in_specs=[tile(D_MODEL), _resident((1, D_MODEL)), _resident(w_in_t.shape), *_rope_specs(tb)],
        out_specs=[tile(Q_WIDTH), tile(128), tile(128), tile(128), tile(128),
                   tile(CONV_WIDTH), tile(CONV_WIDTH), tile(CONV_WIDTH), tile(D_MODEL)],
        out_shape=[SDS((seq, Q_WIDTH), BF16)] + [SDS((seq, 128), BF16)] * 4
                  + [SDS((seq, CONV_WIDTH), BF16)] * 3 + [SDS((seq, D_MODEL), BF16)],
        operands=(x, g_pre, w_in_t, *rope), comm=comm)


def _attn_valid(i):
    shape = (4 * QBLOCK, 2 * QBLOCK)
    row = lax.broadcasted_iota(jnp.int32, shape, 0)
    col = lax.broadcasted_iota(jnp.int32, shape, 1)
    qi = row & (QBLOCK - 1)
    return (col > qi) & (col <= qi + QBLOCK) & ((col >= QBLOCK) | (i > 0))


def _stack_heads(pair0, pair1):
    low = _lane_lt64(pair0.shape)
    zero = jnp.zeros_like(pair0)
    return jnp.concatenate([jnp.where(low, pair0, zero), jnp.where(low, zero, pair0),
                            jnp.where(low, pair1, zero), jnp.where(low, zero, pair1)], axis=0)


def _unstack_heads(stacked):
    low = _lane_lt64((QBLOCK, 128))
    pair0 = jnp.where(low, stacked[0:QBLOCK], stacked[QBLOCK:2 * QBLOCK])
    pair1 = jnp.where(low, stacked[2 * QBLOCK:3 * QBLOCK], stacked[3 * QBLOCK:4 * QBLOCK])
    return pair0, pair1


def _sink_column(sink_ref, kv_head):
    row = lax.broadcasted_iota(jnp.int32, (4 * QBLOCK, 1), 0)
    s = [sink_ref[0, 4 * kv_head + j] for j in range(4)]
    return jnp.where(row < QBLOCK, s[0], jnp.where(row < 2 * QBLOCK, s[1], jnp.where(row < 3 * QBLOCK, s[2], s[3])))


def _band(ref, i):
    prev = pl.multiple_of(jnp.maximum(i - 1, 0) * QBLOCK, QBLOCK)
    own = pl.multiple_of(i * QBLOCK, QBLOCK)
    return jnp.concatenate([ref[pl.ds(prev, QBLOCK), :], ref[pl.ds(own, QBLOCK), :]], axis=0), prev, own


def _softmax_with_sink(s, sink_col):
    m = jnp.maximum(jnp.max(s, axis=-1, keepdims=True), sink_col)
    p = jnp.exp(s - m)
    e_sink = jnp.exp(sink_col - m)
    inv_l = 1.0 / (jnp.sum(p, axis=-1, keepdims=True) + e_sink)
    return p, e_sink, inv_l


def _attention_fwd(q, kd0, kd1, vd0, vd1, sinks, comm=None):
    seq = q.shape[0]

    nb = ATTN_FWD_BLOCKS

    def body(sink_ref, q_ref, kd0_ref, kd1_ref, vd0_ref, vd1_ref, o_ref):
        for b in range(nb):
            i = pl.program_id(0) * nb + b
            rows = slice(QBLOCK * b, QBLOCK * (b + 1))
            valid = _attn_valid(i)
            for kv_head, (k_ref, v_ref) in enumerate(((kd0_ref, vd0_ref), (kd1_ref, vd1_ref))):
                kband, _, _ = _band(k_ref, i)
                vband, _, _ = _band(v_ref, i)
                base = 256 * kv_head
                qm = _stack_heads(q_ref[rows, base:base + 128], q_ref[rows, base + 128:base + 256])
                s = jnp.where(valid, _dot_nt(qm, kband), NEG_INF)
                p, _, inv_l = _softmax_with_sink(s, _sink_column(sink_ref, kv_head))
                o = _dot(p.astype(BF16), vband) * inv_l
                pair0, pair1 = _unstack_heads(o)
                o_ref[rows, base:base + 128] = pair0.astype(BF16)
                o_ref[rows, base + 128:base + 256] = pair1.astype(BF16)

    blk = pl.BlockSpec((nb * QBLOCK, Q_WIDTH), lambda i: (i, 0))
    full = _resident((seq, 128))
    return _pallas(
        body, name="attention_fwd", grid=(seq // (nb * QBLOCK),),
        in_specs=[pl.BlockSpec(memory_space=pltpu.SMEM), blk, full, full, full, full],
        out_specs=[blk], out_shape=[SDS((seq, Q_WIDTH), BF16)],
        operands=(sinks, q, kd0, kd1, vd0, vd1), comm=comm)


HALO = 16


def _conv_parts(gc, xin, gc_halo, xin_halo, conv_w, first):
    tb = gc.shape[0]
    u = gc.astype(F32) * xin.astype(F32)
    u_halo = jnp.where(first, 0.0, gc_halo.astype(F32) * xin_halo.astype(F32))
    ext = jnp.concatenate([u_halo, u], axis=0)
    u1 = pltpu.roll(ext, 1, 0)[HALO:HALO + tb]
    u2 = pltpu.roll(ext, 2, 0)[HALO:HALO + tb]
    y = conv_w[0:1, :] * u2 + conv_w[1:2, :] * u1 + conv_w[2:3, :] * u
    return u, u1, u2, y


def _halo_prev(tb, w):
    return pl.BlockSpec((HALO, w), lambda i: (jnp.maximum(i * (tb // HALO) - 1, 0), 0))


def _residual_mid(x, mix, g_post_mix):
    mix_f = mix.astype(F32)
    return x + mix_f * _rms(mix_f) * g_post_mix


def _mix_out(attn, gb, gc, xin, conv_w, g_attn, g_conv, w_out, comm=None):
    seq = attn.shape[0]
    tb = TOKEN_TILE

    def body(a_ref, gb_ref, gc_ref, xin_ref, gch_ref, xinh_ref, cw_ref, ga_ref, gcn_ref, w_ref, mix_ref, mixed_ref):
        first = pl.program_id(0) == 0
        _, _, _, y = _conv_parts(gc_ref[...], xin_ref[...], gch_ref[...], xinh_ref[...], cw_ref[...], first)
        conv = gb_ref[...].astype(F32) * y
        a = a_ref[...].astype(F32)
        mixed_ref[:, 0:Q_WIDTH] = (a * _rms(a) * ga_ref[...]).astype(BF16)
        mixed_ref[:, Q_WIDTH:] = (conv * _rms(conv) * gcn_ref[...]).astype(BF16)
        mix_ref[...] = _dot(mixed_ref[...], w_ref[...].reshape(D_MODEL, D_MODEL)).astype(BF16)

    tile = lambda w: pl.BlockSpec((tb, w), lambda i: (i, 0))
    return _pallas(
        body, name="mix_out", grid=(seq // tb,),
        in_specs=[tile(Q_WIDTH), tile(CONV_WIDTH), tile(CONV_WIDTH), tile(CONV_WIDTH),
                  _halo_prev(tb, CONV_WIDTH), _halo_prev(tb, CONV_WIDTH),
                  _resident((CONV_K, CONV_WIDTH)), _resident((1, Q_WIDTH)), _resident((1, CONV_WIDTH)),
                  _resident(w_out.shape)],
        out_specs=[tile(D_MODEL), tile(D_MODEL)],
        out_shape=[SDS((seq, D_MODEL), BF16), SDS((seq, D_MODEL), BF16)],
        operands=(attn, gb, gc, xin, gc, xin, conv_w, g_attn, g_conv, w_out), comm=comm)


def _mlp_loss(x, mix, target, g_post_mix, g_pre_mlp, g_post_mlp, w_up, w_down):
    seq = x.shape[0]
    tb = TOKEN_TILE

    def body(x_ref, mix_ref, t_ref, gpm_ref, g2_ref, g4_ref, wup_ref, wdown_ref,
             up_ref, hn2_ref, dout_ref, dmlp_ref, loss_ref, dg4_ref, act_ref):
        @pl.when(pl.program_id(0) == 0)
        def _():
            loss_ref[...] = jnp.zeros_like(loss_ref)
            dg4_ref[...] = jnp.zeros_like(dg4_ref)

        halves = [slice(0, tb // 2), slice(tb // 2, tb)]
        hv, hn2 = [], []
        for rows in halves:
            hv.append(_residual_mid(x_ref[rows, :], mix_ref[rows, :], gpm_ref[...]))
            hn2.append((hv[-1] * _rms(hv[-1]) * g2_ref[...]).astype(BF16))
            hn2_ref[rows, :] = hn2[-1]
        for k, rows in enumerate(halves):
            for j in range(N_CHIPS):
                up = _dot(hn2[k], _chip_block(wup_ref, j))
                up = jnp.maximum(up, 0.0)
                up_ref[rows, 1024 * j:1024 * (j + 1)] = up.astype(BF16)
                act_ref[rows, 1024 * j:1024 * (j + 1)] = (up * up).astype(BF16)
        w_down_all = wdown_ref[...].reshape(D_FF, D_MODEL)
        loss = jnp.zeros((1, 1), F32)
        dg4 = jnp.zeros((1, D_MODEL), F32)
        for k, rows in enumerate(halves):
            mlp = _dot(act_ref[rows, :], w_down_all)
            rstd = _rms(mlp)
            zhat = mlp * rstd
            diff = hv[k] + zhat * g4_ref[...] - t_ref[rows, :]
            loss = loss + jnp.sum(jnp.sum(diff * diff, axis=1, keepdims=True), axis=0, keepdims=True)
            dout = diff * (1.0 / D_MODEL)
            dout_ref[rows, :] = dout
            dg4 = dg4 + _colsum(dout * zhat)
            dmlp_ref[rows, :] = _norm_bwd(dout, g4_ref[...], zhat, rstd).astype(BF16)
        loss_ref[...] += loss
        dg4_ref[...] += dg4

    tile = lambda w: pl.BlockSpec((tb, w), lambda i: (i, 0))
    return _pallas(
        body, name="mlp_loss", grid=(seq // tb,),
        in_specs=[tile(D_MODEL), tile(D_MODEL), tile(D_MODEL), _resident((1, D_MODEL)), _resident((1, D_MODEL)),
                  _resident((1, D_MODEL)), _resident(w_up.shape), _resident(w_down.shape)],
        out_specs=[tile(D_FF), tile(D_MODEL), tile(D_MODEL), tile(D_MODEL),
                   pl.BlockSpec((1, 1), lambda i: (0, 0)), pl.BlockSpec((1, D_MODEL), lambda i: (0, 0))],
        out_shape=[SDS((seq, D_FF), BF16), SDS((seq, D_MODEL), BF16), SDS((seq, D_MODEL), F32),
                   SDS((seq, D_MODEL), BF16), SDS((1, 1), F32), SDS((1, D_MODEL), F32)],
        scratch=[pltpu.VMEM((tb, D_FF), BF16)],
        operands=(x, mix, target, g_post_mix, g_pre_mlp, g_post_mlp, w_up, w_down))


def _mlp_bwd(dmlp, up, x, dout, mix, g_pre_mlp, g_post_mix, w_up, w_down):
    seq = x.shape[0]
    tb = MLP_BWD_TOKEN_TILE

    def body(dmlp_ref, up_ref, x_ref, dout_ref, mix_ref, g2_ref, gpm_ref, wup_ref, wdown_ref,
             dup_ref, dh_ref, dmix_ref, dg2_ref, dgpm_ref):
        @pl.when(pl.program_id(0) == 0)
        def _():
            dg2_ref[...] = jnp.zeros_like(dg2_ref)
            dgpm_ref[...] = jnp.zeros_like(dgpm_ref)

        subs = [slice(k * MLP_BWD_SUB_TILE, (k + 1) * MLP_BWD_SUB_TILE) for k in range(tb // MLP_BWD_SUB_TILE)]
        dhn2 = []
        for rows in subs:
            dmlp_v = dmlp_ref[rows, :]
            acc = None
            for j in range(N_CHIPS):
                cols = slice(1024 * j, 1024 * (j + 1))
                dact = _dot_nt(dmlp_v, _chip_block(wdown_ref, j))
                dup = (dact * (2.0 * up_ref[rows, cols].astype(F32))).astype(BF16)
                dup_ref[rows, cols] = dup
                part = _dot_nt(dup, _chip_block(wup_ref, j))
                acc = part if acc is None else acc + part
            dhn2.append(acc)
        dg2 = jnp.zeros((1, D_MODEL), F32)
        dgpm = jnp.zeros((1, D_MODEL), F32)
        for k, rows in enumerate(subs):
            mix_v = mix_ref[rows, :].astype(F32)
            hv = _residual_mid(x_ref[rows, :], mix_ref[rows, :], gpm_ref[...])
            r2 = _rms(hv)
            hhat = hv * r2
            dg2 = dg2 + _colsum(dhn2[k] * hhat)
            dh = dout_ref[rows, :] + _norm_bwd(dhn2[k], g2_ref[...], hhat, r2)
            dh_ref[rows, :] = dh.astype(BF16)
            rz = _rms(mix_v)
            zhat = mix_v * rz
            dgpm = dgpm + _colsum(dh * zhat)
            dmix_ref[rows, :] = _norm_bwd(dh, gpm_ref[...], zhat, rz).astype(BF16)
        dg2_ref[...] += dg2
        dgpm_ref[...] += dgpm

    tile = lambda w: pl.BlockSpec((tb, w), lambda i: (i, 0))
    vec = pl.BlockSpec((1, D_MODEL), lambda i: (0, 0))
    return _pallas(
        body, name="mlp_bwd", grid=(seq // tb,),
        in_specs=[tile(D_MODEL), tile(D_FF), tile(D_MODEL), tile(D_MODEL), tile(D_MODEL),
                  _resident((1, D_MODEL)), _resident((1, D_MODEL)), _resident(w_up.shape), _resident(w_down.shape)],
        out_specs=[tile(D_FF), tile(D_MODEL), tile(D_MODEL), vec, vec],
        out_shape=[SDS((seq, D_FF), BF16), SDS((seq, D_MODEL), BF16), SDS((seq, D_MODEL), BF16),
                   SDS((1, D_MODEL), F32), SDS((1, D_MODEL), F32)],
        operands=(dmlp, up, x, dout, mix, g_pre_mlp, g_post_mix, w_up, w_down))


def _mix_bwd(dmix, attn, gb, gc, xin, conv_w, g_attn, g_conv, w_out, n_k):
    seq = attn.shape[0]
    tb = seq // (N_CHIPS * n_k)

    def body(first, dmix_ref, a_ref, gb_ref, gc_ref, xin_ref, gch_ref, xinh_ref, cw_ref, ga_ref, gcn_ref, w_ref,
             dattn_ref, dgb_ref, dy_ref, dga_ref, dgcn_ref, dcw_ref):
        @pl.when(first)
        def _():
            dga_ref[...] = jnp.zeros_like(dga_ref)
            dgcn_ref[...] = jnp.zeros_like(dgcn_ref)
            dcw_ref[...] = jnp.zeros_like(dcw_ref)

        dmixed = _dot_nt(dmix_ref[...], w_ref[...].reshape(D_MODEL, D_MODEL))
        a = a_ref[...].astype(F32)
        ra = _rms(a)
        ahat = a * ra
        dan = dmixed[:, 0:Q_WIDTH]
        dga_ref[...] += _colsum(dan * ahat)
        dattn_ref[...] = _norm_bwd(dan, ga_ref[...], ahat, ra).astype(BF16)
        gbv = gb_ref[...].astype(F32)
        u, u1, u2, y = _conv_parts(gc_ref[...], xin_ref[...], gch_ref[...], xinh_ref[...], cw_ref[...], first)
        conv = gbv * y
        rc = _rms(conv)
        chat = conv * rc
        dcn = dmixed[:, Q_WIDTH:]
        dgcn_ref[...] += _colsum(dcn * chat)
        dconv = _norm_bwd(dcn, gcn_ref[...], chat, rc)
        dgb_ref[...] = (dconv * y).astype(BF16)
        dy = dconv * gbv
        dy_ref[...] = dy.astype(BF16)
        dcw_ref[0:1, :] += _colsum(dy * u2)
        dcw_ref[1:2, :] += _colsum(dy * u1)
        dcw_ref[2:3, :] += _colsum(dy * u)

    tile = lambda w: pl.BlockSpec((tb, w), lambda j, k: (j * n_k + k, 0))
    halo = lambda w: pl.BlockSpec((HALO, w), lambda j, k: (jnp.maximum((j * n_k + k) * (tb // HALO) - 1, 0), 0))
    whole = lambda shape: pl.BlockSpec(shape, lambda j, k: (0,) * len(shape))
    return _Rider(
        body,
        in_specs=[tile(D_MODEL), tile(Q_WIDTH), tile(CONV_WIDTH), tile(CONV_WIDTH), tile(CONV_WIDTH),
                  halo(CONV_WIDTH), halo(CONV_WIDTH),
                  _resident((CONV_K, CONV_WIDTH)), _resident((1, Q_WIDTH)), _resident((1, CONV_WIDTH)),
                  _resident(w_out.shape)],
        out_specs=[tile(Q_WIDTH), tile(CONV_WIDTH), tile(CONV_WIDTH),
                   whole((1, Q_WIDTH)), whole((1, CONV_WIDTH)), whole((CONV_K, CONV_WIDTH))],
        out_shape=[SDS((seq, Q_WIDTH), BF16), SDS((seq, CONV_WIDTH), BF16), SDS((seq, CONV_WIDTH), BF16),
                   SDS((1, Q_WIDTH), F32), SDS((1, CONV_WIDTH), F32), SDS((CONV_K, CONV_WIDTH), F32)],
        operands=(dmix, attn, gb, gc, xin, gc, xin, conv_w, g_attn, g_conv, w_out))


def _attention_bwd(q, dattn, attn, kd0, kd1, vd0, vd1, sinks, comm=None):
    seq = q.shape[0]
    nb = ATTN_BWD_BLOCKS

    def body(sink_ref, q_ref, do_ref, o_ref, kd0_ref, kd1_ref, vd0_ref, vd1_ref,
             dq_ref, dk0_ref, dk1_ref, dv0_ref, dv1_ref, dsink_ref):
        @pl.when(pl.program_id(0) == 0)
        def _():
            for r in (dk0_ref, dk1_ref, dv0_ref, dv1_ref, dsink_ref):
                r[...] = jnp.zeros_like(r)

        lane = lax.broadcasted_iota(jnp.int32, (1, 128), 1)
        dsink = jnp.zeros((1, 128), F32)
        for b in range(nb):
            i = pl.program_id(0) * nb + b
            rows = slice(QBLOCK * b, QBLOCK * (b + 1))
            valid = _attn_valid(i)
            for kv_head, (k_ref, v_ref, dk_ref, dv_ref) in enumerate(
                    ((kd0_ref, vd0_ref, dk0_ref, dv0_ref), (kd1_ref, vd1_ref, dk1_ref, dv1_ref))):
                kband, prev, own = _band(k_ref, i)
                vband, _, _ = _band(v_ref, i)
                base = 256 * kv_head
                qm = _stack_heads(q_ref[rows, base:base + 128], q_ref[rows, base + 128:base + 256])
                dom = _stack_heads(do_ref[rows, base:base + 128], do_ref[rows, base + 128:base + 256])
                om = _stack_heads(o_ref[rows, base:base + 128], o_ref[rows, base + 128:base + 256])
                s = jnp.where(valid, _dot_nt(qm, kband), NEG_INF)
                p, e_sink, inv_l = _softmax_with_sink(s, _sink_column(sink_ref, kv_head))
                p = p * inv_l
                delta = jnp.sum(dom.astype(F32) * om.astype(F32), axis=-1, keepdims=True)
                ds = (p * (_dot_nt(dom, vband) - delta)).astype(BF16)
                sink_term = -(e_sink * inv_l) * delta
                for j in range(4):
                    part = jnp.sum(sink_term[QBLOCK * j:QBLOCK * (j + 1)], axis=0, keepdims=True)
                    dsink = dsink + jnp.where(lane == 4 * kv_head + j, part, 0.0)
                pair0, pair1 = _unstack_heads(_dot(ds, kband))
                dq_ref[rows, base:base + 128] = pair0.astype(BF16)
                dq_ref[rows, base + 128:base + 256] = pair1.astype(BF16)
                dkd = _dot_tn(ds, qm)
                dkd = dkd + pltpu.roll(dkd, HEAD_DIM, 1)
                dvd = _dot_tn(p.astype(BF16), dom)
                dvd = dvd + pltpu.roll(dvd, HEAD_DIM, 1)
                dk_ref[pl.ds(prev, QBLOCK), :] += dkd[0:QBLOCK]
                dk_ref[pl.ds(own, QBLOCK), :] += dkd[QBLOCK:]
                dv_ref[pl.ds(prev, QBLOCK), :] += dvd[0:QBLOCK]
                dv_ref[pl.ds(own, QBLOCK), :] += dvd[QBLOCK:]
        dsink_ref[...] += dsink

    blk = pl.BlockSpec((nb * QBLOCK, Q_WIDTH), lambda i: (i, 0))
    full = _resident((seq, 128))
    acc = pl.BlockSpec((seq, 128), lambda i: (0, 0))
    return _pallas(
        body, name="attention_bwd", grid=(seq // (nb * QBLOCK),),
        in_specs=[pl.BlockSpec(memory_space=pltpu.SMEM), blk, blk, blk, full, full, full, full],
        out_specs=[blk, acc, acc, acc, acc, pl.BlockSpec((1, 128), lambda i: (0, 0))],
        out_shape=[SDS((seq, Q_WIDTH), BF16)] + [SDS((seq, 128), F32)] * 4 + [SDS((1, 128), F32)],
        operands=(sinks, q, dattn, attn, kd0, kd1, vd0, vd1), comm=comm)


def _in_proj_bwd(dq, dk0, dk1, dv0, dv1, dgb, dy, gc, xin, conv_w, x, dh, g_pre, w_in_t, rope):
    seq = x.shape[0]
    tb = TOKEN_TILE
    n_tiles = seq // tb

    def body(dq_ref, dk0_ref, dk1_ref, dv0_ref, dv1_ref, dgb_ref, dy_ref, dyh_ref, gc_ref, xin_ref, cw_ref,
             x_ref, dh_ref, g_ref, w_ref, c_ref, sa_ref, sb_ref,
             dproj_ref, gx_ref, dg_ref):
        i = pl.program_id(0)

        @pl.when(i == 0)
        def _():
            dg_ref[...] = jnp.zeros_like(dg_ref)

        dy = dy_ref[...].astype(F32)
        ext = jnp.concatenate([dy, jnp.where(i == n_tiles - 1, 0.0, dyh_ref[...].astype(F32))], axis=0)
        dy1 = pltpu.roll(ext, tb + HALO - 1, 0)[0:tb]
        dy2 = pltpu.roll(ext, tb + HALO - 2, 0)[0:tb]
        cw = cw_ref[...]
        du = cw[2:3, :] * dy + cw[1:2, :] * dy1 + cw[0:1, :] * dy2
        scale = 1.0 / math.sqrt(HEAD_DIM)
        base = Q_WIDTH + 2 * KV_WIDTH
        halves = [slice(0, tb // 2), slice(tb // 2, tb)]
        low = _lane_lt64((tb // 2, 128))
        for rows in halves:
            c, sa, sb = _rope_tile(c_ref.at[rows, :], sa_ref, sb_ref)
            for p in range(Q_WIDTH // 128):
                dproj_ref[rows, 128 * p:128 * (p + 1)] = _rope_transposed(
                    dq_ref[rows, 128 * p:128 * (p + 1)].astype(F32) * scale, c, sa, sb).astype(BF16)
            dk = jnp.where(low, dk0_ref[rows, :], dk1_ref[rows, :])
            dproj_ref[rows, Q_WIDTH:Q_WIDTH + KV_WIDTH] = _rope_transposed(dk, c, sa, sb).astype(BF16)
            dproj_ref[rows, Q_WIDTH + KV_WIDTH:base] = jnp.where(low, dv0_ref[rows, :], dv1_ref[rows, :]).astype(BF16)
            dproj_ref[rows, base:base + CONV_WIDTH] = dgb_ref[rows, :]
            dproj_ref[rows, base + CONV_WIDTH:base + 2 * CONV_WIDTH] = (du[rows] * xin_ref[rows, :].astype(F32)).astype(BF16)
            dproj_ref[rows, base + 2 * CONV_WIDTH:] = (du[rows] * gc_ref[rows, :].astype(F32)).astype(BF16)
        w_all = w_ref[...].reshape(IN_COLS, D_MODEL)
        dhn = [_dot(dproj_ref[rows, :], w_all) for rows in halves]
        dg = jnp.zeros((1, D_MODEL), F32)
        for k, rows in enumerate(halves):
            xv = x_ref[rows, :]
            r = _rms(xv)
            xhat = xv * r
            dg = dg + _colsum(dhn[k] * xhat)
            gx_ref[rows, :] = dh_ref[rows, :].astype(F32) + _norm_bwd(dhn[k], g_ref[...], xhat, r)
        dg_ref[...] += dg

    tile = lambda w: pl.BlockSpec((tb, w), lambda i: (i, 0))
    halo_next = pl.BlockSpec((HALO, CONV_WIDTH), lambda i: (jnp.minimum((i + 1) * (tb // HALO), seq // HALO - 1), 0))
    return _pallas(
        body, name="in_proj_bwd", grid=(n_tiles,),
        in_specs=[tile(Q_WIDTH), tile(128), tile(128), tile(128), tile(128), tile(CONV_WIDTH), tile(CONV_WIDTH), halo_next,
                  tile(CONV_WIDTH), tile(CONV_WIDTH), _resident((CONV_K, CONV_WIDTH)),
                  tile(D_MODEL), tile(D_MODEL), _resident((1, D_MODEL)), _resident(w_in_t.shape), *_rope_specs(tb)],
        out_specs=[tile(IN_COLS), tile(D_MODEL), pl.BlockSpec((1, D_MODEL), lambda i: (0, 0))],
        out_shape=[SDS((seq, IN_COLS), BF16), SDS((seq, D_MODEL), F32), SDS((1, D_MODEL), F32)],
        operands=(dq, dk0, dk1, dv0, dv1, dgb, dy, dy, gc, xin, conv_w, x, dh, g_pre, w_in_t, *rope))


def _wgrad_grid(seq, per_chip, h_rows):
    chips_per_step = 1 if per_chip else N_CHIPS
    m = chips_per_step * 2 * h_rows
    bt = min(seq, WGRAD_TOKEN_TILE)
    return chips_per_step, m, bt, seq // bt


def _wgrad(name, a, b, *, per_chip, h_rows, square_a=False, comm=None, rider=None):
    seq = a.shape[0]
    chips_per_step, m, bt, n_k = _wgrad_grid(seq, per_chip, h_rows)
    a_cols = m if per_chip else a.shape[1]
    a_wide = a.shape[1] > a_cols
    b_wide = b.shape[1] > D_MODEL

    def body(a_ref, b_ref, g_ref):
        @pl.when(pl.program_id(1) == 0)
        def _():
            g_ref[...] = jnp.zeros_like(g_ref)

        av = a_ref[...]
        if square_a:
            av = (av.astype(F32) * av.astype(F32)).astype(BF16)
        g_ref[...] += _dot_tn(av, b_ref[...]).reshape(g_ref.shape)

    a_spec = pl.BlockSpec((bt, a_cols), (lambda j, k: (k, j)) if a_wide else (lambda j, k: (k, 0)))
    b_spec = pl.BlockSpec((bt, D_MODEL), (lambda j, k: (k, j)) if b_wide else (lambda j, k: (k, 0)))
    g_spec = pl.BlockSpec((chips_per_step, 2, h_rows, D_MODEL), lambda j, k: (j, 0, 0, 0),
                          pipeline_mode=None if per_chip else pl.Buffered(1))
    return _pallas(
        body, name=name, grid=(N_CHIPS if per_chip else 1, n_k),
        in_specs=[a_spec, b_spec], out_specs=[g_spec], out_shape=[SDS((N_CHIPS, 2, h_rows, D_MODEL), F32)],
        operands=(a, b), comm=comm, rider=rider)


def _adamw_math(w, g, m, v):
    m = ADAM_B1 * m + (1.0 - ADAM_B1) * g
    v = ADAM_B2 * v + (1.0 - ADAM_B2) * (g * g)
    m_hat = m / (1.0 - ADAM_B1 ** ADAM_STEP)
    v_hat = v / (1.0 - ADAM_B2 ** ADAM_STEP)
    delta = -ADAM_LR * (m_hat / (jnp.sqrt(v_hat) + ADAM_EPS) + ADAM_WD * w)
    return delta, m, v


def _adamw_rows(name, reduced, w, m, v, rt):
    per_half = reduced.shape[1] // rt

    def body(r_ref, w_ref, m_ref, v_ref, g_out, d_out, m_out, v_out):
        g = r_ref[0]
        g_out[...] = g
        d_out[...], m_out[...], v_out[...] = _adamw_math(w_ref[...], g, m_ref[...], v_ref[...])

    blk = pl.BlockSpec((rt, D_MODEL), lambda h, r: (h * per_half + r, 0))
    return _pallas(
        body, name=name, grid=(2, per_half),
        in_specs=[pl.BlockSpec((1, rt, D_MODEL), lambda h, r: (h, r, 0)), blk, blk, blk],
        out_specs=[blk, blk, blk, blk], out_shape=[SDS(w.shape, F32)] * 4, operands=(reduced, w, m, v))


def _adamw_plain(name, w, g, m, v):
    rows, cols = w.shape
    rt = 256

    def body(w_ref, g_ref, m_ref, v_ref, d_out, m_out, v_out):
        d_out[...], m_out[...], v_out[...] = _adamw_math(w_ref[...], g_ref[...], m_ref[...], v_ref[...])

    blk = pl.BlockSpec((rt, cols), lambda r: (r, 0))
    return _pallas(body, name=name, grid=(rows // rt,), in_specs=[blk] * 4, out_specs=[blk] * 3,
                   out_shape=[SDS(w.shape, F32)] * 3, operands=(w, g, m, v))


def _adamw_small(w, g, m, v):
    def body(w_ref, g_ref, m_ref, v_ref, d_out, m_out, v_out):
        d_out[...], m_out[...], v_out[...] = _adamw_math(w_ref[...], g_ref[...], m_ref[...], v_ref[...])

    return pl.pallas_call(body, name="adamw_small", in_specs=[VMEM_WHOLE] * 4, out_specs=[VMEM_WHOLE] * 3,
                          out_shape=[SDS(w.shape, F32)] * 3)(w, g, m, v)


SMALL_VECTORS = ("pre_mix_norm", "post_mix_norm", "pre_mlp_norm", "post_mlp_norm")
SMALL_NAMES = SMALL_VECTORS + ("attn_group_norm", "conv_group_norm", "conv_w", "attn_sinks")


def _pack_small(p):
    rows = [p[n].reshape(1, D_MODEL) for n in SMALL_VECTORS]
    rows.append(jnp.concatenate([p["attn_group_norm"].reshape(1, -1), p["conv_group_norm"].reshape(1, -1)], axis=1))
    cw = p["conv_w"].reshape(CONV_K, -1)
    rows.append(jnp.pad(cw, ((0, 1), (0, CONV_WIDTH - cw.shape[1]))).reshape(2, D_MODEL))
    last = jnp.concatenate([p["attn_sinks"].reshape(1, 8), p.get("loss_sum", jnp.zeros((1, 1), F32))], axis=1)
    rows.append(jnp.pad(last, ((0, 0), (0, D_MODEL - 9))))
    return jnp.concatenate(rows, axis=0)


def _unpack_small(packed, conv_width):
    out = {n: packed[i:i + 1] for i, n in enumerate(SMALL_VECTORS)}
    out["attn_group_norm"] = packed[4:5, :Q_WIDTH]
    out["conv_group_norm"] = packed[4:5, Q_WIDTH:]
    out["conv_w"] = packed[5:7].reshape(4, CONV_WIDTH)[:CONV_K, :conv_width].reshape(1, CONV_K, conv_width)
    out["attn_sinks"] = packed[7:8, :8]
    out["loss_sum"] = packed[7, 8]
    return out


WEIGHT_ORDER = ("pre_mix_norm", "w_in", "conv_w", "attn_sinks", "attn_group_norm", "conv_group_norm", "w_out",
                "post_mix_norm", "pre_mlp_norm", "w_up", "w_down", "post_mlp_norm")


def kernel(x, pre_mix_norm, w_in, conv_w, attn_sinks, attn_group_norm, conv_group_norm, w_out, post_mix_norm, pre_mlp_norm, w_up, w_down, post_mlp_norm, loss_target, m_pre_mix_norm, m_w_in, m_conv_w, m_attn_sinks, m_attn_group_norm, m_conv_group_norm, m_w_out, m_post_mix_norm, m_pre_mlp_norm, m_w_up, m_w_down, m_post_mlp_norm, v_pre_mix_norm, v_w_in, v_conv_w, v_attn_sinks, v_attn_group_norm, v_conv_group_norm, v_w_out, v_post_mix_norm, v_pre_mlp_norm, v_w_up, v_w_down, v_post_mlp_norm):
    w = dict(pre_mix_norm=pre_mix_norm, w_in=w_in, conv_w=conv_w, attn_sinks=attn_sinks, attn_group_norm=attn_group_norm,
             conv_group_norm=conv_group_norm, w_out=w_out, post_mix_norm=post_mix_norm, pre_mlp_norm=pre_mlp_norm,
             w_up=w_up, w_down=w_down, post_mlp_norm=post_mlp_norm)
    m = dict(pre_mix_norm=m_pre_mix_norm, w_in=m_w_in, conv_w=m_conv_w, attn_sinks=m_attn_sinks,
             attn_group_norm=m_attn_group_norm, conv_group_norm=m_conv_group_norm, w_out=m_w_out,
             post_mix_norm=m_post_mix_norm, pre_mlp_norm=m_pre_mlp_norm, w_up=m_w_up, w_down=m_w_down,
             post_mlp_norm=m_post_mlp_norm)
    v = dict(pre_mix_norm=v_pre_mix_norm, w_in=v_w_in, conv_w=v_conv_w, attn_sinks=v_attn_sinks,
             attn_group_norm=v_attn_group_norm, conv_group_norm=v_conv_group_norm, w_out=v_w_out,
             post_mix_norm=v_post_mix_norm, pre_mlp_norm=v_pre_mlp_norm, w_up=v_w_up, w_down=v_w_down,
             post_mlp_norm=v_post_mlp_norm)
    core = lax.axis_index("c").astype(jnp.int32).reshape(1)
    chip = 2 * lax.axis_index("x") + lax.axis_index("y")
    local_conv = conv_w.shape[2]
    xs, target = x[0], loss_target[0]
    rope = _rope_inputs(xs.shape[0])

    hb_up, hb_down, hb_out, hb_in = _cast_halves(core, w_up[0], w_down[0], w_out[0], w_in[0].T)
    conv_pad = jnp.pad(conv_w[0], ((0, 8 - CONV_K), (0, 0)))
    wf_in, conv_all = _gather_whole(hb_in, conv_pad)
    conv_full = conv_all[:, :CONV_K, :].transpose(1, 0, 2).reshape(CONV_K, CONV_WIDTH)

    *proj, wf_up, wf_out = _in_proj(xs, pre_mix_norm, wf_in, rope, comm=_merge(_gather_first(hb_up), _gather_first(hb_out)))
    q, kd0, kd1, vd0, vd1, gb, gc, xin, hn = proj
    attn, wf_up, wf_out, wf_down = _attention_fwd(
        q, kd0, kd1, vd0, vd1, attn_sinks,
        comm=_merge(_gather_second(wf_up), _gather_second(wf_out), _gather_first(hb_down)))
    mix, mixed, wf_down = _mix_out(attn, gb, gc, xin, conv_full, attn_group_norm, conv_group_norm, wf_out,
                                   comm=_gather_second(wf_down))
    up, hn2, dout, dmlp, loss_sum, dg_post_mlp = _mlp_loss(xs, mix, target, post_mix_norm, pre_mlp_norm, post_mlp_norm,
                                                           wf_up, wf_down)

    dup, dh, dmix, dg_pre_mlp, dg_post_mix = _mlp_bwd(dmlp, up, xs, dout, mix, pre_mlp_norm, post_mix_norm, wf_up, wf_down)
    n_k = _wgrad_grid(xs.shape[0], True, H_DOWN)[3]
    g_down, dattn, dgb, dy, dg_attn, dg_conv, dconv_w = _wgrad(
        "wgrad_down", up, dmlp, per_chip=True, h_rows=H_DOWN, square_a=True,
        rider=_mix_bwd(dmix, attn, gb, gc, xin, conv_full, attn_group_norm, conv_group_norm, wf_out, n_k))
    g_up, got_down = _wgrad("wgrad_up", hn2, dup, per_chip=True, h_rows=H_UP, comm=_pair_send(g_down))
    p_down = _pair_sum("pair_sum_down", core, g_down, got_down)
    g_out, got_up = _wgrad("wgrad_out", mixed, dmix, per_chip=False, h_rows=H_OUT, comm=_pair_send(g_up))
    p_up = _pair_sum("pair_sum_up", core, g_up, got_up)
    dq, dk0, dk1, dv0, dv1, dsink, ex_down, ex_up, got_out = _attention_bwd(
        q, dattn, attn, kd0, kd1, vd0, vd1, attn_sinks,
        comm=_merge(_chip_exchange(p_down), _chip_exchange(p_up), _pair_send(g_out)))
    p_out = _pair_sum("pair_sum_out", core, g_out, got_out)
    dproj, grad_x, dg_pre_mix = _in_proj_bwd(dq, dk0, dk1, dv0, dv1, dgb, dy, gc, xin, conv_full, xs, dh, pre_mix_norm,
                                             wf_in, rope)
    g_in, ex_out = _wgrad("wgrad_in", dproj, hn, per_chip=False, h_rows=H_IN, comm=_chip_exchange(p_out))
    small = dict(pre_mix_norm=dg_pre_mix, conv_w=dconv_w, attn_sinks=dsink[:, :8], attn_group_norm=dg_attn,
                 conv_group_norm=dg_conv, post_mix_norm=dg_post_mix, pre_mlp_norm=dg_pre_mlp, post_mlp_norm=dg_post_mlp,
                 loss_sum=loss_sum)
    r_down, r_up, r_out, r_in, small_total = _tail_reduce(g_in, [ex_down, ex_up, ex_out], _pack_small(small))

    out_g, out_d, out_m, out_v = {}, {}, {}, {}
    out_g["w_up"], out_d["w_up"], out_m["w_up"], out_v["w_up"] = _adamw_rows(
        "adamw_up", r_up, w_up[0], m_w_up[0], v_w_up[0], 256)
    out_g["w_down"], out_d["w_down"], out_m["w_down"], out_v["w_down"] = _adamw_rows(
        "adamw_down", r_down, w_down[0], m_w_down[0], v_w_down[0], 256)
    out_g["w_out"], out_d["w_out"], out_m["w_out"], out_v["w_out"] = _adamw_rows(
        "adamw_out", r_out, w_out[0], m_w_out[0], v_w_out[0], H_OUT)
    out_g["w_in"] = r_in.reshape(2 * H_IN, D_MODEL).T
    out_d["w_in"], out_m["w_in"], out_v["w_in"] = _adamw_plain("adamw_in", w_in[0], out_g["w_in"], m_w_in[0], v_w_in[0])

    small_sum = _unpack_small(small_total, CONV_WIDTH)
    loss = small_sum["loss_sum"] * (0.5 / D_MODEL)
    small_sum["conv_w"] = lax.dynamic_slice_in_dim(small_sum["conv_w"], chip * local_conv, local_conv, axis=2)
    packed = [_pack_small({n: t[n] for n in SMALL_NAMES}) for t in (w, small_sum, m, v)]
    small_d, small_m, small_v = [_unpack_small(t, local_conv) for t in _adamw_small(*packed)]
    for n in SMALL_NAMES:
        out_g[n], out_d[n], out_m[n], out_v[n] = small_sum[n], small_d[n], small_m[n], small_v[n]

    def shaped(d):
        return [d[n].reshape(w[n].shape) for n in WEIGHT_ORDER]

    return (loss, grad_x[None], *shaped(out_g), *shaped(out_d), *shaped(out_m), *shaped(out_v))
```

```python
import math
from typing import Callable, NamedTuple

import jax
import jax.numpy as jnp
import numpy as np
from jax import lax
from jax.experimental import pallas as pl
from jax.experimental.pallas import tpu as pltpu

F32 = jnp.float32
BF16 = jnp.bfloat16

D_MODEL = 1024
HEAD_DIM = 64
Q_WIDTH = 512
KV_WIDTH = 128
CONV_WIDTH = 512
CONV_K = 3
D_FF = 4096
IN_COLS = 2304
QBLOCK = 128
ROT_DIM = 16
ROPE_THETA = 500000.0
NORM_EPS = 1e-6
NEG_INF = -1e30
N_CHIPS = 4

ADAM_LR = 0.001
ADAM_B1 = 0.9
ADAM_B2 = 0.999
ADAM_EPS = 1e-08
ADAM_WD = 0.01
ADAM_STEP = 10

H_UP, H_DOWN, H_OUT, H_IN = 512, 512, 128, 288

TOKEN_TILE = 512
WIDE_TOKEN_TILE = 1024
MLP_BWD_TOKEN_TILE = 512
MLP_BWD_SUB_TILE = 256
ATTN_FWD_BLOCKS = 16
ATTN_BWD_BLOCKS = 2
WGRAD_TOKEN_TILE = 2048
VMEM_LIMIT_V7X = 56 * 1024 * 1024

MESH = pl.DeviceIdType.MESH
ANY = pl.BlockSpec(memory_space=pl.ANY)
VMEM_WHOLE = pl.BlockSpec(memory_space=pltpu.VMEM)
SDS = jax.ShapeDtypeStruct


def _resident(shape):
    zeros = (0,) * len(shape)
    return pl.BlockSpec(shape, lambda *_: zeros, pipeline_mode=pl.Buffered(1))


def _rms(v):
    return lax.rsqrt(jnp.mean(v * v, axis=-1, keepdims=True) + NORM_EPS)


def _norm_bwd(dy, gain, vhat, rstd):
    t = dy * gain
    return rstd * (t - vhat * jnp.mean(t * vhat, axis=-1, keepdims=True))


def _colsum(v):
    return jnp.sum(v, axis=0, keepdims=True)


def _dot_nt(a, b):
    return lax.dot_general(a, b, (((1,), (1,)), ((), ())), preferred_element_type=F32)


def _dot_tn(a, b):
    return lax.dot_general(a, b, (((0,), (0,)), ((), ())), preferred_element_type=F32)


def _dot(a, b):
    return jnp.dot(a, b, preferred_element_type=F32)


def _chip_block(w_ref, chip):
    both = w_ref[pl.ds(2 * chip, 2)]
    return both.reshape(2 * both.shape[1], both.shape[2])


def _lane_lt64(shape):
    return lax.broadcasted_iota(jnp.int32, shape, 1) < HEAD_DIM


class _Comm(NamedTuple):
    operands: tuple
    out_shapes: tuple
    aliases: dict
    n_remote: int
    n_local: int
    plan: Callable


def _merge(*comms):
    operands, out_shapes, aliases, parts = [], [], {}, []
    n_remote = n_local = 0
    for cm in comms:
        parts.append((len(operands), len(out_shapes), n_remote, n_local, cm))
        for k, v in cm.aliases.items():
            aliases[len(operands) + k] = len(out_shapes) + v
        operands += cm.operands
        out_shapes += cm.out_shapes
        n_remote += cm.n_remote
        n_local += cm.n_local

    def plan(ins, outs, send, recv, loc):
        sends, recvs, locs = [], [], []
        for i0, o0, r0, l0, cm in parts:
            s, r, l = cm.plan(ins[i0:i0 + len(cm.operands)], outs[o0:o0 + len(cm.out_shapes)],
                              lambda k, r0=r0: send(r0 + k), lambda k, r0=r0: recv(r0 + k), lambda k, l0=l0: loc(l0 + k))
            sends, recvs, locs = sends + s, recvs + r, locs + l
        return sends, recvs, locs

    return _Comm(tuple(operands), tuple(out_shapes), aliases, n_remote, n_local, plan)


def _sem_scratch(comm):
    return [pltpu.SemaphoreType.DMA((max(comm.n_remote, 1),)), pltpu.SemaphoreType.DMA((max(comm.n_remote, 1),)),
            pltpu.SemaphoreType.DMA((max(comm.n_local, 1),))]


class _Rider(NamedTuple):
    body: Callable
    in_specs: list
    out_specs: list
    out_shape: list
    operands: tuple


def _pallas(body, *, name, grid, in_specs, out_specs, out_shape, operands, scratch=(), comm=None, rider=None):
    params = pltpu.CompilerParams(dimension_semantics=("arbitrary",) * len(grid), vmem_limit_bytes=VMEM_LIMIT_V7X)
    if rider is not None:
        own_in, own_out, ride_in, ride_out = len(in_specs), len(out_specs), len(rider.in_specs), len(rider.out_specs)
        own_body = body

        def body(*refs):
            o0 = own_in + ride_in
            s0 = o0 + own_out + ride_out
            own_body(*refs[:own_in], *refs[o0:o0 + own_out], *refs[s0:])
            first = None
            for axis in range(len(grid)):
                at_start = pl.program_id(axis) == 0
                first = at_start if first is None else jnp.logical_and(first, at_start)
            rider.body(first, *refs[own_in:o0], *refs[o0 + own_out:s0])

        in_specs, out_specs = list(in_specs) + rider.in_specs, list(out_specs) + rider.out_specs
        out_shape, operands = list(out_shape) + rider.out_shape, tuple(operands) + tuple(rider.operands)
    if comm is None:
        return pl.pallas_call(body, name=name, grid=grid, in_specs=in_specs, out_specs=out_specs, out_shape=out_shape,
                              scratch_shapes=list(scratch), compiler_params=params)(*operands)
    n_in, n_out, n_scr = len(in_specs), len(out_specs), len(scratch)
    c_in, c_out = len(comm.operands), len(comm.out_shapes)

    def with_comm(*refs):
        ins, c_ins = refs[:n_in], refs[n_in:n_in + c_in]
        o0 = n_in + c_in
        outs, c_outs = refs[o0:o0 + n_out], refs[o0 + n_out:o0 + n_out + c_out]
        s0 = o0 + n_out + c_out
        scr = refs[s0:s0 + n_scr]
        send_sems, recv_sems, local_sems = refs[s0 + n_scr:]
        first = last = None
        for axis, size in enumerate(grid):
            at_start, at_end = pl.program_id(axis) == 0, pl.program_id(axis) == size - 1
            first = at_start if first is None else jnp.logical_and(first, at_start)
            last = at_end if last is None else jnp.logical_and(last, at_end)

        def copies():
            return comm.plan(c_ins, c_outs, lambda k: send_sems.at[k], lambda k: recv_sems.at[k],
                             lambda k: local_sems.at[k])

        @pl.when(first)
        def _():
            sends, _, locs = copies()
            for cp in sends + locs:
                cp.start()

        body(*ins, *outs, *scr)

        @pl.when(last)
        def _():
            sends, recvs, locs = copies()
            for cp in recvs:
                cp.wait_recv()
            for cp in sends:
                cp.wait_send()
            for cp in locs:
                cp.wait()

    return pl.pallas_call(
        with_comm, name=name, grid=grid,
        in_specs=list(in_specs) + [ANY] * c_in, out_specs=list(out_specs) + [ANY] * c_out,
        out_shape=list(out_shape) + list(comm.out_shapes),
        scratch_shapes=list(scratch) + _sem_scratch(comm),
        input_output_aliases={n_in + k: n_out + v for k, v in comm.aliases.items()},
        compiler_params=params)(*operands, *comm.operands)


def _place():
    return lax.axis_index("x"), lax.axis_index("y"), lax.axis_index("c")


def _other_chips(x, y):
    return [(1 - x, y), (x, 1 - y), (1 - x, 1 - y)]


def _slot(px, py, pc):
    return 4 * px + 2 * py + pc


def _remote(src, dst, send_sem, recv_sem, to):
    return pltpu.make_async_remote_copy(src_ref=src, dst_ref=dst, send_sem=send_sem, recv_sem=recv_sem,
                                        device_id=to, device_id_type=MESH)


def _gather_first(half_block):
    def plan(ins, outs, send, recv, loc):
        (blk,), (full,) = ins, outs
        x, y, c = _place()
        chips = _other_chips(x, y)
        mine = full.at[_slot(x, y, c)]
        sends = [_remote(blk, mine, send(0), recv(0), (x, y, 1 - c))]
        sends += [_remote(blk, mine, send(1 + j), recv(1 + j), (*chip, c)) for j, chip in enumerate(chips)]
        recvs = [_remote(blk, full.at[_slot(x, y, 1 - c)], send(0), recv(0), (x, y, 1 - c))]
        recvs += [_remote(blk, full.at[_slot(*chip, c)], send(1 + j), recv(1 + j), (*chip, c))
                  for j, chip in enumerate(chips)]
        return sends, recvs, [pltpu.make_async_copy(blk, mine, loc(0))]

    return _Comm((half_block,), (SDS((2 * N_CHIPS,) + half_block.shape, half_block.dtype),), {}, 4, 1, plan)


def _gather_second(partly_gathered):
    def plan(ins, outs, send, recv, loc):
        (src,), (full,) = ins, outs
        x, y, c = _place()
        chips = _other_chips(x, y)
        sends = [_remote(src.at[_slot(*chip, c)], full.at[_slot(*chip, c)], send(j), recv(j), (x, y, 1 - c))
                 for j, chip in enumerate(chips)]
        recvs = [_remote(src.at[_slot(*chip, 1 - c)], full.at[_slot(*chip, 1 - c)], send(j), recv(j), (x, y, 1 - c))
                 for j, chip in enumerate(chips)]
        return sends, recvs, []

    return _Comm((partly_gathered,), (SDS(partly_gathered.shape, partly_gathered.dtype),), {0: 0}, 3, 0, plan)


def _gather_whole(half_block, small_block):
    def body(blk_ref, small_ref, out_ref, small_out_ref, send_sems, recv_sems, local_sems):
        x, y, c = _place()
        me, sibling = (x, y, c), (x, y, 1 - c)
        chips = _other_chips(x, y)

        def copy(k, block, to, src=None):
            return _remote(out_ref.at[_slot(*block)] if src is None else src, out_ref.at[_slot(*block)],
                           send_sems.at[k], recv_sems.at[k], to)

        def small_copy(k, chip, to):
            return _remote(small_ref, small_out_ref.at[2 * chip[0] + chip[1]], send_sems.at[7 + k], recv_sems.at[7 + k], to)

        mine = pltpu.make_async_copy(blk_ref, out_ref.at[_slot(*me)], local_sems.at[0])
        mine_small = pltpu.make_async_copy(small_ref, small_out_ref.at[2 * x + y], local_sems.at[1])
        mine.start()
        mine_small.start()
        first = [copy(0, me, sibling, src=blk_ref)]
        first += [copy(1 + j, me, (*chip, c), src=blk_ref) for j, chip in enumerate(chips)]
        first += [small_copy(j, (x, y), (*chip, c)) for j, chip in enumerate(chips)]
        for cp in first:
            cp.start()
        passed = [copy(4 + j, (*chip, c), sibling) for j, chip in enumerate(chips)]
        for j, chip in enumerate(chips):
            copy(1 + j, (*chip, c), me).wait_recv()
            passed[j].start()
        copy(0, sibling, me).wait_recv()
        for j, chip in enumerate(chips):
            copy(4 + j, (*chip, 1 - c), me).wait_recv()
            small_copy(j, chip, me).wait_recv()
        for cp in first + passed:
            cp.wait_send()
        mine.wait()
        mine_small.wait()

    return pl.pallas_call(
        body, name="gather_whole", in_specs=[ANY, ANY], out_specs=[ANY, ANY],
        out_shape=[SDS((2 * N_CHIPS,) + half_block.shape, half_block.dtype),
                   SDS((N_CHIPS,) + small_block.shape, small_block.dtype)],
        scratch_shapes=[pltpu.SemaphoreType.DMA((10,)), pltpu.SemaphoreType.DMA((10,)), pltpu.SemaphoreType.DMA((2,))],
    )(half_block, small_block)


def _pair_send(grads):
    def plan(ins, outs, send, recv, loc):
        (g,), (got,) = ins, outs
        x, y, c = _place()
        copies = [_remote(g.at[j, 1 - c], got.at[j], send(j), recv(j), (x, y, 1 - c)) for j in range(N_CHIPS)]
        return copies, copies, []

    shape = (grads.shape[0],) + grads.shape[2:]
    return _Comm((grads,), (SDS(shape, grads.dtype),), {}, N_CHIPS, 0, plan)


def _chip_exchange(partial):
    def plan(ins, outs, send, recv, loc):
        (p,), (got,) = ins, outs
        x, y, c = _place()
        my_chip = 2 * x + y
        chips = _other_chips(x, y)
        sends = [_remote(p.at[2 * chip[0] + chip[1]], got.at[my_chip], send(j), recv(j), (*chip, c))
                 for j, chip in enumerate(chips)]
        recvs = [_remote(p.at[my_chip], got.at[2 * chip[0] + chip[1]], send(j), recv(j), (*chip, c))
                 for j, chip in enumerate(chips)]
        return sends, recvs, [pltpu.make_async_copy(p.at[my_chip], got.at[my_chip], loc(0))]

    return _Comm((partial,), (SDS(partial.shape, partial.dtype),), {}, 3, 1, plan)


def _pair_sum(name, core, grads, received):
    h = grads.shape[2]

    def body(core_ref, g_ref, r_ref, o_ref):
        o_ref[...] = (g_ref[0] + r_ref[...]).astype(BF16)

    return pl.pallas_call(
        body, name=name,
        grid_spec=pltpu.PrefetchScalarGridSpec(
            num_scalar_prefetch=1, grid=(N_CHIPS,),
            in_specs=[pl.BlockSpec((1, 1, h, D_MODEL), lambda j, core_ref: (j, core_ref[0], 0, 0)),
                      pl.BlockSpec((1, h, D_MODEL), lambda j, core_ref: (j, 0, 0))],
            out_specs=pl.BlockSpec((1, h, D_MODEL), lambda j, core_ref: (j, 0, 0))),
        out_shape=SDS((N_CHIPS, h, D_MODEL), BF16),
        compiler_params=pltpu.CompilerParams(dimension_semantics=("arbitrary",), vmem_limit_bytes=VMEM_LIMIT_V7X),
    )(core, grads, received)


SMALL_ROWS = 8


def _sum_blocks(ref):
    return (ref[0].astype(F32) + ref[1].astype(F32)) + (ref[2].astype(F32) + ref[3].astype(F32))


def _tail_reduce(last_grads, exchanged, small):
    n = len(exchanged)
    h = last_grads.shape[2]

    def body(*refs):
        g_ref, ex, small_ref = refs[0], refs[1:1 + n], refs[1 + n]
        o0 = 2 + n
        out, out_last, small_out = refs[o0:o0 + n], refs[o0 + n], refs[o0 + n + 1]
        s0 = o0 + n + 2
        halves, half_last = refs[s0:s0 + n], refs[s0 + n]
        own, got, part, exch, small_buf = refs[s0 + n + 1:s0 + n + 6]
        pair_send, pair_recv, chip_send, chip_recv, share_send, share_recv, small_send, small_recv, local_sems = refs[s0 + n + 6:]
        x, y, c = _place()
        sibling = (x, y, 1 - c)
        my_chip, me = 2 * x + y, _slot(x, y, c)
        chips = _other_chips(x, y)

        to_sibling = [_remote(g_ref.at[j, 1 - c], got.at[j], pair_send.at[j], pair_recv.at[j], sibling)
                      for j in range(N_CHIPS)]
        load_own = [pltpu.make_async_copy(g_ref.at[j, c], own.at[j], local_sems.at[j]) for j in range(N_CHIPS)]
        for cp in to_sibling + load_own:
            cp.start()

        small_buf[me] = small_ref[...]
        small_copies = []
        for mask in range(1, 8):
            peer = (x ^ (mask >> 2), y ^ ((mask >> 1) & 1), c ^ (mask & 1))
            small_copies.append(_remote(small_ref, small_buf.at[me], small_send.at[mask - 1], small_recv.at[mask - 1], peer))
        for cp in small_copies:
            cp.start()

        def share(k, half_ref, out_ref):
            keep = pltpu.make_async_copy(half_ref, out_ref.at[c], local_sems.at[N_CHIPS + k])
            give = _remote(half_ref, out_ref.at[c], share_send.at[k], share_recv.at[k], sibling)
            take = _remote(half_ref, out_ref.at[1 - c], share_send.at[k], share_recv.at[k], sibling)
            keep.start()
            give.start()
            return keep, give, take

        shares = []
        for k in range(n):
            halves[k][...] = _sum_blocks(ex[k])
            shares.append(share(k, halves[k], out[k]))

        for cp in to_sibling:
            cp.wait_recv()
        for cp in load_own:
            cp.wait()
        part[...] = (own[...] + got[...]).astype(BF16)
        exch[my_chip] = part[my_chip]
        to_chips = [_remote(part.at[2 * chip[0] + chip[1]], exch.at[my_chip], chip_send.at[j], chip_recv.at[j], (*chip, c))
                    for j, chip in enumerate(chips)]
        from_chips = [_remote(part.at[my_chip], exch.at[2 * chip[0] + chip[1]], chip_send.at[j], chip_recv.at[j], (*chip, c))
                      for j, chip in enumerate(chips)]
        for cp in to_chips:
            cp.start()

        for cp in small_copies:
            cp.wait_recv()
        total = small_buf[0]
        for d in range(1, 8):
            total = total + small_buf[d]
        small_out[...] = total

        for cp in from_chips:
            cp.wait_recv()
        half_last[...] = _sum_blocks(exch)
        shares.append(share(n, half_last, out_last))

        for keep, give, take in shares:
            take.wait_recv()
            give.wait_send()
            keep.wait()
        for cp in to_sibling + to_chips + small_copies:
            cp.wait_send()

    blocks = (N_CHIPS, h, D_MODEL)
    return pl.pallas_call(
        body, name="tail_reduce",
        in_specs=[ANY] + [VMEM_WHOLE] * (n + 1), out_specs=[ANY] * (n + 1) + [VMEM_WHOLE],
        out_shape=[SDS((2,) + e.shape[1:], F32) for e in exchanged] + [SDS((2, h, D_MODEL), F32), SDS(small.shape, F32)],
        scratch_shapes=[pltpu.VMEM(e.shape[1:], F32) for e in exchanged] + [pltpu.VMEM((h, D_MODEL), F32)]
                       + [pltpu.VMEM(blocks, F32), pltpu.VMEM(blocks, F32), pltpu.VMEM(blocks, BF16), pltpu.VMEM(blocks, BF16),
                          pltpu.VMEM((8,) + small.shape, F32)]
                       + [pltpu.SemaphoreType.DMA((N_CHIPS,)), pltpu.SemaphoreType.DMA((N_CHIPS,)),
                          pltpu.SemaphoreType.DMA((3,)), pltpu.SemaphoreType.DMA((3,)),
                          pltpu.SemaphoreType.DMA((n + 1,)), pltpu.SemaphoreType.DMA((n + 1,)),
                          pltpu.SemaphoreType.DMA((7,)), pltpu.SemaphoreType.DMA((7,)),
                          pltpu.SemaphoreType.DMA((N_CHIPS + n + 1,))],
        compiler_params=pltpu.CompilerParams(vmem_limit_bytes=VMEM_LIMIT_V7X),
    )(last_grads, *exchanged, small)


def _rope_expansion():
    half = ROT_DIM // 2
    expand = np.zeros((2 * half, 3 * 128), np.float32)
    const = np.zeros((1, 3 * 128), np.float32)
    for lane in range(128):
        d = lane % HEAD_DIM
        if d < ROT_DIM:
            expand[d % half, lane] = 1.0
        else:
            const[0, lane] = 1.0
        if d < half:
            expand[half + d, 128 + lane] = -1.0
        elif d < ROT_DIM:
            expand[half + d - half, 256 + lane] = 1.0
    return expand, const


ROPE_PIECES = 3 * ROT_DIM


def _rope_inputs(seq):
    pos = jnp.arange(seq, dtype=F32)
    inv_freq = ROPE_THETA ** (-jnp.arange(0, ROT_DIM, 2, dtype=F32) / ROT_DIM)
    ang = pos[:, None] * inv_freq[None, :]
    cs = jnp.concatenate([jnp.cos(ang), jnp.sin(ang)], axis=1)
    hi = lax.reduce_precision(cs, 8, 7)
    mid = lax.reduce_precision(cs - hi, 8, 7)
    low = cs - hi - mid
    expand, const = _rope_expansion()
    pieces = jnp.concatenate([hi, mid, low], axis=1).astype(BF16)
    return pieces, jnp.asarray(np.concatenate([expand] * 3, axis=0), BF16), jnp.asarray(const)


def _rope_specs(tb):
    return [pl.BlockSpec((tb, ROPE_PIECES), lambda i: (i, 0)), _resident((ROPE_PIECES, 3 * 128)), _resident((1, 3 * 128))]


def _rope_tile(pieces_ref, expand_ref, const_ref):
    tables = _dot(pieces_ref[...], expand_ref[...]) + const_ref[...]
    return tables[:, 0:128], tables[:, 128:256], tables[:, 256:384]


def _rope(t, c, sa, sb):
    half = ROT_DIM // 2
    return t * c + pltpu.roll(t, 128 - half, 1) * sa + pltpu.roll(t, half, 1) * sb


def _rope_transposed(dt, c, sa, sb):
    half = ROT_DIM // 2
    return dt * c + pltpu.roll(dt * sa, half, 1) + pltpu.roll(dt * sb, 128 - half, 1)


def _cast_halves(core, w_up, w_down, w_out, w_in_t):
    def body(core_ref, up_ref, down_ref, out_ref, in_ref, up_o, down_o, out_o, in_o):
        up_o[...] = up_ref[...].astype(BF16)
        down_o[...] = down_ref[...].astype(BF16)
        out_o[...] = out_ref[...].astype(BF16)
        in_o[...] = in_ref[...].astype(BF16)

    half = lambda rows: pl.BlockSpec((rows, D_MODEL), lambda i, core_ref: (core_ref[0], 0))
    whole = lambda rows: pl.BlockSpec((rows, D_MODEL), lambda i, core_ref: (0, 0))
    rows = (H_UP, H_DOWN, H_OUT, H_IN)
    return pl.pallas_call(
        body, name="cast_halves",
        grid_spec=pltpu.PrefetchScalarGridSpec(
            num_scalar_prefetch=1, grid=(1,), in_specs=[half(r) for r in rows], out_specs=[whole(r) for r in rows]),
        out_shape=[SDS((r, D_MODEL), BF16) for r in rows],
        compiler_params=pltpu.CompilerParams(dimension_semantics=("arbitrary",), vmem_limit_bytes=VMEM_LIMIT_V7X),
    )(core, w_up, w_down, w_out, w_in_t)


def _in_proj(x, g_pre, w_in_t, rope, comm=None):
    seq = x.shape[0]
    tb = min(seq, WIDE_TOKEN_TILE)

    def body(x_ref, g_ref, w_ref, c_ref, sa_ref, sb_ref,
             q_ref, kd0_ref, kd1_ref, vd0_ref, vd1_ref, gb_ref, gc_ref, xin_ref, hn_ref):
        xv = x_ref[...]
        hn = (xv * _rms(xv) * g_ref[...]).astype(BF16)
        hn_ref[...] = hn
        proj = _dot_nt(hn, w_ref[...].reshape(IN_COLS, D_MODEL))
        c, sa, sb = _rope_tile(c_ref, sa_ref, sb_ref)
        scale = 1.0 / math.sqrt(HEAD_DIM)
        for p in range(Q_WIDTH // 128):
            q_ref[:, 128 * p:128 * (p + 1)] = (_rope(proj[:, 128 * p:128 * (p + 1)], c, sa, sb) * scale).astype(BF16)
        k = _rope(proj[:, Q_WIDTH:Q_WIDTH + KV_WIDTH], c, sa, sb)
        v = proj[:, Q_WIDTH + KV_WIDTH:Q_WIDTH + 2 * KV_WIDTH]
        low = _lane_lt64(k.shape)
        k_sw, v_sw = pltpu.roll(k, HEAD_DIM, 1), pltpu.roll(v, HEAD_DIM, 1)
        kd0_ref[...] = jnp.where(low, k, k_sw).astype(BF16)
        kd1_ref[...] = jnp.where(low, k_sw, k).astype(BF16)
        vd0_ref[...] = jnp.where(low, v, v_sw).astype(BF16)
        vd1_ref[...] = jnp.where(low, v_sw, v).astype(BF16)
        base = Q_WIDTH + 2 * KV_WIDTH
        gb_ref[...] = proj[:, base:base + CONV_WIDTH].astype(BF16)
        gc_ref[...] = proj[:, base + CONV_WIDTH:base + 2 * CONV_WIDTH].astype(BF16)
        xin_ref[...] = proj[:, base + 2 * CONV_WIDTH:base + 3 * CONV_WIDTH].astype(BF16)

    tile = lambda w: pl.BlockSpec((tb, w), lambda i: (i, 0))
    return _pallas(
        body, name="in_proj", grid=(seq // tb,),
        in_specs=[tile(D_MODEL), _resident((1, D_MODEL)), _resident(w_in_t.shape), *_rope_specs(tb)],
        out_specs=[tile(Q_WIDTH), tile(128), tile(128), tile(128), tile(128),
                   tile(CONV_WIDTH), tile(CONV_WIDTH), tile(CONV_WIDTH), tile(D_MODEL)],
        out_shape=[SDS((seq, Q_WIDTH), BF16)] + [SDS((seq, 128), BF16)] * 4
                  + [SDS((seq, CONV_WIDTH), BF16)] * 3 + [SDS((seq, D_MODEL), BF16)],
        operands=(x, g_pre, w_in_t, *rope), comm=comm)


def _attn_valid(i):
    shape = (4 * QBLOCK, 2 * QBLOCK)
    row = lax.broadcasted_iota(jnp.int32, shape, 0)
    col = lax.broadcasted_iota(jnp.int32, shape, 1)
    qi = row & (QBLOCK - 1)
    return (col > qi) & (col <= qi + QBLOCK) & ((col >= QBLOCK) | (i > 0))


def _stack_heads(pair0, pair1):
    low = _lane_lt64(pair0.shape)
    zero = jnp.zeros_like(pair0)
    return jnp.concatenate([jnp.where(low, pair0, zero), jnp.where(low, zero, pair0),
                            jnp.where(low, pair1, zero), jnp.where(low, zero, pair1)], axis=0)


def _unstack_heads(stacked):
    low = _lane_lt64((QBLOCK, 128))
    pair0 = jnp.where(low, stacked[0:QBLOCK], stacked[QBLOCK:2 * QBLOCK])
    pair1 = jnp.where(low, stacked[2 * QBLOCK:3 * QBLOCK], stacked[3 * QBLOCK:4 * QBLOCK])
    return pair0, pair1


def _sink_column(sink_ref, kv_head):
    row = lax.broadcasted_iota(jnp.int32, (4 * QBLOCK, 1), 0)
    s = [sink_ref[0, 4 * kv_head + j] for j in range(4)]
    return jnp.where(row < QBLOCK, s[0], jnp.where(row < 2 * QBLOCK, s[1], jnp.where(row < 3 * QBLOCK, s[2], s[3])))


def _band(ref, i):
    prev = pl.multiple_of(jnp.maximum(i - 1, 0) * QBLOCK, QBLOCK)
    own = pl.multiple_of(i * QBLOCK, QBLOCK)
    return jnp.concatenate([ref[pl.ds(prev, QBLOCK), :], ref[pl.ds(own, QBLOCK), :]], axis=0), prev, own


def _softmax_with_sink(s, sink_col):
    m = jnp.maximum(jnp.max(s, axis=-1, keepdims=True), sink_col)
    p = jnp.exp(s - m)
    e_sink = jnp.exp(sink_col - m)
    inv_l = 1.0 / (jnp.sum(p, axis=-1, keepdims=True) + e_sink)
    return p, e_sink, inv_l


def _attention_fwd(q, kd0, kd1, vd0, vd1, sinks, comm=None):
    seq = q.shape[0]

    nb = ATTN_FWD_BLOCKS

    def body(sink_ref, q_ref, kd0_ref, kd1_ref, vd0_ref, vd1_ref, o_ref):
        for b in range(nb):
            i = pl.program_id(0) * nb + b
            rows = slice(QBLOCK * b, QBLOCK * (b + 1))
            valid = _attn_valid(i)
            for kv_head, (k_ref, v_ref) in enumerate(((kd0_ref, vd0_ref), (kd1_ref, vd1_ref))):
                kband, _, _ = _band(k_ref, i)
                vband, _, _ = _band(v_ref, i)
                base = 256 * kv_head
                qm = _stack_heads(q_ref[rows, base:base + 128], q_ref[rows, base + 128:base + 256])
                s = jnp.where(valid, _dot_nt(qm, kband), NEG_INF)
                p, _, inv_l = _softmax_with_sink(s, _sink_column(sink_ref, kv_head))
                o = _dot(p.astype(BF16), vband) * inv_l
                pair0, pair1 = _unstack_heads(o)
                o_ref[rows, base:base + 128] = pair0.astype(BF16)
                o_ref[rows, base + 128:base + 256] = pair1.astype(BF16)

    blk = pl.BlockSpec((nb * QBLOCK, Q_WIDTH), lambda i: (i, 0))
    full = _resident((seq, 128))
    return _pallas(
        body, name="attention_fwd", grid=(seq // (nb * QBLOCK),),
        in_specs=[pl.BlockSpec(memory_space=pltpu.SMEM), blk, full, full, full, full],
        out_specs=[blk], out_shape=[SDS((seq, Q_WIDTH), BF16)],
        operands=(sinks, q, kd0, kd1, vd0, vd1), comm=comm)


HALO = 16


def _conv_parts(gc, xin, gc_halo, xin_halo, conv_w, first):
    tb = gc.shape[0]
    u = gc.astype(F32) * xin.astype(F32)
    u_halo = jnp.where(first, 0.0, gc_halo.astype(F32) * xin_halo.astype(F32))
    ext = jnp.concatenate([u_halo, u], axis=0)
    u1 = pltpu.roll(ext, 1, 0)[HALO:HALO + tb]
    u2 = pltpu.roll(ext, 2, 0)[HALO:HALO + tb]
    y = conv_w[0:1, :] * u2 + conv_w[1:2, :] * u1 + conv_w[2:3, :] * u
    return u, u1, u2, y


def _halo_prev(tb, w):
    return pl.BlockSpec((HALO, w), lambda i: (jnp.maximum(i * (tb // HALO) - 1, 0), 0))


def _residual_mid(x, mix, g_post_mix):
    mix_f = mix.astype(F32)
    return x + mix_f * _rms(mix_f) * g_post_mix


def _mix_out(attn, gb, gc, xin, conv_w, g_attn, g_conv, w_out, comm=None):
    seq = attn.shape[0]
    tb = min(seq, WIDE_TOKEN_TILE)

    def body(a_ref, gb_ref, gc_ref, xin_ref, gch_ref, xinh_ref, cw_ref, ga_ref, gcn_ref, w_ref, mix_ref, mixed_ref):
        first = pl.program_id(0) == 0
        _, _, _, y = _conv_parts(gc_ref[...], xin_ref[...], gch_ref[...], xinh_ref[...], cw_ref[...], first)
        conv = gb_ref[...].astype(F32) * y
        a = a_ref[...].astype(F32)
        mixed_ref[:, 0:Q_WIDTH] = (a * _rms(a) * ga_ref[...]).astype(BF16)
        mixed_ref[:, Q_WIDTH:] = (conv * _rms(conv) * gcn_ref[...]).astype(BF16)
        mix_ref[...] = _dot(mixed_ref[...], w_ref[...].reshape(D_MODEL, D_MODEL)).astype(BF16)

    tile = lambda w: pl.BlockSpec((tb, w), lambda i: (i, 0))
    return _pallas(
        body, name="mix_out", grid=(seq // tb,),
        in_specs=[tile(Q_WIDTH), tile(CONV_WIDTH), tile(CONV_WIDTH), tile(CONV_WIDTH),
                  _halo_prev(tb, CONV_WIDTH), _halo_prev(tb, CONV_WIDTH),
                  _resident((CONV_K, CONV_WIDTH)), _resident((1, Q_WIDTH)), _resident((1, CONV_WIDTH)),
                  _resident(w_out.shape)],
        out_specs=[tile(D_MODEL), tile(D_MODEL)],
        out_shape=[SDS((seq, D_MODEL), BF16), SDS((seq, D_MODEL), BF16)],
        operands=(attn, gb, gc, xin, gc, xin, conv_w, g_attn, g_conv, w_out), comm=comm)


def _mlp_loss(x, mix, target, g_post_mix, g_pre_mlp, g_post_mlp, w_up, w_down):
    seq = x.shape[0]
    tb = TOKEN_TILE

    def body(x_ref, mix_ref, t_ref, gpm_ref, g2_ref, g4_ref, wup_ref, wdown_ref,
             up_ref, hn2_ref, dout_ref, dmlp_ref, loss_ref, dg4_ref, act_ref):
        @pl.when(pl.program_id(0) == 0)
        def _():
            loss_ref[...] = jnp.zeros_like(loss_ref)
            dg4_ref[...] = jnp.zeros_like(dg4_ref)

        halves = [slice(0, tb // 2), slice(tb // 2, tb)]
        hv, hn2 = [], []
        for rows in halves:
            hv.append(_residual_mid(x_ref[rows, :], mix_ref[rows, :], gpm_ref[...]))
            hn2.append((hv[-1] * _rms(hv[-1]) * g2_ref[...]).astype(BF16))
            hn2_ref[rows, :] = hn2[-1]
        for k, rows in enumerate(halves):
            for j in range(N_CHIPS):
                up = _dot(hn2[k], _chip_block(wup_ref, j))
                up = jnp.maximum(up, 0.0)
                up_ref[rows, 1024 * j:1024 * (j + 1)] = up.astype(BF16)
                act_ref[rows, 1024 * j:1024 * (j + 1)] = (up * up).astype(BF16)
        w_down_all = wdown_ref[...].reshape(D_FF, D_MODEL)
        loss = jnp.zeros((1, 1), F32)
        dg4 = jnp.zeros((1, D_MODEL), F32)
        for k, rows in enumerate(halves):
            mlp = _dot(act_ref[rows, :], w_down_all)
            rstd = _rms(mlp)
            zhat = mlp * rstd
            diff = hv[k] + zhat * g4_ref[...] - t_ref[rows, :]
            loss = loss + jnp.sum(jnp.sum(diff * diff, axis=1, keepdims=True), axis=0, keepdims=True)
            dout = diff * (1.0 / D_MODEL)
            dout_ref[rows, :] = dout
            dg4 = dg4 + _colsum(dout * zhat)
            dmlp_ref[rows, :] = _norm_bwd(dout, g4_ref[...], zhat, rstd).astype(BF16)
        loss_ref[...] += loss
        dg4_ref[...] += dg4

    tile = lambda w: pl.BlockSpec((tb, w), lambda i: (i, 0))
    return _pallas(
        body, name="mlp_loss", grid=(seq // tb,),
        in_specs=[tile(D_MODEL), tile(D_MODEL), tile(D_MODEL), _resident((1, D_MODEL)), _resident((1, D_MODEL)),
                  _resident((1, D_MODEL)), _resident(w_up.shape), _resident(w_down.shape)],
        out_specs=[tile(D_FF), tile(D_MODEL), tile(D_MODEL), tile(D_MODEL),
                   pl.BlockSpec((1, 1), lambda i: (0, 0)), pl.BlockSpec((1, D_MODEL), lambda i: (0, 0))],
        out_shape=[SDS((seq, D_FF), BF16), SDS((seq, D_MODEL), BF16), SDS((seq, D_MODEL), F32),
                   SDS((seq, D_MODEL), BF16), SDS((1, 1), F32), SDS((1, D_MODEL), F32)],
        scratch=[pltpu.VMEM((tb, D_FF), BF16)],
        operands=(x, mix, target, g_post_mix, g_pre_mlp, g_post_mlp, w_up, w_down))


def _mlp_bwd(dmlp, up, x, dout, mix, g_pre_mlp, g_post_mix, w_up, w_down):
    seq = x.shape[0]
    tb = MLP_BWD_TOKEN_TILE

    def body(dmlp_ref, up_ref, x_ref, dout_ref, mix_ref, g2_ref, gpm_ref, wup_ref, wdown_ref,
             dup_ref, dh_ref, dmix_ref, dg2_ref, dgpm_ref):
        @pl.when(pl.program_id(0) == 0)
        def _():
            dg2_ref[...] = jnp.zeros_like(dg2_ref)
            dgpm_ref[...] = jnp.zeros_like(dgpm_ref)

        subs = [slice(k * MLP_BWD_SUB_TILE, (k + 1) * MLP_BWD_SUB_TILE) for k in range(tb // MLP_BWD_SUB_TILE)]
        dhn2 = []
        for rows in subs:
            dmlp_v = dmlp_ref[rows, :]
            acc = None
            for j in range(N_CHIPS):
                cols = slice(1024 * j, 1024 * (j + 1))
                dact = _dot_nt(dmlp_v, _chip_block(wdown_ref, j))
                dup = (dact * (2.0 * up_ref[rows, cols].astype(F32))).astype(BF16)
                dup_ref[rows, cols] = dup
                part = _dot_nt(dup, _chip_block(wup_ref, j))
                acc = part if acc is None else acc + part
            dhn2.append(acc)
        dg2 = jnp.zeros((1, D_MODEL), F32)
        dgpm = jnp.zeros((1, D_MODEL), F32)
        for k, rows in enumerate(subs):
            mix_v = mix_ref[rows, :].astype(F32)
            hv = _residual_mid(x_ref[rows, :], mix_ref[rows, :], gpm_ref[...])
            r2 = _rms(hv)
            hhat = hv * r2
            dg2 = dg2 + _colsum(dhn2[k] * hhat)
            dh = dout_ref[rows, :] + _norm_bwd(dhn2[k], g2_ref[...], hhat, r2)
            dh_ref[rows, :] = dh.astype(BF16)
            rz = _rms(mix_v)
            zhat = mix_v * rz
            dgpm = dgpm + _colsum(dh * zhat)
            dmix_ref[rows, :] = _norm_bwd(dh, gpm_ref[...], zhat, rz).astype(BF16)
        dg2_ref[...] += dg2
        dgpm_ref[...] += dgpm

    tile = lambda w: pl.BlockSpec((tb, w), lambda i: (i, 0))
    vec = pl.BlockSpec((1, D_MODEL), lambda i: (0, 0))
    return _pallas(
        body, name="mlp_bwd", grid=(seq // tb,),
        in_specs=[tile(D_MODEL), tile(D_FF), tile(D_MODEL), tile(D_MODEL), tile(D_MODEL),
                  _resident((1, D_MODEL)), _resident((1, D_MODEL)), _resident(w_up.shape), _resident(w_down.shape)],
        out_specs=[tile(D_FF), tile(D_MODEL), tile(D_MODEL), vec, vec],
        out_shape=[SDS((seq, D_FF), BF16), SDS((seq, D_MODEL), BF16), SDS((seq, D_MODEL), BF16),
                   SDS((1, D_MODEL), F32), SDS((1, D_MODEL), F32)],
        operands=(dmlp, up, x, dout, mix, g_pre_mlp, g_post_mix, w_up, w_down))


def _mix_bwd(dmix, attn, gb, gc, xin, conv_w, g_attn, g_conv, w_out, n_k):
    seq = attn.shape[0]
    tb = seq // (N_CHIPS * n_k)

    def body(first, dmix_ref, a_ref, gb_ref, gc_ref, xin_ref, gch_ref, xinh_ref, cw_ref, ga_ref, gcn_ref, w_ref,
             dattn_ref, dgb_ref, dy_ref, dga_ref, dgcn_ref, dcw_ref):
        @pl.when(first)
        def _():
            dga_ref[...] = jnp.zeros_like(dga_ref)
            dgcn_ref[...] = jnp.zeros_like(dgcn_ref)
            dcw_ref[...] = jnp.zeros_like(dcw_ref)

        dmixed = _dot_nt(dmix_ref[...], w_ref[...].reshape(D_MODEL, D_MODEL))
        a = a_ref[...].astype(F32)
        ra = _rms(a)
        ahat = a * ra
        dan = dmixed[:, 0:Q_WIDTH]
        dga_ref[...] += _colsum(dan * ahat)
        dattn_ref[...] = _norm_bwd(dan, ga_ref[...], ahat, ra).astype(BF16)
        gbv = gb_ref[...].astype(F32)
        u, u1, u2, y = _conv_parts(gc_ref[...], xin_ref[...], gch_ref[...], xinh_ref[...], cw_ref[...], first)
        conv = gbv * y
        rc = _rms(conv)
        chat = conv * rc
        dcn = dmixed[:, Q_WIDTH:]
        dgcn_ref[...] += _colsum(dcn * chat)
        dconv = _norm_bwd(dcn, gcn_ref[...], chat, rc)
        dgb_ref[...] = (dconv * y).astype(BF16)
        dy = dconv * gbv
        dy_ref[...] = dy.astype(BF16)
        dcw_ref[0:1, :] += _colsum(dy * u2)
        dcw_ref[1:2, :] += _colsum(dy * u1)
        dcw_ref[2:3, :] += _colsum(dy * u)

    tile = lambda w: pl.BlockSpec((tb, w), lambda j, k: (j * n_k + k, 0))
    halo = lambda w: pl.BlockSpec((HALO, w), lambda j, k: (jnp.maximum((j * n_k + k) * (tb // HALO) - 1, 0), 0))
    whole = lambda shape: pl.BlockSpec(shape, lambda j, k: (0,) * len(shape))
    return _Rider(
        body,
        in_specs=[tile(D_MODEL), tile(Q_WIDTH), tile(CONV_WIDTH), tile(CONV_WIDTH), tile(CONV_WIDTH),
                  halo(CONV_WIDTH), halo(CONV_WIDTH),
                  _resident((CONV_K, CONV_WIDTH)), _resident((1, Q_WIDTH)), _resident((1, CONV_WIDTH)),
                  _resident(w_out.shape)],
        out_specs=[tile(Q_WIDTH), tile(CONV_WIDTH), tile(CONV_WIDTH),
                   whole((1, Q_WIDTH)), whole((1, CONV_WIDTH)), whole((CONV_K, CONV_WIDTH))],
        out_shape=[SDS((seq, Q_WIDTH), BF16), SDS((seq, CONV_WIDTH), BF16), SDS((seq, CONV_WIDTH), BF16),
                   SDS((1, Q_WIDTH), F32), SDS((1, CONV_WIDTH), F32), SDS((CONV_K, CONV_WIDTH), F32)],
        operands=(dmix, attn, gb, gc, xin, gc, xin, conv_w, g_attn, g_conv, w_out))


def _attention_bwd(q, dattn, attn, kd0, kd1, vd0, vd1, sinks, comm=None):
    seq = q.shape[0]
    nb = ATTN_BWD_BLOCKS

    def body(sink_ref, q_ref, do_ref, o_ref, kd0_ref, kd1_ref, vd0_ref, vd1_ref,
             dq_ref, dk0_ref, dk1_ref, dv0_ref, dv1_ref, dsink_ref):
        @pl.when(pl.program_id(0) == 0)
        def _():
            for r in (dk0_ref, dk1_ref, dv0_ref, dv1_ref, dsink_ref):
                r[...] = jnp.zeros_like(r)

        lane = lax.broadcasted_iota(jnp.int32, (1, 128), 1)
        dsink = jnp.zeros((1, 128), F32)
        for b in range(nb):
            i = pl.program_id(0) * nb + b
            rows = slice(QBLOCK * b, QBLOCK * (b + 1))
            valid = _attn_valid(i)
            for kv_head, (k_ref, v_ref, dk_ref, dv_ref) in enumerate(
                    ((kd0_ref, vd0_ref, dk0_ref, dv0_ref), (kd1_ref, vd1_ref, dk1_ref, dv1_ref))):
                kband, prev, own = _band(k_ref, i)
                vband, _, _ = _band(v_ref, i)
                base = 256 * kv_head
                qm = _stack_heads(q_ref[rows, base:base + 128], q_ref[rows, base + 128:base + 256])
                dom = _stack_heads(do_ref[rows, base:base + 128], do_ref[rows, base + 128:base + 256])
                om = _stack_heads(o_ref[rows, base:base + 128], o_ref[rows, base + 128:base + 256])
                s = jnp.where(valid, _dot_nt(qm, kband), NEG_INF)
                p, e_sink, inv_l = _softmax_with_sink(s, _sink_column(sink_ref, kv_head))
                p = p * inv_l
                delta = jnp.sum(dom.astype(F32) * om.astype(F32), axis=-1, keepdims=True)
                ds = (p * (_dot_nt(dom, vband) - delta)).astype(BF16)
                sink_term = -(e_sink * inv_l) * delta
                for j in range(4):
                    part = jnp.sum(sink_term[QBLOCK * j:QBLOCK * (j + 1)], axis=0, keepdims=True)
                    dsink = dsink + jnp.where(lane == 4 * kv_head + j, part, 0.0)
                pair0, pair1 = _unstack_heads(_dot(ds, kband))
                dq_ref[rows, base:base + 128] = pair0.astype(BF16)
                dq_ref[rows, base + 128:base + 256] = pair1.astype(BF16)
                dkd = _dot_tn(ds, qm)
                dkd = dkd + pltpu.roll(dkd, HEAD_DIM, 1)
                dvd = _dot_tn(p.astype(BF16), dom)
                dvd = dvd + pltpu.roll(dvd, HEAD_DIM, 1)
                dk_ref[pl.ds(prev, QBLOCK), :] += dkd[0:QBLOCK]
                dk_ref[pl.ds(own, QBLOCK), :] += dkd[QBLOCK:]
                dv_ref[pl.ds(prev, QBLOCK), :] += dvd[0:QBLOCK]
                dv_ref[pl.ds(own, QBLOCK), :] += dvd[QBLOCK:]
        dsink_ref[...] += dsink

    blk = pl.BlockSpec((nb * QBLOCK, Q_WIDTH), lambda i: (i, 0))
    full = _resident((seq, 128))
    acc = pl.BlockSpec((seq, 128), lambda i: (0, 0))
    return _pallas(
        body, name="attention_bwd", grid=(seq // (nb * QBLOCK),),
        in_specs=[pl.BlockSpec(memory_space=pltpu.SMEM), blk, blk, blk, full, full, full, full],
        out_specs=[blk, acc, acc, acc, acc, pl.BlockSpec((1, 128), lambda i: (0, 0))],
        out_shape=[SDS((seq, Q_WIDTH), BF16)] + [SDS((seq, 128), F32)] * 4 + [SDS((1, 128), F32)],
        operands=(sinks, q, dattn, attn, kd0, kd1, vd0, vd1), comm=comm)


def _in_proj_bwd(dq, dk0, dk1, dv0, dv1, dgb, dy, gc, xin, conv_w, x, dh, g_pre, w_in_t, rope):
    seq = x.shape[0]
    tb = TOKEN_TILE
    n_tiles = seq // tb

    def body(dq_ref, dk0_ref, dk1_ref, dv0_ref, dv1_ref, dgb_ref, dy_ref, dyh_ref, gc_ref, xin_ref, cw_ref,
             x_ref, dh_ref, g_ref, w_ref, c_ref, sa_ref, sb_ref,
             dproj_ref, gx_ref, dg_ref):
        i = pl.program_id(0)

        @pl.when(i == 0)
        def _():
            dg_ref[...] = jnp.zeros_like(dg_ref)

        dy = dy_ref[...].astype(F32)
        ext = jnp.concatenate([dy, jnp.where(i == n_tiles - 1, 0.0, dyh_ref[...].astype(F32))], axis=0)
        dy1 = pltpu.roll(ext, tb + HALO - 1, 0)[0:tb]
        dy2 = pltpu.roll(ext, tb + HALO - 2, 0)[0:tb]
        cw = cw_ref[...]
        du = cw[2:3, :] * dy + cw[1:2, :] * dy1 + cw[0:1, :] * dy2
        scale = 1.0 / math.sqrt(HEAD_DIM)
        base = Q_WIDTH + 2 * KV_WIDTH
        halves = [slice(0, tb // 2), slice(tb // 2, tb)]
        low = _lane_lt64((tb // 2, 128))
        for rows in halves:
            c, sa, sb = _rope_tile(c_ref.at[rows, :], sa_ref, sb_ref)
            for p in range(Q_WIDTH // 128):
                dproj_ref[rows, 128 * p:128 * (p + 1)] = _rope_transposed(
                    dq_ref[rows, 128 * p:128 * (p + 1)].astype(F32) * scale, c, sa, sb).astype(BF16)
            dk = jnp.where(low, dk0_ref[rows, :], dk1_ref[rows, :])
            dproj_ref[rows, Q_WIDTH:Q_WIDTH + KV_WIDTH] = _rope_transposed(dk, c, sa, sb).astype(BF16)
            dproj_ref[rows, Q_WIDTH + KV_WIDTH:base] = jnp.where(low, dv0_ref[rows, :], dv1_ref[rows, :]).astype(BF16)
            dproj_ref[rows, base:base + CONV_WIDTH] = dgb_ref[rows, :]
            dproj_ref[rows, base + CONV_WIDTH:base + 2 * CONV_WIDTH] = (du[rows] * xin_ref[rows, :].astype(F32)).astype(BF16)
            dproj_ref[rows, base + 2 * CONV_WIDTH:] = (du[rows] * gc_ref[rows, :].astype(F32)).astype(BF16)
        w_all = w_ref[...].reshape(IN_COLS, D_MODEL)
        dhn = [_dot(dproj_ref[rows, :], w_all) for rows in halves]
        dg = jnp.zeros((1, D_MODEL), F32)
        for k, rows in enumerate(halves):
            xv = x_ref[rows, :]
            r = _rms(xv)
            xhat = xv * r
            dg = dg + _colsum(dhn[k] * xhat)
            gx_ref[rows, :] = dh_ref[rows, :].astype(F32) + _norm_bwd(dhn[k], g_ref[...], xhat, r)
        dg_ref[...] += dg

    tile = lambda w: pl.BlockSpec((tb, w), lambda i: (i, 0))
    halo_next = pl.BlockSpec((HALO, CONV_WIDTH), lambda i: (jnp.minimum((i + 1) * (tb // HALO), seq // HALO - 1), 0))
    return _pallas(
        body, name="in_proj_bwd", grid=(n_tiles,),
        in_specs=[tile(Q_WIDTH), tile(128), tile(128), tile(128), tile(128), tile(CONV_WIDTH), tile(CONV_WIDTH), halo_next,
                  tile(CONV_WIDTH), tile(CONV_WIDTH), _resident((CONV_K, CONV_WIDTH)),
                  tile(D_MODEL), tile(D_MODEL), _resident((1, D_MODEL)), _resident(w_in_t.shape), *_rope_specs(tb)],
        out_specs=[tile(IN_COLS), tile(D_MODEL), pl.BlockSpec((1, D_MODEL), lambda i: (0, 0))],
        out_shape=[SDS((seq, IN_COLS), BF16), SDS((seq, D_MODEL), F32), SDS((1, D_MODEL), F32)],
        operands=(dq, dk0, dk1, dv0, dv1, dgb, dy, dy, gc, xin, conv_w, x, dh, g_pre, w_in_t, *rope))


def _wgrad_grid(seq, per_chip, h_rows):
    chips_per_step = 1 if per_chip else N_CHIPS
    m = chips_per_step * 2 * h_rows
    bt = min(seq, WGRAD_TOKEN_TILE)
    return chips_per_step, m, bt, seq // bt


def _wgrad(name, a, b, *, per_chip, h_rows, square_a=False, comm=None, rider=None):
    seq = a.shape[0]
    chips_per_step, m, bt, n_k = _wgrad_grid(seq, per_chip, h_rows)
    a_cols = m if per_chip else a.shape[1]
    a_wide = a.shape[1] > a_cols
    b_wide = b.shape[1] > D_MODEL

    def body(a_ref, b_ref, g_ref):
        @pl.when(pl.program_id(1) == 0)
        def _():
            g_ref[...] = jnp.zeros_like(g_ref)

        av = a_ref[...]
        if square_a:
            av = (av.astype(F32) * av.astype(F32)).astype(BF16)
        g_ref[...] += _dot_tn(av, b_ref[...]).reshape(g_ref.shape)

    a_spec = pl.BlockSpec((bt, a_cols), (lambda j, k: (k, j)) if a_wide else (lambda j, k: (k, 0)))
    b_spec = pl.BlockSpec((bt, D_MODEL), (lambda j, k: (k, j)) if b_wide else (lambda j, k: (k, 0)))
    g_spec = pl.BlockSpec((chips_per_step, 2, h_rows, D_MODEL), lambda j, k: (j, 0, 0, 0),
                          pipeline_mode=None if per_chip else pl.Buffered(1))
    return _pallas(
        body, name=name, grid=(N_CHIPS if per_chip else 1, n_k),
        in_specs=[a_spec, b_spec], out_specs=[g_spec], out_shape=[SDS((N_CHIPS, 2, h_rows, D_MODEL), F32)],
        operands=(a, b), comm=comm, rider=rider)


def _adamw_math(w, g, m, v):
    m = ADAM_B1 * m + (1.0 - ADAM_B1) * g
    v = ADAM_B2 * v + (1.0 - ADAM_B2) * (g * g)
    m_hat = m / (1.0 - ADAM_B1 ** ADAM_STEP)
    v_hat = v / (1.0 - ADAM_B2 ** ADAM_STEP)
    delta = -ADAM_LR * (m_hat / (jnp.sqrt(v_hat) + ADAM_EPS) + ADAM_WD * w)
    return delta, m, v


def _adamw_rows(name, reduced, w, m, v, rt):
    per_half = reduced.shape[1] // rt

    def body(r_ref, w_ref, m_ref, v_ref, g_out, d_out, m_out, v_out):
        g = r_ref[0]
        g_out[...] = g
        d_out[...], m_out[...], v_out[...] = _adamw_math(w_ref[...], g, m_ref[...], v_ref[...])

    blk = pl.BlockSpec((rt, D_MODEL), lambda h, r: (h * per_half + r, 0))
    return _pallas(
        body, name=name, grid=(2, per_half),
        in_specs=[pl.BlockSpec((1, rt, D_MODEL), lambda h, r: (h, r, 0)), blk, blk, blk],
        out_specs=[blk, blk, blk, blk], out_shape=[SDS(w.shape, F32)] * 4, operands=(reduced, w, m, v))


def _adamw_small(w, g, m, v):
    def body(w_ref, g_ref, m_ref, v_ref, d_out, m_out, v_out):
        d_out[...], m_out[...], v_out[...] = _adamw_math(w_ref[...], g_ref[...], m_ref[...], v_ref[...])

    return pl.pallas_call(body, name="adamw_small", in_specs=[VMEM_WHOLE] * 4, out_specs=[VMEM_WHOLE] * 3,
                          out_shape=[SDS(w.shape, F32)] * 3)(w, g, m, v)


SMALL_VECTORS = ("pre_mix_norm", "post_mix_norm", "pre_mlp_norm", "post_mlp_norm")
SMALL_NAMES = SMALL_VECTORS + ("attn_group_norm", "conv_group_norm", "conv_w", "attn_sinks")


def _pack_small(p):
    rows = [p[n].reshape(1, D_MODEL) for n in SMALL_VECTORS]
    rows.append(jnp.concatenate([p["attn_group_norm"].reshape(1, -1), p["conv_group_norm"].reshape(1, -1)], axis=1))
    cw = p["conv_w"].reshape(CONV_K, -1)
    rows.append(jnp.pad(cw, ((0, 1), (0, CONV_WIDTH - cw.shape[1]))).reshape(2, D_MODEL))
    last = jnp.concatenate([p["attn_sinks"].reshape(1, 8), p.get("loss_sum", jnp.zeros((1, 1), F32))], axis=1)
    rows.append(jnp.pad(last, ((0, 0), (0, D_MODEL - 9))))
    return jnp.concatenate(rows, axis=0)


def _unpack_small(packed, conv_width):
    out = {n: packed[i:i + 1] for i, n in enumerate(SMALL_VECTORS)}
    out["attn_group_norm"] = packed[4:5, :Q_WIDTH]
    out["conv_group_norm"] = packed[4:5, Q_WIDTH:]
    out["conv_w"] = packed[5:7].reshape(4, CONV_WIDTH)[:CONV_K, :conv_width].reshape(1, CONV_K, conv_width)
    out["attn_sinks"] = packed[7:8, :8]
    out["loss_sum"] = packed[7, 8]
    return out


WEIGHT_ORDER = ("pre_mix_norm", "w_in", "conv_w", "attn_sinks", "attn_group_norm", "conv_group_norm", "w_out",
                "post_mix_norm", "pre_mlp_norm", "w_up", "w_down", "post_mlp_norm")


def kernel(x, pre_mix_norm, w_in, conv_w, attn_sinks, attn_group_norm, conv_group_norm, w_out, post_mix_norm, pre_mlp_norm, w_up, w_down, post_mlp_norm, loss_target, m_pre_mix_norm, m_w_in, m_conv_w, m_attn_sinks, m_attn_group_norm, m_conv_group_norm, m_w_out, m_post_mix_norm, m_pre_mlp_norm, m_w_up, m_w_down, m_post_mlp_norm, v_pre_mix_norm, v_w_in, v_conv_w, v_attn_sinks, v_attn_group_norm, v_conv_group_norm, v_w_out, v_post_mix_norm, v_pre_mlp_norm, v_w_up, v_w_down, v_post_mlp_norm):
    w = dict(pre_mix_norm=pre_mix_norm, w_in=w_in, conv_w=conv_w, attn_sinks=attn_sinks, attn_group_norm=attn_group_norm,
             conv_group_norm=conv_group_norm, w_out=w_out, post_mix_norm=post_mix_norm, pre_mlp_norm=pre_mlp_norm,
             w_up=w_up, w_down=w_down, post_mlp_norm=post_mlp_norm)
    m = dict(pre_mix_norm=m_pre_mix_norm, w_in=m_w_in, conv_w=m_conv_w, attn_sinks=m_attn_sinks,
             attn_group_norm=m_attn_group_norm, conv_group_norm=m_conv_group_norm, w_out=m_w_out,
             post_mix_norm=m_post_mix_norm, pre_mlp_norm=m_pre_mlp_norm, w_up=m_w_up, w_down=m_w_down,
             post_mlp_norm=m_post_mlp_norm)
    v = dict(pre_mix_norm=v_pre_mix_norm, w_in=v_w_in, conv_w=v_conv_w, attn_sinks=v_attn_sinks,
             attn_group_norm=v_attn_group_norm, conv_group_norm=v_conv_group_norm, w_out=v_w_out,
             post_mix_norm=v_post_mix_norm, pre_mlp_norm=v_pre_mlp_norm, w_up=v_w_up, w_down=v_w_down,
             post_mlp_norm=v_post_mlp_norm)
    core = lax.axis_index("c").astype(jnp.int32).reshape(1)
    chip = 2 * lax.axis_index("x") + lax.axis_index("y")
    local_conv = conv_w.shape[2]
    xs, target = x[0], loss_target[0]
    rope = _rope_inputs(xs.shape[0])

    hb_up, hb_down, hb_out, hb_in = _cast_halves(core, w_up[0], w_down[0], w_out[0], w_in[0].T)
    conv_pad = jnp.pad(conv_w[0], ((0, 8 - CONV_K), (0, 0)))
    wf_in, conv_all = _gather_whole(hb_in, conv_pad)
    conv_full = conv_all[:, :CONV_K, :].transpose(1, 0, 2).reshape(CONV_K, CONV_WIDTH)

    *proj, wf_up, wf_out = _in_proj(xs, pre_mix_norm, wf_in, rope, comm=_merge(_gather_first(hb_up), _gather_first(hb_out)))
    q, kd0, kd1, vd0, vd1, gb, gc, xin, hn = proj
    attn, wf_up, wf_out, wf_down = _attention_fwd(
        q, kd0, kd1, vd0, vd1, attn_sinks,
        comm=_merge(_gather_second(wf_up), _gather_second(wf_out), _gather_first(hb_down)))
    mix, mixed, wf_down = _mix_out(attn, gb, gc, xin, conv_full, attn_group_norm, conv_group_norm, wf_out,
                                   comm=_gather_second(wf_down))
    up, hn2, dout, dmlp, loss_sum, dg_post_mlp = _mlp_loss(xs, mix, target, post_mix_norm, pre_mlp_norm, post_mlp_norm,
                                                           wf_up, wf_down)

    dup, dh, dmix, dg_pre_mlp, dg_post_mix = _mlp_bwd(dmlp, up, xs, dout, mix, pre_mlp_norm, post_mix_norm, wf_up, wf_down)
    n_k = _wgrad_grid(xs.shape[0], True, H_DOWN)[3]
    g_down, dattn, dgb, dy, dg_attn, dg_conv, dconv_w = _wgrad(
        "wgrad_down", up, dmlp, per_chip=True, h_rows=H_DOWN, square_a=True,
        rider=_mix_bwd(dmix, attn, gb, gc, xin, conv_full, attn_group_norm, conv_group_norm, wf_out, n_k))
    g_up, got_down = _wgrad("wgrad_up", hn2, dup, per_chip=True, h_rows=H_UP, comm=_pair_send(g_down))
    p_down = _pair_sum("pair_sum_down", core, g_down, got_down)
    g_out, got_up = _wgrad("wgrad_out", mixed, dmix, per_chip=False, h_rows=H_OUT, comm=_pair_send(g_up))
    p_up = _pair_sum("pair_sum_up", core, g_up, got_up)
    dq, dk0, dk1, dv0, dv1, dsink, ex_down, ex_up, got_out = _attention_bwd(
        q, dattn, attn, kd0, kd1, vd0, vd1, attn_sinks,
        comm=_merge(_chip_exchange(p_down), _chip_exchange(p_up), _pair_send(g_out)))
    p_out = _pair_sum("pair_sum_out", core, g_out, got_out)
    dproj, grad_x, dg_pre_mix = _in_proj_bwd(dq, dk0, dk1, dv0, dv1, dgb, dy, gc, xin, conv_full, xs, dh, pre_mix_norm,
                                             wf_in, rope)
    g_in, ex_out = _wgrad("wgrad_in", dproj, hn, per_chip=False, h_rows=H_IN, comm=_chip_exchange(p_out))
    small = dict(pre_mix_norm=dg_pre_mix, conv_w=dconv_w, attn_sinks=dsink[:, :8], attn_group_norm=dg_attn,
                 conv_group_norm=dg_conv, post_mix_norm=dg_post_mix, pre_mlp_norm=dg_pre_mlp, post_mlp_norm=dg_post_mlp,
                 loss_sum=loss_sum)
    r_down, r_up, r_out, r_in, small_total = _tail_reduce(g_in, [ex_down, ex_up, ex_out], _pack_small(small))

    out_g, out_d, out_m, out_v = {}, {}, {}, {}
    out_g["w_up"], out_d["w_up"], out_m["w_up"], out_v["w_up"] = _adamw_rows(
        "adamw_up", r_up, w_up[0], m_w_up[0], v_w_up[0], 256)
    out_g["w_down"], out_d["w_down"], out_m["w_down"], out_v["w_down"] = _adamw_rows(
        "adamw_down", r_down, w_down[0], m_w_down[0], v_w_down[0], 256)
    out_g["w_out"], out_d["w_out"], out_m["w_out"], out_v["w_out"] = _adamw_rows(
        "adamw_out", r_out, w_out[0], m_w_out[0], v_w_out[0], H_OUT)
    in_t = _adamw_rows("adamw_in", r_in, w_in[0].T, m_w_in[0].T, v_w_in[0].T, H_IN)
    out_g["w_in"], out_d["w_in"], out_m["w_in"], out_v["w_in"] = [t.T for t in in_t]

    small_sum = _unpack_small(small_total, CONV_WIDTH)
    loss = small_sum["loss_sum"] * (0.5 / D_MODEL)
    small_sum["conv_w"] = lax.dynamic_slice_in_dim(small_sum["conv_w"], chip * local_conv, local_conv, axis=2)
    packed = [_pack_small({n: t[n] for n in SMALL_NAMES}) for t in (w, small_sum, m, v)]
    small_d, small_m, small_v = [_unpack_small(t, local_conv) for t in _adamw_small(*packed)]
    for n in SMALL_NAMES:
        out_g[n], out_d[n], out_m[n], out_v[n] = small_sum[n], small_d[n], small_m[n], small_v[n]

    def shaped(d):
        return [d[n].reshape(w[n].shape) for n in WEIGHT_ORDER]

    return (loss, grad_x[None], *shaped(out_g), *shaped(out_d), *shaped(out_m), *shaped(out_v))
```

```python
import math
from typing import Callable, NamedTuple

import jax
import jax.numpy as jnp
import numpy as np
from jax import lax
from jax.experimental import pallas as pl
from jax.experimental.pallas import tpu as pltpu

F32 = jnp.float32
BF16 = jnp.bfloat16

D_MODEL = 1024
HEAD_DIM = 64
Q_WIDTH = 512
KV_WIDTH = 128
CONV_WIDTH = 512
CONV_K = 3
D_FF = 4096
IN_COLS = 2304
QBLOCK = 128
ROT_DIM = 16
ROPE_THETA = 500000.0
NORM_EPS = 1e-6
NEG_INF = -1e30
N_CHIPS = 4

ADAM_LR = 0.001
ADAM_B1 = 0.9
ADAM_B2 = 0.999
ADAM_EPS = 1e-08
ADAM_WD = 0.01
ADAM_STEP = 10

H_UP, H_DOWN, H_OUT, H_IN = 512, 512, 128, 288

TOKEN_TILE = 512
WIDE_TOKEN_TILE = 1024
MLP_BWD_TOKEN_TILE = 512
MLP_BWD_SUB_TILE = 256
ATTN_FWD_BLOCKS = 16
ATTN_BWD_BLOCKS = 2
WGRAD_TOKEN_TILE = 2048
VMEM_LIMIT_V7X = 56 * 1024 * 1024

MESH = pl.DeviceIdType.MESH
ANY = pl.BlockSpec(memory_space=pl.ANY)
VMEM_WHOLE = pl.BlockSpec(memory_space=pltpu.VMEM)
SDS = jax.ShapeDtypeStruct


def _resident(shape):
    zeros = (0,) * len(shape)
    return pl.BlockSpec(shape, lambda *_: zeros, pipeline_mode=pl.Buffered(1))


def _rms(v):
    return lax.rsqrt(jnp.mean(v * v, axis=-1, keepdims=True) + NORM_EPS)


def _norm_bwd(dy, gain, vhat, rstd):
    t = dy * gain
    return rstd * (t - vhat * jnp.mean(t * vhat, axis=-1, keepdims=True))


def _colsum(v):
    return jnp.sum(v, axis=0, keepdims=True)


def _dot_nt(a, b):
    return lax.dot_general(a, b, (((1,), (1,)), ((), ())), preferred_element_type=F32)


def _dot_tn(a, b):
    return lax.dot_general(a, b, (((0,), (0,)), ((), ())), preferred_element_type=F32)


def _dot(a, b):
    return jnp.dot(a, b, preferred_element_type=F32)


def _chip_block(w_ref, chip):
    both = w_ref[pl.ds(2 * chip, 2)]
    return both.reshape(2 * both.shape[1], both.shape[2])


def _lane_lt64(shape):
    return lax.broadcasted_iota(jnp.int32, shape, 1) < HEAD_DIM


class _Comm(NamedTuple):
    operands: tuple
    out_shapes: tuple
    aliases: dict
    n_remote: int
    n_local: int
    plan: Callable
    after: Callable = None


def _merge(*comms):
    operands, out_shapes, aliases, parts = [], [], {}, []
    n_remote = n_local = 0
    for cm in comms:
        parts.append((len(operands), len(out_shapes), n_remote, n_local, cm))
        for k, v in cm.aliases.items():
            aliases[len(operands) + k] = len(out_shapes) + v
        operands += cm.operands
        out_shapes += cm.out_shapes
        n_remote += cm.n_remote
        n_local += cm.n_local

    def run(which, ins, outs, send, recv, loc):
        sends, recvs, locs = [], [], []
        for i0, o0, r0, l0, cm in parts:
            stage = getattr(cm, which)
            if stage is not None:
                s, r, l = stage(ins[i0:i0 + len(cm.operands)], outs[o0:o0 + len(cm.out_shapes)],
                                lambda k, r0=r0: send(r0 + k), lambda k, r0=r0: recv(r0 + k), lambda k, l0=l0: loc(l0 + k))
                sends, recvs, locs = sends + s, recvs + r, locs + l
        return sends, recvs, locs

    def plan(*args):
        return run("plan", *args)

    def after(*args):
        return run("after", *args)

    return _Comm(tuple(operands), tuple(out_shapes), aliases, n_remote, n_local, plan,
                 after if any(cm.after is not None for cm in comms) else None)


def _sem_scratch(comm):
    return [pltpu.SemaphoreType.DMA((max(comm.n_remote, 1),)), pltpu.SemaphoreType.DMA((max(comm.n_remote, 1),)),
            pltpu.SemaphoreType.DMA((max(comm.n_local, 1),))]


class _Rider(NamedTuple):
    body: Callable
    in_specs: list
    out_specs: list
    out_shape: list
    operands: tuple


def _pallas(body, *, name, grid, in_specs, out_specs, out_shape, operands, scratch=(), comm=None, rider=None):
    params = pltpu.CompilerParams(dimension_semantics=("arbitrary",) * len(grid), vmem_limit_bytes=VMEM_LIMIT_V7X)
    if rider is not None:
        own_in, own_out, ride_in, ride_out = len(in_specs), len(out_specs), len(rider.in_specs), len(rider.out_specs)
        own_body = body

        def body(*refs):
            o0 = own_in + ride_in
            s0 = o0 + own_out + ride_out
            own_body(*refs[:own_in], *refs[o0:o0 + own_out], *refs[s0:])
            first = None
            for axis in range(len(grid)):
                at_start = pl.program_id(axis) == 0
                first = at_start if first is None else jnp.logical_and(first, at_start)
            rider.body(first, *refs[own_in:o0], *refs[o0 + own_out:s0])

        in_specs, out_specs = list(in_specs) + rider.in_specs, list(out_specs) + rider.out_specs
        out_shape, operands = list(out_shape) + rider.out_shape, tuple(operands) + tuple(rider.operands)
    if comm is None:
        return pl.pallas_call(body, name=name, grid=grid, in_specs=in_specs, out_specs=out_specs, out_shape=out_shape,
                              scratch_shapes=list(scratch), compiler_params=params)(*operands)
    n_in, n_out, n_scr = len(in_specs), len(out_specs), len(scratch)
    c_in, c_out = len(comm.operands), len(comm.out_shapes)

    def with_comm(*refs):
        ins, c_ins = refs[:n_in], refs[n_in:n_in + c_in]
        o0 = n_in + c_in
        outs, c_outs = refs[o0:o0 + n_out], refs[o0 + n_out:o0 + n_out + c_out]
        s0 = o0 + n_out + c_out
        scr = refs[s0:s0 + n_scr]
        send_sems, recv_sems, local_sems = refs[s0 + n_scr:]
        first = last = None
        for axis, size in enumerate(grid):
            at_start, at_end = pl.program_id(axis) == 0, pl.program_id(axis) == size - 1
            first = at_start if first is None else jnp.logical_and(first, at_start)
            last = at_end if last is None else jnp.logical_and(last, at_end)

        def copies():
            return comm.plan(c_ins, c_outs, lambda k: send_sems.at[k], lambda k: recv_sems.at[k],
                             lambda k: local_sems.at[k])

        @pl.when(first)
        def _():
            sends, _, locs = copies()
            for cp in sends + locs:
                cp.start()

        body(*ins, *outs, *scr)

        @pl.when(last)
        def _():
            sends, recvs, locs = copies()
            for cp in recvs:
                cp.wait_recv()
            for cp in sends:
                cp.wait_send()
            for cp in locs:
                cp.wait()
            if comm.after is not None:
                sends, recvs, _ = comm.after(c_ins, c_outs, lambda k: send_sems.at[k], lambda k: recv_sems.at[k],
                                             lambda k: local_sems.at[k])
                for cp in sends:
                    cp.start()
                for cp in recvs:
                    cp.wait_recv()
                for cp in sends:
                    cp.wait_send()

    return pl.pallas_call(
        with_comm, name=name, grid=grid,
        in_specs=list(in_specs) + [ANY] * c_in, out_specs=list(out_specs) + [ANY] * c_out,
        out_shape=list(out_shape) + list(comm.out_shapes),
        scratch_shapes=list(scratch) + _sem_scratch(comm),
        input_output_aliases={n_in + k: n_out + v for k, v in comm.aliases.items()},
        compiler_params=params)(*operands, *comm.operands)


def _place():
    return lax.axis_index("x"), lax.axis_index("y"), lax.axis_index("c")


def _other_chips(x, y):
    return [(1 - x, y), (x, 1 - y), (1 - x, 1 - y)]


def _slot(px, py, pc):
    return 4 * px + 2 * py + pc


def _remote(src, dst, send_sem, recv_sem, to):
    return pltpu.make_async_remote_copy(src_ref=src, dst_ref=dst, send_sem=send_sem, recv_sem=recv_sem,
                                        device_id=to, device_id_type=MESH)


def _gather_first(half_block):
    def plan(ins, outs, send, recv, loc):
        (blk,), (full,) = ins, outs
        x, y, c = _place()
        chips = _other_chips(x, y)
        mine = full.at[_slot(x, y, c)]
        sends = [_remote(blk, mine, send(0), recv(0), (x, y, 1 - c))]
        sends += [_remote(blk, mine, send(1 + j), recv(1 + j), (*chip, c)) for j, chip in enumerate(chips)]
        recvs = [_remote(blk, full.at[_slot(x, y, 1 - c)], send(0), recv(0), (x, y, 1 - c))]
        recvs += [_remote(blk, full.at[_slot(*chip, c)], send(1 + j), recv(1 + j), (*chip, c))
                  for j, chip in enumerate(chips)]
        return sends, recvs, [pltpu.make_async_copy(blk, mine, loc(0))]

    return _Comm((half_block,), (SDS((2 * N_CHIPS,) + half_block.shape, half_block.dtype),), {}, 4, 1, plan)


def _gather_second(partly_gathered):
    def plan(ins, outs, send, recv, loc):
        (src,), (full,) = ins, outs
        x, y, c = _place()
        chips = _other_chips(x, y)
        sends = [_remote(src.at[_slot(*chip, c)], full.at[_slot(*chip, c)], send(j), recv(j), (x, y, 1 - c))
                 for j, chip in enumerate(chips)]
        recvs = [_remote(src.at[_slot(*chip, 1 - c)], full.at[_slot(*chip, 1 - c)], send(j), recv(j), (x, y, 1 - c))
                 for j, chip in enumerate(chips)]
        return sends, recvs, []

    return _Comm((partly_gathered,), (SDS(partly_gathered.shape, partly_gathered.dtype),), {0: 0}, 3, 0, plan)


def _relay_pieces(full, rows, x, y, c):
    half = rows // 2
    upper, lower = pl.ds(0, half), pl.ds(half, half)
    diagonal = full.at[_slot(1 - x, 1 - y, c)]
    return [(full.at[_slot(1 - x, y, c), upper], diagonal.at[upper], (x, 1 - y, c)),
            (full.at[_slot(x, 1 - y, c), lower], diagonal.at[lower], (1 - x, y, c))]


def _relay_first(half_block):
    def plan(ins, outs, send, recv, loc):
        (blk,), (full,) = ins, outs
        x, y, c = _place()
        peers = [(x, y, 1 - c), (1 - x, y, c), (x, 1 - y, c)]
        mine = full.at[_slot(x, y, c)]
        sends = [_remote(blk, mine, send(k), recv(k), peer) for k, peer in enumerate(peers)]
        recvs = [_remote(blk, full.at[_slot(*peer)], send(k), recv(k), peer) for k, peer in enumerate(peers)]
        return sends, recvs, [pltpu.make_async_copy(blk, mine, loc(0))]

    return _Comm((half_block,), (SDS((2 * N_CHIPS,) + half_block.shape, half_block.dtype),), {}, 3, 1, plan)


def _third_leg(src, full, send, recv, k):
    x, y, c = _place()
    sibling = (x, y, 1 - c)
    here, there = _slot(1 - x, 1 - y, c), _slot(1 - x, 1 - y, 1 - c)
    return ([_remote(src.at[here], full.at[here], send(k), recv(k), sibling)],
            [_remote(src.at[there], full.at[there], send(k), recv(k), sibling)], [])


def _relay_second(partly_gathered, then_third=False):
    rows = partly_gathered.shape[1]

    def plan(ins, outs, send, recv, loc):
        (src,), (full,) = ins, outs
        x, y, c = _place()
        sibling = (x, y, 1 - c)
        sends, recvs = [], []
        for k, chip in enumerate([(1 - x, y), (x, 1 - y)]):
            sends.append(_remote(src.at[_slot(*chip, c)], full.at[_slot(*chip, c)], send(k), recv(k), sibling))
            recvs.append(_remote(src.at[_slot(*chip, 1 - c)], full.at[_slot(*chip, 1 - c)], send(k), recv(k), sibling))
        for k, (piece, lands, peer) in enumerate(_relay_pieces(full, rows, x, y, c)):
            sends.append(_remote(piece, piece, send(2 + k), recv(2 + k), peer))
            recvs.append(_remote(lands, lands, send(2 + k), recv(2 + k), peer))
        return sends, recvs, []

    def after(ins, outs, send, recv, loc):
        return _third_leg(ins[0], outs[0], send, recv, 4)

    return _Comm((partly_gathered,), (SDS(partly_gathered.shape, partly_gathered.dtype),), {0: 0}, 5, 0, plan,
                 after if then_third else None)


def _relay_third(mostly_gathered):
    def plan(ins, outs, send, recv, loc):
        return _third_leg(ins[0], outs[0], send, recv, 0)

    return _Comm((mostly_gathered,), (SDS(mostly_gathered.shape, mostly_gathered.dtype),), {0: 0}, 1, 0, plan)


def _gather_whole(half_block, small_block):
    rows = half_block.shape[0]

    def body(blk_ref, small_ref, out_ref, small_out_ref, send_sems, recv_sems, local_sems):
        x, y, c = _place()
        me, sibling = (x, y, c), (x, y, 1 - c)
        neighbours, diagonal = [(1 - x, y), (x, 1 - y)], (1 - x, 1 - y)

        def copy(k, block, to, src=None):
            return _remote(out_ref.at[_slot(*block)] if src is None else src, out_ref.at[_slot(*block)],
                           send_sems.at[k], recv_sems.at[k], to)

        def small_copy(k, chip, to):
            return _remote(small_ref, small_out_ref.at[2 * chip[0] + chip[1]], send_sems.at[8 + k], recv_sems.at[8 + k], to)

        mine = pltpu.make_async_copy(blk_ref, out_ref.at[_slot(*me)], local_sems.at[0])
        mine_small = pltpu.make_async_copy(small_ref, small_out_ref.at[2 * x + y], local_sems.at[1])
        mine.start()
        mine_small.start()
        started = [copy(0, me, sibling, src=blk_ref)]
        started += [copy(1 + k, me, (*chip, c), src=blk_ref) for k, chip in enumerate(neighbours)]
        started += [small_copy(k, (x, y), (*chip, c)) for k, chip in enumerate(neighbours + [diagonal])]
        for cp in started:
            cp.start()
        pieces = _relay_pieces(out_ref, rows, x, y, c)
        for k, chip in enumerate(neighbours):
            copy(1 + k, (*chip, c), me).wait_recv()
            piece, _, peer = pieces[k]
            started += [copy(3 + k, (*chip, c), sibling), _remote(piece, piece, send_sems.at[5 + k], recv_sems.at[5 + k], peer)]
            started[-2].start()
            started[-1].start()
        for k, (_, lands, peer) in enumerate(pieces):
            _remote(lands, lands, send_sems.at[5 + k], recv_sems.at[5 + k], peer).wait_recv()
        started.append(copy(7, (*diagonal, c), sibling))
        started[-1].start()
        copy(0, sibling, me).wait_recv()
        for k, chip in enumerate(neighbours):
            copy(3 + k, (*chip, 1 - c), me).wait_recv()
        copy(7, (*diagonal, 1 - c), me).wait_recv()
        for k, chip in enumerate(neighbours + [diagonal]):
            small_copy(k, chip, me).wait_recv()
        for cp in started:
            cp.wait_send()
        mine.wait()
        mine_small.wait()

    return pl.pallas_call(
        body, name="gather_whole", in_specs=[ANY, ANY], out_specs=[ANY, ANY],
        out_shape=[SDS((2 * N_CHIPS,) + half_block.shape, half_block.dtype),
                   SDS((N_CHIPS,) + small_block.shape, small_block.dtype)],
        scratch_shapes=[pltpu.SemaphoreType.DMA((11,)), pltpu.SemaphoreType.DMA((11,)), pltpu.SemaphoreType.DMA((2,))],
    )(half_block, small_block)


def _pair_send(grads):
    def plan(ins, outs, send, recv, loc):
        (g,), (got,) = ins, outs
        x, y, c = _place()
        copies = [_remote(g.at[j, 1 - c], got.at[j], send(j), recv(j), (x, y, 1 - c)) for j in range(N_CHIPS)]
        return copies, copies, []

    shape = (grads.shape[0],) + grads.shape[2:]
    return _Comm((grads,), (SDS(shape, grads.dtype),), {}, N_CHIPS, 0, plan)


def _chip_exchange(partial):
    def plan(ins, outs, send, recv, loc):
        (p,), (got,) = ins, outs
        x, y, c = _place()
        my_chip = 2 * x + y
        chips = _other_chips(x, y)
        sends = [_remote(p.at[2 * chip[0] + chip[1]], got.at[my_chip], send(j), recv(j), (*chip, c))
                 for j, chip in enumerate(chips)]
        recvs = [_remote(p.at[my_chip], got.at[2 * chip[0] + chip[1]], send(j), recv(j), (*chip, c))
                 for j, chip in enumerate(chips)]
        return sends, recvs, [pltpu.make_async_copy(p.at[my_chip], got.at[my_chip], loc(0))]

    return _Comm((partial,), (SDS(partial.shape, partial.dtype),), {}, 3, 1, plan)


def _pair_sum(name, core, grads, received):
    h = grads.shape[2]

    def body(core_ref, g_ref, r_ref, o_ref):
        o_ref[...] = (g_ref[0] + r_ref[...]).astype(BF16)

    return pl.pallas_call(
        body, name=name,
        grid_spec=pltpu.PrefetchScalarGridSpec(
            num_scalar_prefetch=1, grid=(N_CHIPS,),
            in_specs=[pl.BlockSpec((1, 1, h, D_MODEL), lambda j, core_ref: (j, core_ref[0], 0, 0)),
                      pl.BlockSpec((1, h, D_MODEL), lambda j, core_ref: (j, 0, 0))],
            out_specs=pl.BlockSpec((1, h, D_MODEL), lambda j, core_ref: (j, 0, 0))),
        out_shape=SDS((N_CHIPS, h, D_MODEL), BF16),
        compiler_params=pltpu.CompilerParams(dimension_semantics=("arbitrary",), vmem_limit_bytes=VMEM_LIMIT_V7X),
    )(core, grads, received)


SMALL_ROWS = 8


def _sum_blocks(ref):
    return (ref[0].astype(F32) + ref[1].astype(F32)) + (ref[2].astype(F32) + ref[3].astype(F32))


def _tail_reduce(last_grads, exchanged, small):
    n = len(exchanged)
    h = last_grads.shape[2]

    def body(*refs):
        g_ref, ex, small_ref = refs[0], refs[1:1 + n], refs[1 + n]
        o0 = 2 + n
        out, out_last, small_out = refs[o0:o0 + n], refs[o0 + n], refs[o0 + n + 1]
        s0 = o0 + n + 2
        halves, half_last = refs[s0:s0 + n], refs[s0 + n]
        own, got, part, exch, small_buf = refs[s0 + n + 1:s0 + n + 6]
        pair_send, pair_recv, chip_send, chip_recv, share_send, share_recv, small_send, small_recv, local_sems = refs[s0 + n + 6:]
        x, y, c = _place()
        sibling = (x, y, 1 - c)
        my_chip, me = 2 * x + y, _slot(x, y, c)
        chips = _other_chips(x, y)

        to_sibling = [_remote(g_ref.at[j, 1 - c], got.at[j], pair_send.at[j], pair_recv.at[j], sibling)
                      for j in range(N_CHIPS)]
        load_own = [pltpu.make_async_copy(g_ref.at[j, c], own.at[j], local_sems.at[j]) for j in range(N_CHIPS)]
        for cp in to_sibling + load_own:
            cp.start()

        small_buf[me] = small_ref[...]
        small_copies = []
        for mask in range(1, 8):
            peer = (x ^ (mask >> 2), y ^ ((mask >> 1) & 1), c ^ (mask & 1))
            small_copies.append(_remote(small_ref, small_buf.at[me], small_send.at[mask - 1], small_recv.at[mask - 1], peer))
        for cp in small_copies:
            cp.start()

        def share(k, half_ref, out_ref):
            keep = pltpu.make_async_copy(half_ref, out_ref.at[c], local_sems.at[N_CHIPS + k])
            give = _remote(half_ref, out_ref.at[c], share_send.at[k], share_recv.at[k], sibling)
            take = _remote(half_ref, out_ref.at[1 - c], share_send.at[k], share_recv.at[k], sibling)
            keep.start()
            give.start()
            return keep, give, take

        shares = []
        for k in range(n):
            halves[k][...] = _sum_blocks(ex[k])
            shares.append(share(k, halves[k], out[k]))

        for cp in to_sibling:
            cp.wait_recv()
        for cp in load_own:
            cp.wait()
        part[...] = (own[...] + got[...]).astype(BF16)
        exch[my_chip] = part[my_chip]
        to_chips = [_remote(part.at[2 * chip[0] + chip[1]], exch.at[my_chip], chip_send.at[j], chip_recv.at[j], (*chip, c))
                    for j, chip in enumerate(chips)]
        from_chips = [_remote(part.at[my_chip], exch.at[2 * chip[0] + chip[1]], chip_send.at[j], chip_recv.at[j], (*chip, c))
                      for j, chip in enumerate(chips)]
        for cp in to_chips:
            cp.start()

        for cp in small_copies:
            cp.wait_recv()
        total = small_buf[0]
        for d in range(1, 8):
            total = total + small_buf[d]
        small_out[...] = total

        for cp in from_chips:
            cp.wait_recv()
        half_last[...] = _sum_blocks(exch)
        shares.append(share(n, half_last, out_last))

        for keep, give, take in shares:
            take.wait_recv()
            give.wait_send()
            keep.wait()
        for cp in to_sibling + to_chips + small_copies:
            cp.wait_send()

    blocks = (N_CHIPS, h, D_MODEL)
    return pl.pallas_call(
        body, name="tail_reduce",
        in_specs=[ANY] + [VMEM_WHOLE] * (n + 1), out_specs=[ANY] * (n + 1) + [VMEM_WHOLE],
        out_shape=[SDS((2,) + e.shape[1:], F32) for e in exchanged] + [SDS((2, h, D_MODEL), F32), SDS(small.shape, F32)],
        scratch_shapes=[pltpu.VMEM(e.shape[1:], F32) for e in exchanged] + [pltpu.VMEM((h, D_MODEL), F32)]
                       + [pltpu.VMEM(blocks, F32), pltpu.VMEM(blocks, F32), pltpu.VMEM(blocks, BF16), pltpu.VMEM(blocks, BF16),
                          pltpu.VMEM((8,) + small.shape, F32)]
                       + [pltpu.SemaphoreType.DMA((N_CHIPS,)), pltpu.SemaphoreType.DMA((N_CHIPS,)),
                          pltpu.SemaphoreType.DMA((3,)), pltpu.SemaphoreType.DMA((3,)),
                          pltpu.SemaphoreType.DMA((n + 1,)), pltpu.SemaphoreType.DMA((n + 1,)),
                          pltpu.SemaphoreType.DMA((7,)), pltpu.SemaphoreType.DMA((7,)),
                          pltpu.SemaphoreType.DMA((N_CHIPS + n + 1,))],
        compiler_params=pltpu.CompilerParams(vmem_limit_bytes=VMEM_LIMIT_V7X),
    )(last_grads, *exchanged, small)


def _rope_expansion():
    half = ROT_DIM // 2
    expand = np.zeros((2 * half, 3 * 128), np.float32)
    const = np.zeros((1, 3 * 128), np.float32)
    for lane in range(128):
        d = lane % HEAD_DIM
        if d < ROT_DIM:
            expand[d % half, lane] = 1.0
        else:
            const[0, lane] = 1.0
        if d < half:
            expand[half + d, 128 + lane] = -1.0
        elif d < ROT_DIM:
            expand[half + d - half, 256 + lane] = 1.0
    return expand, const


ROPE_PIECES = 3 * ROT_DIM


def _rope_inputs(seq):
    pos = jnp.arange(seq, dtype=F32)
    inv_freq = ROPE_THETA ** (-jnp.arange(0, ROT_DIM, 2, dtype=F32) / ROT_DIM)
    ang = pos[:, None] * inv_freq[None, :]
    cs = jnp.concatenate([jnp.cos(ang), jnp.sin(ang)], axis=1)
    hi = lax.reduce_precision(cs, 8, 7)
    mid = lax.reduce_precision(cs - hi, 8, 7)
    low = cs - hi - mid
    expand, const = _rope_expansion()
    pieces = jnp.concatenate([hi, mid, low], axis=1).astype(BF16)
    return pieces, jnp.asarray(np.concatenate([expand] * 3, axis=0), BF16), jnp.asarray(const)


def _rope_specs(tb):
    return [pl.BlockSpec((tb, ROPE_PIECES), lambda i: (i, 0)), _resident((ROPE_PIECES, 3 * 128)), _resident((1, 3 * 128))]


def _rope_tile(pieces_ref, expand_ref, const_ref):
    tables = _dot(pieces_ref[...], expand_ref[...]) + const_ref[...]
    return tables[:, 0:128], tables[:, 128:256], tables[:, 256:384]


def _rope(t, c, sa, sb):
    half = ROT_DIM // 2
    return t * c + pltpu.roll(t, 128 - half, 1) * sa + pltpu.roll(t, half, 1) * sb


def _rope_transposed(dt, c, sa, sb):
    half = ROT_DIM // 2
    return dt * c + pltpu.roll(dt * sa, half, 1) + pltpu.roll(dt * sb, 128 - half, 1)


def _cast_halves(core, w_up, w_down, w_out, w_in_t):
    def body(core_ref, up_ref, down_ref, out_ref, in_ref, up_o, down_o, out_o, in_o):
        up_o[...] = up_ref[...].astype(BF16)
        down_o[...] = down_ref[...].astype(BF16)
        out_o[...] = out_ref[...].astype(BF16)
        in_o[...] = in_ref[...].astype(BF16)

    half = lambda rows: pl.BlockSpec((rows, D_MODEL), lambda i, core_ref: (core_ref[0], 0))
    whole = lambda rows: pl.BlockSpec((rows, D_MODEL), lambda i, core_ref: (0, 0))
    rows = (H_UP, H_DOWN, H_OUT, H_IN)
    return pl.pallas_call(
        body, name="cast_halves",
        grid_spec=pltpu.PrefetchScalarGridSpec(
            num_scalar_prefetch=1, grid=(1,), in_specs=[half(r) for r in rows], out_specs=[whole(r) for r in rows]),
        out_shape=[SDS((r, D_MODEL), BF16) for r in rows],
        compiler_params=pltpu.CompilerParams(dimension_semantics=("arbitrary",), vmem_limit_bytes=VMEM_LIMIT_V7X),
    )(core, w_up, w_down, w_out, w_in_t)


def _in_proj(x, g_pre, w_in_t, rope, comm=None):
    seq = x.shape[0]
    tb = min(seq, WIDE_TOKEN_TILE)

    def body(x_ref, g_ref, w_ref, c_ref, sa_ref, sb_ref,
             q_ref, kd0_ref, kd1_ref, vd0_ref, vd1_ref, gb_ref, gc_ref, xin_ref, hn_ref):
        xv = x_ref[...]
        hn = (xv * _rms(xv) * g_ref[...]).astype(BF16)
        hn_ref[...] = hn
        proj = _dot_nt(hn, w_ref[...].reshape(IN_COLS, D_MODEL))
        c, sa, sb = _rope_tile(c_ref, sa_ref, sb_ref)
        scale = 1.0 / math.sqrt(HEAD_DIM)
        for p in range(Q_WIDTH // 128):
            q_ref[:, 128 * p:128 * (p + 1)] = (_rope(proj[:, 128 * p:128 * (p + 1)], c, sa, sb) * scale).astype(BF16)
        k = _rope(proj[:, Q_WIDTH:Q_WIDTH + KV_WIDTH], c, sa, sb)
        v = proj[:, Q_WIDTH + KV_WIDTH:Q_WIDTH + 2 * KV_WIDTH]
        low = _lane_lt64(k.shape)
        k_sw, v_sw = pltpu.roll(k, HEAD_DIM, 1), pltpu.roll(v, HEAD_DIM, 1)
        kd0_ref[...] = jnp.where(low, k, k_sw).astype(BF16)
        kd1_ref[...] = jnp.where(low, k_sw, k).astype(BF16)
        vd0_ref[...] = jnp.where(low, v, v_sw).astype(BF16)
        vd1_ref[...] = jnp.where(low, v_sw, v).astype(BF16)
        base = Q_WIDTH + 2 * KV_WIDTH
        gb_ref[...] = proj[:, base:base + CONV_WIDTH].astype(BF16)
        gc_ref[...] = proj[:, base + CONV_WIDTH:base + 2 * CONV_WIDTH].astype(BF16)
        xin_ref[...] = proj[:, base + 2 * CONV_WIDTH:base + 3 * CONV_WIDTH].astype(BF16)

    tile = lambda w: pl.BlockSpec((tb, w), lambda i: (i, 0))
    return _pallas(
        body, name="in_proj", grid=(seq // tb,),
        in_specs=[tile(D_MODEL), _resident((1, D_MODEL)), _resident(w_in_t.shape), *_rope_specs(tb)],
        out_specs=[tile(Q_WIDTH), tile(128), tile(128), tile(128), tile(128),
                   tile(CONV_WIDTH), tile(CONV_WIDTH), tile(CONV_WIDTH), tile(D_MODEL)],
        out_shape=[SDS((seq, Q_WIDTH), BF16)] + [SDS((seq, 128), BF16)] * 4
                  + [SDS((seq, CONV_WIDTH), BF16)] * 3 + [SDS((seq, D_MODEL), BF16)],
        operands=(x, g_pre, w_in_t, *rope), comm=comm)


def _attn_valid(i):
    shape = (4 * QBLOCK, 2 * QBLOCK)
    row = lax.broadcasted_iota(jnp.int32, shape, 0)
    col = lax.broadcasted_iota(jnp.int32, shape, 1)
    qi = row & (QBLOCK - 1)
    return (col > qi) & (col <= qi + QBLOCK) & ((col >= QBLOCK) | (i > 0))


def _stack_heads(pair0, pair1):
    low = _lane_lt64(pair0.shape)
    zero = jnp.zeros_like(pair0)
    return jnp.concatenate([jnp.where(low, pair0, zero), jnp.where(low, zero, pair0),
                            jnp.where(low, pair1, zero), jnp.where(low, zero, pair1)], axis=0)


def _unstack_heads(stacked):
    low = _lane_lt64((QBLOCK, 128))
    pair0 = jnp.where(low, stacked[0:QBLOCK], stacked[QBLOCK:2 * QBLOCK])
    pair1 = jnp.where(low, stacked[2 * QBLOCK:3 * QBLOCK], stacked[3 * QBLOCK:4 * QBLOCK])
    return pair0, pair1


def _sink_column(sink_ref, kv_head):
    row = lax.broadcasted_iota(jnp.int32, (4 * QBLOCK, 1), 0)
    s = [sink_ref[0, 4 * kv_head + j] for j in range(4)]
    return jnp.where(row < QBLOCK, s[0], jnp.where(row < 2 * QBLOCK, s[1], jnp.where(row < 3 * QBLOCK, s[2], s[3])))


def _band(ref, i):
    prev = pl.multiple_of(jnp.maximum(i - 1, 0) * QBLOCK, QBLOCK)
    own = pl.multiple_of(i * QBLOCK, QBLOCK)
    return jnp.concatenate([ref[pl.ds(prev, QBLOCK), :], ref[pl.ds(own, QBLOCK), :]], axis=0), prev, own


def _softmax_with_sink(s, sink_col):
    m = jnp.maximum(jnp.max(s, axis=-1, keepdims=True), sink_col)
    p = jnp.exp(s - m)
    e_sink = jnp.exp(sink_col - m)
    inv_l = 1.0 / (jnp.sum(p, axis=-1, keepdims=True) + e_sink)
    return p, e_sink, inv_l


def _attention_fwd(q, kd0, kd1, vd0, vd1, sinks, comm=None):
    seq = q.shape[0]

    nb = ATTN_FWD_BLOCKS

    def body(sink_ref, q_ref, kd0_ref, kd1_ref, vd0_ref, vd1_ref, o_ref):
        for b in range(nb):
            i = pl.program_id(0) * nb + b
            rows = slice(QBLOCK * b, QBLOCK * (b + 1))
            valid = _attn_valid(i)
            for kv_head, (k_ref, v_ref) in enumerate(((kd0_ref, vd0_ref), (kd1_ref, vd1_ref))):
                kband, _, _ = _band(k_ref, i)
                vband, _, _ = _band(v_ref, i)
                base = 256 * kv_head
                qm = _stack_heads(q_ref[rows, base:base + 128], q_ref[rows, base + 128:base + 256])
                s = jnp.where(valid, _dot_nt(qm, kband), NEG_INF)
                p, _, inv_l = _softmax_with_sink(s, _sink_column(sink_ref, kv_head))
                o = _dot(p.astype(BF16), vband) * inv_l
                pair0, pair1 = _unstack_heads(o)
                o_ref[rows, base:base + 128] = pair0.astype(BF16)
                o_ref[rows, base + 128:base + 256] = pair1.astype(BF16)

    blk = pl.BlockSpec((nb * QBLOCK, Q_WIDTH), lambda i: (i, 0))
    full = _resident((seq, 128))
    return _pallas(
        body, name="attention_fwd", grid=(seq // (nb * QBLOCK),),
        in_specs=[pl.BlockSpec(memory_space=pltpu.SMEM), blk, full, full, full, full],
        out_specs=[blk], out_shape=[SDS((seq, Q_WIDTH), BF16)],
        operands=(sinks, q, kd0, kd1, vd0, vd1), comm=comm)


HALO = 16


def _conv_parts(gc, xin, gc_halo, xin_halo, conv_w, first):
    tb = gc.shape[0]
    u = gc.astype(F32) * xin.astype(F32)
    u_halo = jnp.where(first, 0.0, gc_halo.astype(F32) * xin_halo.astype(F32))
    ext = jnp.concatenate([u_halo, u], axis=0)
    u1 = pltpu.roll(ext, 1, 0)[HALO:HALO + tb]
    u2 = pltpu.roll(ext, 2, 0)[HALO:HALO + tb]
    y = conv_w[0:1, :] * u2 + conv_w[1:2, :] * u1 + conv_w[2:3, :] * u
    return u, u1, u2, y


def _halo_prev(tb, w):
    return pl.BlockSpec((HALO, w), lambda i: (jnp.maximum(i * (tb // HALO) - 1, 0), 0))


def _residual_mid(x, mix, g_post_mix):
    mix_f = mix.astype(F32)
    return x + mix_f * _rms(mix_f) * g_post_mix


def _mix_out(attn, gb, gc, xin, conv_w, g_attn, g_conv, w_out, comm=None):
    seq = attn.shape[0]
    tb = min(seq, WIDE_TOKEN_TILE)

    def body(a_ref, gb_ref, gc_ref, xin_ref, gch_ref, xinh_ref, cw_ref, ga_ref, gcn_ref, w_ref, mix_ref, mixed_ref):
        first = pl.program_id(0) == 0
        _, _, _, y = _conv_parts(gc_ref[...], xin_ref[...], gch_ref[...], xinh_ref[...], cw_ref[...], first)
        conv = gb_ref[...].astype(F32) * y
        a = a_ref[...].astype(F32)
        mixed_ref[:, 0:Q_WIDTH] = (a * _rms(a) * ga_ref[...]).astype(BF16)
        mixed_ref[:, Q_WIDTH:] = (conv * _rms(conv) * gcn_ref[...]).astype(BF16)
        mix_ref[...] = _dot(mixed_ref[...], w_ref[...].reshape(D_MODEL, D_MODEL)).astype(BF16)

    tile = lambda w: pl.BlockSpec((tb, w), lambda i: (i, 0))
    return _pallas(
        body, name="mix_out", grid=(seq // tb,),
        in_specs=[tile(Q_WIDTH), tile(CONV_WIDTH), tile(CONV_WIDTH), tile(CONV_WIDTH),
                  _halo_prev(tb, CONV_WIDTH), _halo_prev(tb, CONV_WIDTH),
                  _resident((CONV_K, CONV_WIDTH)), _resident((1, Q_WIDTH)), _resident((1, CONV_WIDTH)),
                  _resident(w_out.shape)],
        out_specs=[tile(D_MODEL), tile(D_MODEL)],
        out_shape=[SDS((seq, D_MODEL), BF16), SDS((seq, D_MODEL), BF16)],
        operands=(attn, gb, gc, xin, gc, xin, conv_w, g_attn, g_conv, w_out), comm=comm)


def _mlp_loss(x, mix, target, g_post_mix, g_pre_mlp, g_post_mlp, w_up, w_down):
    seq = x.shape[0]
    tb = TOKEN_TILE

    def body(x_ref, mix_ref, t_ref, gpm_ref, g2_ref, g4_ref, wup_ref, wdown_ref,
             up_ref, hn2_ref, dout_ref, dmlp_ref, loss_ref, dg4_ref, act_ref):
        @pl.when(pl.program_id(0) == 0)
        def _():
            loss_ref[...] = jnp.zeros_like(loss_ref)
            dg4_ref[...] = jnp.zeros_like(dg4_ref)

        halves = [slice(0, tb // 2), slice(tb // 2, tb)]
        hv, hn2 = [], []
        for rows in halves:
            hv.append(_residual_mid(x_ref[rows, :], mix_ref[rows, :], gpm_ref[...]))
            hn2.append((hv[-1] * _rms(hv[-1]) * g2_ref[...]).astype(BF16))
            hn2_ref[rows, :] = hn2[-1]
        for k, rows in enumerate(halves):
            for j in range(N_CHIPS):
                up = _dot(hn2[k], _chip_block(wup_ref, j))
                up = jnp.maximum(up, 0.0)
                up_ref[rows, 1024 * j:1024 * (j + 1)] = up.astype(BF16)
                act_ref[rows, 1024 * j:1024 * (j + 1)] = (up * up).astype(BF16)
        w_down_all = wdown_ref[...].reshape(D_FF, D_MODEL)
        loss = jnp.zeros((1, 1), F32)
        dg4 = jnp.zeros((1, D_MODEL), F32)
        for k, rows in enumerate(halves):
            mlp = _dot(act_ref[rows, :], w_down_all)
            rstd = _rms(mlp)
            zhat = mlp * rstd
            diff = hv[k] + zhat * g4_ref[...] - t_ref[rows, :]
            loss = loss + jnp.sum(jnp.sum(diff * diff, axis=1, keepdims=True), axis=0, keepdims=True)
            dout = diff * (1.0 / D_MODEL)
            dout_ref[rows, :] = dout
            dg4 = dg4 + _colsum(dout * zhat)
            dmlp_ref[rows, :] = _norm_bwd(dout, g4_ref[...], zhat, rstd).astype(BF16)
        loss_ref[...] += loss
        dg4_ref[...] += dg4

    tile = lambda w: pl.BlockSpec((tb, w), lambda i: (i, 0))
    return _pallas(
        body, name="mlp_loss", grid=(seq // tb,),
        in_specs=[tile(D_MODEL), tile(D_MODEL), tile(D_MODEL), _resident((1, D_MODEL)), _resident((1, D_MODEL)),
                  _resident((1, D_MODEL)), _resident(w_up.shape), _resident(w_down.shape)],
        out_specs=[tile(D_FF), tile(D_MODEL), tile(D_MODEL), tile(D_MODEL),
                   pl.BlockSpec((1, 1), lambda i: (0, 0)), pl.BlockSpec((1, D_MODEL), lambda i: (0, 0))],
        out_shape=[SDS((seq, D_FF), BF16), SDS((seq, D_MODEL), BF16), SDS((seq, D_MODEL), F32),
                   SDS((seq, D_MODEL), BF16), SDS((1, 1), F32), SDS((1, D_MODEL), F32)],
        scratch=[pltpu.VMEM((tb, D_FF), BF16)],
        operands=(x, mix, target, g_post_mix, g_pre_mlp, g_post_mlp, w_up, w_down))


def _mlp_bwd(dmlp, up, x, dout, mix, g_pre_mlp, g_post_mix, w_up, w_down):
    seq = x.shape[0]
    tb = MLP_BWD_TOKEN_TILE

    def body(dmlp_ref, up_ref, x_ref, dout_ref, mix_ref, g2_ref, gpm_ref, wup_ref, wdown_ref,
             dup_ref, dh_ref, dmix_ref, dg2_ref, dgpm_ref):
        @pl.when(pl.program_id(0) == 0)
        def _():
            dg2_ref[...] = jnp.zeros_like(dg2_ref)
            dgpm_ref[...] = jnp.zeros_like(dgpm_ref)

        subs = [slice(k * MLP_BWD_SUB_TILE, (k + 1) * MLP_BWD_SUB_TILE) for k in range(tb // MLP_BWD_SUB_TILE)]
        dhn2 = []
        for rows in subs:
            dmlp_v = dmlp_ref[rows, :]
            acc = None
            for j in range(N_CHIPS):
                cols = slice(1024 * j, 1024 * (j + 1))
                dact = _dot_nt(dmlp_v, _chip_block(wdown_ref, j))
                dup = (dact * (2.0 * up_ref[rows, cols].astype(F32))).astype(BF16)
                dup_ref[rows, cols] = dup
                part = _dot_nt(dup, _chip_block(wup_ref, j))
                acc = part if acc is None else acc + part
            dhn2.append(acc)
        dg2 = jnp.zeros((1, D_MODEL), F32)
        dgpm = jnp.zeros((1, D_MODEL), F32)
        for k, rows in enumerate(subs):
            mix_v = mix_ref[rows, :].astype(F32)
            hv = _residual_mid(x_ref[rows, :], mix_ref[rows, :], gpm_ref[...])
            r2 = _rms(hv)
            hhat = hv * r2
            dg2 = dg2 + _colsum(dhn2[k] * hhat)
            dh = dout_ref[rows, :] + _norm_bwd(dhn2[k], g2_ref[...], hhat, r2)
            dh_ref[rows, :] = dh.astype(BF16)
            rz = _rms(mix_v)
            zhat = mix_v * rz
            dgpm = dgpm + _colsum(dh * zhat)
            dmix_ref[rows, :] = _norm_bwd(dh, gpm_ref[...], zhat, rz).astype(BF16)
        dg2_ref[...] += dg2
        dgpm_ref[...] += dgpm

    tile = lambda w: pl.BlockSpec((tb, w), lambda i: (i, 0))
    vec = pl.BlockSpec((1, D_MODEL), lambda i: (0, 0))
    return _pallas(
        body, name="mlp_bwd", grid=(seq // tb,),
        in_specs=[tile(D_MODEL), tile(D_FF), tile(D_MODEL), tile(D_MODEL), tile(D_MODEL),
                  _resident((1, D_MODEL)), _resident((1, D_MODEL)), _resident(w_up.shape), _resident(w_down.shape)],
        out_specs=[tile(D_FF), tile(D_MODEL), tile(D_MODEL), vec, vec],
        out_shape=[SDS((seq, D_FF), BF16), SDS((seq, D_MODEL), BF16), SDS((seq, D_MODEL), BF16),
                   SDS((1, D_MODEL), F32), SDS((1, D_MODEL), F32)],
        operands=(dmlp, up, x, dout, mix, g_pre_mlp, g_post_mix, w_up, w_down))


def _mix_bwd(dmix, attn, gb, gc, xin, conv_w, g_attn, g_conv, w_out, n_k):
    seq = attn.shape[0]
    tb = seq // (N_CHIPS * n_k)

    def body(first, dmix_ref, a_ref, gb_ref, gc_ref, xin_ref, gch_ref, xinh_ref, cw_ref, ga_ref, gcn_ref, w_ref,
             dattn_ref, dgb_ref, dy_ref, dga_ref, dgcn_ref, dcw_ref):
        @pl.when(first)
        def _():
            dga_ref[...] = jnp.zeros_like(dga_ref)
            dgcn_ref[...] = jnp.zeros_like(dgcn_ref)
            dcw_ref[...] = jnp.zeros_like(dcw_ref)

        dmixed = _dot_nt(dmix_ref[...], w_ref[...].reshape(D_MODEL, D_MODEL))
        a = a_ref[...].astype(F32)
        ra = _rms(a)
        ahat = a * ra
        dan = dmixed[:, 0:Q_WIDTH]
        dga_ref[...] += _colsum(dan * ahat)
        dattn_ref[...] = _norm_bwd(dan, ga_ref[...], ahat, ra).astype(BF16)
        gbv = gb_ref[...].astype(F32)
        u, u1, u2, y = _conv_parts(gc_ref[...], xin_ref[...], gch_ref[...], xinh_ref[...], cw_ref[...], first)
        conv = gbv * y
        rc = _rms(conv)
        chat = conv * rc
        dcn = dmixed[:, Q_WIDTH:]
        dgcn_ref[...] += _colsum(dcn * chat)
        dconv = _norm_bwd(dcn, gcn_ref[...], chat, rc)
        dgb_ref[...] = (dconv * y).astype(BF16)
        dy = dconv * gbv
        dy_ref[...] = dy.astype(BF16)
        dcw_ref[0:1, :] += _colsum(dy * u2)
        dcw_ref[1:2, :] += _colsum(dy * u1)
        dcw_ref[2:3, :] += _colsum(dy * u)

    tile = lambda w: pl.BlockSpec((tb, w), lambda j, k: (j * n_k + k, 0))
    halo = lambda w: pl.BlockSpec((HALO, w), lambda j, k: (jnp.maximum((j * n_k + k) * (tb // HALO) - 1, 0), 0))
    whole = lambda shape: pl.BlockSpec(shape, lambda j, k: (0,) * len(shape))
    return _Rider(
        body,
        in_specs=[tile(D_MODEL), tile(Q_WIDTH), tile(CONV_WIDTH), tile(CONV_WIDTH), tile(CONV_WIDTH),
                  halo(CONV_WIDTH), halo(CONV_WIDTH),
                  _resident((CONV_K, CONV_WIDTH)), _resident((1, Q_WIDTH)), _resident((1, CONV_WIDTH)),
                  _resident(w_out.shape)],
        out_specs=[tile(Q_WIDTH), tile(CONV_WIDTH), tile(CONV_WIDTH),
                   whole((1, Q_WIDTH)), whole((1, CONV_WIDTH)), whole((CONV_K, CONV_WIDTH))],
        out_shape=[SDS((seq, Q_WIDTH), BF16), SDS((seq, CONV_WIDTH), BF16), SDS((seq, CONV_WIDTH), BF16),
                   SDS((1, Q_WIDTH), F32), SDS((1, CONV_WIDTH), F32), SDS((CONV_K, CONV_WIDTH), F32)],
        operands=(dmix, attn, gb, gc, xin, gc, xin, conv_w, g_attn, g_conv, w_out))


def _attention_bwd(q, dattn, attn, kd0, kd1, vd0, vd1, sinks, comm=None):
    seq = q.shape[0]
    nb = ATTN_BWD_BLOCKS

    def body(sink_ref, q_ref, do_ref, o_ref, kd0_ref, kd1_ref, vd0_ref, vd1_ref,
             dq_ref, dk0_ref, dk1_ref, dv0_ref, dv1_ref, dsink_ref):
        @pl.when(pl.program_id(0) == 0)
        def _():
            for r in (dk0_ref, dk1_ref, dv0_ref, dv1_ref, dsink_ref):
                r[...] = jnp.zeros_like(r)

        lane = lax.broadcasted_iota(jnp.int32, (1, 128), 1)
        dsink = jnp.zeros((1, 128), F32)
        for b in range(nb):
            i = pl.program_id(0) * nb + b
            rows = slice(QBLOCK * b, QBLOCK * (b + 1))
            valid = _attn_valid(i)
            for kv_head, (k_ref, v_ref, dk_ref, dv_ref) in enumerate(
                    ((kd0_ref, vd0_ref, dk0_ref, dv0_ref), (kd1_ref, vd1_ref, dk1_ref, dv1_ref))):
                kband, prev, own = _band(k_ref, i)
                vband, _, _ = _band(v_ref, i)
                base = 256 * kv_head
                qm = _stack_heads(q_ref[rows, base:base + 128], q_ref[rows, base + 128:base + 256])
                dom = _stack_heads(do_ref[rows, base:base + 128], do_ref[rows, base + 128:base + 256])
                om = _stack_heads(o_ref[rows, base:base + 128], o_ref[rows, base + 128:base + 256])
                s = jnp.where(valid, _dot_nt(qm, kband), NEG_INF)
                p, e_sink, inv_l = _softmax_with_sink(s, _sink_column(sink_ref, kv_head))
                p = p * inv_l
                delta = jnp.sum(dom.astype(F32) * om.astype(F32), axis=-1, keepdims=True)
                ds = (p * (_dot_nt(dom, vband) - delta)).astype(BF16)
                sink_term = -(e_sink * inv_l) * delta
                for j in range(4):
                    part = jnp.sum(sink_term[QBLOCK * j:QBLOCK * (j + 1)], axis=0, keepdims=True)
                    dsink = dsink + jnp.where(lane == 4 * kv_head + j, part, 0.0)
                pair0, pair1 = _unstack_heads(_dot(ds, kband))
                dq_ref[rows, base:base + 128] = pair0.astype(BF16)
                dq_ref[rows, base + 128:base + 256] = pair1.astype(BF16)
                dkd = _dot_tn(ds, qm)
                dkd = dkd + pltpu.roll(dkd, HEAD_DIM, 1)
                dvd = _dot_tn(p.astype(BF16), dom)
                dvd = dvd + pltpu.roll(dvd, HEAD_DIM, 1)
                dk_ref[pl.ds(prev, QBLOCK), :] += dkd[0:QBLOCK]
                dk_ref[pl.ds(own, QBLOCK), :] += dkd[QBLOCK:]
                dv_ref[pl.ds(prev, QBLOCK), :] += dvd[0:QBLOCK]
                dv_ref[pl.ds(own, QBLOCK), :] += dvd[QBLOCK:]
        dsink_ref[...] += dsink

    blk = pl.BlockSpec((nb * QBLOCK, Q_WIDTH), lambda i: (i, 0))
    full = _resident((seq, 128))
    acc = pl.BlockSpec((seq, 128), lambda i: (0, 0))
    return _pallas(
        body, name="attention_bwd", grid=(seq // (nb * QBLOCK),),
        in_specs=[pl.BlockSpec(memory_space=pltpu.SMEM), blk, blk, blk, full, full, full, full],
        out_specs=[blk, acc, acc, acc, acc, pl.BlockSpec((1, 128), lambda i: (0, 0))],
        out_shape=[SDS((seq, Q_WIDTH), BF16)] + [SDS((seq, 128), F32)] * 4 + [SDS((1, 128), F32)],
        operands=(sinks, q, dattn, attn, kd0, kd1, vd0, vd1), comm=comm)


def _in_proj_bwd(dq, dk0, dk1, dv0, dv1, dgb, dy, gc, xin, conv_w, x, dh, g_pre, w_in_t, rope):
    seq = x.shape[0]
    tb = TOKEN_TILE
    n_tiles = seq // tb

    def body(dq_ref, dk0_ref, dk1_ref, dv0_ref, dv1_ref, dgb_ref, dy_ref, dyh_ref, gc_ref, xin_ref, cw_ref,
             x_ref, dh_ref, g_ref, w_ref, c_ref, sa_ref, sb_ref,
             dproj_ref, gx_ref, dg_ref):
        i = pl.program_id(0)

        @pl.when(i == 0)
        def _():
            dg_ref[...] = jnp.zeros_like(dg_ref)

        dy = dy_ref[...].astype(F32)
        ext = jnp.concatenate([dy, jnp.where(i == n_tiles - 1, 0.0, dyh_ref[...].astype(F32))], axis=0)
        dy1 = pltpu.roll(ext, tb + HALO - 1, 0)[0:tb]
        dy2 = pltpu.roll(ext, tb + HALO - 2, 0)[0:tb]
        cw = cw_ref[...]
        du = cw[2:3, :] * dy + cw[1:2, :] * dy1 + cw[0:1, :] * dy2
        scale = 1.0 / math.sqrt(HEAD_DIM)
        base = Q_WIDTH + 2 * KV_WIDTH
        halves = [slice(0, tb // 2), slice(tb // 2, tb)]
        low = _lane_lt64((tb // 2, 128))
        for rows in halves:
            c, sa, sb = _rope_tile(c_ref.at[rows, :], sa_ref, sb_ref)
            for p in range(Q_WIDTH // 128):
                dproj_ref[rows, 128 * p:128 * (p + 1)] = _rope_transposed(
                    dq_ref[rows, 128 * p:128 * (p + 1)].astype(F32) * scale, c, sa, sb).astype(BF16)
            dk = jnp.where(low, dk0_ref[rows, :], dk1_ref[rows, :])
            dproj_ref[rows, Q_WIDTH:Q_WIDTH + KV_WIDTH] = _rope_transposed(dk, c, sa, sb).astype(BF16)
            dproj_ref[rows, Q_WIDTH + KV_WIDTH:base] = jnp.where(low, dv0_ref[rows, :], dv1_ref[rows, :]).astype(BF16)
            dproj_ref[rows, base:base + CONV_WIDTH] = dgb_ref[rows, :]
            dproj_ref[rows, base + CONV_WIDTH:base + 2 * CONV_WIDTH] = (du[rows] * xin_ref[rows, :].astype(F32)).astype(BF16)
            dproj_ref[rows, base + 2 * CONV_WIDTH:] = (du[rows] * gc_ref[rows, :].astype(F32)).astype(BF16)
        w_all = w_ref[...].reshape(IN_COLS, D_MODEL)
        dhn = [_dot(dproj_ref[rows, :], w_all) for rows in halves]
        dg = jnp.zeros((1, D_MODEL), F32)
        for k, rows in enumerate(halves):
            xv = x_ref[rows, :]
            r = _rms(xv)
            xhat = xv * r
            dg = dg + _colsum(dhn[k] * xhat)
            gx_ref[rows, :] = dh_ref[rows, :].astype(F32) + _norm_bwd(dhn[k], g_ref[...], xhat, r)
        dg_ref[...] += dg

    tile = lambda w: pl.BlockSpec((tb, w), lambda i: (i, 0))
    halo_next = pl.BlockSpec((HALO, CONV_WIDTH), lambda i: (jnp.minimum((i + 1) * (tb // HALO), seq // HALO - 1), 0))
    return _pallas(
        body, name="in_proj_bwd", grid=(n_tiles,),
        in_specs=[tile(Q_WIDTH), tile(128), tile(128), tile(128), tile(128), tile(CONV_WIDTH), tile(CONV_WIDTH), halo_next,
                  tile(CONV_WIDTH), tile(CONV_WIDTH), _resident((CONV_K, CONV_WIDTH)),
                  tile(D_MODEL), tile(D_MODEL), _resident((1, D_MODEL)), _resident(w_in_t.shape), *_rope_specs(tb)],
        out_specs=[tile(IN_COLS), tile(D_MODEL), pl.BlockSpec((1, D_MODEL), lambda i: (0, 0))],
        out_shape=[SDS((seq, IN_COLS), BF16), SDS((seq, D_MODEL), F32), SDS((1, D_MODEL), F32)],
        operands=(dq, dk0, dk1, dv0, dv1, dgb, dy, dy, gc, xin, conv_w, x, dh, g_pre, w_in_t, *rope))


def _wgrad_grid(seq, per_chip, h_rows):
    chips_per_step = 1 if per_chip else N_CHIPS
    m = chips_per_step * 2 * h_rows
    bt = min(seq, WGRAD_TOKEN_TILE)
    return chips_per_step, m, bt, seq // bt


def _wgrad(name, a, b, *, per_chip, h_rows, square_a=False, comm=None, rider=None):
    seq = a.shape[0]
    chips_per_step, m, bt, n_k = _wgrad_grid(seq, per_chip, h_rows)
    a_cols = m if per_chip else a.shape[1]
    a_wide = a.shape[1] > a_cols
    b_wide = b.shape[1] > D_MODEL

    def body(a_ref, b_ref, g_ref):
        @pl.when(pl.program_id(1) == 0)
        def _():
            g_ref[...] = jnp.zeros_like(g_ref)

        av = a_ref[...]
        if square_a:
            av = (av.astype(F32) * av.astype(F32)).astype(BF16)
        g_ref[...] += _dot_tn(av, b_ref[...]).reshape(g_ref.shape)

    a_spec = pl.BlockSpec((bt, a_cols), (lambda j, k: (k, j)) if a_wide else (lambda j, k: (k, 0)))
    b_spec = pl.BlockSpec((bt, D_MODEL), (lambda j, k: (k, j)) if b_wide else (lambda j, k: (k, 0)))
    g_spec = pl.BlockSpec((chips_per_step, 2, h_rows, D_MODEL), lambda j, k: (j, 0, 0, 0),
                          pipeline_mode=None if per_chip else pl.Buffered(1))
    return _pallas(
        body, name=name, grid=(N_CHIPS if per_chip else 1, n_k),
        in_specs=[a_spec, b_spec], out_specs=[g_spec], out_shape=[SDS((N_CHIPS, 2, h_rows, D_MODEL), F32)],
        operands=(a, b), comm=comm, rider=rider)


def _adamw_math(w, g, m, v):
    m = ADAM_B1 * m + (1.0 - ADAM_B1) * g
    v = ADAM_B2 * v + (1.0 - ADAM_B2) * (g * g)
    m_hat = m / (1.0 - ADAM_B1 ** ADAM_STEP)
    v_hat = v / (1.0 - ADAM_B2 ** ADAM_STEP)
    delta = -ADAM_LR * (m_hat / (jnp.sqrt(v_hat) + ADAM_EPS) + ADAM_WD * w)
    return delta, m, v


def _adamw_rows(name, reduced, w, m, v, rt):
    per_half = reduced.shape[1] // rt

    def body(r_ref, w_ref, m_ref, v_ref, g_out, d_out, m_out, v_out):
        g = r_ref[0]
        g_out[...] = g
        d_out[...], m_out[...], v_out[...] = _adamw_math(w_ref[...], g, m_ref[...], v_ref[...])

    blk = pl.BlockSpec((rt, D_MODEL), lambda h, r: (h * per_half + r, 0))
    return _pallas(
        body, name=name, grid=(2, per_half),
        in_specs=[pl.BlockSpec((1, rt, D_MODEL), lambda h, r: (h, r, 0)), blk, blk, blk],
        out_specs=[blk, blk, blk, blk], out_shape=[SDS(w.shape, F32)] * 4, operands=(reduced, w, m, v))


def _adamw_small(w, g, m, v):
    def body(w_ref, g_ref, m_ref, v_ref, d_out, m_out, v_out):
        d_out[...], m_out[...], v_out[...] = _adamw_math(w_ref[...], g_ref[...], m_ref[...], v_ref[...])

    return pl.pallas_call(body, name="adamw_small", in_specs=[VMEM_WHOLE] * 4, out_specs=[VMEM_WHOLE] * 3,
                          out_shape=[SDS(w.shape, F32)] * 3)(w, g, m, v)


SMALL_VECTORS = ("pre_mix_norm", "post_mix_norm", "pre_mlp_norm", "post_mlp_norm")
SMALL_NAMES = SMALL_VECTORS + ("attn_group_norm", "conv_group_norm", "conv_w", "attn_sinks")


def _pack_small(p):
    rows = [p[n].reshape(1, D_MODEL) for n in SMALL_VECTORS]
    rows.append(jnp.concatenate([p["attn_group_norm"].reshape(1, -1), p["conv_group_norm"].reshape(1, -1)], axis=1))
    cw = p["conv_w"].reshape(CONV_K, -1)
    rows.append(jnp.pad(cw, ((0, 1), (0, CONV_WIDTH - cw.shape[1]))).reshape(2, D_MODEL))
    last = jnp.concatenate([p["attn_sinks"].reshape(1, 8), p.get("loss_sum", jnp.zeros((1, 1), F32))], axis=1)
    rows.append(jnp.pad(last, ((0, 0), (0, D_MODEL - 9))))
    return jnp.concatenate(rows, axis=0)


def _unpack_small(packed, conv_width):
    out = {n: packed[i:i + 1] for i, n in enumerate(SMALL_VECTORS)}
    out["attn_group_norm"] = packed[4:5, :Q_WIDTH]
    out["conv_group_norm"] = packed[4:5, Q_WIDTH:]
    out["conv_w"] = packed[5:7].reshape(4, CONV_WIDTH)[:CONV_K, :conv_width].reshape(1, CONV_K, conv_width)
    out["attn_sinks"] = packed[7:8, :8]
    out["loss_sum"] = packed[7, 8]
    return out


WEIGHT_ORDER = ("pre_mix_norm", "w_in", "conv_w", "attn_sinks", "attn_group_norm", "conv_group_norm", "w_out",
                "post_mix_norm", "pre_mlp_norm", "w_up", "w_down", "post_mlp_norm")


def kernel(x, pre_mix_norm, w_in, conv_w, attn_sinks, attn_group_norm, conv_group_norm, w_out, post_mix_norm, pre_mlp_norm, w_up, w_down, post_mlp_norm, loss_target, m_pre_mix_norm, m_w_in, m_conv_w, m_attn_sinks, m_attn_group_norm, m_conv_group_norm, m_w_out, m_post_mix_norm, m_pre_mlp_norm, m_w_up, m_w_down, m_post_mlp_norm, v_pre_mix_norm, v_w_in, v_conv_w, v_attn_sinks, v_attn_group_norm, v_conv_group_norm, v_w_out, v_post_mix_norm, v_pre_mlp_norm, v_w_up, v_w_down, v_post_mlp_norm):
    w = dict(pre_mix_norm=pre_mix_norm, w_in=w_in, conv_w=conv_w, attn_sinks=attn_sinks, attn_group_norm=attn_group_norm,
             conv_group_norm=conv_group_norm, w_out=w_out, post_mix_norm=post_mix_norm, pre_mlp_norm=pre_mlp_norm,
             w_up=w_up, w_down=w_down, post_mlp_norm=post_mlp_norm)
    m = dict(pre_mix_norm=m_pre_mix_norm, w_in=m_w_in, conv_w=m_conv_w, attn_sinks=m_attn_sinks,
             attn_group_norm=m_attn_group_norm, conv_group_norm=m_conv_group_norm, w_out=m_w_out,
             post_mix_norm=m_post_mix_norm, pre_mlp_norm=m_pre_mlp_norm, w_up=m_w_up, w_down=m_w_down,
             post_mlp_norm=m_post_mlp_norm)
    v = dict(pre_mix_norm=v_pre_mix_norm, w_in=v_w_in, conv_w=v_conv_w, attn_sinks=v_attn_sinks,
             attn_group_norm=v_attn_group_norm, conv_group_norm=v_conv_group_norm, w_out=v_w_out,
             post_mix_norm=v_post_mix_norm, pre_mlp_norm=v_pre_mlp_norm, w_up=v_w_up, w_down=v_w_down,
             post_mlp_norm=v_post_mlp_norm)
    core = lax.axis_index("c").astype(jnp.int32).reshape(1)
    chip = 2 * lax.axis_index("x") + lax.axis_index("y")
    local_conv = conv_w.shape[2]
    xs, target = x[0], loss_target[0]
    rope = _rope_inputs(xs.shape[0])

    hb_up, hb_down, hb_out, hb_in = _cast_halves(core, w_up[0], w_down[0], w_out[0], w_in[0].T)
    conv_pad = jnp.pad(conv_w[0], ((0, 8 - CONV_K), (0, 0)))
    wf_in, conv_all = _gather_whole(hb_in, conv_pad)
    conv_full = conv_all[:, :CONV_K, :].transpose(1, 0, 2).reshape(CONV_K, CONV_WIDTH)

    *proj, wf_up, wf_out = _in_proj(xs, pre_mix_norm, wf_in, rope, comm=_merge(_relay_first(hb_up), _gather_first(hb_out)))
    q, kd0, kd1, vd0, vd1, gb, gc, xin, hn = proj
    attn, wf_up, wf_out, wf_down = _attention_fwd(
        q, kd0, kd1, vd0, vd1, attn_sinks,
        comm=_merge(_relay_second(wf_up), _gather_second(wf_out), _relay_first(hb_down)))
    mix, mixed, wf_up, wf_down = _mix_out(attn, gb, gc, xin, conv_full, attn_group_norm, conv_group_norm, wf_out,
                                          comm=_merge(_relay_third(wf_up), _relay_second(wf_down, then_third=True)))
    up, hn2, dout, dmlp, loss_sum, dg_post_mlp = _mlp_loss(xs, mix, target, post_mix_norm, pre_mlp_norm, post_mlp_norm,
                                                           wf_up, wf_down)

    dup, dh, dmix, dg_pre_mlp, dg_post_mix = _mlp_bwd(dmlp, up, xs, dout, mix, pre_mlp_norm, post_mix_norm, wf_up, wf_down)
    n_k = _wgrad_grid(xs.shape[0], True, H_DOWN)[3]
    g_down, dattn, dgb, dy, dg_attn, dg_conv, dconv_w = _wgrad(
        "wgrad_down", up, dmlp, per_chip=True, h_rows=H_DOWN, square_a=True,
        rider=_mix_bwd(dmix, attn, gb, gc, xin, conv_full, attn_group_norm, conv_group_norm, wf_out, n_k))
    g_up, got_down = _wgrad("wgrad_up", hn2, dup, per_chip=True, h_rows=H_UP, comm=_pair_send(g_down))
    p_down = _pair_sum("pair_sum_down", core, g_down, got_down)
    g_out, got_up = _wgrad("wgrad_out", mixed, dmix, per_chip=False, h_rows=H_OUT, comm=_pair_send(g_up))
    p_up = _pair_sum("pair_sum_up", core, g_up, got_up)
    dq, dk0, dk1, dv0, dv1, dsink, ex_down, ex_up, got_out = _attention_bwd(
        q, dattn, attn, kd0, kd1, vd0, vd1, attn_sinks,
        comm=_merge(_chip_exchange(p_down), _chip_exchange(p_up), _pair_send(g_out)))
    p_out = _pair_sum("pair_sum_out", core, g_out, got_out)
    dproj, grad_x, dg_pre_mix = _in_proj_bwd(dq, dk0, dk1, dv0, dv1, dgb, dy, gc, xin, conv_full, xs, dh, pre_mix_norm,
                                             wf_in, rope)
    g_in, ex_out = _wgrad("wgrad_in", dproj, hn, per_chip=False, h_rows=H_IN, comm=_chip_exchange(p_out))
    small = dict(pre_mix_norm=dg_pre_mix, conv_w=dconv_w, attn_sinks=dsink[:, :8], attn_group_norm=dg_attn,
                 conv_group_norm=dg_conv, post_mix_norm=dg_post_mix, pre_mlp_norm=dg_pre_mlp, post_mlp_norm=dg_post_mlp,
                 loss_sum=loss_sum)
    r_down, r_up, r_out, r_in, small_total = _tail_reduce(g_in, [ex_down, ex_up, ex_out], _pack_small(small))

    out_g, out_d, out_m, out_v = {}, {}, {}, {}
    out_g["w_up"], out_d["w_up"], out_m["w_up"], out_v["w_up"] = _adamw_rows(
        "adamw_up", r_up, w_up[0], m_w_up[0], v_w_up[0], 256)
    out_g["w_down"], out_d["w_down"], out_m["w_down"], out_v["w_down"] = _adamw_rows(
        "adamw_down", r_down, w_down[0], m_w_down[0], v_w_down[0], 256)
    out_g["w_out"], out_d["w_out"], out_m["w_out"], out_v["w_out"] = _adamw_rows(
        "adamw_out", r_out, w_out[0], m_w_out[0], v_w_out[0], H_OUT)
    in_t = _adamw_rows("adamw_in", r_in, w_in[0].T, m_w_in[0].T, v_w_in[0].T, H_IN)
    out_g["w_in"], out_d["w_in"], out_m["w_in"], out_v["w_in"] = [t.T for t in in_t]

    small_sum = _unpack_small(small_total, CONV_WIDTH)
    loss = small_sum["loss_sum"] * (0.5 / D_MODEL)
    small_sum["conv_w"] = lax.dynamic_slice_in_dim(small_sum["conv_w"], chip * local_conv, local_conv, axis=2)
    packed = [_pack_small({n: t[n] for n in SMALL_NAMES}) for t in (w, small_sum, m, v)]
    small_d, small_m, small_v = [_unpack_small(t, local_conv) for t in _adamw_small(*packed)]
    for n in SMALL_NAMES:
        out_g[n], out_d[n], out_m[n], out_v[n] = small_sum[n], small_d[n], small_m[n], small_v[n]

    def shaped(d):
        return [d[n].reshape(w[n].shape) for n in WEIGHT_ORDER]

    return (loss, grad_x[None], *shaped(out_g), *shaped(out_d), *shaped(out_m), *shaped(out_v))
```

```python
import math
from typing import Callable, NamedTuple

import jax
import jax.numpy as jnp
import numpy as np
from jax import lax
from jax.experimental import pallas as pl
from jax.experimental.pallas import tpu as pltpu

F32 = jnp.float32
BF16 = jnp.bfloat16

D_MODEL = 1024
HEAD_DIM = 64
Q_WIDTH = 512
KV_WIDTH = 128
CONV_WIDTH = 512
CONV_K = 3
D_FF = 4096
IN_COLS = 2304
QBLOCK = 128
ROT_DIM = 16
ROPE_THETA = 500000.0
NORM_EPS = 1e-6
NEG_INF = -1e30
N_CHIPS = 4

ADAM_LR = 0.001
ADAM_B1 = 0.9
ADAM_B2 = 0.999
ADAM_EPS = 1e-08
ADAM_WD = 0.01
ADAM_STEP = 10

H_UP, H_DOWN, H_OUT, H_IN = 512, 512, 128, 288

TOKEN_TILE = 512
WIDE_TOKEN_TILE = 1024
MLP_BWD_TOKEN_TILE = 512
MLP_BWD_SUB_TILE = 256
ATTN_FWD_BLOCKS = 16
ATTN_BWD_BLOCKS = 2
WGRAD_TOKEN_TILE = 2048
VMEM_LIMIT_V7X = 56 * 1024 * 1024

MESH = pl.DeviceIdType.MESH
ANY = pl.BlockSpec(memory_space=pl.ANY)
VMEM_WHOLE = pl.BlockSpec(memory_space=pltpu.VMEM)
SDS = jax.ShapeDtypeStruct


def _resident(shape):
    zeros = (0,) * len(shape)
    return pl.BlockSpec(shape, lambda *_: zeros, pipeline_mode=pl.Buffered(1))


def _rms(v):
    return lax.rsqrt(jnp.mean(v * v, axis=-1, keepdims=True) + NORM_EPS)


def _norm_bwd(dy, gain, vhat, rstd):
    t = dy * gain
    return rstd * (t - vhat * jnp.mean(t * vhat, axis=-1, keepdims=True))


def _colsum(v):
    return jnp.sum(v, axis=0, keepdims=True)


def _dot_nt(a, b):
    return lax.dot_general(a, b, (((1,), (1,)), ((), ())), preferred_element_type=F32)


def _dot_tn(a, b):
    return lax.dot_general(a, b, (((0,), (0,)), ((), ())), preferred_element_type=F32)


def _dot(a, b):
    return jnp.dot(a, b, preferred_element_type=F32)


def _chip_block(w_ref, chip):
    both = w_ref[pl.ds(2 * chip, 2)]
    return both.reshape(2 * both.shape[1], both.shape[2])


def _lane_lt64(shape):
    return lax.broadcasted_iota(jnp.int32, shape, 1) < HEAD_DIM


class _Comm(NamedTuple):
    operands: tuple
    out_shapes: tuple
    aliases: dict
    n_remote: int
    n_local: int
    plan: Callable
    after: Callable = None


def _merge(*comms):
    operands, out_shapes, aliases, parts = [], [], {}, []
    n_remote = n_local = 0
    for cm in comms:
        parts.append((len(operands), len(out_shapes), n_remote, n_local, cm))
        for k, v in cm.aliases.items():
            aliases[len(operands) + k] = len(out_shapes) + v
        operands += cm.operands
        out_shapes += cm.out_shapes
        n_remote += cm.n_remote
        n_local += cm.n_local

    def run(which, ins, outs, send, recv, loc):
        sends, recvs, locs = [], [], []
        for i0, o0, r0, l0, cm in parts:
            stage = getattr(cm, which)
            if stage is not None:
                s, r, l = stage(ins[i0:i0 + len(cm.operands)], outs[o0:o0 + len(cm.out_shapes)],
                                lambda k, r0=r0: send(r0 + k), lambda k, r0=r0: recv(r0 + k), lambda k, l0=l0: loc(l0 + k))
                sends, recvs, locs = sends + s, recvs + r, locs + l
        return sends, recvs, locs

    def plan(*args):
        return run("plan", *args)

    def after(*args):
        return run("after", *args)

    return _Comm(tuple(operands), tuple(out_shapes), aliases, n_remote, n_local, plan,
                 after if any(cm.after is not None for cm in comms) else None)


def _sem_scratch(comm):
    return [pltpu.SemaphoreType.DMA((max(comm.n_remote, 1),)), pltpu.SemaphoreType.DMA((max(comm.n_remote, 1),)),
            pltpu.SemaphoreType.DMA((max(comm.n_local, 1),))]


class _Rider(NamedTuple):
    body: Callable
    in_specs: list
    out_specs: list
    out_shape: list
    operands: tuple


def _pallas(body, *, name, grid, in_specs, out_specs, out_shape, operands, scratch=(), comm=None, rider=None):
    params = pltpu.CompilerParams(dimension_semantics=("arbitrary",) * len(grid), vmem_limit_bytes=VMEM_LIMIT_V7X)
    if rider is not None:
        own_in, own_out, ride_in, ride_out = len(in_specs), len(out_specs), len(rider.in_specs), len(rider.out_specs)
        own_body = body

        def body(*refs):
            o0 = own_in + ride_in
            s0 = o0 + own_out + ride_out
            own_body(*refs[:own_in], *refs[o0:o0 + own_out], *refs[s0:])
            first = None
            for axis in range(len(grid)):
                at_start = pl.program_id(axis) == 0
                first = at_start if first is None else jnp.logical_and(first, at_start)
            rider.body(first, *refs[own_in:o0], *refs[o0 + own_out:s0])

        in_specs, out_specs = list(in_specs) + rider.in_specs, list(out_specs) + rider.out_specs
        out_shape, operands = list(out_shape) + rider.out_shape, tuple(operands) + tuple(rider.operands)
    if comm is None:
        return pl.pallas_call(body, name=name, grid=grid, in_specs=in_specs, out_specs=out_specs, out_shape=out_shape,
                              scratch_shapes=list(scratch), compiler_params=params)(*operands)
    n_in, n_out, n_scr = len(in_specs), len(out_specs), len(scratch)
    c_in, c_out = len(comm.operands), len(comm.out_shapes)

    def with_comm(*refs):
        ins, c_ins = refs[:n_in], refs[n_in:n_in + c_in]
        o0 = n_in + c_in
        outs, c_outs = refs[o0:o0 + n_out], refs[o0 + n_out:o0 + n_out + c_out]
        s0 = o0 + n_out + c_out
        scr = refs[s0:s0 + n_scr]
        send_sems, recv_sems, local_sems = refs[s0 + n_scr:]
        first = last = None
        for axis, size in enumerate(grid):
            at_start, at_end = pl.program_id(axis) == 0, pl.program_id(axis) == size - 1
            first = at_start if first is None else jnp.logical_and(first, at_start)
            last = at_end if last is None else jnp.logical_and(last, at_end)

        def copies():
            return comm.plan(c_ins, c_outs, lambda k: send_sems.at[k], lambda k: recv_sems.at[k],
                             lambda k: local_sems.at[k])

        @pl.when(first)
        def _():
            sends, _, locs = copies()
            for cp in sends + locs:
                cp.start()

        body(*ins, *outs, *scr)

        @pl.when(last)
        def _():
            sends, recvs, locs = copies()
            for cp in recvs:
                cp.wait_recv()
            for cp in sends:
                cp.wait_send()
            for cp in locs:
                cp.wait()
            if comm.after is not None:
                sends, recvs, _ = comm.after(c_ins, c_outs, lambda k: send_sems.at[k], lambda k: recv_sems.at[k],
                                             lambda k: local_sems.at[k])
                for cp in sends:
                    cp.start()
                for cp in recvs:
                    cp.wait_recv()
                for cp in sends:
                    cp.wait_send()

    return pl.pallas_call(
        with_comm, name=name, grid=grid,
        in_specs=list(in_specs) + [ANY] * c_in, out_specs=list(out_specs) + [ANY] * c_out,
        out_shape=list(out_shape) + list(comm.out_shapes),
        scratch_shapes=list(scratch) + _sem_scratch(comm),
        input_output_aliases={n_in + k: n_out + v for k, v in comm.aliases.items()},
        compiler_params=params)(*operands, *comm.operands)


def _place():
    return lax.axis_index("x"), lax.axis_index("y"), lax.axis_index("c")


def _other_chips(x, y):
    return [(1 - x, y), (x, 1 - y), (1 - x, 1 - y)]


def _slot(px, py, pc):
    return 4 * px + 2 * py + pc


def _remote(src, dst, send_sem, recv_sem, to):
    return pltpu.make_async_remote_copy(src_ref=src, dst_ref=dst, send_sem=send_sem, recv_sem=recv_sem,
                                        device_id=to, device_id_type=MESH)


def _gather_first(half_block):
    def plan(ins, outs, send, recv, loc):
        (blk,), (full,) = ins, outs
        x, y, c = _place()
        chips = _other_chips(x, y)
        mine = full.at[_slot(x, y, c)]
        sends = [_remote(blk, mine, send(0), recv(0), (x, y, 1 - c))]
        sends += [_remote(blk, mine, send(1 + j), recv(1 + j), (*chip, c)) for j, chip in enumerate(chips)]
        recvs = [_remote(blk, full.at[_slot(x, y, 1 - c)], send(0), recv(0), (x, y, 1 - c))]
        recvs += [_remote(blk, full.at[_slot(*chip, c)], send(1 + j), recv(1 + j), (*chip, c))
                  for j, chip in enumerate(chips)]
        return sends, recvs, [pltpu.make_async_copy(blk, mine, loc(0))]

    return _Comm((half_block,), (SDS((2 * N_CHIPS,) + half_block.shape, half_block.dtype),), {}, 4, 1, plan)


def _gather_second(partly_gathered):
    def plan(ins, outs, send, recv, loc):
        (src,), (full,) = ins, outs
        x, y, c = _place()
        chips = _other_chips(x, y)
        sends = [_remote(src.at[_slot(*chip, c)], full.at[_slot(*chip, c)], send(j), recv(j), (x, y, 1 - c))
                 for j, chip in enumerate(chips)]
        recvs = [_remote(src.at[_slot(*chip, 1 - c)], full.at[_slot(*chip, 1 - c)], send(j), recv(j), (x, y, 1 - c))
                 for j, chip in enumerate(chips)]
        return sends, recvs, []

    return _Comm((partly_gathered,), (SDS(partly_gathered.shape, partly_gathered.dtype),), {0: 0}, 3, 0, plan)


def _relay_pieces(full, rows, x, y, c):
    half = rows // 2
    upper, lower = pl.ds(0, half), pl.ds(half, half)
    diagonal = full.at[_slot(1 - x, 1 - y, c)]
    return [(full.at[_slot(1 - x, y, c), upper], diagonal.at[upper], (x, 1 - y, c)),
            (full.at[_slot(x, 1 - y, c), lower], diagonal.at[lower], (1 - x, y, c))]


def _relay_first(half_block):
    def plan(ins, outs, send, recv, loc):
        (blk,), (full,) = ins, outs
        x, y, c = _place()
        peers = [(x, y, 1 - c), (1 - x, y, c), (x, 1 - y, c)]
        mine = full.at[_slot(x, y, c)]
        sends = [_remote(blk, mine, send(k), recv(k), peer) for k, peer in enumerate(peers)]
        recvs = [_remote(blk, full.at[_slot(*peer)], send(k), recv(k), peer) for k, peer in enumerate(peers)]
        return sends, recvs, [pltpu.make_async_copy(blk, mine, loc(0))]

    return _Comm((half_block,), (SDS((2 * N_CHIPS,) + half_block.shape, half_block.dtype),), {}, 3, 1, plan)


def _third_leg(src, full, send, recv, k):
    x, y, c = _place()
    sibling = (x, y, 1 - c)
    here, there = _slot(1 - x, 1 - y, c), _slot(1 - x, 1 - y, 1 - c)
    return ([_remote(src.at[here], full.at[here], send(k), recv(k), sibling)],
            [_remote(src.at[there], full.at[there], send(k), recv(k), sibling)], [])


def _relay_second(partly_gathered, then_third=False):
    rows = partly_gathered.shape[1]

    def plan(ins, outs, send, recv, loc):
        (src,), (full,) = ins, outs
        x, y, c = _place()
        sibling = (x, y, 1 - c)
        sends, recvs = [], []
        for k, chip in enumerate([(1 - x, y), (x, 1 - y)]):
            sends.append(_remote(src.at[_slot(*chip, c)], full.at[_slot(*chip, c)], send(k), recv(k), sibling))
            recvs.append(_remote(src.at[_slot(*chip, 1 - c)], full.at[_slot(*chip, 1 - c)], send(k), recv(k), sibling))
        for k, (piece, lands, peer) in enumerate(_relay_pieces(full, rows, x, y, c)):
            sends.append(_remote(piece, piece, send(2 + k), recv(2 + k), peer))
            recvs.append(_remote(lands, lands, send(2 + k), recv(2 + k), peer))
        return sends, recvs, []

    def after(ins, outs, send, recv, loc):
        return _third_leg(ins[0], outs[0], send, recv, 4)

    return _Comm((partly_gathered,), (SDS(partly_gathered.shape, partly_gathered.dtype),), {0: 0}, 5, 0, plan,
                 after if then_third else None)


def _relay_third(mostly_gathered):
    def plan(ins, outs, send, recv, loc):
        return _third_leg(ins[0], outs[0], send, recv, 0)

    return _Comm((mostly_gathered,), (SDS(mostly_gathered.shape, mostly_gathered.dtype),), {0: 0}, 1, 0, plan)


def _gather_whole(half_block, small_block):
    rows = half_block.shape[0]

    def body(blk_ref, small_ref, out_ref, small_out_ref, send_sems, recv_sems, local_sems):
        x, y, c = _place()
        me, sibling = (x, y, c), (x, y, 1 - c)
        neighbours, diagonal = [(1 - x, y), (x, 1 - y)], (1 - x, 1 - y)

        def copy(k, block, to, src=None):
            return _remote(out_ref.at[_slot(*block)] if src is None else src, out_ref.at[_slot(*block)],
                           send_sems.at[k], recv_sems.at[k], to)

        def small_copy(k, chip, to):
            return _remote(small_ref, small_out_ref.at[2 * chip[0] + chip[1]], send_sems.at[8 + k], recv_sems.at[8 + k], to)

        mine = pltpu.make_async_copy(blk_ref, out_ref.at[_slot(*me)], local_sems.at[0])
        mine_small = pltpu.make_async_copy(small_ref, small_out_ref.at[2 * x + y], local_sems.at[1])
        mine.start()
        mine_small.start()
        started = [copy(0, me, sibling, src=blk_ref)]
        started += [copy(1 + k, me, (*chip, c), src=blk_ref) for k, chip in enumerate(neighbours)]
        started += [small_copy(k, (x, y), (*chip, c)) for k, chip in enumerate(neighbours + [diagonal])]
        for cp in started:
            cp.start()
        pieces = _relay_pieces(out_ref, rows, x, y, c)
        for k, chip in enumerate(neighbours):
            copy(1 + k, (*chip, c), me).wait_recv()
            piece, _, peer = pieces[k]
            started += [copy(3 + k, (*chip, c), sibling), _remote(piece, piece, send_sems.at[5 + k], recv_sems.at[5 + k], peer)]
            started[-2].start()
            started[-1].start()
        for k, (_, lands, peer) in enumerate(pieces):
            _remote(lands, lands, send_sems.at[5 + k], recv_sems.at[5 + k], peer).wait_recv()
        started.append(copy(7, (*diagonal, c), sibling))
        started[-1].start()
        copy(0, sibling, me).wait_recv()
        for k, chip in enumerate(neighbours):
            copy(3 + k, (*chip, 1 - c), me).wait_recv()
        copy(7, (*diagonal, 1 - c), me).wait_recv()
        for k, chip in enumerate(neighbours + [diagonal]):
            small_copy(k, chip, me).wait_recv()
        for cp in started:
            cp.wait_send()
        mine.wait()
        mine_small.wait()

    return pl.pallas_call(
        body, name="gather_whole", in_specs=[ANY, ANY], out_specs=[ANY, ANY],
        out_shape=[SDS((2 * N_CHIPS,) + half_block.shape, half_block.dtype),
                   SDS((N_CHIPS,) + small_block.shape, small_block.dtype)],
        scratch_shapes=[pltpu.SemaphoreType.DMA((11,)), pltpu.SemaphoreType.DMA((11,)), pltpu.SemaphoreType.DMA((2,))],
    )(half_block, small_block)


def _pair_send(grads):
    def plan(ins, outs, send, recv, loc):
        (g,), (got,) = ins, outs
        x, y, c = _place()
        copies = [_remote(g.at[j, 1 - c], got.at[j], send(j), recv(j), (x, y, 1 - c)) for j in range(N_CHIPS)]
        return copies, copies, []

    shape = (grads.shape[0],) + grads.shape[2:]
    return _Comm((grads,), (SDS(shape, grads.dtype),), {}, N_CHIPS, 0, plan)


def _chip_exchange(partial):
    def plan(ins, outs, send, recv, loc):
        (p,), (got,) = ins, outs
        x, y, c = _place()
        my_chip = 2 * x + y
        chips = _other_chips(x, y)
        sends = [_remote(p.at[2 * chip[0] + chip[1]], got.at[my_chip], send(j), recv(j), (*chip, c))
                 for j, chip in enumerate(chips)]
        recvs = [_remote(p.at[my_chip], got.at[2 * chip[0] + chip[1]], send(j), recv(j), (*chip, c))
                 for j, chip in enumerate(chips)]
        return sends, recvs, [pltpu.make_async_copy(p.at[my_chip], got.at[my_chip], loc(0))]

    return _Comm((partial,), (SDS(partial.shape, partial.dtype),), {}, 3, 1, plan)


def _pair_sum(name, core, grads, received):
    h = grads.shape[2]

    def body(core_ref, g_ref, r_ref, o_ref):
        o_ref[...] = (g_ref[0] + r_ref[...]).astype(BF16)

    return pl.pallas_call(
        body, name=name,
        grid_spec=pltpu.PrefetchScalarGridSpec(
            num_scalar_prefetch=1, grid=(N_CHIPS,),
            in_specs=[pl.BlockSpec((1, 1, h, D_MODEL), lambda j, core_ref: (j, core_ref[0], 0, 0)),
                      pl.BlockSpec((1, h, D_MODEL), lambda j, core_ref: (j, 0, 0))],
            out_specs=pl.BlockSpec((1, h, D_MODEL), lambda j, core_ref: (j, 0, 0))),
        out_shape=SDS((N_CHIPS, h, D_MODEL), BF16),
        compiler_params=pltpu.CompilerParams(dimension_semantics=("arbitrary",), vmem_limit_bytes=VMEM_LIMIT_V7X),
    )(core, grads, received)


SMALL_ROWS = 8


def _sum_blocks(ref):
    return (ref[0].astype(F32) + ref[1].astype(F32)) + (ref[2].astype(F32) + ref[3].astype(F32))


def _tail_reduce(last_grads, exchanged, small):
    n = len(exchanged)
    h = last_grads.shape[2]

    def body(*refs):
        g_ref, ex, small_ref = refs[0], refs[1:1 + n], refs[1 + n]
        o0 = 2 + n
        out, out_last, small_out = refs[o0:o0 + n], refs[o0 + n], refs[o0 + n + 1]
        s0 = o0 + n + 2
        halves, half_last = refs[s0:s0 + n], refs[s0 + n]
        own, got, part, exch, small_buf = refs[s0 + n + 1:s0 + n + 6]
        pair_send, pair_recv, chip_send, chip_recv, share_send, share_recv, small_send, small_recv, local_sems = refs[s0 + n + 6:]
        x, y, c = _place()
        sibling = (x, y, 1 - c)
        my_chip, me = 2 * x + y, _slot(x, y, c)
        chips = _other_chips(x, y)

        to_sibling = [_remote(g_ref.at[j, 1 - c], got.at[j], pair_send.at[j], pair_recv.at[j], sibling)
                      for j in range(N_CHIPS)]
        load_own = [pltpu.make_async_copy(g_ref.at[j, c], own.at[j], local_sems.at[j]) for j in range(N_CHIPS)]
        for cp in to_sibling + load_own:
            cp.start()

        small_buf[me] = small_ref[...]
        small_copies = []
        for mask in range(1, 8):
            peer = (x ^ (mask >> 2), y ^ ((mask >> 1) & 1), c ^ (mask & 1))
            small_copies.append(_remote(small_ref, small_buf.at[me], small_send.at[mask - 1], small_recv.at[mask - 1], peer))
        for cp in small_copies:
            cp.start()

        def share(k, half_ref, out_ref):
            keep = pltpu.make_async_copy(half_ref, out_ref.at[c], local_sems.at[N_CHIPS + k])
            give = _remote(half_ref, out_ref.at[c], share_send.at[k], share_recv.at[k], sibling)
            take = _remote(half_ref, out_ref.at[1 - c], share_send.at[k], share_recv.at[k], sibling)
            keep.start()
            give.start()
            return keep, give, take

        shares = []
        for k in range(n):
            halves[k][...] = _sum_blocks(ex[k])
            shares.append(share(k, halves[k], out[k]))

        for cp in to_sibling:
            cp.wait_recv()
        for cp in load_own:
            cp.wait()
        part[...] = (own[...] + got[...]).astype(BF16)
        exch[my_chip] = part[my_chip]
        to_chips = [_remote(part.at[2 * chip[0] + chip[1]], exch.at[my_chip], chip_send.at[j], chip_recv.at[j], (*chip, c))
                    for j, chip in enumerate(chips)]
        from_chips = [_remote(part.at[my_chip], exch.at[2 * chip[0] + chip[1]], chip_send.at[j], chip_recv.at[j], (*chip, c))
                      for j, chip in enumerate(chips)]
        for cp in to_chips:
            cp.start()

        for cp in small_copies:
            cp.wait_recv()
        total = small_buf[0]
        for d in range(1, 8):
            total = total + small_buf[d]
        small_out[...] = total

        for cp in from_chips:
            cp.wait_recv()
        half_last[...] = _sum_blocks(exch)
        shares.append(share(n, half_last, out_last))

        for keep, give, take in shares:
            take.wait_recv()
            give.wait_send()
            keep.wait()
        for cp in to_sibling + to_chips + small_copies:
            cp.wait_send()

    blocks = (N_CHIPS, h, D_MODEL)
    return pl.pallas_call(
        body, name="tail_reduce",
        in_specs=[ANY] + [VMEM_WHOLE] * (n + 1), out_specs=[ANY] * (n + 1) + [VMEM_WHOLE],
        out_shape=[SDS((2,) + e.shape[1:], F32) for e in exchanged] + [SDS((2, h, D_MODEL), F32), SDS(small.shape, F32)],
        scratch_shapes=[pltpu.VMEM(e.shape[1:], F32) for e in exchanged] + [pltpu.VMEM((h, D_MODEL), F32)]
                       + [pltpu.VMEM(blocks, F32), pltpu.VMEM(blocks, F32), pltpu.VMEM(blocks, BF16), pltpu.VMEM(blocks, BF16),
                          pltpu.VMEM((8,) + small.shape, F32)]
                       + [pltpu.SemaphoreType.DMA((N_CHIPS,)), pltpu.SemaphoreType.DMA((N_CHIPS,)),
                          pltpu.SemaphoreType.DMA((3,)), pltpu.SemaphoreType.DMA((3,)),
                          pltpu.SemaphoreType.DMA((n + 1,)), pltpu.SemaphoreType.DMA((n + 1,)),
                          pltpu.SemaphoreType.DMA((7,)), pltpu.SemaphoreType.DMA((7,)),
                          pltpu.SemaphoreType.DMA((N_CHIPS + n + 1,))],
        compiler_params=pltpu.CompilerParams(vmem_limit_bytes=VMEM_LIMIT_V7X),
    )(last_grads, *exchanged, small)


def _rope_expansion():
    half = ROT_DIM // 2
    expand = np.zeros((2 * half, 3 * 128), np.float32)
    const = np.zeros((1, 3 * 128), np.float32)
    for lane in range(128):
        d = lane % HEAD_DIM
        if d < ROT_DIM:
            expand[d % half, lane] = 1.0
        else:
            const[0, lane] = 1.0
        if d < half:
            expand[half + d, 128 + lane] = -1.0
        elif d < ROT_DIM:
            expand[half + d - half, 256 + lane] = 1.0
    return expand, const


ROPE_PIECES = 3 * ROT_DIM


def _rope_inputs(seq):
    pos = jnp.arange(seq, dtype=F32)
    inv_freq = ROPE_THETA ** (-jnp.arange(0, ROT_DIM, 2, dtype=F32) / ROT_DIM)
    ang = pos[:, None] * inv_freq[None, :]
    cs = jnp.concatenate([jnp.cos(ang), jnp.sin(ang)], axis=1)
    hi = lax.reduce_precision(cs, 8, 7)
    mid = lax.reduce_precision(cs - hi, 8, 7)
    low = cs - hi - mid
    expand, const = _rope_expansion()
    pieces = jnp.concatenate([hi, mid, low], axis=1).astype(BF16)
    return pieces, jnp.asarray(np.concatenate([expand] * 3, axis=0), BF16), jnp.asarray(const)


def _rope_specs(tb):
    return [pl.BlockSpec((tb, ROPE_PIECES), lambda i: (i, 0)), _resident((ROPE_PIECES, 3 * 128)), _resident((1, 3 * 128))]


def _rope_tile(pieces_ref, expand_ref, const_ref):
    tables = _dot(pieces_ref[...], expand_ref[...]) + const_ref[...]
    return tables[:, 0:128], tables[:, 128:256], tables[:, 256:384]


def _rope(t, c, sa, sb):
    half = ROT_DIM // 2
    return t * c + pltpu.roll(t, 128 - half, 1) * sa + pltpu.roll(t, half, 1) * sb


def _rope_transposed(dt, c, sa, sb):
    half = ROT_DIM // 2
    return dt * c + pltpu.roll(dt * sa, half, 1) + pltpu.roll(dt * sb, 128 - half, 1)


def _cast_halves(core, w_up, w_down, w_out, w_in_t):
    def body(core_ref, up_ref, down_ref, out_ref, in_ref, up_o, down_o, out_o, in_o):
        up_o[...] = up_ref[...].astype(BF16)
        down_o[...] = down_ref[...].astype(BF16)
        out_o[...] = out_ref[...].astype(BF16)
        in_o[...] = in_ref[...].astype(BF16)

    half = lambda rows: pl.BlockSpec((rows, D_MODEL), lambda i, core_ref: (core_ref[0], 0))
    whole = lambda rows: pl.BlockSpec((rows, D_MODEL), lambda i, core_ref: (0, 0))
    rows = (H_UP, H_DOWN, H_OUT, H_IN)
    return pl.pallas_call(
        body, name="cast_halves",
        grid_spec=pltpu.PrefetchScalarGridSpec(
            num_scalar_prefetch=1, grid=(1,), in_specs=[half(r) for r in rows], out_specs=[whole(r) for r in rows]),
        out_shape=[SDS((r, D_MODEL), BF16) for r in rows],
        compiler_params=pltpu.CompilerParams(dimension_semantics=("arbitrary",), vmem_limit_bytes=VMEM_LIMIT_V7X),
    )(core, w_up, w_down, w_out, w_in_t)


def _in_proj(x, g_pre, w_in_t, rope, comm=None):
    seq = x.shape[0]
    tb = min(seq, WIDE_TOKEN_TILE)

    def body(x_ref, g_ref, w_ref, c_ref, sa_ref, sb_ref,
             q_ref, kd0_ref, kd1_ref, vd0_ref, vd1_ref, gb_ref, gc_ref, xin_ref, hn_ref):
        xv = x_ref[...]
        hn = (xv * _rms(xv) * g_ref[...]).astype(BF16)
        hn_ref[...] = hn
        proj = _dot_nt(hn, w_ref[...].reshape(IN_COLS, D_MODEL))
        c, sa, sb = _rope_tile(c_ref, sa_ref, sb_ref)
        scale = 1.0 / math.sqrt(HEAD_DIM)
        for p in range(Q_WIDTH // 128):
            q_ref[:, 128 * p:128 * (p + 1)] = (_rope(proj[:, 128 * p:128 * (p + 1)], c, sa, sb) * scale).astype(BF16)
        k = _rope(proj[:, Q_WIDTH:Q_WIDTH + KV_WIDTH], c, sa, sb)
        v = proj[:, Q_WIDTH + KV_WIDTH:Q_WIDTH + 2 * KV_WIDTH]
        low = _lane_lt64(k.shape)
        k_sw, v_sw = pltpu.roll(k, HEAD_DIM, 1), pltpu.roll(v, HEAD_DIM, 1)
        kd0_ref[...] = jnp.where(low, k, k_sw).astype(BF16)
        kd1_ref[...] = jnp.where(low, k_sw, k).astype(BF16)
        vd0_ref[...] = jnp.where(low, v, v_sw).astype(BF16)
        vd1_ref[...] = jnp.where(low, v_sw, v).astype(BF16)
        base = Q_WIDTH + 2 * KV_WIDTH
        gb_ref[...] = proj[:, base:base + CONV_WIDTH].astype(BF16)
        gc_ref[...] = proj[:, base + CONV_WIDTH:base + 2 * CONV_WIDTH].astype(BF16)
        xin_ref[...] = proj[:, base + 2 * CONV_WIDTH:base + 3 * CONV_WIDTH].astype(BF16)

    tile = lambda w: pl.BlockSpec((tb, w), lambda i: (i, 0))
    return _pallas(
        body, name="in_proj", grid=(seq // tb,),
        in_specs=[tile(D_MODEL), _resident((1, D_MODEL)), _resident(w_in_t.shape), *_rope_specs(tb)],
        out_specs=[tile(Q_WIDTH), tile(128), tile(128), tile(128), tile(128),
                   tile(CONV_WIDTH), tile(CONV_WIDTH), tile(CONV_WIDTH), tile(D_MODEL)],
        out_shape=[SDS((seq, Q_WIDTH), BF16)] + [SDS((seq, 128), BF16)] * 4
                  + [SDS((seq, CONV_WIDTH), BF16)] * 3 + [SDS((seq, D_MODEL), BF16)],
        operands=(x, g_pre, w_in_t, *rope), comm=comm)


def _attn_valid(i):
    shape = (4 * QBLOCK, 2 * QBLOCK)
    row = lax.broadcasted_iota(jnp.int32, shape, 0)
    col = lax.broadcasted_iota(jnp.int32, shape, 1)
    qi = row & (QBLOCK - 1)
    return (col > qi) & (col <= qi + QBLOCK) & ((col >= QBLOCK) | (i > 0))


def _stack_heads(pair0, pair1):
    low = _lane_lt64(pair0.shape)
    zero = jnp.zeros_like(pair0)
    return jnp.concatenate([jnp.where(low, pair0, zero), jnp.where(low, zero, pair0),
                            jnp.where(low, pair1, zero), jnp.where(low, zero, pair1)], axis=0)


def _unstack_heads(stacked):
    low = _lane_lt64((QBLOCK, 128))
    pair0 = jnp.where(low, stacked[0:QBLOCK], stacked[QBLOCK:2 * QBLOCK])
    pair1 = jnp.where(low, stacked[2 * QBLOCK:3 * QBLOCK], stacked[3 * QBLOCK:4 * QBLOCK])
    return pair0, pair1


def _sink_column(sink_ref, kv_head):
    row = lax.broadcasted_iota(jnp.int32, (4 * QBLOCK, 1), 0)
    s = [sink_ref[0, 4 * kv_head + j] for j in range(4)]
    return jnp.where(row < QBLOCK, s[0], jnp.where(row < 2 * QBLOCK, s[1], jnp.where(row < 3 * QBLOCK, s[2], s[3])))


def _band(ref, i):
    prev = pl.multiple_of(jnp.maximum(i - 1, 0) * QBLOCK, QBLOCK)
    own = pl.multiple_of(i * QBLOCK, QBLOCK)
    return jnp.concatenate([ref[pl.ds(prev, QBLOCK), :], ref[pl.ds(own, QBLOCK), :]], axis=0), prev, own


def _softmax_with_sink(s, sink_col):
    m = jnp.maximum(jnp.max(s, axis=-1, keepdims=True), sink_col)
    p = jnp.exp(s - m)
    e_sink = jnp.exp(sink_col - m)
    inv_l = 1.0 / (jnp.sum(p, axis=-1, keepdims=True) + e_sink)
    return p, e_sink, inv_l


def _attention_fwd(q, kd0, kd1, vd0, vd1, sinks, comm=None):
    seq = q.shape[0]

    nb = ATTN_FWD_BLOCKS

    def body(sink_ref, q_ref, kd0_ref, kd1_ref, vd0_ref, vd1_ref, o_ref):
        for b in range(nb):
            i = pl.program_id(0) * nb + b
            rows = slice(QBLOCK * b, QBLOCK * (b + 1))
            valid = _attn_valid(i)
            for kv_head, (k_ref, v_ref) in enumerate(((kd0_ref, vd0_ref), (kd1_ref, vd1_ref))):
                kband, _, _ = _band(k_ref, i)
                vband, _, _ = _band(v_ref, i)
                base = 256 * kv_head
                qm = _stack_heads(q_ref[rows, base:base + 128], q_ref[rows, base + 128:base + 256])
                s = jnp.where(valid, _dot_nt(qm, kband), NEG_INF)
                p, _, inv_l = _softmax_with_sink(s, _sink_column(sink_ref, kv_head))
                o = _dot(p.astype(BF16), vband) * inv_l
                pair0, pair1 = _unstack_heads(o)
                o_ref[rows, base:base + 128] = pair0.astype(BF16)
                o_ref[rows, base + 128:base + 256] = pair1.astype(BF16)

    blk = pl.BlockSpec((nb * QBLOCK, Q_WIDTH), lambda i: (i, 0))
    full = _resident((seq, 128))
    return _pallas(
        body, name="attention_fwd", grid=(seq // (nb * QBLOCK),),
        in_specs=[pl.BlockSpec(memory_space=pltpu.SMEM), blk, full, full, full, full],
        out_specs=[blk], out_shape=[SDS((seq, Q_WIDTH), BF16)],
        operands=(sinks, q, kd0, kd1, vd0, vd1), comm=comm)


HALO = 16


def _conv_parts(gc, xin, gc_halo, xin_halo, conv_w, first):
    tb = gc.shape[0]
    u = gc.astype(F32) * xin.astype(F32)
    u_halo = jnp.where(first, 0.0, gc_halo.astype(F32) * xin_halo.astype(F32))
    ext = jnp.concatenate([u_halo, u], axis=0)
    u1 = pltpu.roll(ext, 1, 0)[HALO:HALO + tb]
    u2 = pltpu.roll(ext, 2, 0)[HALO:HALO + tb]
    y = conv_w[0:1, :] * u2 + conv_w[1:2, :] * u1 + conv_w[2:3, :] * u
    return u, u1, u2, y


def _halo_prev(tb, w):
    return pl.BlockSpec((HALO, w), lambda i: (jnp.maximum(i * (tb // HALO) - 1, 0), 0))


def _residual_mid(x, mix, g_post_mix):
    mix_f = mix.astype(F32)
    return x + mix_f * _rms(mix_f) * g_post_mix


def _mix_out(attn, gb, gc, xin, conv_w, g_attn, g_conv, w_out, comm=None):
    seq = attn.shape[0]
    tb = min(seq, WIDE_TOKEN_TILE)

    def body(a_ref, gb_ref, gc_ref, xin_ref, gch_ref, xinh_ref, cw_ref, ga_ref, gcn_ref, w_ref, mix_ref, mixed_ref):
        first = pl.program_id(0) == 0
        _, _, _, y = _conv_parts(gc_ref[...], xin_ref[...], gch_ref[...], xinh_ref[...], cw_ref[...], first)
        conv = gb_ref[...].astype(F32) * y
        a = a_ref[...].astype(F32)
        mixed_ref[:, 0:Q_WIDTH] = (a * _rms(a) * ga_ref[...]).astype(BF16)
        mixed_ref[:, Q_WIDTH:] = (conv * _rms(conv) * gcn_ref[...]).astype(BF16)
        mix_ref[...] = _dot(mixed_ref[...], w_ref[...].reshape(D_MODEL, D_MODEL)).astype(BF16)

    tile = lambda w: pl.BlockSpec((tb, w), lambda i: (i, 0))
    return _pallas(
        body, name="mix_out", grid=(seq // tb,),
        in_specs=[tile(Q_WIDTH), tile(CONV_WIDTH), tile(CONV_WIDTH), tile(CONV_WIDTH),
                  _halo_prev(tb, CONV_WIDTH), _halo_prev(tb, CONV_WIDTH),
                  _resident((CONV_K, CONV_WIDTH)), _resident((1, Q_WIDTH)), _resident((1, CONV_WIDTH)),
                  _resident(w_out.shape)],
        out_specs=[tile(D_MODEL), tile(D_MODEL)],
        out_shape=[SDS((seq, D_MODEL), BF16), SDS((seq, D_MODEL), BF16)],
        operands=(attn, gb, gc, xin, gc, xin, conv_w, g_attn, g_conv, w_out), comm=comm)


def _mlp_loss(x, mix, target, g_post_mix, g_pre_mlp, g_post_mlp, w_up, w_down):
    seq = x.shape[0]
    tb = TOKEN_TILE

    def body(x_ref, mix_ref, t_ref, gpm_ref, g2_ref, g4_ref, wup_ref, wdown_ref,
             up_ref, hn2_ref, dout_ref, dmlp_ref, loss_ref, dg4_ref, act_ref):
        @pl.when(pl.program_id(0) == 0)
        def _():
            loss_ref[...] = jnp.zeros_like(loss_ref)
            dg4_ref[...] = jnp.zeros_like(dg4_ref)

        halves = [slice(0, tb // 2), slice(tb // 2, tb)]
        hv, hn2 = [], []
        for rows in halves:
            hv.append(_residual_mid(x_ref[rows, :], mix_ref[rows, :], gpm_ref[...]))
            hn2.append((hv[-1] * _rms(hv[-1]) * g2_ref[...]).astype(BF16))
            hn2_ref[rows, :] = hn2[-1]
        for k, rows in enumerate(halves):
            for j in range(N_CHIPS):
                up = _dot(hn2[k], _chip_block(wup_ref, j))
                up = jnp.maximum(up, 0.0)
                up_ref[rows, 1024 * j:1024 * (j + 1)] = up.astype(BF16)
                act_ref[rows, 1024 * j:1024 * (j + 1)] = (up * up).astype(BF16)
        w_down_all = wdown_ref[...].reshape(D_FF, D_MODEL)
        loss = jnp.zeros((1, 1), F32)
        dg4 = jnp.zeros((1, D_MODEL), F32)
        for k, rows in enumerate(halves):
            mlp = _dot(act_ref[rows, :], w_down_all)
            rstd = _rms(mlp)
            zhat = mlp * rstd
            diff = hv[k] + zhat * g4_ref[...] - t_ref[rows, :]
            loss = loss + jnp.sum(jnp.sum(diff * diff, axis=1, keepdims=True), axis=0, keepdims=True)
            dout = diff * (1.0 / D_MODEL)
            dout_ref[rows, :] = dout
            dg4 = dg4 + _colsum(dout * zhat)
            dmlp_ref[rows, :] = _norm_bwd(dout, g4_ref[...], zhat, rstd).astype(BF16)
        loss_ref[...] += loss
        dg4_ref[...] += dg4

    tile = lambda w: pl.BlockSpec((tb, w), lambda i: (i, 0))
    return _pallas(
        body, name="mlp_loss", grid=(seq // tb,),
        in_specs=[tile(D_MODEL), tile(D_MODEL), tile(D_MODEL), _resident((1, D_MODEL)), _resident((1, D_MODEL)),
                  _resident((1, D_MODEL)), _resident(w_up.shape), _resident(w_down.shape)],
        out_specs=[tile(D_FF), tile(D_MODEL), tile(D_MODEL), tile(D_MODEL),
                   pl.BlockSpec((1, 1), lambda i: (0, 0)), pl.BlockSpec((1, D_MODEL), lambda i: (0, 0))],
        out_shape=[SDS((seq, D_FF), BF16), SDS((seq, D_MODEL), BF16), SDS((seq, D_MODEL), F32),
                   SDS((seq, D_MODEL), BF16), SDS((1, 1), F32), SDS((1, D_MODEL), F32)],
        scratch=[pltpu.VMEM((tb, D_FF), BF16)],
        operands=(x, mix, target, g_post_mix, g_pre_mlp, g_post_mlp, w_up, w_down))


def _mlp_bwd(dmlp, up, x, dout, mix, g_pre_mlp, g_post_mix, w_up, w_down):
    seq = x.shape[0]
    tb = MLP_BWD_TOKEN_TILE

    def body(dmlp_ref, up_ref, x_ref, dout_ref, mix_ref, g2_ref, gpm_ref, wup_ref, wdown_ref,
             dup_ref, dh_ref, dmix_ref, dg2_ref, dgpm_ref):
        @pl.when(pl.program_id(0) == 0)
        def _():
            dg2_ref[...] = jnp.zeros_like(dg2_ref)
            dgpm_ref[...] = jnp.zeros_like(dgpm_ref)

        subs = [slice(k * MLP_BWD_SUB_TILE, (k + 1) * MLP_BWD_SUB_TILE) for k in range(tb // MLP_BWD_SUB_TILE)]
        dhn2 = []
        for rows in subs:
            dmlp_v = dmlp_ref[rows, :]
            acc = None
            for j in range(N_CHIPS):
                cols = slice(1024 * j, 1024 * (j + 1))
                dact = _dot_nt(dmlp_v, _chip_block(wdown_ref, j))
                dup = (dact * (2.0 * up_ref[rows, cols].astype(F32))).astype(BF16)
                dup_ref[rows, cols] = dup
                part = _dot_nt(dup, _chip_block(wup_ref, j))
                acc = part if acc is None else acc + part
            dhn2.append(acc)
        dg2 = jnp.zeros((1, D_MODEL), F32)
        dgpm = jnp.zeros((1, D_MODEL), F32)
        for k, rows in enumerate(subs):
            mix_v = mix_ref[rows, :].astype(F32)
            hv = _residual_mid(x_ref[rows, :], mix_ref[rows, :], gpm_ref[...])
            r2 = _rms(hv)
            hhat = hv * r2
            dg2 = dg2 + _colsum(dhn2[k] * hhat)
            dh = dout_ref[rows, :] + _norm_bwd(dhn2[k], g2_ref[...], hhat, r2)
            dh_ref[rows, :] = dh.astype(BF16)
            rz = _rms(mix_v)
            zhat = mix_v * rz
            dgpm = dgpm + _colsum(dh * zhat)
            dmix_ref[rows, :] = _norm_bwd(dh, gpm_ref[...], zhat, rz).astype(BF16)
        dg2_ref[...] += dg2
        dgpm_ref[...] += dgpm

    tile = lambda w: pl.BlockSpec((tb, w), lambda i: (i, 0))
    vec = pl.BlockSpec((1, D_MODEL), lambda i: (0, 0))
    return _pallas(
        body, name="mlp_bwd", grid=(seq // tb,),
        in_specs=[tile(D_MODEL), tile(D_FF), tile(D_MODEL), tile(D_MODEL), tile(D_MODEL),
                  _resident((1, D_MODEL)), _resident((1, D_MODEL)), _resident(w_up.shape), _resident(w_down.shape)],
        out_specs=[tile(D_FF), tile(D_MODEL), tile(D_MODEL), vec, vec],
        out_shape=[SDS((seq, D_FF), BF16), SDS((seq, D_MODEL), BF16), SDS((seq, D_MODEL), BF16),
                   SDS((1, D_MODEL), F32), SDS((1, D_MODEL), F32)],
        operands=(dmlp, up, x, dout, mix, g_pre_mlp, g_post_mix, w_up, w_down))


def _mix_bwd(dmix, attn, gb, gc, xin, conv_w, g_attn, g_conv, w_out, n_k):
    seq = attn.shape[0]
    tb = seq // (N_CHIPS * n_k)

    def body(first, dmix_ref, a_ref, gb_ref, gc_ref, xin_ref, gch_ref, xinh_ref, cw_ref, ga_ref, gcn_ref, w_ref,
             dattn_ref, dgb_ref, dy_ref, dga_ref, dgcn_ref, dcw_ref):
        @pl.when(first)
        def _():
            dga_ref[...] = jnp.zeros_like(dga_ref)
            dgcn_ref[...] = jnp.zeros_like(dgcn_ref)
            dcw_ref[...] = jnp.zeros_like(dcw_ref)

        dmixed = _dot_nt(dmix_ref[...], w_ref[...].reshape(D_MODEL, D_MODEL))
        a = a_ref[...].astype(F32)
        ra = _rms(a)
        ahat = a * ra
        dan = dmixed[:, 0:Q_WIDTH]
        dga_ref[...] += _colsum(dan * ahat)
        dattn_ref[...] = _norm_bwd(dan, ga_ref[...], ahat, ra).astype(BF16)
        gbv = gb_ref[...].astype(F32)
        u, u1, u2, y = _conv_parts(gc_ref[...], xin_ref[...], gch_ref[...], xinh_ref[...], cw_ref[...], first)
        conv = gbv * y
        rc = _rms(conv)
        chat = conv * rc
        dcn = dmixed[:, Q_WIDTH:]
        dgcn_ref[...] += _colsum(dcn * chat)
        dconv = _norm_bwd(dcn, gcn_ref[...], chat, rc)
        dgb_ref[...] = (dconv * y).astype(BF16)
        dy = dconv * gbv
        dy_ref[...] = dy.astype(BF16)
        dcw_ref[0:1, :] += _colsum(dy * u2)
        dcw_ref[1:2, :] += _colsum(dy * u1)
        dcw_ref[2:3, :] += _colsum(dy * u)

    tile = lambda w: pl.BlockSpec((tb, w), lambda j, k: (j * n_k + k, 0))
    halo = lambda w: pl.BlockSpec((HALO, w), lambda j, k: (jnp.maximum((j * n_k + k) * (tb // HALO) - 1, 0), 0))
    whole = lambda shape: pl.BlockSpec(shape, lambda j, k: (0,) * len(shape))
    return _Rider(
        body,
        in_specs=[tile(D_MODEL), tile(Q_WIDTH), tile(CONV_WIDTH), tile(CONV_WIDTH), tile(CONV_WIDTH),
                  halo(CONV_WIDTH), halo(CONV_WIDTH),
                  _resident((CONV_K, CONV_WIDTH)), _resident((1, Q_WIDTH)), _resident((1, CONV_WIDTH)),
                  _resident(w_out.shape)],
        out_specs=[tile(Q_WIDTH), tile(CONV_WIDTH), tile(CONV_WIDTH),
                   whole((1, Q_WIDTH)), whole((1, CONV_WIDTH)), whole((CONV_K, CONV_WIDTH))],
        out_shape=[SDS((seq, Q_WIDTH), BF16), SDS((seq, CONV_WIDTH), BF16), SDS((seq, CONV_WIDTH), BF16),
                   SDS((1, Q_WIDTH), F32), SDS((1, CONV_WIDTH), F32), SDS((CONV_K, CONV_WIDTH), F32)],
        operands=(dmix, attn, gb, gc, xin, gc, xin, conv_w, g_attn, g_conv, w_out))


def _attention_bwd(q, dattn, attn, kd0, kd1, vd0, vd1, sinks, comm=None):
    seq = q.shape[0]
    nb = ATTN_BWD_BLOCKS

    def body(sink_ref, q_ref, do_ref, o_ref, kd0_ref, kd1_ref, vd0_ref, vd1_ref,
             dq_ref, dk0_ref, dk1_ref, dv0_ref, dv1_ref, dsink_ref):
        @pl.when(pl.program_id(0) == 0)
        def _():
            for r in (dk0_ref, dk1_ref, dv0_ref, dv1_ref, dsink_ref):
                r[...] = jnp.zeros_like(r)

        lane = lax.broadcasted_iota(jnp.int32, (1, 128), 1)
        dsink = jnp.zeros((1, 128), F32)
        for b in range(nb):
            i = pl.program_id(0) * nb + b
            rows = slice(QBLOCK * b, QBLOCK * (b + 1))
            valid = _attn_valid(i)
            for kv_head, (k_ref, v_ref, dk_ref, dv_ref) in enumerate(
                    ((kd0_ref, vd0_ref, dk0_ref, dv0_ref), (kd1_ref, vd1_ref, dk1_ref, dv1_ref))):
                kband, prev, own = _band(k_ref, i)
                vband, _, _ = _band(v_ref, i)
                base = 256 * kv_head
                qm = _stack_heads(q_ref[rows, base:base + 128], q_ref[rows, base + 128:base + 256])
                dom = _stack_heads(do_ref[rows, base:base + 128], do_ref[rows, base + 128:base + 256])
                om = _stack_heads(o_ref[rows, base:base + 128], o_ref[rows, base + 128:base + 256])
                s = jnp.where(valid, _dot_nt(qm, kband), NEG_INF)
                p, e_sink, inv_l = _softmax_with_sink(s, _sink_column(sink_ref, kv_head))
                p = p * inv_l
                delta = jnp.sum(dom.astype(F32) * om.astype(F32), axis=-1, keepdims=True)
                ds = (p * (_dot_nt(dom, vband) - delta)).astype(BF16)
                sink_term = -(e_sink * inv_l) * delta
                for j in range(4):
                    part = jnp.sum(sink_term[QBLOCK * j:QBLOCK * (j + 1)], axis=0, keepdims=True)
                    dsink = dsink + jnp.where(lane == 4 * kv_head + j, part, 0.0)
                pair0, pair1 = _unstack_heads(_dot(ds, kband))
                dq_ref[rows, base:base + 128] = pair0.astype(BF16)
                dq_ref[rows, base + 128:base + 256] = pair1.astype(BF16)
                dkd = _dot_tn(ds, qm)
                dkd = dkd + pltpu.roll(dkd, HEAD_DIM, 1)
                dvd = _dot_tn(p.astype(BF16), dom)
                dvd = dvd + pltpu.roll(dvd, HEAD_DIM, 1)
                dk_ref[pl.ds(prev, QBLOCK), :] += dkd[0:QBLOCK]
                dk_ref[pl.ds(own, QBLOCK), :] += dkd[QBLOCK:]
                dv_ref[pl.ds(prev, QBLOCK), :] += dvd[0:QBLOCK]
                dv_ref[pl.ds(own, QBLOCK), :] += dvd[QBLOCK:]
        dsink_ref[...] += dsink

    blk = pl.BlockSpec((nb * QBLOCK, Q_WIDTH), lambda i: (i, 0))
    full = _resident((seq, 128))
    acc = pl.BlockSpec((seq, 128), lambda i: (0, 0))
    return _pallas(
        body, name="attention_bwd", grid=(seq // (nb * QBLOCK),),
        in_specs=[pl.BlockSpec(memory_space=pltpu.SMEM), blk, blk, blk, full, full, full, full],
        out_specs=[blk, acc, acc, acc, acc, pl.BlockSpec((1, 128), lambda i: (0, 0))],
        out_shape=[SDS((seq, Q_WIDTH), BF16)] + [SDS((seq, 128), F32)] * 4 + [SDS((1, 128), F32)],
        operands=(sinks, q, dattn, attn, kd0, kd1, vd0, vd1), comm=comm)


def _in_proj_bwd(dq, dk0, dk1, dv0, dv1, dgb, dy, gc, xin, conv_w, x, dh, g_pre, w_in_t, rope):
    seq = x.shape[0]
    tb = min(seq, WIDE_TOKEN_TILE)
    n_tiles = seq // tb

    def body(dq_ref, dk0_ref, dk1_ref, dv0_ref, dv1_ref, dgb_ref, dy_ref, dyh_ref, gc_ref, xin_ref, cw_ref,
             x_ref, dh_ref, g_ref, w_ref, c_ref, sa_ref, sb_ref,
             dproj_ref, gx_ref, dg_ref):
        i = pl.program_id(0)

        @pl.when(i == 0)
        def _():
            dg_ref[...] = jnp.zeros_like(dg_ref)

        dy = dy_ref[...].astype(F32)
        ext = jnp.concatenate([dy, jnp.where(i == n_tiles - 1, 0.0, dyh_ref[...].astype(F32))], axis=0)
        dy1 = pltpu.roll(ext, tb + HALO - 1, 0)[0:tb]
        dy2 = pltpu.roll(ext, tb + HALO - 2, 0)[0:tb]
        cw = cw_ref[...]
        du = cw[2:3, :] * dy + cw[1:2, :] * dy1 + cw[0:1, :] * dy2
        scale = 1.0 / math.sqrt(HEAD_DIM)
        base = Q_WIDTH + 2 * KV_WIDTH
        halves = [slice(0, tb // 2), slice(tb // 2, tb)]
        low = _lane_lt64((tb // 2, 128))
        for rows in halves:
            c, sa, sb = _rope_tile(c_ref.at[rows, :], sa_ref, sb_ref)
            for p in range(Q_WIDTH // 128):
                dproj_ref[rows, 128 * p:128 * (p + 1)] = _rope_transposed(
                    dq_ref[rows, 128 * p:128 * (p + 1)].astype(F32) * scale, c, sa, sb).astype(BF16)
            dk = jnp.where(low, dk0_ref[rows, :], dk1_ref[rows, :])
            dproj_ref[rows, Q_WIDTH:Q_WIDTH + KV_WIDTH] = _rope_transposed(dk, c, sa, sb).astype(BF16)
            dproj_ref[rows, Q_WIDTH + KV_WIDTH:base] = jnp.where(low, dv0_ref[rows, :], dv1_ref[rows, :]).astype(BF16)
            dproj_ref[rows, base:base + CONV_WIDTH] = dgb_ref[rows, :]
            dproj_ref[rows, base + CONV_WIDTH:base + 2 * CONV_WIDTH] = (du[rows] * xin_ref[rows, :].astype(F32)).astype(BF16)
            dproj_ref[rows, base + 2 * CONV_WIDTH:] = (du[rows] * gc_ref[rows, :].astype(F32)).astype(BF16)
        w_all = w_ref[...].reshape(IN_COLS, D_MODEL)
        dhn = [_dot(dproj_ref[rows, :], w_all) for rows in halves]
        dg = jnp.zeros((1, D_MODEL), F32)
        for k, rows in enumerate(halves):
            xv = x_ref[rows, :]
            r = _rms(xv)
            xhat = xv * r
            dg = dg + _colsum(dhn[k] * xhat)
            gx_ref[rows, :] = dh_ref[rows, :].astype(F32) + _norm_bwd(dhn[k], g_ref[...], xhat, r)
        dg_ref[...] += dg

    tile = lambda w: pl.BlockSpec((tb, w), lambda i: (i, 0))
    halo_next = pl.BlockSpec((HALO, CONV_WIDTH), lambda i: (jnp.minimum((i + 1) * (tb // HALO), seq // HALO - 1), 0))
    return _pallas(
        body, name="in_proj_bwd", grid=(n_tiles,),
        in_specs=[tile(Q_WIDTH), tile(128), tile(128), tile(128), tile(128), tile(CONV_WIDTH), tile(CONV_WIDTH), halo_next,
                  tile(CONV_WIDTH), tile(CONV_WIDTH), _resident((CONV_K, CONV_WIDTH)),
                  tile(D_MODEL), tile(D_MODEL), _resident((1, D_MODEL)), _resident(w_in_t.shape), *_rope_specs(tb)],
        out_specs=[tile(IN_COLS), tile(D_MODEL), pl.BlockSpec((1, D_MODEL), lambda i: (0, 0))],
        out_shape=[SDS((seq, IN_COLS), BF16), SDS((seq, D_MODEL), F32), SDS((1, D_MODEL), F32)],
        operands=(dq, dk0, dk1, dv0, dv1, dgb, dy, dy, gc, xin, conv_w, x, dh, g_pre, w_in_t, *rope))


def _wgrad_grid(seq, per_chip, h_rows):
    chips_per_step = 1 if per_chip else N_CHIPS
    m = chips_per_step * 2 * h_rows
    bt = min(seq, WGRAD_TOKEN_TILE)
    return chips_per_step, m, bt, seq // bt


def _wgrad(name, a, b, *, per_chip, h_rows, square_a=False, comm=None, rider=None):
    seq = a.shape[0]
    chips_per_step, m, bt, n_k = _wgrad_grid(seq, per_chip, h_rows)
    a_cols = m if per_chip else a.shape[1]
    a_wide = a.shape[1] > a_cols
    b_wide = b.shape[1] > D_MODEL

    def body(a_ref, b_ref, g_ref):
        @pl.when(pl.program_id(1) == 0)
        def _():
            g_ref[...] = jnp.zeros_like(g_ref)

        av = a_ref[...]
        if square_a:
            av = (av.astype(F32) * av.astype(F32)).astype(BF16)
        g_ref[...] += _dot_tn(av, b_ref[...]).reshape(g_ref.shape)

    a_spec = pl.BlockSpec((bt, a_cols), (lambda j, k: (k, j)) if a_wide else (lambda j, k: (k, 0)))
    b_spec = pl.BlockSpec((bt, D_MODEL), (lambda j, k: (k, j)) if b_wide else (lambda j, k: (k, 0)))
    g_spec = pl.BlockSpec((chips_per_step, 2, h_rows, D_MODEL), lambda j, k: (j, 0, 0, 0),
                          pipeline_mode=None if per_chip else pl.Buffered(1))
    return _pallas(
        body, name=name, grid=(N_CHIPS if per_chip else 1, n_k),
        in_specs=[a_spec, b_spec], out_specs=[g_spec], out_shape=[SDS((N_CHIPS, 2, h_rows, D_MODEL), F32)],
        operands=(a, b), comm=comm, rider=rider)


def _adamw_math(w, g, m, v):
    m = ADAM_B1 * m + (1.0 - ADAM_B1) * g
    v = ADAM_B2 * v + (1.0 - ADAM_B2) * (g * g)
    m_hat = m / (1.0 - ADAM_B1 ** ADAM_STEP)
    v_hat = v / (1.0 - ADAM_B2 ** ADAM_STEP)
    delta = -ADAM_LR * (m_hat / (jnp.sqrt(v_hat) + ADAM_EPS) + ADAM_WD * w)
    return delta, m, v


def _adamw_rows(name, reduced, w, m, v, rt):
    per_half = reduced.shape[1] // rt

    def body(r_ref, w_ref, m_ref, v_ref, g_out, d_out, m_out, v_out):
        g = r_ref[0]
        g_out[...] = g
        d_out[...], m_out[...], v_out[...] = _adamw_math(w_ref[...], g, m_ref[...], v_ref[...])

    blk = pl.BlockSpec((rt, D_MODEL), lambda h, r: (h * per_half + r, 0))
    return _pallas(
        body, name=name, grid=(2, per_half),
        in_specs=[pl.BlockSpec((1, rt, D_MODEL), lambda h, r: (h, r, 0)), blk, blk, blk],
        out_specs=[blk, blk, blk, blk], out_shape=[SDS(w.shape, F32)] * 4, operands=(reduced, w, m, v))


def _adamw_small(w, g, m, v):
    def body(w_ref, g_ref, m_ref, v_ref, d_out, m_out, v_out):
        d_out[...], m_out[...], v_out[...] = _adamw_math(w_ref[...], g_ref[...], m_ref[...], v_ref[...])

    return pl.pallas_call(body, name="adamw_small", in_specs=[VMEM_WHOLE] * 4, out_specs=[VMEM_WHOLE] * 3,
                          out_shape=[SDS(w.shape, F32)] * 3)(w, g, m, v)


SMALL_VECTORS = ("pre_mix_norm", "post_mix_norm", "pre_mlp_norm", "post_mlp_norm")
SMALL_NAMES = SMALL_VECTORS + ("attn_group_norm", "conv_group_norm", "conv_w", "attn_sinks")


def _pack_small(p):
    rows = [p[n].reshape(1, D_MODEL) for n in SMALL_VECTORS]
    rows.append(jnp.concatenate([p["attn_group_norm"].reshape(1, -1), p["conv_group_norm"].reshape(1, -1)], axis=1))
    cw = p["conv_w"].reshape(CONV_K, -1)
    rows.append(jnp.pad(cw, ((0, 1), (0, CONV_WIDTH - cw.shape[1]))).reshape(2, D_MODEL))
    last = jnp.concatenate([p["attn_sinks"].reshape(1, 8), p.get("loss_sum", jnp.zeros((1, 1), F32))], axis=1)
    rows.append(jnp.pad(last, ((0, 0), (0, D_MODEL - 9))))
    return jnp.concatenate(rows, axis=0)


def _unpack_small(packed, conv_width):
    out = {n: packed[i:i + 1] for i, n in enumerate(SMALL_VECTORS)}
    out["attn_group_norm"] = packed[4:5, :Q_WIDTH]
    out["conv_group_norm"] = packed[4:5, Q_WIDTH:]
    out["conv_w"] = packed[5:7].reshape(4, CONV_WIDTH)[:CONV_K, :conv_width].reshape(1, CONV_K, conv_width)
    out["attn_sinks"] = packed[7:8, :8]
    out["loss_sum"] = packed[7, 8]
    return out


WEIGHT_ORDER = ("pre_mix_norm", "w_in", "conv_w", "attn_sinks", "attn_group_norm", "conv_group_norm", "w_out",
                "post_mix_norm", "pre_mlp_norm", "w_up", "w_down", "post_mlp_norm")


def kernel(x, pre_mix_norm, w_in, conv_w, attn_sinks, attn_group_norm, conv_group_norm, w_out, post_mix_norm, pre_mlp_norm, w_up, w_down, post_mlp_norm, loss_target, m_pre_mix_norm, m_w_in, m_conv_w, m_attn_sinks, m_attn_group_norm, m_conv_group_norm, m_w_out, m_post_mix_norm, m_pre_mlp_norm, m_w_up, m_w_down, m_post_mlp_norm, v_pre_mix_norm, v_w_in, v_conv_w, v_attn_sinks, v_attn_group_norm, v_conv_group_norm, v_w_out, v_post_mix_norm, v_pre_mlp_norm, v_w_up, v_w_down, v_post_mlp_norm):
    w = dict(pre_mix_norm=pre_mix_norm, w_in=w_in, conv_w=conv_w, attn_sinks=attn_sinks, attn_group_norm=attn_group_norm,
             conv_group_norm=conv_group_norm, w_out=w_out, post_mix_norm=post_mix_norm, pre_mlp_norm=pre_mlp_norm,
             w_up=w_up, w_down=w_down, post_mlp_norm=post_mlp_norm)
    m = dict(pre_mix_norm=m_pre_mix_norm, w_in=m_w_in, conv_w=m_conv_w, attn_sinks=m_attn_sinks,
             attn_group_norm=m_attn_group_norm, conv_group_norm=m_conv_group_norm, w_out=m_w_out,
             post_mix_norm=m_post_mix_norm, pre_mlp_norm=m_pre_mlp_norm, w_up=m_w_up, w_down=m_w_down,
             post_mlp_norm=m_post_mlp_norm)
    v = dict(pre_mix_norm=v_pre_mix_norm, w_in=v_w_in, conv_w=v_conv_w, attn_sinks=v_attn_sinks,
             attn_group_norm=v_attn_group_norm, conv_group_norm=v_conv_group_norm, w_out=v_w_out,
             post_mix_norm=v_post_mix_norm, pre_mlp_norm=v_pre_mlp_norm, w_up=v_w_up, w_down=v_w_down,
             post_mlp_norm=v_post_mlp_norm)
    core = lax.axis_index("c").astype(jnp.int32).reshape(1)
    chip = 2 * lax.axis_index("x") + lax.axis_index("y")
    local_conv = conv_w.shape[2]
    xs, target = x[0], loss_target[0]
    rope = _rope_inputs(xs.shape[0])

    hb_up, hb_down, hb_out, hb_in = _cast_halves(core, w_up[0], w_down[0], w_out[0], w_in[0].T)
    conv_pad = jnp.pad(conv_w[0], ((0, 8 - CONV_K), (0, 0)))
    wf_in, conv_all = _gather_whole(hb_in, conv_pad)
    conv_full = conv_all[:, :CONV_K, :].transpose(1, 0, 2).reshape(CONV_K, CONV_WIDTH)

    *proj, wf_up, wf_out = _in_proj(xs, pre_mix_norm, wf_in, rope, comm=_merge(_relay_first(hb_up), _gather_first(hb_out)))
    q, kd0, kd1, vd0, vd1, gb, gc, xin, hn = proj
    attn, wf_up, wf_out, wf_down = _attention_fwd(
        q, kd0, kd1, vd0, vd1, attn_sinks,
        comm=_merge(_relay_second(wf_up), _gather_second(wf_out), _relay_first(hb_down)))
    mix, mixed, wf_up, wf_down = _mix_out(attn, gb, gc, xin, conv_full, attn_group_norm, conv_group_norm, wf_out,
                                          comm=_merge(_relay_third(wf_up), _relay_second(wf_down, then_third=True)))
    up, hn2, dout, dmlp, loss_sum, dg_post_mlp = _mlp_loss(xs, mix, target, post_mix_norm, pre_mlp_norm, post_mlp_norm,
                                                           wf_up, wf_down)

    dup, dh, dmix, dg_pre_mlp, dg_post_mix = _mlp_bwd(dmlp, up, xs, dout, mix, pre_mlp_norm, post_mix_norm, wf_up, wf_down)
    n_k = _wgrad_grid(xs.shape[0], True, H_DOWN)[3]
    g_down, dattn, dgb, dy, dg_attn, dg_conv, dconv_w = _wgrad(
        "wgrad_down", up, dmlp, per_chip=True, h_rows=H_DOWN, square_a=True,
        rider=_mix_bwd(dmix, attn, gb, gc, xin, conv_full, attn_group_norm, conv_group_norm, wf_out, n_k))
    g_up, got_down = _wgrad("wgrad_up", hn2, dup, per_chip=True, h_rows=H_UP, comm=_pair_send(g_down))
    p_down = _pair_sum("pair_sum_down", core, g_down, got_down)
    g_out, got_up = _wgrad("wgrad_out", mixed, dmix, per_chip=False, h_rows=H_OUT, comm=_pair_send(g_up))
    p_up = _pair_sum("pair_sum_up", core, g_up, got_up)
    dq, dk0, dk1, dv0, dv1, dsink, ex_down, ex_up, got_out = _attention_bwd(
        q, dattn, attn, kd0, kd1, vd0, vd1, attn_sinks,
        comm=_merge(_chip_exchange(p_down), _chip_exchange(p_up), _pair_send(g_out)))
    p_out = _pair_sum("pair_sum_out", core, g_out, got_out)
    dproj, grad_x, dg_pre_mix = _in_proj_bwd(dq, dk0, dk1, dv0, dv1, dgb, dy, gc, xin, conv_full, xs, dh, pre_mix_norm,
                                             wf_in, rope)
    g_in, ex_out = _wgrad("wgrad_in", dproj, hn, per_chip=False, h_rows=H_IN, comm=_chip_exchange(p_out))
    small = dict(pre_mix_norm=dg_pre_mix, conv_w=dconv_w, attn_sinks=dsink[:, :8], attn_group_norm=dg_attn,
                 conv_group_norm=dg_conv, post_mix_norm=dg_post_mix, pre_mlp_norm=dg_pre_mlp, post_mlp_norm=dg_post_mlp,
                 loss_sum=loss_sum)
    r_down, r_up, r_out, r_in, small_total = _tail_reduce(g_in, [ex_down, ex_up, ex_out], _pack_small(small))

    out_g, out_d, out_m, out_v = {}, {}, {}, {}
    out_g["w_up"], out_d["w_up"], out_m["w_up"], out_v["w_up"] = _adamw_rows(
        "adamw_up", r_up, w_up[0], m_w_up[0], v_w_up[0], 256)
    out_g["w_down"], out_d["w_down"], out_m["w_down"], out_v["w_down"] = _adamw_rows(
        "adamw_down", r_down, w_down[0], m_w_down[0], v_w_down[0], 256)
    out_g["w_out"], out_d["w_out"], out_m["w_out"], out_v["w_out"] = _adamw_rows(
        "adamw_out", r_out, w_out[0], m_w_out[0], v_w_out[0], H_OUT)
    in_t = _adamw_rows("adamw_in", r_in, w_in[0].T, m_w_in[0].T, v_w_in[0].T, H_IN)
    out_g["w_in"], out_d["w_in"], out_m["w_in"], out_v["w_in"] = [t.T for t in in_t]

    small_sum = _unpack_small(small_total, CONV_WIDTH)
    loss = small_sum["loss_sum"] * (0.5 / D_MODEL)
    small_sum["conv_w"] = lax.dynamic_slice_in_dim(small_sum["conv_w"], chip * local_conv, local_conv, axis=2)
    packed = [_pack_small({n: t[n] for n in SMALL_NAMES}) for t in (w, small_sum, m, v)]
    small_d, small_m, small_v = [_unpack_small(t, local_conv) for t in _adamw_small(*packed)]
    for n in SMALL_NAMES:
        out_g[n], out_d[n], out_m[n], out_v[n] = small_sum[n], small_d[n], small_m[n], small_v[n]

    def shaped(d):
        return [d[n].reshape(w[n].shape) for n in WEIGHT_ORDER]

    return (loss, grad_x[None], *shaped(out_g), *shaped(out_d), *shaped(out_m), *shaped(out_v))
```

```python
import math
from typing import Callable, NamedTuple

import jax
import jax.numpy as jnp
import numpy as np
from jax import lax
from jax.experimental import pallas as pl
from jax.experimental.pallas import tpu as pltpu

F32 = jnp.float32
BF16 = jnp.bfloat16

D_MODEL = 1024
HEAD_DIM = 64
Q_WIDTH = 512
KV_WIDTH = 128
CONV_WIDTH = 512
CONV_K = 3
D_FF = 4096
IN_COLS = 2304
QBLOCK = 128
ROT_DIM = 16
ROPE_THETA = 500000.0
NORM_EPS = 1e-6
NEG_INF = -1e30
N_CHIPS = 4

ADAM_LR = 0.001
ADAM_B1 = 0.9
ADAM_B2 = 0.999
ADAM_EPS = 1e-08
ADAM_WD = 0.01
ADAM_STEP = 10

H_UP, H_DOWN, H_OUT, H_IN = 512, 512, 128, 288

TOKEN_TILE = 512
WIDE_TOKEN_TILE = 1024
MLP_BWD_TOKEN_TILE = 512
MLP_BWD_SUB_TILE = 256
ATTN_FWD_BLOCKS = 16
ATTN_BWD_BLOCKS = 2
WGRAD_TOKEN_TILE = 2048
VMEM_LIMIT_V7X = 56 * 1024 * 1024

MESH = pl.DeviceIdType.MESH
ANY = pl.BlockSpec(memory_space=pl.ANY)
VMEM_WHOLE = pl.BlockSpec(memory_space=pltpu.VMEM)
SDS = jax.ShapeDtypeStruct


def _resident(shape):
    zeros = (0,) * len(shape)
    return pl.BlockSpec(shape, lambda *_: zeros, pipeline_mode=pl.Buffered(1))


def _rms(v):
    return lax.rsqrt(jnp.mean(v * v, axis=-1, keepdims=True) + NORM_EPS)


def _norm_bwd(dy, gain, vhat, rstd):
    t = dy * gain
    return rstd * (t - vhat * jnp.mean(t * vhat, axis=-1, keepdims=True))


def _colsum(v):
    return jnp.sum(v, axis=0, keepdims=True)


def _dot_nt(a, b):
    return lax.dot_general(a, b, (((1,), (1,)), ((), ())), preferred_element_type=F32)


def _dot_tn(a, b):
    return lax.dot_general(a, b, (((0,), (0,)), ((), ())), preferred_element_type=F32)


def _dot(a, b):
    return jnp.dot(a, b, preferred_element_type=F32)


def _chip_block(w_ref, chip):
    both = w_ref[pl.ds(2 * chip, 2)]
    return both.reshape(2 * both.shape[1], both.shape[2])


def _lane_lt64(shape):
    return lax.broadcasted_iota(jnp.int32, shape, 1) < HEAD_DIM


class _Comm(NamedTuple):
    operands: tuple
    out_shapes: tuple
    aliases: dict
    n_remote: int
    n_local: int
    plan: Callable
    after: Callable = None


def _merge(*comms):
    operands, out_shapes, aliases, parts = [], [], {}, []
    n_remote = n_local = 0
    for cm in comms:
        parts.append((len(operands), len(out_shapes), n_remote, n_local, cm))
        for k, v in cm.aliases.items():
            aliases[len(operands) + k] = len(out_shapes) + v
        operands += cm.operands
        out_shapes += cm.out_shapes
        n_remote += cm.n_remote
        n_local += cm.n_local

    def run(which, ins, outs, send, recv, loc):
        sends, recvs, locs = [], [], []
        for i0, o0, r0, l0, cm in parts:
            stage = getattr(cm, which)
            if stage is not None:
                s, r, l = stage(ins[i0:i0 + len(cm.operands)], outs[o0:o0 + len(cm.out_shapes)],
                                lambda k, r0=r0: send(r0 + k), lambda k, r0=r0: recv(r0 + k), lambda k, l0=l0: loc(l0 + k))
                sends, recvs, locs = sends + s, recvs + r, locs + l
        return sends, recvs, locs

    def plan(*args):
        return run("plan", *args)

    def after(*args):
        return run("after", *args)

    return _Comm(tuple(operands), tuple(out_shapes), aliases, n_remote, n_local, plan,
                 after if any(cm.after is not None for cm in comms) else None)


def _sem_scratch(comm):
    return [pltpu.SemaphoreType.DMA((max(comm.n_remote, 1),)), pltpu.SemaphoreType.DMA((max(comm.n_remote, 1),)),
            pltpu.SemaphoreType.DMA((max(comm.n_local, 1),))]


class _Rider(NamedTuple):
    body: Callable
    in_specs: list
    out_specs: list
    out_shape: list
    operands: tuple


def _pallas(body, *, name, grid, in_specs, out_specs, out_shape, operands, scratch=(), comm=None, rider=None):
    params = pltpu.CompilerParams(dimension_semantics=("arbitrary",) * len(grid), vmem_limit_bytes=VMEM_LIMIT_V7X)
    if rider is not None:
        own_in, own_out, ride_in, ride_out = len(in_specs), len(out_specs), len(rider.in_specs), len(rider.out_specs)
        own_body = body

        def body(*refs):
            o0 = own_in + ride_in
            s0 = o0 + own_out + ride_out
            own_body(*refs[:own_in], *refs[o0:o0 + own_out], *refs[s0:])
            first = None
            for axis in range(len(grid)):
                at_start = pl.program_id(axis) == 0
                first = at_start if first is None else jnp.logical_and(first, at_start)
            rider.body(first, *refs[own_in:o0], *refs[o0 + own_out:s0])

        in_specs, out_specs = list(in_specs) + rider.in_specs, list(out_specs) + rider.out_specs
        out_shape, operands = list(out_shape) + rider.out_shape, tuple(operands) + tuple(rider.operands)
    if comm is None:
        return pl.pallas_call(body, name=name, grid=grid, in_specs=in_specs, out_specs=out_specs, out_shape=out_shape,
                              scratch_shapes=list(scratch), compiler_params=params)(*operands)
    n_in, n_out, n_scr = len(in_specs), len(out_specs), len(scratch)
    c_in, c_out = len(comm.operands), len(comm.out_shapes)

    def with_comm(*refs):
        ins, c_ins = refs[:n_in], refs[n_in:n_in + c_in]
        o0 = n_in + c_in
        outs, c_outs = refs[o0:o0 + n_out], refs[o0 + n_out:o0 + n_out + c_out]
        s0 = o0 + n_out + c_out
        scr = refs[s0:s0 + n_scr]
        send_sems, recv_sems, local_sems = refs[s0 + n_scr:]
        first = last = None
        for axis, size in enumerate(grid):
            at_start, at_end = pl.program_id(axis) == 0, pl.program_id(axis) == size - 1
            first = at_start if first is None else jnp.logical_and(first, at_start)
            last = at_end if last is None else jnp.logical_and(last, at_end)

        def copies():
            return comm.plan(c_ins, c_outs, lambda k: send_sems.at[k], lambda k: recv_sems.at[k],
                             lambda k: local_sems.at[k])

        @pl.when(first)
        def _():
            sends, _, locs = copies()
            for cp in sends + locs:
                cp.start()

        body(*ins, *outs, *scr)

        @pl.when(last)
        def _():
            sends, recvs, locs = copies()
            for cp in recvs:
                cp.wait_recv()
            for cp in sends:
                cp.wait_send()
            for cp in locs:
                cp.wait()
            if comm.after is not None:
                sends, recvs, _ = comm.after(c_ins, c_outs, lambda k: send_sems.at[k], lambda k: recv_sems.at[k],
                                             lambda k: local_sems.at[k])
                for cp in sends:
                    cp.start()
                for cp in recvs:
                    cp.wait_recv()
                for cp in sends:
                    cp.wait_send()

    return pl.pallas_call(
        with_comm, name=name, grid=grid,
        in_specs=list(in_specs) + [ANY] * c_in, out_specs=list(out_specs) + [ANY] * c_out,
        out_shape=list(out_shape) + list(comm.out_shapes),
        scratch_shapes=list(scratch) + _sem_scratch(comm),
        input_output_aliases={n_in + k: n_out + v for k, v in comm.aliases.items()},
        compiler_params=params)(*operands, *comm.operands)


def _place():
    return lax.axis_index("x"), lax.axis_index("y"), lax.axis_index("c")


def _other_chips(x, y):
    return [(1 - x, y), (x, 1 - y), (1 - x, 1 - y)]


def _slot(px, py, pc):
    return 4 * px + 2 * py + pc


def _remote(src, dst, send_sem, recv_sem, to):
    return pltpu.make_async_remote_copy(src_ref=src, dst_ref=dst, send_sem=send_sem, recv_sem=recv_sem,
                                        device_id=to, device_id_type=MESH)


def _gather_first(half_block):
    def plan(ins, outs, send, recv, loc):
        (blk,), (full,) = ins, outs
        x, y, c = _place()
        chips = _other_chips(x, y)
        mine = full.at[_slot(x, y, c)]
        sends = [_remote(blk, mine, send(0), recv(0), (x, y, 1 - c))]
        sends += [_remote(blk, mine, send(1 + j), recv(1 + j), (*chip, c)) for j, chip in enumerate(chips)]
        recvs = [_remote(blk, full.at[_slot(x, y, 1 - c)], send(0), recv(0), (x, y, 1 - c))]
        recvs += [_remote(blk, full.at[_slot(*chip, c)], send(1 + j), recv(1 + j), (*chip, c))
                  for j, chip in enumerate(chips)]
        return sends, recvs, [pltpu.make_async_copy(blk, mine, loc(0))]

    return _Comm((half_block,), (SDS((2 * N_CHIPS,) + half_block.shape, half_block.dtype),), {}, 4, 1, plan)


def _gather_second(partly_gathered):
    def plan(ins, outs, send, recv, loc):
        (src,), (full,) = ins, outs
        x, y, c = _place()
        chips = _other_chips(x, y)
        sends = [_remote(src.at[_slot(*chip, c)], full.at[_slot(*chip, c)], send(j), recv(j), (x, y, 1 - c))
                 for j, chip in enumerate(chips)]
        recvs = [_remote(src.at[_slot(*chip, 1 - c)], full.at[_slot(*chip, 1 - c)], send(j), recv(j), (x, y, 1 - c))
                 for j, chip in enumerate(chips)]
        return sends, recvs, []

    return _Comm((partly_gathered,), (SDS(partly_gathered.shape, partly_gathered.dtype),), {0: 0}, 3, 0, plan)


def _relay_pieces(full, rows, x, y, c):
    half = rows // 2
    upper, lower = pl.ds(0, half), pl.ds(half, half)
    diagonal = full.at[_slot(1 - x, 1 - y, c)]
    return [(full.at[_slot(1 - x, y, c), upper], diagonal.at[upper], (x, 1 - y, c)),
            (full.at[_slot(x, 1 - y, c), lower], diagonal.at[lower], (1 - x, y, c))]


def _relay_first(half_block):
    def plan(ins, outs, send, recv, loc):
        (blk,), (full,) = ins, outs
        x, y, c = _place()
        peers = [(x, y, 1 - c), (1 - x, y, c), (x, 1 - y, c)]
        mine = full.at[_slot(x, y, c)]
        sends = [_remote(blk, mine, send(k), recv(k), peer) for k, peer in enumerate(peers)]
        recvs = [_remote(blk, full.at[_slot(*peer)], send(k), recv(k), peer) for k, peer in enumerate(peers)]
        return sends, recvs, [pltpu.make_async_copy(blk, mine, loc(0))]

    return _Comm((half_block,), (SDS((2 * N_CHIPS,) + half_block.shape, half_block.dtype),), {}, 3, 1, plan)


def _third_leg(src, full, send, recv, k):
    x, y, c = _place()
    sibling = (x, y, 1 - c)
    here, there = _slot(1 - x, 1 - y, c), _slot(1 - x, 1 - y, 1 - c)
    return ([_remote(src.at[here], full.at[here], send(k), recv(k), sibling)],
            [_remote(src.at[there], full.at[there], send(k), recv(k), sibling)], [])


def _relay_second(partly_gathered, then_third=False):
    rows = partly_gathered.shape[1]

    def plan(ins, outs, send, recv, loc):
        (src,), (full,) = ins, outs
        x, y, c = _place()
        sibling = (x, y, 1 - c)
        sends, recvs = [], []
        for k, chip in enumerate([(1 - x, y), (x, 1 - y)]):
            sends.append(_remote(src.at[_slot(*chip, c)], full.at[_slot(*chip, c)], send(k), recv(k), sibling))
            recvs.append(_remote(src.at[_slot(*chip, 1 - c)], full.at[_slot(*chip, 1 - c)], send(k), recv(k), sibling))
        for k, (piece, lands, peer) in enumerate(_relay_pieces(full, rows, x, y, c)):
            sends.append(_remote(piece, piece, send(2 + k), recv(2 + k), peer))
            recvs.append(_remote(lands, lands, send(2 + k), recv(2 + k), peer))
        return sends, recvs, []

    def after(ins, outs, send, recv, loc):
        return _third_leg(ins[0], outs[0], send, recv, 4)

    return _Comm((partly_gathered,), (SDS(partly_gathered.shape, partly_gathered.dtype),), {0: 0}, 5, 0, plan,
                 after if then_third else None)


def _relay_third(mostly_gathered):
    def plan(ins, outs, send, recv, loc):
        return _third_leg(ins[0], outs[0], send, recv, 0)

    return _Comm((mostly_gathered,), (SDS(mostly_gathered.shape, mostly_gathered.dtype),), {0: 0}, 1, 0, plan)


def _gather_whole(half_block, small_block):
    rows = half_block.shape[0]

    def body(blk_ref, small_ref, out_ref, small_out_ref, send_sems, recv_sems, local_sems):
        x, y, c = _place()
        me, sibling = (x, y, c), (x, y, 1 - c)
        neighbours, diagonal = [(1 - x, y), (x, 1 - y)], (1 - x, 1 - y)

        def copy(k, block, to, src=None):
            return _remote(out_ref.at[_slot(*block)] if src is None else src, out_ref.at[_slot(*block)],
                           send_sems.at[k], recv_sems.at[k], to)

        def small_copy(k, chip, to):
            return _remote(small_ref, small_out_ref.at[2 * chip[0] + chip[1]], send_sems.at[8 + k], recv_sems.at[8 + k], to)

        mine = pltpu.make_async_copy(blk_ref, out_ref.at[_slot(*me)], local_sems.at[0])
        mine_small = pltpu.make_async_copy(small_ref, small_out_ref.at[2 * x + y], local_sems.at[1])
        mine.start()
        mine_small.start()
        started = [copy(0, me, sibling, src=blk_ref)]
        started += [copy(1 + k, me, (*chip, c), src=blk_ref) for k, chip in enumerate(neighbours)]
        started += [small_copy(k, (x, y), (*chip, c)) for k, chip in enumerate(neighbours + [diagonal])]
        for cp in started:
            cp.start()
        pieces = _relay_pieces(out_ref, rows, x, y, c)
        for k, chip in enumerate(neighbours):
            copy(1 + k, (*chip, c), me).wait_recv()
            piece, _, peer = pieces[k]
            started += [copy(3 + k, (*chip, c), sibling), _remote(piece, piece, send_sems.at[5 + k], recv_sems.at[5 + k], peer)]
            started[-2].start()
            started[-1].start()
        for k, (_, lands, peer) in enumerate(pieces):
            _remote(lands, lands, send_sems.at[5 + k], recv_sems.at[5 + k], peer).wait_recv()
        started.append(copy(7, (*diagonal, c), sibling))
        started[-1].start()
        copy(0, sibling, me).wait_recv()
        for k, chip in enumerate(neighbours):
            copy(3 + k, (*chip, 1 - c), me).wait_recv()
        copy(7, (*diagonal, 1 - c), me).wait_recv()
        for k, chip in enumerate(neighbours + [diagonal]):
            small_copy(k, chip, me).wait_recv()
        for cp in started:
            cp.wait_send()
        mine.wait()
        mine_small.wait()

    return pl.pallas_call(
        body, name="gather_whole", in_specs=[ANY, ANY], out_specs=[ANY, ANY],
        out_shape=[SDS((2 * N_CHIPS,) + half_block.shape, half_block.dtype),
                   SDS((N_CHIPS,) + small_block.shape, small_block.dtype)],
        scratch_shapes=[pltpu.SemaphoreType.DMA((11,)), pltpu.SemaphoreType.DMA((11,)), pltpu.SemaphoreType.DMA((2,))],
    )(half_block, small_block)


def _pair_send(grads):
    def plan(ins, outs, send, recv, loc):
        (g,), (got,) = ins, outs
        x, y, c = _place()
        copies = [_remote(g.at[j, 1 - c], got.at[j], send(j), recv(j), (x, y, 1 - c)) for j in range(N_CHIPS)]
        return copies, copies, []

    shape = (grads.shape[0],) + grads.shape[2:]
    return _Comm((grads,), (SDS(shape, grads.dtype),), {}, N_CHIPS, 0, plan)


def _chip_exchange(partial):
    def plan(ins, outs, send, recv, loc):
        (p,), (got,) = ins, outs
        x, y, c = _place()
        my_chip = 2 * x + y
        chips = _other_chips(x, y)
        sends = [_remote(p.at[2 * chip[0] + chip[1]], got.at[my_chip], send(j), recv(j), (*chip, c))
                 for j, chip in enumerate(chips)]
        recvs = [_remote(p.at[my_chip], got.at[2 * chip[0] + chip[1]], send(j), recv(j), (*chip, c))
                 for j, chip in enumerate(chips)]
        return sends, recvs, [pltpu.make_async_copy(p.at[my_chip], got.at[my_chip], loc(0))]

    return _Comm((partial,), (SDS(partial.shape, partial.dtype),), {}, 3, 1, plan)


def _pair_sum(name, core, grads, received):
    h = grads.shape[2]

    def body(core_ref, g_ref, r_ref, o_ref):
        o_ref[...] = (g_ref[0] + r_ref[...]).astype(BF16)

    return pl.pallas_call(
        body, name=name,
        grid_spec=pltpu.PrefetchScalarGridSpec(
            num_scalar_prefetch=1, grid=(N_CHIPS,),
            in_specs=[pl.BlockSpec((1, 1, h, D_MODEL), lambda j, core_ref: (j, core_ref[0], 0, 0)),
                      pl.BlockSpec((1, h, D_MODEL), lambda j, core_ref: (j, 0, 0))],
            out_specs=pl.BlockSpec((1, h, D_MODEL), lambda j, core_ref: (j, 0, 0))),
        out_shape=SDS((N_CHIPS, h, D_MODEL), BF16),
        compiler_params=pltpu.CompilerParams(dimension_semantics=("arbitrary",), vmem_limit_bytes=VMEM_LIMIT_V7X),
    )(core, grads, received)


SMALL_ROWS = 8


def _sum_blocks(ref):
    return (ref[0].astype(F32) + ref[1].astype(F32)) + (ref[2].astype(F32) + ref[3].astype(F32))


def _tail_reduce(last_grads, exchanged, small):
    n = len(exchanged)
    h = last_grads.shape[2]

    def body(*refs):
        g_ref, ex, small_ref = refs[0], refs[1:1 + n], refs[1 + n]
        o0 = 2 + n
        out, out_last, small_out = refs[o0:o0 + n], refs[o0 + n], refs[o0 + n + 1]
        s0 = o0 + n + 2
        halves, half_last = refs[s0:s0 + n], refs[s0 + n]
        own, got, part, exch, small_buf = refs[s0 + n + 1:s0 + n + 6]
        pair_send, pair_recv, chip_send, chip_recv, share_send, share_recv, small_send, small_recv, local_sems = refs[s0 + n + 6:]
        x, y, c = _place()
        sibling = (x, y, 1 - c)
        my_chip, me = 2 * x + y, _slot(x, y, c)
        chips = _other_chips(x, y)

        to_sibling = [_remote(g_ref.at[j, 1 - c], got.at[j], pair_send.at[j], pair_recv.at[j], sibling)
                      for j in range(N_CHIPS)]
        load_own = [pltpu.make_async_copy(g_ref.at[j, c], own.at[j], local_sems.at[j]) for j in range(N_CHIPS)]
        for cp in to_sibling + load_own:
            cp.start()

        small_buf[me] = small_ref[...]
        small_copies = []
        for mask in range(1, 8):
            peer = (x ^ (mask >> 2), y ^ ((mask >> 1) & 1), c ^ (mask & 1))
            small_copies.append(_remote(small_ref, small_buf.at[me], small_send.at[mask - 1], small_recv.at[mask - 1], peer))
        for cp in small_copies:
            cp.start()

        def share(k, half_ref, out_ref):
            keep = pltpu.make_async_copy(half_ref, out_ref.at[c], local_sems.at[N_CHIPS + k])
            give = _remote(half_ref, out_ref.at[c], share_send.at[k], share_recv.at[k], sibling)
            take = _remote(half_ref, out_ref.at[1 - c], share_send.at[k], share_recv.at[k], sibling)
            keep.start()
            give.start()
            return keep, give, take

        shares = []
        for k in range(n):
            halves[k][...] = _sum_blocks(ex[k])
            shares.append(share(k, halves[k], out[k]))

        for cp in to_sibling:
            cp.wait_recv()
        for cp in load_own:
            cp.wait()
        part[...] = (own[...] + got[...]).astype(BF16)
        exch[my_chip] = part[my_chip]
        to_chips = [_remote(part.at[2 * chip[0] + chip[1]], exch.at[my_chip], chip_send.at[j], chip_recv.at[j], (*chip, c))
                    for j, chip in enumerate(chips)]
        from_chips = [_remote(part.at[my_chip], exch.at[2 * chip[0] + chip[1]], chip_send.at[j], chip_recv.at[j], (*chip, c))
                      for j, chip in enumerate(chips)]
        for cp in to_chips:
            cp.start()

        for cp in small_copies:
            cp.wait_recv()
        total = small_buf[0]
        for d in range(1, 8):
            total = total + small_buf[d]
        small_out[...] = total

        for cp in from_chips:
            cp.wait_recv()
        half_last[...] = _sum_blocks(exch)
        shares.append(share(n, half_last, out_last))

        for keep, give, take in shares:
            take.wait_recv()
            give.wait_send()
            keep.wait()
        for cp in to_sibling + to_chips + small_copies:
            cp.wait_send()

    blocks = (N_CHIPS, h, D_MODEL)
    return pl.pallas_call(
        body, name="tail_reduce",
        in_specs=[ANY] + [VMEM_WHOLE] * (n + 1), out_specs=[ANY] * (n + 1) + [VMEM_WHOLE],
        out_shape=[SDS((2,) + e.shape[1:], F32) for e in exchanged] + [SDS((2, h, D_MODEL), F32), SDS(small.shape, F32)],
        scratch_shapes=[pltpu.VMEM(e.shape[1:], F32) for e in exchanged] + [pltpu.VMEM((h, D_MODEL), F32)]
                       + [pltpu.VMEM(blocks, F32), pltpu.VMEM(blocks, F32), pltpu.VMEM(blocks, BF16), pltpu.VMEM(blocks, BF16),
                          pltpu.VMEM((8,) + small.shape, F32)]
                       + [pltpu.SemaphoreType.DMA((N_CHIPS,)), pltpu.SemaphoreType.DMA((N_CHIPS,)),
                          pltpu.SemaphoreType.DMA((3,)), pltpu.SemaphoreType.DMA((3,)),
                          pltpu.SemaphoreType.DMA((n + 1,)), pltpu.SemaphoreType.DMA((n + 1,)),
                          pltpu.SemaphoreType.DMA((7,)), pltpu.SemaphoreType.DMA((7,)),
                          pltpu.SemaphoreType.DMA((N_CHIPS + n + 1,))],
        compiler_params=pltpu.CompilerParams(vmem_limit_bytes=VMEM_LIMIT_V7X),
    )(last_grads, *exchanged, small)


def _rope_expansion():
    half = ROT_DIM // 2
    expand = np.zeros((2 * half, 3 * 128), np.float32)
    const = np.zeros((1, 3 * 128), np.float32)
    for lane in range(128):
        d = lane % HEAD_DIM
        if d < ROT_DIM:
            expand[d % half, lane] = 1.0
        else:
            const[0, lane] = 1.0
        if d < half:
            expand[half + d, 128 + lane] = -1.0
        elif d < ROT_DIM:
            expand[half + d - half, 256 + lane] = 1.0
    return expand, const


ROPE_PIECES = 3 * ROT_DIM


def _rope_inputs(seq):
    pos = jnp.arange(seq, dtype=F32)
    inv_freq = ROPE_THETA ** (-jnp.arange(0, ROT_DIM, 2, dtype=F32) / ROT_DIM)
    ang = pos[:, None] * inv_freq[None, :]
    cs = jnp.concatenate([jnp.cos(ang), jnp.sin(ang)], axis=1)
    hi = lax.reduce_precision(cs, 8, 7)
    mid = lax.reduce_precision(cs - hi, 8, 7)
    low = cs - hi - mid
    expand, const = _rope_expansion()
    pieces = jnp.concatenate([hi, mid, low], axis=1).astype(BF16)
    return pieces, jnp.asarray(np.concatenate([expand] * 3, axis=0), BF16), jnp.asarray(const)


def _rope_specs(tb):
    return [pl.BlockSpec((tb, ROPE_PIECES), lambda i: (i, 0)), _resident((ROPE_PIECES, 3 * 128)), _resident((1, 3 * 128))]


def _rope_tile(pieces_ref, expand_ref, const_ref):
    tables = _dot(pieces_ref[...], expand_ref[...]) + const_ref[...]
    return tables[:, 0:128], tables[:, 128:256], tables[:, 256:384]


def _rope(t, c, sa, sb):
    half = ROT_DIM // 2
    return t * c + pltpu.roll(t, 128 - half, 1) * sa + pltpu.roll(t, half, 1) * sb


def _rope_transposed(dt, c, sa, sb):
    half = ROT_DIM // 2
    return dt * c + pltpu.roll(dt * sa, half, 1) + pltpu.roll(dt * sb, 128 - half, 1)


def _cast_halves(core, w_up, w_down, w_out, w_in_t):
    def body(core_ref, up_ref, down_ref, out_ref, in_ref, up_o, down_o, out_o, in_o):
        up_o[...] = up_ref[...].astype(BF16)
        down_o[...] = down_ref[...].astype(BF16)
        out_o[...] = out_ref[...].astype(BF16)
        in_o[...] = in_ref[...].astype(BF16)

    half = lambda rows: pl.BlockSpec((rows, D_MODEL), lambda i, core_ref: (core_ref[0], 0))
    whole = lambda rows: pl.BlockSpec((rows, D_MODEL), lambda i, core_ref: (0, 0))
    rows = (H_UP, H_DOWN, H_OUT, H_IN)
    return pl.pallas_call(
        body, name="cast_halves",
        grid_spec=pltpu.PrefetchScalarGridSpec(
            num_scalar_prefetch=1, grid=(1,), in_specs=[half(r) for r in rows], out_specs=[whole(r) for r in rows]),
        out_shape=[SDS((r, D_MODEL), BF16) for r in rows],
        compiler_params=pltpu.CompilerParams(dimension_semantics=("arbitrary",), vmem_limit_bytes=VMEM_LIMIT_V7X),
    )(core, w_up, w_down, w_out, w_in_t)


def _in_proj(x, g_pre, w_in_t, rope, comm=None):
    seq = x.shape[0]
    tb = min(seq, WIDE_TOKEN_TILE)

    def body(x_ref, g_ref, w_ref, c_ref, sa_ref, sb_ref,
             q_ref, kd0_ref, kd1_ref, vd0_ref, vd1_ref, gb_ref, gc_ref, xin_ref, hn_ref):
        xv = x_ref[...]
        hn = (xv * _rms(xv) * g_ref[...]).astype(BF16)
        hn_ref[...] = hn
        proj = _dot_nt(hn, w_ref[...].reshape(IN_COLS, D_MODEL))
        c, sa, sb = _rope_tile(c_ref, sa_ref, sb_ref)
        scale = 1.0 / math.sqrt(HEAD_DIM)
        for p in range(Q_WIDTH // 128):
            q_ref[:, 128 * p:128 * (p + 1)] = (_rope(proj[:, 128 * p:128 * (p + 1)], c, sa, sb) * scale).astype(BF16)
        k = _rope(proj[:, Q_WIDTH:Q_WIDTH + KV_WIDTH], c, sa, sb)
        v = proj[:, Q_WIDTH + KV_WIDTH:Q_WIDTH + 2 * KV_WIDTH]
        low = _lane_lt64(k.shape)
        k_sw, v_sw = pltpu.roll(k, HEAD_DIM, 1), pltpu.roll(v, HEAD_DIM, 1)
        kd0_ref[...] = jnp.where(low, k, k_sw).astype(BF16)
        kd1_ref[...] = jnp.where(low, k_sw, k).astype(BF16)
        vd0_ref[...] = jnp.where(low, v, v_sw).astype(BF16)
        vd1_ref[...] = jnp.where(low, v_sw, v).astype(BF16)
        base = Q_WIDTH + 2 * KV_WIDTH
        gb_ref[...] = proj[:, base:base + CONV_WIDTH].astype(BF16)
        gc_ref[...] = proj[:, base + CONV_WIDTH:base + 2 * CONV_WIDTH].astype(BF16)
        xin_ref[...] = proj[:, base + 2 * CONV_WIDTH:base + 3 * CONV_WIDTH].astype(BF16)

    tile = lambda w: pl.BlockSpec((tb, w), lambda i: (i, 0))
    return _pallas(
        body, name="in_proj", grid=(seq // tb,),
        in_specs=[tile(D_MODEL), _resident((1, D_MODEL)), _resident(w_in_t.shape), *_rope_specs(tb)],
        out_specs=[tile(Q_WIDTH), tile(128), tile(128), tile(128), tile(128),
                   tile(CONV_WIDTH), tile(CONV_WIDTH), tile(CONV_WIDTH), tile(D_MODEL)],
        out_shape=[SDS((seq, Q_WIDTH), BF16)] + [SDS((seq, 128), BF16)] * 4
                  + [SDS((seq, CONV_WIDTH), BF16)] * 3 + [SDS((seq, D_MODEL), BF16)],
        operands=(x, g_pre, w_in_t, *rope), comm=comm)


def _attn_valid(i):
    shape = (4 * QBLOCK, 2 * QBLOCK)
    row = lax.broadcasted_iota(jnp.int32, shape, 0)
    col = lax.broadcasted_iota(jnp.int32, shape, 1)
    qi = row & (QBLOCK - 1)
    return (col > qi) & (col <= qi + QBLOCK) & ((col >= QBLOCK) | (i > 0))


def _stack_heads(pair0, pair1):
    low = _lane_lt64(pair0.shape)
    zero = jnp.zeros_like(pair0)
    return jnp.concatenate([jnp.where(low, pair0, zero), jnp.where(low, zero, pair0),
                            jnp.where(low, pair1, zero), jnp.where(low, zero, pair1)], axis=0)


def _unstack_heads(stacked):
    low = _lane_lt64((QBLOCK, 128))
    pair0 = jnp.where(low, stacked[0:QBLOCK], stacked[QBLOCK:2 * QBLOCK])
    pair1 = jnp.where(low, stacked[2 * QBLOCK:3 * QBLOCK], stacked[3 * QBLOCK:4 * QBLOCK])
    return pair0, pair1


def _sink_column(sink_ref, kv_head):
    row = lax.broadcasted_iota(jnp.int32, (4 * QBLOCK, 1), 0)
    s = [sink_ref[0, 4 * kv_head + j] for j in range(4)]
    return jnp.where(row < QBLOCK, s[0], jnp.where(row < 2 * QBLOCK, s[1], jnp.where(row < 3 * QBLOCK, s[2], s[3])))


def _band(ref, i):
    prev = pl.multiple_of(jnp.maximum(i - 1, 0) * QBLOCK, QBLOCK)
    own = pl.multiple_of(i * QBLOCK, QBLOCK)
    return jnp.concatenate([ref[pl.ds(prev, QBLOCK), :], ref[pl.ds(own, QBLOCK), :]], axis=0), prev, own


def _softmax_with_sink(s, sink_col):
    m = jnp.maximum(jnp.max(s, axis=-1, keepdims=True), sink_col)
    p = jnp.exp(s - m)
    e_sink = jnp.exp(sink_col - m)
    inv_l = 1.0 / (jnp.sum(p, axis=-1, keepdims=True) + e_sink)
    return p, e_sink, inv_l


def _attention_fwd(q, kd0, kd1, vd0, vd1, sinks, comm=None):
    seq = q.shape[0]

    nb = ATTN_FWD_BLOCKS

    def body(sink_ref, q_ref, kd0_ref, kd1_ref, vd0_ref, vd1_ref, o_ref):
        for b in range(nb):
            i = pl.program_id(0) * nb + b
            rows = slice(QBLOCK * b, QBLOCK * (b + 1))
            valid = _attn_valid(i)
            for kv_head, (k_ref, v_ref) in enumerate(((kd0_ref, vd0_ref), (kd1_ref, vd1_ref))):
                kband, _, _ = _band(k_ref, i)
                vband, _, _ = _band(v_ref, i)
                base = 256 * kv_head
                qm = _stack_heads(q_ref[rows, base:base + 128], q_ref[rows, base + 128:base + 256])
                s = jnp.where(valid, _dot_nt(qm, kband), NEG_INF)
                p, _, inv_l = _softmax_with_sink(s, _sink_column(sink_ref, kv_head))
                o = _dot(p.astype(BF16), vband) * inv_l
                pair0, pair1 = _unstack_heads(o)
                o_ref[rows, base:base + 128] = pair0.astype(BF16)
                o_ref[rows, base + 128:base + 256] = pair1.astype(BF16)

    blk = pl.BlockSpec((nb * QBLOCK, Q_WIDTH), lambda i: (i, 0))
    full = _resident((seq, 128))
    return _pallas(
        body, name="attention_fwd", grid=(seq // (nb * QBLOCK),),
        in_specs=[pl.BlockSpec(memory_space=pltpu.SMEM), blk, full, full, full, full],
        out_specs=[blk], out_shape=[SDS((seq, Q_WIDTH), BF16)],
        operands=(sinks, q, kd0, kd1, vd0, vd1), comm=comm)


HALO = 16


def _conv_parts(gc, xin, gc_halo, xin_halo, conv_w, first):
    tb = gc.shape[0]
    u = gc.astype(F32) * xin.astype(F32)
    u_halo = jnp.where(first, 0.0, gc_halo.astype(F32) * xin_halo.astype(F32))
    ext = jnp.concatenate([u_halo, u], axis=0)
    u1 = pltpu.roll(ext, 1, 0)[HALO:HALO + tb]
    u2 = pltpu.roll(ext, 2, 0)[HALO:HALO + tb]
    y = conv_w[0:1, :] * u2 + conv_w[1:2, :] * u1 + conv_w[2:3, :] * u
    return u, u1, u2, y


def _halo_prev(tb, w):
    return pl.BlockSpec((HALO, w), lambda i: (jnp.maximum(i * (tb // HALO) - 1, 0), 0))


def _residual_mid(x, mix, g_post_mix):
    mix_f = mix.astype(F32)
    return x + mix_f * _rms(mix_f) * g_post_mix


def _mix_out(attn, gb, gc, xin, conv_w, g_attn, g_conv, w_out, comm=None):
    seq = attn.shape[0]
    tb = min(seq, WIDE_TOKEN_TILE)

    def body(a_ref, gb_ref, gc_ref, xin_ref, gch_ref, xinh_ref, cw_ref, ga_ref, gcn_ref, w_ref, mix_ref, mixed_ref):
        first = pl.program_id(0) == 0
        _, _, _, y = _conv_parts(gc_ref[...], xin_ref[...], gch_ref[...], xinh_ref[...], cw_ref[...], first)
        conv = gb_ref[...].astype(F32) * y
        a = a_ref[...].astype(F32)
        mixed_ref[:, 0:Q_WIDTH] = (a * _rms(a) * ga_ref[...]).astype(BF16)
        mixed_ref[:, Q_WIDTH:] = (conv * _rms(conv) * gcn_ref[...]).astype(BF16)
        mix_ref[...] = _dot(mixed_ref[...], w_ref[...].reshape(D_MODEL, D_MODEL)).astype(BF16)

    tile = lambda w: pl.BlockSpec((tb, w), lambda i: (i, 0))
    return _pallas(
        body, name="mix_out", grid=(seq // tb,),
        in_specs=[tile(Q_WIDTH), tile(CONV_WIDTH), tile(CONV_WIDTH), tile(CONV_WIDTH),
                  _halo_prev(tb, CONV_WIDTH), _halo_prev(tb, CONV_WIDTH),
                  _resident((CONV_K, CONV_WIDTH)), _resident((1, Q_WIDTH)), _resident((1, CONV_WIDTH)),
                  _resident(w_out.shape)],
        out_specs=[tile(D_MODEL), tile(D_MODEL)],
        out_shape=[SDS((seq, D_MODEL), BF16), SDS((seq, D_MODEL), BF16)],
        operands=(attn, gb, gc, xin, gc, xin, conv_w, g_attn, g_conv, w_out), comm=comm)


def _mlp_loss(x, mix, target, g_post_mix, g_pre_mlp, g_post_mlp, w_up, w_down):
    seq = x.shape[0]
    tb = TOKEN_TILE

    def body(x_ref, mix_ref, t_ref, gpm_ref, g2_ref, g4_ref, wup_ref, wdown_ref,
             up_ref, hn2_ref, dout_ref, dmlp_ref, loss_ref, dg4_ref, act_ref):
        @pl.when(pl.program_id(0) == 0)
        def _():
            loss_ref[...] = jnp.zeros_like(loss_ref)
            dg4_ref[...] = jnp.zeros_like(dg4_ref)

        halves = [slice(0, tb // 2), slice(tb // 2, tb)]
        hv, hn2 = [], []
        for rows in halves:
            hv.append(_residual_mid(x_ref[rows, :], mix_ref[rows, :], gpm_ref[...]))
            hn2.append((hv[-1] * _rms(hv[-1]) * g2_ref[...]).astype(BF16))
            hn2_ref[rows, :] = hn2[-1]
        for k, rows in enumerate(halves):
            for j in range(N_CHIPS):
                up = _dot(hn2[k], _chip_block(wup_ref, j))
                up = jnp.maximum(up, 0.0)
                up_ref[rows, 1024 * j:1024 * (j + 1)] = up.astype(BF16)
                act_ref[rows, 1024 * j:1024 * (j + 1)] = (up * up).astype(BF16)
        w_down_all = wdown_ref[...].reshape(D_FF, D_MODEL)
        loss = jnp.zeros((1, 1), F32)
        dg4 = jnp.zeros((1, D_MODEL), F32)
        for k, rows in enumerate(halves):
            mlp = _dot(act_ref[rows, :], w_down_all)
            rstd = _rms(mlp)
            zhat = mlp * rstd
            diff = hv[k] + zhat * g4_ref[...] - t_ref[rows, :]
            loss = loss + jnp.sum(jnp.sum(diff * diff, axis=1, keepdims=True), axis=0, keepdims=True)
            dout = diff * (1.0 / D_MODEL)
            dout_ref[rows, :] = dout
            dg4 = dg4 + _colsum(dout * zhat)
            dmlp_ref[rows, :] = _norm_bwd(dout, g4_ref[...], zhat, rstd).astype(BF16)
        loss_ref[...] += loss
        dg4_ref[...] += dg4

    tile = lambda w: pl.BlockSpec((tb, w), lambda i: (i, 0))
    return _pallas(
        body, name="mlp_loss", grid=(seq // tb,),
        in_specs=[tile(D_MODEL), tile(D_MODEL), tile(D_MODEL), _resident((1, D_MODEL)), _resident((1, D_MODEL)),
                  _resident((1, D_MODEL)), _resident(w_up.shape), _resident(w_down.shape)],
        out_specs=[tile(D_FF), tile(D_MODEL), tile(D_MODEL), tile(D_MODEL),
                   pl.BlockSpec((1, 1), lambda i: (0, 0)), pl.BlockSpec((1, D_MODEL), lambda i: (0, 0))],
        out_shape=[SDS((seq, D_FF), BF16), SDS((seq, D_MODEL), BF16), SDS((seq, D_MODEL), F32),
                   SDS((seq, D_MODEL), BF16), SDS((1, 1), F32), SDS((1, D_MODEL), F32)],
        scratch=[pltpu.VMEM((tb, D_FF), BF16)],
        operands=(x, mix, target, g_post_mix, g_pre_mlp, g_post_mlp, w_up, w_down))


def _mlp_bwd(dmlp, up, x, dout, mix, g_pre_mlp, g_post_mix, w_up, w_down):
    seq = x.shape[0]
    tb = MLP_BWD_TOKEN_TILE

    def body(dmlp_ref, up_ref, x_ref, dout_ref, mix_ref, g2_ref, gpm_ref, wup_ref, wdown_ref,
             dup_ref, dh_ref, dmix_ref, dg2_ref, dgpm_ref):
        @pl.when(pl.program_id(0) == 0)
        def _():
            dg2_ref[...] = jnp.zeros_like(dg2_ref)
            dgpm_ref[...] = jnp.zeros_like(dgpm_ref)

        subs = [slice(k * MLP_BWD_SUB_TILE, (k + 1) * MLP_BWD_SUB_TILE) for k in range(tb // MLP_BWD_SUB_TILE)]
        dhn2 = []
        for rows in subs:
            dmlp_v = dmlp_ref[rows, :]
            acc = None
            for j in range(N_CHIPS):
                cols = slice(1024 * j, 1024 * (j + 1))
                dact = _dot_nt(dmlp_v, _chip_block(wdown_ref, j))
                dup = (dact * (2.0 * up_ref[rows, cols].astype(F32))).astype(BF16)
                dup_ref[rows, cols] = dup
                part = _dot_nt(dup, _chip_block(wup_ref, j))
                acc = part if acc is None else acc + part
            dhn2.append(acc)
        dg2 = jnp.zeros((1, D_MODEL), F32)
        dgpm = jnp.zeros((1, D_MODEL), F32)
        for k, rows in enumerate(subs):
            mix_v = mix_ref[rows, :].astype(F32)
            hv = _residual_mid(x_ref[rows, :], mix_ref[rows, :], gpm_ref[...])
            r2 = _rms(hv)
            hhat = hv * r2
            dg2 = dg2 + _colsum(dhn2[k] * hhat)
            dh = dout_ref[rows, :] + _norm_bwd(dhn2[k], g2_ref[...], hhat, r2)
            dh_ref[rows, :] = dh.astype(BF16)
            rz = _rms(mix_v)
            zhat = mix_v * rz
            dgpm = dgpm + _colsum(dh * zhat)
            dmix_ref[rows, :] = _norm_bwd(dh, gpm_ref[...], zhat, rz).astype(BF16)
        dg2_ref[...] += dg2
        dgpm_ref[...] += dgpm

    tile = lambda w: pl.BlockSpec((tb, w), lambda i: (i, 0))
    vec = pl.BlockSpec((1, D_MODEL), lambda i: (0, 0))
    return _pallas(
        body, name="mlp_bwd", grid=(seq // tb,),
        in_specs=[tile(D_MODEL), tile(D_FF), tile(D_MODEL), tile(D_MODEL), tile(D_MODEL),
                  _resident((1, D_MODEL)), _resident((1, D_MODEL)), _resident(w_up.shape), _resident(w_down.shape)],
        out_specs=[tile(D_FF), tile(D_MODEL), tile(D_MODEL), vec, vec],
        out_shape=[SDS((seq, D_FF), BF16), SDS((seq, D_MODEL), BF16), SDS((seq, D_MODEL), BF16),
                   SDS((1, D_MODEL), F32), SDS((1, D_MODEL), F32)],
        operands=(dmlp, up, x, dout, mix, g_pre_mlp, g_post_mix, w_up, w_down))


def _mix_bwd(dmix, attn, gb, gc, xin, conv_w, g_attn, g_conv, w_out, n_k):
    seq = attn.shape[0]
    tb = seq // (N_CHIPS * n_k)

    def body(first, dmix_ref, a_ref, gb_ref, gc_ref, xin_ref, gch_ref, xinh_ref, cw_ref, ga_ref, gcn_ref, w_ref,
             dattn_ref, dgb_ref, dy_ref, dga_ref, dgcn_ref, dcw_ref):
        @pl.when(first)
        def _():
            dga_ref[...] = jnp.zeros_like(dga_ref)
            dgcn_ref[...] = jnp.zeros_like(dgcn_ref)
            dcw_ref[...] = jnp.zeros_like(dcw_ref)

        dmixed = _dot_nt(dmix_ref[...], w_ref[...].reshape(D_MODEL, D_MODEL))
        a = a_ref[...].astype(F32)
        ra = _rms(a)
        ahat = a * ra
        dan = dmixed[:, 0:Q_WIDTH]
        dga_ref[...] += _colsum(dan * ahat)
        dattn_ref[...] = _norm_bwd(dan, ga_ref[...], ahat, ra).astype(BF16)
        gbv = gb_ref[...].astype(F32)
        u, u1, u2, y = _conv_parts(gc_ref[...], xin_ref[...], gch_ref[...], xinh_ref[...], cw_ref[...], first)
        conv = gbv * y
        rc = _rms(conv)
        chat = conv * rc
        dcn = dmixed[:, Q_WIDTH:]
        dgcn_ref[...] += _colsum(dcn * chat)
        dconv = _norm_bwd(dcn, gcn_ref[...], chat, rc)
        dgb_ref[...] = (dconv * y).astype(BF16)
        dy = dconv * gbv
        dy_ref[...] = dy.astype(BF16)
        dcw_ref[0:1, :] += _colsum(dy * u2)
        dcw_ref[1:2, :] += _colsum(dy * u1)
        dcw_ref[2:3, :] += _colsum(dy * u)

    tile = lambda w: pl.BlockSpec((tb, w), lambda j, k: (j * n_k + k, 0))
    halo = lambda w: pl.BlockSpec((HALO, w), lambda j, k: (jnp.maximum((j * n_k + k) * (tb // HALO) - 1, 0), 0))
    whole = lambda shape: pl.BlockSpec(shape, lambda j, k: (0,) * len(shape))
    return _Rider(
        body,
        in_specs=[tile(D_MODEL), tile(Q_WIDTH), tile(CONV_WIDTH), tile(CONV_WIDTH), tile(CONV_WIDTH),
                  halo(CONV_WIDTH), halo(CONV_WIDTH),
                  _resident((CONV_K, CONV_WIDTH)), _resident((1, Q_WIDTH)), _resident((1, CONV_WIDTH)),
                  _resident(w_out.shape)],
        out_specs=[tile(Q_WIDTH), tile(CONV_WIDTH), tile(CONV_WIDTH),
                   whole((1, Q_WIDTH)), whole((1, CONV_WIDTH)), whole((CONV_K, CONV_WIDTH))],
        out_shape=[SDS((seq, Q_WIDTH), BF16), SDS((seq, CONV_WIDTH), BF16), SDS((seq, CONV_WIDTH), BF16),
                   SDS((1, Q_WIDTH), F32), SDS((1, CONV_WIDTH), F32), SDS((CONV_K, CONV_WIDTH), F32)],
        operands=(dmix, attn, gb, gc, xin, gc, xin, conv_w, g_attn, g_conv, w_out))


def _attention_bwd(q, dattn, attn, kd0, kd1, vd0, vd1, sinks, comm=None):
    seq = q.shape[0]
    nb = ATTN_BWD_BLOCKS

    def body(sink_ref, q_ref, do_ref, o_ref, kd0_ref, kd1_ref, vd0_ref, vd1_ref,
             dq_ref, dk0_ref, dk1_ref, dv0_ref, dv1_ref, dsink_ref):
        @pl.when(pl.program_id(0) == 0)
        def _():
            for r in (dk0_ref, dk1_ref, dv0_ref, dv1_ref, dsink_ref):
                r[...] = jnp.zeros_like(r)

        lane = lax.broadcasted_iota(jnp.int32, (1, 128), 1)
        dsink = jnp.zeros((1, 128), F32)
        for b in range(nb):
            i = pl.program_id(0) * nb + b
            rows = slice(QBLOCK * b, QBLOCK * (b + 1))
            valid = _attn_valid(i)
            for kv_head, (k_ref, v_ref, dk_ref, dv_ref) in enumerate(
                    ((kd0_ref, vd0_ref, dk0_ref, dv0_ref), (kd1_ref, vd1_ref, dk1_ref, dv1_ref))):
                kband, prev, own = _band(k_ref, i)
                vband, _, _ = _band(v_ref, i)
                base = 256 * kv_head
                qm = _stack_heads(q_ref[rows, base:base + 128], q_ref[rows, base + 128:base + 256])
                dom = _stack_heads(do_ref[rows, base:base + 128], do_ref[rows, base + 128:base + 256])
                om = _stack_heads(o_ref[rows, base:base + 128], o_ref[rows, base + 128:base + 256])
                s = jnp.where(valid, _dot_nt(qm, kband), NEG_INF)
                p, e_sink, inv_l = _softmax_with_sink(s, _sink_column(sink_ref, kv_head))
                p = p * inv_l
                delta = jnp.sum(dom.astype(F32) * om.astype(F32), axis=-1, keepdims=True)
                ds = (p * (_dot_nt(dom, vband) - delta)).astype(BF16)
                sink_term = -(e_sink * inv_l) * delta
                for j in range(4):
                    part = jnp.sum(sink_term[QBLOCK * j:QBLOCK * (j + 1)], axis=0, keepdims=True)
                    dsink = dsink + jnp.where(lane == 4 * kv_head + j, part, 0.0)
                pair0, pair1 = _unstack_heads(_dot(ds, kband))
                dq_ref[rows, base:base + 128] = pair0.astype(BF16)
                dq_ref[rows, base + 128:base + 256] = pair1.astype(BF16)
                dkd = _dot_tn(ds, qm)
                dkd = dkd + pltpu.roll(dkd, HEAD_DIM, 1)
                dvd = _dot_tn(p.astype(BF16), dom)
                dvd = dvd + pltpu.roll(dvd, HEAD_DIM, 1)
                dk_ref[pl.ds(prev, QBLOCK), :] += dkd[0:QBLOCK]
                dk_ref[pl.ds(own, QBLOCK), :] += dkd[QBLOCK:]
                dv_ref[pl.ds(prev, QBLOCK), :] += dvd[0:QBLOCK]
                dv_ref[pl.ds(own, QBLOCK), :] += dvd[QBLOCK:]
        dsink_ref[...] += dsink

    blk = pl.BlockSpec((nb * QBLOCK, Q_WIDTH), lambda i: (i, 0))
    full = _resident((seq, 128))
    acc = pl.BlockSpec((seq, 128), lambda i: (0, 0))
    return _pallas(
        body, name="attention_bwd", grid=(seq // (nb * QBLOCK),),
        in_specs=[pl.BlockSpec(memory_space=pltpu.SMEM), blk, blk, blk, full, full, full, full],
        out_specs=[blk, acc, acc, acc, acc, pl.BlockSpec((1, 128), lambda i: (0, 0))],
        out_shape=[SDS((seq, Q_WIDTH), BF16)] + [SDS((seq, 128), F32)] * 4 + [SDS((1, 128), F32)],
        operands=(sinks, q, dattn, attn, kd0, kd1, vd0, vd1), comm=comm)


def _in_proj_bwd(dq, dk0, dk1, dv0, dv1, dgb, dy, gc, xin, conv_w, x, dh, g_pre, w_in_t, rope):
    seq = x.shape[0]
    tb = min(seq, WIDE_TOKEN_TILE)
    n_tiles = seq // tb

    def body(dq_ref, dk0_ref, dk1_ref, dv0_ref, dv1_ref, dgb_ref, dy_ref, dyh_ref, gc_ref, xin_ref, cw_ref,
             x_ref, dh_ref, g_ref, w_ref, c_ref, sa_ref, sb_ref,
             dproj_ref, gx_ref, dg_ref):
        i = pl.program_id(0)

        @pl.when(i == 0)
        def _():
            dg_ref[...] = jnp.zeros_like(dg_ref)

        dy = dy_ref[...].astype(F32)
        ext = jnp.concatenate([dy, jnp.where(i == n_tiles - 1, 0.0, dyh_ref[...].astype(F32))], axis=0)
        dy1 = pltpu.roll(ext, tb + HALO - 1, 0)[0:tb]
        dy2 = pltpu.roll(ext, tb + HALO - 2, 0)[0:tb]
        cw = cw_ref[...]
        du = cw[2:3, :] * dy + cw[1:2, :] * dy1 + cw[0:1, :] * dy2
        scale = 1.0 / math.sqrt(HEAD_DIM)
        base = Q_WIDTH + 2 * KV_WIDTH
        halves = [slice(0, tb // 2), slice(tb // 2, tb)]
        low = _lane_lt64((tb // 2, 128))
        for rows in halves:
            c, sa, sb = _rope_tile(c_ref.at[rows, :], sa_ref, sb_ref)
            for p in range(Q_WIDTH // 128):
                dproj_ref[rows, 128 * p:128 * (p + 1)] = _rope_transposed(
                    dq_ref[rows, 128 * p:128 * (p + 1)].astype(F32) * scale, c, sa, sb).astype(BF16)
            dk = jnp.where(low, dk0_ref[rows, :], dk1_ref[rows, :])
            dproj_ref[rows, Q_WIDTH:Q_WIDTH + KV_WIDTH] = _rope_transposed(dk, c, sa, sb).astype(BF16)
            dproj_ref[rows, Q_WIDTH + KV_WIDTH:base] = jnp.where(low, dv0_ref[rows, :], dv1_ref[rows, :]).astype(BF16)
            dproj_ref[rows, base:base + CONV_WIDTH] = dgb_ref[rows, :]
            dproj_ref[rows, base + CONV_WIDTH:base + 2 * CONV_WIDTH] = (du[rows] * xin_ref[rows, :].astype(F32)).astype(BF16)
            dproj_ref[rows, base + 2 * CONV_WIDTH:] = (du[rows] * gc_ref[rows, :].astype(F32)).astype(BF16)
        w_all = w_ref[...].reshape(IN_COLS, D_MODEL)
        dhn = [_dot(dproj_ref[rows, :], w_all) for rows in halves]
        dg = jnp.zeros((1, D_MODEL), F32)
        for k, rows in enumerate(halves):
            xv = x_ref[rows, :]
            r = _rms(xv)
            xhat = xv * r
            dg = dg + _colsum(dhn[k] * xhat)
            gx_ref[rows, :] = dh_ref[rows, :].astype(F32) + _norm_bwd(dhn[k], g_ref[...], xhat, r)
        dg_ref[...] += dg

    tile = lambda w: pl.BlockSpec((tb, w), lambda i: (i, 0))
    halo_next = pl.BlockSpec((HALO, CONV_WIDTH), lambda i: (jnp.minimum((i + 1) * (tb // HALO), seq // HALO - 1), 0))
    return _pallas(
        body, name="in_proj_bwd", grid=(n_tiles,),
        in_specs=[tile(Q_WIDTH), tile(128), tile(128), tile(128), tile(128), tile(CONV_WIDTH), tile(CONV_WIDTH), halo_next,
                  tile(CONV_WIDTH), tile(CONV_WIDTH), _resident((CONV_K, CONV_WIDTH)),
                  tile(D_MODEL), tile(D_MODEL), _resident((1, D_MODEL)), _resident(w_in_t.shape), *_rope_specs(tb)],
        out_specs=[tile(IN_COLS), tile(D_MODEL), pl.BlockSpec((1, D_MODEL), lambda i: (0, 0))],
        out_shape=[SDS((seq, IN_COLS), BF16), SDS((seq, D_MODEL), F32), SDS((1, D_MODEL), F32)],
        operands=(dq, dk0, dk1, dv0, dv1, dgb, dy, dy, gc, xin, conv_w, x, dh, g_pre, w_in_t, *rope))


def _wgrad_grid(seq, per_chip, h_rows):
    chips_per_step = 1 if per_chip else N_CHIPS
    m = chips_per_step * 2 * h_rows
    bt = min(seq, WGRAD_TOKEN_TILE)
    return chips_per_step, m, bt, seq // bt


def _wgrad(name, a, b, *, per_chip, h_rows, square_a=False, comm=None, rider=None):
    seq = a.shape[0]
    chips_per_step, m, bt, n_k = _wgrad_grid(seq, per_chip, h_rows)
    a_cols = m if per_chip else a.shape[1]
    a_wide = a.shape[1] > a_cols
    b_wide = b.shape[1] > D_MODEL

    def body(a_ref, b_ref, g_ref):
        @pl.when(pl.program_id(1) == 0)
        def _():
            g_ref[...] = jnp.zeros_like(g_ref)

        av = a_ref[...]
        if square_a:
            av = (av.astype(F32) * av.astype(F32)).astype(BF16)
        g_ref[...] += _dot_tn(av, b_ref[...]).reshape(g_ref.shape)

    a_spec = pl.BlockSpec((bt, a_cols), (lambda j, k: (k, j)) if a_wide else (lambda j, k: (k, 0)))
    b_spec = pl.BlockSpec((bt, D_MODEL), (lambda j, k: (k, j)) if b_wide else (lambda j, k: (k, 0)))
    g_spec = pl.BlockSpec((chips_per_step, 2, h_rows, D_MODEL), lambda j, k: (j, 0, 0, 0),
                          pipeline_mode=None if per_chip else pl.Buffered(1))
    return _pallas(
        body, name=name, grid=(N_CHIPS if per_chip else 1, n_k),
        in_specs=[a_spec, b_spec], out_specs=[g_spec], out_shape=[SDS((N_CHIPS, 2, h_rows, D_MODEL), F32)],
        operands=(a, b), comm=comm, rider=rider)


def _adamw_math(w, g, m, v):
    m = ADAM_B1 * m + (1.0 - ADAM_B1) * g
    v = ADAM_B2 * v + (1.0 - ADAM_B2) * (g * g)
    m_hat = m / (1.0 - ADAM_B1 ** ADAM_STEP)
    v_hat = v / (1.0 - ADAM_B2 ** ADAM_STEP)
    delta = -ADAM_LR * (m_hat / (jnp.sqrt(v_hat) + ADAM_EPS) + ADAM_WD * w)
    return delta, m, v


def _adamw_rows(name, reduced, w, m, v, rt):
    per_half = reduced.shape[1] // rt

    def body(r_ref, w_ref, m_ref, v_ref, g_out, d_out, m_out, v_out):
        g = r_ref[0]
        g_out[...] = g
        d_out[...], m_out[...], v_out[...] = _adamw_math(w_ref[...], g, m_ref[...], v_ref[...])

    blk = pl.BlockSpec((rt, D_MODEL), lambda h, r: (h * per_half + r, 0))
    return _pallas(
        body, name=name, grid=(2, per_half),
        in_specs=[pl.BlockSpec((1, rt, D_MODEL), lambda h, r: (h, r, 0)), blk, blk, blk],
        out_specs=[blk, blk, blk, blk], out_shape=[SDS(w.shape, F32)] * 4, operands=(reduced, w, m, v))


def _adamw_small(packed_grads, w, m, v):
    names = SMALL_NAMES
    n = len(names)
    conv_local = w["conv_w"].shape[-1]

    def body(*refs):
        gp = refs[0]
        w_refs, m_refs, v_refs = refs[1:1 + n], refs[1 + n:1 + 2 * n], refs[1 + 2 * n:1 + 3 * n]
        outs = refs[1 + 3 * n:]
        g_out, d_out, m_out, v_out = outs[0:n], outs[n:2 * n], outs[2 * n:3 * n], outs[3 * n:4 * n]
        chip = 2 * lax.axis_index("x") + lax.axis_index("y")

        def step(k, g, index=None):
            pick = (lambda r: r[...]) if index is None else (lambda r: r[index])
            d, new_m, new_v = _adamw_math(pick(w_refs[k]), g, pick(m_refs[k]), pick(v_refs[k]))
            for ref, val in ((g_out[k], g), (d_out[k], d), (m_out[k], new_m), (v_out[k], new_v)):
                if index is None:
                    ref[...] = val
                else:
                    ref[index] = val

        for k, name in enumerate(names):
            if name in SMALL_VECTORS:
                step(k, gp[SMALL_VECTORS.index(name):SMALL_VECTORS.index(name) + 1, :])
            elif name == "attn_group_norm":
                step(k, gp[4:5, 0:Q_WIDTH])
            elif name == "conv_group_norm":
                step(k, gp[4:5, Q_WIDTH:])
            elif name == "attn_sinks":
                step(k, gp[7:8, 0:8])
            else:
                for t in range(CONV_K):
                    row, base = 5 + t // 2, CONV_WIDTH * (t % 2)
                    g = gp[row:row + 1, base:base + conv_local]
                    for j in range(1, CONV_WIDTH // conv_local):
                        g = jnp.where(chip == j, gp[row:row + 1, base + conv_local * j:base + conv_local * (j + 1)], g)
                    step(k, g, index=(0, slice(t, t + 1), slice(None)))

    shapes = [SDS(w[name].shape, F32) for name in names]
    res = pl.pallas_call(
        body, name="adamw_small", in_specs=[VMEM_WHOLE] * (1 + 3 * n), out_specs=[VMEM_WHOLE] * (4 * n),
        out_shape=shapes * 4,
    )(packed_grads, *[w[k] for k in names], *[m[k] for k in names], *[v[k] for k in names])
    return [dict(zip(names, res[i * n:(i + 1) * n])) for i in range(4)]


SMALL_VECTORS = ("pre_mix_norm", "post_mix_norm", "pre_mlp_norm", "post_mlp_norm")
SMALL_NAMES = SMALL_VECTORS + ("attn_group_norm", "conv_group_norm", "conv_w", "attn_sinks")


def _pack_small(p):
    rows = [p[n].reshape(1, D_MODEL) for n in SMALL_VECTORS]
    rows.append(jnp.concatenate([p["attn_group_norm"].reshape(1, -1), p["conv_group_norm"].reshape(1, -1)], axis=1))
    cw = p["conv_w"].reshape(CONV_K, -1)
    rows.append(jnp.pad(cw, ((0, 1), (0, CONV_WIDTH - cw.shape[1]))).reshape(2, D_MODEL))
    last = jnp.concatenate([p["attn_sinks"].reshape(1, 8), p.get("loss_sum", jnp.zeros((1, 1), F32))], axis=1)
    rows.append(jnp.pad(last, ((0, 0), (0, D_MODEL - 9))))
    return jnp.concatenate(rows, axis=0)


WEIGHT_ORDER = ("pre_mix_norm", "w_in", "conv_w", "attn_sinks", "attn_group_norm", "conv_group_norm", "w_out",
                "post_mix_norm", "pre_mlp_norm", "w_up", "w_down", "post_mlp_norm")


def kernel(x, pre_mix_norm, w_in, conv_w, attn_sinks, attn_group_norm, conv_group_norm, w_out, post_mix_norm, pre_mlp_norm, w_up, w_down, post_mlp_norm, loss_target, m_pre_mix_norm, m_w_in, m_conv_w, m_attn_sinks, m_attn_group_norm, m_conv_group_norm, m_w_out, m_post_mix_norm, m_pre_mlp_norm, m_w_up, m_w_down, m_post_mlp_norm, v_pre_mix_norm, v_w_in, v_conv_w, v_attn_sinks, v_attn_group_norm, v_conv_group_norm, v_w_out, v_post_mix_norm, v_pre_mlp_norm, v_w_up, v_w_down, v_post_mlp_norm):
    w = dict(pre_mix_norm=pre_mix_norm, w_in=w_in, conv_w=conv_w, attn_sinks=attn_sinks, attn_group_norm=attn_group_norm,
             conv_group_norm=conv_group_norm, w_out=w_out, post_mix_norm=post_mix_norm, pre_mlp_norm=pre_mlp_norm,
             w_up=w_up, w_down=w_down, post_mlp_norm=post_mlp_norm)
    m = dict(pre_mix_norm=m_pre_mix_norm, w_in=m_w_in, conv_w=m_conv_w, attn_sinks=m_attn_sinks,
             attn_group_norm=m_attn_group_norm, conv_group_norm=m_conv_group_norm, w_out=m_w_out,
             post_mix_norm=m_post_mix_norm, pre_mlp_norm=m_pre_mlp_norm, w_up=m_w_up, w_down=m_w_down,
             post_mlp_norm=m_post_mlp_norm)
    v = dict(pre_mix_norm=v_pre_mix_norm, w_in=v_w_in, conv_w=v_conv_w, attn_sinks=v_attn_sinks,
             attn_group_norm=v_attn_group_norm, conv_group_norm=v_conv_group_norm, w_out=v_w_out,
             post_mix_norm=v_post_mix_norm, pre_mlp_norm=v_pre_mlp_norm, w_up=v_w_up, w_down=v_w_down,
             post_mlp_norm=v_post_mlp_norm)
    core = lax.axis_index("c").astype(jnp.int32).reshape(1)
    xs, target = x[0], loss_target[0]
    rope = _rope_inputs(xs.shape[0])

    hb_up, hb_down, hb_out, hb_in = _cast_halves(core, w_up[0], w_down[0], w_out[0], w_in[0].T)
    conv_pad = jnp.pad(conv_w[0], ((0, 8 - CONV_K), (0, 0)))
    wf_in, conv_all = _gather_whole(hb_in, conv_pad)
    conv_full = conv_all[:, :CONV_K, :].transpose(1, 0, 2).reshape(CONV_K, CONV_WIDTH)

    *proj, wf_up, wf_out = _in_proj(xs, pre_mix_norm, wf_in, rope, comm=_merge(_relay_first(hb_up), _gather_first(hb_out)))
    q, kd0, kd1, vd0, vd1, gb, gc, xin, hn = proj
    attn, wf_up, wf_out, wf_down = _attention_fwd(
        q, kd0, kd1, vd0, vd1, attn_sinks,
        comm=_merge(_relay_second(wf_up), _gather_second(wf_out), _relay_first(hb_down)))
    mix, mixed, wf_up, wf_down = _mix_out(attn, gb, gc, xin, conv_full, attn_group_norm, conv_group_norm, wf_out,
                                          comm=_merge(_relay_third(wf_up), _relay_second(wf_down, then_third=True)))
    up, hn2, dout, dmlp, loss_sum, dg_post_mlp = _mlp_loss(xs, mix, target, post_mix_norm, pre_mlp_norm, post_mlp_norm,
                                                           wf_up, wf_down)

    dup, dh, dmix, dg_pre_mlp, dg_post_mix = _mlp_bwd(dmlp, up, xs, dout, mix, pre_mlp_norm, post_mix_norm, wf_up, wf_down)
    n_k = _wgrad_grid(xs.shape[0], True, H_DOWN)[3]
    g_down, dattn, dgb, dy, dg_attn, dg_conv, dconv_w = _wgrad(
        "wgrad_down", up, dmlp, per_chip=True, h_rows=H_DOWN, square_a=True,
        rider=_mix_bwd(dmix, attn, gb, gc, xin, conv_full, attn_group_norm, conv_group_norm, wf_out, n_k))
    g_up, got_down = _wgrad("wgrad_up", hn2, dup, per_chip=True, h_rows=H_UP, comm=_pair_send(g_down))
    p_down = _pair_sum("pair_sum_down", core, g_down, got_down)
    g_out, got_up = _wgrad("wgrad_out", mixed, dmix, per_chip=False, h_rows=H_OUT, comm=_pair_send(g_up))
    p_up = _pair_sum("pair_sum_up", core, g_up, got_up)
    dq, dk0, dk1, dv0, dv1, dsink, ex_down, ex_up, got_out = _attention_bwd(
        q, dattn, attn, kd0, kd1, vd0, vd1, attn_sinks,
        comm=_merge(_chip_exchange(p_down), _chip_exchange(p_up), _pair_send(g_out)))
    p_out = _pair_sum("pair_sum_out", core, g_out, got_out)
    dproj, grad_x, dg_pre_mix = _in_proj_bwd(dq, dk0, dk1, dv0, dv1, dgb, dy, gc, xin, conv_full, xs, dh, pre_mix_norm,
                                             wf_in, rope)
    g_in, ex_out = _wgrad("wgrad_in", dproj, hn, per_chip=False, h_rows=H_IN, comm=_chip_exchange(p_out))
    small = dict(pre_mix_norm=dg_pre_mix, conv_w=dconv_w, attn_sinks=dsink[:, :8], attn_group_norm=dg_attn,
                 conv_group_norm=dg_conv, post_mix_norm=dg_post_mix, pre_mlp_norm=dg_pre_mlp, post_mlp_norm=dg_post_mlp,
                 loss_sum=loss_sum)
    r_down, r_up, r_out, r_in, small_total = _tail_reduce(g_in, [ex_down, ex_up, ex_out], _pack_small(small))

    out_g, out_d, out_m, out_v = {}, {}, {}, {}
    out_g["w_up"], out_d["w_up"], out_m["w_up"], out_v["w_up"] = _adamw_rows(
        "adamw_up", r_up, w_up[0], m_w_up[0], v_w_up[0], 256)
    out_g["w_down"], out_d["w_down"], out_m["w_down"], out_v["w_down"] = _adamw_rows(
        "adamw_down", r_down, w_down[0], m_w_down[0], v_w_down[0], 256)
    out_g["w_out"], out_d["w_out"], out_m["w_out"], out_v["w_out"] = _adamw_rows(
        "adamw_out", r_out, w_out[0], m_w_out[0], v_w_out[0], H_OUT)
    in_t = _adamw_rows("adamw_in", r_in, w_in[0].T, m_w_in[0].T, v_w_in[0].T, H_IN)
    out_g["w_in"], out_d["w_in"], out_m["w_in"], out_v["w_in"] = [t.T for t in in_t]

    loss = small_total[7, 8] * (0.5 / D_MODEL)
    for out, part in zip((out_g, out_d, out_m, out_v), _adamw_small(small_total, w, m, v)):
        out.update(part)

    def shaped(d):
        return [d[n].reshape(w[n].shape) for n in WEIGHT_ORDER]

    return (loss, grad_x[None], *shaped(out_g), *shaped(out_d), *shaped(out_m), *shaped(out_v))
```

```python
import math
from typing import Callable, NamedTuple

import jax
import jax.numpy as jnp
import numpy as np
from jax import lax
from jax.experimental import pallas as pl
from jax.experimental.pallas import tpu as pltpu

F32 = jnp.float32
BF16 = jnp.bfloat16

D_MODEL = 1024
HEAD_DIM = 64
Q_WIDTH = 512
KV_WIDTH = 128
CONV_WIDTH = 512
CONV_K = 3
D_FF = 4096
IN_COLS = 2304
QBLOCK = 128
ROT_DIM = 16
ROPE_THETA = 500000.0
NORM_EPS = 1e-6
NEG_INF = -1e30
N_CHIPS = 4

ADAM_LR = 0.001
ADAM_B1 = 0.9
ADAM_B2 = 0.999
ADAM_EPS = 1e-08
ADAM_WD = 0.01
ADAM_STEP = 10

H_UP, H_DOWN, H_OUT, H_IN = 512, 512, 128, 288

TOKEN_TILE = 512
WIDE_TOKEN_TILE = 1024
MLP_BWD_TOKEN_TILE = 512
MLP_BWD_SUB_TILE = 256
ATTN_FWD_BLOCKS = 16
ATTN_BWD_BLOCKS = 2
WGRAD_TOKEN_TILE = 2048
VMEM_LIMIT_V7X = 56 * 1024 * 1024

MESH = pl.DeviceIdType.MESH
ANY = pl.BlockSpec(memory_space=pl.ANY)
VMEM_WHOLE = pl.BlockSpec(memory_space=pltpu.VMEM)
SDS = jax.ShapeDtypeStruct


def _resident(shape):
    zeros = (0,) * len(shape)
    return pl.BlockSpec(shape, lambda *_: zeros, pipeline_mode=pl.Buffered(1))


def _rms(v):
    return lax.rsqrt(jnp.mean(v * v, axis=-1, keepdims=True) + NORM_EPS)


def _norm_bwd(dy, gain, vhat, rstd):
    t = dy * gain
    return rstd * (t - vhat * jnp.mean(t * vhat, axis=-1, keepdims=True))


def _colsum(v):
    return jnp.sum(v, axis=0, keepdims=True)


def _dot_nt(a, b):
    return lax.dot_general(a, b, (((1,), (1,)), ((), ())), preferred_element_type=F32)


def _dot_tn(a, b):
    return lax.dot_general(a, b, (((0,), (0,)), ((), ())), preferred_element_type=F32)


def _dot(a, b):
    return jnp.dot(a, b, preferred_element_type=F32)


def _chip_block(w_ref, chip):
    both = w_ref[pl.ds(2 * chip, 2)]
    return both.reshape(2 * both.shape[1], both.shape[2])


def _lane_lt64(shape):
    return lax.broadcasted_iota(jnp.int32, shape, 1) < HEAD_DIM


class _Comm(NamedTuple):
    operands: tuple
    out_shapes: tuple
    aliases: dict
    n_remote: int
    n_local: int
    plan: Callable
    after: Callable = None


def _merge(*comms):
    operands, out_shapes, aliases, parts = [], [], {}, []
    n_remote = n_local = 0
    for cm in comms:
        parts.append((len(operands), len(out_shapes), n_remote, n_local, cm))
        for k, v in cm.aliases.items():
            aliases[len(operands) + k] = len(out_shapes) + v
        operands += cm.operands
        out_shapes += cm.out_shapes
        n_remote += cm.n_remote
        n_local += cm.n_local

    def run(which, ins, outs, send, recv, loc):
        sends, recvs, locs = [], [], []
        for i0, o0, r0, l0, cm in parts:
            stage = getattr(cm, which)
            if stage is not None:
                s, r, l = stage(ins[i0:i0 + len(cm.operands)], outs[o0:o0 + len(cm.out_shapes)],
                                lambda k, r0=r0: send(r0 + k), lambda k, r0=r0: recv(r0 + k), lambda k, l0=l0: loc(l0 + k))
                sends, recvs, locs = sends + s, recvs + r, locs + l
        return sends, recvs, locs

    def plan(*args):
        return run("plan", *args)

    def after(*args):
        return run("after", *args)

    return _Comm(tuple(operands), tuple(out_shapes), aliases, n_remote, n_local, plan,
                 after if any(cm.after is not None for cm in comms) else None)


def _sem_scratch(comm):
    return [pltpu.SemaphoreType.DMA((max(comm.n_remote, 1),)), pltpu.SemaphoreType.DMA((max(comm.n_remote, 1),)),
            pltpu.SemaphoreType.DMA((max(comm.n_local, 1),))]


class _Rider(NamedTuple):
    body: Callable
    in_specs: list
    out_specs: list
    out_shape: list
    operands: tuple


def _pallas(body, *, name, grid, in_specs, out_specs, out_shape, operands, scratch=(), comm=None, rider=None):
    params = pltpu.CompilerParams(dimension_semantics=("arbitrary",) * len(grid), vmem_limit_bytes=VMEM_LIMIT_V7X)
    if rider is not None:
        own_in, own_out, ride_in, ride_out = len(in_specs), len(out_specs), len(rider.in_specs), len(rider.out_specs)
        own_body = body

        def body(*refs):
            o0 = own_in + ride_in
            s0 = o0 + own_out + ride_out
            own_body(*refs[:own_in], *refs[o0:o0 + own_out], *refs[s0:])
            first = None
            for axis in range(len(grid)):
                at_start = pl.program_id(axis) == 0
                first = at_start if first is None else jnp.logical_and(first, at_start)
            rider.body(first, *refs[own_in:o0], *refs[o0 + own_out:s0])

        in_specs, out_specs = list(in_specs) + rider.in_specs, list(out_specs) + rider.out_specs
        out_shape, operands = list(out_shape) + rider.out_shape, tuple(operands) + tuple(rider.operands)
    if comm is None:
        return pl.pallas_call(body, name=name, grid=grid, in_specs=in_specs, out_specs=out_specs, out_shape=out_shape,
                              scratch_shapes=list(scratch), compiler_params=params)(*operands)
    n_in, n_out, n_scr = len(in_specs), len(out_specs), len(scratch)
    c_in, c_out = len(comm.operands), len(comm.out_shapes)

    def with_comm(*refs):
        ins, c_ins = refs[:n_in], refs[n_in:n_in + c_in]
        o0 = n_in + c_in
        outs, c_outs = refs[o0:o0 + n_out], refs[o0 + n_out:o0 + n_out + c_out]
        s0 = o0 + n_out + c_out
        scr = refs[s0:s0 + n_scr]
        send_sems, recv_sems, local_sems = refs[s0 + n_scr:]
        first = last = None
        for axis, size in enumerate(grid):
            at_start, at_end = pl.program_id(axis) == 0, pl.program_id(axis) == size - 1
            first = at_start if first is None else jnp.logical_and(first, at_start)
            last = at_end if last is None else jnp.logical_and(last, at_end)

        def copies():
            return comm.plan(c_ins, c_outs, lambda k: send_sems.at[k], lambda k: recv_sems.at[k],
                             lambda k: local_sems.at[k])

        @pl.when(first)
        def _():
            sends, _, locs = copies()
            for cp in sends + locs:
                cp.start()

        body(*ins, *outs, *scr)

        @pl.when(last)
        def _():
            sends, recvs, locs = copies()
            for cp in recvs:
                cp.wait_recv()
            for cp in sends:
                cp.wait_send()
            for cp in locs:
                cp.wait()
            if comm.after is not None:
                sends, recvs, _ = comm.after(c_ins, c_outs, lambda k: send_sems.at[k], lambda k: recv_sems.at[k],
                                             lambda k: local_sems.at[k])
                for cp in sends:
                    cp.start()
                for cp in recvs:
                    cp.wait_recv()
                for cp in sends:
                    cp.wait_send()

    return pl.pallas_call(
        with_comm, name=name, grid=grid,
        in_specs=list(in_specs) + [ANY] * c_in, out_specs=list(out_specs) + [ANY] * c_out,
        out_shape=list(out_shape) + list(comm.out_shapes),
        scratch_shapes=list(scratch) + _sem_scratch(comm),
        input_output_aliases={n_in + k: n_out + v for k, v in comm.aliases.items()},
        compiler_params=params)(*operands, *comm.operands)


def _place():
    return lax.axis_index("x"), lax.axis_index("y"), lax.axis_index("c")


def _other_chips(x, y):
    return [(1 - x, y), (x, 1 - y), (1 - x, 1 - y)]


def _slot(px, py, pc):
    return 4 * px + 2 * py + pc


def _remote(src, dst, send_sem, recv_sem, to):
    return pltpu.make_async_remote_copy(src_ref=src, dst_ref=dst, send_sem=send_sem, recv_sem=recv_sem,
                                        device_id=to, device_id_type=MESH)


def _gather_first(half_block):
    def plan(ins, outs, send, recv, loc):
        (blk,), (full,) = ins, outs
        x, y, c = _place()
        chips = _other_chips(x, y)
        mine = full.at[_slot(x, y, c)]
        sends = [_remote(blk, mine, send(0), recv(0), (x, y, 1 - c))]
        sends += [_remote(blk, mine, send(1 + j), recv(1 + j), (*chip, c)) for j, chip in enumerate(chips)]
        recvs = [_remote(blk, full.at[_slot(x, y, 1 - c)], send(0), recv(0), (x, y, 1 - c))]
        recvs += [_remote(blk, full.at[_slot(*chip, c)], send(1 + j), recv(1 + j), (*chip, c))
                  for j, chip in enumerate(chips)]
        return sends, recvs, [pltpu.make_async_copy(blk, mine, loc(0))]

    return _Comm((half_block,), (SDS((2 * N_CHIPS,) + half_block.shape, half_block.dtype),), {}, 4, 1, plan)


def _gather_second(partly_gathered):
    def plan(ins, outs, send, recv, loc):
        (src,), (full,) = ins, outs
        x, y, c = _place()
        chips = _other_chips(x, y)
        sends = [_remote(src.at[_slot(*chip, c)], full.at[_slot(*chip, c)], send(j), recv(j), (x, y, 1 - c))
                 for j, chip in enumerate(chips)]
        recvs = [_remote(src.at[_slot(*chip, 1 - c)], full.at[_slot(*chip, 1 - c)], send(j), recv(j), (x, y, 1 - c))
                 for j, chip in enumerate(chips)]
        return sends, recvs, []

    return _Comm((partly_gathered,), (SDS(partly_gathered.shape, partly_gathered.dtype),), {0: 0}, 3, 0, plan)


def _relay_pieces(full, rows, x, y, c):
    half = rows // 2
    upper, lower = pl.ds(0, half), pl.ds(half, half)
    diagonal = full.at[_slot(1 - x, 1 - y, c)]
    return [(full.at[_slot(1 - x, y, c), upper], diagonal.at[upper], (x, 1 - y, c)),
            (full.at[_slot(x, 1 - y, c), lower], diagonal.at[lower], (1 - x, y, c))]


def _relay_first(half_block):
    def plan(ins, outs, send, recv, loc):
        (blk,), (full,) = ins, outs
        x, y, c = _place()
        peers = [(x, y, 1 - c), (1 - x, y, c), (x, 1 - y, c)]
        mine = full.at[_slot(x, y, c)]
        sends = [_remote(blk, mine, send(k), recv(k), peer) for k, peer in enumerate(peers)]
        recvs = [_remote(blk, full.at[_slot(*peer)], send(k), recv(k), peer) for k, peer in enumerate(peers)]
        return sends, recvs, [pltpu.make_async_copy(blk, mine, loc(0))]

    return _Comm((half_block,), (SDS((2 * N_CHIPS,) + half_block.shape, half_block.dtype),), {}, 3, 1, plan)


def _third_leg(src, full, send, recv, k):
    x, y, c = _place()
    sibling = (x, y, 1 - c)
    here, there = _slot(1 - x, 1 - y, c), _slot(1 - x, 1 - y, 1 - c)
    return ([_remote(src.at[here], full.at[here], send(k), recv(k), sibling)],
            [_remote(src.at[there], full.at[there], send(k), recv(k), sibling)], [])


def _relay_second(partly_gathered, then_third=False):
    rows = partly_gathered.shape[1]

    def plan(ins, outs, send, recv, loc):
        (src,), (full,) = ins, outs
        x, y, c = _place()
        sibling = (x, y, 1 - c)
        sends, recvs = [], []
        for k, chip in enumerate([(1 - x, y), (x, 1 - y)]):
            sends.append(_remote(src.at[_slot(*chip, c)], full.at[_slot(*chip, c)], send(k), recv(k), sibling))
            recvs.append(_remote(src.at[_slot(*chip, 1 - c)], full.at[_slot(*chip, 1 - c)], send(k), recv(k), sibling))
        for k, (piece, lands, peer) in enumerate(_relay_pieces(full, rows, x, y, c)):
            sends.append(_remote(piece, piece, send(2 + k), recv(2 + k), peer))
            recvs.append(_remote(lands, lands, send(2 + k), recv(2 + k), peer))
        return sends, recvs, []

    def after(ins, outs, send, recv, loc):
        return _third_leg(ins[0], outs[0], send, recv, 4)

    return _Comm((partly_gathered,), (SDS(partly_gathered.shape, partly_gathered.dtype),), {0: 0}, 5, 0, plan,
                 after if then_third else None)


def _relay_third(mostly_gathered):
    def plan(ins, outs, send, recv, loc):
        return _third_leg(ins[0], outs[0], send, recv, 0)

    return _Comm((mostly_gathered,), (SDS(mostly_gathered.shape, mostly_gathered.dtype),), {0: 0}, 1, 0, plan)


def _gather_whole(half_block, small_block):
    rows = half_block.shape[0]

    def body(blk_ref, small_ref, out_ref, small_out_ref, send_sems, recv_sems, local_sems):
        x, y, c = _place()
        me, sibling = (x, y, c), (x, y, 1 - c)
        neighbours, diagonal = [(1 - x, y), (x, 1 - y)], (1 - x, 1 - y)

        def copy(k, block, to, src=None):
            return _remote(out_ref.at[_slot(*block)] if src is None else src, out_ref.at[_slot(*block)],
                           send_sems.at[k], recv_sems.at[k], to)

        def small_copy(k, chip, to):
            return _remote(small_ref, small_out_ref.at[2 * chip[0] + chip[1]], send_sems.at[8 + k], recv_sems.at[8 + k], to)

        mine = pltpu.make_async_copy(blk_ref, out_ref.at[_slot(*me)], local_sems.at[0])
        mine_small = pltpu.make_async_copy(small_ref, small_out_ref.at[2 * x + y], local_sems.at[1])
        mine.start()
        mine_small.start()
        started = [copy(0, me, sibling, src=blk_ref)]
        started += [copy(1 + k, me, (*chip, c), src=blk_ref) for k, chip in enumerate(neighbours)]
        started += [small_copy(k, (x, y), (*chip, c)) for k, chip in enumerate(neighbours + [diagonal])]
        for cp in started:
            cp.start()
        pieces = _relay_pieces(out_ref, rows, x, y, c)
        for k, chip in enumerate(neighbours):
            copy(1 + k, (*chip, c), me).wait_recv()
            piece, _, peer = pieces[k]
            started += [copy(3 + k, (*chip, c), sibling), _remote(piece, piece, send_sems.at[5 + k], recv_sems.at[5 + k], peer)]
            started[-2].start()
            started[-1].start()
        for k, (_, lands, peer) in enumerate(pieces):
            _remote(lands, lands, send_sems.at[5 + k], recv_sems.at[5 + k], peer).wait_recv()
        started.append(copy(7, (*diagonal, c), sibling))
        started[-1].start()
        copy(0, sibling, me).wait_recv()
        for k, chip in enumerate(neighbours):
            copy(3 + k, (*chip, 1 - c), me).wait_recv()
        copy(7, (*diagonal, 1 - c), me).wait_recv()
        for k, chip in enumerate(neighbours + [diagonal]):
            small_copy(k, chip, me).wait_recv()
        for cp in started:
            cp.wait_send()
        mine.wait()
        mine_small.wait()

    return pl.pallas_call(
        body, name="gather_whole", in_specs=[ANY, ANY], out_specs=[ANY, ANY],
        out_shape=[SDS((2 * N_CHIPS,) + half_block.shape, half_block.dtype),
                   SDS((N_CHIPS,) + small_block.shape, small_block.dtype)],
        scratch_shapes=[pltpu.SemaphoreType.DMA((11,)), pltpu.SemaphoreType.DMA((11,)), pltpu.SemaphoreType.DMA((2,))],
    )(half_block, small_block)


def _pair_send(grads):
    def plan(ins, outs, send, recv, loc):
        (g,), (got,) = ins, outs
        x, y, c = _place()
        copies = [_remote(g.at[j, 1 - c], got.at[j], send(j), recv(j), (x, y, 1 - c)) for j in range(N_CHIPS)]
        return copies, copies, []

    shape = (grads.shape[0],) + grads.shape[2:]
    return _Comm((grads,), (SDS(shape, grads.dtype),), {}, N_CHIPS, 0, plan)


def _chip_exchange(partial):
    def plan(ins, outs, send, recv, loc):
        (p,), (got,) = ins, outs
        x, y, c = _place()
        my_chip = 2 * x + y
        chips = _other_chips(x, y)
        sends = [_remote(p.at[2 * chip[0] + chip[1]], got.at[my_chip], send(j), recv(j), (*chip, c))
                 for j, chip in enumerate(chips)]
        recvs = [_remote(p.at[my_chip], got.at[2 * chip[0] + chip[1]], send(j), recv(j), (*chip, c))
                 for j, chip in enumerate(chips)]
        return sends, recvs, [pltpu.make_async_copy(p.at[my_chip], got.at[my_chip], loc(0))]

    return _Comm((partial,), (SDS(partial.shape, partial.dtype),), {}, 3, 1, plan)


def _pair_sum(name, core, grads, received):
    h = grads.shape[2]

    def body(core_ref, g_ref, r_ref, o_ref):
        o_ref[...] = (g_ref[0] + r_ref[...]).astype(BF16)

    return pl.pallas_call(
        body, name=name,
        grid_spec=pltpu.PrefetchScalarGridSpec(
            num_scalar_prefetch=1, grid=(N_CHIPS,),
            in_specs=[pl.BlockSpec((1, 1, h, D_MODEL), lambda j, core_ref: (j, core_ref[0], 0, 0)),
                      pl.BlockSpec((1, h, D_MODEL), lambda j, core_ref: (j, 0, 0))],
            out_specs=pl.BlockSpec((1, h, D_MODEL), lambda j, core_ref: (j, 0, 0))),
        out_shape=SDS((N_CHIPS, h, D_MODEL), BF16),
        compiler_params=pltpu.CompilerParams(dimension_semantics=("arbitrary",), vmem_limit_bytes=VMEM_LIMIT_V7X),
    )(core, grads, received)


SMALL_ROWS = 8


def _sum_blocks(ref):
    return (ref[0].astype(F32) + ref[1].astype(F32)) + (ref[2].astype(F32) + ref[3].astype(F32))


def _tail_reduce(last_grads, exchanged, small):
    n = len(exchanged)
    h = last_grads.shape[2]

    def body(*refs):
        g_ref, ex, small_ref = refs[0], refs[1:1 + n], refs[1 + n]
        o0 = 2 + n
        out, out_last, small_out = refs[o0:o0 + n], refs[o0 + n], refs[o0 + n + 1]
        s0 = o0 + n + 2
        halves, half_last = refs[s0:s0 + n], refs[s0 + n]
        own, got, part, exch, small_buf = refs[s0 + n + 1:s0 + n + 6]
        pair_send, pair_recv, chip_send, chip_recv, share_send, share_recv, small_send, small_recv, local_sems = refs[s0 + n + 6:]
        x, y, c = _place()
        sibling = (x, y, 1 - c)
        my_chip, me = 2 * x + y, _slot(x, y, c)
        chips = _other_chips(x, y)

        to_sibling = [_remote(g_ref.at[j, 1 - c], got.at[j], pair_send.at[j], pair_recv.at[j], sibling)
                      for j in range(N_CHIPS)]
        load_own = [pltpu.make_async_copy(g_ref.at[j, c], own.at[j], local_sems.at[j]) for j in range(N_CHIPS)]
        for cp in to_sibling + load_own:
            cp.start()

        small_buf[me] = small_ref[...]
        small_copies = []
        for mask in range(1, 8):
            peer = (x ^ (mask >> 2), y ^ ((mask >> 1) & 1), c ^ (mask & 1))
            small_copies.append(_remote(small_ref, small_buf.at[me], small_send.at[mask - 1], small_recv.at[mask - 1], peer))
        for cp in small_copies:
            cp.start()

        def share(k, half_ref, out_ref):
            keep = pltpu.make_async_copy(half_ref, out_ref.at[c], local_sems.at[N_CHIPS + k])
            give = _remote(half_ref, out_ref.at[c], share_send.at[k], share_recv.at[k], sibling)
            take = _remote(half_ref, out_ref.at[1 - c], share_send.at[k], share_recv.at[k], sibling)
            keep.start()
            give.start()
            return keep, give, take

        shares = []
        for k in range(n):
            halves[k][...] = _sum_blocks(ex[k])
            shares.append(share(k, halves[k], out[k]))

        def pair_sum(block):
            _remote(g_ref.at[block, 1 - c], got.at[block], pair_send.at[block], pair_recv.at[block], sibling).wait_recv()
            pltpu.make_async_copy(g_ref.at[block, c], own.at[block], local_sems.at[block]).wait()
            part[block] = (own[block] + got[block]).astype(BF16)

        to_chips = []
        for j, chip in enumerate(chips):
            block = 2 * chip[0] + chip[1]
            pair_sum(block)
            to_chips.append(_remote(part.at[block], exch.at[my_chip], chip_send.at[j], chip_recv.at[j], (*chip, c)))
            to_chips[-1].start()
        pair_sum(my_chip)
        exch[my_chip] = part[my_chip]
        from_chips = [_remote(part.at[my_chip], exch.at[2 * chip[0] + chip[1]], chip_send.at[j], chip_recv.at[j], (*chip, c))
                      for j, chip in enumerate(chips)]

        for cp in small_copies:
            cp.wait_recv()
        total = small_buf[0]
        for d in range(1, 8):
            total = total + small_buf[d]
        small_out[...] = total

        for cp in from_chips:
            cp.wait_recv()
        half_last[...] = _sum_blocks(exch)
        shares.append(share(n, half_last, out_last))

        for keep, give, take in shares:
            take.wait_recv()
            give.wait_send()
            keep.wait()
        for cp in to_sibling + to_chips + small_copies:
            cp.wait_send()

    blocks = (N_CHIPS, h, D_MODEL)
    return pl.pallas_call(
        body, name="tail_reduce",
        in_specs=[ANY] + [VMEM_WHOLE] * (n + 1), out_specs=[ANY] * (n + 1) + [VMEM_WHOLE],
        out_shape=[SDS((2,) + e.shape[1:], F32) for e in exchanged] + [SDS((2, h, D_MODEL), F32), SDS(small.shape, F32)],
        scratch_shapes=[pltpu.VMEM(e.shape[1:], F32) for e in exchanged] + [pltpu.VMEM((h, D_MODEL), F32)]
                       + [pltpu.VMEM(blocks, F32), pltpu.VMEM(blocks, F32), pltpu.VMEM(blocks, BF16), pltpu.VMEM(blocks, BF16),
                          pltpu.VMEM((8,) + small.shape, F32)]
                       + [pltpu.SemaphoreType.DMA((N_CHIPS,)), pltpu.SemaphoreType.DMA((N_CHIPS,)),
                          pltpu.SemaphoreType.DMA((3,)), pltpu.SemaphoreType.DMA((3,)),
                          pltpu.SemaphoreType.DMA((n + 1,)), pltpu.SemaphoreType.DMA((n + 1,)),
                          pltpu.SemaphoreType.DMA((7,)), pltpu.SemaphoreType.DMA((7,)),
                          pltpu.SemaphoreType.DMA((N_CHIPS + n + 1,))],
        compiler_params=pltpu.CompilerParams(vmem_limit_bytes=VMEM_LIMIT_V7X),
    )(last_grads, *exchanged, small)


def _rope_expansion():
    half = ROT_DIM // 2
    expand = np.zeros((2 * half, 3 * 128), np.float32)
    const = np.zeros((1, 3 * 128), np.float32)
    for lane in range(128):
        d = lane % HEAD_DIM
        if d < ROT_DIM:
            expand[d % half, lane] = 1.0
        else:
            const[0, lane] = 1.0
        if d < half:
            expand[half + d, 128 + lane] = -1.0
        elif d < ROT_DIM:
            expand[half + d - half, 256 + lane] = 1.0
    return expand, const


ROPE_PIECES = 3 * ROT_DIM


def _rope_inputs(seq):
    pos = jnp.arange(seq, dtype=F32)
    inv_freq = ROPE_THETA ** (-jnp.arange(0, ROT_DIM, 2, dtype=F32) / ROT_DIM)
    ang = pos[:, None] * inv_freq[None, :]
    cs = jnp.concatenate([jnp.cos(ang), jnp.sin(ang)], axis=1)
    hi = lax.reduce_precision(cs, 8, 7)
    mid = lax.reduce_precision(cs - hi, 8, 7)
    low = cs - hi - mid
    expand, const = _rope_expansion()
    pieces = jnp.concatenate([hi, mid, low], axis=1).astype(BF16)
    return pieces, jnp.asarray(np.concatenate([expand] * 3, axis=0), BF16), jnp.asarray(const)


def _rope_specs(tb):
    return [pl.BlockSpec((tb, ROPE_PIECES), lambda i: (i, 0)), _resident((ROPE_PIECES, 3 * 128)), _resident((1, 3 * 128))]


def _rope_tile(pieces_ref, expand_ref, const_ref):
    tables = _dot(pieces_ref[...], expand_ref[...]) + const_ref[...]
    return tables[:, 0:128], tables[:, 128:256], tables[:, 256:384]


def _rope(t, c, sa, sb):
    half = ROT_DIM // 2
    return t * c + pltpu.roll(t, 128 - half, 1) * sa + pltpu.roll(t, half, 1) * sb


def _rope_transposed(dt, c, sa, sb):
    half = ROT_DIM // 2
    return dt * c + pltpu.roll(dt * sa, half, 1) + pltpu.roll(dt * sb, 128 - half, 1)


def _cast_halves(core, w_up, w_down, w_out, w_in_t):
    def body(core_ref, up_ref, down_ref, out_ref, in_ref, up_o, down_o, out_o, in_o):
        up_o[...] = up_ref[...].astype(BF16)
        down_o[...] = down_ref[...].astype(BF16)
        out_o[...] = out_ref[...].astype(BF16)
        in_o[...] = in_ref[...].astype(BF16)

    half = lambda rows: pl.BlockSpec((rows, D_MODEL), lambda i, core_ref: (core_ref[0], 0))
    whole = lambda rows: pl.BlockSpec((rows, D_MODEL), lambda i, core_ref: (0, 0))
    rows = (H_UP, H_DOWN, H_OUT, H_IN)
    return pl.pallas_call(
        body, name="cast_halves",
        grid_spec=pltpu.PrefetchScalarGridSpec(
            num_scalar_prefetch=1, grid=(1,), in_specs=[half(r) for r in rows], out_specs=[whole(r) for r in rows]),
        out_shape=[SDS((r, D_MODEL), BF16) for r in rows],
        compiler_params=pltpu.CompilerParams(dimension_semantics=("arbitrary",), vmem_limit_bytes=VMEM_LIMIT_V7X),
    )(core, w_up, w_down, w_out, w_in_t)


def _in_proj(x, g_pre, w_in_t, rope, comm=None):
    seq = x.shape[0]
    tb = min(seq, WIDE_TOKEN_TILE)

    def body(x_ref, g_ref, w_ref, c_ref, sa_ref, sb_ref,
             q_ref, kd0_ref, kd1_ref, vd0_ref, vd1_ref, gb_ref, gc_ref, xin_ref, hn_ref):
        xv = x_ref[...]
        hn = (xv * _rms(xv) * g_ref[...]).astype(BF16)
        hn_ref[...] = hn
        proj = _dot_nt(hn, w_ref[...].reshape(IN_COLS, D_MODEL))
        c, sa, sb = _rope_tile(c_ref, sa_ref, sb_ref)
        scale = 1.0 / math.sqrt(HEAD_DIM)
        for p in range(Q_WIDTH // 128):
            q_ref[:, 128 * p:128 * (p + 1)] = (_rope(proj[:, 128 * p:128 * (p + 1)], c, sa, sb) * scale).astype(BF16)
        k = _rope(proj[:, Q_WIDTH:Q_WIDTH + KV_WIDTH], c, sa, sb)
        v = proj[:, Q_WIDTH + KV_WIDTH:Q_WIDTH + 2 * KV_WIDTH]
        low = _lane_lt64(k.shape)
        k_sw, v_sw = pltpu.roll(k, HEAD_DIM, 1), pltpu.roll(v, HEAD_DIM, 1)
        kd0_ref[...] = jnp.where(low, k, k_sw).astype(BF16)
        kd1_ref[...] = jnp.where(low, k_sw, k).astype(BF16)
        vd0_ref[...] = jnp.where(low, v, v_sw).astype(BF16)
        vd1_ref[...] = jnp.where(low, v_sw, v).astype(BF16)
        base = Q_WIDTH + 2 * KV_WIDTH
        gb_ref[...] = proj[:, base:base + CONV_WIDTH].astype(BF16)
        gc_ref[...] = proj[:, base + CONV_WIDTH:base + 2 * CONV_WIDTH].astype(BF16)
        xin_ref[...] = proj[:, base + 2 * CONV_WIDTH:base + 3 * CONV_WIDTH].astype(BF16)

    tile = lambda w: pl.BlockSpec((tb, w), lambda i: (i, 0))
    return _pallas(
        body, name="in_proj", grid=(seq // tb,),
        in_specs=[tile(D_MODEL), _resident((1, D_MODEL)), _resident(w_in_t.shape), *_rope_specs(tb)],
        out_specs=[tile(Q_WIDTH), tile(128), tile(128), tile(128), tile(128),
                   tile(CONV_WIDTH), tile(CONV_WIDTH), tile(CONV_WIDTH), tile(D_MODEL)],
        out_shape=[SDS((seq, Q_WIDTH), BF16)] + [SDS((seq, 128), BF16)] * 4
                  + [SDS((seq, CONV_WIDTH), BF16)] * 3 + [SDS((seq, D_MODEL), BF16)],
        operands=(x, g_pre, w_in_t, *rope), comm=comm)


def _attn_valid(i):
    shape = (4 * QBLOCK, 2 * QBLOCK)
    row = lax.broadcasted_iota(jnp.int32, shape, 0)
    col = lax.broadcasted_iota(jnp.int32, shape, 1)
    qi = row & (QBLOCK - 1)
    return (col > qi) & (col <= qi + QBLOCK) & ((col >= QBLOCK) | (i > 0))


def _stack_heads(pair0, pair1):
    low = _lane_lt64(pair0.shape)
    zero = jnp.zeros_like(pair0)
    return jnp.concatenate([jnp.where(low, pair0, zero), jnp.where(low, zero, pair0),
                            jnp.where(low, pair1, zero), jnp.where(low, zero, pair1)], axis=0)


def _unstack_heads(stacked):
    low = _lane_lt64((QBLOCK, 128))
    pair0 = jnp.where(low, stacked[0:QBLOCK], stacked[QBLOCK:2 * QBLOCK])
    pair1 = jnp.where(low, stacked[2 * QBLOCK:3 * QBLOCK], stacked[3 * QBLOCK:4 * QBLOCK])
    return pair0, pair1


def _sink_column(sink_ref, kv_head):
    row = lax.broadcasted_iota(jnp.int32, (4 * QBLOCK, 1), 0)
    s = [sink_ref[0, 4 * kv_head + j] for j in range(4)]
    return jnp.where(row < QBLOCK, s[0], jnp.where(row < 2 * QBLOCK, s[1], jnp.where(row < 3 * QBLOCK, s[2], s[3])))


def _band(ref, i):
    prev = pl.multiple_of(jnp.maximum(i - 1, 0) * QBLOCK, QBLOCK)
    own = pl.multiple_of(i * QBLOCK, QBLOCK)
    return jnp.concatenate([ref[pl.ds(prev, QBLOCK), :], ref[pl.ds(own, QBLOCK), :]], axis=0), prev, own


def _softmax_with_sink(s, sink_col):
    m = jnp.maximum(jnp.max(s, axis=-1, keepdims=True), sink_col)
    p = jnp.exp(s - m)
    e_sink = jnp.exp(sink_col - m)
    inv_l = 1.0 / (jnp.sum(p, axis=-1, keepdims=True) + e_sink)
    return p, e_sink, inv_l


def _attention_fwd(q, kd0, kd1, vd0, vd1, sinks, comm=None):
    seq = q.shape[0]

    nb = ATTN_FWD_BLOCKS

    def body(sink_ref, q_ref, kd0_ref, kd1_ref, vd0_ref, vd1_ref, o_ref):
        for b in range(nb):
            i = pl.program_id(0) * nb + b
            rows = slice(QBLOCK * b, QBLOCK * (b + 1))
            valid = _attn_valid(i)
            for kv_head, (k_ref, v_ref) in enumerate(((kd0_ref, vd0_ref), (kd1_ref, vd1_ref))):
                kband, _, _ = _band(k_ref, i)
                vband, _, _ = _band(v_ref, i)
                base = 256 * kv_head
                qm = _stack_heads(q_ref[rows, base:base + 128], q_ref[rows, base + 128:base + 256])
                s = jnp.where(valid, _dot_nt(qm, kband), NEG_INF)
                p, _, inv_l = _softmax_with_sink(s, _sink_column(sink_ref, kv_head))
                o = _dot(p.astype(BF16), vband) * inv_l
                pair0, pair1 = _unstack_heads(o)
                o_ref[rows, base:base + 128] = pair0.astype(BF16)
                o_ref[rows, base + 128:base + 256] = pair1.astype(BF16)

    blk = pl.BlockSpec((nb * QBLOCK, Q_WIDTH), lambda i: (i, 0))
    full = _resident((seq, 128))
    return _pallas(
        body, name="attention_fwd", grid=(seq // (nb * QBLOCK),),
        in_specs=[pl.BlockSpec(memory_space=pltpu.SMEM), blk, full, full, full, full],
        out_specs=[blk], out_shape=[SDS((seq, Q_WIDTH), BF16)],
        operands=(sinks, q, kd0, kd1, vd0, vd1), comm=comm)


HALO = 16


def _conv_parts(gc, xin, gc_halo, xin_halo, conv_w, first):
    tb = gc.shape[0]
    u = gc.astype(F32) * xin.astype(F32)
    u_halo = jnp.where(first, 0.0, gc_halo.astype(F32) * xin_halo.astype(F32))
    ext = jnp.concatenate([u_halo, u], axis=0)
    u1 = pltpu.roll(ext, 1, 0)[HALO:HALO + tb]
    u2 = pltpu.roll(ext, 2, 0)[HALO:HALO + tb]
    y = conv_w[0:1, :] * u2 + conv_w[1:2, :] * u1 + conv_w[2:3, :] * u
    return u, u1, u2, y


def _halo_prev(tb, w):
    return pl.BlockSpec((HALO, w), lambda i: (jnp.maximum(i * (tb // HALO) - 1, 0), 0))


def _residual_mid(x, mix, g_post_mix):
    mix_f = mix.astype(F32)
    return x + mix_f * _rms(mix_f) * g_post_mix


def _mix_out(attn, gb, gc, xin, conv_w, g_attn, g_conv, w_out, comm=None):
    seq = attn.shape[0]
    tb = min(seq, WIDE_TOKEN_TILE)

    def body(a_ref, gb_ref, gc_ref, xin_ref, gch_ref, xinh_ref, cw_ref, ga_ref, gcn_ref, w_ref, mix_ref, mixed_ref):
        first = pl.program_id(0) == 0
        _, _, _, y = _conv_parts(gc_ref[...], xin_ref[...], gch_ref[...], xinh_ref[...], cw_ref[...], first)
        conv = gb_ref[...].astype(F32) * y
        a = a_ref[...].astype(F32)
        mixed_ref[:, 0:Q_WIDTH] = (a * _rms(a) * ga_ref[...]).astype(BF16)
        mixed_ref[:, Q_WIDTH:] = (conv * _rms(conv) * gcn_ref[...]).astype(BF16)
        mix_ref[...] = _dot(mixed_ref[...], w_ref[...].reshape(D_MODEL, D_MODEL)).astype(BF16)

    tile = lambda w: pl.BlockSpec((tb, w), lambda i: (i, 0))
    return _pallas(
        body, name="mix_out", grid=(seq // tb,),
        in_specs=[tile(Q_WIDTH), tile(CONV_WIDTH), tile(CONV_WIDTH), tile(CONV_WIDTH),
                  _halo_prev(tb, CONV_WIDTH), _halo_prev(tb, CONV_WIDTH),
                  _resident((CONV_K, CONV_WIDTH)), _resident((1, Q_WIDTH)), _resident((1, CONV_WIDTH)),
                  _resident(w_out.shape)],
        out_specs=[tile(D_MODEL), tile(D_MODEL)],
        out_shape=[SDS((seq, D_MODEL), BF16), SDS((seq, D_MODEL), BF16)],
        operands=(attn, gb, gc, xin, gc, xin, conv_w, g_attn, g_conv, w_out), comm=comm)


def _mlp_loss(x, mix, target, g_post_mix, g_pre_mlp, g_post_mlp, w_up, w_down):
    seq = x.shape[0]
    tb = TOKEN_TILE

    def body(x_ref, mix_ref, t_ref, gpm_ref, g2_ref, g4_ref, wup_ref, wdown_ref,
             up_ref, hn2_ref, dout_ref, dmlp_ref, loss_ref, dg4_ref, act_ref):
        @pl.when(pl.program_id(0) == 0)
        def _():
            loss_ref[...] = jnp.zeros_like(loss_ref)
            dg4_ref[...] = jnp.zeros_like(dg4_ref)

        halves = [slice(0, tb // 2), slice(tb // 2, tb)]
        hv, hn2 = [], []
        for rows in halves:
            hv.append(_residual_mid(x_ref[rows, :], mix_ref[rows, :], gpm_ref[...]))
            hn2.append((hv[-1] * _rms(hv[-1]) * g2_ref[...]).astype(BF16))
            hn2_ref[rows, :] = hn2[-1]
        for k, rows in enumerate(halves):
            for j in range(N_CHIPS):
                up = _dot(hn2[k], _chip_block(wup_ref, j))
                up = jnp.maximum(up, 0.0)
                up_ref[rows, 1024 * j:1024 * (j + 1)] = up.astype(BF16)
                act_ref[rows, 1024 * j:1024 * (j + 1)] = (up * up).astype(BF16)
        w_down_all = wdown_ref[...].reshape(D_FF, D_MODEL)
        loss = jnp.zeros((1, 1), F32)
        dg4 = jnp.zeros((1, D_MODEL), F32)
        for k, rows in enumerate(halves):
            mlp = _dot(act_ref[rows, :], w_down_all)
            rstd = _rms(mlp)
            zhat = mlp * rstd
            diff = hv[k] + zhat * g4_ref[...] - t_ref[rows, :]
            loss = loss + jnp.sum(jnp.sum(diff * diff, axis=1, keepdims=True), axis=0, keepdims=True)
            dout = diff * (1.0 / D_MODEL)
            dout_ref[rows, :] = dout
            dg4 = dg4 + _colsum(dout * zhat)
            dmlp_ref[rows, :] = _norm_bwd(dout, g4_ref[...], zhat, rstd).astype(BF16)
        loss_ref[...] += loss
        dg4_ref[...] += dg4

    tile = lambda w: pl.BlockSpec((tb, w), lambda i: (i, 0))
    return _pallas(
        body, name="mlp_loss", grid=(seq // tb,),
        in_specs=[tile(D_MODEL), tile(D_MODEL), tile(D_MODEL), _resident((1, D_MODEL)), _resident((1, D_MODEL)),
                  _resident((1, D_MODEL)), _resident(w_up.shape), _resident(w_down.shape)],
        out_specs=[tile(D_FF), tile(D_MODEL), tile(D_MODEL), tile(D_MODEL),
                   pl.BlockSpec((1, 1), lambda i: (0, 0)), pl.BlockSpec((1, D_MODEL), lambda i: (0, 0))],
        out_shape=[SDS((seq, D_FF), BF16), SDS((seq, D_MODEL), BF16), SDS((seq, D_MODEL), F32),
                   SDS((seq, D_MODEL), BF16), SDS((1, 1), F32), SDS((1, D_MODEL), F32)],
        scratch=[pltpu.VMEM((tb, D_FF), BF16)],
        operands=(x, mix, target, g_post_mix, g_pre_mlp, g_post_mlp, w_up, w_down))


def _mlp_bwd(dmlp, up, x, dout, mix, g_pre_mlp, g_post_mix, w_up, w_down):
    seq = x.shape[0]
    tb = MLP_BWD_TOKEN_TILE

    def body(dmlp_ref, up_ref, x_ref, dout_ref, mix_ref, g2_ref, gpm_ref, wup_ref, wdown_ref,
             dup_ref, dh_ref, dmix_ref, dg2_ref, dgpm_ref):
        @pl.when(pl.program_id(0) == 0)
        def _():
            dg2_ref[...] = jnp.zeros_like(dg2_ref)
            dgpm_ref[...] = jnp.zeros_like(dgpm_ref)

        subs = [slice(k * MLP_BWD_SUB_TILE, (k + 1) * MLP_BWD_SUB_TILE) for k in range(tb // MLP_BWD_SUB_TILE)]
        dhn2 = []
        for rows in subs:
            dmlp_v = dmlp_ref[rows, :]
            acc = None
            for j in range(N_CHIPS):
                cols = slice(1024 * j, 1024 * (j + 1))
                dact = _dot_nt(dmlp_v, _chip_block(wdown_ref, j))
                dup = (dact * (2.0 * up_ref[rows, cols].astype(F32))).astype(BF16)
                dup_ref[rows, cols] = dup
                part = _dot_nt(dup, _chip_block(wup_ref, j))
                acc = part if acc is None else acc + part
            dhn2.append(acc)
        dg2 = jnp.zeros((1, D_MODEL), F32)
        dgpm = jnp.zeros((1, D_MODEL), F32)
        for k, rows in enumerate(subs):
            mix_v = mix_ref[rows, :].astype(F32)
            hv = _residual_mid(x_ref[rows, :], mix_ref[rows, :], gpm_ref[...])
            r2 = _rms(hv)
            hhat = hv * r2
            dg2 = dg2 + _colsum(dhn2[k] * hhat)
            dh = dout_ref[rows, :] + _norm_bwd(dhn2[k], g2_ref[...], hhat, r2)
            dh_ref[rows, :] = dh.astype(BF16)
            rz = _rms(mix_v)
            zhat = mix_v * rz
            dgpm = dgpm + _colsum(dh * zhat)
            dmix_ref[rows, :] = _norm_bwd(dh, gpm_ref[...], zhat, rz).astype(BF16)
        dg2_ref[...] += dg2
        dgpm_ref[...] += dgpm

    tile = lambda w: pl.BlockSpec((tb, w), lambda i: (i, 0))
    vec = pl.BlockSpec((1, D_MODEL), lambda i: (0, 0))
    return _pallas(
        body, name="mlp_bwd", grid=(seq // tb,),
        in_specs=[tile(D_MODEL), tile(D_FF), tile(D_MODEL), tile(D_MODEL), tile(D_MODEL),
                  _resident((1, D_MODEL)), _resident((1, D_MODEL)), _resident(w_up.shape), _resident(w_down.shape)],
        out_specs=[tile(D_FF), tile(D_MODEL), tile(D_MODEL), vec, vec],
        out_shape=[SDS((seq, D_FF), BF16), SDS((seq, D_MODEL), BF16), SDS((seq, D_MODEL), BF16),
                   SDS((1, D_MODEL), F32), SDS((1, D_MODEL), F32)],
        operands=(dmlp, up, x, dout, mix, g_pre_mlp, g_post_mix, w_up, w_down))


def _mix_bwd(dmix, attn, gb, gc, xin, conv_w, g_attn, g_conv, w_out, n_k):
    seq = attn.shape[0]
    tb = seq // (N_CHIPS * n_k)

    def body(first, dmix_ref, a_ref, gb_ref, gc_ref, xin_ref, gch_ref, xinh_ref, cw_ref, ga_ref, gcn_ref, w_ref,
             dattn_ref, dgb_ref, dy_ref, dga_ref, dgcn_ref, dcw_ref):
        @pl.when(first)
        def _():
            dga_ref[...] = jnp.zeros_like(dga_ref)
            dgcn_ref[...] = jnp.zeros_like(dgcn_ref)
            dcw_ref[...] = jnp.zeros_like(dcw_ref)

        dmixed = _dot_nt(dmix_ref[...], w_ref[...].reshape(D_MODEL, D_MODEL))
        a = a_ref[...].astype(F32)
        ra = _rms(a)
        ahat = a * ra
        dan = dmixed[:, 0:Q_WIDTH]
        dga_ref[...] += _colsum(dan * ahat)
        dattn_ref[...] = _norm_bwd(dan, ga_ref[...], ahat, ra).astype(BF16)
        gbv = gb_ref[...].astype(F32)
        u, u1, u2, y = _conv_parts(gc_ref[...], xin_ref[...], gch_ref[...], xinh_ref[...], cw_ref[...], first)
        conv = gbv * y
        rc = _rms(conv)
        chat = conv * rc
        dcn = dmixed[:, Q_WIDTH:]
        dgcn_ref[...] += _colsum(dcn * chat)
        dconv = _norm_bwd(dcn, gcn_ref[...], chat, rc)
        dgb_ref[...] = (dconv * y).astype(BF16)
        dy = dconv * gbv
        dy_ref[...] = dy.astype(BF16)
        dcw_ref[0:1, :] += _colsum(dy * u2)
        dcw_ref[1:2, :] += _colsum(dy * u1)
        dcw_ref[2:3, :] += _colsum(dy * u)

    tile = lambda w: pl.BlockSpec((tb, w), lambda j, k: (j * n_k + k, 0))
    halo = lambda w: pl.BlockSpec((HALO, w), lambda j, k: (jnp.maximum((j * n_k + k) * (tb // HALO) - 1, 0), 0))
    whole = lambda shape: pl.BlockSpec(shape, lambda j, k: (0,) * len(shape))
    return _Rider(
        body,
        in_specs=[tile(D_MODEL), tile(Q_WIDTH), tile(CONV_WIDTH), tile(CONV_WIDTH), tile(CONV_WIDTH),
                  halo(CONV_WIDTH), halo(CONV_WIDTH),
                  _resident((CONV_K, CONV_WIDTH)), _resident((1, Q_WIDTH)), _resident((1, CONV_WIDTH)),
                  _resident(w_out.shape)],
        out_specs=[tile(Q_WIDTH), tile(CONV_WIDTH), tile(CONV_WIDTH),
                   whole((1, Q_WIDTH)), whole((1, CONV_WIDTH)), whole((CONV_K, CONV_WIDTH))],
        out_shape=[SDS((seq, Q_WIDTH), BF16), SDS((seq, CONV_WIDTH), BF16), SDS((seq, CONV_WIDTH), BF16),
                   SDS((1, Q_WIDTH), F32), SDS((1, CONV_WIDTH), F32), SDS((CONV_K, CONV_WIDTH), F32)],
        operands=(dmix, attn, gb, gc, xin, gc, xin, conv_w, g_attn, g_conv, w_out))


def _attention_bwd(q, dattn, attn, kd0, kd1, vd0, vd1, sinks, comm=None):
    seq = q.shape[0]
    nb = ATTN_BWD_BLOCKS

    def body(sink_ref, q_ref, do_ref, o_ref, kd0_ref, kd1_ref, vd0_ref, vd1_ref,
             dq_ref, dk0_ref, dk1_ref, dv0_ref, dv1_ref, dsink_ref):
        @pl.when(pl.program_id(0) == 0)
        def _():
            for r in (dk0_ref, dk1_ref, dv0_ref, dv1_ref, dsink_ref):
                r[...] = jnp.zeros_like(r)

        lane = lax.broadcasted_iota(jnp.int32, (1, 128), 1)
        dsink = jnp.zeros((1, 128), F32)
        for b in range(nb):
            i = pl.program_id(0) * nb + b
            rows = slice(QBLOCK * b, QBLOCK * (b + 1))
            valid = _attn_valid(i)
            for kv_head, (k_ref, v_ref, dk_ref, dv_ref) in enumerate(
                    ((kd0_ref, vd0_ref, dk0_ref, dv0_ref), (kd1_ref, vd1_ref, dk1_ref, dv1_ref))):
                kband, prev, own = _band(k_ref, i)
                vband, _, _ = _band(v_ref, i)
                base = 256 * kv_head
                qm = _stack_heads(q_ref[rows, base:base + 128], q_ref[rows, base + 128:base + 256])
                dom = _stack_heads(do_ref[rows, base:base + 128], do_ref[rows, base + 128:base + 256])
                om = _stack_heads(o_ref[rows, base:base + 128], o_ref[rows, base + 128:base + 256])
                s = jnp.where(valid, _dot_nt(qm, kband), NEG_INF)
                p, e_sink, inv_l = _softmax_with_sink(s, _sink_column(sink_ref, kv_head))
                p = p * inv_l
                delta = jnp.sum(dom.astype(F32) * om.astype(F32), axis=-1, keepdims=True)
                ds = (p * (_dot_nt(dom, vband) - delta)).astype(BF16)
                sink_term = -(e_sink * inv_l) * delta
                for j in range(4):
                    part = jnp.sum(sink_term[QBLOCK * j:QBLOCK * (j + 1)], axis=0, keepdims=True)
                    dsink = dsink + jnp.where(lane == 4 * kv_head + j, part, 0.0)
                pair0, pair1 = _unstack_heads(_dot(ds, kband))
                dq_ref[rows, base:base + 128] = pair0.astype(BF16)
                dq_ref[rows, base + 128:base + 256] = pair1.astype(BF16)
                dkd = _dot_tn(ds, qm)
                dkd = dkd + pltpu.roll(dkd, HEAD_DIM, 1)
                dvd = _dot_tn(p.astype(BF16), dom)
                dvd = dvd + pltpu.roll(dvd, HEAD_DIM, 1)
                dk_ref[pl.ds(prev, QBLOCK), :] += dkd[0:QBLOCK]
                dk_ref[pl.ds(own, QBLOCK), :] += dkd[QBLOCK:]
                dv_ref[pl.ds(prev, QBLOCK), :] += dvd[0:QBLOCK]
                dv_ref[pl.ds(own, QBLOCK), :] += dvd[QBLOCK:]
        dsink_ref[...] += dsink

    blk = pl.BlockSpec((nb * QBLOCK, Q_WIDTH), lambda i: (i, 0))
    full = _resident((seq, 128))
    acc = pl.BlockSpec((seq, 128), lambda i: (0, 0))
    return _pallas(
        body, name="attention_bwd", grid=(seq // (nb * QBLOCK),),
        in_specs=[pl.BlockSpec(memory_space=pltpu.SMEM), blk, blk, blk, full, full, full, full],
        out_specs=[blk, acc, acc, acc, acc, pl.BlockSpec((1, 128), lambda i: (0, 0))],
        out_shape=[SDS((seq, Q_WIDTH), BF16)] + [SDS((seq, 128), F32)] * 4 + [SDS((1, 128), F32)],
        operands=(sinks, q, dattn, attn, kd0, kd1, vd0, vd1), comm=comm)


def _in_proj_bwd(dq, dk0, dk1, dv0, dv1, dgb, dy, gc, xin, conv_w, x, dh, g_pre, w_in_t, rope):
    seq = x.shape[0]
    tb = min(seq, WIDE_TOKEN_TILE)
    n_tiles = seq // tb

    def body(dq_ref, dk0_ref, dk1_ref, dv0_ref, dv1_ref, dgb_ref, dy_ref, dyh_ref, gc_ref, xin_ref, cw_ref,
             x_ref, dh_ref, g_ref, w_ref, c_ref, sa_ref, sb_ref,
             dproj_ref, gx_ref, dg_ref):
        i = pl.program_id(0)

        @pl.when(i == 0)
        def _():
            dg_ref[...] = jnp.zeros_like(dg_ref)

        dy = dy_ref[...].astype(F32)
        ext = jnp.concatenate([dy, jnp.where(i == n_tiles - 1, 0.0, dyh_ref[...].astype(F32))], axis=0)
        dy1 = pltpu.roll(ext, tb + HALO - 1, 0)[0:tb]
        dy2 = pltpu.roll(ext, tb + HALO - 2, 0)[0:tb]
        cw = cw_ref[...]
        du = cw[2:3, :] * dy + cw[1:2, :] * dy1 + cw[0:1, :] * dy2
        scale = 1.0 / math.sqrt(HEAD_DIM)
        base = Q_WIDTH + 2 * KV_WIDTH
        halves = [slice(0, tb // 2), slice(tb // 2, tb)]
        low = _lane_lt64((tb // 2, 128))
        for rows in halves:
            c, sa, sb = _rope_tile(c_ref.at[rows, :], sa_ref, sb_ref)
            for p in range(Q_WIDTH // 128):
                dproj_ref[rows, 128 * p:128 * (p + 1)] = _rope_transposed(
                    dq_ref[rows, 128 * p:128 * (p + 1)].astype(F32) * scale, c, sa, sb).astype(BF16)
            dk = jnp.where(low, dk0_ref[rows, :], dk1_ref[rows, :])
            dproj_ref[rows, Q_WIDTH:Q_WIDTH + KV_WIDTH] = _rope_transposed(dk, c, sa, sb).astype(BF16)
            dproj_ref[rows, Q_WIDTH + KV_WIDTH:base] = jnp.where(low, dv0_ref[rows, :], dv1_ref[rows, :]).astype(BF16)
            dproj_ref[rows, base:base + CONV_WIDTH] = dgb_ref[rows, :]
            dproj_ref[rows, base + CONV_WIDTH:base + 2 * CONV_WIDTH] = (du[rows] * xin_ref[rows, :].astype(F32)).astype(BF16)
            dproj_ref[rows, base + 2 * CONV_WIDTH:] = (du[rows] * gc_ref[rows, :].astype(F32)).astype(BF16)
        w_all = w_ref[...].reshape(IN_COLS, D_MODEL)
        dhn = [_dot(dproj_ref[rows, :], w_all) for rows in halves]
        dg = jnp.zeros((1, D_MODEL), F32)
        for k, rows in enumerate(halves):
            xv = x_ref[rows, :]
            r = _rms(xv)
            xhat = xv * r
            dg = dg + _colsum(dhn[k] * xhat)
            gx_ref[rows, :] = dh_ref[rows, :].astype(F32) + _norm_bwd(dhn[k], g_ref[...], xhat, r)
        dg_ref[...] += dg

    tile = lambda w: pl.BlockSpec((tb, w), lambda i: (i, 0))
    halo_next = pl.BlockSpec((HALO, CONV_WIDTH), lambda i: (jnp.minimum((i + 1) * (tb // HALO), seq // HALO - 1), 0))
    return _pallas(
        body, name="in_proj_bwd", grid=(n_tiles,),
        in_specs=[tile(Q_WIDTH), tile(128), tile(128), tile(128), tile(128), tile(CONV_WIDTH), tile(CONV_WIDTH), halo_next,
                  tile(CONV_WIDTH), tile(CONV_WIDTH), _resident((CONV_K, CONV_WIDTH)),
                  tile(D_MODEL), tile(D_MODEL), _resident((1, D_MODEL)), _resident(w_in_t.shape), *_rope_specs(tb)],
        out_specs=[tile(IN_COLS), tile(D_MODEL), pl.BlockSpec((1, D_MODEL), lambda i: (0, 0))],
        out_shape=[SDS((seq, IN_COLS), BF16), SDS((seq, D_MODEL), F32), SDS((1, D_MODEL), F32)],
        operands=(dq, dk0, dk1, dv0, dv1, dgb, dy, dy, gc, xin, conv_w, x, dh, g_pre, w_in_t, *rope))


def _wgrad_grid(seq, per_chip, h_rows):
    chips_per_step = 1 if per_chip else N_CHIPS
    m = chips_per_step * 2 * h_rows
    bt = min(seq, WGRAD_TOKEN_TILE)
    return chips_per_step, m, bt, seq // bt


def _wgrad(name, a, b, *, per_chip, h_rows, square_a=False, comm=None, rider=None):
    seq = a.shape[0]
    chips_per_step, m, bt, n_k = _wgrad_grid(seq, per_chip, h_rows)
    a_cols = m if per_chip else a.shape[1]
    a_wide = a.shape[1] > a_cols
    b_wide = b.shape[1] > D_MODEL

    def body(a_ref, b_ref, g_ref):
        @pl.when(pl.program_id(1) == 0)
        def _():
            g_ref[...] = jnp.zeros_like(g_ref)

        av = a_ref[...]
        if square_a:
            av = (av.astype(F32) * av.astype(F32)).astype(BF16)
        g_ref[...] += _dot_tn(av, b_ref[...]).reshape(g_ref.shape)

    a_spec = pl.BlockSpec((bt, a_cols), (lambda j, k: (k, j)) if a_wide else (lambda j, k: (k, 0)))
    b_spec = pl.BlockSpec((bt, D_MODEL), (lambda j, k: (k, j)) if b_wide else (lambda j, k: (k, 0)))
    g_spec = pl.BlockSpec((chips_per_step, 2, h_rows, D_MODEL), lambda j, k: (j, 0, 0, 0),
                          pipeline_mode=None if per_chip else pl.Buffered(1))
    return _pallas(
        body, name=name, grid=(N_CHIPS if per_chip else 1, n_k),
        in_specs=[a_spec, b_spec], out_specs=[g_spec], out_shape=[SDS((N_CHIPS, 2, h_rows, D_MODEL), F32)],
        operands=(a, b), comm=comm, rider=rider)


def _adamw_math(w, g, m, v):
    m = ADAM_B1 * m + (1.0 - ADAM_B1) * g
    v = ADAM_B2 * v + (1.0 - ADAM_B2) * (g * g)
    m_hat = m / (1.0 - ADAM_B1 ** ADAM_STEP)
    v_hat = v / (1.0 - ADAM_B2 ** ADAM_STEP)
    delta = -ADAM_LR * (m_hat / (jnp.sqrt(v_hat) + ADAM_EPS) + ADAM_WD * w)
    return delta, m, v


def _adamw_rows(name, reduced, w, m, v, rt):
    per_half = reduced.shape[1] // rt

    def body(r_ref, w_ref, m_ref, v_ref, g_out, d_out, m_out, v_out):
        g = r_ref[0]
        g_out[...] = g
        d_out[...], m_out[...], v_out[...] = _adamw_math(w_ref[...], g, m_ref[...], v_ref[...])

    blk = pl.BlockSpec((rt, D_MODEL), lambda h, r: (h * per_half + r, 0))
    return _pallas(
        body, name=name, grid=(2, per_half),
        in_specs=[pl.BlockSpec((1, rt, D_MODEL), lambda h, r: (h, r, 0)), blk, blk, blk],
        out_specs=[blk, blk, blk, blk], out_shape=[SDS(w.shape, F32)] * 4, operands=(reduced, w, m, v))


def _adamw_small(packed_grads, w, m, v):
    names = SMALL_NAMES
    n = len(names)
    conv_local = w["conv_w"].shape[-1]

    def body(*refs):
        gp = refs[0]
        w_refs, m_refs, v_refs = refs[1:1 + n], refs[1 + n:1 + 2 * n], refs[1 + 2 * n:1 + 3 * n]
        outs = refs[1 + 3 * n:]
        g_out, d_out, m_out, v_out = outs[0:n], outs[n:2 * n], outs[2 * n:3 * n], outs[3 * n:4 * n]
        chip = 2 * lax.axis_index("x") + lax.axis_index("y")

        def step(k, g, index=None):
            pick = (lambda r: r[...]) if index is None else (lambda r: r[index])
            d, new_m, new_v = _adamw_math(pick(w_refs[k]), g, pick(m_refs[k]), pick(v_refs[k]))
            for ref, val in ((g_out[k], g), (d_out[k], d), (m_out[k], new_m), (v_out[k], new_v)):
                if index is None:
                    ref[...] = val
                else:
                    ref[index] = val

        for k, name in enumerate(names):
            if name in SMALL_VECTORS:
                step(k, gp[SMALL_VECTORS.index(name):SMALL_VECTORS.index(name) + 1, :])
            elif name == "attn_group_norm":
                step(k, gp[4:5, 0:Q_WIDTH])
            elif name == "conv_group_norm":
                step(k, gp[4:5, Q_WIDTH:])
            elif name == "attn_sinks":
                step(k, gp[7:8, 0:8])
            else:
                for t in range(CONV_K):
                    row, base = 5 + t // 2, CONV_WIDTH * (t % 2)
                    g = gp[row:row + 1, base:base + conv_local]
                    for j in range(1, CONV_WIDTH // conv_local):
                        g = jnp.where(chip == j, gp[row:row + 1, base + conv_local * j:base + conv_local * (j + 1)], g)
                    step(k, g, index=(0, slice(t, t + 1), slice(None)))

    shapes = [SDS(w[name].shape, F32) for name in names]
    res = pl.pallas_call(
        body, name="adamw_small", in_specs=[VMEM_WHOLE] * (1 + 3 * n), out_specs=[VMEM_WHOLE] * (4 * n),
        out_shape=shapes * 4,
    )(packed_grads, *[w[k] for k in names], *[m[k] for k in names], *[v[k] for k in names])
    return [dict(zip(names, res[i * n:(i + 1) * n])) for i in range(4)]


SMALL_VECTORS = ("pre_mix_norm", "post_mix_norm", "pre_mlp_norm", "post_mlp_norm")
SMALL_NAMES = SMALL_VECTORS + ("attn_group_norm", "conv_group_norm", "conv_w", "attn_sinks")


def _pack_small(p):
    rows = [p[n].reshape(1, D_MODEL) for n in SMALL_VECTORS]
    rows.append(jnp.concatenate([p["attn_group_norm"].reshape(1, -1), p["conv_group_norm"].reshape(1, -1)], axis=1))
    cw = p["conv_w"].reshape(CONV_K, -1)
    rows.append(jnp.pad(cw, ((0, 1), (0, CONV_WIDTH - cw.shape[1]))).reshape(2, D_MODEL))
    last = jnp.concatenate([p["attn_sinks"].reshape(1, 8), p.get("loss_sum", jnp.zeros((1, 1), F32))], axis=1)
    rows.append(jnp.pad(last, ((0, 0), (0, D_MODEL - 9))))
    return jnp.concatenate(rows, axis=0)


WEIGHT_ORDER = ("pre_mix_norm", "w_in", "conv_w", "attn_sinks", "attn_group_norm", "conv_group_norm", "w_out",
                "post_mix_norm", "pre_mlp_norm", "w_up", "w_down", "post_mlp_norm")


def kernel(x, pre_mix_norm, w_in, conv_w, attn_sinks, attn_group_norm, conv_group_norm, w_out, post_mix_norm, pre_mlp_norm, w_up, w_down, post_mlp_norm, loss_target, m_pre_mix_norm, m_w_in, m_conv_w, m_attn_sinks, m_attn_group_norm, m_conv_group_norm, m_w_out, m_post_mix_norm, m_pre_mlp_norm, m_w_up, m_w_down, m_post_mlp_norm, v_pre_mix_norm, v_w_in, v_conv_w, v_attn_sinks, v_attn_group_norm, v_conv_group_norm, v_w_out, v_post_mix_norm, v_pre_mlp_norm, v_w_up, v_w_down, v_post_mlp_norm):
    w = dict(pre_mix_norm=pre_mix_norm, w_in=w_in, conv_w=conv_w, attn_sinks=attn_sinks, attn_group_norm=attn_group_norm,
             conv_group_norm=conv_group_norm, w_out=w_out, post_mix_norm=post_mix_norm, pre_mlp_norm=pre_mlp_norm,
             w_up=w_up, w_down=w_down, post_mlp_norm=post_mlp_norm)
    m = dict(pre_mix_norm=m_pre_mix_norm, w_in=m_w_in, conv_w=m_conv_w, attn_sinks=m_attn_sinks,
             attn_group_norm=m_attn_group_norm, conv_group_norm=m_conv_group_norm, w_out=m_w_out,
             post_mix_norm=m_post_mix_norm, pre_mlp_norm=m_pre_mlp_norm, w_up=m_w_up, w_down=m_w_down,
             post_mlp_norm=m_post_mlp_norm)
    v = dict(pre_mix_norm=v_pre_mix_norm, w_in=v_w_in, conv_w=v_conv_w, attn_sinks=v_attn_sinks,
             attn_group_norm=v_attn_group_norm, conv_group_norm=v_conv_group_norm, w_out=v_w_out,
             post_mix_norm=v_post_mix_norm, pre_mlp_norm=v_pre_mlp_norm, w_up=v_w_up, w_down=v_w_down,
             post_mlp_norm=v_post_mlp_norm)
    core = lax.axis_index("c").astype(jnp.int32).reshape(1)
    xs, target = x[0], loss_target[0]
    rope = _rope_inputs(xs.shape[0])

    hb_up, hb_down, hb_out, hb_in = _cast_halves(core, w_up[0], w_down[0], w_out[0], w_in[0].T)
    conv_pad = jnp.pad(conv_w[0], ((0, 8 - CONV_K), (0, 0)))
    wf_in, conv_all = _gather_whole(hb_in, conv_pad)
    conv_full = conv_all[:, :CONV_K, :].transpose(1, 0, 2).reshape(CONV_K, CONV_WIDTH)

    *proj, wf_up, wf_out = _in_proj(xs, pre_mix_norm, wf_in, rope, comm=_merge(_relay_first(hb_up), _gather_first(hb_out)))
    q, kd0, kd1, vd0, vd1, gb, gc, xin, hn = proj
    attn, wf_up, wf_out, wf_down = _attention_fwd(
        q, kd0, kd1, vd0, vd1, attn_sinks,
        comm=_merge(_relay_second(wf_up), _gather_second(wf_out), _relay_first(hb_down)))
    mix, mixed, wf_up, wf_down = _mix_out(attn, gb, gc, xin, conv_full, attn_group_norm, conv_group_norm, wf_out,
                                          comm=_merge(_relay_third(wf_up), _relay_second(wf_down, then_third=True)))
    up, hn2, dout, dmlp, loss_sum, dg_post_mlp = _mlp_loss(xs, mix, target, post_mix_norm, pre_mlp_norm, post_mlp_norm,
                                                           wf_up, wf_down)

    dup, dh, dmix, dg_pre_mlp, dg_post_mix = _mlp_bwd(dmlp, up, xs, dout, mix, pre_mlp_norm, post_mix_norm, wf_up, wf_down)
    n_k = _wgrad_grid(xs.shape[0], True, H_DOWN)[3]
    g_down, dattn, dgb, dy, dg_attn, dg_conv, dconv_w = _wgrad(
        "wgrad_down", up, dmlp, per_chip=True, h_rows=H_DOWN, square_a=True,
        rider=_mix_bwd(dmix, attn, gb, gc, xin, conv_full, attn_group_norm, conv_group_norm, wf_out, n_k))
    g_up, got_down = _wgrad("wgrad_up", hn2, dup, per_chip=True, h_rows=H_UP, comm=_pair_send(g_down))
    p_down = _pair_sum("pair_sum_down", core, g_down, got_down)
    g_out, got_up = _wgrad("wgrad_out", mixed, dmix, per_chip=False, h_rows=H_OUT, comm=_pair_send(g_up))
    p_up = _pair_sum("pair_sum_up", core, g_up, got_up)
    dq, dk0, dk1, dv0, dv1, dsink, ex_down, ex_up, got_out = _attention_bwd(
        q, dattn, attn, kd0, kd1, vd0, vd1, attn_sinks,
        comm=_merge(_chip_exchange(p_down), _chip_exchange(p_up), _pair_send(g_out)))
    p_out = _pair_sum("pair_sum_out", core, g_out, got_out)
    dproj, grad_x, dg_pre_mix = _in_proj_bwd(dq, dk0, dk1, dv0, dv1, dgb, dy, gc, xin, conv_full, xs, dh, pre_mix_norm,
                                             wf_in, rope)
    g_in, ex_out = _wgrad("wgrad_in", dproj, hn, per_chip=False, h_rows=H_IN, comm=_chip_exchange(p_out))
    small = dict(pre_mix_norm=dg_pre_mix, conv_w=dconv_w, attn_sinks=dsink[:, :8], attn_group_norm=dg_attn,
                 conv_group_norm=dg_conv, post_mix_norm=dg_post_mix, pre_mlp_norm=dg_pre_mlp, post_mlp_norm=dg_post_mlp,
                 loss_sum=loss_sum)
    r_down, r_up, r_out, r_in, small_total = _tail_reduce(g_in, [ex_down, ex_up, ex_out], _pack_small(small))

    out_g, out_d, out_m, out_v = {}, {}, {}, {}
    out_g["w_up"], out_d["w_up"], out_m["w_up"], out_v["w_up"] = _adamw_rows(
        "adamw_up", r_up, w_up[0], m_w_up[0], v_w_up[0], 256)
    out_g["w_down"], out_d["w_down"], out_m["w_down"], out_v["w_down"] = _adamw_rows(
        "adamw_down", r_down, w_down[0], m_w_down[0], v_w_down[0], 256)
    out_g["w_out"], out_d["w_out"], out_m["w_out"], out_v["w_out"] = _adamw_rows(
        "adamw_out", r_out, w_out[0], m_w_out[0], v_w_out[0], H_OUT)
    in_t = _adamw_rows("adamw_in", r_in, w_in[0].T, m_w_in[0].T, v_w_in[0].T, H_IN)
    out_g["w_in"], out_d["w_in"], out_m["w_in"], out_v["w_in"] = [t.T for t in in_t]

    loss = small_total[7, 8] * (0.5 / D_MODEL)
    for out, part in zip((out_g, out_d, out_m, out_v), _adamw_small(small_total, w, m, v)):
        out.update(part)

    def shaped(d):
        return [d[n].reshape(w[n].shape) for n in WEIGHT_ORDER]

    return (loss, grad_x[None], *shaped(out_g), *shaped(out_d), *shaped(out_m), *shaped(out_v))
```

```python
import math
from typing import Callable, NamedTuple

import jax
import jax.numpy as jnp
import numpy as np
from jax import lax
from jax.experimental import pallas as pl
from jax.experimental.pallas import tpu as pltpu

F32 = jnp.float32
BF16 = jnp.bfloat16

D_MODEL = 1024
HEAD_DIM = 64
Q_WIDTH = 512
KV_WIDTH = 128
CONV_WIDTH = 512
CONV_K = 3
D_FF = 4096
IN_COLS = 2304
QBLOCK = 128
ROT_DIM = 16
ROPE_THETA = 500000.0
NORM_EPS = 1e-6
NEG_INF = -1e30
N_CHIPS = 4

ADAM_LR = 0.001
ADAM_B1 = 0.9
ADAM_B2 = 0.999
ADAM_EPS = 1e-08
ADAM_WD = 0.01
ADAM_STEP = 10

H_UP, H_DOWN, H_OUT, H_IN = 512, 512, 128, 288
DOWN_EARLY_ROWS = 224

TOKEN_TILE = 512
WIDE_TOKEN_TILE = 1024
MLP_BWD_TOKEN_TILE = 512
MLP_BWD_SUB_TILE = 256
ATTN_FWD_BLOCKS = 16
ATTN_BWD_BLOCKS = 2
WGRAD_TOKEN_TILE = 2048
VMEM_LIMIT_V7X = 56 * 1024 * 1024

MESH = pl.DeviceIdType.MESH
ANY = pl.BlockSpec(memory_space=pl.ANY)
VMEM_WHOLE = pl.BlockSpec(memory_space=pltpu.VMEM)
SDS = jax.ShapeDtypeStruct


def _resident(shape):
    zeros = (0,) * len(shape)
    return pl.BlockSpec(shape, lambda *_: zeros, pipeline_mode=pl.Buffered(1))


def _rms(v):
    return lax.rsqrt(jnp.mean(v * v, axis=-1, keepdims=True) + NORM_EPS)


def _norm_bwd(dy, gain, vhat, rstd):
    t = dy * gain
    return rstd * (t - vhat * jnp.mean(t * vhat, axis=-1, keepdims=True))


def _colsum(v):
    return jnp.sum(v, axis=0, keepdims=True)


def _dot_nt(a, b):
    return lax.dot_general(a, b, (((1,), (1,)), ((), ())), preferred_element_type=F32)


def _dot_tn(a, b):
    return lax.dot_general(a, b, (((0,), (0,)), ((), ())), preferred_element_type=F32)


def _dot(a, b):
    return jnp.dot(a, b, preferred_element_type=F32)


def _chip_block(w_ref, chip):
    both = w_ref[pl.ds(2 * chip, 2)]
    return both.reshape(2 * both.shape[1], both.shape[2])


def _lane_lt64(shape):
    return lax.broadcasted_iota(jnp.int32, shape, 1) < HEAD_DIM


class _Comm(NamedTuple):
    operands: tuple
    out_shapes: tuple
    aliases: dict
    n_remote: int
    n_local: int
    plan: Callable
    after: Callable = None


def _merge(*comms):
    operands, out_shapes, aliases, parts = [], [], {}, []
    n_remote = n_local = 0
    for cm in comms:
        parts.append((len(operands), len(out_shapes), n_remote, n_local, cm))
        for k, v in cm.aliases.items():
            aliases[len(operands) + k] = len(out_shapes) + v
        operands += cm.operands
        out_shapes += cm.out_shapes
        n_remote += cm.n_remote
        n_local += cm.n_local

    def run(which, ins, outs, send, recv, loc):
        sends, recvs, locs = [], [], []
        for i0, o0, r0, l0, cm in parts:
            stage = getattr(cm, which)
            if stage is not None:
                s, r, l = stage(ins[i0:i0 + len(cm.operands)], outs[o0:o0 + len(cm.out_shapes)],
                                lambda k, r0=r0: send(r0 + k), lambda k, r0=r0: recv(r0 + k), lambda k, l0=l0: loc(l0 + k))
                sends, recvs, locs = sends + s, recvs + r, locs + l
        return sends, recvs, locs

    def plan(*args):
        return run("plan", *args)

    def after(*args):
        return run("after", *args)

    return _Comm(tuple(operands), tuple(out_shapes), aliases, n_remote, n_local, plan,
                 after if any(cm.after is not None for cm in comms) else None)


def _sem_scratch(comm):
    return [pltpu.SemaphoreType.DMA((max(comm.n_remote, 1),)), pltpu.SemaphoreType.DMA((max(comm.n_remote, 1),)),
            pltpu.SemaphoreType.DMA((max(comm.n_local, 1),))]


class _Rider(NamedTuple):
    body: Callable
    in_specs: list
    out_specs: list
    out_shape: list
    operands: tuple


def _pallas(body, *, name, grid, in_specs, out_specs, out_shape, operands, scratch=(), comm=None, rider=None):
    params = pltpu.CompilerParams(dimension_semantics=("arbitrary",) * len(grid), vmem_limit_bytes=VMEM_LIMIT_V7X)
    if rider is not None:
        own_in, own_out, ride_in, ride_out = len(in_specs), len(out_specs), len(rider.in_specs), len(rider.out_specs)
        own_body = body

        def body(*refs):
            o0 = own_in + ride_in
            s0 = o0 + own_out + ride_out
            own_body(*refs[:own_in], *refs[o0:o0 + own_out], *refs[s0:])
            first = None
            for axis in range(len(grid)):
                at_start = pl.program_id(axis) == 0
                first = at_start if first is None else jnp.logical_and(first, at_start)
            rider.body(first, *refs[own_in:o0], *refs[o0 + own_out:s0])

        in_specs, out_specs = list(in_specs) + rider.in_specs, list(out_specs) + rider.out_specs
        out_shape, operands = list(out_shape) + rider.out_shape, tuple(operands) + tuple(rider.operands)
    if comm is None:
        return pl.pallas_call(body, name=name, grid=grid, in_specs=in_specs, out_specs=out_specs, out_shape=out_shape,
                              scratch_shapes=list(scratch), compiler_params=params)(*operands)
    n_in, n_out, n_scr = len(in_specs), len(out_specs), len(scratch)
    c_in, c_out = len(comm.operands), len(comm.out_shapes)

    def with_comm(*refs):
        ins, c_ins = refs[:n_in], refs[n_in:n_in + c_in]
        o0 = n_in + c_in
        outs, c_outs = refs[o0:o0 + n_out], refs[o0 + n_out:o0 + n_out + c_out]
        s0 = o0 + n_out + c_out
        scr = refs[s0:s0 + n_scr]
        send_sems, recv_sems, local_sems = refs[s0 + n_scr:]
        first = last = None
        for axis, size in enumerate(grid):
            at_start, at_end = pl.program_id(axis) == 0, pl.program_id(axis) == size - 1
            first = at_start if first is None else jnp.logical_and(first, at_start)
            last = at_end if last is None else jnp.logical_and(last, at_end)

        def copies():
            return comm.plan(c_ins, c_outs, lambda k: send_sems.at[k], lambda k: recv_sems.at[k],
                             lambda k: local_sems.at[k])

        @pl.when(first)
        def _():
            sends, _, locs = copies()
            for cp in sends + locs:
                cp.start()

        body(*ins, *outs, *scr)

        @pl.when(last)
        def _():
            sends, recvs, locs = copies()
            for cp in recvs:
                cp.wait_recv()
            for cp in sends:
                cp.wait_send()
            for cp in locs:
                cp.wait()
            if comm.after is not None:
                sends, recvs, _ = comm.after(c_ins, c_outs, lambda k: send_sems.at[k], lambda k: recv_sems.at[k],
                                             lambda k: local_sems.at[k])
                for cp in sends:
                    cp.start()
                for cp in recvs:
                    cp.wait_recv()
                for cp in sends:
                    cp.wait_send()

    return pl.pallas_call(
        with_comm, name=name, grid=grid,
        in_specs=list(in_specs) + [ANY] * c_in, out_specs=list(out_specs) + [ANY] * c_out,
        out_shape=list(out_shape) + list(comm.out_shapes),
        scratch_shapes=list(scratch) + _sem_scratch(comm),
        input_output_aliases={n_in + k: n_out + v for k, v in comm.aliases.items()},
        compiler_params=params)(*operands, *comm.operands)


def _place():
    return lax.axis_index("x"), lax.axis_index("y"), lax.axis_index("c")


def _other_chips(x, y):
    return [(1 - x, y), (x, 1 - y), (1 - x, 1 - y)]


def _slot(px, py, pc):
    return 4 * px + 2 * py + pc


def _remote(src, dst, send_sem, recv_sem, to):
    return pltpu.make_async_remote_copy(src_ref=src, dst_ref=dst, send_sem=send_sem, recv_sem=recv_sem,
                                        device_id=to, device_id_type=MESH)


def _gather_first(half_block):
    def plan(ins, outs, send, recv, loc):
        (blk,), (full,) = ins, outs
        x, y, c = _place()
        chips = _other_chips(x, y)
        mine = full.at[_slot(x, y, c)]
        sends = [_remote(blk, mine, send(0), recv(0), (x, y, 1 - c))]
        sends += [_remote(blk, mine, send(1 + j), recv(1 + j), (*chip, c)) for j, chip in enumerate(chips)]
        recvs = [_remote(blk, full.at[_slot(x, y, 1 - c)], send(0), recv(0), (x, y, 1 - c))]
        recvs += [_remote(blk, full.at[_slot(*chip, c)], send(1 + j), recv(1 + j), (*chip, c))
                  for j, chip in enumerate(chips)]
        return sends, recvs, [pltpu.make_async_copy(blk, mine, loc(0))]

    return _Comm((half_block,), (SDS((2 * N_CHIPS,) + half_block.shape, half_block.dtype),), {}, 4, 1, plan)


def _gather_second(partly_gathered):
    def plan(ins, outs, send, recv, loc):
        (src,), (full,) = ins, outs
        x, y, c = _place()
        chips = _other_chips(x, y)
        sends = [_remote(src.at[_slot(*chip, c)], full.at[_slot(*chip, c)], send(j), recv(j), (x, y, 1 - c))
                 for j, chip in enumerate(chips)]
        recvs = [_remote(src.at[_slot(*chip, 1 - c)], full.at[_slot(*chip, 1 - c)], send(j), recv(j), (x, y, 1 - c))
                 for j, chip in enumerate(chips)]
        return sends, recvs, []

    return _Comm((partly_gathered,), (SDS(partly_gathered.shape, partly_gathered.dtype),), {0: 0}, 3, 0, plan)


def _relay_pieces(full, rows, x, y, c):
    start, half = rows[0], rows[1] // 2
    upper, lower = pl.ds(start, half), pl.ds(start + half, half)
    diagonal = full.at[_slot(1 - x, 1 - y, c)]
    return [(full.at[_slot(1 - x, y, c), upper], diagonal.at[upper], (x, 1 - y, c)),
            (full.at[_slot(x, 1 - y, c), lower], diagonal.at[lower], (1 - x, y, c))]


def _relay(half_block, so_far, first=None, second=None, third=None, third_after=None):
    has_block, has_buffer = half_block is not None, so_far is not None
    shape = so_far.shape if has_buffer else (2 * N_CHIPS,) + half_block.shape
    dtype = so_far.dtype if has_buffer else half_block.dtype

    def third_leg(rows, k, ins, outs, send, recv):
        src, full = (ins[-1] if has_buffer else outs[0]), outs[0]
        x, y, c = _place()
        span, sibling = pl.ds(*rows), (x, y, 1 - c)
        here, there = _slot(1 - x, 1 - y, c), _slot(1 - x, 1 - y, 1 - c)
        return ([_remote(src.at[here, span], full.at[here, span], send(k), recv(k), sibling)],
                [_remote(src.at[there, span], full.at[there, span], send(k), recv(k), sibling)])

    def plan(ins, outs, send, recv, loc):
        src, full = (ins[-1] if has_buffer else outs[0]), outs[0]
        x, y, c = _place()
        sibling = (x, y, 1 - c)
        sends, recvs, locs = [], [], []
        if first is not None:
            span = pl.ds(*first)
            blk, mine = ins[0].at[span], full.at[_slot(x, y, c), span]
            for k, peer in enumerate([sibling, (1 - x, y, c), (x, 1 - y, c)]):
                sends.append(_remote(blk, mine, send(k), recv(k), peer))
                recvs.append(_remote(blk, full.at[_slot(*peer), span], send(k), recv(k), peer))
            locs.append(pltpu.make_async_copy(blk, mine, loc(0)))
        if second is not None:
            span = pl.ds(*second)
            for k, chip in enumerate([(1 - x, y), (x, 1 - y)]):
                sends.append(_remote(src.at[_slot(*chip, c), span], full.at[_slot(*chip, c), span], send(3 + k), recv(3 + k),
                                     sibling))
                recvs.append(_remote(src.at[_slot(*chip, 1 - c), span], full.at[_slot(*chip, 1 - c), span], send(3 + k),
                                     recv(3 + k), sibling))
            for k, (piece, lands, peer) in enumerate(_relay_pieces(full, second, x, y, c)):
                sends.append(_remote(piece, piece, send(5 + k), recv(5 + k), peer))
                recvs.append(_remote(lands, lands, send(5 + k), recv(5 + k), peer))
        if third is not None:
            s, r = third_leg(third, 7, ins, outs, send, recv)
            sends, recvs = sends + s, recvs + r
        return sends, recvs, locs

    def after(ins, outs, send, recv, loc):
        s, r = third_leg(third_after, 8, ins, outs, send, recv)
        return s, r, []

    operands = ((half_block,) if has_block else ()) + ((so_far,) if has_buffer else ())
    return _Comm(operands, (SDS(shape, dtype),), {len(operands) - 1: 0} if has_buffer else {}, 9, 1, plan,
                 after if third_after is not None else None)


def _gather_whole(half_block, small_block):
    rows = half_block.shape[0]

    def body(blk_ref, small_ref, out_ref, small_out_ref, send_sems, recv_sems, local_sems):
        x, y, c = _place()
        me, sibling = (x, y, c), (x, y, 1 - c)
        neighbours, diagonal = [(1 - x, y), (x, 1 - y)], (1 - x, 1 - y)

        def copy(k, block, to, src=None):
            return _remote(out_ref.at[_slot(*block)] if src is None else src, out_ref.at[_slot(*block)],
                           send_sems.at[k], recv_sems.at[k], to)

        def small_copy(k, chip, to):
            return _remote(small_ref, small_out_ref.at[2 * chip[0] + chip[1]], send_sems.at[8 + k], recv_sems.at[8 + k], to)

        mine = pltpu.make_async_copy(blk_ref, out_ref.at[_slot(*me)], local_sems.at[0])
        mine_small = pltpu.make_async_copy(small_ref, small_out_ref.at[2 * x + y], local_sems.at[1])
        mine.start()
        mine_small.start()
        started = [copy(0, me, sibling, src=blk_ref)]
        started += [copy(1 + k, me, (*chip, c), src=blk_ref) for k, chip in enumerate(neighbours)]
        started += [small_copy(k, (x, y), (*chip, c)) for k, chip in enumerate(neighbours + [diagonal])]
        for cp in started:
            cp.start()
        pieces = _relay_pieces(out_ref, (0, rows), x, y, c)
        for k, chip in enumerate(neighbours):
            copy(1 + k, (*chip, c), me).wait_recv()
            piece, _, peer = pieces[k]
            started += [copy(3 + k, (*chip, c), sibling), _remote(piece, piece, send_sems.at[5 + k], recv_sems.at[5 + k], peer)]
            started[-2].start()
            started[-1].start()
        for k, (_, lands, peer) in enumerate(pieces):
            _remote(lands, lands, send_sems.at[5 + k], recv_sems.at[5 + k], peer).wait_recv()
        started.append(copy(7, (*diagonal, c), sibling))
        started[-1].start()
        copy(0, sibling, me).wait_recv()
        for k, chip in enumerate(neighbours):
            copy(3 + k, (*chip, 1 - c), me).wait_recv()
        copy(7, (*diagonal, 1 - c), me).wait_recv()
        for k, chip in enumerate(neighbours + [diagonal]):
            small_copy(k, chip, me).wait_recv()
        for cp in started:
            cp.wait_send()
        mine.wait()
        mine_small.wait()

    return pl.pallas_call(
        body, name="gather_whole", in_specs=[ANY, ANY], out_specs=[ANY, ANY],
        out_shape=[SDS((2 * N_CHIPS,) + half_block.shape, half_block.dtype),
                   SDS((N_CHIPS,) + small_block.shape, small_block.dtype)],
        scratch_shapes=[pltpu.SemaphoreType.DMA((11,)), pltpu.SemaphoreType.DMA((11,)), pltpu.SemaphoreType.DMA((2,))],
    )(half_block, small_block)


def _pair_send(grads):
    def plan(ins, outs, send, recv, loc):
        (g,), (got,) = ins, outs
        x, y, c = _place()
        copies = [_remote(g.at[j, 1 - c], got.at[j], send(j), recv(j), (x, y, 1 - c)) for j in range(N_CHIPS)]
        return copies, copies, []

    shape = (grads.shape[0],) + grads.shape[2:]
    return _Comm((grads,), (SDS(shape, grads.dtype),), {}, N_CHIPS, 0, plan)


def _chip_exchange(partial):
    def plan(ins, outs, send, recv, loc):
        (p,), (got,) = ins, outs
        x, y, c = _place()
        my_chip = 2 * x + y
        chips = _other_chips(x, y)
        sends = [_remote(p.at[2 * chip[0] + chip[1]], got.at[my_chip], send(j), recv(j), (*chip, c))
                 for j, chip in enumerate(chips)]
        recvs = [_remote(p.at[my_chip], got.at[2 * chip[0] + chip[1]], send(j), recv(j), (*chip, c))
                 for j, chip in enumerate(chips)]
        return sends, recvs, [pltpu.make_async_copy(p.at[my_chip], got.at[my_chip], loc(0))]

    return _Comm((partial,), (SDS(partial.shape, partial.dtype),), {}, 3, 1, plan)


def _pair_sum(name, core, grads, received):
    h = grads.shape[2]

    def body(core_ref, g_ref, r_ref, o_ref):
        o_ref[...] = (g_ref[0] + r_ref[...]).astype(BF16)

    return pl.pallas_call(
        body, name=name,
        grid_spec=pltpu.PrefetchScalarGridSpec(
            num_scalar_prefetch=1, grid=(N_CHIPS,),
            in_specs=[pl.BlockSpec((1, 1, h, D_MODEL), lambda j, core_ref: (j, core_ref[0], 0, 0)),
                      pl.BlockSpec((1, h, D_MODEL), lambda j, core_ref: (j, 0, 0))],
            out_specs=pl.BlockSpec((1, h, D_MODEL), lambda j, core_ref: (j, 0, 0))),
        out_shape=SDS((N_CHIPS, h, D_MODEL), BF16),
        compiler_params=pltpu.CompilerParams(dimension_semantics=("arbitrary",), vmem_limit_bytes=VMEM_LIMIT_V7X),
    )(core, grads, received)


SMALL_ROWS = 8


def _sum_blocks(ref):
    return (ref[0].astype(F32) + ref[1].astype(F32)) + (ref[2].astype(F32) + ref[3].astype(F32))


def _tail_reduce(last_grads, exchanged, small):
    n = len(exchanged)
    h = last_grads.shape[2]

    def body(*refs):
        g_ref, ex, small_ref = refs[0], refs[1:1 + n], refs[1 + n]
        o0 = 2 + n
        out, out_last, small_out = refs[o0:o0 + n], refs[o0 + n], refs[o0 + n + 1]
        s0 = o0 + n + 2
        halves, half_last = refs[s0:s0 + n], refs[s0 + n]
        own, got, part, exch, small_buf = refs[s0 + n + 1:s0 + n + 6]
        pair_send, pair_recv, chip_send, chip_recv, share_send, share_recv, small_send, small_recv, local_sems = refs[s0 + n + 6:]
        x, y, c = _place()
        sibling = (x, y, 1 - c)
        my_chip, me = 2 * x + y, _slot(x, y, c)
        chips = _other_chips(x, y)

        to_sibling = [_remote(g_ref.at[j, 1 - c], got.at[j], pair_send.at[j], pair_recv.at[j], sibling)
                      for j in range(N_CHIPS)]
        load_own = [pltpu.make_async_copy(g_ref.at[j, c], own.at[j], local_sems.at[j]) for j in range(N_CHIPS)]
        for cp in to_sibling + load_own:
            cp.start()

        small_buf[me] = small_ref[...]
        small_copies = []
        for mask in range(1, 8):
            peer = (x ^ (mask >> 2), y ^ ((mask >> 1) & 1), c ^ (mask & 1))
            small_copies.append(_remote(small_ref, small_buf.at[me], small_send.at[mask - 1], small_recv.at[mask - 1], peer))
        for cp in small_copies:
            cp.start()

        def share(k, half_ref, out_ref):
            keep = pltpu.make_async_copy(half_ref, out_ref.at[c], local_sems.at[N_CHIPS + k])
            give = _remote(half_ref, out_ref.at[c], share_send.at[k], share_recv.at[k], sibling)
            take = _remote(half_ref, out_ref.at[1 - c], share_send.at[k], share_recv.at[k], sibling)
            keep.start()
            give.start()
            return keep, give, take

        shares = []
        for k in range(n):
            halves[k][...] = _sum_blocks(ex[k])
            shares.append(share(k, halves[k], out[k]))

        def pair_sum(block):
            _remote(g_ref.at[block, 1 - c], got.at[block], pair_send.at[block], pair_recv.at[block], sibling).wait_recv()
            pltpu.make_async_copy(g_ref.at[block, c], own.at[block], local_sems.at[block]).wait()
            part[block] = (own[block] + got[block]).astype(BF16)

        to_chips = []
        for j, chip in enumerate(chips):
            block = 2 * chip[0] + chip[1]
            pair_sum(block)
            to_chips.append(_remote(part.at[block], exch.at[my_chip], chip_send.at[j], chip_recv.at[j], (*chip, c)))
            to_chips[-1].start()
        pair_sum(my_chip)
        exch[my_chip] = part[my_chip]
        from_chips = [_remote(part.at[my_chip], exch.at[2 * chip[0] + chip[1]], chip_send.at[j], chip_recv.at[j], (*chip, c))
                      for j, chip in enumerate(chips)]

        for cp in small_copies:
            cp.wait_recv()
        total = small_buf[0]
        for d in range(1, 8):
            total = total + small_buf[d]
        small_out[...] = total

        for cp in from_chips:
            cp.wait_recv()
        half_last[...] = _sum_blocks(exch)
        shares.append(share(n, half_last, out_last))

        for keep, give, take in shares:
            take.wait_recv()
            give.wait_send()
            keep.wait()
        for cp in to_sibling + to_chips + small_copies:
            cp.wait_send()

    blocks = (N_CHIPS, h, D_MODEL)
    return pl.pallas_call(
        body, name="tail_reduce",
        in_specs=[ANY] + [VMEM_WHOLE] * (n + 1), out_specs=[ANY] * (n + 1) + [VMEM_WHOLE],
        out_shape=[SDS((2,) + e.shape[1:], F32) for e in exchanged] + [SDS((2, h, D_MODEL), F32), SDS(small.shape, F32)],
        scratch_shapes=[pltpu.VMEM(e.shape[1:], F32) for e in exchanged] + [pltpu.VMEM((h, D_MODEL), F32)]
                       + [pltpu.VMEM(blocks, F32), pltpu.VMEM(blocks, F32), pltpu.VMEM(blocks, BF16), pltpu.VMEM(blocks, BF16),
                          pltpu.VMEM((8,) + small.shape, F32)]
                       + [pltpu.SemaphoreType.DMA((N_CHIPS,)), pltpu.SemaphoreType.DMA((N_CHIPS,)),
                          pltpu.SemaphoreType.DMA((3,)), pltpu.SemaphoreType.DMA((3,)),
                          pltpu.SemaphoreType.DMA((n + 1,)), pltpu.SemaphoreType.DMA((n + 1,)),
                          pltpu.SemaphoreType.DMA((7,)), pltpu.SemaphoreType.DMA((7,)),
                          pltpu.SemaphoreType.DMA((N_CHIPS + n + 1,))],
        compiler_params=pltpu.CompilerParams(vmem_limit_bytes=VMEM_LIMIT_V7X),
    )(last_grads, *exchanged, small)


def _rope_expansion():
    half = ROT_DIM // 2
    expand = np.zeros((2 * half, 3 * 128), np.float32)
    const = np.zeros((1, 3 * 128), np.float32)
    for lane in range(128):
        d = lane % HEAD_DIM
        if d < ROT_DIM:
            expand[d % half, lane] = 1.0
        else:
            const[0, lane] = 1.0
        if d < half:
            expand[half + d, 128 + lane] = -1.0
        elif d < ROT_DIM:
            expand[half + d - half, 256 + lane] = 1.0
    return expand, const


ROPE_PIECES = 3 * ROT_DIM


def _rope_inputs(seq):
    pos = jnp.arange(seq, dtype=F32)
    inv_freq = ROPE_THETA ** (-jnp.arange(0, ROT_DIM, 2, dtype=F32) / ROT_DIM)
    ang = pos[:, None] * inv_freq[None, :]
    cs = jnp.concatenate([jnp.cos(ang), jnp.sin(ang)], axis=1)
    hi = lax.reduce_precision(cs, 8, 7)
    mid = lax.reduce_precision(cs - hi, 8, 7)
    low = cs - hi - mid
    expand, const = _rope_expansion()
    pieces = jnp.concatenate([hi, mid, low], axis=1).astype(BF16)
    return pieces, jnp.asarray(np.concatenate([expand] * 3, axis=0), BF16), jnp.asarray(const)


def _rope_specs(tb):
    return [pl.BlockSpec((tb, ROPE_PIECES), lambda i: (i, 0)), _resident((ROPE_PIECES, 3 * 128)), _resident((1, 3 * 128))]


def _rope_tile(pieces_ref, expand_ref, const_ref):
    tables = _dot(pieces_ref[...], expand_ref[...]) + const_ref[...]
    return tables[:, 0:128], tables[:, 128:256], tables[:, 256:384]


def _rope(t, c, sa, sb):
    half = ROT_DIM // 2
    return t * c + pltpu.roll(t, 128 - half, 1) * sa + pltpu.roll(t, half, 1) * sb


def _rope_transposed(dt, c, sa, sb):
    half = ROT_DIM // 2
    return dt * c + pltpu.roll(dt * sa, half, 1) + pltpu.roll(dt * sb, 128 - half, 1)


def _cast_halves(core, w_up, w_down, w_out, w_in_t):
    def body(core_ref, up_ref, down_ref, out_ref, in_ref, up_o, down_o, out_o, in_o):
        up_o[...] = up_ref[...].astype(BF16)
        down_o[...] = down_ref[...].astype(BF16)
        out_o[...] = out_ref[...].astype(BF16)
        in_o[...] = in_ref[...].astype(BF16)

    half = lambda rows: pl.BlockSpec((rows, D_MODEL), lambda i, core_ref: (core_ref[0], 0))
    whole = lambda rows: pl.BlockSpec((rows, D_MODEL), lambda i, core_ref: (0, 0))
    rows = (H_UP, H_DOWN, H_OUT, H_IN)
    return pl.pallas_call(
        body, name="cast_halves",
        grid_spec=pltpu.PrefetchScalarGridSpec(
            num_scalar_prefetch=1, grid=(1,), in_specs=[half(r) for r in rows], out_specs=[whole(r) for r in rows]),
        out_shape=[SDS((r, D_MODEL), BF16) for r in rows],
        compiler_params=pltpu.CompilerParams(dimension_semantics=("arbitrary",), vmem_limit_bytes=VMEM_LIMIT_V7X),
    )(core, w_up, w_down, w_out, w_in_t)


def _in_proj(x, g_pre, w_in_t, rope, comm=None):
    seq = x.shape[0]
    tb = min(seq, WIDE_TOKEN_TILE)

    def body(x_ref, g_ref, w_ref, c_ref, sa_ref, sb_ref,
             q_ref, kd0_ref, kd1_ref, vd0_ref, vd1_ref, gb_ref, gc_ref, xin_ref, hn_ref):
        xv = x_ref[...]
        hn = (xv * _rms(xv) * g_ref[...]).astype(BF16)
        hn_ref[...] = hn
        proj = _dot_nt(hn, w_ref[...].reshape(IN_COLS, D_MODEL))
        c, sa, sb = _rope_tile(c_ref, sa_ref, sb_ref)
        scale = 1.0 / math.sqrt(HEAD_DIM)
        for p in range(Q_WIDTH // 128):
            q_ref[:, 128 * p:128 * (p + 1)] = (_rope(proj[:, 128 * p:128 * (p + 1)], c, sa, sb) * scale).astype(BF16)
        k = _rope(proj[:, Q_WIDTH:Q_WIDTH + KV_WIDTH], c, sa, sb)
        v = proj[:, Q_WIDTH + KV_WIDTH:Q_WIDTH + 2 * KV_WIDTH]
        low = _lane_lt64(k.shape)
        k_sw, v_sw = pltpu.roll(k, HEAD_DIM, 1), pltpu.roll(v, HEAD_DIM, 1)
        kd0_ref[...] = jnp.where(low, k, k_sw).astype(BF16)
        kd1_ref[...] = jnp.where(low, k_sw, k).astype(BF16)
        vd0_ref[...] = jnp.where(low, v, v_sw).astype(BF16)
        vd1_ref[...] = jnp.where(low, v_sw, v).astype(BF16)
        base = Q_WIDTH + 2 * KV_WIDTH
        gb_ref[...] = proj[:, base:base + CONV_WIDTH].astype(BF16)
        gc_ref[...] = proj[:, base + CONV_WIDTH:base + 2 * CONV_WIDTH].astype(BF16)
        xin_ref[...] = proj[:, base + 2 * CONV_WIDTH:base + 3 * CONV_WIDTH].astype(BF16)

    tile = lambda w: pl.BlockSpec((tb, w), lambda i: (i, 0))
    return _pallas(
        body, name="in_proj", grid=(seq // tb,),
        in_specs=[tile(D_MODEL), _resident((1, D_MODEL)), _resident(w_in_t.shape), *_rope_specs(tb)],
        out_specs=[tile(Q_WIDTH), tile(128), tile(128), tile(128), tile(128),
                   tile(CONV_WIDTH), tile(CONV_WIDTH), tile(CONV_WIDTH), tile(D_MODEL)],
        out_shape=[SDS((seq, Q_WIDTH), BF16)] + [SDS((seq, 128), BF16)] * 4
                  + [SDS((seq, CONV_WIDTH), BF16)] * 3 + [SDS((seq, D_MODEL), BF16)],
        operands=(x, g_pre, w_in_t, *rope), comm=comm)


def _attn_valid(i):
    shape = (4 * QBLOCK, 2 * QBLOCK)
    row = lax.broadcasted_iota(jnp.int32, shape, 0)
    col = lax.broadcasted_iota(jnp.int32, shape, 1)
    qi = row & (QBLOCK - 1)
    return (col > qi) & (col <= qi + QBLOCK) & ((col >= QBLOCK) | (i > 0))


def _stack_heads(pair0, pair1):
    low = _lane_lt64(pair0.shape)
    zero = jnp.zeros_like(pair0)
    return jnp.concatenate([jnp.where(low, pair0, zero), jnp.where(low, zero, pair0),
                            jnp.where(low, pair1, zero), jnp.where(low, zero, pair1)], axis=0)


def _unstack_heads(stacked):
    low = _lane_lt64((QBLOCK, 128))
    pair0 = jnp.where(low, stacked[0:QBLOCK], stacked[QBLOCK:2 * QBLOCK])
    pair1 = jnp.where(low, stacked[2 * QBLOCK:3 * QBLOCK], stacked[3 * QBLOCK:4 * QBLOCK])
    return pair0, pair1


def _sink_column(sink_ref, kv_head):
    row = lax.broadcasted_iota(jnp.int32, (4 * QBLOCK, 1), 0)
    s = [sink_ref[0, 4 * kv_head + j] for j in range(4)]
    return jnp.where(row < QBLOCK, s[0], jnp.where(row < 2 * QBLOCK, s[1], jnp.where(row < 3 * QBLOCK, s[2], s[3])))


def _band(ref, i):
    prev = pl.multiple_of(jnp.maximum(i - 1, 0) * QBLOCK, QBLOCK)
    own = pl.multiple_of(i * QBLOCK, QBLOCK)
    return jnp.concatenate([ref[pl.ds(prev, QBLOCK), :], ref[pl.ds(own, QBLOCK), :]], axis=0), prev, own


def _softmax_with_sink(s, sink_col):
    m = jnp.maximum(jnp.max(s, axis=-1, keepdims=True), sink_col)
    p = jnp.exp(s - m)
    e_sink = jnp.exp(sink_col - m)
    inv_l = 1.0 / (jnp.sum(p, axis=-1, keepdims=True) + e_sink)
    return p, e_sink, inv_l


def _attention_fwd(q, kd0, kd1, vd0, vd1, sinks, comm=None):
    seq = q.shape[0]

    nb = ATTN_FWD_BLOCKS

    def body(sink_ref, q_ref, kd0_ref, kd1_ref, vd0_ref, vd1_ref, o_ref):
        for b in range(nb):
            i = pl.program_id(0) * nb + b
            rows = slice(QBLOCK * b, QBLOCK * (b + 1))
            valid = _attn_valid(i)
            for kv_head, (k_ref, v_ref) in enumerate(((kd0_ref, vd0_ref), (kd1_ref, vd1_ref))):
                kband, _, _ = _band(k_ref, i)
                vband, _, _ = _band(v_ref, i)
                base = 256 * kv_head
                qm = _stack_heads(q_ref[rows, base:base + 128], q_ref[rows, base + 128:base + 256])
                s = jnp.where(valid, _dot_nt(qm, kband), NEG_INF)
                p, _, inv_l = _softmax_with_sink(s, _sink_column(sink_ref, kv_head))
                o = _dot(p.astype(BF16), vband) * inv_l
                pair0, pair1 = _unstack_heads(o)
                o_ref[rows, base:base + 128] = pair0.astype(BF16)
                o_ref[rows, base + 128:base + 256] = pair1.astype(BF16)

    blk = pl.BlockSpec((nb * QBLOCK, Q_WIDTH), lambda i: (i, 0))
    full = _resident((seq, 128))
    return _pallas(
        body, name="attention_fwd", grid=(seq // (nb * QBLOCK),),
        in_specs=[pl.BlockSpec(memory_space=pltpu.SMEM), blk, full, full, full, full],
        out_specs=[blk], out_shape=[SDS((seq, Q_WIDTH), BF16)],
        operands=(sinks, q, kd0, kd1, vd0, vd1), comm=comm)


HALO = 16


def _conv_parts(gc, xin, gc_halo, xin_halo, conv_w, first):
    tb = gc.shape[0]
    u = gc.astype(F32) * xin.astype(F32)
    u_halo = jnp.where(first, 0.0, gc_halo.astype(F32) * xin_halo.astype(F32))
    ext = jnp.concatenate([u_halo, u], axis=0)
    u1 = pltpu.roll(ext, 1, 0)[HALO:HALO + tb]
    u2 = pltpu.roll(ext, 2, 0)[HALO:HALO + tb]
    y = conv_w[0:1, :] * u2 + conv_w[1:2, :] * u1 + conv_w[2:3, :] * u
    return u, u1, u2, y


def _halo_prev(tb, w):
    return pl.BlockSpec((HALO, w), lambda i: (jnp.maximum(i * (tb // HALO) - 1, 0), 0))


def _residual_mid(x, mix, g_post_mix):
    mix_f = mix.astype(F32)
    return x + mix_f * _rms(mix_f) * g_post_mix


def _mix_out(attn, gb, gc, xin, conv_w, g_attn, g_conv, w_out, comm=None):
    seq = attn.shape[0]
    tb = min(seq, WIDE_TOKEN_TILE)

    def body(a_ref, gb_ref, gc_ref, xin_ref, gch_ref, xinh_ref, cw_ref, ga_ref, gcn_ref, w_ref, mix_ref, mixed_ref):
        first = pl.program_id(0) == 0
        _, _, _, y = _conv_parts(gc_ref[...], xin_ref[...], gch_ref[...], xinh_ref[...], cw_ref[...], first)
        conv = gb_ref[...].astype(F32) * y
        a = a_ref[...].astype(F32)
        mixed_ref[:, 0:Q_WIDTH] = (a * _rms(a) * ga_ref[...]).astype(BF16)
        mixed_ref[:, Q_WIDTH:] = (conv * _rms(conv) * gcn_ref[...]).astype(BF16)
        mix_ref[...] = _dot(mixed_ref[...], w_ref[...].reshape(D_MODEL, D_MODEL)).astype(BF16)

    tile = lambda w: pl.BlockSpec((tb, w), lambda i: (i, 0))
    return _pallas(
        body, name="mix_out", grid=(seq // tb,),
        in_specs=[tile(Q_WIDTH), tile(CONV_WIDTH), tile(CONV_WIDTH), tile(CONV_WIDTH),
                  _halo_prev(tb, CONV_WIDTH), _halo_prev(tb, CONV_WIDTH),
                  _resident((CONV_K, CONV_WIDTH)), _resident((1, Q_WIDTH)), _resident((1, CONV_WIDTH)),
                  _resident(w_out.shape)],
        out_specs=[tile(D_MODEL), tile(D_MODEL)],
        out_shape=[SDS((seq, D_MODEL), BF16), SDS((seq, D_MODEL), BF16)],
        operands=(attn, gb, gc, xin, gc, xin, conv_w, g_attn, g_conv, w_out), comm=comm)


def _mlp_loss(x, mix, target, g_post_mix, g_pre_mlp, g_post_mlp, w_up, w_down):
    seq = x.shape[0]
    tb = TOKEN_TILE

    def body(x_ref, mix_ref, t_ref, gpm_ref, g2_ref, g4_ref, wup_ref, wdown_ref,
             up_ref, hn2_ref, dout_ref, dmlp_ref, loss_ref, dg4_ref, act_ref):
        @pl.when(pl.program_id(0) == 0)
        def _():
            loss_ref[...] = jnp.zeros_like(loss_ref)
            dg4_ref[...] = jnp.zeros_like(dg4_ref)

        halves = [slice(0, tb // 2), slice(tb // 2, tb)]
        hv, hn2 = [], []
        for rows in halves:
            hv.append(_residual_mid(x_ref[rows, :], mix_ref[rows, :], gpm_ref[...]))
            hn2.append((hv[-1] * _rms(hv[-1]) * g2_ref[...]).astype(BF16))
            hn2_ref[rows, :] = hn2[-1]
        for k, rows in enumerate(halves):
            for j in range(N_CHIPS):
                up = _dot(hn2[k], _chip_block(wup_ref, j))
                up = jnp.maximum(up, 0.0)
                up_ref[rows, 1024 * j:1024 * (j + 1)] = up.astype(BF16)
                act_ref[rows, 1024 * j:1024 * (j + 1)] = (up * up).astype(BF16)
        w_down_all = wdown_ref[...].reshape(D_FF, D_MODEL)
        loss = jnp.zeros((1, 1), F32)
        dg4 = jnp.zeros((1, D_MODEL), F32)
        for k, rows in enumerate(halves):
            mlp = _dot(act_ref[rows, :], w_down_all)
            rstd = _rms(mlp)
            zhat = mlp * rstd
            diff = hv[k] + zhat * g4_ref[...] - t_ref[rows, :]
            loss = loss + jnp.sum(jnp.sum(diff * diff, axis=1, keepdims=True), axis=0, keepdims=True)
            dout = diff * (1.0 / D_MODEL)
            dout_ref[rows, :] = dout
            dg4 = dg4 + _colsum(dout * zhat)
            dmlp_ref[rows, :] = _norm_bwd(dout, g4_ref[...], zhat, rstd).astype(BF16)
        loss_ref[...] += loss
        dg4_ref[...] += dg4

    tile = lambda w: pl.BlockSpec((tb, w), lambda i: (i, 0))
    return _pallas(
        body, name="mlp_loss", grid=(seq // tb,),
        in_specs=[tile(D_MODEL), tile(D_MODEL), tile(D_MODEL), _resident((1, D_MODEL)), _resident((1, D_MODEL)),
                  _resident((1, D_MODEL)), _resident(w_up.shape), _resident(w_down.shape)],
        out_specs=[tile(D_FF), tile(D_MODEL), tile(D_MODEL), tile(D_MODEL),
                   pl.BlockSpec((1, 1), lambda i: (0, 0)), pl.BlockSpec((1, D_MODEL), lambda i: (0, 0))],
        out_shape=[SDS((seq, D_FF), BF16), SDS((seq, D_MODEL), BF16), SDS((seq, D_MODEL), F32),
                   SDS((seq, D_MODEL), BF16), SDS((1, 1), F32), SDS((1, D_MODEL), F32)],
        scratch=[pltpu.VMEM((tb, D_FF), BF16)],
        operands=(x, mix, target, g_post_mix, g_pre_mlp, g_post_mlp, w_up, w_down))


def _mlp_bwd(dmlp, up, x, dout, mix, g_pre_mlp, g_post_mix, w_up, w_down):
    seq = x.shape[0]
    tb = MLP_BWD_TOKEN_TILE

    def body(dmlp_ref, up_ref, x_ref, dout_ref, mix_ref, g2_ref, gpm_ref, wup_ref, wdown_ref,
             dup_ref, dh_ref, dmix_ref, dg2_ref, dgpm_ref):
        @pl.when(pl.program_id(0) == 0)
        def _():
            dg2_ref[...] = jnp.zeros_like(dg2_ref)
            dgpm_ref[...] = jnp.zeros_like(dgpm_ref)

        subs = [slice(k * MLP_BWD_SUB_TILE, (k + 1) * MLP_BWD_SUB_TILE) for k in range(tb // MLP_BWD_SUB_TILE)]
        dhn2 = []
        for rows in subs:
            dmlp_v = dmlp_ref[rows, :]
            acc = None
            for j in range(N_CHIPS):
                cols = slice(1024 * j, 1024 * (j + 1))
                dact = _dot_nt(dmlp_v, _chip_block(wdown_ref, j))
                dup = (dact * (2.0 * up_ref[rows, cols].astype(F32))).astype(BF16)
                dup_ref[rows, cols] = dup
                part = _dot_nt(dup, _chip_block(wup_ref, j))
                acc = part if acc is None else acc + part
            dhn2.append(acc)
        dg2 = jnp.zeros((1, D_MODEL), F32)
        dgpm = jnp.zeros((1, D_MODEL), F32)
        for k, rows in enumerate(subs):
            mix_v = mix_ref[rows, :].astype(F32)
            hv = _residual_mid(x_ref[rows, :], mix_ref[rows, :], gpm_ref[...])
            r2 = _rms(hv)
            hhat = hv * r2
            dg2 = dg2 + _colsum(dhn2[k] * hhat)
            dh = dout_ref[rows, :] + _norm_bwd(dhn2[k], g2_ref[...], hhat, r2)
            dh_ref[rows, :] = dh.astype(BF16)
            rz = _rms(mix_v)
            zhat = mix_v * rz
            dgpm = dgpm + _colsum(dh * zhat)
            dmix_ref[rows, :] = _norm_bwd(dh, gpm_ref[...], zhat, rz).astype(BF16)
        dg2_ref[...] += dg2
        dgpm_ref[...] += dgpm

    tile = lambda w: pl.BlockSpec((tb, w), lambda i: (i, 0))
    vec = pl.BlockSpec((1, D_MODEL), lambda i: (0, 0))
    return _pallas(
        body, name="mlp_bwd", grid=(seq // tb,),
        in_specs=[tile(D_MODEL), tile(D_FF), tile(D_MODEL), tile(D_MODEL), tile(D_MODEL),
                  _resident((1, D_MODEL)), _resident((1, D_MODEL)), _resident(w_up.shape), _resident(w_down.shape)],
        out_specs=[tile(D_FF), tile(D_MODEL), tile(D_MODEL), vec, vec],
        out_shape=[SDS((seq, D_FF), BF16), SDS((seq, D_MODEL), BF16), SDS((seq, D_MODEL), BF16),
                   SDS((1, D_MODEL), F32), SDS((1, D_MODEL), F32)],
        operands=(dmlp, up, x, dout, mix, g_pre_mlp, g_post_mix, w_up, w_down))


def _mix_bwd(dmix, attn, gb, gc, xin, conv_w, g_attn, g_conv, w_out, n_k):
    seq = attn.shape[0]
    tb = seq // (N_CHIPS * n_k)

    def body(first, dmix_ref, a_ref, gb_ref, gc_ref, xin_ref, gch_ref, xinh_ref, cw_ref, ga_ref, gcn_ref, w_ref,
             dattn_ref, dgb_ref, dy_ref, dga_ref, dgcn_ref, dcw_ref):
        @pl.when(first)
        def _():
            dga_ref[...] = jnp.zeros_like(dga_ref)
            dgcn_ref[...] = jnp.zeros_like(dgcn_ref)
            dcw_ref[...] = jnp.zeros_like(dcw_ref)

        dmixed = _dot_nt(dmix_ref[...], w_ref[...].reshape(D_MODEL, D_MODEL))
        a = a_ref[...].astype(F32)
        ra = _rms(a)
        ahat = a * ra
        dan = dmixed[:, 0:Q_WIDTH]
        dga_ref[...] += _colsum(dan * ahat)
        dattn_ref[...] = _norm_bwd(dan, ga_ref[...], ahat, ra).astype(BF16)
        gbv = gb_ref[...].astype(F32)
        u, u1, u2, y = _conv_parts(gc_ref[...], xin_ref[...], gch_ref[...], xinh_ref[...], cw_ref[...], first)
        conv = gbv * y
        rc = _rms(conv)
        chat = conv * rc
        dcn = dmixed[:, Q_WIDTH:]
        dgcn_ref[...] += _colsum(dcn * chat)
        dconv = _norm_bwd(dcn, gcn_ref[...], chat, rc)
        dgb_ref[...] = (dconv * y).astype(BF16)
        dy = dconv * gbv
        dy_ref[...] = dy.astype(BF16)
        dcw_ref[0:1, :] += _colsum(dy * u2)
        dcw_ref[1:2, :] += _colsum(dy * u1)
        dcw_ref[2:3, :] += _colsum(dy * u)

    tile = lambda w: pl.BlockSpec((tb, w), lambda j, k: (j * n_k + k, 0))
    halo = lambda w: pl.BlockSpec((HALO, w), lambda j, k: (jnp.maximum((j * n_k + k) * (tb // HALO) - 1, 0), 0))
    whole = lambda shape: pl.BlockSpec(shape, lambda j, k: (0,) * len(shape))
    return _Rider(
        body,
        in_specs=[tile(D_MODEL), tile(Q_WIDTH), tile(CONV_WIDTH), tile(CONV_WIDTH), tile(CONV_WIDTH),
                  halo(CONV_WIDTH), halo(CONV_WIDTH),
                  _resident((CONV_K, CONV_WIDTH)), _resident((1, Q_WIDTH)), _resident((1, CONV_WIDTH)),
                  _resident(w_out.shape)],
        out_specs=[tile(Q_WIDTH), tile(CONV_WIDTH), tile(CONV_WIDTH),
                   whole((1, Q_WIDTH)), whole((1, CONV_WIDTH)), whole((CONV_K, CONV_WIDTH))],
        out_shape=[SDS((seq, Q_WIDTH), BF16), SDS((seq, CONV_WIDTH), BF16), SDS((seq, CONV_WIDTH), BF16),
                   SDS((1, Q_WIDTH), F32), SDS((1, CONV_WIDTH), F32), SDS((CONV_K, CONV_WIDTH), F32)],
        operands=(dmix, attn, gb, gc, xin, gc, xin, conv_w, g_attn, g_conv, w_out))


def _attention_bwd(q, dattn, attn, kd0, kd1, vd0, vd1, sinks, comm=None):
    seq = q.shape[0]
    nb = ATTN_BWD_BLOCKS

    def body(sink_ref, q_ref, do_ref, o_ref, kd0_ref, kd1_ref, vd0_ref, vd1_ref,
             dq_ref, dk0_ref, dk1_ref, dv0_ref, dv1_ref, dsink_ref):
        @pl.when(pl.program_id(0) == 0)
        def _():
            for r in (dk0_ref, dk1_ref, dv0_ref, dv1_ref, dsink_ref):
                r[...] = jnp.zeros_like(r)

        lane = lax.broadcasted_iota(jnp.int32, (1, 128), 1)
        dsink = jnp.zeros((1, 128), F32)
        for b in range(nb):
            i = pl.program_id(0) * nb + b
            rows = slice(QBLOCK * b, QBLOCK * (b + 1))
            valid = _attn_valid(i)
            for kv_head, (k_ref, v_ref, dk_ref, dv_ref) in enumerate(
                    ((kd0_ref, vd0_ref, dk0_ref, dv0_ref), (kd1_ref, vd1_ref, dk1_ref, dv1_ref))):
                kband, prev, own = _band(k_ref, i)
                vband, _, _ = _band(v_ref, i)
                base = 256 * kv_head
                qm = _stack_heads(q_ref[rows, base:base + 128], q_ref[rows, base + 128:base + 256])
                dom = _stack_heads(do_ref[rows, base:base + 128], do_ref[rows, base + 128:base + 256])
                om = _stack_heads(o_ref[rows, base:base + 128], o_ref[rows, base + 128:base + 256])
                s = jnp.where(valid, _dot_nt(qm, kband), NEG_INF)
                p, e_sink, inv_l = _softmax_with_sink(s, _sink_column(sink_ref, kv_head))
                p = p * inv_l
                delta = jnp.sum(dom.astype(F32) * om.astype(F32), axis=-1, keepdims=True)
                ds = (p * (_dot_nt(dom, vband) - delta)).astype(BF16)
                sink_term = -(e_sink * inv_l) * delta
                for j in range(4):
                    part = jnp.sum(sink_term[QBLOCK * j:QBLOCK * (j + 1)], axis=0, keepdims=True)
                    dsink = dsink + jnp.where(lane == 4 * kv_head + j, part, 0.0)
                pair0, pair1 = _unstack_heads(_dot(ds, kband))
                dq_ref[rows, base:base + 128] = pair0.astype(BF16)
                dq_ref[rows, base + 128:base + 256] = pair1.astype(BF16)
                dkd = _dot_tn(ds, qm)
                dkd = dkd + pltpu.roll(dkd, HEAD_DIM, 1)
                dvd = _dot_tn(p.astype(BF16), dom)
                dvd = dvd + pltpu.roll(dvd, HEAD_DIM, 1)
                dk_ref[pl.ds(prev, QBLOCK), :] += dkd[0:QBLOCK]
                dk_ref[pl.ds(own, QBLOCK), :] += dkd[QBLOCK:]
                dv_ref[pl.ds(prev, QBLOCK), :] += dvd[0:QBLOCK]
                dv_ref[pl.ds(own, QBLOCK), :] += dvd[QBLOCK:]
        dsink_ref[...] += dsink

    blk = pl.BlockSpec((nb * QBLOCK, Q_WIDTH), lambda i: (i, 0))
    full = _resident((seq, 128))
    acc = pl.BlockSpec((seq, 128), lambda i: (0, 0))
    return _pallas(
        body, name="attention_bwd", grid=(seq // (nb * QBLOCK),),
        in_specs=[pl.BlockSpec(memory_space=pltpu.SMEM), blk, blk, blk, full, full, full, full],
        out_specs=[blk, acc, acc, acc, acc, pl.BlockSpec((1, 128), lambda i: (0, 0))],
        out_shape=[SDS((seq, Q_WIDTH), BF16)] + [SDS((seq, 128), F32)] * 4 + [SDS((1, 128), F32)],
        operands=(sinks, q, dattn, attn, kd0, kd1, vd0, vd1), comm=comm)


def _in_proj_bwd(dq, dk0, dk1, dv0, dv1, dgb, dy, gc, xin, conv_w, x, dh, g_pre, w_in_t, rope):
    seq = x.shape[0]
    tb = min(seq, WIDE_TOKEN_TILE)
    n_tiles = seq // tb

    def body(dq_ref, dk0_ref, dk1_ref, dv0_ref, dv1_ref, dgb_ref, dy_ref, dyh_ref, gc_ref, xin_ref, cw_ref,
             x_ref, dh_ref, g_ref, w_ref, c_ref, sa_ref, sb_ref,
             dproj_ref, gx_ref, dg_ref):
        i = pl.program_id(0)

        @pl.when(i == 0)
        def _():
            dg_ref[...] = jnp.zeros_like(dg_ref)

        dy = dy_ref[...].astype(F32)
        ext = jnp.concatenate([dy, jnp.where(i == n_tiles - 1, 0.0, dyh_ref[...].astype(F32))], axis=0)
        dy1 = pltpu.roll(ext, tb + HALO - 1, 0)[0:tb]
        dy2 = pltpu.roll(ext, tb + HALO - 2, 0)[0:tb]
        cw = cw_ref[...]
        du = cw[2:3, :] * dy + cw[1:2, :] * dy1 + cw[0:1, :] * dy2
        scale = 1.0 / math.sqrt(HEAD_DIM)
        base = Q_WIDTH + 2 * KV_WIDTH
        halves = [slice(0, tb // 2), slice(tb // 2, tb)]
        low = _lane_lt64((tb // 2, 128))
        for rows in halves:
            c, sa, sb = _rope_tile(c_ref.at[rows, :], sa_ref, sb_ref)
            for p in range(Q_WIDTH // 128):
                dproj_ref[rows, 128 * p:128 * (p + 1)] = _rope_transposed(
                    dq_ref[rows, 128 * p:128 * (p + 1)].astype(F32) * scale, c, sa, sb).astype(BF16)
            dk = jnp.where(low, dk0_ref[rows, :], dk1_ref[rows, :])
            dproj_ref[rows, Q_WIDTH:Q_WIDTH + KV_WIDTH] = _rope_transposed(dk, c, sa, sb).astype(BF16)
            dproj_ref[rows, Q_WIDTH + KV_WIDTH:base] = jnp.where(low, dv0_ref[rows, :], dv1_ref[rows, :]).astype(BF16)
            dproj_ref[rows, base:base + CONV_WIDTH] = dgb_ref[rows, :]
            dproj_ref[rows, base + CONV_WIDTH:base + 2 * CONV_WIDTH] = (du[rows] * xin_ref[rows, :].astype(F32)).astype(BF16)
            dproj_ref[rows, base + 2 * CONV_WIDTH:] = (du[rows] * gc_ref[rows, :].astype(F32)).astype(BF16)
        w_all = w_ref[...].reshape(IN_COLS, D_MODEL)
        dhn = [_dot(dproj_ref[rows, :], w_all) for rows in halves]
        dg = jnp.zeros((1, D_MODEL), F32)
        for k, rows in enumerate(halves):
            xv = x_ref[rows, :]
            r = _rms(xv)
            xhat = xv * r
            dg = dg + _colsum(dhn[k] * xhat)
            gx_ref[rows, :] = dh_ref[rows, :].astype(F32) + _norm_bwd(dhn[k], g_ref[...], xhat, r)
        dg_ref[...] += dg

    tile = lambda w: pl.BlockSpec((tb, w), lambda i: (i, 0))
    halo_next = pl.BlockSpec((HALO, CONV_WIDTH), lambda i: (jnp.minimum((i + 1) * (tb // HALO), seq // HALO - 1), 0))
    return _pallas(
        body, name="in_proj_bwd", grid=(n_tiles,),
        in_specs=[tile(Q_WIDTH), tile(128), tile(128), tile(128), tile(128), tile(CONV_WIDTH), tile(CONV_WIDTH), halo_next,
                  tile(CONV_WIDTH), tile(CONV_WIDTH), _resident((CONV_K, CONV_WIDTH)),
                  tile(D_MODEL), tile(D_MODEL), _resident((1, D_MODEL)), _resident(w_in_t.shape), *_rope_specs(tb)],
        out_specs=[tile(IN_COLS), tile(D_MODEL), pl.BlockSpec((1, D_MODEL), lambda i: (0, 0))],
        out_shape=[SDS((seq, IN_COLS), BF16), SDS((seq, D_MODEL), F32), SDS((1, D_MODEL), F32)],
        operands=(dq, dk0, dk1, dv0, dv1, dgb, dy, dy, gc, xin, conv_w, x, dh, g_pre, w_in_t, *rope))


def _wgrad_grid(seq, per_chip, h_rows):
    chips_per_step = 1 if per_chip else N_CHIPS
    m = chips_per_step * 2 * h_rows
    bt = min(seq, WGRAD_TOKEN_TILE)
    return chips_per_step, m, bt, seq // bt


def _wgrad(name, a, b, *, per_chip, h_rows, square_a=False, comm=None, rider=None):
    seq = a.shape[0]
    chips_per_step, m, bt, n_k = _wgrad_grid(seq, per_chip, h_rows)
    a_cols = m if per_chip else a.shape[1]
    a_wide = a.shape[1] > a_cols
    b_wide = b.shape[1] > D_MODEL

    def body(a_ref, b_ref, g_ref):
        @pl.when(pl.program_id(1) == 0)
        def _():
            g_ref[...] = jnp.zeros_like(g_ref)

        av = a_ref[...]
        if square_a:
            av = (av.astype(F32) * av.astype(F32)).astype(BF16)
        g_ref[...] += _dot_tn(av, b_ref[...]).reshape(g_ref.shape)

    a_spec = pl.BlockSpec((bt, a_cols), (lambda j, k: (k, j)) if a_wide else (lambda j, k: (k, 0)))
    b_spec = pl.BlockSpec((bt, D_MODEL), (lambda j, k: (k, j)) if b_wide else (lambda j, k: (k, 0)))
    g_spec = pl.BlockSpec((chips_per_step, 2, h_rows, D_MODEL), lambda j, k: (j, 0, 0, 0),
                          pipeline_mode=None if per_chip else pl.Buffered(1))
    return _pallas(
        body, name=name, grid=(N_CHIPS if per_chip else 1, n_k),
        in_specs=[a_spec, b_spec], out_specs=[g_spec], out_shape=[SDS((N_CHIPS, 2, h_rows, D_MODEL), F32)],
        operands=(a, b), comm=comm, rider=rider)


def _adamw_math(w, g, m, v):
    m = ADAM_B1 * m + (1.0 - ADAM_B1) * g
    v = ADAM_B2 * v + (1.0 - ADAM_B2) * (g * g)
    m_hat = m / (1.0 - ADAM_B1 ** ADAM_STEP)
    v_hat = v / (1.0 - ADAM_B2 ** ADAM_STEP)
    delta = -ADAM_LR * (m_hat / (jnp.sqrt(v_hat) + ADAM_EPS) + ADAM_WD * w)
    return delta, m, v


def _adamw_rows(name, reduced, w, m, v, rt):
    per_half = reduced.shape[1] // rt

    def body(r_ref, w_ref, m_ref, v_ref, g_out, d_out, m_out, v_out):
        g = r_ref[0]
        g_out[...] = g
        d_out[...], m_out[...], v_out[...] = _adamw_math(w_ref[...], g, m_ref[...], v_ref[...])

    blk = pl.BlockSpec((rt, D_MODEL), lambda h, r: (h * per_half + r, 0))
    return _pallas(
        body, name=name, grid=(2, per_half),
        in_specs=[pl.BlockSpec((1, rt, D_MODEL), lambda h, r: (h, r, 0)), blk, blk, blk],
        out_specs=[blk, blk, blk, blk], out_shape=[SDS(w.shape, F32)] * 4, operands=(reduced, w, m, v))


def _adamw_small(packed_grads, w, m, v):
    names = SMALL_NAMES
    n = len(names)
    conv_local = w["conv_w"].shape[-1]

    def body(*refs):
        gp = refs[0]
        w_refs, m_refs, v_refs = refs[1:1 + n], refs[1 + n:1 + 2 * n], refs[1 + 2 * n:1 + 3 * n]
        outs = refs[1 + 3 * n:]
        g_out, d_out, m_out, v_out = outs[0:n], outs[n:2 * n], outs[2 * n:3 * n], outs[3 * n:4 * n]
        chip = 2 * lax.axis_index("x") + lax.axis_index("y")

        def step(k, g, index=None):
            pick = (lambda r: r[...]) if index is None else (lambda r: r[index])
            d, new_m, new_v = _adamw_math(pick(w_refs[k]), g, pick(m_refs[k]), pick(v_refs[k]))
            for ref, val in ((g_out[k], g), (d_out[k], d), (m_out[k], new_m), (v_out[k], new_v)):
                if index is None:
                    ref[...] = val
                else:
                    ref[index] = val

        for k, name in enumerate(names):
            if name in SMALL_VECTORS:
                step(k, gp[SMALL_VECTORS.index(name):SMALL_VECTORS.index(name) + 1, :])
            elif name == "attn_group_norm":
                step(k, gp[4:5, 0:Q_WIDTH])
            elif name == "conv_group_norm":
                step(k, gp[4:5, Q_WIDTH:])
            elif name == "attn_sinks":
                step(k, gp[7:8, 0:8])
            else:
                for t in range(CONV_K):
                    row, base = 5 + t // 2, CONV_WIDTH * (t % 2)
                    g = gp[row:row + 1, base:base + conv_local]
                    for j in range(1, CONV_WIDTH // conv_local):
                        g = jnp.where(chip == j, gp[row:row + 1, base + conv_local * j:base + conv_local * (j + 1)], g)
                    step(k, g, index=(0, slice(t, t + 1), slice(None)))

    shapes = [SDS(w[name].shape, F32) for name in names]
    res = pl.pallas_call(
        body, name="adamw_small", in_specs=[VMEM_WHOLE] * (1 + 3 * n), out_specs=[VMEM_WHOLE] * (4 * n),
        out_shape=shapes * 4,
    )(packed_grads, *[w[k] for k in names], *[m[k] for k in names], *[v[k] for k in names])
    return [dict(zip(names, res[i * n:(i + 1) * n])) for i in range(4)]


SMALL_VECTORS = ("pre_mix_norm", "post_mix_norm", "pre_mlp_norm", "post_mlp_norm")
SMALL_NAMES = SMALL_VECTORS + ("attn_group_norm", "conv_group_norm", "conv_w", "attn_sinks")


def _pack_small(p):
    rows = [p[n].reshape(1, D_MODEL) for n in SMALL_VECTORS]
    rows.append(jnp.concatenate([p["attn_group_norm"].reshape(1, -1), p["conv_group_norm"].reshape(1, -1)], axis=1))
    cw = p["conv_w"].reshape(CONV_K, -1)
    rows.append(jnp.pad(cw, ((0, 1), (0, CONV_WIDTH - cw.shape[1]))).reshape(2, D_MODEL))
    last = jnp.concatenate([p["attn_sinks"].reshape(1, 8), p.get("loss_sum", jnp.zeros((1, 1), F32))], axis=1)
    rows.append(jnp.pad(last, ((0, 0), (0, D_MODEL - 9))))
    return jnp.concatenate(rows, axis=0)


WEIGHT_ORDER = ("pre_mix_norm", "w_in", "conv_w", "attn_sinks", "attn_group_norm", "conv_group_norm", "w_out",
                "post_mix_norm", "pre_mlp_norm", "w_up", "w_down", "post_mlp_norm")


def kernel(x, pre_mix_norm, w_in, conv_w, attn_sinks, attn_group_norm, conv_group_norm, w_out, post_mix_norm, pre_mlp_norm, w_up, w_down, post_mlp_norm, loss_target, m_pre_mix_norm, m_w_in, m_conv_w, m_attn_sinks, m_attn_group_norm, m_conv_group_norm, m_w_out, m_post_mix_norm, m_pre_mlp_norm, m_w_up, m_w_down, m_post_mlp_norm, v_pre_mix_norm, v_w_in, v_conv_w, v_attn_sinks, v_attn_group_norm, v_conv_group_norm, v_w_out, v_post_mix_norm, v_pre_mlp_norm, v_w_up, v_w_down, v_post_mlp_norm):
    w = dict(pre_mix_norm=pre_mix_norm, w_in=w_in, conv_w=conv_w, attn_sinks=attn_sinks, attn_group_norm=attn_group_norm,
             conv_group_norm=conv_group_norm, w_out=w_out, post_mix_norm=post_mix_norm, pre_mlp_norm=pre_mlp_norm,
             w_up=w_up, w_down=w_down, post_mlp_norm=post_mlp_norm)
    m = dict(pre_mix_norm=m_pre_mix_norm, w_in=m_w_in, conv_w=m_conv_w, attn_sinks=m_attn_sinks,
             attn_group_norm=m_attn_group_norm, conv_group_norm=m_conv_group_norm, w_out=m_w_out,
             post_mix_norm=m_post_mix_norm, pre_mlp_norm=m_pre_mlp_norm, w_up=m_w_up, w_down=m_w_down,
             post_mlp_norm=m_post_mlp_norm)
    v = dict(pre_mix_norm=v_pre_mix_norm, w_in=v_w_in, conv_w=v_conv_w, attn_sinks=v_attn_sinks,
             attn_group_norm=v_attn_group_norm, conv_group_norm=v_conv_group_norm, w_out=v_w_out,
             post_mix_norm=v_post_mix_norm, pre_mlp_norm=v_pre_mlp_norm, w_up=v_w_up, w_down=v_w_down,
             post_mlp_norm=v_post_mlp_norm)
    core = lax.axis_index("c").astype(jnp.int32).reshape(1)
    xs, target = x[0], loss_target[0]
    rope = _rope_inputs(xs.shape[0])

    hb_up, hb_down, hb_out, hb_in = _cast_halves(core, w_up[0], w_down[0], w_out[0], w_in[0].T)
    conv_pad = jnp.pad(conv_w[0], ((0, 8 - CONV_K), (0, 0)))
    wf_in, conv_all = _gather_whole(hb_in, conv_pad)
    conv_full = conv_all[:, :CONV_K, :].transpose(1, 0, 2).reshape(CONV_K, CONV_WIDTH)

    whole_up, early, late = (0, H_UP), (0, DOWN_EARLY_ROWS), (DOWN_EARLY_ROWS, H_DOWN - DOWN_EARLY_ROWS)
    *proj, wf_up, wf_out, wf_down = _in_proj(
        xs, pre_mix_norm, wf_in, rope,
        comm=_merge(_relay(hb_up, None, first=whole_up), _gather_first(hb_out), _relay(hb_down, None, first=early)))
    q, kd0, kd1, vd0, vd1, gb, gc, xin, hn = proj
    attn, wf_up, wf_out, wf_down = _attention_fwd(
        q, kd0, kd1, vd0, vd1, attn_sinks,
        comm=_merge(_relay(None, wf_up, second=whole_up), _gather_second(wf_out),
                    _relay(hb_down, wf_down, first=late, second=early)))
    mix, mixed, wf_up, wf_down = _mix_out(
        attn, gb, gc, xin, conv_full, attn_group_norm, conv_group_norm, wf_out,
        comm=_merge(_relay(None, wf_up, third=whole_up), _relay(None, wf_down, second=late, third=early, third_after=late)))
    up, hn2, dout, dmlp, loss_sum, dg_post_mlp = _mlp_loss(xs, mix, target, post_mix_norm, pre_mlp_norm, post_mlp_norm,
                                                           wf_up, wf_down)

    dup, dh, dmix, dg_pre_mlp, dg_post_mix = _mlp_bwd(dmlp, up, xs, dout, mix, pre_mlp_norm, post_mix_norm, wf_up, wf_down)
    n_k = _wgrad_grid(xs.shape[0], True, H_DOWN)[3]
    g_down, dattn, dgb, dy, dg_attn, dg_conv, dconv_w = _wgrad(
        "wgrad_down", up, dmlp, per_chip=True, h_rows=H_DOWN, square_a=True,
        rider=_mix_bwd(dmix, attn, gb, gc, xin, conv_full, attn_group_norm, conv_group_norm, wf_out, n_k))
    g_up, got_down = _wgrad("wgrad_up", hn2, dup, per_chip=True, h_rows=H_UP, comm=_pair_send(g_down))
    p_down = _pair_sum("pair_sum_down", core, g_down, got_down)
    g_out, got_up = _wgrad("wgrad_out", mixed, dmix, per_chip=False, h_rows=H_OUT, comm=_pair_send(g_up))
    p_up = _pair_sum("pair_sum_up", core, g_up, got_up)
    dq, dk0, dk1, dv0, dv1, dsink, ex_down, ex_up, got_out = _attention_bwd(
        q, dattn, attn, kd0, kd1, vd0, vd1, attn_sinks,
        comm=_merge(_chip_exchange(p_down), _chip_exchange(p_up), _pair_send(g_out)))
    p_out = _pair_sum("pair_sum_out", core, g_out, got_out)
    dproj, grad_x, dg_pre_mix = _in_proj_bwd(dq, dk0, dk1, dv0, dv1, dgb, dy, gc, xin, conv_full, xs, dh, pre_mix_norm,
                                             wf_in, rope)
    g_in, ex_out = _wgrad("wgrad_in", dproj, hn, per_chip=False, h_rows=H_IN, comm=_chip_exchange(p_out))
    small = dict(pre_mix_norm=dg_pre_mix, conv_w=dconv_w, attn_sinks=dsink[:, :8], attn_group_norm=dg_attn,
                 conv_group_norm=dg_conv, post_mix_norm=dg_post_mix, pre_mlp_norm=dg_pre_mlp, post_mlp_norm=dg_post_mlp,
                 loss_sum=loss_sum)
    r_down, r_up, r_out, r_in, small_total = _tail_reduce(g_in, [ex_down, ex_up, ex_out], _pack_small(small))

    out_g, out_d, out_m, out_v = {}, {}, {}, {}
    out_g["w_up"], out_d["w_up"], out_m["w_up"], out_v["w_up"] = _adamw_rows(
        "adamw_up", r_up, w_up[0], m_w_up[0], v_w_up[0], 256)
    out_g["w_down"], out_d["w_down"], out_m["w_down"], out_v["w_down"] = _adamw_rows(
        "adamw_down", r_down, w_down[0], m_w_down[0], v_w_down[0], 256)
    out_g["w_out"], out_d["w_out"], out_m["w_out"], out_v["w_out"] = _adamw_rows(
        "adamw_out", r_out, w_out[0], m_w_out[0], v_w_out[0], H_OUT)
    in_t = _adamw_rows("adamw_in", r_in, w_in[0].T, m_w_in[0].T, v_w_in[0].T, H_IN)
    out_g["w_in"], out_d["w_in"], out_m["w_in"], out_v["w_in"] = [t.T for t in in_t]

    loss = small_total[7, 8] * (0.5 / D_MODEL)
    for out, part in zip((out_g, out_d, out_m, out_v), _adamw_small(small_total, w, m, v)):
        out.update(part)

    def shaped(d):
        return [d[n].reshape(w[n].shape) for n in WEIGHT_ORDER]

    return (loss, grad_x[None], *shaped(out_g), *shaped(out_d), *shaped(out_m), *shaped(out_v))
```

```python
import math
from typing import Callable, NamedTuple

import jax
import jax.numpy as jnp
import numpy as np
from jax import lax
from jax.experimental import pallas as pl
from jax.experimental.pallas import tpu as pltpu

F32 = jnp.float32
BF16 = jnp.bfloat16

D_MODEL = 1024
HEAD_DIM = 64
Q_WIDTH = 512
KV_WIDTH = 128
CONV_WIDTH = 512
CONV_K = 3
D_FF = 4096
IN_COLS = 2304
QBLOCK = 128
ROT_DIM = 16
ROPE_THETA = 500000.0
NORM_EPS = 1e-6
NEG_INF = -1e30
N_CHIPS = 4

ADAM_LR = 0.001
ADAM_B1 = 0.9
ADAM_B2 = 0.999
ADAM_EPS = 1e-08
ADAM_WD = 0.01
ADAM_STEP = 10

H_UP, H_DOWN, H_OUT, H_IN = 512, 512, 128, 288
DOWN_EARLY_ROWS = 224
UP_EXCHANGE_EARLY_ROWS = 256

TOKEN_TILE = 512
WIDE_TOKEN_TILE = 1024
MLP_BWD_TOKEN_TILE = 512
MLP_BWD_SUB_TILE = 256
ATTN_FWD_BLOCKS = 16
ATTN_BWD_BLOCKS = 2
WGRAD_TOKEN_TILE = 2048
VMEM_LIMIT_V7X = 56 * 1024 * 1024

MESH = pl.DeviceIdType.MESH
ANY = pl.BlockSpec(memory_space=pl.ANY)
VMEM_WHOLE = pl.BlockSpec(memory_space=pltpu.VMEM)
SDS = jax.ShapeDtypeStruct


def _resident(shape):
    zeros = (0,) * len(shape)
    return pl.BlockSpec(shape, lambda *_: zeros, pipeline_mode=pl.Buffered(1))


def _rms(v):
    return lax.rsqrt(jnp.mean(v * v, axis=-1, keepdims=True) + NORM_EPS)


def _norm_bwd(dy, gain, vhat, rstd):
    t = dy * gain
    return rstd * (t - vhat * jnp.mean(t * vhat, axis=-1, keepdims=True))


def _colsum(v):
    return jnp.sum(v, axis=0, keepdims=True)


def _dot_nt(a, b):
    return lax.dot_general(a, b, (((1,), (1,)), ((), ())), preferred_element_type=F32)


def _dot_tn(a, b):
    return lax.dot_general(a, b, (((0,), (0,)), ((), ())), preferred_element_type=F32)


def _dot(a, b):
    return jnp.dot(a, b, preferred_element_type=F32)


def _chip_block(w_ref, chip):
    both = w_ref[pl.ds(2 * chip, 2)]
    return both.reshape(2 * both.shape[1], both.shape[2])


def _lane_lt64(shape):
    return lax.broadcasted_iota(jnp.int32, shape, 1) < HEAD_DIM


class _Comm(NamedTuple):
    operands: tuple
    out_shapes: tuple
    aliases: dict
    n_remote: int
    n_local: int
    plan: Callable
    after: Callable = None


def _merge(*comms):
    operands, out_shapes, aliases, parts = [], [], {}, []
    n_remote = n_local = 0
    for cm in comms:
        parts.append((len(operands), len(out_shapes), n_remote, n_local, cm))
        for k, v in cm.aliases.items():
            aliases[len(operands) + k] = len(out_shapes) + v
        operands += cm.operands
        out_shapes += cm.out_shapes
        n_remote += cm.n_remote
        n_local += cm.n_local

    def run(which, ins, outs, send, recv, loc):
        sends, recvs, locs = [], [], []
        for i0, o0, r0, l0, cm in parts:
            stage = getattr(cm, which)
            if stage is not None:
                s, r, l = stage(ins[i0:i0 + len(cm.operands)], outs[o0:o0 + len(cm.out_shapes)],
                                lambda k, r0=r0: send(r0 + k), lambda k, r0=r0: recv(r0 + k), lambda k, l0=l0: loc(l0 + k))
                sends, recvs, locs = sends + s, recvs + r, locs + l
        return sends, recvs, locs

    def plan(*args):
        return run("plan", *args)

    def after(*args):
        return run("after", *args)

    return _Comm(tuple(operands), tuple(out_shapes), aliases, n_remote, n_local, plan,
                 after if any(cm.after is not None for cm in comms) else None)


def _sem_scratch(comm):
    return [pltpu.SemaphoreType.DMA((max(comm.n_remote, 1),)), pltpu.SemaphoreType.DMA((max(comm.n_remote, 1),)),
            pltpu.SemaphoreType.DMA((max(comm.n_local, 1),))]


class _Rider(NamedTuple):
    body: Callable
    in_specs: list
    out_specs: list
    out_shape: list
    operands: tuple


def _pallas(body, *, name, grid, in_specs, out_specs, out_shape, operands, scratch=(), comm=None, rider=None):
    params = pltpu.CompilerParams(dimension_semantics=("arbitrary",) * len(grid), vmem_limit_bytes=VMEM_LIMIT_V7X)
    if rider is not None:
        own_in, own_out, ride_in, ride_out = len(in_specs), len(out_specs), len(rider.in_specs), len(rider.out_specs)
        own_body = body

        def body(*refs):
            o0 = own_in + ride_in
            s0 = o0 + own_out + ride_out
            own_body(*refs[:own_in], *refs[o0:o0 + own_out], *refs[s0:])
            first = None
            for axis in range(len(grid)):
                at_start = pl.program_id(axis) == 0
                first = at_start if first is None else jnp.logical_and(first, at_start)
            rider.body(first, *refs[own_in:o0], *refs[o0 + own_out:s0])

        in_specs, out_specs = list(in_specs) + rider.in_specs, list(out_specs) + rider.out_specs
        out_shape, operands = list(out_shape) + rider.out_shape, tuple(operands) + tuple(rider.operands)
    if comm is None:
        return pl.pallas_call(body, name=name, grid=grid, in_specs=in_specs, out_specs=out_specs, out_shape=out_shape,
                              scratch_shapes=list(scratch), compiler_params=params)(*operands)
    n_in, n_out, n_scr = len(in_specs), len(out_specs), len(scratch)
    c_in, c_out = len(comm.operands), len(comm.out_shapes)

    def with_comm(*refs):
        ins, c_ins = refs[:n_in], refs[n_in:n_in + c_in]
        o0 = n_in + c_in
        outs, c_outs = refs[o0:o0 + n_out], refs[o0 + n_out:o0 + n_out + c_out]
        s0 = o0 + n_out + c_out
        scr = refs[s0:s0 + n_scr]
        send_sems, recv_sems, local_sems = refs[s0 + n_scr:]
        first = last = None
        for axis, size in enumerate(grid):
            at_start, at_end = pl.program_id(axis) == 0, pl.program_id(axis) == size - 1
            first = at_start if first is None else jnp.logical_and(first, at_start)
            last = at_end if last is None else jnp.logical_and(last, at_end)

        def copies():
            return comm.plan(c_ins, c_outs, lambda k: send_sems.at[k], lambda k: recv_sems.at[k],
                             lambda k: local_sems.at[k])

        @pl.when(first)
        def _():
            sends, _, locs = copies()
            for cp in sends + locs:
                cp.start()

        body(*ins, *outs, *scr)

        @pl.when(last)
        def _():
            sends, recvs, locs = copies()
            for cp in recvs:
                cp.wait_recv()
            for cp in sends:
                cp.wait_send()
            for cp in locs:
                cp.wait()
            if comm.after is not None:
                sends, recvs, _ = comm.after(c_ins, c_outs, lambda k: send_sems.at[k], lambda k: recv_sems.at[k],
                                             lambda k: local_sems.at[k])
                for cp in sends:
                    cp.start()
                for cp in recvs:
                    cp.wait_recv()
                for cp in sends:
                    cp.wait_send()

    return pl.pallas_call(
        with_comm, name=name, grid=grid,
        in_specs=list(in_specs) + [ANY] * c_in, out_specs=list(out_specs) + [ANY] * c_out,
        out_shape=list(out_shape) + list(comm.out_shapes),
        scratch_shapes=list(scratch) + _sem_scratch(comm),
        input_output_aliases={n_in + k: n_out + v for k, v in comm.aliases.items()},
        compiler_params=params)(*operands, *comm.operands)


def _place():
    return lax.axis_index("x"), lax.axis_index("y"), lax.axis_index("c")


def _other_chips(x, y):
    return [(1 - x, y), (x, 1 - y), (1 - x, 1 - y)]


def _slot(px, py, pc):
    return 4 * px + 2 * py + pc


def _remote(src, dst, send_sem, recv_sem, to):
    return pltpu.make_async_remote_copy(src_ref=src, dst_ref=dst, send_sem=send_sem, recv_sem=recv_sem,
                                        device_id=to, device_id_type=MESH)


def _gather_first(half_block):
    def plan(ins, outs, send, recv, loc):
        (blk,), (full,) = ins, outs
        x, y, c = _place()
        chips = _other_chips(x, y)
        mine = full.at[_slot(x, y, c)]
        sends = [_remote(blk, mine, send(0), recv(0), (x, y, 1 - c))]
        sends += [_remote(blk, mine, send(1 + j), recv(1 + j), (*chip, c)) for j, chip in enumerate(chips)]
        recvs = [_remote(blk, full.at[_slot(x, y, 1 - c)], send(0), recv(0), (x, y, 1 - c))]
        recvs += [_remote(blk, full.at[_slot(*chip, c)], send(1 + j), recv(1 + j), (*chip, c))
                  for j, chip in enumerate(chips)]
        return sends, recvs, [pltpu.make_async_copy(blk, mine, loc(0))]

    return _Comm((half_block,), (SDS((2 * N_CHIPS,) + half_block.shape, half_block.dtype),), {}, 4, 1, plan)


def _gather_second(partly_gathered):
    def plan(ins, outs, send, recv, loc):
        (src,), (full,) = ins, outs
        x, y, c = _place()
        chips = _other_chips(x, y)
        sends = [_remote(src.at[_slot(*chip, c)], full.at[_slot(*chip, c)], send(j), recv(j), (x, y, 1 - c))
                 for j, chip in enumerate(chips)]
        recvs = [_remote(src.at[_slot(*chip, 1 - c)], full.at[_slot(*chip, 1 - c)], send(j), recv(j), (x, y, 1 - c))
                 for j, chip in enumerate(chips)]
        return sends, recvs, []

    return _Comm((partly_gathered,), (SDS(partly_gathered.shape, partly_gathered.dtype),), {0: 0}, 3, 0, plan)


def _relay_pieces(full, rows, x, y, c):
    start, half = rows[0], rows[1] // 2
    upper, lower = pl.ds(start, half), pl.ds(start + half, half)
    diagonal = full.at[_slot(1 - x, 1 - y, c)]
    return [(full.at[_slot(1 - x, y, c), upper], diagonal.at[upper], (x, 1 - y, c)),
            (full.at[_slot(x, 1 - y, c), lower], diagonal.at[lower], (1 - x, y, c))]


def _relay(half_block, so_far, first=None, second=None, third=None, third_after=None):
    has_block, has_buffer = half_block is not None, so_far is not None
    shape = so_far.shape if has_buffer else (2 * N_CHIPS,) + half_block.shape
    dtype = so_far.dtype if has_buffer else half_block.dtype

    def third_leg(rows, k, ins, outs, send, recv):
        src, full = (ins[-1] if has_buffer else outs[0]), outs[0]
        x, y, c = _place()
        span, sibling = pl.ds(*rows), (x, y, 1 - c)
        here, there = _slot(1 - x, 1 - y, c), _slot(1 - x, 1 - y, 1 - c)
        return ([_remote(src.at[here, span], full.at[here, span], send(k), recv(k), sibling)],
                [_remote(src.at[there, span], full.at[there, span], send(k), recv(k), sibling)])

    def plan(ins, outs, send, recv, loc):
        src, full = (ins[-1] if has_buffer else outs[0]), outs[0]
        x, y, c = _place()
        sibling = (x, y, 1 - c)
        sends, recvs, locs = [], [], []
        if first is not None:
            span = pl.ds(*first)
            blk, mine = ins[0].at[span], full.at[_slot(x, y, c), span]
            for k, peer in enumerate([sibling, (1 - x, y, c), (x, 1 - y, c)]):
                sends.append(_remote(blk, mine, send(k), recv(k), peer))
                recvs.append(_remote(blk, full.at[_slot(*peer), span], send(k), recv(k), peer))
            locs.append(pltpu.make_async_copy(blk, mine, loc(0)))
        if second is not None:
            span = pl.ds(*second)
            for k, chip in enumerate([(1 - x, y), (x, 1 - y)]):
                sends.append(_remote(src.at[_slot(*chip, c), span], full.at[_slot(*chip, c), span], send(3 + k), recv(3 + k),
                                     sibling))
                recvs.append(_remote(src.at[_slot(*chip, 1 - c), span], full.at[_slot(*chip, 1 - c), span], send(3 + k),
                                     recv(3 + k), sibling))
            for k, (piece, lands, peer) in enumerate(_relay_pieces(full, second, x, y, c)):
                sends.append(_remote(piece, piece, send(5 + k), recv(5 + k), peer))
                recvs.append(_remote(lands, lands, send(5 + k), recv(5 + k), peer))
        if third is not None:
            s, r = third_leg(third, 7, ins, outs, send, recv)
            sends, recvs = sends + s, recvs + r
        return sends, recvs, locs

    def after(ins, outs, send, recv, loc):
        s, r = third_leg(third_after, 8, ins, outs, send, recv)
        return s, r, []

    operands = ((half_block,) if has_block else ()) + ((so_far,) if has_buffer else ())
    return _Comm(operands, (SDS(shape, dtype),), {len(operands) - 1: 0} if has_buffer else {}, 9, 1, plan,
                 after if third_after is not None else None)


def _gather_whole(half_block, small_block):
    rows = half_block.shape[0]

    def body(blk_ref, small_ref, out_ref, small_out_ref, send_sems, recv_sems, local_sems):
        x, y, c = _place()
        me, sibling = (x, y, c), (x, y, 1 - c)
        neighbours, diagonal = [(1 - x, y), (x, 1 - y)], (1 - x, 1 - y)

        def copy(k, block, to, src=None):
            return _remote(out_ref.at[_slot(*block)] if src is None else src, out_ref.at[_slot(*block)],
                           send_sems.at[k], recv_sems.at[k], to)

        def small_copy(k, chip, to):
            return _remote(small_ref, small_out_ref.at[2 * chip[0] + chip[1]], send_sems.at[8 + k], recv_sems.at[8 + k], to)

        mine = pltpu.make_async_copy(blk_ref, out_ref.at[_slot(*me)], local_sems.at[0])
        mine_small = pltpu.make_async_copy(small_ref, small_out_ref.at[2 * x + y], local_sems.at[1])
        mine.start()
        mine_small.start()
        started = [copy(0, me, sibling, src=blk_ref)]
        started += [copy(1 + k, me, (*chip, c), src=blk_ref) for k, chip in enumerate(neighbours)]
        started += [small_copy(k, (x, y), (*chip, c)) for k, chip in enumerate(neighbours + [diagonal])]
        for cp in started:
            cp.start()
        pieces = _relay_pieces(out_ref, (0, rows), x, y, c)
        for k, chip in enumerate(neighbours):
            copy(1 + k, (*chip, c), me).wait_recv()
            piece, _, peer = pieces[k]
            started += [copy(3 + k, (*chip, c), sibling), _remote(piece, piece, send_sems.at[5 + k], recv_sems.at[5 + k], peer)]
            started[-2].start()
            started[-1].start()
        for k, (_, lands, peer) in enumerate(pieces):
            _remote(lands, lands, send_sems.at[5 + k], recv_sems.at[5 + k], peer).wait_recv()
        started.append(copy(7, (*diagonal, c), sibling))
        started[-1].start()
        copy(0, sibling, me).wait_recv()
        for k, chip in enumerate(neighbours):
            copy(3 + k, (*chip, 1 - c), me).wait_recv()
        copy(7, (*diagonal, 1 - c), me).wait_recv()
        for k, chip in enumerate(neighbours + [diagonal]):
            small_copy(k, chip, me).wait_recv()
        for cp in started:
            cp.wait_send()
        mine.wait()
        mine_small.wait()

    return pl.pallas_call(
        body, name="gather_whole", in_specs=[ANY, ANY], out_specs=[ANY, ANY],
        out_shape=[SDS((2 * N_CHIPS,) + half_block.shape, half_block.dtype),
                   SDS((N_CHIPS,) + small_block.shape, small_block.dtype)],
        scratch_shapes=[pltpu.SemaphoreType.DMA((11,)), pltpu.SemaphoreType.DMA((11,)), pltpu.SemaphoreType.DMA((2,))],
    )(half_block, small_block)


def _pair_send(grads):
    def plan(ins, outs, send, recv, loc):
        (g,), (got,) = ins, outs
        x, y, c = _place()
        copies = [_remote(g.at[j, 1 - c], got.at[j], send(j), recv(j), (x, y, 1 - c)) for j in range(N_CHIPS)]
        return copies, copies, []

    shape = (grads.shape[0],) + grads.shape[2:]
    return _Comm((grads,), (SDS(shape, grads.dtype),), {}, N_CHIPS, 0, plan)


def _chip_exchange(partial, rows=None, so_far=None):
    span = pl.ds(*(rows or (0, partial.shape[1])))

    def plan(ins, outs, send, recv, loc):
        p, got = ins[0], outs[0]
        x, y, c = _place()
        my_chip = 2 * x + y
        chips = _other_chips(x, y)
        sends = [_remote(p.at[2 * chip[0] + chip[1], span], got.at[my_chip, span], send(j), recv(j), (*chip, c))
                 for j, chip in enumerate(chips)]
        recvs = [_remote(p.at[my_chip, span], got.at[2 * chip[0] + chip[1], span], send(j), recv(j), (*chip, c))
                 for j, chip in enumerate(chips)]
        return sends, recvs, [pltpu.make_async_copy(p.at[my_chip, span], got.at[my_chip, span], loc(0))]

    operands = (partial,) if so_far is None else (partial, so_far)
    return _Comm(operands, (SDS(partial.shape, partial.dtype),), {} if so_far is None else {1: 0}, 3, 1, plan)


def _pair_sum(name, core, grads, received):
    h = grads.shape[2]

    def body(core_ref, g_ref, r_ref, o_ref):
        o_ref[...] = (g_ref[0] + r_ref[...]).astype(BF16)

    return pl.pallas_call(
        body, name=name,
        grid_spec=pltpu.PrefetchScalarGridSpec(
            num_scalar_prefetch=1, grid=(N_CHIPS,),
            in_specs=[pl.BlockSpec((1, 1, h, D_MODEL), lambda j, core_ref: (j, core_ref[0], 0, 0)),
                      pl.BlockSpec((1, h, D_MODEL), lambda j, core_ref: (j, 0, 0))],
            out_specs=pl.BlockSpec((1, h, D_MODEL), lambda j, core_ref: (j, 0, 0))),
        out_shape=SDS((N_CHIPS, h, D_MODEL), BF16),
        compiler_params=pltpu.CompilerParams(dimension_semantics=("arbitrary",), vmem_limit_bytes=VMEM_LIMIT_V7X),
    )(core, grads, received)


SMALL_ROWS = 8


def _sum_blocks(ref):
    return (ref[0].astype(F32) + ref[1].astype(F32)) + (ref[2].astype(F32) + ref[3].astype(F32))


def _tail_reduce(last_grads, exchanged, small):
    n = len(exchanged)
    h = last_grads.shape[2]

    def body(*refs):
        g_ref, ex, small_ref = refs[0], refs[1:1 + n], refs[1 + n]
        o0 = 2 + n
        out, out_last, small_out = refs[o0:o0 + n], refs[o0 + n], refs[o0 + n + 1]
        s0 = o0 + n + 2
        halves, half_last = refs[s0:s0 + n], refs[s0 + n]
        own, got, part, exch, small_buf = refs[s0 + n + 1:s0 + n + 6]
        pair_send, pair_recv, chip_send, chip_recv, share_send, share_recv, small_send, small_recv, local_sems = refs[s0 + n + 6:]
        x, y, c = _place()
        sibling = (x, y, 1 - c)
        my_chip, me = 2 * x + y, _slot(x, y, c)
        chips = _other_chips(x, y)

        to_sibling = [_remote(g_ref.at[j, 1 - c], got.at[j], pair_send.at[j], pair_recv.at[j], sibling)
                      for j in range(N_CHIPS)]
        load_own = [pltpu.make_async_copy(g_ref.at[j, c], own.at[j], local_sems.at[j]) for j in range(N_CHIPS)]
        for cp in to_sibling + load_own:
            cp.start()

        small_buf[me] = small_ref[...]
        small_copies = []
        for mask in range(1, 8):
            peer = (x ^ (mask >> 2), y ^ ((mask >> 1) & 1), c ^ (mask & 1))
            small_copies.append(_remote(small_ref, small_buf.at[me], small_send.at[mask - 1], small_recv.at[mask - 1], peer))
        for cp in small_copies:
            cp.start()

        def share(k, half_ref, out_ref):
            keep = pltpu.make_async_copy(half_ref, out_ref.at[c], local_sems.at[N_CHIPS + k])
            give = _remote(half_ref, out_ref.at[c], share_send.at[k], share_recv.at[k], sibling)
            take = _remote(half_ref, out_ref.at[1 - c], share_send.at[k], share_recv.at[k], sibling)
            keep.start()
            give.start()
            return keep, give, take

        shares = []
        for k in range(n):
            halves[k][...] = _sum_blocks(ex[k])
            shares.append(share(k, halves[k], out[k]))

        def pair_sum(block):
            _remote(g_ref.at[block, 1 - c], got.at[block], pair_send.at[block], pair_recv.at[block], sibling).wait_recv()
            pltpu.make_async_copy(g_ref.at[block, c], own.at[block], local_sems.at[block]).wait()
            part[block] = (own[block] + got[block]).astype(BF16)

        to_chips = []
        for j, chip in enumerate(chips):
            block = 2 * chip[0] + chip[1]
            pair_sum(block)
            to_chips.append(_remote(part.at[block], exch.at[my_chip], chip_send.at[j], chip_recv.at[j], (*chip, c)))
            to_chips[-1].start()
        pair_sum(my_chip)
        exch[my_chip] = part[my_chip]
        from_chips = [_remote(part.at[my_chip], exch.at[2 * chip[0] + chip[1]], chip_send.at[j], chip_recv.at[j], (*chip, c))
                      for j, chip in enumerate(chips)]

        for cp in small_copies:
            cp.wait_recv()
        total = small_buf[0]
        for d in range(1, 8):
            total = total + small_buf[d]
        small_out[...] = total

        for cp in from_chips:
            cp.wait_recv()
        half_last[...] = _sum_blocks(exch)
        shares.append(share(n, half_last, out_last))

        for keep, give, take in shares:
            take.wait_recv()
            give.wait_send()
            keep.wait()
        for cp in to_sibling + to_chips + small_copies:
            cp.wait_send()

    blocks = (N_CHIPS, h, D_MODEL)
    return pl.pallas_call(
        body, name="tail_reduce",
        in_specs=[ANY] + [VMEM_WHOLE] * (n + 1), out_specs=[ANY] * (n + 1) + [VMEM_WHOLE],
        out_shape=[SDS((2,) + e.shape[1:], F32) for e in exchanged] + [SDS((2, h, D_MODEL), F32), SDS(small.shape, F32)],
        scratch_shapes=[pltpu.VMEM(e.shape[1:], F32) for e in exchanged] + [pltpu.VMEM((h, D_MODEL), F32)]
                       + [pltpu.VMEM(blocks, F32), pltpu.VMEM(blocks, F32), pltpu.VMEM(blocks, BF16), pltpu.VMEM(blocks, BF16),
                          pltpu.VMEM((8,) + small.shape, F32)]
                       + [pltpu.SemaphoreType.DMA((N_CHIPS,)), pltpu.SemaphoreType.DMA((N_CHIPS,)),
                          pltpu.SemaphoreType.DMA((3,)), pltpu.SemaphoreType.DMA((3,)),
                          pltpu.SemaphoreType.DMA((n + 1,)), pltpu.SemaphoreType.DMA((n + 1,)),
                          pltpu.SemaphoreType.DMA((7,)), pltpu.SemaphoreType.DMA((7,)),
                          pltpu.SemaphoreType.DMA((N_CHIPS + n + 1,))],
        compiler_params=pltpu.CompilerParams(vmem_limit_bytes=VMEM_LIMIT_V7X),
    )(last_grads, *exchanged, small)


def _rope_expansion():
    half = ROT_DIM // 2
    expand = np.zeros((2 * half, 3 * 128), np.float32)
    const = np.zeros((1, 3 * 128), np.float32)
    for lane in range(128):
        d = lane % HEAD_DIM
        if d < ROT_DIM:
            expand[d % half, lane] = 1.0
        else:
            const[0, lane] = 1.0
        if d < half:
            expand[half + d, 128 + lane] = -1.0
        elif d < ROT_DIM:
            expand[half + d - half, 256 + lane] = 1.0
    return expand, const


ROPE_PIECES = 3 * ROT_DIM


def _rope_inputs(seq):
    pos = jnp.arange(seq, dtype=F32)
    inv_freq = ROPE_THETA ** (-jnp.arange(0, ROT_DIM, 2, dtype=F32) / ROT_DIM)
    ang = pos[:, None] * inv_freq[None, :]
    cs = jnp.concatenate([jnp.cos(ang), jnp.sin(ang)], axis=1)
    hi = lax.reduce_precision(cs, 8, 7)
    mid = lax.reduce_precision(cs - hi, 8, 7)
    low = cs - hi - mid
    expand, const = _rope_expansion()
    pieces = jnp.concatenate([hi, mid, low], axis=1).astype(BF16)
    return pieces, jnp.asarray(np.concatenate([expand] * 3, axis=0), BF16), jnp.asarray(const)


def _rope_specs(tb):
    return [pl.BlockSpec((tb, ROPE_PIECES), lambda i: (i, 0)), _resident((ROPE_PIECES, 3 * 128)), _resident((1, 3 * 128))]


def _rope_tile(pieces_ref, expand_ref, const_ref):
    tables = _dot(pieces_ref[...], expand_ref[...]) + const_ref[...]
    return tables[:, 0:128], tables[:, 128:256], tables[:, 256:384]


def _rope(t, c, sa, sb):
    half = ROT_DIM // 2
    return t * c + pltpu.roll(t, 128 - half, 1) * sa + pltpu.roll(t, half, 1) * sb


def _rope_transposed(dt, c, sa, sb):
    half = ROT_DIM // 2
    return dt * c + pltpu.roll(dt * sa, half, 1) + pltpu.roll(dt * sb, 128 - half, 1)


def _cast_halves(core, w_up, w_down, w_out, w_in_t):
    def body(core_ref, up_ref, down_ref, out_ref, in_ref, up_o, down_o, out_o, in_o):
        up_o[...] = up_ref[...].astype(BF16)
        down_o[...] = down_ref[...].astype(BF16)
        out_o[...] = out_ref[...].astype(BF16)
        in_o[...] = in_ref[...].astype(BF16)

    half = lambda rows: pl.BlockSpec((rows, D_MODEL), lambda i, core_ref: (core_ref[0], 0))
    whole = lambda rows: pl.BlockSpec((rows, D_MODEL), lambda i, core_ref: (0, 0))
    rows = (H_UP, H_DOWN, H_OUT, H_IN)
    return pl.pallas_call(
        body, name="cast_halves",
        grid_spec=pltpu.PrefetchScalarGridSpec(
            num_scalar_prefetch=1, grid=(1,), in_specs=[half(r) for r in rows], out_specs=[whole(r) for r in rows]),
        out_shape=[SDS((r, D_MODEL), BF16) for r in rows],
        compiler_params=pltpu.CompilerParams(dimension_semantics=("arbitrary",), vmem_limit_bytes=VMEM_LIMIT_V7X),
    )(core, w_up, w_down, w_out, w_in_t)


def _in_proj(x, g_pre, w_in_t, rope, comm=None):
    seq = x.shape[0]
    tb = min(seq, WIDE_TOKEN_TILE)

    def body(x_ref, g_ref, w_ref, c_ref, sa_ref, sb_ref,
             q_ref, kd0_ref, kd1_ref, vd0_ref, vd1_ref, gb_ref, gc_ref, xin_ref, hn_ref):
        xv = x_ref[...]
        hn = (xv * _rms(xv) * g_ref[...]).astype(BF16)
        hn_ref[...] = hn
        proj = _dot_nt(hn, w_ref[...].reshape(IN_COLS, D_MODEL))
        c, sa, sb = _rope_tile(c_ref, sa_ref, sb_ref)
        scale = 1.0 / math.sqrt(HEAD_DIM)
        for p in range(Q_WIDTH // 128):
            q_ref[:, 128 * p:128 * (p + 1)] = (_rope(proj[:, 128 * p:128 * (p + 1)], c, sa, sb) * scale).astype(BF16)
        k = _rope(proj[:, Q_WIDTH:Q_WIDTH + KV_WIDTH], c, sa, sb)
        v = proj[:, Q_WIDTH + KV_WIDTH:Q_WIDTH + 2 * KV_WIDTH]
        low = _lane_lt64(k.shape)
        k_sw, v_sw = pltpu.roll(k, HEAD_DIM, 1), pltpu.roll(v, HEAD_DIM, 1)
        kd0_ref[...] = jnp.where(low, k, k_sw).astype(BF16)
        kd1_ref[...] = jnp.where(low, k_sw, k).astype(BF16)
        vd0_ref[...] = jnp.where(low, v, v_sw).astype(BF16)
        vd1_ref[...] = jnp.where(low, v_sw, v).astype(BF16)
        base = Q_WIDTH + 2 * KV_WIDTH
        gb_ref[...] = proj[:, base:base + CONV_WIDTH].astype(BF16)
        gc_ref[...] = proj[:, base + CONV_WIDTH:base + 2 * CONV_WIDTH].astype(BF16)
        xin_ref[...] = proj[:, base + 2 * CONV_WIDTH:base + 3 * CONV_WIDTH].astype(BF16)

    tile = lambda w: pl.BlockSpec((tb, w), lambda i: (i, 0))
    return _pallas(
        body, name="in_proj", grid=(seq // tb,),
        in_specs=[tile(D_MODEL), _resident((1, D_MODEL)), _resident(w_in_t.shape), *_rope_specs(tb)],
        out_specs=[tile(Q_WIDTH), tile(128), tile(128), tile(128), tile(128),
                   tile(CONV_WIDTH), tile(CONV_WIDTH), tile(CONV_WIDTH), tile(D_MODEL)],
        out_shape=[SDS((seq, Q_WIDTH), BF16)] + [SDS((seq, 128), BF16)] * 4
                  + [SDS((seq, CONV_WIDTH), BF16)] * 3 + [SDS((seq, D_MODEL), BF16)],
        operands=(x, g_pre, w_in_t, *rope), comm=comm)


def _attn_valid(i):
    shape = (4 * QBLOCK, 2 * QBLOCK)
    row = lax.broadcasted_iota(jnp.int32, shape, 0)
    col = lax.broadcasted_iota(jnp.int32, shape, 1)
    qi = row & (QBLOCK - 1)
    return (col > qi) & (col <= qi + QBLOCK) & ((col >= QBLOCK) | (i > 0))


def _stack_heads(pair0, pair1):
    low = _lane_lt64(pair0.shape)
    zero = jnp.zeros_like(pair0)
    return jnp.concatenate([jnp.where(low, pair0, zero), jnp.where(low, zero, pair0),
                            jnp.where(low, pair1, zero), jnp.where(low, zero, pair1)], axis=0)


def _unstack_heads(stacked):
    low = _lane_lt64((QBLOCK, 128))
    pair0 = jnp.where(low, stacked[0:QBLOCK], stacked[QBLOCK:2 * QBLOCK])
    pair1 = jnp.where(low, stacked[2 * QBLOCK:3 * QBLOCK], stacked[3 * QBLOCK:4 * QBLOCK])
    return pair0, pair1


def _sink_column(sink_ref, kv_head):
    row = lax.broadcasted_iota(jnp.int32, (4 * QBLOCK, 1), 0)
    s = [sink_ref[0, 4 * kv_head + j] for j in range(4)]
    return jnp.where(row < QBLOCK, s[0], jnp.where(row < 2 * QBLOCK, s[1], jnp.where(row < 3 * QBLOCK, s[2], s[3])))


def _band(ref, i):
    prev = pl.multiple_of(jnp.maximum(i - 1, 0) * QBLOCK, QBLOCK)
    own = pl.multiple_of(i * QBLOCK, QBLOCK)
    return jnp.concatenate([ref[pl.ds(prev, QBLOCK), :], ref[pl.ds(own, QBLOCK), :]], axis=0), prev, own


def _softmax_with_sink(s, sink_col):
    m = jnp.maximum(jnp.max(s, axis=-1, keepdims=True), sink_col)
    p = jnp.exp(s - m)
    e_sink = jnp.exp(sink_col - m)
    inv_l = 1.0 / (jnp.sum(p, axis=-1, keepdims=True) + e_sink)
    return p, e_sink, inv_l


def _attention_fwd(q, kd0, kd1, vd0, vd1, sinks, comm=None):
    seq = q.shape[0]

    nb = ATTN_FWD_BLOCKS

    def body(sink_ref, q_ref, kd0_ref, kd1_ref, vd0_ref, vd1_ref, o_ref):
        for b in range(nb):
            i = pl.program_id(0) * nb + b
            rows = slice(QBLOCK * b, QBLOCK * (b + 1))
            valid = _attn_valid(i)
            for kv_head, (k_ref, v_ref) in enumerate(((kd0_ref, vd0_ref), (kd1_ref, vd1_ref))):
                kband, _, _ = _band(k_ref, i)
                vband, _, _ = _band(v_ref, i)
                base = 256 * kv_head
                qm = _stack_heads(q_ref[rows, base:base + 128], q_ref[rows, base + 128:base + 256])
                s = jnp.where(valid, _dot_nt(qm, kband), NEG_INF)
                p, _, inv_l = _softmax_with_sink(s, _sink_column(sink_ref, kv_head))
                o = _dot(p.astype(BF16), vband) * inv_l
                pair0, pair1 = _unstack_heads(o)
                o_ref[rows, base:base + 128] = pair0.astype(BF16)
                o_ref[rows, base + 128:base + 256] = pair1.astype(BF16)

    blk = pl.BlockSpec((nb * QBLOCK, Q_WIDTH), lambda i: (i, 0))
    full = _resident((seq, 128))
    return _pallas(
        body, name="attention_fwd", grid=(seq // (nb * QBLOCK),),
        in_specs=[pl.BlockSpec(memory_space=pltpu.SMEM), blk, full, full, full, full],
        out_specs=[blk], out_shape=[SDS((seq, Q_WIDTH), BF16)],
        operands=(sinks, q, kd0, kd1, vd0, vd1), comm=comm)


HALO = 16


def _conv_parts(gc, xin, gc_halo, xin_halo, conv_w, first):
    tb = gc.shape[0]
    u = gc.astype(F32) * xin.astype(F32)
    u_halo = jnp.where(first, 0.0, gc_halo.astype(F32) * xin_halo.astype(F32))
    ext = jnp.concatenate([u_halo, u], axis=0)
    u1 = pltpu.roll(ext, 1, 0)[HALO:HALO + tb]
    u2 = pltpu.roll(ext, 2, 0)[HALO:HALO + tb]
    y = conv_w[0:1, :] * u2 + conv_w[1:2, :] * u1 + conv_w[2:3, :] * u
    return u, u1, u2, y


def _halo_prev(tb, w):
    return pl.BlockSpec((HALO, w), lambda i: (jnp.maximum(i * (tb // HALO) - 1, 0), 0))


def _residual_mid(x, mix, g_post_mix):
    mix_f = mix.astype(F32)
    return x + mix_f * _rms(mix_f) * g_post_mix


def _mix_out(attn, gb, gc, xin, conv_w, g_attn, g_conv, w_out, comm=None):
    seq = attn.shape[0]
    tb = min(seq, WIDE_TOKEN_TILE)

    def body(a_ref, gb_ref, gc_ref, xin_ref, gch_ref, xinh_ref, cw_ref, ga_ref, gcn_ref, w_ref, mix_ref, mixed_ref):
        first = pl.program_id(0) == 0
        _, _, _, y = _conv_parts(gc_ref[...], xin_ref[...], gch_ref[...], xinh_ref[...], cw_ref[...], first)
        conv = gb_ref[...].astype(F32) * y
        a = a_ref[...].astype(F32)
        mixed_ref[:, 0:Q_WIDTH] = (a * _rms(a) * ga_ref[...]).astype(BF16)
        mixed_ref[:, Q_WIDTH:] = (conv * _rms(conv) * gcn_ref[...]).astype(BF16)
        mix_ref[...] = _dot(mixed_ref[...], w_ref[...].reshape(D_MODEL, D_MODEL)).astype(BF16)

    tile = lambda w: pl.BlockSpec((tb, w), lambda i: (i, 0))
    return _pallas(
        body, name="mix_out", grid=(seq // tb,),
        in_specs=[tile(Q_WIDTH), tile(CONV_WIDTH), tile(CONV_WIDTH), tile(CONV_WIDTH),
                  _halo_prev(tb, CONV_WIDTH), _halo_prev(tb, CONV_WIDTH),
                  _resident((CONV_K, CONV_WIDTH)), _resident((1, Q_WIDTH)), _resident((1, CONV_WIDTH)),
                  _resident(w_out.shape)],
        out_specs=[tile(D_MODEL), tile(D_MODEL)],
        out_shape=[SDS((seq, D_MODEL), BF16), SDS((seq, D_MODEL), BF16)],
        operands=(attn, gb, gc, xin, gc, xin, conv_w, g_attn, g_conv, w_out), comm=comm)


def _mlp_loss(x, mix, target, g_post_mix, g_pre_mlp, g_post_mlp, w_up, w_down):
    seq = x.shape[0]
    tb = TOKEN_TILE

    def body(x_ref, mix_ref, t_ref, gpm_ref, g2_ref, g4_ref, wup_ref, wdown_ref,
             up_ref, hn2_ref, dout_ref, dmlp_ref, loss_ref, dg4_ref, act_ref):
        @pl.when(pl.program_id(0) == 0)
        def _():
            loss_ref[...] = jnp.zeros_like(loss_ref)
            dg4_ref[...] = jnp.zeros_like(dg4_ref)

        halves = [slice(0, tb // 2), slice(tb // 2, tb)]
        hv, hn2 = [], []
        for rows in halves:
            hv.append(_residual_mid(x_ref[rows, :], mix_ref[rows, :], gpm_ref[...]))
            hn2.append((hv[-1] * _rms(hv[-1]) * g2_ref[...]).astype(BF16))
            hn2_ref[rows, :] = hn2[-1]
        for k, rows in enumerate(halves):
            for j in range(N_CHIPS):
                up = _dot(hn2[k], _chip_block(wup_ref, j))
                up = jnp.maximum(up, 0.0)
                up_ref[rows, 1024 * j:1024 * (j + 1)] = up.astype(BF16)
                act_ref[rows, 1024 * j:1024 * (j + 1)] = (up * up).astype(BF16)
        w_down_all = wdown_ref[...].reshape(D_FF, D_MODEL)
        loss = jnp.zeros((1, 1), F32)
        dg4 = jnp.zeros((1, D_MODEL), F32)
        for k, rows in enumerate(halves):
            mlp = _dot(act_ref[rows, :], w_down_all)
            rstd = _rms(mlp)
            zhat = mlp * rstd
            diff = hv[k] + zhat * g4_ref[...] - t_ref[rows, :]
            loss = loss + jnp.sum(jnp.sum(diff * diff, axis=1, keepdims=True), axis=0, keepdims=True)
            dout = diff * (1.0 / D_MODEL)
            dout_ref[rows, :] = dout
            dg4 = dg4 + _colsum(dout * zhat)
            dmlp_ref[rows, :] = _norm_bwd(dout, g4_ref[...], zhat, rstd).astype(BF16)
        loss_ref[...] += loss
        dg4_ref[...] += dg4

    tile = lambda w: pl.BlockSpec((tb, w), lambda i: (i, 0))
    return _pallas(
        body, name="mlp_loss", grid=(seq // tb,),
        in_specs=[tile(D_MODEL), tile(D_MODEL), tile(D_MODEL), _resident((1, D_MODEL)), _resident((1, D_MODEL)),
                  _resident((1, D_MODEL)), _resident(w_up.shape), _resident(w_down.shape)],
        out_specs=[tile(D_FF), tile(D_MODEL), tile(D_MODEL), tile(D_MODEL),
                   pl.BlockSpec((1, 1), lambda i: (0, 0)), pl.BlockSpec((1, D_MODEL), lambda i: (0, 0))],
        out_shape=[SDS((seq, D_FF), BF16), SDS((seq, D_MODEL), BF16), SDS((seq, D_MODEL), F32),
                   SDS((seq, D_MODEL), BF16), SDS((1, 1), F32), SDS((1, D_MODEL), F32)],
        scratch=[pltpu.VMEM((tb, D_FF), BF16)],
        operands=(x, mix, target, g_post_mix, g_pre_mlp, g_post_mlp, w_up, w_down))


def _mlp_bwd(dmlp, up, x, dout, mix, g_pre_mlp, g_post_mix, w_up, w_down):
    seq = x.shape[0]
    tb = MLP_BWD_TOKEN_TILE

    def body(dmlp_ref, up_ref, x_ref, dout_ref, mix_ref, g2_ref, gpm_ref, wup_ref, wdown_ref,
             dup_ref, dh_ref, dmix_ref, dg2_ref, dgpm_ref):
        @pl.when(pl.program_id(0) == 0)
        def _():
            dg2_ref[...] = jnp.zeros_like(dg2_ref)
            dgpm_ref[...] = jnp.zeros_like(dgpm_ref)

        subs = [slice(k * MLP_BWD_SUB_TILE, (k + 1) * MLP_BWD_SUB_TILE) for k in range(tb // MLP_BWD_SUB_TILE)]
        dhn2 = []
        for rows in subs:
            dmlp_v = dmlp_ref[rows, :]
            acc = None
            for j in range(N_CHIPS):
                cols = slice(1024 * j, 1024 * (j + 1))
                dact = _dot_nt(dmlp_v, _chip_block(wdown_ref, j))
                dup = (dact * (2.0 * up_ref[rows, cols].astype(F32))).astype(BF16)
                dup_ref[rows, cols] = dup
                part = _dot_nt(dup, _chip_block(wup_ref, j))
                acc = part if acc is None else acc + part
            dhn2.append(acc)
        dg2 = jnp.zeros((1, D_MODEL), F32)
        dgpm = jnp.zeros((1, D_MODEL), F32)
        for k, rows in enumerate(subs):
            mix_v = mix_ref[rows, :].astype(F32)
            hv = _residual_mid(x_ref[rows, :], mix_ref[rows, :], gpm_ref[...])
            r2 = _rms(hv)
            hhat = hv * r2
            dg2 = dg2 + _colsum(dhn2[k] * hhat)
            dh = dout_ref[rows, :] + _norm_bwd(dhn2[k], g2_ref[...], hhat, r2)
            dh_ref[rows, :] = dh.astype(BF16)
            rz = _rms(mix_v)
            zhat = mix_v * rz
            dgpm = dgpm + _colsum(dh * zhat)
            dmix_ref[rows, :] = _norm_bwd(dh, gpm_ref[...], zhat, rz).astype(BF16)
        dg2_ref[...] += dg2
        dgpm_ref[...] += dgpm

    tile = lambda w: pl.BlockSpec((tb, w), lambda i: (i, 0))
    vec = pl.BlockSpec((1, D_MODEL), lambda i: (0, 0))
    return _pallas(
        body, name="mlp_bwd", grid=(seq // tb,),
        in_specs=[tile(D_MODEL), tile(D_FF), tile(D_MODEL), tile(D_MODEL), tile(D_MODEL),
                  _resident((1, D_MODEL)), _resident((1, D_MODEL)), _resident(w_up.shape), _resident(w_down.shape)],
        out_specs=[tile(D_FF), tile(D_MODEL), tile(D_MODEL), vec, vec],
        out_shape=[SDS((seq, D_FF), BF16), SDS((seq, D_MODEL), BF16), SDS((seq, D_MODEL), BF16),
                   SDS((1, D_MODEL), F32), SDS((1, D_MODEL), F32)],
        operands=(dmlp, up, x, dout, mix, g_pre_mlp, g_post_mix, w_up, w_down))


def _mix_bwd(dmix, attn, gb, gc, xin, conv_w, g_attn, g_conv, w_out, n_k):
    seq = attn.shape[0]
    tb = seq // (N_CHIPS * n_k)

    def body(first, dmix_ref, a_ref, gb_ref, gc_ref, xin_ref, gch_ref, xinh_ref, cw_ref, ga_ref, gcn_ref, w_ref,
             dattn_ref, dgb_ref, dy_ref, dga_ref, dgcn_ref, dcw_ref):
        @pl.when(first)
        def _():
            dga_ref[...] = jnp.zeros_like(dga_ref)
            dgcn_ref[...] = jnp.zeros_like(dgcn_ref)
            dcw_ref[...] = jnp.zeros_like(dcw_ref)

        dmixed = _dot_nt(dmix_ref[...], w_ref[...].reshape(D_MODEL, D_MODEL))
        a = a_ref[...].astype(F32)
        ra = _rms(a)
        ahat = a * ra
        dan = dmixed[:, 0:Q_WIDTH]
        dga_ref[...] += _colsum(dan * ahat)
        dattn_ref[...] = _norm_bwd(dan, ga_ref[...], ahat, ra).astype(BF16)
        gbv = gb_ref[...].astype(F32)
        u, u1, u2, y = _conv_parts(gc_ref[...], xin_ref[...], gch_ref[...], xinh_ref[...], cw_ref[...], first)
        conv = gbv * y
        rc = _rms(conv)
        chat = conv * rc
        dcn = dmixed[:, Q_WIDTH:]
        dgcn_ref[...] += _colsum(dcn * chat)
        dconv = _norm_bwd(dcn, gcn_ref[...], chat, rc)
        dgb_ref[...] = (dconv * y).astype(BF16)
        dy = dconv * gbv
        dy_ref[...] = dy.astype(BF16)
        dcw_ref[0:1, :] += _colsum(dy * u2)
        dcw_ref[1:2, :] += _colsum(dy * u1)
        dcw_ref[2:3, :] += _colsum(dy * u)

    tile = lambda w: pl.BlockSpec((tb, w), lambda j, k: (j * n_k + k, 0))
    halo = lambda w: pl.BlockSpec((HALO, w), lambda j, k: (jnp.maximum((j * n_k + k) * (tb // HALO) - 1, 0), 0))
    whole = lambda shape: pl.BlockSpec(shape, lambda j, k: (0,) * len(shape))
    return _Rider(
        body,
        in_specs=[tile(D_MODEL), tile(Q_WIDTH), tile(CONV_WIDTH), tile(CONV_WIDTH), tile(CONV_WIDTH),
                  halo(CONV_WIDTH), halo(CONV_WIDTH),
                  _resident((CONV_K, CONV_WIDTH)), _resident((1, Q_WIDTH)), _resident((1, CONV_WIDTH)),
                  _resident(w_out.shape)],
        out_specs=[tile(Q_WIDTH), tile(CONV_WIDTH), tile(CONV_WIDTH),
                   whole((1, Q_WIDTH)), whole((1, CONV_WIDTH)), whole((CONV_K, CONV_WIDTH))],
        out_shape=[SDS((seq, Q_WIDTH), BF16), SDS((seq, CONV_WIDTH), BF16), SDS((seq, CONV_WIDTH), BF16),
                   SDS((1, Q_WIDTH), F32), SDS((1, CONV_WIDTH), F32), SDS((CONV_K, CONV_WIDTH), F32)],
        operands=(dmix, attn, gb, gc, xin, gc, xin, conv_w, g_attn, g_conv, w_out))


def _attention_bwd(q, dattn, attn, kd0, kd1, vd0, vd1, sinks, comm=None):
    seq = q.shape[0]
    nb = ATTN_BWD_BLOCKS

    def body(sink_ref, q_ref, do_ref, o_ref, kd0_ref, kd1_ref, vd0_ref, vd1_ref,
             dq_ref, dk0_ref, dk1_ref, dv0_ref, dv1_ref, dsink_ref):
        @pl.when(pl.program_id(0) == 0)
        def _():
            for r in (dk0_ref, dk1_ref, dv0_ref, dv1_ref, dsink_ref):
                r[...] = jnp.zeros_like(r)

        lane = lax.broadcasted_iota(jnp.int32, (1, 128), 1)
        dsink = jnp.zeros((1, 128), F32)
        for b in range(nb):
            i = pl.program_id(0) * nb + b
            rows = slice(QBLOCK * b, QBLOCK * (b + 1))
            valid = _attn_valid(i)
            for kv_head, (k_ref, v_ref, dk_ref, dv_ref) in enumerate(
                    ((kd0_ref, vd0_ref, dk0_ref, dv0_ref), (kd1_ref, vd1_ref, dk1_ref, dv1_ref))):
                kband, prev, own = _band(k_ref, i)
                vband, _, _ = _band(v_ref, i)
                base = 256 * kv_head
                qm = _stack_heads(q_ref[rows, base:base + 128], q_ref[rows, base + 128:base + 256])
                dom = _stack_heads(do_ref[rows, base:base + 128], do_ref[rows, base + 128:base + 256])
                om = _stack_heads(o_ref[rows, base:base + 128], o_ref[rows, base + 128:base + 256])
                s = jnp.where(valid, _dot_nt(qm, kband), NEG_INF)
                p, e_sink, inv_l = _softmax_with_sink(s, _sink_column(sink_ref, kv_head))
                p = p * inv_l
                delta = jnp.sum(dom.astype(F32) * om.astype(F32), axis=-1, keepdims=True)
                ds = (p * (_dot_nt(dom, vband) - delta)).astype(BF16)
                sink_term = -(e_sink * inv_l) * delta
                for j in range(4):
                    part = jnp.sum(sink_term[QBLOCK * j:QBLOCK * (j + 1)], axis=0, keepdims=True)
                    dsink = dsink + jnp.where(lane == 4 * kv_head + j, part, 0.0)
                pair0, pair1 = _unstack_heads(_dot(ds, kband))
                dq_ref[rows, base:base + 128] = pair0.astype(BF16)
                dq_ref[rows, base + 128:base + 256] = pair1.astype(BF16)
                dkd = _dot_tn(ds, qm)
                dkd = dkd + pltpu.roll(dkd, HEAD_DIM, 1)
                dvd = _dot_tn(p.astype(BF16), dom)
                dvd = dvd + pltpu.roll(dvd, HEAD_DIM, 1)
                dk_ref[pl.ds(prev, QBLOCK), :] += dkd[0:QBLOCK]
                dk_ref[pl.ds(own, QBLOCK), :] += dkd[QBLOCK:]
                dv_ref[pl.ds(prev, QBLOCK), :] += dvd[0:QBLOCK]
                dv_ref[pl.ds(own, QBLOCK), :] += dvd[QBLOCK:]
        dsink_ref[...] += dsink

    blk = pl.BlockSpec((nb * QBLOCK, Q_WIDTH), lambda i: (i, 0))
    full = _resident((seq, 128))
    acc = pl.BlockSpec((seq, 128), lambda i: (0, 0))
    return _pallas(
        body, name="attention_bwd", grid=(seq // (nb * QBLOCK),),
        in_specs=[pl.BlockSpec(memory_space=pltpu.SMEM), blk, blk, blk, full, full, full, full],
        out_specs=[blk, acc, acc, acc, acc, pl.BlockSpec((1, 128), lambda i: (0, 0))],
        out_shape=[SDS((seq, Q_WIDTH), BF16)] + [SDS((seq, 128), F32)] * 4 + [SDS((1, 128), F32)],
        operands=(sinks, q, dattn, attn, kd0, kd1, vd0, vd1), comm=comm)


def _in_proj_bwd(dq, dk0, dk1, dv0, dv1, dgb, dy, gc, xin, conv_w, x, dh, g_pre, w_in_t, rope):
    seq = x.shape[0]
    tb = min(seq, WIDE_TOKEN_TILE)
    n_tiles = seq // tb

    def body(dq_ref, dk0_ref, dk1_ref, dv0_ref, dv1_ref, dgb_ref, dy_ref, dyh_ref, gc_ref, xin_ref, cw_ref,
             x_ref, dh_ref, g_ref, w_ref, c_ref, sa_ref, sb_ref,
             dproj_ref, gx_ref, dg_ref):
        i = pl.program_id(0)

        @pl.when(i == 0)
        def _():
            dg_ref[...] = jnp.zeros_like(dg_ref)

        dy = dy_ref[...].astype(F32)
        ext = jnp.concatenate([dy, jnp.where(i == n_tiles - 1, 0.0, dyh_ref[...].astype(F32))], axis=0)
        dy1 = pltpu.roll(ext, tb + HALO - 1, 0)[0:tb]
        dy2 = pltpu.roll(ext, tb + HALO - 2, 0)[0:tb]
        cw = cw_ref[...]
        du = cw[2:3, :] * dy + cw[1:2, :] * dy1 + cw[0:1, :] * dy2
        scale = 1.0 / math.sqrt(HEAD_DIM)
        base = Q_WIDTH + 2 * KV_WIDTH
        halves = [slice(0, tb // 2), slice(tb // 2, tb)]
        low = _lane_lt64((tb // 2, 128))
        for rows in halves:
            c, sa, sb = _rope_tile(c_ref.at[rows, :], sa_ref, sb_ref)
            for p in range(Q_WIDTH // 128):
                dproj_ref[rows, 128 * p:128 * (p + 1)] = _rope_transposed(
                    dq_ref[rows, 128 * p:128 * (p + 1)].astype(F32) * scale, c, sa, sb).astype(BF16)
            dk = jnp.where(low, dk0_ref[rows, :], dk1_ref[rows, :])
            dproj_ref[rows, Q_WIDTH:Q_WIDTH + KV_WIDTH] = _rope_transposed(dk, c, sa, sb).astype(BF16)
            dproj_ref[rows, Q_WIDTH + KV_WIDTH:base] = jnp.where(low, dv0_ref[rows, :], dv1_ref[rows, :]).astype(BF16)
            dproj_ref[rows, base:base + CONV_WIDTH] = dgb_ref[rows, :]
            dproj_ref[rows, base + CONV_WIDTH:base + 2 * CONV_WIDTH] = (du[rows] * xin_ref[rows, :].astype(F32)).astype(BF16)
            dproj_ref[rows, base + 2 * CONV_WIDTH:] = (du[rows] * gc_ref[rows, :].astype(F32)).astype(BF16)
        w_all = w_ref[...].reshape(IN_COLS, D_MODEL)
        dhn = [_dot(dproj_ref[rows, :], w_all) for rows in halves]
        dg = jnp.zeros((1, D_MODEL), F32)
        for k, rows in enumerate(halves):
            xv = x_ref[rows, :]
            r = _rms(xv)
            xhat = xv * r
            dg = dg + _colsum(dhn[k] * xhat)
            gx_ref[rows, :] = dh_ref[rows, :].astype(F32) + _norm_bwd(dhn[k], g_ref[...], xhat, r)
        dg_ref[...] += dg

    tile = lambda w: pl.BlockSpec((tb, w), lambda i: (i, 0))
    halo_next = pl.BlockSpec((HALO, CONV_WIDTH), lambda i: (jnp.minimum((i + 1) * (tb // HALO), seq // HALO - 1), 0))
    return _pallas(
        body, name="in_proj_bwd", grid=(n_tiles,),
        in_specs=[tile(Q_WIDTH), tile(128), tile(128), tile(128), tile(128), tile(CONV_WIDTH), tile(CONV_WIDTH), halo_next,
                  tile(CONV_WIDTH), tile(CONV_WIDTH), _resident((CONV_K, CONV_WIDTH)),
                  tile(D_MODEL), tile(D_MODEL), _resident((1, D_MODEL)), _resident(w_in_t.shape), *_rope_specs(tb)],
        out_specs=[tile(IN_COLS), tile(D_MODEL), pl.BlockSpec((1, D_MODEL), lambda i: (0, 0))],
        out_shape=[SDS((seq, IN_COLS), BF16), SDS((seq, D_MODEL), F32), SDS((1, D_MODEL), F32)],
        operands=(dq, dk0, dk1, dv0, dv1, dgb, dy, dy, gc, xin, conv_w, x, dh, g_pre, w_in_t, *rope))


def _wgrad_grid(seq, per_chip, h_rows):
    chips_per_step = 1 if per_chip else N_CHIPS
    m = chips_per_step * 2 * h_rows
    bt = min(seq, WGRAD_TOKEN_TILE)
    return chips_per_step, m, bt, seq // bt


def _wgrad(name, a, b, *, per_chip, h_rows, square_a=False, comm=None, rider=None):
    seq = a.shape[0]
    chips_per_step, m, bt, n_k = _wgrad_grid(seq, per_chip, h_rows)
    a_cols = m if per_chip else a.shape[1]
    a_wide = a.shape[1] > a_cols
    b_wide = b.shape[1] > D_MODEL

    def body(a_ref, b_ref, g_ref):
        @pl.when(pl.program_id(1) == 0)
        def _():
            g_ref[...] = jnp.zeros_like(g_ref)

        av = a_ref[...]
        if square_a:
            av = (av.astype(F32) * av.astype(F32)).astype(BF16)
        g_ref[...] += _dot_tn(av, b_ref[...]).reshape(g_ref.shape)

    a_spec = pl.BlockSpec((bt, a_cols), (lambda j, k: (k, j)) if a_wide else (lambda j, k: (k, 0)))
    b_spec = pl.BlockSpec((bt, D_MODEL), (lambda j, k: (k, j)) if b_wide else (lambda j, k: (k, 0)))
    g_spec = pl.BlockSpec((chips_per_step, 2, h_rows, D_MODEL), lambda j, k: (j, 0, 0, 0),
                          pipeline_mode=None if per_chip else pl.Buffered(1))
    return _pallas(
        body, name=name, grid=(N_CHIPS if per_chip else 1, n_k),
        in_specs=[a_spec, b_spec], out_specs=[g_spec], out_shape=[SDS((N_CHIPS, 2, h_rows, D_MODEL), F32)],
        operands=(a, b), comm=comm, rider=rider)


def _adamw_math(w, g, m, v):
    m = ADAM_B1 * m + (1.0 - ADAM_B1) * g
    v = ADAM_B2 * v + (1.0 - ADAM_B2) * (g * g)
    m_hat = m / (1.0 - ADAM_B1 ** ADAM_STEP)
    v_hat = v / (1.0 - ADAM_B2 ** ADAM_STEP)
    delta = -ADAM_LR * (m_hat / (jnp.sqrt(v_hat) + ADAM_EPS) + ADAM_WD * w)
    return delta, m, v


def _adamw_rows(name, reduced, w, m, v, rt):
    per_half = reduced.shape[1] // rt

    def body(r_ref, w_ref, m_ref, v_ref, g_out, d_out, m_out, v_out):
        g = r_ref[0]
        g_out[...] = g
        d_out[...], m_out[...], v_out[...] = _adamw_math(w_ref[...], g, m_ref[...], v_ref[...])

    blk = pl.BlockSpec((rt, D_MODEL), lambda h, r: (h * per_half + r, 0))
    return _pallas(
        body, name=name, grid=(2, per_half),
        in_specs=[pl.BlockSpec((1, rt, D_MODEL), lambda h, r: (h, r, 0)), blk, blk, blk],
        out_specs=[blk, blk, blk, blk], out_shape=[SDS(w.shape, F32)] * 4, operands=(reduced, w, m, v))


def _adamw_small(packed_grads, w, m, v):
    names = SMALL_NAMES
    n = len(names)
    conv_local = w["conv_w"].shape[-1]

    def body(*refs):
        gp = refs[0]
        w_refs, m_refs, v_refs = refs[1:1 + n], refs[1 + n:1 + 2 * n], refs[1 + 2 * n:1 + 3 * n]
        outs = refs[1 + 3 * n:]
        g_out, d_out, m_out, v_out = outs[0:n], outs[n:2 * n], outs[2 * n:3 * n], outs[3 * n:4 * n]
        chip = 2 * lax.axis_index("x") + lax.axis_index("y")

        def step(k, g, index=None):
            pick = (lambda r: r[...]) if index is None else (lambda r: r[index])
            d, new_m, new_v = _adamw_math(pick(w_refs[k]), g, pick(m_refs[k]), pick(v_refs[k]))
            for ref, val in ((g_out[k], g), (d_out[k], d), (m_out[k], new_m), (v_out[k], new_v)):
                if index is None:
                    ref[...] = val
                else:
                    ref[index] = val

        for k, name in enumerate(names):
            if name in SMALL_VECTORS:
                step(k, gp[SMALL_VECTORS.index(name):SMALL_VECTORS.index(name) + 1, :])
            elif name == "attn_group_norm":
                step(k, gp[4:5, 0:Q_WIDTH])
            elif name == "conv_group_norm":
                step(k, gp[4:5, Q_WIDTH:])
            elif name == "attn_sinks":
                step(k, gp[7:8, 0:8])
            else:
                for t in range(CONV_K):
                    row, base = 5 + t // 2, CONV_WIDTH * (t % 2)
                    g = gp[row:row + 1, base:base + conv_local]
                    for j in range(1, CONV_WIDTH // conv_local):
                        g = jnp.where(chip == j, gp[row:row + 1, base + conv_local * j:base + conv_local * (j + 1)], g)
                    step(k, g, index=(0, slice(t, t + 1), slice(None)))

    shapes = [SDS(w[name].shape, F32) for name in names]
    res = pl.pallas_call(
        body, name="adamw_small", in_specs=[VMEM_WHOLE] * (1 + 3 * n), out_specs=[VMEM_WHOLE] * (4 * n),
        out_shape=shapes * 4,
    )(packed_grads, *[w[k] for k in names], *[m[k] for k in names], *[v[k] for k in names])
    return [dict(zip(names, res[i * n:(i + 1) * n])) for i in range(4)]


SMALL_VECTORS = ("pre_mix_norm", "post_mix_norm", "pre_mlp_norm", "post_mlp_norm")
SMALL_NAMES = SMALL_VECTORS + ("attn_group_norm", "conv_group_norm", "conv_w", "attn_sinks")


def _pack_small(p):
    rows = [p[n].reshape(1, D_MODEL) for n in SMALL_VECTORS]
    rows.append(jnp.concatenate([p["attn_group_norm"].reshape(1, -1), p["conv_group_norm"].reshape(1, -1)], axis=1))
    cw = p["conv_w"].reshape(CONV_K, -1)
    rows.append(jnp.pad(cw, ((0, 1), (0, CONV_WIDTH - cw.shape[1]))).reshape(2, D_MODEL))
    last = jnp.concatenate([p["attn_sinks"].reshape(1, 8), p.get("loss_sum", jnp.zeros((1, 1), F32))], axis=1)
    rows.append(jnp.pad(last, ((0, 0), (0, D_MODEL - 9))))
    return jnp.concatenate(rows, axis=0)


WEIGHT_ORDER = ("pre_mix_norm", "w_in", "conv_w", "attn_sinks", "attn_group_norm", "conv_group_norm", "w_out",
                "post_mix_norm", "pre_mlp_norm", "w_up", "w_down", "post_mlp_norm")


def kernel(x, pre_mix_norm, w_in, conv_w, attn_sinks, attn_group_norm, conv_group_norm, w_out, post_mix_norm, pre_mlp_norm, w_up, w_down, post_mlp_norm, loss_target, m_pre_mix_norm, m_w_in, m_conv_w, m_attn_sinks, m_attn_group_norm, m_conv_group_norm, m_w_out, m_post_mix_norm, m_pre_mlp_norm, m_w_up, m_w_down, m_post_mlp_norm, v_pre_mix_norm, v_w_in, v_conv_w, v_attn_sinks, v_attn_group_norm, v_conv_group_norm, v_w_out, v_post_mix_norm, v_pre_mlp_norm, v_w_up, v_w_down, v_post_mlp_norm):
    w = dict(pre_mix_norm=pre_mix_norm, w_in=w_in, conv_w=conv_w, attn_sinks=attn_sinks, attn_group_norm=attn_group_norm,
             conv_group_norm=conv_group_norm, w_out=w_out, post_mix_norm=post_mix_norm, pre_mlp_norm=pre_mlp_norm,
             w_up=w_up, w_down=w_down, post_mlp_norm=post_mlp_norm)
    m = dict(pre_mix_norm=m_pre_mix_norm, w_in=m_w_in, conv_w=m_conv_w, attn_sinks=m_attn_sinks,
             attn_group_norm=m_attn_group_norm, conv_group_norm=m_conv_group_norm, w_out=m_w_out,
             post_mix_norm=m_post_mix_norm, pre_mlp_norm=m_pre_mlp_norm, w_up=m_w_up, w_down=m_w_down,
             post_mlp_norm=m_post_mlp_norm)
    v = dict(pre_mix_norm=v_pre_mix_norm, w_in=v_w_in, conv_w=v_conv_w, attn_sinks=v_attn_sinks,
             attn_group_norm=v_attn_group_norm, conv_group_norm=v_conv_group_norm, w_out=v_w_out,
             post_mix_norm=v_post_mix_norm, pre_mlp_norm=v_pre_mlp_norm, w_up=v_w_up, w_down=v_w_down,
             post_mlp_norm=v_post_mlp_norm)
    core = lax.axis_index("c").astype(jnp.int32).reshape(1)
    xs, target = x[0], loss_target[0]
    rope = _rope_inputs(xs.shape[0])

    hb_up, hb_down, hb_out, hb_in = _cast_halves(core, w_up[0], w_down[0], w_out[0], w_in[0].T)
    conv_pad = jnp.pad(conv_w[0], ((0, 8 - CONV_K), (0, 0)))
    wf_in, conv_all = _gather_whole(hb_in, conv_pad)
    conv_full = conv_all[:, :CONV_K, :].transpose(1, 0, 2).reshape(CONV_K, CONV_WIDTH)

    whole_up, early, late = (0, H_UP), (0, DOWN_EARLY_ROWS), (DOWN_EARLY_ROWS, H_DOWN - DOWN_EARLY_ROWS)
    *proj, wf_up, wf_out, wf_down = _in_proj(
        xs, pre_mix_norm, wf_in, rope,
        comm=_merge(_relay(hb_up, None, first=whole_up), _gather_first(hb_out), _relay(hb_down, None, first=early)))
    q, kd0, kd1, vd0, vd1, gb, gc, xin, hn = proj
    attn, wf_up, wf_out, wf_down = _attention_fwd(
        q, kd0, kd1, vd0, vd1, attn_sinks,
        comm=_merge(_relay(None, wf_up, second=whole_up), _gather_second(wf_out),
                    _relay(hb_down, wf_down, first=late, second=early)))
    mix, mixed, wf_up, wf_down = _mix_out(
        attn, gb, gc, xin, conv_full, attn_group_norm, conv_group_norm, wf_out,
        comm=_merge(_relay(None, wf_up, third=whole_up), _relay(None, wf_down, second=late, third=early, third_after=late)))
    up, hn2, dout, dmlp, loss_sum, dg_post_mlp = _mlp_loss(xs, mix, target, post_mix_norm, pre_mlp_norm, post_mlp_norm,
                                                           wf_up, wf_down)

    dup, dh, dmix, dg_pre_mlp, dg_post_mix = _mlp_bwd(dmlp, up, xs, dout, mix, pre_mlp_norm, post_mix_norm, wf_up, wf_down)
    n_k = _wgrad_grid(xs.shape[0], True, H_DOWN)[3]
    g_down, dattn, dgb, dy, dg_attn, dg_conv, dconv_w = _wgrad(
        "wgrad_down", up, dmlp, per_chip=True, h_rows=H_DOWN, square_a=True,
        rider=_mix_bwd(dmix, attn, gb, gc, xin, conv_full, attn_group_norm, conv_group_norm, wf_out, n_k))
    g_up, got_down = _wgrad("wgrad_up", hn2, dup, per_chip=True, h_rows=H_UP, comm=_pair_send(g_down))
    p_down = _pair_sum("pair_sum_down", core, g_down, got_down)
    g_out, got_up = _wgrad("wgrad_out", mixed, dmix, per_chip=False, h_rows=H_OUT, comm=_pair_send(g_up))
    p_up = _pair_sum("pair_sum_up", core, g_up, got_up)
    up_early, up_late = (0, UP_EXCHANGE_EARLY_ROWS), (UP_EXCHANGE_EARLY_ROWS, H_UP - UP_EXCHANGE_EARLY_ROWS)
    dq, dk0, dk1, dv0, dv1, dsink, ex_down, ex_up, got_out = _attention_bwd(
        q, dattn, attn, kd0, kd1, vd0, vd1, attn_sinks,
        comm=_merge(_chip_exchange(p_down), _chip_exchange(p_up, rows=up_early), _pair_send(g_out)))
    p_out = _pair_sum("pair_sum_out", core, g_out, got_out)
    dproj, grad_x, dg_pre_mix = _in_proj_bwd(dq, dk0, dk1, dv0, dv1, dgb, dy, gc, xin, conv_full, xs, dh, pre_mix_norm,
                                             wf_in, rope)
    g_in, ex_out, ex_up = _wgrad("wgrad_in", dproj, hn, per_chip=False, h_rows=H_IN,
                                 comm=_merge(_chip_exchange(p_out), _chip_exchange(p_up, rows=up_late, so_far=ex_up)))
    small = dict(pre_mix_norm=dg_pre_mix, conv_w=dconv_w, attn_sinks=dsink[:, :8], attn_group_norm=dg_attn,
                 conv_group_norm=dg_conv, post_mix_norm=dg_post_mix, pre_mlp_norm=dg_pre_mlp, post_mlp_norm=dg_post_mlp,
                 loss_sum=loss_sum)
    r_down, r_up, r_out, r_in, small_total = _tail_reduce(g_in, [ex_down, ex_up, ex_out], _pack_small(small))

    out_g, out_d, out_m, out_v = {}, {}, {}, {}
    out_g["w_up"], out_d["w_up"], out_m["w_up"], out_v["w_up"] = _adamw_rows(
        "adamw_up", r_up, w_up[0], m_w_up[0], v_w_up[0], 256)
    out_g["w_down"], out_d["w_down"], out_m["w_down"], out_v["w_down"] = _adamw_rows(
        "adamw_down", r_down, w_down[0], m_w_down[0], v_w_down[0], 256)
    out_g["w_out"], out_d["w_out"], out_m["w_out"], out_v["w_out"] = _adamw_rows(
        "adamw_out", r_out, w_out[0], m_w_out[0], v_w_out[0], H_OUT)
    in_t = _adamw_rows("adamw_in", r_in, w_in[0].T, m_w_in[0].T, v_w_in[0].T, H_IN)
    out_g["w_in"], out_d["w_in"], out_m["w_in"], out_v["w_in"] = [t.T for t in in_t]

    loss = small_total[7, 8] * (0.5 / D_MODEL)
    for out, part in zip((out_g, out_d, out_m, out_v), _adamw_small(small_total, w, m, v)):
        out.update(part)

    def shaped(d):
        return [d[n].reshape(w[n].shape) for n in WEIGHT_ORDER]

    return (loss, grad_x[None], *shaped(out_g), *shaped(out_d), *shaped(out_m), *shaped(out_v))
```

```python
import math
from typing import Callable, NamedTuple

import jax
import jax.numpy as jnp
import numpy as np
from jax import lax
from jax.experimental import pallas as pl
from jax.experimental.pallas import tpu as pltpu

F32 = jnp.float32
BF16 = jnp.bfloat16

D_MODEL = 1024
HEAD_DIM = 64
Q_WIDTH = 512
KV_WIDTH = 128
CONV_WIDTH = 512
CONV_K = 3
D_FF = 4096
IN_COLS = 2304
QBLOCK = 128
ROT_DIM = 16
ROPE_THETA = 500000.0
NORM_EPS = 1e-6
NEG_INF = -1e30
N_CHIPS = 4

ADAM_LR = 0.001
ADAM_B1 = 0.9
ADAM_B2 = 0.999
ADAM_EPS = 1e-08
ADAM_WD = 0.01
ADAM_STEP = 10

H_UP, H_DOWN, H_OUT, H_IN = 512, 512, 128, 288
DOWN_EARLY_ROWS = 224

TOKEN_TILE = 512
WIDE_TOKEN_TILE = 1024
MLP_BWD_TOKEN_TILE = 512
MLP_BWD_SUB_TILE = 256
ATTN_FWD_BLOCKS = 16
ATTN_BWD_BLOCKS = 2
WGRAD_TOKEN_TILE = 4096
VMEM_LIMIT_V7X = 56 * 1024 * 1024

MESH = pl.DeviceIdType.MESH
ANY = pl.BlockSpec(memory_space=pl.ANY)
VMEM_WHOLE = pl.BlockSpec(memory_space=pltpu.VMEM)
SDS = jax.ShapeDtypeStruct


def _resident(shape):
    zeros = (0,) * len(shape)
    return pl.BlockSpec(shape, lambda *_: zeros, pipeline_mode=pl.Buffered(1))


def _rms(v):
    return lax.rsqrt(jnp.mean(v * v, axis=-1, keepdims=True) + NORM_EPS)


def _norm_bwd(dy, gain, vhat, rstd):
    t = dy * gain
    return rstd * (t - vhat * jnp.mean(t * vhat, axis=-1, keepdims=True))


def _colsum(v):
    return jnp.sum(v, axis=0, keepdims=True)


def _dot_nt(a, b):
    return lax.dot_general(a, b, (((1,), (1,)), ((), ())), preferred_element_type=F32)


def _dot_tn(a, b):
    return lax.dot_general(a, b, (((0,), (0,)), ((), ())), preferred_element_type=F32)


def _dot(a, b):
    return jnp.dot(a, b, preferred_element_type=F32)


def _chip_block(w_ref, chip):
    both = w_ref[pl.ds(2 * chip, 2)]
    return both.reshape(2 * both.shape[1], both.shape[2])


def _lane_lt64(shape):
    return lax.broadcasted_iota(jnp.int32, shape, 1) < HEAD_DIM


class _Comm(NamedTuple):
    operands: tuple
    out_shapes: tuple
    aliases: dict
    n_remote: int
    n_local: int
    plan: Callable
    after: Callable = None


def _merge(*comms):
    operands, out_shapes, aliases, parts = [], [], {}, []
    n_remote = n_local = 0
    for cm in comms:
        parts.append((len(operands), len(out_shapes), n_remote, n_local, cm))
        for k, v in cm.aliases.items():
            aliases[len(operands) + k] = len(out_shapes) + v
        operands += cm.operands
        out_shapes += cm.out_shapes
        n_remote += cm.n_remote
        n_local += cm.n_local

    def run(which, ins, outs, send, recv, loc):
        sends, recvs, locs = [], [], []
        for i0, o0, r0, l0, cm in parts:
            stage = getattr(cm, which)
            if stage is not None:
                s, r, l = stage(ins[i0:i0 + len(cm.operands)], outs[o0:o0 + len(cm.out_shapes)],
                                lambda k, r0=r0: send(r0 + k), lambda k, r0=r0: recv(r0 + k), lambda k, l0=l0: loc(l0 + k))
                sends, recvs, locs = sends + s, recvs + r, locs + l
        return sends, recvs, locs

    def plan(*args):
        return run("plan", *args)

    def after(*args):
        return run("after", *args)

    return _Comm(tuple(operands), tuple(out_shapes), aliases, n_remote, n_local, plan,
                 after if any(cm.after is not None for cm in comms) else None)


def _sem_scratch(comm):
    return [pltpu.SemaphoreType.DMA((max(comm.n_remote, 1),)), pltpu.SemaphoreType.DMA((max(comm.n_remote, 1),)),
            pltpu.SemaphoreType.DMA((max(comm.n_local, 1),))]


class _Rider(NamedTuple):
    body: Callable
    in_specs: list
    out_specs: list
    out_shape: list
    operands: tuple


def _pallas(body, *, name, grid, in_specs, out_specs, out_shape, operands, scratch=(), comm=None, rider=None):
    params = pltpu.CompilerParams(dimension_semantics=("arbitrary",) * len(grid), vmem_limit_bytes=VMEM_LIMIT_V7X)
    if rider is not None:
        own_in, own_out, ride_in, ride_out = len(in_specs), len(out_specs), len(rider.in_specs), len(rider.out_specs)
        own_body = body

        def body(*refs):
            o0 = own_in + ride_in
            s0 = o0 + own_out + ride_out
            own_body(*refs[:own_in], *refs[o0:o0 + own_out], *refs[s0:])
            first = None
            for axis in range(len(grid)):
                at_start = pl.program_id(axis) == 0
                first = at_start if first is None else jnp.logical_and(first, at_start)
            rider.body(first, *refs[own_in:o0], *refs[o0 + own_out:s0])

        in_specs, out_specs = list(in_specs) + rider.in_specs, list(out_specs) + rider.out_specs
        out_shape, operands = list(out_shape) + rider.out_shape, tuple(operands) + tuple(rider.operands)
    if comm is None:
        return pl.pallas_call(body, name=name, grid=grid, in_specs=in_specs, out_specs=out_specs, out_shape=out_shape,
                              scratch_shapes=list(scratch), compiler_params=params)(*operands)
    n_in, n_out, n_scr = len(in_specs), len(out_specs), len(scratch)
    c_in, c_out = len(comm.operands), len(comm.out_shapes)

    def with_comm(*refs):
        ins, c_ins = refs[:n_in], refs[n_in:n_in + c_in]
        o0 = n_in + c_in
        outs, c_outs = refs[o0:o0 + n_out], refs[o0 + n_out:o0 + n_out + c_out]
        s0 = o0 + n_out + c_out
        scr = refs[s0:s0 + n_scr]
        send_sems, recv_sems, local_sems = refs[s0 + n_scr:]
        first = last = None
        for axis, size in enumerate(grid):
            at_start, at_end = pl.program_id(axis) == 0, pl.program_id(axis) == size - 1
            first = at_start if first is None else jnp.logical_and(first, at_start)
            last = at_end if last is None else jnp.logical_and(last, at_end)

        def copies():
            return comm.plan(c_ins, c_outs, lambda k: send_sems.at[k], lambda k: recv_sems.at[k],
                             lambda k: local_sems.at[k])

        @pl.when(first)
        def _():
            sends, _, locs = copies()
            for cp in sends + locs:
                cp.start()

        body(*ins, *outs, *scr)

        @pl.when(last)
        def _():
            sends, recvs, locs = copies()
            for cp in recvs:
                cp.wait_recv()
            for cp in sends:
                cp.wait_send()
            for cp in locs:
                cp.wait()
            if comm.after is not None:
                sends, recvs, _ = comm.after(c_ins, c_outs, lambda k: send_sems.at[k], lambda k: recv_sems.at[k],
                                             lambda k: local_sems.at[k])
                for cp in sends:
                    cp.start()
                for cp in recvs:
                    cp.wait_recv()
                for cp in sends:
                    cp.wait_send()

    return pl.pallas_call(
        with_comm, name=name, grid=grid,
        in_specs=list(in_specs) + [ANY] * c_in, out_specs=list(out_specs) + [ANY] * c_out,
        out_shape=list(out_shape) + list(comm.out_shapes),
        scratch_shapes=list(scratch) + _sem_scratch(comm),
        input_output_aliases={n_in + k: n_out + v for k, v in comm.aliases.items()},
        compiler_params=params)(*operands, *comm.operands)


def _place():
    return lax.axis_index("x"), lax.axis_index("y"), lax.axis_index("c")


def _other_chips(x, y):
    return [(1 - x, y), (x, 1 - y), (1 - x, 1 - y)]


def _slot(px, py, pc):
    return 4 * px + 2 * py + pc


def _remote(src, dst, send_sem, recv_sem, to):
    return pltpu.make_async_remote_copy(src_ref=src, dst_ref=dst, send_sem=send_sem, recv_sem=recv_sem,
                                        device_id=to, device_id_type=MESH)


def _gather_first(half_block):
    def plan(ins, outs, send, recv, loc):
        (blk,), (full,) = ins, outs
        x, y, c = _place()
        chips = _other_chips(x, y)
        mine = full.at[_slot(x, y, c)]
        sends = [_remote(blk, mine, send(0), recv(0), (x, y, 1 - c))]
        sends += [_remote(blk, mine, send(1 + j), recv(1 + j), (*chip, c)) for j, chip in enumerate(chips)]
        recvs = [_remote(blk, full.at[_slot(x, y, 1 - c)], send(0), recv(0), (x, y, 1 - c))]
        recvs += [_remote(blk, full.at[_slot(*chip, c)], send(1 + j), recv(1 + j), (*chip, c))
                  for j, chip in enumerate(chips)]
        return sends, recvs, [pltpu.make_async_copy(blk, mine, loc(0))]

    return _Comm((half_block,), (SDS((2 * N_CHIPS,) + half_block.shape, half_block.dtype),), {}, 4, 1, plan)


def _gather_second(partly_gathered):
    def plan(ins, outs, send, recv, loc):
        (src,), (full,) = ins, outs
        x, y, c = _place()
        chips = _other_chips(x, y)
        sends = [_remote(src.at[_slot(*chip, c)], full.at[_slot(*chip, c)], send(j), recv(j), (x, y, 1 - c))
                 for j, chip in enumerate(chips)]
        recvs = [_remote(src.at[_slot(*chip, 1 - c)], full.at[_slot(*chip, 1 - c)], send(j), recv(j), (x, y, 1 - c))
                 for j, chip in enumerate(chips)]
        return sends, recvs, []

    return _Comm((partly_gathered,), (SDS(partly_gathered.shape, partly_gathered.dtype),), {0: 0}, 3, 0, plan)


def _relay_pieces(full, rows, x, y, c):
    start, half = rows[0], rows[1] // 2
    upper, lower = pl.ds(start, half), pl.ds(start + half, half)
    diagonal = full.at[_slot(1 - x, 1 - y, c)]
    return [(full.at[_slot(1 - x, y, c), upper], diagonal.at[upper], (x, 1 - y, c)),
            (full.at[_slot(x, 1 - y, c), lower], diagonal.at[lower], (1 - x, y, c))]


def _relay(half_block, so_far, first=None, second=None, third=None, third_after=None):
    has_block, has_buffer = half_block is not None, so_far is not None
    shape = so_far.shape if has_buffer else (2 * N_CHIPS,) + half_block.shape
    dtype = so_far.dtype if has_buffer else half_block.dtype

    def third_leg(rows, k, ins, outs, send, recv):
        src, full = (ins[-1] if has_buffer else outs[0]), outs[0]
        x, y, c = _place()
        span, sibling = pl.ds(*rows), (x, y, 1 - c)
        here, there = _slot(1 - x, 1 - y, c), _slot(1 - x, 1 - y, 1 - c)
        return ([_remote(src.at[here, span], full.at[here, span], send(k), recv(k), sibling)],
                [_remote(src.at[there, span], full.at[there, span], send(k), recv(k), sibling)])

    def plan(ins, outs, send, recv, loc):
        src, full = (ins[-1] if has_buffer else outs[0]), outs[0]
        x, y, c = _place()
        sibling = (x, y, 1 - c)
        sends, recvs, locs = [], [], []
        if first is not None:
            span = pl.ds(*first)
            blk, mine = ins[0].at[span], full.at[_slot(x, y, c), span]
            for k, peer in enumerate([sibling, (1 - x, y, c), (x, 1 - y, c)]):
                sends.append(_remote(blk, mine, send(k), recv(k), peer))
                recvs.append(_remote(blk, full.at[_slot(*peer), span], send(k), recv(k), peer))
            locs.append(pltpu.make_async_copy(blk, mine, loc(0)))
        if second is not None:
            span = pl.ds(*second)
            for k, chip in enumerate([(1 - x, y), (x, 1 - y)]):
                sends.append(_remote(src.at[_slot(*chip, c), span], full.at[_slot(*chip, c), span], send(3 + k), recv(3 + k),
                                     sibling))
                recvs.append(_remote(src.at[_slot(*chip, 1 - c), span], full.at[_slot(*chip, 1 - c), span], send(3 + k),
                                     recv(3 + k), sibling))
            for k, (piece, lands, peer) in enumerate(_relay_pieces(full, second, x, y, c)):
                sends.append(_remote(piece, piece, send(5 + k), recv(5 + k), peer))
                recvs.append(_remote(lands, lands, send(5 + k), recv(5 + k), peer))
        if third is not None:
            s, r = third_leg(third, 7, ins, outs, send, recv)
            sends, recvs = sends + s, recvs + r
        return sends, recvs, locs

    def after(ins, outs, send, recv, loc):
        s, r = third_leg(third_after, 8, ins, outs, send, recv)
        return s, r, []

    operands = ((half_block,) if has_block else ()) + ((so_far,) if has_buffer else ())
    return _Comm(operands, (SDS(shape, dtype),), {len(operands) - 1: 0} if has_buffer else {}, 9, 1, plan,
                 after if third_after is not None else None)


def _gather_whole(half_block, small_block):
    rows = half_block.shape[0]

    def body(blk_ref, small_ref, out_ref, small_out_ref, send_sems, recv_sems, local_sems):
        x, y, c = _place()
        me, sibling = (x, y, c), (x, y, 1 - c)
        neighbours, diagonal = [(1 - x, y), (x, 1 - y)], (1 - x, 1 - y)

        def copy(k, block, to, src=None):
            return _remote(out_ref.at[_slot(*block)] if src is None else src, out_ref.at[_slot(*block)],
                           send_sems.at[k], recv_sems.at[k], to)

        def small_copy(k, chip, to):
            return _remote(small_ref, small_out_ref.at[2 * chip[0] + chip[1]], send_sems.at[8 + k], recv_sems.at[8 + k], to)

        mine = pltpu.make_async_copy(blk_ref, out_ref.at[_slot(*me)], local_sems.at[0])
        mine_small = pltpu.make_async_copy(small_ref, small_out_ref.at[2 * x + y], local_sems.at[1])
        mine.start()
        mine_small.start()
        started = [copy(0, me, sibling, src=blk_ref)]
        started += [copy(1 + k, me, (*chip, c), src=blk_ref) for k, chip in enumerate(neighbours)]
        started += [small_copy(k, (x, y), (*chip, c)) for k, chip in enumerate(neighbours + [diagonal])]
        for cp in started:
            cp.start()
        pieces = _relay_pieces(out_ref, (0, rows), x, y, c)
        for k, chip in enumerate(neighbours):
            copy(1 + k, (*chip, c), me).wait_recv()
            piece, _, peer = pieces[k]
            started += [copy(3 + k, (*chip, c), sibling), _remote(piece, piece, send_sems.at[5 + k], recv_sems.at[5 + k], peer)]
            started[-2].start()
            started[-1].start()
        for k, (_, lands, peer) in enumerate(pieces):
            _remote(lands, lands, send_sems.at[5 + k], recv_sems.at[5 + k], peer).wait_recv()
        started.append(copy(7, (*diagonal, c), sibling))
        started[-1].start()
        copy(0, sibling, me).wait_recv()
        for k, chip in enumerate(neighbours):
            copy(3 + k, (*chip, 1 - c), me).wait_recv()
        copy(7, (*diagonal, 1 - c), me).wait_recv()
        for k, chip in enumerate(neighbours + [diagonal]):
            small_copy(k, chip, me).wait_recv()
        for cp in started:
            cp.wait_send()
        mine.wait()
        mine_small.wait()

    return pl.pallas_call(
        body, name="gather_whole", in_specs=[ANY, ANY], out_specs=[ANY, ANY],
        out_shape=[SDS((2 * N_CHIPS,) + half_block.shape, half_block.dtype),
                   SDS((N_CHIPS,) + small_block.shape, small_block.dtype)],
        scratch_shapes=[pltpu.SemaphoreType.DMA((11,)), pltpu.SemaphoreType.DMA((11,)), pltpu.SemaphoreType.DMA((2,))],
    )(half_block, small_block)


def _pair_send(grads):
    def plan(ins, outs, send, recv, loc):
        (g,), (got,) = ins, outs
        x, y, c = _place()
        copies = [_remote(g.at[j, 1 - c], got.at[j], send(j), recv(j), (x, y, 1 - c)) for j in range(N_CHIPS)]
        return copies, copies, []

    shape = (grads.shape[0],) + grads.shape[2:]
    return _Comm((grads,), (SDS(shape, grads.dtype),), {}, N_CHIPS, 0, plan)


def _chip_exchange(partial):
    def plan(ins, outs, send, recv, loc):
        (p,), (got,) = ins, outs
        x, y, c = _place()
        my_chip = 2 * x + y
        chips = _other_chips(x, y)
        sends = [_remote(p.at[2 * chip[0] + chip[1]], got.at[my_chip], send(j), recv(j), (*chip, c))
                 for j, chip in enumerate(chips)]
        recvs = [_remote(p.at[my_chip], got.at[2 * chip[0] + chip[1]], send(j), recv(j), (*chip, c))
                 for j, chip in enumerate(chips)]
        return sends, recvs, [pltpu.make_async_copy(p.at[my_chip], got.at[my_chip], loc(0))]

    return _Comm((partial,), (SDS(partial.shape, partial.dtype),), {}, 3, 1, plan)


def _pair_sum(name, core, grads, received):
    h = grads.shape[2]

    def body(core_ref, g_ref, r_ref, o_ref):
        o_ref[...] = (g_ref[0] + r_ref[...]).astype(BF16)

    return pl.pallas_call(
        body, name=name,
        grid_spec=pltpu.PrefetchScalarGridSpec(
            num_scalar_prefetch=1, grid=(N_CHIPS,),
            in_specs=[pl.BlockSpec((1, 1, h, D_MODEL), lambda j, core_ref: (j, core_ref[0], 0, 0)),
                      pl.BlockSpec((1, h, D_MODEL), lambda j, core_ref: (j, 0, 0))],
            out_specs=pl.BlockSpec((1, h, D_MODEL), lambda j, core_ref: (j, 0, 0))),
        out_shape=SDS((N_CHIPS, h, D_MODEL), BF16),
        compiler_params=pltpu.CompilerParams(dimension_semantics=("arbitrary",), vmem_limit_bytes=VMEM_LIMIT_V7X),
    )(core, grads, received)


SMALL_ROWS = 8


def _sum_blocks(ref):
    return (ref[0].astype(F32) + ref[1].astype(F32)) + (ref[2].astype(F32) + ref[3].astype(F32))


def _tail_reduce(last_grads, exchanged, small):
    n = len(exchanged)
    h = last_grads.shape[2]

    def body(*refs):
        g_ref, ex, small_ref = refs[0], refs[1:1 + n], refs[1 + n]
        o0 = 2 + n
        out, out_last, small_out = refs[o0:o0 + n], refs[o0 + n], refs[o0 + n + 1]
        s0 = o0 + n + 2
        halves, half_last = refs[s0:s0 + n], refs[s0 + n]
        own, got, part, exch, small_buf = refs[s0 + n + 1:s0 + n + 6]
        pair_send, pair_recv, chip_send, chip_recv, share_send, share_recv, small_send, small_recv, local_sems = refs[s0 + n + 6:]
        x, y, c = _place()
        sibling = (x, y, 1 - c)
        my_chip, me = 2 * x + y, _slot(x, y, c)
        chips = _other_chips(x, y)

        to_sibling = [_remote(g_ref.at[j, 1 - c], got.at[j], pair_send.at[j], pair_recv.at[j], sibling)
                      for j in range(N_CHIPS)]
        load_own = [pltpu.make_async_copy(g_ref.at[j, c], own.at[j], local_sems.at[j]) for j in range(N_CHIPS)]
        for cp in to_sibling + load_own:
            cp.start()

        small_buf[me] = small_ref[...]
        small_copies = []
        for mask in range(1, 8):
            peer = (x ^ (mask >> 2), y ^ ((mask >> 1) & 1), c ^ (mask & 1))
            small_copies.append(_remote(small_ref, small_buf.at[me], small_send.at[mask - 1], small_recv.at[mask - 1], peer))
        for cp in small_copies:
            cp.start()

        def share(k, half_ref, out_ref):
            keep = pltpu.make_async_copy(half_ref, out_ref.at[c], local_sems.at[N_CHIPS + k])
            give = _remote(half_ref, out_ref.at[c], share_send.at[k], share_recv.at[k], sibling)
            take = _remote(half_ref, out_ref.at[1 - c], share_send.at[k], share_recv.at[k], sibling)
            keep.start()
            give.start()
            return keep, give, take

        shares = []
        for k in range(n):
            halves[k][...] = _sum_blocks(ex[k])
            shares.append(share(k, halves[k], out[k]))

        def pair_sum(block):
            _remote(g_ref.at[block, 1 - c], got.at[block], pair_send.at[block], pair_recv.at[block], sibling).wait_recv()
            pltpu.make_async_copy(g_ref.at[block, c], own.at[block], local_sems.at[block]).wait()
            part[block] = (own[block] + got[block]).astype(BF16)

        to_chips = []
        for j, chip in enumerate(chips):
            block = 2 * chip[0] + chip[1]
            pair_sum(block)
            to_chips.append(_remote(part.at[block], exch.at[my_chip], chip_send.at[j], chip_recv.at[j], (*chip, c)))
            to_chips[-1].start()
        pair_sum(my_chip)
        exch[my_chip] = part[my_chip]
        from_chips = [_remote(part.at[my_chip], exch.at[2 * chip[0] + chip[1]], chip_send.at[j], chip_recv.at[j], (*chip, c))
                      for j, chip in enumerate(chips)]

        for cp in small_copies:
            cp.wait_recv()
        total = small_buf[0]
        for d in range(1, 8):
            total = total + small_buf[d]
        small_out[...] = total

        for cp in from_chips:
            cp.wait_recv()
        half_last[...] = _sum_blocks(exch)
        shares.append(share(n, half_last, out_last))

        for keep, give, take in shares:
            take.wait_recv()
            give.wait_send()
            keep.wait()
        for cp in to_sibling + to_chips + small_copies:
            cp.wait_send()

    blocks = (N_CHIPS, h, D_MODEL)
    return pl.pallas_call(
        body, name="tail_reduce",
        in_specs=[ANY] + [VMEM_WHOLE] * (n + 1), out_specs=[ANY] * (n + 1) + [VMEM_WHOLE],
        out_shape=[SDS((2,) + e.shape[1:], F32) for e in exchanged] + [SDS((2, h, D_MODEL), F32), SDS(small.shape, F32)],
        scratch_shapes=[pltpu.VMEM(e.shape[1:], F32) for e in exchanged] + [pltpu.VMEM((h, D_MODEL), F32)]
                       + [pltpu.VMEM(blocks, F32), pltpu.VMEM(blocks, F32), pltpu.VMEM(blocks, BF16), pltpu.VMEM(blocks, BF16),
                          pltpu.VMEM((8,) + small.shape, F32)]
                       + [pltpu.SemaphoreType.DMA((N_CHIPS,)), pltpu.SemaphoreType.DMA((N_CHIPS,)),
                          pltpu.SemaphoreType.DMA((3,)), pltpu.SemaphoreType.DMA((3,)),
                          pltpu.SemaphoreType.DMA((n + 1,)), pltpu.SemaphoreType.DMA((n + 1,)),
                          pltpu.SemaphoreType.DMA((7,)), pltpu.SemaphoreType.DMA((7,)),
                          pltpu.SemaphoreType.DMA((N_CHIPS + n + 1,))],
        compiler_params=pltpu.CompilerParams(vmem_limit_bytes=VMEM_LIMIT_V7X),
    )(last_grads, *exchanged, small)


def _rope_expansion():
    half = ROT_DIM // 2
    expand = np.zeros((2 * half, 3 * 128), np.float32)
    const = np.zeros((1, 3 * 128), np.float32)
    for lane in range(128):
        d = lane % HEAD_DIM
        if d < ROT_DIM:
            expand[d % half, lane] = 1.0
        else:
            const[0, lane] = 1.0
        if d < half:
            expand[half + d, 128 + lane] = -1.0
        elif d < ROT_DIM:
            expand[half + d - half, 256 + lane] = 1.0
    return expand, const


ROPE_PIECES = 3 * ROT_DIM


def _rope_inputs(seq):
    pos = jnp.arange(seq, dtype=F32)
    inv_freq = ROPE_THETA ** (-jnp.arange(0, ROT_DIM, 2, dtype=F32) / ROT_DIM)
    ang = pos[:, None] * inv_freq[None, :]
    cs = jnp.concatenate([jnp.cos(ang), jnp.sin(ang)], axis=1)
    hi = lax.reduce_precision(cs, 8, 7)
    mid = lax.reduce_precision(cs - hi, 8, 7)
    low = cs - hi - mid
    expand, const = _rope_expansion()
    pieces = jnp.concatenate([hi, mid, low], axis=1).astype(BF16)
    return pieces, jnp.asarray(np.concatenate([expand] * 3, axis=0), BF16), jnp.asarray(const)


def _rope_specs(tb):
    return [pl.BlockSpec((tb, ROPE_PIECES), lambda i: (i, 0)), _resident((ROPE_PIECES, 3 * 128)), _resident((1, 3 * 128))]


def _rope_tile(pieces_ref, expand_ref, const_ref):
    tables = _dot(pieces_ref[...], expand_ref[...]) + const_ref[...]
    return tables[:, 0:128], tables[:, 128:256], tables[:, 256:384]


def _rope(t, c, sa, sb):
    half = ROT_DIM // 2
    return t * c + pltpu.roll(t, 128 - half, 1) * sa + pltpu.roll(t, half, 1) * sb


def _rope_transposed(dt, c, sa, sb):
    half = ROT_DIM // 2
    return dt * c + pltpu.roll(dt * sa, half, 1) + pltpu.roll(dt * sb, 128 - half, 1)


def _cast_halves(core, w_up, w_down, w_out, w_in_t):
    def body(core_ref, up_ref, down_ref, out_ref, in_ref, up_o, down_o, out_o, in_o):
        up_o[...] = up_ref[...].astype(BF16)
        down_o[...] = down_ref[...].astype(BF16)
        out_o[...] = out_ref[...].astype(BF16)
        in_o[...] = in_ref[...].astype(BF16)

    half = lambda rows: pl.BlockSpec((rows, D_MODEL), lambda i, core_ref: (core_ref[0], 0))
    whole = lambda rows: pl.BlockSpec((rows, D_MODEL), lambda i, core_ref: (0, 0))
    rows = (H_UP, H_DOWN, H_OUT, H_IN)
    return pl.pallas_call(
        body, name="cast_halves",
        grid_spec=pltpu.PrefetchScalarGridSpec(
            num_scalar_prefetch=1, grid=(1,), in_specs=[half(r) for r in rows], out_specs=[whole(r) for r in rows]),
        out_shape=[SDS((r, D_MODEL), BF16) for r in rows],
        compiler_params=pltpu.CompilerParams(dimension_semantics=("arbitrary",), vmem_limit_bytes=VMEM_LIMIT_V7X),
    )(core, w_up, w_down, w_out, w_in_t)


def _in_proj(x, g_pre, w_in_t, rope, comm=None):
    seq = x.shape[0]
    tb = min(seq, WIDE_TOKEN_TILE)

    def body(x_ref, g_ref, w_ref, c_ref, sa_ref, sb_ref,
             q_ref, kd0_ref, kd1_ref, vd0_ref, vd1_ref, gb_ref, gc_ref, xin_ref, hn_ref):
        xv = x_ref[...]
        hn = (xv * _rms(xv) * g_ref[...]).astype(BF16)
        hn_ref[...] = hn
        proj = _dot_nt(hn, w_ref[...].reshape(IN_COLS, D_MODEL))
        c, sa, sb = _rope_tile(c_ref, sa_ref, sb_ref)
        scale = 1.0 / math.sqrt(HEAD_DIM)
        for p in range(Q_WIDTH // 128):
            q_ref[:, 128 * p:128 * (p + 1)] = (_rope(proj[:, 128 * p:128 * (p + 1)], c, sa, sb) * scale).astype(BF16)
        k = _rope(proj[:, Q_WIDTH:Q_WIDTH + KV_WIDTH], c, sa, sb)
        v = proj[:, Q_WIDTH + KV_WIDTH:Q_WIDTH + 2 * KV_WIDTH]
        low = _lane_lt64(k.shape)
        k_sw, v_sw = pltpu.roll(k, HEAD_DIM, 1), pltpu.roll(v, HEAD_DIM, 1)
        kd0_ref[...] = jnp.where(low, k, k_sw).astype(BF16)
        kd1_ref[...] = jnp.where(low, k_sw, k).astype(BF16)
        vd0_ref[...] = jnp.where(low, v, v_sw).astype(BF16)
        vd1_ref[...] = jnp.where(low, v_sw, v).astype(BF16)
        base = Q_WIDTH + 2 * KV_WIDTH
        gb_ref[...] = proj[:, base:base + CONV_WIDTH].astype(BF16)
        gc_ref[...] = proj[:, base + CONV_WIDTH:base + 2 * CONV_WIDTH].astype(BF16)
        xin_ref[...] = proj[:, base + 2 * CONV_WIDTH:base + 3 * CONV_WIDTH].astype(BF16)

    tile = lambda w: pl.BlockSpec((tb, w), lambda i: (i, 0))
    return _pallas(
        body, name="in_proj", grid=(seq // tb,),
        in_specs=[tile(D_MODEL), _resident((1, D_MODEL)), _resident(w_in_t.shape), *_rope_specs(tb)],
        out_specs=[tile(Q_WIDTH), tile(128), tile(128), tile(128), tile(128),
                   tile(CONV_WIDTH), tile(CONV_WIDTH), tile(CONV_WIDTH), tile(D_MODEL)],
        out_shape=[SDS((seq, Q_WIDTH), BF16)] + [SDS((seq, 128), BF16)] * 4
                  + [SDS((seq, CONV_WIDTH), BF16)] * 3 + [SDS((seq, D_MODEL), BF16)],
        operands=(x, g_pre, w_in_t, *rope), comm=comm)


def _attn_valid(i):
    shape = (4 * QBLOCK, 2 * QBLOCK)
    row = lax.broadcasted_iota(jnp.int32, shape, 0)
    col = lax.broadcasted_iota(jnp.int32, shape, 1)
    qi = row & (QBLOCK - 1)
    return (col > qi) & (col <= qi + QBLOCK) & ((col >= QBLOCK) | (i > 0))


def _stack_heads(pair0, pair1):
    low = _lane_lt64(pair0.shape)
    zero = jnp.zeros_like(pair0)
    return jnp.concatenate([jnp.where(low, pair0, zero), jnp.where(low, zero, pair0),
                            jnp.where(low, pair1, zero), jnp.where(low, zero, pair1)], axis=0)


def _unstack_heads(stacked):
    low = _lane_lt64((QBLOCK, 128))
    pair0 = jnp.where(low, stacked[0:QBLOCK], stacked[QBLOCK:2 * QBLOCK])
    pair1 = jnp.where(low, stacked[2 * QBLOCK:3 * QBLOCK], stacked[3 * QBLOCK:4 * QBLOCK])
    return pair0, pair1


def _sink_column(sink_ref, kv_head):
    row = lax.broadcasted_iota(jnp.int32, (4 * QBLOCK, 1), 0)
    s = [sink_ref[0, 4 * kv_head + j] for j in range(4)]
    return jnp.where(row < QBLOCK, s[0], jnp.where(row < 2 * QBLOCK, s[1], jnp.where(row < 3 * QBLOCK, s[2], s[3])))


def _band(ref, i):
    prev = pl.multiple_of(jnp.maximum(i - 1, 0) * QBLOCK, QBLOCK)
    own = pl.multiple_of(i * QBLOCK, QBLOCK)
    return jnp.concatenate([ref[pl.ds(prev, QBLOCK), :], ref[pl.ds(own, QBLOCK), :]], axis=0), prev, own


def _softmax_with_sink(s, sink_col):
    m = jnp.maximum(jnp.max(s, axis=-1, keepdims=True), sink_col)
    p = jnp.exp(s - m)
    e_sink = jnp.exp(sink_col - m)
    inv_l = 1.0 / (jnp.sum(p, axis=-1, keepdims=True) + e_sink)
    return p, e_sink, inv_l


def _attention_fwd(q, kd0, kd1, vd0, vd1, sinks, comm=None):
    seq = q.shape[0]

    nb = ATTN_FWD_BLOCKS

    def body(sink_ref, q_ref, kd0_ref, kd1_ref, vd0_ref, vd1_ref, o_ref):
        for b in range(nb):
            i = pl.program_id(0) * nb + b
            rows = slice(QBLOCK * b, QBLOCK * (b + 1))
            valid = _attn_valid(i)
            for kv_head, (k_ref, v_ref) in enumerate(((kd0_ref, vd0_ref), (kd1_ref, vd1_ref))):
                kband, _, _ = _band(k_ref, i)
                vband, _, _ = _band(v_ref, i)
                base = 256 * kv_head
                qm = _stack_heads(q_ref[rows, base:base + 128], q_ref[rows, base + 128:base + 256])
                s = jnp.where(valid, _dot_nt(qm, kband), NEG_INF)
                p, _, inv_l = _softmax_with_sink(s, _sink_column(sink_ref, kv_head))
                o = _dot(p.astype(BF16), vband) * inv_l
                pair0, pair1 = _unstack_heads(o)
                o_ref[rows, base:base + 128] = pair0.astype(BF16)
                o_ref[rows, base + 128:base + 256] = pair1.astype(BF16)

    blk = pl.BlockSpec((nb * QBLOCK, Q_WIDTH), lambda i: (i, 0))
    full = _resident((seq, 128))
    return _pallas(
        body, name="attention_fwd", grid=(seq // (nb * QBLOCK),),
        in_specs=[pl.BlockSpec(memory_space=pltpu.SMEM), blk, full, full, full, full],
        out_specs=[blk], out_shape=[SDS((seq, Q_WIDTH), BF16)],
        operands=(sinks, q, kd0, kd1, vd0, vd1), comm=comm)


HALO = 16


def _conv_parts(gc, xin, gc_halo, xin_halo, conv_w, first):
    tb = gc.shape[0]
    u = gc.astype(F32) * xin.astype(F32)
    u_halo = jnp.where(first, 0.0, gc_halo.astype(F32) * xin_halo.astype(F32))
    ext = jnp.concatenate([u_halo, u], axis=0)
    u1 = pltpu.roll(ext, 1, 0)[HALO:HALO + tb]
    u2 = pltpu.roll(ext, 2, 0)[HALO:HALO + tb]
    y = conv_w[0:1, :] * u2 + conv_w[1:2, :] * u1 + conv_w[2:3, :] * u
    return u, u1, u2, y


def _halo_prev(tb, w):
    return pl.BlockSpec((HALO, w), lambda i: (jnp.maximum(i * (tb // HALO) - 1, 0), 0))


def _residual_mid(x, mix, g_post_mix):
    mix_f = mix.astype(F32)
    return x + mix_f * _rms(mix_f) * g_post_mix


def _mix_out(attn, gb, gc, xin, conv_w, g_attn, g_conv, w_out, comm=None):
    seq = attn.shape[0]
    tb = min(seq, WIDE_TOKEN_TILE)

    def body(a_ref, gb_ref, gc_ref, xin_ref, gch_ref, xinh_ref, cw_ref, ga_ref, gcn_ref, w_ref, mix_ref, mixed_ref):
        first = pl.program_id(0) == 0
        _, _, _, y = _conv_parts(gc_ref[...], xin_ref[...], gch_ref[...], xinh_ref[...], cw_ref[...], first)
        conv = gb_ref[...].astype(F32) * y
        a = a_ref[...].astype(F32)
        mixed_ref[:, 0:Q_WIDTH] = (a * _rms(a) * ga_ref[...]).astype(BF16)
        mixed_ref[:, Q_WIDTH:] = (conv * _rms(conv) * gcn_ref[...]).astype(BF16)
        mix_ref[...] = _dot(mixed_ref[...], w_ref[...].reshape(D_MODEL, D_MODEL)).astype(BF16)

    tile = lambda w: pl.BlockSpec((tb, w), lambda i: (i, 0))
    return _pallas(
        body, name="mix_out", grid=(seq // tb,),
        in_specs=[tile(Q_WIDTH), tile(CONV_WIDTH), tile(CONV_WIDTH), tile(CONV_WIDTH),
                  _halo_prev(tb, CONV_WIDTH), _halo_prev(tb, CONV_WIDTH),
                  _resident((CONV_K, CONV_WIDTH)), _resident((1, Q_WIDTH)), _resident((1, CONV_WIDTH)),
                  _resident(w_out.shape)],
        out_specs=[tile(D_MODEL), tile(D_MODEL)],
        out_shape=[SDS((seq, D_MODEL), BF16), SDS((seq, D_MODEL), BF16)],
        operands=(attn, gb, gc, xin, gc, xin, conv_w, g_attn, g_conv, w_out), comm=comm)


def _mlp_loss(x, mix, target, g_post_mix, g_pre_mlp, g_post_mlp, w_up, w_down):
    seq = x.shape[0]
    tb = TOKEN_TILE

    def body(x_ref, mix_ref, t_ref, gpm_ref, g2_ref, g4_ref, wup_ref, wdown_ref,
             up_ref, hn2_ref, dout_ref, dmlp_ref, loss_ref, dg4_ref, act_ref):
        @pl.when(pl.program_id(0) == 0)
        def _():
            loss_ref[...] = jnp.zeros_like(loss_ref)
            dg4_ref[...] = jnp.zeros_like(dg4_ref)

        halves = [slice(0, tb // 2), slice(tb // 2, tb)]
        hv, hn2 = [], []
        for rows in halves:
            hv.append(_residual_mid(x_ref[rows, :], mix_ref[rows, :], gpm_ref[...]))
            hn2.append((hv[-1] * _rms(hv[-1]) * g2_ref[...]).astype(BF16))
            hn2_ref[rows, :] = hn2[-1]
        for k, rows in enumerate(halves):
            for j in range(N_CHIPS):
                up = _dot(hn2[k], _chip_block(wup_ref, j))
                up = jnp.maximum(up, 0.0)
                up_ref[rows, 1024 * j:1024 * (j + 1)] = up.astype(BF16)
                act_ref[rows, 1024 * j:1024 * (j + 1)] = (up * up).astype(BF16)
        w_down_all = wdown_ref[...].reshape(D_FF, D_MODEL)
        loss = jnp.zeros((1, 1), F32)
        dg4 = jnp.zeros((1, D_MODEL), F32)
        for k, rows in enumerate(halves):
            mlp = _dot(act_ref[rows, :], w_down_all)
            rstd = _rms(mlp)
            zhat = mlp * rstd
            diff = hv[k] + zhat * g4_ref[...] - t_ref[rows, :]
            loss = loss + jnp.sum(jnp.sum(diff * diff, axis=1, keepdims=True), axis=0, keepdims=True)
            dout = diff * (1.0 / D_MODEL)
            dout_ref[rows, :] = dout
            dg4 = dg4 + _colsum(dout * zhat)
            dmlp_ref[rows, :] = _norm_bwd(dout, g4_ref[...], zhat, rstd).astype(BF16)
        loss_ref[...] += loss
        dg4_ref[...] += dg4

    tile = lambda w: pl.BlockSpec((tb, w), lambda i: (i, 0))
    return _pallas(
        body, name="mlp_loss", grid=(seq // tb,),
        in_specs=[tile(D_MODEL), tile(D_MODEL), tile(D_MODEL), _resident((1, D_MODEL)), _resident((1, D_MODEL)),
                  _resident((1, D_MODEL)), _resident(w_up.shape), _resident(w_down.shape)],
        out_specs=[tile(D_FF), tile(D_MODEL), tile(D_MODEL), tile(D_MODEL),
                   pl.BlockSpec((1, 1), lambda i: (0, 0)), pl.BlockSpec((1, D_MODEL), lambda i: (0, 0))],
        out_shape=[SDS((seq, D_FF), BF16), SDS((seq, D_MODEL), BF16), SDS((seq, D_MODEL), F32),
                   SDS((seq, D_MODEL), BF16), SDS((1, 1), F32), SDS((1, D_MODEL), F32)],
        scratch=[pltpu.VMEM((tb, D_FF), BF16)],
        operands=(x, mix, target, g_post_mix, g_pre_mlp, g_post_mlp, w_up, w_down))


def _mlp_bwd(dmlp, up, x, dout, mix, g_pre_mlp, g_post_mix, w_up, w_down):
    seq = x.shape[0]
    tb = MLP_BWD_TOKEN_TILE

    def body(dmlp_ref, up_ref, x_ref, dout_ref, mix_ref, g2_ref, gpm_ref, wup_ref, wdown_ref,
             dup_ref, dh_ref, dmix_ref, dg2_ref, dgpm_ref):
        @pl.when(pl.program_id(0) == 0)
        def _():
            dg2_ref[...] = jnp.zeros_like(dg2_ref)
            dgpm_ref[...] = jnp.zeros_like(dgpm_ref)

        subs = [slice(k * MLP_BWD_SUB_TILE, (k + 1) * MLP_BWD_SUB_TILE) for k in range(tb // MLP_BWD_SUB_TILE)]
        dhn2 = []
        for rows in subs:
            dmlp_v = dmlp_ref[rows, :]
            acc = None
            for j in range(N_CHIPS):
                cols = slice(1024 * j, 1024 * (j + 1))
                dact = _dot_nt(dmlp_v, _chip_block(wdown_ref, j))
                dup = (dact * (2.0 * up_ref[rows, cols].astype(F32))).astype(BF16)
                dup_ref[rows, cols] = dup
                part = _dot_nt(dup, _chip_block(wup_ref, j))
                acc = part if acc is None else acc + part
            dhn2.append(acc)
        dg2 = jnp.zeros((1, D_MODEL), F32)
        dgpm = jnp.zeros((1, D_MODEL), F32)
        for k, rows in enumerate(subs):
            mix_v = mix_ref[rows, :].astype(F32)
            hv = _residual_mid(x_ref[rows, :], mix_ref[rows, :], gpm_ref[...])
            r2 = _rms(hv)
            hhat = hv * r2
            dg2 = dg2 + _colsum(dhn2[k] * hhat)
            dh = dout_ref[rows, :] + _norm_bwd(dhn2[k], g2_ref[...], hhat, r2)
            dh_ref[rows, :] = dh.astype(BF16)
            rz = _rms(mix_v)
            zhat = mix_v * rz
            dgpm = dgpm + _colsum(dh * zhat)
            dmix_ref[rows, :] = _norm_bwd(dh, gpm_ref[...], zhat, rz).astype(BF16)
        dg2_ref[...] += dg2
        dgpm_ref[...] += dgpm

    tile = lambda w: pl.BlockSpec((tb, w), lambda i: (i, 0))
    vec = pl.BlockSpec((1, D_MODEL), lambda i: (0, 0))
    return _pallas(
        body, name="mlp_bwd", grid=(seq // tb,),
        in_specs=[tile(D_MODEL), tile(D_FF), tile(D_MODEL), tile(D_MODEL), tile(D_MODEL),
                  _resident((1, D_MODEL)), _resident((1, D_MODEL)), _resident(w_up.shape), _resident(w_down.shape)],
        out_specs=[tile(D_FF), tile(D_MODEL), tile(D_MODEL), vec, vec],
        out_shape=[SDS((seq, D_FF), BF16), SDS((seq, D_MODEL), BF16), SDS((seq, D_MODEL), BF16),
                   SDS((1, D_MODEL), F32), SDS((1, D_MODEL), F32)],
        operands=(dmlp, up, x, dout, mix, g_pre_mlp, g_post_mix, w_up, w_down))


def _mix_bwd(dmix, attn, gb, gc, xin, conv_w, g_attn, g_conv, w_out, n_k):
    seq = attn.shape[0]
    tb = seq // (N_CHIPS * n_k)

    def body(first, dmix_ref, a_ref, gb_ref, gc_ref, xin_ref, gch_ref, xinh_ref, cw_ref, ga_ref, gcn_ref, w_ref,
             dattn_ref, dgb_ref, dy_ref, dga_ref, dgcn_ref, dcw_ref):
        @pl.when(first)
        def _():
            dga_ref[...] = jnp.zeros_like(dga_ref)
            dgcn_ref[...] = jnp.zeros_like(dgcn_ref)
            dcw_ref[...] = jnp.zeros_like(dcw_ref)

        dmixed = _dot_nt(dmix_ref[...], w_ref[...].reshape(D_MODEL, D_MODEL))
        a = a_ref[...].astype(F32)
        ra = _rms(a)
        ahat = a * ra
        dan = dmixed[:, 0:Q_WIDTH]
        dga_ref[...] += _colsum(dan * ahat)
        dattn_ref[...] = _norm_bwd(dan, ga_ref[...], ahat, ra).astype(BF16)
        gbv = gb_ref[...].astype(F32)
        u, u1, u2, y = _conv_parts(gc_ref[...], xin_ref[...], gch_ref[...], xinh_ref[...], cw_ref[...], first)
        conv = gbv * y
        rc = _rms(conv)
        chat = conv * rc
        dcn = dmixed[:, Q_WIDTH:]
        dgcn_ref[...] += _colsum(dcn * chat)
        dconv = _norm_bwd(dcn, gcn_ref[...], chat, rc)
        dgb_ref[...] = (dconv * y).astype(BF16)
        dy = dconv * gbv
        dy_ref[...] = dy.astype(BF16)
        dcw_ref[0:1, :] += _colsum(dy * u2)
        dcw_ref[1:2, :] += _colsum(dy * u1)
        dcw_ref[2:3, :] += _colsum(dy * u)

    tile = lambda w: pl.BlockSpec((tb, w), lambda j, k: (j * n_k + k, 0))
    halo = lambda w: pl.BlockSpec((HALO, w), lambda j, k: (jnp.maximum((j * n_k + k) * (tb // HALO) - 1, 0), 0))
    whole = lambda shape: pl.BlockSpec(shape, lambda j, k: (0,) * len(shape))
    return _Rider(
        body,
        in_specs=[tile(D_MODEL), tile(Q_WIDTH), tile(CONV_WIDTH), tile(CONV_WIDTH), tile(CONV_WIDTH),
                  halo(CONV_WIDTH), halo(CONV_WIDTH),
                  _resident((CONV_K, CONV_WIDTH)), _resident((1, Q_WIDTH)), _resident((1, CONV_WIDTH)),
                  _resident(w_out.shape)],
        out_specs=[tile(Q_WIDTH), tile(CONV_WIDTH), tile(CONV_WIDTH),
                   whole((1, Q_WIDTH)), whole((1, CONV_WIDTH)), whole((CONV_K, CONV_WIDTH))],
        out_shape=[SDS((seq, Q_WIDTH), BF16), SDS((seq, CONV_WIDTH), BF16), SDS((seq, CONV_WIDTH), BF16),
                   SDS((1, Q_WIDTH), F32), SDS((1, CONV_WIDTH), F32), SDS((CONV_K, CONV_WIDTH), F32)],
        operands=(dmix, attn, gb, gc, xin, gc, xin, conv_w, g_attn, g_conv, w_out))


def _attention_bwd(q, dattn, attn, kd0, kd1, vd0, vd1, sinks, comm=None):
    seq = q.shape[0]
    nb = ATTN_BWD_BLOCKS

    def body(sink_ref, q_ref, do_ref, o_ref, kd0_ref, kd1_ref, vd0_ref, vd1_ref,
             dq_ref, dk0_ref, dk1_ref, dv0_ref, dv1_ref, dsink_ref):
        @pl.when(pl.program_id(0) == 0)
        def _():
            for r in (dk0_ref, dk1_ref, dv0_ref, dv1_ref, dsink_ref):
                r[...] = jnp.zeros_like(r)

        lane = lax.broadcasted_iota(jnp.int32, (1, 128), 1)
        dsink = jnp.zeros((1, 128), F32)
        for b in range(nb):
            i = pl.program_id(0) * nb + b
            rows = slice(QBLOCK * b, QBLOCK * (b + 1))
            valid = _attn_valid(i)
            for kv_head, (k_ref, v_ref, dk_ref, dv_ref) in enumerate(
                    ((kd0_ref, vd0_ref, dk0_ref, dv0_ref), (kd1_ref, vd1_ref, dk1_ref, dv1_ref))):
                kband, prev, own = _band(k_ref, i)
                vband, _, _ = _band(v_ref, i)
                base = 256 * kv_head
                qm = _stack_heads(q_ref[rows, base:base + 128], q_ref[rows, base + 128:base + 256])
                dom = _stack_heads(do_ref[rows, base:base + 128], do_ref[rows, base + 128:base + 256])
                om = _stack_heads(o_ref[rows, base:base + 128], o_ref[rows, base + 128:base + 256])
                s = jnp.where(valid, _dot_nt(qm, kband), NEG_INF)
                p, e_sink, inv_l = _softmax_with_sink(s, _sink_column(sink_ref, kv_head))
                p = p * inv_l
                delta = jnp.sum(dom.astype(F32) * om.astype(F32), axis=-1, keepdims=True)
                ds = (p * (_dot_nt(dom, vband) - delta)).astype(BF16)
                sink_term = -(e_sink * inv_l) * delta
                for j in range(4):
                    part = jnp.sum(sink_term[QBLOCK * j:QBLOCK * (j + 1)], axis=0, keepdims=True)
                    dsink = dsink + jnp.where(lane == 4 * kv_head + j, part, 0.0)
                pair0, pair1 = _unstack_heads(_dot(ds, kband))
                dq_ref[rows, base:base + 128] = pair0.astype(BF16)
                dq_ref[rows, base + 128:base + 256] = pair1.astype(BF16)
                dkd = _dot_tn(ds, qm)
                dkd = dkd + pltpu.roll(dkd, HEAD_DIM, 1)
                dvd = _dot_tn(p.astype(BF16), dom)
                dvd = dvd + pltpu.roll(dvd, HEAD_DIM, 1)
                dk_ref[pl.ds(prev, QBLOCK), :] += dkd[0:QBLOCK]
                dk_ref[pl.ds(own, QBLOCK), :] += dkd[QBLOCK:]
                dv_ref[pl.ds(prev, QBLOCK), :] += dvd[0:QBLOCK]
                dv_ref[pl.ds(own, QBLOCK), :] += dvd[QBLOCK:]
        dsink_ref[...] += dsink

    blk = pl.BlockSpec((nb * QBLOCK, Q_WIDTH), lambda i: (i, 0))
    full = _resident((seq, 128))
    acc = pl.BlockSpec((seq, 128), lambda i: (0, 0))
    return _pallas(
        body, name="attention_bwd", grid=(seq // (nb * QBLOCK),),
        in_specs=[pl.BlockSpec(memory_space=pltpu.SMEM), blk, blk, blk, full, full, full, full],
        out_specs=[blk, acc, acc, acc, acc, pl.BlockSpec((1, 128), lambda i: (0, 0))],
        out_shape=[SDS((seq, Q_WIDTH), BF16)] + [SDS((seq, 128), F32)] * 4 + [SDS((1, 128), F32)],
        operands=(sinks, q, dattn, attn, kd0, kd1, vd0, vd1), comm=comm)


def _in_proj_bwd(dq, dk0, dk1, dv0, dv1, dgb, dy, gc, xin, conv_w, x, dh, g_pre, w_in_t, rope):
    seq = x.shape[0]
    tb = min(seq, WIDE_TOKEN_TILE)
    n_tiles = seq // tb

    def body(dq_ref, dk0_ref, dk1_ref, dv0_ref, dv1_ref, dgb_ref, dy_ref, dyh_ref, gc_ref, xin_ref, cw_ref,
             x_ref, dh_ref, g_ref, w_ref, c_ref, sa_ref, sb_ref,
             dproj_ref, gx_ref, dg_ref):
        i = pl.program_id(0)

        @pl.when(i == 0)
        def _():
            dg_ref[...] = jnp.zeros_like(dg_ref)

        dy = dy_ref[...].astype(F32)
        ext = jnp.concatenate([dy, jnp.where(i == n_tiles - 1, 0.0, dyh_ref[...].astype(F32))], axis=0)
        dy1 = pltpu.roll(ext, tb + HALO - 1, 0)[0:tb]
        dy2 = pltpu.roll(ext, tb + HALO - 2, 0)[0:tb]
        cw = cw_ref[...]
        du = cw[2:3, :] * dy + cw[1:2, :] * dy1 + cw[0:1, :] * dy2
        scale = 1.0 / math.sqrt(HEAD_DIM)
        base = Q_WIDTH + 2 * KV_WIDTH
        halves = [slice(0, tb // 2), slice(tb // 2, tb)]
        low = _lane_lt64((tb // 2, 128))
        for rows in halves:
            c, sa, sb = _rope_tile(c_ref.at[rows, :], sa_ref, sb_ref)
            for p in range(Q_WIDTH // 128):
                dproj_ref[rows, 128 * p:128 * (p + 1)] = _rope_transposed(
                    dq_ref[rows, 128 * p:128 * (p + 1)].astype(F32) * scale, c, sa, sb).astype(BF16)
            dk = jnp.where(low, dk0_ref[rows, :], dk1_ref[rows, :])
            dproj_ref[rows, Q_WIDTH:Q_WIDTH + KV_WIDTH] = _rope_transposed(dk, c, sa, sb).astype(BF16)
            dproj_ref[rows, Q_WIDTH + KV_WIDTH:base] = jnp.where(low, dv0_ref[rows, :], dv1_ref[rows, :]).astype(BF16)
            dproj_ref[rows, base:base + CONV_WIDTH] = dgb_ref[rows, :]
            dproj_ref[rows, base + CONV_WIDTH:base + 2 * CONV_WIDTH] = (du[rows] * xin_ref[rows, :].astype(F32)).astype(BF16)
            dproj_ref[rows, base + 2 * CONV_WIDTH:] = (du[rows] * gc_ref[rows, :].astype(F32)).astype(BF16)
        w_all = w_ref[...].reshape(IN_COLS, D_MODEL)
        dhn = [_dot(dproj_ref[rows, :], w_all) for rows in halves]
        dg = jnp.zeros((1, D_MODEL), F32)
        for k, rows in enumerate(halves):
            xv = x_ref[rows, :]
            r = _rms(xv)
            xhat = xv * r
            dg = dg + _colsum(dhn[k] * xhat)
            gx_ref[rows, :] = dh_ref[rows, :].astype(F32) + _norm_bwd(dhn[k], g_ref[...], xhat, r)
        dg_ref[...] += dg

    tile = lambda w: pl.BlockSpec((tb, w), lambda i: (i, 0))
    halo_next = pl.BlockSpec((HALO, CONV_WIDTH), lambda i: (jnp.minimum((i + 1) * (tb // HALO), seq // HALO - 1), 0))
    return _pallas(
        body, name="in_proj_bwd", grid=(n_tiles,),
        in_specs=[tile(Q_WIDTH), tile(128), tile(128), tile(128), tile(128), tile(CONV_WIDTH), tile(CONV_WIDTH), halo_next,
                  tile(CONV_WIDTH), tile(CONV_WIDTH), _resident((CONV_K, CONV_WIDTH)),
                  tile(D_MODEL), tile(D_MODEL), _resident((1, D_MODEL)), _resident(w_in_t.shape), *_rope_specs(tb)],
        out_specs=[tile(IN_COLS), tile(D_MODEL), pl.BlockSpec((1, D_MODEL), lambda i: (0, 0))],
        out_shape=[SDS((seq, IN_COLS), BF16), SDS((seq, D_MODEL), F32), SDS((1, D_MODEL), F32)],
        operands=(dq, dk0, dk1, dv0, dv1, dgb, dy, dy, gc, xin, conv_w, x, dh, g_pre, w_in_t, *rope))


def _wgrad_grid(seq, per_chip, h_rows, with_rider=False):
    chips_per_step = 1 if per_chip else N_CHIPS
    m = chips_per_step * 2 * h_rows
    bt = min(seq, WGRAD_TOKEN_TILE if per_chip and not with_rider else WGRAD_TOKEN_TILE // 2)
    return chips_per_step, m, bt, seq // bt


def _wgrad(name, a, b, *, per_chip, h_rows, square_a=False, comm=None, rider=None):
    seq = a.shape[0]
    chips_per_step, m, bt, n_k = _wgrad_grid(seq, per_chip, h_rows, rider is not None)
    a_cols = m if per_chip else a.shape[1]
    a_wide = a.shape[1] > a_cols
    b_wide = b.shape[1] > D_MODEL

    def body(a_ref, b_ref, g_ref):
        @pl.when(pl.program_id(1) == 0)
        def _():
            g_ref[...] = jnp.zeros_like(g_ref)

        av = a_ref[...]
        if square_a:
            av = (av.astype(F32) * av.astype(F32)).astype(BF16)
        g_ref[...] += _dot_tn(av, b_ref[...]).reshape(g_ref.shape)

    a_spec = pl.BlockSpec((bt, a_cols), (lambda j, k: (k, j)) if a_wide else (lambda j, k: (k, 0)))
    b_spec = pl.BlockSpec((bt, D_MODEL), (lambda j, k: (k, j)) if b_wide else (lambda j, k: (k, 0)))
    g_spec = pl.BlockSpec((chips_per_step, 2, h_rows, D_MODEL), lambda j, k: (j, 0, 0, 0),
                          pipeline_mode=None if per_chip else pl.Buffered(1))
    return _pallas(
        body, name=name, grid=(N_CHIPS if per_chip else 1, n_k),
        in_specs=[a_spec, b_spec], out_specs=[g_spec], out_shape=[SDS((N_CHIPS, 2, h_rows, D_MODEL), F32)],
        operands=(a, b), comm=comm, rider=rider)


def _adamw_math(w, g, m, v):
    m = ADAM_B1 * m + (1.0 - ADAM_B1) * g
    v = ADAM_B2 * v + (1.0 - ADAM_B2) * (g * g)
    m_hat = m / (1.0 - ADAM_B1 ** ADAM_STEP)
    v_hat = v / (1.0 - ADAM_B2 ** ADAM_STEP)
    delta = -ADAM_LR * (m_hat / (jnp.sqrt(v_hat) + ADAM_EPS) + ADAM_WD * w)
    return delta, m, v


def _adamw_rows(name, reduced, w, m, v, rt):
    per_half = reduced.shape[1] // rt

    def body(r_ref, w_ref, m_ref, v_ref, g_out, d_out, m_out, v_out):
        g = r_ref[0]
        g_out[...] = g
        d_out[...], m_out[...], v_out[...] = _adamw_math(w_ref[...], g, m_ref[...], v_ref[...])

    blk = pl.BlockSpec((rt, D_MODEL), lambda h, r: (h * per_half + r, 0))
    return _pallas(
        body, name=name, grid=(2, per_half),
        in_specs=[pl.BlockSpec((1, rt, D_MODEL), lambda h, r: (h, r, 0)), blk, blk, blk],
        out_specs=[blk, blk, blk, blk], out_shape=[SDS(w.shape, F32)] * 4, operands=(reduced, w, m, v))


def _adamw_small(packed_grads, w, m, v):
    names = SMALL_NAMES
    n = len(names)
    conv_local = w["conv_w"].shape[-1]

    def body(*refs):
        gp = refs[0]
        w_refs, m_refs, v_refs = refs[1:1 + n], refs[1 + n:1 + 2 * n], refs[1 + 2 * n:1 + 3 * n]
        outs = refs[1 + 3 * n:]
        g_out, d_out, m_out, v_out = outs[0:n], outs[n:2 * n], outs[2 * n:3 * n], outs[3 * n:4 * n]
        chip = 2 * lax.axis_index("x") + lax.axis_index("y")

        def step(k, g, index=None):
            pick = (lambda r: r[...]) if index is None else (lambda r: r[index])
            d, new_m, new_v = _adamw_math(pick(w_refs[k]), g, pick(m_refs[k]), pick(v_refs[k]))
            for ref, val in ((g_out[k], g), (d_out[k], d), (m_out[k], new_m), (v_out[k], new_v)):
                if index is None:
                    ref[...] = val
                else:
                    ref[index] = val

        for k, name in enumerate(names):
            if name in SMALL_VECTORS:
                step(k, gp[SMALL_VECTORS.index(name):SMALL_VECTORS.index(name) + 1, :])
            elif name == "attn_group_norm":
                step(k, gp[4:5, 0:Q_WIDTH])
            elif name == "conv_group_norm":
                step(k, gp[4:5, Q_WIDTH:])
            elif name == "attn_sinks":
                step(k, gp[7:8, 0:8])
            else:
                for t in range(CONV_K):
                    row, base = 5 + t // 2, CONV_WIDTH * (t % 2)
                    g = gp[row:row + 1, base:base + conv_local]
                    for j in range(1, CONV_WIDTH // conv_local):
                        g = jnp.where(chip == j, gp[row:row + 1, base + conv_local * j:base + conv_local * (j + 1)], g)
                    step(k, g, index=(0, slice(t, t + 1), slice(None)))

    shapes = [SDS(w[name].shape, F32) for name in names]
    res = pl.pallas_call(
        body, name="adamw_small", in_specs=[VMEM_WHOLE] * (1 + 3 * n), out_specs=[VMEM_WHOLE] * (4 * n),
        out_shape=shapes * 4,
    )(packed_grads, *[w[k] for k in names], *[m[k] for k in names], *[v[k] for k in names])
    return [dict(zip(names, res[i * n:(i + 1) * n])) for i in range(4)]


SMALL_VECTORS = ("pre_mix_norm", "post_mix_norm", "pre_mlp_norm", "post_mlp_norm")
SMALL_NAMES = SMALL_VECTORS + ("attn_group_norm", "conv_group_norm", "conv_w", "attn_sinks")


def _pack_small(p):
    rows = [p[n].reshape(1, D_MODEL) for n in SMALL_VECTORS]
    rows.append(jnp.concatenate([p["attn_group_norm"].reshape(1, -1), p["conv_group_norm"].reshape(1, -1)], axis=1))
    cw = p["conv_w"].reshape(CONV_K, -1)
    rows.append(jnp.pad(cw, ((0, 1), (0, CONV_WIDTH - cw.shape[1]))).reshape(2, D_MODEL))
    last = jnp.concatenate([p["attn_sinks"].reshape(1, 8), p.get("loss_sum", jnp.zeros((1, 1), F32))], axis=1)
    rows.append(jnp.pad(last, ((0, 0), (0, D_MODEL - 9))))
    return jnp.concatenate(rows, axis=0)


WEIGHT_ORDER = ("pre_mix_norm", "w_in", "conv_w", "attn_sinks", "attn_group_norm", "conv_group_norm", "w_out",
                "post_mix_norm", "pre_mlp_norm", "w_up", "w_down", "post_mlp_norm")


def kernel(x, pre_mix_norm, w_in, conv_w, attn_sinks, attn_group_norm, conv_group_norm, w_out, post_mix_norm, pre_mlp_norm, w_up, w_down, post_mlp_norm, loss_target, m_pre_mix_norm, m_w_in, m_conv_w, m_attn_sinks, m_attn_group_norm, m_conv_group_norm, m_w_out, m_post_mix_norm, m_pre_mlp_norm, m_w_up, m_w_down, m_post_mlp_norm, v_pre_mix_norm, v_w_in, v_conv_w, v_attn_sinks, v_attn_group_norm, v_conv_group_norm, v_w_out, v_post_mix_norm, v_pre_mlp_norm, v_w_up, v_w_down, v_post_mlp_norm):
    w = dict(pre_mix_norm=pre_mix_norm, w_in=w_in, conv_w=conv_w, attn_sinks=attn_sinks, attn_group_norm=attn_group_norm,
             conv_group_norm=conv_group_norm, w_out=w_out, post_mix_norm=post_mix_norm, pre_mlp_norm=pre_mlp_norm,
             w_up=w_up, w_down=w_down, post_mlp_norm=post_mlp_norm)
    m = dict(pre_mix_norm=m_pre_mix_norm, w_in=m_w_in, conv_w=m_conv_w, attn_sinks=m_attn_sinks,
             attn_group_norm=m_attn_group_norm, conv_group_norm=m_conv_group_norm, w_out=m_w_out,
             post_mix_norm=m_post_mix_norm, pre_mlp_norm=m_pre_mlp_norm, w_up=m_w_up, w_down=m_w_down,
             post_mlp_norm=m_post_mlp_norm)
    v = dict(pre_mix_norm=v_pre_mix_norm, w_in=v_w_in, conv_w=v_conv_w, attn_sinks=v_attn_sinks,
             attn_group_norm=v_attn_group_norm, conv_group_norm=v_conv_group_norm, w_out=v_w_out,
             post_mix_norm=v_post_mix_norm, pre_mlp_norm=v_pre_mlp_norm, w_up=v_w_up, w_down=v_w_down,
             post_mlp_norm=v_post_mlp_norm)
    core = lax.axis_index("c").astype(jnp.int32).reshape(1)
    xs, target = x[0], loss_target[0]
    rope = _rope_inputs(xs.shape[0])

    hb_up, hb_down, hb_out, hb_in = _cast_halves(core, w_up[0], w_down[0], w_out[0], w_in[0].T)
    conv_pad = jnp.pad(conv_w[0], ((0, 8 - CONV_K), (0, 0)))
    wf_in, conv_all = _gather_whole(hb_in, conv_pad)
    conv_full = conv_all[:, :CONV_K, :].transpose(1, 0, 2).reshape(CONV_K, CONV_WIDTH)

    whole_up, early, late = (0, H_UP), (0, DOWN_EARLY_ROWS), (DOWN_EARLY_ROWS, H_DOWN - DOWN_EARLY_ROWS)
    *proj, wf_up, wf_out, wf_down = _in_proj(
        xs, pre_mix_norm, wf_in, rope,
        comm=_merge(_relay(hb_up, None, first=whole_up), _gather_first(hb_out), _relay(hb_down, None, first=early)))
    q, kd0, kd1, vd0, vd1, gb, gc, xin, hn = proj
    attn, wf_up, wf_out, wf_down = _attention_fwd(
        q, kd0, kd1, vd0, vd1, attn_sinks,
        comm=_merge(_relay(None, wf_up, second=whole_up), _gather_second(wf_out),
                    _relay(hb_down, wf_down, first=late, second=early)))
    mix, mixed, wf_up, wf_down = _mix_out(
        attn, gb, gc, xin, conv_full, attn_group_norm, conv_group_norm, wf_out,
        comm=_merge(_relay(None, wf_up, third=whole_up), _relay(None, wf_down, second=late, third=early, third_after=late)))
    up, hn2, dout, dmlp, loss_sum, dg_post_mlp = _mlp_loss(xs, mix, target, post_mix_norm, pre_mlp_norm, post_mlp_norm,
                                                           wf_up, wf_down)

    dup, dh, dmix, dg_pre_mlp, dg_post_mix = _mlp_bwd(dmlp, up, xs, dout, mix, pre_mlp_norm, post_mix_norm, wf_up, wf_down)
    n_k = _wgrad_grid(xs.shape[0], True, H_DOWN, with_rider=True)[3]
    g_down, dattn, dgb, dy, dg_attn, dg_conv, dconv_w = _wgrad(
        "wgrad_down", up, dmlp, per_chip=True, h_rows=H_DOWN, square_a=True,
        rider=_mix_bwd(dmix, attn, gb, gc, xin, conv_full, attn_group_norm, conv_group_norm, wf_out, n_k))
    g_up, got_down = _wgrad("wgrad_up", hn2, dup, per_chip=True, h_rows=H_UP, comm=_pair_send(g_down))
    p_down = _pair_sum("pair_sum_down", core, g_down, got_down)
    g_out, got_up = _wgrad("wgrad_out", mixed, dmix, per_chip=False, h_rows=H_OUT, comm=_pair_send(g_up))
    p_up = _pair_sum("pair_sum_up", core, g_up, got_up)
    dq, dk0, dk1, dv0, dv1, dsink, ex_down, ex_up, got_out = _attention_bwd(
        q, dattn, attn, kd0, kd1, vd0, vd1, attn_sinks,
        comm=_merge(_chip_exchange(p_down), _chip_exchange(p_up), _pair_send(g_out)))
    p_out = _pair_sum("pair_sum_out", core, g_out, got_out)
    dproj, grad_x, dg_pre_mix = _in_proj_bwd(dq, dk0, dk1, dv0, dv1, dgb, dy, gc, xin, conv_full, xs, dh, pre_mix_norm,
                                             wf_in, rope)
    g_in, ex_out = _wgrad("wgrad_in", dproj, hn, per_chip=False, h_rows=H_IN, comm=_chip_exchange(p_out))
    small = dict(pre_mix_norm=dg_pre_mix, conv_w=dconv_w, attn_sinks=dsink[:, :8], attn_group_norm=dg_attn,
                 conv_group_norm=dg_conv, post_mix_norm=dg_post_mix, pre_mlp_norm=dg_pre_mlp, post_mlp_norm=dg_post_mlp,
                 loss_sum=loss_sum)
    r_down, r_up, r_out, r_in, small_total = _tail_reduce(g_in, [ex_down, ex_up, ex_out], _pack_small(small))

    out_g, out_d, out_m, out_v = {}, {}, {}, {}
    out_g["w_up"], out_d["w_up"], out_m["w_up"], out_v["w_up"] = _adamw_rows(
        "adamw_up", r_up, w_up[0], m_w_up[0], v_w_up[0], 256)
    out_g["w_down"], out_d["w_down"], out_m["w_down"], out_v["w_down"] = _adamw_rows(
        "adamw_down", r_down, w_down[0], m_w_down[0], v_w_down[0], 256)
    out_g["w_out"], out_d["w_out"], out_m["w_out"], out_v["w_out"] = _adamw_rows(
        "adamw_out", r_out, w_out[0], m_w_out[0], v_w_out[0], H_OUT)
    in_t = _adamw_rows("adamw_in", r_in, w_in[0].T, m_w_in[0].T, v_w_in[0].T, H_IN)
    out_g["w_in"], out_d["w_in"], out_m["w_in"], out_v["w_in"] = [t.T for t in in_t]

    loss = small_total[7, 8] * (0.5 / D_MODEL)
    for out, part in zip((out_g, out_d, out_m, out_v), _adamw_small(small_total, w, m, v)):
        out.update(part)

    def shaped(d):
        return [d[n].reshape(w[n].shape) for n in WEIGHT_ORDER]

    return (loss, grad_x[None], *shaped(out_g), *shaped(out_d), *shaped(out_m), *shaped(out_v))
```

```python
import math
from typing import Callable, NamedTuple

import jax
import jax.numpy as jnp
import numpy as np
from jax import lax
from jax.experimental import pallas as pl
from jax.experimental.pallas import tpu as pltpu
from jax.experimental.pallas import tpu_sc as plsc

F32 = jnp.float32
BF16 = jnp.bfloat16

D_MODEL = 1024
HEAD_DIM = 64
Q_WIDTH = 512
KV_WIDTH = 128
CONV_WIDTH = 512
CONV_K = 3
D_FF = 4096
IN_COLS = 2304
QBLOCK = 128
ROT_DIM = 16
ROPE_THETA = 500000.0
NORM_EPS = 1e-6
NEG_INF = -1e30
N_CHIPS = 4

ADAM_LR = 0.001
ADAM_B1 = 0.9
ADAM_B2 = 0.999
ADAM_EPS = 1e-08
ADAM_WD = 0.01
ADAM_STEP = 10

H_UP, H_DOWN, H_OUT, H_IN = 512, 512, 128, 288
DOWN_EARLY_ROWS = 224

TOKEN_TILE = 512
WIDE_TOKEN_TILE = 1024
MLP_BWD_TOKEN_TILE = 512
MLP_BWD_SUB_TILE = 256
ATTN_FWD_BLOCKS = 16
ATTN_BWD_BLOCKS = 2
WGRAD_TOKEN_TILE = 4096
VMEM_LIMIT_V7X = 56 * 1024 * 1024

MESH = pl.DeviceIdType.MESH
ANY = pl.BlockSpec(memory_space=pl.ANY)
VMEM_WHOLE = pl.BlockSpec(memory_space=pltpu.VMEM)
SDS = jax.ShapeDtypeStruct


def _resident(shape):
    zeros = (0,) * len(shape)
    return pl.BlockSpec(shape, lambda *_: zeros, pipeline_mode=pl.Buffered(1))


def _rms(v):
    return lax.rsqrt(jnp.mean(v * v, axis=-1, keepdims=True) + NORM_EPS)


def _norm_bwd(dy, gain, vhat, rstd):
    t = dy * gain
    return rstd * (t - vhat * jnp.mean(t * vhat, axis=-1, keepdims=True))


def _colsum(v):
    return jnp.sum(v, axis=0, keepdims=True)


def _dot_nt(a, b):
    return lax.dot_general(a, b, (((1,), (1,)), ((), ())), preferred_element_type=F32)


def _dot_tn(a, b):
    return lax.dot_general(a, b, (((0,), (0,)), ((), ())), preferred_element_type=F32)


def _dot(a, b):
    return jnp.dot(a, b, preferred_element_type=F32)


def _chip_block(w_ref, chip):
    both = w_ref[pl.ds(2 * chip, 2)]
    return both.reshape(2 * both.shape[1], both.shape[2])


def _lane_lt64(shape):
    return lax.broadcasted_iota(jnp.int32, shape, 1) < HEAD_DIM


class _Comm(NamedTuple):
    operands: tuple
    out_shapes: tuple
    aliases: dict
    n_remote: int
    n_local: int
    plan: Callable
    after: Callable = None


def _merge(*comms):
    operands, out_shapes, aliases, parts = [], [], {}, []
    n_remote = n_local = 0
    for cm in comms:
        parts.append((len(operands), len(out_shapes), n_remote, n_local, cm))
        for k, v in cm.aliases.items():
            aliases[len(operands) + k] = len(out_shapes) + v
        operands += cm.operands
        out_shapes += cm.out_shapes
        n_remote += cm.n_remote
        n_local += cm.n_local

    def run(which, ins, outs, send, recv, loc):
        sends, recvs, locs = [], [], []
        for i0, o0, r0, l0, cm in parts:
            stage = getattr(cm, which)
            if stage is not None:
                s, r, l = stage(ins[i0:i0 + len(cm.operands)], outs[o0:o0 + len(cm.out_shapes)],
                                lambda k, r0=r0: send(r0 + k), lambda k, r0=r0: recv(r0 + k), lambda k, l0=l0: loc(l0 + k))
                sends, recvs, locs = sends + s, recvs + r, locs + l
        return sends, recvs, locs

    def plan(*args):
        return run("plan", *args)

    def after(*args):
        return run("after", *args)

    return _Comm(tuple(operands), tuple(out_shapes), aliases, n_remote, n_local, plan,
                 after if any(cm.after is not None for cm in comms) else None)


def _sem_scratch(comm):
    return [pltpu.SemaphoreType.DMA((max(comm.n_remote, 1),)), pltpu.SemaphoreType.DMA((max(comm.n_remote, 1),)),
            pltpu.SemaphoreType.DMA((max(comm.n_local, 1),))]


class _Rider(NamedTuple):
    body: Callable
    in_specs: list
    out_specs: list
    out_shape: list
    operands: tuple


def _pallas(body, *, name, grid, in_specs, out_specs, out_shape, operands, scratch=(), comm=None, rider=None):
    params = pltpu.CompilerParams(dimension_semantics=("arbitrary",) * len(grid), vmem_limit_bytes=VMEM_LIMIT_V7X)
    if rider is not None:
        own_in, own_out, ride_in, ride_out = len(in_specs), len(out_specs), len(rider.in_specs), len(rider.out_specs)
        own_body = body

        def body(*refs):
            o0 = own_in + ride_in
            s0 = o0 + own_out + ride_out
            own_body(*refs[:own_in], *refs[o0:o0 + own_out], *refs[s0:])
            first = None
            for axis in range(len(grid)):
                at_start = pl.program_id(axis) == 0
                first = at_start if first is None else jnp.logical_and(first, at_start)
            rider.body(first, *refs[own_in:o0], *refs[o0 + own_out:s0])

        in_specs, out_specs = list(in_specs) + rider.in_specs, list(out_specs) + rider.out_specs
        out_shape, operands = list(out_shape) + rider.out_shape, tuple(operands) + tuple(rider.operands)
    if comm is None:
        return pl.pallas_call(body, name=name, grid=grid, in_specs=in_specs, out_specs=out_specs, out_shape=out_shape,
                              scratch_shapes=list(scratch), compiler_params=params)(*operands)
    n_in, n_out, n_scr = len(in_specs), len(out_specs), len(scratch)
    c_in, c_out = len(comm.operands), len(comm.out_shapes)

    def with_comm(*refs):
        ins, c_ins = refs[:n_in], refs[n_in:n_in + c_in]
        o0 = n_in + c_in
        outs, c_outs = refs[o0:o0 + n_out], refs[o0 + n_out:o0 + n_out + c_out]
        s0 = o0 + n_out + c_out
        scr = refs[s0:s0 + n_scr]
        send_sems, recv_sems, local_sems = refs[s0 + n_scr:]
        first = last = None
        for axis, size in enumerate(grid):
            at_start, at_end = pl.program_id(axis) == 0, pl.program_id(axis) == size - 1
            first = at_start if first is None else jnp.logical_and(first, at_start)
            last = at_end if last is None else jnp.logical_and(last, at_end)

        def copies():
            return comm.plan(c_ins, c_outs, lambda k: send_sems.at[k], lambda k: recv_sems.at[k],
                             lambda k: local_sems.at[k])

        @pl.when(first)
        def _():
            sends, _, locs = copies()
            for cp in sends + locs:
                cp.start()

        body(*ins, *outs, *scr)

        @pl.when(last)
        def _():
            sends, recvs, locs = copies()
            for cp in recvs:
                cp.wait_recv()
            for cp in sends:
                cp.wait_send()
            for cp in locs:
                cp.wait()
            if comm.after is not None:
                sends, recvs, _ = comm.after(c_ins, c_outs, lambda k: send_sems.at[k], lambda k: recv_sems.at[k],
                                             lambda k: local_sems.at[k])
                for cp in sends:
                    cp.start()
                for cp in recvs:
                    cp.wait_recv()
                for cp in sends:
                    cp.wait_send()

    return pl.pallas_call(
        with_comm, name=name, grid=grid,
        in_specs=list(in_specs) + [ANY] * c_in, out_specs=list(out_specs) + [ANY] * c_out,
        out_shape=list(out_shape) + list(comm.out_shapes),
        scratch_shapes=list(scratch) + _sem_scratch(comm),
        input_output_aliases={n_in + k: n_out + v for k, v in comm.aliases.items()},
        compiler_params=params)(*operands, *comm.operands)


def _place():
    return lax.axis_index("x"), lax.axis_index("y"), lax.axis_index("c")


def _other_chips(x, y):
    return [(1 - x, y), (x, 1 - y), (1 - x, 1 - y)]


def _slot(px, py, pc):
    return 4 * px + 2 * py + pc


def _remote(src, dst, send_sem, recv_sem, to):
    return pltpu.make_async_remote_copy(src_ref=src, dst_ref=dst, send_sem=send_sem, recv_sem=recv_sem,
                                        device_id=to, device_id_type=MESH)


def _gather_first(half_block):
    def plan(ins, outs, send, recv, loc):
        (blk,), (full,) = ins, outs
        x, y, c = _place()
        chips = _other_chips(x, y)
        mine = full.at[_slot(x, y, c)]
        sends = [_remote(blk, mine, send(0), recv(0), (x, y, 1 - c))]
        sends += [_remote(blk, mine, send(1 + j), recv(1 + j), (*chip, c)) for j, chip in enumerate(chips)]
        recvs = [_remote(blk, full.at[_slot(x, y, 1 - c)], send(0), recv(0), (x, y, 1 - c))]
        recvs += [_remote(blk, full.at[_slot(*chip, c)], send(1 + j), recv(1 + j), (*chip, c))
                  for j, chip in enumerate(chips)]
        return sends, recvs, [pltpu.make_async_copy(blk, mine, loc(0))]

    return _Comm((half_block,), (SDS((2 * N_CHIPS,) + half_block.shape, half_block.dtype),), {}, 4, 1, plan)


def _gather_second(partly_gathered):
    def plan(ins, outs, send, recv, loc):
        (src,), (full,) = ins, outs
        x, y, c = _place()
        chips = _other_chips(x, y)
        sends = [_remote(src.at[_slot(*chip, c)], full.at[_slot(*chip, c)], send(j), recv(j), (x, y, 1 - c))
                 for j, chip in enumerate(chips)]
        recvs = [_remote(src.at[_slot(*chip, 1 - c)], full.at[_slot(*chip, 1 - c)], send(j), recv(j), (x, y, 1 - c))
                 for j, chip in enumerate(chips)]
        return sends, recvs, []

    return _Comm((partly_gathered,), (SDS(partly_gathered.shape, partly_gathered.dtype),), {0: 0}, 3, 0, plan)


def _relay_pieces(full, rows, x, y, c):
    start, half = rows[0], rows[1] // 2
    upper, lower = pl.ds(start, half), pl.ds(start + half, half)
    diagonal = full.at[_slot(1 - x, 1 - y, c)]
    return [(full.at[_slot(1 - x, y, c), upper], diagonal.at[upper], (x, 1 - y, c)),
            (full.at[_slot(x, 1 - y, c), lower], diagonal.at[lower], (1 - x, y, c))]


def _relay(half_block, so_far, first=None, second=None, third=None, third_after=None):
    has_block, has_buffer = half_block is not None, so_far is not None
    shape = so_far.shape if has_buffer else (2 * N_CHIPS,) + half_block.shape
    dtype = so_far.dtype if has_buffer else half_block.dtype

    def third_leg(rows, k, ins, outs, send, recv):
        src, full = (ins[-1] if has_buffer else outs[0]), outs[0]
        x, y, c = _place()
        span, sibling = pl.ds(*rows), (x, y, 1 - c)
        here, there = _slot(1 - x, 1 - y, c), _slot(1 - x, 1 - y, 1 - c)
        return ([_remote(src.at[here, span], full.at[here, span], send(k), recv(k), sibling)],
                [_remote(src.at[there, span], full.at[there, span], send(k), recv(k), sibling)])

    def plan(ins, outs, send, recv, loc):
        src, full = (ins[-1] if has_buffer else outs[0]), outs[0]
        x, y, c = _place()
        sibling = (x, y, 1 - c)
        sends, recvs, locs = [], [], []
        if first is not None:
            span = pl.ds(*first)
            blk, mine = ins[0].at[span], full.at[_slot(x, y, c), span]
            for k, peer in enumerate([sibling, (1 - x, y, c), (x, 1 - y, c)]):
                sends.append(_remote(blk, mine, send(k), recv(k), peer))
                recvs.append(_remote(blk, full.at[_slot(*peer), span], send(k), recv(k), peer))
            locs.append(pltpu.make_async_copy(blk, mine, loc(0)))
        if second is not None:
            span = pl.ds(*second)
            for k, chip in enumerate([(1 - x, y), (x, 1 - y)]):
                sends.append(_remote(src.at[_slot(*chip, c), span], full.at[_slot(*chip, c), span], send(3 + k), recv(3 + k),
                                     sibling))
                recvs.append(_remote(src.at[_slot(*chip, 1 - c), span], full.at[_slot(*chip, 1 - c), span], send(3 + k),
                                     recv(3 + k), sibling))
            for k, (piece, lands, peer) in enumerate(_relay_pieces(full, second, x, y, c)):
                sends.append(_remote(piece, piece, send(5 + k), recv(5 + k), peer))
                recvs.append(_remote(lands, lands, send(5 + k), recv(5 + k), peer))
        if third is not None:
            s, r = third_leg(third, 7, ins, outs, send, recv)
            sends, recvs = sends + s, recvs + r
        return sends, recvs, locs

    def after(ins, outs, send, recv, loc):
        s, r = third_leg(third_after, 8, ins, outs, send, recv)
        return s, r, []

    operands = ((half_block,) if has_block else ()) + ((so_far,) if has_buffer else ())
    return _Comm(operands, (SDS(shape, dtype),), {len(operands) - 1: 0} if has_buffer else {}, 9, 1, plan,
                 after if third_after is not None else None)


def _gather_whole(half_block, small_block):
    rows = half_block.shape[0]

    def body(blk_ref, small_ref, out_ref, small_out_ref, send_sems, recv_sems, local_sems):
        x, y, c = _place()
        me, sibling = (x, y, c), (x, y, 1 - c)
        neighbours, diagonal = [(1 - x, y), (x, 1 - y)], (1 - x, 1 - y)

        def copy(k, block, to, src=None):
            return _remote(out_ref.at[_slot(*block)] if src is None else src, out_ref.at[_slot(*block)],
                           send_sems.at[k], recv_sems.at[k], to)

        def small_copy(k, chip, to):
            return _remote(small_ref, small_out_ref.at[2 * chip[0] + chip[1]], send_sems.at[8 + k], recv_sems.at[8 + k], to)

        mine = pltpu.make_async_copy(blk_ref, out_ref.at[_slot(*me)], local_sems.at[0])
        mine_small = pltpu.make_async_copy(small_ref, small_out_ref.at[2 * x + y], local_sems.at[1])
        mine.start()
        mine_small.start()
        started = [copy(0, me, sibling, src=blk_ref)]
        started += [copy(1 + k, me, (*chip, c), src=blk_ref) for k, chip in enumerate(neighbours)]
        started += [small_copy(k, (x, y), (*chip, c)) for k, chip in enumerate(neighbours + [diagonal])]
        for cp in started:
            cp.start()
        pieces = _relay_pieces(out_ref, (0, rows), x, y, c)
        for k, chip in enumerate(neighbours):
            copy(1 + k, (*chip, c), me).wait_recv()
            piece, _, peer = pieces[k]
            started += [copy(3 + k, (*chip, c), sibling), _remote(piece, piece, send_sems.at[5 + k], recv_sems.at[5 + k], peer)]
            started[-2].start()
            started[-1].start()
        for k, (_, lands, peer) in enumerate(pieces):
            _remote(lands, lands, send_sems.at[5 + k], recv_sems.at[5 + k], peer).wait_recv()
        started.append(copy(7, (*diagonal, c), sibling))
        started[-1].start()
        copy(0, sibling, me).wait_recv()
        for k, chip in enumerate(neighbours):
            copy(3 + k, (*chip, 1 - c), me).wait_recv()
        copy(7, (*diagonal, 1 - c), me).wait_recv()
        for k, chip in enumerate(neighbours + [diagonal]):
            small_copy(k, chip, me).wait_recv()
        for cp in started:
            cp.wait_send()
        mine.wait()
        mine_small.wait()

    return pl.pallas_call(
        body, name="gather_whole", in_specs=[ANY, ANY], out_specs=[ANY, ANY],
        out_shape=[SDS((2 * N_CHIPS,) + half_block.shape, half_block.dtype),
                   SDS((N_CHIPS,) + small_block.shape, small_block.dtype)],
        scratch_shapes=[pltpu.SemaphoreType.DMA((11,)), pltpu.SemaphoreType.DMA((11,)), pltpu.SemaphoreType.DMA((2,))],
    )(half_block, small_block)


def _pair_send(grads):
    def plan(ins, outs, send, recv, loc):
        (g,), (got,) = ins, outs
        x, y, c = _place()
        copies = [_remote(g.at[j, 1 - c], got.at[j], send(j), recv(j), (x, y, 1 - c)) for j in range(N_CHIPS)]
        return copies, copies, []

    shape = (grads.shape[0],) + grads.shape[2:]
    return _Comm((grads,), (SDS(shape, grads.dtype),), {}, N_CHIPS, 0, plan)


def _chip_exchange(partial):
    def plan(ins, outs, send, recv, loc):
        (p,), (got,) = ins, outs
        x, y, c = _place()
        my_chip = 2 * x + y
        chips = _other_chips(x, y)
        sends = [_remote(p.at[2 * chip[0] + chip[1]], got.at[my_chip], send(j), recv(j), (*chip, c))
                 for j, chip in enumerate(chips)]
        recvs = [_remote(p.at[my_chip], got.at[2 * chip[0] + chip[1]], send(j), recv(j), (*chip, c))
                 for j, chip in enumerate(chips)]
        return sends, recvs, [pltpu.make_async_copy(p.at[my_chip], got.at[my_chip], loc(0))]

    return _Comm((partial,), (SDS(partial.shape, partial.dtype),), {}, 3, 1, plan)


def _pair_sum(name, core, grads, received):
    h = grads.shape[2]

    def body(core_ref, g_ref, r_ref, o_ref):
        o_ref[...] = (g_ref[0] + r_ref[...]).astype(BF16)

    return pl.pallas_call(
        body, name=name,
        grid_spec=pltpu.PrefetchScalarGridSpec(
            num_scalar_prefetch=1, grid=(N_CHIPS,),
            in_specs=[pl.BlockSpec((1, 1, h, D_MODEL), lambda j, core_ref: (j, core_ref[0], 0, 0)),
                      pl.BlockSpec((1, h, D_MODEL), lambda j, core_ref: (j, 0, 0))],
            out_specs=pl.BlockSpec((1, h, D_MODEL), lambda j, core_ref: (j, 0, 0))),
        out_shape=SDS((N_CHIPS, h, D_MODEL), BF16),
        compiler_params=pltpu.CompilerParams(dimension_semantics=("arbitrary",), vmem_limit_bytes=VMEM_LIMIT_V7X),
    )(core, grads, received)


SMALL_ROWS = 8


def _sum_blocks(ref):
    return (ref[0].astype(F32) + ref[1].astype(F32)) + (ref[2].astype(F32) + ref[3].astype(F32))


def _tail_reduce(last_grads, exchanged, small):
    n = len(exchanged)
    h = last_grads.shape[2]

    def body(*refs):
        g_ref, ex, small_ref = refs[0], refs[1:1 + n], refs[1 + n]
        o0 = 2 + n
        out, out_last, small_out = refs[o0:o0 + n], refs[o0 + n], refs[o0 + n + 1]
        s0 = o0 + n + 2
        halves, half_last = refs[s0:s0 + n], refs[s0 + n]
        own, got, part, exch, small_buf = refs[s0 + n + 1:s0 + n + 6]
        pair_send, pair_recv, chip_send, chip_recv, share_send, share_recv, small_send, small_recv, local_sems = refs[s0 + n + 6:]
        x, y, c = _place()
        sibling = (x, y, 1 - c)
        my_chip, me = 2 * x + y, _slot(x, y, c)
        chips = _other_chips(x, y)

        to_sibling = [_remote(g_ref.at[j, 1 - c], got.at[j], pair_send.at[j], pair_recv.at[j], sibling)
                      for j in range(N_CHIPS)]
        load_own = [pltpu.make_async_copy(g_ref.at[j, c], own.at[j], local_sems.at[j]) for j in range(N_CHIPS)]
        for cp in to_sibling + load_own:
            cp.start()

        small_buf[me] = small_ref[...]
        small_copies = []
        for mask in range(1, 8):
            peer = (x ^ (mask >> 2), y ^ ((mask >> 1) & 1), c ^ (mask & 1))
            small_copies.append(_remote(small_ref, small_buf.at[me], small_send.at[mask - 1], small_recv.at[mask - 1], peer))
        for cp in small_copies:
            cp.start()

        def share(k, half_ref, out_ref):
            keep = pltpu.make_async_copy(half_ref, out_ref.at[c], local_sems.at[N_CHIPS + k])
            give = _remote(half_ref, out_ref.at[c], share_send.at[k], share_recv.at[k], sibling)
            take = _remote(half_ref, out_ref.at[1 - c], share_send.at[k], share_recv.at[k], sibling)
            keep.start()
            give.start()
            return keep, give, take

        shares = []
        for k in range(n):
            halves[k][...] = _sum_blocks(ex[k])
            shares.append(share(k, halves[k], out[k]))

        def pair_sum(block):
            _remote(g_ref.at[block, 1 - c], got.at[block], pair_send.at[block], pair_recv.at[block], sibling).wait_recv()
            pltpu.make_async_copy(g_ref.at[block, c], own.at[block], local_sems.at[block]).wait()
            part[block] = (own[block] + got[block]).astype(BF16)

        to_chips = []
        for j, chip in enumerate(chips):
            block = 2 * chip[0] + chip[1]
            pair_sum(block)
            to_chips.append(_remote(part.at[block], exch.at[my_chip], chip_send.at[j], chip_recv.at[j], (*chip, c)))
            to_chips[-1].start()
        pair_sum(my_chip)
        exch[my_chip] = part[my_chip]
        from_chips = [_remote(part.at[my_chip], exch.at[2 * chip[0] + chip[1]], chip_send.at[j], chip_recv.at[j], (*chip, c))
                      for j, chip in enumerate(chips)]

        for cp in small_copies:
            cp.wait_recv()
        total = small_buf[0]
        for d in range(1, 8):
            total = total + small_buf[d]
        small_out[...] = total

        for cp in from_chips:
            cp.wait_recv()
        half_last[...] = _sum_blocks(exch)
        shares.append(share(n, half_last, out_last))

        for keep, give, take in shares:
            take.wait_recv()
            give.wait_send()
            keep.wait()
        for cp in to_sibling + to_chips + small_copies:
            cp.wait_send()

    blocks = (N_CHIPS, h, D_MODEL)
    return pl.pallas_call(
        body, name="tail_reduce",
        in_specs=[ANY] + [VMEM_WHOLE] * (n + 1), out_specs=[ANY] * (n + 1) + [VMEM_WHOLE],
        out_shape=[SDS((2,) + e.shape[1:], F32) for e in exchanged] + [SDS((2, h, D_MODEL), F32), SDS(small.shape, F32)],
        scratch_shapes=[pltpu.VMEM(e.shape[1:], F32) for e in exchanged] + [pltpu.VMEM((h, D_MODEL), F32)]
                       + [pltpu.VMEM(blocks, F32), pltpu.VMEM(blocks, F32), pltpu.VMEM(blocks, BF16), pltpu.VMEM(blocks, BF16),
                          pltpu.VMEM((8,) + small.shape, F32)]
                       + [pltpu.SemaphoreType.DMA((N_CHIPS,)), pltpu.SemaphoreType.DMA((N_CHIPS,)),
                          pltpu.SemaphoreType.DMA((3,)), pltpu.SemaphoreType.DMA((3,)),
                          pltpu.SemaphoreType.DMA((n + 1,)), pltpu.SemaphoreType.DMA((n + 1,)),
                          pltpu.SemaphoreType.DMA((7,)), pltpu.SemaphoreType.DMA((7,)),
                          pltpu.SemaphoreType.DMA((N_CHIPS + n + 1,))],
        compiler_params=pltpu.CompilerParams(vmem_limit_bytes=VMEM_LIMIT_V7X),
    )(last_grads, *exchanged, small)


def _rope_expansion():
    half = ROT_DIM // 2
    expand = np.zeros((2 * half, 3 * 128), np.float32)
    const = np.zeros((1, 3 * 128), np.float32)
    for lane in range(128):
        d = lane % HEAD_DIM
        if d < ROT_DIM:
            expand[d % half, lane] = 1.0
        else:
            const[0, lane] = 1.0
        if d < half:
            expand[half + d, 128 + lane] = -1.0
        elif d < ROT_DIM:
            expand[half + d - half, 256 + lane] = 1.0
    return expand, const


ROPE_PIECES = 3 * ROT_DIM


def _rope_inputs(seq):
    pos = jnp.arange(seq, dtype=F32)
    inv_freq = ROPE_THETA ** (-jnp.arange(0, ROT_DIM, 2, dtype=F32) / ROT_DIM)
    ang = pos[:, None] * inv_freq[None, :]
    cs = jnp.concatenate([jnp.cos(ang), jnp.sin(ang)], axis=1)
    hi = lax.reduce_precision(cs, 8, 7)
    mid = lax.reduce_precision(cs - hi, 8, 7)
    low = cs - hi - mid
    expand, const = _rope_expansion()
    pieces = jnp.concatenate([hi, mid, low], axis=1).astype(BF16)
    return pieces, jnp.asarray(np.concatenate([expand] * 3, axis=0), BF16), jnp.asarray(const)


def _rope_specs(tb):
    return [pl.BlockSpec((tb, ROPE_PIECES), lambda i: (i, 0)), _resident((ROPE_PIECES, 3 * 128)), _resident((1, 3 * 128))]


def _rope_tile(pieces_ref, expand_ref, const_ref):
    tables = _dot(pieces_ref[...], expand_ref[...]) + const_ref[...]
    return tables[:, 0:128], tables[:, 128:256], tables[:, 256:384]


def _rope(t, c, sa, sb):
    half = ROT_DIM // 2
    return t * c + pltpu.roll(t, 128 - half, 1) * sa + pltpu.roll(t, half, 1) * sb


def _rope_transposed(dt, c, sa, sb):
    half = ROT_DIM // 2
    return dt * c + pltpu.roll(dt * sa, half, 1) + pltpu.roll(dt * sb, 128 - half, 1)


def _cast_halves(core, w_up, w_down, w_out, w_in_t):
    def body(core_ref, up_ref, down_ref, out_ref, in_ref, up_o, down_o, out_o, in_o):
        up_o[...] = up_ref[...].astype(BF16)
        down_o[...] = down_ref[...].astype(BF16)
        out_o[...] = out_ref[...].astype(BF16)
        in_o[...] = in_ref[...].astype(BF16)

    half = lambda rows: pl.BlockSpec((rows, D_MODEL), lambda i, core_ref: (core_ref[0], 0))
    whole = lambda rows: pl.BlockSpec((rows, D_MODEL), lambda i, core_ref: (0, 0))
    rows = (H_UP, H_DOWN, H_OUT, H_IN)
    return pl.pallas_call(
        body, name="cast_halves",
        grid_spec=pltpu.PrefetchScalarGridSpec(
            num_scalar_prefetch=1, grid=(1,), in_specs=[half(r) for r in rows], out_specs=[whole(r) for r in rows]),
        out_shape=[SDS((r, D_MODEL), BF16) for r in rows],
        compiler_params=pltpu.CompilerParams(dimension_semantics=("arbitrary",), vmem_limit_bytes=VMEM_LIMIT_V7X),
    )(core, w_up, w_down, w_out, w_in_t)


def _in_proj(x, g_pre, w_in_t, rope, comm=None):
    seq = x.shape[0]
    tb = min(seq, WIDE_TOKEN_TILE)

    def body(x_ref, g_ref, w_ref, c_ref, sa_ref, sb_ref,
             q_ref, kd0_ref, kd1_ref, vd0_ref, vd1_ref, gb_ref, gc_ref, xin_ref, hn_ref):
        xv = x_ref[...]
        hn = (xv * _rms(xv) * g_ref[...]).astype(BF16)
        hn_ref[...] = hn
        proj = _dot_nt(hn, w_ref[...].reshape(IN_COLS, D_MODEL))
        c, sa, sb = _rope_tile(c_ref, sa_ref, sb_ref)
        scale = 1.0 / math.sqrt(HEAD_DIM)
        for p in range(Q_WIDTH // 128):
            q_ref[:, 128 * p:128 * (p + 1)] = (_rope(proj[:, 128 * p:128 * (p + 1)], c, sa, sb) * scale).astype(BF16)
        k = _rope(proj[:, Q_WIDTH:Q_WIDTH + KV_WIDTH], c, sa, sb)
        v = proj[:, Q_WIDTH + KV_WIDTH:Q_WIDTH + 2 * KV_WIDTH]
        low = _lane_lt64(k.shape)
        k_sw, v_sw = pltpu.roll(k, HEAD_DIM, 1), pltpu.roll(v, HEAD_DIM, 1)
        kd0_ref[...] = jnp.where(low, k, k_sw).astype(BF16)
        kd1_ref[...] = jnp.where(low, k_sw, k).astype(BF16)
        vd0_ref[...] = jnp.where(low, v, v_sw).astype(BF16)
        vd1_ref[...] = jnp.where(low, v_sw, v).astype(BF16)
        base = Q_WIDTH + 2 * KV_WIDTH
        gb_ref[...] = proj[:, base:base + CONV_WIDTH].astype(BF16)
        gc_ref[...] = proj[:, base + CONV_WIDTH:base + 2 * CONV_WIDTH].astype(BF16)
        xin_ref[...] = proj[:, base + 2 * CONV_WIDTH:base + 3 * CONV_WIDTH].astype(BF16)

    tile = lambda w: pl.BlockSpec((tb, w), lambda i: (i, 0))
    return _pallas(
        body, name="in_proj", grid=(seq // tb,),
        in_specs=[tile(D_MODEL), _resident((1, D_MODEL)), _resident(w_in_t.shape), *_rope_specs(tb)],
        out_specs=[tile(Q_WIDTH), tile(128), tile(128), tile(128), tile(128),
                   tile(CONV_WIDTH), tile(CONV_WIDTH), tile(CONV_WIDTH), tile(D_MODEL)],
        out_shape=[SDS((seq, Q_WIDTH), BF16)] + [SDS((seq, 128), BF16)] * 4
                  + [SDS((seq, CONV_WIDTH), BF16)] * 3 + [SDS((seq, D_MODEL), BF16)],
        operands=(x, g_pre, w_in_t, *rope), comm=comm)


def _attn_valid(i):
    shape = (4 * QBLOCK, 2 * QBLOCK)
    row = lax.broadcasted_iota(jnp.int32, shape, 0)
    col = lax.broadcasted_iota(jnp.int32, shape, 1)
    qi = row & (QBLOCK - 1)
    return (col > qi) & (col <= qi + QBLOCK) & ((col >= QBLOCK) | (i > 0))


def _stack_heads(pair0, pair1):
    low = _lane_lt64(pair0.shape)
    zero = jnp.zeros_like(pair0)
    return jnp.concatenate([jnp.where(low, pair0, zero), jnp.where(low, zero, pair0),
                            jnp.where(low, pair1, zero), jnp.where(low, zero, pair1)], axis=0)


def _unstack_heads(stacked):
    low = _lane_lt64((QBLOCK, 128))
    pair0 = jnp.where(low, stacked[0:QBLOCK], stacked[QBLOCK:2 * QBLOCK])
    pair1 = jnp.where(low, stacked[2 * QBLOCK:3 * QBLOCK], stacked[3 * QBLOCK:4 * QBLOCK])
    return pair0, pair1


def _sink_column(sink_ref, kv_head):
    row = lax.broadcasted_iota(jnp.int32, (4 * QBLOCK, 1), 0)
    s = [sink_ref[0, 4 * kv_head + j] for j in range(4)]
    return jnp.where(row < QBLOCK, s[0], jnp.where(row < 2 * QBLOCK, s[1], jnp.where(row < 3 * QBLOCK, s[2], s[3])))


def _band(ref, i):
    prev = pl.multiple_of(jnp.maximum(i - 1, 0) * QBLOCK, QBLOCK)
    own = pl.multiple_of(i * QBLOCK, QBLOCK)
    return jnp.concatenate([ref[pl.ds(prev, QBLOCK), :], ref[pl.ds(own, QBLOCK), :]], axis=0), prev, own


def _softmax_with_sink(s, sink_col):
    m = jnp.maximum(jnp.max(s, axis=-1, keepdims=True), sink_col)
    p = jnp.exp(s - m)
    e_sink = jnp.exp(sink_col - m)
    inv_l = 1.0 / (jnp.sum(p, axis=-1, keepdims=True) + e_sink)
    return p, e_sink, inv_l


def _attention_fwd(q, kd0, kd1, vd0, vd1, sinks, comm=None):
    seq = q.shape[0]

    nb = ATTN_FWD_BLOCKS

    def body(sink_ref, q_ref, kd0_ref, kd1_ref, vd0_ref, vd1_ref, o_ref):
        for b in range(nb):
            i = pl.program_id(0) * nb + b
            rows = slice(QBLOCK * b, QBLOCK * (b + 1))
            valid = _attn_valid(i)
            for kv_head, (k_ref, v_ref) in enumerate(((kd0_ref, vd0_ref), (kd1_ref, vd1_ref))):
                kband, _, _ = _band(k_ref, i)
                vband, _, _ = _band(v_ref, i)
                base = 256 * kv_head
                qm = _stack_heads(q_ref[rows, base:base + 128], q_ref[rows, base + 128:base + 256])
                s = jnp.where(valid, _dot_nt(qm, kband), NEG_INF)
                p, _, inv_l = _softmax_with_sink(s, _sink_column(sink_ref, kv_head))
                o = _dot(p.astype(BF16), vband) * inv_l
                pair0, pair1 = _unstack_heads(o)
                o_ref[rows, base:base + 128] = pair0.astype(BF16)
                o_ref[rows, base + 128:base + 256] = pair1.astype(BF16)

    blk = pl.BlockSpec((nb * QBLOCK, Q_WIDTH), lambda i: (i, 0))
    full = _resident((seq, 128))
    return _pallas(
        body, name="attention_fwd", grid=(seq // (nb * QBLOCK),),
        in_specs=[pl.BlockSpec(memory_space=pltpu.SMEM), blk, full, full, full, full],
        out_specs=[blk], out_shape=[SDS((seq, Q_WIDTH), BF16)],
        operands=(sinks, q, kd0, kd1, vd0, vd1), comm=comm)


HALO = 16


def _conv_parts(gc, xin, gc_halo, xin_halo, conv_w, first):
    tb = gc.shape[0]
    u = gc.astype(F32) * xin.astype(F32)
    u_halo = jnp.where(first, 0.0, gc_halo.astype(F32) * xin_halo.astype(F32))
    ext = jnp.concatenate([u_halo, u], axis=0)
    u1 = pltpu.roll(ext, 1, 0)[HALO:HALO + tb]
    u2 = pltpu.roll(ext, 2, 0)[HALO:HALO + tb]
    y = conv_w[0:1, :] * u2 + conv_w[1:2, :] * u1 + conv_w[2:3, :] * u
    return u, u1, u2, y


def _halo_prev(tb, w):
    return pl.BlockSpec((HALO, w), lambda i: (jnp.maximum(i * (tb // HALO) - 1, 0), 0))


def _residual_mid(x, mix, g_post_mix):
    mix_f = mix.astype(F32)
    return x + mix_f * _rms(mix_f) * g_post_mix


def _mix_out(attn, gb, gc, xin, conv_w, g_attn, g_conv, w_out, comm=None):
    seq = attn.shape[0]
    tb = min(seq, WIDE_TOKEN_TILE)

    def body(a_ref, gb_ref, gc_ref, xin_ref, gch_ref, xinh_ref, cw_ref, ga_ref, gcn_ref, w_ref, mix_ref, mixed_ref):
        first = pl.program_id(0) == 0
        _, _, _, y = _conv_parts(gc_ref[...], xin_ref[...], gch_ref[...], xinh_ref[...], cw_ref[...], first)
        conv = gb_ref[...].astype(F32) * y
        a = a_ref[...].astype(F32)
        mixed_ref[:, 0:Q_WIDTH] = (a * _rms(a) * ga_ref[...]).astype(BF16)
        mixed_ref[:, Q_WIDTH:] = (conv * _rms(conv) * gcn_ref[...]).astype(BF16)
        mix_ref[...] = _dot(mixed_ref[...], w_ref[...].reshape(D_MODEL, D_MODEL)).astype(BF16)

    tile = lambda w: pl.BlockSpec((tb, w), lambda i: (i, 0))
    return _pallas(
        body, name="mix_out", grid=(seq // tb,),
        in_specs=[tile(Q_WIDTH), tile(CONV_WIDTH), tile(CONV_WIDTH), tile(CONV_WIDTH),
                  _halo_prev(tb, CONV_WIDTH), _halo_prev(tb, CONV_WIDTH),
                  _resident((CONV_K, CONV_WIDTH)), _resident((1, Q_WIDTH)), _resident((1, CONV_WIDTH)),
                  _resident(w_out.shape)],
        out_specs=[tile(D_MODEL), tile(D_MODEL)],
        out_shape=[SDS((seq, D_MODEL), BF16), SDS((seq, D_MODEL), BF16)],
        operands=(attn, gb, gc, xin, gc, xin, conv_w, g_attn, g_conv, w_out), comm=comm)


def _mlp_loss(x, mix, target, g_post_mix, g_pre_mlp, g_post_mlp, w_up, w_down):
    seq = x.shape[0]
    tb = TOKEN_TILE

    def body(x_ref, mix_ref, t_ref, gpm_ref, g2_ref, g4_ref, wup_ref, wdown_ref,
             up_ref, hn2_ref, dout_ref, dmlp_ref, loss_ref, dg4_ref, act_ref):
        @pl.when(pl.program_id(0) == 0)
        def _():
            loss_ref[...] = jnp.zeros_like(loss_ref)
            dg4_ref[...] = jnp.zeros_like(dg4_ref)

        halves = [slice(0, tb // 2), slice(tb // 2, tb)]
        hv, hn2 = [], []
        for rows in halves:
            hv.append(_residual_mid(x_ref[rows, :], mix_ref[rows, :], gpm_ref[...]))
            hn2.append((hv[-1] * _rms(hv[-1]) * g2_ref[...]).astype(BF16))
            hn2_ref[rows, :] = hn2[-1]
        for k, rows in enumerate(halves):
            for j in range(N_CHIPS):
                up = _dot(hn2[k], _chip_block(wup_ref, j))
                up = jnp.maximum(up, 0.0)
                up_ref[rows, 1024 * j:1024 * (j + 1)] = up.astype(BF16)
                act_ref[rows, 1024 * j:1024 * (j + 1)] = (up * up).astype(BF16)
        w_down_all = wdown_ref[...].reshape(D_FF, D_MODEL)
        loss = jnp.zeros((1, 1), F32)
        dg4 = jnp.zeros((1, D_MODEL), F32)
        for k, rows in enumerate(halves):
            mlp = _dot(act_ref[rows, :], w_down_all)
            rstd = _rms(mlp)
            zhat = mlp * rstd
            diff = hv[k] + zhat * g4_ref[...] - t_ref[rows, :]
            loss = loss + jnp.sum(jnp.sum(diff * diff, axis=1, keepdims=True), axis=0, keepdims=True)
            dout = diff * (1.0 / D_MODEL)
            dout_ref[rows, :] = dout
            dg4 = dg4 + _colsum(dout * zhat)
            dmlp_ref[rows, :] = _norm_bwd(dout, g4_ref[...], zhat, rstd).astype(BF16)
        loss_ref[...] += loss
        dg4_ref[...] += dg4

    tile = lambda w: pl.BlockSpec((tb, w), lambda i: (i, 0))
    return _pallas(
        body, name="mlp_loss", grid=(seq // tb,),
        in_specs=[tile(D_MODEL), tile(D_MODEL), tile(D_MODEL), _resident((1, D_MODEL)), _resident((1, D_MODEL)),
                  _resident((1, D_MODEL)), _resident(w_up.shape), _resident(w_down.shape)],
        out_specs=[tile(D_FF), tile(D_MODEL), tile(D_MODEL), tile(D_MODEL),
                   pl.BlockSpec((1, 1), lambda i: (0, 0)), pl.BlockSpec((1, D_MODEL), lambda i: (0, 0))],
        out_shape=[SDS((seq, D_FF), BF16), SDS((seq, D_MODEL), BF16), SDS((seq, D_MODEL), F32),
                   SDS((seq, D_MODEL), BF16), SDS((1, 1), F32), SDS((1, D_MODEL), F32)],
        scratch=[pltpu.VMEM((tb, D_FF), BF16)],
        operands=(x, mix, target, g_post_mix, g_pre_mlp, g_post_mlp, w_up, w_down))


def _mlp_bwd(dmlp, up, x, dout, mix, g_pre_mlp, g_post_mix, w_up, w_down):
    seq = x.shape[0]
    tb = MLP_BWD_TOKEN_TILE

    def body(dmlp_ref, up_ref, x_ref, dout_ref, mix_ref, g2_ref, gpm_ref, wup_ref, wdown_ref,
             dup_ref, dh_ref, dmix_ref, dg2_ref, dgpm_ref):
        @pl.when(pl.program_id(0) == 0)
        def _():
            dg2_ref[...] = jnp.zeros_like(dg2_ref)
            dgpm_ref[...] = jnp.zeros_like(dgpm_ref)

        subs = [slice(k * MLP_BWD_SUB_TILE, (k + 1) * MLP_BWD_SUB_TILE) for k in range(tb // MLP_BWD_SUB_TILE)]
        dhn2 = []
        for rows in subs:
            dmlp_v = dmlp_ref[rows, :]
            acc = None
            for j in range(N_CHIPS):
                cols = slice(1024 * j, 1024 * (j + 1))
                dact = _dot_nt(dmlp_v, _chip_block(wdown_ref, j))
                dup = (dact * (2.0 * up_ref[rows, cols].astype(F32))).astype(BF16)
                dup_ref[rows, cols] = dup
                part = _dot_nt(dup, _chip_block(wup_ref, j))
                acc = part if acc is None else acc + part
            dhn2.append(acc)
        dg2 = jnp.zeros((1, D_MODEL), F32)
        dgpm = jnp.zeros((1, D_MODEL), F32)
        for k, rows in enumerate(subs):
            mix_v = mix_ref[rows, :].astype(F32)
            hv = _residual_mid(x_ref[rows, :], mix_ref[rows, :], gpm_ref[...])
            r2 = _rms(hv)
            hhat = hv * r2
            dg2 = dg2 + _colsum(dhn2[k] * hhat)
            dh = dout_ref[rows, :] + _norm_bwd(dhn2[k], g2_ref[...], hhat, r2)
            dh_ref[rows, :] = dh.astype(BF16)
            rz = _rms(mix_v)
            zhat = mix_v * rz
            dgpm = dgpm + _colsum(dh * zhat)
            dmix_ref[rows, :] = _norm_bwd(dh, gpm_ref[...], zhat, rz).astype(BF16)
        dg2_ref[...] += dg2
        dgpm_ref[...] += dgpm

    tile = lambda w: pl.BlockSpec((tb, w), lambda i: (i, 0))
    vec = pl.BlockSpec((1, D_MODEL), lambda i: (0, 0))
    return _pallas(
        body, name="mlp_bwd", grid=(seq // tb,),
        in_specs=[tile(D_MODEL), tile(D_FF), tile(D_MODEL), tile(D_MODEL), tile(D_MODEL),
                  _resident((1, D_MODEL)), _resident((1, D_MODEL)), _resident(w_up.shape), _resident(w_down.shape)],
        out_specs=[tile(D_FF), tile(D_MODEL), tile(D_MODEL), vec, vec],
        out_shape=[SDS((seq, D_FF), BF16), SDS((seq, D_MODEL), BF16), SDS((seq, D_MODEL), BF16),
                   SDS((1, D_MODEL), F32), SDS((1, D_MODEL), F32)],
        operands=(dmlp, up, x, dout, mix, g_pre_mlp, g_post_mix, w_up, w_down))


def _mix_bwd(dmix, attn, gb, gc, xin, conv_w, g_attn, g_conv, w_out, n_k):
    seq = attn.shape[0]
    tb = seq // (N_CHIPS * n_k)

    def body(first, dmix_ref, a_ref, gb_ref, gc_ref, xin_ref, gch_ref, xinh_ref, cw_ref, ga_ref, gcn_ref, w_ref,
             dattn_ref, dgb_ref, dy_ref, dga_ref, dgcn_ref, dcw_ref):
        @pl.when(first)
        def _():
            dga_ref[...] = jnp.zeros_like(dga_ref)
            dgcn_ref[...] = jnp.zeros_like(dgcn_ref)
            dcw_ref[...] = jnp.zeros_like(dcw_ref)

        dmixed = _dot_nt(dmix_ref[...], w_ref[...].reshape(D_MODEL, D_MODEL))
        a = a_ref[...].astype(F32)
        ra = _rms(a)
        ahat = a * ra
        dan = dmixed[:, 0:Q_WIDTH]
        dga_ref[...] += _colsum(dan * ahat)
        dattn_ref[...] = _norm_bwd(dan, ga_ref[...], ahat, ra).astype(BF16)
        gbv = gb_ref[...].astype(F32)
        u, u1, u2, y = _conv_parts(gc_ref[...], xin_ref[...], gch_ref[...], xinh_ref[...], cw_ref[...], first)
        conv = gbv * y
        rc = _rms(conv)
        chat = conv * rc
        dcn = dmixed[:, Q_WIDTH:]
        dgcn_ref[...] += _colsum(dcn * chat)
        dconv = _norm_bwd(dcn, gcn_ref[...], chat, rc)
        dgb_ref[...] = (dconv * y).astype(BF16)
        dy = dconv * gbv
        dy_ref[...] = dy.astype(BF16)
        dcw_ref[0:1, :] += _colsum(dy * u2)
        dcw_ref[1:2, :] += _colsum(dy * u1)
        dcw_ref[2:3, :] += _colsum(dy * u)

    tile = lambda w: pl.BlockSpec((tb, w), lambda j, k: (j * n_k + k, 0))
    halo = lambda w: pl.BlockSpec((HALO, w), lambda j, k: (jnp.maximum((j * n_k + k) * (tb // HALO) - 1, 0), 0))
    whole = lambda shape: pl.BlockSpec(shape, lambda j, k: (0,) * len(shape))
    return _Rider(
        body,
        in_specs=[tile(D_MODEL), tile(Q_WIDTH), tile(CONV_WIDTH), tile(CONV_WIDTH), tile(CONV_WIDTH),
                  halo(CONV_WIDTH), halo(CONV_WIDTH),
                  _resident((CONV_K, CONV_WIDTH)), _resident((1, Q_WIDTH)), _resident((1, CONV_WIDTH)),
                  _resident(w_out.shape)],
        out_specs=[tile(Q_WIDTH), tile(CONV_WIDTH), tile(CONV_WIDTH),
                   whole((1, Q_WIDTH)), whole((1, CONV_WIDTH)), whole((CONV_K, CONV_WIDTH))],
        out_shape=[SDS((seq, Q_WIDTH), BF16), SDS((seq, CONV_WIDTH), BF16), SDS((seq, CONV_WIDTH), BF16),
                   SDS((1, Q_WIDTH), F32), SDS((1, CONV_WIDTH), F32), SDS((CONV_K, CONV_WIDTH), F32)],
        operands=(dmix, attn, gb, gc, xin, gc, xin, conv_w, g_attn, g_conv, w_out))


def _attention_bwd(q, dattn, attn, kd0, kd1, vd0, vd1, sinks, comm=None):
    seq = q.shape[0]
    nb = ATTN_BWD_BLOCKS

    def body(sink_ref, q_ref, do_ref, o_ref, kd0_ref, kd1_ref, vd0_ref, vd1_ref,
             dq_ref, dk0_ref, dk1_ref, dv0_ref, dv1_ref, dsink_ref):
        @pl.when(pl.program_id(0) == 0)
        def _():
            for r in (dk0_ref, dk1_ref, dv0_ref, dv1_ref, dsink_ref):
                r[...] = jnp.zeros_like(r)

        lane = lax.broadcasted_iota(jnp.int32, (1, 128), 1)
        dsink = jnp.zeros((1, 128), F32)
        for b in range(nb):
            i = pl.program_id(0) * nb + b
            rows = slice(QBLOCK * b, QBLOCK * (b + 1))
            valid = _attn_valid(i)
            for kv_head, (k_ref, v_ref, dk_ref, dv_ref) in enumerate(
                    ((kd0_ref, vd0_ref, dk0_ref, dv0_ref), (kd1_ref, vd1_ref, dk1_ref, dv1_ref))):
                kband, prev, own = _band(k_ref, i)
                vband, _, _ = _band(v_ref, i)
                base = 256 * kv_head
                qm = _stack_heads(q_ref[rows, base:base + 128], q_ref[rows, base + 128:base + 256])
                dom = _stack_heads(do_ref[rows, base:base + 128], do_ref[rows, base + 128:base + 256])
                om = _stack_heads(o_ref[rows, base:base + 128], o_ref[rows, base + 128:base + 256])
                s = jnp.where(valid, _dot_nt(qm, kband), NEG_INF)
                p, e_sink, inv_l = _softmax_with_sink(s, _sink_column(sink_ref, kv_head))
                p = p * inv_l
                delta = jnp.sum(dom.astype(F32) * om.astype(F32), axis=-1, keepdims=True)
                ds = (p * (_dot_nt(dom, vband) - delta)).astype(BF16)
                sink_term = -(e_sink * inv_l) * delta
                for j in range(4):
                    part = jnp.sum(sink_term[QBLOCK * j:QBLOCK * (j + 1)], axis=0, keepdims=True)
                    dsink = dsink + jnp.where(lane == 4 * kv_head + j, part, 0.0)
                pair0, pair1 = _unstack_heads(_dot(ds, kband))
                dq_ref[rows, base:base + 128] = pair0.astype(BF16)
                dq_ref[rows, base + 128:base + 256] = pair1.astype(BF16)
                dkd = _dot_tn(ds, qm)
                dkd = dkd + pltpu.roll(dkd, HEAD_DIM, 1)
                dvd = _dot_tn(p.astype(BF16), dom)
                dvd = dvd + pltpu.roll(dvd, HEAD_DIM, 1)
                dk_ref[pl.ds(prev, QBLOCK), :] += dkd[0:QBLOCK]
                dk_ref[pl.ds(own, QBLOCK), :] += dkd[QBLOCK:]
                dv_ref[pl.ds(prev, QBLOCK), :] += dvd[0:QBLOCK]
                dv_ref[pl.ds(own, QBLOCK), :] += dvd[QBLOCK:]
        dsink_ref[...] += dsink

    blk = pl.BlockSpec((nb * QBLOCK, Q_WIDTH), lambda i: (i, 0))
    full = _resident((seq, 128))
    acc = pl.BlockSpec((seq, 128), lambda i: (0, 0))
    return _pallas(
        body, name="attention_bwd", grid=(seq // (nb * QBLOCK),),
        in_specs=[pl.BlockSpec(memory_space=pltpu.SMEM), blk, blk, blk, full, full, full, full],
        out_specs=[blk, acc, acc, acc, acc, pl.BlockSpec((1, 128), lambda i: (0, 0))],
        out_shape=[SDS((seq, Q_WIDTH), BF16)] + [SDS((seq, 128), F32)] * 4 + [SDS((1, 128), F32)],
        operands=(sinks, q, dattn, attn, kd0, kd1, vd0, vd1), comm=comm)


def _in_proj_bwd(dq, dk0, dk1, dv0, dv1, dgb, dy, gc, xin, conv_w, x, dh, g_pre, w_in_t, rope):
    seq = x.shape[0]
    tb = min(seq, WIDE_TOKEN_TILE)
    n_tiles = seq // tb

    def body(dq_ref, dk0_ref, dk1_ref, dv0_ref, dv1_ref, dgb_ref, dy_ref, dyh_ref, gc_ref, xin_ref, cw_ref,
             x_ref, dh_ref, g_ref, w_ref, c_ref, sa_ref, sb_ref,
             dproj_ref, gx_ref, dg_ref):
        i = pl.program_id(0)

        @pl.when(i == 0)
        def _():
            dg_ref[...] = jnp.zeros_like(dg_ref)

        dy = dy_ref[...].astype(F32)
        ext = jnp.concatenate([dy, jnp.where(i == n_tiles - 1, 0.0, dyh_ref[...].astype(F32))], axis=0)
        dy1 = pltpu.roll(ext, tb + HALO - 1, 0)[0:tb]
        dy2 = pltpu.roll(ext, tb + HALO - 2, 0)[0:tb]
        cw = cw_ref[...]
        du = cw[2:3, :] * dy + cw[1:2, :] * dy1 + cw[0:1, :] * dy2
        scale = 1.0 / math.sqrt(HEAD_DIM)
        base = Q_WIDTH + 2 * KV_WIDTH
        halves = [slice(0, tb // 2), slice(tb // 2, tb)]
        low = _lane_lt64((tb // 2, 128))
        for rows in halves:
            c, sa, sb = _rope_tile(c_ref.at[rows, :], sa_ref, sb_ref)
            for p in range(Q_WIDTH // 128):
                dproj_ref[rows, 128 * p:128 * (p + 1)] = _rope_transposed(
                    dq_ref[rows, 128 * p:128 * (p + 1)].astype(F32) * scale, c, sa, sb).astype(BF16)
            dk = jnp.where(low, dk0_ref[rows, :], dk1_ref[rows, :])
            dproj_ref[rows, Q_WIDTH:Q_WIDTH + KV_WIDTH] = _rope_transposed(dk, c, sa, sb).astype(BF16)
            dproj_ref[rows, Q_WIDTH + KV_WIDTH:base] = jnp.where(low, dv0_ref[rows, :], dv1_ref[rows, :]).astype(BF16)
            dproj_ref[rows, base:base + CONV_WIDTH] = dgb_ref[rows, :]
            dproj_ref[rows, base + CONV_WIDTH:base + 2 * CONV_WIDTH] = (du[rows] * xin_ref[rows, :].astype(F32)).astype(BF16)
            dproj_ref[rows, base + 2 * CONV_WIDTH:] = (du[rows] * gc_ref[rows, :].astype(F32)).astype(BF16)
        w_all = w_ref[...].reshape(IN_COLS, D_MODEL)
        dhn = [_dot(dproj_ref[rows, :], w_all) for rows in halves]
        dg = jnp.zeros((1, D_MODEL), F32)
        for k, rows in enumerate(halves):
            xv = x_ref[rows, :]
            r = _rms(xv)
            xhat = xv * r
            dg = dg + _colsum(dhn[k] * xhat)
            gx_ref[rows, :] = dh_ref[rows, :].astype(F32) + _norm_bwd(dhn[k], g_ref[...], xhat, r)
        dg_ref[...] += dg

    tile = lambda w: pl.BlockSpec((tb, w), lambda i: (i, 0))
    halo_next = pl.BlockSpec((HALO, CONV_WIDTH), lambda i: (jnp.minimum((i + 1) * (tb // HALO), seq // HALO - 1), 0))
    return _pallas(
        body, name="in_proj_bwd", grid=(n_tiles,),
        in_specs=[tile(Q_WIDTH), tile(128), tile(128), tile(128), tile(128), tile(CONV_WIDTH), tile(CONV_WIDTH), halo_next,
                  tile(CONV_WIDTH), tile(CONV_WIDTH), _resident((CONV_K, CONV_WIDTH)),
                  tile(D_MODEL), tile(D_MODEL), _resident((1, D_MODEL)), _resident(w_in_t.shape), *_rope_specs(tb)],
        out_specs=[tile(IN_COLS), tile(D_MODEL), pl.BlockSpec((1, D_MODEL), lambda i: (0, 0))],
        out_shape=[SDS((seq, IN_COLS), BF16), SDS((seq, D_MODEL), F32), SDS((1, D_MODEL), F32)],
        operands=(dq, dk0, dk1, dv0, dv1, dgb, dy, dy, gc, xin, conv_w, x, dh, g_pre, w_in_t, *rope))


def _wgrad_grid(seq, per_chip, h_rows, with_rider=False):
    chips_per_step = 1 if per_chip else N_CHIPS
    m = chips_per_step * 2 * h_rows
    bt = min(seq, WGRAD_TOKEN_TILE if per_chip and not with_rider else WGRAD_TOKEN_TILE // 2)
    return chips_per_step, m, bt, seq // bt


def _wgrad(name, a, b, *, per_chip, h_rows, square_a=False, comm=None, rider=None):
    seq = a.shape[0]
    chips_per_step, m, bt, n_k = _wgrad_grid(seq, per_chip, h_rows, rider is not None)
    a_cols = m if per_chip else a.shape[1]
    a_wide = a.shape[1] > a_cols
    b_wide = b.shape[1] > D_MODEL

    def body(a_ref, b_ref, g_ref):
        @pl.when(pl.program_id(1) == 0)
        def _():
            g_ref[...] = jnp.zeros_like(g_ref)

        av = a_ref[...]
        if square_a:
            av = (av.astype(F32) * av.astype(F32)).astype(BF16)
        g_ref[...] += _dot_tn(av, b_ref[...]).reshape(g_ref.shape)

    a_spec = pl.BlockSpec((bt, a_cols), (lambda j, k: (k, j)) if a_wide else (lambda j, k: (k, 0)))
    b_spec = pl.BlockSpec((bt, D_MODEL), (lambda j, k: (k, j)) if b_wide else (lambda j, k: (k, 0)))
    g_spec = pl.BlockSpec((chips_per_step, 2, h_rows, D_MODEL), lambda j, k: (j, 0, 0, 0),
                          pipeline_mode=None if per_chip else pl.Buffered(1))
    return _pallas(
        body, name=name, grid=(N_CHIPS if per_chip else 1, n_k),
        in_specs=[a_spec, b_spec], out_specs=[g_spec], out_shape=[SDS((N_CHIPS, 2, h_rows, D_MODEL), F32)],
        operands=(a, b), comm=comm, rider=rider)


def _adamw_math(w, g, m, v):
    m = ADAM_B1 * m + (1.0 - ADAM_B1) * g
    v = ADAM_B2 * v + (1.0 - ADAM_B2) * (g * g)
    m_hat = m / (1.0 - ADAM_B1 ** ADAM_STEP)
    v_hat = v / (1.0 - ADAM_B2 ** ADAM_STEP)
    delta = -ADAM_LR * (m_hat / (jnp.sqrt(v_hat) + ADAM_EPS) + ADAM_WD * w)
    return delta, m, v


def _adamw_rows(name, reduced, w, m, v, rt):
    per_half = reduced.shape[1] // rt

    def body(r_ref, w_ref, m_ref, v_ref, g_out, d_out, m_out, v_out):
        g = r_ref[0]
        g_out[...] = g
        d_out[...], m_out[...], v_out[...] = _adamw_math(w_ref[...], g, m_ref[...], v_ref[...])

    blk = pl.BlockSpec((rt, D_MODEL), lambda h, r: (h * per_half + r, 0))
    return _pallas(
        body, name=name, grid=(2, per_half),
        in_specs=[pl.BlockSpec((1, rt, D_MODEL), lambda h, r: (h, r, 0)), blk, blk, blk],
        out_specs=[blk, blk, blk, blk], out_shape=[SDS(w.shape, F32)] * 4, operands=(reduced, w, m, v))


SC_TILES = 32
SC_LANES = 16
SC_CHUNK_ROWS = 8


def _adamw_sparsecore(name, g, w, m, v):
    rows = w.shape[0]
    per_tile = rows // SC_TILES

    def body(g_hbm, w_hbm, m_hbm, v_hbm, go_hbm, d_hbm, mo_hbm, vo_hbm, g_buf, w_buf, m_buf, v_buf, d_buf):
        tile = lax.axis_index("subcore") * 2 + lax.axis_index("sparsecore")

        @pl.loop(0, per_tile, step=SC_CHUNK_ROWS)
        def _(r0):
            band = pl.ds(tile * per_tile + r0, SC_CHUNK_ROWS)
            for hbm, buf in ((g_hbm, g_buf), (w_hbm, w_buf), (m_hbm, m_buf), (v_hbm, v_buf)):
                pltpu.sync_copy(hbm.at[band, :], buf)

            @pl.loop(0, SC_CHUNK_ROWS)
            def _(r):
                @pl.loop(0, D_MODEL, step=SC_LANES)
                def _(i):
                    at = (r, pl.ds(i, SC_LANES))
                    d_buf[at], m_buf[at], v_buf[at] = _adamw_math(w_buf[at], g_buf[at], m_buf[at], v_buf[at])

            for buf, hbm in ((g_buf, go_hbm), (d_buf, d_hbm), (m_buf, mo_hbm), (v_buf, vo_hbm)):
                pltpu.sync_copy(buf, hbm.at[band, :])

    return pl.kernel(
        body, name=name, out_type=[SDS(w.shape, F32)] * 4,
        mesh=plsc.VectorSubcoreMesh(core_axis_name="sparsecore", subcore_axis_name="subcore"),
        scratch_types=[pltpu.VMEM((SC_CHUNK_ROWS, D_MODEL), F32)] * 5,
    )(g, w, m, v)


def _adamw_small(packed_grads, w, m, v):
    names = SMALL_NAMES
    n = len(names)
    conv_local = w["conv_w"].shape[-1]

    def body(*refs):
        gp = refs[0]
        w_refs, m_refs, v_refs = refs[1:1 + n], refs[1 + n:1 + 2 * n], refs[1 + 2 * n:1 + 3 * n]
        outs = refs[1 + 3 * n:]
        g_out, d_out, m_out, v_out = outs[0:n], outs[n:2 * n], outs[2 * n:3 * n], outs[3 * n:4 * n]
        chip = 2 * lax.axis_index("x") + lax.axis_index("y")

        def step(k, g, index=None):
            pick = (lambda r: r[...]) if index is None else (lambda r: r[index])
            d, new_m, new_v = _adamw_math(pick(w_refs[k]), g, pick(m_refs[k]), pick(v_refs[k]))
            for ref, val in ((g_out[k], g), (d_out[k], d), (m_out[k], new_m), (v_out[k], new_v)):
                if index is None:
                    ref[...] = val
                else:
                    ref[index] = val

        for k, name in enumerate(names):
            if name in SMALL_VECTORS:
                step(k, gp[SMALL_VECTORS.index(name):SMALL_VECTORS.index(name) + 1, :])
            elif name == "attn_group_norm":
                step(k, gp[4:5, 0:Q_WIDTH])
            elif name == "conv_group_norm":
                step(k, gp[4:5, Q_WIDTH:])
            elif name == "attn_sinks":
                step(k, gp[7:8, 0:8])
            else:
                for t in range(CONV_K):
                    row, base = 5 + t // 2, CONV_WIDTH * (t % 2)
                    g = gp[row:row + 1, base:base + conv_local]
                    for j in range(1, CONV_WIDTH // conv_local):
                        g = jnp.where(chip == j, gp[row:row + 1, base + conv_local * j:base + conv_local * (j + 1)], g)
                    step(k, g, index=(0, slice(t, t + 1), slice(None)))

    shapes = [SDS(w[name].shape, F32) for name in names]
    res = pl.pallas_call(
        body, name="adamw_small", in_specs=[VMEM_WHOLE] * (1 + 3 * n), out_specs=[VMEM_WHOLE] * (4 * n),
        out_shape=shapes * 4,
    )(packed_grads, *[w[k] for k in names], *[m[k] for k in names], *[v[k] for k in names])
    return [dict(zip(names, res[i * n:(i + 1) * n])) for i in range(4)]


SMALL_VECTORS = ("pre_mix_norm", "post_mix_norm", "pre_mlp_norm", "post_mlp_norm")
SMALL_NAMES = SMALL_VECTORS + ("attn_group_norm", "conv_group_norm", "conv_w", "attn_sinks")


def _pack_small(p):
    rows = [p[n].reshape(1, D_MODEL) for n in SMALL_VECTORS]
    rows.append(jnp.concatenate([p["attn_group_norm"].reshape(1, -1), p["conv_group_norm"].reshape(1, -1)], axis=1))
    cw = p["conv_w"].reshape(CONV_K, -1)
    rows.append(jnp.pad(cw, ((0, 1), (0, CONV_WIDTH - cw.shape[1]))).reshape(2, D_MODEL))
    last = jnp.concatenate([p["attn_sinks"].reshape(1, 8), p.get("loss_sum", jnp.zeros((1, 1), F32))], axis=1)
    rows.append(jnp.pad(last, ((0, 0), (0, D_MODEL - 9))))
    return jnp.concatenate(rows, axis=0)


WEIGHT_ORDER = ("pre_mix_norm", "w_in", "conv_w", "attn_sinks", "attn_group_norm", "conv_group_norm", "w_out",
                "post_mix_norm", "pre_mlp_norm", "w_up", "w_down", "post_mlp_norm")


def kernel(x, pre_mix_norm, w_in, conv_w, attn_sinks, attn_group_norm, conv_group_norm, w_out, post_mix_norm, pre_mlp_norm, w_up, w_down, post_mlp_norm, loss_target, m_pre_mix_norm, m_w_in, m_conv_w, m_attn_sinks, m_attn_group_norm, m_conv_group_norm, m_w_out, m_post_mix_norm, m_pre_mlp_norm, m_w_up, m_w_down, m_post_mlp_norm, v_pre_mix_norm, v_w_in, v_conv_w, v_attn_sinks, v_attn_group_norm, v_conv_group_norm, v_w_out, v_post_mix_norm, v_pre_mlp_norm, v_w_up, v_w_down, v_post_mlp_norm):
    w = dict(pre_mix_norm=pre_mix_norm, w_in=w_in, conv_w=conv_w, attn_sinks=attn_sinks, attn_group_norm=attn_group_norm,
             conv_group_norm=conv_group_norm, w_out=w_out, post_mix_norm=post_mix_norm, pre_mlp_norm=pre_mlp_norm,
             w_up=w_up, w_down=w_down, post_mlp_norm=post_mlp_norm)
    m = dict(pre_mix_norm=m_pre_mix_norm, w_in=m_w_in, conv_w=m_conv_w, attn_sinks=m_attn_sinks,
             attn_group_norm=m_attn_group_norm, conv_group_norm=m_conv_group_norm, w_out=m_w_out,
             post_mix_norm=m_post_mix_norm, pre_mlp_norm=m_pre_mlp_norm, w_up=m_w_up, w_down=m_w_down,
             post_mlp_norm=m_post_mlp_norm)
    v = dict(pre_mix_norm=v_pre_mix_norm, w_in=v_w_in, conv_w=v_conv_w, attn_sinks=v_attn_sinks,
             attn_group_norm=v_attn_group_norm, conv_group_norm=v_conv_group_norm, w_out=v_w_out,
             post_mix_norm=v_post_mix_norm, pre_mlp_norm=v_pre_mlp_norm, w_up=v_w_up, w_down=v_w_down,
             post_mlp_norm=v_post_mlp_norm)
    core = lax.axis_index("c").astype(jnp.int32).reshape(1)
    xs, target = x[0], loss_target[0]
    rope = _rope_inputs(xs.shape[0])

    hb_up, hb_down, hb_out, hb_in = _cast_halves(core, w_up[0], w_down[0], w_out[0], w_in[0].T)
    conv_pad = jnp.pad(conv_w[0], ((0, 8 - CONV_K), (0, 0)))
    wf_in, conv_all = _gather_whole(hb_in, conv_pad)
    conv_full = conv_all[:, :CONV_K, :].transpose(1, 0, 2).reshape(CONV_K, CONV_WIDTH)

    whole_up, early, late = (0, H_UP), (0, DOWN_EARLY_ROWS), (DOWN_EARLY_ROWS, H_DOWN - DOWN_EARLY_ROWS)
    *proj, wf_up, wf_out, wf_down = _in_proj(
        xs, pre_mix_norm, wf_in, rope,
        comm=_merge(_relay(hb_up, None, first=whole_up), _gather_first(hb_out), _relay(hb_down, None, first=early)))
    q, kd0, kd1, vd0, vd1, gb, gc, xin, hn = proj
    attn, wf_up, wf_out, wf_down = _attention_fwd(
        q, kd0, kd1, vd0, vd1, attn_sinks,
        comm=_merge(_relay(None, wf_up, second=whole_up), _gather_second(wf_out),
                    _relay(hb_down, wf_down, first=late, second=early)))
    mix, mixed, wf_up, wf_down = _mix_out(
        attn, gb, gc, xin, conv_full, attn_group_norm, conv_group_norm, wf_out,
        comm=_merge(_relay(None, wf_up, third=whole_up), _relay(None, wf_down, second=late, third=early, third_after=late)))
    up, hn2, dout, dmlp, loss_sum, dg_post_mlp = _mlp_loss(xs, mix, target, post_mix_norm, pre_mlp_norm, post_mlp_norm,
                                                           wf_up, wf_down)

    dup, dh, dmix, dg_pre_mlp, dg_post_mix = _mlp_bwd(dmlp, up, xs, dout, mix, pre_mlp_norm, post_mix_norm, wf_up, wf_down)
    n_k = _wgrad_grid(xs.shape[0], True, H_DOWN, with_rider=True)[3]
    g_down, dattn, dgb, dy, dg_attn, dg_conv, dconv_w = _wgrad(
        "wgrad_down", up, dmlp, per_chip=True, h_rows=H_DOWN, square_a=True,
        rider=_mix_bwd(dmix, attn, gb, gc, xin, conv_full, attn_group_norm, conv_group_norm, wf_out, n_k))
    g_up, got_down = _wgrad("wgrad_up", hn2, dup, per_chip=True, h_rows=H_UP, comm=_pair_send(g_down))
    p_down = _pair_sum("pair_sum_down", core, g_down, got_down)
    g_out, got_up = _wgrad("wgrad_out", mixed, dmix, per_chip=False, h_rows=H_OUT, comm=_pair_send(g_up))
    p_up = _pair_sum("pair_sum_up", core, g_up, got_up)
    dq, dk0, dk1, dv0, dv1, dsink, ex_down, ex_up, got_out = _attention_bwd(
        q, dattn, attn, kd0, kd1, vd0, vd1, attn_sinks,
        comm=_merge(_chip_exchange(p_down), _chip_exchange(p_up), _pair_send(g_out)))
    p_out = _pair_sum("pair_sum_out", core, g_out, got_out)
    dproj, grad_x, dg_pre_mix = _in_proj_bwd(dq, dk0, dk1, dv0, dv1, dgb, dy, gc, xin, conv_full, xs, dh, pre_mix_norm,
                                             wf_in, rope)
    g_in, ex_out = _wgrad("wgrad_in", dproj, hn, per_chip=False, h_rows=H_IN, comm=_chip_exchange(p_out))
    small = dict(pre_mix_norm=dg_pre_mix, conv_w=dconv_w, attn_sinks=dsink[:, :8], attn_group_norm=dg_attn,
                 conv_group_norm=dg_conv, post_mix_norm=dg_post_mix, pre_mlp_norm=dg_pre_mlp, post_mlp_norm=dg_post_mlp,
                 loss_sum=loss_sum)
    r_down, r_up, r_out, r_in, small_total = _tail_reduce(g_in, [ex_down, ex_up, ex_out], _pack_small(small))

    out_g, out_d, out_m, out_v = {}, {}, {}, {}
    out_g["w_up"], out_d["w_up"], out_m["w_up"], out_v["w_up"] = _adamw_sparsecore(
        "adamw_up", r_up.reshape(2 * H_UP, D_MODEL), w_up[0], m_w_up[0], v_w_up[0])
    out_g["w_down"], out_d["w_down"], out_m["w_down"], out_v["w_down"] = _adamw_rows(
        "adamw_down", r_down, w_down[0], m_w_down[0], v_w_down[0], 256)
    out_g["w_out"], out_d["w_out"], out_m["w_out"], out_v["w_out"] = _adamw_rows(
        "adamw_out", r_out, w_out[0], m_w_out[0], v_w_out[0], H_OUT)
    in_t = _adamw_rows("adamw_in", r_in, w_in[0].T, m_w_in[0].T, v_w_in[0].T, H_IN)
    out_g["w_in"], out_d["w_in"], out_m["w_in"], out_v["w_in"] = [t.T for t in in_t]

    loss = small_total[7, 8] * (0.5 / D_MODEL)
    for out, part in zip((out_g, out_d, out_m, out_v), _adamw_small(small_total, w, m, v)):
        out.update(part)

    def shaped(d):
        return [d[n].reshape(w[n].shape) for n in WEIGHT_ORDER]

    return (loss, grad_x[None], *shaped(out_g), *shaped(out_d), *shaped(out_m), *shaped(out_v))
```

```python
import math
from typing import Callable, NamedTuple

import jax
import jax.numpy as jnp
import numpy as np
from jax import lax
from jax.experimental import pallas as pl
from jax.experimental.pallas import tpu as pltpu
from jax.experimental.pallas import tpu_sc as plsc

F32 = jnp.float32
BF16 = jnp.bfloat16

D_MODEL = 1024
HEAD_DIM = 64
Q_WIDTH = 512
KV_WIDTH = 128
CONV_WIDTH = 512
CONV_K = 3
D_FF = 4096
IN_COLS = 2304
QBLOCK = 128
ROT_DIM = 16
ROPE_THETA = 500000.0
NORM_EPS = 1e-6
NEG_INF = -1e30
N_CHIPS = 4

ADAM_LR = 0.001
ADAM_B1 = 0.9
ADAM_B2 = 0.999
ADAM_EPS = 1e-08
ADAM_WD = 0.01
ADAM_STEP = 10

H_UP, H_DOWN, H_OUT, H_IN = 512, 512, 128, 288
DOWN_EARLY_ROWS = 224

TOKEN_TILE = 512
WIDE_TOKEN_TILE = 1024
MLP_BWD_TOKEN_TILE = 512
MLP_BWD_SUB_TILE = 256
ATTN_FWD_BLOCKS = 16
ATTN_BWD_BLOCKS = 2
WGRAD_TOKEN_TILE = 4096
VMEM_LIMIT_V7X = 56 * 1024 * 1024

MESH = pl.DeviceIdType.MESH
ANY = pl.BlockSpec(memory_space=pl.ANY)
VMEM_WHOLE = pl.BlockSpec(memory_space=pltpu.VMEM)
SDS = jax.ShapeDtypeStruct


def _resident(shape):
    zeros = (0,) * len(shape)
    return pl.BlockSpec(shape, lambda *_: zeros, pipeline_mode=pl.Buffered(1))


def _rms(v):
    return lax.rsqrt(jnp.mean(v * v, axis=-1, keepdims=True) + NORM_EPS)


def _norm_bwd(dy, gain, vhat, rstd):
    t = dy * gain
    return rstd * (t - vhat * jnp.mean(t * vhat, axis=-1, keepdims=True))


def _colsum(v):
    return jnp.sum(v, axis=0, keepdims=True)


def _dot_nt(a, b):
    return lax.dot_general(a, b, (((1,), (1,)), ((), ())), preferred_element_type=F32)


def _dot_tn(a, b):
    return lax.dot_general(a, b, (((0,), (0,)), ((), ())), preferred_element_type=F32)


def _dot(a, b):
    return jnp.dot(a, b, preferred_element_type=F32)


def _chip_block(w_ref, chip):
    both = w_ref[pl.ds(2 * chip, 2)]
    return both.reshape(2 * both.shape[1], both.shape[2])


def _lane_lt64(shape):
    return lax.broadcasted_iota(jnp.int32, shape, 1) < HEAD_DIM


class _Comm(NamedTuple):
    operands: tuple
    out_shapes: tuple
    aliases: dict
    n_remote: int
    n_local: int
    plan: Callable
    after: Callable = None


def _merge(*comms):
    operands, out_shapes, aliases, parts = [], [], {}, []
    n_remote = n_local = 0
    for cm in comms:
        parts.append((len(operands), len(out_shapes), n_remote, n_local, cm))
        for k, v in cm.aliases.items():
            aliases[len(operands) + k] = len(out_shapes) + v
        operands += cm.operands
        out_shapes += cm.out_shapes
        n_remote += cm.n_remote
        n_local += cm.n_local

    def run(which, ins, outs, send, recv, loc):
        sends, recvs, locs = [], [], []
        for i0, o0, r0, l0, cm in parts:
            stage = getattr(cm, which)
            if stage is not None:
                s, r, l = stage(ins[i0:i0 + len(cm.operands)], outs[o0:o0 + len(cm.out_shapes)],
                                lambda k, r0=r0: send(r0 + k), lambda k, r0=r0: recv(r0 + k), lambda k, l0=l0: loc(l0 + k))
                sends, recvs, locs = sends + s, recvs + r, locs + l
        return sends, recvs, locs

    def plan(*args):
        return run("plan", *args)

    def after(*args):
        return run("after", *args)

    return _Comm(tuple(operands), tuple(out_shapes), aliases, n_remote, n_local, plan,
                 after if any(cm.after is not None for cm in comms) else None)


def _sem_scratch(comm):
    return [pltpu.SemaphoreType.DMA((max(comm.n_remote, 1),)), pltpu.SemaphoreType.DMA((max(comm.n_remote, 1),)),
            pltpu.SemaphoreType.DMA((max(comm.n_local, 1),))]


class _Rider(NamedTuple):
    body: Callable
    in_specs: list
    out_specs: list
    out_shape: list
    operands: tuple


def _pallas(body, *, name, grid, in_specs, out_specs, out_shape, operands, scratch=(), comm=None, rider=None):
    params = pltpu.CompilerParams(dimension_semantics=("arbitrary",) * len(grid), vmem_limit_bytes=VMEM_LIMIT_V7X)
    if rider is not None:
        own_in, own_out, ride_in, ride_out = len(in_specs), len(out_specs), len(rider.in_specs), len(rider.out_specs)
        own_body = body

        def body(*refs):
            o0 = own_in + ride_in
            s0 = o0 + own_out + ride_out
            own_body(*refs[:own_in], *refs[o0:o0 + own_out], *refs[s0:])
            first = None
            for axis in range(len(grid)):
                at_start = pl.program_id(axis) == 0
                first = at_start if first is None else jnp.logical_and(first, at_start)
            rider.body(first, *refs[own_in:o0], *refs[o0 + own_out:s0])

        in_specs, out_specs = list(in_specs) + rider.in_specs, list(out_specs) + rider.out_specs
        out_shape, operands = list(out_shape) + rider.out_shape, tuple(operands) + tuple(rider.operands)
    if comm is None:
        return pl.pallas_call(body, name=name, grid=grid, in_specs=in_specs, out_specs=out_specs, out_shape=out_shape,
                              scratch_shapes=list(scratch), compiler_params=params)(*operands)
    n_in, n_out, n_scr = len(in_specs), len(out_specs), len(scratch)
    c_in, c_out = len(comm.operands), len(comm.out_shapes)

    def with_comm(*refs):
        ins, c_ins = refs[:n_in], refs[n_in:n_in + c_in]
        o0 = n_in + c_in
        outs, c_outs = refs[o0:o0 + n_out], refs[o0 + n_out:o0 + n_out + c_out]
        s0 = o0 + n_out + c_out
        scr = refs[s0:s0 + n_scr]
        send_sems, recv_sems, local_sems = refs[s0 + n_scr:]
        first = last = None
        for axis, size in enumerate(grid):
            at_start, at_end = pl.program_id(axis) == 0, pl.program_id(axis) == size - 1
            first = at_start if first is None else jnp.logical_and(first, at_start)
            last = at_end if last is None else jnp.logical_and(last, at_end)

        def copies():
            return comm.plan(c_ins, c_outs, lambda k: send_sems.at[k], lambda k: recv_sems.at[k],
                             lambda k: local_sems.at[k])

        @pl.when(first)
        def _():
            sends, _, locs = copies()
            for cp in sends + locs:
                cp.start()

        body(*ins, *outs, *scr)

        @pl.when(last)
        def _():
            sends, recvs, locs = copies()
            for cp in recvs:
                cp.wait_recv()
            for cp in sends:
                cp.wait_send()
            for cp in locs:
                cp.wait()
            if comm.after is not None:
                sends, recvs, _ = comm.after(c_ins, c_outs, lambda k: send_sems.at[k], lambda k: recv_sems.at[k],
                                             lambda k: local_sems.at[k])
                for cp in sends:
                    cp.start()
                for cp in recvs:
                    cp.wait_recv()
                for cp in sends:
                    cp.wait_send()

    return pl.pallas_call(
        with_comm, name=name, grid=grid,
        in_specs=list(in_specs) + [ANY] * c_in, out_specs=list(out_specs) + [ANY] * c_out,
        out_shape=list(out_shape) + list(comm.out_shapes),
        scratch_shapes=list(scratch) + _sem_scratch(comm),
        input_output_aliases={n_in + k: n_out + v for k, v in comm.aliases.items()},
        compiler_params=params)(*operands, *comm.operands)


def _place():
    return lax.axis_index("x"), lax.axis_index("y"), lax.axis_index("c")


def _other_chips(x, y):
    return [(1 - x, y), (x, 1 - y), (1 - x, 1 - y)]


def _slot(px, py, pc):
    return 4 * px + 2 * py + pc


def _remote(src, dst, send_sem, recv_sem, to):
    return pltpu.make_async_remote_copy(src_ref=src, dst_ref=dst, send_sem=send_sem, recv_sem=recv_sem,
                                        device_id=to, device_id_type=MESH)


def _gather_first(half_block):
    def plan(ins, outs, send, recv, loc):
        (blk,), (full,) = ins, outs
        x, y, c = _place()
        chips = _other_chips(x, y)
        mine = full.at[_slot(x, y, c)]
        sends = [_remote(blk, mine, send(0), recv(0), (x, y, 1 - c))]
        sends += [_remote(blk, mine, send(1 + j), recv(1 + j), (*chip, c)) for j, chip in enumerate(chips)]
        recvs = [_remote(blk, full.at[_slot(x, y, 1 - c)], send(0), recv(0), (x, y, 1 - c))]
        recvs += [_remote(blk, full.at[_slot(*chip, c)], send(1 + j), recv(1 + j), (*chip, c))
                  for j, chip in enumerate(chips)]
        return sends, recvs, [pltpu.make_async_copy(blk, mine, loc(0))]

    return _Comm((half_block,), (SDS((2 * N_CHIPS,) + half_block.shape, half_block.dtype),), {}, 4, 1, plan)


def _gather_second(partly_gathered):
    def plan(ins, outs, send, recv, loc):
        (src,), (full,) = ins, outs
        x, y, c = _place()
        chips = _other_chips(x, y)
        sends = [_remote(src.at[_slot(*chip, c)], full.at[_slot(*chip, c)], send(j), recv(j), (x, y, 1 - c))
                 for j, chip in enumerate(chips)]
        recvs = [_remote(src.at[_slot(*chip, 1 - c)], full.at[_slot(*chip, 1 - c)], send(j), recv(j), (x, y, 1 - c))
                 for j, chip in enumerate(chips)]
        return sends, recvs, []

    return _Comm((partly_gathered,), (SDS(partly_gathered.shape, partly_gathered.dtype),), {0: 0}, 3, 0, plan)


def _relay_pieces(full, rows, x, y, c):
    start, half = rows[0], rows[1] // 2
    upper, lower = pl.ds(start, half), pl.ds(start + half, half)
    diagonal = full.at[_slot(1 - x, 1 - y, c)]
    return [(full.at[_slot(1 - x, y, c), upper], diagonal.at[upper], (x, 1 - y, c)),
            (full.at[_slot(x, 1 - y, c), lower], diagonal.at[lower], (1 - x, y, c))]


def _relay(half_block, so_far, first=None, second=None, third=None, third_after=None):
    has_block, has_buffer = half_block is not None, so_far is not None
    shape = so_far.shape if has_buffer else (2 * N_CHIPS,) + half_block.shape
    dtype = so_far.dtype if has_buffer else half_block.dtype

    def third_leg(rows, k, ins, outs, send, recv):
        src, full = (ins[-1] if has_buffer else outs[0]), outs[0]
        x, y, c = _place()
        span, sibling = pl.ds(*rows), (x, y, 1 - c)
        here, there = _slot(1 - x, 1 - y, c), _slot(1 - x, 1 - y, 1 - c)
        return ([_remote(src.at[here, span], full.at[here, span], send(k), recv(k), sibling)],
                [_remote(src.at[there, span], full.at[there, span], send(k), recv(k), sibling)])

    def plan(ins, outs, send, recv, loc):
        src, full = (ins[-1] if has_buffer else outs[0]), outs[0]
        x, y, c = _place()
        sibling = (x, y, 1 - c)
        sends, recvs, locs = [], [], []
        if first is not None:
            span = pl.ds(*first)
            blk, mine = ins[0].at[span], full.at[_slot(x, y, c), span]
            for k, peer in enumerate([sibling, (1 - x, y, c), (x, 1 - y, c)]):
                sends.append(_remote(blk, mine, send(k), recv(k), peer))
                recvs.append(_remote(blk, full.at[_slot(*peer), span], send(k), recv(k), peer))
            locs.append(pltpu.make_async_copy(blk, mine, loc(0)))
        if second is not None:
            span = pl.ds(*second)
            for k, chip in enumerate([(1 - x, y), (x, 1 - y)]):
                sends.append(_remote(src.at[_slot(*chip, c), span], full.at[_slot(*chip, c), span], send(3 + k), recv(3 + k),
                                     sibling))
                recvs.append(_remote(src.at[_slot(*chip, 1 - c), span], full.at[_slot(*chip, 1 - c), span], send(3 + k),
                                     recv(3 + k), sibling))
            for k, (piece, lands, peer) in enumerate(_relay_pieces(full, second, x, y, c)):
                sends.append(_remote(piece, piece, send(5 + k), recv(5 + k), peer))
                recvs.append(_remote(lands, lands, send(5 + k), recv(5 + k), peer))
        if third is not None:
            s, r = third_leg(third, 7, ins, outs, send, recv)
            sends, recvs = sends + s, recvs + r
        return sends, recvs, locs

    def after(ins, outs, send, recv, loc):
        s, r = third_leg(third_after, 8, ins, outs, send, recv)
        return s, r, []

    operands = ((half_block,) if has_block else ()) + ((so_far,) if has_buffer else ())
    return _Comm(operands, (SDS(shape, dtype),), {len(operands) - 1: 0} if has_buffer else {}, 9, 1, plan,
                 after if third_after is not None else None)


def _gather_whole(half_block, small_block):
    rows = half_block.shape[0]

    def body(blk_ref, small_ref, out_ref, small_out_ref, send_sems, recv_sems, local_sems):
        x, y, c = _place()
        me, sibling = (x, y, c), (x, y, 1 - c)
        neighbours, diagonal = [(1 - x, y), (x, 1 - y)], (1 - x, 1 - y)

        def copy(k, block, to, src=None):
            return _remote(out_ref.at[_slot(*block)] if src is None else src, out_ref.at[_slot(*block)],
                           send_sems.at[k], recv_sems.at[k], to)

        def small_copy(k, chip, to):
            return _remote(small_ref, small_out_ref.at[2 * chip[0] + chip[1]], send_sems.at[8 + k], recv_sems.at[8 + k], to)

        mine = pltpu.make_async_copy(blk_ref, out_ref.at[_slot(*me)], local_sems.at[0])
        mine_small = pltpu.make_async_copy(small_ref, small_out_ref.at[2 * x + y], local_sems.at[1])
        mine.start()
        mine_small.start()
        started = [copy(0, me, sibling, src=blk_ref)]
        started += [copy(1 + k, me, (*chip, c), src=blk_ref) for k, chip in enumerate(neighbours)]
        started += [small_copy(k, (x, y), (*chip, c)) for k, chip in enumerate(neighbours + [diagonal])]
        for cp in started:
            cp.start()
        pieces = _relay_pieces(out_ref, (0, rows), x, y, c)
        for k, chip in enumerate(neighbours):
            copy(1 + k, (*chip, c), me).wait_recv()
            piece, _, peer = pieces[k]
            started += [copy(3 + k, (*chip, c), sibling), _remote(piece, piece, send_sems.at[5 + k], recv_sems.at[5 + k], peer)]
            started[-2].start()
            started[-1].start()
        for k, (_, lands, peer) in enumerate(pieces):
            _remote(lands, lands, send_sems.at[5 + k], recv_sems.at[5 + k], peer).wait_recv()
        started.append(copy(7, (*diagonal, c), sibling))
        started[-1].start()
        copy(0, sibling, me).wait_recv()
        for k, chip in enumerate(neighbours):
            copy(3 + k, (*chip, 1 - c), me).wait_recv()
        copy(7, (*diagonal, 1 - c), me).wait_recv()
        for k, chip in enumerate(neighbours + [diagonal]):
            small_copy(k, chip, me).wait_recv()
        for cp in started:
            cp.wait_send()
        mine.wait()
        mine_small.wait()

    return pl.pallas_call(
        body, name="gather_whole", in_specs=[ANY, ANY], out_specs=[ANY, ANY],
        out_shape=[SDS((2 * N_CHIPS,) + half_block.shape, half_block.dtype),
                   SDS((N_CHIPS,) + small_block.shape, small_block.dtype)],
        scratch_shapes=[pltpu.SemaphoreType.DMA((11,)), pltpu.SemaphoreType.DMA((11,)), pltpu.SemaphoreType.DMA((2,))],
    )(half_block, small_block)


def _pair_send(grads):
    def plan(ins, outs, send, recv, loc):
        (g,), (got,) = ins, outs
        x, y, c = _place()
        copies = [_remote(g.at[j, 1 - c], got.at[j], send(j), recv(j), (x, y, 1 - c)) for j in range(N_CHIPS)]
        return copies, copies, []

    shape = (grads.shape[0],) + grads.shape[2:]
    return _Comm((grads,), (SDS(shape, grads.dtype),), {}, N_CHIPS, 0, plan)


def _chip_exchange(partial):
    def plan(ins, outs, send, recv, loc):
        (p,), (got,) = ins, outs
        x, y, c = _place()
        my_chip = 2 * x + y
        chips = _other_chips(x, y)
        sends = [_remote(p.at[2 * chip[0] + chip[1]], got.at[my_chip], send(j), recv(j), (*chip, c))
                 for j, chip in enumerate(chips)]
        recvs = [_remote(p.at[my_chip], got.at[2 * chip[0] + chip[1]], send(j), recv(j), (*chip, c))
                 for j, chip in enumerate(chips)]
        return sends, recvs, [pltpu.make_async_copy(p.at[my_chip], got.at[my_chip], loc(0))]

    return _Comm((partial,), (SDS(partial.shape, partial.dtype),), {}, 3, 1, plan)


def _pair_sum(name, core, grads, received):
    h = grads.shape[2]

    def body(core_ref, g_ref, r_ref, o_ref):
        o_ref[...] = (g_ref[0] + r_ref[...]).astype(BF16)

    return pl.pallas_call(
        body, name=name,
        grid_spec=pltpu.PrefetchScalarGridSpec(
            num_scalar_prefetch=1, grid=(N_CHIPS,),
            in_specs=[pl.BlockSpec((1, 1, h, D_MODEL), lambda j, core_ref: (j, core_ref[0], 0, 0)),
                      pl.BlockSpec((1, h, D_MODEL), lambda j, core_ref: (j, 0, 0))],
            out_specs=pl.BlockSpec((1, h, D_MODEL), lambda j, core_ref: (j, 0, 0))),
        out_shape=SDS((N_CHIPS, h, D_MODEL), BF16),
        compiler_params=pltpu.CompilerParams(dimension_semantics=("arbitrary",), vmem_limit_bytes=VMEM_LIMIT_V7X),
    )(core, grads, received)


SMALL_ROWS = 8


def _sum_blocks(ref):
    return (ref[0].astype(F32) + ref[1].astype(F32)) + (ref[2].astype(F32) + ref[3].astype(F32))


def _finish_reduce(exchanged):
    n = len(exchanged)

    def body(*refs):
        got, out, halves = refs[:n], refs[n:2 * n], refs[2 * n:3 * n]
        send_sems, recv_sems, local_sems = refs[3 * n:]
        x, y, c = _place()
        sibling = (x, y, 1 - c)
        started = []
        for k in range(n):
            halves[k][...] = _sum_blocks(got[k])
            started.append((pltpu.make_async_copy(halves[k], out[k].at[c], local_sems.at[k]),
                            _remote(halves[k], out[k].at[c], send_sems.at[k], recv_sems.at[k], sibling),
                            _remote(halves[k], out[k].at[1 - c], send_sems.at[k], recv_sems.at[k], sibling)))
            started[-1][0].start()
            started[-1][1].start()
        for keep, give, take in started:
            take.wait_recv()
            give.wait_send()
            keep.wait()

    return pl.pallas_call(
        body, name="finish_reduce", in_specs=[VMEM_WHOLE] * n, out_specs=[ANY] * n,
        out_shape=[SDS((2,) + e.shape[1:], F32) for e in exchanged],
        scratch_shapes=[pltpu.VMEM(e.shape[1:], F32) for e in exchanged]
                       + [pltpu.SemaphoreType.DMA((n,)), pltpu.SemaphoreType.DMA((n,)), pltpu.SemaphoreType.DMA((n,))],
        compiler_params=pltpu.CompilerParams(vmem_limit_bytes=VMEM_LIMIT_V7X),
    )(*exchanged)


def _tail_reduce(last_grads, exchanged, small):
    n = len(exchanged)
    h = last_grads.shape[2]

    def body(*refs):
        g_ref, ex, small_ref = refs[0], refs[1:1 + n], refs[1 + n]
        o0 = 2 + n
        out, out_last, small_out = refs[o0:o0 + n], refs[o0 + n], refs[o0 + n + 1]
        s0 = o0 + n + 2
        halves, half_last = refs[s0:s0 + n], refs[s0 + n]
        own, got, part, exch, small_buf = refs[s0 + n + 1:s0 + n + 6]
        pair_send, pair_recv, chip_send, chip_recv, share_send, share_recv, small_send, small_recv, local_sems = refs[s0 + n + 6:]
        x, y, c = _place()
        sibling = (x, y, 1 - c)
        my_chip, me = 2 * x + y, _slot(x, y, c)
        chips = _other_chips(x, y)

        to_sibling = [_remote(g_ref.at[j, 1 - c], got.at[j], pair_send.at[j], pair_recv.at[j], sibling)
                      for j in range(N_CHIPS)]
        load_own = [pltpu.make_async_copy(g_ref.at[j, c], own.at[j], local_sems.at[j]) for j in range(N_CHIPS)]
        for cp in to_sibling + load_own:
            cp.start()

        small_buf[me] = small_ref[...]
        small_copies = []
        for mask in range(1, 8):
            peer = (x ^ (mask >> 2), y ^ ((mask >> 1) & 1), c ^ (mask & 1))
            small_copies.append(_remote(small_ref, small_buf.at[me], small_send.at[mask - 1], small_recv.at[mask - 1], peer))
        for cp in small_copies:
            cp.start()

        def share(k, half_ref, out_ref):
            keep = pltpu.make_async_copy(half_ref, out_ref.at[c], local_sems.at[N_CHIPS + k])
            give = _remote(half_ref, out_ref.at[c], share_send.at[k], share_recv.at[k], sibling)
            take = _remote(half_ref, out_ref.at[1 - c], share_send.at[k], share_recv.at[k], sibling)
            keep.start()
            give.start()
            return keep, give, take

        shares = []
        for k in range(n):
            halves[k][...] = _sum_blocks(ex[k])
            shares.append(share(k, halves[k], out[k]))

        def pair_sum(block):
            _remote(g_ref.at[block, 1 - c], got.at[block], pair_send.at[block], pair_recv.at[block], sibling).wait_recv()
            pltpu.make_async_copy(g_ref.at[block, c], own.at[block], local_sems.at[block]).wait()
            part[block] = (own[block] + got[block]).astype(BF16)

        to_chips = []
        for j, chip in enumerate(chips):
            block = 2 * chip[0] + chip[1]
            pair_sum(block)
            to_chips.append(_remote(part.at[block], exch.at[my_chip], chip_send.at[j], chip_recv.at[j], (*chip, c)))
            to_chips[-1].start()
        pair_sum(my_chip)
        exch[my_chip] = part[my_chip]
        from_chips = [_remote(part.at[my_chip], exch.at[2 * chip[0] + chip[1]], chip_send.at[j], chip_recv.at[j], (*chip, c))
                      for j, chip in enumerate(chips)]

        for cp in small_copies:
            cp.wait_recv()
        total = small_buf[0]
        for d in range(1, 8):
            total = total + small_buf[d]
        small_out[...] = total

        for cp in from_chips:
            cp.wait_recv()
        half_last[...] = _sum_blocks(exch)
        shares.append(share(n, half_last, out_last))

        for keep, give, take in shares:
            take.wait_recv()
            give.wait_send()
            keep.wait()
        for cp in to_sibling + to_chips + small_copies:
            cp.wait_send()

    blocks = (N_CHIPS, h, D_MODEL)
    return pl.pallas_call(
        body, name="tail_reduce",
        in_specs=[ANY] + [VMEM_WHOLE] * (n + 1), out_specs=[ANY] * (n + 1) + [VMEM_WHOLE],
        out_shape=[SDS((2,) + e.shape[1:], F32) for e in exchanged] + [SDS((2, h, D_MODEL), F32), SDS(small.shape, F32)],
        scratch_shapes=[pltpu.VMEM(e.shape[1:], F32) for e in exchanged] + [pltpu.VMEM((h, D_MODEL), F32)]
                       + [pltpu.VMEM(blocks, F32), pltpu.VMEM(blocks, F32), pltpu.VMEM(blocks, BF16), pltpu.VMEM(blocks, BF16),
                          pltpu.VMEM((8,) + small.shape, F32)]
                       + [pltpu.SemaphoreType.DMA((N_CHIPS,)), pltpu.SemaphoreType.DMA((N_CHIPS,)),
                          pltpu.SemaphoreType.DMA((3,)), pltpu.SemaphoreType.DMA((3,)),
                          pltpu.SemaphoreType.DMA((n + 1,)), pltpu.SemaphoreType.DMA((n + 1,)),
                          pltpu.SemaphoreType.DMA((7,)), pltpu.SemaphoreType.DMA((7,)),
                          pltpu.SemaphoreType.DMA((N_CHIPS + n + 1,))],
        compiler_params=pltpu.CompilerParams(vmem_limit_bytes=VMEM_LIMIT_V7X),
    )(last_grads, *exchanged, small)


def _rope_expansion():
    half = ROT_DIM // 2
    expand = np.zeros((2 * half, 3 * 128), np.float32)
    const = np.zeros((1, 3 * 128), np.float32)
    for lane in range(128):
        d = lane % HEAD_DIM
        if d < ROT_DIM:
            expand[d % half, lane] = 1.0
        else:
            const[0, lane] = 1.0
        if d < half:
            expand[half + d, 128 + lane] = -1.0
        elif d < ROT_DIM:
            expand[half + d - half, 256 + lane] = 1.0
    return expand, const


ROPE_PIECES = 3 * ROT_DIM


def _rope_inputs(seq):
    pos = jnp.arange(seq, dtype=F32)
    inv_freq = ROPE_THETA ** (-jnp.arange(0, ROT_DIM, 2, dtype=F32) / ROT_DIM)
    ang = pos[:, None] * inv_freq[None, :]
    cs = jnp.concatenate([jnp.cos(ang), jnp.sin(ang)], axis=1)
    hi = lax.reduce_precision(cs, 8, 7)
    mid = lax.reduce_precision(cs - hi, 8, 7)
    low = cs - hi - mid
    expand, const = _rope_expansion()
    pieces = jnp.concatenate([hi, mid, low], axis=1).astype(BF16)
    return pieces, jnp.asarray(np.concatenate([expand] * 3, axis=0), BF16), jnp.asarray(const)


def _rope_specs(tb):
    return [pl.BlockSpec((tb, ROPE_PIECES), lambda i: (i, 0)), _resident((ROPE_PIECES, 3 * 128)), _resident((1, 3 * 128))]


def _rope_tile(pieces_ref, expand_ref, const_ref):
    tables = _dot(pieces_ref[...], expand_ref[...]) + const_ref[...]
    return tables[:, 0:128], tables[:, 128:256], tables[:, 256:384]


def _rope(t, c, sa, sb):
    half = ROT_DIM // 2
    return t * c + pltpu.roll(t, 128 - half, 1) * sa + pltpu.roll(t, half, 1) * sb


def _rope_transposed(dt, c, sa, sb):
    half = ROT_DIM // 2
    return dt * c + pltpu.roll(dt * sa, half, 1) + pltpu.roll(dt * sb, 128 - half, 1)


def _cast_halves(core, w_up, w_down, w_out, w_in_t):
    def body(core_ref, up_ref, down_ref, out_ref, in_ref, up_o, down_o, out_o, in_o):
        up_o[...] = up_ref[...].astype(BF16)
        down_o[...] = down_ref[...].astype(BF16)
        out_o[...] = out_ref[...].astype(BF16)
        in_o[...] = in_ref[...].astype(BF16)

    half = lambda rows: pl.BlockSpec((rows, D_MODEL), lambda i, core_ref: (core_ref[0], 0))
    whole = lambda rows: pl.BlockSpec((rows, D_MODEL), lambda i, core_ref: (0, 0))
    rows = (H_UP, H_DOWN, H_OUT, H_IN)
    return pl.pallas_call(
        body, name="cast_halves",
        grid_spec=pltpu.PrefetchScalarGridSpec(
            num_scalar_prefetch=1, grid=(1,), in_specs=[half(r) for r in rows], out_specs=[whole(r) for r in rows]),
        out_shape=[SDS((r, D_MODEL), BF16) for r in rows],
        compiler_params=pltpu.CompilerParams(dimension_semantics=("arbitrary",), vmem_limit_bytes=VMEM_LIMIT_V7X),
    )(core, w_up, w_down, w_out, w_in_t)


def _in_proj(x, g_pre, w_in_t, rope, comm=None):
    seq = x.shape[0]
    tb = min(seq, WIDE_TOKEN_TILE)

    def body(x_ref, g_ref, w_ref, c_ref, sa_ref, sb_ref,
             q_ref, kd0_ref, kd1_ref, vd0_ref, vd1_ref, gb_ref, gc_ref, xin_ref, hn_ref):
        xv = x_ref[...]
        hn = (xv * _rms(xv) * g_ref[...]).astype(BF16)
        hn_ref[...] = hn
        proj = _dot_nt(hn, w_ref[...].reshape(IN_COLS, D_MODEL))
        c, sa, sb = _rope_tile(c_ref, sa_ref, sb_ref)
        scale = 1.0 / math.sqrt(HEAD_DIM)
        for p in range(Q_WIDTH // 128):
            q_ref[:, 128 * p:128 * (p + 1)] = (_rope(proj[:, 128 * p:128 * (p + 1)], c, sa, sb) * scale).astype(BF16)
        k = _rope(proj[:, Q_WIDTH:Q_WIDTH + KV_WIDTH], c, sa, sb)
        v = proj[:, Q_WIDTH + KV_WIDTH:Q_WIDTH + 2 * KV_WIDTH]
        low = _lane_lt64(k.shape)
        k_sw, v_sw = pltpu.roll(k, HEAD_DIM, 1), pltpu.roll(v, HEAD_DIM, 1)
        kd0_ref[...] = jnp.where(low, k, k_sw).astype(BF16)
        kd1_ref[...] = jnp.where(low, k_sw, k).astype(BF16)
        vd0_ref[...] = jnp.where(low, v, v_sw).astype(BF16)
        vd1_ref[...] = jnp.where(low, v_sw, v).astype(BF16)
        base = Q_WIDTH + 2 * KV_WIDTH
        gb_ref[...] = proj[:, base:base + CONV_WIDTH].astype(BF16)
        gc_ref[...] = proj[:, base + CONV_WIDTH:base + 2 * CONV_WIDTH].astype(BF16)
        xin_ref[...] = proj[:, base + 2 * CONV_WIDTH:base + 3 * CONV_WIDTH].astype(BF16)

    tile = lambda w: pl.BlockSpec((tb, w), lambda i: (i, 0))
    return _pallas(
        body, name="in_proj", grid=(seq // tb,),
        in_specs=[tile(D_MODEL), _resident((1, D_MODEL)), _resident(w_in_t.shape), *_rope_specs(tb)],
        out_specs=[tile(Q_WIDTH), tile(128), tile(128), tile(128), tile(128),
                   tile(CONV_WIDTH), tile(CONV_WIDTH), tile(CONV_WIDTH), tile(D_MODEL)],
        out_shape=[SDS((seq, Q_WIDTH), BF16)] + [SDS((seq, 128), BF16)] * 4
                  + [SDS((seq, CONV_WIDTH), BF16)] * 3 + [SDS((seq, D_MODEL), BF16)],
        operands=(x, g_pre, w_in_t, *rope), comm=comm)


def _attn_valid(i):
    shape = (4 * QBLOCK, 2 * QBLOCK)
    row = lax.broadcasted_iota(jnp.int32, shape, 0)
    col = lax.broadcasted_iota(jnp.int32, shape, 1)
    qi = row & (QBLOCK - 1)
    return (col > qi) & (col <= qi + QBLOCK) & ((col >= QBLOCK) | (i > 0))


def _stack_heads(pair0, pair1):
    low = _lane_lt64(pair0.shape)
    zero = jnp.zeros_like(pair0)
    return jnp.concatenate([jnp.where(low, pair0, zero), jnp.where(low, zero, pair0),
                            jnp.where(low, pair1, zero), jnp.where(low, zero, pair1)], axis=0)


def _unstack_heads(stacked):
    low = _lane_lt64((QBLOCK, 128))
    pair0 = jnp.where(low, stacked[0:QBLOCK], stacked[QBLOCK:2 * QBLOCK])
    pair1 = jnp.where(low, stacked[2 * QBLOCK:3 * QBLOCK], stacked[3 * QBLOCK:4 * QBLOCK])
    return pair0, pair1


def _sink_column(sink_ref, kv_head):
    row = lax.broadcasted_iota(jnp.int32, (4 * QBLOCK, 1), 0)
    s = [sink_ref[0, 4 * kv_head + j] for j in range(4)]
    return jnp.where(row < QBLOCK, s[0], jnp.where(row < 2 * QBLOCK, s[1], jnp.where(row < 3 * QBLOCK, s[2], s[3])))


def _band(ref, i):
    prev = pl.multiple_of(jnp.maximum(i - 1, 0) * QBLOCK, QBLOCK)
    own = pl.multiple_of(i * QBLOCK, QBLOCK)
    return jnp.concatenate([ref[pl.ds(prev, QBLOCK), :], ref[pl.ds(own, QBLOCK), :]], axis=0), prev, own


def _softmax_with_sink(s, sink_col):
    m = jnp.maximum(jnp.max(s, axis=-1, keepdims=True), sink_col)
    p = jnp.exp(s - m)
    e_sink = jnp.exp(sink_col - m)
    inv_l = 1.0 / (jnp.sum(p, axis=-1, keepdims=True) + e_sink)
    return p, e_sink, inv_l


def _attention_fwd(q, kd0, kd1, vd0, vd1, sinks, comm=None):
    seq = q.shape[0]

    nb = ATTN_FWD_BLOCKS

    def body(sink_ref, q_ref, kd0_ref, kd1_ref, vd0_ref, vd1_ref, o_ref):
        for b in range(nb):
            i = pl.program_id(0) * nb + b
            rows = slice(QBLOCK * b, QBLOCK * (b + 1))
            valid = _attn_valid(i)
            for kv_head, (k_ref, v_ref) in enumerate(((kd0_ref, vd0_ref), (kd1_ref, vd1_ref))):
                kband, _, _ = _band(k_ref, i)
                vband, _, _ = _band(v_ref, i)
                base = 256 * kv_head
                qm = _stack_heads(q_ref[rows, base:base + 128], q_ref[rows, base + 128:base + 256])
                s = jnp.where(valid, _dot_nt(qm, kband), NEG_INF)
                p, _, inv_l = _softmax_with_sink(s, _sink_column(sink_ref, kv_head))
                o = _dot(p.astype(BF16), vband) * inv_l
                pair0, pair1 = _unstack_heads(o)
                o_ref[rows, base:base + 128] = pair0.astype(BF16)
                o_ref[rows, base + 128:base + 256] = pair1.astype(BF16)

    blk = pl.BlockSpec((nb * QBLOCK, Q_WIDTH), lambda i: (i, 0))
    full = _resident((seq, 128))
    return _pallas(
        body, name="attention_fwd", grid=(seq // (nb * QBLOCK),),
        in_specs=[pl.BlockSpec(memory_space=pltpu.SMEM), blk, full, full, full, full],
        out_specs=[blk], out_shape=[SDS((seq, Q_WIDTH), BF16)],
        operands=(sinks, q, kd0, kd1, vd0, vd1), comm=comm)


HALO = 16


def _conv_parts(gc, xin, gc_halo, xin_halo, conv_w, first):
    tb = gc.shape[0]
    u = gc.astype(F32) * xin.astype(F32)
    u_halo = jnp.where(first, 0.0, gc_halo.astype(F32) * xin_halo.astype(F32))
    ext = jnp.concatenate([u_halo, u], axis=0)
    u1 = pltpu.roll(ext, 1, 0)[HALO:HALO + tb]
    u2 = pltpu.roll(ext, 2, 0)[HALO:HALO + tb]
    y = conv_w[0:1, :] * u2 + conv_w[1:2, :] * u1 + conv_w[2:3, :] * u
    return u, u1, u2, y


def _halo_prev(tb, w):
    return pl.BlockSpec((HALO, w), lambda i: (jnp.maximum(i * (tb // HALO) - 1, 0), 0))


def _residual_mid(x, mix, g_post_mix):
    mix_f = mix.astype(F32)
    return x + mix_f * _rms(mix_f) * g_post_mix


def _mix_out(attn, gb, gc, xin, conv_w, g_attn, g_conv, w_out, comm=None):
    seq = attn.shape[0]
    tb = min(seq, WIDE_TOKEN_TILE)

    def body(a_ref, gb_ref, gc_ref, xin_ref, gch_ref, xinh_ref, cw_ref, ga_ref, gcn_ref, w_ref, mix_ref, mixed_ref):
        first = pl.program_id(0) == 0
        _, _, _, y = _conv_parts(gc_ref[...], xin_ref[...], gch_ref[...], xinh_ref[...], cw_ref[...], first)
        conv = gb_ref[...].astype(F32) * y
        a = a_ref[...].astype(F32)
        mixed_ref[:, 0:Q_WIDTH] = (a * _rms(a) * ga_ref[...]).astype(BF16)
        mixed_ref[:, Q_WIDTH:] = (conv * _rms(conv) * gcn_ref[...]).astype(BF16)
        mix_ref[...] = _dot(mixed_ref[...], w_ref[...].reshape(D_MODEL, D_MODEL)).astype(BF16)

    tile = lambda w: pl.BlockSpec((tb, w), lambda i: (i, 0))
    return _pallas(
        body, name="mix_out", grid=(seq // tb,),
        in_specs=[tile(Q_WIDTH), tile(CONV_WIDTH), tile(CONV_WIDTH), tile(CONV_WIDTH),
                  _halo_prev(tb, CONV_WIDTH), _halo_prev(tb, CONV_WIDTH),
                  _resident((CONV_K, CONV_WIDTH)), _resident((1, Q_WIDTH)), _resident((1, CONV_WIDTH)),
                  _resident(w_out.shape)],
        out_specs=[tile(D_MODEL), tile(D_MODEL)],
        out_shape=[SDS((seq, D_MODEL), BF16), SDS((seq, D_MODEL), BF16)],
        operands=(attn, gb, gc, xin, gc, xin, conv_w, g_attn, g_conv, w_out), comm=comm)


def _mlp_loss(x, mix, target, g_post_mix, g_pre_mlp, g_post_mlp, w_up, w_down):
    seq = x.shape[0]
    tb = TOKEN_TILE

    def body(x_ref, mix_ref, t_ref, gpm_ref, g2_ref, g4_ref, wup_ref, wdown_ref,
             up_ref, hn2_ref, dout_ref, dmlp_ref, loss_ref, dg4_ref, act_ref):
        @pl.when(pl.program_id(0) == 0)
        def _():
            loss_ref[...] = jnp.zeros_like(loss_ref)
            dg4_ref[...] = jnp.zeros_like(dg4_ref)

        halves = [slice(0, tb // 2), slice(tb // 2, tb)]
        hv, hn2 = [], []
        for rows in halves:
            hv.append(_residual_mid(x_ref[rows, :], mix_ref[rows, :], gpm_ref[...]))
            hn2.append((hv[-1] * _rms(hv[-1]) * g2_ref[...]).astype(BF16))
            hn2_ref[rows, :] = hn2[-1]
        for k, rows in enumerate(halves):
            for j in range(N_CHIPS):
                up = _dot(hn2[k], _chip_block(wup_ref, j))
                up = jnp.maximum(up, 0.0)
                up_ref[rows, 1024 * j:1024 * (j + 1)] = up.astype(BF16)
                act_ref[rows, 1024 * j:1024 * (j + 1)] = (up * up).astype(BF16)
        w_down_all = wdown_ref[...].reshape(D_FF, D_MODEL)
        loss = jnp.zeros((1, 1), F32)
        dg4 = jnp.zeros((1, D_MODEL), F32)
        for k, rows in enumerate(halves):
            mlp = _dot(act_ref[rows, :], w_down_all)
            rstd = _rms(mlp)
            zhat = mlp * rstd
            diff = hv[k] + zhat * g4_ref[...] - t_ref[rows, :]
            loss = loss + jnp.sum(jnp.sum(diff * diff, axis=1, keepdims=True), axis=0, keepdims=True)
            dout = diff * (1.0 / D_MODEL)
            dout_ref[rows, :] = dout
            dg4 = dg4 + _colsum(dout * zhat)
            dmlp_ref[rows, :] = _norm_bwd(dout, g4_ref[...], zhat, rstd).astype(BF16)
        loss_ref[...] += loss
        dg4_ref[...] += dg4

    tile = lambda w: pl.BlockSpec((tb, w), lambda i: (i, 0))
    return _pallas(
        body, name="mlp_loss", grid=(seq // tb,),
        in_specs=[tile(D_MODEL), tile(D_MODEL), tile(D_MODEL), _resident((1, D_MODEL)), _resident((1, D_MODEL)),
                  _resident((1, D_MODEL)), _resident(w_up.shape), _resident(w_down.shape)],
        out_specs=[tile(D_FF), tile(D_MODEL), tile(D_MODEL), tile(D_MODEL),
                   pl.BlockSpec((1, 1), lambda i: (0, 0)), pl.BlockSpec((1, D_MODEL), lambda i: (0, 0))],
        out_shape=[SDS((seq, D_FF), BF16), SDS((seq, D_MODEL), BF16), SDS((seq, D_MODEL), F32),
                   SDS((seq, D_MODEL), BF16), SDS((1, 1), F32), SDS((1, D_MODEL), F32)],
        scratch=[pltpu.VMEM((tb, D_FF), BF16)],
        operands=(x, mix, target, g_post_mix, g_pre_mlp, g_post_mlp, w_up, w_down))


def _mlp_bwd(dmlp, up, x, dout, mix, g_pre_mlp, g_post_mix, w_up, w_down):
    seq = x.shape[0]
    tb = MLP_BWD_TOKEN_TILE

    def body(dmlp_ref, up_ref, x_ref, dout_ref, mix_ref, g2_ref, gpm_ref, wup_ref, wdown_ref,
             dup_ref, dh_ref, dmix_ref, dg2_ref, dgpm_ref):
        @pl.when(pl.program_id(0) == 0)
        def _():
            dg2_ref[...] = jnp.zeros_like(dg2_ref)
            dgpm_ref[...] = jnp.zeros_like(dgpm_ref)

        subs = [slice(k * MLP_BWD_SUB_TILE, (k + 1) * MLP_BWD_SUB_TILE) for k in range(tb // MLP_BWD_SUB_TILE)]
        dhn2 = []
        for rows in subs:
            dmlp_v = dmlp_ref[rows, :]
            acc = None
            for j in range(N_CHIPS):
                cols = slice(1024 * j, 1024 * (j + 1))
                dact = _dot_nt(dmlp_v, _chip_block(wdown_ref, j))
                dup = (dact * (2.0 * up_ref[rows, cols].astype(F32))).astype(BF16)
                dup_ref[rows, cols] = dup
                part = _dot_nt(dup, _chip_block(wup_ref, j))
                acc = part if acc is None else acc + part
            dhn2.append(acc)
        dg2 = jnp.zeros((1, D_MODEL), F32)
        dgpm = jnp.zeros((1, D_MODEL), F32)
        for k, rows in enumerate(subs):
            mix_v = mix_ref[rows, :].astype(F32)
            hv = _residual_mid(x_ref[rows, :], mix_ref[rows, :], gpm_ref[...])
            r2 = _rms(hv)
            hhat = hv * r2
            dg2 = dg2 + _colsum(dhn2[k] * hhat)
            dh = dout_ref[rows, :] + _norm_bwd(dhn2[k], g2_ref[...], hhat, r2)
            dh_ref[rows, :] = dh.astype(BF16)
            rz = _rms(mix_v)
            zhat = mix_v * rz
            dgpm = dgpm + _colsum(dh * zhat)
            dmix_ref[rows, :] = _norm_bwd(dh, gpm_ref[...], zhat, rz).astype(BF16)
        dg2_ref[...] += dg2
        dgpm_ref[...] += dgpm

    tile = lambda w: pl.BlockSpec((tb, w), lambda i: (i, 0))
    vec = pl.BlockSpec((1, D_MODEL), lambda i: (0, 0))
    return _pallas(
        body, name="mlp_bwd", grid=(seq // tb,),
        in_specs=[tile(D_MODEL), tile(D_FF), tile(D_MODEL), tile(D_MODEL), tile(D_MODEL),
                  _resident((1, D_MODEL)), _resident((1, D_MODEL)), _resident(w_up.shape), _resident(w_down.shape)],
        out_specs=[tile(D_FF), tile(D_MODEL), tile(D_MODEL), vec, vec],
        out_shape=[SDS((seq, D_FF), BF16), SDS((seq, D_MODEL), BF16), SDS((seq, D_MODEL), BF16),
                   SDS((1, D_MODEL), F32), SDS((1, D_MODEL), F32)],
        operands=(dmlp, up, x, dout, mix, g_pre_mlp, g_post_mix, w_up, w_down))


def _mix_bwd(dmix, attn, gb, gc, xin, conv_w, g_attn, g_conv, w_out, n_k):
    seq = attn.shape[0]
    tb = seq // (N_CHIPS * n_k)

    def body(first, dmix_ref, a_ref, gb_ref, gc_ref, xin_ref, gch_ref, xinh_ref, cw_ref, ga_ref, gcn_ref, w_ref,
             dattn_ref, dgb_ref, dy_ref, dga_ref, dgcn_ref, dcw_ref):
        @pl.when(first)
        def _():
            dga_ref[...] = jnp.zeros_like(dga_ref)
            dgcn_ref[...] = jnp.zeros_like(dgcn_ref)
            dcw_ref[...] = jnp.zeros_like(dcw_ref)

        dmixed = _dot_nt(dmix_ref[...], w_ref[...].reshape(D_MODEL, D_MODEL))
        a = a_ref[...].astype(F32)
        ra = _rms(a)
        ahat = a * ra
        dan = dmixed[:, 0:Q_WIDTH]
        dga_ref[...] += _colsum(dan * ahat)
        dattn_ref[...] = _norm_bwd(dan, ga_ref[...], ahat, ra).astype(BF16)
        gbv = gb_ref[...].astype(F32)
        u, u1, u2, y = _conv_parts(gc_ref[...], xin_ref[...], gch_ref[...], xinh_ref[...], cw_ref[...], first)
        conv = gbv * y
        rc = _rms(conv)
        chat = conv * rc
        dcn = dmixed[:, Q_WIDTH:]
        dgcn_ref[...] += _colsum(dcn * chat)
        dconv = _norm_bwd(dcn, gcn_ref[...], chat, rc)
        dgb_ref[...] = (dconv * y).astype(BF16)
        dy = dconv * gbv
        dy_ref[...] = dy.astype(BF16)
        dcw_ref[0:1, :] += _colsum(dy * u2)
        dcw_ref[1:2, :] += _colsum(dy * u1)
        dcw_ref[2:3, :] += _colsum(dy * u)

    tile = lambda w: pl.BlockSpec((tb, w), lambda j, k: (j * n_k + k, 0))
    halo = lambda w: pl.BlockSpec((HALO, w), lambda j, k: (jnp.maximum((j * n_k + k) * (tb // HALO) - 1, 0), 0))
    whole = lambda shape: pl.BlockSpec(shape, lambda j, k: (0,) * len(shape))
    return _Rider(
        body,
        in_specs=[tile(D_MODEL), tile(Q_WIDTH), tile(CONV_WIDTH), tile(CONV_WIDTH), tile(CONV_WIDTH),
                  halo(CONV_WIDTH), halo(CONV_WIDTH),
                  _resident((CONV_K, CONV_WIDTH)), _resident((1, Q_WIDTH)), _resident((1, CONV_WIDTH)),
                  _resident(w_out.shape)],
        out_specs=[tile(Q_WIDTH), tile(CONV_WIDTH), tile(CONV_WIDTH),
                   whole((1, Q_WIDTH)), whole((1, CONV_WIDTH)), whole((CONV_K, CONV_WIDTH))],
        out_shape=[SDS((seq, Q_WIDTH), BF16), SDS((seq, CONV_WIDTH), BF16), SDS((seq, CONV_WIDTH), BF16),
                   SDS((1, Q_WIDTH), F32), SDS((1, CONV_WIDTH), F32), SDS((CONV_K, CONV_WIDTH), F32)],
        operands=(dmix, attn, gb, gc, xin, gc, xin, conv_w, g_attn, g_conv, w_out))


def _attention_bwd(q, dattn, attn, kd0, kd1, vd0, vd1, sinks, comm=None):
    seq = q.shape[0]
    nb = ATTN_BWD_BLOCKS

    def body(sink_ref, q_ref, do_ref, o_ref, kd0_ref, kd1_ref, vd0_ref, vd1_ref,
             dq_ref, dk0_ref, dk1_ref, dv0_ref, dv1_ref, dsink_ref):
        @pl.when(pl.program_id(0) == 0)
        def _():
            for r in (dk0_ref, dk1_ref, dv0_ref, dv1_ref, dsink_ref):
                r[...] = jnp.zeros_like(r)

        lane = lax.broadcasted_iota(jnp.int32, (1, 128), 1)
        dsink = jnp.zeros((1, 128), F32)
        for b in range(nb):
            i = pl.program_id(0) * nb + b
            rows = slice(QBLOCK * b, QBLOCK * (b + 1))
            valid = _attn_valid(i)
            for kv_head, (k_ref, v_ref, dk_ref, dv_ref) in enumerate(
                    ((kd0_ref, vd0_ref, dk0_ref, dv0_ref), (kd1_ref, vd1_ref, dk1_ref, dv1_ref))):
                kband, prev, own = _band(k_ref, i)
                vband, _, _ = _band(v_ref, i)
                base = 256 * kv_head
                qm = _stack_heads(q_ref[rows, base:base + 128], q_ref[rows, base + 128:base + 256])
                dom = _stack_heads(do_ref[rows, base:base + 128], do_ref[rows, base + 128:base + 256])
                om = _stack_heads(o_ref[rows, base:base + 128], o_ref[rows, base + 128:base + 256])
                s = jnp.where(valid, _dot_nt(qm, kband), NEG_INF)
                p, e_sink, inv_l = _softmax_with_sink(s, _sink_column(sink_ref, kv_head))
                p = p * inv_l
                delta = jnp.sum(dom.astype(F32) * om.astype(F32), axis=-1, keepdims=True)
                ds = (p * (_dot_nt(dom, vband) - delta)).astype(BF16)
                sink_term = -(e_sink * inv_l) * delta
                for j in range(4):
                    part = jnp.sum(sink_term[QBLOCK * j:QBLOCK * (j + 1)], axis=0, keepdims=True)
                    dsink = dsink + jnp.where(lane == 4 * kv_head + j, part, 0.0)
                pair0, pair1 = _unstack_heads(_dot(ds, kband))
                dq_ref[rows, base:base + 128] = pair0.astype(BF16)
                dq_ref[rows, base + 128:base + 256] = pair1.astype(BF16)
                dkd = _dot_tn(ds, qm)
                dkd = dkd + pltpu.roll(dkd, HEAD_DIM, 1)
                dvd = _dot_tn(p.astype(BF16), dom)
                dvd = dvd + pltpu.roll(dvd, HEAD_DIM, 1)
                dk_ref[pl.ds(prev, QBLOCK), :] += dkd[0:QBLOCK]
                dk_ref[pl.ds(own, QBLOCK), :] += dkd[QBLOCK:]
                dv_ref[pl.ds(prev, QBLOCK), :] += dvd[0:QBLOCK]
                dv_ref[pl.ds(own, QBLOCK), :] += dvd[QBLOCK:]
        dsink_ref[...] += dsink

    blk = pl.BlockSpec((nb * QBLOCK, Q_WIDTH), lambda i: (i, 0))
    full = _resident((seq, 128))
    acc = pl.BlockSpec((seq, 128), lambda i: (0, 0))
    return _pallas(
        body, name="attention_bwd", grid=(seq // (nb * QBLOCK),),
        in_specs=[pl.BlockSpec(memory_space=pltpu.SMEM), blk, blk, blk, full, full, full, full],
        out_specs=[blk, acc, acc, acc, acc, pl.BlockSpec((1, 128), lambda i: (0, 0))],
        out_shape=[SDS((seq, Q_WIDTH), BF16)] + [SDS((seq, 128), F32)] * 4 + [SDS((1, 128), F32)],
        operands=(sinks, q, dattn, attn, kd0, kd1, vd0, vd1), comm=comm)


def _in_proj_bwd(dq, dk0, dk1, dv0, dv1, dgb, dy, gc, xin, conv_w, x, dh, g_pre, w_in_t, rope):
    seq = x.shape[0]
    tb = min(seq, WIDE_TOKEN_TILE)
    n_tiles = seq // tb

    def body(dq_ref, dk0_ref, dk1_ref, dv0_ref, dv1_ref, dgb_ref, dy_ref, dyh_ref, gc_ref, xin_ref, cw_ref,
             x_ref, dh_ref, g_ref, w_ref, c_ref, sa_ref, sb_ref,
             dproj_ref, gx_ref, dg_ref):
        i = pl.program_id(0)

        @pl.when(i == 0)
        def _():
            dg_ref[...] = jnp.zeros_like(dg_ref)

        dy = dy_ref[...].astype(F32)
        ext = jnp.concatenate([dy, jnp.where(i == n_tiles - 1, 0.0, dyh_ref[...].astype(F32))], axis=0)
        dy1 = pltpu.roll(ext, tb + HALO - 1, 0)[0:tb]
        dy2 = pltpu.roll(ext, tb + HALO - 2, 0)[0:tb]
        cw = cw_ref[...]
        du = cw[2:3, :] * dy + cw[1:2, :] * dy1 + cw[0:1, :] * dy2
        scale = 1.0 / math.sqrt(HEAD_DIM)
        base = Q_WIDTH + 2 * KV_WIDTH
        halves = [slice(0, tb // 2), slice(tb // 2, tb)]
        low = _lane_lt64((tb // 2, 128))
        for rows in halves:
            c, sa, sb = _rope_tile(c_ref.at[rows, :], sa_ref, sb_ref)
            for p in range(Q_WIDTH // 128):
                dproj_ref[rows, 128 * p:128 * (p + 1)] = _rope_transposed(
                    dq_ref[rows, 128 * p:128 * (p + 1)].astype(F32) * scale, c, sa, sb).astype(BF16)
            dk = jnp.where(low, dk0_ref[rows, :], dk1_ref[rows, :])
            dproj_ref[rows, Q_WIDTH:Q_WIDTH + KV_WIDTH] = _rope_transposed(dk, c, sa, sb).astype(BF16)
            dproj_ref[rows, Q_WIDTH + KV_WIDTH:base] = jnp.where(low, dv0_ref[rows, :], dv1_ref[rows, :]).astype(BF16)
            dproj_ref[rows, base:base + CONV_WIDTH] = dgb_ref[rows, :]
            dproj_ref[rows, base + CONV_WIDTH:base + 2 * CONV_WIDTH] = (du[rows] * xin_ref[rows, :].astype(F32)).astype(BF16)
            dproj_ref[rows, base + 2 * CONV_WIDTH:] = (du[rows] * gc_ref[rows, :].astype(F32)).astype(BF16)
        w_all = w_ref[...].reshape(IN_COLS, D_MODEL)
        dhn = [_dot(dproj_ref[rows, :], w_all) for rows in halves]
        dg = jnp.zeros((1, D_MODEL), F32)
        for k, rows in enumerate(halves):
            xv = x_ref[rows, :]
            r = _rms(xv)
            xhat = xv * r
            dg = dg + _colsum(dhn[k] * xhat)
            gx_ref[rows, :] = dh_ref[rows, :].astype(F32) + _norm_bwd(dhn[k], g_ref[...], xhat, r)
        dg_ref[...] += dg

    tile = lambda w: pl.BlockSpec((tb, w), lambda i: (i, 0))
    halo_next = pl.BlockSpec((HALO, CONV_WIDTH), lambda i: (jnp.minimum((i + 1) * (tb // HALO), seq // HALO - 1), 0))
    return _pallas(
        body, name="in_proj_bwd", grid=(n_tiles,),
        in_specs=[tile(Q_WIDTH), tile(128), tile(128), tile(128), tile(128), tile(CONV_WIDTH), tile(CONV_WIDTH), halo_next,
                  tile(CONV_WIDTH), tile(CONV_WIDTH), _resident((CONV_K, CONV_WIDTH)),
                  tile(D_MODEL), tile(D_MODEL), _resident((1, D_MODEL)), _resident(w_in_t.shape), *_rope_specs(tb)],
        out_specs=[tile(IN_COLS), tile(D_MODEL), pl.BlockSpec((1, D_MODEL), lambda i: (0, 0))],
        out_shape=[SDS((seq, IN_COLS), BF16), SDS((seq, D_MODEL), F32), SDS((1, D_MODEL), F32)],
        operands=(dq, dk0, dk1, dv0, dv1, dgb, dy, dy, gc, xin, conv_w, x, dh, g_pre, w_in_t, *rope))


def _wgrad_grid(seq, per_chip, h_rows, with_rider=False):
    chips_per_step = 1 if per_chip else N_CHIPS
    m = chips_per_step * 2 * h_rows
    bt = min(seq, WGRAD_TOKEN_TILE if per_chip and not with_rider else WGRAD_TOKEN_TILE // 2)
    return chips_per_step, m, bt, seq // bt


def _wgrad(name, a, b, *, per_chip, h_rows, square_a=False, comm=None, rider=None):
    seq = a.shape[0]
    chips_per_step, m, bt, n_k = _wgrad_grid(seq, per_chip, h_rows, rider is not None)
    a_cols = m if per_chip else a.shape[1]
    a_wide = a.shape[1] > a_cols
    b_wide = b.shape[1] > D_MODEL

    def body(a_ref, b_ref, g_ref):
        @pl.when(pl.program_id(1) == 0)
        def _():
            g_ref[...] = jnp.zeros_like(g_ref)

        av = a_ref[...]
        if square_a:
            av = (av.astype(F32) * av.astype(F32)).astype(BF16)
        g_ref[...] += _dot_tn(av, b_ref[...]).reshape(g_ref.shape)

    a_spec = pl.BlockSpec((bt, a_cols), (lambda j, k: (k, j)) if a_wide else (lambda j, k: (k, 0)))
    b_spec = pl.BlockSpec((bt, D_MODEL), (lambda j, k: (k, j)) if b_wide else (lambda j, k: (k, 0)))
    g_spec = pl.BlockSpec((chips_per_step, 2, h_rows, D_MODEL), lambda j, k: (j, 0, 0, 0),
                          pipeline_mode=None if per_chip else pl.Buffered(1))
    return _pallas(
        body, name=name, grid=(N_CHIPS if per_chip else 1, n_k),
        in_specs=[a_spec, b_spec], out_specs=[g_spec], out_shape=[SDS((N_CHIPS, 2, h_rows, D_MODEL), F32)],
        operands=(a, b), comm=comm, rider=rider)


def _adamw_math(w, g, m, v):
    m = ADAM_B1 * m + (1.0 - ADAM_B1) * g
    v = ADAM_B2 * v + (1.0 - ADAM_B2) * (g * g)
    m_hat = m / (1.0 - ADAM_B1 ** ADAM_STEP)
    v_hat = v / (1.0 - ADAM_B2 ** ADAM_STEP)
    delta = -ADAM_LR * (m_hat / (jnp.sqrt(v_hat) + ADAM_EPS) + ADAM_WD * w)
    return delta, m, v


def _adamw_rows(name, reduced, w, m, v, rt):
    per_half = reduced.shape[1] // rt

    def body(r_ref, w_ref, m_ref, v_ref, g_out, d_out, m_out, v_out):
        g = r_ref[0]
        g_out[...] = g
        d_out[...], m_out[...], v_out[...] = _adamw_math(w_ref[...], g, m_ref[...], v_ref[...])

    blk = pl.BlockSpec((rt, D_MODEL), lambda h, r: (h * per_half + r, 0))
    return _pallas(
        body, name=name, grid=(2, per_half),
        in_specs=[pl.BlockSpec((1, rt, D_MODEL), lambda h, r: (h, r, 0)), blk, blk, blk],
        out_specs=[blk, blk, blk, blk], out_shape=[SDS(w.shape, F32)] * 4, operands=(reduced, w, m, v))


SC_TILES = 32
SC_LANES = 16
SC_CHUNK_ROWS = 8


def _adamw_sparsecore(name, g, w, m, v):
    rows = w.shape[0]
    per_tile = rows // SC_TILES

    def body(g_hbm, w_hbm, m_hbm, v_hbm, go_hbm, d_hbm, mo_hbm, vo_hbm, g_buf, w_buf, m_buf, v_buf, d_buf):
        tile = lax.axis_index("subcore") * 2 + lax.axis_index("sparsecore")

        @pl.loop(0, per_tile, step=SC_CHUNK_ROWS)
        def _(r0):
            band = pl.ds(tile * per_tile + r0, SC_CHUNK_ROWS)
            for hbm, buf in ((g_hbm, g_buf), (w_hbm, w_buf), (m_hbm, m_buf), (v_hbm, v_buf)):
                pltpu.sync_copy(hbm.at[band, :], buf)

            @pl.loop(0, SC_CHUNK_ROWS)
            def _(r):
                @pl.loop(0, D_MODEL, step=SC_LANES)
                def _(i):
                    at = (r, pl.ds(i, SC_LANES))
                    d_buf[at], m_buf[at], v_buf[at] = _adamw_math(w_buf[at], g_buf[at], m_buf[at], v_buf[at])

            for buf, hbm in ((g_buf, go_hbm), (d_buf, d_hbm), (m_buf, mo_hbm), (v_buf, vo_hbm)):
                pltpu.sync_copy(buf, hbm.at[band, :])

    return pl.kernel(
        body, name=name, out_type=[SDS(w.shape, F32)] * 4,
        mesh=plsc.VectorSubcoreMesh(core_axis_name="sparsecore", subcore_axis_name="subcore"),
        scratch_types=[pltpu.VMEM((SC_CHUNK_ROWS, D_MODEL), F32)] * 5,
    )(g, w, m, v)


def _adamw_small(packed_grads, w, m, v):
    names = SMALL_NAMES
    n = len(names)
    conv_local = w["conv_w"].shape[-1]

    def body(*refs):
        gp = refs[0]
        w_refs, m_refs, v_refs = refs[1:1 + n], refs[1 + n:1 + 2 * n], refs[1 + 2 * n:1 + 3 * n]
        outs = refs[1 + 3 * n:]
        g_out, d_out, m_out, v_out = outs[0:n], outs[n:2 * n], outs[2 * n:3 * n], outs[3 * n:4 * n]
        chip = 2 * lax.axis_index("x") + lax.axis_index("y")

        def step(k, g, index=None):
            pick = (lambda r: r[...]) if index is None else (lambda r: r[index])
            d, new_m, new_v = _adamw_math(pick(w_refs[k]), g, pick(m_refs[k]), pick(v_refs[k]))
            for ref, val in ((g_out[k], g), (d_out[k], d), (m_out[k], new_m), (v_out[k], new_v)):
                if index is None:
                    ref[...] = val
                else:
                    ref[index] = val

        for k, name in enumerate(names):
            if name in SMALL_VECTORS:
                step(k, gp[SMALL_VECTORS.index(name):SMALL_VECTORS.index(name) + 1, :])
            elif name == "attn_group_norm":
                step(k, gp[4:5, 0:Q_WIDTH])
            elif name == "conv_group_norm":
                step(k, gp[4:5, Q_WIDTH:])
            elif name == "attn_sinks":
                step(k, gp[7:8, 0:8])
            else:
                for t in range(CONV_K):
                    row, base = 5 + t // 2, CONV_WIDTH * (t % 2)
                    g = gp[row:row + 1, base:base + conv_local]
                    for j in range(1, CONV_WIDTH // conv_local):
                        g = jnp.where(chip == j, gp[row:row + 1, base + conv_local * j:base + conv_local * (j + 1)], g)
                    step(k, g, index=(0, slice(t, t + 1), slice(None)))

    shapes = [SDS(w[name].shape, F32) for name in names]
    res = pl.pallas_call(
        body, name="adamw_small", in_specs=[VMEM_WHOLE] * (1 + 3 * n), out_specs=[VMEM_WHOLE] * (4 * n),
        out_shape=shapes * 4,
    )(packed_grads, *[w[k] for k in names], *[m[k] for k in names], *[v[k] for k in names])
    return [dict(zip(names, res[i * n:(i + 1) * n])) for i in range(4)]


SMALL_VECTORS = ("pre_mix_norm", "post_mix_norm", "pre_mlp_norm", "post_mlp_norm")
SMALL_NAMES = SMALL_VECTORS + ("attn_group_norm", "conv_group_norm", "conv_w", "attn_sinks")


def _pack_small(p):
    rows = [p[n].reshape(1, D_MODEL) for n in SMALL_VECTORS]
    rows.append(jnp.concatenate([p["attn_group_norm"].reshape(1, -1), p["conv_group_norm"].reshape(1, -1)], axis=1))
    cw = p["conv_w"].reshape(CONV_K, -1)
    rows.append(jnp.pad(cw, ((0, 1), (0, CONV_WIDTH - cw.shape[1]))).reshape(2, D_MODEL))
    last = jnp.concatenate([p["attn_sinks"].reshape(1, 8), p.get("loss_sum", jnp.zeros((1, 1), F32))], axis=1)
    rows.append(jnp.pad(last, ((0, 0), (0, D_MODEL - 9))))
    return jnp.concatenate(rows, axis=0)


WEIGHT_ORDER = ("pre_mix_norm", "w_in", "conv_w", "attn_sinks", "attn_group_norm", "conv_group_norm", "w_out",
                "post_mix_norm", "pre_mlp_norm", "w_up", "w_down", "post_mlp_norm")


def kernel(x, pre_mix_norm, w_in, conv_w, attn_sinks, attn_group_norm, conv_group_norm, w_out, post_mix_norm, pre_mlp_norm, w_up, w_down, post_mlp_norm, loss_target, m_pre_mix_norm, m_w_in, m_conv_w, m_attn_sinks, m_attn_group_norm, m_conv_group_norm, m_w_out, m_post_mix_norm, m_pre_mlp_norm, m_w_up, m_w_down, m_post_mlp_norm, v_pre_mix_norm, v_w_in, v_conv_w, v_attn_sinks, v_attn_group_norm, v_conv_group_norm, v_w_out, v_post_mix_norm, v_pre_mlp_norm, v_w_up, v_w_down, v_post_mlp_norm):
    w = dict(pre_mix_norm=pre_mix_norm, w_in=w_in, conv_w=conv_w, attn_sinks=attn_sinks, attn_group_norm=attn_group_norm,
             conv_group_norm=conv_group_norm, w_out=w_out, post_mix_norm=post_mix_norm, pre_mlp_norm=pre_mlp_norm,
             w_up=w_up, w_down=w_down, post_mlp_norm=post_mlp_norm)
    m = dict(pre_mix_norm=m_pre_mix_norm, w_in=m_w_in, conv_w=m_conv_w, attn_sinks=m_attn_sinks,
             attn_group_norm=m_attn_group_norm, conv_group_norm=m_conv_group_norm, w_out=m_w_out,
             post_mix_norm=m_post_mix_norm, pre_mlp_norm=m_pre_mlp_norm, w_up=m_w_up, w_down=m_w_down,
             post_mlp_norm=m_post_mlp_norm)
    v = dict(pre_mix_norm=v_pre_mix_norm, w_in=v_w_in, conv_w=v_conv_w, attn_sinks=v_attn_sinks,
             attn_group_norm=v_attn_group_norm, conv_group_norm=v_conv_group_norm, w_out=v_w_out,
             post_mix_norm=v_post_mix_norm, pre_mlp_norm=v_pre_mlp_norm, w_up=v_w_up, w_down=v_w_down,
             post_mlp_norm=v_post_mlp_norm)
    core = lax.axis_index("c").astype(jnp.int32).reshape(1)
    xs, target = x[0], loss_target[0]
    rope = _rope_inputs(xs.shape[0])

    hb_up, hb_down, hb_out, hb_in = _cast_halves(core, w_up[0], w_down[0], w_out[0], w_in[0].T)
    conv_pad = jnp.pad(conv_w[0], ((0, 8 - CONV_K), (0, 0)))
    wf_in, conv_all = _gather_whole(hb_in, conv_pad)
    conv_full = conv_all[:, :CONV_K, :].transpose(1, 0, 2).reshape(CONV_K, CONV_WIDTH)

    whole_up, early, late = (0, H_UP), (0, DOWN_EARLY_ROWS), (DOWN_EARLY_ROWS, H_DOWN - DOWN_EARLY_ROWS)
    *proj, wf_up, wf_out, wf_down = _in_proj(
        xs, pre_mix_norm, wf_in, rope,
        comm=_merge(_relay(hb_up, None, first=whole_up), _gather_first(hb_out), _relay(hb_down, None, first=early)))
    q, kd0, kd1, vd0, vd1, gb, gc, xin, hn = proj
    attn, wf_up, wf_out, wf_down = _attention_fwd(
        q, kd0, kd1, vd0, vd1, attn_sinks,
        comm=_merge(_relay(None, wf_up, second=whole_up), _gather_second(wf_out),
                    _relay(hb_down, wf_down, first=late, second=early)))
    mix, mixed, wf_up, wf_down = _mix_out(
        attn, gb, gc, xin, conv_full, attn_group_norm, conv_group_norm, wf_out,
        comm=_merge(_relay(None, wf_up, third=whole_up), _relay(None, wf_down, second=late, third=early, third_after=late)))
    up, hn2, dout, dmlp, loss_sum, dg_post_mlp = _mlp_loss(xs, mix, target, post_mix_norm, pre_mlp_norm, post_mlp_norm,
                                                           wf_up, wf_down)

    dup, dh, dmix, dg_pre_mlp, dg_post_mix = _mlp_bwd(dmlp, up, xs, dout, mix, pre_mlp_norm, post_mix_norm, wf_up, wf_down)
    n_k = _wgrad_grid(xs.shape[0], True, H_DOWN, with_rider=True)[3]
    g_down, dattn, dgb, dy, dg_attn, dg_conv, dconv_w = _wgrad(
        "wgrad_down", up, dmlp, per_chip=True, h_rows=H_DOWN, square_a=True,
        rider=_mix_bwd(dmix, attn, gb, gc, xin, conv_full, attn_group_norm, conv_group_norm, wf_out, n_k))
    g_up, got_down = _wgrad("wgrad_up", hn2, dup, per_chip=True, h_rows=H_UP, comm=_pair_send(g_down))
    p_down = _pair_sum("pair_sum_down", core, g_down, got_down)
    g_out, got_up = _wgrad("wgrad_out", mixed, dmix, per_chip=False, h_rows=H_OUT, comm=_pair_send(g_up))
    p_up = _pair_sum("pair_sum_up", core, g_up, got_up)
    dq, dk0, dk1, dv0, dv1, dsink, ex_down, ex_up, got_out = _attention_bwd(
        q, dattn, attn, kd0, kd1, vd0, vd1, attn_sinks,
        comm=_merge(_chip_exchange(p_down), _chip_exchange(p_up), _pair_send(g_out)))
    out_g, out_d, out_m, out_v = {}, {}, {}, {}
    r_down, r_up = _finish_reduce([ex_down, ex_up])
    out_g["w_down"], out_d["w_down"], out_m["w_down"], out_v["w_down"] = _adamw_sparsecore(
        "adamw_down", r_down.reshape(2 * H_DOWN, D_MODEL), w_down[0], m_w_down[0], v_w_down[0])
    out_g["w_up"], out_d["w_up"], out_m["w_up"], out_v["w_up"] = _adamw_sparsecore(
        "adamw_up", r_up.reshape(2 * H_UP, D_MODEL), w_up[0], m_w_up[0], v_w_up[0])

    p_out = _pair_sum("pair_sum_out", core, g_out, got_out)
    dproj, grad_x, dg_pre_mix = _in_proj_bwd(dq, dk0, dk1, dv0, dv1, dgb, dy, gc, xin, conv_full, xs, dh, pre_mix_norm,
                                             wf_in, rope)
    g_in, ex_out = _wgrad("wgrad_in", dproj, hn, per_chip=False, h_rows=H_IN, comm=_chip_exchange(p_out))
    small = dict(pre_mix_norm=dg_pre_mix, conv_w=dconv_w, attn_sinks=dsink[:, :8], attn_group_norm=dg_attn,
                 conv_group_norm=dg_conv, post_mix_norm=dg_post_mix, pre_mlp_norm=dg_pre_mlp, post_mlp_norm=dg_post_mlp,
                 loss_sum=loss_sum)
    r_out, r_in, small_total = _tail_reduce(g_in, [ex_out], _pack_small(small))

    out_g["w_out"], out_d["w_out"], out_m["w_out"], out_v["w_out"] = _adamw_rows(
        "adamw_out", r_out, w_out[0], m_w_out[0], v_w_out[0], H_OUT)
    in_t = _adamw_rows("adamw_in", r_in, w_in[0].T, m_w_in[0].T, v_w_in[0].T, H_IN)
    out_g["w_in"], out_d["w_in"], out_m["w_in"], out_v["w_in"] = [t.T for t in in_t]

    loss = small_total[7, 8] * (0.5 / D_MODEL)
    for out, part in zip((out_g, out_d, out_m, out_v), _adamw_small(small_total, w, m, v)):
        out.update(part)

    def shaped(d):
        return [d[n].reshape(w[n].shape) for n in WEIGHT_ORDER]

    return (loss, grad_x[None], *shaped(out_g), *shaped(out_d), *shaped(out_m), *shaped(out_v))
```

```python
import math
from typing import Callable, NamedTuple

import jax
import jax.numpy as jnp
import numpy as np
from jax import lax
from jax.experimental import pallas as pl
from jax.experimental.pallas import tpu as pltpu

F32 = jnp.float32
BF16 = jnp.bfloat16

D_MODEL = 1024
HEAD_DIM = 64
Q_WIDTH = 512
KV_WIDTH = 128
CONV_WIDTH = 512
CONV_K = 3
D_FF = 4096
IN_COLS = 2304
QBLOCK = 128
ROT_DIM = 16
ROPE_THETA = 500000.0
NORM_EPS = 1e-6
NEG_INF = -1e30
N_CHIPS = 4

ADAM_LR = 0.001
ADAM_B1 = 0.9
ADAM_B2 = 0.999
ADAM_EPS = 1e-08
ADAM_WD = 0.01
ADAM_STEP = 10

H_UP, H_DOWN, H_OUT, H_IN = 512, 512, 128, 288
DOWN_EARLY_ROWS = 224

TOKEN_TILE = 512
WIDE_TOKEN_TILE = 1024
MLP_BWD_TOKEN_TILE = 512
MLP_BWD_SUB_TILE = 256
ATTN_FWD_BLOCKS = 16
ATTN_BWD_BLOCKS = 2
WGRAD_TOKEN_TILE = 4096
VMEM_LIMIT_V7X = 56 * 1024 * 1024

MESH = pl.DeviceIdType.MESH
ANY = pl.BlockSpec(memory_space=pl.ANY)
VMEM_WHOLE = pl.BlockSpec(memory_space=pltpu.VMEM)
SDS = jax.ShapeDtypeStruct


def _resident(shape):
    zeros = (0,) * len(shape)
    return pl.BlockSpec(shape, lambda *_: zeros, pipeline_mode=pl.Buffered(1))


def _rms(v):
    return lax.rsqrt(jnp.mean(v * v, axis=-1, keepdims=True) + NORM_EPS)


def _norm_bwd(dy, gain, vhat, rstd):
    t = dy * gain
    return rstd * (t - vhat * jnp.mean(t * vhat, axis=-1, keepdims=True))


def _colsum(v):
    return jnp.sum(v, axis=0, keepdims=True)


def _dot_nt(a, b):
    return lax.dot_general(a, b, (((1,), (1,)), ((), ())), preferred_element_type=F32)


def _dot_tn(a, b):
    return lax.dot_general(a, b, (((0,), (0,)), ((), ())), preferred_element_type=F32)


def _dot(a, b):
    return jnp.dot(a, b, preferred_element_type=F32)


def _chip_block(w_ref, chip):
    both = w_ref[pl.ds(2 * chip, 2)]
    return both.reshape(2 * both.shape[1], both.shape[2])


def _lane_lt64(shape):
    return lax.broadcasted_iota(jnp.int32, shape, 1) < HEAD_DIM


class _Comm(NamedTuple):
    operands: tuple
    out_shapes: tuple
    aliases: dict
    n_remote: int
    n_local: int
    plan: Callable
    after: Callable = None


def _merge(*comms):
    operands, out_shapes, aliases, parts = [], [], {}, []
    n_remote = n_local = 0
    for cm in comms:
        parts.append((len(operands), len(out_shapes), n_remote, n_local, cm))
        for k, v in cm.aliases.items():
            aliases[len(operands) + k] = len(out_shapes) + v
        operands += cm.operands
        out_shapes += cm.out_shapes
        n_remote += cm.n_remote
        n_local += cm.n_local

    def run(which, ins, outs, send, recv, loc):
        sends, recvs, locs = [], [], []
        for i0, o0, r0, l0, cm in parts:
            stage = getattr(cm, which)
            if stage is not None:
                s, r, l = stage(ins[i0:i0 + len(cm.operands)], outs[o0:o0 + len(cm.out_shapes)],
                                lambda k, r0=r0: send(r0 + k), lambda k, r0=r0: recv(r0 + k), lambda k, l0=l0: loc(l0 + k))
                sends, recvs, locs = sends + s, recvs + r, locs + l
        return sends, recvs, locs

    def plan(*args):
        return run("plan", *args)

    def after(*args):
        return run("after", *args)

    return _Comm(tuple(operands), tuple(out_shapes), aliases, n_remote, n_local, plan,
                 after if any(cm.after is not None for cm in comms) else None)


def _sem_scratch(comm):
    return [pltpu.SemaphoreType.DMA((max(comm.n_remote, 1),)), pltpu.SemaphoreType.DMA((max(comm.n_remote, 1),)),
            pltpu.SemaphoreType.DMA((max(comm.n_local, 1),))]


class _Rider(NamedTuple):
    body: Callable
    in_specs: list
    out_specs: list
    out_shape: list
    operands: tuple


def _pallas(body, *, name, grid, in_specs, out_specs, out_shape, operands, scratch=(), comm=None, rider=None, aliases=None):
    params = pltpu.CompilerParams(dimension_semantics=("arbitrary",) * len(grid), vmem_limit_bytes=VMEM_LIMIT_V7X)
    aliases = dict(aliases or {})
    if rider is not None:
        own_in, own_out, ride_in, ride_out = len(in_specs), len(out_specs), len(rider.in_specs), len(rider.out_specs)
        own_body = body

        def body(*refs):
            o0 = own_in + ride_in
            s0 = o0 + own_out + ride_out
            own_body(*refs[:own_in], *refs[o0:o0 + own_out], *refs[s0:])
            first = None
            for axis in range(len(grid)):
                at_start = pl.program_id(axis) == 0
                first = at_start if first is None else jnp.logical_and(first, at_start)
            rider.body(first, *refs[own_in:o0], *refs[o0 + own_out:s0])

        in_specs, out_specs = list(in_specs) + rider.in_specs, list(out_specs) + rider.out_specs
        out_shape, operands = list(out_shape) + rider.out_shape, tuple(operands) + tuple(rider.operands)
    if comm is None:
        return pl.pallas_call(body, name=name, grid=grid, in_specs=in_specs, out_specs=out_specs, out_shape=out_shape,
                              scratch_shapes=list(scratch), input_output_aliases=aliases,
                              compiler_params=params)(*operands)
    n_in, n_out, n_scr = len(in_specs), len(out_specs), len(scratch)
    c_in, c_out = len(comm.operands), len(comm.out_shapes)

    def with_comm(*refs):
        ins, c_ins = refs[:n_in], refs[n_in:n_in + c_in]
        o0 = n_in + c_in
        outs, c_outs = refs[o0:o0 + n_out], refs[o0 + n_out:o0 + n_out + c_out]
        s0 = o0 + n_out + c_out
        scr = refs[s0:s0 + n_scr]
        send_sems, recv_sems, local_sems = refs[s0 + n_scr:]
        first = last = None
        for axis, size in enumerate(grid):
            at_start, at_end = pl.program_id(axis) == 0, pl.program_id(axis) == size - 1
            first = at_start if first is None else jnp.logical_and(first, at_start)
            last = at_end if last is None else jnp.logical_and(last, at_end)

        def copies():
            return comm.plan(c_ins, c_outs, lambda k: send_sems.at[k], lambda k: recv_sems.at[k],
                             lambda k: local_sems.at[k])

        @pl.when(first)
        def _():
            sends, _, locs = copies()
            for cp in sends + locs:
                cp.start()

        body(*ins, *outs, *scr)

        @pl.when(last)
        def _():
            sends, recvs, locs = copies()
            for cp in recvs:
                cp.wait_recv()
            for cp in sends:
                cp.wait_send()
            for cp in locs:
                cp.wait()
            if comm.after is not None:
                sends, recvs, _ = comm.after(c_ins, c_outs, lambda k: send_sems.at[k], lambda k: recv_sems.at[k],
                                             lambda k: local_sems.at[k])
                for cp in sends:
                    cp.start()
                for cp in recvs:
                    cp.wait_recv()
                for cp in sends:
                    cp.wait_send()

    return pl.pallas_call(
        with_comm, name=name, grid=grid,
        in_specs=list(in_specs) + [ANY] * c_in, out_specs=list(out_specs) + [ANY] * c_out,
        out_shape=list(out_shape) + list(comm.out_shapes),
        scratch_shapes=list(scratch) + _sem_scratch(comm),
        input_output_aliases={**aliases, **{n_in + k: n_out + v for k, v in comm.aliases.items()}},
        compiler_params=params)(*operands, *comm.operands)


def _place():
    return lax.axis_index("x"), lax.axis_index("y"), lax.axis_index("c")


def _other_chips(x, y):
    return [(1 - x, y), (x, 1 - y), (1 - x, 1 - y)]


def _slot(px, py, pc):
    return 4 * px + 2 * py + pc


def _remote(src, dst, send_sem, recv_sem, to):
    return pltpu.make_async_remote_copy(src_ref=src, dst_ref=dst, send_sem=send_sem, recv_sem=recv_sem,
                                        device_id=to, device_id_type=MESH)


def _gather_first(half_block):
    def plan(ins, outs, send, recv, loc):
        (blk,), (full,) = ins, outs
        x, y, c = _place()
        chips = _other_chips(x, y)
        mine = full.at[_slot(x, y, c)]
        sends = [_remote(blk, mine, send(0), recv(0), (x, y, 1 - c))]
        sends += [_remote(blk, mine, send(1 + j), recv(1 + j), (*chip, c)) for j, chip in enumerate(chips)]
        recvs = [_remote(blk, full.at[_slot(x, y, 1 - c)], send(0), recv(0), (x, y, 1 - c))]
        recvs += [_remote(blk, full.at[_slot(*chip, c)], send(1 + j), recv(1 + j), (*chip, c))
                  for j, chip in enumerate(chips)]
        return sends, recvs, [pltpu.make_async_copy(blk, mine, loc(0))]

    return _Comm((half_block,), (SDS((2 * N_CHIPS,) + half_block.shape, half_block.dtype),), {}, 4, 1, plan)


def _gather_second(partly_gathered):
    def plan(ins, outs, send, recv, loc):
        (src,), (full,) = ins, outs
        x, y, c = _place()
        chips = _other_chips(x, y)
        sends = [_remote(src.at[_slot(*chip, c)], full.at[_slot(*chip, c)], send(j), recv(j), (x, y, 1 - c))
                 for j, chip in enumerate(chips)]
        recvs = [_remote(src.at[_slot(*chip, 1 - c)], full.at[_slot(*chip, 1 - c)], send(j), recv(j), (x, y, 1 - c))
                 for j, chip in enumerate(chips)]
        return sends, recvs, []

    return _Comm((partly_gathered,), (SDS(partly_gathered.shape, partly_gathered.dtype),), {0: 0}, 3, 0, plan)


def _relay_pieces(full, rows, x, y, c):
    start, half = rows[0], rows[1] // 2
    upper, lower = pl.ds(start, half), pl.ds(start + half, half)
    diagonal = full.at[_slot(1 - x, 1 - y, c)]
    return [(full.at[_slot(1 - x, y, c), upper], diagonal.at[upper], (x, 1 - y, c)),
            (full.at[_slot(x, 1 - y, c), lower], diagonal.at[lower], (1 - x, y, c))]


def _relay(half_block, so_far, first=None, second=None, third=None, third_after=None):
    has_block, has_buffer = half_block is not None, so_far is not None
    shape = so_far.shape if has_buffer else (2 * N_CHIPS,) + half_block.shape
    dtype = so_far.dtype if has_buffer else half_block.dtype

    def third_leg(rows, k, ins, outs, send, recv):
        src, full = (ins[-1] if has_buffer else outs[0]), outs[0]
        x, y, c = _place()
        span, sibling = pl.ds(*rows), (x, y, 1 - c)
        here, there = _slot(1 - x, 1 - y, c), _slot(1 - x, 1 - y, 1 - c)
        return ([_remote(src.at[here, span], full.at[here, span], send(k), recv(k), sibling)],
                [_remote(src.at[there, span], full.at[there, span], send(k), recv(k), sibling)])

    def plan(ins, outs, send, recv, loc):
        src, full = (ins[-1] if has_buffer else outs[0]), outs[0]
        x, y, c = _place()
        sibling = (x, y, 1 - c)
        sends, recvs, locs = [], [], []
        if first is not None:
            span = pl.ds(*first)
            blk, mine = ins[0].at[span], full.at[_slot(x, y, c), span]
            for k, peer in enumerate([sibling, (1 - x, y, c), (x, 1 - y, c)]):
                sends.append(_remote(blk, mine, send(k), recv(k), peer))
                recvs.append(_remote(blk, full.at[_slot(*peer), span], send(k), recv(k), peer))
            locs.append(pltpu.make_async_copy(blk, mine, loc(0)))
        if second is not None:
            span = pl.ds(*second)
            for k, chip in enumerate([(1 - x, y), (x, 1 - y)]):
                sends.append(_remote(src.at[_slot(*chip, c), span], full.at[_slot(*chip, c), span], send(3 + k), recv(3 + k),
                                     sibling))
                recvs.append(_remote(src.at[_slot(*chip, 1 - c), span], full.at[_slot(*chip, 1 - c), span], send(3 + k),
                                     recv(3 + k), sibling))
            for k, (piece, lands, peer) in enumerate(_relay_pieces(full, second, x, y, c)):
                sends.append(_remote(piece, piece, send(5 + k), recv(5 + k), peer))
                recvs.append(_remote(lands, lands, send(5 + k), recv(5 + k), peer))
        if third is not None:
            s, r = third_leg(third, 7, ins, outs, send, recv)
            sends, recvs = sends + s, recvs + r
        return sends, recvs, locs

    def after(ins, outs, send, recv, loc):
        s, r = third_leg(third_after, 8, ins, outs, send, recv)
        return s, r, []

    operands = ((half_block,) if has_block else ()) + ((so_far,) if has_buffer else ())
    return _Comm(operands, (SDS(shape, dtype),), {len(operands) - 1: 0} if has_buffer else {}, 9, 1, plan,
                 after if third_after is not None else None)


def _gather_whole(half_block, small_block):
    rows = half_block.shape[0]

    def body(blk_ref, small_ref, out_ref, small_out_ref, send_sems, recv_sems, local_sems):
        x, y, c = _place()
        me, sibling = (x, y, c), (x, y, 1 - c)
        neighbours, diagonal = [(1 - x, y), (x, 1 - y)], (1 - x, 1 - y)

        def copy(k, block, to, src=None):
            return _remote(out_ref.at[_slot(*block)] if src is None else src, out_ref.at[_slot(*block)],
                           send_sems.at[k], recv_sems.at[k], to)

        def small_copy(k, chip, to):
            return _remote(small_ref, small_out_ref.at[2 * chip[0] + chip[1]], send_sems.at[8 + k], recv_sems.at[8 + k], to)

        mine = pltpu.make_async_copy(blk_ref, out_ref.at[_slot(*me)], local_sems.at[0])
        mine_small = pltpu.make_async_copy(small_ref, small_out_ref.at[2 * x + y], local_sems.at[1])
        mine.start()
        mine_small.start()
        started = [copy(0, me, sibling, src=blk_ref)]
        started += [copy(1 + k, me, (*chip, c), src=blk_ref) for k, chip in enumerate(neighbours)]
        started += [small_copy(k, (x, y), (*chip, c)) for k, chip in enumerate(neighbours + [diagonal])]
        for cp in started:
            cp.start()
        pieces = _relay_pieces(out_ref, (0, rows), x, y, c)
        for k, chip in enumerate(neighbours):
            copy(1 + k, (*chip, c), me).wait_recv()
            piece, _, peer = pieces[k]
            started += [copy(3 + k, (*chip, c), sibling), _remote(piece, piece, send_sems.at[5 + k], recv_sems.at[5 + k], peer)]
            started[-2].start()
            started[-1].start()
        for k, (_, lands, peer) in enumerate(pieces):
            _remote(lands, lands, send_sems.at[5 + k], recv_sems.at[5 + k], peer).wait_recv()
        started.append(copy(7, (*diagonal, c), sibling))
        started[-1].start()
        copy(0, sibling, me).wait_recv()
        for k, chip in enumerate(neighbours):
            copy(3 + k, (*chip, 1 - c), me).wait_recv()
        copy(7, (*diagonal, 1 - c), me).wait_recv()
        for k, chip in enumerate(neighbours + [diagonal]):
            small_copy(k, chip, me).wait_recv()
        for cp in started:
            cp.wait_send()
        mine.wait()
        mine_small.wait()

    return pl.pallas_call(
        body, name="gather_whole", in_specs=[ANY, ANY], out_specs=[ANY, ANY],
        out_shape=[SDS((2 * N_CHIPS,) + half_block.shape, half_block.dtype),
                   SDS((N_CHIPS,) + small_block.shape, small_block.dtype)],
        scratch_shapes=[pltpu.SemaphoreType.DMA((11,)), pltpu.SemaphoreType.DMA((11,)), pltpu.SemaphoreType.DMA((2,))],
    )(half_block, small_block)


def _pair_send(grads):
    def plan(ins, outs, send, recv, loc):
        (g,), (got,) = ins, outs
        x, y, c = _place()
        copies = [_remote(g.at[j, 1 - c], got.at[j], send(j), recv(j), (x, y, 1 - c)) for j in range(N_CHIPS)]
        return copies, copies, []

    shape = (grads.shape[0],) + grads.shape[2:]
    return _Comm((grads,), (SDS(shape, grads.dtype),), {}, N_CHIPS, 0, plan)


def _chip_exchange(partial):
    def plan(ins, outs, send, recv, loc):
        (p,), (got,) = ins, outs
        x, y, c = _place()
        my_chip = 2 * x + y
        chips = _other_chips(x, y)
        sends = [_remote(p.at[2 * chip[0] + chip[1]], got.at[my_chip], send(j), recv(j), (*chip, c))
                 for j, chip in enumerate(chips)]
        recvs = [_remote(p.at[my_chip], got.at[2 * chip[0] + chip[1]], send(j), recv(j), (*chip, c))
                 for j, chip in enumerate(chips)]
        return sends, recvs, [pltpu.make_async_copy(p.at[my_chip], got.at[my_chip], loc(0))]

    return _Comm((partial,), (SDS(partial.shape, partial.dtype),), {}, 3, 1, plan)


def _pair_sum(name, core, grads, received):
    h = grads.shape[2]

    def body(core_ref, g_ref, r_ref, o_ref):
        o_ref[...] = (g_ref[0] + r_ref[...]).astype(BF16)

    return pl.pallas_call(
        body, name=name,
        grid_spec=pltpu.PrefetchScalarGridSpec(
            num_scalar_prefetch=1, grid=(N_CHIPS,),
            in_specs=[pl.BlockSpec((1, 1, h, D_MODEL), lambda j, core_ref: (j, core_ref[0], 0, 0)),
                      pl.BlockSpec((1, h, D_MODEL), lambda j, core_ref: (j, 0, 0))],
            out_specs=pl.BlockSpec((1, h, D_MODEL), lambda j, core_ref: (j, 0, 0))),
        out_shape=SDS((N_CHIPS, h, D_MODEL), BF16),
        compiler_params=pltpu.CompilerParams(dimension_semantics=("arbitrary",), vmem_limit_bytes=VMEM_LIMIT_V7X),
    )(core, grads, received)


SMALL_ROWS = 8


def _sum_blocks(ref):
    return (ref[0].astype(F32) + ref[1].astype(F32)) + (ref[2].astype(F32) + ref[3].astype(F32))


def _tail_reduce(exchanged, small):
    n = len(exchanged)

    def body(*refs):
        ex, small_ref = refs[:n], refs[n]
        out, small_out = refs[n + 1:2 * n + 1], refs[2 * n + 1]
        s0 = 2 * n + 2
        halves, small_buf = refs[s0:s0 + n], refs[s0 + n]
        share_send, share_recv, small_send, small_recv, local_sems = refs[s0 + n + 1:]
        x, y, c = _place()
        sibling = (x, y, 1 - c)
        me = _slot(x, y, c)

        small_buf[me] = small_ref[...]
        small_copies = []
        for mask in range(1, 8):
            peer = (x ^ (mask >> 2), y ^ ((mask >> 1) & 1), c ^ (mask & 1))
            small_copies.append(_remote(small_ref, small_buf.at[me], small_send.at[mask - 1], small_recv.at[mask - 1], peer))
        for cp in small_copies:
            cp.start()

        def share(k, half_ref, out_ref):
            keep = pltpu.make_async_copy(half_ref, out_ref.at[c], local_sems.at[k])
            give = _remote(half_ref, out_ref.at[c], share_send.at[k], share_recv.at[k], sibling)
            take = _remote(half_ref, out_ref.at[1 - c], share_send.at[k], share_recv.at[k], sibling)
            keep.start()
            give.start()
            return keep, give, take

        shares = []
        for k in range(n):
            halves[k][...] = _sum_blocks(ex[k])
            shares.append(share(k, halves[k], out[k]))

        for cp in small_copies:
            cp.wait_recv()
        total = small_buf[0]
        for d in range(1, 8):
            total = total + small_buf[d]
        small_out[...] = total

        for keep, give, take in shares:
            take.wait_recv()
            give.wait_send()
            keep.wait()
        for cp in small_copies:
            cp.wait_send()

    return pl.pallas_call(
        body, name="tail_reduce",
        in_specs=[VMEM_WHOLE] * (n + 1), out_specs=[ANY] * n + [VMEM_WHOLE],
        out_shape=[SDS((2,) + e.shape[1:], F32) for e in exchanged] + [SDS(small.shape, F32)],
        scratch_shapes=[pltpu.VMEM(e.shape[1:], F32) for e in exchanged] + [pltpu.VMEM((8,) + small.shape, F32)]
                       + [pltpu.SemaphoreType.DMA((n,)), pltpu.SemaphoreType.DMA((n,)),
                          pltpu.SemaphoreType.DMA((7,)), pltpu.SemaphoreType.DMA((7,)),
                          pltpu.SemaphoreType.DMA((n,))],
        compiler_params=pltpu.CompilerParams(vmem_limit_bytes=VMEM_LIMIT_V7X),
    )(*exchanged, small)


def _rope_expansion():
    half = ROT_DIM // 2
    expand = np.zeros((2 * half, 3 * 128), np.float32)
    const = np.zeros((1, 3 * 128), np.float32)
    for lane in range(128):
        d = lane % HEAD_DIM
        if d < ROT_DIM:
            expand[d % half, lane] = 1.0
        else:
            const[0, lane] = 1.0
        if d < half:
            expand[half + d, 128 + lane] = -1.0
        elif d < ROT_DIM:
            expand[half + d - half, 256 + lane] = 1.0
    return expand, const


ROPE_PIECES = 3 * ROT_DIM


def _rope_inputs(seq):
    pos = jnp.arange(seq, dtype=F32)
    inv_freq = ROPE_THETA ** (-jnp.arange(0, ROT_DIM, 2, dtype=F32) / ROT_DIM)
    ang = pos[:, None] * inv_freq[None, :]
    cs = jnp.concatenate([jnp.cos(ang), jnp.sin(ang)], axis=1)
    hi = lax.reduce_precision(cs, 8, 7)
    mid = lax.reduce_precision(cs - hi, 8, 7)
    low = cs - hi - mid
    expand, const = _rope_expansion()
    pieces = jnp.concatenate([hi, mid, low], axis=1).astype(BF16)
    return pieces, jnp.asarray(np.concatenate([expand] * 3, axis=0), BF16), jnp.asarray(const)


def _rope_specs(tb):
    return [pl.BlockSpec((tb, ROPE_PIECES), lambda i: (i, 0)), _resident((ROPE_PIECES, 3 * 128)), _resident((1, 3 * 128))]


def _rope_tile(pieces_ref, expand_ref, const_ref):
    tables = _dot(pieces_ref[...], expand_ref[...]) + const_ref[...]
    return tables[:, 0:128], tables[:, 128:256], tables[:, 256:384]


def _rope(t, c, sa, sb):
    half = ROT_DIM // 2
    return t * c + pltpu.roll(t, 128 - half, 1) * sa + pltpu.roll(t, half, 1) * sb


def _rope_transposed(dt, c, sa, sb):
    half = ROT_DIM // 2
    return dt * c + pltpu.roll(dt * sa, half, 1) + pltpu.roll(dt * sb, 128 - half, 1)


def _cast_halves(core, w_up, w_down, w_out, w_in_t):
    def body(core_ref, up_ref, down_ref, out_ref, in_ref, up_o, down_o, out_o, in_o):
        up_o[...] = up_ref[...].astype(BF16)
        down_o[...] = down_ref[...].astype(BF16)
        out_o[...] = out_ref[...].astype(BF16)
        in_o[...] = in_ref[...].astype(BF16)

    half = lambda rows: pl.BlockSpec((rows, D_MODEL), lambda i, core_ref: (core_ref[0], 0))
    whole = lambda rows: pl.BlockSpec((rows, D_MODEL), lambda i, core_ref: (0, 0))
    rows = (H_UP, H_DOWN, H_OUT, H_IN)
    return pl.pallas_call(
        body, name="cast_halves",
        grid_spec=pltpu.PrefetchScalarGridSpec(
            num_scalar_prefetch=1, grid=(1,), in_specs=[half(r) for r in rows], out_specs=[whole(r) for r in rows]),
        out_shape=[SDS((r, D_MODEL), BF16) for r in rows],
        compiler_params=pltpu.CompilerParams(dimension_semantics=("arbitrary",), vmem_limit_bytes=VMEM_LIMIT_V7X),
    )(core, w_up, w_down, w_out, w_in_t)


def _in_proj(x, g_pre, w_in_t, rope, comm=None):
    seq = x.shape[0]
    tb = min(seq, WIDE_TOKEN_TILE)

    def body(x_ref, g_ref, w_ref, c_ref, sa_ref, sb_ref,
             q_ref, kd0_ref, kd1_ref, vd0_ref, vd1_ref, gb_ref, gc_ref, xin_ref, hn_ref):
        xv = x_ref[...]
        hn = (xv * _rms(xv) * g_ref[...]).astype(BF16)
        hn_ref[...] = hn
        proj = _dot_nt(hn, w_ref[...].reshape(IN_COLS, D_MODEL))
        c, sa, sb = _rope_tile(c_ref, sa_ref, sb_ref)
        scale = 1.0 / math.sqrt(HEAD_DIM)
        for p in range(Q_WIDTH // 128):
            q_ref[:, 128 * p:128 * (p + 1)] = (_rope(proj[:, 128 * p:128 * (p + 1)], c, sa, sb) * scale).astype(BF16)
        k = _rope(proj[:, Q_WIDTH:Q_WIDTH + KV_WIDTH], c, sa, sb)
        v = proj[:, Q_WIDTH + KV_WIDTH:Q_WIDTH + 2 * KV_WIDTH]
        low = _lane_lt64(k.shape)
        k_sw, v_sw = pltpu.roll(k, HEAD_DIM, 1), pltpu.roll(v, HEAD_DIM, 1)
        kd0_ref[...] = jnp.where(low, k, k_sw).astype(BF16)
        kd1_ref[...] = jnp.where(low, k_sw, k).astype(BF16)
        vd0_ref[...] = jnp.where(low, v, v_sw).astype(BF16)
        vd1_ref[...] = jnp.where(low, v_sw, v).astype(BF16)
        base = Q_WIDTH + 2 * KV_WIDTH
        gb_ref[...] = proj[:, base:base + CONV_WIDTH].astype(BF16)
        gc_ref[...] = proj[:, base + CONV_WIDTH:base + 2 * CONV_WIDTH].astype(BF16)
        xin_ref[...] = proj[:, base + 2 * CONV_WIDTH:base + 3 * CONV_WIDTH].astype(BF16)

    tile = lambda w: pl.BlockSpec((tb, w), lambda i: (i, 0))
    return _pallas(
        body, name="in_proj", grid=(seq // tb,),
        in_specs=[tile(D_MODEL), _resident((1, D_MODEL)), _resident(w_in_t.shape), *_rope_specs(tb)],
        out_specs=[tile(Q_WIDTH), tile(128), tile(128), tile(128), tile(128),
                   tile(CONV_WIDTH), tile(CONV_WIDTH), tile(CONV_WIDTH), tile(D_MODEL)],
        out_shape=[SDS((seq, Q_WIDTH), BF16)] + [SDS((seq, 128), BF16)] * 4
                  + [SDS((seq, CONV_WIDTH), BF16)] * 3 + [SDS((seq, D_MODEL), BF16)],
        operands=(x, g_pre, w_in_t, *rope), comm=comm)


def _attn_valid(i):
    shape = (4 * QBLOCK, 2 * QBLOCK)
    row = lax.broadcasted_iota(jnp.int32, shape, 0)
    col = lax.broadcasted_iota(jnp.int32, shape, 1)
    qi = row & (QBLOCK - 1)
    return (col > qi) & (col <= qi + QBLOCK) & ((col >= QBLOCK) | (i > 0))


def _stack_heads(pair0, pair1):
    low = _lane_lt64(pair0.shape)
    zero = jnp.zeros_like(pair0)
    return jnp.concatenate([jnp.where(low, pair0, zero), jnp.where(low, zero, pair0),
                            jnp.where(low, pair1, zero), jnp.where(low, zero, pair1)], axis=0)


def _unstack_heads(stacked):
    low = _lane_lt64((QBLOCK, 128))
    pair0 = jnp.where(low, stacked[0:QBLOCK], stacked[QBLOCK:2 * QBLOCK])
    pair1 = jnp.where(low, stacked[2 * QBLOCK:3 * QBLOCK], stacked[3 * QBLOCK:4 * QBLOCK])
    return pair0, pair1


def _sink_column(sink_ref, kv_head):
    row = lax.broadcasted_iota(jnp.int32, (4 * QBLOCK, 1), 0)
    s = [sink_ref[0, 4 * kv_head + j] for j in range(4)]
    return jnp.where(row < QBLOCK, s[0], jnp.where(row < 2 * QBLOCK, s[1], jnp.where(row < 3 * QBLOCK, s[2], s[3])))


def _band(ref, i):
    prev = pl.multiple_of(jnp.maximum(i - 1, 0) * QBLOCK, QBLOCK)
    own = pl.multiple_of(i * QBLOCK, QBLOCK)
    return jnp.concatenate([ref[pl.ds(prev, QBLOCK), :], ref[pl.ds(own, QBLOCK), :]], axis=0), prev, own


def _softmax_with_sink(s, sink_col):
    m = jnp.maximum(jnp.max(s, axis=-1, keepdims=True), sink_col)
    p = jnp.exp(s - m)
    e_sink = jnp.exp(sink_col - m)
    inv_l = 1.0 / (jnp.sum(p, axis=-1, keepdims=True) + e_sink)
    return p, e_sink, inv_l


def _attention_fwd(q, kd0, kd1, vd0, vd1, sinks, comm=None):
    seq = q.shape[0]

    nb = ATTN_FWD_BLOCKS

    def body(sink_ref, q_ref, kd0_ref, kd1_ref, vd0_ref, vd1_ref, o_ref):
        for b in range(nb):
            i = pl.program_id(0) * nb + b
            rows = slice(QBLOCK * b, QBLOCK * (b + 1))
            valid = _attn_valid(i)
            for kv_head, (k_ref, v_ref) in enumerate(((kd0_ref, vd0_ref), (kd1_ref, vd1_ref))):
                kband, _, _ = _band(k_ref, i)
                vband, _, _ = _band(v_ref, i)
                base = 256 * kv_head
                qm = _stack_heads(q_ref[rows, base:base + 128], q_ref[rows, base + 128:base + 256])
                s = jnp.where(valid, _dot_nt(qm, kband), NEG_INF)
                p, _, inv_l = _softmax_with_sink(s, _sink_column(sink_ref, kv_head))
                o = _dot(p.astype(BF16), vband) * inv_l
                pair0, pair1 = _unstack_heads(o)
                o_ref[rows, base:base + 128] = pair0.astype(BF16)
                o_ref[rows, base + 128:base + 256] = pair1.astype(BF16)

    blk = pl.BlockSpec((nb * QBLOCK, Q_WIDTH), lambda i: (i, 0))
    full = _resident((seq, 128))
    return _pallas(
        body, name="attention_fwd", grid=(seq // (nb * QBLOCK),),
        in_specs=[pl.BlockSpec(memory_space=pltpu.SMEM), blk, full, full, full, full],
        out_specs=[blk], out_shape=[SDS((seq, Q_WIDTH), BF16)],
        operands=(sinks, q, kd0, kd1, vd0, vd1), comm=comm)


HALO = 16


def _conv_parts(gc, xin, gc_halo, xin_halo, conv_w, first):
    tb = gc.shape[0]
    u = gc.astype(F32) * xin.astype(F32)
    u_halo = jnp.where(first, 0.0, gc_halo.astype(F32) * xin_halo.astype(F32))
    ext = jnp.concatenate([u_halo, u], axis=0)
    u1 = pltpu.roll(ext, 1, 0)[HALO:HALO + tb]
    u2 = pltpu.roll(ext, 2, 0)[HALO:HALO + tb]
    y = conv_w[0:1, :] * u2 + conv_w[1:2, :] * u1 + conv_w[2:3, :] * u
    return u, u1, u2, y


def _halo_prev(tb, w):
    return pl.BlockSpec((HALO, w), lambda i: (jnp.maximum(i * (tb // HALO) - 1, 0), 0))


def _residual_mid(x, mix, g_post_mix):
    mix_f = mix.astype(F32)
    return x + mix_f * _rms(mix_f) * g_post_mix


def _mix_out(attn, gb, gc, xin, conv_w, g_attn, g_conv, w_out, comm=None):
    seq = attn.shape[0]
    tb = min(seq, WIDE_TOKEN_TILE)

    def body(a_ref, gb_ref, gc_ref, xin_ref, gch_ref, xinh_ref, cw_ref, ga_ref, gcn_ref, w_ref, mix_ref, mixed_ref):
        first = pl.program_id(0) == 0
        _, _, _, y = _conv_parts(gc_ref[...], xin_ref[...], gch_ref[...], xinh_ref[...], cw_ref[...], first)
        conv = gb_ref[...].astype(F32) * y
        a = a_ref[...].astype(F32)
        mixed_ref[:, 0:Q_WIDTH] = (a * _rms(a) * ga_ref[...]).astype(BF16)
        mixed_ref[:, Q_WIDTH:] = (conv * _rms(conv) * gcn_ref[...]).astype(BF16)
        mix_ref[...] = _dot(mixed_ref[...], w_ref[...].reshape(D_MODEL, D_MODEL)).astype(BF16)

    tile = lambda w: pl.BlockSpec((tb, w), lambda i: (i, 0))
    return _pallas(
        body, name="mix_out", grid=(seq // tb,),
        in_specs=[tile(Q_WIDTH), tile(CONV_WIDTH), tile(CONV_WIDTH), tile(CONV_WIDTH),
                  _halo_prev(tb, CONV_WIDTH), _halo_prev(tb, CONV_WIDTH),
                  _resident((CONV_K, CONV_WIDTH)), _resident((1, Q_WIDTH)), _resident((1, CONV_WIDTH)),
                  _resident(w_out.shape)],
        out_specs=[tile(D_MODEL), tile(D_MODEL)],
        out_shape=[SDS((seq, D_MODEL), BF16), SDS((seq, D_MODEL), BF16)],
        operands=(attn, gb, gc, xin, gc, xin, conv_w, g_attn, g_conv, w_out), comm=comm)


def _mlp_loss(x, mix, target, g_post_mix, g_pre_mlp, g_post_mlp, w_up, w_down):
    seq = x.shape[0]
    tb = TOKEN_TILE

    def body(x_ref, mix_ref, t_ref, gpm_ref, g2_ref, g4_ref, wup_ref, wdown_ref,
             up_ref, hn2_ref, dout_ref, dmlp_ref, loss_ref, dg4_ref, act_ref):
        @pl.when(pl.program_id(0) == 0)
        def _():
            loss_ref[...] = jnp.zeros_like(loss_ref)
            dg4_ref[...] = jnp.zeros_like(dg4_ref)

        halves = [slice(0, tb // 2), slice(tb // 2, tb)]
        hv, hn2 = [], []
        for rows in halves:
            hv.append(_residual_mid(x_ref[rows, :], mix_ref[rows, :], gpm_ref[...]))
            hn2.append((hv[-1] * _rms(hv[-1]) * g2_ref[...]).astype(BF16))
            hn2_ref[rows, :] = hn2[-1]
        for k, rows in enumerate(halves):
            for j in range(N_CHIPS):
                up = _dot(hn2[k], _chip_block(wup_ref, j))
                up = jnp.maximum(up, 0.0)
                up_ref[rows, 1024 * j:1024 * (j + 1)] = up.astype(BF16)
                act_ref[rows, 1024 * j:1024 * (j + 1)] = (up * up).astype(BF16)
        w_down_all = wdown_ref[...].reshape(D_FF, D_MODEL)
        loss = jnp.zeros((1, 1), F32)
        dg4 = jnp.zeros((1, D_MODEL), F32)
        for k, rows in enumerate(halves):
            mlp = _dot(act_ref[rows, :], w_down_all)
            rstd = _rms(mlp)
            zhat = mlp * rstd
            diff = hv[k] + zhat * g4_ref[...] - t_ref[rows, :]
            loss = loss + jnp.sum(jnp.sum(diff * diff, axis=1, keepdims=True), axis=0, keepdims=True)
            dout = diff * (1.0 / D_MODEL)
            dout_ref[rows, :] = dout
            dg4 = dg4 + _colsum(dout * zhat)
            dmlp_ref[rows, :] = _norm_bwd(dout, g4_ref[...], zhat, rstd).astype(BF16)
        loss_ref[...] += loss
        dg4_ref[...] += dg4

    tile = lambda w: pl.BlockSpec((tb, w), lambda i: (i, 0))
    return _pallas(
        body, name="mlp_loss", grid=(seq // tb,),
        in_specs=[tile(D_MODEL), tile(D_MODEL), tile(D_MODEL), _resident((1, D_MODEL)), _resident((1, D_MODEL)),
                  _resident((1, D_MODEL)), _resident(w_up.shape), _resident(w_down.shape)],
        out_specs=[tile(D_FF), tile(D_MODEL), tile(D_MODEL), tile(D_MODEL),
                   pl.BlockSpec((1, 1), lambda i: (0, 0)), pl.BlockSpec((1, D_MODEL), lambda i: (0, 0))],
        out_shape=[SDS((seq, D_FF), BF16), SDS((seq, D_MODEL), BF16), SDS((seq, D_MODEL), F32),
                   SDS((seq, D_MODEL), BF16), SDS((1, 1), F32), SDS((1, D_MODEL), F32)],
        scratch=[pltpu.VMEM((tb, D_FF), BF16)],
        operands=(x, mix, target, g_post_mix, g_pre_mlp, g_post_mlp, w_up, w_down))


def _mlp_bwd(dmlp, up, x, dout, mix, g_pre_mlp, g_post_mix, w_up, w_down):
    seq = x.shape[0]
    tb = MLP_BWD_TOKEN_TILE

    def body(dmlp_ref, up_ref, x_ref, dout_ref, mix_ref, g2_ref, gpm_ref, wup_ref, wdown_ref,
             dup_ref, dh_ref, dmix_ref, dg2_ref, dgpm_ref):
        @pl.when(pl.program_id(0) == 0)
        def _():
            dg2_ref[...] = jnp.zeros_like(dg2_ref)
            dgpm_ref[...] = jnp.zeros_like(dgpm_ref)

        subs = [slice(k * MLP_BWD_SUB_TILE, (k + 1) * MLP_BWD_SUB_TILE) for k in range(tb // MLP_BWD_SUB_TILE)]
        dhn2 = []
        for rows in subs:
            dmlp_v = dmlp_ref[rows, :]
            acc = None
            for j in range(N_CHIPS):
                cols = slice(1024 * j, 1024 * (j + 1))
                dact = _dot_nt(dmlp_v, _chip_block(wdown_ref, j))
                dup = (dact * (2.0 * up_ref[rows, cols].astype(F32))).astype(BF16)
                dup_ref[rows, cols] = dup
                part = _dot_nt(dup, _chip_block(wup_ref, j))
                acc = part if acc is None else acc + part
            dhn2.append(acc)
        dg2 = jnp.zeros((1, D_MODEL), F32)
        dgpm = jnp.zeros((1, D_MODEL), F32)
        for k, rows in enumerate(subs):
            mix_v = mix_ref[rows, :].astype(F32)
            hv = _residual_mid(x_ref[rows, :], mix_ref[rows, :], gpm_ref[...])
            r2 = _rms(hv)
            hhat = hv * r2
            dg2 = dg2 + _colsum(dhn2[k] * hhat)
            dh = dout_ref[rows, :] + _norm_bwd(dhn2[k], g2_ref[...], hhat, r2)
            dh_ref[rows, :] = dh.astype(BF16)
            rz = _rms(mix_v)
            zhat = mix_v * rz
            dgpm = dgpm + _colsum(dh * zhat)
            dmix_ref[rows, :] = _norm_bwd(dh, gpm_ref[...], zhat, rz).astype(BF16)
        dg2_ref[...] += dg2
        dgpm_ref[...] += dgpm

    tile = lambda w: pl.BlockSpec((tb, w), lambda i: (i, 0))
    vec = pl.BlockSpec((1, D_MODEL), lambda i: (0, 0))
    return _pallas(
        body, name="mlp_bwd", grid=(seq // tb,),
        in_specs=[tile(D_MODEL), tile(D_FF), tile(D_MODEL), tile(D_MODEL), tile(D_MODEL),
                  _resident((1, D_MODEL)), _resident((1, D_MODEL)), _resident(w_up.shape), _resident(w_down.shape)],
        out_specs=[tile(D_FF), tile(D_MODEL), tile(D_MODEL), vec, vec],
        out_shape=[SDS((seq, D_FF), BF16), SDS((seq, D_MODEL), BF16), SDS((seq, D_MODEL), BF16),
                   SDS((1, D_MODEL), F32), SDS((1, D_MODEL), F32)],
        operands=(dmlp, up, x, dout, mix, g_pre_mlp, g_post_mix, w_up, w_down))


def _mix_bwd(dmix, attn, gb, gc, xin, conv_w, g_attn, g_conv, w_out, n_k):
    seq = attn.shape[0]
    tb = seq // (N_CHIPS * n_k)

    def body(first, dmix_ref, a_ref, gb_ref, gc_ref, xin_ref, gch_ref, xinh_ref, cw_ref, ga_ref, gcn_ref, w_ref,
             dattn_ref, dgb_ref, dy_ref, dga_ref, dgcn_ref, dcw_ref):
        @pl.when(first)
        def _():
            dga_ref[...] = jnp.zeros_like(dga_ref)
            dgcn_ref[...] = jnp.zeros_like(dgcn_ref)
            dcw_ref[...] = jnp.zeros_like(dcw_ref)

        dmixed = _dot_nt(dmix_ref[...], w_ref[...].reshape(D_MODEL, D_MODEL))
        a = a_ref[...].astype(F32)
        ra = _rms(a)
        ahat = a * ra
        dan = dmixed[:, 0:Q_WIDTH]
        dga_ref[...] += _colsum(dan * ahat)
        dattn_ref[...] = _norm_bwd(dan, ga_ref[...], ahat, ra).astype(BF16)
        gbv = gb_ref[...].astype(F32)
        u, u1, u2, y = _conv_parts(gc_ref[...], xin_ref[...], gch_ref[...], xinh_ref[...], cw_ref[...], first)
        conv = gbv * y
        rc = _rms(conv)
        chat = conv * rc
        dcn = dmixed[:, Q_WIDTH:]
        dgcn_ref[...] += _colsum(dcn * chat)
        dconv = _norm_bwd(dcn, gcn_ref[...], chat, rc)
        dgb_ref[...] = (dconv * y).astype(BF16)
        dy = dconv * gbv
        dy_ref[...] = dy.astype(BF16)
        dcw_ref[0:1, :] += _colsum(dy * u2)
        dcw_ref[1:2, :] += _colsum(dy * u1)
        dcw_ref[2:3, :] += _colsum(dy * u)

    tile = lambda w: pl.BlockSpec((tb, w), lambda j, k: (j * n_k + k, 0))
    halo = lambda w: pl.BlockSpec((HALO, w), lambda j, k: (jnp.maximum((j * n_k + k) * (tb // HALO) - 1, 0), 0))
    whole = lambda shape: pl.BlockSpec(shape, lambda j, k: (0,) * len(shape))
    return _Rider(
        body,
        in_specs=[tile(D_MODEL), tile(Q_WIDTH), tile(CONV_WIDTH), tile(CONV_WIDTH), tile(CONV_WIDTH),
                  halo(CONV_WIDTH), halo(CONV_WIDTH),
                  _resident((CONV_K, CONV_WIDTH)), _resident((1, Q_WIDTH)), _resident((1, CONV_WIDTH)),
                  _resident(w_out.shape)],
        out_specs=[tile(Q_WIDTH), tile(CONV_WIDTH), tile(CONV_WIDTH),
                   whole((1, Q_WIDTH)), whole((1, CONV_WIDTH)), whole((CONV_K, CONV_WIDTH))],
        out_shape=[SDS((seq, Q_WIDTH), BF16), SDS((seq, CONV_WIDTH), BF16), SDS((seq, CONV_WIDTH), BF16),
                   SDS((1, Q_WIDTH), F32), SDS((1, CONV_WIDTH), F32), SDS((CONV_K, CONV_WIDTH), F32)],
        operands=(dmix, attn, gb, gc, xin, gc, xin, conv_w, g_attn, g_conv, w_out))


def _attention_bwd(q, dattn, attn, kd0, kd1, vd0, vd1, sinks, comm=None):
    seq = q.shape[0]
    nb = ATTN_BWD_BLOCKS

    def body(sink_ref, q_ref, do_ref, o_ref, kd0_ref, kd1_ref, vd0_ref, vd1_ref,
             dq_ref, dk0_ref, dk1_ref, dv0_ref, dv1_ref, dsink_ref):
        @pl.when(pl.program_id(0) == 0)
        def _():
            for r in (dk0_ref, dk1_ref, dv0_ref, dv1_ref, dsink_ref):
                r[...] = jnp.zeros_like(r)

        lane = lax.broadcasted_iota(jnp.int32, (1, 128), 1)
        dsink = jnp.zeros((1, 128), F32)
        for b in range(nb):
            i = pl.program_id(0) * nb + b
            rows = slice(QBLOCK * b, QBLOCK * (b + 1))
            valid = _attn_valid(i)
            for kv_head, (k_ref, v_ref, dk_ref, dv_ref) in enumerate(
                    ((kd0_ref, vd0_ref, dk0_ref, dv0_ref), (kd1_ref, vd1_ref, dk1_ref, dv1_ref))):
                kband, prev, own = _band(k_ref, i)
                vband, _, _ = _band(v_ref, i)
                base = 256 * kv_head
                qm = _stack_heads(q_ref[rows, base:base + 128], q_ref[rows, base + 128:base + 256])
                dom = _stack_heads(do_ref[rows, base:base + 128], do_ref[rows, base + 128:base + 256])
                om = _stack_heads(o_ref[rows, base:base + 128], o_ref[rows, base + 128:base + 256])
                s = jnp.where(valid, _dot_nt(qm, kband), NEG_INF)
                p, e_sink, inv_l = _softmax_with_sink(s, _sink_column(sink_ref, kv_head))
                p = p * inv_l
                delta = jnp.sum(dom.astype(F32) * om.astype(F32), axis=-1, keepdims=True)
                ds = (p * (_dot_nt(dom, vband) - delta)).astype(BF16)
                sink_term = -(e_sink * inv_l) * delta
                for j in range(4):
                    part = jnp.sum(sink_term[QBLOCK * j:QBLOCK * (j + 1)], axis=0, keepdims=True)
                    dsink = dsink + jnp.where(lane == 4 * kv_head + j, part, 0.0)
                pair0, pair1 = _unstack_heads(_dot(ds, kband))
                dq_ref[rows, base:base + 128] = pair0.astype(BF16)
                dq_ref[rows, base + 128:base + 256] = pair1.astype(BF16)
                dkd = _dot_tn(ds, qm)
                dkd = dkd + pltpu.roll(dkd, HEAD_DIM, 1)
                dvd = _dot_tn(p.astype(BF16), dom)
                dvd = dvd + pltpu.roll(dvd, HEAD_DIM, 1)
                dk_ref[pl.ds(prev, QBLOCK), :] += dkd[0:QBLOCK]
                dk_ref[pl.ds(own, QBLOCK), :] += dkd[QBLOCK:]
                dv_ref[pl.ds(prev, QBLOCK), :] += dvd[0:QBLOCK]
                dv_ref[pl.ds(own, QBLOCK), :] += dvd[QBLOCK:]
        dsink_ref[...] += dsink

    blk = pl.BlockSpec((nb * QBLOCK, Q_WIDTH), lambda i: (i, 0))
    full = _resident((seq, 128))
    acc = pl.BlockSpec((seq, 128), lambda i: (0, 0))
    return _pallas(
        body, name="attention_bwd", grid=(seq // (nb * QBLOCK),),
        in_specs=[pl.BlockSpec(memory_space=pltpu.SMEM), blk, blk, blk, full, full, full, full],
        out_specs=[blk, acc, acc, acc, acc, pl.BlockSpec((1, 128), lambda i: (0, 0))],
        out_shape=[SDS((seq, Q_WIDTH), BF16)] + [SDS((seq, 128), F32)] * 4 + [SDS((1, 128), F32)],
        operands=(sinks, q, dattn, attn, kd0, kd1, vd0, vd1), comm=comm)


def _dproj_wgrad_in(dq, dk0, dk1, dv0, dv1, dgb, dy, gc, xin, conv_w, hn, rope, comm=None):
    seq = hn.shape[0]
    tb = min(seq, WIDE_TOKEN_TILE)
    n_tiles = seq // tb
    half_cols = IN_COLS // 2

    def body(dq_ref, dk0_ref, dk1_ref, dv0_ref, dv1_ref, dgb_ref, dy_ref, dyh_ref, gc_ref, xin_ref, cw_ref,
             hn_ref, c_ref, sa_ref, sb_ref, dproj_ref, g_ref):
        i = pl.program_id(0)

        @pl.when(i == 0)
        def _():
            g_ref[...] = jnp.zeros_like(g_ref)

        dy = dy_ref[...].astype(F32)
        ext = jnp.concatenate([dy, jnp.where(i == n_tiles - 1, 0.0, dyh_ref[...].astype(F32))], axis=0)
        dy1 = pltpu.roll(ext, tb + HALO - 1, 0)[0:tb]
        dy2 = pltpu.roll(ext, tb + HALO - 2, 0)[0:tb]
        cw = cw_ref[...]
        du = cw[2:3, :] * dy + cw[1:2, :] * dy1 + cw[0:1, :] * dy2
        scale = 1.0 / math.sqrt(HEAD_DIM)
        base = Q_WIDTH + 2 * KV_WIDTH
        halves = [slice(0, tb // 2), slice(tb // 2, tb)]
        low = _lane_lt64((tb // 2, 128))
        for rows in halves:
            c, sa, sb = _rope_tile(c_ref.at[rows, :], sa_ref, sb_ref)
            for p in range(Q_WIDTH // 128):
                dproj_ref[rows, 128 * p:128 * (p + 1)] = _rope_transposed(
                    dq_ref[rows, 128 * p:128 * (p + 1)].astype(F32) * scale, c, sa, sb).astype(BF16)
            dk = jnp.where(low, dk0_ref[rows, :], dk1_ref[rows, :])
            dproj_ref[rows, Q_WIDTH:Q_WIDTH + KV_WIDTH] = _rope_transposed(dk, c, sa, sb).astype(BF16)
            dproj_ref[rows, Q_WIDTH + KV_WIDTH:base] = jnp.where(low, dv0_ref[rows, :], dv1_ref[rows, :]).astype(BF16)
            dproj_ref[rows, base:base + CONV_WIDTH] = dgb_ref[rows, :]
            dproj_ref[rows, base + CONV_WIDTH:base + 2 * CONV_WIDTH] = (du[rows] * xin_ref[rows, :].astype(F32)).astype(BF16)
            dproj_ref[rows, base + 2 * CONV_WIDTH:] = (du[rows] * gc_ref[rows, :].astype(F32)).astype(BF16)
        for k in range(2):
            cols = slice(half_cols * k, half_cols * (k + 1))
            g_ref[2 * k:2 * k + 2] += _dot_tn(dproj_ref[:, cols], hn_ref[...]).reshape(2, 2, H_IN, D_MODEL)

    tile = lambda w: pl.BlockSpec((tb, w), lambda i: (i, 0))
    halo_next = pl.BlockSpec((HALO, CONV_WIDTH), lambda i: (jnp.minimum((i + 1) * (tb // HALO), seq // HALO - 1), 0))
    return _pallas(
        body, name="dproj_wgrad_in", grid=(n_tiles,),
        in_specs=[tile(Q_WIDTH), tile(128), tile(128), tile(128), tile(128), tile(CONV_WIDTH), tile(CONV_WIDTH), halo_next,
                  tile(CONV_WIDTH), tile(CONV_WIDTH), _resident((CONV_K, CONV_WIDTH)), tile(D_MODEL), *_rope_specs(tb)],
        out_specs=[tile(IN_COLS), pl.BlockSpec((N_CHIPS, 2, H_IN, D_MODEL), lambda i: (0, 0, 0, 0),
                                               pipeline_mode=pl.Buffered(1))],
        out_shape=[SDS((seq, IN_COLS), BF16), SDS((N_CHIPS, 2, H_IN, D_MODEL), F32)],
        operands=(dq, dk0, dk1, dv0, dv1, dgb, dy, dy, gc, xin, conv_w, hn, *rope), comm=comm)


def _in_proj_dx(name, dproj, x, dh, g_pre, w_in_t, first_tile, n_tiles, so_far=None, comm=None):
    seq = x.shape[0]
    tb = min(seq, WIDE_TOKEN_TILE)

    def body(dproj_ref, x_ref, dh_ref, g_ref, w_ref, *rest):
        gx_ref, dg_ref = rest[-2:]

        @pl.when(pl.program_id(0) == 0)
        def _():
            dg_ref[...] = jnp.zeros_like(dg_ref)

        halves = [slice(0, tb // 2), slice(tb // 2, tb)]
        w_all = w_ref[...].reshape(IN_COLS, D_MODEL)
        dhn = [_dot(dproj_ref[rows, :], w_all) for rows in halves]
        dg = jnp.zeros((1, D_MODEL), F32)
        for k, rows in enumerate(halves):
            xv = x_ref[rows, :]
            r = _rms(xv)
            xhat = xv * r
            dg = dg + _colsum(dhn[k] * xhat)
            gx_ref[rows, :] = dh_ref[rows, :].astype(F32) + _norm_bwd(dhn[k], g_ref[...], xhat, r)
        dg_ref[...] += dg

    tile = lambda w: pl.BlockSpec((tb, w), lambda i: (i + first_tile, 0))
    began = [] if so_far is None else [so_far]
    return _pallas(
        body, name=name, grid=(n_tiles,),
        in_specs=[tile(IN_COLS), tile(D_MODEL), tile(D_MODEL), _resident((1, D_MODEL)), _resident(w_in_t.shape)] + [ANY] * len(began),
        out_specs=[tile(D_MODEL), pl.BlockSpec((1, D_MODEL), lambda i: (0, 0))],
        out_shape=[SDS((seq, D_MODEL), F32), SDS((1, D_MODEL), F32)],
        operands=(dproj, x, dh, g_pre, w_in_t, *began), comm=comm, aliases={5: 0} if began else None)


def _wgrad_grid(seq, per_chip, h_rows, with_rider=False):
    chips_per_step = 1 if per_chip else N_CHIPS
    m = chips_per_step * 2 * h_rows
    bt = min(seq, WGRAD_TOKEN_TILE if per_chip and not with_rider else WGRAD_TOKEN_TILE // 2)
    return chips_per_step, m, bt, seq // bt


def _wgrad(name, a, b, *, per_chip, h_rows, square_a=False, comm=None, rider=None):
    seq = a.shape[0]
    chips_per_step, m, bt, n_k = _wgrad_grid(seq, per_chip, h_rows, rider is not None)
    a_cols = m if per_chip else a.shape[1]
    a_wide = a.shape[1] > a_cols
    b_wide = b.shape[1] > D_MODEL

    def body(a_ref, b_ref, g_ref):
        @pl.when(pl.program_id(1) == 0)
        def _():
            g_ref[...] = jnp.zeros_like(g_ref)

        av = a_ref[...]
        if square_a:
            av = (av.astype(F32) * av.astype(F32)).astype(BF16)
        g_ref[...] += _dot_tn(av, b_ref[...]).reshape(g_ref.shape)

    a_spec = pl.BlockSpec((bt, a_cols), (lambda j, k: (k, j)) if a_wide else (lambda j, k: (k, 0)))
    b_spec = pl.BlockSpec((bt, D_MODEL), (lambda j, k: (k, j)) if b_wide else (lambda j, k: (k, 0)))
    g_spec = pl.BlockSpec((chips_per_step, 2, h_rows, D_MODEL), lambda j, k: (j, 0, 0, 0),
                          pipeline_mode=None if per_chip else pl.Buffered(1))
    return _pallas(
        body, name=name, grid=(N_CHIPS if per_chip else 1, n_k),
        in_specs=[a_spec, b_spec], out_specs=[g_spec], out_shape=[SDS((N_CHIPS, 2, h_rows, D_MODEL), F32)],
        operands=(a, b), comm=comm, rider=rider)


def _adamw_math(w, g, m, v):
    m = ADAM_B1 * m + (1.0 - ADAM_B1) * g
    v = ADAM_B2 * v + (1.0 - ADAM_B2) * (g * g)
    m_hat = m / (1.0 - ADAM_B1 ** ADAM_STEP)
    v_hat = v / (1.0 - ADAM_B2 ** ADAM_STEP)
    delta = -ADAM_LR * (m_hat / (jnp.sqrt(v_hat) + ADAM_EPS) + ADAM_WD * w)
    return delta, m, v


def _adamw_rows(name, reduced, w, m, v, rt):
    per_half = reduced.shape[1] // rt

    def body(r_ref, w_ref, m_ref, v_ref, g_out, d_out, m_out, v_out):
        g = r_ref[0]
        g_out[...] = g
        d_out[...], m_out[...], v_out[...] = _adamw_math(w_ref[...], g, m_ref[...], v_ref[...])

    blk = pl.BlockSpec((rt, D_MODEL), lambda h, r: (h * per_half + r, 0))
    return _pallas(
        body, name=name, grid=(2, per_half),
        in_specs=[pl.BlockSpec((1, rt, D_MODEL), lambda h, r: (h, r, 0)), blk, blk, blk],
        out_specs=[blk, blk, blk, blk], out_shape=[SDS(w.shape, F32)] * 4, operands=(reduced, w, m, v))


def _adamw_small(packed_grads, w, m, v):
    names = SMALL_NAMES
    n = len(names)
    conv_local = w["conv_w"].shape[-1]

    def body(*refs):
        gp = refs[0]
        w_refs, m_refs, v_refs = refs[1:1 + n], refs[1 + n:1 + 2 * n], refs[1 + 2 * n:1 + 3 * n]
        outs = refs[1 + 3 * n:]
        g_out, d_out, m_out, v_out = outs[0:n], outs[n:2 * n], outs[2 * n:3 * n], outs[3 * n:4 * n]
        chip = 2 * lax.axis_index("x") + lax.axis_index("y")

        def step(k, g, index=None):
            pick = (lambda r: r[...]) if index is None else (lambda r: r[index])
            d, new_m, new_v = _adamw_math(pick(w_refs[k]), g, pick(m_refs[k]), pick(v_refs[k]))
            for ref, val in ((g_out[k], g), (d_out[k], d), (m_out[k], new_m), (v_out[k], new_v)):
                if index is None:
                    ref[...] = val
                else:
                    ref[index] = val

        for k, name in enumerate(names):
            if name in SMALL_VECTORS:
                step(k, gp[SMALL_VECTORS.index(name):SMALL_VECTORS.index(name) + 1, :])
            elif name == "attn_group_norm":
                step(k, gp[4:5, 0:Q_WIDTH])
            elif name == "conv_group_norm":
                step(k, gp[4:5, Q_WIDTH:])
            elif name == "attn_sinks":
                step(k, gp[7:8, 0:8])
            else:
                for t in range(CONV_K):
                    row, base = 5 + t // 2, CONV_WIDTH * (t % 2)
                    g = gp[row:row + 1, base:base + conv_local]
                    for j in range(1, CONV_WIDTH // conv_local):
                        g = jnp.where(chip == j, gp[row:row + 1, base + conv_local * j:base + conv_local * (j + 1)], g)
                    step(k, g, index=(0, slice(t, t + 1), slice(None)))

    shapes = [SDS(w[name].shape, F32) for name in names]
    res = pl.pallas_call(
        body, name="adamw_small", in_specs=[VMEM_WHOLE] * (1 + 3 * n), out_specs=[VMEM_WHOLE] * (4 * n),
        out_shape=shapes * 4,
    )(packed_grads, *[w[k] for k in names], *[m[k] for k in names], *[v[k] for k in names])
    return [dict(zip(names, res[i * n:(i + 1) * n])) for i in range(4)]


SMALL_VECTORS = ("pre_mix_norm", "post_mix_norm", "pre_mlp_norm", "post_mlp_norm")
SMALL_NAMES = SMALL_VECTORS + ("attn_group_norm", "conv_group_norm", "conv_w", "attn_sinks")


def _pack_small(p):
    rows = [p[n].reshape(1, D_MODEL) for n in SMALL_VECTORS]
    rows.append(jnp.concatenate([p["attn_group_norm"].reshape(1, -1), p["conv_group_norm"].reshape(1, -1)], axis=1))
    cw = p["conv_w"].reshape(CONV_K, -1)
    rows.append(jnp.pad(cw, ((0, 1), (0, CONV_WIDTH - cw.shape[1]))).reshape(2, D_MODEL))
    last = jnp.concatenate([p["attn_sinks"].reshape(1, 8), p.get("loss_sum", jnp.zeros((1, 1), F32))], axis=1)
    rows.append(jnp.pad(last, ((0, 0), (0, D_MODEL - 9))))
    return jnp.concatenate(rows, axis=0)


WEIGHT_ORDER = ("pre_mix_norm", "w_in", "conv_w", "attn_sinks", "attn_group_norm", "conv_group_norm", "w_out",
                "post_mix_norm", "pre_mlp_norm", "w_up", "w_down", "post_mlp_norm")


def kernel(x, pre_mix_norm, w_in, conv_w, attn_sinks, attn_group_norm, conv_group_norm, w_out, post_mix_norm, pre_mlp_norm, w_up, w_down, post_mlp_norm, loss_target, m_pre_mix_norm, m_w_in, m_conv_w, m_attn_sinks, m_attn_group_norm, m_conv_group_norm, m_w_out, m_post_mix_norm, m_pre_mlp_norm, m_w_up, m_w_down, m_post_mlp_norm, v_pre_mix_norm, v_w_in, v_conv_w, v_attn_sinks, v_attn_group_norm, v_conv_group_norm, v_w_out, v_post_mix_norm, v_pre_mlp_norm, v_w_up, v_w_down, v_post_mlp_norm):
    w = dict(pre_mix_norm=pre_mix_norm, w_in=w_in, conv_w=conv_w, attn_sinks=attn_sinks, attn_group_norm=attn_group_norm,
             conv_group_norm=conv_group_norm, w_out=w_out, post_mix_norm=post_mix_norm, pre_mlp_norm=pre_mlp_norm,
             w_up=w_up, w_down=w_down, post_mlp_norm=post_mlp_norm)
    m = dict(pre_mix_norm=m_pre_mix_norm, w_in=m_w_in, conv_w=m_conv_w, attn_sinks=m_attn_sinks,
             attn_group_norm=m_attn_group_norm, conv_group_norm=m_conv_group_norm, w_out=m_w_out,
             post_mix_norm=m_post_mix_norm, pre_mlp_norm=m_pre_mlp_norm, w_up=m_w_up, w_down=m_w_down,
             post_mlp_norm=m_post_mlp_norm)
    v = dict(pre_mix_norm=v_pre_mix_norm, w_in=v_w_in, conv_w=v_conv_w, attn_sinks=v_attn_sinks,
             attn_group_norm=v_attn_group_norm, conv_group_norm=v_conv_group_norm, w_out=v_w_out,
             post_mix_norm=v_post_mix_norm, pre_mlp_norm=v_pre_mlp_norm, w_up=v_w_up, w_down=v_w_down,
             post_mlp_norm=v_post_mlp_norm)
    core = lax.axis_index("c").astype(jnp.int32).reshape(1)
    xs, target = x[0], loss_target[0]
    rope = _rope_inputs(xs.shape[0])

    hb_up, hb_down, hb_out, hb_in = _cast_halves(core, w_up[0], w_down[0], w_out[0], w_in[0].T)
    conv_pad = jnp.pad(conv_w[0], ((0, 8 - CONV_K), (0, 0)))
    wf_in, conv_all = _gather_whole(hb_in, conv_pad)
    conv_full = conv_all[:, :CONV_K, :].transpose(1, 0, 2).reshape(CONV_K, CONV_WIDTH)

    whole_up, early, late = (0, H_UP), (0, DOWN_EARLY_ROWS), (DOWN_EARLY_ROWS, H_DOWN - DOWN_EARLY_ROWS)
    *proj, wf_up, wf_out, wf_down = _in_proj(
        xs, pre_mix_norm, wf_in, rope,
        comm=_merge(_relay(hb_up, None, first=whole_up), _gather_first(hb_out), _relay(hb_down, None, first=early)))
    q, kd0, kd1, vd0, vd1, gb, gc, xin, hn = proj
    attn, wf_up, wf_out, wf_down = _attention_fwd(
        q, kd0, kd1, vd0, vd1, attn_sinks,
        comm=_merge(_relay(None, wf_up, second=whole_up), _gather_second(wf_out),
                    _relay(hb_down, wf_down, first=late, second=early)))
    mix, mixed, wf_up, wf_down = _mix_out(
        attn, gb, gc, xin, conv_full, attn_group_norm, conv_group_norm, wf_out,
        comm=_merge(_relay(None, wf_up, third=whole_up), _relay(None, wf_down, second=late, third=early, third_after=late)))
    up, hn2, dout, dmlp, loss_sum, dg_post_mlp = _mlp_loss(xs, mix, target, post_mix_norm, pre_mlp_norm, post_mlp_norm,
                                                           wf_up, wf_down)

    dup, dh, dmix, dg_pre_mlp, dg_post_mix = _mlp_bwd(dmlp, up, xs, dout, mix, pre_mlp_norm, post_mix_norm, wf_up, wf_down)
    n_k = _wgrad_grid(xs.shape[0], True, H_DOWN, with_rider=True)[3]
    g_down, dattn, dgb, dy, dg_attn, dg_conv, dconv_w = _wgrad(
        "wgrad_down", up, dmlp, per_chip=True, h_rows=H_DOWN, square_a=True,
        rider=_mix_bwd(dmix, attn, gb, gc, xin, conv_full, attn_group_norm, conv_group_norm, wf_out, n_k))
    g_up, got_down = _wgrad("wgrad_up", hn2, dup, per_chip=True, h_rows=H_UP, comm=_pair_send(g_down))
    p_down = _pair_sum("pair_sum_down", core, g_down, got_down)
    g_out, got_up = _wgrad("wgrad_out", mixed, dmix, per_chip=False, h_rows=H_OUT, comm=_pair_send(g_up))
    p_up = _pair_sum("pair_sum_up", core, g_up, got_up)
    dq, dk0, dk1, dv0, dv1, dsink, ex_down, ex_up, got_out = _attention_bwd(
        q, dattn, attn, kd0, kd1, vd0, vd1, attn_sinks,
        comm=_merge(_chip_exchange(p_down), _chip_exchange(p_up), _pair_send(g_out)))
    p_out = _pair_sum("pair_sum_out", core, g_out, got_out)
    dproj, g_in, ex_out = _dproj_wgrad_in(dq, dk0, dk1, dv0, dv1, dgb, dy, gc, xin, conv_full, hn, rope,
                                          comm=_chip_exchange(p_out))
    n_tiles = xs.shape[0] // WIDE_TOKEN_TILE
    n_first = max(n_tiles // 4, 1)
    grad_x, dg_first, got_in = _in_proj_dx("in_proj_dx_first", dproj, xs, dh, pre_mix_norm, wf_in, 0, n_first,
                                           comm=_pair_send(g_in))
    p_in = _pair_sum("pair_sum_in", core, g_in, got_in)
    grad_x, dg_rest, ex_in = _in_proj_dx("in_proj_dx_rest", dproj, xs, dh, pre_mix_norm, wf_in, n_first, n_tiles - n_first,
                                         so_far=grad_x, comm=_chip_exchange(p_in))
    small = dict(pre_mix_norm=dg_first + dg_rest, conv_w=dconv_w, attn_sinks=dsink[:, :8], attn_group_norm=dg_attn,
                 conv_group_norm=dg_conv, post_mix_norm=dg_post_mix, pre_mlp_norm=dg_pre_mlp, post_mlp_norm=dg_post_mlp,
                 loss_sum=loss_sum)
    r_down, r_up, r_out, r_in, small_total = _tail_reduce([ex_down, ex_up, ex_out, ex_in], _pack_small(small))

    out_g, out_d, out_m, out_v = {}, {}, {}, {}
    out_g["w_up"], out_d["w_up"], out_m["w_up"], out_v["w_up"] = _adamw_rows(
        "adamw_up", r_up, w_up[0], m_w_up[0], v_w_up[0], 256)
    out_g["w_down"], out_d["w_down"], out_m["w_down"], out_v["w_down"] = _adamw_rows(
        "adamw_down", r_down, w_down[0], m_w_down[0], v_w_down[0], 256)
    out_g["w_out"], out_d["w_out"], out_m["w_out"], out_v["w_out"] = _adamw_rows(
        "adamw_out", r_out, w_out[0], m_w_out[0], v_w_out[0], H_OUT)
    in_t = _adamw_rows("adamw_in", r_in, w_in[0].T, m_w_in[0].T, v_w_in[0].T, H_IN)
    out_g["w_in"], out_d["w_in"], out_m["w_in"], out_v["w_in"] = [t.T for t in in_t]

    loss = small_total[7, 8] * (0.5 / D_MODEL)
    for out, part in zip((out_g, out_d, out_m, out_v), _adamw_small(small_total, w, m, v)):
        out.update(part)

    def shaped(d):
        return [d[n].reshape(w[n].shape) for n in WEIGHT_ORDER]

    return (loss, grad_x[None], *shaped(out_g), *shaped(out_d), *shaped(out_m), *shaped(out_v))
```

```python
import math
from typing import Callable, NamedTuple

import jax
import jax.numpy as jnp
import numpy as np
from jax import lax
from jax.experimental import pallas as pl
from jax.experimental.pallas import tpu as pltpu

F32 = jnp.float32
BF16 = jnp.bfloat16

D_MODEL = 1024
HEAD_DIM = 64
Q_WIDTH = 512
KV_WIDTH = 128
CONV_WIDTH = 512
CONV_K = 3
D_FF = 4096
IN_COLS = 2304
QBLOCK = 128
ROT_DIM = 16
ROPE_THETA = 500000.0
NORM_EPS = 1e-6
NEG_INF = -1e30
N_CHIPS = 4

ADAM_LR = 0.001
ADAM_B1 = 0.9
ADAM_B2 = 0.999
ADAM_EPS = 1e-08
ADAM_WD = 0.01
ADAM_STEP = 10

H_UP, H_DOWN, H_OUT, H_IN = 512, 512, 128, 288
DOWN_EARLY_ROWS = 224

TOKEN_TILE = 512
WIDE_TOKEN_TILE = 1024
MLP_BWD_TOKEN_TILE = 512
MLP_BWD_SUB_TILE = 256
ATTN_FWD_BLOCKS = 16
ATTN_BWD_BLOCKS = 2
WGRAD_TOKEN_TILE = 4096
VMEM_LIMIT_V7X = 56 * 1024 * 1024

MESH = pl.DeviceIdType.MESH
ANY = pl.BlockSpec(memory_space=pl.ANY)
VMEM_WHOLE = pl.BlockSpec(memory_space=pltpu.VMEM)
SDS = jax.ShapeDtypeStruct


def _resident(shape):
    zeros = (0,) * len(shape)
    return pl.BlockSpec(shape, lambda *_: zeros, pipeline_mode=pl.Buffered(1))


def _rms(v):
    return lax.rsqrt(jnp.mean(v * v, axis=-1, keepdims=True) + NORM_EPS)


def _norm_bwd(dy, gain, vhat, rstd):
    t = dy * gain
    return rstd * (t - vhat * jnp.mean(t * vhat, axis=-1, keepdims=True))


def _colsum(v):
    return jnp.sum(v, axis=0, keepdims=True)


def _dot_nt(a, b):
    return lax.dot_general(a, b, (((1,), (1,)), ((), ())), preferred_element_type=F32)


def _dot_tn(a, b):
    return lax.dot_general(a, b, (((0,), (0,)), ((), ())), preferred_element_type=F32)


def _dot(a, b):
    return jnp.dot(a, b, preferred_element_type=F32)


def _chip_block(w_ref, chip):
    both = w_ref[pl.ds(2 * chip, 2)]
    return both.reshape(2 * both.shape[1], both.shape[2])


def _lane_lt64(shape):
    return lax.broadcasted_iota(jnp.int32, shape, 1) < HEAD_DIM


class _Comm(NamedTuple):
    operands: tuple
    out_shapes: tuple
    aliases: dict
    n_remote: int
    n_local: int
    plan: Callable
    after: Callable = None


def _merge(*comms):
    operands, out_shapes, aliases, parts = [], [], {}, []
    n_remote = n_local = 0
    for cm in comms:
        parts.append((len(operands), len(out_shapes), n_remote, n_local, cm))
        for k, v in cm.aliases.items():
            aliases[len(operands) + k] = len(out_shapes) + v
        operands += cm.operands
        out_shapes += cm.out_shapes
        n_remote += cm.n_remote
        n_local += cm.n_local

    def run(which, ins, outs, send, recv, loc):
        sends, recvs, locs = [], [], []
        for i0, o0, r0, l0, cm in parts:
            stage = getattr(cm, which)
            if stage is not None:
                s, r, l = stage(ins[i0:i0 + len(cm.operands)], outs[o0:o0 + len(cm.out_shapes)],
                                lambda k, r0=r0: send(r0 + k), lambda k, r0=r0: recv(r0 + k), lambda k, l0=l0: loc(l0 + k))
                sends, recvs, locs = sends + s, recvs + r, locs + l
        return sends, recvs, locs

    def plan(*args):
        return run("plan", *args)

    def after(*args):
        return run("after", *args)

    return _Comm(tuple(operands), tuple(out_shapes), aliases, n_remote, n_local, plan,
                 after if any(cm.after is not None for cm in comms) else None)


def _sem_scratch(comm):
    return [pltpu.SemaphoreType.DMA((max(comm.n_remote, 1),)), pltpu.SemaphoreType.DMA((max(comm.n_remote, 1),)),
            pltpu.SemaphoreType.DMA((max(comm.n_local, 1),))]


class _Rider(NamedTuple):
    body: Callable
    in_specs: list
    out_specs: list
    out_shape: list
    operands: tuple


def _pallas(body, *, name, grid, in_specs, out_specs, out_shape, operands, scratch=(), comm=None, rider=None):
    params = pltpu.CompilerParams(dimension_semantics=("arbitrary",) * len(grid), vmem_limit_bytes=VMEM_LIMIT_V7X)
    if rider is not None:
        own_in, own_out, ride_in, ride_out = len(in_specs), len(out_specs), len(rider.in_specs), len(rider.out_specs)
        own_body = body

        def body(*refs):
            o0 = own_in + ride_in
            s0 = o0 + own_out + ride_out
            own_body(*refs[:own_in], *refs[o0:o0 + own_out], *refs[s0:])
            first = None
            for axis in range(len(grid)):
                at_start = pl.program_id(axis) == 0
                first = at_start if first is None else jnp.logical_and(first, at_start)
            rider.body(first, *refs[own_in:o0], *refs[o0 + own_out:s0])

        in_specs, out_specs = list(in_specs) + rider.in_specs, list(out_specs) + rider.out_specs
        out_shape, operands = list(out_shape) + rider.out_shape, tuple(operands) + tuple(rider.operands)
    if comm is None:
        return pl.pallas_call(body, name=name, grid=grid, in_specs=in_specs, out_specs=out_specs, out_shape=out_shape,
                              scratch_shapes=list(scratch), compiler_params=params)(*operands)
    n_in, n_out, n_scr = len(in_specs), len(out_specs), len(scratch)
    c_in, c_out = len(comm.operands), len(comm.out_shapes)

    def with_comm(*refs):
        ins, c_ins = refs[:n_in], refs[n_in:n_in + c_in]
        o0 = n_in + c_in
        outs, c_outs = refs[o0:o0 + n_out], refs[o0 + n_out:o0 + n_out + c_out]
        s0 = o0 + n_out + c_out
        scr = refs[s0:s0 + n_scr]
        send_sems, recv_sems, local_sems = refs[s0 + n_scr:]
        first = last = None
        for axis, size in enumerate(grid):
            at_start, at_end = pl.program_id(axis) == 0, pl.program_id(axis) == size - 1
            first = at_start if first is None else jnp.logical_and(first, at_start)
            last = at_end if last is None else jnp.logical_and(last, at_end)

        def copies():
            return comm.plan(c_ins, c_outs, lambda k: send_sems.at[k], lambda k: recv_sems.at[k],
                             lambda k: local_sems.at[k])

        @pl.when(first)
        def _():
            sends, _, locs = copies()
            for cp in sends + locs:
                cp.start()

        body(*ins, *outs, *scr)

        @pl.when(last)
        def _():
            sends, recvs, locs = copies()
            for cp in recvs:
                cp.wait_recv()
            for cp in sends:
                cp.wait_send()
            for cp in locs:
                cp.wait()
            if comm.after is not None:
                sends, recvs, _ = comm.after(c_ins, c_outs, lambda k: send_sems.at[k], lambda k: recv_sems.at[k],
                                             lambda k: local_sems.at[k])
                for cp in sends:
                    cp.start()
                for cp in recvs:
                    cp.wait_recv()
                for cp in sends:
                    cp.wait_send()

    return pl.pallas_call(
        with_comm, name=name, grid=grid,
        in_specs=list(in_specs) + [ANY] * c_in, out_specs=list(out_specs) + [ANY] * c_out,
        out_shape=list(out_shape) + list(comm.out_shapes),
        scratch_shapes=list(scratch) + _sem_scratch(comm),
        input_output_aliases={n_in + k: n_out + v for k, v in comm.aliases.items()},
        compiler_params=params)(*operands, *comm.operands)


def _place():
    return lax.axis_index("x"), lax.axis_index("y"), lax.axis_index("c")


def _other_chips(x, y):
    return [(1 - x, y), (x, 1 - y), (1 - x, 1 - y)]


def _slot(px, py, pc):
    return 4 * px + 2 * py + pc


def _remote(src, dst, send_sem, recv_sem, to):
    return pltpu.make_async_remote_copy(src_ref=src, dst_ref=dst, send_sem=send_sem, recv_sem=recv_sem,
                                        device_id=to, device_id_type=MESH)


def _gather_first(half_block):
    def plan(ins, outs, send, recv, loc):
        (blk,), (full,) = ins, outs
        x, y, c = _place()
        chips = _other_chips(x, y)
        mine = full.at[_slot(x, y, c)]
        sends = [_remote(blk, mine, send(0), recv(0), (x, y, 1 - c))]
        sends += [_remote(blk, mine, send(1 + j), recv(1 + j), (*chip, c)) for j, chip in enumerate(chips)]
        recvs = [_remote(blk, full.at[_slot(x, y, 1 - c)], send(0), recv(0), (x, y, 1 - c))]
        recvs += [_remote(blk, full.at[_slot(*chip, c)], send(1 + j), recv(1 + j), (*chip, c))
                  for j, chip in enumerate(chips)]
        return sends, recvs, [pltpu.make_async_copy(blk, mine, loc(0))]

    return _Comm((half_block,), (SDS((2 * N_CHIPS,) + half_block.shape, half_block.dtype),), {}, 4, 1, plan)


def _gather_second(partly_gathered):
    def plan(ins, outs, send, recv, loc):
        (src,), (full,) = ins, outs
        x, y, c = _place()
        chips = _other_chips(x, y)
        sends = [_remote(src.at[_slot(*chip, c)], full.at[_slot(*chip, c)], send(j), recv(j), (x, y, 1 - c))
                 for j, chip in enumerate(chips)]
        recvs = [_remote(src.at[_slot(*chip, 1 - c)], full.at[_slot(*chip, 1 - c)], send(j), recv(j), (x, y, 1 - c))
                 for j, chip in enumerate(chips)]
        return sends, recvs, []

    return _Comm((partly_gathered,), (SDS(partly_gathered.shape, partly_gathered.dtype),), {0: 0}, 3, 0, plan)


def _relay_pieces(full, rows, x, y, c):
    start, half = rows[0], rows[1] // 2
    upper, lower = pl.ds(start, half), pl.ds(start + half, half)
    diagonal = full.at[_slot(1 - x, 1 - y, c)]
    return [(full.at[_slot(1 - x, y, c), upper], diagonal.at[upper], (x, 1 - y, c)),
            (full.at[_slot(x, 1 - y, c), lower], diagonal.at[lower], (1 - x, y, c))]


def _relay(half_block, so_far, first=None, second=None, third=None, third_after=None):
    has_block, has_buffer = half_block is not None, so_far is not None
    shape = so_far.shape if has_buffer else (2 * N_CHIPS,) + half_block.shape
    dtype = so_far.dtype if has_buffer else half_block.dtype

    def third_leg(rows, k, ins, outs, send, recv):
        src, full = (ins[-1] if has_buffer else outs[0]), outs[0]
        x, y, c = _place()
        span, sibling = pl.ds(*rows), (x, y, 1 - c)
        here, there = _slot(1 - x, 1 - y, c), _slot(1 - x, 1 - y, 1 - c)
        return ([_remote(src.at[here, span], full.at[here, span], send(k), recv(k), sibling)],
                [_remote(src.at[there, span], full.at[there, span], send(k), recv(k), sibling)])

    def plan(ins, outs, send, recv, loc):
        src, full = (ins[-1] if has_buffer else outs[0]), outs[0]
        x, y, c = _place()
        sibling = (x, y, 1 - c)
        sends, recvs, locs = [], [], []
        if first is not None:
            span = pl.ds(*first)
            blk, mine = ins[0].at[span], full.at[_slot(x, y, c), span]
            for k, peer in enumerate([sibling, (1 - x, y, c), (x, 1 - y, c)]):
                sends.append(_remote(blk, mine, send(k), recv(k), peer))
                recvs.append(_remote(blk, full.at[_slot(*peer), span], send(k), recv(k), peer))
            locs.append(pltpu.make_async_copy(blk, mine, loc(0)))
        if second is not None:
            span = pl.ds(*second)
            for k, chip in enumerate([(1 - x, y), (x, 1 - y)]):
                sends.append(_remote(src.at[_slot(*chip, c), span], full.at[_slot(*chip, c), span], send(3 + k), recv(3 + k),
                                     sibling))
                recvs.append(_remote(src.at[_slot(*chip, 1 - c), span], full.at[_slot(*chip, 1 - c), span], send(3 + k),
                                     recv(3 + k), sibling))
            for k, (piece, lands, peer) in enumerate(_relay_pieces(full, second, x, y, c)):
                sends.append(_remote(piece, piece, send(5 + k), recv(5 + k), peer))
                recvs.append(_remote(lands, lands, send(5 + k), recv(5 + k), peer))
        if third is not None:
            s, r = third_leg(third, 7, ins, outs, send, recv)
            sends, recvs = sends + s, recvs + r
        return sends, recvs, locs

    def after(ins, outs, send, recv, loc):
        s, r = third_leg(third_after, 8, ins, outs, send, recv)
        return s, r, []

    operands = ((half_block,) if has_block else ()) + ((so_far,) if has_buffer else ())
    return _Comm(operands, (SDS(shape, dtype),), {len(operands) - 1: 0} if has_buffer else {}, 9, 1, plan,
                 after if third_after is not None else None)


def _gather_whole(half_block, small_block):
    rows = half_block.shape[0]

    def body(blk_ref, small_ref, out_ref, small_out_ref, send_sems, recv_sems, local_sems):
        x, y, c = _place()
        me, sibling = (x, y, c), (x, y, 1 - c)
        neighbours, diagonal = [(1 - x, y), (x, 1 - y)], (1 - x, 1 - y)

        def copy(k, block, to, src=None):
            return _remote(out_ref.at[_slot(*block)] if src is None else src, out_ref.at[_slot(*block)],
                           send_sems.at[k], recv_sems.at[k], to)

        def small_copy(k, chip, to):
            return _remote(small_ref, small_out_ref.at[2 * chip[0] + chip[1]], send_sems.at[8 + k], recv_sems.at[8 + k], to)

        mine = pltpu.make_async_copy(blk_ref, out_ref.at[_slot(*me)], local_sems.at[0])
        mine_small = pltpu.make_async_copy(small_ref, small_out_ref.at[2 * x + y], local_sems.at[1])
        mine.start()
        mine_small.start()
        started = [copy(0, me, sibling, src=blk_ref)]
        started += [copy(1 + k, me, (*chip, c), src=blk_ref) for k, chip in enumerate(neighbours)]
        started += [small_copy(k, (x, y), (*chip, c)) for k, chip in enumerate(neighbours + [diagonal])]
        for cp in started:
            cp.start()
        pieces = _relay_pieces(out_ref, (0, rows), x, y, c)
        for k, chip in enumerate(neighbours):
            copy(1 + k, (*chip, c), me).wait_recv()
            piece, _, peer = pieces[k]
            started += [copy(3 + k, (*chip, c), sibling), _remote(piece, piece, send_sems.at[5 + k], recv_sems.at[5 + k], peer)]
            started[-2].start()
            started[-1].start()
        for k, (_, lands, peer) in enumerate(pieces):
            _remote(lands, lands, send_sems.at[5 + k], recv_sems.at[5 + k], peer).wait_recv()
        started.append(copy(7, (*diagonal, c), sibling))
        started[-1].start()
        copy(0, sibling, me).wait_recv()
        for k, chip in enumerate(neighbours):
            copy(3 + k, (*chip, 1 - c), me).wait_recv()
        copy(7, (*diagonal, 1 - c), me).wait_recv()
        for k, chip in enumerate(neighbours + [diagonal]):
            small_copy(k, chip, me).wait_recv()
        for cp in started:
            cp.wait_send()
        mine.wait()
        mine_small.wait()

    return pl.pallas_call(
        body, name="gather_whole", in_specs=[ANY, ANY], out_specs=[ANY, ANY],
        out_shape=[SDS((2 * N_CHIPS,) + half_block.shape, half_block.dtype),
                   SDS((N_CHIPS,) + small_block.shape, small_block.dtype)],
        scratch_shapes=[pltpu.SemaphoreType.DMA((11,)), pltpu.SemaphoreType.DMA((11,)), pltpu.SemaphoreType.DMA((2,))],
    )(half_block, small_block)


def _pair_send(grads):
    def plan(ins, outs, send, recv, loc):
        (g,), (got,) = ins, outs
        x, y, c = _place()
        copies = [_remote(g.at[j, 1 - c], got.at[j], send(j), recv(j), (x, y, 1 - c)) for j in range(N_CHIPS)]
        return copies, copies, []

    shape = (grads.shape[0],) + grads.shape[2:]
    return _Comm((grads,), (SDS(shape, grads.dtype),), {}, N_CHIPS, 0, plan)


def _chip_exchange(partial):
    def plan(ins, outs, send, recv, loc):
        (p,), (got,) = ins, outs
        x, y, c = _place()
        my_chip = 2 * x + y
        chips = _other_chips(x, y)
        sends = [_remote(p.at[2 * chip[0] + chip[1]], got.at[my_chip], send(j), recv(j), (*chip, c))
                 for j, chip in enumerate(chips)]
        recvs = [_remote(p.at[my_chip], got.at[2 * chip[0] + chip[1]], send(j), recv(j), (*chip, c))
                 for j, chip in enumerate(chips)]
        return sends, recvs, [pltpu.make_async_copy(p.at[my_chip], got.at[my_chip], loc(0))]

    return _Comm((partial,), (SDS(partial.shape, partial.dtype),), {}, 3, 1, plan)


def _pair_sum(name, core, grads, received):
    h = grads.shape[2]

    def body(core_ref, g_ref, r_ref, o_ref):
        o_ref[...] = (g_ref[0] + r_ref[...]).astype(BF16)

    return pl.pallas_call(
        body, name=name,
        grid_spec=pltpu.PrefetchScalarGridSpec(
            num_scalar_prefetch=1, grid=(N_CHIPS,),
            in_specs=[pl.BlockSpec((1, 1, h, D_MODEL), lambda j, core_ref: (j, core_ref[0], 0, 0)),
                      pl.BlockSpec((1, h, D_MODEL), lambda j, core_ref: (j, 0, 0))],
            out_specs=pl.BlockSpec((1, h, D_MODEL), lambda j, core_ref: (j, 0, 0))),
        out_shape=SDS((N_CHIPS, h, D_MODEL), BF16),
        compiler_params=pltpu.CompilerParams(dimension_semantics=("arbitrary",), vmem_limit_bytes=VMEM_LIMIT_V7X),
    )(core, grads, received)


SMALL_ROWS = 8


def _sum_blocks(ref):
    return (ref[0].astype(F32) + ref[1].astype(F32)) + (ref[2].astype(F32) + ref[3].astype(F32))


def _tail_reduce(last_grads, exchanged, small):
    n = len(exchanged)
    h = last_grads.shape[2]

    def body(*refs):
        g_ref, ex, small_ref = refs[0], refs[1:1 + n], refs[1 + n]
        o0 = 2 + n
        out, out_last, small_out = refs[o0:o0 + n], refs[o0 + n], refs[o0 + n + 1]
        s0 = o0 + n + 2
        halves, half_last = refs[s0:s0 + n], refs[s0 + n]
        own, got, part, exch, small_buf = refs[s0 + n + 1:s0 + n + 6]
        pair_send, pair_recv, chip_send, chip_recv, share_send, share_recv, small_send, small_recv, local_sems = refs[s0 + n + 6:]
        x, y, c = _place()
        sibling = (x, y, 1 - c)
        my_chip, me = 2 * x + y, _slot(x, y, c)
        chips = _other_chips(x, y)

        to_sibling = [_remote(g_ref.at[j, 1 - c], got.at[j], pair_send.at[j], pair_recv.at[j], sibling)
                      for j in range(N_CHIPS)]
        load_own = [pltpu.make_async_copy(g_ref.at[j, c], own.at[j], local_sems.at[j]) for j in range(N_CHIPS)]
        for cp in to_sibling + load_own:
            cp.start()

        small_buf[me] = small_ref[...]
        small_copies = []
        for mask in range(1, 8):
            peer = (x ^ (mask >> 2), y ^ ((mask >> 1) & 1), c ^ (mask & 1))
            small_copies.append(_remote(small_ref, small_buf.at[me], small_send.at[mask - 1], small_recv.at[mask - 1], peer))
        for cp in small_copies:
            cp.start()

        def share(k, half_ref, out_ref):
            keep = pltpu.make_async_copy(half_ref, out_ref.at[c], local_sems.at[N_CHIPS + k])
            give = _remote(half_ref, out_ref.at[c], share_send.at[k], share_recv.at[k], sibling)
            take = _remote(half_ref, out_ref.at[1 - c], share_send.at[k], share_recv.at[k], sibling)
            keep.start()
            give.start()
            return keep, give, take

        shares = []
        for k in range(n):
            halves[k][...] = _sum_blocks(ex[k])
            shares.append(share(k, halves[k], out[k]))

        def pair_sum(block):
            _remote(g_ref.at[block, 1 - c], got.at[block], pair_send.at[block], pair_recv.at[block], sibling).wait_recv()
            pltpu.make_async_copy(g_ref.at[block, c], own.at[block], local_sems.at[block]).wait()
            part[block] = (own[block] + got[block]).astype(BF16)

        to_chips = []
        for j, chip in enumerate(chips):
            block = 2 * chip[0] + chip[1]
            pair_sum(block)
            to_chips.append(_remote(part.at[block], exch.at[my_chip], chip_send.at[j], chip_recv.at[j], (*chip, c)))
            to_chips[-1].start()
        pair_sum(my_chip)
        exch[my_chip] = part[my_chip]
        from_chips = [_remote(part.at[my_chip], exch.at[2 * chip[0] + chip[1]], chip_send.at[j], chip_recv.at[j], (*chip, c))
                      for j, chip in enumerate(chips)]

        for cp in small_copies:
            cp.wait_recv()
        total = small_buf[0]
        for d in range(1, 8):
            total = total + small_buf[d]
        small_out[...] = total

        for cp in from_chips:
            cp.wait_recv()
        half_last[...] = _sum_blocks(exch)
        shares.append(share(n, half_last, out_last))

        for keep, give, take in shares:
            take.wait_recv()
            give.wait_send()
            keep.wait()
        for cp in to_sibling + to_chips + small_copies:
            cp.wait_send()

    blocks = (N_CHIPS, h, D_MODEL)
    return pl.pallas_call(
        body, name="tail_reduce",
        in_specs=[ANY] + [VMEM_WHOLE] * (n + 1), out_specs=[ANY] * (n + 1) + [VMEM_WHOLE],
        out_shape=[SDS((2,) + e.shape[1:], F32) for e in exchanged] + [SDS((2, h, D_MODEL), F32), SDS(small.shape, F32)],
        scratch_shapes=[pltpu.VMEM(e.shape[1:], F32) for e in exchanged] + [pltpu.VMEM((h, D_MODEL), F32)]
                       + [pltpu.VMEM(blocks, F32), pltpu.VMEM(blocks, F32), pltpu.VMEM(blocks, BF16), pltpu.VMEM(blocks, BF16),
                          pltpu.VMEM((8,) + small.shape, F32)]
                       + [pltpu.SemaphoreType.DMA((N_CHIPS,)), pltpu.SemaphoreType.DMA((N_CHIPS,)),
                          pltpu.SemaphoreType.DMA((3,)), pltpu.SemaphoreType.DMA((3,)),
                          pltpu.SemaphoreType.DMA((n + 1,)), pltpu.SemaphoreType.DMA((n + 1,)),
                          pltpu.SemaphoreType.DMA((7,)), pltpu.SemaphoreType.DMA((7,)),
                          pltpu.SemaphoreType.DMA((N_CHIPS + n + 1,))],
        compiler_params=pltpu.CompilerParams(vmem_limit_bytes=VMEM_LIMIT_V7X),
    )(last_grads, *exchanged, small)


def _rope_expansion():
    half = ROT_DIM // 2
    expand = np.zeros((2 * half, 3 * 128), np.float32)
    const = np.zeros((1, 3 * 128), np.float32)
    for lane in range(128):
        d = lane % HEAD_DIM
        if d < ROT_DIM:
            expand[d % half, lane] = 1.0
        else:
            const[0, lane] = 1.0
        if d < half:
            expand[half + d, 128 + lane] = -1.0
        elif d < ROT_DIM:
            expand[half + d - half, 256 + lane] = 1.0
    return expand, const


ROPE_PIECES = 3 * ROT_DIM


def _rope_inputs(seq):
    pos = jnp.arange(seq, dtype=F32)
    inv_freq = ROPE_THETA ** (-jnp.arange(0, ROT_DIM, 2, dtype=F32) / ROT_DIM)
    ang = pos[:, None] * inv_freq[None, :]
    cs = jnp.concatenate([jnp.cos(ang), jnp.sin(ang)], axis=1)
    hi = lax.reduce_precision(cs, 8, 7)
    mid = lax.reduce_precision(cs - hi, 8, 7)
    low = cs - hi - mid
    expand, const = _rope_expansion()
    pieces = jnp.concatenate([hi, mid, low], axis=1).astype(BF16)
    return pieces, jnp.asarray(np.concatenate([expand] * 3, axis=0), BF16), jnp.asarray(const)


def _rope_specs(tb):
    return [pl.BlockSpec((tb, ROPE_PIECES), lambda i: (i, 0)), _resident((ROPE_PIECES, 3 * 128)), _resident((1, 3 * 128))]


def _rope_tile(pieces_ref, expand_ref, const_ref):
    tables = _dot(pieces_ref[...], expand_ref[...]) + const_ref[...]
    return tables[:, 0:128], tables[:, 128:256], tables[:, 256:384]


def _rope(t, c, sa, sb):
    half = ROT_DIM // 2
    return t * c + pltpu.roll(t, 128 - half, 1) * sa + pltpu.roll(t, half, 1) * sb


def _rope_transposed(dt, c, sa, sb):
    half = ROT_DIM // 2
    return dt * c + pltpu.roll(dt * sa, half, 1) + pltpu.roll(dt * sb, 128 - half, 1)


def _cast_halves(core, w_up, w_down, w_out, w_in_t):
    def body(core_ref, up_ref, down_ref, out_ref, in_ref, up_o, down_o, out_o, in_o):
        up_o[...] = up_ref[...].astype(BF16)
        down_o[...] = down_ref[...].astype(BF16)
        out_o[...] = out_ref[...].astype(BF16)
        in_o[...] = in_ref[...].astype(BF16)

    half = lambda rows: pl.BlockSpec((rows, D_MODEL), lambda i, core_ref: (core_ref[0], 0))
    whole = lambda rows: pl.BlockSpec((rows, D_MODEL), lambda i, core_ref: (0, 0))
    rows = (H_UP, H_DOWN, H_OUT, H_IN)
    return pl.pallas_call(
        body, name="cast_halves",
        grid_spec=pltpu.PrefetchScalarGridSpec(
            num_scalar_prefetch=1, grid=(1,), in_specs=[half(r) for r in rows], out_specs=[whole(r) for r in rows]),
        out_shape=[SDS((r, D_MODEL), BF16) for r in rows],
        compiler_params=pltpu.CompilerParams(dimension_semantics=("arbitrary",), vmem_limit_bytes=VMEM_LIMIT_V7X),
    )(core, w_up, w_down, w_out, w_in_t)


def _in_proj(x, g_pre, w_in_t, rope, comm=None):
    seq = x.shape[0]
    tb = min(seq, WIDE_TOKEN_TILE)

    def body(x_ref, g_ref, w_ref, c_ref, sa_ref, sb_ref,
             q_ref, kd0_ref, kd1_ref, vd0_ref, vd1_ref, gb_ref, gc_ref, xin_ref, hn_ref):
        xv = x_ref[...]
        hn = (xv * _rms(xv) * g_ref[...]).astype(BF16)
        hn_ref[...] = hn
        proj = _dot_nt(hn, w_ref[...].reshape(IN_COLS, D_MODEL))
        c, sa, sb = _rope_tile(c_ref, sa_ref, sb_ref)
        scale = 1.0 / math.sqrt(HEAD_DIM)
        for p in range(Q_WIDTH // 128):
            q_ref[:, 128 * p:128 * (p + 1)] = (_rope(proj[:, 128 * p:128 * (p + 1)], c, sa, sb) * scale).astype(BF16)
        k = _rope(proj[:, Q_WIDTH:Q_WIDTH + KV_WIDTH], c, sa, sb)
        v = proj[:, Q_WIDTH + KV_WIDTH:Q_WIDTH + 2 * KV_WIDTH]
        low = _lane_lt64(k.shape)
        k_sw, v_sw = pltpu.roll(k, HEAD_DIM, 1), pltpu.roll(v, HEAD_DIM, 1)
        kd0_ref[...] = jnp.where(low, k, k_sw).astype(BF16)
        kd1_ref[...] = jnp.where(low, k_sw, k).astype(BF16)
        vd0_ref[...] = jnp.where(low, v, v_sw).astype(BF16)
        vd1_ref[...] = jnp.where(low, v_sw, v).astype(BF16)
        base = Q_WIDTH + 2 * KV_WIDTH
        gb_ref[...] = proj[:, base:base + CONV_WIDTH].astype(BF16)
        gc_ref[...] = proj[:, base + CONV_WIDTH:base + 2 * CONV_WIDTH].astype(BF16)
        xin_ref[...] = proj[:, base + 2 * CONV_WIDTH:base + 3 * CONV_WIDTH].astype(BF16)

    tile = lambda w: pl.BlockSpec((tb, w), lambda i: (i, 0))
    return _pallas(
        body, name="in_proj", grid=(seq // tb,),
        in_specs=[tile(D_MODEL), _resident((1, D_MODEL)), _resident(w_in_t.shape), *_rope_specs(tb)],
        out_specs=[tile(Q_WIDTH), tile(128), tile(128), tile(128), tile(128),
                   tile(CONV_WIDTH), tile(CONV_WIDTH), tile(CONV_WIDTH), tile(D_MODEL)],
        out_shape=[SDS((seq, Q_WIDTH), BF16)] + [SDS((seq, 128), BF16)] * 4
                  + [SDS((seq, CONV_WIDTH), BF16)] * 3 + [SDS((seq, D_MODEL), BF16)],
        operands=(x, g_pre, w_in_t, *rope), comm=comm)


def _attn_valid(i):
    shape = (4 * QBLOCK, 2 * QBLOCK)
    row = lax.broadcasted_iota(jnp.int32, shape, 0)
    col = lax.broadcasted_iota(jnp.int32, shape, 1)
    qi = row & (QBLOCK - 1)
    return (col > qi) & (col <= qi + QBLOCK) & ((col >= QBLOCK) | (i > 0))


def _stack_heads(pair0, pair1):
    low = _lane_lt64(pair0.shape)
    zero = jnp.zeros_like(pair0)
    return jnp.concatenate([jnp.where(low, pair0, zero), jnp.where(low, zero, pair0),
                            jnp.where(low, pair1, zero), jnp.where(low, zero, pair1)], axis=0)


def _unstack_heads(stacked):
    low = _lane_lt64((QBLOCK, 128))
    pair0 = jnp.where(low, stacked[0:QBLOCK], stacked[QBLOCK:2 * QBLOCK])
    pair1 = jnp.where(low, stacked[2 * QBLOCK:3 * QBLOCK], stacked[3 * QBLOCK:4 * QBLOCK])
    return pair0, pair1


def _sink_column(sink_ref, kv_head):
    row = lax.broadcasted_iota(jnp.int32, (4 * QBLOCK, 1), 0)
    s = [sink_ref[0, 4 * kv_head + j] for j in range(4)]
    return jnp.where(row < QBLOCK, s[0], jnp.where(row < 2 * QBLOCK, s[1], jnp.where(row < 3 * QBLOCK, s[2], s[3])))


def _band(ref, i):
    prev = pl.multiple_of(jnp.maximum(i - 1, 0) * QBLOCK, QBLOCK)
    own = pl.multiple_of(i * QBLOCK, QBLOCK)
    return jnp.concatenate([ref[pl.ds(prev, QBLOCK), :], ref[pl.ds(own, QBLOCK), :]], axis=0), prev, own


def _softmax_with_sink(s, sink_col):
    m = jnp.maximum(jnp.max(s, axis=-1, keepdims=True), sink_col)
    p = jnp.exp(s - m)
    e_sink = jnp.exp(sink_col - m)
    inv_l = 1.0 / (jnp.sum(p, axis=-1, keepdims=True) + e_sink)
    return p, e_sink, inv_l


def _attention_fwd(q, kd0, kd1, vd0, vd1, sinks, comm=None):
    seq = q.shape[0]

    nb = ATTN_FWD_BLOCKS

    def body(sink_ref, q_ref, kd0_ref, kd1_ref, vd0_ref, vd1_ref, o_ref):
        for b in range(nb):
            i = pl.program_id(0) * nb + b
            rows = slice(QBLOCK * b, QBLOCK * (b + 1))
            valid = _attn_valid(i)
            for kv_head, (k_ref, v_ref) in enumerate(((kd0_ref, vd0_ref), (kd1_ref, vd1_ref))):
                kband, _, _ = _band(k_ref, i)
                vband, _, _ = _band(v_ref, i)
                base = 256 * kv_head
                qm = _stack_heads(q_ref[rows, base:base + 128], q_ref[rows, base + 128:base + 256])
                s = jnp.where(valid, _dot_nt(qm, kband), NEG_INF)
                p, _, inv_l = _softmax_with_sink(s, _sink_column(sink_ref, kv_head))
                o = _dot(p.astype(BF16), vband) * inv_l
                pair0, pair1 = _unstack_heads(o)
                o_ref[rows, base:base + 128] = pair0.astype(BF16)
                o_ref[rows, base + 128:base + 256] = pair1.astype(BF16)

    blk = pl.BlockSpec((nb * QBLOCK, Q_WIDTH), lambda i: (i, 0))
    full = _resident((seq, 128))
    return _pallas(
        body, name="attention_fwd", grid=(seq // (nb * QBLOCK),),
        in_specs=[pl.BlockSpec(memory_space=pltpu.SMEM), blk, full, full, full, full],
        out_specs=[blk], out_shape=[SDS((seq, Q_WIDTH), BF16)],
        operands=(sinks, q, kd0, kd1, vd0, vd1), comm=comm)


HALO = 16


def _conv_parts(gc, xin, gc_halo, xin_halo, conv_w, first):
    tb = gc.shape[0]
    u = gc.astype(F32) * xin.astype(F32)
    u_halo = jnp.where(first, 0.0, gc_halo.astype(F32) * xin_halo.astype(F32))
    ext = jnp.concatenate([u_halo, u], axis=0)
    u1 = pltpu.roll(ext, 1, 0)[HALO:HALO + tb]
    u2 = pltpu.roll(ext, 2, 0)[HALO:HALO + tb]
    y = conv_w[0:1, :] * u2 + conv_w[1:2, :] * u1 + conv_w[2:3, :] * u
    return u, u1, u2, y


def _halo_prev(tb, w):
    return pl.BlockSpec((HALO, w), lambda i: (jnp.maximum(i * (tb // HALO) - 1, 0), 0))


def _residual_mid(x, mix, g_post_mix):
    mix_f = mix.astype(F32)
    return x + mix_f * _rms(mix_f) * g_post_mix


def _mix_out(attn, gb, gc, xin, conv_w, g_attn, g_conv, w_out, comm=None):
    seq = attn.shape[0]
    tb = min(seq, WIDE_TOKEN_TILE)

    def body(a_ref, gb_ref, gc_ref, xin_ref, gch_ref, xinh_ref, cw_ref, ga_ref, gcn_ref, w_ref, mix_ref, mixed_ref):
        first = pl.program_id(0) == 0
        _, _, _, y = _conv_parts(gc_ref[...], xin_ref[...], gch_ref[...], xinh_ref[...], cw_ref[...], first)
        conv = gb_ref[...].astype(F32) * y
        a = a_ref[...].astype(F32)
        mixed_ref[:, 0:Q_WIDTH] = (a * _rms(a) * ga_ref[...]).astype(BF16)
        mixed_ref[:, Q_WIDTH:] = (conv * _rms(conv) * gcn_ref[...]).astype(BF16)
        mix_ref[...] = _dot(mixed_ref[...], w_ref[...].reshape(D_MODEL, D_MODEL)).astype(BF16)

    tile = lambda w: pl.BlockSpec((tb, w), lambda i: (i, 0))
    return _pallas(
        body, name="mix_out", grid=(seq // tb,),
        in_specs=[tile(Q_WIDTH), tile(CONV_WIDTH), tile(CONV_WIDTH), tile(CONV_WIDTH),
                  _halo_prev(tb, CONV_WIDTH), _halo_prev(tb, CONV_WIDTH),
                  _resident((CONV_K, CONV_WIDTH)), _resident((1, Q_WIDTH)), _resident((1, CONV_WIDTH)),
                  _resident(w_out.shape)],
        out_specs=[tile(D_MODEL), tile(D_MODEL)],
        out_shape=[SDS((seq, D_MODEL), BF16), SDS((seq, D_MODEL), BF16)],
        operands=(attn, gb, gc, xin, gc, xin, conv_w, g_attn, g_conv, w_out), comm=comm)


def _mlp_loss(x, mix, target, g_post_mix, g_pre_mlp, g_post_mlp, w_up, w_down):
    seq = x.shape[0]
    tb = TOKEN_TILE

    def body(x_ref, mix_ref, t_ref, gpm_ref, g2_ref, g4_ref, wup_ref, wdown_ref,
             up_ref, hn2_ref, dout_ref, dmlp_ref, loss_ref, dg4_ref, act_ref):
        @pl.when(pl.program_id(0) == 0)
        def _():
            loss_ref[...] = jnp.zeros_like(loss_ref)
            dg4_ref[...] = jnp.zeros_like(dg4_ref)

        halves = [slice(0, tb // 2), slice(tb // 2, tb)]
        hv, hn2 = [], []
        for rows in halves:
            hv.append(_residual_mid(x_ref[rows, :], mix_ref[rows, :], gpm_ref[...]))
            hn2.append((hv[-1] * _rms(hv[-1]) * g2_ref[...]).astype(BF16))
            hn2_ref[rows, :] = hn2[-1]
        for k, rows in enumerate(halves):
            for j in range(N_CHIPS):
                up = _dot(hn2[k], _chip_block(wup_ref, j))
                up = jnp.maximum(up, 0.0)
                up_ref[rows, 1024 * j:1024 * (j + 1)] = up.astype(BF16)
                act_ref[rows, 1024 * j:1024 * (j + 1)] = (up * up).astype(BF16)
        w_down_all = wdown_ref[...].reshape(D_FF, D_MODEL)
        loss = jnp.zeros((1, 1), F32)
        dg4 = jnp.zeros((1, D_MODEL), F32)
        for k, rows in enumerate(halves):
            mlp = _dot(act_ref[rows, :], w_down_all)
            rstd = _rms(mlp)
            zhat = mlp * rstd
            diff = hv[k] + zhat * g4_ref[...] - t_ref[rows, :]
            loss = loss + jnp.sum(jnp.sum(diff * diff, axis=1, keepdims=True), axis=0, keepdims=True)
            dout = diff * (1.0 / D_MODEL)
            dout_ref[rows, :] = dout
            dg4 = dg4 + _colsum(dout * zhat)
            dmlp_ref[rows, :] = _norm_bwd(dout, g4_ref[...], zhat, rstd).astype(BF16)
        loss_ref[...] += loss
        dg4_ref[...] += dg4

    tile = lambda w: pl.BlockSpec((tb, w), lambda i: (i, 0))
    return _pallas(
        body, name="mlp_loss", grid=(seq // tb,),
        in_specs=[tile(D_MODEL), tile(D_MODEL), tile(D_MODEL), _resident((1, D_MODEL)), _resident((1, D_MODEL)),
                  _resident((1, D_MODEL)), _resident(w_up.shape), _resident(w_down.shape)],
        out_specs=[tile(D_FF), tile(D_MODEL), tile(D_MODEL), tile(D_MODEL),
                   pl.BlockSpec((1, 1), lambda i: (0, 0)), pl.BlockSpec((1, D_MODEL), lambda i: (0, 0))],
        out_shape=[SDS((seq, D_FF), BF16), SDS((seq, D_MODEL), BF16), SDS((seq, D_MODEL), F32),
                   SDS((seq, D_MODEL), BF16), SDS((1, 1), F32), SDS((1, D_MODEL), F32)],
        scratch=[pltpu.VMEM((tb, D_FF), BF16)],
        operands=(x, mix, target, g_post_mix, g_pre_mlp, g_post_mlp, w_up, w_down))


def _mlp_bwd(dmlp, up, x, dout, mix, g_pre_mlp, g_post_mix, w_up, w_down):
    seq = x.shape[0]
    tb = MLP_BWD_TOKEN_TILE

    def body(dmlp_ref, up_ref, x_ref, dout_ref, mix_ref, g2_ref, gpm_ref, wup_ref, wdown_ref,
             dup_ref, dh_ref, dmix_ref, dg2_ref, dgpm_ref):
        @pl.when(pl.program_id(0) == 0)
        def _():
            dg2_ref[...] = jnp.zeros_like(dg2_ref)
            dgpm_ref[...] = jnp.zeros_like(dgpm_ref)

        subs = [slice(k * MLP_BWD_SUB_TILE, (k + 1) * MLP_BWD_SUB_TILE) for k in range(tb // MLP_BWD_SUB_TILE)]
        dhn2 = []
        for rows in subs:
            dmlp_v = dmlp_ref[rows, :]
            acc = None
            for j in range(N_CHIPS):
                cols = slice(1024 * j, 1024 * (j + 1))
                dact = _dot_nt(dmlp_v, _chip_block(wdown_ref, j))
                dup = (dact * (2.0 * up_ref[rows, cols].astype(F32))).astype(BF16)
                dup_ref[rows, cols] = dup
                part = _dot_nt(dup, _chip_block(wup_ref, j))
                acc = part if acc is None else acc + part
            dhn2.append(acc)
        dg2 = jnp.zeros((1, D_MODEL), F32)
        dgpm = jnp.zeros((1, D_MODEL), F32)
        for k, rows in enumerate(subs):
            mix_v = mix_ref[rows, :].astype(F32)
            hv = _residual_mid(x_ref[rows, :], mix_ref[rows, :], gpm_ref[...])
            r2 = _rms(hv)
            hhat = hv * r2
            dg2 = dg2 + _colsum(dhn2[k] * hhat)
            dh = dout_ref[rows, :] + _norm_bwd(dhn2[k], g2_ref[...], hhat, r2)
            dh_ref[rows, :] = dh.astype(BF16)
            rz = _rms(mix_v)
            zhat = mix_v * rz
            dgpm = dgpm + _colsum(dh * zhat)
            dmix_ref[rows, :] = _norm_bwd(dh, gpm_ref[...], zhat, rz).astype(BF16)
        dg2_ref[...] += dg2
        dgpm_ref[...] += dgpm

    tile = lambda w: pl.BlockSpec((tb, w), lambda i: (i, 0))
    vec = pl.BlockSpec((1, D_MODEL), lambda i: (0, 0))
    return _pallas(
        body, name="mlp_bwd", grid=(seq // tb,),
        in_specs=[tile(D_MODEL), tile(D_FF), tile(D_MODEL), tile(D_MODEL), tile(D_MODEL),
                  _resident((1, D_MODEL)), _resident((1, D_MODEL)), _resident(w_up.shape), _resident(w_down.shape)],
        out_specs=[tile(D_FF), tile(D_MODEL), tile(D_MODEL), vec, vec],
        out_shape=[SDS((seq, D_FF), BF16), SDS((seq, D_MODEL), BF16), SDS((seq, D_MODEL), BF16),
                   SDS((1, D_MODEL), F32), SDS((1, D_MODEL), F32)],
        operands=(dmlp, up, x, dout, mix, g_pre_mlp, g_post_mix, w_up, w_down))


def _mix_bwd(dmix, attn, gb, gc, xin, conv_w, g_attn, g_conv, w_out, n_k):
    seq = attn.shape[0]
    tb = seq // (N_CHIPS * n_k)

    def body(first, dmix_ref, a_ref, gb_ref, gc_ref, xin_ref, gch_ref, xinh_ref, cw_ref, ga_ref, gcn_ref, w_ref,
             dattn_ref, dgb_ref, dy_ref, dga_ref, dgcn_ref, dcw_ref):
        @pl.when(first)
        def _():
            dga_ref[...] = jnp.zeros_like(dga_ref)
            dgcn_ref[...] = jnp.zeros_like(dgcn_ref)
            dcw_ref[...] = jnp.zeros_like(dcw_ref)

        dmixed = _dot_nt(dmix_ref[...], w_ref[...].reshape(D_MODEL, D_MODEL))
        a = a_ref[...].astype(F32)
        ra = _rms(a)
        ahat = a * ra
        dan = dmixed[:, 0:Q_WIDTH]
        dga_ref[...] += _colsum(dan * ahat)
        dattn_ref[...] = _norm_bwd(dan, ga_ref[...], ahat, ra).astype(BF16)
        gbv = gb_ref[...].astype(F32)
        u, u1, u2, y = _conv_parts(gc_ref[...], xin_ref[...], gch_ref[...], xinh_ref[...], cw_ref[...], first)
        conv = gbv * y
        rc = _rms(conv)
        chat = conv * rc
        dcn = dmixed[:, Q_WIDTH:]
        dgcn_ref[...] += _colsum(dcn * chat)
        dconv = _norm_bwd(dcn, gcn_ref[...], chat, rc)
        dgb_ref[...] = (dconv * y).astype(BF16)
        dy = dconv * gbv
        dy_ref[...] = dy.astype(BF16)
        dcw_ref[0:1, :] += _colsum(dy * u2)
        dcw_ref[1:2, :] += _colsum(dy * u1)
        dcw_ref[2:3, :] += _colsum(dy * u)

    tile = lambda w: pl.BlockSpec((tb, w), lambda j, k: (j * n_k + k, 0))
    halo = lambda w: pl.BlockSpec((HALO, w), lambda j, k: (jnp.maximum((j * n_k + k) * (tb // HALO) - 1, 0), 0))
    whole = lambda shape: pl.BlockSpec(shape, lambda j, k: (0,) * len(shape))
    return _Rider(
        body,
        in_specs=[tile(D_MODEL), tile(Q_WIDTH), tile(CONV_WIDTH), tile(CONV_WIDTH), tile(CONV_WIDTH),
                  halo(CONV_WIDTH), halo(CONV_WIDTH),
                  _resident((CONV_K, CONV_WIDTH)), _resident((1, Q_WIDTH)), _resident((1, CONV_WIDTH)),
                  _resident(w_out.shape)],
        out_specs=[tile(Q_WIDTH), tile(CONV_WIDTH), tile(CONV_WIDTH),
                   whole((1, Q_WIDTH)), whole((1, CONV_WIDTH)), whole((CONV_K, CONV_WIDTH))],
        out_shape=[SDS((seq, Q_WIDTH), BF16), SDS((seq, CONV_WIDTH), BF16), SDS((seq, CONV_WIDTH), BF16),
                   SDS((1, Q_WIDTH), F32), SDS((1, CONV_WIDTH), F32), SDS((CONV_K, CONV_WIDTH), F32)],
        operands=(dmix, attn, gb, gc, xin, gc, xin, conv_w, g_attn, g_conv, w_out))


def _attention_bwd(q, dattn, attn, kd0, kd1, vd0, vd1, sinks, comm=None):
    seq = q.shape[0]
    nb = ATTN_BWD_BLOCKS

    def body(sink_ref, q_ref, do_ref, o_ref, kd0_ref, kd1_ref, vd0_ref, vd1_ref,
             dq_ref, dk0_ref, dk1_ref, dv0_ref, dv1_ref, dsink_ref):
        @pl.when(pl.program_id(0) == 0)
        def _():
            for r in (dk0_ref, dk1_ref, dv0_ref, dv1_ref, dsink_ref):
                r[...] = jnp.zeros_like(r)

        lane = lax.broadcasted_iota(jnp.int32, (1, 128), 1)
        dsink = jnp.zeros((1, 128), F32)
        for b in range(nb):
            i = pl.program_id(0) * nb + b
            rows = slice(QBLOCK * b, QBLOCK * (b + 1))
            valid = _attn_valid(i)
            for kv_head, (k_ref, v_ref, dk_ref, dv_ref) in enumerate(
                    ((kd0_ref, vd0_ref, dk0_ref, dv0_ref), (kd1_ref, vd1_ref, dk1_ref, dv1_ref))):
                kband, prev, own = _band(k_ref, i)
                vband, _, _ = _band(v_ref, i)
                base = 256 * kv_head
                qm = _stack_heads(q_ref[rows, base:base + 128], q_ref[rows, base + 128:base + 256])
                dom = _stack_heads(do_ref[rows, base:base + 128], do_ref[rows, base + 128:base + 256])
                om = _stack_heads(o_ref[rows, base:base + 128], o_ref[rows, base + 128:base + 256])
                s = jnp.where(valid, _dot_nt(qm, kband), NEG_INF)
                p, e_sink, inv_l = _softmax_with_sink(s, _sink_column(sink_ref, kv_head))
                p = p * inv_l
                delta = jnp.sum(dom.astype(F32) * om.astype(F32), axis=-1, keepdims=True)
                ds = (p * (_dot_nt(dom, vband) - delta)).astype(BF16)
                sink_term = -(e_sink * inv_l) * delta
                for j in range(4):
                    part = jnp.sum(sink_term[QBLOCK * j:QBLOCK * (j + 1)], axis=0, keepdims=True)
                    dsink = dsink + jnp.where(lane == 4 * kv_head + j, part, 0.0)
                pair0, pair1 = _unstack_heads(_dot(ds, kband))
                dq_ref[rows, base:base + 128] = pair0.astype(BF16)
                dq_ref[rows, base + 128:base + 256] = pair1.astype(BF16)
                dkd = _dot_tn(ds, qm)
                dkd = dkd + pltpu.roll(dkd, HEAD_DIM, 1)
                dvd = _dot_tn(p.astype(BF16), dom)
                dvd = dvd + pltpu.roll(dvd, HEAD_DIM, 1)
                dk_ref[pl.ds(prev, QBLOCK), :] += dkd[0:QBLOCK]
                dk_ref[pl.ds(own, QBLOCK), :] += dkd[QBLOCK:]
                dv_ref[pl.ds(prev, QBLOCK), :] += dvd[0:QBLOCK]
                dv_ref[pl.ds(own, QBLOCK), :] += dvd[QBLOCK:]
        dsink_ref[...] += dsink

    blk = pl.BlockSpec((nb * QBLOCK, Q_WIDTH), lambda i: (i, 0))
    full = _resident((seq, 128))
    acc = pl.BlockSpec((seq, 128), lambda i: (0, 0))
    return _pallas(
        body, name="attention_bwd", grid=(seq // (nb * QBLOCK),),
        in_specs=[pl.BlockSpec(memory_space=pltpu.SMEM), blk, blk, blk, full, full, full, full],
        out_specs=[blk, acc, acc, acc, acc, pl.BlockSpec((1, 128), lambda i: (0, 0))],
        out_shape=[SDS((seq, Q_WIDTH), BF16)] + [SDS((seq, 128), F32)] * 4 + [SDS((1, 128), F32)],
        operands=(sinks, q, dattn, attn, kd0, kd1, vd0, vd1), comm=comm)


def _in_proj_bwd(dq, dk0, dk1, dv0, dv1, dgb, dy, gc, xin, conv_w, x, dh, g_pre, w_in_t, rope):
    seq = x.shape[0]
    tb = min(seq, WIDE_TOKEN_TILE)
    n_tiles = seq // tb

    def body(dq_ref, dk0_ref, dk1_ref, dv0_ref, dv1_ref, dgb_ref, dy_ref, dyh_ref, gc_ref, xin_ref, cw_ref,
             x_ref, dh_ref, g_ref, w_ref, c_ref, sa_ref, sb_ref,
             dproj_ref, gx_ref, dg_ref):
        i = pl.program_id(0)

        @pl.when(i == 0)
        def _():
            dg_ref[...] = jnp.zeros_like(dg_ref)

        dy = dy_ref[...].astype(F32)
        ext = jnp.concatenate([dy, jnp.where(i == n_tiles - 1, 0.0, dyh_ref[...].astype(F32))], axis=0)
        dy1 = pltpu.roll(ext, tb + HALO - 1, 0)[0:tb]
        dy2 = pltpu.roll(ext, tb + HALO - 2, 0)[0:tb]
        cw = cw_ref[...]
        du = cw[2:3, :] * dy + cw[1:2, :] * dy1 + cw[0:1, :] * dy2
        scale = 1.0 / math.sqrt(HEAD_DIM)
        base = Q_WIDTH + 2 * KV_WIDTH
        halves = [slice(0, tb // 2), slice(tb // 2, tb)]
        low = _lane_lt64((tb // 2, 128))
        for rows in halves:
            c, sa, sb = _rope_tile(c_ref.at[rows, :], sa_ref, sb_ref)
            for p in range(Q_WIDTH // 128):
                dproj_ref[rows, 128 * p:128 * (p + 1)] = _rope_transposed(
                    dq_ref[rows, 128 * p:128 * (p + 1)].astype(F32) * scale, c, sa, sb).astype(BF16)
            dk = jnp.where(low, dk0_ref[rows, :], dk1_ref[rows, :])
            dproj_ref[rows, Q_WIDTH:Q_WIDTH + KV_WIDTH] = _rope_transposed(dk, c, sa, sb).astype(BF16)
            dproj_ref[rows, Q_WIDTH + KV_WIDTH:base] = jnp.where(low, dv0_ref[rows, :], dv1_ref[rows, :]).astype(BF16)
            dproj_ref[rows, base:base + CONV_WIDTH] = dgb_ref[rows, :]
            dproj_ref[rows, base + CONV_WIDTH:base + 2 * CONV_WIDTH] = (du[rows] * xin_ref[rows, :].astype(F32)).astype(BF16)
            dproj_ref[rows, base + 2 * CONV_WIDTH:] = (du[rows] * gc_ref[rows, :].astype(F32)).astype(BF16)
        w_all = w_ref[...].reshape(IN_COLS, D_MODEL)
        dhn = [_dot(dproj_ref[rows, :], w_all) for rows in halves]
        dg = jnp.zeros((1, D_MODEL), F32)
        for k, rows in enumerate(halves):
            xv = x_ref[rows, :]
            r = _rms(xv)
            xhat = xv * r
            dg = dg + _colsum(dhn[k] * xhat)
            gx_ref[rows, :] = dh_ref[rows, :].astype(F32) + _norm_bwd(dhn[k], g_ref[...], xhat, r)
        dg_ref[...] += dg

    tile = lambda w: pl.BlockSpec((tb, w), lambda i: (i, 0))
    halo_next = pl.BlockSpec((HALO, CONV_WIDTH), lambda i: (jnp.minimum((i + 1) * (tb // HALO), seq // HALO - 1), 0))
    return _pallas(
        body, name="in_proj_bwd", grid=(n_tiles,),
        in_specs=[tile(Q_WIDTH), tile(128), tile(128), tile(128), tile(128), tile(CONV_WIDTH), tile(CONV_WIDTH), halo_next,
                  tile(CONV_WIDTH), tile(CONV_WIDTH), _resident((CONV_K, CONV_WIDTH)),
                  tile(D_MODEL), tile(D_MODEL), _resident((1, D_MODEL)), _resident(w_in_t.shape), *_rope_specs(tb)],
        out_specs=[tile(IN_COLS), tile(D_MODEL), pl.BlockSpec((1, D_MODEL), lambda i: (0, 0))],
        out_shape=[SDS((seq, IN_COLS), BF16), SDS((seq, D_MODEL), F32), SDS((1, D_MODEL), F32)],
        operands=(dq, dk0, dk1, dv0, dv1, dgb, dy, dy, gc, xin, conv_w, x, dh, g_pre, w_in_t, *rope))


def _wgrad_grid(seq, per_chip, h_rows, with_rider=False):
    chips_per_step = 1 if per_chip else N_CHIPS
    m = chips_per_step * 2 * h_rows
    bt = min(seq, WGRAD_TOKEN_TILE if per_chip and not with_rider else WGRAD_TOKEN_TILE // 2)
    return chips_per_step, m, bt, seq // bt


def _wgrad(name, a, b, *, per_chip, h_rows, square_a=False, comm=None, rider=None):
    seq = a.shape[0]
    chips_per_step, m, bt, n_k = _wgrad_grid(seq, per_chip, h_rows, rider is not None)
    a_cols = m if per_chip else a.shape[1]
    a_wide = a.shape[1] > a_cols
    b_wide = b.shape[1] > D_MODEL

    def body(a_ref, b_ref, g_ref):
        @pl.when(pl.program_id(1) == 0)
        def _():
            g_ref[...] = jnp.zeros_like(g_ref)

        av = a_ref[...]
        if square_a:
            av = (av.astype(F32) * av.astype(F32)).astype(BF16)
        g_ref[...] += _dot_tn(av, b_ref[...]).reshape(g_ref.shape)

    a_spec = pl.BlockSpec((bt, a_cols), (lambda j, k: (k, j)) if a_wide else (lambda j, k: (k, 0)))
    b_spec = pl.BlockSpec((bt, D_MODEL), (lambda j, k: (k, j)) if b_wide else (lambda j, k: (k, 0)))
    g_spec = pl.BlockSpec((chips_per_step, 2, h_rows, D_MODEL), lambda j, k: (j, 0, 0, 0),
                          pipeline_mode=None if per_chip else pl.Buffered(1))
    return _pallas(
        body, name=name, grid=(N_CHIPS if per_chip else 1, n_k),
        in_specs=[a_spec, b_spec], out_specs=[g_spec], out_shape=[SDS((N_CHIPS, 2, h_rows, D_MODEL), F32)],
        operands=(a, b), comm=comm, rider=rider)


def _adamw_math(w, g, m, v):
    m = ADAM_B1 * m + (1.0 - ADAM_B1) * g
    v = ADAM_B2 * v + (1.0 - ADAM_B2) * (g * g)
    m_hat = m / (1.0 - ADAM_B1 ** ADAM_STEP)
    v_hat = v / (1.0 - ADAM_B2 ** ADAM_STEP)
    delta = -ADAM_LR * (m_hat / (jnp.sqrt(v_hat) + ADAM_EPS) + ADAM_WD * w)
    return delta, m, v


def _adamw_rows(name, reduced, w, m, v, rt):
    per_half = reduced.shape[1] // rt

    def body(r_ref, w_ref, m_ref, v_ref, d_out, m_out, v_out):
        d_out[...], m_out[...], v_out[...] = _adamw_math(w_ref[...], r_ref[0], m_ref[...], v_ref[...])

    blk = pl.BlockSpec((rt, D_MODEL), lambda h, r: (h * per_half + r, 0))
    d, new_m, new_v = _pallas(
        body, name=name, grid=(2, per_half),
        in_specs=[pl.BlockSpec((1, rt, D_MODEL), lambda h, r: (h, r, 0)), blk, blk, blk],
        out_specs=[blk, blk, blk], out_shape=[SDS(w.shape, F32)] * 3, operands=(reduced, w, m, v))
    return reduced.reshape(w.shape), d, new_m, new_v


def _adamw_small(packed_grads, w, m, v):
    names = SMALL_NAMES
    n = len(names)
    conv_local = w["conv_w"].shape[-1]

    def body(*refs):
        gp = refs[0]
        w_refs, m_refs, v_refs = refs[1:1 + n], refs[1 + n:1 + 2 * n], refs[1 + 2 * n:1 + 3 * n]
        outs = refs[1 + 3 * n:]
        g_out, d_out, m_out, v_out = outs[0:n], outs[n:2 * n], outs[2 * n:3 * n], outs[3 * n:4 * n]
        chip = 2 * lax.axis_index("x") + lax.axis_index("y")

        def step(k, g, index=None):
            pick = (lambda r: r[...]) if index is None else (lambda r: r[index])
            d, new_m, new_v = _adamw_math(pick(w_refs[k]), g, pick(m_refs[k]), pick(v_refs[k]))
            for ref, val in ((g_out[k], g), (d_out[k], d), (m_out[k], new_m), (v_out[k], new_v)):
                if index is None:
                    ref[...] = val
                else:
                    ref[index] = val

        for k, name in enumerate(names):
            if name in SMALL_VECTORS:
                step(k, gp[SMALL_VECTORS.index(name):SMALL_VECTORS.index(name) + 1, :])
            elif name == "attn_group_norm":
                step(k, gp[4:5, 0:Q_WIDTH])
            elif name == "conv_group_norm":
                step(k, gp[4:5, Q_WIDTH:])
            elif name == "attn_sinks":
                step(k, gp[7:8, 0:8])
            else:
                for t in range(CONV_K):
                    row, base = 5 + t // 2, CONV_WIDTH * (t % 2)
                    g = gp[row:row + 1, base:base + conv_local]
                    for j in range(1, CONV_WIDTH // conv_local):
                        g = jnp.where(chip == j, gp[row:row + 1, base + conv_local * j:base + conv_local * (j + 1)], g)
                    step(k, g, index=(0, slice(t, t + 1), slice(None)))

    shapes = [SDS(w[name].shape, F32) for name in names]
    res = pl.pallas_call(
        body, name="adamw_small", in_specs=[VMEM_WHOLE] * (1 + 3 * n), out_specs=[VMEM_WHOLE] * (4 * n),
        out_shape=shapes * 4,
    )(packed_grads, *[w[k] for k in names], *[m[k] for k in names], *[v[k] for k in names])
    return [dict(zip(names, res[i * n:(i + 1) * n])) for i in range(4)]


SMALL_VECTORS = ("pre_mix_norm", "post_mix_norm", "pre_mlp_norm", "post_mlp_norm")
SMALL_NAMES = SMALL_VECTORS + ("attn_group_norm", "conv_group_norm", "conv_w", "attn_sinks")


def _pack_small(p):
    rows = [p[n].reshape(1, D_MODEL) for n in SMALL_VECTORS]
    rows.append(jnp.concatenate([p["attn_group_norm"].reshape(1, -1), p["conv_group_norm"].reshape(1, -1)], axis=1))
    cw = p["conv_w"].reshape(CONV_K, -1)
    rows.append(jnp.pad(cw, ((0, 1), (0, CONV_WIDTH - cw.shape[1]))).reshape(2, D_MODEL))
    last = jnp.concatenate([p["attn_sinks"].reshape(1, 8), p.get("loss_sum", jnp.zeros((1, 1), F32))], axis=1)
    rows.append(jnp.pad(last, ((0, 0), (0, D_MODEL - 9))))
    return jnp.concatenate(rows, axis=0)


WEIGHT_ORDER = ("pre_mix_norm", "w_in", "conv_w", "attn_sinks", "attn_group_norm", "conv_group_norm", "w_out",
                "post_mix_norm", "pre_mlp_norm", "w_up", "w_down", "post_mlp_norm")


def kernel(x, pre_mix_norm, w_in, conv_w, attn_sinks, attn_group_norm, conv_group_norm, w_out, post_mix_norm, pre_mlp_norm, w_up, w_down, post_mlp_norm, loss_target, m_pre_mix_norm, m_w_in, m_conv_w, m_attn_sinks, m_attn_group_norm, m_conv_group_norm, m_w_out, m_post_mix_norm, m_pre_mlp_norm, m_w_up, m_w_down, m_post_mlp_norm, v_pre_mix_norm, v_w_in, v_conv_w, v_attn_sinks, v_attn_group_norm, v_conv_group_norm, v_w_out, v_post_mix_norm, v_pre_mlp_norm, v_w_up, v_w_down, v_post_mlp_norm):
    w = dict(pre_mix_norm=pre_mix_norm, w_in=w_in, conv_w=conv_w, attn_sinks=attn_sinks, attn_group_norm=attn_group_norm,
             conv_group_norm=conv_group_norm, w_out=w_out, post_mix_norm=post_mix_norm, pre_mlp_norm=pre_mlp_norm,
             w_up=w_up, w_down=w_down, post_mlp_norm=post_mlp_norm)
    m = dict(pre_mix_norm=m_pre_mix_norm, w_in=m_w_in, conv_w=m_conv_w, attn_sinks=m_attn_sinks,
             attn_group_norm=m_attn_group_norm, conv_group_norm=m_conv_group_norm, w_out=m_w_out,
             post_mix_norm=m_post_mix_norm, pre_mlp_norm=m_pre_mlp_norm, w_up=m_w_up, w_down=m_w_down,
             post_mlp_norm=m_post_mlp_norm)
    v = dict(pre_mix_norm=v_pre_mix_norm, w_in=v_w_in, conv_w=v_conv_w, attn_sinks=v_attn_sinks,
             attn_group_norm=v_attn_group_norm, conv_group_norm=v_conv_group_norm, w_out=v_w_out,
             post_mix_norm=v_post_mix_norm, pre_mlp_norm=v_pre_mlp_norm, w_up=v_w_up, w_down=v_w_down,
             post_mlp_norm=v_post_mlp_norm)
    core = lax.axis_index("c").astype(jnp.int32).reshape(1)
    xs, target = x[0], loss_target[0]
    rope = _rope_inputs(xs.shape[0])

    hb_up, hb_down, hb_out, hb_in = _cast_halves(core, w_up[0], w_down[0], w_out[0], w_in[0].T)
    conv_pad = jnp.pad(conv_w[0], ((0, 8 - CONV_K), (0, 0)))
    wf_in, conv_all = _gather_whole(hb_in, conv_pad)
    conv_full = conv_all[:, :CONV_K, :].transpose(1, 0, 2).reshape(CONV_K, CONV_WIDTH)

    whole_up, early, late = (0, H_UP), (0, DOWN_EARLY_ROWS), (DOWN_EARLY_ROWS, H_DOWN - DOWN_EARLY_ROWS)
    *proj, wf_up, wf_out, wf_down = _in_proj(
        xs, pre_mix_norm, wf_in, rope,
        comm=_merge(_relay(hb_up, None, first=whole_up), _gather_first(hb_out), _relay(hb_down, None, first=early)))
    q, kd0, kd1, vd0, vd1, gb, gc, xin, hn = proj
    attn, wf_up, wf_out, wf_down = _attention_fwd(
        q, kd0, kd1, vd0, vd1, attn_sinks,
        comm=_merge(_relay(None, wf_up, second=whole_up), _gather_second(wf_out),
                    _relay(hb_down, wf_down, first=late, second=early)))
    mix, mixed, wf_up, wf_down = _mix_out(
        attn, gb, gc, xin, conv_full, attn_group_norm, conv_group_norm, wf_out,
        comm=_merge(_relay(None, wf_up, third=whole_up), _relay(None, wf_down, second=late, third=early, third_after=late)))
    up, hn2, dout, dmlp, loss_sum, dg_post_mlp = _mlp_loss(xs, mix, target, post_mix_norm, pre_mlp_norm, post_mlp_norm,
                                                           wf_up, wf_down)

    dup, dh, dmix, dg_pre_mlp, dg_post_mix = _mlp_bwd(dmlp, up, xs, dout, mix, pre_mlp_norm, post_mix_norm, wf_up, wf_down)
    n_k = _wgrad_grid(xs.shape[0], True, H_DOWN, with_rider=True)[3]
    g_down, dattn, dgb, dy, dg_attn, dg_conv, dconv_w = _wgrad(
        "wgrad_down", up, dmlp, per_chip=True, h_rows=H_DOWN, square_a=True,
        rider=_mix_bwd(dmix, attn, gb, gc, xin, conv_full, attn_group_norm, conv_group_norm, wf_out, n_k))
    g_up, got_down = _wgrad("wgrad_up", hn2, dup, per_chip=True, h_rows=H_UP, comm=_pair_send(g_down))
    p_down = _pair_sum("pair_sum_down", core, g_down, got_down)
    g_out, got_up = _wgrad("wgrad_out", mixed, dmix, per_chip=False, h_rows=H_OUT, comm=_pair_send(g_up))
    p_up = _pair_sum("pair_sum_up", core, g_up, got_up)
    dq, dk0, dk1, dv0, dv1, dsink, ex_down, ex_up, got_out = _attention_bwd(
        q, dattn, attn, kd0, kd1, vd0, vd1, attn_sinks,
        comm=_merge(_chip_exchange(p_down), _chip_exchange(p_up), _pair_send(g_out)))
    p_out = _pair_sum("pair_sum_out", core, g_out, got_out)
    dproj, grad_x, dg_pre_mix = _in_proj_bwd(dq, dk0, dk1, dv0, dv1, dgb, dy, gc, xin, conv_full, xs, dh, pre_mix_norm,
                                             wf_in, rope)
    g_in, ex_out = _wgrad("wgrad_in", dproj, hn, per_chip=False, h_rows=H_IN, comm=_chip_exchange(p_out))
    small = dict(pre_mix_norm=dg_pre_mix, conv_w=dconv_w, attn_sinks=dsink[:, :8], attn_group_norm=dg_attn,
                 conv_group_norm=dg_conv, post_mix_norm=dg_post_mix, pre_mlp_norm=dg_pre_mlp, post_mlp_norm=dg_post_mlp,
                 loss_sum=loss_sum)
    r_down, r_up, r_out, r_in, small_total = _tail_reduce(g_in, [ex_down, ex_up, ex_out], _pack_small(small))

    out_g, out_d, out_m, out_v = {}, {}, {}, {}
    out_g["w_up"], out_d["w_up"], out_m["w_up"], out_v["w_up"] = _adamw_rows(
        "adamw_up", r_up, w_up[0], m_w_up[0], v_w_up[0], 256)
    out_g["w_down"], out_d["w_down"], out_m["w_down"], out_v["w_down"] = _adamw_rows(
        "adamw_down", r_down, w_down[0], m_w_down[0], v_w_down[0], 256)
    out_g["w_out"], out_d["w_out"], out_m["w_out"], out_v["w_out"] = _adamw_rows(
        "adamw_out", r_out, w_out[0], m_w_out[0], v_w_out[0], H_OUT)
    in_t = _adamw_rows("adamw_in", r_in, w_in[0].T, m_w_in[0].T, v_w_in[0].T, H_IN)
    out_g["w_in"], out_d["w_in"], out_m["w_in"], out_v["w_in"] = [t.T for t in in_t]

    loss = small_total[7, 8] * (0.5 / D_MODEL)
    for out, part in zip((out_g, out_d, out_m, out_v), _adamw_small(small_total, w, m, v)):
        out.update(part)

    def shaped(d):
        return [d[n].reshape(w[n].shape) for n in WEIGHT_ORDER]

    return (loss, grad_x[None], *shaped(out_g), *shaped(out_d), *shaped(out_m), *shaped(out_v))
```

```python
import math
from typing import Callable, NamedTuple

import jax
import jax.numpy as jnp
import numpy as np
from jax import lax
from jax.experimental import pallas as pl
from jax.experimental.pallas import tpu as pltpu

F32 = jnp.float32
BF16 = jnp.bfloat16

D_MODEL = 1024
HEAD_DIM = 64
Q_WIDTH = 512
KV_WIDTH = 128
CONV_WIDTH = 512
CONV_K = 3
D_FF = 4096
IN_COLS = 2304
QBLOCK = 128
ROT_DIM = 16
ROPE_THETA = 500000.0
NORM_EPS = 1e-6
NEG_INF = -1e30
N_CHIPS = 4

ADAM_LR = 0.001
ADAM_B1 = 0.9
ADAM_B2 = 0.999
ADAM_EPS = 1e-08
ADAM_WD = 0.01
ADAM_STEP = 10

H_UP, H_DOWN, H_OUT, H_IN = 512, 512, 128, 288
DOWN_EARLY_ROWS = 224

TOKEN_TILE = 512
WIDE_TOKEN_TILE = 1024
ATTN_FWD_BLOCKS = 16
ATTN_BWD_BLOCKS = 2
WGRAD_TOKEN_TILE = 4096
VMEM_LIMIT_V7X = 60 * 1024 * 1024

MESH = pl.DeviceIdType.MESH
ANY = pl.BlockSpec(memory_space=pl.ANY)
VMEM_WHOLE = pl.BlockSpec(memory_space=pltpu.VMEM)
SDS = jax.ShapeDtypeStruct


def _resident(shape):
    zeros = (0,) * len(shape)
    return pl.BlockSpec(shape, lambda *_: zeros, pipeline_mode=pl.Buffered(1))


def _rms(v):
    return lax.rsqrt(jnp.mean(v * v, axis=-1, keepdims=True) + NORM_EPS)


def _norm_bwd(dy, gain, vhat, rstd):
    t = dy * gain
    return rstd * (t - vhat * jnp.mean(t * vhat, axis=-1, keepdims=True))


def _colsum(v):
    return jnp.sum(v, axis=0, keepdims=True)


def _dot_nt(a, b):
    return lax.dot_general(a, b, (((1,), (1,)), ((), ())), preferred_element_type=F32)


def _dot_tn(a, b):
    return lax.dot_general(a, b, (((0,), (0,)), ((), ())), preferred_element_type=F32)


def _dot(a, b):
    return jnp.dot(a, b, preferred_element_type=F32)


def _chip_block(w_ref, chip):
    both = w_ref[pl.ds(2 * chip, 2)]
    return both.reshape(2 * both.shape[1], both.shape[2])


def _lane_lt64(shape):
    return lax.broadcasted_iota(jnp.int32, shape, 1) < HEAD_DIM


class _Comm(NamedTuple):
    operands: tuple
    out_shapes: tuple
    aliases: dict
    n_remote: int
    n_local: int
    plan: Callable
    after: Callable = None


def _merge(*comms):
    operands, out_shapes, aliases, parts = [], [], {}, []
    n_remote = n_local = 0
    for cm in comms:
        parts.append((len(operands), len(out_shapes), n_remote, n_local, cm))
        for k, v in cm.aliases.items():
            aliases[len(operands) + k] = len(out_shapes) + v
        operands += cm.operands
        out_shapes += cm.out_shapes
        n_remote += cm.n_remote
        n_local += cm.n_local

    def run(which, ins, outs, send, recv, loc):
        sends, recvs, locs = [], [], []
        for i0, o0, r0, l0, cm in parts:
            stage = getattr(cm, which)
            if stage is not None:
                s, r, l = stage(ins[i0:i0 + len(cm.operands)], outs[o0:o0 + len(cm.out_shapes)],
                                lambda k, r0=r0: send(r0 + k), lambda k, r0=r0: recv(r0 + k), lambda k, l0=l0: loc(l0 + k))
                sends, recvs, locs = sends + s, recvs + r, locs + l
        return sends, recvs, locs

    def plan(*args):
        return run("plan", *args)

    def after(*args):
        return run("after", *args)

    return _Comm(tuple(operands), tuple(out_shapes), aliases, n_remote, n_local, plan,
                 after if any(cm.after is not None for cm in comms) else None)


def _sem_scratch(comm):
    return [pltpu.SemaphoreType.DMA((max(comm.n_remote, 1),)), pltpu.SemaphoreType.DMA((max(comm.n_remote, 1),)),
            pltpu.SemaphoreType.DMA((max(comm.n_local, 1),))]


class _Rider(NamedTuple):
    body: Callable
    in_specs: list
    out_specs: list
    out_shape: list
    operands: tuple


def _pallas(body, *, name, grid, in_specs, out_specs, out_shape, operands, scratch=(), comm=None, rider=None):
    params = pltpu.CompilerParams(dimension_semantics=("arbitrary",) * len(grid), vmem_limit_bytes=VMEM_LIMIT_V7X)
    if rider is not None:
        own_in, own_out, ride_in, ride_out = len(in_specs), len(out_specs), len(rider.in_specs), len(rider.out_specs)
        own_body = body

        def body(*refs):
            o0 = own_in + ride_in
            s0 = o0 + own_out + ride_out
            own_body(*refs[:own_in], *refs[o0:o0 + own_out], *refs[s0:])
            first = None
            for axis in range(len(grid)):
                at_start = pl.program_id(axis) == 0
                first = at_start if first is None else jnp.logical_and(first, at_start)
            rider.body(first, *refs[own_in:o0], *refs[o0 + own_out:s0])

        in_specs, out_specs = list(in_specs) + rider.in_specs, list(out_specs) + rider.out_specs
        out_shape, operands = list(out_shape) + rider.out_shape, tuple(operands) + tuple(rider.operands)
    if comm is None:
        return pl.pallas_call(body, name=name, grid=grid, in_specs=in_specs, out_specs=out_specs, out_shape=out_shape,
                              scratch_shapes=list(scratch), compiler_params=params)(*operands)
    n_in, n_out, n_scr = len(in_specs), len(out_specs), len(scratch)
    c_in, c_out = len(comm.operands), len(comm.out_shapes)

    def with_comm(*refs):
        ins, c_ins = refs[:n_in], refs[n_in:n_in + c_in]
        o0 = n_in + c_in
        outs, c_outs = refs[o0:o0 + n_out], refs[o0 + n_out:o0 + n_out + c_out]
        s0 = o0 + n_out + c_out
        scr = refs[s0:s0 + n_scr]
        send_sems, recv_sems, local_sems = refs[s0 + n_scr:]
        first = last = None
        for axis, size in enumerate(grid):
            at_start, at_end = pl.program_id(axis) == 0, pl.program_id(axis) == size - 1
            first = at_start if first is None else jnp.logical_and(first, at_start)
            last = at_end if last is None else jnp.logical_and(last, at_end)

        def copies():
            return comm.plan(c_ins, c_outs, lambda k: send_sems.at[k], lambda k: recv_sems.at[k],
                             lambda k: local_sems.at[k])

        @pl.when(first)
        def _():
            sends, _, locs = copies()
            for cp in sends + locs:
                cp.start()

        body(*ins, *outs, *scr)

        @pl.when(last)
        def _():
            sends, recvs, locs = copies()
            for cp in recvs:
                cp.wait_recv()
            for cp in sends:
                cp.wait_send()
            for cp in locs:
                cp.wait()
            if comm.after is not None:
                sends, recvs, _ = comm.after(c_ins, c_outs, lambda k: send_sems.at[k], lambda k: recv_sems.at[k],
                                             lambda k: local_sems.at[k])
                for cp in sends:
                    cp.start()
                for cp in recvs:
                    cp.wait_recv()
                for cp in sends:
                    cp.wait_send()

    return pl.pallas_call(
        with_comm, name=name, grid=grid,
        in_specs=list(in_specs) + [ANY] * c_in, out_specs=list(out_specs) + [ANY] * c_out,
        out_shape=list(out_shape) + list(comm.out_shapes),
        scratch_shapes=list(scratch) + _sem_scratch(comm),
        input_output_aliases={n_in + k: n_out + v for k, v in comm.aliases.items()},
        compiler_params=params)(*operands, *comm.operands)


def _place():
    return lax.axis_index("x"), lax.axis_index("y"), lax.axis_index("c")


def _other_chips(x, y):
    return [(1 - x, y), (x, 1 - y), (1 - x, 1 - y)]


def _slot(px, py, pc):
    return 4 * px + 2 * py + pc


def _remote(src, dst, send_sem, recv_sem, to):
    return pltpu.make_async_remote_copy(src_ref=src, dst_ref=dst, send_sem=send_sem, recv_sem=recv_sem,
                                        device_id=to, device_id_type=MESH)


def _gather_first(half_block):
    def plan(ins, outs, send, recv, loc):
        (blk,), (full,) = ins, outs
        x, y, c = _place()
        chips = _other_chips(x, y)
        mine = full.at[_slot(x, y, c)]
        sends = [_remote(blk, mine, send(0), recv(0), (x, y, 1 - c))]
        sends += [_remote(blk, mine, send(1 + j), recv(1 + j), (*chip, c)) for j, chip in enumerate(chips)]
        recvs = [_remote(blk, full.at[_slot(x, y, 1 - c)], send(0), recv(0), (x, y, 1 - c))]
        recvs += [_remote(blk, full.at[_slot(*chip, c)], send(1 + j), recv(1 + j), (*chip, c))
                  for j, chip in enumerate(chips)]
        return sends, recvs, [pltpu.make_async_copy(blk, mine, loc(0))]

    return _Comm((half_block,), (SDS((2 * N_CHIPS,) + half_block.shape, half_block.dtype),), {}, 4, 1, plan)


def _gather_second(partly_gathered):
    def plan(ins, outs, send, recv, loc):
        (src,), (full,) = ins, outs
        x, y, c = _place()
        chips = _other_chips(x, y)
        sends = [_remote(src.at[_slot(*chip, c)], full.at[_slot(*chip, c)], send(j), recv(j), (x, y, 1 - c))
                 for j, chip in enumerate(chips)]
        recvs = [_remote(src.at[_slot(*chip, 1 - c)], full.at[_slot(*chip, 1 - c)], send(j), recv(j), (x, y, 1 - c))
                 for j, chip in enumerate(chips)]
        return sends, recvs, []

    return _Comm((partly_gathered,), (SDS(partly_gathered.shape, partly_gathered.dtype),), {0: 0}, 3, 0, plan)


def _relay_pieces(full, rows, x, y, c):
    start, half = rows[0], rows[1] // 2
    upper, lower = pl.ds(start, half), pl.ds(start + half, half)
    diagonal = full.at[_slot(1 - x, 1 - y, c)]
    return [(full.at[_slot(1 - x, y, c), upper], diagonal.at[upper], (x, 1 - y, c)),
            (full.at[_slot(x, 1 - y, c), lower], diagonal.at[lower], (1 - x, y, c))]


def _relay(half_block, so_far, first=None, second=None, third=None, third_after=None):
    has_block, has_buffer = half_block is not None, so_far is not None
    shape = so_far.shape if has_buffer else (2 * N_CHIPS,) + half_block.shape
    dtype = so_far.dtype if has_buffer else half_block.dtype

    def third_leg(rows, k, ins, outs, send, recv):
        src, full = (ins[-1] if has_buffer else outs[0]), outs[0]
        x, y, c = _place()
        span, sibling = pl.ds(*rows), (x, y, 1 - c)
        here, there = _slot(1 - x, 1 - y, c), _slot(1 - x, 1 - y, 1 - c)
        return ([_remote(src.at[here, span], full.at[here, span], send(k), recv(k), sibling)],
                [_remote(src.at[there, span], full.at[there, span], send(k), recv(k), sibling)])

    def plan(ins, outs, send, recv, loc):
        src, full = (ins[-1] if has_buffer else outs[0]), outs[0]
        x, y, c = _place()
        sibling = (x, y, 1 - c)
        sends, recvs, locs = [], [], []
        if first is not None:
            span = pl.ds(*first)
            blk, mine = ins[0].at[span], full.at[_slot(x, y, c), span]
            for k, peer in enumerate([sibling, (1 - x, y, c), (x, 1 - y, c)]):
                sends.append(_remote(blk, mine, send(k), recv(k), peer))
                recvs.append(_remote(blk, full.at[_slot(*peer), span], send(k), recv(k), peer))
            locs.append(pltpu.make_async_copy(blk, mine, loc(0)))
        if second is not None:
            span = pl.ds(*second)
            for k, chip in enumerate([(1 - x, y), (x, 1 - y)]):
                sends.append(_remote(src.at[_slot(*chip, c), span], full.at[_slot(*chip, c), span], send(3 + k), recv(3 + k),
                                     sibling))
                recvs.append(_remote(src.at[_slot(*chip, 1 - c), span], full.at[_slot(*chip, 1 - c), span], send(3 + k),
                                     recv(3 + k), sibling))
            for k, (piece, lands, peer) in enumerate(_relay_pieces(full, second, x, y, c)):
                sends.append(_remote(piece, piece, send(5 + k), recv(5 + k), peer))
                recvs.append(_remote(lands, lands, send(5 + k), recv(5 + k), peer))
        if third is not None:
            s, r = third_leg(third, 7, ins, outs, send, recv)
            sends, recvs = sends + s, recvs + r
        return sends, recvs, locs

    def after(ins, outs, send, recv, loc):
        s, r = third_leg(third_after, 8, ins, outs, send, recv)
        return s, r, []

    operands = ((half_block,) if has_block else ()) + ((so_far,) if has_buffer else ())
    return _Comm(operands, (SDS(shape, dtype),), {len(operands) - 1: 0} if has_buffer else {}, 9, 1, plan,
                 after if third_after is not None else None)


def _gather_whole(half_block, small_block):
    rows = half_block.shape[0]

    def body(blk_ref, small_ref, out_ref, small_out_ref, send_sems, recv_sems, local_sems):
        x, y, c = _place()
        me, sibling = (x, y, c), (x, y, 1 - c)
        neighbours, diagonal = [(1 - x, y), (x, 1 - y)], (1 - x, 1 - y)

        def copy(k, block, to, src=None):
            return _remote(out_ref.at[_slot(*block)] if src is None else src, out_ref.at[_slot(*block)],
                           send_sems.at[k], recv_sems.at[k], to)

        def small_copy(k, chip, to):
            return _remote(small_ref, small_out_ref.at[2 * chip[0] + chip[1]], send_sems.at[8 + k], recv_sems.at[8 + k], to)

        mine = pltpu.make_async_copy(blk_ref, out_ref.at[_slot(*me)], local_sems.at[0])
        mine_small = pltpu.make_async_copy(small_ref, small_out_ref.at[2 * x + y], local_sems.at[1])
        mine.start()
        mine_small.start()
        started = [copy(0, me, sibling, src=blk_ref)]
        started += [copy(1 + k, me, (*chip, c), src=blk_ref) for k, chip in enumerate(neighbours)]
        started += [small_copy(k, (x, y), (*chip, c)) for k, chip in enumerate(neighbours + [diagonal])]
        for cp in started:
            cp.start()
        pieces = _relay_pieces(out_ref, (0, rows), x, y, c)
        for k, chip in enumerate(neighbours):
            copy(1 + k, (*chip, c), me).wait_recv()
            piece, _, peer = pieces[k]
            started += [copy(3 + k, (*chip, c), sibling), _remote(piece, piece, send_sems.at[5 + k], recv_sems.at[5 + k], peer)]
            started[-2].start()
            started[-1].start()
        for k, (_, lands, peer) in enumerate(pieces):
            _remote(lands, lands, send_sems.at[5 + k], recv_sems.at[5 + k], peer).wait_recv()
        started.append(copy(7, (*diagonal, c), sibling))
        started[-1].start()
        copy(0, sibling, me).wait_recv()
        for k, chip in enumerate(neighbours):
            copy(3 + k, (*chip, 1 - c), me).wait_recv()
        copy(7, (*diagonal, 1 - c), me).wait_recv()
        for k, chip in enumerate(neighbours + [diagonal]):
            small_copy(k, chip, me).wait_recv()
        for cp in started:
            cp.wait_send()
        mine.wait()
        mine_small.wait()

    return pl.pallas_call(
        body, name="gather_whole", in_specs=[ANY, ANY], out_specs=[ANY, ANY],
        out_shape=[SDS((2 * N_CHIPS,) + half_block.shape, half_block.dtype),
                   SDS((N_CHIPS,) + small_block.shape, small_block.dtype)],
        scratch_shapes=[pltpu.SemaphoreType.DMA((11,)), pltpu.SemaphoreType.DMA((11,)), pltpu.SemaphoreType.DMA((2,))],
    )(half_block, small_block)


def _pair_send(grads):
    def plan(ins, outs, send, recv, loc):
        (g,), (got,) = ins, outs
        x, y, c = _place()
        copies = [_remote(g.at[j, 1 - c], got.at[j], send(j), recv(j), (x, y, 1 - c)) for j in range(N_CHIPS)]
        return copies, copies, []

    shape = (grads.shape[0],) + grads.shape[2:]
    return _Comm((grads,), (SDS(shape, grads.dtype),), {}, N_CHIPS, 0, plan)


def _chip_exchange(partial):
    def plan(ins, outs, send, recv, loc):
        (p,), (got,) = ins, outs
        x, y, c = _place()
        my_chip = 2 * x + y
        chips = _other_chips(x, y)
        sends = [_remote(p.at[2 * chip[0] + chip[1]], got.at[my_chip], send(j), recv(j), (*chip, c))
                 for j, chip in enumerate(chips)]
        recvs = [_remote(p.at[my_chip], got.at[2 * chip[0] + chip[1]], send(j), recv(j), (*chip, c))
                 for j, chip in enumerate(chips)]
        return sends, recvs, [pltpu.make_async_copy(p.at[my_chip], got.at[my_chip], loc(0))]

    return _Comm((partial,), (SDS(partial.shape, partial.dtype),), {}, 3, 1, plan)


def _pair_sum(name, core, grads, received):
    h = grads.shape[2]

    def body(core_ref, g_ref, r_ref, o_ref):
        o_ref[...] = (g_ref[0] + r_ref[...]).astype(BF16)

    return pl.pallas_call(
        body, name=name,
        grid_spec=pltpu.PrefetchScalarGridSpec(
            num_scalar_prefetch=1, grid=(N_CHIPS,),
            in_specs=[pl.BlockSpec((1, 1, h, D_MODEL), lambda j, core_ref: (j, core_ref[0], 0, 0)),
                      pl.BlockSpec((1, h, D_MODEL), lambda j, core_ref: (j, 0, 0))],
            out_specs=pl.BlockSpec((1, h, D_MODEL), lambda j, core_ref: (j, 0, 0))),
        out_shape=SDS((N_CHIPS, h, D_MODEL), BF16),
        compiler_params=pltpu.CompilerParams(dimension_semantics=("arbitrary",), vmem_limit_bytes=VMEM_LIMIT_V7X),
    )(core, grads, received)


SMALL_ROWS = 8


def _sum_blocks(ref):
    return (ref[0].astype(F32) + ref[1].astype(F32)) + (ref[2].astype(F32) + ref[3].astype(F32))


def _tail_reduce(last_grads, exchanged, small):
    n = len(exchanged)
    h = last_grads.shape[2]

    def body(*refs):
        g_ref, ex, small_ref = refs[0], refs[1:1 + n], refs[1 + n]
        o0 = 2 + n
        out, out_last, small_out = refs[o0:o0 + n], refs[o0 + n], refs[o0 + n + 1]
        s0 = o0 + n + 2
        halves, half_last = refs[s0:s0 + n], refs[s0 + n]
        own, got, part, exch, small_buf = refs[s0 + n + 1:s0 + n + 6]
        pair_send, pair_recv, chip_send, chip_recv, share_send, share_recv, small_send, small_recv, local_sems = refs[s0 + n + 6:]
        x, y, c = _place()
        sibling = (x, y, 1 - c)
        my_chip, me = 2 * x + y, _slot(x, y, c)
        chips = _other_chips(x, y)

        to_sibling = [_remote(g_ref.at[j, 1 - c], got.at[j], pair_send.at[j], pair_recv.at[j], sibling)
                      for j in range(N_CHIPS)]
        load_own = [pltpu.make_async_copy(g_ref.at[j, c], own.at[j], local_sems.at[j]) for j in range(N_CHIPS)]
        for cp in to_sibling + load_own:
            cp.start()

        small_buf[me] = small_ref[...]
        small_copies = []
        for mask in range(1, 8):
            peer = (x ^ (mask >> 2), y ^ ((mask >> 1) & 1), c ^ (mask & 1))
            small_copies.append(_remote(small_ref, small_buf.at[me], small_send.at[mask - 1], small_recv.at[mask - 1], peer))
        for cp in small_copies:
            cp.start()

        def share(k, half_ref, out_ref):
            keep = pltpu.make_async_copy(half_ref, out_ref.at[c], local_sems.at[N_CHIPS + k])
            give = _remote(half_ref, out_ref.at[c], share_send.at[k], share_recv.at[k], sibling)
            take = _remote(half_ref, out_ref.at[1 - c], share_send.at[k], share_recv.at[k], sibling)
            keep.start()
            give.start()
            return keep, give, take

        shares = []
        for k in range(n):
            halves[k][...] = _sum_blocks(ex[k])
            shares.append(share(k, halves[k], out[k]))

        def pair_sum(block):
            _remote(g_ref.at[block, 1 - c], got.at[block], pair_send.at[block], pair_recv.at[block], sibling).wait_recv()
            pltpu.make_async_copy(g_ref.at[block, c], own.at[block], local_sems.at[block]).wait()
            part[block] = (own[block] + got[block]).astype(BF16)

        to_chips = []
        for j, chip in enumerate(chips):
            block = 2 * chip[0] + chip[1]
            pair_sum(block)
            to_chips.append(_remote(part.at[block], exch.at[my_chip], chip_send.at[j], chip_recv.at[j], (*chip, c)))
            to_chips[-1].start()
        pair_sum(my_chip)
        exch[my_chip] = part[my_chip]
        from_chips = [_remote(part.at[my_chip], exch.at[2 * chip[0] + chip[1]], chip_send.at[j], chip_recv.at[j], (*chip, c))
                      for j, chip in enumerate(chips)]

        for cp in small_copies:
            cp.wait_recv()
        total = small_buf[0]
        for d in range(1, 8):
            total = total + small_buf[d]
        small_out[...] = total

        for cp in from_chips:
            cp.wait_recv()
        half_last[...] = _sum_blocks(exch)
        shares.append(share(n, half_last, out_last))

        for keep, give, take in shares:
            take.wait_recv()
            give.wait_send()
            keep.wait()
        for cp in to_sibling + to_chips + small_copies:
            cp.wait_send()

    blocks = (N_CHIPS, h, D_MODEL)
    return pl.pallas_call(
        body, name="tail_reduce",
        in_specs=[ANY] + [VMEM_WHOLE] * (n + 1), out_specs=[ANY] * (n + 1) + [VMEM_WHOLE],
        out_shape=[SDS((2,) + e.shape[1:], F32) for e in exchanged] + [SDS((2, h, D_MODEL), F32), SDS(small.shape, F32)],
        scratch_shapes=[pltpu.VMEM(e.shape[1:], F32) for e in exchanged] + [pltpu.VMEM((h, D_MODEL), F32)]
                       + [pltpu.VMEM(blocks, F32), pltpu.VMEM(blocks, F32), pltpu.VMEM(blocks, BF16), pltpu.VMEM(blocks, BF16),
                          pltpu.VMEM((8,) + small.shape, F32)]
                       + [pltpu.SemaphoreType.DMA((N_CHIPS,)), pltpu.SemaphoreType.DMA((N_CHIPS,)),
                          pltpu.SemaphoreType.DMA((3,)), pltpu.SemaphoreType.DMA((3,)),
                          pltpu.SemaphoreType.DMA((n + 1,)), pltpu.SemaphoreType.DMA((n + 1,)),
                          pltpu.SemaphoreType.DMA((7,)), pltpu.SemaphoreType.DMA((7,)),
                          pltpu.SemaphoreType.DMA((N_CHIPS + n + 1,))],
        compiler_params=pltpu.CompilerParams(vmem_limit_bytes=VMEM_LIMIT_V7X),
    )(last_grads, *exchanged, small)


def _rope_expansion():
    half = ROT_DIM // 2
    expand = np.zeros((2 * half, 3 * 128), np.float32)
    const = np.zeros((1, 3 * 128), np.float32)
    for lane in range(128):
        d = lane % HEAD_DIM
        if d < ROT_DIM:
            expand[d % half, lane] = 1.0
        else:
            const[0, lane] = 1.0
        if d < half:
            expand[half + d, 128 + lane] = -1.0
        elif d < ROT_DIM:
            expand[half + d - half, 256 + lane] = 1.0
    return expand, const


ROPE_PIECES = 3 * ROT_DIM


def _rope_inputs(seq):
    pos = jnp.arange(seq, dtype=F32)
    inv_freq = ROPE_THETA ** (-jnp.arange(0, ROT_DIM, 2, dtype=F32) / ROT_DIM)
    ang = pos[:, None] * inv_freq[None, :]
    cs = jnp.concatenate([jnp.cos(ang), jnp.sin(ang)], axis=1)
    hi = lax.reduce_precision(cs, 8, 7)
    mid = lax.reduce_precision(cs - hi, 8, 7)
    low = cs - hi - mid
    expand, const = _rope_expansion()
    pieces = jnp.concatenate([hi, mid, low], axis=1).astype(BF16)
    return pieces, jnp.asarray(np.concatenate([expand] * 3, axis=0), BF16), jnp.asarray(const)


def _rope_specs(tb):
    return [pl.BlockSpec((tb, ROPE_PIECES), lambda i: (i, 0)), _resident((ROPE_PIECES, 3 * 128)), _resident((1, 3 * 128))]


def _rope_tile(pieces_ref, expand_ref, const_ref):
    tables = _dot(pieces_ref[...], expand_ref[...]) + const_ref[...]
    return tables[:, 0:128], tables[:, 128:256], tables[:, 256:384]


def _rope(t, c, sa, sb):
    half = ROT_DIM // 2
    return t * c + pltpu.roll(t, 128 - half, 1) * sa + pltpu.roll(t, half, 1) * sb


def _rope_transposed(dt, c, sa, sb):
    half = ROT_DIM // 2
    return dt * c + pltpu.roll(dt * sa, half, 1) + pltpu.roll(dt * sb, 128 - half, 1)


def _cast_halves(core, w_up, w_down, w_out, w_in_t):
    def body(core_ref, up_ref, down_ref, out_ref, in_ref, up_o, down_o, out_o, in_o):
        up_o[...] = up_ref[...].astype(BF16)
        down_o[...] = down_ref[...].astype(BF16)
        out_o[...] = out_ref[...].astype(BF16)
        in_o[...] = in_ref[...].astype(BF16)

    half = lambda rows: pl.BlockSpec((rows, D_MODEL), lambda i, core_ref: (core_ref[0], 0))
    whole = lambda rows: pl.BlockSpec((rows, D_MODEL), lambda i, core_ref: (0, 0))
    rows = (H_UP, H_DOWN, H_OUT, H_IN)
    return pl.pallas_call(
        body, name="cast_halves",
        grid_spec=pltpu.PrefetchScalarGridSpec(
            num_scalar_prefetch=1, grid=(1,), in_specs=[half(r) for r in rows], out_specs=[whole(r) for r in rows]),
        out_shape=[SDS((r, D_MODEL), BF16) for r in rows],
        compiler_params=pltpu.CompilerParams(dimension_semantics=("arbitrary",), vmem_limit_bytes=VMEM_LIMIT_V7X),
    )(core, w_up, w_down, w_out, w_in_t)


def _in_proj(x, g_pre, w_in_t, rope, comm=None):
    seq = x.shape[0]
    tb = min(seq, WIDE_TOKEN_TILE)

    def body(x_ref, g_ref, w_ref, c_ref, sa_ref, sb_ref,
             q_ref, kd0_ref, kd1_ref, vd0_ref, vd1_ref, gb_ref, gc_ref, xin_ref, hn_ref):
        xv = x_ref[...]
        hn = (xv * _rms(xv) * g_ref[...]).astype(BF16)
        hn_ref[...] = hn
        proj = _dot_nt(hn, w_ref[...].reshape(IN_COLS, D_MODEL))
        c, sa, sb = _rope_tile(c_ref, sa_ref, sb_ref)
        scale = 1.0 / math.sqrt(HEAD_DIM)
        for p in range(Q_WIDTH // 128):
            q_ref[:, 128 * p:128 * (p + 1)] = (_rope(proj[:, 128 * p:128 * (p + 1)], c, sa, sb) * scale).astype(BF16)
        k = _rope(proj[:, Q_WIDTH:Q_WIDTH + KV_WIDTH], c, sa, sb)
        v = proj[:, Q_WIDTH + KV_WIDTH:Q_WIDTH + 2 * KV_WIDTH]
        low = _lane_lt64(k.shape)
        k_sw, v_sw = pltpu.roll(k, HEAD_DIM, 1), pltpu.roll(v, HEAD_DIM, 1)
        kd0_ref[...] = jnp.where(low, k, k_sw).astype(BF16)
        kd1_ref[...] = jnp.where(low, k_sw, k).astype(BF16)
        vd0_ref[...] = jnp.where(low, v, v_sw).astype(BF16)
        vd1_ref[...] = jnp.where(low, v_sw, v).astype(BF16)
        base = Q_WIDTH + 2 * KV_WIDTH
        gb_ref[...] = proj[:, base:base + CONV_WIDTH].astype(BF16)
        gc_ref[...] = proj[:, base + CONV_WIDTH:base + 2 * CONV_WIDTH].astype(BF16)
        xin_ref[...] = proj[:, base + 2 * CONV_WIDTH:base + 3 * CONV_WIDTH].astype(BF16)

    tile = lambda w: pl.BlockSpec((tb, w), lambda i: (i, 0))
    return _pallas(
        body, name="in_proj", grid=(seq // tb,),
        in_specs=[tile(D_MODEL), _resident((1, D_MODEL)), _resident(w_in_t.shape), *_rope_specs(tb)],
        out_specs=[tile(Q_WIDTH), tile(128), tile(128), tile(128), tile(128),
                   tile(CONV_WIDTH), tile(CONV_WIDTH), tile(CONV_WIDTH), tile(D_MODEL)],
        out_shape=[SDS((seq, Q_WIDTH), BF16)] + [SDS((seq, 128), BF16)] * 4
                  + [SDS((seq, CONV_WIDTH), BF16)] * 3 + [SDS((seq, D_MODEL), BF16)],
        operands=(x, g_pre, w_in_t, *rope), comm=comm)


def _attn_valid(i):
    shape = (4 * QBLOCK, 2 * QBLOCK)
    row = lax.broadcasted_iota(jnp.int32, shape, 0)
    col = lax.broadcasted_iota(jnp.int32, shape, 1)
    qi = row & (QBLOCK - 1)
    return (col > qi) & (col <= qi + QBLOCK) & ((col >= QBLOCK) | (i > 0))


def _stack_heads(pair0, pair1):
    low = _lane_lt64(pair0.shape)
    zero = jnp.zeros_like(pair0)
    return jnp.concatenate([jnp.where(low, pair0, zero), jnp.where(low, zero, pair0),
                            jnp.where(low, pair1, zero), jnp.where(low, zero, pair1)], axis=0)


def _unstack_heads(stacked):
    low = _lane_lt64((QBLOCK, 128))
    pair0 = jnp.where(low, stacked[0:QBLOCK], stacked[QBLOCK:2 * QBLOCK])
    pair1 = jnp.where(low, stacked[2 * QBLOCK:3 * QBLOCK], stacked[3 * QBLOCK:4 * QBLOCK])
    return pair0, pair1


def _sink_column(sink_ref, kv_head):
    row = lax.broadcasted_iota(jnp.int32, (4 * QBLOCK, 1), 0)
    s = [sink_ref[0, 4 * kv_head + j] for j in range(4)]
    return jnp.where(row < QBLOCK, s[0], jnp.where(row < 2 * QBLOCK, s[1], jnp.where(row < 3 * QBLOCK, s[2], s[3])))


def _band(ref, i):
    prev = pl.multiple_of(jnp.maximum(i - 1, 0) * QBLOCK, QBLOCK)
    own = pl.multiple_of(i * QBLOCK, QBLOCK)
    return jnp.concatenate([ref[pl.ds(prev, QBLOCK), :], ref[pl.ds(own, QBLOCK), :]], axis=0), prev, own


def _softmax_with_sink(s, sink_col):
    m = jnp.maximum(jnp.max(s, axis=-1, keepdims=True), sink_col)
    p = jnp.exp(s - m)
    e_sink = jnp.exp(sink_col - m)
    inv_l = 1.0 / (jnp.sum(p, axis=-1, keepdims=True) + e_sink)
    return p, e_sink, inv_l


def _attention_fwd(q, kd0, kd1, vd0, vd1, sinks, comm=None):
    seq = q.shape[0]

    nb = ATTN_FWD_BLOCKS

    def body(sink_ref, q_ref, kd0_ref, kd1_ref, vd0_ref, vd1_ref, o_ref):
        for b in range(nb):
            i = pl.program_id(0) * nb + b
            rows = slice(QBLOCK * b, QBLOCK * (b + 1))
            valid = _attn_valid(i)
            for kv_head, (k_ref, v_ref) in enumerate(((kd0_ref, vd0_ref), (kd1_ref, vd1_ref))):
                kband, _, _ = _band(k_ref, i)
                vband, _, _ = _band(v_ref, i)
                base = 256 * kv_head
                qm = _stack_heads(q_ref[rows, base:base + 128], q_ref[rows, base + 128:base + 256])
                s = jnp.where(valid, _dot_nt(qm, kband), NEG_INF)
                p, _, inv_l = _softmax_with_sink(s, _sink_column(sink_ref, kv_head))
                o = _dot(p.astype(BF16), vband) * inv_l
                pair0, pair1 = _unstack_heads(o)
                o_ref[rows, base:base + 128] = pair0.astype(BF16)
                o_ref[rows, base + 128:base + 256] = pair1.astype(BF16)

    blk = pl.BlockSpec((nb * QBLOCK, Q_WIDTH), lambda i: (i, 0))
    full = _resident((seq, 128))
    return _pallas(
        body, name="attention_fwd", grid=(seq // (nb * QBLOCK),),
        in_specs=[pl.BlockSpec(memory_space=pltpu.SMEM), blk, full, full, full, full],
        out_specs=[blk], out_shape=[SDS((seq, Q_WIDTH), BF16)],
        operands=(sinks, q, kd0, kd1, vd0, vd1), comm=comm)


HALO = 16


def _conv_parts(gc, xin, gc_halo, xin_halo, conv_w, first):
    tb = gc.shape[0]
    u = gc.astype(F32) * xin.astype(F32)
    u_halo = jnp.where(first, 0.0, gc_halo.astype(F32) * xin_halo.astype(F32))
    ext = jnp.concatenate([u_halo, u], axis=0)
    u1 = pltpu.roll(ext, 1, 0)[HALO:HALO + tb]
    u2 = pltpu.roll(ext, 2, 0)[HALO:HALO + tb]
    y = conv_w[0:1, :] * u2 + conv_w[1:2, :] * u1 + conv_w[2:3, :] * u
    return u, u1, u2, y


def _halo_prev(tb, w):
    return pl.BlockSpec((HALO, w), lambda i: (jnp.maximum(i * (tb // HALO) - 1, 0), 0))


def _residual_mid(x, mix, g_post_mix):
    mix_f = mix.astype(F32)
    return x + mix_f * _rms(mix_f) * g_post_mix


def _mix_out(attn, gb, gc, xin, conv_w, g_attn, g_conv, w_out, comm=None):
    seq = attn.shape[0]
    tb = min(seq, WIDE_TOKEN_TILE)

    def body(a_ref, gb_ref, gc_ref, xin_ref, gch_ref, xinh_ref, cw_ref, ga_ref, gcn_ref, w_ref, mix_ref, mixed_ref):
        first = pl.program_id(0) == 0
        _, _, _, y = _conv_parts(gc_ref[...], xin_ref[...], gch_ref[...], xinh_ref[...], cw_ref[...], first)
        conv = gb_ref[...].astype(F32) * y
        a = a_ref[...].astype(F32)
        mixed_ref[:, 0:Q_WIDTH] = (a * _rms(a) * ga_ref[...]).astype(BF16)
        mixed_ref[:, Q_WIDTH:] = (conv * _rms(conv) * gcn_ref[...]).astype(BF16)
        mix_ref[...] = _dot(mixed_ref[...], w_ref[...].reshape(D_MODEL, D_MODEL)).astype(BF16)

    tile = lambda w: pl.BlockSpec((tb, w), lambda i: (i, 0))
    return _pallas(
        body, name="mix_out", grid=(seq // tb,),
        in_specs=[tile(Q_WIDTH), tile(CONV_WIDTH), tile(CONV_WIDTH), tile(CONV_WIDTH),
                  _halo_prev(tb, CONV_WIDTH), _halo_prev(tb, CONV_WIDTH),
                  _resident((CONV_K, CONV_WIDTH)), _resident((1, Q_WIDTH)), _resident((1, CONV_WIDTH)),
                  _resident(w_out.shape)],
        out_specs=[tile(D_MODEL), tile(D_MODEL)],
        out_shape=[SDS((seq, D_MODEL), BF16), SDS((seq, D_MODEL), BF16)],
        operands=(attn, gb, gc, xin, gc, xin, conv_w, g_attn, g_conv, w_out), comm=comm)


def _mlp_fwd_bwd(x, mix, target, g_post_mix, g_pre_mlp, g_post_mlp, w_up, w_down):
    seq = x.shape[0]
    tb = TOKEN_TILE

    def body(x_ref, mix_ref, t_ref, gpm_ref, g2_ref, g4_ref, wup_ref, wdown_ref,
             up_ref, hn2_ref, dmlp_ref, dup_ref, dh_ref, dmix_ref, loss_ref, dg4_ref, dg2_ref, dgpm_ref):
        @pl.when(pl.program_id(0) == 0)
        def _():
            for ref in (loss_ref, dg4_ref, dg2_ref, dgpm_ref):
                ref[...] = jnp.zeros_like(ref)

        halves = [slice(0, tb // 2), slice(tb // 2, tb)]
        chunks = [slice(1024 * j, 1024 * (j + 1)) for j in range(N_CHIPS)]
        hv, hn2, mlp, dout, dmlp, dhn2 = [], [], [], [], [], []
        for rows in halves:
            hv.append(_residual_mid(x_ref[rows, :], mix_ref[rows, :], gpm_ref[...]))
            hn2.append((hv[-1] * _rms(hv[-1]) * g2_ref[...]).astype(BF16))
            hn2_ref[rows, :] = hn2[-1]
        for k, rows in enumerate(halves):
            acc = None
            for j, cols in enumerate(chunks):
                up = jnp.maximum(_dot(hn2[k], _chip_block(wup_ref, j)), 0.0)
                up_ref[rows, cols] = up.astype(BF16)
                part = _dot((up * up).astype(BF16), _chip_block(wdown_ref, j))
                acc = part if acc is None else acc + part
            mlp.append(acc)
        loss = jnp.zeros((1, 1), F32)
        dg4 = jnp.zeros((1, D_MODEL), F32)
        for k, rows in enumerate(halves):
            rstd = _rms(mlp[k])
            zhat = mlp[k] * rstd
            diff = hv[k] + zhat * g4_ref[...] - t_ref[rows, :]
            loss = loss + jnp.sum(jnp.sum(diff * diff, axis=1, keepdims=True), axis=0, keepdims=True)
            dout.append(diff * (1.0 / D_MODEL))
            dg4 = dg4 + _colsum(dout[k] * zhat)
            dmlp.append(_norm_bwd(dout[k], g4_ref[...], zhat, rstd).astype(BF16))
            dmlp_ref[rows, :] = dmlp[k]
        for k, rows in enumerate(halves):
            acc = None
            for j, cols in enumerate(chunks):
                dact = _dot_nt(dmlp[k], _chip_block(wdown_ref, j))
                dup = (dact * (2.0 * up_ref[rows, cols].astype(F32))).astype(BF16)
                dup_ref[rows, cols] = dup
                part = _dot_nt(dup, _chip_block(wup_ref, j))
                acc = part if acc is None else acc + part
            dhn2.append(acc)
        dg2 = jnp.zeros((1, D_MODEL), F32)
        dgpm = jnp.zeros((1, D_MODEL), F32)
        for k, rows in enumerate(halves):
            r2 = _rms(hv[k])
            hhat = hv[k] * r2
            dg2 = dg2 + _colsum(dhn2[k] * hhat)
            dh = dout[k] + _norm_bwd(dhn2[k], g2_ref[...], hhat, r2)
            dh_ref[rows, :] = dh.astype(BF16)
            mix_v = mix_ref[rows, :].astype(F32)
            rz = _rms(mix_v)
            zhat = mix_v * rz
            dgpm = dgpm + _colsum(dh * zhat)
            dmix_ref[rows, :] = _norm_bwd(dh, gpm_ref[...], zhat, rz).astype(BF16)
        loss_ref[...] += loss
        dg4_ref[...] += dg4
        dg2_ref[...] += dg2
        dgpm_ref[...] += dgpm

    tile = lambda w: pl.BlockSpec((tb, w), lambda i: (i, 0))
    vec = pl.BlockSpec((1, D_MODEL), lambda i: (0, 0))
    return _pallas(
        body, name="mlp_fwd_bwd", grid=(seq // tb,),
        in_specs=[tile(D_MODEL), tile(D_MODEL), tile(D_MODEL), _resident((1, D_MODEL)), _resident((1, D_MODEL)),
                  _resident((1, D_MODEL)), _resident(w_up.shape), _resident(w_down.shape)],
        out_specs=[tile(D_FF), tile(D_MODEL), tile(D_MODEL), tile(D_FF), tile(D_MODEL), tile(D_MODEL),
                   pl.BlockSpec((1, 1), lambda i: (0, 0)), vec, vec, vec],
        out_shape=[SDS((seq, D_FF), BF16), SDS((seq, D_MODEL), BF16), SDS((seq, D_MODEL), BF16), SDS((seq, D_FF), BF16),
                   SDS((seq, D_MODEL), BF16), SDS((seq, D_MODEL), BF16),
                   SDS((1, 1), F32), SDS((1, D_MODEL), F32), SDS((1, D_MODEL), F32), SDS((1, D_MODEL), F32)],
        operands=(x, mix, target, g_post_mix, g_pre_mlp, g_post_mlp, w_up, w_down))


def _mix_bwd(dmix, attn, gb, gc, xin, conv_w, g_attn, g_conv, w_out, n_k):
    seq = attn.shape[0]
    tb = seq // (N_CHIPS * n_k)

    def body(first, dmix_ref, a_ref, gb_ref, gc_ref, xin_ref, gch_ref, xinh_ref, cw_ref, ga_ref, gcn_ref, w_ref,
             dattn_ref, dgb_ref, dy_ref, dga_ref, dgcn_ref, dcw_ref):
        @pl.when(first)
        def _():
            dga_ref[...] = jnp.zeros_like(dga_ref)
            dgcn_ref[...] = jnp.zeros_like(dgcn_ref)
            dcw_ref[...] = jnp.zeros_like(dcw_ref)

        dmixed = _dot_nt(dmix_ref[...], w_ref[...].reshape(D_MODEL, D_MODEL))
        a = a_ref[...].astype(F32)
        ra = _rms(a)
        ahat = a * ra
        dan = dmixed[:, 0:Q_WIDTH]
        dga_ref[...] += _colsum(dan * ahat)
        dattn_ref[...] = _norm_bwd(dan, ga_ref[...], ahat, ra).astype(BF16)
        gbv = gb_ref[...].astype(F32)
        u, u1, u2, y = _conv_parts(gc_ref[...], xin_ref[...], gch_ref[...], xinh_ref[...], cw_ref[...], first)
        conv = gbv * y
        rc = _rms(conv)
        chat = conv * rc
        dcn = dmixed[:, Q_WIDTH:]
        dgcn_ref[...] += _colsum(dcn * chat)
        dconv = _norm_bwd(dcn, gcn_ref[...], chat, rc)
        dgb_ref[...] = (dconv * y).astype(BF16)
        dy = dconv * gbv
        dy_ref[...] = dy.astype(BF16)
        dcw_ref[0:1, :] += _colsum(dy * u2)
        dcw_ref[1:2, :] += _colsum(dy * u1)
        dcw_ref[2:3, :] += _colsum(dy * u)

    tile = lambda w: pl.BlockSpec((tb, w), lambda j, k: (j * n_k + k, 0))
    halo = lambda w: pl.BlockSpec((HALO, w), lambda j, k: (jnp.maximum((j * n_k + k) * (tb // HALO) - 1, 0), 0))
    whole = lambda shape: pl.BlockSpec(shape, lambda j, k: (0,) * len(shape))
    return _Rider(
        body,
        in_specs=[tile(D_MODEL), tile(Q_WIDTH), tile(CONV_WIDTH), tile(CONV_WIDTH), tile(CONV_WIDTH),
                  halo(CONV_WIDTH), halo(CONV_WIDTH),
                  _resident((CONV_K, CONV_WIDTH)), _resident((1, Q_WIDTH)), _resident((1, CONV_WIDTH)),
                  _resident(w_out.shape)],
        out_specs=[tile(Q_WIDTH), tile(CONV_WIDTH), tile(CONV_WIDTH),
                   whole((1, Q_WIDTH)), whole((1, CONV_WIDTH)), whole((CONV_K, CONV_WIDTH))],
        out_shape=[SDS((seq, Q_WIDTH), BF16), SDS((seq, CONV_WIDTH), BF16), SDS((seq, CONV_WIDTH), BF16),
                   SDS((1, Q_WIDTH), F32), SDS((1, CONV_WIDTH), F32), SDS((CONV_K, CONV_WIDTH), F32)],
        operands=(dmix, attn, gb, gc, xin, gc, xin, conv_w, g_attn, g_conv, w_out))


def _attention_bwd(q, dattn, attn, kd0, kd1, vd0, vd1, sinks, comm=None):
    seq = q.shape[0]
    nb = ATTN_BWD_BLOCKS

    def body(sink_ref, q_ref, do_ref, o_ref, kd0_ref, kd1_ref, vd0_ref, vd1_ref,
             dq_ref, dk0_ref, dk1_ref, dv0_ref, dv1_ref, dsink_ref):
        @pl.when(pl.program_id(0) == 0)
        def _():
            for r in (dk0_ref, dk1_ref, dv0_ref, dv1_ref, dsink_ref):
                r[...] = jnp.zeros_like(r)

        lane = lax.broadcasted_iota(jnp.int32, (1, 128), 1)
        dsink = jnp.zeros((1, 128), F32)
        for b in range(nb):
            i = pl.program_id(0) * nb + b
            rows = slice(QBLOCK * b, QBLOCK * (b + 1))
            valid = _attn_valid(i)
            for kv_head, (k_ref, v_ref, dk_ref, dv_ref) in enumerate(
                    ((kd0_ref, vd0_ref, dk0_ref, dv0_ref), (kd1_ref, vd1_ref, dk1_ref, dv1_ref))):
                kband, prev, own = _band(k_ref, i)
                vband, _, _ = _band(v_ref, i)
                base = 256 * kv_head
                qm = _stack_heads(q_ref[rows, base:base + 128], q_ref[rows, base + 128:base + 256])
                dom = _stack_heads(do_ref[rows, base:base + 128], do_ref[rows, base + 128:base + 256])
                om = _stack_heads(o_ref[rows, base:base + 128], o_ref[rows, base + 128:base + 256])
                s = jnp.where(valid, _dot_nt(qm, kband), NEG_INF)
                p, e_sink, inv_l = _softmax_with_sink(s, _sink_column(sink_ref, kv_head))
                p = p * inv_l
                delta = jnp.sum(dom.astype(F32) * om.astype(F32), axis=-1, keepdims=True)
                ds = (p * (_dot_nt(dom, vband) - delta)).astype(BF16)
                sink_term = -(e_sink * inv_l) * delta
                for j in range(4):
                    part = jnp.sum(sink_term[QBLOCK * j:QBLOCK * (j + 1)], axis=0, keepdims=True)
                    dsink = dsink + jnp.where(lane == 4 * kv_head + j, part, 0.0)
                pair0, pair1 = _unstack_heads(_dot(ds, kband))
                dq_ref[rows, base:base + 128] = pair0.astype(BF16)
                dq_ref[rows, base + 128:base + 256] = pair1.astype(BF16)
                dkd = _dot_tn(ds, qm)
                dkd = dkd + pltpu.roll(dkd, HEAD_DIM, 1)
                dvd = _dot_tn(p.astype(BF16), dom)
                dvd = dvd + pltpu.roll(dvd, HEAD_DIM, 1)
                dk_ref[pl.ds(prev, QBLOCK), :] += dkd[0:QBLOCK]
                dk_ref[pl.ds(own, QBLOCK), :] += dkd[QBLOCK:]
                dv_ref[pl.ds(prev, QBLOCK), :] += dvd[0:QBLOCK]
                dv_ref[pl.ds(own, QBLOCK), :] += dvd[QBLOCK:]
        dsink_ref[...] += dsink

    blk = pl.BlockSpec((nb * QBLOCK, Q_WIDTH), lambda i: (i, 0))
    full = _resident((seq, 128))
    acc = pl.BlockSpec((seq, 128), lambda i: (0, 0))
    return _pallas(
        body, name="attention_bwd", grid=(seq // (nb * QBLOCK),),
        in_specs=[pl.BlockSpec(memory_space=pltpu.SMEM), blk, blk, blk, full, full, full, full],
        out_specs=[blk, acc, acc, acc, acc, pl.BlockSpec((1, 128), lambda i: (0, 0))],
        out_shape=[SDS((seq, Q_WIDTH), BF16)] + [SDS((seq, 128), F32)] * 4 + [SDS((1, 128), F32)],
        operands=(sinks, q, dattn, attn, kd0, kd1, vd0, vd1), comm=comm)


def _in_proj_bwd(dq, dk0, dk1, dv0, dv1, dgb, dy, gc, xin, conv_w, x, dh, g_pre, w_in_t, rope):
    seq = x.shape[0]
    tb = min(seq, WIDE_TOKEN_TILE)
    n_tiles = seq // tb

    def body(dq_ref, dk0_ref, dk1_ref, dv0_ref, dv1_ref, dgb_ref, dy_ref, dyh_ref, gc_ref, xin_ref, cw_ref,
             x_ref, dh_ref, g_ref, w_ref, c_ref, sa_ref, sb_ref,
             dproj_ref, gx_ref, dg_ref):
        i = pl.program_id(0)

        @pl.when(i == 0)
        def _():
            dg_ref[...] = jnp.zeros_like(dg_ref)

        dy = dy_ref[...].astype(F32)
        ext = jnp.concatenate([dy, jnp.where(i == n_tiles - 1, 0.0, dyh_ref[...].astype(F32))], axis=0)
        dy1 = pltpu.roll(ext, tb + HALO - 1, 0)[0:tb]
        dy2 = pltpu.roll(ext, tb + HALO - 2, 0)[0:tb]
        cw = cw_ref[...]
        du = cw[2:3, :] * dy + cw[1:2, :] * dy1 + cw[0:1, :] * dy2
        scale = 1.0 / math.sqrt(HEAD_DIM)
        base = Q_WIDTH + 2 * KV_WIDTH
        halves = [slice(0, tb // 2), slice(tb // 2, tb)]
        low = _lane_lt64((tb // 2, 128))
        for rows in halves:
            c, sa, sb = _rope_tile(c_ref.at[rows, :], sa_ref, sb_ref)
            for p in range(Q_WIDTH // 128):
                dproj_ref[rows, 128 * p:128 * (p + 1)] = _rope_transposed(
                    dq_ref[rows, 128 * p:128 * (p + 1)].astype(F32) * scale, c, sa, sb).astype(BF16)
            dk = jnp.where(low, dk0_ref[rows, :], dk1_ref[rows, :])
            dproj_ref[rows, Q_WIDTH:Q_WIDTH + KV_WIDTH] = _rope_transposed(dk, c, sa, sb).astype(BF16)
            dproj_ref[rows, Q_WIDTH + KV_WIDTH:base] = jnp.where(low, dv0_ref[rows, :], dv1_ref[rows, :]).astype(BF16)
            dproj_ref[rows, base:base + CONV_WIDTH] = dgb_ref[rows, :]
            dproj_ref[rows, base + CONV_WIDTH:base + 2 * CONV_WIDTH] = (du[rows] * xin_ref[rows, :].astype(F32)).astype(BF16)
            dproj_ref[rows, base + 2 * CONV_WIDTH:] = (du[rows] * gc_ref[rows, :].astype(F32)).astype(BF16)
        w_all = w_ref[...].reshape(IN_COLS, D_MODEL)
        dhn = [_dot(dproj_ref[rows, :], w_all) for rows in halves]
        dg = jnp.zeros((1, D_MODEL), F32)
        for k, rows in enumerate(halves):
            xv = x_ref[rows, :]
            r = _rms(xv)
            xhat = xv * r
            dg = dg + _colsum(dhn[k] * xhat)
            gx_ref[rows, :] = dh_ref[rows, :].astype(F32) + _norm_bwd(dhn[k], g_ref[...], xhat, r)
        dg_ref[...] += dg

    tile = lambda w: pl.BlockSpec((tb, w), lambda i: (i, 0))
    halo_next = pl.BlockSpec((HALO, CONV_WIDTH), lambda i: (jnp.minimum((i + 1) * (tb // HALO), seq // HALO - 1), 0))
    return _pallas(
        body, name="in_proj_bwd", grid=(n_tiles,),
        in_specs=[tile(Q_WIDTH), tile(128), tile(128), tile(128), tile(128), tile(CONV_WIDTH), tile(CONV_WIDTH), halo_next,
                  tile(CONV_WIDTH), tile(CONV_WIDTH), _resident((CONV_K, CONV_WIDTH)),
                  tile(D_MODEL), tile(D_MODEL), _resident((1, D_MODEL)), _resident(w_in_t.shape), *_rope_specs(tb)],
        out_specs=[tile(IN_COLS), tile(D_MODEL), pl.BlockSpec((1, D_MODEL), lambda i: (0, 0))],
        out_shape=[SDS((seq, IN_COLS), BF16), SDS((seq, D_MODEL), F32), SDS((1, D_MODEL), F32)],
        operands=(dq, dk0, dk1, dv0, dv1, dgb, dy, dy, gc, xin, conv_w, x, dh, g_pre, w_in_t, *rope))


def _wgrad_grid(seq, per_chip, h_rows, with_rider=False):
    chips_per_step = 1 if per_chip else N_CHIPS
    m = chips_per_step * 2 * h_rows
    bt = min(seq, WGRAD_TOKEN_TILE if per_chip and not with_rider else WGRAD_TOKEN_TILE // 2)
    return chips_per_step, m, bt, seq // bt


def _wgrad(name, a, b, *, per_chip, h_rows, square_a=False, comm=None, rider=None):
    seq = a.shape[0]
    chips_per_step, m, bt, n_k = _wgrad_grid(seq, per_chip, h_rows, rider is not None)
    a_cols = m if per_chip else a.shape[1]
    a_wide = a.shape[1] > a_cols
    b_wide = b.shape[1] > D_MODEL

    def body(a_ref, b_ref, g_ref):
        @pl.when(pl.program_id(1) == 0)
        def _():
            g_ref[...] = jnp.zeros_like(g_ref)

        av = a_ref[...]
        if square_a:
            av = (av.astype(F32) * av.astype(F32)).astype(BF16)
        g_ref[...] += _dot_tn(av, b_ref[...]).reshape(g_ref.shape)

    a_spec = pl.BlockSpec((bt, a_cols), (lambda j, k: (k, j)) if a_wide else (lambda j, k: (k, 0)))
    b_spec = pl.BlockSpec((bt, D_MODEL), (lambda j, k: (k, j)) if b_wide else (lambda j, k: (k, 0)))
    g_spec = pl.BlockSpec((chips_per_step, 2, h_rows, D_MODEL), lambda j, k: (j, 0, 0, 0),
                          pipeline_mode=None if per_chip else pl.Buffered(1))
    return _pallas(
        body, name=name, grid=(N_CHIPS if per_chip else 1, n_k),
        in_specs=[a_spec, b_spec], out_specs=[g_spec], out_shape=[SDS((N_CHIPS, 2, h_rows, D_MODEL), F32)],
        operands=(a, b), comm=comm, rider=rider)


def _adamw_math(w, g, m, v):
    m = ADAM_B1 * m + (1.0 - ADAM_B1) * g
    v = ADAM_B2 * v + (1.0 - ADAM_B2) * (g * g)
    m_hat = m / (1.0 - ADAM_B1 ** ADAM_STEP)
    v_hat = v / (1.0 - ADAM_B2 ** ADAM_STEP)
    delta = -ADAM_LR * (m_hat / (jnp.sqrt(v_hat) + ADAM_EPS) + ADAM_WD * w)
    return delta, m, v


def _adamw_rows(name, reduced, w, m, v, rt):
    per_half = reduced.shape[1] // rt

    def body(r_ref, w_ref, m_ref, v_ref, g_out, d_out, m_out, v_out):
        g = r_ref[0]
        g_out[...] = g
        d_out[...], m_out[...], v_out[...] = _adamw_math(w_ref[...], g, m_ref[...], v_ref[...])

    blk = pl.BlockSpec((rt, D_MODEL), lambda h, r: (h * per_half + r, 0))
    return _pallas(
        body, name=name, grid=(2, per_half),
        in_specs=[pl.BlockSpec((1, rt, D_MODEL), lambda h, r: (h, r, 0)), blk, blk, blk],
        out_specs=[blk, blk, blk, blk], out_shape=[SDS(w.shape, F32)] * 4, operands=(reduced, w, m, v))


def _adamw_small(packed_grads, w, m, v):
    names = SMALL_NAMES
    n = len(names)
    conv_local = w["conv_w"].shape[-1]

    def body(*refs):
        gp = refs[0]
        w_refs, m_refs, v_refs = refs[1:1 + n], refs[1 + n:1 + 2 * n], refs[1 + 2 * n:1 + 3 * n]
        outs = refs[1 + 3 * n:]
        g_out, d_out, m_out, v_out = outs[0:n], outs[n:2 * n], outs[2 * n:3 * n], outs[3 * n:4 * n]
        chip = 2 * lax.axis_index("x") + lax.axis_index("y")

        def step(k, g, index=None):
            pick = (lambda r: r[...]) if index is None else (lambda r: r[index])
            d, new_m, new_v = _adamw_math(pick(w_refs[k]), g, pick(m_refs[k]), pick(v_refs[k]))
            for ref, val in ((g_out[k], g), (d_out[k], d), (m_out[k], new_m), (v_out[k], new_v)):
                if index is None:
                    ref[...] = val
                else:
                    ref[index] = val

        for k, name in enumerate(names):
            if name in SMALL_VECTORS:
                step(k, gp[SMALL_VECTORS.index(name):SMALL_VECTORS.index(name) + 1, :])
            elif name == "attn_group_norm":
                step(k, gp[4:5, 0:Q_WIDTH])
            elif name == "conv_group_norm":
                step(k, gp[4:5, Q_WIDTH:])
            elif name == "attn_sinks":
                step(k, gp[7:8, 0:8])
            else:
                for t in range(CONV_K):
                    row, base = 5 + t // 2, CONV_WIDTH * (t % 2)
                    g = gp[row:row + 1, base:base + conv_local]
                    for j in range(1, CONV_WIDTH // conv_local):
                        g = jnp.where(chip == j, gp[row:row + 1, base + conv_local * j:base + conv_local * (j + 1)], g)
                    step(k, g, index=(0, slice(t, t + 1), slice(None)))

    shapes = [SDS(w[name].shape, F32) for name in names]
    res = pl.pallas_call(
        body, name="adamw_small", in_specs=[VMEM_WHOLE] * (1 + 3 * n), out_specs=[VMEM_WHOLE] * (4 * n),
        out_shape=shapes * 4,
    )(packed_grads, *[w[k] for k in names], *[m[k] for k in names], *[v[k] for k in names])
    return [dict(zip(names, res[i * n:(i + 1) * n])) for i in range(4)]


SMALL_VECTORS = ("pre_mix_norm", "post_mix_norm", "pre_mlp_norm", "post_mlp_norm")
SMALL_NAMES = SMALL_VECTORS + ("attn_group_norm", "conv_group_norm", "conv_w", "attn_sinks")


def _pack_small(p):
    rows = [p[n].reshape(1, D_MODEL) for n in SMALL_VECTORS]
    rows.append(jnp.concatenate([p["attn_group_norm"].reshape(1, -1), p["conv_group_norm"].reshape(1, -1)], axis=1))
    cw = p["conv_w"].reshape(CONV_K, -1)
    rows.append(jnp.pad(cw, ((0, 1), (0, CONV_WIDTH - cw.shape[1]))).reshape(2, D_MODEL))
    last = jnp.concatenate([p["attn_sinks"].reshape(1, 8), p.get("loss_sum", jnp.zeros((1, 1), F32))], axis=1)
    rows.append(jnp.pad(last, ((0, 0), (0, D_MODEL - 9))))
    return jnp.concatenate(rows, axis=0)


WEIGHT_ORDER = ("pre_mix_norm", "w_in", "conv_w", "attn_sinks", "attn_group_norm", "conv_group_norm", "w_out",
                "post_mix_norm", "pre_mlp_norm", "w_up", "w_down", "post_mlp_norm")


def kernel(x, pre_mix_norm, w_in, conv_w, attn_sinks, attn_group_norm, conv_group_norm, w_out, post_mix_norm, pre_mlp_norm, w_up, w_down, post_mlp_norm, loss_target, m_pre_mix_norm, m_w_in, m_conv_w, m_attn_sinks, m_attn_group_norm, m_conv_group_norm, m_w_out, m_post_mix_norm, m_pre_mlp_norm, m_w_up, m_w_down, m_post_mlp_norm, v_pre_mix_norm, v_w_in, v_conv_w, v_attn_sinks, v_attn_group_norm, v_conv_group_norm, v_w_out, v_post_mix_norm, v_pre_mlp_norm, v_w_up, v_w_down, v_post_mlp_norm):
    w = dict(pre_mix_norm=pre_mix_norm, w_in=w_in, conv_w=conv_w, attn_sinks=attn_sinks, attn_group_norm=attn_group_norm,
             conv_group_norm=conv_group_norm, w_out=w_out, post_mix_norm=post_mix_norm, pre_mlp_norm=pre_mlp_norm,
             w_up=w_up, w_down=w_down, post_mlp_norm=post_mlp_norm)
    m = dict(pre_mix_norm=m_pre_mix_norm, w_in=m_w_in, conv_w=m_conv_w, attn_sinks=m_attn_sinks,
             attn_group_norm=m_attn_group_norm, conv_group_norm=m_conv_group_norm, w_out=m_w_out,
             post_mix_norm=m_post_mix_norm, pre_mlp_norm=m_pre_mlp_norm, w_up=m_w_up, w_down=m_w_down,
             post_mlp_norm=m_post_mlp_norm)
    v = dict(pre_mix_norm=v_pre_mix_norm, w_in=v_w_in, conv_w=v_conv_w, attn_sinks=v_attn_sinks,
             attn_group_norm=v_attn_group_norm, conv_group_norm=v_conv_group_norm, w_out=v_w_out,
             post_mix_norm=v_post_mix_norm, pre_mlp_norm=v_pre_mlp_norm, w_up=v_w_up, w_down=v_w_down,
             post_mlp_norm=v_post_mlp_norm)
    core = lax.axis_index("c").astype(jnp.int32).reshape(1)
    xs, target = x[0], loss_target[0]
    rope = _rope_inputs(xs.shape[0])

    hb_up, hb_down, hb_out, hb_in = _cast_halves(core, w_up[0], w_down[0], w_out[0], w_in[0].T)
    conv_pad = jnp.pad(conv_w[0], ((0, 8 - CONV_K), (0, 0)))
    wf_in, conv_all = _gather_whole(hb_in, conv_pad)
    conv_full = conv_all[:, :CONV_K, :].transpose(1, 0, 2).reshape(CONV_K, CONV_WIDTH)

    whole_up, early, late = (0, H_UP), (0, DOWN_EARLY_ROWS), (DOWN_EARLY_ROWS, H_DOWN - DOWN_EARLY_ROWS)
    *proj, wf_up, wf_out, wf_down = _in_proj(
        xs, pre_mix_norm, wf_in, rope,
        comm=_merge(_relay(hb_up, None, first=whole_up), _gather_first(hb_out), _relay(hb_down, None, first=early)))
    q, kd0, kd1, vd0, vd1, gb, gc, xin, hn = proj
    attn, wf_up, wf_out, wf_down = _attention_fwd(
        q, kd0, kd1, vd0, vd1, attn_sinks,
        comm=_merge(_relay(None, wf_up, second=whole_up), _gather_second(wf_out),
                    _relay(hb_down, wf_down, first=late, second=early)))
    mix, mixed, wf_up, wf_down = _mix_out(
        attn, gb, gc, xin, conv_full, attn_group_norm, conv_group_norm, wf_out,
        comm=_merge(_relay(None, wf_up, third=whole_up), _relay(None, wf_down, second=late, third=early, third_after=late)))
    up, hn2, dmlp, dup, dh, dmix, loss_sum, dg_post_mlp, dg_pre_mlp, dg_post_mix = _mlp_fwd_bwd(
        xs, mix, target, post_mix_norm, pre_mlp_norm, post_mlp_norm, wf_up, wf_down)

    n_k = _wgrad_grid(xs.shape[0], True, H_DOWN, with_rider=True)[3]
    g_down, dattn, dgb, dy, dg_attn, dg_conv, dconv_w = _wgrad(
        "wgrad_down", up, dmlp, per_chip=True, h_rows=H_DOWN, square_a=True,
        rider=_mix_bwd(dmix, attn, gb, gc, xin, conv_full, attn_group_norm, conv_group_norm, wf_out, n_k))
    g_up, got_down = _wgrad("wgrad_up", hn2, dup, per_chip=True, h_rows=H_UP, comm=_pair_send(g_down))
    p_down = _pair_sum("pair_sum_down", core, g_down, got_down)
    g_out, got_up = _wgrad("wgrad_out", mixed, dmix, per_chip=False, h_rows=H_OUT, comm=_pair_send(g_up))
    p_up = _pair_sum("pair_sum_up", core, g_up, got_up)
    dq, dk0, dk1, dv0, dv1, dsink, ex_down, ex_up, got_out = _attention_bwd(
        q, dattn, attn, kd0, kd1, vd0, vd1, attn_sinks,
        comm=_merge(_chip_exchange(p_down), _chip_exchange(p_up), _pair_send(g_out)))
    p_out = _pair_sum("pair_sum_out", core, g_out, got_out)
    dproj, grad_x, dg_pre_mix = _in_proj_bwd(dq, dk0, dk1, dv0, dv1, dgb, dy, gc, xin, conv_full, xs, dh, pre_mix_norm,
                                             wf_in, rope)
    g_in, ex_out = _wgrad("wgrad_in", dproj, hn, per_chip=False, h_rows=H_IN, comm=_chip_exchange(p_out))
    small = dict(pre_mix_norm=dg_pre_mix, conv_w=dconv_w, attn_sinks=dsink[:, :8], attn_group_norm=dg_attn,
                 conv_group_norm=dg_conv, post_mix_norm=dg_post_mix, pre_mlp_norm=dg_pre_mlp, post_mlp_norm=dg_post_mlp,
                 loss_sum=loss_sum)
    r_down, r_up, r_out, r_in, small_total = _tail_reduce(g_in, [ex_down, ex_up, ex_out], _pack_small(small))

    out_g, out_d, out_m, out_v = {}, {}, {}, {}
    out_g["w_up"], out_d["w_up"], out_m["w_up"], out_v["w_up"] = _adamw_rows(
        "adamw_up", r_up, w_up[0], m_w_up[0], v_w_up[0], 256)
    out_g["w_down"], out_d["w_down"], out_m["w_down"], out_v["w_down"] = _adamw_rows(
        "adamw_down", r_down, w_down[0], m_w_down[0], v_w_down[0], 256)
    out_g["w_out"], out_d["w_out"], out_m["w_out"], out_v["w_out"] = _adamw_rows(
        "adamw_out", r_out, w_out[0], m_w_out[0], v_w_out[0], H_OUT)
    in_t = _adamw_rows("adamw_in", r_in, w_in[0].T, m_w_in[0].T, v_w_in[0].T, H_IN)
    out_g["w_in"], out_d["w_in"], out_m["w_in"], out_v["w_in"] = [t.T for t in in_t]

    loss = small_total[7, 8] * (0.5 / D_MODEL)
    for out, part in zip((out_g, out_d, out_m, out_v), _adamw_small(small_total, w, m, v)):
        out.update(part)

    def shaped(d):
        return [d[n].reshape(w[n].shape) for n in WEIGHT_ORDER]

    return (loss, grad_x[None], *shaped(out_g), *shaped(out_d), *shaped(out_m), *shaped(out_v))
```

```python
import math
from typing import Callable, NamedTuple

import jax
import jax.numpy as jnp
import numpy as np
from jax import lax
from jax.experimental import pallas as pl
from jax.experimental.pallas import tpu as pltpu

F32 = jnp.float32
BF16 = jnp.bfloat16

D_MODEL = 1024
HEAD_DIM = 64
Q_WIDTH = 512
KV_WIDTH = 128
CONV_WIDTH = 512
CONV_K = 3
D_FF = 4096
IN_COLS = 2304
QBLOCK = 128
ROT_DIM = 16
ROPE_THETA = 500000.0
NORM_EPS = 1e-6
NEG_INF = -1e30
N_CHIPS = 4

ADAM_LR = 0.001
ADAM_B1 = 0.9
ADAM_B2 = 0.999
ADAM_EPS = 1e-08
ADAM_WD = 0.01
ADAM_STEP = 10

H_UP, H_DOWN, H_OUT, H_IN = 512, 512, 128, 288
DOWN_EARLY_ROWS = 224

TOKEN_TILE = 512
WIDE_TOKEN_TILE = 1024
ATTN_FWD_BLOCKS = 16
ATTN_BWD_BLOCKS = 2
WGRAD_TOKEN_TILE = 4096
VMEM_LIMIT_V7X = 60 * 1024 * 1024

MESH = pl.DeviceIdType.MESH
ANY = pl.BlockSpec(memory_space=pl.ANY)
VMEM_WHOLE = pl.BlockSpec(memory_space=pltpu.VMEM)
SDS = jax.ShapeDtypeStruct


def _resident(shape):
    zeros = (0,) * len(shape)
    return pl.BlockSpec(shape, lambda *_: zeros, pipeline_mode=pl.Buffered(1))


def _rms(v):
    return lax.rsqrt(jnp.mean(v * v, axis=-1, keepdims=True) + NORM_EPS)


def _norm_bwd(dy, gain, vhat, rstd):
    t = dy * gain
    return rstd * (t - vhat * jnp.mean(t * vhat, axis=-1, keepdims=True))


def _colsum(v):
    return jnp.sum(v, axis=0, keepdims=True)


def _dot_nt(a, b):
    return lax.dot_general(a, b, (((1,), (1,)), ((), ())), preferred_element_type=F32)


def _dot_tn(a, b):
    return lax.dot_general(a, b, (((0,), (0,)), ((), ())), preferred_element_type=F32)


def _dot(a, b):
    return jnp.dot(a, b, preferred_element_type=F32)


def _chip_block(w_ref, chip):
    both = w_ref[pl.ds(2 * chip, 2)]
    return both.reshape(2 * both.shape[1], both.shape[2])


def _lane_lt64(shape):
    return lax.broadcasted_iota(jnp.int32, shape, 1) < HEAD_DIM


class _Comm(NamedTuple):
    operands: tuple
    out_shapes: tuple
    aliases: dict
    n_remote: int
    n_local: int
    plan: Callable
    after: Callable = None


def _merge(*comms):
    operands, out_shapes, aliases, parts = [], [], {}, []
    n_remote = n_local = 0
    for cm in comms:
        parts.append((len(operands), len(out_shapes), n_remote, n_local, cm))
        for k, v in cm.aliases.items():
            aliases[len(operands) + k] = len(out_shapes) + v
        operands += cm.operands
        out_shapes += cm.out_shapes
        n_remote += cm.n_remote
        n_local += cm.n_local

    def run(which, ins, outs, send, recv, loc):
        sends, recvs, locs = [], [], []
        for i0, o0, r0, l0, cm in parts:
            stage = getattr(cm, which)
            if stage is not None:
                s, r, l = stage(ins[i0:i0 + len(cm.operands)], outs[o0:o0 + len(cm.out_shapes)],
                                lambda k, r0=r0: send(r0 + k), lambda k, r0=r0: recv(r0 + k), lambda k, l0=l0: loc(l0 + k))
                sends, recvs, locs = sends + s, recvs + r, locs + l
        return sends, recvs, locs

    def plan(*args):
        return run("plan", *args)

    def after(*args):
        return run("after", *args)

    return _Comm(tuple(operands), tuple(out_shapes), aliases, n_remote, n_local, plan,
                 after if any(cm.after is not None for cm in comms) else None)


def _sem_scratch(comm):
    return [pltpu.SemaphoreType.DMA((max(comm.n_remote, 1),)), pltpu.SemaphoreType.DMA((max(comm.n_remote, 1),)),
            pltpu.SemaphoreType.DMA((max(comm.n_local, 1),))]


class _Rider(NamedTuple):
    body: Callable
    in_specs: list
    out_specs: list
    out_shape: list
    operands: tuple


def _pallas(body, *, name, grid, in_specs, out_specs, out_shape, operands, scratch=(), comm=None, rider=None):
    params = pltpu.CompilerParams(dimension_semantics=("arbitrary",) * len(grid), vmem_limit_bytes=VMEM_LIMIT_V7X)
    if rider is not None:
        own_in, own_out, ride_in, ride_out = len(in_specs), len(out_specs), len(rider.in_specs), len(rider.out_specs)
        own_body = body

        def body(*refs):
            o0 = own_in + ride_in
            s0 = o0 + own_out + ride_out
            own_body(*refs[:own_in], *refs[o0:o0 + own_out], *refs[s0:])
            first = None
            for axis in range(len(grid)):
                at_start = pl.program_id(axis) == 0
                first = at_start if first is None else jnp.logical_and(first, at_start)
            rider.body(first, *refs[own_in:o0], *refs[o0 + own_out:s0])

        in_specs, out_specs = list(in_specs) + rider.in_specs, list(out_specs) + rider.out_specs
        out_shape, operands = list(out_shape) + rider.out_shape, tuple(operands) + tuple(rider.operands)
    if comm is None:
        return pl.pallas_call(body, name=name, grid=grid, in_specs=in_specs, out_specs=out_specs, out_shape=out_shape,
                              scratch_shapes=list(scratch), compiler_params=params)(*operands)
    n_in, n_out, n_scr = len(in_specs), len(out_specs), len(scratch)
    c_in, c_out = len(comm.operands), len(comm.out_shapes)

    def with_comm(*refs):
        ins, c_ins = refs[:n_in], refs[n_in:n_in + c_in]
        o0 = n_in + c_in
        outs, c_outs = refs[o0:o0 + n_out], refs[o0 + n_out:o0 + n_out + c_out]
        s0 = o0 + n_out + c_out
        scr = refs[s0:s0 + n_scr]
        send_sems, recv_sems, local_sems = refs[s0 + n_scr:]
        first = last = None
        for axis, size in enumerate(grid):
            at_start, at_end = pl.program_id(axis) == 0, pl.program_id(axis) == size - 1
            first = at_start if first is None else jnp.logical_and(first, at_start)
            last = at_end if last is None else jnp.logical_and(last, at_end)

        def copies():
            return comm.plan(c_ins, c_outs, lambda k: send_sems.at[k], lambda k: recv_sems.at[k],
                             lambda k: local_sems.at[k])

        @pl.when(first)
        def _():
            sends, _, locs = copies()
            for cp in sends + locs:
                cp.start()

        body(*ins, *outs, *scr)

        @pl.when(last)
        def _():
            sends, recvs, locs = copies()
            for cp in recvs:
                cp.wait_recv()
            for cp in sends:
                cp.wait_send()
            for cp in locs:
                cp.wait()
            if comm.after is not None:
                sends, recvs, _ = comm.after(c_ins, c_outs, lambda k: send_sems.at[k], lambda k: recv_sems.at[k],
                                             lambda k: local_sems.at[k])
                for cp in sends:
                    cp.start()
                for cp in recvs:
                    cp.wait_recv()
                for cp in sends:
                    cp.wait_send()

    return pl.pallas_call(
        with_comm, name=name, grid=grid,
        in_specs=list(in_specs) + [ANY] * c_in, out_specs=list(out_specs) + [ANY] * c_out,
        out_shape=list(out_shape) + list(comm.out_shapes),
        scratch_shapes=list(scratch) + _sem_scratch(comm),
        input_output_aliases={n_in + k: n_out + v for k, v in comm.aliases.items()},
        compiler_params=params)(*operands, *comm.operands)


def _place():
    return lax.axis_index("x"), lax.axis_index("y"), lax.axis_index("c")


def _other_chips(x, y):
    return [(1 - x, y), (x, 1 - y), (1 - x, 1 - y)]


def _slot(px, py, pc):
    return 4 * px + 2 * py + pc


def _remote(src, dst, send_sem, recv_sem, to):
    return pltpu.make_async_remote_copy(src_ref=src, dst_ref=dst, send_sem=send_sem, recv_sem=recv_sem,
                                        device_id=to, device_id_type=MESH)


def _gather_first(half_block):
    def plan(ins, outs, send, recv, loc):
        (blk,), (full,) = ins, outs
        x, y, c = _place()
        chips = _other_chips(x, y)
        mine = full.at[_slot(x, y, c)]
        sends = [_remote(blk, mine, send(0), recv(0), (x, y, 1 - c))]
        sends += [_remote(blk, mine, send(1 + j), recv(1 + j), (*chip, c)) for j, chip in enumerate(chips)]
        recvs = [_remote(blk, full.at[_slot(x, y, 1 - c)], send(0), recv(0), (x, y, 1 - c))]
        recvs += [_remote(blk, full.at[_slot(*chip, c)], send(1 + j), recv(1 + j), (*chip, c))
                  for j, chip in enumerate(chips)]
        return sends, recvs, [pltpu.make_async_copy(blk, mine, loc(0))]

    return _Comm((half_block,), (SDS((2 * N_CHIPS,) + half_block.shape, half_block.dtype),), {}, 4, 1, plan)


def _gather_second(partly_gathered):
    def plan(ins, outs, send, recv, loc):
        (src,), (full,) = ins, outs
        x, y, c = _place()
        chips = _other_chips(x, y)
        sends = [_remote(src.at[_slot(*chip, c)], full.at[_slot(*chip, c)], send(j), recv(j), (x, y, 1 - c))
                 for j, chip in enumerate(chips)]
        recvs = [_remote(src.at[_slot(*chip, 1 - c)], full.at[_slot(*chip, 1 - c)], send(j), recv(j), (x, y, 1 - c))
                 for j, chip in enumerate(chips)]
        return sends, recvs, []

    return _Comm((partly_gathered,), (SDS(partly_gathered.shape, partly_gathered.dtype),), {0: 0}, 3, 0, plan)


def _relay_pieces(full, rows, x, y, c):
    start, half = rows[0], rows[1] // 2
    upper, lower = pl.ds(start, half), pl.ds(start + half, half)
    diagonal = full.at[_slot(1 - x, 1 - y, c)]
    return [(full.at[_slot(1 - x, y, c), upper], diagonal.at[upper], (x, 1 - y, c)),
            (full.at[_slot(x, 1 - y, c), lower], diagonal.at[lower], (1 - x, y, c))]


def _relay(half_block, so_far, first=None, second=None, third=None, third_after=None):
    has_block, has_buffer = half_block is not None, so_far is not None
    shape = so_far.shape if has_buffer else (2 * N_CHIPS,) + half_block.shape
    dtype = so_far.dtype if has_buffer else half_block.dtype

    def third_leg(rows, k, ins, outs, send, recv):
        src, full = (ins[-1] if has_buffer else outs[0]), outs[0]
        x, y, c = _place()
        span, sibling = pl.ds(*rows), (x, y, 1 - c)
        here, there = _slot(1 - x, 1 - y, c), _slot(1 - x, 1 - y, 1 - c)
        return ([_remote(src.at[here, span], full.at[here, span], send(k), recv(k), sibling)],
                [_remote(src.at[there, span], full.at[there, span], send(k), recv(k), sibling)])

    def plan(ins, outs, send, recv, loc):
        src, full = (ins[-1] if has_buffer else outs[0]), outs[0]
        x, y, c = _place()
        sibling = (x, y, 1 - c)
        sends, recvs, locs = [], [], []
        if first is not None:
            span = pl.ds(*first)
            blk, mine = ins[0].at[span], full.at[_slot(x, y, c), span]
            for k, peer in enumerate([sibling, (1 - x, y, c), (x, 1 - y, c)]):
                sends.append(_remote(blk, mine, send(k), recv(k), peer))
                recvs.append(_remote(blk, full.at[_slot(*peer), span], send(k), recv(k), peer))
            locs.append(pltpu.make_async_copy(blk, mine, loc(0)))
        if second is not None:
            span = pl.ds(*second)
            for k, chip in enumerate([(1 - x, y), (x, 1 - y)]):
                sends.append(_remote(src.at[_slot(*chip, c), span], full.at[_slot(*chip, c), span], send(3 + k), recv(3 + k),
                                     sibling))
                recvs.append(_remote(src.at[_slot(*chip, 1 - c), span], full.at[_slot(*chip, 1 - c), span], send(3 + k),
                                     recv(3 + k), sibling))
            for k, (piece, lands, peer) in enumerate(_relay_pieces(full, second, x, y, c)):
                sends.append(_remote(piece, piece, send(5 + k), recv(5 + k), peer))
                recvs.append(_remote(lands, lands, send(5 + k), recv(5 + k), peer))
        if third is not None:
            s, r = third_leg(third, 7, ins, outs, send, recv)
            sends, recvs = sends + s, recvs + r
        return sends, recvs, locs

    def after(ins, outs, send, recv, loc):
        s, r = third_leg(third_after, 8, ins, outs, send, recv)
        return s, r, []

    operands = ((half_block,) if has_block else ()) + ((so_far,) if has_buffer else ())
    return _Comm(operands, (SDS(shape, dtype),), {len(operands) - 1: 0} if has_buffer else {}, 9, 1, plan,
                 after if third_after is not None else None)


def _gather_whole(first, others, small_block):
    shards = (first, *others)
    n = len(shards)
    hs = [s.shape[0] // 2 for s in shards]
    rows = hs[0]

    def body(*refs):
        src, small_ref = refs[:n], refs[n]
        out_ref, small_out_ref, half_out = refs[n + 1], refs[n + 2], refs[n + 3:2 * n + 2]
        stage, half = refs[2 * n + 2:3 * n + 2], refs[3 * n + 2:4 * n + 2]
        send_sems, recv_sems, local_sems = refs[4 * n + 2:]
        x, y, c = _place()
        me, sibling = (x, y, c), (x, y, 1 - c)
        neighbours, diagonal = [(1 - x, y), (x, 1 - y)], (1 - x, 1 - y)
        loads = [pltpu.make_async_copy(src[k].at[pl.ds(c * hs[k], hs[k])], stage[k], local_sems.at[2 + k]) for k in range(n)]
        for cp in loads:
            cp.start()
        loads[0].wait()
        blk_ref = half[0]
        blk_ref[...] = stage[0][...].astype(BF16)

        def copy(k, block, to, src=None):
            return _remote(out_ref.at[_slot(*block)] if src is None else src, out_ref.at[_slot(*block)],
                           send_sems.at[k], recv_sems.at[k], to)

        def small_copy(k, chip, to):
            return _remote(small_ref, small_out_ref.at[2 * chip[0] + chip[1]], send_sems.at[8 + k], recv_sems.at[8 + k], to)

        mine = pltpu.make_async_copy(blk_ref, out_ref.at[_slot(*me)], local_sems.at[0])
        mine_small = pltpu.make_async_copy(small_ref, small_out_ref.at[2 * x + y], local_sems.at[1])
        mine.start()
        mine_small.start()
        started = [copy(0, me, sibling, src=blk_ref)]
        started += [copy(1 + k, me, (*chip, c), src=blk_ref) for k, chip in enumerate(neighbours)]
        started += [small_copy(k, (x, y), (*chip, c)) for k, chip in enumerate(neighbours + [diagonal])]
        for cp in started:
            cp.start()
        stores = []
        for k in range(1, n):
            loads[k].wait()
            half[k][...] = stage[k][...].astype(BF16)
            stores.append(pltpu.make_async_copy(half[k], half_out[k - 1], local_sems.at[2 + n + k]))
            stores[-1].start()
        pieces = _relay_pieces(out_ref, (0, rows), x, y, c)
        for k, chip in enumerate(neighbours):
            copy(1 + k, (*chip, c), me).wait_recv()
            piece, _, peer = pieces[k]
            started += [copy(3 + k, (*chip, c), sibling), _remote(piece, piece, send_sems.at[5 + k], recv_sems.at[5 + k], peer)]
            started[-2].start()
            started[-1].start()
        for k, (_, lands, peer) in enumerate(pieces):
            _remote(lands, lands, send_sems.at[5 + k], recv_sems.at[5 + k], peer).wait_recv()
        started.append(copy(7, (*diagonal, c), sibling))
        started[-1].start()
        copy(0, sibling, me).wait_recv()
        for k, chip in enumerate(neighbours):
            copy(3 + k, (*chip, 1 - c), me).wait_recv()
        copy(7, (*diagonal, 1 - c), me).wait_recv()
        for k, chip in enumerate(neighbours + [diagonal]):
            small_copy(k, chip, me).wait_recv()
        for cp in started:
            cp.wait_send()
        mine.wait()
        mine_small.wait()
        for cp in stores:
            cp.wait()

    return pl.pallas_call(
        body, name="gather_whole", in_specs=[ANY] * (n + 1), out_specs=[ANY] * (n + 1),
        out_shape=[SDS((2 * N_CHIPS, rows, D_MODEL), BF16), SDS((N_CHIPS,) + small_block.shape, small_block.dtype)]
                  + [SDS((h, D_MODEL), BF16) for h in hs[1:]],
        scratch_shapes=[pltpu.VMEM((h, D_MODEL), F32) for h in hs] + [pltpu.VMEM((h, D_MODEL), BF16) for h in hs]
                       + [pltpu.SemaphoreType.DMA((11,)), pltpu.SemaphoreType.DMA((11,)), pltpu.SemaphoreType.DMA((2 + 2 * n,))],
        compiler_params=pltpu.CompilerParams(vmem_limit_bytes=VMEM_LIMIT_V7X),
    )(*shards, small_block)


def _pair_send(grads):
    def plan(ins, outs, send, recv, loc):
        (g,), (got,) = ins, outs
        x, y, c = _place()
        copies = [_remote(g.at[j, 1 - c], got.at[j], send(j), recv(j), (x, y, 1 - c)) for j in range(N_CHIPS)]
        return copies, copies, []

    shape = (grads.shape[0],) + grads.shape[2:]
    return _Comm((grads,), (SDS(shape, grads.dtype),), {}, N_CHIPS, 0, plan)


def _chip_exchange(partial):
    def plan(ins, outs, send, recv, loc):
        (p,), (got,) = ins, outs
        x, y, c = _place()
        my_chip = 2 * x + y
        chips = _other_chips(x, y)
        sends = [_remote(p.at[2 * chip[0] + chip[1]], got.at[my_chip], send(j), recv(j), (*chip, c))
                 for j, chip in enumerate(chips)]
        recvs = [_remote(p.at[my_chip], got.at[2 * chip[0] + chip[1]], send(j), recv(j), (*chip, c))
                 for j, chip in enumerate(chips)]
        return sends, recvs, [pltpu.make_async_copy(p.at[my_chip], got.at[my_chip], loc(0))]

    return _Comm((partial,), (SDS(partial.shape, partial.dtype),), {}, 3, 1, plan)


def _pair_sum(name, core, grads, received):
    h = grads.shape[2]

    def body(core_ref, g_ref, r_ref, o_ref):
        o_ref[...] = (g_ref[0] + r_ref[...]).astype(BF16)

    return pl.pallas_call(
        body, name=name,
        grid_spec=pltpu.PrefetchScalarGridSpec(
            num_scalar_prefetch=1, grid=(N_CHIPS,),
            in_specs=[pl.BlockSpec((1, 1, h, D_MODEL), lambda j, core_ref: (j, core_ref[0], 0, 0)),
                      pl.BlockSpec((1, h, D_MODEL), lambda j, core_ref: (j, 0, 0))],
            out_specs=pl.BlockSpec((1, h, D_MODEL), lambda j, core_ref: (j, 0, 0))),
        out_shape=SDS((N_CHIPS, h, D_MODEL), BF16),
        compiler_params=pltpu.CompilerParams(dimension_semantics=("arbitrary",), vmem_limit_bytes=VMEM_LIMIT_V7X),
    )(core, grads, received)


SMALL_ROWS = 8


def _sum_blocks(ref):
    return (ref[0].astype(F32) + ref[1].astype(F32)) + (ref[2].astype(F32) + ref[3].astype(F32))


def _tail_reduce(last_grads, exchanged, small):
    n = len(exchanged)
    h = last_grads.shape[2]

    def body(*refs):
        g_ref, ex, small_ref = refs[0], refs[1:1 + n], refs[1 + n]
        o0 = 2 + n
        out, out_last, small_out = refs[o0:o0 + n], refs[o0 + n], refs[o0 + n + 1]
        s0 = o0 + n + 2
        halves, half_last = refs[s0:s0 + n], refs[s0 + n]
        own, got, part, exch, small_buf = refs[s0 + n + 1:s0 + n + 6]
        ex_buf = refs[s0 + n + 6:s0 + 2 * n + 6]
        pair_send, pair_recv, chip_send, chip_recv, share_send, share_recv, small_send, small_recv, local_sems = refs[s0 + 2 * n + 6:]
        x, y, c = _place()
        sibling = (x, y, 1 - c)
        my_chip, me = 2 * x + y, _slot(x, y, c)
        chips = _other_chips(x, y)

        to_sibling = [_remote(g_ref.at[j, 1 - c], got.at[j], pair_send.at[j], pair_recv.at[j], sibling)
                      for j in range(N_CHIPS)]
        load_own = [pltpu.make_async_copy(g_ref.at[j, c], own.at[j], local_sems.at[j]) for j in range(N_CHIPS)]
        load_ex = [pltpu.make_async_copy(ex[k], ex_buf[k], local_sems.at[N_CHIPS + n + 1 + k]) for k in range(n)]
        for cp in to_sibling + load_own + load_ex:
            cp.start()

        small_buf[me] = small_ref[...]
        small_copies = []
        for mask in range(1, 8):
            peer = (x ^ (mask >> 2), y ^ ((mask >> 1) & 1), c ^ (mask & 1))
            small_copies.append(_remote(small_ref, small_buf.at[me], small_send.at[mask - 1], small_recv.at[mask - 1], peer))
        for cp in small_copies:
            cp.start()

        def share(k, half_ref, out_ref):
            keep = pltpu.make_async_copy(half_ref, out_ref.at[c], local_sems.at[N_CHIPS + k])
            give = _remote(half_ref, out_ref.at[c], share_send.at[k], share_recv.at[k], sibling)
            take = _remote(half_ref, out_ref.at[1 - c], share_send.at[k], share_recv.at[k], sibling)
            keep.start()
            give.start()
            return keep, give, take

        shares = []
        for k in range(n):
            load_ex[k].wait()
            halves[k][...] = _sum_blocks(ex_buf[k])
            shares.append(share(k, halves[k], out[k]))

        def pair_sum(block):
            _remote(g_ref.at[block, 1 - c], got.at[block], pair_send.at[block], pair_recv.at[block], sibling).wait_recv()
            pltpu.make_async_copy(g_ref.at[block, c], own.at[block], local_sems.at[block]).wait()
            part[block] = (own[block] + got[block]).astype(BF16)

        to_chips = []
        for j, chip in enumerate(chips):
            block = 2 * chip[0] + chip[1]
            pair_sum(block)
            to_chips.append(_remote(part.at[block], exch.at[my_chip], chip_send.at[j], chip_recv.at[j], (*chip, c)))
            to_chips[-1].start()
        pair_sum(my_chip)
        exch[my_chip] = part[my_chip]
        from_chips = [_remote(part.at[my_chip], exch.at[2 * chip[0] + chip[1]], chip_send.at[j], chip_recv.at[j], (*chip, c))
                      for j, chip in enumerate(chips)]

        for cp in small_copies:
            cp.wait_recv()
        total = small_buf[0]
        for d in range(1, 8):
            total = total + small_buf[d]
        small_out[...] = total

        for cp in from_chips:
            cp.wait_recv()
        half_last[...] = _sum_blocks(exch)
        shares.append(share(n, half_last, out_last))

        for keep, give, take in shares:
            take.wait_recv()
            give.wait_send()
            keep.wait()
        for cp in to_sibling + to_chips + small_copies:
            cp.wait_send()

    blocks = (N_CHIPS, h, D_MODEL)
    return pl.pallas_call(
        body, name="tail_reduce",
        in_specs=[ANY] * (n + 1) + [VMEM_WHOLE], out_specs=[ANY] * (n + 1) + [VMEM_WHOLE],
        out_shape=[SDS((2,) + e.shape[1:], F32) for e in exchanged] + [SDS((2, h, D_MODEL), F32), SDS(small.shape, F32)],
        scratch_shapes=[pltpu.VMEM(e.shape[1:], F32) for e in exchanged] + [pltpu.VMEM((h, D_MODEL), F32)]
                       + [pltpu.VMEM(blocks, F32), pltpu.VMEM(blocks, F32), pltpu.VMEM(blocks, BF16), pltpu.VMEM(blocks, BF16),
                          pltpu.VMEM((8,) + small.shape, F32)]
                       + [pltpu.VMEM(e.shape, BF16) for e in exchanged]
                       + [pltpu.SemaphoreType.DMA((N_CHIPS,)), pltpu.SemaphoreType.DMA((N_CHIPS,)),
                          pltpu.SemaphoreType.DMA((3,)), pltpu.SemaphoreType.DMA((3,)),
                          pltpu.SemaphoreType.DMA((n + 1,)), pltpu.SemaphoreType.DMA((n + 1,)),
                          pltpu.SemaphoreType.DMA((7,)), pltpu.SemaphoreType.DMA((7,)),
                          pltpu.SemaphoreType.DMA((N_CHIPS + 2 * n + 1,))],
        compiler_params=pltpu.CompilerParams(vmem_limit_bytes=VMEM_LIMIT_V7X),
    )(last_grads, *exchanged, small)


def _rope_expansion():
    half = ROT_DIM // 2
    expand = np.zeros((2 * half, 3 * 128), np.float32)
    const = np.zeros((1, 3 * 128), np.float32)
    for lane in range(128):
        d = lane % HEAD_DIM
        if d < ROT_DIM:
            expand[d % half, lane] = 1.0
        else:
            const[0, lane] = 1.0
        if d < half:
            expand[half + d, 128 + lane] = -1.0
        elif d < ROT_DIM:
            expand[half + d - half, 256 + lane] = 1.0
    return expand, const


ROPE_PIECES = 3 * ROT_DIM


def _rope_inputs(seq):
    pos = jnp.arange(seq, dtype=F32)
    inv_freq = ROPE_THETA ** (-jnp.arange(0, ROT_DIM, 2, dtype=F32) / ROT_DIM)
    ang = pos[:, None] * inv_freq[None, :]
    cs = jnp.concatenate([jnp.cos(ang), jnp.sin(ang)], axis=1)
    hi = lax.reduce_precision(cs, 8, 7)
    mid = lax.reduce_precision(cs - hi, 8, 7)
    low = cs - hi - mid
    expand, const = _rope_expansion()
    pieces = jnp.concatenate([hi, mid, low], axis=1).astype(BF16)
    return pieces, jnp.asarray(np.concatenate([expand] * 3, axis=0), BF16), jnp.asarray(const)


def _rope_specs(tb):
    return [pl.BlockSpec((tb, ROPE_PIECES), lambda i: (i, 0)), _resident((ROPE_PIECES, 3 * 128)), _resident((1, 3 * 128))]


def _rope_tile(pieces_ref, expand_ref, const_ref):
    tables = _dot(pieces_ref[...], expand_ref[...]) + const_ref[...]
    return tables[:, 0:128], tables[:, 128:256], tables[:, 256:384]


def _rope(t, c, sa, sb):
    half = ROT_DIM // 2
    return t * c + pltpu.roll(t, 128 - half, 1) * sa + pltpu.roll(t, half, 1) * sb


def _rope_transposed(dt, c, sa, sb):
    half = ROT_DIM // 2
    return dt * c + pltpu.roll(dt * sa, half, 1) + pltpu.roll(dt * sb, 128 - half, 1)


def _in_proj(x, g_pre, w_in_t, rope, comm=None):
    seq = x.shape[0]
    tb = min(seq, WIDE_TOKEN_TILE)

    def body(x_ref, g_ref, w_ref, c_ref, sa_ref, sb_ref,
             q_ref, kd0_ref, kd1_ref, vd0_ref, vd1_ref, gb_ref, gc_ref, xin_ref, hn_ref):
        xv = x_ref[...]
        hn = (xv * _rms(xv) * g_ref[...]).astype(BF16)
        hn_ref[...] = hn
        proj = _dot_nt(hn, w_ref[...].reshape(IN_COLS, D_MODEL))
        c, sa, sb = _rope_tile(c_ref, sa_ref, sb_ref)
        scale = 1.0 / math.sqrt(HEAD_DIM)
        for p in range(Q_WIDTH // 128):
            q_ref[:, 128 * p:128 * (p + 1)] = (_rope(proj[:, 128 * p:128 * (p + 1)], c, sa, sb) * scale).astype(BF16)
        k = _rope(proj[:, Q_WIDTH:Q_WIDTH + KV_WIDTH], c, sa, sb)
        v = proj[:, Q_WIDTH + KV_WIDTH:Q_WIDTH + 2 * KV_WIDTH]
        low = _lane_lt64(k.shape)
        k_sw, v_sw = pltpu.roll(k, HEAD_DIM, 1), pltpu.roll(v, HEAD_DIM, 1)
        kd0_ref[...] = jnp.where(low, k, k_sw).astype(BF16)
        kd1_ref[...] = jnp.where(low, k_sw, k).astype(BF16)
        vd0_ref[...] = jnp.where(low, v, v_sw).astype(BF16)
        vd1_ref[...] = jnp.where(low, v_sw, v).astype(BF16)
        base = Q_WIDTH + 2 * KV_WIDTH
        gb_ref[...] = proj[:, base:base + CONV_WIDTH].astype(BF16)
        gc_ref[...] = proj[:, base + CONV_WIDTH:base + 2 * CONV_WIDTH].astype(BF16)
        xin_ref[...] = proj[:, base + 2 * CONV_WIDTH:base + 3 * CONV_WIDTH].astype(BF16)

    tile = lambda w: pl.BlockSpec((tb, w), lambda i: (i, 0))
    return _pallas(
        body, name="in_proj", grid=(seq // tb,),
        in_specs=[tile(D_MODEL), _resident((1, D_MODEL)), _resident(w_in_t.shape), *_rope_specs(tb)],
        out_specs=[tile(Q_WIDTH), tile(128), tile(128), tile(128), tile(128),
                   tile(CONV_WIDTH), tile(CONV_WIDTH), tile(CONV_WIDTH), tile(D_MODEL)],
        out_shape=[SDS((seq, Q_WIDTH), BF16)] + [SDS((seq, 128), BF16)] * 4
                  + [SDS((seq, CONV_WIDTH), BF16)] * 3 + [SDS((seq, D_MODEL), BF16)],
        operands=(x, g_pre, w_in_t, *rope), comm=comm)


def _attn_valid(i):
    shape = (4 * QBLOCK, 2 * QBLOCK)
    row = lax.broadcasted_iota(jnp.int32, shape, 0)
    col = lax.broadcasted_iota(jnp.int32, shape, 1)
    qi = row & (QBLOCK - 1)
    return (col > qi) & (col <= qi + QBLOCK) & ((col >= QBLOCK) | (i > 0))


def _stack_heads(pair0, pair1):
    low = _lane_lt64(pair0.shape)
    zero = jnp.zeros_like(pair0)
    return jnp.concatenate([jnp.where(low, pair0, zero), jnp.where(low, zero, pair0),
                            jnp.where(low, pair1, zero), jnp.where(low, zero, pair1)], axis=0)


def _unstack_heads(stacked):
    low = _lane_lt64((QBLOCK, 128))
    pair0 = jnp.where(low, stacked[0:QBLOCK], stacked[QBLOCK:2 * QBLOCK])
    pair1 = jnp.where(low, stacked[2 * QBLOCK:3 * QBLOCK], stacked[3 * QBLOCK:4 * QBLOCK])
    return pair0, pair1


def _sink_column(sink_ref, kv_head):
    row = lax.broadcasted_iota(jnp.int32, (4 * QBLOCK, 1), 0)
    s = [sink_ref[0, 4 * kv_head + j] for j in range(4)]
    return jnp.where(row < QBLOCK, s[0], jnp.where(row < 2 * QBLOCK, s[1], jnp.where(row < 3 * QBLOCK, s[2], s[3])))


def _band(ref, i):
    prev = pl.multiple_of(jnp.maximum(i - 1, 0) * QBLOCK, QBLOCK)
    own = pl.multiple_of(i * QBLOCK, QBLOCK)
    return jnp.concatenate([ref[pl.ds(prev, QBLOCK), :], ref[pl.ds(own, QBLOCK), :]], axis=0), prev, own


def _softmax_with_sink(s, sink_col):
    m = jnp.maximum(jnp.max(s, axis=-1, keepdims=True), sink_col)
    p = jnp.exp(s - m)
    e_sink = jnp.exp(sink_col - m)
    inv_l = 1.0 / (jnp.sum(p, axis=-1, keepdims=True) + e_sink)
    return p, e_sink, inv_l


def _attention_fwd(q, kd0, kd1, vd0, vd1, sinks, comm=None):
    seq = q.shape[0]

    nb = ATTN_FWD_BLOCKS

    def body(sink_ref, q_ref, kd0_ref, kd1_ref, vd0_ref, vd1_ref, o_ref):
        for b in range(nb):
            i = pl.program_id(0) * nb + b
            rows = slice(QBLOCK * b, QBLOCK * (b + 1))
            valid = _attn_valid(i)
            for kv_head, (k_ref, v_ref) in enumerate(((kd0_ref, vd0_ref), (kd1_ref, vd1_ref))):
                kband, _, _ = _band(k_ref, i)
                vband, _, _ = _band(v_ref, i)
                base = 256 * kv_head
                qm = _stack_heads(q_ref[rows, base:base + 128], q_ref[rows, base + 128:base + 256])
                s = jnp.where(valid, _dot_nt(qm, kband), NEG_INF)
                p, _, inv_l = _softmax_with_sink(s, _sink_column(sink_ref, kv_head))
                o = _dot(p.astype(BF16), vband) * inv_l
                pair0, pair1 = _unstack_heads(o)
                o_ref[rows, base:base + 128] = pair0.astype(BF16)
                o_ref[rows, base + 128:base + 256] = pair1.astype(BF16)

    blk = pl.BlockSpec((nb * QBLOCK, Q_WIDTH), lambda i: (i, 0))
    full = _resident((seq, 128))
    return _pallas(
        body, name="attention_fwd", grid=(seq // (nb * QBLOCK),),
        in_specs=[pl.BlockSpec(memory_space=pltpu.SMEM), blk, full, full, full, full],
        out_specs=[blk], out_shape=[SDS((seq, Q_WIDTH), BF16)],
        operands=(sinks, q, kd0, kd1, vd0, vd1), comm=comm)


HALO = 16


def _conv_parts(gc, xin, gc_halo, xin_halo, conv_w, first):
    tb = gc.shape[0]
    u = gc.astype(F32) * xin.astype(F32)
    u_halo = jnp.where(first, 0.0, gc_halo.astype(F32) * xin_halo.astype(F32))
    ext = jnp.concatenate([u_halo, u], axis=0)
    u1 = pltpu.roll(ext, 1, 0)[HALO:HALO + tb]
    u2 = pltpu.roll(ext, 2, 0)[HALO:HALO + tb]
    y = conv_w[0:1, :] * u2 + conv_w[1:2, :] * u1 + conv_w[2:3, :] * u
    return u, u1, u2, y


def _halo_prev(tb, w):
    return pl.BlockSpec((HALO, w), lambda i: (jnp.maximum(i * (tb // HALO) - 1, 0), 0))


def _residual_mid(x, mix, g_post_mix):
    mix_f = mix.astype(F32)
    return x + mix_f * _rms(mix_f) * g_post_mix


def _mix_out(attn, gb, gc, xin, conv_w, g_attn, g_conv, w_out, comm=None):
    seq = attn.shape[0]
    tb = min(seq, WIDE_TOKEN_TILE)

    def body(a_ref, gb_ref, gc_ref, xin_ref, gch_ref, xinh_ref, cw_ref, ga_ref, gcn_ref, w_ref, mix_ref, mixed_ref):
        first = pl.program_id(0) == 0
        _, _, _, y = _conv_parts(gc_ref[...], xin_ref[...], gch_ref[...], xinh_ref[...], cw_ref[...], first)
        conv = gb_ref[...].astype(F32) * y
        a = a_ref[...].astype(F32)
        mixed_ref[:, 0:Q_WIDTH] = (a * _rms(a) * ga_ref[...]).astype(BF16)
        mixed_ref[:, Q_WIDTH:] = (conv * _rms(conv) * gcn_ref[...]).astype(BF16)
        mix_ref[...] = _dot(mixed_ref[...], w_ref[...].reshape(D_MODEL, D_MODEL)).astype(BF16)

    tile = lambda w: pl.BlockSpec((tb, w), lambda i: (i, 0))
    return _pallas(
        body, name="mix_out", grid=(seq // tb,),
        in_specs=[tile(Q_WIDTH), tile(CONV_WIDTH), tile(CONV_WIDTH), tile(CONV_WIDTH),
                  _halo_prev(tb, CONV_WIDTH), _halo_prev(tb, CONV_WIDTH),
                  _resident((CONV_K, CONV_WIDTH)), _resident((1, Q_WIDTH)), _resident((1, CONV_WIDTH)),
                  _resident(w_out.shape)],
        out_specs=[tile(D_MODEL), tile(D_MODEL)],
        out_shape=[SDS((seq, D_MODEL), BF16), SDS((seq, D_MODEL), BF16)],
        operands=(attn, gb, gc, xin, gc, xin, conv_w, g_attn, g_conv, w_out), comm=comm)


def _mlp_fwd_bwd(x, mix, target, g_post_mix, g_pre_mlp, g_post_mlp, w_up, w_down):
    seq = x.shape[0]
    tb = TOKEN_TILE

    def body(x_ref, mix_ref, t_ref, gpm_ref, g2_ref, g4_ref, wup_ref, wdown_ref,
             up_ref, hn2_ref, dmlp_ref, dup_ref, dh_ref, dmix_ref, loss_ref, dg4_ref, dg2_ref, dgpm_ref):
        @pl.when(pl.program_id(0) == 0)
        def _():
            for ref in (loss_ref, dg4_ref, dg2_ref, dgpm_ref):
                ref[...] = jnp.zeros_like(ref)

        halves = [slice(0, tb // 2), slice(tb // 2, tb)]
        chunks = [slice(1024 * j, 1024 * (j + 1)) for j in range(N_CHIPS)]
        hv, hn2, mlp, dout, dmlp, dhn2 = [], [], [], [], [], []
        for rows in halves:
            hv.append(_residual_mid(x_ref[rows, :], mix_ref[rows, :], gpm_ref[...]))
            hn2.append((hv[-1] * _rms(hv[-1]) * g2_ref[...]).astype(BF16))
            hn2_ref[rows, :] = hn2[-1]
        for k, rows in enumerate(halves):
            acc = None
            for j, cols in enumerate(chunks):
                up = jnp.maximum(_dot(hn2[k], _chip_block(wup_ref, j)), 0.0)
                up_ref[rows, cols] = up.astype(BF16)
                part = _dot((up * up).astype(BF16), _chip_block(wdown_ref, j))
                acc = part if acc is None else acc + part
            mlp.append(acc)
        loss = jnp.zeros((1, 1), F32)
        dg4 = jnp.zeros((1, D_MODEL), F32)
        for k, rows in enumerate(halves):
            rstd = _rms(mlp[k])
            zhat = mlp[k] * rstd
            diff = hv[k] + zhat * g4_ref[...] - t_ref[rows, :]
            loss = loss + jnp.sum(jnp.sum(diff * diff, axis=1, keepdims=True), axis=0, keepdims=True)
            dout.append(diff * (1.0 / D_MODEL))
            dg4 = dg4 + _colsum(dout[k] * zhat)
            dmlp.append(_norm_bwd(dout[k], g4_ref[...], zhat, rstd).astype(BF16))
            dmlp_ref[rows, :] = dmlp[k]
        for k, rows in enumerate(halves):
            acc = None
            for j, cols in enumerate(chunks):
                dact = _dot_nt(dmlp[k], _chip_block(wdown_ref, j))
                dup = (dact * (2.0 * up_ref[rows, cols].astype(F32))).astype(BF16)
                dup_ref[rows, cols] = dup
                part = _dot_nt(dup, _chip_block(wup_ref, j))
                acc = part if acc is None else acc + part
            dhn2.append(acc)
        dg2 = jnp.zeros((1, D_MODEL), F32)
        dgpm = jnp.zeros((1, D_MODEL), F32)
        for k, rows in enumerate(halves):
            r2 = _rms(hv[k])
            hhat = hv[k] * r2
            dg2 = dg2 + _colsum(dhn2[k] * hhat)
            dh = dout[k] + _norm_bwd(dhn2[k], g2_ref[...], hhat, r2)
            dh_ref[rows, :] = dh.astype(BF16)
            mix_v = mix_ref[rows, :].astype(F32)
            rz = _rms(mix_v)
            zhat = mix_v * rz
            dgpm = dgpm + _colsum(dh * zhat)
            dmix_ref[rows, :] = _norm_bwd(dh, gpm_ref[...], zhat, rz).astype(BF16)
        loss_ref[...] += loss
        dg4_ref[...] += dg4
        dg2_ref[...] += dg2
        dgpm_ref[...] += dgpm

    tile = lambda w: pl.BlockSpec((tb, w), lambda i: (i, 0))
    vec = pl.BlockSpec((1, D_MODEL), lambda i: (0, 0))
    return _pallas(
        body, name="mlp_fwd_bwd", grid=(seq // tb,),
        in_specs=[tile(D_MODEL), tile(D_MODEL), tile(D_MODEL), _resident((1, D_MODEL)), _resident((1, D_MODEL)),
                  _resident((1, D_MODEL)), _resident(w_up.shape), _resident(w_down.shape)],
        out_specs=[tile(D_FF), tile(D_MODEL), tile(D_MODEL), tile(D_FF), tile(D_MODEL), tile(D_MODEL),
                   pl.BlockSpec((1, 1), lambda i: (0, 0)), vec, vec, vec],
        out_shape=[SDS((seq, D_FF), BF16), SDS((seq, D_MODEL), BF16), SDS((seq, D_MODEL), BF16), SDS((seq, D_FF), BF16),
                   SDS((seq, D_MODEL), BF16), SDS((seq, D_MODEL), BF16),
                   SDS((1, 1), F32), SDS((1, D_MODEL), F32), SDS((1, D_MODEL), F32), SDS((1, D_MODEL), F32)],
        operands=(x, mix, target, g_post_mix, g_pre_mlp, g_post_mlp, w_up, w_down))


def _mix_bwd(dmix, attn, gb, gc, xin, conv_w, g_attn, g_conv, w_out, n_k):
    seq = attn.shape[0]
    tb = seq // (N_CHIPS * n_k)

    def body(first, dmix_ref, a_ref, gb_ref, gc_ref, xin_ref, gch_ref, xinh_ref, cw_ref, ga_ref, gcn_ref, w_ref,
             dattn_ref, dgb_ref, dy_ref, dga_ref, dgcn_ref, dcw_ref):
        @pl.when(first)
        def _():
            dga_ref[...] = jnp.zeros_like(dga_ref)
            dgcn_ref[...] = jnp.zeros_like(dgcn_ref)
            dcw_ref[...] = jnp.zeros_like(dcw_ref)

        dmixed = _dot_nt(dmix_ref[...], w_ref[...].reshape(D_MODEL, D_MODEL))
        a = a_ref[...].astype(F32)
        ra = _rms(a)
        ahat = a * ra
        dan = dmixed[:, 0:Q_WIDTH]
        dga_ref[...] += _colsum(dan * ahat)
        dattn_ref[...] = _norm_bwd(dan, ga_ref[...], ahat, ra).astype(BF16)
        gbv = gb_ref[...].astype(F32)
        u, u1, u2, y = _conv_parts(gc_ref[...], xin_ref[...], gch_ref[...], xinh_ref[...], cw_ref[...], first)
        conv = gbv * y
        rc = _rms(conv)
        chat = conv * rc
        dcn = dmixed[:, Q_WIDTH:]
        dgcn_ref[...] += _colsum(dcn * chat)
        dconv = _norm_bwd(dcn, gcn_ref[...], chat, rc)
        dgb_ref[...] = (dconv * y).astype(BF16)
        dy = dconv * gbv
        dy_ref[...] = dy.astype(BF16)
        dcw_ref[0:1, :] += _colsum(dy * u2)
        dcw_ref[1:2, :] += _colsum(dy * u1)
        dcw_ref[2:3, :] += _colsum(dy * u)

    tile = lambda w: pl.BlockSpec((tb, w), lambda j, k: (j * n_k + k, 0))
    halo = lambda w: pl.BlockSpec((HALO, w), lambda j, k: (jnp.maximum((j * n_k + k) * (tb // HALO) - 1, 0), 0))
    whole = lambda shape: pl.BlockSpec(shape, lambda j, k: (0,) * len(shape))
    return _Rider(
        body,
        in_specs=[tile(D_MODEL), tile(Q_WIDTH), tile(CONV_WIDTH), tile(CONV_WIDTH), tile(CONV_WIDTH),
                  halo(CONV_WIDTH), halo(CONV_WIDTH),
                  _resident((CONV_K, CONV_WIDTH)), _resident((1, Q_WIDTH)), _resident((1, CONV_WIDTH)),
                  _resident(w_out.shape)],
        out_specs=[tile(Q_WIDTH), tile(CONV_WIDTH), tile(CONV_WIDTH),
                   whole((1, Q_WIDTH)), whole((1, CONV_WIDTH)), whole((CONV_K, CONV_WIDTH))],
        out_shape=[SDS((seq, Q_WIDTH), BF16), SDS((seq, CONV_WIDTH), BF16), SDS((seq, CONV_WIDTH), BF16),
                   SDS((1, Q_WIDTH), F32), SDS((1, CONV_WIDTH), F32), SDS((CONV_K, CONV_WIDTH), F32)],
        operands=(dmix, attn, gb, gc, xin, gc, xin, conv_w, g_attn, g_conv, w_out))


def _attention_bwd(q, dattn, attn, kd0, kd1, vd0, vd1, sinks, comm=None):
    seq = q.shape[0]
    nb = ATTN_BWD_BLOCKS

    def body(sink_ref, q_ref, do_ref, o_ref, kd0_ref, kd1_ref, vd0_ref, vd1_ref,
             dq_ref, dk0_ref, dk1_ref, dv0_ref, dv1_ref, dsink_ref):
        @pl.when(pl.program_id(0) == 0)
        def _():
            for r in (dk0_ref, dk1_ref, dv0_ref, dv1_ref, dsink_ref):
                r[...] = jnp.zeros_like(r)

        lane = lax.broadcasted_iota(jnp.int32, (1, 128), 1)
        dsink = jnp.zeros((1, 128), F32)
        for b in range(nb):
            i = pl.program_id(0) * nb + b
            rows = slice(QBLOCK * b, QBLOCK * (b + 1))
            valid = _attn_valid(i)
            for kv_head, (k_ref, v_ref, dk_ref, dv_ref) in enumerate(
                    ((kd0_ref, vd0_ref, dk0_ref, dv0_ref), (kd1_ref, vd1_ref, dk1_ref, dv1_ref))):
                kband, prev, own = _band(k_ref, i)
                vband, _, _ = _band(v_ref, i)
                base = 256 * kv_head
                qm = _stack_heads(q_ref[rows, base:base + 128], q_ref[rows, base + 128:base + 256])
                dom = _stack_heads(do_ref[rows, base:base + 128], do_ref[rows, base + 128:base + 256])
                om = _stack_heads(o_ref[rows, base:base + 128], o_ref[rows, base + 128:base + 256])
                s = jnp.where(valid, _dot_nt(qm, kband), NEG_INF)
                p, e_sink, inv_l = _softmax_with_sink(s, _sink_column(sink_ref, kv_head))
                p = p * inv_l
                delta = jnp.sum(dom.astype(F32) * om.astype(F32), axis=-1, keepdims=True)
                ds = (p * (_dot_nt(dom, vband) - delta)).astype(BF16)
                sink_term = -(e_sink * inv_l) * delta
                for j in range(4):
                    part = jnp.sum(sink_term[QBLOCK * j:QBLOCK * (j + 1)], axis=0, keepdims=True)
                    dsink = dsink + jnp.where(lane == 4 * kv_head + j, part, 0.0)
                pair0, pair1 = _unstack_heads(_dot(ds, kband))
                dq_ref[rows, base:base + 128] = pair0.astype(BF16)
                dq_ref[rows, base + 128:base + 256] = pair1.astype(BF16)
                dkd = _dot_tn(ds, qm)
                dkd = dkd + pltpu.roll(dkd, HEAD_DIM, 1)
                dvd = _dot_tn(p.astype(BF16), dom)
                dvd = dvd + pltpu.roll(dvd, HEAD_DIM, 1)
                dk_ref[pl.ds(prev, QBLOCK), :] += dkd[0:QBLOCK]
                dk_ref[pl.ds(own, QBLOCK), :] += dkd[QBLOCK:]
                dv_ref[pl.ds(prev, QBLOCK), :] += dvd[0:QBLOCK]
                dv_ref[pl.ds(own, QBLOCK), :] += dvd[QBLOCK:]
        dsink_ref[...] += dsink

    blk = pl.BlockSpec((nb * QBLOCK, Q_WIDTH), lambda i: (i, 0))
    full = _resident((seq, 128))
    acc = pl.BlockSpec((seq, 128), lambda i: (0, 0))
    return _pallas(
        body, name="attention_bwd", grid=(seq // (nb * QBLOCK),),
        in_specs=[pl.BlockSpec(memory_space=pltpu.SMEM), blk, blk, blk, full, full, full, full],
        out_specs=[blk, acc, acc, acc, acc, pl.BlockSpec((1, 128), lambda i: (0, 0))],
        out_shape=[SDS((seq, Q_WIDTH), BF16)] + [SDS((seq, 128), F32)] * 4 + [SDS((1, 128), F32)],
        operands=(sinks, q, dattn, attn, kd0, kd1, vd0, vd1), comm=comm)


def _in_proj_bwd(dq, dk0, dk1, dv0, dv1, dgb, dy, gc, xin, conv_w, x, dh, g_pre, w_in_t, rope):
    seq = x.shape[0]
    tb = min(seq, WIDE_TOKEN_TILE)
    n_tiles = seq // tb

    def body(dq_ref, dk0_ref, dk1_ref, dv0_ref, dv1_ref, dgb_ref, dy_ref, dyh_ref, gc_ref, xin_ref, cw_ref,
             x_ref, dh_ref, g_ref, w_ref, c_ref, sa_ref, sb_ref,
             dproj_ref, gx_ref, dg_ref):
        i = pl.program_id(0)

        @pl.when(i == 0)
        def _():
            dg_ref[...] = jnp.zeros_like(dg_ref)

        dy = dy_ref[...].astype(F32)
        ext = jnp.concatenate([dy, jnp.where(i == n_tiles - 1, 0.0, dyh_ref[...].astype(F32))], axis=0)
        dy1 = pltpu.roll(ext, tb + HALO - 1, 0)[0:tb]
        dy2 = pltpu.roll(ext, tb + HALO - 2, 0)[0:tb]
        cw = cw_ref[...]
        du = cw[2:3, :] * dy + cw[1:2, :] * dy1 + cw[0:1, :] * dy2
        scale = 1.0 / math.sqrt(HEAD_DIM)
        base = Q_WIDTH + 2 * KV_WIDTH
        halves = [slice(0, tb // 2), slice(tb // 2, tb)]
        low = _lane_lt64((tb // 2, 128))
        for rows in halves:
            c, sa, sb = _rope_tile(c_ref.at[rows, :], sa_ref, sb_ref)
            for p in range(Q_WIDTH // 128):
                dproj_ref[rows, 128 * p:128 * (p + 1)] = _rope_transposed(
                    dq_ref[rows, 128 * p:128 * (p + 1)].astype(F32) * scale, c, sa, sb).astype(BF16)
            dk = jnp.where(low, dk0_ref[rows, :], dk1_ref[rows, :])
            dproj_ref[rows, Q_WIDTH:Q_WIDTH + KV_WIDTH] = _rope_transposed(dk, c, sa, sb).astype(BF16)
            dproj_ref[rows, Q_WIDTH + KV_WIDTH:base] = jnp.where(low, dv0_ref[rows, :], dv1_ref[rows, :]).astype(BF16)
            dproj_ref[rows, base:base + CONV_WIDTH] = dgb_ref[rows, :]
            dproj_ref[rows, base + CONV_WIDTH:base + 2 * CONV_WIDTH] = (du[rows] * xin_ref[rows, :].astype(F32)).astype(BF16)
            dproj_ref[rows, base + 2 * CONV_WIDTH:] = (du[rows] * gc_ref[rows, :].astype(F32)).astype(BF16)
        w_all = w_ref[...].reshape(IN_COLS, D_MODEL)
        dhn = [_dot(dproj_ref[rows, :], w_all) for rows in halves]
        dg = jnp.zeros((1, D_MODEL), F32)
        for k, rows in enumerate(halves):
            xv = x_ref[rows, :]
            r = _rms(xv)
            xhat = xv * r
            dg = dg + _colsum(dhn[k] * xhat)
            gx_ref[rows, :] = dh_ref[rows, :].astype(F32) + _norm_bwd(dhn[k], g_ref[...], xhat, r)
        dg_ref[...] += dg

    tile = lambda w: pl.BlockSpec((tb, w), lambda i: (i, 0))
    halo_next = pl.BlockSpec((HALO, CONV_WIDTH), lambda i: (jnp.minimum((i + 1) * (tb // HALO), seq // HALO - 1), 0))
    return _pallas(
        body, name="in_proj_bwd", grid=(n_tiles,),
        in_specs=[tile(Q_WIDTH), tile(128), tile(128), tile(128), tile(128), tile(CONV_WIDTH), tile(CONV_WIDTH), halo_next,
                  tile(CONV_WIDTH), tile(CONV_WIDTH), _resident((CONV_K, CONV_WIDTH)),
                  tile(D_MODEL), tile(D_MODEL), _resident((1, D_MODEL)), _resident(w_in_t.shape), *_rope_specs(tb)],
        out_specs=[tile(IN_COLS), tile(D_MODEL), pl.BlockSpec((1, D_MODEL), lambda i: (0, 0))],
        out_shape=[SDS((seq, IN_COLS), BF16), SDS((seq, D_MODEL), F32), SDS((1, D_MODEL), F32)],
        operands=(dq, dk0, dk1, dv0, dv1, dgb, dy, dy, gc, xin, conv_w, x, dh, g_pre, w_in_t, *rope))


def _wgrad_grid(seq, per_chip, h_rows, with_rider=False):
    chips_per_step = 1 if per_chip else N_CHIPS
    m = chips_per_step * 2 * h_rows
    bt = min(seq, WGRAD_TOKEN_TILE if per_chip and not with_rider else WGRAD_TOKEN_TILE // 2)
    return chips_per_step, m, bt, seq // bt


def _wgrad(name, a, b, *, per_chip, h_rows, square_a=False, comm=None, rider=None):
    seq = a.shape[0]
    chips_per_step, m, bt, n_k = _wgrad_grid(seq, per_chip, h_rows, rider is not None)
    a_cols = m if per_chip else a.shape[1]
    a_wide = a.shape[1] > a_cols
    b_wide = b.shape[1] > D_MODEL

    def body(a_ref, b_ref, g_ref):
        @pl.when(pl.program_id(1) == 0)
        def _():
            g_ref[...] = jnp.zeros_like(g_ref)

        av = a_ref[...]
        if square_a:
            av = (av.astype(F32) * av.astype(F32)).astype(BF16)
        g_ref[...] += _dot_tn(av, b_ref[...]).reshape(g_ref.shape)

    a_spec = pl.BlockSpec((bt, a_cols), (lambda j, k: (k, j)) if a_wide else (lambda j, k: (k, 0)))
    b_spec = pl.BlockSpec((bt, D_MODEL), (lambda j, k: (k, j)) if b_wide else (lambda j, k: (k, 0)))
    g_spec = pl.BlockSpec((chips_per_step, 2, h_rows, D_MODEL), lambda j, k: (j, 0, 0, 0),
                          pipeline_mode=None if per_chip else pl.Buffered(1))
    return _pallas(
        body, name=name, grid=(N_CHIPS if per_chip else 1, n_k),
        in_specs=[a_spec, b_spec], out_specs=[g_spec], out_shape=[SDS((N_CHIPS, 2, h_rows, D_MODEL), F32)],
        operands=(a, b), comm=comm, rider=rider)


def _adamw_math(w, g, m, v):
    m = ADAM_B1 * m + (1.0 - ADAM_B1) * g
    v = ADAM_B2 * v + (1.0 - ADAM_B2) * (g * g)
    m_hat = m / (1.0 - ADAM_B1 ** ADAM_STEP)
    v_hat = v / (1.0 - ADAM_B2 ** ADAM_STEP)
    delta = -ADAM_LR * (m_hat / (jnp.sqrt(v_hat) + ADAM_EPS) + ADAM_WD * w)
    return delta, m, v


def _adamw_rows(name, reduced, w, m, v, rt):
    per_half = reduced.shape[1] // rt

    def body(r_ref, w_ref, m_ref, v_ref, g_out, d_out, m_out, v_out):
        g = r_ref[0]
        g_out[...] = g
        d_out[...], m_out[...], v_out[...] = _adamw_math(w_ref[...], g, m_ref[...], v_ref[...])

    blk = pl.BlockSpec((rt, D_MODEL), lambda h, r: (h * per_half + r, 0))
    return _pallas(
        body, name=name, grid=(2, per_half),
        in_specs=[pl.BlockSpec((1, rt, D_MODEL), lambda h, r: (h, r, 0)), blk, blk, blk],
        out_specs=[blk, blk, blk, blk], out_shape=[SDS(w.shape, F32)] * 4, operands=(reduced, w, m, v))


def _adamw_small(packed_grads, w, m, v):
    names = SMALL_NAMES
    n = len(names)
    conv_local = w["conv_w"].shape[-1]

    def body(*refs):
        gp = refs[0]
        w_refs, m_refs, v_refs = refs[1:1 + n], refs[1 + n:1 + 2 * n], refs[1 + 2 * n:1 + 3 * n]
        outs = refs[1 + 3 * n:]
        g_out, d_out, m_out, v_out = outs[0:n], outs[n:2 * n], outs[2 * n:3 * n], outs[3 * n:4 * n]
        chip = 2 * lax.axis_index("x") + lax.axis_index("y")

        def step(k, g, index=None):
            pick = (lambda r: r[...]) if index is None else (lambda r: r[index])
            d, new_m, new_v = _adamw_math(pick(w_refs[k]), g, pick(m_refs[k]), pick(v_refs[k]))
            for ref, val in ((g_out[k], g), (d_out[k], d), (m_out[k], new_m), (v_out[k], new_v)):
                if index is None:
                    ref[...] = val
                else:
                    ref[index] = val

        for k, name in enumerate(names):
            if name in SMALL_VECTORS:
                step(k, gp[SMALL_VECTORS.index(name):SMALL_VECTORS.index(name) + 1, :])
            elif name == "attn_group_norm":
                step(k, gp[4:5, 0:Q_WIDTH])
            elif name == "conv_group_norm":
                step(k, gp[4:5, Q_WIDTH:])
            elif name == "attn_sinks":
                step(k, gp[7:8, 0:8])
            else:
                for t in range(CONV_K):
                    row, base = 5 + t // 2, CONV_WIDTH * (t % 2)
                    g = gp[row:row + 1, base:base + conv_local]
                    for j in range(1, CONV_WIDTH // conv_local):
                        g = jnp.where(chip == j, gp[row:row + 1, base + conv_local * j:base + conv_local * (j + 1)], g)
                    step(k, g, index=(0, slice(t, t + 1), slice(None)))

    shapes = [SDS(w[name].shape, F32) for name in names]
    res = pl.pallas_call(
        body, name="adamw_small", in_specs=[VMEM_WHOLE] * (1 + 3 * n), out_specs=[VMEM_WHOLE] * (4 * n),
        out_shape=shapes * 4,
    )(packed_grads, *[w[k] for k in names], *[m[k] for k in names], *[v[k] for k in names])
    return [dict(zip(names, res[i * n:(i + 1) * n])) for i in range(4)]


SMALL_VECTORS = ("pre_mix_norm", "post_mix_norm", "pre_mlp_norm", "post_mlp_norm")
SMALL_NAMES = SMALL_VECTORS + ("attn_group_norm", "conv_group_norm", "conv_w", "attn_sinks")


def _pack_small(p):
    rows = [p[n].reshape(1, D_MODEL) for n in SMALL_VECTORS]
    rows.append(jnp.concatenate([p["attn_group_norm"].reshape(1, -1), p["conv_group_norm"].reshape(1, -1)], axis=1))
    cw = p["conv_w"].reshape(CONV_K, -1)
    rows.append(jnp.pad(cw, ((0, 1), (0, CONV_WIDTH - cw.shape[1]))).reshape(2, D_MODEL))
    last = jnp.concatenate([p["attn_sinks"].reshape(1, 8), p.get("loss_sum", jnp.zeros((1, 1), F32))], axis=1)
    rows.append(jnp.pad(last, ((0, 0), (0, D_MODEL - 9))))
    return jnp.concatenate(rows, axis=0)


WEIGHT_ORDER = ("pre_mix_norm", "w_in", "conv_w", "attn_sinks", "attn_group_norm", "conv_group_norm", "w_out",
                "post_mix_norm", "pre_mlp_norm", "w_up", "w_down", "post_mlp_norm")


def kernel(x, pre_mix_norm, w_in, conv_w, attn_sinks, attn_group_norm, conv_group_norm, w_out, post_mix_norm, pre_mlp_norm, w_up, w_down, post_mlp_norm, loss_target, m_pre_mix_norm, m_w_in, m_conv_w, m_attn_sinks, m_attn_group_norm, m_conv_group_norm, m_w_out, m_post_mix_norm, m_pre_mlp_norm, m_w_up, m_w_down, m_post_mlp_norm, v_pre_mix_norm, v_w_in, v_conv_w, v_attn_sinks, v_attn_group_norm, v_conv_group_norm, v_w_out, v_post_mix_norm, v_pre_mlp_norm, v_w_up, v_w_down, v_post_mlp_norm):
    w = dict(pre_mix_norm=pre_mix_norm, w_in=w_in, conv_w=conv_w, attn_sinks=attn_sinks, attn_group_norm=attn_group_norm,
             conv_group_norm=conv_group_norm, w_out=w_out, post_mix_norm=post_mix_norm, pre_mlp_norm=pre_mlp_norm,
             w_up=w_up, w_down=w_down, post_mlp_norm=post_mlp_norm)
    m = dict(pre_mix_norm=m_pre_mix_norm, w_in=m_w_in, conv_w=m_conv_w, attn_sinks=m_attn_sinks,
             attn_group_norm=m_attn_group_norm, conv_group_norm=m_conv_group_norm, w_out=m_w_out,
             post_mix_norm=m_post_mix_norm, pre_mlp_norm=m_pre_mlp_norm, w_up=m_w_up, w_down=m_w_down,
             post_mlp_norm=m_post_mlp_norm)
    v = dict(pre_mix_norm=v_pre_mix_norm, w_in=v_w_in, conv_w=v_conv_w, attn_sinks=v_attn_sinks,
             attn_group_norm=v_attn_group_norm, conv_group_norm=v_conv_group_norm, w_out=v_w_out,
             post_mix_norm=v_post_mix_norm, pre_mlp_norm=v_pre_mlp_norm, w_up=v_w_up, w_down=v_w_down,
             post_mlp_norm=v_post_mlp_norm)
    core = lax.axis_index("c").astype(jnp.int32).reshape(1)
    xs, target = x[0], loss_target[0]
    rope = _rope_inputs(xs.shape[0])

    conv_pad = jnp.pad(conv_w[0], ((0, 8 - CONV_K), (0, 0)))
    wf_in, conv_all, hb_up, hb_down, hb_out = _gather_whole(w_in[0].T, (w_up[0], w_down[0], w_out[0]), conv_pad)
    conv_full = conv_all[:, :CONV_K, :].transpose(1, 0, 2).reshape(CONV_K, CONV_WIDTH)

    whole_up, early, late = (0, H_UP), (0, DOWN_EARLY_ROWS), (DOWN_EARLY_ROWS, H_DOWN - DOWN_EARLY_ROWS)
    *proj, wf_up, wf_out, wf_down = _in_proj(
        xs, pre_mix_norm, wf_in, rope,
        comm=_merge(_relay(hb_up, None, first=whole_up), _gather_first(hb_out), _relay(hb_down, None, first=early)))
    q, kd0, kd1, vd0, vd1, gb, gc, xin, hn = proj
    attn, wf_up, wf_out, wf_down = _attention_fwd(
        q, kd0, kd1, vd0, vd1, attn_sinks,
        comm=_merge(_relay(None, wf_up, second=whole_up), _gather_second(wf_out),
                    _relay(hb_down, wf_down, first=late, second=early)))
    mix, mixed, wf_up, wf_down = _mix_out(
        attn, gb, gc, xin, conv_full, attn_group_norm, conv_group_norm, wf_out,
        comm=_merge(_relay(None, wf_up, third=whole_up), _relay(None, wf_down, second=late, third=early, third_after=late)))
    up, hn2, dmlp, dup, dh, dmix, loss_sum, dg_post_mlp, dg_pre_mlp, dg_post_mix = _mlp_fwd_bwd(
        xs, mix, target, post_mix_norm, pre_mlp_norm, post_mlp_norm, wf_up, wf_down)

    n_k = _wgrad_grid(xs.shape[0], True, H_DOWN, with_rider=True)[3]
    g_down, dattn, dgb, dy, dg_attn, dg_conv, dconv_w = _wgrad(
        "wgrad_down", up, dmlp, per_chip=True, h_rows=H_DOWN, square_a=True,
        rider=_mix_bwd(dmix, attn, gb, gc, xin, conv_full, attn_group_norm, conv_group_norm, wf_out, n_k))
    g_up, got_down = _wgrad("wgrad_up", hn2, dup, per_chip=True, h_rows=H_UP, comm=_pair_send(g_down))
    p_down = _pair_sum("pair_sum_down", core, g_down, got_down)
    g_out, got_up = _wgrad("wgrad_out", mixed, dmix, per_chip=False, h_rows=H_OUT, comm=_pair_send(g_up))
    p_up = _pair_sum("pair_sum_up", core, g_up, got_up)
    dq, dk0, dk1, dv0, dv1, dsink, ex_down, ex_up, got_out = _attention_bwd(
        q, dattn, attn, kd0, kd1, vd0, vd1, attn_sinks,
        comm=_merge(_chip_exchange(p_down), _chip_exchange(p_up), _pair_send(g_out)))
    p_out = _pair_sum("pair_sum_out", core, g_out, got_out)
    dproj, grad_x, dg_pre_mix = _in_proj_bwd(dq, dk0, dk1, dv0, dv1, dgb, dy, gc, xin, conv_full, xs, dh, pre_mix_norm,
                                             wf_in, rope)
    g_in, ex_out = _wgrad("wgrad_in", dproj, hn, per_chip=False, h_rows=H_IN, comm=_chip_exchange(p_out))
    small = dict(pre_mix_norm=dg_pre_mix, conv_w=dconv_w, attn_sinks=dsink[:, :8], attn_group_norm=dg_attn,
                 conv_group_norm=dg_conv, post_mix_norm=dg_post_mix, pre_mlp_norm=dg_pre_mlp, post_mlp_norm=dg_post_mlp,
                 loss_sum=loss_sum)
    r_down, r_up, r_out, r_in, small_total = _tail_reduce(g_in, [ex_down, ex_up, ex_out], _pack_small(small))

    out_g, out_d, out_m, out_v = {}, {}, {}, {}
    out_g["w_up"], out_d["w_up"], out_m["w_up"], out_v["w_up"] = _adamw_rows(
        "adamw_up", r_up, w_up[0], m_w_up[0], v_w_up[0], 256)
    out_g["w_down"], out_d["w_down"], out_m["w_down"], out_v["w_down"] = _adamw_rows(
        "adamw_down", r_down, w_down[0], m_w_down[0], v_w_down[0], 256)
    out_g["w_out"], out_d["w_out"], out_m["w_out"], out_v["w_out"] = _adamw_rows(
        "adamw_out", r_out, w_out[0], m_w_out[0], v_w_out[0], H_OUT)
    in_t = _adamw_rows("adamw_in", r_in, w_in[0].T, m_w_in[0].T, v_w_in[0].T, H_IN)
    out_g["w_in"], out_d["w_in"], out_m["w_in"], out_v["w_in"] = [t.T for t in in_t]

    loss = small_total[7, 8] * (0.5 / D_MODEL)
    for out, part in zip((out_g, out_d, out_m, out_v), _adamw_small(small_total, w, m, v)):
        out.update(part)

    def shaped(d):
        return [d[n].reshape(w[n].shape) for n in WEIGHT_ORDER]

    return (loss, grad_x[None], *shaped(out_g), *shaped(out_d), *shaped(out_m), *shaped(out_v))
```

```python
import math
from typing import Callable, NamedTuple

import jax
import jax.numpy as jnp
import numpy as np
from jax import lax
from jax.experimental import pallas as pl
from jax.experimental.pallas import tpu as pltpu

F32 = jnp.float32
BF16 = jnp.bfloat16

D_MODEL = 1024
HEAD_DIM = 64
Q_WIDTH = 512
KV_WIDTH = 128
CONV_WIDTH = 512
CONV_K = 3
D_FF = 4096
IN_COLS = 2304
QBLOCK = 128
ROT_DIM = 16
ROPE_THETA = 500000.0
NORM_EPS = 1e-6
NEG_INF = -1e30
N_CHIPS = 4

ADAM_LR = 0.001
ADAM_B1 = 0.9
ADAM_B2 = 0.999
ADAM_EPS = 1e-08
ADAM_WD = 0.01
ADAM_STEP = 10

H_UP, H_DOWN, H_OUT, H_IN = 512, 512, 128, 288
DOWN_EARLY_ROWS = 224

TOKEN_TILE = 512
WIDE_TOKEN_TILE = 1024
ATTN_FWD_BLOCKS = 16
ATTN_BWD_BLOCKS = 2
WGRAD_TOKEN_TILE = 4096
VMEM_LIMIT_V7X = 60 * 1024 * 1024

MESH = pl.DeviceIdType.MESH
ANY = pl.BlockSpec(memory_space=pl.ANY)
VMEM_WHOLE = pl.BlockSpec(memory_space=pltpu.VMEM)
SDS = jax.ShapeDtypeStruct


def _resident(shape):
    zeros = (0,) * len(shape)
    return pl.BlockSpec(shape, lambda *_: zeros, pipeline_mode=pl.Buffered(1))


def _rms(v):
    return lax.rsqrt(jnp.mean(v * v, axis=-1, keepdims=True) + NORM_EPS)


def _norm_bwd(dy, gain, vhat, rstd):
    t = dy * gain
    return rstd * (t - vhat * jnp.mean(t * vhat, axis=-1, keepdims=True))


def _colsum(v):
    return jnp.sum(v, axis=0, keepdims=True)


def _dot_nt(a, b):
    return lax.dot_general(a, b, (((1,), (1,)), ((), ())), preferred_element_type=F32)


def _dot_tn(a, b):
    return lax.dot_general(a, b, (((0,), (0,)), ((), ())), preferred_element_type=F32)


def _dot(a, b):
    return jnp.dot(a, b, preferred_element_type=F32)


def _chip_block(w_ref, chip):
    both = w_ref[pl.ds(2 * chip, 2)]
    return both.reshape(2 * both.shape[1], both.shape[2])


def _lane_lt64(shape):
    return lax.broadcasted_iota(jnp.int32, shape, 1) < HEAD_DIM


class _Comm(NamedTuple):
    operands: tuple
    out_shapes: tuple
    aliases: dict
    n_remote: int
    n_local: int
    plan: Callable
    after: Callable = None


def _merge(*comms):
    operands, out_shapes, aliases, parts = [], [], {}, []
    n_remote = n_local = 0
    for cm in comms:
        parts.append((len(operands), len(out_shapes), n_remote, n_local, cm))
        for k, v in cm.aliases.items():
            aliases[len(operands) + k] = len(out_shapes) + v
        operands += cm.operands
        out_shapes += cm.out_shapes
        n_remote += cm.n_remote
        n_local += cm.n_local

    def run(which, ins, outs, send, recv, loc):
        sends, recvs, locs = [], [], []
        for i0, o0, r0, l0, cm in parts:
            stage = getattr(cm, which)
            if stage is not None:
                s, r, l = stage(ins[i0:i0 + len(cm.operands)], outs[o0:o0 + len(cm.out_shapes)],
                                lambda k, r0=r0: send(r0 + k), lambda k, r0=r0: recv(r0 + k), lambda k, l0=l0: loc(l0 + k))
                sends, recvs, locs = sends + s, recvs + r, locs + l
        return sends, recvs, locs

    def plan(*args):
        return run("plan", *args)

    def after(*args):
        return run("after", *args)

    return _Comm(tuple(operands), tuple(out_shapes), aliases, n_remote, n_local, plan,
                 after if any(cm.after is not None for cm in comms) else None)


def _sem_scratch(comm):
    return [pltpu.SemaphoreType.DMA((max(comm.n_remote, 1),)), pltpu.SemaphoreType.DMA((max(comm.n_remote, 1),)),
            pltpu.SemaphoreType.DMA((max(comm.n_local, 1),))]


class _Rider(NamedTuple):
    body: Callable
    in_specs: list
    out_specs: list
    out_shape: list
    operands: tuple


def _pallas(body, *, name, grid, in_specs, out_specs, out_shape, operands, scratch=(), comm=None, rider=None):
    params = pltpu.CompilerParams(dimension_semantics=("arbitrary",) * len(grid), vmem_limit_bytes=VMEM_LIMIT_V7X)
    if rider is not None:
        own_in, own_out, ride_in, ride_out = len(in_specs), len(out_specs), len(rider.in_specs), len(rider.out_specs)
        own_body = body

        def body(*refs):
            o0 = own_in + ride_in
            s0 = o0 + own_out + ride_out
            own_body(*refs[:own_in], *refs[o0:o0 + own_out], *refs[s0:])
            first = None
            for axis in range(len(grid)):
                at_start = pl.program_id(axis) == 0
                first = at_start if first is None else jnp.logical_and(first, at_start)
            rider.body(first, *refs[own_in:o0], *refs[o0 + own_out:s0])

        in_specs, out_specs = list(in_specs) + rider.in_specs, list(out_specs) + rider.out_specs
        out_shape, operands = list(out_shape) + rider.out_shape, tuple(operands) + tuple(rider.operands)
    if comm is None:
        return pl.pallas_call(body, name=name, grid=grid, in_specs=in_specs, out_specs=out_specs, out_shape=out_shape,
                              scratch_shapes=list(scratch), compiler_params=params)(*operands)
    n_in, n_out, n_scr = len(in_specs), len(out_specs), len(scratch)
    c_in, c_out = len(comm.operands), len(comm.out_shapes)

    def with_comm(*refs):
        ins, c_ins = refs[:n_in], refs[n_in:n_in + c_in]
        o0 = n_in + c_in
        outs, c_outs = refs[o0:o0 + n_out], refs[o0 + n_out:o0 + n_out + c_out]
        s0 = o0 + n_out + c_out
        scr = refs[s0:s0 + n_scr]
        send_sems, recv_sems, local_sems = refs[s0 + n_scr:]
        first = last = None
        for axis, size in enumerate(grid):
            at_start, at_end = pl.program_id(axis) == 0, pl.program_id(axis) == size - 1
            first = at_start if first is None else jnp.logical_and(first, at_start)
            last = at_end if last is None else jnp.logical_and(last, at_end)

        def copies():
            return comm.plan(c_ins, c_outs, lambda k: send_sems.at[k], lambda k: recv_sems.at[k],
                             lambda k: local_sems.at[k])

        @pl.when(first)
        def _():
            sends, _, locs = copies()
            for cp in sends + locs:
                cp.start()

        body(*ins, *outs, *scr)

        @pl.when(last)
        def _():
            sends, recvs, locs = copies()
            for cp in recvs:
                cp.wait_recv()
            for cp in sends:
                cp.wait_send()
            for cp in locs:
                cp.wait()
            if comm.after is not None:
                sends, recvs, _ = comm.after(c_ins, c_outs, lambda k: send_sems.at[k], lambda k: recv_sems.at[k],
                                             lambda k: local_sems.at[k])
                for cp in sends:
                    cp.start()
                for cp in recvs:
                    cp.wait_recv()
                for cp in sends:
                    cp.wait_send()

    return pl.pallas_call(
        with_comm, name=name, grid=grid,
        in_specs=list(in_specs) + [ANY] * c_in, out_specs=list(out_specs) + [ANY] * c_out,
        out_shape=list(out_shape) + list(comm.out_shapes),
        scratch_shapes=list(scratch) + _sem_scratch(comm),
        input_output_aliases={n_in + k: n_out + v for k, v in comm.aliases.items()},
        compiler_params=params)(*operands, *comm.operands)


def _place():
    return lax.axis_index("x"), lax.axis_index("y"), lax.axis_index("c")


def _other_chips(x, y):
    return [(1 - x, y), (x, 1 - y), (1 - x, 1 - y)]


def _slot(px, py, pc):
    return 4 * px + 2 * py + pc


def _remote(src, dst, send_sem, recv_sem, to):
    return pltpu.make_async_remote_copy(src_ref=src, dst_ref=dst, send_sem=send_sem, recv_sem=recv_sem,
                                        device_id=to, device_id_type=MESH)


def _gather_first(half_block):
    def plan(ins, outs, send, recv, loc):
        (blk,), (full,) = ins, outs
        x, y, c = _place()
        chips = _other_chips(x, y)
        mine = full.at[_slot(x, y, c)]
        sends = [_remote(blk, mine, send(0), recv(0), (x, y, 1 - c))]
        sends += [_remote(blk, mine, send(1 + j), recv(1 + j), (*chip, c)) for j, chip in enumerate(chips)]
        recvs = [_remote(blk, full.at[_slot(x, y, 1 - c)], send(0), recv(0), (x, y, 1 - c))]
        recvs += [_remote(blk, full.at[_slot(*chip, c)], send(1 + j), recv(1 + j), (*chip, c))
                  for j, chip in enumerate(chips)]
        return sends, recvs, [pltpu.make_async_copy(blk, mine, loc(0))]

    return _Comm((half_block,), (SDS((2 * N_CHIPS,) + half_block.shape, half_block.dtype),), {}, 4, 1, plan)


def _gather_second(partly_gathered):
    def plan(ins, outs, send, recv, loc):
        (src,), (full,) = ins, outs
        x, y, c = _place()
        chips = _other_chips(x, y)
        sends = [_remote(src.at[_slot(*chip, c)], full.at[_slot(*chip, c)], send(j), recv(j), (x, y, 1 - c))
                 for j, chip in enumerate(chips)]
        recvs = [_remote(src.at[_slot(*chip, 1 - c)], full.at[_slot(*chip, 1 - c)], send(j), recv(j), (x, y, 1 - c))
                 for j, chip in enumerate(chips)]
        return sends, recvs, []

    return _Comm((partly_gathered,), (SDS(partly_gathered.shape, partly_gathered.dtype),), {0: 0}, 3, 0, plan)


def _relay_pieces(full, rows, x, y, c):
    start, half = rows[0], rows[1] // 2
    upper, lower = pl.ds(start, half), pl.ds(start + half, half)
    diagonal = full.at[_slot(1 - x, 1 - y, c)]
    return [(full.at[_slot(1 - x, y, c), upper], diagonal.at[upper], (x, 1 - y, c)),
            (full.at[_slot(x, 1 - y, c), lower], diagonal.at[lower], (1 - x, y, c))]


def _relay(half_block, so_far, first=None, second=None, third=None, third_after=None):
    has_block, has_buffer = half_block is not None, so_far is not None
    shape = so_far.shape if has_buffer else (2 * N_CHIPS,) + half_block.shape
    dtype = so_far.dtype if has_buffer else half_block.dtype

    def third_leg(rows, k, ins, outs, send, recv):
        src, full = (ins[-1] if has_buffer else outs[0]), outs[0]
        x, y, c = _place()
        span, sibling = pl.ds(*rows), (x, y, 1 - c)
        here, there = _slot(1 - x, 1 - y, c), _slot(1 - x, 1 - y, 1 - c)
        return ([_remote(src.at[here, span], full.at[here, span], send(k), recv(k), sibling)],
                [_remote(src.at[there, span], full.at[there, span], send(k), recv(k), sibling)])

    def plan(ins, outs, send, recv, loc):
        src, full = (ins[-1] if has_buffer else outs[0]), outs[0]
        x, y, c = _place()
        sibling = (x, y, 1 - c)
        sends, recvs, locs = [], [], []
        if first is not None:
            span = pl.ds(*first)
            blk, mine = ins[0].at[span], full.at[_slot(x, y, c), span]
            for k, peer in enumerate([sibling, (1 - x, y, c), (x, 1 - y, c)]):
                sends.append(_remote(blk, mine, send(k), recv(k), peer))
                recvs.append(_remote(blk, full.at[_slot(*peer), span], send(k), recv(k), peer))
            locs.append(pltpu.make_async_copy(blk, mine, loc(0)))
        if second is not None:
            span = pl.ds(*second)
            for k, chip in enumerate([(1 - x, y), (x, 1 - y)]):
                sends.append(_remote(src.at[_slot(*chip, c), span], full.at[_slot(*chip, c), span], send(3 + k), recv(3 + k),
                                     sibling))
                recvs.append(_remote(src.at[_slot(*chip, 1 - c), span], full.at[_slot(*chip, 1 - c), span], send(3 + k),
                                     recv(3 + k), sibling))
            for k, (piece, lands, peer) in enumerate(_relay_pieces(full, second, x, y, c)):
                sends.append(_remote(piece, piece, send(5 + k), recv(5 + k), peer))
                recvs.append(_remote(lands, lands, send(5 + k), recv(5 + k), peer))
        if third is not None:
            s, r = third_leg(third, 7, ins, outs, send, recv)
            sends, recvs = sends + s, recvs + r
        return sends, recvs, locs

    def after(ins, outs, send, recv, loc):
        s, r = third_leg(third_after, 8, ins, outs, send, recv)
        return s, r, []

    operands = ((half_block,) if has_block else ()) + ((so_far,) if has_buffer else ())
    return _Comm(operands, (SDS(shape, dtype),), {len(operands) - 1: 0} if has_buffer else {}, 9, 1, plan,
                 after if third_after is not None else None)


def _gather_whole(first, others, small_block):
    shards = (first, *others)
    n = len(shards)
    hs = [s.shape[0] // 2 for s in shards]
    rows = hs[0]

    def body(*refs):
        src, small_ref = refs[:n], refs[n]
        out_ref, small_out_ref, half_out = refs[n + 1], refs[n + 2], refs[n + 3:2 * n + 2]
        stage, half = refs[2 * n + 2:3 * n + 2], refs[3 * n + 2:4 * n + 2]
        send_sems, recv_sems, local_sems = refs[4 * n + 2:]
        x, y, c = _place()
        me, sibling = (x, y, c), (x, y, 1 - c)
        neighbours, diagonal = [(1 - x, y), (x, 1 - y)], (1 - x, 1 - y)
        loads = [pltpu.make_async_copy(src[k].at[pl.ds(c * hs[k], hs[k])], stage[k], local_sems.at[2 + k]) for k in range(n)]
        for cp in loads:
            cp.start()
        loads[0].wait()
        blk_ref = half[0]
        blk_ref[...] = stage[0][...].astype(BF16)

        def copy(k, block, to, src=None):
            return _remote(out_ref.at[_slot(*block)] if src is None else src, out_ref.at[_slot(*block)],
                           send_sems.at[k], recv_sems.at[k], to)

        def small_copy(k, chip, to):
            return _remote(small_ref, small_out_ref.at[2 * chip[0] + chip[1]], send_sems.at[8 + k], recv_sems.at[8 + k], to)

        mine = pltpu.make_async_copy(blk_ref, out_ref.at[_slot(*me)], local_sems.at[0])
        mine_small = pltpu.make_async_copy(small_ref, small_out_ref.at[2 * x + y], local_sems.at[1])
        mine.start()
        mine_small.start()
        started = [copy(0, me, sibling, src=blk_ref)]
        started += [copy(1 + k, me, (*chip, c), src=blk_ref) for k, chip in enumerate(neighbours)]
        started += [small_copy(k, (x, y), (*chip, c)) for k, chip in enumerate(neighbours + [diagonal])]
        for cp in started:
            cp.start()
        stores = []
        for k in range(1, n):
            loads[k].wait()
            half[k][...] = stage[k][...].astype(BF16)
            stores.append(pltpu.make_async_copy(half[k], half_out[k - 1], local_sems.at[2 + n + k]))
            stores[-1].start()
        pieces = _relay_pieces(out_ref, (0, rows), x, y, c)
        for k, chip in enumerate(neighbours):
            copy(1 + k, (*chip, c), me).wait_recv()
            piece, _, peer = pieces[k]
            started += [copy(3 + k, (*chip, c), sibling), _remote(piece, piece, send_sems.at[5 + k], recv_sems.at[5 + k], peer)]
            started[-2].start()
            started[-1].start()
        for k, (_, lands, peer) in enumerate(pieces):
            _remote(lands, lands, send_sems.at[5 + k], recv_sems.at[5 + k], peer).wait_recv()
        started.append(copy(7, (*diagonal, c), sibling))
        started[-1].start()
        copy(0, sibling, me).wait_recv()
        for k, chip in enumerate(neighbours):
            copy(3 + k, (*chip, 1 - c), me).wait_recv()
        copy(7, (*diagonal, 1 - c), me).wait_recv()
        for k, chip in enumerate(neighbours + [diagonal]):
            small_copy(k, chip, me).wait_recv()
        for cp in started:
            cp.wait_send()
        mine.wait()
        mine_small.wait()
        for cp in stores:
            cp.wait()

    return pl.pallas_call(
        body, name="gather_whole", in_specs=[ANY] * (n + 1), out_specs=[ANY] * (n + 1),
        out_shape=[SDS((2 * N_CHIPS, rows, D_MODEL), BF16), SDS((N_CHIPS,) + small_block.shape, small_block.dtype)]
                  + [SDS((h, D_MODEL), BF16) for h in hs[1:]],
        scratch_shapes=[pltpu.VMEM((h, D_MODEL), F32) for h in hs] + [pltpu.VMEM((h, D_MODEL), BF16) for h in hs]
                       + [pltpu.SemaphoreType.DMA((11,)), pltpu.SemaphoreType.DMA((11,)), pltpu.SemaphoreType.DMA((2 + 2 * n,))],
        compiler_params=pltpu.CompilerParams(vmem_limit_bytes=VMEM_LIMIT_V7X),
    )(*shards, small_block)


def _pair_send(grads):
    def plan(ins, outs, send, recv, loc):
        (g,), (got,) = ins, outs
        x, y, c = _place()
        copies = [_remote(g.at[j, 1 - c], got.at[j], send(j), recv(j), (x, y, 1 - c)) for j in range(N_CHIPS)]
        return copies, copies, []

    shape = (grads.shape[0],) + grads.shape[2:]
    return _Comm((grads,), (SDS(shape, grads.dtype),), {}, N_CHIPS, 0, plan)


def _chip_exchange(partial):
    def plan(ins, outs, send, recv, loc):
        (p,), (got,) = ins, outs
        x, y, c = _place()
        my_chip = 2 * x + y
        chips = _other_chips(x, y)
        sends = [_remote(p.at[2 * chip[0] + chip[1]], got.at[my_chip], send(j), recv(j), (*chip, c))
                 for j, chip in enumerate(chips)]
        recvs = [_remote(p.at[my_chip], got.at[2 * chip[0] + chip[1]], send(j), recv(j), (*chip, c))
                 for j, chip in enumerate(chips)]
        return sends, recvs, [pltpu.make_async_copy(p.at[my_chip], got.at[my_chip], loc(0))]

    return _Comm((partial,), (SDS(partial.shape, partial.dtype),), {}, 3, 1, plan)


def _pair_sum(name, core, grads, received):
    h = grads.shape[2]

    def body(core_ref, g_ref, r_ref, o_ref):
        o_ref[...] = (g_ref[0] + r_ref[...]).astype(BF16)

    return pl.pallas_call(
        body, name=name,
        grid_spec=pltpu.PrefetchScalarGridSpec(
            num_scalar_prefetch=1, grid=(N_CHIPS,),
            in_specs=[pl.BlockSpec((1, 1, h, D_MODEL), lambda j, core_ref: (j, core_ref[0], 0, 0)),
                      pl.BlockSpec((1, h, D_MODEL), lambda j, core_ref: (j, 0, 0))],
            out_specs=pl.BlockSpec((1, h, D_MODEL), lambda j, core_ref: (j, 0, 0))),
        out_shape=SDS((N_CHIPS, h, D_MODEL), BF16),
        compiler_params=pltpu.CompilerParams(dimension_semantics=("arbitrary",), vmem_limit_bytes=VMEM_LIMIT_V7X),
    )(core, grads, received)


SMALL_ROWS = 8


def _sum_blocks(ref):
    return (ref[0].astype(F32) + ref[1].astype(F32)) + (ref[2].astype(F32) + ref[3].astype(F32))


def _tail_reduce(last_grads, exchanged, small):
    n = len(exchanged)
    h = last_grads.shape[2]

    def body(*refs):
        g_ref, ex, small_ref = refs[0], refs[1:1 + n], refs[1 + n]
        o0 = 2 + n
        out, out_last, small_out = refs[o0:o0 + n], refs[o0 + n], refs[o0 + n + 1]
        s0 = o0 + n + 2
        halves, half_last = refs[s0:s0 + n], refs[s0 + n]
        own, got, part, exch, small_buf = refs[s0 + n + 1:s0 + n + 6]
        ex_buf = refs[s0 + n + 6:s0 + 2 * n + 6]
        pair_send, pair_recv, chip_send, chip_recv, share_send, share_recv, small_send, small_recv, local_sems = refs[s0 + 2 * n + 6:]
        x, y, c = _place()
        sibling = (x, y, 1 - c)
        my_chip, me = 2 * x + y, _slot(x, y, c)
        chips = _other_chips(x, y)

        to_sibling = [_remote(g_ref.at[j, 1 - c], got.at[j], pair_send.at[j], pair_recv.at[j], sibling)
                      for j in range(N_CHIPS)]
        load_own = [pltpu.make_async_copy(g_ref.at[j, c], own.at[j], local_sems.at[j]) for j in range(N_CHIPS)]
        load_ex = [pltpu.make_async_copy(ex[k], ex_buf[k], local_sems.at[N_CHIPS + n + 1 + k]) for k in range(n)]
        for cp in to_sibling + load_own + load_ex:
            cp.start()

        small_buf[me] = small_ref[...]
        small_copies = []
        for mask in range(1, 8):
            peer = (x ^ (mask >> 2), y ^ ((mask >> 1) & 1), c ^ (mask & 1))
            small_copies.append(_remote(small_ref, small_buf.at[me], small_send.at[mask - 1], small_recv.at[mask - 1], peer))
        for cp in small_copies:
            cp.start()

        def share(k, half_ref, out_ref):
            keep = pltpu.make_async_copy(half_ref, out_ref.at[c], local_sems.at[N_CHIPS + k])
            give = _remote(half_ref, out_ref.at[c], share_send.at[k], share_recv.at[k], sibling)
            take = _remote(half_ref, out_ref.at[1 - c], share_send.at[k], share_recv.at[k], sibling)
            keep.start()
            give.start()
            return keep, give, take

        def pair_sum(block):
            _remote(g_ref.at[block, 1 - c], got.at[block], pair_send.at[block], pair_recv.at[block], sibling).wait_recv()
            pltpu.make_async_copy(g_ref.at[block, c], own.at[block], local_sems.at[block]).wait()
            part[block] = (own[block] + got[block]).astype(BF16)

        to_chips = []
        for j, chip in enumerate(chips):
            block = 2 * chip[0] + chip[1]
            pair_sum(block)
            to_chips.append(_remote(part.at[block], exch.at[my_chip], chip_send.at[j], chip_recv.at[j], (*chip, c)))
            to_chips[-1].start()
        pair_sum(my_chip)
        exch[my_chip] = part[my_chip]
        from_chips = [_remote(part.at[my_chip], exch.at[2 * chip[0] + chip[1]], chip_send.at[j], chip_recv.at[j], (*chip, c))
                      for j, chip in enumerate(chips)]

        shares = []
        for k in range(n):
            load_ex[k].wait()
            halves[k][...] = _sum_blocks(ex_buf[k])
            shares.append(share(k, halves[k], out[k]))

        for cp in small_copies:
            cp.wait_recv()
        total = small_buf[0]
        for d in range(1, 8):
            total = total + small_buf[d]
        small_out[...] = total

        for cp in from_chips:
            cp.wait_recv()
        half_last[...] = _sum_blocks(exch)
        shares.append(share(n, half_last, out_last))

        for keep, give, take in shares:
            take.wait_recv()
            give.wait_send()
            keep.wait()
        for cp in to_sibling + to_chips + small_copies:
            cp.wait_send()

    blocks = (N_CHIPS, h, D_MODEL)
    return pl.pallas_call(
        body, name="tail_reduce",
        in_specs=[ANY] * (n + 1) + [VMEM_WHOLE], out_specs=[ANY] * (n + 1) + [VMEM_WHOLE],
        out_shape=[SDS((2,) + e.shape[1:], F32) for e in exchanged] + [SDS((2, h, D_MODEL), F32), SDS(small.shape, F32)],
        scratch_shapes=[pltpu.VMEM(e.shape[1:], F32) for e in exchanged] + [pltpu.VMEM((h, D_MODEL), F32)]
                       + [pltpu.VMEM(blocks, F32), pltpu.VMEM(blocks, F32), pltpu.VMEM(blocks, BF16), pltpu.VMEM(blocks, BF16),
                          pltpu.VMEM((8,) + small.shape, F32)]
                       + [pltpu.VMEM(e.shape, BF16) for e in exchanged]
                       + [pltpu.SemaphoreType.DMA((N_CHIPS,)), pltpu.SemaphoreType.DMA((N_CHIPS,)),
                          pltpu.SemaphoreType.DMA((3,)), pltpu.SemaphoreType.DMA((3,)),
                          pltpu.SemaphoreType.DMA((n + 1,)), pltpu.SemaphoreType.DMA((n + 1,)),
                          pltpu.SemaphoreType.DMA((7,)), pltpu.SemaphoreType.DMA((7,)),
                          pltpu.SemaphoreType.DMA((N_CHIPS + 2 * n + 1,))],
        compiler_params=pltpu.CompilerParams(vmem_limit_bytes=VMEM_LIMIT_V7X),
    )(last_grads, *exchanged, small)


def _rope_expansion():
    half = ROT_DIM // 2
    expand = np.zeros((2 * half, 3 * 128), np.float32)
    const = np.zeros((1, 3 * 128), np.float32)
    for lane in range(128):
        d = lane % HEAD_DIM
        if d < ROT_DIM:
            expand[d % half, lane] = 1.0
        else:
            const[0, lane] = 1.0
        if d < half:
            expand[half + d, 128 + lane] = -1.0
        elif d < ROT_DIM:
            expand[half + d - half, 256 + lane] = 1.0
    return expand, const


ROPE_PIECES = 3 * ROT_DIM


def _rope_inputs(seq):
    pos = jnp.arange(seq, dtype=F32)
    inv_freq = ROPE_THETA ** (-jnp.arange(0, ROT_DIM, 2, dtype=F32) / ROT_DIM)
    ang = pos[:, None] * inv_freq[None, :]
    cs = jnp.concatenate([jnp.cos(ang), jnp.sin(ang)], axis=1)
    hi = lax.reduce_precision(cs, 8, 7)
    mid = lax.reduce_precision(cs - hi, 8, 7)
    low = cs - hi - mid
    expand, const = _rope_expansion()
    pieces = jnp.concatenate([hi, mid, low], axis=1).astype(BF16)
    return pieces, jnp.asarray(np.concatenate([expand] * 3, axis=0), BF16), jnp.asarray(const)


def _rope_specs(tb):
    return [pl.BlockSpec((tb, ROPE_PIECES), lambda i: (i, 0)), _resident((ROPE_PIECES, 3 * 128)), _resident((1, 3 * 128))]


def _rope_tile(pieces_ref, expand_ref, const_ref):
    tables = _dot(pieces_ref[...], expand_ref[...]) + const_ref[...]
    return tables[:, 0:128], tables[:, 128:256], tables[:, 256:384]


def _rope(t, c, sa, sb):
    half = ROT_DIM // 2
    return t * c + pltpu.roll(t, 128 - half, 1) * sa + pltpu.roll(t, half, 1) * sb


def _rope_transposed(dt, c, sa, sb):
    half = ROT_DIM // 2
    return dt * c + pltpu.roll(dt * sa, half, 1) + pltpu.roll(dt * sb, 128 - half, 1)


def _in_proj(x, g_pre, w_in_t, rope, comm=None):
    seq = x.shape[0]
    tb = min(seq, WIDE_TOKEN_TILE)

    def body(x_ref, g_ref, w_ref, c_ref, sa_ref, sb_ref,
             q_ref, kd0_ref, kd1_ref, vd0_ref, vd1_ref, gb_ref, gc_ref, xin_ref, hn_ref):
        xv = x_ref[...]
        hn = (xv * _rms(xv) * g_ref[...]).astype(BF16)
        hn_ref[...] = hn
        proj = _dot_nt(hn, w_ref[...].reshape(IN_COLS, D_MODEL))
        c, sa, sb = _rope_tile(c_ref, sa_ref, sb_ref)
        scale = 1.0 / math.sqrt(HEAD_DIM)
        for p in range(Q_WIDTH // 128):
            q_ref[:, 128 * p:128 * (p + 1)] = (_rope(proj[:, 128 * p:128 * (p + 1)], c, sa, sb) * scale).astype(BF16)
        k = _rope(proj[:, Q_WIDTH:Q_WIDTH + KV_WIDTH], c, sa, sb)
        v = proj[:, Q_WIDTH + KV_WIDTH:Q_WIDTH + 2 * KV_WIDTH]
        low = _lane_lt64(k.shape)
        k_sw, v_sw = pltpu.roll(k, HEAD_DIM, 1), pltpu.roll(v, HEAD_DIM, 1)
        kd0_ref[...] = jnp.where(low, k, k_sw).astype(BF16)
        kd1_ref[...] = jnp.where(low, k_sw, k).astype(BF16)
        vd0_ref[...] = jnp.where(low, v, v_sw).astype(BF16)
        vd1_ref[...] = jnp.where(low, v_sw, v).astype(BF16)
        base = Q_WIDTH + 2 * KV_WIDTH
        gb_ref[...] = proj[:, base:base + CONV_WIDTH].astype(BF16)
        gc_ref[...] = proj[:, base + CONV_WIDTH:base + 2 * CONV_WIDTH].astype(BF16)
        xin_ref[...] = proj[:, base + 2 * CONV_WIDTH:base + 3 * CONV_WIDTH].astype(BF16)

    tile = lambda w: pl.BlockSpec((tb, w), lambda i: (i, 0))
    return _pallas(
        body, name="in_proj", grid=(seq // tb,),
        in_specs=[tile(D_MODEL), _resident((1, D_MODEL)), _resident(w_in_t.shape), *_rope_specs(tb)],
        out_specs=[tile(Q_WIDTH), tile(128), tile(128), tile(128), tile(128),
                   tile(CONV_WIDTH), tile(CONV_WIDTH), tile(CONV_WIDTH), tile(D_MODEL)],
        out_shape=[SDS((seq, Q_WIDTH), BF16)] + [SDS((seq, 128), BF16)] * 4
                  + [SDS((seq, CONV_WIDTH), BF16)] * 3 + [SDS((seq, D_MODEL), BF16)],
        operands=(x, g_pre, w_in_t, *rope), comm=comm)


def _attn_valid(i):
    shape = (4 * QBLOCK, 2 * QBLOCK)
    row = lax.broadcasted_iota(jnp.int32, shape, 0)
    col = lax.broadcasted_iota(jnp.int32, shape, 1)
    qi = row & (QBLOCK - 1)
    return (col > qi) & (col <= qi + QBLOCK) & ((col >= QBLOCK) | (i > 0))


def _stack_heads(pair0, pair1):
    low = _lane_lt64(pair0.shape)
    zero = jnp.zeros_like(pair0)
    return jnp.concatenate([jnp.where(low, pair0, zero), jnp.where(low, zero, pair0),
                            jnp.where(low, pair1, zero), jnp.where(low, zero, pair1)], axis=0)


def _unstack_heads(stacked):
    low = _lane_lt64((QBLOCK, 128))
    pair0 = jnp.where(low, stacked[0:QBLOCK], stacked[QBLOCK:2 * QBLOCK])
    pair1 = jnp.where(low, stacked[2 * QBLOCK:3 * QBLOCK], stacked[3 * QBLOCK:4 * QBLOCK])
    return pair0, pair1


def _sink_column(sink_ref, kv_head):
    row = lax.broadcasted_iota(jnp.int32, (4 * QBLOCK, 1), 0)
    s = [sink_ref[0, 4 * kv_head + j] for j in range(4)]
    return jnp.where(row < QBLOCK, s[0], jnp.where(row < 2 * QBLOCK, s[1], jnp.where(row < 3 * QBLOCK, s[2], s[3])))


def _band(ref, i):
    prev = pl.multiple_of(jnp.maximum(i - 1, 0) * QBLOCK, QBLOCK)
    own = pl.multiple_of(i * QBLOCK, QBLOCK)
    return jnp.concatenate([ref[pl.ds(prev, QBLOCK), :], ref[pl.ds(own, QBLOCK), :]], axis=0), prev, own


def _softmax_with_sink(s, sink_col):
    m = jnp.maximum(jnp.max(s, axis=-1, keepdims=True), sink_col)
    p = jnp.exp(s - m)
    e_sink = jnp.exp(sink_col - m)
    inv_l = 1.0 / (jnp.sum(p, axis=-1, keepdims=True) + e_sink)
    return p, e_sink, inv_l


def _attention_fwd(q, kd0, kd1, vd0, vd1, sinks, comm=None):
    seq = q.shape[0]

    nb = ATTN_FWD_BLOCKS

    def body(sink_ref, q_ref, kd0_ref, kd1_ref, vd0_ref, vd1_ref, o_ref):
        for b in range(nb):
            i = pl.program_id(0) * nb + b
            rows = slice(QBLOCK * b, QBLOCK * (b + 1))
            valid = _attn_valid(i)
            for kv_head, (k_ref, v_ref) in enumerate(((kd0_ref, vd0_ref), (kd1_ref, vd1_ref))):
                kband, _, _ = _band(k_ref, i)
                vband, _, _ = _band(v_ref, i)
                base = 256 * kv_head
                qm = _stack_heads(q_ref[rows, base:base + 128], q_ref[rows, base + 128:base + 256])
                s = jnp.where(valid, _dot_nt(qm, kband), NEG_INF)
                p, _, inv_l = _softmax_with_sink(s, _sink_column(sink_ref, kv_head))
                o = _dot(p.astype(BF16), vband) * inv_l
                pair0, pair1 = _unstack_heads(o)
                o_ref[rows, base:base + 128] = pair0.astype(BF16)
                o_ref[rows, base + 128:base + 256] = pair1.astype(BF16)

    blk = pl.BlockSpec((nb * QBLOCK, Q_WIDTH), lambda i: (i, 0))
    full = _resident((seq, 128))
    return _pallas(
        body, name="attention_fwd", grid=(seq // (nb * QBLOCK),),
        in_specs=[pl.BlockSpec(memory_space=pltpu.SMEM), blk, full, full, full, full],
        out_specs=[blk], out_shape=[SDS((seq, Q_WIDTH), BF16)],
        operands=(sinks, q, kd0, kd1, vd0, vd1), comm=comm)


HALO = 16


def _conv_parts(gc, xin, gc_halo, xin_halo, conv_w, first):
    tb = gc.shape[0]
    u = gc.astype(F32) * xin.astype(F32)
    u_halo = jnp.where(first, 0.0, gc_halo.astype(F32) * xin_halo.astype(F32))
    ext = jnp.concatenate([u_halo, u], axis=0)
    u1 = pltpu.roll(ext, 1, 0)[HALO:HALO + tb]
    u2 = pltpu.roll(ext, 2, 0)[HALO:HALO + tb]
    y = conv_w[0:1, :] * u2 + conv_w[1:2, :] * u1 + conv_w[2:3, :] * u
    return u, u1, u2, y


def _halo_prev(tb, w):
    return pl.BlockSpec((HALO, w), lambda i: (jnp.maximum(i * (tb // HALO) - 1, 0), 0))


def _residual_mid(x, mix, g_post_mix):
    mix_f = mix.astype(F32)
    return x + mix_f * _rms(mix_f) * g_post_mix


def _mix_out(attn, gb, gc, xin, conv_w, g_attn, g_conv, w_out, comm=None):
    seq = attn.shape[0]
    tb = min(seq, WIDE_TOKEN_TILE)

    def body(a_ref, gb_ref, gc_ref, xin_ref, gch_ref, xinh_ref, cw_ref, ga_ref, gcn_ref, w_ref, mix_ref, mixed_ref):
        first = pl.program_id(0) == 0
        _, _, _, y = _conv_parts(gc_ref[...], xin_ref[...], gch_ref[...], xinh_ref[...], cw_ref[...], first)
        conv = gb_ref[...].astype(F32) * y
        a = a_ref[...].astype(F32)
        mixed_ref[:, 0:Q_WIDTH] = (a * _rms(a) * ga_ref[...]).astype(BF16)
        mixed_ref[:, Q_WIDTH:] = (conv * _rms(conv) * gcn_ref[...]).astype(BF16)
        mix_ref[...] = _dot(mixed_ref[...], w_ref[...].reshape(D_MODEL, D_MODEL)).astype(BF16)

    tile = lambda w: pl.BlockSpec((tb, w), lambda i: (i, 0))
    return _pallas(
        body, name="mix_out", grid=(seq // tb,),
        in_specs=[tile(Q_WIDTH), tile(CONV_WIDTH), tile(CONV_WIDTH), tile(CONV_WIDTH),
                  _halo_prev(tb, CONV_WIDTH), _halo_prev(tb, CONV_WIDTH),
                  _resident((CONV_K, CONV_WIDTH)), _resident((1, Q_WIDTH)), _resident((1, CONV_WIDTH)),
                  _resident(w_out.shape)],
        out_specs=[tile(D_MODEL), tile(D_MODEL)],
        out_shape=[SDS((seq, D_MODEL), BF16), SDS((seq, D_MODEL), BF16)],
        operands=(attn, gb, gc, xin, gc, xin, conv_w, g_attn, g_conv, w_out), comm=comm)


def _mlp_fwd_bwd(x, mix, target, g_post_mix, g_pre_mlp, g_post_mlp, w_up, w_down):
    seq = x.shape[0]
    tb = TOKEN_TILE

    def body(x_ref, mix_ref, t_ref, gpm_ref, g2_ref, g4_ref, wup_ref, wdown_ref,
             up_ref, hn2_ref, dmlp_ref, dup_ref, dh_ref, dmix_ref, loss_ref, dg4_ref, dg2_ref, dgpm_ref):
        @pl.when(pl.program_id(0) == 0)
        def _():
            for ref in (loss_ref, dg4_ref, dg2_ref, dgpm_ref):
                ref[...] = jnp.zeros_like(ref)

        halves = [slice(0, tb // 2), slice(tb // 2, tb)]
        chunks = [slice(1024 * j, 1024 * (j + 1)) for j in range(N_CHIPS)]
        hv, hn2, mlp, dout, dmlp, dhn2 = [], [], [], [], [], []
        for rows in halves:
            hv.append(_residual_mid(x_ref[rows, :], mix_ref[rows, :], gpm_ref[...]))
            hn2.append((hv[-1] * _rms(hv[-1]) * g2_ref[...]).astype(BF16))
            hn2_ref[rows, :] = hn2[-1]
        for k, rows in enumerate(halves):
            acc = None
            for j, cols in enumerate(chunks):
                up = jnp.maximum(_dot(hn2[k], _chip_block(wup_ref, j)), 0.0)
                up_ref[rows, cols] = up.astype(BF16)
                part = _dot((up * up).astype(BF16), _chip_block(wdown_ref, j))
                acc = part if acc is None else acc + part
            mlp.append(acc)
        loss = jnp.zeros((1, 1), F32)
        dg4 = jnp.zeros((1, D_MODEL), F32)
        for k, rows in enumerate(halves):
            rstd = _rms(mlp[k])
            zhat = mlp[k] * rstd
            diff = hv[k] + zhat * g4_ref[...] - t_ref[rows, :]
            loss = loss + jnp.sum(jnp.sum(diff * diff, axis=1, keepdims=True), axis=0, keepdims=True)
            dout.append(diff * (1.0 / D_MODEL))
            dg4 = dg4 + _colsum(dout[k] * zhat)
            dmlp.append(_norm_bwd(dout[k], g4_ref[...], zhat, rstd).astype(BF16))
            dmlp_ref[rows, :] = dmlp[k]
        for k, rows in enumerate(halves):
            acc = None
            for j, cols in enumerate(chunks):
                dact = _dot_nt(dmlp[k], _chip_block(wdown_ref, j))
                dup = (dact * (2.0 * up_ref[rows, cols].astype(F32))).astype(BF16)
                dup_ref[rows, cols] = dup
                part = _dot_nt(dup, _chip_block(wup_ref, j))
                acc = part if acc is None else acc + part
            dhn2.append(acc)
        dg2 = jnp.zeros((1, D_MODEL), F32)
        dgpm = jnp.zeros((1, D_MODEL), F32)
        for k, rows in enumerate(halves):
            r2 = _rms(hv[k])
            hhat = hv[k] * r2
            dg2 = dg2 + _colsum(dhn2[k] * hhat)
            dh = dout[k] + _norm_bwd(dhn2[k], g2_ref[...], hhat, r2)
            dh_ref[rows, :] = dh.astype(BF16)
            mix_v = mix_ref[rows, :].astype(F32)
            rz = _rms(mix_v)
            zhat = mix_v * rz
            dgpm = dgpm + _colsum(dh * zhat)
            dmix_ref[rows, :] = _norm_bwd(dh, gpm_ref[...], zhat, rz).astype(BF16)
        loss_ref[...] += loss
        dg4_ref[...] += dg4
        dg2_ref[...] += dg2
        dgpm_ref[...] += dgpm

    tile = lambda w: pl.BlockSpec((tb, w), lambda i: (i, 0))
    vec = pl.BlockSpec((1, D_MODEL), lambda i: (0, 0))
    return _pallas(
        body, name="mlp_fwd_bwd", grid=(seq // tb,),
        in_specs=[tile(D_MODEL), tile(D_MODEL), tile(D_MODEL), _resident((1, D_MODEL)), _resident((1, D_MODEL)),
                  _resident((1, D_MODEL)), _resident(w_up.shape), _resident(w_down.shape)],
        out_specs=[tile(D_FF), tile(D_MODEL), tile(D_MODEL), tile(D_FF), tile(D_MODEL), tile(D_MODEL),
                   pl.BlockSpec((1, 1), lambda i: (0, 0)), vec, vec, vec],
        out_shape=[SDS((seq, D_FF), BF16), SDS((seq, D_MODEL), BF16), SDS((seq, D_MODEL), BF16), SDS((seq, D_FF), BF16),
                   SDS((seq, D_MODEL), BF16), SDS((seq, D_MODEL), BF16),
                   SDS((1, 1), F32), SDS((1, D_MODEL), F32), SDS((1, D_MODEL), F32), SDS((1, D_MODEL), F32)],
        operands=(x, mix, target, g_post_mix, g_pre_mlp, g_post_mlp, w_up, w_down))


def _mix_bwd(dmix, attn, gb, gc, xin, conv_w, g_attn, g_conv, w_out, n_k):
    seq = attn.shape[0]
    tb = seq // (N_CHIPS * n_k)

    def body(first, dmix_ref, a_ref, gb_ref, gc_ref, xin_ref, gch_ref, xinh_ref, cw_ref, ga_ref, gcn_ref, w_ref,
             dattn_ref, dgb_ref, dy_ref, dga_ref, dgcn_ref, dcw_ref):
        @pl.when(first)
        def _():
            dga_ref[...] = jnp.zeros_like(dga_ref)
            dgcn_ref[...] = jnp.zeros_like(dgcn_ref)
            dcw_ref[...] = jnp.zeros_like(dcw_ref)

        dmixed = _dot_nt(dmix_ref[...], w_ref[...].reshape(D_MODEL, D_MODEL))
        a = a_ref[...].astype(F32)
        ra = _rms(a)
        ahat = a * ra
        dan = dmixed[:, 0:Q_WIDTH]
        dga_ref[...] += _colsum(dan * ahat)
        dattn_ref[...] = _norm_bwd(dan, ga_ref[...], ahat, ra).astype(BF16)
        gbv = gb_ref[...].astype(F32)
        u, u1, u2, y = _conv_parts(gc_ref[...], xin_ref[...], gch_ref[...], xinh_ref[...], cw_ref[...], first)
        conv = gbv * y
        rc = _rms(conv)
        chat = conv * rc
        dcn = dmixed[:, Q_WIDTH:]
        dgcn_ref[...] += _colsum(dcn * chat)
        dconv = _norm_bwd(dcn, gcn_ref[...], chat, rc)
        dgb_ref[...] = (dconv * y).astype(BF16)
        dy = dconv * gbv
        dy_ref[...] = dy.astype(BF16)
        dcw_ref[0:1, :] += _colsum(dy * u2)
        dcw_ref[1:2, :] += _colsum(dy * u1)
        dcw_ref[2:3, :] += _colsum(dy * u)

    tile = lambda w: pl.BlockSpec((tb, w), lambda j, k: (j * n_k + k, 0))
    halo = lambda w: pl.BlockSpec((HALO, w), lambda j, k: (jnp.maximum((j * n_k + k) * (tb // HALO) - 1, 0), 0))
    whole = lambda shape: pl.BlockSpec(shape, lambda j, k: (0,) * len(shape))
    return _Rider(
        body,
        in_specs=[tile(D_MODEL), tile(Q_WIDTH), tile(CONV_WIDTH), tile(CONV_WIDTH), tile(CONV_WIDTH),
                  halo(CONV_WIDTH), halo(CONV_WIDTH),
                  _resident((CONV_K, CONV_WIDTH)), _resident((1, Q_WIDTH)), _resident((1, CONV_WIDTH)),
                  _resident(w_out.shape)],
        out_specs=[tile(Q_WIDTH), tile(CONV_WIDTH), tile(CONV_WIDTH),
                   whole((1, Q_WIDTH)), whole((1, CONV_WIDTH)), whole((CONV_K, CONV_WIDTH))],
        out_shape=[SDS((seq, Q_WIDTH), BF16), SDS((seq, CONV_WIDTH), BF16), SDS((seq, CONV_WIDTH), BF16),
                   SDS((1, Q_WIDTH), F32), SDS((1, CONV_WIDTH), F32), SDS((CONV_K, CONV_WIDTH), F32)],
        operands=(dmix, attn, gb, gc, xin, gc, xin, conv_w, g_attn, g_conv, w_out))


def _attention_bwd(q, dattn, attn, kd0, kd1, vd0, vd1, sinks, comm=None):
    seq = q.shape[0]
    nb = ATTN_BWD_BLOCKS

    def body(sink_ref, q_ref, do_ref, o_ref, kd0_ref, kd1_ref, vd0_ref, vd1_ref,
             dq_ref, dk0_ref, dk1_ref, dv0_ref, dv1_ref, dsink_ref):
        @pl.when(pl.program_id(0) == 0)
        def _():
            for r in (dk0_ref, dk1_ref, dv0_ref, dv1_ref, dsink_ref):
                r[...] = jnp.zeros_like(r)

        lane = lax.broadcasted_iota(jnp.int32, (1, 128), 1)
        dsink = jnp.zeros((1, 128), F32)
        for b in range(nb):
            i = pl.program_id(0) * nb + b
            rows = slice(QBLOCK * b, QBLOCK * (b + 1))
            valid = _attn_valid(i)
            for kv_head, (k_ref, v_ref, dk_ref, dv_ref) in enumerate(
                    ((kd0_ref, vd0_ref, dk0_ref, dv0_ref), (kd1_ref, vd1_ref, dk1_ref, dv1_ref))):
                kband, prev, own = _band(k_ref, i)
                vband, _, _ = _band(v_ref, i)
                base = 256 * kv_head
                qm = _stack_heads(q_ref[rows, base:base + 128], q_ref[rows, base + 128:base + 256])
                dom = _stack_heads(do_ref[rows, base:base + 128], do_ref[rows, base + 128:base + 256])
                om = _stack_heads(o_ref[rows, base:base + 128], o_ref[rows, base + 128:base + 256])
                s = jnp.where(valid, _dot_nt(qm, kband), NEG_INF)
                p, e_sink, inv_l = _softmax_with_sink(s, _sink_column(sink_ref, kv_head))
                p = p * inv_l
                delta = jnp.sum(dom.astype(F32) * om.astype(F32), axis=-1, keepdims=True)
                ds = (p * (_dot_nt(dom, vband) - delta)).astype(BF16)
                sink_term = -(e_sink * inv_l) * delta
                for j in range(4):
                    part = jnp.sum(sink_term[QBLOCK * j:QBLOCK * (j + 1)], axis=0, keepdims=True)
                    dsink = dsink + jnp.where(lane == 4 * kv_head + j, part, 0.0)
                pair0, pair1 = _unstack_heads(_dot(ds, kband))
                dq_ref[rows, base:base + 128] = pair0.astype(BF16)
                dq_ref[rows, base + 128:base + 256] = pair1.astype(BF16)
                dkd = _dot_tn(ds, qm)
                dkd = dkd + pltpu.roll(dkd, HEAD_DIM, 1)
                dvd = _dot_tn(p.astype(BF16), dom)
                dvd = dvd + pltpu.roll(dvd, HEAD_DIM, 1)
                dk_ref[pl.ds(prev, QBLOCK), :] += dkd[0:QBLOCK]
                dk_ref[pl.ds(own, QBLOCK), :] += dkd[QBLOCK:]
                dv_ref[pl.ds(prev, QBLOCK), :] += dvd[0:QBLOCK]
                dv_ref[pl.ds(own, QBLOCK), :] += dvd[QBLOCK:]
        dsink_ref[...] += dsink

    blk = pl.BlockSpec((nb * QBLOCK, Q_WIDTH), lambda i: (i, 0))
    full = _resident((seq, 128))
    acc = pl.BlockSpec((seq, 128), lambda i: (0, 0))
    return _pallas(
        body, name="attention_bwd", grid=(seq // (nb * QBLOCK),),
        in_specs=[pl.BlockSpec(memory_space=pltpu.SMEM), blk, blk, blk, full, full, full, full],
        out_specs=[blk, acc, acc, acc, acc, pl.BlockSpec((1, 128), lambda i: (0, 0))],
        out_shape=[SDS((seq, Q_WIDTH), BF16)] + [SDS((seq, 128), F32)] * 4 + [SDS((1, 128), F32)],
        operands=(sinks, q, dattn, attn, kd0, kd1, vd0, vd1), comm=comm)


def _in_proj_bwd(dq, dk0, dk1, dv0, dv1, dgb, dy, gc, xin, conv_w, x, dh, g_pre, w_in_t, rope):
    seq = x.shape[0]
    tb = min(seq, WIDE_TOKEN_TILE)
    n_tiles = seq // tb

    def body(dq_ref, dk0_ref, dk1_ref, dv0_ref, dv1_ref, dgb_ref, dy_ref, dyh_ref, gc_ref, xin_ref, cw_ref,
             x_ref, dh_ref, g_ref, w_ref, c_ref, sa_ref, sb_ref,
             dproj_ref, gx_ref, dg_ref):
        i = pl.program_id(0)

        @pl.when(i == 0)
        def _():
            dg_ref[...] = jnp.zeros_like(dg_ref)

        dy = dy_ref[...].astype(F32)
        ext = jnp.concatenate([dy, jnp.where(i == n_tiles - 1, 0.0, dyh_ref[...].astype(F32))], axis=0)
        dy1 = pltpu.roll(ext, tb + HALO - 1, 0)[0:tb]
        dy2 = pltpu.roll(ext, tb + HALO - 2, 0)[0:tb]
        cw = cw_ref[...]
        du = cw[2:3, :] * dy + cw[1:2, :] * dy1 + cw[0:1, :] * dy2
        scale = 1.0 / math.sqrt(HEAD_DIM)
        base = Q_WIDTH + 2 * KV_WIDTH
        halves = [slice(0, tb // 2), slice(tb // 2, tb)]
        low = _lane_lt64((tb // 2, 128))
        for rows in halves:
            c, sa, sb = _rope_tile(c_ref.at[rows, :], sa_ref, sb_ref)
            for p in range(Q_WIDTH // 128):
                dproj_ref[rows, 128 * p:128 * (p + 1)] = _rope_transposed(
                    dq_ref[rows, 128 * p:128 * (p + 1)].astype(F32) * scale, c, sa, sb).astype(BF16)
            dk = jnp.where(low, dk0_ref[rows, :], dk1_ref[rows, :])
            dproj_ref[rows, Q_WIDTH:Q_WIDTH + KV_WIDTH] = _rope_transposed(dk, c, sa, sb).astype(BF16)
            dproj_ref[rows, Q_WIDTH + KV_WIDTH:base] = jnp.where(low, dv0_ref[rows, :], dv1_ref[rows, :]).astype(BF16)
            dproj_ref[rows, base:base + CONV_WIDTH] = dgb_ref[rows, :]
            dproj_ref[rows, base + CONV_WIDTH:base + 2 * CONV_WIDTH] = (du[rows] * xin_ref[rows, :].astype(F32)).astype(BF16)
            dproj_ref[rows, base + 2 * CONV_WIDTH:] = (du[rows] * gc_ref[rows, :].astype(F32)).astype(BF16)
        w_all = w_ref[...].reshape(IN_COLS, D_MODEL)
        dhn = [_dot(dproj_ref[rows, :], w_all) for rows in halves]
        dg = jnp.zeros((1, D_MODEL), F32)
        for k, rows in enumerate(halves):
            xv = x_ref[rows, :]
            r = _rms(xv)
            xhat = xv * r
            dg = dg + _colsum(dhn[k] * xhat)
            gx_ref[rows, :] = dh_ref[rows, :].astype(F32) + _norm_bwd(dhn[k], g_ref[...], xhat, r)
        dg_ref[...] += dg

    tile = lambda w: pl.BlockSpec((tb, w), lambda i: (i, 0))
    halo_next = pl.BlockSpec((HALO, CONV_WIDTH), lambda i: (jnp.minimum((i + 1) * (tb // HALO), seq // HALO - 1), 0))
    return _pallas(
        body, name="in_proj_bwd", grid=(n_tiles,),
        in_specs=[tile(Q_WIDTH), tile(128), tile(128), tile(128), tile(128), tile(CONV_WIDTH), tile(CONV_WIDTH), halo_next,
                  tile(CONV_WIDTH), tile(CONV_WIDTH), _resident((CONV_K, CONV_WIDTH)),
                  tile(D_MODEL), tile(D_MODEL), _resident((1, D_MODEL)), _resident(w_in_t.shape), *_rope_specs(tb)],
        out_specs=[tile(IN_COLS), tile(D_MODEL), pl.BlockSpec((1, D_MODEL), lambda i: (0, 0))],
        out_shape=[SDS((seq, IN_COLS), BF16), SDS((seq, D_MODEL), F32), SDS((1, D_MODEL), F32)],
        operands=(dq, dk0, dk1, dv0, dv1, dgb, dy, dy, gc, xin, conv_w, x, dh, g_pre, w_in_t, *rope))


def _wgrad_grid(seq, per_chip, h_rows, with_rider=False):
    chips_per_step = 1 if per_chip else N_CHIPS
    m = chips_per_step * 2 * h_rows
    bt = min(seq, WGRAD_TOKEN_TILE if per_chip and not with_rider else WGRAD_TOKEN_TILE // 2)
    return chips_per_step, m, bt, seq // bt


def _wgrad(name, a, b, *, per_chip, h_rows, square_a=False, comm=None, rider=None):
    seq = a.shape[0]
    chips_per_step, m, bt, n_k = _wgrad_grid(seq, per_chip, h_rows, rider is not None)
    a_cols = m if per_chip else a.shape[1]
    a_wide = a.shape[1] > a_cols
    b_wide = b.shape[1] > D_MODEL

    def body(a_ref, b_ref, g_ref):
        @pl.when(pl.program_id(1) == 0)
        def _():
            g_ref[...] = jnp.zeros_like(g_ref)

        av = a_ref[...]
        if square_a:
            av = (av.astype(F32) * av.astype(F32)).astype(BF16)
        g_ref[...] += _dot_tn(av, b_ref[...]).reshape(g_ref.shape)

    a_spec = pl.BlockSpec((bt, a_cols), (lambda j, k: (k, j)) if a_wide else (lambda j, k: (k, 0)))
    b_spec = pl.BlockSpec((bt, D_MODEL), (lambda j, k: (k, j)) if b_wide else (lambda j, k: (k, 0)))
    g_spec = pl.BlockSpec((chips_per_step, 2, h_rows, D_MODEL), lambda j, k: (j, 0, 0, 0),
                          pipeline_mode=None if per_chip else pl.Buffered(1))
    return _pallas(
        body, name=name, grid=(N_CHIPS if per_chip else 1, n_k),
        in_specs=[a_spec, b_spec], out_specs=[g_spec], out_shape=[SDS((N_CHIPS, 2, h_rows, D_MODEL), F32)],
        operands=(a, b), comm=comm, rider=rider)


def _adamw_math(w, g, m, v):
    m = ADAM_B1 * m + (1.0 - ADAM_B1) * g
    v = ADAM_B2 * v + (1.0 - ADAM_B2) * (g * g)
    m_hat = m / (1.0 - ADAM_B1 ** ADAM_STEP)
    v_hat = v / (1.0 - ADAM_B2 ** ADAM_STEP)
    delta = -ADAM_LR * (m_hat / (jnp.sqrt(v_hat) + ADAM_EPS) + ADAM_WD * w)
    return delta, m, v


def _adamw_rows(name, reduced, w, m, v, rt):
    per_half = reduced.shape[1] // rt

    def body(r_ref, w_ref, m_ref, v_ref, g_out, d_out, m_out, v_out):
        g = r_ref[0]
        g_out[...] = g
        d_out[...], m_out[...], v_out[...] = _adamw_math(w_ref[...], g, m_ref[...], v_ref[...])

    blk = pl.BlockSpec((rt, D_MODEL), lambda h, r: (h * per_half + r, 0))
    return _pallas(
        body, name=name, grid=(2, per_half),
        in_specs=[pl.BlockSpec((1, rt, D_MODEL), lambda h, r: (h, r, 0)), blk, blk, blk],
        out_specs=[blk, blk, blk, blk], out_shape=[SDS(w.shape, F32)] * 4, operands=(reduced, w, m, v))


def _adamw_small(packed_grads, w, m, v):
    names = SMALL_NAMES
    n = len(names)
    conv_local = w["conv_w"].shape[-1]

    def body(*refs):
        gp = refs[0]
        w_refs, m_refs, v_refs = refs[1:1 + n], refs[1 + n:1 + 2 * n], refs[1 + 2 * n:1 + 3 * n]
        outs = refs[1 + 3 * n:]
        g_out, d_out, m_out, v_out = outs[0:n], outs[n:2 * n], outs[2 * n:3 * n], outs[3 * n:4 * n]
        chip = 2 * lax.axis_index("x") + lax.axis_index("y")

        def step(k, g, index=None):
            pick = (lambda r: r[...]) if index is None else (lambda r: r[index])
            d, new_m, new_v = _adamw_math(pick(w_refs[k]), g, pick(m_refs[k]), pick(v_refs[k]))
            for ref, val in ((g_out[k], g), (d_out[k], d), (m_out[k], new_m), (v_out[k], new_v)):
                if index is None:
                    ref[...] = val
                else:
                    ref[index] = val

        for k, name in enumerate(names):
            if name in SMALL_VECTORS:
                step(k, gp[SMALL_VECTORS.index(name):SMALL_VECTORS.index(name) + 1, :])
            elif name == "attn_group_norm":
                step(k, gp[4:5, 0:Q_WIDTH])
            elif name == "conv_group_norm":
                step(k, gp[4:5, Q_WIDTH:])
            elif name == "attn_sinks":
                step(k, gp[7:8, 0:8])
            else:
                for t in range(CONV_K):
                    row, base = 5 + t // 2, CONV_WIDTH * (t % 2)
                    g = gp[row:row + 1, base:base + conv_local]
                    for j in range(1, CONV_WIDTH // conv_local):
                        g = jnp.where(chip == j, gp[row:row + 1, base + conv_local * j:base + conv_local * (j + 1)], g)
                    step(k, g, index=(0, slice(t, t + 1), slice(None)))

    shapes = [SDS(w[name].shape, F32) for name in names]
    res = pl.pallas_call(
        body, name="adamw_small", in_specs=[VMEM_WHOLE] * (1 + 3 * n), out_specs=[VMEM_WHOLE] * (4 * n),
        out_shape=shapes * 4,
    )(packed_grads, *[w[k] for k in names], *[m[k] for k in names], *[v[k] for k in names])
    return [dict(zip(names, res[i * n:(i + 1) * n])) for i in range(4)]


SMALL_VECTORS = ("pre_mix_norm", "post_mix_norm", "pre_mlp_norm", "post_mlp_norm")
SMALL_NAMES = SMALL_VECTORS + ("attn_group_norm", "conv_group_norm", "conv_w", "attn_sinks")


def _pack_small(p):
    rows = [p[n].reshape(1, D_MODEL) for n in SMALL_VECTORS]
    rows.append(jnp.concatenate([p["attn_group_norm"].reshape(1, -1), p["conv_group_norm"].reshape(1, -1)], axis=1))
    cw = p["conv_w"].reshape(CONV_K, -1)
    rows.append(jnp.pad(cw, ((0, 1), (0, CONV_WIDTH - cw.shape[1]))).reshape(2, D_MODEL))
    last = jnp.concatenate([p["attn_sinks"].reshape(1, 8), p.get("loss_sum", jnp.zeros((1, 1), F32))], axis=1)
    rows.append(jnp.pad(last, ((0, 0), (0, D_MODEL - 9))))
    return jnp.concatenate(rows, axis=0)


WEIGHT_ORDER = ("pre_mix_norm", "w_in", "conv_w", "attn_sinks", "attn_group_norm", "conv_group_norm", "w_out",
                "post_mix_norm", "pre_mlp_norm", "w_up", "w_down", "post_mlp_norm")


def kernel(x, pre_mix_norm, w_in, conv_w, attn_sinks, attn_group_norm, conv_group_norm, w_out, post_mix_norm, pre_mlp_norm, w_up, w_down, post_mlp_norm, loss_target, m_pre_mix_norm, m_w_in, m_conv_w, m_attn_sinks, m_attn_group_norm, m_conv_group_norm, m_w_out, m_post_mix_norm, m_pre_mlp_norm, m_w_up, m_w_down, m_post_mlp_norm, v_pre_mix_norm, v_w_in, v_conv_w, v_attn_sinks, v_attn_group_norm, v_conv_group_norm, v_w_out, v_post_mix_norm, v_pre_mlp_norm, v_w_up, v_w_down, v_post_mlp_norm):
    w = dict(pre_mix_norm=pre_mix_norm, w_in=w_in, conv_w=conv_w, attn_sinks=attn_sinks, attn_group_norm=attn_group_norm,
             conv_group_norm=conv_group_norm, w_out=w_out, post_mix_norm=post_mix_norm, pre_mlp_norm=pre_mlp_norm,
             w_up=w_up, w_down=w_down, post_mlp_norm=post_mlp_norm)
    m = dict(pre_mix_norm=m_pre_mix_norm, w_in=m_w_in, conv_w=m_conv_w, attn_sinks=m_attn_sinks,
             attn_group_norm=m_attn_group_norm, conv_group_norm=m_conv_group_norm, w_out=m_w_out,
             post_mix_norm=m_post_mix_norm, pre_mlp_norm=m_pre_mlp_norm, w_up=m_w_up, w_down=m_w_down,
             post_mlp_norm=m_post_mlp_norm)
    v = dict(pre_mix_norm=v_pre_mix_norm, w_in=v_w_in, conv_w=v_conv_w, attn_sinks=v_attn_sinks,
             attn_group_norm=v_attn_group_norm, conv_group_norm=v_conv_group_norm, w_out=v_w_out,
             post_mix_norm=v_post_mix_norm, pre_mlp_norm=v_pre_mlp_norm, w_up=v_w_up, w_down=v_w_down,
             post_mlp_norm=v_post_mlp_norm)
    core = lax.axis_index("c").astype(jnp.int32).reshape(1)
    xs, target = x[0], loss_target[0]
    rope = _rope_inputs(xs.shape[0])

    conv_pad = jnp.pad(conv_w[0], ((0, 8 - CONV_K), (0, 0)))
    wf_in, conv_all, hb_up, hb_down, hb_out = _gather_whole(w_in[0].T, (w_up[0], w_down[0], w_out[0]), conv_pad)
    conv_full = conv_all[:, :CONV_K, :].transpose(1, 0, 2).reshape(CONV_K, CONV_WIDTH)

    whole_up, early, late = (0, H_UP), (0, DOWN_EARLY_ROWS), (DOWN_EARLY_ROWS, H_DOWN - DOWN_EARLY_ROWS)
    *proj, wf_up, wf_out, wf_down = _in_proj(
        xs, pre_mix_norm, wf_in, rope,
        comm=_merge(_relay(hb_up, None, first=whole_up), _gather_first(hb_out), _relay(hb_down, None, first=early)))
    q, kd0, kd1, vd0, vd1, gb, gc, xin, hn = proj
    attn, wf_up, wf_out, wf_down = _attention_fwd(
        q, kd0, kd1, vd0, vd1, attn_sinks,
        comm=_merge(_relay(None, wf_up, second=whole_up), _gather_second(wf_out),
                    _relay(hb_down, wf_down, first=late, second=early)))
    mix, mixed, wf_up, wf_down = _mix_out(
        attn, gb, gc, xin, conv_full, attn_group_norm, conv_group_norm, wf_out,
        comm=_merge(_relay(None, wf_up, third=whole_up), _relay(None, wf_down, second=late, third=early, third_after=late)))
    up, hn2, dmlp, dup, dh, dmix, loss_sum, dg_post_mlp, dg_pre_mlp, dg_post_mix = _mlp_fwd_bwd(
        xs, mix, target, post_mix_norm, pre_mlp_norm, post_mlp_norm, wf_up, wf_down)

    n_k = _wgrad_grid(xs.shape[0], True, H_DOWN, with_rider=True)[3]
    g_down, dattn, dgb, dy, dg_attn, dg_conv, dconv_w = _wgrad(
        "wgrad_down", up, dmlp, per_chip=True, h_rows=H_DOWN, square_a=True,
        rider=_mix_bwd(dmix, attn, gb, gc, xin, conv_full, attn_group_norm, conv_group_norm, wf_out, n_k))
    g_up, got_down = _wgrad("wgrad_up", hn2, dup, per_chip=True, h_rows=H_UP, comm=_pair_send(g_down))
    p_down = _pair_sum("pair_sum_down", core, g_down, got_down)
    g_out, got_up = _wgrad("wgrad_out", mixed, dmix, per_chip=False, h_rows=H_OUT, comm=_pair_send(g_up))
    p_up = _pair_sum("pair_sum_up", core, g_up, got_up)
    dq, dk0, dk1, dv0, dv1, dsink, ex_down, ex_up, got_out = _attention_bwd(
        q, dattn, attn, kd0, kd1, vd0, vd1, attn_sinks,
        comm=_merge(_chip_exchange(p_down), _chip_exchange(p_up), _pair_send(g_out)))
    p_out = _pair_sum("pair_sum_out", core, g_out, got_out)
    dproj, grad_x, dg_pre_mix = _in_proj_bwd(dq, dk0, dk1, dv0, dv1, dgb, dy, gc, xin, conv_full, xs, dh, pre_mix_norm,
                                             wf_in, rope)
    g_in, ex_out = _wgrad("wgrad_in", dproj, hn, per_chip=False, h_rows=H_IN, comm=_chip_exchange(p_out))
    small = dict(pre_mix_norm=dg_pre_mix, conv_w=dconv_w, attn_sinks=dsink[:, :8], attn_group_norm=dg_attn,
                 conv_group_norm=dg_conv, post_mix_norm=dg_post_mix, pre_mlp_norm=dg_pre_mlp, post_mlp_norm=dg_post_mlp,
                 loss_sum=loss_sum)
    r_down, r_up, r_out, r_in, small_total = _tail_reduce(g_in, [ex_down, ex_up, ex_out], _pack_small(small))

    out_g, out_d, out_m, out_v = {}, {}, {}, {}
    out_g["w_up"], out_d["w_up"], out_m["w_up"], out_v["w_up"] = _adamw_rows(
        "adamw_up", r_up, w_up[0], m_w_up[0], v_w_up[0], 256)
    out_g["w_down"], out_d["w_down"], out_m["w_down"], out_v["w_down"] = _adamw_rows(
        "adamw_down", r_down, w_down[0], m_w_down[0], v_w_down[0], 256)
    out_g["w_out"], out_d["w_out"], out_m["w_out"], out_v["w_out"] = _adamw_rows(
        "adamw_out", r_out, w_out[0], m_w_out[0], v_w_out[0], H_OUT)
    in_t = _adamw_rows("adamw_in", r_in, w_in[0].T, m_w_in[0].T, v_w_in[0].T, H_IN)
    out_g["w_in"], out_d["w_in"], out_m["w_in"], out_v["w_in"] = [t.T for t in in_t]

    loss = small_total[7, 8] * (0.5 / D_MODEL)
    for out, part in zip((out_g, out_d, out_m, out_v), _adamw_small(small_total, w, m, v)):
        out.update(part)

    def shaped(d):
        return [d[n].reshape(w[n].shape) for n in WEIGHT_ORDER]

    return (loss, grad_x[None], *shaped(out_g), *shaped(out_d), *shaped(out_m), *shaped(out_v))
```

```python
import math
from typing import Callable, NamedTuple

import jax
import jax.numpy as jnp
import numpy as np
from jax import lax
from jax.experimental import pallas as pl
from jax.experimental.pallas import tpu as pltpu

F32 = jnp.float32
BF16 = jnp.bfloat16

D_MODEL = 1024
HEAD_DIM = 64
Q_WIDTH = 512
KV_WIDTH = 128
CONV_WIDTH = 512
CONV_K = 3
D_FF = 4096
IN_COLS = 2304
QBLOCK = 128
ROT_DIM = 16
ROPE_THETA = 500000.0
NORM_EPS = 1e-6
NEG_INF = -1e30
N_CHIPS = 4

ADAM_LR = 0.001
ADAM_B1 = 0.9
ADAM_B2 = 0.999
ADAM_EPS = 1e-08
ADAM_WD = 0.01
ADAM_STEP = 10

H_UP, H_DOWN, H_OUT, H_IN = 512, 512, 128, 288
DOWN_EARLY_ROWS = 224

TOKEN_TILE = 512
WIDE_TOKEN_TILE = 1024
ATTN_FWD_BLOCKS = 16
ATTN_BWD_BLOCKS = 2
WGRAD_TOKEN_TILE = 4096
VMEM_LIMIT_V7X = 60 * 1024 * 1024

MESH = pl.DeviceIdType.MESH
ANY = pl.BlockSpec(memory_space=pl.ANY)
VMEM_WHOLE = pl.BlockSpec(memory_space=pltpu.VMEM)
SDS = jax.ShapeDtypeStruct


def _resident(shape):
    zeros = (0,) * len(shape)
    return pl.BlockSpec(shape, lambda *_: zeros, pipeline_mode=pl.Buffered(1))


def _rms(v):
    return lax.rsqrt(jnp.mean(v * v, axis=-1, keepdims=True) + NORM_EPS)


def _norm_bwd(dy, gain, vhat, rstd):
    t = dy * gain
    return rstd * (t - vhat * jnp.mean(t * vhat, axis=-1, keepdims=True))


def _colsum(v):
    return jnp.sum(v, axis=0, keepdims=True)


def _dot_nt(a, b):
    return lax.dot_general(a, b, (((1,), (1,)), ((), ())), preferred_element_type=F32)


def _dot_tn(a, b):
    return lax.dot_general(a, b, (((0,), (0,)), ((), ())), preferred_element_type=F32)


def _dot(a, b):
    return jnp.dot(a, b, preferred_element_type=F32)


def _chip_block(w_ref, chip):
    both = w_ref[pl.ds(2 * chip, 2)]
    return both.reshape(2 * both.shape[1], both.shape[2])


def _lane_lt64(shape):
    return lax.broadcasted_iota(jnp.int32, shape, 1) < HEAD_DIM


class _Comm(NamedTuple):
    operands: tuple
    out_shapes: tuple
    aliases: dict
    n_remote: int
    n_local: int
    plan: Callable
    after: Callable = None


def _merge(*comms):
    operands, out_shapes, aliases, parts = [], [], {}, []
    n_remote = n_local = 0
    for cm in comms:
        parts.append((len(operands), len(out_shapes), n_remote, n_local, cm))
        for k, v in cm.aliases.items():
            aliases[len(operands) + k] = len(out_shapes) + v
        operands += cm.operands
        out_shapes += cm.out_shapes
        n_remote += cm.n_remote
        n_local += cm.n_local

    def run(which, ins, outs, send, recv, loc):
        sends, recvs, locs = [], [], []
        for i0, o0, r0, l0, cm in parts:
            stage = getattr(cm, which)
            if stage is not None:
                s, r, l = stage(ins[i0:i0 + len(cm.operands)], outs[o0:o0 + len(cm.out_shapes)],
                                lambda k, r0=r0: send(r0 + k), lambda k, r0=r0: recv(r0 + k), lambda k, l0=l0: loc(l0 + k))
                sends, recvs, locs = sends + s, recvs + r, locs + l
        return sends, recvs, locs

    def plan(*args):
        return run("plan", *args)

    def after(*args):
        return run("after", *args)

    return _Comm(tuple(operands), tuple(out_shapes), aliases, n_remote, n_local, plan,
                 after if any(cm.after is not None for cm in comms) else None)


def _sem_scratch(comm):
    return [pltpu.SemaphoreType.DMA((max(comm.n_remote, 1),)), pltpu.SemaphoreType.DMA((max(comm.n_remote, 1),)),
            pltpu.SemaphoreType.DMA((max(comm.n_local, 1),))]


class _Rider(NamedTuple):
    body: Callable
    in_specs: list
    out_specs: list
    out_shape: list
    operands: tuple


def _pallas(body, *, name, grid, in_specs, out_specs, out_shape, operands, scratch=(), comm=None, rider=None):
    params = pltpu.CompilerParams(dimension_semantics=("arbitrary",) * len(grid), vmem_limit_bytes=VMEM_LIMIT_V7X)
    if rider is not None:
        own_in, own_out, ride_in, ride_out = len(in_specs), len(out_specs), len(rider.in_specs), len(rider.out_specs)
        own_body = body

        def body(*refs):
            o0 = own_in + ride_in
            s0 = o0 + own_out + ride_out
            own_body(*refs[:own_in], *refs[o0:o0 + own_out], *refs[s0:])
            first = None
            for axis in range(len(grid)):
                at_start = pl.program_id(axis) == 0
                first = at_start if first is None else jnp.logical_and(first, at_start)
            rider.body(first, *refs[own_in:o0], *refs[o0 + own_out:s0])

        in_specs, out_specs = list(in_specs) + rider.in_specs, list(out_specs) + rider.out_specs
        out_shape, operands = list(out_shape) + rider.out_shape, tuple(operands) + tuple(rider.operands)
    if comm is None:
        return pl.pallas_call(body, name=name, grid=grid, in_specs=in_specs, out_specs=out_specs, out_shape=out_shape,
                              scratch_shapes=list(scratch), compiler_params=params)(*operands)
    n_in, n_out, n_scr = len(in_specs), len(out_specs), len(scratch)
    c_in, c_out = len(comm.operands), len(comm.out_shapes)

    def with_comm(*refs):
        ins, c_ins = refs[:n_in], refs[n_in:n_in + c_in]
        o0 = n_in + c_in
        outs, c_outs = refs[o0:o0 + n_out], refs[o0 + n_out:o0 + n_out + c_out]
        s0 = o0 + n_out + c_out
        scr = refs[s0:s0 + n_scr]
        send_sems, recv_sems, local_sems = refs[s0 + n_scr:]
        first = last = None
        for axis, size in enumerate(grid):
            at_start, at_end = pl.program_id(axis) == 0, pl.program_id(axis) == size - 1
            first = at_start if first is None else jnp.logical_and(first, at_start)
            last = at_end if last is None else jnp.logical_and(last, at_end)

        def copies():
            return comm.plan(c_ins, c_outs, lambda k: send_sems.at[k], lambda k: recv_sems.at[k],
                             lambda k: local_sems.at[k])

        @pl.when(first)
        def _():
            sends, _, locs = copies()
            for cp in sends + locs:
                cp.start()

        body(*ins, *outs, *scr)

        @pl.when(last)
        def _():
            sends, recvs, locs = copies()
            for cp in recvs:
                cp.wait_recv()
            for cp in sends:
                cp.wait_send()
            for cp in locs:
                cp.wait()
            if comm.after is not None:
                sends, recvs, _ = comm.after(c_ins, c_outs, lambda k: send_sems.at[k], lambda k: recv_sems.at[k],
                                             lambda k: local_sems.at[k])
                for cp in sends:
                    cp.start()
                for cp in recvs:
                    cp.wait_recv()
                for cp in sends:
                    cp.wait_send()

    return pl.pallas_call(
        with_comm, name=name, grid=grid,
        in_specs=list(in_specs) + [ANY] * c_in, out_specs=list(out_specs) + [ANY] * c_out,
        out_shape=list(out_shape) + list(comm.out_shapes),
        scratch_shapes=list(scratch) + _sem_scratch(comm),
        input_output_aliases={n_in + k: n_out + v for k, v in comm.aliases.items()},
        compiler_params=params)(*operands, *comm.operands)


def _place():
    return lax.axis_index("x"), lax.axis_index("y"), lax.axis_index("c")


def _other_chips(x, y):
    return [(1 - x, y), (x, 1 - y), (1 - x, 1 - y)]


def _slot(px, py, pc):
    return 4 * px + 2 * py + pc


def _remote(src, dst, send_sem, recv_sem, to):
    return pltpu.make_async_remote_copy(src_ref=src, dst_ref=dst, send_sem=send_sem, recv_sem=recv_sem,
                                        device_id=to, device_id_type=MESH)


def _gather_first(half_block):
    def plan(ins, outs, send, recv, loc):
        (blk,), (full,) = ins, outs
        x, y, c = _place()
        chips = _other_chips(x, y)
        mine = full.at[_slot(x, y, c)]
        sends = [_remote(blk, mine, send(0), recv(0), (x, y, 1 - c))]
        sends += [_remote(blk, mine, send(1 + j), recv(1 + j), (*chip, c)) for j, chip in enumerate(chips)]
        recvs = [_remote(blk, full.at[_slot(x, y, 1 - c)], send(0), recv(0), (x, y, 1 - c))]
        recvs += [_remote(blk, full.at[_slot(*chip, c)], send(1 + j), recv(1 + j), (*chip, c))
                  for j, chip in enumerate(chips)]
        return sends, recvs, [pltpu.make_async_copy(blk, mine, loc(0))]

    return _Comm((half_block,), (SDS((2 * N_CHIPS,) + half_block.shape, half_block.dtype),), {}, 4, 1, plan)


def _gather_second(partly_gathered):
    def plan(ins, outs, send, recv, loc):
        (src,), (full,) = ins, outs
        x, y, c = _place()
        chips = _other_chips(x, y)
        sends = [_remote(src.at[_slot(*chip, c)], full.at[_slot(*chip, c)], send(j), recv(j), (x, y, 1 - c))
                 for j, chip in enumerate(chips)]
        recvs = [_remote(src.at[_slot(*chip, 1 - c)], full.at[_slot(*chip, 1 - c)], send(j), recv(j), (x, y, 1 - c))
                 for j, chip in enumerate(chips)]
        return sends, recvs, []

    return _Comm((partly_gathered,), (SDS(partly_gathered.shape, partly_gathered.dtype),), {0: 0}, 3, 0, plan)


def _relay_pieces(full, rows, x, y, c):
    start, half = rows[0], rows[1] // 2
    upper, lower = pl.ds(start, half), pl.ds(start + half, half)
    diagonal = full.at[_slot(1 - x, 1 - y, c)]
    return [(full.at[_slot(1 - x, y, c), upper], diagonal.at[upper], (x, 1 - y, c)),
            (full.at[_slot(x, 1 - y, c), lower], diagonal.at[lower], (1 - x, y, c))]


def _relay(half_block, so_far, first=None, second=None, third=None, third_after=None):
    has_block, has_buffer = half_block is not None, so_far is not None
    shape = so_far.shape if has_buffer else (2 * N_CHIPS,) + half_block.shape
    dtype = so_far.dtype if has_buffer else half_block.dtype

    def third_leg(rows, k, ins, outs, send, recv):
        src, full = (ins[-1] if has_buffer else outs[0]), outs[0]
        x, y, c = _place()
        span, sibling = pl.ds(*rows), (x, y, 1 - c)
        here, there = _slot(1 - x, 1 - y, c), _slot(1 - x, 1 - y, 1 - c)
        return ([_remote(src.at[here, span], full.at[here, span], send(k), recv(k), sibling)],
                [_remote(src.at[there, span], full.at[there, span], send(k), recv(k), sibling)])

    def plan(ins, outs, send, recv, loc):
        src, full = (ins[-1] if has_buffer else outs[0]), outs[0]
        x, y, c = _place()
        sibling = (x, y, 1 - c)
        sends, recvs, locs = [], [], []
        if first is not None:
            span = pl.ds(*first)
            blk, mine = ins[0].at[span], full.at[_slot(x, y, c), span]
            for k, peer in enumerate([sibling, (1 - x, y, c), (x, 1 - y, c)]):
                sends.append(_remote(blk, mine, send(k), recv(k), peer))
                recvs.append(_remote(blk, full.at[_slot(*peer), span], send(k), recv(k), peer))
            locs.append(pltpu.make_async_copy(blk, mine, loc(0)))
        if second is not None:
            span = pl.ds(*second)
            for k, chip in enumerate([(1 - x, y), (x, 1 - y)]):
                sends.append(_remote(src.at[_slot(*chip, c), span], full.at[_slot(*chip, c), span], send(3 + k), recv(3 + k),
                                     sibling))
                recvs.append(_remote(src.at[_slot(*chip, 1 - c), span], full.at[_slot(*chip, 1 - c), span], send(3 + k),
                                     recv(3 + k), sibling))
            for k, (piece, lands, peer) in enumerate(_relay_pieces(full, second, x, y, c)):
                sends.append(_remote(piece, piece, send(5 + k), recv(5 + k), peer))
                recvs.append(_remote(lands, lands, send(5 + k), recv(5 + k), peer))
        if third is not None:
            s, r = third_leg(third, 7, ins, outs, send, recv)
            sends, recvs = sends + s, recvs + r
        return sends, recvs, locs

    def after(ins, outs, send, recv, loc):
        s, r = third_leg(third_after, 8, ins, outs, send, recv)
        return s, r, []

    operands = ((half_block,) if has_block else ()) + ((so_far,) if has_buffer else ())
    return _Comm(operands, (SDS(shape, dtype),), {len(operands) - 1: 0} if has_buffer else {}, 9, 1, plan,
                 after if third_after is not None else None)


def _gather_whole(first, others, small_block):
    shards = (first, *others)
    n = len(shards)
    hs = [s.shape[0] // 2 for s in shards]
    rows = hs[0]

    def body(*refs):
        src, small_ref = refs[:n], refs[n]
        out_ref, small_out_ref, half_out = refs[n + 1], refs[n + 2], refs[n + 3:2 * n + 2]
        stage, half = refs[2 * n + 2:3 * n + 2], refs[3 * n + 2:4 * n + 2]
        send_sems, recv_sems, local_sems = refs[4 * n + 2:]
        x, y, c = _place()
        me, sibling = (x, y, c), (x, y, 1 - c)
        neighbours, diagonal = [(1 - x, y), (x, 1 - y)], (1 - x, 1 - y)
        loads = [pltpu.make_async_copy(src[k].at[pl.ds(c * hs[k], hs[k])], stage[k], local_sems.at[2 + k]) for k in range(n)]
        loads[0].start()
        loads[0].wait()
        for cp in loads[1:]:
            cp.start()
        blk_ref = half[0]
        blk_ref[...] = stage[0][...].astype(BF16)

        def copy(k, block, to, src=None):
            return _remote(out_ref.at[_slot(*block)] if src is None else src, out_ref.at[_slot(*block)],
                           send_sems.at[k], recv_sems.at[k], to)

        def small_copy(k, chip, to):
            return _remote(small_ref, small_out_ref.at[2 * chip[0] + chip[1]], send_sems.at[8 + k], recv_sems.at[8 + k], to)

        mine = pltpu.make_async_copy(blk_ref, out_ref.at[_slot(*me)], local_sems.at[0])
        mine_small = pltpu.make_async_copy(small_ref, small_out_ref.at[2 * x + y], local_sems.at[1])
        mine.start()
        mine_small.start()
        started = [copy(0, me, sibling, src=blk_ref)]
        started += [copy(1 + k, me, (*chip, c), src=blk_ref) for k, chip in enumerate(neighbours)]
        started += [small_copy(k, (x, y), (*chip, c)) for k, chip in enumerate(neighbours + [diagonal])]
        for cp in started:
            cp.start()
        stores = []
        for k in range(1, n):
            loads[k].wait()
            half[k][...] = stage[k][...].astype(BF16)
            stores.append(pltpu.make_async_copy(half[k], half_out[k - 1], local_sems.at[2 + n + k]))
            stores[-1].start()
        pieces = _relay_pieces(out_ref, (0, rows), x, y, c)
        for k, chip in enumerate(neighbours):
            copy(1 + k, (*chip, c), me).wait_recv()
            piece, _, peer = pieces[k]
            started += [copy(3 + k, (*chip, c), sibling), _remote(piece, piece, send_sems.at[5 + k], recv_sems.at[5 + k], peer)]
            started[-2].start()
            started[-1].start()
        for k, (_, lands, peer) in enumerate(pieces):
            _remote(lands, lands, send_sems.at[5 + k], recv_sems.at[5 + k], peer).wait_recv()
        started.append(copy(7, (*diagonal, c), sibling))
        started[-1].start()
        copy(0, sibling, me).wait_recv()
        for k, chip in enumerate(neighbours):
            copy(3 + k, (*chip, 1 - c), me).wait_recv()
        copy(7, (*diagonal, 1 - c), me).wait_recv()
        for k, chip in enumerate(neighbours + [diagonal]):
            small_copy(k, chip, me).wait_recv()
        for cp in started:
            cp.wait_send()
        mine.wait()
        mine_small.wait()
        for cp in stores:
            cp.wait()

    return pl.pallas_call(
        body, name="gather_whole", in_specs=[ANY] * (n + 1), out_specs=[ANY] * (n + 1),
        out_shape=[SDS((2 * N_CHIPS, rows, D_MODEL), BF16), SDS((N_CHIPS,) + small_block.shape, small_block.dtype)]
                  + [SDS((h, D_MODEL), BF16) for h in hs[1:]],
        scratch_shapes=[pltpu.VMEM((h, D_MODEL), F32) for h in hs] + [pltpu.VMEM((h, D_MODEL), BF16) for h in hs]
                       + [pltpu.SemaphoreType.DMA((11,)), pltpu.SemaphoreType.DMA((11,)), pltpu.SemaphoreType.DMA((2 + 2 * n,))],
        compiler_params=pltpu.CompilerParams(vmem_limit_bytes=VMEM_LIMIT_V7X),
    )(*shards, small_block)


def _pair_send(grads):
    def plan(ins, outs, send, recv, loc):
        (g,), (got,) = ins, outs
        x, y, c = _place()
        copies = [_remote(g.at[j, 1 - c], got.at[j], send(j), recv(j), (x, y, 1 - c)) for j in range(N_CHIPS)]
        return copies, copies, []

    shape = (grads.shape[0],) + grads.shape[2:]
    return _Comm((grads,), (SDS(shape, grads.dtype),), {}, N_CHIPS, 0, plan)


def _chip_exchange(partial):
    def plan(ins, outs, send, recv, loc):
        (p,), (got,) = ins, outs
        x, y, c = _place()
        my_chip = 2 * x + y
        chips = _other_chips(x, y)
        sends = [_remote(p.at[2 * chip[0] + chip[1]], got.at[my_chip], send(j), recv(j), (*chip, c))
                 for j, chip in enumerate(chips)]
        recvs = [_remote(p.at[my_chip], got.at[2 * chip[0] + chip[1]], send(j), recv(j), (*chip, c))
                 for j, chip in enumerate(chips)]
        return sends, recvs, [pltpu.make_async_copy(p.at[my_chip], got.at[my_chip], loc(0))]

    return _Comm((partial,), (SDS(partial.shape, partial.dtype),), {}, 3, 1, plan)


def _pair_sum(name, core, grads, received):
    h = grads.shape[2]

    def body(core_ref, g_ref, r_ref, o_ref):
        o_ref[...] = (g_ref[0] + r_ref[...]).astype(BF16)

    return pl.pallas_call(
        body, name=name,
        grid_spec=pltpu.PrefetchScalarGridSpec(
            num_scalar_prefetch=1, grid=(N_CHIPS,),
            in_specs=[pl.BlockSpec((1, 1, h, D_MODEL), lambda j, core_ref: (j, core_ref[0], 0, 0)),
                      pl.BlockSpec((1, h, D_MODEL), lambda j, core_ref: (j, 0, 0))],
            out_specs=pl.BlockSpec((1, h, D_MODEL), lambda j, core_ref: (j, 0, 0))),
        out_shape=SDS((N_CHIPS, h, D_MODEL), BF16),
        compiler_params=pltpu.CompilerParams(dimension_semantics=("arbitrary",), vmem_limit_bytes=VMEM_LIMIT_V7X),
    )(core, grads, received)


SMALL_ROWS = 8


def _sum_blocks(ref):
    return (ref[0].astype(F32) + ref[1].astype(F32)) + (ref[2].astype(F32) + ref[3].astype(F32))


def _tail_reduce(last_grads, exchanged, small):
    n = len(exchanged)
    h = last_grads.shape[2]

    def body(*refs):
        g_ref, ex, small_ref = refs[0], refs[1:1 + n], refs[1 + n]
        o0 = 2 + n
        out, out_last, small_out = refs[o0:o0 + n], refs[o0 + n], refs[o0 + n + 1]
        s0 = o0 + n + 2
        halves, half_last = refs[s0:s0 + n], refs[s0 + n]
        own, got, part, exch, small_buf = refs[s0 + n + 1:s0 + n + 6]
        ex_buf = refs[s0 + n + 6:s0 + 2 * n + 6]
        pair_send, pair_recv, chip_send, chip_recv, share_send, share_recv, small_send, small_recv, local_sems = refs[s0 + 2 * n + 6:]
        x, y, c = _place()
        sibling = (x, y, 1 - c)
        my_chip, me = 2 * x + y, _slot(x, y, c)
        chips = _other_chips(x, y)[::-1]

        order = [2 * chip[0] + chip[1] for chip in chips] + [my_chip]
        to_sibling = [_remote(g_ref.at[j, 1 - c], got.at[j], pair_send.at[j], pair_recv.at[j], sibling) for j in order]
        load_own = [pltpu.make_async_copy(g_ref.at[j, c], own.at[j], local_sems.at[j]) for j in order]
        load_ex = [pltpu.make_async_copy(ex[k], ex_buf[k], local_sems.at[N_CHIPS + n + 1 + k]) for k in range(n)]
        for give, keep in zip(to_sibling, load_own):
            give.start()
            keep.start()
        for cp in load_ex:
            cp.start()

        small_buf[me] = small_ref[...]
        small_copies = []
        for mask in range(1, 8):
            peer = (x ^ (mask >> 2), y ^ ((mask >> 1) & 1), c ^ (mask & 1))
            small_copies.append(_remote(small_ref, small_buf.at[me], small_send.at[mask - 1], small_recv.at[mask - 1], peer))
        for cp in small_copies:
            cp.start()

        def share(k, half_ref, out_ref):
            keep = pltpu.make_async_copy(half_ref, out_ref.at[c], local_sems.at[N_CHIPS + k])
            give = _remote(half_ref, out_ref.at[c], share_send.at[k], share_recv.at[k], sibling)
            take = _remote(half_ref, out_ref.at[1 - c], share_send.at[k], share_recv.at[k], sibling)
            keep.start()
            give.start()
            return keep, give, take

        def pair_sum(block):
            _remote(g_ref.at[block, 1 - c], got.at[block], pair_send.at[block], pair_recv.at[block], sibling).wait_recv()
            pltpu.make_async_copy(g_ref.at[block, c], own.at[block], local_sems.at[block]).wait()
            part[block] = (own[block] + got[block]).astype(BF16)

        to_chips = []
        for j, chip in enumerate(chips):
            block = 2 * chip[0] + chip[1]
            pair_sum(block)
            to_chips.append(_remote(part.at[block], exch.at[my_chip], chip_send.at[j], chip_recv.at[j], (*chip, c)))
            to_chips[-1].start()
        pair_sum(my_chip)
        exch[my_chip] = part[my_chip]
        from_chips = [_remote(part.at[my_chip], exch.at[2 * chip[0] + chip[1]], chip_send.at[j], chip_recv.at[j], (*chip, c))
                      for j, chip in enumerate(chips)]

        shares = []
        for k in range(n):
            load_ex[k].wait()
            halves[k][...] = _sum_blocks(ex_buf[k])
            shares.append(share(k, halves[k], out[k]))

        for cp in small_copies:
            cp.wait_recv()
        total = small_buf[0]
        for d in range(1, 8):
            total = total + small_buf[d]
        small_out[...] = total

        for cp in from_chips:
            cp.wait_recv()
        half_last[...] = _sum_blocks(exch)
        shares.append(share(n, half_last, out_last))

        for keep, give, take in shares:
            take.wait_recv()
            give.wait_send()
            keep.wait()
        for cp in to_sibling + to_chips + small_copies:
            cp.wait_send()

    blocks = (N_CHIPS, h, D_MODEL)
    return pl.pallas_call(
        body, name="tail_reduce",
        in_specs=[ANY] * (n + 1) + [VMEM_WHOLE], out_specs=[ANY] * (n + 1) + [VMEM_WHOLE],
        out_shape=[SDS((2,) + e.shape[1:], F32) for e in exchanged] + [SDS((2, h, D_MODEL), F32), SDS(small.shape, F32)],
        scratch_shapes=[pltpu.VMEM(e.shape[1:], F32) for e in exchanged] + [pltpu.VMEM((h, D_MODEL), F32)]
                       + [pltpu.VMEM(blocks, F32), pltpu.VMEM(blocks, F32), pltpu.VMEM(blocks, BF16), pltpu.VMEM(blocks, BF16),
                          pltpu.VMEM((8,) + small.shape, F32)]
                       + [pltpu.VMEM(e.shape, BF16) for e in exchanged]
                       + [pltpu.SemaphoreType.DMA((N_CHIPS,)), pltpu.SemaphoreType.DMA((N_CHIPS,)),
                          pltpu.SemaphoreType.DMA((3,)), pltpu.SemaphoreType.DMA((3,)),
                          pltpu.SemaphoreType.DMA((n + 1,)), pltpu.SemaphoreType.DMA((n + 1,)),
                          pltpu.SemaphoreType.DMA((7,)), pltpu.SemaphoreType.DMA((7,)),
                          pltpu.SemaphoreType.DMA((N_CHIPS + 2 * n + 1,))],
        compiler_params=pltpu.CompilerParams(vmem_limit_bytes=VMEM_LIMIT_V7X),
    )(last_grads, *exchanged, small)


def _rope_expansion():
    half = ROT_DIM // 2
    expand = np.zeros((2 * half, 3 * 128), np.float32)
    const = np.zeros((1, 3 * 128), np.float32)
    for lane in range(128):
        d = lane % HEAD_DIM
        if d < ROT_DIM:
            expand[d % half, lane] = 1.0
        else:
            const[0, lane] = 1.0
        if d < half:
            expand[half + d, 128 + lane] = -1.0
        elif d < ROT_DIM:
            expand[half + d - half, 256 + lane] = 1.0
    return expand, const


ROPE_PIECES = 3 * ROT_DIM


def _rope_inputs(seq):
    pos = jnp.arange(seq, dtype=F32)
    inv_freq = ROPE_THETA ** (-jnp.arange(0, ROT_DIM, 2, dtype=F32) / ROT_DIM)
    ang = pos[:, None] * inv_freq[None, :]
    cs = jnp.concatenate([jnp.cos(ang), jnp.sin(ang)], axis=1)
    hi = lax.reduce_precision(cs, 8, 7)
    mid = lax.reduce_precision(cs - hi, 8, 7)
    low = cs - hi - mid
    expand, const = _rope_expansion()
    pieces = jnp.concatenate([hi, mid, low], axis=1).astype(BF16)
    return pieces, jnp.asarray(np.concatenate([expand] * 3, axis=0), BF16), jnp.asarray(const)


def _rope_specs(tb):
    return [pl.BlockSpec((tb, ROPE_PIECES), lambda i: (i, 0)), _resident((ROPE_PIECES, 3 * 128)), _resident((1, 3 * 128))]


def _rope_tile(pieces_ref, expand_ref, const_ref):
    tables = _dot(pieces_ref[...], expand_ref[...]) + const_ref[...]
    return tables[:, 0:128], tables[:, 128:256], tables[:, 256:384]


def _rope(t, c, sa, sb):
    half = ROT_DIM // 2
    return t * c + pltpu.roll(t, 128 - half, 1) * sa + pltpu.roll(t, half, 1) * sb


def _rope_transposed(dt, c, sa, sb):
    half = ROT_DIM // 2
    return dt * c + pltpu.roll(dt * sa, half, 1) + pltpu.roll(dt * sb, 128 - half, 1)


def _in_proj(x, g_pre, w_in_t, rope, comm=None):
    seq = x.shape[0]
    tb = min(seq, WIDE_TOKEN_TILE)

    def body(x_ref, g_ref, w_ref, c_ref, sa_ref, sb_ref,
             q_ref, kd0_ref, kd1_ref, vd0_ref, vd1_ref, gb_ref, gc_ref, xin_ref, hn_ref):
        xv = x_ref[...]
        hn = (xv * _rms(xv) * g_ref[...]).astype(BF16)
        hn_ref[...] = hn
        proj = _dot_nt(hn, w_ref[...].reshape(IN_COLS, D_MODEL))
        c, sa, sb = _rope_tile(c_ref, sa_ref, sb_ref)
        scale = 1.0 / math.sqrt(HEAD_DIM)
        for p in range(Q_WIDTH // 128):
            q_ref[:, 128 * p:128 * (p + 1)] = (_rope(proj[:, 128 * p:128 * (p + 1)], c, sa, sb) * scale).astype(BF16)
        k = _rope(proj[:, Q_WIDTH:Q_WIDTH + KV_WIDTH], c, sa, sb)
        v = proj[:, Q_WIDTH + KV_WIDTH:Q_WIDTH + 2 * KV_WIDTH]
        low = _lane_lt64(k.shape)
        k_sw, v_sw = pltpu.roll(k, HEAD_DIM, 1), pltpu.roll(v, HEAD_DIM, 1)
        kd0_ref[...] = jnp.where(low, k, k_sw).astype(BF16)
        kd1_ref[...] = jnp.where(low, k_sw, k).astype(BF16)
        vd0_ref[...] = jnp.where(low, v, v_sw).astype(BF16)
        vd1_ref[...] = jnp.where(low, v_sw, v).astype(BF16)
        base = Q_WIDTH + 2 * KV_WIDTH
        gb_ref[...] = proj[:, base:base + CONV_WIDTH].astype(BF16)
        gc_ref[...] = proj[:, base + CONV_WIDTH:base + 2 * CONV_WIDTH].astype(BF16)
        xin_ref[...] = proj[:, base + 2 * CONV_WIDTH:base + 3 * CONV_WIDTH].astype(BF16)

    tile = lambda w: pl.BlockSpec((tb, w), lambda i: (i, 0))
    return _pallas(
        body, name="in_proj", grid=(seq // tb,),
        in_specs=[tile(D_MODEL), _resident((1, D_MODEL)), _resident(w_in_t.shape), *_rope_specs(tb)],
        out_specs=[tile(Q_WIDTH), tile(128), tile(128), tile(128), tile(128),
                   tile(CONV_WIDTH), tile(CONV_WIDTH), tile(CONV_WIDTH), tile(D_MODEL)],
        out_shape=[SDS((seq, Q_WIDTH), BF16)] + [SDS((seq, 128), BF16)] * 4
                  + [SDS((seq, CONV_WIDTH), BF16)] * 3 + [SDS((seq, D_MODEL), BF16)],
        operands=(x, g_pre, w_in_t, *rope), comm=comm)


def _attn_valid(i):
    shape = (4 * QBLOCK, 2 * QBLOCK)
    row = lax.broadcasted_iota(jnp.int32, shape, 0)
    col = lax.broadcasted_iota(jnp.int32, shape, 1)
    qi = row & (QBLOCK - 1)
    return (col > qi) & (col <= qi + QBLOCK) & ((col >= QBLOCK) | (i > 0))


def _stack_heads(pair0, pair1):
    low = _lane_lt64(pair0.shape)
    zero = jnp.zeros_like(pair0)
    return jnp.concatenate([jnp.where(low, pair0, zero), jnp.where(low, zero, pair0),
                            jnp.where(low, pair1, zero), jnp.where(low, zero, pair1)], axis=0)


def _unstack_heads(stacked):
    low = _lane_lt64((QBLOCK, 128))
    pair0 = jnp.where(low, stacked[0:QBLOCK], stacked[QBLOCK:2 * QBLOCK])
    pair1 = jnp.where(low, stacked[2 * QBLOCK:3 * QBLOCK], stacked[3 * QBLOCK:4 * QBLOCK])
    return pair0, pair1


def _sink_column(sink_ref, kv_head):
    row = lax.broadcasted_iota(jnp.int32, (4 * QBLOCK, 1), 0)
    s = [sink_ref[0, 4 * kv_head + j] for j in range(4)]
    return jnp.where(row < QBLOCK, s[0], jnp.where(row < 2 * QBLOCK, s[1], jnp.where(row < 3 * QBLOCK, s[2], s[3])))


def _band(ref, i):
    prev = pl.multiple_of(jnp.maximum(i - 1, 0) * QBLOCK, QBLOCK)
    own = pl.multiple_of(i * QBLOCK, QBLOCK)
    return jnp.concatenate([ref[pl.ds(prev, QBLOCK), :], ref[pl.ds(own, QBLOCK), :]], axis=0), prev, own


def _softmax_with_sink(s, sink_col):
    m = jnp.maximum(jnp.max(s, axis=-1, keepdims=True), sink_col)
    p = jnp.exp(s - m)
    e_sink = jnp.exp(sink_col - m)
    inv_l = 1.0 / (jnp.sum(p, axis=-1, keepdims=True) + e_sink)
    return p, e_sink, inv_l


def _attention_fwd(q, kd0, kd1, vd0, vd1, sinks, comm=None):
    seq = q.shape[0]

    nb = ATTN_FWD_BLOCKS

    def body(sink_ref, q_ref, kd0_ref, kd1_ref, vd0_ref, vd1_ref, o_ref):
        for b in range(nb):
            i = pl.program_id(0) * nb + b
            rows = slice(QBLOCK * b, QBLOCK * (b + 1))
            valid = _attn_valid(i)
            for kv_head, (k_ref, v_ref) in enumerate(((kd0_ref, vd0_ref), (kd1_ref, vd1_ref))):
                kband, _, _ = _band(k_ref, i)
                vband, _, _ = _band(v_ref, i)
                base = 256 * kv_head
                qm = _stack_heads(q_ref[rows, base:base + 128], q_ref[rows, base + 128:base + 256])
                s = jnp.where(valid, _dot_nt(qm, kband), NEG_INF)
                p, _, inv_l = _softmax_with_sink(s, _sink_column(sink_ref, kv_head))
                o = _dot(p.astype(BF16), vband) * inv_l
                pair0, pair1 = _unstack_heads(o)
                o_ref[rows, base:base + 128] = pair0.astype(BF16)
                o_ref[rows, base + 128:base + 256] = pair1.astype(BF16)

    blk = pl.BlockSpec((nb * QBLOCK, Q_WIDTH), lambda i: (i, 0))
    full = _resident((seq, 128))
    return _pallas(
        body, name="attention_fwd", grid=(seq // (nb * QBLOCK),),
        in_specs=[pl.BlockSpec(memory_space=pltpu.SMEM), blk, full, full, full, full],
        out_specs=[blk], out_shape=[SDS((seq, Q_WIDTH), BF16)],
        operands=(sinks, q, kd0, kd1, vd0, vd1), comm=comm)


HALO = 16


def _conv_parts(gc, xin, gc_halo, xin_halo, conv_w, first):
    tb = gc.shape[0]
    u = gc.astype(F32) * xin.astype(F32)
    u_halo = jnp.where(first, 0.0, gc_halo.astype(F32) * xin_halo.astype(F32))
    ext = jnp.concatenate([u_halo, u], axis=0)
    u1 = pltpu.roll(ext, 1, 0)[HALO:HALO + tb]
    u2 = pltpu.roll(ext, 2, 0)[HALO:HALO + tb]
    y = conv_w[0:1, :] * u2 + conv_w[1:2, :] * u1 + conv_w[2:3, :] * u
    return u, u1, u2, y


def _halo_prev(tb, w):
    return pl.BlockSpec((HALO, w), lambda i: (jnp.maximum(i * (tb // HALO) - 1, 0), 0))


def _residual_mid(x, mix, g_post_mix):
    mix_f = mix.astype(F32)
    return x + mix_f * _rms(mix_f) * g_post_mix


def _mix_out(attn, gb, gc, xin, conv_w, g_attn, g_conv, w_out, comm=None):
    seq = attn.shape[0]
    tb = min(seq, WIDE_TOKEN_TILE)

    def body(a_ref, gb_ref, gc_ref, xin_ref, gch_ref, xinh_ref, cw_ref, ga_ref, gcn_ref, w_ref, mix_ref, mixed_ref):
        first = pl.program_id(0) == 0
        _, _, _, y = _conv_parts(gc_ref[...], xin_ref[...], gch_ref[...], xinh_ref[...], cw_ref[...], first)
        conv = gb_ref[...].astype(F32) * y
        a = a_ref[...].astype(F32)
        mixed_ref[:, 0:Q_WIDTH] = (a * _rms(a) * ga_ref[...]).astype(BF16)
        mixed_ref[:, Q_WIDTH:] = (conv * _rms(conv) * gcn_ref[...]).astype(BF16)
        mix_ref[...] = _dot(mixed_ref[...], w_ref[...].reshape(D_MODEL, D_MODEL)).astype(BF16)

    tile = lambda w: pl.BlockSpec((tb, w), lambda i: (i, 0))
    return _pallas(
        body, name="mix_out", grid=(seq // tb,),
        in_specs=[tile(Q_WIDTH), tile(CONV_WIDTH), tile(CONV_WIDTH), tile(CONV_WIDTH),
                  _halo_prev(tb, CONV_WIDTH), _halo_prev(tb, CONV_WIDTH),
                  _resident((CONV_K, CONV_WIDTH)), _resident((1, Q_WIDTH)), _resident((1, CONV_WIDTH)),
                  _resident(w_out.shape)],
        out_specs=[tile(D_MODEL), tile(D_MODEL)],
        out_shape=[SDS((seq, D_MODEL), BF16), SDS((seq, D_MODEL), BF16)],
        operands=(attn, gb, gc, xin, gc, xin, conv_w, g_attn, g_conv, w_out), comm=comm)


def _mlp_fwd_bwd(x, mix, target, g_post_mix, g_pre_mlp, g_post_mlp, w_up, w_down):
    seq = x.shape[0]
    tb = TOKEN_TILE

    def body(x_ref, mix_ref, t_ref, gpm_ref, g2_ref, g4_ref, wup_ref, wdown_ref,
             up_ref, hn2_ref, dmlp_ref, dup_ref, dh_ref, dmix_ref, loss_ref, dg4_ref, dg2_ref, dgpm_ref):
        @pl.when(pl.program_id(0) == 0)
        def _():
            for ref in (loss_ref, dg4_ref, dg2_ref, dgpm_ref):
                ref[...] = jnp.zeros_like(ref)

        halves = [slice(0, tb // 2), slice(tb // 2, tb)]
        chunks = [slice(1024 * j, 1024 * (j + 1)) for j in range(N_CHIPS)]
        hv, hn2, mlp, dout, dmlp, dhn2 = [], [], [], [], [], []
        for rows in halves:
            hv.append(_residual_mid(x_ref[rows, :], mix_ref[rows, :], gpm_ref[...]))
            hn2.append((hv[-1] * _rms(hv[-1]) * g2_ref[...]).astype(BF16))
            hn2_ref[rows, :] = hn2[-1]
        for k, rows in enumerate(halves):
            acc = None
            for j, cols in enumerate(chunks):
                up = jnp.maximum(_dot(hn2[k], _chip_block(wup_ref, j)), 0.0)
                up_ref[rows, cols] = up.astype(BF16)
                part = _dot((up * up).astype(BF16), _chip_block(wdown_ref, j))
                acc = part if acc is None else acc + part
            mlp.append(acc)
        loss = jnp.zeros((1, 1), F32)
        dg4 = jnp.zeros((1, D_MODEL), F32)
        for k, rows in enumerate(halves):
            rstd = _rms(mlp[k])
            zhat = mlp[k] * rstd
            diff = hv[k] + zhat * g4_ref[...] - t_ref[rows, :]
            loss = loss + jnp.sum(jnp.sum(diff * diff, axis=1, keepdims=True), axis=0, keepdims=True)
            dout.append(diff * (1.0 / D_MODEL))
            dg4 = dg4 + _colsum(dout[k] * zhat)
            dmlp.append(_norm_bwd(dout[k], g4_ref[...], zhat, rstd).astype(BF16))
            dmlp_ref[rows, :] = dmlp[k]
        for k, rows in enumerate(halves):
            acc = None
            for j, cols in enumerate(chunks):
                dact = _dot_nt(dmlp[k], _chip_block(wdown_ref, j))
                dup = (dact * (2.0 * up_ref[rows, cols].astype(F32))).astype(BF16)
                dup_ref[rows, cols] = dup
                part = _dot_nt(dup, _chip_block(wup_ref, j))
                acc = part if acc is None else acc + part
            dhn2.append(acc)
        dg2 = jnp.zeros((1, D_MODEL), F32)
        dgpm = jnp.zeros((1, D_MODEL), F32)
        for k, rows in enumerate(halves):
            r2 = _rms(hv[k])
            hhat = hv[k] * r2
            dg2 = dg2 + _colsum(dhn2[k] * hhat)
            dh = dout[k] + _norm_bwd(dhn2[k], g2_ref[...], hhat, r2)
            dh_ref[rows, :] = dh.astype(BF16)
            mix_v = mix_ref[rows, :].astype(F32)
            rz = _rms(mix_v)
            zhat = mix_v * rz
            dgpm = dgpm + _colsum(dh * zhat)
            dmix_ref[rows, :] = _norm_bwd(dh, gpm_ref[...], zhat, rz).astype(BF16)
        loss_ref[...] += loss
        dg4_ref[...] += dg4
        dg2_ref[...] += dg2
        dgpm_ref[...] += dgpm

    tile = lambda w: pl.BlockSpec((tb, w), lambda i: (i, 0))
    vec = pl.BlockSpec((1, D_MODEL), lambda i: (0, 0))
    return _pallas(
        body, name="mlp_fwd_bwd", grid=(seq // tb,),
        in_specs=[tile(D_MODEL), tile(D_MODEL), tile(D_MODEL), _resident((1, D_MODEL)), _resident((1, D_MODEL)),
                  _resident((1, D_MODEL)), _resident(w_up.shape), _resident(w_down.shape)],
        out_specs=[tile(D_FF), tile(D_MODEL), tile(D_MODEL), tile(D_FF), tile(D_MODEL), tile(D_MODEL),
                   pl.BlockSpec((1, 1), lambda i: (0, 0)), vec, vec, vec],
        out_shape=[SDS((seq, D_FF), BF16), SDS((seq, D_MODEL), BF16), SDS((seq, D_MODEL), BF16), SDS((seq, D_FF), BF16),
                   SDS((seq, D_MODEL), BF16), SDS((seq, D_MODEL), BF16),
                   SDS((1, 1), F32), SDS((1, D_MODEL), F32), SDS((1, D_MODEL), F32), SDS((1, D_MODEL), F32)],
        operands=(x, mix, target, g_post_mix, g_pre_mlp, g_post_mlp, w_up, w_down))


def _mix_bwd(dmix, attn, gb, gc, xin, conv_w, g_attn, g_conv, w_out, n_k):
    seq = attn.shape[0]
    tb = seq // (N_CHIPS * n_k)

    def body(first, dmix_ref, a_ref, gb_ref, gc_ref, xin_ref, gch_ref, xinh_ref, cw_ref, ga_ref, gcn_ref, w_ref,
             dattn_ref, dgb_ref, dy_ref, dga_ref, dgcn_ref, dcw_ref):
        @pl.when(first)
        def _():
            dga_ref[...] = jnp.zeros_like(dga_ref)
            dgcn_ref[...] = jnp.zeros_like(dgcn_ref)
            dcw_ref[...] = jnp.zeros_like(dcw_ref)

        dmixed = _dot_nt(dmix_ref[...], w_ref[...].reshape(D_MODEL, D_MODEL))
        a = a_ref[...].astype(F32)
        ra = _rms(a)
        ahat = a * ra
        dan = dmixed[:, 0:Q_WIDTH]
        dga_ref[...] += _colsum(dan * ahat)
        dattn_ref[...] = _norm_bwd(dan, ga_ref[...], ahat, ra).astype(BF16)
        gbv = gb_ref[...].astype(F32)
        u, u1, u2, y = _conv_parts(gc_ref[...], xin_ref[...], gch_ref[...], xinh_ref[...], cw_ref[...], first)
        conv = gbv * y
        rc = _rms(conv)
        chat = conv * rc
        dcn = dmixed[:, Q_WIDTH:]
        dgcn_ref[...] += _colsum(dcn * chat)
        dconv = _norm_bwd(dcn, gcn_ref[...], chat, rc)
        dgb_ref[...] = (dconv * y).astype(BF16)
        dy = dconv * gbv
        dy_ref[...] = dy.astype(BF16)
        dcw_ref[0:1, :] += _colsum(dy * u2)
        dcw_ref[1:2, :] += _colsum(dy * u1)
        dcw_ref[2:3, :] += _colsum(dy * u)

    tile = lambda w: pl.BlockSpec((tb, w), lambda j, k: (j * n_k + k, 0))
    halo = lambda w: pl.BlockSpec((HALO, w), lambda j, k: (jnp.maximum((j * n_k + k) * (tb // HALO) - 1, 0), 0))
    whole = lambda shape: pl.BlockSpec(shape, lambda j, k: (0,) * len(shape))
    return _Rider(
        body,
        in_specs=[tile(D_MODEL), tile(Q_WIDTH), tile(CONV_WIDTH), tile(CONV_WIDTH), tile(CONV_WIDTH),
                  halo(CONV_WIDTH), halo(CONV_WIDTH),
                  _resident((CONV_K, CONV_WIDTH)), _resident((1, Q_WIDTH)), _resident((1, CONV_WIDTH)),
                  _resident(w_out.shape)],
        out_specs=[tile(Q_WIDTH), tile(CONV_WIDTH), tile(CONV_WIDTH),
                   whole((1, Q_WIDTH)), whole((1, CONV_WIDTH)), whole((CONV_K, CONV_WIDTH))],
        out_shape=[SDS((seq, Q_WIDTH), BF16), SDS((seq, CONV_WIDTH), BF16), SDS((seq, CONV_WIDTH), BF16),
                   SDS((1, Q_WIDTH), F32), SDS((1, CONV_WIDTH), F32), SDS((CONV_K, CONV_WIDTH), F32)],
        operands=(dmix, attn, gb, gc, xin, gc, xin, conv_w, g_attn, g_conv, w_out))


def _attention_bwd(q, dattn, attn, kd0, kd1, vd0, vd1, sinks, comm=None):
    seq = q.shape[0]
    nb = ATTN_BWD_BLOCKS

    def body(sink_ref, q_ref, do_ref, o_ref, kd0_ref, kd1_ref, vd0_ref, vd1_ref,
             dq_ref, dk0_ref, dk1_ref, dv0_ref, dv1_ref, dsink_ref):
        @pl.when(pl.program_id(0) == 0)
        def _():
            for r in (dk0_ref, dk1_ref, dv0_ref, dv1_ref, dsink_ref):
                r[...] = jnp.zeros_like(r)

        lane = lax.broadcasted_iota(jnp.int32, (1, 128), 1)
        dsink = jnp.zeros((1, 128), F32)
        for b in range(nb):
            i = pl.program_id(0) * nb + b
            rows = slice(QBLOCK * b, QBLOCK * (b + 1))
            valid = _attn_valid(i)
            for kv_head, (k_ref, v_ref, dk_ref, dv_ref) in enumerate(
                    ((kd0_ref, vd0_ref, dk0_ref, dv0_ref), (kd1_ref, vd1_ref, dk1_ref, dv1_ref))):
                kband, prev, own = _band(k_ref, i)
                vband, _, _ = _band(v_ref, i)
                base = 256 * kv_head
                qm = _stack_heads(q_ref[rows, base:base + 128], q_ref[rows, base + 128:base + 256])
                dom = _stack_heads(do_ref[rows, base:base + 128], do_ref[rows, base + 128:base + 256])
                om = _stack_heads(o_ref[rows, base:base + 128], o_ref[rows, base + 128:base + 256])
                s = jnp.where(valid, _dot_nt(qm, kband), NEG_INF)
                p, e_sink, inv_l = _softmax_with_sink(s, _sink_column(sink_ref, kv_head))
                p = p * inv_l
                delta = jnp.sum(dom.astype(F32) * om.astype(F32), axis=-1, keepdims=True)
                ds = (p * (_dot_nt(dom, vband) - delta)).astype(BF16)
                sink_term = -(e_sink * inv_l) * delta
                for j in range(4):
                    part = jnp.sum(sink_term[QBLOCK * j:QBLOCK * (j + 1)], axis=0, keepdims=True)
                    dsink = dsink + jnp.where(lane == 4 * kv_head + j, part, 0.0)
                pair0, pair1 = _unstack_heads(_dot(ds, kband))
                dq_ref[rows, base:base + 128] = pair0.astype(BF16)
                dq_ref[rows, base + 128:base + 256] = pair1.astype(BF16)
                dkd = _dot_tn(ds, qm)
                dkd = dkd + pltpu.roll(dkd, HEAD_DIM, 1)
                dvd = _dot_tn(p.astype(BF16), dom)
                dvd = dvd + pltpu.roll(dvd, HEAD_DIM, 1)
                dk_ref[pl.ds(prev, QBLOCK), :] += dkd[0:QBLOCK]
                dk_ref[pl.ds(own, QBLOCK), :] += dkd[QBLOCK:]
                dv_ref[pl.ds(prev, QBLOCK), :] += dvd[0:QBLOCK]
                dv_ref[pl.ds(own, QBLOCK), :] += dvd[QBLOCK:]
        dsink_ref[...] += dsink

    blk = pl.BlockSpec((nb * QBLOCK, Q_WIDTH), lambda i: (i, 0))
    full = _resident((seq, 128))
    acc = pl.BlockSpec((seq, 128), lambda i: (0, 0))
    return _pallas(
        body, name="attention_bwd", grid=(seq // (nb * QBLOCK),),
        in_specs=[pl.BlockSpec(memory_space=pltpu.SMEM), blk, blk, blk, full, full, full, full],
        out_specs=[blk, acc, acc, acc, acc, pl.BlockSpec((1, 128), lambda i: (0, 0))],
        out_shape=[SDS((seq, Q_WIDTH), BF16)] + [SDS((seq, 128), F32)] * 4 + [SDS((1, 128), F32)],
        operands=(sinks, q, dattn, attn, kd0, kd1, vd0, vd1), comm=comm)


def _in_proj_bwd(dq, dk0, dk1, dv0, dv1, dgb, dy, gc, xin, conv_w, x, dh, g_pre, w_in_t, rope):
    seq = x.shape[0]
    tb = min(seq, WIDE_TOKEN_TILE)
    n_tiles = seq // tb

    def body(dq_ref, dk0_ref, dk1_ref, dv0_ref, dv1_ref, dgb_ref, dy_ref, dyh_ref, gc_ref, xin_ref, cw_ref,
             x_ref, dh_ref, g_ref, w_ref, c_ref, sa_ref, sb_ref,
             dproj_ref, gx_ref, dg_ref):
        i = pl.program_id(0)

        @pl.when(i == 0)
        def _():
            dg_ref[...] = jnp.zeros_like(dg_ref)

        dy = dy_ref[...].astype(F32)
        ext = jnp.concatenate([dy, jnp.where(i == n_tiles - 1, 0.0, dyh_ref[...].astype(F32))], axis=0)
        dy1 = pltpu.roll(ext, tb + HALO - 1, 0)[0:tb]
        dy2 = pltpu.roll(ext, tb + HALO - 2, 0)[0:tb]
        cw = cw_ref[...]
        du = cw[2:3, :] * dy + cw[1:2, :] * dy1 + cw[0:1, :] * dy2
        scale = 1.0 / math.sqrt(HEAD_DIM)
        base = Q_WIDTH + 2 * KV_WIDTH
        halves = [slice(0, tb // 2), slice(tb // 2, tb)]
        low = _lane_lt64((tb // 2, 128))
        for rows in halves:
            c, sa, sb = _rope_tile(c_ref.at[rows, :], sa_ref, sb_ref)
            for p in range(Q_WIDTH // 128):
                dproj_ref[rows, 128 * p:128 * (p + 1)] = _rope_transposed(
                    dq_ref[rows, 128 * p:128 * (p + 1)].astype(F32) * scale, c, sa, sb).astype(BF16)
            dk = jnp.where(low, dk0_ref[rows, :], dk1_ref[rows, :])
            dproj_ref[rows, Q_WIDTH:Q_WIDTH + KV_WIDTH] = _rope_transposed(dk, c, sa, sb).astype(BF16)
            dproj_ref[rows, Q_WIDTH + KV_WIDTH:base] = jnp.where(low, dv0_ref[rows, :], dv1_ref[rows, :]).astype(BF16)
            dproj_ref[rows, base:base + CONV_WIDTH] = dgb_ref[rows, :]
            dproj_ref[rows, base + CONV_WIDTH:base + 2 * CONV_WIDTH] = (du[rows] * xin_ref[rows, :].astype(F32)).astype(BF16)
            dproj_ref[rows, base + 2 * CONV_WIDTH:] = (du[rows] * gc_ref[rows, :].astype(F32)).astype(BF16)
        w_all = w_ref[...].reshape(IN_COLS, D_MODEL)
        dhn = [_dot(dproj_ref[rows, :], w_all) for rows in halves]
        dg = jnp.zeros((1, D_MODEL), F32)
        for k, rows in enumerate(halves):
            xv = x_ref[rows, :]
            r = _rms(xv)
            xhat = xv * r
            dg = dg + _colsum(dhn[k] * xhat)
            gx_ref[rows, :] = dh_ref[rows, :].astype(F32) + _norm_bwd(dhn[k], g_ref[...], xhat, r)
        dg_ref[...] += dg

    tile = lambda w: pl.BlockSpec((tb, w), lambda i: (i, 0))
    halo_next = pl.BlockSpec((HALO, CONV_WIDTH), lambda i: (jnp.minimum((i + 1) * (tb // HALO), seq // HALO - 1), 0))
    return _pallas(
        body, name="in_proj_bwd", grid=(n_tiles,),
        in_specs=[tile(Q_WIDTH), tile(128), tile(128), tile(128), tile(128), tile(CONV_WIDTH), tile(CONV_WIDTH), halo_next,
                  tile(CONV_WIDTH), tile(CONV_WIDTH), _resident((CONV_K, CONV_WIDTH)),
                  tile(D_MODEL), tile(D_MODEL), _resident((1, D_MODEL)), _resident(w_in_t.shape), *_rope_specs(tb)],
        out_specs=[tile(IN_COLS), tile(D_MODEL), pl.BlockSpec((1, D_MODEL), lambda i: (0, 0))],
        out_shape=[SDS((seq, IN_COLS), BF16), SDS((seq, D_MODEL), F32), SDS((1, D_MODEL), F32)],
        operands=(dq, dk0, dk1, dv0, dv1, dgb, dy, dy, gc, xin, conv_w, x, dh, g_pre, w_in_t, *rope))


def _wgrad_grid(seq, per_chip, h_rows, with_rider=False):
    chips_per_step = 1 if per_chip else N_CHIPS
    m = chips_per_step * 2 * h_rows
    bt = min(seq, WGRAD_TOKEN_TILE if per_chip and not with_rider else WGRAD_TOKEN_TILE // 2)
    return chips_per_step, m, bt, seq // bt


def _wgrad(name, a, b, *, per_chip, h_rows, square_a=False, comm=None, rider=None):
    seq = a.shape[0]
    chips_per_step, m, bt, n_k = _wgrad_grid(seq, per_chip, h_rows, rider is not None)
    a_cols = m if per_chip else a.shape[1]
    a_wide = a.shape[1] > a_cols
    b_wide = b.shape[1] > D_MODEL

    def body(a_ref, b_ref, g_ref):
        @pl.when(pl.program_id(1) == 0)
        def _():
            g_ref[...] = jnp.zeros_like(g_ref)

        av = a_ref[...]
        if square_a:
            av = (av.astype(F32) * av.astype(F32)).astype(BF16)
        g_ref[...] += _dot_tn(av, b_ref[...]).reshape(g_ref.shape)

    a_spec = pl.BlockSpec((bt, a_cols), (lambda j, k: (k, j)) if a_wide else (lambda j, k: (k, 0)))
    b_spec = pl.BlockSpec((bt, D_MODEL), (lambda j, k: (k, j)) if b_wide else (lambda j, k: (k, 0)))
    g_spec = pl.BlockSpec((chips_per_step, 2, h_rows, D_MODEL), lambda j, k: (j, 0, 0, 0),
                          pipeline_mode=None if per_chip else pl.Buffered(1))
    return _pallas(
        body, name=name, grid=(N_CHIPS if per_chip else 1, n_k),
        in_specs=[a_spec, b_spec], out_specs=[g_spec], out_shape=[SDS((N_CHIPS, 2, h_rows, D_MODEL), F32)],
        operands=(a, b), comm=comm, rider=rider)


def _adamw_math(w, g, m, v):
    m = ADAM_B1 * m + (1.0 - ADAM_B1) * g
    v = ADAM_B2 * v + (1.0 - ADAM_B2) * (g * g)
    m_hat = m / (1.0 - ADAM_B1 ** ADAM_STEP)
    v_hat = v / (1.0 - ADAM_B2 ** ADAM_STEP)
    delta = -ADAM_LR * (m_hat / (jnp.sqrt(v_hat) + ADAM_EPS) + ADAM_WD * w)
    return delta, m, v


def _adamw_rows(name, reduced, w, m, v, rt):
    per_half = reduced.shape[1] // rt

    def body(r_ref, w_ref, m_ref, v_ref, g_out, d_out, m_out, v_out):
        g = r_ref[0]
        g_out[...] = g
        d_out[...], m_out[...], v_out[...] = _adamw_math(w_ref[...], g, m_ref[...], v_ref[...])

    blk = pl.BlockSpec((rt, D_MODEL), lambda h, r: (h * per_half + r, 0))
    return _pallas(
        body, name=name, grid=(2, per_half),
        in_specs=[pl.BlockSpec((1, rt, D_MODEL), lambda h, r: (h, r, 0)), blk, blk, blk],
        out_specs=[blk, blk, blk, blk], out_shape=[SDS(w.shape, F32)] * 4, operands=(reduced, w, m, v))


def _adamw_small(packed_grads, w, m, v):
    names = SMALL_NAMES
    n = len(names)
    conv_local = w["conv_w"].shape[-1]

    def body(*refs):
        gp = refs[0]
        w_refs, m_refs, v_refs = refs[1:1 + n], refs[1 + n:1 + 2 * n], refs[1 + 2 * n:1 + 3 * n]
        outs = refs[1 + 3 * n:]
        g_out, d_out, m_out, v_out = outs[0:n], outs[n:2 * n], outs[2 * n:3 * n], outs[3 * n:4 * n]
        chip = 2 * lax.axis_index("x") + lax.axis_index("y")

        def step(k, g, index=None):
            pick = (lambda r: r[...]) if index is None else (lambda r: r[index])
            d, new_m, new_v = _adamw_math(pick(w_refs[k]), g, pick(m_refs[k]), pick(v_refs[k]))
            for ref, val in ((g_out[k], g), (d_out[k], d), (m_out[k], new_m), (v_out[k], new_v)):
                if index is None:
                    ref[...] = val
                else:
                    ref[index] = val

        for k, name in enumerate(names):
            if name in SMALL_VECTORS:
                step(k, gp[SMALL_VECTORS.index(name):SMALL_VECTORS.index(name) + 1, :])
            elif name == "attn_group_norm":
                step(k, gp[4:5, 0:Q_WIDTH])
            elif name == "conv_group_norm":
                step(k, gp[4:5, Q_WIDTH:])
            elif name == "attn_sinks":
                step(k, gp[7:8, 0:8])
            else:
                for t in range(CONV_K):
                    row, base = 5 + t // 2, CONV_WIDTH * (t % 2)
                    g = gp[row:row + 1, base:base + conv_local]
                    for j in range(1, CONV_WIDTH // conv_local):
                        g = jnp.where(chip == j, gp[row:row + 1, base + conv_local * j:base + conv_local * (j + 1)], g)
                    step(k, g, index=(0, slice(t, t + 1), slice(None)))

    shapes = [SDS(w[name].shape, F32) for name in names]
    res = pl.pallas_call(
        body, name="adamw_small", in_specs=[VMEM_WHOLE] * (1 + 3 * n), out_specs=[VMEM_WHOLE] * (4 * n),
        out_shape=shapes * 4,
    )(packed_grads, *[w[k] for k in names], *[m[k] for k in names], *[v[k] for k in names])
    return [dict(zip(names, res[i * n:(i + 1) * n])) for i in range(4)]


SMALL_VECTORS = ("pre_mix_norm", "post_mix_norm", "pre_mlp_norm", "post_mlp_norm")
SMALL_NAMES = SMALL_VECTORS + ("attn_group_norm", "conv_group_norm", "conv_w", "attn_sinks")


def _pack_small(p):
    rows = [p[n].reshape(1, D_MODEL) for n in SMALL_VECTORS]
    rows.append(jnp.concatenate([p["attn_group_norm"].reshape(1, -1), p["conv_group_norm"].reshape(1, -1)], axis=1))
    cw = p["conv_w"].reshape(CONV_K, -1)
    rows.append(jnp.pad(cw, ((0, 1), (0, CONV_WIDTH - cw.shape[1]))).reshape(2, D_MODEL))
    last = jnp.concatenate([p["attn_sinks"].reshape(1, 8), p.get("loss_sum", jnp.zeros((1, 1), F32))], axis=1)
    rows.append(jnp.pad(last, ((0, 0), (0, D_MODEL - 9))))
    return jnp.concatenate(rows, axis=0)


WEIGHT_ORDER = ("pre_mix_norm", "w_in", "conv_w", "attn_sinks", "attn_group_norm", "conv_group_norm", "w_out",
                "post_mix_norm", "pre_mlp_norm", "w_up", "w_down", "post_mlp_norm")


def kernel(x, pre_mix_norm, w_in, conv_w, attn_sinks, attn_group_norm, conv_group_norm, w_out, post_mix_norm, pre_mlp_norm, w_up, w_down, post_mlp_norm, loss_target, m_pre_mix_norm, m_w_in, m_conv_w, m_attn_sinks, m_attn_group_norm, m_conv_group_norm, m_w_out, m_post_mix_norm, m_pre_mlp_norm, m_w_up, m_w_down, m_post_mlp_norm, v_pre_mix_norm, v_w_in, v_conv_w, v_attn_sinks, v_attn_group_norm, v_conv_group_norm, v_w_out, v_post_mix_norm, v_pre_mlp_norm, v_w_up, v_w_down, v_post_mlp_norm):
    w = dict(pre_mix_norm=pre_mix_norm, w_in=w_in, conv_w=conv_w, attn_sinks=attn_sinks, attn_group_norm=attn_group_norm,
             conv_group_norm=conv_group_norm, w_out=w_out, post_mix_norm=post_mix_norm, pre_mlp_norm=pre_mlp_norm,
             w_up=w_up, w_down=w_down, post_mlp_norm=post_mlp_norm)
    m = dict(pre_mix_norm=m_pre_mix_norm, w_in=m_w_in, conv_w=m_conv_w, attn_sinks=m_attn_sinks,
             attn_group_norm=m_attn_group_norm, conv_group_norm=m_conv_group_norm, w_out=m_w_out,
             post_mix_norm=m_post_mix_norm, pre_mlp_norm=m_pre_mlp_norm, w_up=m_w_up, w_down=m_w_down,
             post_mlp_norm=m_post_mlp_norm)
    v = dict(pre_mix_norm=v_pre_mix_norm, w_in=v_w_in, conv_w=v_conv_w, attn_sinks=v_attn_sinks,
             attn_group_norm=v_attn_group_norm, conv_group_norm=v_conv_group_norm, w_out=v_w_out,
             post_mix_norm=v_post_mix_norm, pre_mlp_norm=v_pre_mlp_norm, w_up=v_w_up, w_down=v_w_down,
             post_mlp_norm=v_post_mlp_norm)
    core = lax.axis_index("c").astype(jnp.int32).reshape(1)
    xs, target = x[0], loss_target[0]
    rope = _rope_inputs(xs.shape[0])

    conv_pad = jnp.pad(conv_w[0], ((0, 8 - CONV_K), (0, 0)))
    wf_in, conv_all, hb_up, hb_down, hb_out = _gather_whole(w_in[0].T, (w_up[0], w_down[0], w_out[0]), conv_pad)
    conv_full = conv_all[:, :CONV_K, :].transpose(1, 0, 2).reshape(CONV_K, CONV_WIDTH)

    whole_up, early, late = (0, H_UP), (0, DOWN_EARLY_ROWS), (DOWN_EARLY_ROWS, H_DOWN - DOWN_EARLY_ROWS)
    *proj, wf_up, wf_out, wf_down = _in_proj(
        xs, pre_mix_norm, wf_in, rope,
        comm=_merge(_relay(hb_up, None, first=whole_up), _gather_first(hb_out), _relay(hb_down, None, first=early)))
    q, kd0, kd1, vd0, vd1, gb, gc, xin, hn = proj
    attn, wf_up, wf_out, wf_down = _attention_fwd(
        q, kd0, kd1, vd0, vd1, attn_sinks,
        comm=_merge(_relay(None, wf_up, second=whole_up), _gather_second(wf_out),
                    _relay(hb_down, wf_down, first=late, second=early)))
    mix, mixed, wf_up, wf_down = _mix_out(
        attn, gb, gc, xin, conv_full, attn_group_norm, conv_group_norm, wf_out,
        comm=_merge(_relay(None, wf_up, third=whole_up), _relay(None, wf_down, second=late, third=early, third_after=late)))
    up, hn2, dmlp, dup, dh, dmix, loss_sum, dg_post_mlp, dg_pre_mlp, dg_post_mix = _mlp_fwd_bwd(
        xs, mix, target, post_mix_norm, pre_mlp_norm, post_mlp_norm, wf_up, wf_down)

    n_k = _wgrad_grid(xs.shape[0], True, H_DOWN, with_rider=True)[3]
    g_down, dattn, dgb, dy, dg_attn, dg_conv, dconv_w = _wgrad(
        "wgrad_down", up, dmlp, per_chip=True, h_rows=H_DOWN, square_a=True,
        rider=_mix_bwd(dmix, attn, gb, gc, xin, conv_full, attn_group_norm, conv_group_norm, wf_out, n_k))
    g_up, got_down = _wgrad("wgrad_up", hn2, dup, per_chip=True, h_rows=H_UP, comm=_pair_send(g_down))
    p_down = _pair_sum("pair_sum_down", core, g_down, got_down)
    g_out, got_up = _wgrad("wgrad_out", mixed, dmix, per_chip=False, h_rows=H_OUT, comm=_pair_send(g_up))
    p_up = _pair_sum("pair_sum_up", core, g_up, got_up)
    dq, dk0, dk1, dv0, dv1, dsink, ex_down, ex_up, got_out = _attention_bwd(
        q, dattn, attn, kd0, kd1, vd0, vd1, attn_sinks,
        comm=_merge(_chip_exchange(p_down), _chip_exchange(p_up), _pair_send(g_out)))
    p_out = _pair_sum("pair_sum_out", core, g_out, got_out)
    dproj, grad_x, dg_pre_mix = _in_proj_bwd(dq, dk0, dk1, dv0, dv1, dgb, dy, gc, xin, conv_full, xs, dh, pre_mix_norm,
                                             wf_in, rope)
    g_in, ex_out = _wgrad("wgrad_in", dproj, hn, per_chip=False, h_rows=H_IN, comm=_chip_exchange(p_out))
    small = dict(pre_mix_norm=dg_pre_mix, conv_w=dconv_w, attn_sinks=dsink[:, :8], attn_group_norm=dg_attn,
                 conv_group_norm=dg_conv, post_mix_norm=dg_post_mix, pre_mlp_norm=dg_pre_mlp, post_mlp_norm=dg_post_mlp,
                 loss_sum=loss_sum)
    r_down, r_up, r_out, r_in, small_total = _tail_reduce(g_in, [ex_down, ex_up, ex_out], _pack_small(small))

    out_g, out_d, out_m, out_v = {}, {}, {}, {}
    out_g["w_up"], out_d["w_up"], out_m["w_up"], out_v["w_up"] = _adamw_rows(
        "adamw_up", r_up, w_up[0], m_w_up[0], v_w_up[0], 256)
    out_g["w_down"], out_d["w_down"], out_m["w_down"], out_v["w_down"] = _adamw_rows(
        "adamw_down", r_down, w_down[0], m_w_down[0], v_w_down[0], 256)
    out_g["w_out"], out_d["w_out"], out_m["w_out"], out_v["w_out"] = _adamw_rows(
        "adamw_out", r_out, w_out[0], m_w_out[0], v_w_out[0], H_OUT)
    in_t = _adamw_rows("adamw_in", r_in, w_in[0].T, m_w_in[0].T, v_w_in[0].T, H_IN)
    out_g["w_in"], out_d["w_in"], out_m["w_in"], out_v["w_in"] = [t.T for t in in_t]

    loss = small_total[7, 8] * (0.5 / D_MODEL)
    for out, part in zip((out_g, out_d, out_m, out_v), _adamw_small(small_total, w, m, v)):
        out.update(part)

    def shaped(d):
        return [d[n].reshape(w[n].shape) for n in WEIGHT_ORDER]

    return (loss, grad_x[None], *shaped(out_g), *shaped(out_d), *shaped(out_m), *shaped(out_v))
```

```python
import math
from typing import Callable, NamedTuple

import jax
import jax.numpy as jnp
import numpy as np
from jax import lax
from jax.experimental import pallas as pl
from jax.experimental.pallas import tpu as pltpu

F32 = jnp.float32
BF16 = jnp.bfloat16

D_MODEL = 1024
HEAD_DIM = 64
Q_WIDTH = 512
KV_WIDTH = 128
CONV_WIDTH = 512
CONV_K = 3
D_FF = 4096
IN_COLS = 2304
QBLOCK = 128
ROT_DIM = 16
ROPE_THETA = 500000.0
NORM_EPS = 1e-6
NEG_INF = -1e30
N_CHIPS = 4

ADAM_LR = 0.001
ADAM_B1 = 0.9
ADAM_B2 = 0.999
ADAM_EPS = 1e-08
ADAM_WD = 0.01
ADAM_STEP = 10

H_UP, H_DOWN, H_OUT, H_IN = 512, 512, 128, 288
DOWN_EARLY_ROWS = 224

TOKEN_TILE = 512
WIDE_TOKEN_TILE = 1024
ATTN_FWD_BLOCKS = 16
ATTN_BWD_BLOCKS = 2
WGRAD_TOKEN_TILE = 4096
VMEM_LIMIT_V7X = 60 * 1024 * 1024

MESH = pl.DeviceIdType.MESH
ANY = pl.BlockSpec(memory_space=pl.ANY)
VMEM_WHOLE = pl.BlockSpec(memory_space=pltpu.VMEM)
SDS = jax.ShapeDtypeStruct


def _resident(shape):
    zeros = (0,) * len(shape)
    return pl.BlockSpec(shape, lambda *_: zeros, pipeline_mode=pl.Buffered(1))


def _rms(v):
    return lax.rsqrt(jnp.mean(v * v, axis=-1, keepdims=True) + NORM_EPS)


def _norm_bwd(dy, gain, vhat, rstd):
    t = dy * gain
    return rstd * (t - vhat * jnp.mean(t * vhat, axis=-1, keepdims=True))


def _colsum(v):
    return jnp.sum(v, axis=0, keepdims=True)


def _dot_nt(a, b):
    return lax.dot_general(a, b, (((1,), (1,)), ((), ())), preferred_element_type=F32)


def _dot_tn(a, b):
    return lax.dot_general(a, b, (((0,), (0,)), ((), ())), preferred_element_type=F32)


def _dot(a, b):
    return jnp.dot(a, b, preferred_element_type=F32)


def _chip_block(w_ref, chip):
    both = w_ref[pl.ds(2 * chip, 2)]
    return both.reshape(2 * both.shape[1], both.shape[2])


def _lane_lt64(shape):
    return lax.broadcasted_iota(jnp.int32, shape, 1) < HEAD_DIM


class _Comm(NamedTuple):
    operands: tuple
    out_shapes: tuple
    aliases: dict
    n_remote: int
    n_local: int
    plan: Callable
    after: Callable = None


def _merge(*comms):
    operands, out_shapes, aliases, parts = [], [], {}, []
    n_remote = n_local = 0
    for cm in comms:
        parts.append((len(operands), len(out_shapes), n_remote, n_local, cm))
        for k, v in cm.aliases.items():
            aliases[len(operands) + k] = len(out_shapes) + v
        operands += cm.operands
        out_shapes += cm.out_shapes
        n_remote += cm.n_remote
        n_local += cm.n_local

    def run(which, ins, outs, send, recv, loc):
        sends, recvs, locs = [], [], []
        for i0, o0, r0, l0, cm in parts:
            stage = getattr(cm, which)
            if stage is not None:
                s, r, l = stage(ins[i0:i0 + len(cm.operands)], outs[o0:o0 + len(cm.out_shapes)],
                                lambda k, r0=r0: send(r0 + k), lambda k, r0=r0: recv(r0 + k), lambda k, l0=l0: loc(l0 + k))
                sends, recvs, locs = sends + s, recvs + r, locs + l
        return sends, recvs, locs

    def plan(*args):
        return run("plan", *args)

    def after(*args):
        return run("after", *args)

    return _Comm(tuple(operands), tuple(out_shapes), aliases, n_remote, n_local, plan,
                 after if any(cm.after is not None for cm in comms) else None)


def _sem_scratch(comm):
    return [pltpu.SemaphoreType.DMA((max(comm.n_remote, 1),)), pltpu.SemaphoreType.DMA((max(comm.n_remote, 1),)),
            pltpu.SemaphoreType.DMA((max(comm.n_local, 1),))]


class _Rider(NamedTuple):
    body: Callable
    in_specs: list
    out_specs: list
    out_shape: list
    operands: tuple


def _pallas(body, *, name, grid, in_specs, out_specs, out_shape, operands, scratch=(), comm=None, rider=None):
    params = pltpu.CompilerParams(dimension_semantics=("arbitrary",) * len(grid), vmem_limit_bytes=VMEM_LIMIT_V7X)
    if rider is not None:
        own_in, own_out, ride_in, ride_out = len(in_specs), len(out_specs), len(rider.in_specs), len(rider.out_specs)
        own_body = body

        def body(*refs):
            o0 = own_in + ride_in
            s0 = o0 + own_out + ride_out
            own_body(*refs[:own_in], *refs[o0:o0 + own_out], *refs[s0:])
            first = None
            for axis in range(len(grid)):
                at_start = pl.program_id(axis) == 0
                first = at_start if first is None else jnp.logical_and(first, at_start)
            rider.body(first, *refs[own_in:o0], *refs[o0 + own_out:s0])

        in_specs, out_specs = list(in_specs) + rider.in_specs, list(out_specs) + rider.out_specs
        out_shape, operands = list(out_shape) + rider.out_shape, tuple(operands) + tuple(rider.operands)
    if comm is None:
        return pl.pallas_call(body, name=name, grid=grid, in_specs=in_specs, out_specs=out_specs, out_shape=out_shape,
                              scratch_shapes=list(scratch), compiler_params=params)(*operands)
    n_in, n_out, n_scr = len(in_specs), len(out_specs), len(scratch)
    c_in, c_out = len(comm.operands), len(comm.out_shapes)

    def with_comm(*refs):
        ins, c_ins = refs[:n_in], refs[n_in:n_in + c_in]
        o0 = n_in + c_in
        outs, c_outs = refs[o0:o0 + n_out], refs[o0 + n_out:o0 + n_out + c_out]
        s0 = o0 + n_out + c_out
        scr = refs[s0:s0 + n_scr]
        send_sems, recv_sems, local_sems = refs[s0 + n_scr:]
        first = last = None
        for axis, size in enumerate(grid):
            at_start, at_end = pl.program_id(axis) == 0, pl.program_id(axis) == size - 1
            first = at_start if first is None else jnp.logical_and(first, at_start)
            last = at_end if last is None else jnp.logical_and(last, at_end)

        def copies():
            return comm.plan(c_ins, c_outs, lambda k: send_sems.at[k], lambda k: recv_sems.at[k],
                             lambda k: local_sems.at[k])

        @pl.when(first)
        def _():
            sends, _, locs = copies()
            for cp in sends + locs:
                cp.start()

        body(*ins, *outs, *scr)

        @pl.when(last)
        def _():
            sends, recvs, locs = copies()
            for cp in recvs:
                cp.wait_recv()
            for cp in sends:
                cp.wait_send()
            for cp in locs:
                cp.wait()
            if comm.after is not None:
                sends, recvs, _ = comm.after(c_ins, c_outs, lambda k: send_sems.at[k], lambda k: recv_sems.at[k],
                                             lambda k: local_sems.at[k])
                for cp in sends:
                    cp.start()
                for cp in recvs:
                    cp.wait_recv()
                for cp in sends:
                    cp.wait_send()

    return pl.pallas_call(
        with_comm, name=name, grid=grid,
        in_specs=list(in_specs) + [ANY] * c_in, out_specs=list(out_specs) + [ANY] * c_out,
        out_shape=list(out_shape) + list(comm.out_shapes),
        scratch_shapes=list(scratch) + _sem_scratch(comm),
        input_output_aliases={n_in + k: n_out + v for k, v in comm.aliases.items()},
        compiler_params=params)(*operands, *comm.operands)


def _place():
    return lax.axis_index("x"), lax.axis_index("y"), lax.axis_index("c")


def _other_chips(x, y):
    return [(1 - x, y), (x, 1 - y), (1 - x, 1 - y)]


def _slot(px, py, pc):
    return 4 * px + 2 * py + pc


def _remote(src, dst, send_sem, recv_sem, to):
    return pltpu.make_async_remote_copy(src_ref=src, dst_ref=dst, send_sem=send_sem, recv_sem=recv_sem,
                                        device_id=to, device_id_type=MESH)


def _gather_first(half_block):
    def plan(ins, outs, send, recv, loc):
        (blk,), (full,) = ins, outs
        x, y, c = _place()
        chips = _other_chips(x, y)
        mine = full.at[_slot(x, y, c)]
        sends = [_remote(blk, mine, send(0), recv(0), (x, y, 1 - c))]
        sends += [_remote(blk, mine, send(1 + j), recv(1 + j), (*chip, c)) for j, chip in enumerate(chips)]
        recvs = [_remote(blk, full.at[_slot(x, y, 1 - c)], send(0), recv(0), (x, y, 1 - c))]
        recvs += [_remote(blk, full.at[_slot(*chip, c)], send(1 + j), recv(1 + j), (*chip, c))
                  for j, chip in enumerate(chips)]
        return sends, recvs, [pltpu.make_async_copy(blk, mine, loc(0))]

    return _Comm((half_block,), (SDS((2 * N_CHIPS,) + half_block.shape, half_block.dtype),), {}, 4, 1, plan)


def _gather_second(partly_gathered):
    def plan(ins, outs, send, recv, loc):
        (src,), (full,) = ins, outs
        x, y, c = _place()
        chips = _other_chips(x, y)
        sends = [_remote(src.at[_slot(*chip, c)], full.at[_slot(*chip, c)], send(j), recv(j), (x, y, 1 - c))
                 for j, chip in enumerate(chips)]
        recvs = [_remote(src.at[_slot(*chip, 1 - c)], full.at[_slot(*chip, 1 - c)], send(j), recv(j), (x, y, 1 - c))
                 for j, chip in enumerate(chips)]
        return sends, recvs, []

    return _Comm((partly_gathered,), (SDS(partly_gathered.shape, partly_gathered.dtype),), {0: 0}, 3, 0, plan)


def _relay_pieces(full, rows, x, y, c):
    start, half = rows[0], rows[1] // 2
    upper, lower = pl.ds(start, half), pl.ds(start + half, half)
    diagonal = full.at[_slot(1 - x, 1 - y, c)]
    return [(full.at[_slot(1 - x, y, c), upper], diagonal.at[upper], (x, 1 - y, c)),
            (full.at[_slot(x, 1 - y, c), lower], diagonal.at[lower], (1 - x, y, c))]


def _relay(half_block, so_far, first=None, second=None, third=None, third_after=None):
    has_block, has_buffer = half_block is not None, so_far is not None
    shape = so_far.shape if has_buffer else (2 * N_CHIPS,) + half_block.shape
    dtype = so_far.dtype if has_buffer else half_block.dtype

    def third_leg(rows, k, ins, outs, send, recv):
        src, full = (ins[-1] if has_buffer else outs[0]), outs[0]
        x, y, c = _place()
        span, sibling = pl.ds(*rows), (x, y, 1 - c)
        here, there = _slot(1 - x, 1 - y, c), _slot(1 - x, 1 - y, 1 - c)
        return ([_remote(src.at[here, span], full.at[here, span], send(k), recv(k), sibling)],
                [_remote(src.at[there, span], full.at[there, span], send(k), recv(k), sibling)])

    def plan(ins, outs, send, recv, loc):
        src, full = (ins[-1] if has_buffer else outs[0]), outs[0]
        x, y, c = _place()
        sibling = (x, y, 1 - c)
        sends, recvs, locs = [], [], []
        if first is not None:
            span = pl.ds(*first)
            blk, mine = ins[0].at[span], full.at[_slot(x, y, c), span]
            for k, peer in enumerate([sibling, (1 - x, y, c), (x, 1 - y, c)]):
                sends.append(_remote(blk, mine, send(k), recv(k), peer))
                recvs.append(_remote(blk, full.at[_slot(*peer), span], send(k), recv(k), peer))
            locs.append(pltpu.make_async_copy(blk, mine, loc(0)))
        if second is not None:
            span = pl.ds(*second)
            for k, chip in enumerate([(1 - x, y), (x, 1 - y)]):
                sends.append(_remote(src.at[_slot(*chip, c), span], full.at[_slot(*chip, c), span], send(3 + k), recv(3 + k),
                                     sibling))
                recvs.append(_remote(src.at[_slot(*chip, 1 - c), span], full.at[_slot(*chip, 1 - c), span], send(3 + k),
                                     recv(3 + k), sibling))
            for k, (piece, lands, peer) in enumerate(_relay_pieces(full, second, x, y, c)):
                sends.append(_remote(piece, piece, send(5 + k), recv(5 + k), peer))
                recvs.append(_remote(lands, lands, send(5 + k), recv(5 + k), peer))
        if third is not None:
            s, r = third_leg(third, 7, ins, outs, send, recv)
            sends, recvs = sends + s, recvs + r
        return sends, recvs, locs

    def after(ins, outs, send, recv, loc):
        s, r = third_leg(third_after, 8, ins, outs, send, recv)
        return s, r, []

    operands = ((half_block,) if has_block else ()) + ((so_far,) if has_buffer else ())
    return _Comm(operands, (SDS(shape, dtype),), {len(operands) - 1: 0} if has_buffer else {}, 9, 1, plan,
                 after if third_after is not None else None)


def _gather_whole(first, others, small_block):
    shards = (first, *others)
    n = len(shards)
    hs = [s.shape[0] // 2 for s in shards]
    rows = hs[0]

    def body(*refs):
        src, small_ref = refs[:n], refs[n]
        out_ref, small_out_ref, half_out = refs[n + 1], refs[n + 2], refs[n + 3:2 * n + 2]
        stage, half = refs[2 * n + 2:3 * n + 2], refs[3 * n + 2:4 * n + 2]
        send_sems, recv_sems, local_sems = refs[4 * n + 2:]
        x, y, c = _place()
        me, sibling = (x, y, c), (x, y, 1 - c)
        neighbours, diagonal = [(1 - x, y), (x, 1 - y)], (1 - x, 1 - y)
        loads = [pltpu.make_async_copy(src[k].at[pl.ds(c * hs[k], hs[k])], stage[k], local_sems.at[2 + k]) for k in range(n)]
        loads[0].start()
        loads[0].wait()
        for cp in loads[1:]:
            cp.start()
        blk_ref = half[0]
        blk_ref[...] = stage[0][...].astype(BF16)

        def copy(k, block, to, src=None):
            return _remote(out_ref.at[_slot(*block)] if src is None else src, out_ref.at[_slot(*block)],
                           send_sems.at[k], recv_sems.at[k], to)

        def small_copy(k, chip, to):
            return _remote(small_ref, small_out_ref.at[2 * chip[0] + chip[1]], send_sems.at[8 + k], recv_sems.at[8 + k], to)

        mine = pltpu.make_async_copy(blk_ref, out_ref.at[_slot(*me)], local_sems.at[0])
        mine_small = pltpu.make_async_copy(small_ref, small_out_ref.at[2 * x + y], local_sems.at[1])
        mine.start()
        mine_small.start()
        started = [copy(0, me, sibling, src=blk_ref)]
        started += [copy(1 + k, me, (*chip, c), src=blk_ref) for k, chip in enumerate(neighbours)]
        started += [small_copy(k, (x, y), (*chip, c)) for k, chip in enumerate(neighbours + [diagonal])]
        for cp in started:
            cp.start()
        stores = []
        for k in range(1, n):
            loads[k].wait()
            half[k][...] = stage[k][...].astype(BF16)
            stores.append(pltpu.make_async_copy(half[k], half_out[k - 1], local_sems.at[2 + n + k]))
            stores[-1].start()
        pieces = _relay_pieces(out_ref, (0, rows), x, y, c)
        for k, chip in enumerate(neighbours):
            copy(1 + k, (*chip, c), me).wait_recv()
            piece, _, peer = pieces[k]
            started += [copy(3 + k, (*chip, c), sibling), _remote(piece, piece, send_sems.at[5 + k], recv_sems.at[5 + k], peer)]
            started[-2].start()
            started[-1].start()
        for k, (_, lands, peer) in enumerate(pieces):
            _remote(lands, lands, send_sems.at[5 + k], recv_sems.at[5 + k], peer).wait_recv()
        started.append(copy(7, (*diagonal, c), sibling))
        started[-1].start()
        copy(0, sibling, me).wait_recv()
        for k, chip in enumerate(neighbours):
            copy(3 + k, (*chip, 1 - c), me).wait_recv()
        copy(7, (*diagonal, 1 - c), me).wait_recv()
        for k, chip in enumerate(neighbours + [diagonal]):
            small_copy(k, chip, me).wait_recv()
        for cp in started:
            cp.wait_send()
        mine.wait()
        mine_small.wait()
        for cp in stores:
            cp.wait()

    return pl.pallas_call(
        body, name="gather_whole", in_specs=[ANY] * (n + 1), out_specs=[ANY] * (n + 1),
        out_shape=[SDS((2 * N_CHIPS, rows, D_MODEL), BF16), SDS((N_CHIPS,) + small_block.shape, small_block.dtype)]
                  + [SDS((h, D_MODEL), BF16) for h in hs[1:]],
        scratch_shapes=[pltpu.VMEM((h, D_MODEL), F32) for h in hs] + [pltpu.VMEM((h, D_MODEL), BF16) for h in hs]
                       + [pltpu.SemaphoreType.DMA((11,)), pltpu.SemaphoreType.DMA((11,)), pltpu.SemaphoreType.DMA((2 + 2 * n,))],
        compiler_params=pltpu.CompilerParams(vmem_limit_bytes=VMEM_LIMIT_V7X),
    )(*shards, small_block)


def _pair_send(grads):
    def plan(ins, outs, send, recv, loc):
        (g,), (got,) = ins, outs
        x, y, c = _place()
        copies = [_remote(g.at[j, 1 - c], got.at[j], send(j), recv(j), (x, y, 1 - c)) for j in range(N_CHIPS)]
        return copies, copies, []

    shape = (grads.shape[0],) + grads.shape[2:]
    return _Comm((grads,), (SDS(shape, grads.dtype),), {}, N_CHIPS, 0, plan)


def _chip_exchange(partial):
    def plan(ins, outs, send, recv, loc):
        (p,), (got,) = ins, outs
        x, y, c = _place()
        my_chip = 2 * x + y
        chips = _other_chips(x, y)
        sends = [_remote(p.at[2 * chip[0] + chip[1]], got.at[my_chip], send(j), recv(j), (*chip, c))
                 for j, chip in enumerate(chips)]
        recvs = [_remote(p.at[my_chip], got.at[2 * chip[0] + chip[1]], send(j), recv(j), (*chip, c))
                 for j, chip in enumerate(chips)]
        return sends, recvs, [pltpu.make_async_copy(p.at[my_chip], got.at[my_chip], loc(0))]

    return _Comm((partial,), (SDS(partial.shape, partial.dtype),), {}, 3, 1, plan)


def _pair_sum(name, core, grads, received):
    h = grads.shape[2]

    def body(core_ref, g_ref, r_ref, o_ref):
        o_ref[...] = (g_ref[0] + r_ref[...]).astype(BF16)

    return pl.pallas_call(
        body, name=name,
        grid_spec=pltpu.PrefetchScalarGridSpec(
            num_scalar_prefetch=1, grid=(N_CHIPS,),
            in_specs=[pl.BlockSpec((1, 1, h, D_MODEL), lambda j, core_ref: (j, core_ref[0], 0, 0)),
                      pl.BlockSpec((1, h, D_MODEL), lambda j, core_ref: (j, 0, 0))],
            out_specs=pl.BlockSpec((1, h, D_MODEL), lambda j, core_ref: (j, 0, 0))),
        out_shape=SDS((N_CHIPS, h, D_MODEL), BF16),
        compiler_params=pltpu.CompilerParams(dimension_semantics=("arbitrary",), vmem_limit_bytes=VMEM_LIMIT_V7X),
    )(core, grads, received)


SMALL_ROWS = 8


def _sum_blocks(ref):
    return (ref[0].astype(F32) + ref[1].astype(F32)) + (ref[2].astype(F32) + ref[3].astype(F32))


def _tail_reduce(last_grads, exchanged, small):
    n = len(exchanged)
    h = last_grads.shape[2]

    def body(*refs):
        g_ref, ex, small_ref = refs[0], refs[1:1 + n], refs[1 + n]
        o0 = 2 + n
        out, out_last, small_out = refs[o0:o0 + n], refs[o0 + n], refs[o0 + n + 1]
        s0 = o0 + n + 2
        halves, half_last = refs[s0:s0 + n], refs[s0 + n]
        own, got, part, exch, small_buf = refs[s0 + n + 1:s0 + n + 6]
        ex_buf = refs[s0 + n + 6:s0 + 2 * n + 6]
        pair_send, pair_recv, chip_send, chip_recv, share_send, share_recv, small_send, small_recv, local_sems = refs[s0 + 2 * n + 6:]
        x, y, c = _place()
        sibling = (x, y, 1 - c)
        my_chip, me = 2 * x + y, _slot(x, y, c)
        chips = _other_chips(x, y)[::-1]

        order = [2 * chip[0] + chip[1] for chip in chips] + [my_chip]
        to_sibling = [_remote(g_ref.at[j, 1 - c], got.at[j], pair_send.at[j], pair_recv.at[j], sibling) for j in order]
        load_own = [pltpu.make_async_copy(g_ref.at[j, c], own.at[j], local_sems.at[j]) for j in order]
        load_ex = [pltpu.make_async_copy(ex[k], ex_buf[k], local_sems.at[N_CHIPS + n + 1 + k]) for k in range(n)]
        for give, keep in zip(to_sibling, load_own):
            give.start()
            keep.start()
        for cp in load_ex:
            cp.start()

        small_buf[me] = small_ref[...]
        small_copies = []
        for mask in range(1, 8):
            peer = (x ^ (mask >> 2), y ^ ((mask >> 1) & 1), c ^ (mask & 1))
            small_copies.append(_remote(small_ref, small_buf.at[me], small_send.at[mask - 1], small_recv.at[mask - 1], peer))
        for cp in small_copies:
            cp.start()

        def share(k, half_ref, out_ref):
            keep = pltpu.make_async_copy(half_ref, out_ref.at[c], local_sems.at[N_CHIPS + k])
            give = _remote(half_ref, out_ref.at[c], share_send.at[k], share_recv.at[k], sibling)
            take = _remote(half_ref, out_ref.at[1 - c], share_send.at[k], share_recv.at[k], sibling)
            keep.start()
            give.start()
            return keep, give, take

        def pair_sum(block):
            _remote(g_ref.at[block, 1 - c], got.at[block], pair_send.at[block], pair_recv.at[block], sibling).wait_recv()
            pltpu.make_async_copy(g_ref.at[block, c], own.at[block], local_sems.at[block]).wait()
            part[block] = (own[block] + got[block]).astype(BF16)

        to_chips = []
        for j, chip in enumerate(chips):
            block = 2 * chip[0] + chip[1]
            pair_sum(block)
            to_chips.append(_remote(part.at[block], exch.at[my_chip], chip_send.at[j], chip_recv.at[j], (*chip, c)))
            to_chips[-1].start()
        pair_sum(my_chip)
        exch[my_chip] = part[my_chip]
        from_chips = [_remote(part.at[my_chip], exch.at[2 * chip[0] + chip[1]], chip_send.at[j], chip_recv.at[j], (*chip, c))
                      for j, chip in enumerate(chips)]

        shares = []
        for k in range(n):
            load_ex[k].wait()
            halves[k][...] = _sum_blocks(ex_buf[k])
            shares.append(share(k, halves[k], out[k]))

        for cp in small_copies:
            cp.wait_recv()
        total = small_buf[0]
        for d in range(1, 8):
            total = total + small_buf[d]
        small_out[...] = total

        for cp in from_chips:
            cp.wait_recv()
        half_last[...] = _sum_blocks(exch)
        shares.append(share(n, half_last, out_last))

        for keep, give, take in shares:
            take.wait_recv()
            give.wait_send()
            keep.wait()
        for cp in to_sibling + to_chips + small_copies:
            cp.wait_send()

    blocks = (N_CHIPS, h, D_MODEL)
    return pl.pallas_call(
        body, name="tail_reduce",
        in_specs=[ANY] * (n + 1) + [VMEM_WHOLE], out_specs=[ANY] * (n + 1) + [VMEM_WHOLE],
        out_shape=[SDS((2,) + e.shape[1:], F32) for e in exchanged] + [SDS((2, h, D_MODEL), F32), SDS(small.shape, F32)],
        scratch_shapes=[pltpu.VMEM(e.shape[1:], F32) for e in exchanged] + [pltpu.VMEM((h, D_MODEL), F32)]
                       + [pltpu.VMEM(blocks, F32), pltpu.VMEM(blocks, F32), pltpu.VMEM(blocks, BF16), pltpu.VMEM(blocks, BF16),
                          pltpu.VMEM((8,) + small.shape, F32)]
                       + [pltpu.VMEM(e.shape, BF16) for e in exchanged]
                       + [pltpu.SemaphoreType.DMA((N_CHIPS,)), pltpu.SemaphoreType.DMA((N_CHIPS,)),
                          pltpu.SemaphoreType.DMA((3,)), pltpu.SemaphoreType.DMA((3,)),
                          pltpu.SemaphoreType.DMA((n + 1,)), pltpu.SemaphoreType.DMA((n + 1,)),
                          pltpu.SemaphoreType.DMA((7,)), pltpu.SemaphoreType.DMA((7,)),
                          pltpu.SemaphoreType.DMA((N_CHIPS + 2 * n + 1,))],
        compiler_params=pltpu.CompilerParams(vmem_limit_bytes=VMEM_LIMIT_V7X),
    )(last_grads, *exchanged, small)


def _rope_expansion():
    half = ROT_DIM // 2
    expand = np.zeros((2 * half, 3 * 128), np.float32)
    const = np.zeros((1, 3 * 128), np.float32)
    for lane in range(128):
        d = lane % HEAD_DIM
        if d < ROT_DIM:
            expand[d % half, lane] = 1.0
        else:
            const[0, lane] = 1.0
        if d < half:
            expand[half + d, 128 + lane] = -1.0
        elif d < ROT_DIM:
            expand[half + d - half, 256 + lane] = 1.0
    return expand, const


ROPE_PIECES = 3 * ROT_DIM


def _rope_inputs(seq):
    pos = jnp.arange(seq, dtype=F32)
    inv_freq = ROPE_THETA ** (-jnp.arange(0, ROT_DIM, 2, dtype=F32) / ROT_DIM)
    ang = pos[:, None] * inv_freq[None, :]
    cs = jnp.concatenate([jnp.cos(ang), jnp.sin(ang)], axis=1)
    hi = lax.reduce_precision(cs, 8, 7)
    mid = lax.reduce_precision(cs - hi, 8, 7)
    low = cs - hi - mid
    expand, const = _rope_expansion()
    pieces = jnp.concatenate([hi, mid, low], axis=1).astype(BF16)
    return pieces, jnp.asarray(np.concatenate([expand] * 3, axis=0), BF16), jnp.asarray(const)


def _rope_specs(tb):
    return [pl.BlockSpec((tb, ROPE_PIECES), lambda i: (i, 0)), _resident((ROPE_PIECES, 3 * 128)), _resident((1, 3 * 128))]


def _rope_tile(pieces_ref, expand_ref, const_ref):
    tables = _dot(pieces_ref[...], expand_ref[...]) + const_ref[...]
    return tables[:, 0:128], tables[:, 128:256], tables[:, 256:384]


def _rope(t, c, sa, sb):
    half = ROT_DIM // 2
    return t * c + pltpu.roll(t, 128 - half, 1) * sa + pltpu.roll(t, half, 1) * sb


def _rope_transposed(dt, c, sa, sb):
    half = ROT_DIM // 2
    return dt * c + pltpu.roll(dt * sa, half, 1) + pltpu.roll(dt * sb, 128 - half, 1)


def _in_proj(x, g_pre, w_in_t, rope, comm=None):
    seq = x.shape[0]
    tb = min(seq, WIDE_TOKEN_TILE)

    def body(x_ref, g_ref, w_ref, c_ref, sa_ref, sb_ref,
             q_ref, kd0_ref, kd1_ref, vd0_ref, vd1_ref, gb_ref, gc_ref, xin_ref, hn_ref):
        xv = x_ref[...]
        hn = (xv * _rms(xv) * g_ref[...]).astype(BF16)
        hn_ref[...] = hn
        proj = _dot_nt(hn, w_ref[...].reshape(IN_COLS, D_MODEL))
        c, sa, sb = _rope_tile(c_ref, sa_ref, sb_ref)
        scale = 1.0 / math.sqrt(HEAD_DIM)
        for p in range(Q_WIDTH // 128):
            q_ref[:, 128 * p:128 * (p + 1)] = (_rope(proj[:, 128 * p:128 * (p + 1)], c, sa, sb) * scale).astype(BF16)
        k = _rope(proj[:, Q_WIDTH:Q_WIDTH + KV_WIDTH], c, sa, sb)
        v = proj[:, Q_WIDTH + KV_WIDTH:Q_WIDTH + 2 * KV_WIDTH]
        low = _lane_lt64(k.shape)
        k_sw, v_sw = pltpu.roll(k, HEAD_DIM, 1), pltpu.roll(v, HEAD_DIM, 1)
        kd0_ref[...] = jnp.where(low, k, k_sw).astype(BF16)
        kd1_ref[...] = jnp.where(low, k_sw, k).astype(BF16)
        vd0_ref[...] = jnp.where(low, v, v_sw).astype(BF16)
        vd1_ref[...] = jnp.where(low, v_sw, v).astype(BF16)
        base = Q_WIDTH + 2 * KV_WIDTH
        gb_ref[...] = proj[:, base:base + CONV_WIDTH].astype(BF16)
        gc_ref[...] = proj[:, base + CONV_WIDTH:base + 2 * CONV_WIDTH].astype(BF16)
        xin_ref[...] = proj[:, base + 2 * CONV_WIDTH:base + 3 * CONV_WIDTH].astype(BF16)

    tile = lambda w: pl.BlockSpec((tb, w), lambda i: (i, 0))
    return _pallas(
        body, name="in_proj", grid=(seq // tb,),
        in_specs=[tile(D_MODEL), _resident((1, D_MODEL)), _resident(w_in_t.shape), *_rope_specs(tb)],
        out_specs=[tile(Q_WIDTH), tile(128), tile(128), tile(128), tile(128),
                   tile(CONV_WIDTH), tile(CONV_WIDTH), tile(CONV_WIDTH), tile(D_MODEL)],
        out_shape=[SDS((seq, Q_WIDTH), BF16)] + [SDS((seq, 128), BF16)] * 4
                  + [SDS((seq, CONV_WIDTH), BF16)] * 3 + [SDS((seq, D_MODEL), BF16)],
        operands=(x, g_pre, w_in_t, *rope), comm=comm)


def _attn_valid(i):
    shape = (4 * QBLOCK, 2 * QBLOCK)
    row = lax.broadcasted_iota(jnp.int32, shape, 0)
    col = lax.broadcasted_iota(jnp.int32, shape, 1)
    qi = row & (QBLOCK - 1)
    return (col > qi) & (col <= qi + QBLOCK) & ((col >= QBLOCK) | (i > 0))


def _stack_heads(pair0, pair1):
    low = _lane_lt64(pair0.shape)
    zero = jnp.zeros_like(pair0)
    return jnp.concatenate([jnp.where(low, pair0, zero), jnp.where(low, zero, pair0),
                            jnp.where(low, pair1, zero), jnp.where(low, zero, pair1)], axis=0)


def _unstack_heads(stacked):
    low = _lane_lt64((QBLOCK, 128))
    pair0 = jnp.where(low, stacked[0:QBLOCK], stacked[QBLOCK:2 * QBLOCK])
    pair1 = jnp.where(low, stacked[2 * QBLOCK:3 * QBLOCK], stacked[3 * QBLOCK:4 * QBLOCK])
    return pair0, pair1


def _sink_column(sink_ref, kv_head):
    row = lax.broadcasted_iota(jnp.int32, (4 * QBLOCK, 1), 0)
    s = [sink_ref[0, 4 * kv_head + j] for j in range(4)]
    return jnp.where(row < QBLOCK, s[0], jnp.where(row < 2 * QBLOCK, s[1], jnp.where(row < 3 * QBLOCK, s[2], s[3])))


def _band(ref, i):
    prev = pl.multiple_of(jnp.maximum(i - 1, 0) * QBLOCK, QBLOCK)
    own = pl.multiple_of(i * QBLOCK, QBLOCK)
    return jnp.concatenate([ref[pl.ds(prev, QBLOCK), :], ref[pl.ds(own, QBLOCK), :]], axis=0), prev, own


def _softmax_with_sink(s, sink_col):
    m = jnp.maximum(jnp.max(s, axis=-1, keepdims=True), sink_col)
    p = jnp.exp(s - m)
    e_sink = jnp.exp(sink_col - m)
    inv_l = 1.0 / (jnp.sum(p, axis=-1, keepdims=True) + e_sink)
    return p, e_sink, inv_l


def _attention_fwd(q, kd0, kd1, vd0, vd1, sinks, comm=None):
    seq = q.shape[0]

    nb = ATTN_FWD_BLOCKS

    def body(sink_ref, q_ref, kd0_ref, kd1_ref, vd0_ref, vd1_ref, o_ref):
        for b in range(nb):
            i = pl.program_id(0) * nb + b
            rows = slice(QBLOCK * b, QBLOCK * (b + 1))
            valid = _attn_valid(i)
            for kv_head, (k_ref, v_ref) in enumerate(((kd0_ref, vd0_ref), (kd1_ref, vd1_ref))):
                kband, _, _ = _band(k_ref, i)
                vband, _, _ = _band(v_ref, i)
                base = 256 * kv_head
                qm = _stack_heads(q_ref[rows, base:base + 128], q_ref[rows, base + 128:base + 256])
                s = jnp.where(valid, _dot_nt(qm, kband), NEG_INF)
                p, _, inv_l = _softmax_with_sink(s, _sink_column(sink_ref, kv_head))
                o = _dot(p.astype(BF16), vband) * inv_l
                pair0, pair1 = _unstack_heads(o)
                o_ref[rows, base:base + 128] = pair0.astype(BF16)
                o_ref[rows, base + 128:base + 256] = pair1.astype(BF16)

    blk = pl.BlockSpec((nb * QBLOCK, Q_WIDTH), lambda i: (i, 0))
    full = _resident((seq, 128))
    return _pallas(
        body, name="attention_fwd", grid=(seq // (nb * QBLOCK),),
        in_specs=[pl.BlockSpec(memory_space=pltpu.SMEM), blk, full, full, full, full],
        out_specs=[blk], out_shape=[SDS((seq, Q_WIDTH), BF16)],
        operands=(sinks, q, kd0, kd1, vd0, vd1), comm=comm)


HALO = 16


def _conv_parts(gc, xin, gc_halo, xin_halo, conv_w, first):
    tb = gc.shape[0]
    u = gc.astype(F32) * xin.astype(F32)
    u_halo = jnp.where(first, 0.0, gc_halo.astype(F32) * xin_halo.astype(F32))
    ext = jnp.concatenate([u_halo, u], axis=0)
    u1 = pltpu.roll(ext, 1, 0)[HALO:HALO + tb]
    u2 = pltpu.roll(ext, 2, 0)[HALO:HALO + tb]
    y = conv_w[0:1, :] * u2 + conv_w[1:2, :] * u1 + conv_w[2:3, :] * u
    return u, u1, u2, y


def _halo_prev(tb, w):
    return pl.BlockSpec((HALO, w), lambda i: (jnp.maximum(i * (tb // HALO) - 1, 0), 0))


def _residual_mid(x, mix, g_post_mix):
    mix_f = mix.astype(F32)
    return x + mix_f * _rms(mix_f) * g_post_mix


def _mix_out(attn, gb, gc, xin, conv_w, g_attn, g_conv, w_out, comm=None):
    seq = attn.shape[0]
    tb = min(seq, WIDE_TOKEN_TILE)

    def body(a_ref, gb_ref, gc_ref, xin_ref, gch_ref, xinh_ref, cw_ref, ga_ref, gcn_ref, w_ref, mix_ref, mixed_ref):
        first = pl.program_id(0) == 0
        _, _, _, y = _conv_parts(gc_ref[...], xin_ref[...], gch_ref[...], xinh_ref[...], cw_ref[...], first)
        conv = gb_ref[...].astype(F32) * y
        a = a_ref[...].astype(F32)
        mixed_ref[:, 0:Q_WIDTH] = (a * _rms(a) * ga_ref[...]).astype(BF16)
        mixed_ref[:, Q_WIDTH:] = (conv * _rms(conv) * gcn_ref[...]).astype(BF16)
        mix_ref[...] = _dot(mixed_ref[...], w_ref[...].reshape(D_MODEL, D_MODEL)).astype(BF16)

    tile = lambda w: pl.BlockSpec((tb, w), lambda i: (i, 0))
    return _pallas(
        body, name="mix_out", grid=(seq // tb,),
        in_specs=[tile(Q_WIDTH), tile(CONV_WIDTH), tile(CONV_WIDTH), tile(CONV_WIDTH),
                  _halo_prev(tb, CONV_WIDTH), _halo_prev(tb, CONV_WIDTH),
                  _resident((CONV_K, CONV_WIDTH)), _resident((1, Q_WIDTH)), _resident((1, CONV_WIDTH)),
                  _resident(w_out.shape)],
        out_specs=[tile(D_MODEL), tile(D_MODEL)],
        out_shape=[SDS((seq, D_MODEL), BF16), SDS((seq, D_MODEL), BF16)],
        operands=(attn, gb, gc, xin, gc, xin, conv_w, g_attn, g_conv, w_out), comm=comm)


def _mlp_fwd_bwd(x, mix, target, g_post_mix, g_pre_mlp, g_post_mlp, w_up, w_down):
    seq = x.shape[0]
    tb = TOKEN_TILE

    def body(x_ref, mix_ref, t_ref, gpm_ref, g2_ref, g4_ref, wup_ref, wdown_ref,
             up_ref, hn2_ref, dmlp_ref, dup_ref, dh_ref, dmix_ref, loss_ref, dg4_ref, dg2_ref, dgpm_ref):
        @pl.when(pl.program_id(0) == 0)
        def _():
            for ref in (loss_ref, dg4_ref, dg2_ref, dgpm_ref):
                ref[...] = jnp.zeros_like(ref)

        halves = [slice(0, tb // 2), slice(tb // 2, tb)]
        chunks = [slice(1024 * j, 1024 * (j + 1)) for j in range(N_CHIPS)]
        hv, hn2, mlp, dout, dmlp, dhn2 = [], [], [], [], [], []
        for rows in halves:
            hv.append(_residual_mid(x_ref[rows, :], mix_ref[rows, :], gpm_ref[...]))
            hn2.append((hv[-1] * _rms(hv[-1]) * g2_ref[...]).astype(BF16))
            hn2_ref[rows, :] = hn2[-1]
        for k, rows in enumerate(halves):
            acc = None
            for j, cols in enumerate(chunks):
                up = jnp.maximum(_dot(hn2[k], _chip_block(wup_ref, j)), 0.0)
                up_ref[rows, cols] = up.astype(BF16)
                part = _dot((up * up).astype(BF16), _chip_block(wdown_ref, j))
                acc = part if acc is None else acc + part
            mlp.append(acc)
        loss = jnp.zeros((1, 1), F32)
        dg4 = jnp.zeros((1, D_MODEL), F32)
        for k, rows in enumerate(halves):
            rstd = _rms(mlp[k])
            zhat = mlp[k] * rstd
            diff = hv[k] + zhat * g4_ref[...] - t_ref[rows, :]
            loss = loss + jnp.sum(jnp.sum(diff * diff, axis=1, keepdims=True), axis=0, keepdims=True)
            dout.append(diff * (1.0 / D_MODEL))
            dg4 = dg4 + _colsum(dout[k] * zhat)
            dmlp.append(_norm_bwd(dout[k], g4_ref[...], zhat, rstd).astype(BF16))
            dmlp_ref[rows, :] = dmlp[k]
        for k, rows in enumerate(halves):
            acc = None
            for j, cols in enumerate(chunks):
                dact = _dot_nt(dmlp[k], _chip_block(wdown_ref, j))
                dup = (dact * (2.0 * up_ref[rows, cols].astype(F32))).astype(BF16)
                dup_ref[rows, cols] = dup
                part = _dot_nt(dup, _chip_block(wup_ref, j))
                acc = part if acc is None else acc + part
            dhn2.append(acc)
        dg2 = jnp.zeros((1, D_MODEL), F32)
        dgpm = jnp.zeros((1, D_MODEL), F32)
        for k, rows in enumerate(halves):
            r2 = _rms(hv[k])
            hhat = hv[k] * r2
            dg2 = dg2 + _colsum(dhn2[k] * hhat)
            dh = dout[k] + _norm_bwd(dhn2[k], g2_ref[...], hhat, r2)
            dh_ref[rows, :] = dh.astype(BF16)
            mix_v = mix_ref[rows, :].astype(F32)
            rz = _rms(mix_v)
            zhat = mix_v * rz
            dgpm = dgpm + _colsum(dh * zhat)
            dmix_ref[rows, :] = _norm_bwd(dh, gpm_ref[...], zhat, rz).astype(BF16)
        loss_ref[...] += loss
        dg4_ref[...] += dg4
        dg2_ref[...] += dg2
        dgpm_ref[...] += dgpm

    tile = lambda w: pl.BlockSpec((tb, w), lambda i: (i, 0))
    vec = pl.BlockSpec((1, D_MODEL), lambda i: (0, 0))
    return _pallas(
        body, name="mlp_fwd_bwd", grid=(seq // tb,),
        in_specs=[tile(D_MODEL), tile(D_MODEL), tile(D_MODEL), _resident((1, D_MODEL)), _resident((1, D_MODEL)),
                  _resident((1, D_MODEL)), _resident(w_up.shape), _resident(w_down.shape)],
        out_specs=[tile(D_FF), tile(D_MODEL), tile(D_MODEL), tile(D_FF), tile(D_MODEL), tile(D_MODEL),
                   pl.BlockSpec((1, 1), lambda i: (0, 0)), vec, vec, vec],
        out_shape=[SDS((seq, D_FF), BF16), SDS((seq, D_MODEL), BF16), SDS((seq, D_MODEL), BF16), SDS((seq, D_FF), BF16),
                   SDS((seq, D_MODEL), BF16), SDS((seq, D_MODEL), BF16),
                   SDS((1, 1), F32), SDS((1, D_MODEL), F32), SDS((1, D_MODEL), F32), SDS((1, D_MODEL), F32)],
        operands=(x, mix, target, g_post_mix, g_pre_mlp, g_post_mlp, w_up, w_down))


def _mix_bwd(dmix, attn, gb, gc, xin, conv_w, g_attn, g_conv, w_out, n_k):
    seq = attn.shape[0]
    tb = seq // (N_CHIPS * n_k)

    def body(first, dmix_ref, a_ref, gb_ref, gc_ref, xin_ref, gch_ref, xinh_ref, cw_ref, ga_ref, gcn_ref, w_ref,
             dattn_ref, dgb_ref, dy_ref, dga_ref, dgcn_ref, dcw_ref):
        @pl.when(first)
        def _():
            dga_ref[...] = jnp.zeros_like(dga_ref)
            dgcn_ref[...] = jnp.zeros_like(dgcn_ref)
            dcw_ref[...] = jnp.zeros_like(dcw_ref)

        dmixed = _dot_nt(dmix_ref[...], w_ref[...].reshape(D_MODEL, D_MODEL))
        a = a_ref[...].astype(F32)
        ra = _rms(a)
        ahat = a * ra
        dan = dmixed[:, 0:Q_WIDTH]
        dga_ref[...] += _colsum(dan * ahat)
        dattn_ref[...] = _norm_bwd(dan, ga_ref[...], ahat, ra).astype(BF16)
        gbv = gb_ref[...].astype(F32)
        u, u1, u2, y = _conv_parts(gc_ref[...], xin_ref[...], gch_ref[...], xinh_ref[...], cw_ref[...], first)
        conv = gbv * y
        rc = _rms(conv)
        chat = conv * rc
        dcn = dmixed[:, Q_WIDTH:]
        dgcn_ref[...] += _colsum(dcn * chat)
        dconv = _norm_bwd(dcn, gcn_ref[...], chat, rc)
        dgb_ref[...] = (dconv * y).astype(BF16)
        dy = dconv * gbv
        dy_ref[...] = dy.astype(BF16)
        dcw_ref[0:1, :] += _colsum(dy * u2)
        dcw_ref[1:2, :] += _colsum(dy * u1)
        dcw_ref[2:3, :] += _colsum(dy * u)

    tile = lambda w: pl.BlockSpec((tb, w), lambda j, k: (j * n_k + k, 0))
    halo = lambda w: pl.BlockSpec((HALO, w), lambda j, k: (jnp.maximum((j * n_k + k) * (tb // HALO) - 1, 0), 0))
    whole = lambda shape: pl.BlockSpec(shape, lambda j, k: (0,) * len(shape))
    return _Rider(
        body,
        in_specs=[tile(D_MODEL), tile(Q_WIDTH), tile(CONV_WIDTH), tile(CONV_WIDTH), tile(CONV_WIDTH),
                  halo(CONV_WIDTH), halo(CONV_WIDTH),
                  _resident((CONV_K, CONV_WIDTH)), _resident((1, Q_WIDTH)), _resident((1, CONV_WIDTH)),
                  _resident(w_out.shape)],
        out_specs=[tile(Q_WIDTH), tile(CONV_WIDTH), tile(CONV_WIDTH),
                   whole((1, Q_WIDTH)), whole((1, CONV_WIDTH)), whole((CONV_K, CONV_WIDTH))],
        out_shape=[SDS((seq, Q_WIDTH), BF16), SDS((seq, CONV_WIDTH), BF16), SDS((seq, CONV_WIDTH), BF16),
                   SDS((1, Q_WIDTH), F32), SDS((1, CONV_WIDTH), F32), SDS((CONV_K, CONV_WIDTH), F32)],
        operands=(dmix, attn, gb, gc, xin, gc, xin, conv_w, g_attn, g_conv, w_out))


def _attention_bwd(q, dattn, attn, kd0, kd1, vd0, vd1, sinks, comm=None):
    seq = q.shape[0]
    nb = ATTN_BWD_BLOCKS

    def body(sink_ref, q_ref, do_ref, o_ref, kd0_ref, kd1_ref, vd0_ref, vd1_ref,
             dq_ref, dk0_ref, dk1_ref, dv0_ref, dv1_ref, dsink_ref):
        @pl.when(pl.program_id(0) == 0)
        def _():
            for r in (dk0_ref, dk1_ref, dv0_ref, dv1_ref, dsink_ref):
                r[...] = jnp.zeros_like(r)

        lane = lax.broadcasted_iota(jnp.int32, (1, 128), 1)
        dsink = jnp.zeros((1, 128), F32)
        for b in range(nb):
            i = pl.program_id(0) * nb + b
            rows = slice(QBLOCK * b, QBLOCK * (b + 1))
            valid = _attn_valid(i)
            for kv_head, (k_ref, v_ref, dk_ref, dv_ref) in enumerate(
                    ((kd0_ref, vd0_ref, dk0_ref, dv0_ref), (kd1_ref, vd1_ref, dk1_ref, dv1_ref))):
                kband, prev, own = _band(k_ref, i)
                vband, _, _ = _band(v_ref, i)
                base = 256 * kv_head
                qm = _stack_heads(q_ref[rows, base:base + 128], q_ref[rows, base + 128:base + 256])
                dom = _stack_heads(do_ref[rows, base:base + 128], do_ref[rows, base + 128:base + 256])
                om = _stack_heads(o_ref[rows, base:base + 128], o_ref[rows, base + 128:base + 256])
                s = jnp.where(valid, _dot_nt(qm, kband), NEG_INF)
                p, e_sink, inv_l = _softmax_with_sink(s, _sink_column(sink_ref, kv_head))
                p = p * inv_l
                delta = jnp.sum(dom.astype(F32) * om.astype(F32), axis=-1, keepdims=True)
                ds = (p * (_dot_nt(dom, vband) - delta)).astype(BF16)
                sink_term = -(e_sink * inv_l) * delta
                for j in range(4):
                    part = jnp.sum(sink_term[QBLOCK * j:QBLOCK * (j + 1)], axis=0, keepdims=True)
                    dsink = dsink + jnp.where(lane == 4 * kv_head + j, part, 0.0)
                pair0, pair1 = _unstack_heads(_dot(ds, kband))
                dq_ref[rows, base:base + 128] = pair0.astype(BF16)
                dq_ref[rows, base + 128:base + 256] = pair1.astype(BF16)
                dkd = _dot_tn(ds, qm)
                dkd = dkd + pltpu.roll(dkd, HEAD_DIM, 1)
                dvd = _dot_tn(p.astype(BF16), dom)
                dvd = dvd + pltpu.roll(dvd, HEAD_DIM, 1)
                dk_ref[pl.ds(prev, QBLOCK), :] += dkd[0:QBLOCK]
                dk_ref[pl.ds(own, QBLOCK), :] += dkd[QBLOCK:]
                dv_ref[pl.ds(prev, QBLOCK), :] += dvd[0:QBLOCK]
                dv_ref[pl.ds(own, QBLOCK), :] += dvd[QBLOCK:]
        dsink_ref[...] += dsink

    blk = pl.BlockSpec((nb * QBLOCK, Q_WIDTH), lambda i: (i, 0))
    full = _resident((seq, 128))
    acc = pl.BlockSpec((seq, 128), lambda i: (0, 0))
    return _pallas(
        body, name="attention_bwd", grid=(seq // (nb * QBLOCK),),
        in_specs=[pl.BlockSpec(memory_space=pltpu.SMEM), blk, blk, blk, full, full, full, full],
        out_specs=[blk, acc, acc, acc, acc, pl.BlockSpec((1, 128), lambda i: (0, 0))],
        out_shape=[SDS((seq, Q_WIDTH), BF16)] + [SDS((seq, 128), F32)] * 4 + [SDS((1, 128), F32)],
        operands=(sinks, q, dattn, attn, kd0, kd1, vd0, vd1), comm=comm)


def _in_proj_bwd(dq, dk0, dk1, dv0, dv1, dgb, dy, gc, xin, conv_w, x, dh, g_pre, w_in_t, rope):
    seq = x.shape[0]
    tb = min(seq, WIDE_TOKEN_TILE)
    n_tiles = seq // tb

    def body(dq_ref, dk0_ref, dk1_ref, dv0_ref, dv1_ref, dgb_ref, dy_ref, dyh_ref, gc_ref, xin_ref, cw_ref,
             x_ref, dh_ref, g_ref, w_ref, c_ref, sa_ref, sb_ref,
             dproj_ref, gx_ref, dg_ref):
        i = pl.program_id(0)

        @pl.when(i == 0)
        def _():
            dg_ref[...] = jnp.zeros_like(dg_ref)

        dy = dy_ref[...].astype(F32)
        ext = jnp.concatenate([dy, jnp.where(i == n_tiles - 1, 0.0, dyh_ref[...].astype(F32))], axis=0)
        dy1 = pltpu.roll(ext, tb + HALO - 1, 0)[0:tb]
        dy2 = pltpu.roll(ext, tb + HALO - 2, 0)[0:tb]
        cw = cw_ref[...]
        du = cw[2:3, :] * dy + cw[1:2, :] * dy1 + cw[0:1, :] * dy2
        scale = 1.0 / math.sqrt(HEAD_DIM)
        base = Q_WIDTH + 2 * KV_WIDTH
        halves = [slice(0, tb // 2), slice(tb // 2, tb)]
        low = _lane_lt64((tb // 2, 128))
        for rows in halves:
            c, sa, sb = _rope_tile(c_ref.at[rows, :], sa_ref, sb_ref)
            for p in range(Q_WIDTH // 128):
                dproj_ref[rows, 128 * p:128 * (p + 1)] = _rope_transposed(
                    dq_ref[rows, 128 * p:128 * (p + 1)].astype(F32) * scale, c, sa, sb).astype(BF16)
            dk = jnp.where(low, dk0_ref[rows, :], dk1_ref[rows, :])
            dproj_ref[rows, Q_WIDTH:Q_WIDTH + KV_WIDTH] = _rope_transposed(dk, c, sa, sb).astype(BF16)
            dproj_ref[rows, Q_WIDTH + KV_WIDTH:base] = jnp.where(low, dv0_ref[rows, :], dv1_ref[rows, :]).astype(BF16)
            dproj_ref[rows, base:base + CONV_WIDTH] = dgb_ref[rows, :]
            dproj_ref[rows, base + CONV_WIDTH:base + 2 * CONV_WIDTH] = (du[rows] * xin_ref[rows, :].astype(F32)).astype(BF16)
            dproj_ref[rows, base + 2 * CONV_WIDTH:] = (du[rows] * gc_ref[rows, :].astype(F32)).astype(BF16)
        w_all = w_ref[...].reshape(IN_COLS, D_MODEL)
        dhn = [_dot(dproj_ref[rows, :], w_all) for rows in halves]
        dg = jnp.zeros((1, D_MODEL), F32)
        for k, rows in enumerate(halves):
            xv = x_ref[rows, :]
            r = _rms(xv)
            xhat = xv * r
            dg = dg + _colsum(dhn[k] * xhat)
            gx_ref[rows, :] = dh_ref[rows, :].astype(F32) + _norm_bwd(dhn[k], g_ref[...], xhat, r)
        dg_ref[...] += dg

    tile = lambda w: pl.BlockSpec((tb, w), lambda i: (i, 0))
    halo_next = pl.BlockSpec((HALO, CONV_WIDTH), lambda i: (jnp.minimum((i + 1) * (tb // HALO), seq // HALO - 1), 0))
    return _pallas(
        body, name="in_proj_bwd", grid=(n_tiles,),
        in_specs=[tile(Q_WIDTH), tile(128), tile(128), tile(128), tile(128), tile(CONV_WIDTH), tile(CONV_WIDTH), halo_next,
                  tile(CONV_WIDTH), tile(CONV_WIDTH), _resident((CONV_K, CONV_WIDTH)),
                  tile(D_MODEL), tile(D_MODEL), _resident((1, D_MODEL)), _resident(w_in_t.shape), *_rope_specs(tb)],
        out_specs=[tile(IN_COLS), tile(D_MODEL), pl.BlockSpec((1, D_MODEL), lambda i: (0, 0))],
        out_shape=[SDS((seq, IN_COLS), BF16), SDS((seq, D_MODEL), F32), SDS((1, D_MODEL), F32)],
        operands=(dq, dk0, dk1, dv0, dv1, dgb, dy, dy, gc, xin, conv_w, x, dh, g_pre, w_in_t, *rope))


def _wgrad_grid(seq, per_chip, h_rows, with_rider=False):
    chips_per_step = 1 if per_chip else N_CHIPS
    m = chips_per_step * 2 * h_rows
    bt = min(seq, WGRAD_TOKEN_TILE if per_chip and not with_rider else WGRAD_TOKEN_TILE // 2)
    return chips_per_step, m, bt, seq // bt


def _wgrad(name, a, b, *, per_chip, h_rows, square_a=False, comm=None, rider=None):
    seq = a.shape[0]
    chips_per_step, m, bt, n_k = _wgrad_grid(seq, per_chip, h_rows, rider is not None)
    a_cols = m if per_chip else a.shape[1]
    a_wide = a.shape[1] > a_cols
    b_wide = b.shape[1] > D_MODEL

    def body(a_ref, b_ref, g_ref):
        @pl.when(pl.program_id(1) == 0)
        def _():
            g_ref[...] = jnp.zeros_like(g_ref)

        av = a_ref[...]
        if square_a:
            av = (av.astype(F32) * av.astype(F32)).astype(BF16)
        g_ref[...] += _dot_tn(av, b_ref[...]).reshape(g_ref.shape)

    a_spec = pl.BlockSpec((bt, a_cols), (lambda j, k: (k, j)) if a_wide else (lambda j, k: (k, 0)))
    b_spec = pl.BlockSpec((bt, D_MODEL), (lambda j, k: (k, j)) if b_wide else (lambda j, k: (k, 0)))
    g_spec = pl.BlockSpec((chips_per_step, 2, h_rows, D_MODEL), lambda j, k: (j, 0, 0, 0),
                          pipeline_mode=None if per_chip else pl.Buffered(1))
    return _pallas(
        body, name=name, grid=(N_CHIPS if per_chip else 1, n_k),
        in_specs=[a_spec, b_spec], out_specs=[g_spec], out_shape=[SDS((N_CHIPS, 2, h_rows, D_MODEL), F32)],
        operands=(a, b), comm=comm, rider=rider)


def _adamw_math(w, g, m, v):
    m = ADAM_B1 * m + (1.0 - ADAM_B1) * g
    v = ADAM_B2 * v + (1.0 - ADAM_B2) * (g * g)
    m_hat = m / (1.0 - ADAM_B1 ** ADAM_STEP)
    v_hat = v / (1.0 - ADAM_B2 ** ADAM_STEP)
    delta = -ADAM_LR * (m_hat / (jnp.sqrt(v_hat) + ADAM_EPS) + ADAM_WD * w)
    return delta, m, v


ADAMW_STEPS_PER_HALF = 2


def _adamw_rows(items):
    n = len(items)
    per_half = ADAMW_STEPS_PER_HALF

    def body(*refs):
        for k in range(n):
            r_ref, w_ref, m_ref, v_ref = refs[4 * k:4 * k + 4]
            g_out, d_out, m_out, v_out = refs[4 * (n + k):4 * (n + k) + 4]
            g = r_ref[0]
            g_out[...] = g
            d_out[...], m_out[...], v_out[...] = _adamw_math(w_ref[...], g, m_ref[...], v_ref[...])

    in_specs, out_specs, out_shape, operands = [], [], [], []
    for reduced, w, m, v in items:
        rt = reduced.shape[1] // per_half
        blk = pl.BlockSpec((rt, D_MODEL), lambda h, r: (h * per_half + r, 0))
        in_specs += [pl.BlockSpec((1, rt, D_MODEL), lambda h, r: (h, r, 0)), blk, blk, blk]
        out_specs += [blk] * 4
        out_shape += [SDS(w.shape, F32)] * 4
        operands += [reduced, w, m, v]
    res = _pallas(body, name="adamw_rows", grid=(2, per_half), in_specs=in_specs, out_specs=out_specs,
                  out_shape=out_shape, operands=tuple(operands))
    return [res[4 * k:4 * k + 4] for k in range(n)]


def _adamw_small(packed_grads, w, m, v):
    names = SMALL_NAMES
    n = len(names)
    conv_local = w["conv_w"].shape[-1]

    def body(*refs):
        gp = refs[0]
        w_refs, m_refs, v_refs = refs[1:1 + n], refs[1 + n:1 + 2 * n], refs[1 + 2 * n:1 + 3 * n]
        outs = refs[1 + 3 * n:]
        g_out, d_out, m_out, v_out = outs[0:n], outs[n:2 * n], outs[2 * n:3 * n], outs[3 * n:4 * n]
        chip = 2 * lax.axis_index("x") + lax.axis_index("y")

        def step(k, g, index=None):
            pick = (lambda r: r[...]) if index is None else (lambda r: r[index])
            d, new_m, new_v = _adamw_math(pick(w_refs[k]), g, pick(m_refs[k]), pick(v_refs[k]))
            for ref, val in ((g_out[k], g), (d_out[k], d), (m_out[k], new_m), (v_out[k], new_v)):
                if index is None:
                    ref[...] = val
                else:
                    ref[index] = val

        for k, name in enumerate(names):
            if name in SMALL_VECTORS:
                step(k, gp[SMALL_VECTORS.index(name):SMALL_VECTORS.index(name) + 1, :])
            elif name == "attn_group_norm":
                step(k, gp[4:5, 0:Q_WIDTH])
            elif name == "conv_group_norm":
                step(k, gp[4:5, Q_WIDTH:])
            elif name == "attn_sinks":
                step(k, gp[7:8, 0:8])
            else:
                for t in range(CONV_K):
                    row, base = 5 + t // 2, CONV_WIDTH * (t % 2)
                    g = gp[row:row + 1, base:base + conv_local]
                    for j in range(1, CONV_WIDTH // conv_local):
                        g = jnp.where(chip == j, gp[row:row + 1, base + conv_local * j:base + conv_local * (j + 1)], g)
                    step(k, g, index=(0, slice(t, t + 1), slice(None)))

    shapes = [SDS(w[name].shape, F32) for name in names]
    res = pl.pallas_call(
        body, name="adamw_small", in_specs=[VMEM_WHOLE] * (1 + 3 * n), out_specs=[VMEM_WHOLE] * (4 * n),
        out_shape=shapes * 4,
    )(packed_grads, *[w[k] for k in names], *[m[k] for k in names], *[v[k] for k in names])
    return [dict(zip(names, res[i * n:(i + 1) * n])) for i in range(4)]


SMALL_VECTORS = ("pre_mix_norm", "post_mix_norm", "pre_mlp_norm", "post_mlp_norm")
SMALL_NAMES = SMALL_VECTORS + ("attn_group_norm", "conv_group_norm", "conv_w", "attn_sinks")


def _pack_small(p):
    rows = [p[n].reshape(1, D_MODEL) for n in SMALL_VECTORS]
    rows.append(jnp.concatenate([p["attn_group_norm"].reshape(1, -1), p["conv_group_norm"].reshape(1, -1)], axis=1))
    cw = p["conv_w"].reshape(CONV_K, -1)
    rows.append(jnp.pad(cw, ((0, 1), (0, CONV_WIDTH - cw.shape[1]))).reshape(2, D_MODEL))
    last = jnp.concatenate([p["attn_sinks"].reshape(1, 8), p.get("loss_sum", jnp.zeros((1, 1), F32))], axis=1)
    rows.append(jnp.pad(last, ((0, 0), (0, D_MODEL - 9))))
    return jnp.concatenate(rows, axis=0)


WEIGHT_ORDER = ("pre_mix_norm", "w_in", "conv_w", "attn_sinks", "attn_group_norm", "conv_group_norm", "w_out",
                "post_mix_norm", "pre_mlp_norm", "w_up", "w_down", "post_mlp_norm")


def kernel(x, pre_mix_norm, w_in, conv_w, attn_sinks, attn_group_norm, conv_group_norm, w_out, post_mix_norm, pre_mlp_norm, w_up, w_down, post_mlp_norm, loss_target, m_pre_mix_norm, m_w_in, m_conv_w, m_attn_sinks, m_attn_group_norm, m_conv_group_norm, m_w_out, m_post_mix_norm, m_pre_mlp_norm, m_w_up, m_w_down, m_post_mlp_norm, v_pre_mix_norm, v_w_in, v_conv_w, v_attn_sinks, v_attn_group_norm, v_conv_group_norm, v_w_out, v_post_mix_norm, v_pre_mlp_norm, v_w_up, v_w_down, v_post_mlp_norm):
    w = dict(pre_mix_norm=pre_mix_norm, w_in=w_in, conv_w=conv_w, attn_sinks=attn_sinks, attn_group_norm=attn_group_norm,
             conv_group_norm=conv_group_norm, w_out=w_out, post_mix_norm=post_mix_norm, pre_mlp_norm=pre_mlp_norm,
             w_up=w_up, w_down=w_down, post_mlp_norm=post_mlp_norm)
    m = dict(pre_mix_norm=m_pre_mix_norm, w_in=m_w_in, conv_w=m_conv_w, attn_sinks=m_attn_sinks,
             attn_group_norm=m_attn_group_norm, conv_group_norm=m_conv_group_norm, w_out=m_w_out,
             post_mix_norm=m_post_mix_norm, pre_mlp_norm=m_pre_mlp_norm, w_up=m_w_up, w_down=m_w_down,
             post_mlp_norm=m_post_mlp_norm)
    v = dict(pre_mix_norm=v_pre_mix_norm, w_in=v_w_in, conv_w=v_conv_w, attn_sinks=v_attn_sinks,
             attn_group_norm=v_attn_group_norm, conv_group_norm=v_conv_group_norm, w_out=v_w_out,
             post_mix_norm=v_post_mix_norm, pre_mlp_norm=v_pre_mlp_norm, w_up=v_w_up, w_down=v_w_down,
             post_mlp_norm=v_post_mlp_norm)
    core = lax.axis_index("c").astype(jnp.int32).reshape(1)
    xs, target = x[0], loss_target[0]
    rope = _rope_inputs(xs.shape[0])

    conv_pad = jnp.pad(conv_w[0], ((0, 8 - CONV_K), (0, 0)))
    wf_in, conv_all, hb_up, hb_down, hb_out = _gather_whole(w_in[0].T, (w_up[0], w_down[0], w_out[0]), conv_pad)
    conv_full = conv_all[:, :CONV_K, :].transpose(1, 0, 2).reshape(CONV_K, CONV_WIDTH)

    whole_up, early, late = (0, H_UP), (0, DOWN_EARLY_ROWS), (DOWN_EARLY_ROWS, H_DOWN - DOWN_EARLY_ROWS)
    *proj, wf_up, wf_out, wf_down = _in_proj(
        xs, pre_mix_norm, wf_in, rope,
        comm=_merge(_relay(hb_up, None, first=whole_up), _gather_first(hb_out), _relay(hb_down, None, first=early)))
    q, kd0, kd1, vd0, vd1, gb, gc, xin, hn = proj
    attn, wf_up, wf_out, wf_down = _attention_fwd(
        q, kd0, kd1, vd0, vd1, attn_sinks,
        comm=_merge(_relay(None, wf_up, second=whole_up), _gather_second(wf_out),
                    _relay(hb_down, wf_down, first=late, second=early)))
    mix, mixed, wf_up, wf_down = _mix_out(
        attn, gb, gc, xin, conv_full, attn_group_norm, conv_group_norm, wf_out,
        comm=_merge(_relay(None, wf_up, third=whole_up), _relay(None, wf_down, second=late, third=early, third_after=late)))
    up, hn2, dmlp, dup, dh, dmix, loss_sum, dg_post_mlp, dg_pre_mlp, dg_post_mix = _mlp_fwd_bwd(
        xs, mix, target, post_mix_norm, pre_mlp_norm, post_mlp_norm, wf_up, wf_down)

    n_k = _wgrad_grid(xs.shape[0], True, H_DOWN, with_rider=True)[3]
    g_down, dattn, dgb, dy, dg_attn, dg_conv, dconv_w = _wgrad(
        "wgrad_down", up, dmlp, per_chip=True, h_rows=H_DOWN, square_a=True,
        rider=_mix_bwd(dmix, attn, gb, gc, xin, conv_full, attn_group_norm, conv_group_norm, wf_out, n_k))
    g_up, got_down = _wgrad("wgrad_up", hn2, dup, per_chip=True, h_rows=H_UP, comm=_pair_send(g_down))
    p_down = _pair_sum("pair_sum_down", core, g_down, got_down)
    g_out, got_up = _wgrad("wgrad_out", mixed, dmix, per_chip=False, h_rows=H_OUT, comm=_pair_send(g_up))
    p_up = _pair_sum("pair_sum_up", core, g_up, got_up)
    dq, dk0, dk1, dv0, dv1, dsink, ex_down, ex_up, got_out = _attention_bwd(
        q, dattn, attn, kd0, kd1, vd0, vd1, attn_sinks,
        comm=_merge(_chip_exchange(p_down), _chip_exchange(p_up), _pair_send(g_out)))
    p_out = _pair_sum("pair_sum_out", core, g_out, got_out)
    dproj, grad_x, dg_pre_mix = _in_proj_bwd(dq, dk0, dk1, dv0, dv1, dgb, dy, gc, xin, conv_full, xs, dh, pre_mix_norm,
                                             wf_in, rope)
    g_in, ex_out = _wgrad("wgrad_in", dproj, hn, per_chip=False, h_rows=H_IN, comm=_chip_exchange(p_out))
    small = dict(pre_mix_norm=dg_pre_mix, conv_w=dconv_w, attn_sinks=dsink[:, :8], attn_group_norm=dg_attn,
                 conv_group_norm=dg_conv, post_mix_norm=dg_post_mix, pre_mlp_norm=dg_pre_mlp, post_mlp_norm=dg_post_mlp,
                 loss_sum=loss_sum)
    r_down, r_up, r_out, r_in, small_total = _tail_reduce(g_in, [ex_down, ex_up, ex_out], _pack_small(small))

    out_g, out_d, out_m, out_v = {}, {}, {}, {}
    res_up, res_down, res_out, res_in_t = _adamw_rows([
        (r_up, w_up[0], m_w_up[0], v_w_up[0]), (r_down, w_down[0], m_w_down[0], v_w_down[0]),
        (r_out, w_out[0], m_w_out[0], v_w_out[0]), (r_in, w_in[0].T, m_w_in[0].T, v_w_in[0].T)])
    for name, res in (("w_up", res_up), ("w_down", res_down), ("w_out", res_out), ("w_in", [t.T for t in res_in_t])):
        out_g[name], out_d[name], out_m[name], out_v[name] = res

    loss = small_total[7, 8] * (0.5 / D_MODEL)
    for out, part in zip((out_g, out_d, out_m, out_v), _adamw_small(small_total, w, m, v)):
        out.update(part)

    def shaped(d):
        return [d[n].reshape(w[n].shape) for n in WEIGHT_ORDER]

    return (loss, grad_x[None], *shaped(out_g), *shaped(out_d), *shaped(out_m), *shaped(out_v))
```

```python
import math
from typing import Callable, NamedTuple

import jax
import jax.numpy as jnp
import numpy as np
from jax import lax
from jax.experimental import pallas as pl
from jax.experimental.pallas import tpu as pltpu

F32 = jnp.float32
BF16 = jnp.bfloat16

D_MODEL = 1024
HEAD_DIM = 64
Q_WIDTH = 512
KV_WIDTH = 128
CONV_WIDTH = 512
CONV_K = 3
D_FF = 4096
IN_COLS = 2304
QBLOCK = 128
ROT_DIM = 16
ROPE_THETA = 500000.0
NORM_EPS = 1e-6
NEG_INF = -1e30
N_CHIPS = 4

ADAM_LR = 0.001
ADAM_B1 = 0.9
ADAM_B2 = 0.999
ADAM_EPS = 1e-08
ADAM_WD = 0.01
ADAM_STEP = 10

H_UP, H_DOWN, H_OUT, H_IN = 512, 512, 128, 288
DOWN_EARLY_ROWS = 224

TOKEN_TILE = 512
WIDE_TOKEN_TILE = 1024
ATTN_FWD_BLOCKS = 16
ATTN_BWD_BLOCKS = 2
WGRAD_TOKEN_TILE = 4096
VMEM_LIMIT_V7X = 60 * 1024 * 1024

MESH = pl.DeviceIdType.MESH
ANY = pl.BlockSpec(memory_space=pl.ANY)
VMEM_WHOLE = pl.BlockSpec(memory_space=pltpu.VMEM)
SDS = jax.ShapeDtypeStruct


def _resident(shape):
    zeros = (0,) * len(shape)
    return pl.BlockSpec(shape, lambda *_: zeros, pipeline_mode=pl.Buffered(1))


def _rms(v):
    return lax.rsqrt(jnp.mean(v * v, axis=-1, keepdims=True) + NORM_EPS)


def _norm_bwd(dy, gain, vhat, rstd):
    t = dy * gain
    return rstd * (t - vhat * jnp.mean(t * vhat, axis=-1, keepdims=True))


def _colsum(v):
    return jnp.sum(v, axis=0, keepdims=True)


def _dot_nt(a, b):
    return lax.dot_general(a, b, (((1,), (1,)), ((), ())), preferred_element_type=F32)


def _dot_tn(a, b):
    return lax.dot_general(a, b, (((0,), (0,)), ((), ())), preferred_element_type=F32)


def _dot(a, b):
    return jnp.dot(a, b, preferred_element_type=F32)


def _chip_block(w_ref, chip):
    both = w_ref[pl.ds(2 * chip, 2)]
    return both.reshape(2 * both.shape[1], both.shape[2])


def _lane_lt64(shape):
    return lax.broadcasted_iota(jnp.int32, shape, 1) < HEAD_DIM


class _Comm(NamedTuple):
    operands: tuple
    out_shapes: tuple
    aliases: dict
    n_remote: int
    n_local: int
    plan: Callable
    after: Callable = None


def _merge(*comms):
    operands, out_shapes, aliases, parts = [], [], {}, []
    n_remote = n_local = 0
    for cm in comms:
        parts.append((len(operands), len(out_shapes), n_remote, n_local, cm))
        for k, v in cm.aliases.items():
            aliases[len(operands) + k] = len(out_shapes) + v
        operands += cm.operands
        out_shapes += cm.out_shapes
        n_remote += cm.n_remote
        n_local += cm.n_local

    def run(which, ins, outs, send, recv, loc):
        sends, recvs, locs = [], [], []
        for i0, o0, r0, l0, cm in parts:
            stage = getattr(cm, which)
            if stage is not None:
                s, r, l = stage(ins[i0:i0 + len(cm.operands)], outs[o0:o0 + len(cm.out_shapes)],
                                lambda k, r0=r0: send(r0 + k), lambda k, r0=r0: recv(r0 + k), lambda k, l0=l0: loc(l0 + k))
                sends, recvs, locs = sends + s, recvs + r, locs + l
        return sends, recvs, locs

    def plan(*args):
        return run("plan", *args)

    def after(*args):
        return run("after", *args)

    return _Comm(tuple(operands), tuple(out_shapes), aliases, n_remote, n_local, plan,
                 after if any(cm.after is not None for cm in comms) else None)


def _sem_scratch(comm):
    return [pltpu.SemaphoreType.DMA((max(comm.n_remote, 1),)), pltpu.SemaphoreType.DMA((max(comm.n_remote, 1),)),
            pltpu.SemaphoreType.DMA((max(comm.n_local, 1),))]


class _Rider(NamedTuple):
    body: Callable
    in_specs: list
    out_specs: list
    out_shape: list
    operands: tuple


def _pallas(body, *, name, grid, in_specs, out_specs, out_shape, operands, scratch=(), comm=None, rider=None):
    params = pltpu.CompilerParams(dimension_semantics=("arbitrary",) * len(grid), vmem_limit_bytes=VMEM_LIMIT_V7X)
    if rider is not None:
        own_in, own_out, ride_in, ride_out = len(in_specs), len(out_specs), len(rider.in_specs), len(rider.out_specs)
        own_body = body

        def body(*refs):
            o0 = own_in + ride_in
            s0 = o0 + own_out + ride_out
            own_body(*refs[:own_in], *refs[o0:o0 + own_out], *refs[s0:])
            first = None
            for axis in range(len(grid)):
                at_start = pl.program_id(axis) == 0
                first = at_start if first is None else jnp.logical_and(first, at_start)
            rider.body(first, *refs[own_in:o0], *refs[o0 + own_out:s0])

        in_specs, out_specs = list(in_specs) + rider.in_specs, list(out_specs) + rider.out_specs
        out_shape, operands = list(out_shape) + rider.out_shape, tuple(operands) + tuple(rider.operands)
    if comm is None:
        return pl.pallas_call(body, name=name, grid=grid, in_specs=in_specs, out_specs=out_specs, out_shape=out_shape,
                              scratch_shapes=list(scratch), compiler_params=params)(*operands)
    n_in, n_out, n_scr = len(in_specs), len(out_specs), len(scratch)
    c_in, c_out = len(comm.operands), len(comm.out_shapes)

    def with_comm(*refs):
        ins, c_ins = refs[:n_in], refs[n_in:n_in + c_in]
        o0 = n_in + c_in
        outs, c_outs = refs[o0:o0 + n_out], refs[o0 + n_out:o0 + n_out + c_out]
        s0 = o0 + n_out + c_out
        scr = refs[s0:s0 + n_scr]
        send_sems, recv_sems, local_sems = refs[s0 + n_scr:]
        first = last = None
        for axis, size in enumerate(grid):
            at_start, at_end = pl.program_id(axis) == 0, pl.program_id(axis) == size - 1
            first = at_start if first is None else jnp.logical_and(first, at_start)
            last = at_end if last is None else jnp.logical_and(last, at_end)

        def copies():
            return comm.plan(c_ins, c_outs, lambda k: send_sems.at[k], lambda k: recv_sems.at[k],
                             lambda k: local_sems.at[k])

        @pl.when(first)
        def _():
            sends, _, locs = copies()
            for cp in sends + locs:
                cp.start()

        body(*ins, *outs, *scr)

        @pl.when(last)
        def _():
            sends, recvs, locs = copies()
            for cp in recvs:
                cp.wait_recv()
            for cp in sends:
                cp.wait_send()
            for cp in locs:
                cp.wait()
            if comm.after is not None:
                sends, recvs, _ = comm.after(c_ins, c_outs, lambda k: send_sems.at[k], lambda k: recv_sems.at[k],
                                             lambda k: local_sems.at[k])
                for cp in sends:
                    cp.start()
                for cp in recvs:
                    cp.wait_recv()
                for cp in sends:
                    cp.wait_send()

    return pl.pallas_call(
        with_comm, name=name, grid=grid,
        in_specs=list(in_specs) + [ANY] * c_in, out_specs=list(out_specs) + [ANY] * c_out,
        out_shape=list(out_shape) + list(comm.out_shapes),
        scratch_shapes=list(scratch) + _sem_scratch(comm),
        input_output_aliases={n_in + k: n_out + v for k, v in comm.aliases.items()},
        compiler_params=params)(*operands, *comm.operands)


def _place():
    return lax.axis_index("x"), lax.axis_index("y"), lax.axis_index("c")


def _other_chips(x, y):
    return [(1 - x, y), (x, 1 - y), (1 - x, 1 - y)]


def _slot(px, py, pc):
    return 4 * px + 2 * py + pc


def _remote(src, dst, send_sem, recv_sem, to):
    return pltpu.make_async_remote_copy(src_ref=src, dst_ref=dst, send_sem=send_sem, recv_sem=recv_sem,
                                        device_id=to, device_id_type=MESH)


def _gather_first(half_block):
    def plan(ins, outs, send, recv, loc):
        (blk,), (full,) = ins, outs
        x, y, c = _place()
        chips = _other_chips(x, y)
        mine = full.at[_slot(x, y, c)]
        sends = [_remote(blk, mine, send(0), recv(0), (x, y, 1 - c))]
        sends += [_remote(blk, mine, send(1 + j), recv(1 + j), (*chip, c)) for j, chip in enumerate(chips)]
        recvs = [_remote(blk, full.at[_slot(x, y, 1 - c)], send(0), recv(0), (x, y, 1 - c))]
        recvs += [_remote(blk, full.at[_slot(*chip, c)], send(1 + j), recv(1 + j), (*chip, c))
                  for j, chip in enumerate(chips)]
        return sends, recvs, [pltpu.make_async_copy(blk, mine, loc(0))]

    return _Comm((half_block,), (SDS((2 * N_CHIPS,) + half_block.shape, half_block.dtype),), {}, 4, 1, plan)


def _gather_second(partly_gathered):
    def plan(ins, outs, send, recv, loc):
        (src,), (full,) = ins, outs
        x, y, c = _place()
        chips = _other_chips(x, y)
        sends = [_remote(src.at[_slot(*chip, c)], full.at[_slot(*chip, c)], send(j), recv(j), (x, y, 1 - c))
                 for j, chip in enumerate(chips)]
        recvs = [_remote(src.at[_slot(*chip, 1 - c)], full.at[_slot(*chip, 1 - c)], send(j), recv(j), (x, y, 1 - c))
                 for j, chip in enumerate(chips)]
        return sends, recvs, []

    return _Comm((partly_gathered,), (SDS(partly_gathered.shape, partly_gathered.dtype),), {0: 0}, 3, 0, plan)


def _relay_pieces(full, rows, x, y, c):
    start, half = rows[0], rows[1] // 2
    upper, lower = pl.ds(start, half), pl.ds(start + half, half)
    diagonal = full.at[_slot(1 - x, 1 - y, c)]
    return [(full.at[_slot(1 - x, y, c), upper], diagonal.at[upper], (x, 1 - y, c)),
            (full.at[_slot(x, 1 - y, c), lower], diagonal.at[lower], (1 - x, y, c))]


def _relay(half_block, so_far, first=None, second=None, third=None, third_after=None):
    has_block, has_buffer = half_block is not None, so_far is not None
    shape = so_far.shape if has_buffer else (2 * N_CHIPS,) + half_block.shape
    dtype = so_far.dtype if has_buffer else half_block.dtype

    def third_leg(rows, k, ins, outs, send, recv):
        src, full = (ins[-1] if has_buffer else outs[0]), outs[0]
        x, y, c = _place()
        span, sibling = pl.ds(*rows), (x, y, 1 - c)
        here, there = _slot(1 - x, 1 - y, c), _slot(1 - x, 1 - y, 1 - c)
        return ([_remote(src.at[here, span], full.at[here, span], send(k), recv(k), sibling)],
                [_remote(src.at[there, span], full.at[there, span], send(k), recv(k), sibling)])

    def plan(ins, outs, send, recv, loc):
        src, full = (ins[-1] if has_buffer else outs[0]), outs[0]
        x, y, c = _place()
        sibling = (x, y, 1 - c)
        sends, recvs, locs = [], [], []
        if first is not None:
            span = pl.ds(*first)
            blk, mine = ins[0].at[span], full.at[_slot(x, y, c), span]
            for k, peer in enumerate([sibling, (1 - x, y, c), (x, 1 - y, c)]):
                sends.append(_remote(blk, mine, send(k), recv(k), peer))
                recvs.append(_remote(blk, full.at[_slot(*peer), span], send(k), recv(k), peer))
            locs.append(pltpu.make_async_copy(blk, mine, loc(0)))
        if second is not None:
            span = pl.ds(*second)
            for k, chip in enumerate([(1 - x, y), (x, 1 - y)]):
                sends.append(_remote(src.at[_slot(*chip, c), span], full.at[_slot(*chip, c), span], send(3 + k), recv(3 + k),
                                     sibling))
                recvs.append(_remote(src.at[_slot(*chip, 1 - c), span], full.at[_slot(*chip, 1 - c), span], send(3 + k),
                                     recv(3 + k), sibling))
            for k, (piece, lands, peer) in enumerate(_relay_pieces(full, second, x, y, c)):
                sends.append(_remote(piece, piece, send(5 + k), recv(5 + k), peer))
                recvs.append(_remote(lands, lands, send(5 + k), recv(5 + k), peer))
        if third is not None:
            s, r = third_leg(third, 7, ins, outs, send, recv)
            sends, recvs = sends + s, recvs + r
        return sends, recvs, locs

    def after(ins, outs, send, recv, loc):
        s, r = third_leg(third_after, 8, ins, outs, send, recv)
        return s, r, []

    operands = ((half_block,) if has_block else ()) + ((so_far,) if has_buffer else ())
    return _Comm(operands, (SDS(shape, dtype),), {len(operands) - 1: 0} if has_buffer else {}, 9, 1, plan,
                 after if third_after is not None else None)


def _gather_whole(first, others, small_block):
    shards = (first, *others)
    n = len(shards)
    hs = [s.shape[0] // 2 for s in shards]
    rows = hs[0]

    def body(*refs):
        src, small_ref = refs[:n], refs[n]
        out_ref, small_out_ref, half_out = refs[n + 1], refs[n + 2], refs[n + 3:2 * n + 2]
        stage, half = refs[2 * n + 2:3 * n + 2], refs[3 * n + 2:4 * n + 2]
        send_sems, recv_sems, local_sems = refs[4 * n + 2:]
        x, y, c = _place()
        me, sibling = (x, y, c), (x, y, 1 - c)
        neighbours, diagonal = [(1 - x, y), (x, 1 - y)], (1 - x, 1 - y)
        loads = [pltpu.make_async_copy(src[k].at[pl.ds(c * hs[k], hs[k])], stage[k], local_sems.at[2 + k]) for k in range(n)]
        loads[0].start()
        loads[0].wait()
        for cp in loads[1:]:
            cp.start()
        blk_ref = half[0]
        blk_ref[...] = stage[0][...].astype(BF16)

        def copy(k, block, to, src=None):
            return _remote(out_ref.at[_slot(*block)] if src is None else src, out_ref.at[_slot(*block)],
                           send_sems.at[k], recv_sems.at[k], to)

        def small_copy(k, chip, to):
            return _remote(small_ref, small_out_ref.at[2 * chip[0] + chip[1]], send_sems.at[8 + k], recv_sems.at[8 + k], to)

        mine = pltpu.make_async_copy(blk_ref, out_ref.at[_slot(*me)], local_sems.at[0])
        mine_small = pltpu.make_async_copy(small_ref, small_out_ref.at[2 * x + y], local_sems.at[1])
        mine.start()
        mine_small.start()
        started = [copy(0, me, sibling, src=blk_ref)]
        started += [copy(1 + k, me, (*chip, c), src=blk_ref) for k, chip in enumerate(neighbours)]
        started += [small_copy(k, (x, y), (*chip, c)) for k, chip in enumerate(neighbours + [diagonal])]
        for cp in started:
            cp.start()
        stores = []
        for k in range(1, n):
            loads[k].wait()
            half[k][...] = stage[k][...].astype(BF16)
            stores.append(pltpu.make_async_copy(half[k], half_out[k - 1], local_sems.at[2 + n + k]))
            stores[-1].start()
        pieces = _relay_pieces(out_ref, (0, rows), x, y, c)
        for k, chip in enumerate(neighbours):
            copy(1 + k, (*chip, c), me).wait_recv()
            piece, _, peer = pieces[k]
            started += [copy(3 + k, (*chip, c), sibling), _remote(piece, piece, send_sems.at[5 + k], recv_sems.at[5 + k], peer)]
            started[-2].start()
            started[-1].start()
        for k, (_, lands, peer) in enumerate(pieces):
            _remote(lands, lands, send_sems.at[5 + k], recv_sems.at[5 + k], peer).wait_recv()
        started.append(copy(7, (*diagonal, c), sibling))
        started[-1].start()
        copy(0, sibling, me).wait_recv()
        for k, chip in enumerate(neighbours):
            copy(3 + k, (*chip, 1 - c), me).wait_recv()
        copy(7, (*diagonal, 1 - c), me).wait_recv()
        for k, chip in enumerate(neighbours + [diagonal]):
            small_copy(k, chip, me).wait_recv()
        for cp in started:
            cp.wait_send()
        mine.wait()
        mine_small.wait()
        for cp in stores:
            cp.wait()

    return pl.pallas_call(
        body, name="gather_whole", in_specs=[ANY] * (n + 1), out_specs=[ANY] * (n + 1),
        out_shape=[SDS((2 * N_CHIPS, rows, D_MODEL), BF16), SDS((N_CHIPS,) + small_block.shape, small_block.dtype)]
                  + [SDS((h, D_MODEL), BF16) for h in hs[1:]],
        scratch_shapes=[pltpu.VMEM((h, D_MODEL), F32) for h in hs] + [pltpu.VMEM((h, D_MODEL), BF16) for h in hs]
                       + [pltpu.SemaphoreType.DMA((11,)), pltpu.SemaphoreType.DMA((11,)), pltpu.SemaphoreType.DMA((2 + 2 * n,))],
        compiler_params=pltpu.CompilerParams(vmem_limit_bytes=VMEM_LIMIT_V7X),
    )(*shards, small_block)


def _pair_send(grads):
    def plan(ins, outs, send, recv, loc):
        (g,), (got,) = ins, outs
        x, y, c = _place()
        copies = [_remote(g.at[j, 1 - c], got.at[j], send(j), recv(j), (x, y, 1 - c)) for j in range(N_CHIPS)]
        return copies, copies, []

    shape = (grads.shape[0],) + grads.shape[2:]
    return _Comm((grads,), (SDS(shape, grads.dtype),), {}, N_CHIPS, 0, plan)


def _chip_exchange(partial):
    def plan(ins, outs, send, recv, loc):
        (p,), (got,) = ins, outs
        x, y, c = _place()
        my_chip = 2 * x + y
        chips = _other_chips(x, y)
        sends = [_remote(p.at[2 * chip[0] + chip[1]], got.at[my_chip], send(j), recv(j), (*chip, c))
                 for j, chip in enumerate(chips)]
        recvs = [_remote(p.at[my_chip], got.at[2 * chip[0] + chip[1]], send(j), recv(j), (*chip, c))
                 for j, chip in enumerate(chips)]
        return sends, recvs, [pltpu.make_async_copy(p.at[my_chip], got.at[my_chip], loc(0))]

    return _Comm((partial,), (SDS(partial.shape, partial.dtype),), {}, 3, 1, plan)


def _pair_sum(name, core, grads, received):
    h = grads.shape[2]

    def body(core_ref, g_ref, r_ref, o_ref):
        o_ref[...] = (g_ref[0] + r_ref[...]).astype(BF16)

    return pl.pallas_call(
        body, name=name,
        grid_spec=pltpu.PrefetchScalarGridSpec(
            num_scalar_prefetch=1, grid=(N_CHIPS,),
            in_specs=[pl.BlockSpec((1, 1, h, D_MODEL), lambda j, core_ref: (j, core_ref[0], 0, 0)),
                      pl.BlockSpec((1, h, D_MODEL), lambda j, core_ref: (j, 0, 0))],
            out_specs=pl.BlockSpec((1, h, D_MODEL), lambda j, core_ref: (j, 0, 0))),
        out_shape=SDS((N_CHIPS, h, D_MODEL), BF16),
        compiler_params=pltpu.CompilerParams(dimension_semantics=("arbitrary",), vmem_limit_bytes=VMEM_LIMIT_V7X),
    )(core, grads, received)


SMALL_ROWS = 8


def _sum_blocks(ref):
    return (ref[0].astype(F32) + ref[1].astype(F32)) + (ref[2].astype(F32) + ref[3].astype(F32))


def _tail_reduce(last_grads, exchanged, small):
    n = len(exchanged)
    h = last_grads.shape[2]

    def body(*refs):
        g_ref, ex, small_ref = refs[0], refs[1:1 + n], refs[1 + n]
        o0 = 2 + n
        out, out_last, small_out = refs[o0:o0 + n], refs[o0 + n], refs[o0 + n + 1]
        s0 = o0 + n + 2
        halves, half_last = refs[s0:s0 + n], refs[s0 + n]
        own, got, part, exch, small_buf = refs[s0 + n + 1:s0 + n + 6]
        ex_buf = refs[s0 + n + 6:s0 + 2 * n + 6]
        pair_send, pair_recv, chip_send, chip_recv, share_send, share_recv, small_send, small_recv, local_sems = refs[s0 + 2 * n + 6:]
        x, y, c = _place()
        sibling = (x, y, 1 - c)
        my_chip, me = 2 * x + y, _slot(x, y, c)
        chips = _other_chips(x, y)[::-1]

        order = [2 * chip[0] + chip[1] for chip in chips] + [my_chip]
        to_sibling = [_remote(g_ref.at[j, 1 - c], got.at[j], pair_send.at[j], pair_recv.at[j], sibling) for j in order]
        load_own = [pltpu.make_async_copy(g_ref.at[j, c], own.at[j], local_sems.at[j]) for j in order]
        load_ex = [pltpu.make_async_copy(ex[k], ex_buf[k], local_sems.at[N_CHIPS + n + 1 + k]) for k in range(n)]
        for give, keep in zip(to_sibling, load_own):
            give.start()
            keep.start()
        for cp in load_ex:
            cp.start()

        small_buf[me] = small_ref[...]
        small_copies = []
        for mask in range(1, 8):
            peer = (x ^ (mask >> 2), y ^ ((mask >> 1) & 1), c ^ (mask & 1))
            small_copies.append(_remote(small_ref, small_buf.at[me], small_send.at[mask - 1], small_recv.at[mask - 1], peer))
        for cp in small_copies:
            cp.start()

        def share(k, half_ref, out_ref):
            keep = pltpu.make_async_copy(half_ref, out_ref.at[c], local_sems.at[N_CHIPS + k])
            give = _remote(half_ref, out_ref.at[c], share_send.at[k], share_recv.at[k], sibling)
            take = _remote(half_ref, out_ref.at[1 - c], share_send.at[k], share_recv.at[k], sibling)
            keep.start()
            give.start()
            return keep, give, take

        def pair_sum(block):
            _remote(g_ref.at[block, 1 - c], got.at[block], pair_send.at[block], pair_recv.at[block], sibling).wait_recv()
            pltpu.make_async_copy(g_ref.at[block, c], own.at[block], local_sems.at[block]).wait()
            part[block] = (own[block] + got[block]).astype(BF16)

        to_chips = []
        for j, chip in enumerate(chips):
            block = 2 * chip[0] + chip[1]
            pair_sum(block)
            to_chips.append(_remote(part.at[block], exch.at[my_chip], chip_send.at[j], chip_recv.at[j], (*chip, c)))
            to_chips[-1].start()
        pair_sum(my_chip)
        exch[my_chip] = part[my_chip]
        from_chips = [_remote(part.at[my_chip], exch.at[2 * chip[0] + chip[1]], chip_send.at[j], chip_recv.at[j], (*chip, c))
                      for j, chip in enumerate(chips)]

        shares = []
        for k in range(n):
            load_ex[k].wait()
            halves[k][...] = _sum_blocks(ex_buf[k])
            shares.append(share(k, halves[k], out[k]))

        for cp in small_copies:
            cp.wait_recv()
        total = small_buf[0]
        for d in range(1, 8):
            total = total + small_buf[d]
        small_out[...] = total

        for cp in from_chips:
            cp.wait_recv()
        half_last[...] = _sum_blocks(exch)
        shares.append(share(n, half_last, out_last))

        for keep, give, take in shares:
            take.wait_recv()
            give.wait_send()
            keep.wait()
        for cp in to_sibling + to_chips + small_copies:
            cp.wait_send()

    blocks = (N_CHIPS, h, D_MODEL)
    return pl.pallas_call(
        body, name="tail_reduce",
        in_specs=[ANY] * (n + 1) + [VMEM_WHOLE], out_specs=[ANY] * (n + 1) + [VMEM_WHOLE],
        out_shape=[SDS((2,) + e.shape[1:], F32) for e in exchanged] + [SDS((2, h, D_MODEL), F32), SDS(small.shape, F32)],
        scratch_shapes=[pltpu.VMEM(e.shape[1:], F32) for e in exchanged] + [pltpu.VMEM((h, D_MODEL), F32)]
                       + [pltpu.VMEM(blocks, F32), pltpu.VMEM(blocks, F32), pltpu.VMEM(blocks, BF16), pltpu.VMEM(blocks, BF16),
                          pltpu.VMEM((8,) + small.shape, F32)]
                       + [pltpu.VMEM(e.shape, BF16) for e in exchanged]
                       + [pltpu.SemaphoreType.DMA((N_CHIPS,)), pltpu.SemaphoreType.DMA((N_CHIPS,)),
                          pltpu.SemaphoreType.DMA((3,)), pltpu.SemaphoreType.DMA((3,)),
                          pltpu.SemaphoreType.DMA((n + 1,)), pltpu.SemaphoreType.DMA((n + 1,)),
                          pltpu.SemaphoreType.DMA((7,)), pltpu.SemaphoreType.DMA((7,)),
                          pltpu.SemaphoreType.DMA((N_CHIPS + 2 * n + 1,))],
        compiler_params=pltpu.CompilerParams(vmem_limit_bytes=VMEM_LIMIT_V7X),
    )(last_grads, *exchanged, small)


def _rope_expansion():
    half = ROT_DIM // 2
    expand = np.zeros((2 * half, 3 * 128), np.float32)
    const = np.zeros((1, 3 * 128), np.float32)
    for lane in range(128):
        d = lane % HEAD_DIM
        if d < ROT_DIM:
            expand[d % half, lane] = 1.0
        else:
            const[0, lane] = 1.0
        if d < half:
            expand[half + d, 128 + lane] = -1.0
        elif d < ROT_DIM:
            expand[half + d - half, 256 + lane] = 1.0
    return expand, const


ROPE_PIECES = 3 * ROT_DIM


def _rope_inputs(seq):
    pos = jnp.arange(seq, dtype=F32)
    inv_freq = ROPE_THETA ** (-jnp.arange(0, ROT_DIM, 2, dtype=F32) / ROT_DIM)
    ang = pos[:, None] * inv_freq[None, :]
    cs = jnp.concatenate([jnp.cos(ang), jnp.sin(ang)], axis=1)
    hi = lax.reduce_precision(cs, 8, 7)
    mid = lax.reduce_precision(cs - hi, 8, 7)
    low = cs - hi - mid
    expand, const = _rope_expansion()
    pieces = jnp.concatenate([hi, mid, low], axis=1).astype(BF16)
    return pieces, jnp.asarray(np.concatenate([expand] * 3, axis=0), BF16), jnp.asarray(const)


def _rope_specs(tb):
    return [pl.BlockSpec((tb, ROPE_PIECES), lambda i: (i, 0)), _resident((ROPE_PIECES, 3 * 128)), _resident((1, 3 * 128))]


def _rope_tile(pieces_ref, expand_ref, const_ref):
    tables = _dot(pieces_ref[...], expand_ref[...]) + const_ref[...]
    return tables[:, 0:128], tables[:, 128:256], tables[:, 256:384]


def _rope(t, c, sa, sb):
    half = ROT_DIM // 2
    return t * c + pltpu.roll(t, 128 - half, 1) * sa + pltpu.roll(t, half, 1) * sb


def _rope_transposed(dt, c, sa, sb):
    half = ROT_DIM // 2
    return dt * c + pltpu.roll(dt * sa, half, 1) + pltpu.roll(dt * sb, 128 - half, 1)


def _in_proj(x, g_pre, w_in_t, rope, comm=None):
    seq = x.shape[0]
    tb = min(seq, WIDE_TOKEN_TILE)

    def body(x_ref, g_ref, w_ref, c_ref, sa_ref, sb_ref,
             q_ref, kd0_ref, kd1_ref, vd0_ref, vd1_ref, gb_ref, gc_ref, xin_ref, hn_ref):
        xv = x_ref[...]
        hn = (xv * _rms(xv) * g_ref[...]).astype(BF16)
        hn_ref[...] = hn
        proj = _dot_nt(hn, w_ref[...].reshape(IN_COLS, D_MODEL))
        c, sa, sb = _rope_tile(c_ref, sa_ref, sb_ref)
        scale = 1.0 / math.sqrt(HEAD_DIM)
        for p in range(Q_WIDTH // 128):
            q_ref[:, 128 * p:128 * (p + 1)] = (_rope(proj[:, 128 * p:128 * (p + 1)], c, sa, sb) * scale).astype(BF16)
        k = _rope(proj[:, Q_WIDTH:Q_WIDTH + KV_WIDTH], c, sa, sb)
        v = proj[:, Q_WIDTH + KV_WIDTH:Q_WIDTH + 2 * KV_WIDTH]
        low = _lane_lt64(k.shape)
        k_sw, v_sw = pltpu.roll(k, HEAD_DIM, 1), pltpu.roll(v, HEAD_DIM, 1)
        kd0_ref[...] = jnp.where(low, k, k_sw).astype(BF16)
        kd1_ref[...] = jnp.where(low, k_sw, k).astype(BF16)
        vd0_ref[...] = jnp.where(low, v, v_sw).astype(BF16)
        vd1_ref[...] = jnp.where(low, v_sw, v).astype(BF16)
        base = Q_WIDTH + 2 * KV_WIDTH
        gb_ref[...] = proj[:, base:base + CONV_WIDTH].astype(BF16)
        gc_ref[...] = proj[:, base + CONV_WIDTH:base + 2 * CONV_WIDTH].astype(BF16)
        xin_ref[...] = proj[:, base + 2 * CONV_WIDTH:base + 3 * CONV_WIDTH].astype(BF16)

    tile = lambda w: pl.BlockSpec((tb, w), lambda i: (i, 0))
    return _pallas(
        body, name="in_proj", grid=(seq // tb,),
        in_specs=[tile(D_MODEL), _resident((1, D_MODEL)), _resident(w_in_t.shape), *_rope_specs(tb)],
        out_specs=[tile(Q_WIDTH), tile(128), tile(128), tile(128), tile(128),
                   tile(CONV_WIDTH), tile(CONV_WIDTH), tile(CONV_WIDTH), tile(D_MODEL)],
        out_shape=[SDS((seq, Q_WIDTH), BF16)] + [SDS((seq, 128), BF16)] * 4
                  + [SDS((seq, CONV_WIDTH), BF16)] * 3 + [SDS((seq, D_MODEL), BF16)],
        operands=(x, g_pre, w_in_t, *rope), comm=comm)


def _attn_valid(i):
    shape = (4 * QBLOCK, 2 * QBLOCK)
    row = lax.broadcasted_iota(jnp.int32, shape, 0)
    col = lax.broadcasted_iota(jnp.int32, shape, 1)
    qi = row & (QBLOCK - 1)
    return (col > qi) & (col <= qi + QBLOCK) & ((col >= QBLOCK) | (i > 0))


def _stack_heads(pair0, pair1):
    low = _lane_lt64(pair0.shape)
    zero = jnp.zeros_like(pair0)
    return jnp.concatenate([jnp.where(low, pair0, zero), jnp.where(low, zero, pair0),
                            jnp.where(low, pair1, zero), jnp.where(low, zero, pair1)], axis=0)


def _unstack_heads(stacked):
    low = _lane_lt64((QBLOCK, 128))
    pair0 = jnp.where(low, stacked[0:QBLOCK], stacked[QBLOCK:2 * QBLOCK])
    pair1 = jnp.where(low, stacked[2 * QBLOCK:3 * QBLOCK], stacked[3 * QBLOCK:4 * QBLOCK])
    return pair0, pair1


def _sink_column(sink_ref, kv_head):
    row = lax.broadcasted_iota(jnp.int32, (4 * QBLOCK, 1), 0)
    s = [sink_ref[0, 4 * kv_head + j] for j in range(4)]
    return jnp.where(row < QBLOCK, s[0], jnp.where(row < 2 * QBLOCK, s[1], jnp.where(row < 3 * QBLOCK, s[2], s[3])))


def _band(ref, i):
    prev = pl.multiple_of(jnp.maximum(i - 1, 0) * QBLOCK, QBLOCK)
    own = pl.multiple_of(i * QBLOCK, QBLOCK)
    return jnp.concatenate([ref[pl.ds(prev, QBLOCK), :], ref[pl.ds(own, QBLOCK), :]], axis=0), prev, own


def _softmax_with_sink(s, sink_col):
    m = jnp.maximum(jnp.max(s, axis=-1, keepdims=True), sink_col)
    p = jnp.exp(s - m)
    e_sink = jnp.exp(sink_col - m)
    inv_l = 1.0 / (jnp.sum(p, axis=-1, keepdims=True) + e_sink)
    return p, e_sink, inv_l


def _attention_fwd(q, kd0, kd1, vd0, vd1, sinks, comm=None):
    seq = q.shape[0]

    nb = ATTN_FWD_BLOCKS

    def body(sink_ref, q_ref, kd0_ref, kd1_ref, vd0_ref, vd1_ref, o_ref):
        for b in range(nb):
            i = pl.program_id(0) * nb + b
            rows = slice(QBLOCK * b, QBLOCK * (b + 1))
            valid = _attn_valid(i)
            for kv_head, (k_ref, v_ref) in enumerate(((kd0_ref, vd0_ref), (kd1_ref, vd1_ref))):
                kband, _, _ = _band(k_ref, i)
                vband, _, _ = _band(v_ref, i)
                base = 256 * kv_head
                qm = _stack_heads(q_ref[rows, base:base + 128], q_ref[rows, base + 128:base + 256])
                s = jnp.where(valid, _dot_nt(qm, kband), NEG_INF)
                p, _, inv_l = _softmax_with_sink(s, _sink_column(sink_ref, kv_head))
                o = _dot(p.astype(BF16), vband) * inv_l
                pair0, pair1 = _unstack_heads(o)
                o_ref[rows, base:base + 128] = pair0.astype(BF16)
                o_ref[rows, base + 128:base + 256] = pair1.astype(BF16)

    blk = pl.BlockSpec((nb * QBLOCK, Q_WIDTH), lambda i: (i, 0))
    full = _resident((seq, 128))
    return _pallas(
        body, name="attention_fwd", grid=(seq // (nb * QBLOCK),),
        in_specs=[pl.BlockSpec(memory_space=pltpu.SMEM), blk, full, full, full, full],
        out_specs=[blk], out_shape=[SDS((seq, Q_WIDTH), BF16)],
        operands=(sinks, q, kd0, kd1, vd0, vd1), comm=comm)


HALO = 16


def _conv_parts(gc, xin, gc_halo, xin_halo, conv_w, first):
    tb = gc.shape[0]
    u = gc.astype(F32) * xin.astype(F32)
    u_halo = jnp.where(first, 0.0, gc_halo.astype(F32) * xin_halo.astype(F32))
    ext = jnp.concatenate([u_halo, u], axis=0)
    u1 = pltpu.roll(ext, 1, 0)[HALO:HALO + tb]
    u2 = pltpu.roll(ext, 2, 0)[HALO:HALO + tb]
    y = conv_w[0:1, :] * u2 + conv_w[1:2, :] * u1 + conv_w[2:3, :] * u
    return u, u1, u2, y


def _halo_prev(tb, w):
    return pl.BlockSpec((HALO, w), lambda i: (jnp.maximum(i * (tb // HALO) - 1, 0), 0))


def _residual_mid(x, mix, g_post_mix):
    mix_f = mix.astype(F32)
    return x + mix_f * _rms(mix_f) * g_post_mix


def _mix_out(attn, gb, gc, xin, conv_w, g_attn, g_conv, w_out, comm=None):
    seq = attn.shape[0]
    tb = min(seq, WIDE_TOKEN_TILE)

    def body(a_ref, gb_ref, gc_ref, xin_ref, gch_ref, xinh_ref, cw_ref, ga_ref, gcn_ref, w_ref, mix_ref, mixed_ref):
        first = pl.program_id(0) == 0
        _, _, _, y = _conv_parts(gc_ref[...], xin_ref[...], gch_ref[...], xinh_ref[...], cw_ref[...], first)
        conv = gb_ref[...].astype(F32) * y
        a = a_ref[...].astype(F32)
        mixed_ref[:, 0:Q_WIDTH] = (a * _rms(a) * ga_ref[...]).astype(BF16)
        mixed_ref[:, Q_WIDTH:] = (conv * _rms(conv) * gcn_ref[...]).astype(BF16)
        mix_ref[...] = _dot(mixed_ref[...], w_ref[...].reshape(D_MODEL, D_MODEL)).astype(BF16)

    tile = lambda w: pl.BlockSpec((tb, w), lambda i: (i, 0))
    return _pallas(
        body, name="mix_out", grid=(seq // tb,),
        in_specs=[tile(Q_WIDTH), tile(CONV_WIDTH), tile(CONV_WIDTH), tile(CONV_WIDTH),
                  _halo_prev(tb, CONV_WIDTH), _halo_prev(tb, CONV_WIDTH),
                  _resident((CONV_K, CONV_WIDTH)), _resident((1, Q_WIDTH)), _resident((1, CONV_WIDTH)),
                  _resident(w_out.shape)],
        out_specs=[tile(D_MODEL), tile(D_MODEL)],
        out_shape=[SDS((seq, D_MODEL), BF16), SDS((seq, D_MODEL), BF16)],
        operands=(attn, gb, gc, xin, gc, xin, conv_w, g_attn, g_conv, w_out), comm=comm)


def _mlp_fwd_bwd(x, mix, target, g_post_mix, g_pre_mlp, g_post_mlp, w_up, w_down):
    seq = x.shape[0]
    tb = TOKEN_TILE

    def body(x_ref, mix_ref, t_ref, gpm_ref, g2_ref, g4_ref, wup_ref, wdown_ref,
             up_ref, hn2_ref, dmlp_ref, dup_ref, dh_ref, dmix_ref, loss_ref, dg4_ref, dg2_ref, dgpm_ref):
        @pl.when(pl.program_id(0) == 0)
        def _():
            for ref in (loss_ref, dg4_ref, dg2_ref, dgpm_ref):
                ref[...] = jnp.zeros_like(ref)

        halves = [slice(0, tb // 2), slice(tb // 2, tb)]
        chunks = [slice(1024 * j, 1024 * (j + 1)) for j in range(N_CHIPS)]
        hv, hn2, mlp, dout, dmlp, dhn2 = [], [], [], [], [], []
        for rows in halves:
            hv.append(_residual_mid(x_ref[rows, :], mix_ref[rows, :], gpm_ref[...]))
            hn2.append((hv[-1] * _rms(hv[-1]) * g2_ref[...]).astype(BF16))
            hn2_ref[rows, :] = hn2[-1]
        for k, rows in enumerate(halves):
            acc = None
            for j, cols in enumerate(chunks):
                up = jnp.maximum(_dot(hn2[k], _chip_block(wup_ref, j)), 0.0)
                up_ref[rows, cols] = up.astype(BF16)
                part = _dot((up * up).astype(BF16), _chip_block(wdown_ref, j))
                acc = part if acc is None else acc + part
            mlp.append(acc)
        loss = jnp.zeros((1, 1), F32)
        dg4 = jnp.zeros((1, D_MODEL), F32)
        for k, rows in enumerate(halves):
            rstd = _rms(mlp[k])
            zhat = mlp[k] * rstd
            diff = hv[k] + zhat * g4_ref[...] - t_ref[rows, :]
            loss = loss + jnp.sum(jnp.sum(diff * diff, axis=1, keepdims=True), axis=0, keepdims=True)
            dout.append(diff * (1.0 / D_MODEL))
            dg4 = dg4 + _colsum(dout[k] * zhat)
            dmlp.append(_norm_bwd(dout[k], g4_ref[...], zhat, rstd).astype(BF16))
            dmlp_ref[rows, :] = dmlp[k]
        for k, rows in enumerate(halves):
            acc = None
            for j, cols in enumerate(chunks):
                dact = _dot_nt(dmlp[k], _chip_block(wdown_ref, j))
                dup = (dact * (2.0 * up_ref[rows, cols].astype(F32))).astype(BF16)
                dup_ref[rows, cols] = dup
                part = _dot_nt(dup, _chip_block(wup_ref, j))
                acc = part if acc is None else acc + part
            dhn2.append(acc)
        dg2 = jnp.zeros((1, D_MODEL), F32)
        dgpm = jnp.zeros((1, D_MODEL), F32)
        for k, rows in enumerate(halves):
            r2 = _rms(hv[k])
            hhat = hv[k] * r2
            dg2 = dg2 + _colsum(dhn2[k] * hhat)
            dh = dout[k] + _norm_bwd(dhn2[k], g2_ref[...], hhat, r2)
            dh_ref[rows, :] = dh.astype(BF16)
            mix_v = mix_ref[rows, :].astype(F32)
            rz = _rms(mix_v)
            zhat = mix_v * rz
            dgpm = dgpm + _colsum(dh * zhat)
            dmix_ref[rows, :] = _norm_bwd(dh, gpm_ref[...], zhat, rz).astype(BF16)
        loss_ref[...] += loss
        dg4_ref[...] += dg4
        dg2_ref[...] += dg2
        dgpm_ref[...] += dgpm

    tile = lambda w: pl.BlockSpec((tb, w), lambda i: (i, 0))
    vec = pl.BlockSpec((1, D_MODEL), lambda i: (0, 0))
    return _pallas(
        body, name="mlp_fwd_bwd", grid=(seq // tb,),
        in_specs=[tile(D_MODEL), tile(D_MODEL), tile(D_MODEL), _resident((1, D_MODEL)), _resident((1, D_MODEL)),
                  _resident((1, D_MODEL)), _resident(w_up.shape), _resident(w_down.shape)],
        out_specs=[tile(D_FF), tile(D_MODEL), tile(D_MODEL), tile(D_FF), tile(D_MODEL), tile(D_MODEL),
                   pl.BlockSpec((1, 1), lambda i: (0, 0)), vec, vec, vec],
        out_shape=[SDS((seq, D_FF), BF16), SDS((seq, D_MODEL), BF16), SDS((seq, D_MODEL), BF16), SDS((seq, D_FF), BF16),
                   SDS((seq, D_MODEL), BF16), SDS((seq, D_MODEL), BF16),
                   SDS((1, 1), F32), SDS((1, D_MODEL), F32), SDS((1, D_MODEL), F32), SDS((1, D_MODEL), F32)],
        operands=(x, mix, target, g_post_mix, g_pre_mlp, g_post_mlp, w_up, w_down))


def _mix_bwd(dmix, attn, gb, gc, xin, conv_w, g_attn, g_conv, w_out, n_k):
    seq = attn.shape[0]
    tb = seq // (N_CHIPS * n_k)

    def body(first, dmix_ref, a_ref, gb_ref, gc_ref, xin_ref, gch_ref, xinh_ref, cw_ref, ga_ref, gcn_ref, w_ref,
             dattn_ref, dgb_ref, dy_ref, dga_ref, dgcn_ref, dcw_ref):
        @pl.when(first)
        def _():
            dga_ref[...] = jnp.zeros_like(dga_ref)
            dgcn_ref[...] = jnp.zeros_like(dgcn_ref)
            dcw_ref[...] = jnp.zeros_like(dcw_ref)

        dmixed = _dot_nt(dmix_ref[...], w_ref[...].reshape(D_MODEL, D_MODEL))
        a = a_ref[...].astype(F32)
        ra = _rms(a)
        ahat = a * ra
        dan = dmixed[:, 0:Q_WIDTH]
        dga_ref[...] += _colsum(dan * ahat)
        dattn_ref[...] = _norm_bwd(dan, ga_ref[...], ahat, ra).astype(BF16)
        gbv = gb_ref[...].astype(F32)
        u, u1, u2, y = _conv_parts(gc_ref[...], xin_ref[...], gch_ref[...], xinh_ref[...], cw_ref[...], first)
        conv = gbv * y
        rc = _rms(conv)
        chat = conv * rc
        dcn = dmixed[:, Q_WIDTH:]
        dgcn_ref[...] += _colsum(dcn * chat)
        dconv = _norm_bwd(dcn, gcn_ref[...], chat, rc)
        dgb_ref[...] = (dconv * y).astype(BF16)
        dy = dconv * gbv
        dy_ref[...] = dy.astype(BF16)
        dcw_ref[0:1, :] += _colsum(dy * u2)
        dcw_ref[1:2, :] += _colsum(dy * u1)
        dcw_ref[2:3, :] += _colsum(dy * u)

    tile = lambda w: pl.BlockSpec((tb, w), lambda j, k: (j * n_k + k, 0))
    halo = lambda w: pl.BlockSpec((HALO, w), lambda j, k: (jnp.maximum((j * n_k + k) * (tb // HALO) - 1, 0), 0))
    whole = lambda shape: pl.BlockSpec(shape, lambda j, k: (0,) * len(shape))
    return _Rider(
        body,
        in_specs=[tile(D_MODEL), tile(Q_WIDTH), tile(CONV_WIDTH), tile(CONV_WIDTH), tile(CONV_WIDTH),
                  halo(CONV_WIDTH), halo(CONV_WIDTH),
                  _resident((CONV_K, CONV_WIDTH)), _resident((1, Q_WIDTH)), _resident((1, CONV_WIDTH)),
                  _resident(w_out.shape)],
        out_specs=[tile(Q_WIDTH), tile(CONV_WIDTH), tile(CONV_WIDTH),
                   whole((1, Q_WIDTH)), whole((1, CONV_WIDTH)), whole((CONV_K, CONV_WIDTH))],
        out_shape=[SDS((seq, Q_WIDTH), BF16), SDS((seq, CONV_WIDTH), BF16), SDS((seq, CONV_WIDTH), BF16),
                   SDS((1, Q_WIDTH), F32), SDS((1, CONV_WIDTH), F32), SDS((CONV_K, CONV_WIDTH), F32)],
        operands=(dmix, attn, gb, gc, xin, gc, xin, conv_w, g_attn, g_conv, w_out))


def _attention_bwd(q, dattn, attn, kd0, kd1, vd0, vd1, sinks, comm=None):
    seq = q.shape[0]
    nb = ATTN_BWD_BLOCKS

    def body(sink_ref, q_ref, do_ref, o_ref, kd0_ref, kd1_ref, vd0_ref, vd1_ref,
             dq_ref, dk0_ref, dk1_ref, dv0_ref, dv1_ref, dsink_ref):
        @pl.when(pl.program_id(0) == 0)
        def _():
            for r in (dk0_ref, dk1_ref, dv0_ref, dv1_ref, dsink_ref):
                r[...] = jnp.zeros_like(r)

        lane = lax.broadcasted_iota(jnp.int32, (1, 128), 1)
        dsink = jnp.zeros((1, 128), F32)
        for b in range(nb):
            i = pl.program_id(0) * nb + b
            rows = slice(QBLOCK * b, QBLOCK * (b + 1))
            valid = _attn_valid(i)
            for kv_head, (k_ref, v_ref, dk_ref, dv_ref) in enumerate(
                    ((kd0_ref, vd0_ref, dk0_ref, dv0_ref), (kd1_ref, vd1_ref, dk1_ref, dv1_ref))):
                kband, prev, own = _band(k_ref, i)
                vband, _, _ = _band(v_ref, i)
                base = 256 * kv_head
                qm = _stack_heads(q_ref[rows, base:base + 128], q_ref[rows, base + 128:base + 256])
                dom = _stack_heads(do_ref[rows, base:base + 128], do_ref[rows, base + 128:base + 256])
                om = _stack_heads(o_ref[rows, base:base + 128], o_ref[rows, base + 128:base + 256])
                s = jnp.where(valid, _dot_nt(qm, kband), NEG_INF)
                p, e_sink, inv_l = _softmax_with_sink(s, _sink_column(sink_ref, kv_head))
                p = p * inv_l
                delta = jnp.sum(dom.astype(F32) * om.astype(F32), axis=-1, keepdims=True)
                ds = (p * (_dot_nt(dom, vband) - delta)).astype(BF16)
                sink_term = -(e_sink * inv_l) * delta
                for j in range(4):
                    part = jnp.sum(sink_term[QBLOCK * j:QBLOCK * (j + 1)], axis=0, keepdims=True)
                    dsink = dsink + jnp.where(lane == 4 * kv_head + j, part, 0.0)
                pair0, pair1 = _unstack_heads(_dot(ds, kband))
                dq_ref[rows, base:base + 128] = pair0.astype(BF16)
                dq_ref[rows, base + 128:base + 256] = pair1.astype(BF16)
                dkd = _dot_tn(ds, qm)
                dkd = dkd + pltpu.roll(dkd, HEAD_DIM, 1)
                dvd = _dot_tn(p.astype(BF16), dom)
                dvd = dvd + pltpu.roll(dvd, HEAD_DIM, 1)
                dk_ref[pl.ds(prev, QBLOCK), :] += dkd[0:QBLOCK]
                dk_ref[pl.ds(own, QBLOCK), :] += dkd[QBLOCK:]
                dv_ref[pl.ds(prev, QBLOCK), :] += dvd[0:QBLOCK]
                dv_ref[pl.ds(own, QBLOCK), :] += dvd[QBLOCK:]
        dsink_ref[...] += dsink

    blk = pl.BlockSpec((nb * QBLOCK, Q_WIDTH), lambda i: (i, 0))
    full = _resident((seq, 128))
    acc = pl.BlockSpec((seq, 128), lambda i: (0, 0))
    return _pallas(
        body, name="attention_bwd", grid=(seq // (nb * QBLOCK),),
        in_specs=[pl.BlockSpec(memory_space=pltpu.SMEM), blk, blk, blk, full, full, full, full],
        out_specs=[blk, acc, acc, acc, acc, pl.BlockSpec((1, 128), lambda i: (0, 0))],
        out_shape=[SDS((seq, Q_WIDTH), BF16)] + [SDS((seq, 128), F32)] * 4 + [SDS((1, 128), F32)],
        operands=(sinks, q, dattn, attn, kd0, kd1, vd0, vd1), comm=comm)


def _in_proj_bwd(dq, dk0, dk1, dv0, dv1, dgb, dy, gc, xin, conv_w, x, dh, g_pre, w_in_t, rope):
    seq = x.shape[0]
    tb = min(seq, WIDE_TOKEN_TILE)
    n_tiles = seq // tb

    def body(dq_ref, dk0_ref, dk1_ref, dv0_ref, dv1_ref, dgb_ref, dy_ref, dyh_ref, gc_ref, xin_ref, cw_ref,
             x_ref, dh_ref, g_ref, w_ref, c_ref, sa_ref, sb_ref,
             dproj_ref, gx_ref, dg_ref):
        i = pl.program_id(0)

        @pl.when(i == 0)
        def _():
            dg_ref[...] = jnp.zeros_like(dg_ref)

        dy = dy_ref[...].astype(F32)
        ext = jnp.concatenate([dy, jnp.where(i == n_tiles - 1, 0.0, dyh_ref[...].astype(F32))], axis=0)
        dy1 = pltpu.roll(ext, tb + HALO - 1, 0)[0:tb]
        dy2 = pltpu.roll(ext, tb + HALO - 2, 0)[0:tb]
        cw = cw_ref[...]
        du = cw[2:3, :] * dy + cw[1:2, :] * dy1 + cw[0:1, :] * dy2
        scale = 1.0 / math.sqrt(HEAD_DIM)
        base = Q_WIDTH + 2 * KV_WIDTH
        halves = [slice(0, tb // 2), slice(tb // 2, tb)]
        low = _lane_lt64((tb // 2, 128))
        for rows in halves:
            c, sa, sb = _rope_tile(c_ref.at[rows, :], sa_ref, sb_ref)
            for p in range(Q_WIDTH // 128):
                dproj_ref[rows, 128 * p:128 * (p + 1)] = _rope_transposed(
                    dq_ref[rows, 128 * p:128 * (p + 1)].astype(F32) * scale, c, sa, sb).astype(BF16)
            dk = jnp.where(low, dk0_ref[rows, :], dk1_ref[rows, :])
            dproj_ref[rows, Q_WIDTH:Q_WIDTH + KV_WIDTH] = _rope_transposed(dk, c, sa, sb).astype(BF16)
            dproj_ref[rows, Q_WIDTH + KV_WIDTH:base] = jnp.where(low, dv0_ref[rows, :], dv1_ref[rows, :]).astype(BF16)
            dproj_ref[rows, base:base + CONV_WIDTH] = dgb_ref[rows, :]
            dproj_ref[rows, base + CONV_WIDTH:base + 2 * CONV_WIDTH] = (du[rows] * xin_ref[rows, :].astype(F32)).astype(BF16)
            dproj_ref[rows, base + 2 * CONV_WIDTH:] = (du[rows] * gc_ref[rows, :].astype(F32)).astype(BF16)
        w_all = w_ref[...].reshape(IN_COLS, D_MODEL)
        dhn = [_dot(dproj_ref[rows, :], w_all) for rows in halves]
        dg = jnp.zeros((1, D_MODEL), F32)
        for k, rows in enumerate(halves):
            xv = x_ref[rows, :]
            r = _rms(xv)
            xhat = xv * r
            dg = dg + _colsum(dhn[k] * xhat)
            gx_ref[rows, :] = dh_ref[rows, :].astype(F32) + _norm_bwd(dhn[k], g_ref[...], xhat, r)
        dg_ref[...] += dg

    tile = lambda w: pl.BlockSpec((tb, w), lambda i: (i, 0))
    halo_next = pl.BlockSpec((HALO, CONV_WIDTH), lambda i: (jnp.minimum((i + 1) * (tb // HALO), seq // HALO - 1), 0))
    return _pallas(
        body, name="in_proj_bwd", grid=(n_tiles,),
        in_specs=[tile(Q_WIDTH), tile(128), tile(128), tile(128), tile(128), tile(CONV_WIDTH), tile(CONV_WIDTH), halo_next,
                  tile(CONV_WIDTH), tile(CONV_WIDTH), _resident((CONV_K, CONV_WIDTH)),
                  tile(D_MODEL), tile(D_MODEL), _resident((1, D_MODEL)), _resident(w_in_t.shape), *_rope_specs(tb)],
        out_specs=[tile(IN_COLS), tile(D_MODEL), pl.BlockSpec((1, D_MODEL), lambda i: (0, 0))],
        out_shape=[SDS((seq, IN_COLS), BF16), SDS((seq, D_MODEL), F32), SDS((1, D_MODEL), F32)],
        operands=(dq, dk0, dk1, dv0, dv1, dgb, dy, dy, gc, xin, conv_w, x, dh, g_pre, w_in_t, *rope))


def _wgrad_grid(seq, per_chip, h_rows, with_rider=False):
    chips_per_step = 1 if per_chip else N_CHIPS
    m = chips_per_step * 2 * h_rows
    bt = min(seq, WGRAD_TOKEN_TILE if per_chip and not with_rider else WGRAD_TOKEN_TILE // 2)
    return chips_per_step, m, bt, seq // bt


def _wgrad(name, a, b, *, per_chip, h_rows, square_a=False, comm=None, rider=None):
    seq = a.shape[0]
    chips_per_step, m, bt, n_k = _wgrad_grid(seq, per_chip, h_rows, rider is not None)
    a_cols = m if per_chip else a.shape[1]
    a_wide = a.shape[1] > a_cols
    b_wide = b.shape[1] > D_MODEL

    def body(a_ref, b_ref, g_ref):
        @pl.when(pl.program_id(1) == 0)
        def _():
            g_ref[...] = jnp.zeros_like(g_ref)

        av = a_ref[...]
        if square_a:
            av = (av.astype(F32) * av.astype(F32)).astype(BF16)
        g_ref[...] += _dot_tn(av, b_ref[...]).reshape(g_ref.shape)

    a_spec = pl.BlockSpec((bt, a_cols), (lambda j, k: (k, j)) if a_wide else (lambda j, k: (k, 0)))
    b_spec = pl.BlockSpec((bt, D_MODEL), (lambda j, k: (k, j)) if b_wide else (lambda j, k: (k, 0)))
    g_spec = pl.BlockSpec((chips_per_step, 2, h_rows, D_MODEL), lambda j, k: (j, 0, 0, 0),
                          pipeline_mode=None if per_chip else pl.Buffered(1))
    return _pallas(
        body, name=name, grid=(N_CHIPS if per_chip else 1, n_k),
        in_specs=[a_spec, b_spec], out_specs=[g_spec], out_shape=[SDS((N_CHIPS, 2, h_rows, D_MODEL), F32)],
        operands=(a, b), comm=comm, rider=rider)


def _adamw_math(w, g, m, v):
    m = ADAM_B1 * m + (1.0 - ADAM_B1) * g
    v = ADAM_B2 * v + (1.0 - ADAM_B2) * (g * g)
    m_hat = m / (1.0 - ADAM_B1 ** ADAM_STEP)
    v_hat = v / (1.0 - ADAM_B2 ** ADAM_STEP)
    delta = -ADAM_LR * (m_hat / (jnp.sqrt(v_hat) + ADAM_EPS) + ADAM_WD * w)
    return delta, m, v


ADAMW_STEPS_PER_HALF = 4


def _adamw_rows(items):
    n = len(items)
    per_half = ADAMW_STEPS_PER_HALF

    def body(*refs):
        for k in range(n):
            r_ref, w_ref, m_ref, v_ref = refs[4 * k:4 * k + 4]
            g_out, d_out, m_out, v_out = refs[4 * (n + k):4 * (n + k) + 4]
            g = r_ref[0]
            g_out[...] = g
            d_out[...], m_out[...], v_out[...] = _adamw_math(w_ref[...], g, m_ref[...], v_ref[...])

    in_specs, out_specs, out_shape, operands = [], [], [], []
    for reduced, w, m, v in items:
        rt = reduced.shape[1] // per_half
        blk = pl.BlockSpec((rt, D_MODEL), lambda h, r: (h * per_half + r, 0))
        in_specs += [pl.BlockSpec((1, rt, D_MODEL), lambda h, r: (h, r, 0)), blk, blk, blk]
        out_specs += [blk] * 4
        out_shape += [SDS(w.shape, F32)] * 4
        operands += [reduced, w, m, v]
    res = _pallas(body, name="adamw_rows", grid=(2, per_half), in_specs=in_specs, out_specs=out_specs,
                  out_shape=out_shape, operands=tuple(operands))
    return [res[4 * k:4 * k + 4] for k in range(n)]


def _adamw_small(packed_grads, w, m, v):
    names = SMALL_NAMES
    n = len(names)
    conv_local = w["conv_w"].shape[-1]

    def body(*refs):
        gp = refs[0]
        w_refs, m_refs, v_refs = refs[1:1 + n], refs[1 + n:1 + 2 * n], refs[1 + 2 * n:1 + 3 * n]
        outs = refs[1 + 3 * n:]
        g_out, d_out, m_out, v_out = outs[0:n], outs[n:2 * n], outs[2 * n:3 * n], outs[3 * n:4 * n]
        chip = 2 * lax.axis_index("x") + lax.axis_index("y")

        def step(k, g, index=None):
            pick = (lambda r: r[...]) if index is None else (lambda r: r[index])
            d, new_m, new_v = _adamw_math(pick(w_refs[k]), g, pick(m_refs[k]), pick(v_refs[k]))
            for ref, val in ((g_out[k], g), (d_out[k], d), (m_out[k], new_m), (v_out[k], new_v)):
                if index is None:
                    ref[...] = val
                else:
                    ref[index] = val

        for k, name in enumerate(names):
            if name in SMALL_VECTORS:
                step(k, gp[SMALL_VECTORS.index(name):SMALL_VECTORS.index(name) + 1, :])
            elif name == "attn_group_norm":
                step(k, gp[4:5, 0:Q_WIDTH])
            elif name == "conv_group_norm":
                step(k, gp[4:5, Q_WIDTH:])
            elif name == "attn_sinks":
                step(k, gp[7:8, 0:8])
            else:
                for t in range(CONV_K):
                    row, base = 5 + t // 2, CONV_WIDTH * (t % 2)
                    g = gp[row:row + 1, base:base + conv_local]
                    for j in range(1, CONV_WIDTH // conv_local):
                        g = jnp.where(chip == j, gp[row:row + 1, base + conv_local * j:base + conv_local * (j + 1)], g)
                    step(k, g, index=(0, slice(t, t + 1), slice(None)))

    shapes = [SDS(w[name].shape, F32) for name in names]
    res = pl.pallas_call(
        body, name="adamw_small", in_specs=[VMEM_WHOLE] * (1 + 3 * n), out_specs=[VMEM_WHOLE] * (4 * n),
        out_shape=shapes * 4,
    )(packed_grads, *[w[k] for k in names], *[m[k] for k in names], *[v[k] for k in names])
    return [dict(zip(names, res[i * n:(i + 1) * n])) for i in range(4)]


SMALL_VECTORS = ("pre_mix_norm", "post_mix_norm", "pre_mlp_norm", "post_mlp_norm")
SMALL_NAMES = SMALL_VECTORS + ("attn_group_norm", "conv_group_norm", "conv_w", "attn_sinks")


def _pack_small(p):
    rows = [p[n].reshape(1, D_MODEL) for n in SMALL_VECTORS]
    rows.append(jnp.concatenate([p["attn_group_norm"].reshape(1, -1), p["conv_group_norm"].reshape(1, -1)], axis=1))
    cw = p["conv_w"].reshape(CONV_K, -1)
    rows.append(jnp.pad(cw, ((0, 1), (0, CONV_WIDTH - cw.shape[1]))).reshape(2, D_MODEL))
    last = jnp.concatenate([p["attn_sinks"].reshape(1, 8), p.get("loss_sum", jnp.zeros((1, 1), F32))], axis=1)
    rows.append(jnp.pad(last, ((0, 0), (0, D_MODEL - 9))))
    return jnp.concatenate(rows, axis=0)


WEIGHT_ORDER = ("pre_mix_norm", "w_in", "conv_w", "attn_sinks", "attn_group_norm", "conv_group_norm", "w_out",
                "post_mix_norm", "pre_mlp_norm", "w_up", "w_down", "post_mlp_norm")


def kernel(x, pre_mix_norm, w_in, conv_w, attn_sinks, attn_group_norm, conv_group_norm, w_out, post_mix_norm, pre_mlp_norm, w_up, w_down, post_mlp_norm, loss_target, m_pre_mix_norm, m_w_in, m_conv_w, m_attn_sinks, m_attn_group_norm, m_conv_group_norm, m_w_out, m_post_mix_norm, m_pre_mlp_norm, m_w_up, m_w_down, m_post_mlp_norm, v_pre_mix_norm, v_w_in, v_conv_w, v_attn_sinks, v_attn_group_norm, v_conv_group_norm, v_w_out, v_post_mix_norm, v_pre_mlp_norm, v_w_up, v_w_down, v_post_mlp_norm):
    w = dict(pre_mix_norm=pre_mix_norm, w_in=w_in, conv_w=conv_w, attn_sinks=attn_sinks, attn_group_norm=attn_group_norm,
             conv_group_norm=conv_group_norm, w_out=w_out, post_mix_norm=post_mix_norm, pre_mlp_norm=pre_mlp_norm,
             w_up=w_up, w_down=w_down, post_mlp_norm=post_mlp_norm)
    m = dict(pre_mix_norm=m_pre_mix_norm, w_in=m_w_in, conv_w=m_conv_w, attn_sinks=m_attn_sinks,
             attn_group_norm=m_attn_group_norm, conv_group_norm=m_conv_group_norm, w_out=m_w_out,
             post_mix_norm=m_post_mix_norm, pre_mlp_norm=m_pre_mlp_norm, w_up=m_w_up, w_down=m_w_down,
             post_mlp_norm=m_post_mlp_norm)
    v = dict(pre_mix_norm=v_pre_mix_norm, w_in=v_w_in, conv_w=v_conv_w, attn_sinks=v_attn_sinks,
             attn_group_norm=v_attn_group_norm, conv_group_norm=v_conv_group_norm, w_out=v_w_out,
             post_mix_norm=v_post_mix_norm, pre_mlp_norm=v_pre_mlp_norm, w_up=v_w_up, w_down=v_w_down,
             post_mlp_norm=v_post_mlp_norm)
    core = lax.axis_index("c").astype(jnp.int32).reshape(1)
    xs, target = x[0], loss_target[0]
    rope = _rope_inputs(xs.shape[0])

    conv_pad = jnp.pad(conv_w[0], ((0, 8 - CONV_K), (0, 0)))
    wf_in, conv_all, hb_up, hb_down, hb_out = _gather_whole(w_in[0].T, (w_up[0], w_down[0], w_out[0]), conv_pad)
    conv_full = conv_all[:, :CONV_K, :].transpose(1, 0, 2).reshape(CONV_K, CONV_WIDTH)

    whole_up, early, late = (0, H_UP), (0, DOWN_EARLY_ROWS), (DOWN_EARLY_ROWS, H_DOWN - DOWN_EARLY_ROWS)
    *proj, wf_up, wf_out, wf_down = _in_proj(
        xs, pre_mix_norm, wf_in, rope,
        comm=_merge(_relay(hb_up, None, first=whole_up), _gather_first(hb_out), _relay(hb_down, None, first=early)))
    q, kd0, kd1, vd0, vd1, gb, gc, xin, hn = proj
    attn, wf_up, wf_out, wf_down = _attention_fwd(
        q, kd0, kd1, vd0, vd1, attn_sinks,
        comm=_merge(_relay(None, wf_up, second=whole_up), _gather_second(wf_out),
                    _relay(hb_down, wf_down, first=late, second=early)))
    mix, mixed, wf_up, wf_down = _mix_out(
        attn, gb, gc, xin, conv_full, attn_group_norm, conv_group_norm, wf_out,
        comm=_merge(_relay(None, wf_up, third=whole_up), _relay(None, wf_down, second=late, third=early, third_after=late)))
    up, hn2, dmlp, dup, dh, dmix, loss_sum, dg_post_mlp, dg_pre_mlp, dg_post_mix = _mlp_fwd_bwd(
        xs, mix, target, post_mix_norm, pre_mlp_norm, post_mlp_norm, wf_up, wf_down)

    n_k = _wgrad_grid(xs.shape[0], True, H_DOWN, with_rider=True)[3]
    g_down, dattn, dgb, dy, dg_attn, dg_conv, dconv_w = _wgrad(
        "wgrad_down", up, dmlp, per_chip=True, h_rows=H_DOWN, square_a=True,
        rider=_mix_bwd(dmix, attn, gb, gc, xin, conv_full, attn_group_norm, conv_group_norm, wf_out, n_k))
    g_up, got_down = _wgrad("wgrad_up", hn2, dup, per_chip=True, h_rows=H_UP, comm=_pair_send(g_down))
    p_down = _pair_sum("pair_sum_down", core, g_down, got_down)
    g_out, got_up = _wgrad("wgrad_out", mixed, dmix, per_chip=False, h_rows=H_OUT, comm=_pair_send(g_up))
    p_up = _pair_sum("pair_sum_up", core, g_up, got_up)
    dq, dk0, dk1, dv0, dv1, dsink, ex_down, ex_up, got_out = _attention_bwd(
        q, dattn, attn, kd0, kd1, vd0, vd1, attn_sinks,
        comm=_merge(_chip_exchange(p_down), _chip_exchange(p_up), _pair_send(g_out)))
    p_out = _pair_sum("pair_sum_out", core, g_out, got_out)
    dproj, grad_x, dg_pre_mix = _in_proj_bwd(dq, dk0, dk1, dv0, dv1, dgb, dy, gc, xin, conv_full, xs, dh, pre_mix_norm,
                                             wf_in, rope)
    g_in, ex_out = _wgrad("wgrad_in", dproj, hn, per_chip=False, h_rows=H_IN, comm=_chip_exchange(p_out))
    small = dict(pre_mix_norm=dg_pre_mix, conv_w=dconv_w, attn_sinks=dsink[:, :8], attn_group_norm=dg_attn,
                 conv_group_norm=dg_conv, post_mix_norm=dg_post_mix, pre_mlp_norm=dg_pre_mlp, post_mlp_norm=dg_post_mlp,
                 loss_sum=loss_sum)
    r_down, r_up, r_out, r_in, small_total = _tail_reduce(g_in, [ex_down, ex_up, ex_out], _pack_small(small))

    out_g, out_d, out_m, out_v = {}, {}, {}, {}
    res_up, res_down, res_out, res_in_t = _adamw_rows([
        (r_up, w_up[0], m_w_up[0], v_w_up[0]), (r_down, w_down[0], m_w_down[0], v_w_down[0]),
        (r_out, w_out[0], m_w_out[0], v_w_out[0]), (r_in, w_in[0].T, m_w_in[0].T, v_w_in[0].T)])
    for name, res in (("w_up", res_up), ("w_down", res_down), ("w_out", res_out), ("w_in", [t.T for t in res_in_t])):
        out_g[name], out_d[name], out_m[name], out_v[name] = res

    loss = small_total[7, 8] * (0.5 / D_MODEL)
    for out, part in zip((out_g, out_d, out_m, out_v), _adamw_small(small_total, w, m, v)):
        out.update(part)

    def shaped(d):
        return [d[n].reshape(w[n].shape) for n in WEIGHT_ORDER]

    return (loss, grad_x[None], *shaped(out_g), *shaped(out_d), *shaped(out_m), *shaped(out_v))
```

```python
import math
from typing import Callable, NamedTuple

import jax
import jax.numpy as jnp
import numpy as np
from jax import lax
from jax.experimental import pallas as pl
from jax.experimental.pallas import tpu as pltpu

F32 = jnp.float32
BF16 = jnp.bfloat16

D_MODEL = 1024
HEAD_DIM = 64
Q_WIDTH = 512
KV_WIDTH = 128
CONV_WIDTH = 512
CONV_K = 3
D_FF = 4096
IN_COLS = 2304
QBLOCK = 128
ROT_DIM = 16
ROPE_THETA = 500000.0
NORM_EPS = 1e-6
NEG_INF = -1e30
N_CHIPS = 4

ADAM_LR = 0.001
ADAM_B1 = 0.9
ADAM_B2 = 0.999
ADAM_EPS = 1e-08
ADAM_WD = 0.01
ADAM_STEP = 10

H_UP, H_DOWN, H_OUT, H_IN = 512, 512, 128, 288
DOWN_EARLY_ROWS = 224

TOKEN_TILE = 512
WIDE_TOKEN_TILE = 1024
ATTN_FWD_BLOCKS = 16
ATTN_BWD_BLOCKS = 2
WGRAD_TOKEN_TILE = 4096
VMEM_LIMIT_V7X = 60 * 1024 * 1024

MESH = pl.DeviceIdType.MESH
ANY = pl.BlockSpec(memory_space=pl.ANY)
VMEM_WHOLE = pl.BlockSpec(memory_space=pltpu.VMEM)
SDS = jax.ShapeDtypeStruct


def _resident(shape):
    zeros = (0,) * len(shape)
    return pl.BlockSpec(shape, lambda *_: zeros, pipeline_mode=pl.Buffered(1))


def _rms(v):
    return lax.rsqrt(jnp.mean(v * v, axis=-1, keepdims=True) + NORM_EPS)


def _norm_bwd(dy, gain, vhat, rstd):
    t = dy * gain
    return rstd * (t - vhat * jnp.mean(t * vhat, axis=-1, keepdims=True))


def _colsum(v):
    return jnp.sum(v, axis=0, keepdims=True)


def _dot_nt(a, b):
    return lax.dot_general(a, b, (((1,), (1,)), ((), ())), preferred_element_type=F32)


def _dot_tn(a, b):
    return lax.dot_general(a, b, (((0,), (0,)), ((), ())), preferred_element_type=F32)


def _dot(a, b):
    return jnp.dot(a, b, preferred_element_type=F32)


def _chip_block(w_ref, chip):
    both = w_ref[pl.ds(2 * chip, 2)]
    return both.reshape(2 * both.shape[1], both.shape[2])


def _lane_lt64(shape):
    return lax.broadcasted_iota(jnp.int32, shape, 1) < HEAD_DIM


class _Comm(NamedTuple):
    operands: tuple
    out_shapes: tuple
    aliases: dict
    n_remote: int
    n_local: int
    plan: Callable
    after: Callable = None


def _merge(*comms):
    operands, out_shapes, aliases, parts = [], [], {}, []
    n_remote = n_local = 0
    for cm in comms:
        parts.append((len(operands), len(out_shapes), n_remote, n_local, cm))
        for k, v in cm.aliases.items():
            aliases[len(operands) + k] = len(out_shapes) + v
        operands += cm.operands
        out_shapes += cm.out_shapes
        n_remote += cm.n_remote
        n_local += cm.n_local

    def run(which, ins, outs, send, recv, loc):
        sends, recvs, locs = [], [], []
        for i0, o0, r0, l0, cm in parts:
            stage = getattr(cm, which)
            if stage is not None:
                s, r, l = stage(ins[i0:i0 + len(cm.operands)], outs[o0:o0 + len(cm.out_shapes)],
                                lambda k, r0=r0: send(r0 + k), lambda k, r0=r0: recv(r0 + k), lambda k, l0=l0: loc(l0 + k))
                sends, recvs, locs = sends + s, recvs + r, locs + l
        return sends, recvs, locs

    def plan(*args):
        return run("plan", *args)

    def after(*args):
        return run("after", *args)

    return _Comm(tuple(operands), tuple(out_shapes), aliases, n_remote, n_local, plan,
                 after if any(cm.after is not None for cm in comms) else None)


def _sem_scratch(comm):
    return [pltpu.SemaphoreType.DMA((max(comm.n_remote, 1),)), pltpu.SemaphoreType.DMA((max(comm.n_remote, 1),)),
            pltpu.SemaphoreType.DMA((max(comm.n_local, 1),))]


class _Rider(NamedTuple):
    body: Callable
    in_specs: list
    out_specs: list
    out_shape: list
    operands: tuple


def _pallas(body, *, name, grid, in_specs, out_specs, out_shape, operands, scratch=(), comm=None, rider=None):
    params = pltpu.CompilerParams(dimension_semantics=("arbitrary",) * len(grid), vmem_limit_bytes=VMEM_LIMIT_V7X)
    if rider is not None:
        own_in, own_out, ride_in, ride_out = len(in_specs), len(out_specs), len(rider.in_specs), len(rider.out_specs)
        own_body = body

        def body(*refs):
            o0 = own_in + ride_in
            s0 = o0 + own_out + ride_out
            own_body(*refs[:own_in], *refs[o0:o0 + own_out], *refs[s0:])
            first = None
            for axis in range(len(grid)):
                at_start = pl.program_id(axis) == 0
                first = at_start if first is None else jnp.logical_and(first, at_start)
            rider.body(first, *refs[own_in:o0], *refs[o0 + own_out:s0])

        in_specs, out_specs = list(in_specs) + rider.in_specs, list(out_specs) + rider.out_specs
        out_shape, operands = list(out_shape) + rider.out_shape, tuple(operands) + tuple(rider.operands)
    if comm is None:
        return pl.pallas_call(body, name=name, grid=grid, in_specs=in_specs, out_specs=out_specs, out_shape=out_shape,
                              scratch_shapes=list(scratch), compiler_params=params)(*operands)
    n_in, n_out, n_scr = len(in_specs), len(out_specs), len(scratch)
    c_in, c_out = len(comm.operands), len(comm.out_shapes)

    def with_comm(*refs):
        ins, c_ins = refs[:n_in], refs[n_in:n_in + c_in]
        o0 = n_in + c_in
        outs, c_outs = refs[o0:o0 + n_out], refs[o0 + n_out:o0 + n_out + c_out]
        s0 = o0 + n_out + c_out
        scr = refs[s0:s0 + n_scr]
        send_sems, recv_sems, local_sems = refs[s0 + n_scr:]
        first = last = None
        for axis, size in enumerate(grid):
            at_start, at_end = pl.program_id(axis) == 0, pl.program_id(axis) == size - 1
            first = at_start if first is None else jnp.logical_and(first, at_start)
            last = at_end if last is None else jnp.logical_and(last, at_end)

        def copies():
            return comm.plan(c_ins, c_outs, lambda k: send_sems.at[k], lambda k: recv_sems.at[k],
                             lambda k: local_sems.at[k])

        @pl.when(first)
        def _():
            sends, _, locs = copies()
            for cp in sends + locs:
                cp.start()

        body(*ins, *outs, *scr)

        @pl.when(last)
        def _():
            sends, recvs, locs = copies()
            for cp in recvs:
                cp.wait_recv()
            for cp in sends:
                cp.wait_send()
            for cp in locs:
                cp.wait()
            if comm.after is not None:
                sends, recvs, _ = comm.after(c_ins, c_outs, lambda k: send_sems.at[k], lambda k: recv_sems.at[k],
                                             lambda k: local_sems.at[k])
                for cp in sends:
                    cp.start()
                for cp in recvs:
                    cp.wait_recv()
                for cp in sends:
                    cp.wait_send()

    return pl.pallas_call(
        with_comm, name=name, grid=grid,
        in_specs=list(in_specs) + [ANY] * c_in, out_specs=list(out_specs) + [ANY] * c_out,
        out_shape=list(out_shape) + list(comm.out_shapes),
        scratch_shapes=list(scratch) + _sem_scratch(comm),
        input_output_aliases={n_in + k: n_out + v for k, v in comm.aliases.items()},
        compiler_params=params)(*operands, *comm.operands)


def _place():
    return lax.axis_index("x"), lax.axis_index("y"), lax.axis_index("c")


def _other_chips(x, y):
    return [(1 - x, y), (x, 1 - y), (1 - x, 1 - y)]


def _slot(px, py, pc):
    return 4 * px + 2 * py + pc


def _remote(src, dst, send_sem, recv_sem, to):
    return pltpu.make_async_remote_copy(src_ref=src, dst_ref=dst, send_sem=send_sem, recv_sem=recv_sem,
                                        device_id=to, device_id_type=MESH)


def _relay_pieces(full, rows, x, y, c):
    start, half = rows[0], rows[1] // 2
    upper, lower = pl.ds(start, half), pl.ds(start + half, half)
    diagonal = full.at[_slot(1 - x, 1 - y, c)]
    return [(full.at[_slot(1 - x, y, c), upper], diagonal.at[upper], (x, 1 - y, c)),
            (full.at[_slot(x, 1 - y, c), lower], diagonal.at[lower], (1 - x, y, c))]


def _relay(half_block, so_far, first=None, second=None, third=None, third_after=None):
    has_block, has_buffer = half_block is not None, so_far is not None
    shape = so_far.shape if has_buffer else (2 * N_CHIPS,) + half_block.shape
    dtype = so_far.dtype if has_buffer else half_block.dtype

    def third_leg(rows, k, ins, outs, send, recv):
        src, full = (ins[-1] if has_buffer else outs[0]), outs[0]
        x, y, c = _place()
        span, sibling = pl.ds(*rows), (x, y, 1 - c)
        here, there = _slot(1 - x, 1 - y, c), _slot(1 - x, 1 - y, 1 - c)
        return ([_remote(src.at[here, span], full.at[here, span], send(k), recv(k), sibling)],
                [_remote(src.at[there, span], full.at[there, span], send(k), recv(k), sibling)])

    def plan(ins, outs, send, recv, loc):
        src, full = (ins[-1] if has_buffer else outs[0]), outs[0]
        x, y, c = _place()
        sibling = (x, y, 1 - c)
        sends, recvs, locs = [], [], []
        if first is not None:
            span = pl.ds(*first)
            blk, mine = ins[0].at[span], full.at[_slot(x, y, c), span]
            for k, peer in enumerate([sibling, (1 - x, y, c), (x, 1 - y, c)]):
                sends.append(_remote(blk, mine, send(k), recv(k), peer))
                recvs.append(_remote(blk, full.at[_slot(*peer), span], send(k), recv(k), peer))
            locs.append(pltpu.make_async_copy(blk, mine, loc(0)))
        if second is not None:
            span = pl.ds(*second)
            for k, chip in enumerate([(1 - x, y), (x, 1 - y)]):
                sends.append(_remote(src.at[_slot(*chip, c), span], full.at[_slot(*chip, c), span], send(3 + k), recv(3 + k),
                                     sibling))
                recvs.append(_remote(src.at[_slot(*chip, 1 - c), span], full.at[_slot(*chip, 1 - c), span], send(3 + k),
                                     recv(3 + k), sibling))
            for k, (piece, lands, peer) in enumerate(_relay_pieces(full, second, x, y, c)):
                sends.append(_remote(piece, piece, send(5 + k), recv(5 + k), peer))
                recvs.append(_remote(lands, lands, send(5 + k), recv(5 + k), peer))
        if third is not None:
            s, r = third_leg(third, 7, ins, outs, send, recv)
            sends, recvs = sends + s, recvs + r
        return sends, recvs, locs

    def after(ins, outs, send, recv, loc):
        s, r = third_leg(third_after, 8, ins, outs, send, recv)
        return s, r, []

    operands = ((half_block,) if has_block else ()) + ((so_far,) if has_buffer else ())
    return _Comm(operands, (SDS(shape, dtype),), {len(operands) - 1: 0} if has_buffer else {}, 9, 1, plan,
                 after if third_after is not None else None)


def _gather_whole(first, others, small_block):
    shards = (first, *others)
    n = len(shards)
    hs = [s.shape[0] // 2 for s in shards]
    rows = hs[0]

    def body(*refs):
        src, small_ref = refs[:n], refs[n]
        out_ref, small_out_ref, half_out = refs[n + 1], refs[n + 2], refs[n + 3:2 * n + 2]
        stage, half = refs[2 * n + 2:3 * n + 2], refs[3 * n + 2:4 * n + 2]
        send_sems, recv_sems, local_sems = refs[4 * n + 2:]
        x, y, c = _place()
        me, sibling = (x, y, c), (x, y, 1 - c)
        neighbours, diagonal = [(1 - x, y), (x, 1 - y)], (1 - x, 1 - y)
        loads = [pltpu.make_async_copy(src[k].at[pl.ds(c * hs[k], hs[k])], stage[k], local_sems.at[2 + k]) for k in range(n)]
        loads[0].start()
        loads[0].wait()
        for cp in loads[1:]:
            cp.start()
        blk_ref = half[0]
        blk_ref[...] = stage[0][...].astype(BF16)

        def copy(k, block, to, src=None):
            return _remote(out_ref.at[_slot(*block)] if src is None else src, out_ref.at[_slot(*block)],
                           send_sems.at[k], recv_sems.at[k], to)

        def small_copy(k, chip, to):
            return _remote(small_ref, small_out_ref.at[2 * chip[0] + chip[1]], send_sems.at[8 + k], recv_sems.at[8 + k], to)

        mine = pltpu.make_async_copy(blk_ref, out_ref.at[_slot(*me)], local_sems.at[0])
        mine_small = pltpu.make_async_copy(small_ref, small_out_ref.at[2 * x + y], local_sems.at[1])
        mine.start()
        mine_small.start()
        started = [copy(0, me, sibling, src=blk_ref)]
        started += [copy(1 + k, me, (*chip, c), src=blk_ref) for k, chip in enumerate(neighbours)]
        started += [small_copy(k, (x, y), (*chip, c)) for k, chip in enumerate(neighbours + [diagonal])]
        for cp in started:
            cp.start()
        stores = []
        for k in range(1, n):
            loads[k].wait()
            half[k][...] = stage[k][...].astype(BF16)
            stores.append(pltpu.make_async_copy(half[k], half_out[k - 1], local_sems.at[2 + n + k]))
            stores[-1].start()
        pieces = _relay_pieces(out_ref, (0, rows), x, y, c)
        for k, chip in enumerate(neighbours):
            copy(1 + k, (*chip, c), me).wait_recv()
            piece, _, peer = pieces[k]
            started += [copy(3 + k, (*chip, c), sibling), _remote(piece, piece, send_sems.at[5 + k], recv_sems.at[5 + k], peer)]
            started[-2].start()
            started[-1].start()
        for k, (_, lands, peer) in enumerate(pieces):
            _remote(lands, lands, send_sems.at[5 + k], recv_sems.at[5 + k], peer).wait_recv()
        started.append(copy(7, (*diagonal, c), sibling))
        started[-1].start()
        copy(0, sibling, me).wait_recv()
        for k, chip in enumerate(neighbours):
            copy(3 + k, (*chip, 1 - c), me).wait_recv()
        copy(7, (*diagonal, 1 - c), me).wait_recv()
        for k, chip in enumerate(neighbours + [diagonal]):
            small_copy(k, chip, me).wait_recv()
        for cp in started:
            cp.wait_send()
        mine.wait()
        mine_small.wait()
        for cp in stores:
            cp.wait()

    return pl.pallas_call(
        body, name="gather_whole", in_specs=[ANY] * (n + 1), out_specs=[ANY] * (n + 1),
        out_shape=[SDS((2 * N_CHIPS, rows, D_MODEL), BF16), SDS((N_CHIPS,) + small_block.shape, small_block.dtype)]
                  + [SDS((h, D_MODEL), BF16) for h in hs[1:]],
        scratch_shapes=[pltpu.VMEM((h, D_MODEL), F32) for h in hs] + [pltpu.VMEM((h, D_MODEL), BF16) for h in hs]
                       + [pltpu.SemaphoreType.DMA((11,)), pltpu.SemaphoreType.DMA((11,)), pltpu.SemaphoreType.DMA((2 + 2 * n,))],
        compiler_params=pltpu.CompilerParams(vmem_limit_bytes=VMEM_LIMIT_V7X),
    )(*shards, small_block)


def _pair_send(grads):
    def plan(ins, outs, send, recv, loc):
        (g,), (got,) = ins, outs
        x, y, c = _place()
        copies = [_remote(g.at[j, 1 - c], got.at[j], send(j), recv(j), (x, y, 1 - c)) for j in range(N_CHIPS)]
        return copies, copies, []

    shape = (grads.shape[0],) + grads.shape[2:]
    return _Comm((grads,), (SDS(shape, grads.dtype),), {}, N_CHIPS, 0, plan)


def _chip_exchange(partial):
    def plan(ins, outs, send, recv, loc):
        (p,), (got,) = ins, outs
        x, y, c = _place()
        my_chip = 2 * x + y
        chips = _other_chips(x, y)
        sends = [_remote(p.at[2 * chip[0] + chip[1]], got.at[my_chip], send(j), recv(j), (*chip, c))
                 for j, chip in enumerate(chips)]
        recvs = [_remote(p.at[my_chip], got.at[2 * chip[0] + chip[1]], send(j), recv(j), (*chip, c))
                 for j, chip in enumerate(chips)]
        return sends, recvs, [pltpu.make_async_copy(p.at[my_chip], got.at[my_chip], loc(0))]

    return _Comm((partial,), (SDS(partial.shape, partial.dtype),), {}, 3, 1, plan)


def _pair_sum(name, core, grads, received):
    h = grads.shape[2]

    def body(core_ref, g_ref, r_ref, o_ref):
        o_ref[...] = (g_ref[0] + r_ref[...]).astype(BF16)

    return pl.pallas_call(
        body, name=name,
        grid_spec=pltpu.PrefetchScalarGridSpec(
            num_scalar_prefetch=1, grid=(N_CHIPS,),
            in_specs=[pl.BlockSpec((1, 1, h, D_MODEL), lambda j, core_ref: (j, core_ref[0], 0, 0)),
                      pl.BlockSpec((1, h, D_MODEL), lambda j, core_ref: (j, 0, 0))],
            out_specs=pl.BlockSpec((1, h, D_MODEL), lambda j, core_ref: (j, 0, 0))),
        out_shape=SDS((N_CHIPS, h, D_MODEL), BF16),
        compiler_params=pltpu.CompilerParams(dimension_semantics=("arbitrary",), vmem_limit_bytes=VMEM_LIMIT_V7X),
    )(core, grads, received)


SMALL_ROWS = 8


def _sum_blocks(ref):
    return (ref[0].astype(F32) + ref[1].astype(F32)) + (ref[2].astype(F32) + ref[3].astype(F32))


def _tail_reduce(last_grads, exchanged, small):
    n = len(exchanged)
    h = last_grads.shape[2]

    def body(*refs):
        g_ref, ex, small_ref = refs[0], refs[1:1 + n], refs[1 + n]
        o0 = 2 + n
        out, out_last, small_out = refs[o0:o0 + n], refs[o0 + n], refs[o0 + n + 1]
        s0 = o0 + n + 2
        halves, half_last = refs[s0:s0 + n], refs[s0 + n]
        own, got, part, exch, small_buf = refs[s0 + n + 1:s0 + n + 6]
        ex_buf = refs[s0 + n + 6:s0 + 2 * n + 6]
        pair_send, pair_recv, chip_send, chip_recv, share_send, share_recv, small_send, small_recv, local_sems = refs[s0 + 2 * n + 6:]
        x, y, c = _place()
        sibling = (x, y, 1 - c)
        my_chip, me = 2 * x + y, _slot(x, y, c)
        chips = _other_chips(x, y)[::-1]

        order = [2 * chip[0] + chip[1] for chip in chips] + [my_chip]
        to_sibling = [_remote(g_ref.at[j, 1 - c], got.at[j], pair_send.at[j], pair_recv.at[j], sibling) for j in order]
        load_own = [pltpu.make_async_copy(g_ref.at[j, c], own.at[j], local_sems.at[j]) for j in order]
        load_ex = [pltpu.make_async_copy(ex[k], ex_buf[k], local_sems.at[N_CHIPS + n + 1 + k]) for k in range(n)]
        for give, keep in zip(to_sibling, load_own):
            give.start()
            keep.start()
        for cp in load_ex:
            cp.start()

        small_buf[me] = small_ref[...]
        small_copies = []
        for mask in range(1, 8):
            peer = (x ^ (mask >> 2), y ^ ((mask >> 1) & 1), c ^ (mask & 1))
            small_copies.append(_remote(small_ref, small_buf.at[me], small_send.at[mask - 1], small_recv.at[mask - 1], peer))
        for cp in small_copies:
            cp.start()

        def share(k, half_ref, out_ref):
            keep = pltpu.make_async_copy(half_ref, out_ref.at[c], local_sems.at[N_CHIPS + k])
            give = _remote(half_ref, out_ref.at[c], share_send.at[k], share_recv.at[k], sibling)
            take = _remote(half_ref, out_ref.at[1 - c], share_send.at[k], share_recv.at[k], sibling)
            keep.start()
            give.start()
            return keep, give, take

        def pair_sum(block):
            _remote(g_ref.at[block, 1 - c], got.at[block], pair_send.at[block], pair_recv.at[block], sibling).wait_recv()
            pltpu.make_async_copy(g_ref.at[block, c], own.at[block], local_sems.at[block]).wait()
            part[block] = (own[block] + got[block]).astype(BF16)

        to_chips = []
        for j, chip in enumerate(chips):
            block = 2 * chip[0] + chip[1]
            pair_sum(block)
            to_chips.append(_remote(part.at[block], exch.at[my_chip], chip_send.at[j], chip_recv.at[j], (*chip, c)))
            to_chips[-1].start()
        pair_sum(my_chip)
        exch[my_chip] = part[my_chip]
        from_chips = [_remote(part.at[my_chip], exch.at[2 * chip[0] + chip[1]], chip_send.at[j], chip_recv.at[j], (*chip, c))
                      for j, chip in enumerate(chips)]

        shares = []
        for k in range(n):
            load_ex[k].wait()
            halves[k][...] = _sum_blocks(ex_buf[k])
            shares.append(share(k, halves[k], out[k]))

        for cp in small_copies:
            cp.wait_recv()
        total = small_buf[0]
        for d in range(1, 8):
            total = total + small_buf[d]
        small_out[...] = total

        for cp in from_chips:
            cp.wait_recv()
        half_last[...] = _sum_blocks(exch)
        shares.append(share(n, half_last, out_last))

        for keep, give, take in shares:
            take.wait_recv()
            give.wait_send()
            keep.wait()
        for cp in to_sibling + to_chips + small_copies:
            cp.wait_send()

    blocks = (N_CHIPS, h, D_MODEL)
    return pl.pallas_call(
        body, name="tail_reduce",
        in_specs=[ANY] * (n + 1) + [VMEM_WHOLE], out_specs=[ANY] * (n + 1) + [VMEM_WHOLE],
        out_shape=[SDS((2,) + e.shape[1:], F32) for e in exchanged] + [SDS((2, h, D_MODEL), F32), SDS(small.shape, F32)],
        scratch_shapes=[pltpu.VMEM(e.shape[1:], F32) for e in exchanged] + [pltpu.VMEM((h, D_MODEL), F32)]
                       + [pltpu.VMEM(blocks, F32), pltpu.VMEM(blocks, F32), pltpu.VMEM(blocks, BF16), pltpu.VMEM(blocks, BF16),
                          pltpu.VMEM((8,) + small.shape, F32)]
                       + [pltpu.VMEM(e.shape, BF16) for e in exchanged]
                       + [pltpu.SemaphoreType.DMA((N_CHIPS,)), pltpu.SemaphoreType.DMA((N_CHIPS,)),
                          pltpu.SemaphoreType.DMA((3,)), pltpu.SemaphoreType.DMA((3,)),
                          pltpu.SemaphoreType.DMA((n + 1,)), pltpu.SemaphoreType.DMA((n + 1,)),
                          pltpu.SemaphoreType.DMA((7,)), pltpu.SemaphoreType.DMA((7,)),
                          pltpu.SemaphoreType.DMA((N_CHIPS + 2 * n + 1,))],
        compiler_params=pltpu.CompilerParams(vmem_limit_bytes=VMEM_LIMIT_V7X),
    )(last_grads, *exchanged, small)


def _rope_expansion():
    half = ROT_DIM // 2
    expand = np.zeros((2 * half, 3 * 128), np.float32)
    const = np.zeros((1, 3 * 128), np.float32)
    for lane in range(128):
        d = lane % HEAD_DIM
        if d < ROT_DIM:
            expand[d % half, lane] = 1.0
        else:
            const[0, lane] = 1.0
        if d < half:
            expand[half + d, 128 + lane] = -1.0
        elif d < ROT_DIM:
            expand[half + d - half, 256 + lane] = 1.0
    return expand, const


ROPE_PIECES = 3 * ROT_DIM


def _rope_inputs(seq):
    pos = jnp.arange(seq, dtype=F32)
    inv_freq = ROPE_THETA ** (-jnp.arange(0, ROT_DIM, 2, dtype=F32) / ROT_DIM)
    ang = pos[:, None] * inv_freq[None, :]
    cs = jnp.concatenate([jnp.cos(ang), jnp.sin(ang)], axis=1)
    hi = lax.reduce_precision(cs, 8, 7)
    mid = lax.reduce_precision(cs - hi, 8, 7)
    low = cs - hi - mid
    expand, const = _rope_expansion()
    pieces = jnp.concatenate([hi, mid, low], axis=1).astype(BF16)
    return pieces, jnp.asarray(np.concatenate([expand] * 3, axis=0), BF16), jnp.asarray(const)


def _rope_specs(tb):
    return [pl.BlockSpec((tb, ROPE_PIECES), lambda i: (i, 0)), _resident((ROPE_PIECES, 3 * 128)), _resident((1, 3 * 128))]


def _rope_tile(pieces_ref, expand_ref, const_ref):
    tables = _dot(pieces_ref[...], expand_ref[...]) + const_ref[...]
    return tables[:, 0:128], tables[:, 128:256], tables[:, 256:384]


def _rope(t, c, sa, sb):
    half = ROT_DIM // 2
    return t * c + pltpu.roll(t, 128 - half, 1) * sa + pltpu.roll(t, half, 1) * sb


def _rope_transposed(dt, c, sa, sb):
    half = ROT_DIM // 2
    return dt * c + pltpu.roll(dt * sa, half, 1) + pltpu.roll(dt * sb, 128 - half, 1)


def _in_proj(x, g_pre, w_in_t, rope, comm=None):
    seq = x.shape[0]
    tb = min(seq, WIDE_TOKEN_TILE)

    def body(x_ref, g_ref, w_ref, c_ref, sa_ref, sb_ref,
             q_ref, kd0_ref, kd1_ref, vd0_ref, vd1_ref, gb_ref, gc_ref, xin_ref, hn_ref):
        xv = x_ref[...]
        hn = (xv * _rms(xv) * g_ref[...]).astype(BF16)
        hn_ref[...] = hn
        proj = _dot_nt(hn, w_ref[...].reshape(IN_COLS, D_MODEL))
        c, sa, sb = _rope_tile(c_ref, sa_ref, sb_ref)
        scale = 1.0 / math.sqrt(HEAD_DIM)
        for p in range(Q_WIDTH // 128):
            q_ref[:, 128 * p:128 * (p + 1)] = (_rope(proj[:, 128 * p:128 * (p + 1)], c, sa, sb) * scale).astype(BF16)
        k = _rope(proj[:, Q_WIDTH:Q_WIDTH + KV_WIDTH], c, sa, sb)
        v = proj[:, Q_WIDTH + KV_WIDTH:Q_WIDTH + 2 * KV_WIDTH]
        low = _lane_lt64(k.shape)
        k_sw, v_sw = pltpu.roll(k, HEAD_DIM, 1), pltpu.roll(v, HEAD_DIM, 1)
        kd0_ref[...] = jnp.where(low, k, k_sw).astype(BF16)
        kd1_ref[...] = jnp.where(low, k_sw, k).astype(BF16)
        vd0_ref[...] = jnp.where(low, v, v_sw).astype(BF16)
        vd1_ref[...] = jnp.where(low, v_sw, v).astype(BF16)
        base = Q_WIDTH + 2 * KV_WIDTH
        gb_ref[...] = proj[:, base:base + CONV_WIDTH].astype(BF16)
        gc_ref[...] = proj[:, base + CONV_WIDTH:base + 2 * CONV_WIDTH].astype(BF16)
        xin_ref[...] = proj[:, base + 2 * CONV_WIDTH:base + 3 * CONV_WIDTH].astype(BF16)

    tile = lambda w: pl.BlockSpec((tb, w), lambda i: (i, 0))
    return _pallas(
        body, name="in_proj", grid=(seq // tb,),
        in_specs=[tile(D_MODEL), _resident((1, D_MODEL)), _resident(w_in_t.shape), *_rope_specs(tb)],
        out_specs=[tile(Q_WIDTH), tile(128), tile(128), tile(128), tile(128),
                   tile(CONV_WIDTH), tile(CONV_WIDTH), tile(CONV_WIDTH), tile(D_MODEL)],
        out_shape=[SDS((seq, Q_WIDTH), BF16)] + [SDS((seq, 128), BF16)] * 4
                  + [SDS((seq, CONV_WIDTH), BF16)] * 3 + [SDS((seq, D_MODEL), BF16)],
        operands=(x, g_pre, w_in_t, *rope), comm=comm)


def _attn_valid(i):
    shape = (4 * QBLOCK, 2 * QBLOCK)
    row = lax.broadcasted_iota(jnp.int32, shape, 0)
    col = lax.broadcasted_iota(jnp.int32, shape, 1)
    qi = row & (QBLOCK - 1)
    return (col > qi) & (col <= qi + QBLOCK) & ((col >= QBLOCK) | (i > 0))


def _stack_heads(pair0, pair1):
    low = _lane_lt64(pair0.shape)
    zero = jnp.zeros_like(pair0)
    return jnp.concatenate([jnp.where(low, pair0, zero), jnp.where(low, zero, pair0),
                            jnp.where(low, pair1, zero), jnp.where(low, zero, pair1)], axis=0)


def _unstack_heads(stacked):
    low = _lane_lt64((QBLOCK, 128))
    pair0 = jnp.where(low, stacked[0:QBLOCK], stacked[QBLOCK:2 * QBLOCK])
    pair1 = jnp.where(low, stacked[2 * QBLOCK:3 * QBLOCK], stacked[3 * QBLOCK:4 * QBLOCK])
    return pair0, pair1


def _sink_column(sink_ref, kv_head):
    row = lax.broadcasted_iota(jnp.int32, (4 * QBLOCK, 1), 0)
    s = [sink_ref[0, 4 * kv_head + j] for j in range(4)]
    return jnp.where(row < QBLOCK, s[0], jnp.where(row < 2 * QBLOCK, s[1], jnp.where(row < 3 * QBLOCK, s[2], s[3])))


def _band(ref, i):
    prev = pl.multiple_of(jnp.maximum(i - 1, 0) * QBLOCK, QBLOCK)
    own = pl.multiple_of(i * QBLOCK, QBLOCK)
    return jnp.concatenate([ref[pl.ds(prev, QBLOCK), :], ref[pl.ds(own, QBLOCK), :]], axis=0), prev, own


def _softmax_with_sink(s, sink_col):
    m = jnp.maximum(jnp.max(s, axis=-1, keepdims=True), sink_col)
    p = jnp.exp(s - m)
    e_sink = jnp.exp(sink_col - m)
    inv_l = 1.0 / (jnp.sum(p, axis=-1, keepdims=True) + e_sink)
    return p, e_sink, inv_l


def _attention_fwd(q, kd0, kd1, vd0, vd1, sinks, comm=None):
    seq = q.shape[0]

    nb = ATTN_FWD_BLOCKS

    def body(sink_ref, q_ref, kd0_ref, kd1_ref, vd0_ref, vd1_ref, o_ref):
        for b in range(nb):
            i = pl.program_id(0) * nb + b
            rows = slice(QBLOCK * b, QBLOCK * (b + 1))
            valid = _attn_valid(i)
            for kv_head, (k_ref, v_ref) in enumerate(((kd0_ref, vd0_ref), (kd1_ref, vd1_ref))):
                kband, _, _ = _band(k_ref, i)
                vband, _, _ = _band(v_ref, i)
                base = 256 * kv_head
                qm = _stack_heads(q_ref[rows, base:base + 128], q_ref[rows, base + 128:base + 256])
                s = jnp.where(valid, _dot_nt(qm, kband), NEG_INF)
                p, _, inv_l = _softmax_with_sink(s, _sink_column(sink_ref, kv_head))
                o = _dot(p.astype(BF16), vband) * inv_l
                pair0, pair1 = _unstack_heads(o)
                o_ref[rows, base:base + 128] = pair0.astype(BF16)
                o_ref[rows, base + 128:base + 256] = pair1.astype(BF16)

    blk = pl.BlockSpec((nb * QBLOCK, Q_WIDTH), lambda i: (i, 0))
    full = _resident((seq, 128))
    return _pallas(
        body, name="attention_fwd", grid=(seq // (nb * QBLOCK),),
        in_specs=[pl.BlockSpec(memory_space=pltpu.SMEM), blk, full, full, full, full],
        out_specs=[blk], out_shape=[SDS((seq, Q_WIDTH), BF16)],
        operands=(sinks, q, kd0, kd1, vd0, vd1), comm=comm)


HALO = 16


def _conv_parts(gc, xin, gc_halo, xin_halo, conv_w, first):
    tb = gc.shape[0]
    u = gc.astype(F32) * xin.astype(F32)
    u_halo = jnp.where(first, 0.0, gc_halo.astype(F32) * xin_halo.astype(F32))
    ext = jnp.concatenate([u_halo, u], axis=0)
    u1 = pltpu.roll(ext, 1, 0)[HALO:HALO + tb]
    u2 = pltpu.roll(ext, 2, 0)[HALO:HALO + tb]
    y = conv_w[0:1, :] * u2 + conv_w[1:2, :] * u1 + conv_w[2:3, :] * u
    return u, u1, u2, y


def _halo_prev(tb, w):
    return pl.BlockSpec((HALO, w), lambda i: (jnp.maximum(i * (tb // HALO) - 1, 0), 0))


def _residual_mid(x, mix, g_post_mix):
    mix_f = mix.astype(F32)
    return x + mix_f * _rms(mix_f) * g_post_mix


def _mix_out(attn, gb, gc, xin, conv_w, g_attn, g_conv, w_out, comm=None):
    seq = attn.shape[0]
    tb = min(seq, WIDE_TOKEN_TILE)

    def body(a_ref, gb_ref, gc_ref, xin_ref, gch_ref, xinh_ref, cw_ref, ga_ref, gcn_ref, w_ref, mix_ref, mixed_ref):
        first = pl.program_id(0) == 0
        _, _, _, y = _conv_parts(gc_ref[...], xin_ref[...], gch_ref[...], xinh_ref[...], cw_ref[...], first)
        conv = gb_ref[...].astype(F32) * y
        a = a_ref[...].astype(F32)
        mixed_ref[:, 0:Q_WIDTH] = (a * _rms(a) * ga_ref[...]).astype(BF16)
        mixed_ref[:, Q_WIDTH:] = (conv * _rms(conv) * gcn_ref[...]).astype(BF16)
        mix_ref[...] = _dot(mixed_ref[...], w_ref[...].reshape(D_MODEL, D_MODEL)).astype(BF16)

    tile = lambda w: pl.BlockSpec((tb, w), lambda i: (i, 0))
    return _pallas(
        body, name="mix_out", grid=(seq // tb,),
        in_specs=[tile(Q_WIDTH), tile(CONV_WIDTH), tile(CONV_WIDTH), tile(CONV_WIDTH),
                  _halo_prev(tb, CONV_WIDTH), _halo_prev(tb, CONV_WIDTH),
                  _resident((CONV_K, CONV_WIDTH)), _resident((1, Q_WIDTH)), _resident((1, CONV_WIDTH)),
                  _resident(w_out.shape)],
        out_specs=[tile(D_MODEL), tile(D_MODEL)],
        out_shape=[SDS((seq, D_MODEL), BF16), SDS((seq, D_MODEL), BF16)],
        operands=(attn, gb, gc, xin, gc, xin, conv_w, g_attn, g_conv, w_out), comm=comm)


def _mlp_fwd_bwd(x, mix, target, g_post_mix, g_pre_mlp, g_post_mlp, w_up, w_down):
    seq = x.shape[0]
    tb = TOKEN_TILE

    def body(x_ref, mix_ref, t_ref, gpm_ref, g2_ref, g4_ref, wup_ref, wdown_ref,
             up_ref, hn2_ref, dmlp_ref, dup_ref, dh_ref, dmix_ref, loss_ref, dg4_ref, dg2_ref, dgpm_ref):
        @pl.when(pl.program_id(0) == 0)
        def _():
            for ref in (loss_ref, dg4_ref, dg2_ref, dgpm_ref):
                ref[...] = jnp.zeros_like(ref)

        halves = [slice(0, tb // 2), slice(tb // 2, tb)]
        chunks = [slice(1024 * j, 1024 * (j + 1)) for j in range(N_CHIPS)]
        hv, hn2, mlp, dout, dmlp, dhn2 = [], [], [], [], [], []
        for rows in halves:
            hv.append(_residual_mid(x_ref[rows, :], mix_ref[rows, :], gpm_ref[...]))
            hn2.append((hv[-1] * _rms(hv[-1]) * g2_ref[...]).astype(BF16))
            hn2_ref[rows, :] = hn2[-1]
        for k, rows in enumerate(halves):
            acc = None
            for j, cols in enumerate(chunks):
                up = jnp.maximum(_dot(hn2[k], _chip_block(wup_ref, j)), 0.0)
                up_ref[rows, cols] = up.astype(BF16)
                part = _dot((up * up).astype(BF16), _chip_block(wdown_ref, j))
                acc = part if acc is None else acc + part
            mlp.append(acc)
        loss = jnp.zeros((1, 1), F32)
        dg4 = jnp.zeros((1, D_MODEL), F32)
        for k, rows in enumerate(halves):
            rstd = _rms(mlp[k])
            zhat = mlp[k] * rstd
            diff = hv[k] + zhat * g4_ref[...] - t_ref[rows, :]
            loss = loss + jnp.sum(jnp.sum(diff * diff, axis=1, keepdims=True), axis=0, keepdims=True)
            dout.append(diff * (1.0 / D_MODEL))
            dg4 = dg4 + _colsum(dout[k] * zhat)
            dmlp.append(_norm_bwd(dout[k], g4_ref[...], zhat, rstd).astype(BF16))
            dmlp_ref[rows, :] = dmlp[k]
        for k, rows in enumerate(halves):
            acc = None
            for j, cols in enumerate(chunks):
                dact = _dot_nt(dmlp[k], _chip_block(wdown_ref, j))
                dup = (dact * (2.0 * up_ref[rows, cols].astype(F32))).astype(BF16)
                dup_ref[rows, cols] = dup
                part = _dot_nt(dup, _chip_block(wup_ref, j))
                acc = part if acc is None else acc + part
            dhn2.append(acc)
        dg2 = jnp.zeros((1, D_MODEL), F32)
        dgpm = jnp.zeros((1, D_MODEL), F32)
        for k, rows in enumerate(halves):
            r2 = _rms(hv[k])
            hhat = hv[k] * r2
            dg2 = dg2 + _colsum(dhn2[k] * hhat)
            dh = dout[k] + _norm_bwd(dhn2[k], g2_ref[...], hhat, r2)
            dh_ref[rows, :] = dh.astype(BF16)
            mix_v = mix_ref[rows, :].astype(F32)
            rz = _rms(mix_v)
            zhat = mix_v * rz
            dgpm = dgpm + _colsum(dh * zhat)
            dmix_ref[rows, :] = _norm_bwd(dh, gpm_ref[...], zhat, rz).astype(BF16)
        loss_ref[...] += loss
        dg4_ref[...] += dg4
        dg2_ref[...] += dg2
        dgpm_ref[...] += dgpm

    tile = lambda w: pl.BlockSpec((tb, w), lambda i: (i, 0))
    vec = pl.BlockSpec((1, D_MODEL), lambda i: (0, 0))
    return _pallas(
        body, name="mlp_fwd_bwd", grid=(seq // tb,),
        in_specs=[tile(D_MODEL), tile(D_MODEL), tile(D_MODEL), _resident((1, D_MODEL)), _resident((1, D_MODEL)),
                  _resident((1, D_MODEL)), _resident(w_up.shape), _resident(w_down.shape)],
        out_specs=[tile(D_FF), tile(D_MODEL), tile(D_MODEL), tile(D_FF), tile(D_MODEL), tile(D_MODEL),
                   pl.BlockSpec((1, 1), lambda i: (0, 0)), vec, vec, vec],
        out_shape=[SDS((seq, D_FF), BF16), SDS((seq, D_MODEL), BF16), SDS((seq, D_MODEL), BF16), SDS((seq, D_FF), BF16),
                   SDS((seq, D_MODEL), BF16), SDS((seq, D_MODEL), BF16),
                   SDS((1, 1), F32), SDS((1, D_MODEL), F32), SDS((1, D_MODEL), F32), SDS((1, D_MODEL), F32)],
        operands=(x, mix, target, g_post_mix, g_pre_mlp, g_post_mlp, w_up, w_down))


def _mix_bwd(dmix, attn, gb, gc, xin, conv_w, g_attn, g_conv, w_out, n_k):
    seq = attn.shape[0]
    tb = seq // (N_CHIPS * n_k)

    def body(first, dmix_ref, a_ref, gb_ref, gc_ref, xin_ref, gch_ref, xinh_ref, cw_ref, ga_ref, gcn_ref, w_ref,
             dattn_ref, dgb_ref, dy_ref, dga_ref, dgcn_ref, dcw_ref):
        @pl.when(first)
        def _():
            dga_ref[...] = jnp.zeros_like(dga_ref)
            dgcn_ref[...] = jnp.zeros_like(dgcn_ref)
            dcw_ref[...] = jnp.zeros_like(dcw_ref)

        dmixed = _dot_nt(dmix_ref[...], w_ref[...].reshape(D_MODEL, D_MODEL))
        a = a_ref[...].astype(F32)
        ra = _rms(a)
        ahat = a * ra
        dan = dmixed[:, 0:Q_WIDTH]
        dga_ref[...] += _colsum(dan * ahat)
        dattn_ref[...] = _norm_bwd(dan, ga_ref[...], ahat, ra).astype(BF16)
        gbv = gb_ref[...].astype(F32)
        u, u1, u2, y = _conv_parts(gc_ref[...], xin_ref[...], gch_ref[...], xinh_ref[...], cw_ref[...], first)
        conv = gbv * y
        rc = _rms(conv)
        chat = conv * rc
        dcn = dmixed[:, Q_WIDTH:]
        dgcn_ref[...] += _colsum(dcn * chat)
        dconv = _norm_bwd(dcn, gcn_ref[...], chat, rc)
        dgb_ref[...] = (dconv * y).astype(BF16)
        dy = dconv * gbv
        dy_ref[...] = dy.astype(BF16)
        dcw_ref[0:1, :] += _colsum(dy * u2)
        dcw_ref[1:2, :] += _colsum(dy * u1)
        dcw_ref[2:3, :] += _colsum(dy * u)

    tile = lambda w: pl.BlockSpec((tb, w), lambda j, k: (j * n_k + k, 0))
    halo = lambda w: pl.BlockSpec((HALO, w), lambda j, k: (jnp.maximum((j * n_k + k) * (tb // HALO) - 1, 0), 0))
    whole = lambda shape: pl.BlockSpec(shape, lambda j, k: (0,) * len(shape))
    return _Rider(
        body,
        in_specs=[tile(D_MODEL), tile(Q_WIDTH), tile(CONV_WIDTH), tile(CONV_WIDTH), tile(CONV_WIDTH),
                  halo(CONV_WIDTH), halo(CONV_WIDTH),
                  _resident((CONV_K, CONV_WIDTH)), _resident((1, Q_WIDTH)), _resident((1, CONV_WIDTH)),
                  _resident(w_out.shape)],
        out_specs=[tile(Q_WIDTH), tile(CONV_WIDTH), tile(CONV_WIDTH),
                   whole((1, Q_WIDTH)), whole((1, CONV_WIDTH)), whole((CONV_K, CONV_WIDTH))],
        out_shape=[SDS((seq, Q_WIDTH), BF16), SDS((seq, CONV_WIDTH), BF16), SDS((seq, CONV_WIDTH), BF16),
                   SDS((1, Q_WIDTH), F32), SDS((1, CONV_WIDTH), F32), SDS((CONV_K, CONV_WIDTH), F32)],
        operands=(dmix, attn, gb, gc, xin, gc, xin, conv_w, g_attn, g_conv, w_out))


def _attention_bwd(q, dattn, attn, kd0, kd1, vd0, vd1, sinks, comm=None):
    seq = q.shape[0]
    nb = ATTN_BWD_BLOCKS

    def body(sink_ref, q_ref, do_ref, o_ref, kd0_ref, kd1_ref, vd0_ref, vd1_ref,
             dq_ref, dk0_ref, dk1_ref, dv0_ref, dv1_ref, dsink_ref):
        @pl.when(pl.program_id(0) == 0)
        def _():
            for r in (dk0_ref, dk1_ref, dv0_ref, dv1_ref, dsink_ref):
                r[...] = jnp.zeros_like(r)

        lane = lax.broadcasted_iota(jnp.int32, (1, 128), 1)
        dsink = jnp.zeros((1, 128), F32)
        for b in range(nb):
            i = pl.program_id(0) * nb + b
            rows = slice(QBLOCK * b, QBLOCK * (b + 1))
            valid = _attn_valid(i)
            for kv_head, (k_ref, v_ref, dk_ref, dv_ref) in enumerate(
                    ((kd0_ref, vd0_ref, dk0_ref, dv0_ref), (kd1_ref, vd1_ref, dk1_ref, dv1_ref))):
                kband, prev, own = _band(k_ref, i)
                vband, _, _ = _band(v_ref, i)
                base = 256 * kv_head
                qm = _stack_heads(q_ref[rows, base:base + 128], q_ref[rows, base + 128:base + 256])
                dom = _stack_heads(do_ref[rows, base:base + 128], do_ref[rows, base + 128:base + 256])
                om = _stack_heads(o_ref[rows, base:base + 128], o_ref[rows, base + 128:base + 256])
                s = jnp.where(valid, _dot_nt(qm, kband), NEG_INF)
                p, e_sink, inv_l = _softmax_with_sink(s, _sink_column(sink_ref, kv_head))
                p = p * inv_l
                delta = jnp.sum(dom.astype(F32) * om.astype(F32), axis=-1, keepdims=True)
                ds = (p * (_dot_nt(dom, vband) - delta)).astype(BF16)
                sink_term = -(e_sink * inv_l) * delta
                for j in range(4):
                    part = jnp.sum(sink_term[QBLOCK * j:QBLOCK * (j + 1)], axis=0, keepdims=True)
                    dsink = dsink + jnp.where(lane == 4 * kv_head + j, part, 0.0)
                pair0, pair1 = _unstack_heads(_dot(ds, kband))
                dq_ref[rows, base:base + 128] = pair0.astype(BF16)
                dq_ref[rows, base + 128:base + 256] = pair1.astype(BF16)
                dkd = _dot_tn(ds, qm)
                dkd = dkd + pltpu.roll(dkd, HEAD_DIM, 1)
                dvd = _dot_tn(p.astype(BF16), dom)
                dvd = dvd + pltpu.roll(dvd, HEAD_DIM, 1)
                dk_ref[pl.ds(prev, QBLOCK), :] += dkd[0:QBLOCK]
                dk_ref[pl.ds(own, QBLOCK), :] += dkd[QBLOCK:]
                dv_ref[pl.ds(prev, QBLOCK), :] += dvd[0:QBLOCK]
                dv_ref[pl.ds(own, QBLOCK), :] += dvd[QBLOCK:]
        dsink_ref[...] += dsink

    blk = pl.BlockSpec((nb * QBLOCK, Q_WIDTH), lambda i: (i, 0))
    full = _resident((seq, 128))
    acc = pl.BlockSpec((seq, 128), lambda i: (0, 0))
    return _pallas(
        body, name="attention_bwd", grid=(seq // (nb * QBLOCK),),
        in_specs=[pl.BlockSpec(memory_space=pltpu.SMEM), blk, blk, blk, full, full, full, full],
        out_specs=[blk, acc, acc, acc, acc, pl.BlockSpec((1, 128), lambda i: (0, 0))],
        out_shape=[SDS((seq, Q_WIDTH), BF16)] + [SDS((seq, 128), F32)] * 4 + [SDS((1, 128), F32)],
        operands=(sinks, q, dattn, attn, kd0, kd1, vd0, vd1), comm=comm)


def _in_proj_bwd(dq, dk0, dk1, dv0, dv1, dgb, dy, gc, xin, conv_w, x, dh, g_pre, w_in_t, rope):
    seq = x.shape[0]
    tb = min(seq, WIDE_TOKEN_TILE)
    n_tiles = seq // tb

    def body(dq_ref, dk0_ref, dk1_ref, dv0_ref, dv1_ref, dgb_ref, dy_ref, dyh_ref, gc_ref, xin_ref, cw_ref,
             x_ref, dh_ref, g_ref, w_ref, c_ref, sa_ref, sb_ref,
             dproj_ref, gx_ref, dg_ref):
        i = pl.program_id(0)

        @pl.when(i == 0)
        def _():
            dg_ref[...] = jnp.zeros_like(dg_ref)

        dy = dy_ref[...].astype(F32)
        ext = jnp.concatenate([dy, jnp.where(i == n_tiles - 1, 0.0, dyh_ref[...].astype(F32))], axis=0)
        dy1 = pltpu.roll(ext, tb + HALO - 1, 0)[0:tb]
        dy2 = pltpu.roll(ext, tb + HALO - 2, 0)[0:tb]
        cw = cw_ref[...]
        du = cw[2:3, :] * dy + cw[1:2, :] * dy1 + cw[0:1, :] * dy2
        scale = 1.0 / math.sqrt(HEAD_DIM)
        base = Q_WIDTH + 2 * KV_WIDTH
        halves = [slice(0, tb // 2), slice(tb // 2, tb)]
        low = _lane_lt64((tb // 2, 128))
        for rows in halves:
            c, sa, sb = _rope_tile(c_ref.at[rows, :], sa_ref, sb_ref)
            for p in range(Q_WIDTH // 128):
                dproj_ref[rows, 128 * p:128 * (p + 1)] = _rope_transposed(
                    dq_ref[rows, 128 * p:128 * (p + 1)].astype(F32) * scale, c, sa, sb).astype(BF16)
            dk = jnp.where(low, dk0_ref[rows, :], dk1_ref[rows, :])
            dproj_ref[rows, Q_WIDTH:Q_WIDTH + KV_WIDTH] = _rope_transposed(dk, c, sa, sb).astype(BF16)
            dproj_ref[rows, Q_WIDTH + KV_WIDTH:base] = jnp.where(low, dv0_ref[rows, :], dv1_ref[rows, :]).astype(BF16)
            dproj_ref[rows, base:base + CONV_WIDTH] = dgb_ref[rows, :]
            dproj_ref[rows, base + CONV_WIDTH:base + 2 * CONV_WIDTH] = (du[rows] * xin_ref[rows, :].astype(F32)).astype(BF16)
            dproj_ref[rows, base + 2 * CONV_WIDTH:] = (du[rows] * gc_ref[rows, :].astype(F32)).astype(BF16)
        w_all = w_ref[...].reshape(IN_COLS, D_MODEL)
        dhn = [_dot(dproj_ref[rows, :], w_all) for rows in halves]
        dg = jnp.zeros((1, D_MODEL), F32)
        for k, rows in enumerate(halves):
            xv = x_ref[rows, :]
            r = _rms(xv)
            xhat = xv * r
            dg = dg + _colsum(dhn[k] * xhat)
            gx_ref[rows, :] = dh_ref[rows, :].astype(F32) + _norm_bwd(dhn[k], g_ref[...], xhat, r)
        dg_ref[...] += dg

    tile = lambda w: pl.BlockSpec((tb, w), lambda i: (i, 0))
    halo_next = pl.BlockSpec((HALO, CONV_WIDTH), lambda i: (jnp.minimum((i + 1) * (tb // HALO), seq // HALO - 1), 0))
    return _pallas(
        body, name="in_proj_bwd", grid=(n_tiles,),
        in_specs=[tile(Q_WIDTH), tile(128), tile(128), tile(128), tile(128), tile(CONV_WIDTH), tile(CONV_WIDTH), halo_next,
                  tile(CONV_WIDTH), tile(CONV_WIDTH), _resident((CONV_K, CONV_WIDTH)),
                  tile(D_MODEL), tile(D_MODEL), _resident((1, D_MODEL)), _resident(w_in_t.shape), *_rope_specs(tb)],
        out_specs=[tile(IN_COLS), tile(D_MODEL), pl.BlockSpec((1, D_MODEL), lambda i: (0, 0))],
        out_shape=[SDS((seq, IN_COLS), BF16), SDS((seq, D_MODEL), F32), SDS((1, D_MODEL), F32)],
        operands=(dq, dk0, dk1, dv0, dv1, dgb, dy, dy, gc, xin, conv_w, x, dh, g_pre, w_in_t, *rope))


def _wgrad_grid(seq, per_chip, h_rows, with_rider=False):
    chips_per_step = 1 if per_chip else N_CHIPS
    m = chips_per_step * 2 * h_rows
    bt = min(seq, WGRAD_TOKEN_TILE if per_chip and not with_rider else WGRAD_TOKEN_TILE // 2)
    return chips_per_step, m, bt, seq // bt


def _wgrad(name, a, b, *, per_chip, h_rows, square_a=False, comm=None, rider=None):
    seq = a.shape[0]
    chips_per_step, m, bt, n_k = _wgrad_grid(seq, per_chip, h_rows, rider is not None)
    a_cols = m if per_chip else a.shape[1]
    a_wide = a.shape[1] > a_cols
    b_wide = b.shape[1] > D_MODEL

    def body(a_ref, b_ref, g_ref):
        @pl.when(pl.program_id(1) == 0)
        def _():
            g_ref[...] = jnp.zeros_like(g_ref)

        av = a_ref[...]
        if square_a:
            av = (av.astype(F32) * av.astype(F32)).astype(BF16)
        g_ref[...] += _dot_tn(av, b_ref[...]).reshape(g_ref.shape)

    a_spec = pl.BlockSpec((bt, a_cols), (lambda j, k: (k, j)) if a_wide else (lambda j, k: (k, 0)))
    b_spec = pl.BlockSpec((bt, D_MODEL), (lambda j, k: (k, j)) if b_wide else (lambda j, k: (k, 0)))
    g_spec = pl.BlockSpec((chips_per_step, 2, h_rows, D_MODEL), lambda j, k: (j, 0, 0, 0),
                          pipeline_mode=None if per_chip else pl.Buffered(1))
    return _pallas(
        body, name=name, grid=(N_CHIPS if per_chip else 1, n_k),
        in_specs=[a_spec, b_spec], out_specs=[g_spec], out_shape=[SDS((N_CHIPS, 2, h_rows, D_MODEL), F32)],
        operands=(a, b), comm=comm, rider=rider)


def _adamw_math(w, g, m, v):
    m = ADAM_B1 * m + (1.0 - ADAM_B1) * g
    v = ADAM_B2 * v + (1.0 - ADAM_B2) * (g * g)
    m_hat = m / (1.0 - ADAM_B1 ** ADAM_STEP)
    v_hat = v / (1.0 - ADAM_B2 ** ADAM_STEP)
    delta = -ADAM_LR * (m_hat / (jnp.sqrt(v_hat) + ADAM_EPS) + ADAM_WD * w)
    return delta, m, v


ADAMW_STEPS_PER_HALF = 4


def _adamw_rows(items):
    n = len(items)
    per_half = ADAMW_STEPS_PER_HALF

    def body(*refs):
        for k in range(n):
            r_ref, w_ref, m_ref, v_ref = refs[4 * k:4 * k + 4]
            g_out, d_out, m_out, v_out = refs[4 * (n + k):4 * (n + k) + 4]
            g = r_ref[0]
            g_out[...] = g
            d_out[...], m_out[...], v_out[...] = _adamw_math(w_ref[...], g, m_ref[...], v_ref[...])

    in_specs, out_specs, out_shape, operands = [], [], [], []
    for reduced, w, m, v in items:
        rt = reduced.shape[1] // per_half
        blk = pl.BlockSpec((rt, D_MODEL), lambda h, r: (h * per_half + r, 0))
        in_specs += [pl.BlockSpec((1, rt, D_MODEL), lambda h, r: (h, r, 0)), blk, blk, blk]
        out_specs += [blk] * 4
        out_shape += [SDS(w.shape, F32)] * 4
        operands += [reduced, w, m, v]
    res = _pallas(body, name="adamw_rows", grid=(2, per_half), in_specs=in_specs, out_specs=out_specs,
                  out_shape=out_shape, operands=tuple(operands))
    return [res[4 * k:4 * k + 4] for k in range(n)]


def _adamw_small(packed_grads, w, m, v):
    names = SMALL_NAMES
    n = len(names)
    conv_local = w["conv_w"].shape[-1]

    def body(*refs):
        gp = refs[0]
        w_refs, m_refs, v_refs = refs[1:1 + n], refs[1 + n:1 + 2 * n], refs[1 + 2 * n:1 + 3 * n]
        outs = refs[1 + 3 * n:]
        g_out, d_out, m_out, v_out = outs[0:n], outs[n:2 * n], outs[2 * n:3 * n], outs[3 * n:4 * n]
        chip = 2 * lax.axis_index("x") + lax.axis_index("y")

        def step(k, g, index=None):
            pick = (lambda r: r[...]) if index is None else (lambda r: r[index])
            d, new_m, new_v = _adamw_math(pick(w_refs[k]), g, pick(m_refs[k]), pick(v_refs[k]))
            for ref, val in ((g_out[k], g), (d_out[k], d), (m_out[k], new_m), (v_out[k], new_v)):
                if index is None:
                    ref[...] = val
                else:
                    ref[index] = val

        for k, name in enumerate(names):
            if name in SMALL_VECTORS:
                step(k, gp[SMALL_VECTORS.index(name):SMALL_VECTORS.index(name) + 1, :])
            elif name == "attn_group_norm":
                step(k, gp[4:5, 0:Q_WIDTH])
            elif name == "conv_group_norm":
                step(k, gp[4:5, Q_WIDTH:])
            elif name == "attn_sinks":
                step(k, gp[7:8, 0:8])
            else:
                for t in range(CONV_K):
                    row, base = 5 + t // 2, CONV_WIDTH * (t % 2)
                    g = gp[row:row + 1, base:base + conv_local]
                    for j in range(1, CONV_WIDTH // conv_local):
                        g = jnp.where(chip == j, gp[row:row + 1, base + conv_local * j:base + conv_local * (j + 1)], g)
                    step(k, g, index=(0, slice(t, t + 1), slice(None)))

    shapes = [SDS(w[name].shape, F32) for name in names]
    res = pl.pallas_call(
        body, name="adamw_small", in_specs=[VMEM_WHOLE] * (1 + 3 * n), out_specs=[VMEM_WHOLE] * (4 * n),
        out_shape=shapes * 4,
    )(packed_grads, *[w[k] for k in names], *[m[k] for k in names], *[v[k] for k in names])
    return [dict(zip(names, res[i * n:(i + 1) * n])) for i in range(4)]


SMALL_VECTORS = ("pre_mix_norm", "post_mix_norm", "pre_mlp_norm", "post_mlp_norm")
SMALL_NAMES = SMALL_VECTORS + ("attn_group_norm", "conv_group_norm", "conv_w", "attn_sinks")


def _pack_small(p):
    rows = [p[n].reshape(1, D_MODEL) for n in SMALL_VECTORS]
    rows.append(jnp.concatenate([p["attn_group_norm"].reshape(1, -1), p["conv_group_norm"].reshape(1, -1)], axis=1))
    cw = p["conv_w"].reshape(CONV_K, -1)
    rows.append(jnp.pad(cw, ((0, 1), (0, CONV_WIDTH - cw.shape[1]))).reshape(2, D_MODEL))
    last = jnp.concatenate([p["attn_sinks"].reshape(1, 8), p.get("loss_sum", jnp.zeros((1, 1), F32))], axis=1)
    rows.append(jnp.pad(last, ((0, 0), (0, D_MODEL - 9))))
    return jnp.concatenate(rows, axis=0)


WEIGHT_ORDER = ("pre_mix_norm", "w_in", "conv_w", "attn_sinks", "attn_group_norm", "conv_group_norm", "w_out",
                "post_mix_norm", "pre_mlp_norm", "w_up", "w_down", "post_mlp_norm")


def kernel(x, pre_mix_norm, w_in, conv_w, attn_sinks, attn_group_norm, conv_group_norm, w_out, post_mix_norm, pre_mlp_norm, w_up, w_down, post_mlp_norm, loss_target, m_pre_mix_norm, m_w_in, m_conv_w, m_attn_sinks, m_attn_group_norm, m_conv_group_norm, m_w_out, m_post_mix_norm, m_pre_mlp_norm, m_w_up, m_w_down, m_post_mlp_norm, v_pre_mix_norm, v_w_in, v_conv_w, v_attn_sinks, v_attn_group_norm, v_conv_group_norm, v_w_out, v_post_mix_norm, v_pre_mlp_norm, v_w_up, v_w_down, v_post_mlp_norm):
    w = dict(pre_mix_norm=pre_mix_norm, w_in=w_in, conv_w=conv_w, attn_sinks=attn_sinks, attn_group_norm=attn_group_norm,
             conv_group_norm=conv_group_norm, w_out=w_out, post_mix_norm=post_mix_norm, pre_mlp_norm=pre_mlp_norm,
             w_up=w_up, w_down=w_down, post_mlp_norm=post_mlp_norm)
    m = dict(pre_mix_norm=m_pre_mix_norm, w_in=m_w_in, conv_w=m_conv_w, attn_sinks=m_attn_sinks,
             attn_group_norm=m_attn_group_norm, conv_group_norm=m_conv_group_norm, w_out=m_w_out,
             post_mix_norm=m_post_mix_norm, pre_mlp_norm=m_pre_mlp_norm, w_up=m_w_up, w_down=m_w_down,
             post_mlp_norm=m_post_mlp_norm)
    v = dict(pre_mix_norm=v_pre_mix_norm, w_in=v_w_in, conv_w=v_conv_w, attn_sinks=v_attn_sinks,
             attn_group_norm=v_attn_group_norm, conv_group_norm=v_conv_group_norm, w_out=v_w_out,
             post_mix_norm=v_post_mix_norm, pre_mlp_norm=v_pre_mlp_norm, w_up=v_w_up, w_down=v_w_down,
             post_mlp_norm=v_post_mlp_norm)
    core = lax.axis_index("c").astype(jnp.int32).reshape(1)
    xs, target = x[0], loss_target[0]
    rope = _rope_inputs(xs.shape[0])

    conv_pad = jnp.pad(conv_w[0], ((0, 8 - CONV_K), (0, 0)))
    wf_in, conv_all, hb_up, hb_down, hb_out = _gather_whole(w_in[0].T, (w_up[0], w_down[0], w_out[0]), conv_pad)
    conv_full = conv_all[:, :CONV_K, :].transpose(1, 0, 2).reshape(CONV_K, CONV_WIDTH)

    whole_up, whole_out = (0, H_UP), (0, H_OUT)
    early, late = (0, DOWN_EARLY_ROWS), (DOWN_EARLY_ROWS, H_DOWN - DOWN_EARLY_ROWS)
    *proj, wf_up, wf_out, wf_down = _in_proj(
        xs, pre_mix_norm, wf_in, rope,
        comm=_merge(_relay(hb_up, None, first=whole_up), _relay(hb_out, None, first=whole_out),
                    _relay(hb_down, None, first=early)))
    q, kd0, kd1, vd0, vd1, gb, gc, xin, hn = proj
    attn, wf_up, wf_out, wf_down = _attention_fwd(
        q, kd0, kd1, vd0, vd1, attn_sinks,
        comm=_merge(_relay(None, wf_up, second=whole_up), _relay(None, wf_out, second=whole_out, third_after=whole_out),
                    _relay(hb_down, wf_down, first=late, second=early)))
    mix, mixed, wf_up, wf_down = _mix_out(
        attn, gb, gc, xin, conv_full, attn_group_norm, conv_group_norm, wf_out,
        comm=_merge(_relay(None, wf_up, third=whole_up), _relay(None, wf_down, second=late, third=early, third_after=late)))
    up, hn2, dmlp, dup, dh, dmix, loss_sum, dg_post_mlp, dg_pre_mlp, dg_post_mix = _mlp_fwd_bwd(
        xs, mix, target, post_mix_norm, pre_mlp_norm, post_mlp_norm, wf_up, wf_down)

    n_k = _wgrad_grid(xs.shape[0], True, H_DOWN, with_rider=True)[3]
    g_down, dattn, dgb, dy, dg_attn, dg_conv, dconv_w = _wgrad(
        "wgrad_down", up, dmlp, per_chip=True, h_rows=H_DOWN, square_a=True,
        rider=_mix_bwd(dmix, attn, gb, gc, xin, conv_full, attn_group_norm, conv_group_norm, wf_out, n_k))
    g_up, got_down = _wgrad("wgrad_up", hn2, dup, per_chip=True, h_rows=H_UP, comm=_pair_send(g_down))
    p_down = _pair_sum("pair_sum_down", core, g_down, got_down)
    g_out, got_up = _wgrad("wgrad_out", mixed, dmix, per_chip=False, h_rows=H_OUT, comm=_pair_send(g_up))
    p_up = _pair_sum("pair_sum_up", core, g_up, got_up)
    dq, dk0, dk1, dv0, dv1, dsink, ex_down, ex_up, got_out = _attention_bwd(
        q, dattn, attn, kd0, kd1, vd0, vd1, attn_sinks,
        comm=_merge(_chip_exchange(p_down), _chip_exchange(p_up), _pair_send(g_out)))
    p_out = _pair_sum("pair_sum_out", core, g_out, got_out)
    dproj, grad_x, dg_pre_mix = _in_proj_bwd(dq, dk0, dk1, dv0, dv1, dgb, dy, gc, xin, conv_full, xs, dh, pre_mix_norm,
                                             wf_in, rope)
    g_in, ex_out = _wgrad("wgrad_in", dproj, hn, per_chip=False, h_rows=H_IN, comm=_chip_exchange(p_out))
    small = dict(pre_mix_norm=dg_pre_mix, conv_w=dconv_w, attn_sinks=dsink[:, :8], attn_group_norm=dg_attn,
                 conv_group_norm=dg_conv, post_mix_norm=dg_post_mix, pre_mlp_norm=dg_pre_mlp, post_mlp_norm=dg_post_mlp,
                 loss_sum=loss_sum)
    r_down, r_up, r_out, r_in, small_total = _tail_reduce(g_in, [ex_down, ex_up, ex_out], _pack_small(small))

    out_g, out_d, out_m, out_v = {}, {}, {}, {}
    res_up, res_down, res_out, res_in_t = _adamw_rows([
        (r_up, w_up[0], m_w_up[0], v_w_up[0]), (r_down, w_down[0], m_w_down[0], v_w_down[0]),
        (r_out, w_out[0], m_w_out[0], v_w_out[0]), (r_in, w_in[0].T, m_w_in[0].T, v_w_in[0].T)])
    for name, res in (("w_up", res_up), ("w_down", res_down), ("w_out", res_out), ("w_in", [t.T for t in res_in_t])):
        out_g[name], out_d[name], out_m[name], out_v[name] = res

    loss = small_total[7, 8] * (0.5 / D_MODEL)
    for out, part in zip((out_g, out_d, out_m, out_v), _adamw_small(small_total, w, m, v)):
        out.update(part)

    def shaped(d):
        return [d[n].reshape(w[n].shape) for n in WEIGHT_ORDER]

    return (loss, grad_x[None], *shaped(out_g), *shaped(out_d), *shaped(out_m), *shaped(out_v))
```

```python
import math
from typing import Callable, NamedTuple

import jax
import jax.numpy as jnp
import numpy as np
from jax import lax
from jax.experimental import pallas as pl
from jax.experimental.pallas import tpu as pltpu

F32 = jnp.float32
BF16 = jnp.bfloat16

D_MODEL = 1024
HEAD_DIM = 64
Q_WIDTH = 512
KV_WIDTH = 128
CONV_WIDTH = 512
CONV_K = 3
D_FF = 4096
IN_COLS = 2304
QBLOCK = 128
ROT_DIM = 16
ROPE_THETA = 500000.0
NORM_EPS = 1e-6
NEG_INF = -1e30
N_CHIPS = 4

ADAM_LR = 0.001
ADAM_B1 = 0.9
ADAM_B2 = 0.999
ADAM_EPS = 1e-08
ADAM_WD = 0.01
ADAM_STEP = 10

H_UP, H_DOWN, H_OUT, H_IN = 512, 512, 128, 288
DOWN_EARLY_ROWS = 288

TOKEN_TILE = 512
WIDE_TOKEN_TILE = 1024
ATTN_FWD_BLOCKS = 16
ATTN_BWD_BLOCKS = 2
WGRAD_TOKEN_TILE = 4096
VMEM_LIMIT_V7X = 60 * 1024 * 1024

MESH = pl.DeviceIdType.MESH
ANY = pl.BlockSpec(memory_space=pl.ANY)
VMEM_WHOLE = pl.BlockSpec(memory_space=pltpu.VMEM)
SDS = jax.ShapeDtypeStruct


def _resident(shape):
    zeros = (0,) * len(shape)
    return pl.BlockSpec(shape, lambda *_: zeros, pipeline_mode=pl.Buffered(1))


def _rms(v):
    return lax.rsqrt(jnp.mean(v * v, axis=-1, keepdims=True) + NORM_EPS)


def _norm_bwd(dy, gain, vhat, rstd):
    t = dy * gain
    return rstd * (t - vhat * jnp.mean(t * vhat, axis=-1, keepdims=True))


def _colsum(v):
    return jnp.sum(v, axis=0, keepdims=True)


def _dot_nt(a, b):
    return lax.dot_general(a, b, (((1,), (1,)), ((), ())), preferred_element_type=F32)


def _dot_tn(a, b):
    return lax.dot_general(a, b, (((0,), (0,)), ((), ())), preferred_element_type=F32)


def _dot(a, b):
    return jnp.dot(a, b, preferred_element_type=F32)


def _chip_block(w_ref, chip):
    both = w_ref[pl.ds(2 * chip, 2)]
    return both.reshape(2 * both.shape[1], both.shape[2])


def _lane_lt64(shape):
    return lax.broadcasted_iota(jnp.int32, shape, 1) < HEAD_DIM


class _Comm(NamedTuple):
    operands: tuple
    out_shapes: tuple
    aliases: dict
    n_remote: int
    n_local: int
    plan: Callable
    after: Callable = None


def _merge(*comms):
    operands, out_shapes, aliases, parts = [], [], {}, []
    n_remote = n_local = 0
    for cm in comms:
        parts.append((len(operands), len(out_shapes), n_remote, n_local, cm))
        for k, v in cm.aliases.items():
            aliases[len(operands) + k] = len(out_shapes) + v
        operands += cm.operands
        out_shapes += cm.out_shapes
        n_remote += cm.n_remote
        n_local += cm.n_local

    def run(which, ins, outs, send, recv, loc):
        sends, recvs, locs = [], [], []
        for i0, o0, r0, l0, cm in parts:
            stage = getattr(cm, which)
            if stage is not None:
                s, r, l = stage(ins[i0:i0 + len(cm.operands)], outs[o0:o0 + len(cm.out_shapes)],
                                lambda k, r0=r0: send(r0 + k), lambda k, r0=r0: recv(r0 + k), lambda k, l0=l0: loc(l0 + k))
                sends, recvs, locs = sends + s, recvs + r, locs + l
        return sends, recvs, locs

    def plan(*args):
        return run("plan", *args)

    def after(*args):
        return run("after", *args)

    return _Comm(tuple(operands), tuple(out_shapes), aliases, n_remote, n_local, plan,
                 after if any(cm.after is not None for cm in comms) else None)


def _sem_scratch(comm):
    return [pltpu.SemaphoreType.DMA((max(comm.n_remote, 1),)), pltpu.SemaphoreType.DMA((max(comm.n_remote, 1),)),
            pltpu.SemaphoreType.DMA((max(comm.n_local, 1),))]


class _Rider(NamedTuple):
    body: Callable
    in_specs: list
    out_specs: list
    out_shape: list
    operands: tuple


def _pallas(body, *, name, grid, in_specs, out_specs, out_shape, operands, scratch=(), comm=None, rider=None):
    params = pltpu.CompilerParams(dimension_semantics=("arbitrary",) * len(grid), vmem_limit_bytes=VMEM_LIMIT_V7X)
    if rider is not None:
        own_in, own_out, ride_in, ride_out = len(in_specs), len(out_specs), len(rider.in_specs), len(rider.out_specs)
        own_body = body

        def body(*refs):
            o0 = own_in + ride_in
            s0 = o0 + own_out + ride_out
            own_body(*refs[:own_in], *refs[o0:o0 + own_out], *refs[s0:])
            first = None
            for axis in range(len(grid)):
                at_start = pl.program_id(axis) == 0
                first = at_start if first is None else jnp.logical_and(first, at_start)
            rider.body(first, *refs[own_in:o0], *refs[o0 + own_out:s0])

        in_specs, out_specs = list(in_specs) + rider.in_specs, list(out_specs) + rider.out_specs
        out_shape, operands = list(out_shape) + rider.out_shape, tuple(operands) + tuple(rider.operands)
    if comm is None:
        return pl.pallas_call(body, name=name, grid=grid, in_specs=in_specs, out_specs=out_specs, out_shape=out_shape,
                              scratch_shapes=list(scratch), compiler_params=params)(*operands)
    n_in, n_out, n_scr = len(in_specs), len(out_specs), len(scratch)
    c_in, c_out = len(comm.operands), len(comm.out_shapes)

    def with_comm(*refs):
        ins, c_ins = refs[:n_in], refs[n_in:n_in + c_in]
        o0 = n_in + c_in
        outs, c_outs = refs[o0:o0 + n_out], refs[o0 + n_out:o0 + n_out + c_out]
        s0 = o0 + n_out + c_out
        scr = refs[s0:s0 + n_scr]
        send_sems, recv_sems, local_sems = refs[s0 + n_scr:]
        first = last = None
        for axis, size in enumerate(grid):
            at_start, at_end = pl.program_id(axis) == 0, pl.program_id(axis) == size - 1
            first = at_start if first is None else jnp.logical_and(first, at_start)
            last = at_end if last is None else jnp.logical_and(last, at_end)

        def copies():
            return comm.plan(c_ins, c_outs, lambda k: send_sems.at[k], lambda k: recv_sems.at[k],
                             lambda k: local_sems.at[k])

        @pl.when(first)
        def _():
            sends, _, locs = copies()
            for cp in sends + locs:
                cp.start()

        body(*ins, *outs, *scr)

        @pl.when(last)
        def _():
            sends, recvs, locs = copies()
            for cp in recvs:
                cp.wait_recv()
            for cp in sends:
                cp.wait_send()
            for cp in locs:
                cp.wait()
            if comm.after is not None:
                sends, recvs, _ = comm.after(c_ins, c_outs, lambda k: send_sems.at[k], lambda k: recv_sems.at[k],
                                             lambda k: local_sems.at[k])
                for cp in sends:
                    cp.start()
                for cp in recvs:
                    cp.wait_recv()
                for cp in sends:
                    cp.wait_send()

    return pl.pallas_call(
        with_comm, name=name, grid=grid,
        in_specs=list(in_specs) + [ANY] * c_in, out_specs=list(out_specs) + [ANY] * c_out,
        out_shape=list(out_shape) + list(comm.out_shapes),
        scratch_shapes=list(scratch) + _sem_scratch(comm),
        input_output_aliases={n_in + k: n_out + v for k, v in comm.aliases.items()},
        compiler_params=params)(*operands, *comm.operands)


def _place():
    return lax.axis_index("x"), lax.axis_index("y"), lax.axis_index("c")


def _other_chips(x, y):
    return [(1 - x, y), (x, 1 - y), (1 - x, 1 - y)]


def _slot(px, py, pc):
    return 4 * px + 2 * py + pc


def _remote(src, dst, send_sem, recv_sem, to):
    return pltpu.make_async_remote_copy(src_ref=src, dst_ref=dst, send_sem=send_sem, recv_sem=recv_sem,
                                        device_id=to, device_id_type=MESH)


def _gather_first(half_block):
    def plan(ins, outs, send, recv, loc):
        (blk,), (full,) = ins, outs
        x, y, c = _place()
        chips = _other_chips(x, y)
        mine = full.at[_slot(x, y, c)]
        sends = [_remote(blk, mine, send(0), recv(0), (x, y, 1 - c))]
        sends += [_remote(blk, mine, send(1 + j), recv(1 + j), (*chip, c)) for j, chip in enumerate(chips)]
        recvs = [_remote(blk, full.at[_slot(x, y, 1 - c)], send(0), recv(0), (x, y, 1 - c))]
        recvs += [_remote(blk, full.at[_slot(*chip, c)], send(1 + j), recv(1 + j), (*chip, c))
                  for j, chip in enumerate(chips)]
        return sends, recvs, [pltpu.make_async_copy(blk, mine, loc(0))]

    return _Comm((half_block,), (SDS((2 * N_CHIPS,) + half_block.shape, half_block.dtype),), {}, 4, 1, plan)


def _gather_second(partly_gathered):
    def plan(ins, outs, send, recv, loc):
        (src,), (full,) = ins, outs
        x, y, c = _place()
        chips = _other_chips(x, y)
        sends = [_remote(src.at[_slot(*chip, c)], full.at[_slot(*chip, c)], send(j), recv(j), (x, y, 1 - c))
                 for j, chip in enumerate(chips)]
        recvs = [_remote(src.at[_slot(*chip, 1 - c)], full.at[_slot(*chip, 1 - c)], send(j), recv(j), (x, y, 1 - c))
                 for j, chip in enumerate(chips)]
        return sends, recvs, []

    return _Comm((partly_gathered,), (SDS(partly_gathered.shape, partly_gathered.dtype),), {0: 0}, 3, 0, plan)


def _relay_pieces(full, rows, x, y, c):
    start, half = rows[0], rows[1] // 2
    upper, lower = pl.ds(start, half), pl.ds(start + half, half)
    diagonal = full.at[_slot(1 - x, 1 - y, c)]
    return [(full.at[_slot(1 - x, y, c), upper], diagonal.at[upper], (x, 1 - y, c)),
            (full.at[_slot(x, 1 - y, c), lower], diagonal.at[lower], (1 - x, y, c))]


def _relay(half_block, so_far, first=None, second=None, third=None, third_after=None):
    has_block, has_buffer = half_block is not None, so_far is not None
    shape = so_far.shape if has_buffer else (2 * N_CHIPS,) + half_block.shape
    dtype = so_far.dtype if has_buffer else half_block.dtype

    def third_leg(rows, k, ins, outs, send, recv):
        src, full = (ins[-1] if has_buffer else outs[0]), outs[0]
        x, y, c = _place()
        span, sibling = pl.ds(*rows), (x, y, 1 - c)
        here, there = _slot(1 - x, 1 - y, c), _slot(1 - x, 1 - y, 1 - c)
        return ([_remote(src.at[here, span], full.at[here, span], send(k), recv(k), sibling)],
                [_remote(src.at[there, span], full.at[there, span], send(k), recv(k), sibling)])

    def plan(ins, outs, send, recv, loc):
        src, full = (ins[-1] if has_buffer else outs[0]), outs[0]
        x, y, c = _place()
        sibling = (x, y, 1 - c)
        sends, recvs, locs = [], [], []
        if first is not None:
            span = pl.ds(*first)
            blk, mine = ins[0].at[span], full.at[_slot(x, y, c), span]
            for k, peer in enumerate([sibling, (1 - x, y, c), (x, 1 - y, c)]):
                sends.append(_remote(blk, mine, send(k), recv(k), peer))
                recvs.append(_remote(blk, full.at[_slot(*peer), span], send(k), recv(k), peer))
            locs.append(pltpu.make_async_copy(blk, mine, loc(0)))
        if second is not None:
            span = pl.ds(*second)
            for k, chip in enumerate([(1 - x, y), (x, 1 - y)]):
                sends.append(_remote(src.at[_slot(*chip, c), span], full.at[_slot(*chip, c), span], send(3 + k), recv(3 + k),
                                     sibling))
                recvs.append(_remote(src.at[_slot(*chip, 1 - c), span], full.at[_slot(*chip, 1 - c), span], send(3 + k),
                                     recv(3 + k), sibling))
            for k, (piece, lands, peer) in enumerate(_relay_pieces(full, second, x, y, c)):
                sends.append(_remote(piece, piece, send(5 + k), recv(5 + k), peer))
                recvs.append(_remote(lands, lands, send(5 + k), recv(5 + k), peer))
        if third is not None:
            s, r = third_leg(third, 7, ins, outs, send, recv)
            sends, recvs = sends + s, recvs + r
        return sends, recvs, locs

    def after(ins, outs, send, recv, loc):
        s, r = third_leg(third_after, 8, ins, outs, send, recv)
        return s, r, []

    operands = ((half_block,) if has_block else ()) + ((so_far,) if has_buffer else ())
    return _Comm(operands, (SDS(shape, dtype),), {len(operands) - 1: 0} if has_buffer else {}, 9, 1, plan,
                 after if third_after is not None else None)


def _gather_whole(first, others, small_block):
    shards = (first, *others)
    n = len(shards)
    hs = [s.shape[0] // 2 for s in shards]
    rows = hs[0]

    def body(*refs):
        src, small_ref = refs[:n], refs[n]
        out_ref, small_out_ref, half_out = refs[n + 1], refs[n + 2], refs[n + 3:2 * n + 2]
        stage, half = refs[2 * n + 2:3 * n + 2], refs[3 * n + 2:4 * n + 2]
        send_sems, recv_sems, local_sems = refs[4 * n + 2:]
        x, y, c = _place()
        me, sibling = (x, y, c), (x, y, 1 - c)
        neighbours, diagonal = [(1 - x, y), (x, 1 - y)], (1 - x, 1 - y)
        loads = [pltpu.make_async_copy(src[k].at[pl.ds(c * hs[k], hs[k])], stage[k], local_sems.at[2 + k]) for k in range(n)]
        loads[0].start()
        loads[0].wait()
        for cp in loads[1:]:
            cp.start()
        blk_ref = half[0]
        blk_ref[...] = stage[0][...].astype(BF16)

        def copy(k, block, to, src=None):
            return _remote(out_ref.at[_slot(*block)] if src is None else src, out_ref.at[_slot(*block)],
                           send_sems.at[k], recv_sems.at[k], to)

        def small_copy(k, chip, to):
            return _remote(small_ref, small_out_ref.at[2 * chip[0] + chip[1]], send_sems.at[8 + k], recv_sems.at[8 + k], to)

        mine = pltpu.make_async_copy(blk_ref, out_ref.at[_slot(*me)], local_sems.at[0])
        mine_small = pltpu.make_async_copy(small_ref, small_out_ref.at[2 * x + y], local_sems.at[1])
        mine.start()
        mine_small.start()
        started = [copy(0, me, sibling, src=blk_ref)]
        started += [copy(1 + k, me, (*chip, c), src=blk_ref) for k, chip in enumerate(neighbours)]
        started += [small_copy(k, (x, y), (*chip, c)) for k, chip in enumerate(neighbours + [diagonal])]
        for cp in started:
            cp.start()
        stores = []
        for k in range(1, n):
            loads[k].wait()
            half[k][...] = stage[k][...].astype(BF16)
            stores.append(pltpu.make_async_copy(half[k], half_out[k - 1], local_sems.at[2 + n + k]))
            stores[-1].start()
        pieces = _relay_pieces(out_ref, (0, rows), x, y, c)
        for k, chip in enumerate(neighbours):
            copy(1 + k, (*chip, c), me).wait_recv()
            piece, _, peer = pieces[k]
            started += [copy(3 + k, (*chip, c), sibling), _remote(piece, piece, send_sems.at[5 + k], recv_sems.at[5 + k], peer)]
            started[-2].start()
            started[-1].start()
        for k, (_, lands, peer) in enumerate(pieces):
            _remote(lands, lands, send_sems.at[5 + k], recv_sems.at[5 + k], peer).wait_recv()
        started.append(copy(7, (*diagonal, c), sibling))
        started[-1].start()
        copy(0, sibling, me).wait_recv()
        for k, chip in enumerate(neighbours):
            copy(3 + k, (*chip, 1 - c), me).wait_recv()
        copy(7, (*diagonal, 1 - c), me).wait_recv()
        for k, chip in enumerate(neighbours + [diagonal]):
            small_copy(k, chip, me).wait_recv()
        for cp in started:
            cp.wait_send()
        mine.wait()
        mine_small.wait()
        for cp in stores:
            cp.wait()

    return pl.pallas_call(
        body, name="gather_whole", in_specs=[ANY] * (n + 1), out_specs=[ANY] * (n + 1),
        out_shape=[SDS((2 * N_CHIPS, rows, D_MODEL), BF16), SDS((N_CHIPS,) + small_block.shape, small_block.dtype)]
                  + [SDS((h, D_MODEL), BF16) for h in hs[1:]],
        scratch_shapes=[pltpu.VMEM((h, D_MODEL), F32) for h in hs] + [pltpu.VMEM((h, D_MODEL), BF16) for h in hs]
                       + [pltpu.SemaphoreType.DMA((11,)), pltpu.SemaphoreType.DMA((11,)), pltpu.SemaphoreType.DMA((2 + 2 * n,))],
        compiler_params=pltpu.CompilerParams(vmem_limit_bytes=VMEM_LIMIT_V7X),
    )(*shards, small_block)


def _pair_send(grads):
    def plan(ins, outs, send, recv, loc):
        (g,), (got,) = ins, outs
        x, y, c = _place()
        copies = [_remote(g.at[j, 1 - c], got.at[j], send(j), recv(j), (x, y, 1 - c)) for j in range(N_CHIPS)]
        return copies, copies, []

    shape = (grads.shape[0],) + grads.shape[2:]
    return _Comm((grads,), (SDS(shape, grads.dtype),), {}, N_CHIPS, 0, plan)


def _chip_exchange(partial):
    def plan(ins, outs, send, recv, loc):
        (p,), (got,) = ins, outs
        x, y, c = _place()
        my_chip = 2 * x + y
        chips = _other_chips(x, y)
        sends = [_remote(p.at[2 * chip[0] + chip[1]], got.at[my_chip], send(j), recv(j), (*chip, c))
                 for j, chip in enumerate(chips)]
        recvs = [_remote(p.at[my_chip], got.at[2 * chip[0] + chip[1]], send(j), recv(j), (*chip, c))
                 for j, chip in enumerate(chips)]
        return sends, recvs, [pltpu.make_async_copy(p.at[my_chip], got.at[my_chip], loc(0))]

    return _Comm((partial,), (SDS(partial.shape, partial.dtype),), {}, 3, 1, plan)


def _pair_sum(name, core, grads, received):
    h = grads.shape[2]

    def body(core_ref, g_ref, r_ref, o_ref):
        o_ref[...] = (g_ref[0] + r_ref[...]).astype(BF16)

    return pl.pallas_call(
        body, name=name,
        grid_spec=pltpu.PrefetchScalarGridSpec(
            num_scalar_prefetch=1, grid=(N_CHIPS,),
            in_specs=[pl.BlockSpec((1, 1, h, D_MODEL), lambda j, core_ref: (j, core_ref[0], 0, 0)),
                      pl.BlockSpec((1, h, D_MODEL), lambda j, core_ref: (j, 0, 0))],
            out_specs=pl.BlockSpec((1, h, D_MODEL), lambda j, core_ref: (j, 0, 0))),
        out_shape=SDS((N_CHIPS, h, D_MODEL), BF16),
        compiler_params=pltpu.CompilerParams(dimension_semantics=("arbitrary",), vmem_limit_bytes=VMEM_LIMIT_V7X),
    )(core, grads, received)


SMALL_ROWS = 8


def _sum_blocks(ref):
    return (ref[0].astype(F32) + ref[1].astype(F32)) + (ref[2].astype(F32) + ref[3].astype(F32))


def _tail_reduce(last_grads, exchanged, small):
    n = len(exchanged)
    h = last_grads.shape[2]

    def body(*refs):
        g_ref, ex, small_ref = refs[0], refs[1:1 + n], refs[1 + n]
        o0 = 2 + n
        out, out_last, small_out = refs[o0:o0 + n], refs[o0 + n], refs[o0 + n + 1]
        s0 = o0 + n + 2
        halves, half_last = refs[s0:s0 + n], refs[s0 + n]
        own, got, part, exch, small_buf = refs[s0 + n + 1:s0 + n + 6]
        ex_buf = refs[s0 + n + 6:s0 + 2 * n + 6]
        pair_send, pair_recv, chip_send, chip_recv, share_send, share_recv, small_send, small_recv, local_sems = refs[s0 + 2 * n + 6:]
        x, y, c = _place()
        sibling = (x, y, 1 - c)
        my_chip, me = 2 * x + y, _slot(x, y, c)
        chips = _other_chips(x, y)[::-1]

        order = [2 * chip[0] + chip[1] for chip in chips] + [my_chip]
        to_sibling = [_remote(g_ref.at[j, 1 - c], got.at[j], pair_send.at[j], pair_recv.at[j], sibling) for j in order]
        load_own = [pltpu.make_async_copy(g_ref.at[j, c], own.at[j], local_sems.at[j]) for j in order]
        load_ex = [pltpu.make_async_copy(ex[k], ex_buf[k], local_sems.at[N_CHIPS + n + 1 + k]) for k in range(n)]
        for give, keep in zip(to_sibling, load_own):
            give.start()
            keep.start()
        for cp in load_ex:
            cp.start()

        small_buf[me] = small_ref[...]
        small_copies = []
        for mask in range(1, 8):
            peer = (x ^ (mask >> 2), y ^ ((mask >> 1) & 1), c ^ (mask & 1))
            small_copies.append(_remote(small_ref, small_buf.at[me], small_send.at[mask - 1], small_recv.at[mask - 1], peer))
        for cp in small_copies:
            cp.start()

        def share(k, half_ref, out_ref):
            keep = pltpu.make_async_copy(half_ref, out_ref.at[c], local_sems.at[N_CHIPS + k])
            give = _remote(half_ref, out_ref.at[c], share_send.at[k], share_recv.at[k], sibling)
            take = _remote(half_ref, out_ref.at[1 - c], share_send.at[k], share_recv.at[k], sibling)
            keep.start()
            give.start()
            return keep, give, take

        def pair_sum(block):
            _remote(g_ref.at[block, 1 - c], got.at[block], pair_send.at[block], pair_recv.at[block], sibling).wait_recv()
            pltpu.make_async_copy(g_ref.at[block, c], own.at[block], local_sems.at[block]).wait()
            part[block] = (own[block] + got[block]).astype(BF16)

        to_chips = []
        for j, chip in enumerate(chips):
            block = 2 * chip[0] + chip[1]
            pair_sum(block)
            to_chips.append(_remote(part.at[block], exch.at[my_chip], chip_send.at[j], chip_recv.at[j], (*chip, c)))
            to_chips[-1].start()
        pair_sum(my_chip)
        exch[my_chip] = part[my_chip]
        from_chips = [_remote(part.at[my_chip], exch.at[2 * chip[0] + chip[1]], chip_send.at[j], chip_recv.at[j], (*chip, c))
                      for j, chip in enumerate(chips)]

        shares = []
        for k in range(n):
            load_ex[k].wait()
            halves[k][...] = _sum_blocks(ex_buf[k])
            shares.append(share(k, halves[k], out[k]))

        for cp in small_copies:
            cp.wait_recv()
        total = small_buf[0]
        for d in range(1, 8):
            total = total + small_buf[d]
        small_out[...] = total

        for cp in from_chips:
            cp.wait_recv()
        half_last[...] = _sum_blocks(exch)
        shares.append(share(n, half_last, out_last))

        for keep, give, take in shares:
            take.wait_recv()
            give.wait_send()
            keep.wait()
        for cp in to_sibling + to_chips + small_copies:
            cp.wait_send()

    blocks = (N_CHIPS, h, D_MODEL)
    return pl.pallas_call(
        body, name="tail_reduce",
        in_specs=[ANY] * (n + 1) + [VMEM_WHOLE], out_specs=[ANY] * (n + 1) + [VMEM_WHOLE],
        out_shape=[SDS((2,) + e.shape[1:], F32) for e in exchanged] + [SDS((2, h, D_MODEL), F32), SDS(small.shape, F32)],
        scratch_shapes=[pltpu.VMEM(e.shape[1:], F32) for e in exchanged] + [pltpu.VMEM((h, D_MODEL), F32)]
                       + [pltpu.VMEM(blocks, F32), pltpu.VMEM(blocks, F32), pltpu.VMEM(blocks, BF16), pltpu.VMEM(blocks, BF16),
                          pltpu.VMEM((8,) + small.shape, F32)]
                       + [pltpu.VMEM(e.shape, BF16) for e in exchanged]
                       + [pltpu.SemaphoreType.DMA((N_CHIPS,)), pltpu.SemaphoreType.DMA((N_CHIPS,)),
                          pltpu.SemaphoreType.DMA((3,)), pltpu.SemaphoreType.DMA((3,)),
                          pltpu.SemaphoreType.DMA((n + 1,)), pltpu.SemaphoreType.DMA((n + 1,)),
                          pltpu.SemaphoreType.DMA((7,)), pltpu.SemaphoreType.DMA((7,)),
                          pltpu.SemaphoreType.DMA((N_CHIPS + 2 * n + 1,))],
        compiler_params=pltpu.CompilerParams(vmem_limit_bytes=VMEM_LIMIT_V7X),
    )(last_grads, *exchanged, small)


def _rope_expansion():
    half = ROT_DIM // 2
    expand = np.zeros((2 * half, 3 * 128), np.float32)
    const = np.zeros((1, 3 * 128), np.float32)
    for lane in range(128):
        d = lane % HEAD_DIM
        if d < ROT_DIM:
            expand[d % half, lane] = 1.0
        else:
            const[0, lane] = 1.0
        if d < half:
            expand[half + d, 128 + lane] = -1.0
        elif d < ROT_DIM:
            expand[half + d - half, 256 + lane] = 1.0
    return expand, const


ROPE_PIECES = 3 * ROT_DIM


def _rope_inputs(seq):
    pos = jnp.arange(seq, dtype=F32)
    inv_freq = ROPE_THETA ** (-jnp.arange(0, ROT_DIM, 2, dtype=F32) / ROT_DIM)
    ang = pos[:, None] * inv_freq[None, :]
    cs = jnp.concatenate([jnp.cos(ang), jnp.sin(ang)], axis=1)
    hi = lax.reduce_precision(cs, 8, 7)
    mid = lax.reduce_precision(cs - hi, 8, 7)
    low = cs - hi - mid
    expand, const = _rope_expansion()
    pieces = jnp.concatenate([hi, mid, low], axis=1).astype(BF16)
    return pieces, jnp.asarray(np.concatenate([expand] * 3, axis=0), BF16), jnp.asarray(const)


def _rope_specs(tb):
    return [pl.BlockSpec((tb, ROPE_PIECES), lambda i: (i, 0)), _resident((ROPE_PIECES, 3 * 128)), _resident((1, 3 * 128))]


def _rope_tile(pieces_ref, expand_ref, const_ref):
    tables = _dot(pieces_ref[...], expand_ref[...]) + const_ref[...]
    return tables[:, 0:128], tables[:, 128:256], tables[:, 256:384]


def _rope(t, c, sa, sb):
    half = ROT_DIM // 2
    return t * c + pltpu.roll(t, 128 - half, 1) * sa + pltpu.roll(t, half, 1) * sb


def _rope_transposed(dt, c, sa, sb):
    half = ROT_DIM // 2
    return dt * c + pltpu.roll(dt * sa, half, 1) + pltpu.roll(dt * sb, 128 - half, 1)


def _in_proj(x, g_pre, w_in_t, rope, comm=None):
    seq = x.shape[0]
    tb = min(seq, WIDE_TOKEN_TILE)

    def body(x_ref, g_ref, w_ref, c_ref, sa_ref, sb_ref,
             q_ref, kd0_ref, kd1_ref, vd0_ref, vd1_ref, gb_ref, gc_ref, xin_ref, hn_ref):
        xv = x_ref[...]
        hn = (xv * _rms(xv) * g_ref[...]).astype(BF16)
        hn_ref[...] = hn
        proj = _dot_nt(hn, w_ref[...].reshape(IN_COLS, D_MODEL))
        c, sa, sb = _rope_tile(c_ref, sa_ref, sb_ref)
        scale = 1.0 / math.sqrt(HEAD_DIM)
        for p in range(Q_WIDTH // 128):
            q_ref[:, 128 * p:128 * (p + 1)] = (_rope(proj[:, 128 * p:128 * (p + 1)], c, sa, sb) * scale).astype(BF16)
        k = _rope(proj[:, Q_WIDTH:Q_WIDTH + KV_WIDTH], c, sa, sb)
        v = proj[:, Q_WIDTH + KV_WIDTH:Q_WIDTH + 2 * KV_WIDTH]
        low = _lane_lt64(k.shape)
        k_sw, v_sw = pltpu.roll(k, HEAD_DIM, 1), pltpu.roll(v, HEAD_DIM, 1)
        kd0_ref[...] = jnp.where(low, k, k_sw).astype(BF16)
        kd1_ref[...] = jnp.where(low, k_sw, k).astype(BF16)
        vd0_ref[...] = jnp.where(low, v, v_sw).astype(BF16)
        vd1_ref[...] = jnp.where(low, v_sw, v).astype(BF16)
        base = Q_WIDTH + 2 * KV_WIDTH
        gb_ref[...] = proj[:, base:base + CONV_WIDTH].astype(BF16)
        gc_ref[...] = proj[:, base + CONV_WIDTH:base + 2 * CONV_WIDTH].astype(BF16)
        xin_ref[...] = proj[:, base + 2 * CONV_WIDTH:base + 3 * CONV_WIDTH].astype(BF16)

    tile = lambda w: pl.BlockSpec((tb, w), lambda i: (i, 0))
    return _pallas(
        body, name="in_proj", grid=(seq // tb,),
        in_specs=[tile(D_MODEL), _resident((1, D_MODEL)), _resident(w_in_t.shape), *_rope_specs(tb)],
        out_specs=[tile(Q_WIDTH), tile(128), tile(128), tile(128), tile(128),
                   tile(CONV_WIDTH), tile(CONV_WIDTH), tile(CONV_WIDTH), tile(D_MODEL)],
        out_shape=[SDS((seq, Q_WIDTH), BF16)] + [SDS((seq, 128), BF16)] * 4
                  + [SDS((seq, CONV_WIDTH), BF16)] * 3 + [SDS((seq, D_MODEL), BF16)],
        operands=(x, g_pre, w_in_t, *rope), comm=comm)


def _attn_valid(i):
    shape = (4 * QBLOCK, 2 * QBLOCK)
    row = lax.broadcasted_iota(jnp.int32, shape, 0)
    col = lax.broadcasted_iota(jnp.int32, shape, 1)
    qi = row & (QBLOCK - 1)
    return (col > qi) & (col <= qi + QBLOCK) & ((col >= QBLOCK) | (i > 0))


def _stack_heads(pair0, pair1):
    low = _lane_lt64(pair0.shape)
    zero = jnp.zeros_like(pair0)
    return jnp.concatenate([jnp.where(low, pair0, zero), jnp.where(low, zero, pair0),
                            jnp.where(low, pair1, zero), jnp.where(low, zero, pair1)], axis=0)


def _unstack_heads(stacked):
    low = _lane_lt64((QBLOCK, 128))
    pair0 = jnp.where(low, stacked[0:QBLOCK], stacked[QBLOCK:2 * QBLOCK])
    pair1 = jnp.where(low, stacked[2 * QBLOCK:3 * QBLOCK], stacked[3 * QBLOCK:4 * QBLOCK])
    return pair0, pair1


def _sink_column(sink_ref, kv_head):
    row = lax.broadcasted_iota(jnp.int32, (4 * QBLOCK, 1), 0)
    s = [sink_ref[0, 4 * kv_head + j] for j in range(4)]
    return jnp.where(row < QBLOCK, s[0], jnp.where(row < 2 * QBLOCK, s[1], jnp.where(row < 3 * QBLOCK, s[2], s[3])))


def _band(ref, i):
    prev = pl.multiple_of(jnp.maximum(i - 1, 0) * QBLOCK, QBLOCK)
    own = pl.multiple_of(i * QBLOCK, QBLOCK)
    return jnp.concatenate([ref[pl.ds(prev, QBLOCK), :], ref[pl.ds(own, QBLOCK), :]], axis=0), prev, own


def _softmax_with_sink(s, sink_col):
    m = jnp.maximum(jnp.max(s, axis=-1, keepdims=True), sink_col)
    p = jnp.exp(s - m)
    e_sink = jnp.exp(sink_col - m)
    inv_l = 1.0 / (jnp.sum(p, axis=-1, keepdims=True) + e_sink)
    return p, e_sink, inv_l


def _attention_fwd(q, kd0, kd1, vd0, vd1, sinks, comm=None):
    seq = q.shape[0]

    nb = ATTN_FWD_BLOCKS

    def body(sink_ref, q_ref, kd0_ref, kd1_ref, vd0_ref, vd1_ref, o_ref):
        for b in range(nb):
            i = pl.program_id(0) * nb + b
            rows = slice(QBLOCK * b, QBLOCK * (b + 1))
            valid = _attn_valid(i)
            for kv_head, (k_ref, v_ref) in enumerate(((kd0_ref, vd0_ref), (kd1_ref, vd1_ref))):
                kband, _, _ = _band(k_ref, i)
                vband, _, _ = _band(v_ref, i)
                base = 256 * kv_head
                qm = _stack_heads(q_ref[rows, base:base + 128], q_ref[rows, base + 128:base + 256])
                s = jnp.where(valid, _dot_nt(qm, kband), NEG_INF)
                p, _, inv_l = _softmax_with_sink(s, _sink_column(sink_ref, kv_head))
                o = _dot(p.astype(BF16), vband) * inv_l
                pair0, pair1 = _unstack_heads(o)
                o_ref[rows, base:base + 128] = pair0.astype(BF16)
                o_ref[rows, base + 128:base + 256] = pair1.astype(BF16)

    blk = pl.BlockSpec((nb * QBLOCK, Q_WIDTH), lambda i: (i, 0))
    full = _resident((seq, 128))
    return _pallas(
        body, name="attention_fwd", grid=(seq // (nb * QBLOCK),),
        in_specs=[pl.BlockSpec(memory_space=pltpu.SMEM), blk, full, full, full, full],
        out_specs=[blk], out_shape=[SDS((seq, Q_WIDTH), BF16)],
        operands=(sinks, q, kd0, kd1, vd0, vd1), comm=comm)


HALO = 16


def _conv_parts(gc, xin, gc_halo, xin_halo, conv_w, first):
    tb = gc.shape[0]
    u = gc.astype(F32) * xin.astype(F32)
    u_halo = jnp.where(first, 0.0, gc_halo.astype(F32) * xin_halo.astype(F32))
    ext = jnp.concatenate([u_halo, u], axis=0)
    u1 = pltpu.roll(ext, 1, 0)[HALO:HALO + tb]
    u2 = pltpu.roll(ext, 2, 0)[HALO:HALO + tb]
    y = conv_w[0:1, :] * u2 + conv_w[1:2, :] * u1 + conv_w[2:3, :] * u
    return u, u1, u2, y


def _halo_prev(tb, w):
    return pl.BlockSpec((HALO, w), lambda i: (jnp.maximum(i * (tb // HALO) - 1, 0), 0))


def _residual_mid(x, mix, g_post_mix):
    mix_f = mix.astype(F32)
    return x + mix_f * _rms(mix_f) * g_post_mix


def _mix_out(attn, gb, gc, xin, conv_w, g_attn, g_conv, w_out, comm=None):
    seq = attn.shape[0]
    tb = min(seq, WIDE_TOKEN_TILE)

    def body(a_ref, gb_ref, gc_ref, xin_ref, gch_ref, xinh_ref, cw_ref, ga_ref, gcn_ref, w_ref, mix_ref, mixed_ref):
        first = pl.program_id(0) == 0
        _, _, _, y = _conv_parts(gc_ref[...], xin_ref[...], gch_ref[...], xinh_ref[...], cw_ref[...], first)
        conv = gb_ref[...].astype(F32) * y
        a = a_ref[...].astype(F32)
        mixed_ref[:, 0:Q_WIDTH] = (a * _rms(a) * ga_ref[...]).astype(BF16)
        mixed_ref[:, Q_WIDTH:] = (conv * _rms(conv) * gcn_ref[...]).astype(BF16)
        mix_ref[...] = _dot(mixed_ref[...], w_ref[...].reshape(D_MODEL, D_MODEL)).astype(BF16)

    tile = lambda w: pl.BlockSpec((tb, w), lambda i: (i, 0))
    return _pallas(
        body, name="mix_out", grid=(seq // tb,),
        in_specs=[tile(Q_WIDTH), tile(CONV_WIDTH), tile(CONV_WIDTH), tile(CONV_WIDTH),
                  _halo_prev(tb, CONV_WIDTH), _halo_prev(tb, CONV_WIDTH),
                  _resident((CONV_K, CONV_WIDTH)), _resident((1, Q_WIDTH)), _resident((1, CONV_WIDTH)),
                  _resident(w_out.shape)],
        out_specs=[tile(D_MODEL), tile(D_MODEL)],
        out_shape=[SDS((seq, D_MODEL), BF16), SDS((seq, D_MODEL), BF16)],
        operands=(attn, gb, gc, xin, gc, xin, conv_w, g_attn, g_conv, w_out), comm=comm)


def _mlp_fwd_bwd(x, mix, target, g_post_mix, g_pre_mlp, g_post_mlp, w_up, w_down):
    seq = x.shape[0]
    tb = TOKEN_TILE

    def body(x_ref, mix_ref, t_ref, gpm_ref, g2_ref, g4_ref, wup_ref, wdown_ref,
             up_ref, hn2_ref, dmlp_ref, dup_ref, dh_ref, dmix_ref, loss_ref, dg4_ref, dg2_ref, dgpm_ref):
        @pl.when(pl.program_id(0) == 0)
        def _():
            for ref in (loss_ref, dg4_ref, dg2_ref, dgpm_ref):
                ref[...] = jnp.zeros_like(ref)

        halves = [slice(0, tb // 2), slice(tb // 2, tb)]
        chunks = [slice(1024 * j, 1024 * (j + 1)) for j in range(N_CHIPS)]
        hv, hn2, mlp, dout, dmlp, dhn2 = [], [], [], [], [], []
        for rows in halves:
            hv.append(_residual_mid(x_ref[rows, :], mix_ref[rows, :], gpm_ref[...]))
            hn2.append((hv[-1] * _rms(hv[-1]) * g2_ref[...]).astype(BF16))
            hn2_ref[rows, :] = hn2[-1]
        for k, rows in enumerate(halves):
            acc = None
            for j, cols in enumerate(chunks):
                up = jnp.maximum(_dot(hn2[k], _chip_block(wup_ref, j)), 0.0)
                up_ref[rows, cols] = up.astype(BF16)
                part = _dot((up * up).astype(BF16), _chip_block(wdown_ref, j))
                acc = part if acc is None else acc + part
            mlp.append(acc)
        loss = jnp.zeros((1, 1), F32)
        dg4 = jnp.zeros((1, D_MODEL), F32)
        for k, rows in enumerate(halves):
            rstd = _rms(mlp[k])
            zhat = mlp[k] * rstd
            diff = hv[k] + zhat * g4_ref[...] - t_ref[rows, :]
            loss = loss + jnp.sum(jnp.sum(diff * diff, axis=1, keepdims=True), axis=0, keepdims=True)
            dout.append(diff * (1.0 / D_MODEL))
            dg4 = dg4 + _colsum(dout[k] * zhat)
            dmlp.append(_norm_bwd(dout[k], g4_ref[...], zhat, rstd).astype(BF16))
            dmlp_ref[rows, :] = dmlp[k]
        for k, rows in enumerate(halves):
            acc = None
            for j, cols in enumerate(chunks):
                dact = _dot_nt(dmlp[k], _chip_block(wdown_ref, j))
                dup = (dact * (2.0 * up_ref[rows, cols].astype(F32))).astype(BF16)
                dup_ref[rows, cols] = dup
                part = _dot_nt(dup, _chip_block(wup_ref, j))
                acc = part if acc is None else acc + part
            dhn2.append(acc)
        dg2 = jnp.zeros((1, D_MODEL), F32)
        dgpm = jnp.zeros((1, D_MODEL), F32)
        for k, rows in enumerate(halves):
            r2 = _rms(hv[k])
            hhat = hv[k] * r2
            dg2 = dg2 + _colsum(dhn2[k] * hhat)
            dh = dout[k] + _norm_bwd(dhn2[k], g2_ref[...], hhat, r2)
            dh_ref[rows, :] = dh.astype(BF16)
            mix_v = mix_ref[rows, :].astype(F32)
            rz = _rms(mix_v)
            zhat = mix_v * rz
            dgpm = dgpm + _colsum(dh * zhat)
            dmix_ref[rows, :] = _norm_bwd(dh, gpm_ref[...], zhat, rz).astype(BF16)
        loss_ref[...] += loss
        dg4_ref[...] += dg4
        dg2_ref[...] += dg2
        dgpm_ref[...] += dgpm

    tile = lambda w: pl.BlockSpec((tb, w), lambda i: (i, 0))
    vec = pl.BlockSpec((1, D_MODEL), lambda i: (0, 0))
    return _pallas(
        body, name="mlp_fwd_bwd", grid=(seq // tb,),
        in_specs=[tile(D_MODEL), tile(D_MODEL), tile(D_MODEL), _resident((1, D_MODEL)), _resident((1, D_MODEL)),
                  _resident((1, D_MODEL)), _resident(w_up.shape), _resident(w_down.shape)],
        out_specs=[tile(D_FF), tile(D_MODEL), tile(D_MODEL), tile(D_FF), tile(D_MODEL), tile(D_MODEL),
                   pl.BlockSpec((1, 1), lambda i: (0, 0)), vec, vec, vec],
        out_shape=[SDS((seq, D_FF), BF16), SDS((seq, D_MODEL), BF16), SDS((seq, D_MODEL), BF16), SDS((seq, D_FF), BF16),
                   SDS((seq, D_MODEL), BF16), SDS((seq, D_MODEL), BF16),
                   SDS((1, 1), F32), SDS((1, D_MODEL), F32), SDS((1, D_MODEL), F32), SDS((1, D_MODEL), F32)],
        operands=(x, mix, target, g_post_mix, g_pre_mlp, g_post_mlp, w_up, w_down))


def _mix_bwd(dmix, attn, gb, gc, xin, conv_w, g_attn, g_conv, w_out, n_k):
    seq = attn.shape[0]
    tb = seq // (N_CHIPS * n_k)

    def body(first, dmix_ref, a_ref, gb_ref, gc_ref, xin_ref, gch_ref, xinh_ref, cw_ref, ga_ref, gcn_ref, w_ref,
             dattn_ref, dgb_ref, dy_ref, dga_ref, dgcn_ref, dcw_ref):
        @pl.when(first)
        def _():
            dga_ref[...] = jnp.zeros_like(dga_ref)
            dgcn_ref[...] = jnp.zeros_like(dgcn_ref)
            dcw_ref[...] = jnp.zeros_like(dcw_ref)

        dmixed = _dot_nt(dmix_ref[...], w_ref[...].reshape(D_MODEL, D_MODEL))
        a = a_ref[...].astype(F32)
        ra = _rms(a)
        ahat = a * ra
        dan = dmixed[:, 0:Q_WIDTH]
        dga_ref[...] += _colsum(dan * ahat)
        dattn_ref[...] = _norm_bwd(dan, ga_ref[...], ahat, ra).astype(BF16)
        gbv = gb_ref[...].astype(F32)
        u, u1, u2, y = _conv_parts(gc_ref[...], xin_ref[...], gch_ref[...], xinh_ref[...], cw_ref[...], first)
        conv = gbv * y
        rc = _rms(conv)
        chat = conv * rc
        dcn = dmixed[:, Q_WIDTH:]
        dgcn_ref[...] += _colsum(dcn * chat)
        dconv = _norm_bwd(dcn, gcn_ref[...], chat, rc)
        dgb_ref[...] = (dconv * y).astype(BF16)
        dy = dconv * gbv
        dy_ref[...] = dy.astype(BF16)
        dcw_ref[0:1, :] += _colsum(dy * u2)
        dcw_ref[1:2, :] += _colsum(dy * u1)
        dcw_ref[2:3, :] += _colsum(dy * u)

    tile = lambda w: pl.BlockSpec((tb, w), lambda j, k: (j * n_k + k, 0))
    halo = lambda w: pl.BlockSpec((HALO, w), lambda j, k: (jnp.maximum((j * n_k + k) * (tb // HALO) - 1, 0), 0))
    whole = lambda shape: pl.BlockSpec(shape, lambda j, k: (0,) * len(shape))
    return _Rider(
        body,
        in_specs=[tile(D_MODEL), tile(Q_WIDTH), tile(CONV_WIDTH), tile(CONV_WIDTH), tile(CONV_WIDTH),
                  halo(CONV_WIDTH), halo(CONV_WIDTH),
                  _resident((CONV_K, CONV_WIDTH)), _resident((1, Q_WIDTH)), _resident((1, CONV_WIDTH)),
                  _resident(w_out.shape)],
        out_specs=[tile(Q_WIDTH), tile(CONV_WIDTH), tile(CONV_WIDTH),
                   whole((1, Q_WIDTH)), whole((1, CONV_WIDTH)), whole((CONV_K, CONV_WIDTH))],
        out_shape=[SDS((seq, Q_WIDTH), BF16), SDS((seq, CONV_WIDTH), BF16), SDS((seq, CONV_WIDTH), BF16),
                   SDS((1, Q_WIDTH), F32), SDS((1, CONV_WIDTH), F32), SDS((CONV_K, CONV_WIDTH), F32)],
        operands=(dmix, attn, gb, gc, xin, gc, xin, conv_w, g_attn, g_conv, w_out))


def _attention_bwd(q, dattn, attn, kd0, kd1, vd0, vd1, sinks, comm=None):
    seq = q.shape[0]
    nb = ATTN_BWD_BLOCKS

    def body(sink_ref, q_ref, do_ref, o_ref, kd0_ref, kd1_ref, vd0_ref, vd1_ref,
             dq_ref, dk0_ref, dk1_ref, dv0_ref, dv1_ref, dsink_ref):
        @pl.when(pl.program_id(0) == 0)
        def _():
            for r in (dk0_ref, dk1_ref, dv0_ref, dv1_ref, dsink_ref):
                r[...] = jnp.zeros_like(r)

        lane = lax.broadcasted_iota(jnp.int32, (1, 128), 1)
        dsink = jnp.zeros((1, 128), F32)
        for b in range(nb):
            i = pl.program_id(0) * nb + b
            rows = slice(QBLOCK * b, QBLOCK * (b + 1))
            valid = _attn_valid(i)
            for kv_head, (k_ref, v_ref, dk_ref, dv_ref) in enumerate(
                    ((kd0_ref, vd0_ref, dk0_ref, dv0_ref), (kd1_ref, vd1_ref, dk1_ref, dv1_ref))):
                kband, prev, own = _band(k_ref, i)
                vband, _, _ = _band(v_ref, i)
                base = 256 * kv_head
                qm = _stack_heads(q_ref[rows, base:base + 128], q_ref[rows, base + 128:base + 256])
                dom = _stack_heads(do_ref[rows, base:base + 128], do_ref[rows, base + 128:base + 256])
                om = _stack_heads(o_ref[rows, base:base + 128], o_ref[rows, base + 128:base + 256])
                s = jnp.where(valid, _dot_nt(qm, kband), NEG_INF)
                p, e_sink, inv_l = _softmax_with_sink(s, _sink_column(sink_ref, kv_head))
                p = p * inv_l
                delta = jnp.sum(dom.astype(F32) * om.astype(F32), axis=-1, keepdims=True)
                ds = (p * (_dot_nt(dom, vband) - delta)).astype(BF16)
                sink_term = -(e_sink * inv_l) * delta
                for j in range(4):
                    part = jnp.sum(sink_term[QBLOCK * j:QBLOCK * (j + 1)], axis=0, keepdims=True)
                    dsink = dsink + jnp.where(lane == 4 * kv_head + j, part, 0.0)
                pair0, pair1 = _unstack_heads(_dot(ds, kband))
                dq_ref[rows, base:base + 128] = pair0.astype(BF16)
                dq_ref[rows, base + 128:base + 256] = pair1.astype(BF16)
                dkd = _dot_tn(ds, qm)
                dkd = dkd + pltpu.roll(dkd, HEAD_DIM, 1)
                dvd = _dot_tn(p.astype(BF16), dom)
                dvd = dvd + pltpu.roll(dvd, HEAD_DIM, 1)
                dk_ref[pl.ds(prev, QBLOCK), :] += dkd[0:QBLOCK]
                dk_ref[pl.ds(own, QBLOCK), :] += dkd[QBLOCK:]
                dv_ref[pl.ds(prev, QBLOCK), :] += dvd[0:QBLOCK]
                dv_ref[pl.ds(own, QBLOCK), :] += dvd[QBLOCK:]
        dsink_ref[...] += dsink

    blk = pl.BlockSpec((nb * QBLOCK, Q_WIDTH), lambda i: (i, 0))
    full = _resident((seq, 128))
    acc = pl.BlockSpec((seq, 128), lambda i: (0, 0))
    return _pallas(
        body, name="attention_bwd", grid=(seq // (nb * QBLOCK),),
        in_specs=[pl.BlockSpec(memory_space=pltpu.SMEM), blk, blk, blk, full, full, full, full],
        out_specs=[blk, acc, acc, acc, acc, pl.BlockSpec((1, 128), lambda i: (0, 0))],
        out_shape=[SDS((seq, Q_WIDTH), BF16)] + [SDS((seq, 128), F32)] * 4 + [SDS((1, 128), F32)],
        operands=(sinks, q, dattn, attn, kd0, kd1, vd0, vd1), comm=comm)


def _in_proj_bwd(dq, dk0, dk1, dv0, dv1, dgb, dy, gc, xin, conv_w, x, dh, g_pre, w_in_t, rope):
    seq = x.shape[0]
    tb = min(seq, WIDE_TOKEN_TILE)
    n_tiles = seq // tb

    def body(dq_ref, dk0_ref, dk1_ref, dv0_ref, dv1_ref, dgb_ref, dy_ref, dyh_ref, gc_ref, xin_ref, cw_ref,
             x_ref, dh_ref, g_ref, w_ref, c_ref, sa_ref, sb_ref,
             dproj_ref, gx_ref, dg_ref):
        i = pl.program_id(0)

        @pl.when(i == 0)
        def _():
            dg_ref[...] = jnp.zeros_like(dg_ref)

        dy = dy_ref[...].astype(F32)
        ext = jnp.concatenate([dy, jnp.where(i == n_tiles - 1, 0.0, dyh_ref[...].astype(F32))], axis=0)
        dy1 = pltpu.roll(ext, tb + HALO - 1, 0)[0:tb]
        dy2 = pltpu.roll(ext, tb + HALO - 2, 0)[0:tb]
        cw = cw_ref[...]
        du = cw[2:3, :] * dy + cw[1:2, :] * dy1 + cw[0:1, :] * dy2
        scale = 1.0 / math.sqrt(HEAD_DIM)
        base = Q_WIDTH + 2 * KV_WIDTH
        halves = [slice(0, tb // 2), slice(tb // 2, tb)]
        low = _lane_lt64((tb // 2, 128))
        for rows in halves:
            c, sa, sb = _rope_tile(c_ref.at[rows, :], sa_ref, sb_ref)
            for p in range(Q_WIDTH // 128):
                dproj_ref[rows, 128 * p:128 * (p + 1)] = _rope_transposed(
                    dq_ref[rows, 128 * p:128 * (p + 1)].astype(F32) * scale, c, sa, sb).astype(BF16)
            dk = jnp.where(low, dk0_ref[rows, :], dk1_ref[rows, :])
            dproj_ref[rows, Q_WIDTH:Q_WIDTH + KV_WIDTH] = _rope_transposed(dk, c, sa, sb).astype(BF16)
            dproj_ref[rows, Q_WIDTH + KV_WIDTH:base] = jnp.where(low, dv0_ref[rows, :], dv1_ref[rows, :]).astype(BF16)
            dproj_ref[rows, base:base + CONV_WIDTH] = dgb_ref[rows, :]
            dproj_ref[rows, base + CONV_WIDTH:base + 2 * CONV_WIDTH] = (du[rows] * xin_ref[rows, :].astype(F32)).astype(BF16)
            dproj_ref[rows, base + 2 * CONV_WIDTH:] = (du[rows] * gc_ref[rows, :].astype(F32)).astype(BF16)
        w_all = w_ref[...].reshape(IN_COLS, D_MODEL)
        dhn = [_dot(dproj_ref[rows, :], w_all) for rows in halves]
        dg = jnp.zeros((1, D_MODEL), F32)
        for k, rows in enumerate(halves):
            xv = x_ref[rows, :]
            r = _rms(xv)
            xhat = xv * r
            dg = dg + _colsum(dhn[k] * xhat)
            gx_ref[rows, :] = dh_ref[rows, :].astype(F32) + _norm_bwd(dhn[k], g_ref[...], xhat, r)
        dg_ref[...] += dg

    tile = lambda w: pl.BlockSpec((tb, w), lambda i: (i, 0))
    halo_next = pl.BlockSpec((HALO, CONV_WIDTH), lambda i: (jnp.minimum((i + 1) * (tb // HALO), seq // HALO - 1), 0))
    return _pallas(
        body, name="in_proj_bwd", grid=(n_tiles,),
        in_specs=[tile(Q_WIDTH), tile(128), tile(128), tile(128), tile(128), tile(CONV_WIDTH), tile(CONV_WIDTH), halo_next,
                  tile(CONV_WIDTH), tile(CONV_WIDTH), _resident((CONV_K, CONV_WIDTH)),
                  tile(D_MODEL), tile(D_MODEL), _resident((1, D_MODEL)), _resident(w_in_t.shape), *_rope_specs(tb)],
        out_specs=[tile(IN_COLS), tile(D_MODEL), pl.BlockSpec((1, D_MODEL), lambda i: (0, 0))],
        out_shape=[SDS((seq, IN_COLS), BF16), SDS((seq, D_MODEL), F32), SDS((1, D_MODEL), F32)],
        operands=(dq, dk0, dk1, dv0, dv1, dgb, dy, dy, gc, xin, conv_w, x, dh, g_pre, w_in_t, *rope))


def _wgrad_grid(seq, per_chip, h_rows, with_rider=False):
    chips_per_step = 1 if per_chip else N_CHIPS
    m = chips_per_step * 2 * h_rows
    bt = min(seq, WGRAD_TOKEN_TILE if per_chip and not with_rider else WGRAD_TOKEN_TILE // 2)
    return chips_per_step, m, bt, seq // bt


def _wgrad(name, a, b, *, per_chip, h_rows, square_a=False, comm=None, rider=None):
    seq = a.shape[0]
    chips_per_step, m, bt, n_k = _wgrad_grid(seq, per_chip, h_rows, rider is not None)
    a_cols = m if per_chip else a.shape[1]
    a_wide = a.shape[1] > a_cols
    b_wide = b.shape[1] > D_MODEL

    def body(a_ref, b_ref, g_ref):
        @pl.when(pl.program_id(1) == 0)
        def _():
            g_ref[...] = jnp.zeros_like(g_ref)

        av = a_ref[...]
        if square_a:
            av = (av.astype(F32) * av.astype(F32)).astype(BF16)
        g_ref[...] += _dot_tn(av, b_ref[...]).reshape(g_ref.shape)

    a_spec = pl.BlockSpec((bt, a_cols), (lambda j, k: (k, j)) if a_wide else (lambda j, k: (k, 0)))
    b_spec = pl.BlockSpec((bt, D_MODEL), (lambda j, k: (k, j)) if b_wide else (lambda j, k: (k, 0)))
    g_spec = pl.BlockSpec((chips_per_step, 2, h_rows, D_MODEL), lambda j, k: (j, 0, 0, 0),
                          pipeline_mode=None if per_chip else pl.Buffered(1))
    return _pallas(
        body, name=name, grid=(N_CHIPS if per_chip else 1, n_k),
        in_specs=[a_spec, b_spec], out_specs=[g_spec], out_shape=[SDS((N_CHIPS, 2, h_rows, D_MODEL), F32)],
        operands=(a, b), comm=comm, rider=rider)


def _adamw_math(w, g, m, v):
    m = ADAM_B1 * m + (1.0 - ADAM_B1) * g
    v = ADAM_B2 * v + (1.0 - ADAM_B2) * (g * g)
    m_hat = m / (1.0 - ADAM_B1 ** ADAM_STEP)
    v_hat = v / (1.0 - ADAM_B2 ** ADAM_STEP)
    delta = -ADAM_LR * (m_hat / (jnp.sqrt(v_hat) + ADAM_EPS) + ADAM_WD * w)
    return delta, m, v


ADAMW_STEPS_PER_HALF = 4


def _adamw_rows(items):
    n = len(items)
    per_half = ADAMW_STEPS_PER_HALF

    def body(*refs):
        for k in range(n):
            r_ref, w_ref, m_ref, v_ref = refs[4 * k:4 * k + 4]
            g_out, d_out, m_out, v_out = refs[4 * (n + k):4 * (n + k) + 4]
            g = r_ref[0]
            g_out[...] = g
            d_out[...], m_out[...], v_out[...] = _adamw_math(w_ref[...], g, m_ref[...], v_ref[...])

    in_specs, out_specs, out_shape, operands = [], [], [], []
    for reduced, w, m, v in items:
        rt = reduced.shape[1] // per_half
        blk = pl.BlockSpec((rt, D_MODEL), lambda h, r: (h * per_half + r, 0))
        in_specs += [pl.BlockSpec((1, rt, D_MODEL), lambda h, r: (h, r, 0)), blk, blk, blk]
        out_specs += [blk] * 4
        out_shape += [SDS(w.shape, F32)] * 4
        operands += [reduced, w, m, v]
    res = _pallas(body, name="adamw_rows", grid=(2, per_half), in_specs=in_specs, out_specs=out_specs,
                  out_shape=out_shape, operands=tuple(operands))
    return [res[4 * k:4 * k + 4] for k in range(n)]


def _adamw_small(packed_grads, w, m, v):
    names = SMALL_NAMES
    n = len(names)
    conv_local = w["conv_w"].shape[-1]

    def body(*refs):
        gp = refs[0]
        w_refs, m_refs, v_refs = refs[1:1 + n], refs[1 + n:1 + 2 * n], refs[1 + 2 * n:1 + 3 * n]
        outs = refs[1 + 3 * n:]
        g_out, d_out, m_out, v_out = outs[0:n], outs[n:2 * n], outs[2 * n:3 * n], outs[3 * n:4 * n]
        chip = 2 * lax.axis_index("x") + lax.axis_index("y")

        def step(k, g, index=None):
            pick = (lambda r: r[...]) if index is None else (lambda r: r[index])
            d, new_m, new_v = _adamw_math(pick(w_refs[k]), g, pick(m_refs[k]), pick(v_refs[k]))
            for ref, val in ((g_out[k], g), (d_out[k], d), (m_out[k], new_m), (v_out[k], new_v)):
                if index is None:
                    ref[...] = val
                else:
                    ref[index] = val

        for k, name in enumerate(names):
            if name in SMALL_VECTORS:
                step(k, gp[SMALL_VECTORS.index(name):SMALL_VECTORS.index(name) + 1, :])
            elif name == "attn_group_norm":
                step(k, gp[4:5, 0:Q_WIDTH])
            elif name == "conv_group_norm":
                step(k, gp[4:5, Q_WIDTH:])
            elif name == "attn_sinks":
                step(k, gp[7:8, 0:8])
            else:
                for t in range(CONV_K):
                    row, base = 5 + t // 2, CONV_WIDTH * (t % 2)
                    g = gp[row:row + 1, base:base + conv_local]
                    for j in range(1, CONV_WIDTH // conv_local):
                        g = jnp.where(chip == j, gp[row:row + 1, base + conv_local * j:base + conv_local * (j + 1)], g)
                    step(k, g, index=(0, slice(t, t + 1), slice(None)))

    shapes = [SDS(w[name].shape, F32) for name in names]
    res = pl.pallas_call(
        body, name="adamw_small", in_specs=[VMEM_WHOLE] * (1 + 3 * n), out_specs=[VMEM_WHOLE] * (4 * n),
        out_shape=shapes * 4,
    )(packed_grads, *[w[k] for k in names], *[m[k] for k in names], *[v[k] for k in names])
    return [dict(zip(names, res[i * n:(i + 1) * n])) for i in range(4)]


SMALL_VECTORS = ("pre_mix_norm", "post_mix_norm", "pre_mlp_norm", "post_mlp_norm")
SMALL_NAMES = SMALL_VECTORS + ("attn_group_norm", "conv_group_norm", "conv_w", "attn_sinks")


def _pack_small(p):
    rows = [p[n].reshape(1, D_MODEL) for n in SMALL_VECTORS]
    rows.append(jnp.concatenate([p["attn_group_norm"].reshape(1, -1), p["conv_group_norm"].reshape(1, -1)], axis=1))
    cw = p["conv_w"].reshape(CONV_K, -1)
    rows.append(jnp.pad(cw, ((0, 1), (0, CONV_WIDTH - cw.shape[1]))).reshape(2, D_MODEL))
    last = jnp.concatenate([p["attn_sinks"].reshape(1, 8), p.get("loss_sum", jnp.zeros((1, 1), F32))], axis=1)
    rows.append(jnp.pad(last, ((0, 0), (0, D_MODEL - 9))))
    return jnp.concatenate(rows, axis=0)


WEIGHT_ORDER = ("pre_mix_norm", "w_in", "conv_w", "attn_sinks", "attn_group_norm", "conv_group_norm", "w_out",
                "post_mix_norm", "pre_mlp_norm", "w_up", "w_down", "post_mlp_norm")


def kernel(x, pre_mix_norm, w_in, conv_w, attn_sinks, attn_group_norm, conv_group_norm, w_out, post_mix_norm, pre_mlp_norm, w_up, w_down, post_mlp_norm, loss_target, m_pre_mix_norm, m_w_in, m_conv_w, m_attn_sinks, m_attn_group_norm, m_conv_group_norm, m_w_out, m_post_mix_norm, m_pre_mlp_norm, m_w_up, m_w_down, m_post_mlp_norm, v_pre_mix_norm, v_w_in, v_conv_w, v_attn_sinks, v_attn_group_norm, v_conv_group_norm, v_w_out, v_post_mix_norm, v_pre_mlp_norm, v_w_up, v_w_down, v_post_mlp_norm):
    w = dict(pre_mix_norm=pre_mix_norm, w_in=w_in, conv_w=conv_w, attn_sinks=attn_sinks, attn_group_norm=attn_group_norm,
             conv_group_norm=conv_group_norm, w_out=w_out, post_mix_norm=post_mix_norm, pre_mlp_norm=pre_mlp_norm,
             w_up=w_up, w_down=w_down, post_mlp_norm=post_mlp_norm)
    m = dict(pre_mix_norm=m_pre_mix_norm, w_in=m_w_in, conv_w=m_conv_w, attn_sinks=m_attn_sinks,
             attn_group_norm=m_attn_group_norm, conv_group_norm=m_conv_group_norm, w_out=m_w_out,
             post_mix_norm=m_post_mix_norm, pre_mlp_norm=m_pre_mlp_norm, w_up=m_w_up, w_down=m_w_down,
             post_mlp_norm=m_post_mlp_norm)
    v = dict(pre_mix_norm=v_pre_mix_norm, w_in=v_w_in, conv_w=v_conv_w, attn_sinks=v_attn_sinks,
             attn_group_norm=v_attn_group_norm, conv_group_norm=v_conv_group_norm, w_out=v_w_out,
             post_mix_norm=v_post_mix_norm, pre_mlp_norm=v_pre_mlp_norm, w_up=v_w_up, w_down=v_w_down,
             post_mlp_norm=v_post_mlp_norm)
    core = lax.axis_index("c").astype(jnp.int32).reshape(1)
    xs, target = x[0], loss_target[0]
    rope = _rope_inputs(xs.shape[0])

    conv_pad = jnp.pad(conv_w[0], ((0, 8 - CONV_K), (0, 0)))
    wf_in, conv_all, hb_up, hb_down, hb_out = _gather_whole(w_in[0].T, (w_up[0], w_down[0], w_out[0]), conv_pad)
    conv_full = conv_all[:, :CONV_K, :].transpose(1, 0, 2).reshape(CONV_K, CONV_WIDTH)

    whole_up, early, late = (0, H_UP), (0, DOWN_EARLY_ROWS), (DOWN_EARLY_ROWS, H_DOWN - DOWN_EARLY_ROWS)
    *proj, wf_up, wf_out, wf_down = _in_proj(
        xs, pre_mix_norm, wf_in, rope,
        comm=_merge(_relay(hb_up, None, first=whole_up), _gather_first(hb_out), _relay(hb_down, None, first=early)))
    q, kd0, kd1, vd0, vd1, gb, gc, xin, hn = proj
    attn, wf_up, wf_out, wf_down = _attention_fwd(
        q, kd0, kd1, vd0, vd1, attn_sinks,
        comm=_merge(_relay(None, wf_up, second=whole_up), _gather_second(wf_out),
                    _relay(hb_down, wf_down, first=late, second=early)))
    mix, mixed, wf_up, wf_down = _mix_out(
        attn, gb, gc, xin, conv_full, attn_group_norm, conv_group_norm, wf_out,
        comm=_merge(_relay(None, wf_up, third=whole_up), _relay(None, wf_down, second=late, third=early, third_after=late)))
    up, hn2, dmlp, dup, dh, dmix, loss_sum, dg_post_mlp, dg_pre_mlp, dg_post_mix = _mlp_fwd_bwd(
        xs, mix, target, post_mix_norm, pre_mlp_norm, post_mlp_norm, wf_up, wf_down)

    n_k = _wgrad_grid(xs.shape[0], True, H_DOWN, with_rider=True)[3]
    g_down, dattn, dgb, dy, dg_attn, dg_conv, dconv_w = _wgrad(
        "wgrad_down", up, dmlp, per_chip=True, h_rows=H_DOWN, square_a=True,
        rider=_mix_bwd(dmix, attn, gb, gc, xin, conv_full, attn_group_norm, conv_group_norm, wf_out, n_k))
    g_up, got_down = _wgrad("wgrad_up", hn2, dup, per_chip=True, h_rows=H_UP, comm=_pair_send(g_down))
    p_down = _pair_sum("pair_sum_down", core, g_down, got_down)
    g_out, got_up = _wgrad("wgrad_out", mixed, dmix, per_chip=False, h_rows=H_OUT, comm=_pair_send(g_up))
    p_up = _pair_sum("pair_sum_up", core, g_up, got_up)
    dq, dk0, dk1, dv0, dv1, dsink, ex_down, ex_up, got_out = _attention_bwd(
        q, dattn, attn, kd0, kd1, vd0, vd1, attn_sinks,
        comm=_merge(_chip_exchange(p_down), _chip_exchange(p_up), _pair_send(g_out)))
    p_out = _pair_sum("pair_sum_out", core, g_out, got_out)
    dproj, grad_x, dg_pre_mix = _in_proj_bwd(dq, dk0, dk1, dv0, dv1, dgb, dy, gc, xin, conv_full, xs, dh, pre_mix_norm,
                                             wf_in, rope)
    g_in, ex_out = _wgrad("wgrad_in", dproj, hn, per_chip=False, h_rows=H_IN, comm=_chip_exchange(p_out))
    small = dict(pre_mix_norm=dg_pre_mix, conv_w=dconv_w, attn_sinks=dsink[:, :8], attn_group_norm=dg_attn,
                 conv_group_norm=dg_conv, post_mix_norm=dg_post_mix, pre_mlp_norm=dg_pre_mlp, post_mlp_norm=dg_post_mlp,
                 loss_sum=loss_sum)
    r_down, r_up, r_out, r_in, small_total = _tail_reduce(g_in, [ex_down, ex_up, ex_out], _pack_small(small))

    out_g, out_d, out_m, out_v = {}, {}, {}, {}
    res_up, res_down, res_out, res_in_t = _adamw_rows([
        (r_up, w_up[0], m_w_up[0], v_w_up[0]), (r_down, w_down[0], m_w_down[0], v_w_down[0]),
        (r_out, w_out[0], m_w_out[0], v_w_out[0]), (r_in, w_in[0].T, m_w_in[0].T, v_w_in[0].T)])
    for name, res in (("w_up", res_up), ("w_down", res_down), ("w_out", res_out), ("w_in", [t.T for t in res_in_t])):
        out_g[name], out_d[name], out_m[name], out_v[name] = res

    loss = small_total[7, 8] * (0.5 / D_MODEL)
    for out, part in zip((out_g, out_d, out_m, out_v), _adamw_small(small_total, w, m, v)):
        out.update(part)

    def shaped(d):
        return [d[n].reshape(w[n].shape) for n in WEIGHT_ORDER]

    return (loss, grad_x[None], *shaped(out_g), *shaped(out_d), *shaped(out_m), *shaped(out_v))
```

```python
import math
from typing import Callable, NamedTuple

import jax
import jax.numpy as jnp
import numpy as np
from jax import lax
from jax.experimental import pallas as pl
from jax.experimental.pallas import tpu as pltpu

F32 = jnp.float32
BF16 = jnp.bfloat16

D_MODEL = 1024
HEAD_DIM = 64
Q_WIDTH = 512
KV_WIDTH = 128
CONV_WIDTH = 512
CONV_K = 3
D_FF = 4096
IN_COLS = 2304
QBLOCK = 128
ROT_DIM = 16
ROPE_THETA = 500000.0
NORM_EPS = 1e-6
NEG_INF = -1e30
N_CHIPS = 4

ADAM_LR = 0.001
ADAM_B1 = 0.9
ADAM_B2 = 0.999
ADAM_EPS = 1e-08
ADAM_WD = 0.01
ADAM_STEP = 10

H_UP, H_DOWN, H_OUT, H_IN = 512, 512, 128, 288
DOWN_EARLY_ROWS = 224

TOKEN_TILE = 512
WIDE_TOKEN_TILE = 1024
ATTN_FWD_BLOCKS = 16
ATTN_BWD_BLOCKS = 2
WGRAD_TOKEN_TILE = 4096
VMEM_LIMIT_V7X = 60 * 1024 * 1024

MESH = pl.DeviceIdType.MESH
ANY = pl.BlockSpec(memory_space=pl.ANY)
VMEM_WHOLE = pl.BlockSpec(memory_space=pltpu.VMEM)
SDS = jax.ShapeDtypeStruct


def _resident(shape):
    zeros = (0,) * len(shape)
    return pl.BlockSpec(shape, lambda *_: zeros, pipeline_mode=pl.Buffered(1))


def _rms(v):
    return lax.rsqrt(jnp.mean(v * v, axis=-1, keepdims=True) + NORM_EPS)


def _norm_bwd(dy, gain, vhat, rstd):
    t = dy * gain
    return rstd * (t - vhat * jnp.mean(t * vhat, axis=-1, keepdims=True))


def _colsum(v):
    return jnp.sum(v, axis=0, keepdims=True)


def _dot_nt(a, b):
    return lax.dot_general(a, b, (((1,), (1,)), ((), ())), preferred_element_type=F32)


def _dot_tn(a, b):
    return lax.dot_general(a, b, (((0,), (0,)), ((), ())), preferred_element_type=F32)


def _dot(a, b):
    return jnp.dot(a, b, preferred_element_type=F32)


def _chip_block(w_ref, chip):
    both = w_ref[pl.ds(2 * chip, 2)]
    return both.reshape(2 * both.shape[1], both.shape[2])


def _lane_lt64(shape):
    return lax.broadcasted_iota(jnp.int32, shape, 1) < HEAD_DIM


class _Comm(NamedTuple):
    operands: tuple
    out_shapes: tuple
    aliases: dict
    n_remote: int
    n_local: int
    plan: Callable
    after: Callable = None


def _merge(*comms):
    operands, out_shapes, aliases, parts = [], [], {}, []
    n_remote = n_local = 0
    for cm in comms:
        parts.append((len(operands), len(out_shapes), n_remote, n_local, cm))
        for k, v in cm.aliases.items():
            aliases[len(operands) + k] = len(out_shapes) + v
        operands += cm.operands
        out_shapes += cm.out_shapes
        n_remote += cm.n_remote
        n_local += cm.n_local

    def run(which, ins, outs, send, recv, loc):
        sends, recvs, locs = [], [], []
        for i0, o0, r0, l0, cm in parts:
            stage = getattr(cm, which)
            if stage is not None:
                s, r, l = stage(ins[i0:i0 + len(cm.operands)], outs[o0:o0 + len(cm.out_shapes)],
                                lambda k, r0=r0: send(r0 + k), lambda k, r0=r0: recv(r0 + k), lambda k, l0=l0: loc(l0 + k))
                sends, recvs, locs = sends + s, recvs + r, locs + l
        return sends, recvs, locs

    def plan(*args):
        return run("plan", *args)

    def after(*args):
        return run("after", *args)

    return _Comm(tuple(operands), tuple(out_shapes), aliases, n_remote, n_local, plan,
                 after if any(cm.after is not None for cm in comms) else None)


def _sem_scratch(comm):
    return [pltpu.SemaphoreType.DMA((max(comm.n_remote, 1),)), pltpu.SemaphoreType.DMA((max(comm.n_remote, 1),)),
            pltpu.SemaphoreType.DMA((max(comm.n_local, 1),))]


class _Rider(NamedTuple):
    body: Callable
    in_specs: list
    out_specs: list
    out_shape: list
    operands: tuple


def _pallas(body, *, name, grid, in_specs, out_specs, out_shape, operands, scratch=(), comm=None, rider=None):
    params = pltpu.CompilerParams(dimension_semantics=("arbitrary",) * len(grid), vmem_limit_bytes=VMEM_LIMIT_V7X)
    if rider is not None:
        own_in, own_out, ride_in, ride_out = len(in_specs), len(out_specs), len(rider.in_specs), len(rider.out_specs)
        own_body = body

        def body(*refs):
            o0 = own_in + ride_in
            s0 = o0 + own_out + ride_out
            own_body(*refs[:own_in], *refs[o0:o0 + own_out], *refs[s0:])
            first = None
            for axis in range(len(grid)):
                at_start = pl.program_id(axis) == 0
                first = at_start if first is None else jnp.logical_and(first, at_start)
            rider.body(first, *refs[own_in:o0], *refs[o0 + own_out:s0])

        in_specs, out_specs = list(in_specs) + rider.in_specs, list(out_specs) + rider.out_specs
        out_shape, operands = list(out_shape) + rider.out_shape, tuple(operands) + tuple(rider.operands)
    if comm is None:
        return pl.pallas_call(body, name=name, grid=grid, in_specs=in_specs, out_specs=out_specs, out_shape=out_shape,
                              scratch_shapes=list(scratch), compiler_params=params)(*operands)
    n_in, n_out, n_scr = len(in_specs), len(out_specs), len(scratch)
    c_in, c_out = len(comm.operands), len(comm.out_shapes)

    def with_comm(*refs):
        ins, c_ins = refs[:n_in], refs[n_in:n_in + c_in]
        o0 = n_in + c_in
        outs, c_outs = refs[o0:o0 + n_out], refs[o0 + n_out:o0 + n_out + c_out]
        s0 = o0 + n_out + c_out
        scr = refs[s0:s0 + n_scr]
        send_sems, recv_sems, local_sems = refs[s0 + n_scr:]
        first = last = None
        for axis, size in enumerate(grid):
            at_start, at_end = pl.program_id(axis) == 0, pl.program_id(axis) == size - 1
            first = at_start if first is None else jnp.logical_and(first, at_start)
            last = at_end if last is None else jnp.logical_and(last, at_end)

        def copies():
            return comm.plan(c_ins, c_outs, lambda k: send_sems.at[k], lambda k: recv_sems.at[k],
                             lambda k: local_sems.at[k])

        @pl.when(first)
        def _():
            sends, _, locs = copies()
            for cp in sends + locs:
                cp.start()

        body(*ins, *outs, *scr)

        @pl.when(last)
        def _():
            sends, recvs, locs = copies()
            for cp in recvs:
                cp.wait_recv()
            for cp in sends:
                cp.wait_send()
            for cp in locs:
                cp.wait()
            if comm.after is not None:
                sends, recvs, _ = comm.after(c_ins, c_outs, lambda k: send_sems.at[k], lambda k: recv_sems.at[k],
                                             lambda k: local_sems.at[k])
                for cp in sends:
                    cp.start()
                for cp in recvs:
                    cp.wait_recv()
                for cp in sends:
                    cp.wait_send()

    return pl.pallas_call(
        with_comm, name=name, grid=grid,
        in_specs=list(in_specs) + [ANY] * c_in, out_specs=list(out_specs) + [ANY] * c_out,
        out_shape=list(out_shape) + list(comm.out_shapes),
        scratch_shapes=list(scratch) + _sem_scratch(comm),
        input_output_aliases={n_in + k: n_out + v for k, v in comm.aliases.items()},
        compiler_params=params)(*operands, *comm.operands)


def _place():
    return lax.axis_index("x"), lax.axis_index("y"), lax.axis_index("c")


def _other_chips(x, y):
    return [(1 - x, y), (x, 1 - y), (1 - x, 1 - y)]


def _slot(px, py, pc):
    return 4 * px + 2 * py + pc


def _remote(src, dst, send_sem, recv_sem, to):
    return pltpu.make_async_remote_copy(src_ref=src, dst_ref=dst, send_sem=send_sem, recv_sem=recv_sem,
                                        device_id=to, device_id_type=MESH)


def _gather_first(half_block):
    def plan(ins, outs, send, recv, loc):
        (blk,), (full,) = ins, outs
        x, y, c = _place()
        chips = _other_chips(x, y)
        mine = full.at[_slot(x, y, c)]
        sends = [_remote(blk, mine, send(0), recv(0), (x, y, 1 - c))]
        sends += [_remote(blk, mine, send(1 + j), recv(1 + j), (*chip, c)) for j, chip in enumerate(chips)]
        recvs = [_remote(blk, full.at[_slot(x, y, 1 - c)], send(0), recv(0), (x, y, 1 - c))]
        recvs += [_remote(blk, full.at[_slot(*chip, c)], send(1 + j), recv(1 + j), (*chip, c))
                  for j, chip in enumerate(chips)]
        return sends, recvs, [pltpu.make_async_copy(blk, mine, loc(0))]

    return _Comm((half_block,), (SDS((2 * N_CHIPS,) + half_block.shape, half_block.dtype),), {}, 4, 1, plan)


def _gather_second(partly_gathered):
    def plan(ins, outs, send, recv, loc):
        (src,), (full,) = ins, outs
        x, y, c = _place()
        chips = _other_chips(x, y)
        sends = [_remote(src.at[_slot(*chip, c)], full.at[_slot(*chip, c)], send(j), recv(j), (x, y, 1 - c))
                 for j, chip in enumerate(chips)]
        recvs = [_remote(src.at[_slot(*chip, 1 - c)], full.at[_slot(*chip, 1 - c)], send(j), recv(j), (x, y, 1 - c))
                 for j, chip in enumerate(chips)]
        return sends, recvs, []

    return _Comm((partly_gathered,), (SDS(partly_gathered.shape, partly_gathered.dtype),), {0: 0}, 3, 0, plan)


def _relay_pieces(full, rows, x, y, c):
    start, half = rows[0], rows[1] // 2
    upper, lower = pl.ds(start, half), pl.ds(start + half, half)
    diagonal = full.at[_slot(1 - x, 1 - y, c)]
    return [(full.at[_slot(1 - x, y, c), upper], diagonal.at[upper], (x, 1 - y, c)),
            (full.at[_slot(x, 1 - y, c), lower], diagonal.at[lower], (1 - x, y, c))]


def _relay(half_block, so_far, first=None, second=None, third=None, third_after=None):
    has_block, has_buffer = half_block is not None, so_far is not None
    shape = so_far.shape if has_buffer else (2 * N_CHIPS,) + half_block.shape
    dtype = so_far.dtype if has_buffer else half_block.dtype

    def third_leg(rows, k, ins, outs, send, recv):
        src, full = (ins[-1] if has_buffer else outs[0]), outs[0]
        x, y, c = _place()
        span, sibling = pl.ds(*rows), (x, y, 1 - c)
        here, there = _slot(1 - x, 1 - y, c), _slot(1 - x, 1 - y, 1 - c)
        return ([_remote(src.at[here, span], full.at[here, span], send(k), recv(k), sibling)],
                [_remote(src.at[there, span], full.at[there, span], send(k), recv(k), sibling)])

    def plan(ins, outs, send, recv, loc):
        src, full = (ins[-1] if has_buffer else outs[0]), outs[0]
        x, y, c = _place()
        sibling = (x, y, 1 - c)
        sends, recvs, locs = [], [], []
        if first is not None:
            span = pl.ds(*first)
            blk, mine = ins[0].at[span], full.at[_slot(x, y, c), span]
            for k, peer in enumerate([sibling, (1 - x, y, c), (x, 1 - y, c)]):
                sends.append(_remote(blk, mine, send(k), recv(k), peer))
                recvs.append(_remote(blk, full.at[_slot(*peer), span], send(k), recv(k), peer))
            locs.append(pltpu.make_async_copy(blk, mine, loc(0)))
        if second is not None:
            span = pl.ds(*second)
            for k, chip in enumerate([(1 - x, y), (x, 1 - y)]):
                sends.append(_remote(src.at[_slot(*chip, c), span], full.at[_slot(*chip, c), span], send(3 + k), recv(3 + k),
                                     sibling))
                recvs.append(_remote(src.at[_slot(*chip, 1 - c), span], full.at[_slot(*chip, 1 - c), span], send(3 + k),
                                     recv(3 + k), sibling))
            for k, (piece, lands, peer) in enumerate(_relay_pieces(full, second, x, y, c)):
                sends.append(_remote(piece, piece, send(5 + k), recv(5 + k), peer))
                recvs.append(_remote(lands, lands, send(5 + k), recv(5 + k), peer))
        if third is not None:
            s, r = third_leg(third, 7, ins, outs, send, recv)
            sends, recvs = sends + s, recvs + r
        return sends, recvs, locs

    def after(ins, outs, send, recv, loc):
        s, r = third_leg(third_after, 8, ins, outs, send, recv)
        return s, r, []

    operands = ((half_block,) if has_block else ()) + ((so_far,) if has_buffer else ())
    return _Comm(operands, (SDS(shape, dtype),), {len(operands) - 1: 0} if has_buffer else {}, 9, 1, plan,
                 after if third_after is not None else None)


def _gather_whole(first, others, small_block):
    shards = (first, *others)
    n = len(shards)
    hs = [s.shape[0] // 2 for s in shards]
    rows = hs[0]

    def body(*refs):
        src, small_ref = refs[:n], refs[n]
        out_ref, small_out_ref, half_out = refs[n + 1], refs[n + 2], refs[n + 3:2 * n + 2]
        stage, half = refs[2 * n + 2:3 * n + 2], refs[3 * n + 2:4 * n + 2]
        send_sems, recv_sems, local_sems = refs[4 * n + 2:]
        x, y, c = _place()
        me, sibling = (x, y, c), (x, y, 1 - c)
        neighbours, diagonal = [(1 - x, y), (x, 1 - y)], (1 - x, 1 - y)
        loads = [pltpu.make_async_copy(src[k].at[pl.ds(c * hs[k], hs[k])], stage[k], local_sems.at[2 + k]) for k in range(n)]
        loads[0].start()
        loads[0].wait()
        for cp in loads[1:]:
            cp.start()
        blk_ref = half[0]
        blk_ref[...] = stage[0][...].astype(BF16)

        def copy(k, block, to, src=None):
            return _remote(out_ref.at[_slot(*block)] if src is None else src, out_ref.at[_slot(*block)],
                           send_sems.at[k], recv_sems.at[k], to)

        def small_copy(k, chip, to):
            return _remote(small_ref, small_out_ref.at[2 * chip[0] + chip[1]], send_sems.at[8 + k], recv_sems.at[8 + k], to)

        mine = pltpu.make_async_copy(blk_ref, out_ref.at[_slot(*me)], local_sems.at[0])
        mine_small = pltpu.make_async_copy(small_ref, small_out_ref.at[2 * x + y], local_sems.at[1])
        mine.start()
        mine_small.start()
        started = [copy(0, me, sibling, src=blk_ref)]
        started += [copy(1 + k, me, (*chip, c), src=blk_ref) for k, chip in enumerate(neighbours)]
        started += [small_copy(k, (x, y), (*chip, c)) for k, chip in enumerate(neighbours + [diagonal])]
        for cp in started:
            cp.start()
        stores = []
        for k in range(1, n):
            loads[k].wait()
            half[k][...] = stage[k][...].astype(BF16)
            stores.append(pltpu.make_async_copy(half[k], half_out[k - 1], local_sems.at[2 + n + k]))
            stores[-1].start()
        pieces = _relay_pieces(out_ref, (0, rows), x, y, c)
        for k, chip in enumerate(neighbours):
            copy(1 + k, (*chip, c), me).wait_recv()
            piece, _, peer = pieces[k]
            started += [copy(3 + k, (*chip, c), sibling), _remote(piece, piece, send_sems.at[5 + k], recv_sems.at[5 + k], peer)]
            started[-2].start()
            started[-1].start()
        for k, (_, lands, peer) in enumerate(pieces):
            _remote(lands, lands, send_sems.at[5 + k], recv_sems.at[5 + k], peer).wait_recv()
        started.append(copy(7, (*diagonal, c), sibling))
        started[-1].start()
        copy(0, sibling, me).wait_recv()
        for k, chip in enumerate(neighbours):
            copy(3 + k, (*chip, 1 - c), me).wait_recv()
        copy(7, (*diagonal, 1 - c), me).wait_recv()
        for k, chip in enumerate(neighbours + [diagonal]):
            small_copy(k, chip, me).wait_recv()
        for cp in started:
            cp.wait_send()
        mine.wait()
        mine_small.wait()
        for cp in stores:
            cp.wait()

    return pl.pallas_call(
        body, name="gather_whole", in_specs=[ANY] * (n + 1), out_specs=[ANY] * (n + 1),
        out_shape=[SDS((2 * N_CHIPS, rows, D_MODEL), BF16), SDS((N_CHIPS,) + small_block.shape, small_block.dtype)]
                  + [SDS((h, D_MODEL), BF16) for h in hs[1:]],
        scratch_shapes=[pltpu.VMEM((h, D_MODEL), F32) for h in hs] + [pltpu.VMEM((h, D_MODEL), BF16) for h in hs]
                       + [pltpu.SemaphoreType.DMA((11,)), pltpu.SemaphoreType.DMA((11,)), pltpu.SemaphoreType.DMA((2 + 2 * n,))],
        compiler_params=pltpu.CompilerParams(vmem_limit_bytes=VMEM_LIMIT_V7X),
    )(*shards, small_block)


def _pair_send(grads):
    def plan(ins, outs, send, recv, loc):
        (g,), (got,) = ins, outs
        x, y, c = _place()
        copies = [_remote(g.at[j, 1 - c], got.at[j], send(j), recv(j), (x, y, 1 - c)) for j in range(N_CHIPS)]
        return copies, copies, []

    shape = (grads.shape[0],) + grads.shape[2:]
    return _Comm((grads,), (SDS(shape, grads.dtype),), {}, N_CHIPS, 0, plan)


def _chip_exchange(partial):
    def plan(ins, outs, send, recv, loc):
        (p,), (got,) = ins, outs
        x, y, c = _place()
        my_chip = 2 * x + y
        chips = _other_chips(x, y)
        sends = [_remote(p.at[2 * chip[0] + chip[1]], got.at[my_chip], send(j), recv(j), (*chip, c))
                 for j, chip in enumerate(chips)]
        recvs = [_remote(p.at[my_chip], got.at[2 * chip[0] + chip[1]], send(j), recv(j), (*chip, c))
                 for j, chip in enumerate(chips)]
        return sends, recvs, [pltpu.make_async_copy(p.at[my_chip], got.at[my_chip], loc(0))]

    return _Comm((partial,), (SDS(partial.shape, partial.dtype),), {}, 3, 1, plan)


def _pair_sum(name, core, grads, received):
    h = grads.shape[2]

    def body(core_ref, g_ref, r_ref, o_ref):
        o_ref[...] = (g_ref[0] + r_ref[...]).astype(BF16)

    return pl.pallas_call(
        body, name=name,
        grid_spec=pltpu.PrefetchScalarGridSpec(
            num_scalar_prefetch=1, grid=(N_CHIPS,),
            in_specs=[pl.BlockSpec((1, 1, h, D_MODEL), lambda j, core_ref: (j, core_ref[0], 0, 0)),
                      pl.BlockSpec((1, h, D_MODEL), lambda j, core_ref: (j, 0, 0))],
            out_specs=pl.BlockSpec((1, h, D_MODEL), lambda j, core_ref: (j, 0, 0))),
        out_shape=SDS((N_CHIPS, h, D_MODEL), BF16),
        compiler_params=pltpu.CompilerParams(dimension_semantics=("arbitrary",), vmem_limit_bytes=VMEM_LIMIT_V7X),
    )(core, grads, received)


SMALL_ROWS = 8


def _sum_blocks(ref):
    return (ref[0].astype(F32) + ref[1].astype(F32)) + (ref[2].astype(F32) + ref[3].astype(F32))


def _tail_reduce(last_grads, exchanged, small):
    n = len(exchanged)
    h = last_grads.shape[2]

    def body(*refs):
        g_ref, ex, small_ref = refs[0], refs[1:1 + n], refs[1 + n]
        o0 = 2 + n
        out, out_last, small_out = refs[o0:o0 + n], refs[o0 + n], refs[o0 + n + 1]
        s0 = o0 + n + 2
        halves, half_last = refs[s0:s0 + n], refs[s0 + n]
        own, got, part, exch, small_buf = refs[s0 + n + 1:s0 + n + 6]
        ex_buf = refs[s0 + n + 6:s0 + 2 * n + 6]
        pair_send, pair_recv, chip_send, chip_recv, share_send, share_recv, small_send, small_recv, local_sems = refs[s0 + 2 * n + 6:]
        x, y, c = _place()
        sibling = (x, y, 1 - c)
        my_chip, me = 2 * x + y, _slot(x, y, c)
        chips = _other_chips(x, y)[::-1]

        order = [2 * chip[0] + chip[1] for chip in chips] + [my_chip]
        to_sibling = [_remote(g_ref.at[j, 1 - c], got.at[j], pair_send.at[j], pair_recv.at[j], sibling) for j in order]
        load_own = [pltpu.make_async_copy(g_ref.at[j, c], own.at[j], local_sems.at[j]) for j in order]
        load_ex = [pltpu.make_async_copy(ex[k], ex_buf[k], local_sems.at[N_CHIPS + n + 1 + k]) for k in range(n)]
        for give, keep in zip(to_sibling, load_own):
            give.start()
            keep.start()
        for cp in load_ex:
            cp.start()

        small_buf[me] = small_ref[...]
        small_copies = []
        for mask in range(1, 8):
            peer = (x ^ (mask >> 2), y ^ ((mask >> 1) & 1), c ^ (mask & 1))
            small_copies.append(_remote(small_ref, small_buf.at[me], small_send.at[mask - 1], small_recv.at[mask - 1], peer))
        for cp in small_copies:
            cp.start()

        def share(k, half_ref, out_ref):
            keep = pltpu.make_async_copy(half_ref, out_ref.at[c], local_sems.at[N_CHIPS + k])
            give = _remote(half_ref, out_ref.at[c], share_send.at[k], share_recv.at[k], sibling)
            take = _remote(half_ref, out_ref.at[1 - c], share_send.at[k], share_recv.at[k], sibling)
            keep.start()
            give.start()
            return keep, give, take

        def pair_sum(block):
            _remote(g_ref.at[block, 1 - c], got.at[block], pair_send.at[block], pair_recv.at[block], sibling).wait_recv()
            pltpu.make_async_copy(g_ref.at[block, c], own.at[block], local_sems.at[block]).wait()
            part[block] = (own[block] + got[block]).astype(BF16)

        to_chips = []
        for j, chip in enumerate(chips):
            block = 2 * chip[0] + chip[1]
            pair_sum(block)
            to_chips.append(_remote(part.at[block], exch.at[my_chip], chip_send.at[j], chip_recv.at[j], (*chip, c)))
            to_chips[-1].start()
        pair_sum(my_chip)
        exch[my_chip] = part[my_chip]
        from_chips = [_remote(part.at[my_chip], exch.at[2 * chip[0] + chip[1]], chip_send.at[j], chip_recv.at[j], (*chip, c))
                      for j, chip in enumerate(chips)]

        shares = []
        for k in range(n):
            load_ex[k].wait()
            halves[k][...] = _sum_blocks(ex_buf[k])
            shares.append(share(k, halves[k], out[k]))

        for cp in small_copies:
            cp.wait_recv()
        total = small_buf[0]
        for d in range(1, 8):
            total = total + small_buf[d]
        small_out[...] = total

        for cp in from_chips:
            cp.wait_recv()
        half_last[...] = _sum_blocks(exch)
        shares.append(share(n, half_last, out_last))

        for keep, give, take in shares:
            take.wait_recv()
            give.wait_send()
            keep.wait()
        for cp in to_sibling + to_chips + small_copies:
            cp.wait_send()

    blocks = (N_CHIPS, h, D_MODEL)
    return pl.pallas_call(
        body, name="tail_reduce",
        in_specs=[ANY] * (n + 1) + [VMEM_WHOLE], out_specs=[ANY] * (n + 1) + [VMEM_WHOLE],
        out_shape=[SDS((2,) + e.shape[1:], F32) for e in exchanged] + [SDS((2, h, D_MODEL), F32), SDS(small.shape, F32)],
        scratch_shapes=[pltpu.VMEM(e.shape[1:], F32) for e in exchanged] + [pltpu.VMEM((h, D_MODEL), F32)]
                       + [pltpu.VMEM(blocks, F32), pltpu.VMEM(blocks, F32), pltpu.VMEM(blocks, BF16), pltpu.VMEM(blocks, BF16),
                          pltpu.VMEM((8,) + small.shape, F32)]
                       + [pltpu.VMEM(e.shape, BF16) for e in exchanged]
                       + [pltpu.SemaphoreType.DMA((N_CHIPS,)), pltpu.SemaphoreType.DMA((N_CHIPS,)),
                          pltpu.SemaphoreType.DMA((3,)), pltpu.SemaphoreType.DMA((3,)),
                          pltpu.SemaphoreType.DMA((n + 1,)), pltpu.SemaphoreType.DMA((n + 1,)),
                          pltpu.SemaphoreType.DMA((7,)), pltpu.SemaphoreType.DMA((7,)),
                          pltpu.SemaphoreType.DMA((N_CHIPS + 2 * n + 1,))],
        compiler_params=pltpu.CompilerParams(vmem_limit_bytes=VMEM_LIMIT_V7X),
    )(last_grads, *exchanged, small)


def _rope_expansion():
    half = ROT_DIM // 2
    expand = np.zeros((2 * half, 3 * 128), np.float32)
    const = np.zeros((1, 3 * 128), np.float32)
    for lane in range(128):
        d = lane % HEAD_DIM
        if d < ROT_DIM:
            expand[d % half, lane] = 1.0
        else:
            const[0, lane] = 1.0
        if d < half:
            expand[half + d, 128 + lane] = -1.0
        elif d < ROT_DIM:
            expand[half + d - half, 256 + lane] = 1.0
    return expand, const


ROPE_PIECES = 3 * ROT_DIM


def _rope_inputs(seq):
    pos = jnp.arange(seq, dtype=F32)
    inv_freq = ROPE_THETA ** (-jnp.arange(0, ROT_DIM, 2, dtype=F32) / ROT_DIM)
    ang = pos[:, None] * inv_freq[None, :]
    cs = jnp.concatenate([jnp.cos(ang), jnp.sin(ang)], axis=1)
    hi = lax.reduce_precision(cs, 8, 7)
    mid = lax.reduce_precision(cs - hi, 8, 7)
    low = cs - hi - mid
    expand, const = _rope_expansion()
    pieces = jnp.concatenate([hi, mid, low], axis=1).astype(BF16)
    return pieces, jnp.asarray(np.concatenate([expand] * 3, axis=0), BF16), jnp.asarray(const)


def _rope_specs(tb):
    return [pl.BlockSpec((tb, ROPE_PIECES), lambda i: (i, 0)), _resident((ROPE_PIECES, 3 * 128)), _resident((1, 3 * 128))]


def _rope_tile(pieces_ref, expand_ref, const_ref):
    tables = _dot(pieces_ref[...], expand_ref[...]) + const_ref[...]
    return tables[:, 0:128], tables[:, 128:256], tables[:, 256:384]


def _rope(t, c, sa, sb):
    half = ROT_DIM // 2
    return t * c + pltpu.roll(t, 128 - half, 1) * sa + pltpu.roll(t, half, 1) * sb


def _rope_transposed(dt, c, sa, sb):
    half = ROT_DIM // 2
    return dt * c + pltpu.roll(dt * sa, half, 1) + pltpu.roll(dt * sb, 128 - half, 1)


def _in_proj(x, g_pre, w_in_t, rope, comm=None):
    seq = x.shape[0]
    tb = min(seq, WIDE_TOKEN_TILE)

    def body(x_ref, g_ref, w_hbm, c_ref, sa_ref, sb_ref,
             q_ref, kd0_ref, kd1_ref, vd0_ref, vd1_ref, gb_ref, gc_ref, xin_ref, hn_ref, w_ref, w_sem):
        @pl.when(pl.program_id(0) == 0)
        def _():
            load = pltpu.make_async_copy(w_hbm, w_ref, w_sem.at[0])
            load.start()
            load.wait()

        xv = x_ref[...]
        hn = (xv * _rms(xv) * g_ref[...]).astype(BF16)
        hn_ref[...] = hn
        proj = _dot_nt(hn, w_ref[...].reshape(IN_COLS, D_MODEL))
        c, sa, sb = _rope_tile(c_ref, sa_ref, sb_ref)
        scale = 1.0 / math.sqrt(HEAD_DIM)
        for p in range(Q_WIDTH // 128):
            q_ref[:, 128 * p:128 * (p + 1)] = (_rope(proj[:, 128 * p:128 * (p + 1)], c, sa, sb) * scale).astype(BF16)
        k = _rope(proj[:, Q_WIDTH:Q_WIDTH + KV_WIDTH], c, sa, sb)
        v = proj[:, Q_WIDTH + KV_WIDTH:Q_WIDTH + 2 * KV_WIDTH]
        low = _lane_lt64(k.shape)
        k_sw, v_sw = pltpu.roll(k, HEAD_DIM, 1), pltpu.roll(v, HEAD_DIM, 1)
        kd0_ref[...] = jnp.where(low, k, k_sw).astype(BF16)
        kd1_ref[...] = jnp.where(low, k_sw, k).astype(BF16)
        vd0_ref[...] = jnp.where(low, v, v_sw).astype(BF16)
        vd1_ref[...] = jnp.where(low, v_sw, v).astype(BF16)
        base = Q_WIDTH + 2 * KV_WIDTH
        gb_ref[...] = proj[:, base:base + CONV_WIDTH].astype(BF16)
        gc_ref[...] = proj[:, base + CONV_WIDTH:base + 2 * CONV_WIDTH].astype(BF16)
        xin_ref[...] = proj[:, base + 2 * CONV_WIDTH:base + 3 * CONV_WIDTH].astype(BF16)

    tile = lambda w: pl.BlockSpec((tb, w), lambda i: (i, 0))
    return _pallas(
        body, name="in_proj", grid=(seq // tb,),
        in_specs=[tile(D_MODEL), _resident((1, D_MODEL)), ANY, *_rope_specs(tb)],
        out_specs=[tile(Q_WIDTH), tile(128), tile(128), tile(128), tile(128),
                   tile(CONV_WIDTH), tile(CONV_WIDTH), tile(CONV_WIDTH), tile(D_MODEL)],
        out_shape=[SDS((seq, Q_WIDTH), BF16)] + [SDS((seq, 128), BF16)] * 4
                  + [SDS((seq, CONV_WIDTH), BF16)] * 3 + [SDS((seq, D_MODEL), BF16)],
        scratch=[pltpu.VMEM(w_in_t.shape, w_in_t.dtype), pltpu.SemaphoreType.DMA((1,))],
        operands=(x, g_pre, w_in_t, *rope), comm=comm)


def _attn_valid(i):
    shape = (4 * QBLOCK, 2 * QBLOCK)
    row = lax.broadcasted_iota(jnp.int32, shape, 0)
    col = lax.broadcasted_iota(jnp.int32, shape, 1)
    qi = row & (QBLOCK - 1)
    return (col > qi) & (col <= qi + QBLOCK) & ((col >= QBLOCK) | (i > 0))


def _stack_heads(pair0, pair1):
    low = _lane_lt64(pair0.shape)
    zero = jnp.zeros_like(pair0)
    return jnp.concatenate([jnp.where(low, pair0, zero), jnp.where(low, zero, pair0),
                            jnp.where(low, pair1, zero), jnp.where(low, zero, pair1)], axis=0)


def _unstack_heads(stacked):
    low = _lane_lt64((QBLOCK, 128))
    pair0 = jnp.where(low, stacked[0:QBLOCK], stacked[QBLOCK:2 * QBLOCK])
    pair1 = jnp.where(low, stacked[2 * QBLOCK:3 * QBLOCK], stacked[3 * QBLOCK:4 * QBLOCK])
    return pair0, pair1


def _sink_column(sink_ref, kv_head):
    row = lax.broadcasted_iota(jnp.int32, (4 * QBLOCK, 1), 0)
    s = [sink_ref[0, 4 * kv_head + j] for j in range(4)]
    return jnp.where(row < QBLOCK, s[0], jnp.where(row < 2 * QBLOCK, s[1], jnp.where(row < 3 * QBLOCK, s[2], s[3])))


def _band(ref, i):
    prev = pl.multiple_of(jnp.maximum(i - 1, 0) * QBLOCK, QBLOCK)
    own = pl.multiple_of(i * QBLOCK, QBLOCK)
    return jnp.concatenate([ref[pl.ds(prev, QBLOCK), :], ref[pl.ds(own, QBLOCK), :]], axis=0), prev, own


def _softmax_with_sink(s, sink_col):
    m = jnp.maximum(jnp.max(s, axis=-1, keepdims=True), sink_col)
    p = jnp.exp(s - m)
    e_sink = jnp.exp(sink_col - m)
    inv_l = 1.0 / (jnp.sum(p, axis=-1, keepdims=True) + e_sink)
    return p, e_sink, inv_l


def _attention_fwd(q, kd0, kd1, vd0, vd1, sinks, comm=None):
    seq = q.shape[0]

    nb = ATTN_FWD_BLOCKS

    def body(sink_ref, q_ref, kd0_ref, kd1_ref, vd0_ref, vd1_ref, o_ref):
        for b in range(nb):
            i = pl.program_id(0) * nb + b
            rows = slice(QBLOCK * b, QBLOCK * (b + 1))
            valid = _attn_valid(i)
            for kv_head, (k_ref, v_ref) in enumerate(((kd0_ref, vd0_ref), (kd1_ref, vd1_ref))):
                kband, _, _ = _band(k_ref, i)
                vband, _, _ = _band(v_ref, i)
                base = 256 * kv_head
                qm = _stack_heads(q_ref[rows, base:base + 128], q_ref[rows, base + 128:base + 256])
                s = jnp.where(valid, _dot_nt(qm, kband), NEG_INF)
                p, _, inv_l = _softmax_with_sink(s, _sink_column(sink_ref, kv_head))
                o = _dot(p.astype(BF16), vband) * inv_l
                pair0, pair1 = _unstack_heads(o)
                o_ref[rows, base:base + 128] = pair0.astype(BF16)
                o_ref[rows, base + 128:base + 256] = pair1.astype(BF16)

    blk = pl.BlockSpec((nb * QBLOCK, Q_WIDTH), lambda i: (i, 0))
    full = _resident((seq, 128))
    return _pallas(
        body, name="attention_fwd", grid=(seq // (nb * QBLOCK),),
        in_specs=[pl.BlockSpec(memory_space=pltpu.SMEM), blk, full, full, full, full],
        out_specs=[blk], out_shape=[SDS((seq, Q_WIDTH), BF16)],
        operands=(sinks, q, kd0, kd1, vd0, vd1), comm=comm)


HALO = 16


def _conv_parts(gc, xin, gc_halo, xin_halo, conv_w, first):
    tb = gc.shape[0]
    u = gc.astype(F32) * xin.astype(F32)
    u_halo = jnp.where(first, 0.0, gc_halo.astype(F32) * xin_halo.astype(F32))
    ext = jnp.concatenate([u_halo, u], axis=0)
    u1 = pltpu.roll(ext, 1, 0)[HALO:HALO + tb]
    u2 = pltpu.roll(ext, 2, 0)[HALO:HALO + tb]
    y = conv_w[0:1, :] * u2 + conv_w[1:2, :] * u1 + conv_w[2:3, :] * u
    return u, u1, u2, y


def _halo_prev(tb, w):
    return pl.BlockSpec((HALO, w), lambda i: (jnp.maximum(i * (tb // HALO) - 1, 0), 0))


def _residual_mid(x, mix, g_post_mix):
    mix_f = mix.astype(F32)
    return x + mix_f * _rms(mix_f) * g_post_mix


def _mix_out(attn, gb, gc, xin, conv_w, g_attn, g_conv, w_out, comm=None):
    seq = attn.shape[0]
    tb = min(seq, WIDE_TOKEN_TILE)

    def body(a_ref, gb_ref, gc_ref, xin_ref, gch_ref, xinh_ref, cw_ref, ga_ref, gcn_ref, w_ref, mix_ref, mixed_ref):
        first = pl.program_id(0) == 0
        _, _, _, y = _conv_parts(gc_ref[...], xin_ref[...], gch_ref[...], xinh_ref[...], cw_ref[...], first)
        conv = gb_ref[...].astype(F32) * y
        a = a_ref[...].astype(F32)
        mixed_ref[:, 0:Q_WIDTH] = (a * _rms(a) * ga_ref[...]).astype(BF16)
        mixed_ref[:, Q_WIDTH:] = (conv * _rms(conv) * gcn_ref[...]).astype(BF16)
        mix_ref[...] = _dot(mixed_ref[...], w_ref[...].reshape(D_MODEL, D_MODEL)).astype(BF16)

    tile = lambda w: pl.BlockSpec((tb, w), lambda i: (i, 0))
    return _pallas(
        body, name="mix_out", grid=(seq // tb,),
        in_specs=[tile(Q_WIDTH), tile(CONV_WIDTH), tile(CONV_WIDTH), tile(CONV_WIDTH),
                  _halo_prev(tb, CONV_WIDTH), _halo_prev(tb, CONV_WIDTH),
                  _resident((CONV_K, CONV_WIDTH)), _resident((1, Q_WIDTH)), _resident((1, CONV_WIDTH)),
                  _resident(w_out.shape)],
        out_specs=[tile(D_MODEL), tile(D_MODEL)],
        out_shape=[SDS((seq, D_MODEL), BF16), SDS((seq, D_MODEL), BF16)],
        operands=(attn, gb, gc, xin, gc, xin, conv_w, g_attn, g_conv, w_out), comm=comm)


def _mlp_fwd_bwd(x, mix, target, g_post_mix, g_pre_mlp, g_post_mlp, w_up, w_down):
    seq = x.shape[0]
    tb = TOKEN_TILE

    def body(x_ref, mix_ref, t_ref, gpm_ref, g2_ref, g4_ref, wup_ref, wdown_ref,
             up_ref, hn2_ref, dmlp_ref, dup_ref, dh_ref, dmix_ref, loss_ref, dg4_ref, dg2_ref, dgpm_ref):
        @pl.when(pl.program_id(0) == 0)
        def _():
            for ref in (loss_ref, dg4_ref, dg2_ref, dgpm_ref):
                ref[...] = jnp.zeros_like(ref)

        halves = [slice(0, tb // 2), slice(tb // 2, tb)]
        chunks = [slice(1024 * j, 1024 * (j + 1)) for j in range(N_CHIPS)]
        hv, hn2, mlp, dout, dmlp, dhn2 = [], [], [], [], [], []
        for rows in halves:
            hv.append(_residual_mid(x_ref[rows, :], mix_ref[rows, :], gpm_ref[...]))
            hn2.append((hv[-1] * _rms(hv[-1]) * g2_ref[...]).astype(BF16))
            hn2_ref[rows, :] = hn2[-1]
        for k, rows in enumerate(halves):
            acc = None
            for j, cols in enumerate(chunks):
                up = jnp.maximum(_dot(hn2[k], _chip_block(wup_ref, j)), 0.0)
                up_ref[rows, cols] = up.astype(BF16)
                part = _dot((up * up).astype(BF16), _chip_block(wdown_ref, j))
                acc = part if acc is None else acc + part
            mlp.append(acc)
        loss = jnp.zeros((1, 1), F32)
        dg4 = jnp.zeros((1, D_MODEL), F32)
        for k, rows in enumerate(halves):
            rstd = _rms(mlp[k])
            zhat = mlp[k] * rstd
            diff = hv[k] + zhat * g4_ref[...] - t_ref[rows, :]
            loss = loss + jnp.sum(jnp.sum(diff * diff, axis=1, keepdims=True), axis=0, keepdims=True)
            dout.append(diff * (1.0 / D_MODEL))
            dg4 = dg4 + _colsum(dout[k] * zhat)
            dmlp.append(_norm_bwd(dout[k], g4_ref[...], zhat, rstd).astype(BF16))
            dmlp_ref[rows, :] = dmlp[k]
        for k, rows in enumerate(halves):
            acc = None
            for j, cols in enumerate(chunks):
                dact = _dot_nt(dmlp[k], _chip_block(wdown_ref, j))
                dup = (dact * (2.0 * up_ref[rows, cols].astype(F32))).astype(BF16)
                dup_ref[rows, cols] = dup
                part = _dot_nt(dup, _chip_block(wup_ref, j))
                acc = part if acc is None else acc + part
            dhn2.append(acc)
        dg2 = jnp.zeros((1, D_MODEL), F32)
        dgpm = jnp.zeros((1, D_MODEL), F32)
        for k, rows in enumerate(halves):
            r2 = _rms(hv[k])
            hhat = hv[k] * r2
            dg2 = dg2 + _colsum(dhn2[k] * hhat)
            dh = dout[k] + _norm_bwd(dhn2[k], g2_ref[...], hhat, r2)
            dh_ref[rows, :] = dh.astype(BF16)
            mix_v = mix_ref[rows, :].astype(F32)
            rz = _rms(mix_v)
            zhat = mix_v * rz
            dgpm = dgpm + _colsum(dh * zhat)
            dmix_ref[rows, :] = _norm_bwd(dh, gpm_ref[...], zhat, rz).astype(BF16)
        loss_ref[...] += loss
        dg4_ref[...] += dg4
        dg2_ref[...] += dg2
        dgpm_ref[...] += dgpm

    tile = lambda w: pl.BlockSpec((tb, w), lambda i: (i, 0))
    vec = pl.BlockSpec((1, D_MODEL), lambda i: (0, 0))
    return _pallas(
        body, name="mlp_fwd_bwd", grid=(seq // tb,),
        in_specs=[tile(D_MODEL), tile(D_MODEL), tile(D_MODEL), _resident((1, D_MODEL)), _resident((1, D_MODEL)),
                  _resident((1, D_MODEL)), _resident(w_up.shape), _resident(w_down.shape)],
        out_specs=[tile(D_FF), tile(D_MODEL), tile(D_MODEL), tile(D_FF), tile(D_MODEL), tile(D_MODEL),
                   pl.BlockSpec((1, 1), lambda i: (0, 0)), vec, vec, vec],
        out_shape=[SDS((seq, D_FF), BF16), SDS((seq, D_MODEL), BF16), SDS((seq, D_MODEL), BF16), SDS((seq, D_FF), BF16),
                   SDS((seq, D_MODEL), BF16), SDS((seq, D_MODEL), BF16),
                   SDS((1, 1), F32), SDS((1, D_MODEL), F32), SDS((1, D_MODEL), F32), SDS((1, D_MODEL), F32)],
        operands=(x, mix, target, g_post_mix, g_pre_mlp, g_post_mlp, w_up, w_down))


def _mix_bwd(dmix, attn, gb, gc, xin, conv_w, g_attn, g_conv, w_out, n_k):
    seq = attn.shape[0]
    tb = seq // (N_CHIPS * n_k)

    def body(first, dmix_ref, a_ref, gb_ref, gc_ref, xin_ref, gch_ref, xinh_ref, cw_ref, ga_ref, gcn_ref, w_ref,
             dattn_ref, dgb_ref, dy_ref, dga_ref, dgcn_ref, dcw_ref):
        @pl.when(first)
        def _():
            dga_ref[...] = jnp.zeros_like(dga_ref)
            dgcn_ref[...] = jnp.zeros_like(dgcn_ref)
            dcw_ref[...] = jnp.zeros_like(dcw_ref)

        dmixed = _dot_nt(dmix_ref[...], w_ref[...].reshape(D_MODEL, D_MODEL))
        a = a_ref[...].astype(F32)
        ra = _rms(a)
        ahat = a * ra
        dan = dmixed[:, 0:Q_WIDTH]
        dga_ref[...] += _colsum(dan * ahat)
        dattn_ref[...] = _norm_bwd(dan, ga_ref[...], ahat, ra).astype(BF16)
        gbv = gb_ref[...].astype(F32)
        u, u1, u2, y = _conv_parts(gc_ref[...], xin_ref[...], gch_ref[...], xinh_ref[...], cw_ref[...], first)
        conv = gbv * y
        rc = _rms(conv)
        chat = conv * rc
        dcn = dmixed[:, Q_WIDTH:]
        dgcn_ref[...] += _colsum(dcn * chat)
        dconv = _norm_bwd(dcn, gcn_ref[...], chat, rc)
        dgb_ref[...] = (dconv * y).astype(BF16)
        dy = dconv * gbv
        dy_ref[...] = dy.astype(BF16)
        dcw_ref[0:1, :] += _colsum(dy * u2)
        dcw_ref[1:2, :] += _colsum(dy * u1)
        dcw_ref[2:3, :] += _colsum(dy * u)

    tile = lambda w: pl.BlockSpec((tb, w), lambda j, k: (j * n_k + k, 0))
    halo = lambda w: pl.BlockSpec((HALO, w), lambda j, k: (jnp.maximum((j * n_k + k) * (tb // HALO) - 1, 0), 0))
    whole = lambda shape: pl.BlockSpec(shape, lambda j, k: (0,) * len(shape))
    return _Rider(
        body,
        in_specs=[tile(D_MODEL), tile(Q_WIDTH), tile(CONV_WIDTH), tile(CONV_WIDTH), tile(CONV_WIDTH),
                  halo(CONV_WIDTH), halo(CONV_WIDTH),
                  _resident((CONV_K, CONV_WIDTH)), _resident((1, Q_WIDTH)), _resident((1, CONV_WIDTH)),
                  _resident(w_out.shape)],
        out_specs=[tile(Q_WIDTH), tile(CONV_WIDTH), tile(CONV_WIDTH),
                   whole((1, Q_WIDTH)), whole((1, CONV_WIDTH)), whole((CONV_K, CONV_WIDTH))],
        out_shape=[SDS((seq, Q_WIDTH), BF16), SDS((seq, CONV_WIDTH), BF16), SDS((seq, CONV_WIDTH), BF16),
                   SDS((1, Q_WIDTH), F32), SDS((1, CONV_WIDTH), F32), SDS((CONV_K, CONV_WIDTH), F32)],
        operands=(dmix, attn, gb, gc, xin, gc, xin, conv_w, g_attn, g_conv, w_out))


def _attention_bwd(q, dattn, attn, kd0, kd1, vd0, vd1, sinks, comm=None):
    seq = q.shape[0]
    nb = ATTN_BWD_BLOCKS

    def body(sink_ref, q_ref, do_ref, o_ref, kd0_ref, kd1_ref, vd0_ref, vd1_ref,
             dq_ref, dk0_ref, dk1_ref, dv0_ref, dv1_ref, dsink_ref):
        @pl.when(pl.program_id(0) == 0)
        def _():
            for r in (dk0_ref, dk1_ref, dv0_ref, dv1_ref, dsink_ref):
                r[...] = jnp.zeros_like(r)

        lane = lax.broadcasted_iota(jnp.int32, (1, 128), 1)
        dsink = jnp.zeros((1, 128), F32)
        for b in range(nb):
            i = pl.program_id(0) * nb + b
            rows = slice(QBLOCK * b, QBLOCK * (b + 1))
            valid = _attn_valid(i)
            for kv_head, (k_ref, v_ref, dk_ref, dv_ref) in enumerate(
                    ((kd0_ref, vd0_ref, dk0_ref, dv0_ref), (kd1_ref, vd1_ref, dk1_ref, dv1_ref))):
                kband, prev, own = _band(k_ref, i)
                vband, _, _ = _band(v_ref, i)
                base = 256 * kv_head
                qm = _stack_heads(q_ref[rows, base:base + 128], q_ref[rows, base + 128:base + 256])
                dom = _stack_heads(do_ref[rows, base:base + 128], do_ref[rows, base + 128:base + 256])
                om = _stack_heads(o_ref[rows, base:base + 128], o_ref[rows, base + 128:base + 256])
                s = jnp.where(valid, _dot_nt(qm, kband), NEG_INF)
                p, e_sink, inv_l = _softmax_with_sink(s, _sink_column(sink_ref, kv_head))
                p = p * inv_l
                delta = jnp.sum(dom.astype(F32) * om.astype(F32), axis=-1, keepdims=True)
                ds = (p * (_dot_nt(dom, vband) - delta)).astype(BF16)
                sink_term = -(e_sink * inv_l) * delta
                for j in range(4):
                    part = jnp.sum(sink_term[QBLOCK * j:QBLOCK * (j + 1)], axis=0, keepdims=True)
                    dsink = dsink + jnp.where(lane == 4 * kv_head + j, part, 0.0)
                pair0, pair1 = _unstack_heads(_dot(ds, kband))
                dq_ref[rows, base:base + 128] = pair0.astype(BF16)
                dq_ref[rows, base + 128:base + 256] = pair1.astype(BF16)
                dkd = _dot_tn(ds, qm)
                dkd = dkd + pltpu.roll(dkd, HEAD_DIM, 1)
                dvd = _dot_tn(p.astype(BF16), dom)
                dvd = dvd + pltpu.roll(dvd, HEAD_DIM, 1)
                dk_ref[pl.ds(prev, QBLOCK), :] += dkd[0:QBLOCK]
                dk_ref[pl.ds(own, QBLOCK), :] += dkd[QBLOCK:]
                dv_ref[pl.ds(prev, QBLOCK), :] += dvd[0:QBLOCK]
                dv_ref[pl.ds(own, QBLOCK), :] += dvd[QBLOCK:]
        dsink_ref[...] += dsink

    blk = pl.BlockSpec((nb * QBLOCK, Q_WIDTH), lambda i: (i, 0))
    full = _resident((seq, 128))
    acc = pl.BlockSpec((seq, 128), lambda i: (0, 0))
    return _pallas(
        body, name="attention_bwd", grid=(seq // (nb * QBLOCK),),
        in_specs=[pl.BlockSpec(memory_space=pltpu.SMEM), blk, blk, blk, full, full, full, full],
        out_specs=[blk, acc, acc, acc, acc, pl.BlockSpec((1, 128), lambda i: (0, 0))],
        out_shape=[SDS((seq, Q_WIDTH), BF16)] + [SDS((seq, 128), F32)] * 4 + [SDS((1, 128), F32)],
        operands=(sinks, q, dattn, attn, kd0, kd1, vd0, vd1), comm=comm)


def _in_proj_bwd(dq, dk0, dk1, dv0, dv1, dgb, dy, gc, xin, conv_w, x, dh, g_pre, w_in_t, rope):
    seq = x.shape[0]
    tb = min(seq, WIDE_TOKEN_TILE)
    n_tiles = seq // tb

    def body(dq_ref, dk0_ref, dk1_ref, dv0_ref, dv1_ref, dgb_ref, dy_ref, dyh_ref, gc_ref, xin_ref, cw_ref,
             x_ref, dh_ref, g_ref, w_ref, c_ref, sa_ref, sb_ref,
             dproj_ref, gx_ref, dg_ref):
        i = pl.program_id(0)

        @pl.when(i == 0)
        def _():
            dg_ref[...] = jnp.zeros_like(dg_ref)

        dy = dy_ref[...].astype(F32)
        ext = jnp.concatenate([dy, jnp.where(i == n_tiles - 1, 0.0, dyh_ref[...].astype(F32))], axis=0)
        dy1 = pltpu.roll(ext, tb + HALO - 1, 0)[0:tb]
        dy2 = pltpu.roll(ext, tb + HALO - 2, 0)[0:tb]
        cw = cw_ref[...]
        du = cw[2:3, :] * dy + cw[1:2, :] * dy1 + cw[0:1, :] * dy2
        scale = 1.0 / math.sqrt(HEAD_DIM)
        base = Q_WIDTH + 2 * KV_WIDTH
        halves = [slice(0, tb // 2), slice(tb // 2, tb)]
        low = _lane_lt64((tb // 2, 128))
        for rows in halves:
            c, sa, sb = _rope_tile(c_ref.at[rows, :], sa_ref, sb_ref)
            for p in range(Q_WIDTH // 128):
                dproj_ref[rows, 128 * p:128 * (p + 1)] = _rope_transposed(
                    dq_ref[rows, 128 * p:128 * (p + 1)].astype(F32) * scale, c, sa, sb).astype(BF16)
            dk = jnp.where(low, dk0_ref[rows, :], dk1_ref[rows, :])
            dproj_ref[rows, Q_WIDTH:Q_WIDTH + KV_WIDTH] = _rope_transposed(dk, c, sa, sb).astype(BF16)
            dproj_ref[rows, Q_WIDTH + KV_WIDTH:base] = jnp.where(low, dv0_ref[rows, :], dv1_ref[rows, :]).astype(BF16)
            dproj_ref[rows, base:base + CONV_WIDTH] = dgb_ref[rows, :]
            dproj_ref[rows, base + CONV_WIDTH:base + 2 * CONV_WIDTH] = (du[rows] * xin_ref[rows, :].astype(F32)).astype(BF16)
            dproj_ref[rows, base + 2 * CONV_WIDTH:] = (du[rows] * gc_ref[rows, :].astype(F32)).astype(BF16)
        w_all = w_ref[...].reshape(IN_COLS, D_MODEL)
        dhn = [_dot(dproj_ref[rows, :], w_all) for rows in halves]
        dg = jnp.zeros((1, D_MODEL), F32)
        for k, rows in enumerate(halves):
            xv = x_ref[rows, :]
            r = _rms(xv)
            xhat = xv * r
            dg = dg + _colsum(dhn[k] * xhat)
            gx_ref[rows, :] = dh_ref[rows, :].astype(F32) + _norm_bwd(dhn[k], g_ref[...], xhat, r)
        dg_ref[...] += dg

    tile = lambda w: pl.BlockSpec((tb, w), lambda i: (i, 0))
    halo_next = pl.BlockSpec((HALO, CONV_WIDTH), lambda i: (jnp.minimum((i + 1) * (tb // HALO), seq // HALO - 1), 0))
    return _pallas(
        body, name="in_proj_bwd", grid=(n_tiles,),
        in_specs=[tile(Q_WIDTH), tile(128), tile(128), tile(128), tile(128), tile(CONV_WIDTH), tile(CONV_WIDTH), halo_next,
                  tile(CONV_WIDTH), tile(CONV_WIDTH), _resident((CONV_K, CONV_WIDTH)),
                  tile(D_MODEL), tile(D_MODEL), _resident((1, D_MODEL)), _resident(w_in_t.shape), *_rope_specs(tb)],
        out_specs=[tile(IN_COLS), tile(D_MODEL), pl.BlockSpec((1, D_MODEL), lambda i: (0, 0))],
        out_shape=[SDS((seq, IN_COLS), BF16), SDS((seq, D_MODEL), F32), SDS((1, D_MODEL), F32)],
        operands=(dq, dk0, dk1, dv0, dv1, dgb, dy, dy, gc, xin, conv_w, x, dh, g_pre, w_in_t, *rope))


def _wgrad_grid(seq, per_chip, h_rows, with_rider=False):
    chips_per_step = 1 if per_chip else N_CHIPS
    m = chips_per_step * 2 * h_rows
    bt = min(seq, WGRAD_TOKEN_TILE if per_chip and not with_rider else WGRAD_TOKEN_TILE // 2)
    return chips_per_step, m, bt, seq // bt


def _wgrad(name, a, b, *, per_chip, h_rows, square_a=False, comm=None, rider=None):
    seq = a.shape[0]
    chips_per_step, m, bt, n_k = _wgrad_grid(seq, per_chip, h_rows, rider is not None)
    a_cols = m if per_chip else a.shape[1]
    a_wide = a.shape[1] > a_cols
    b_wide = b.shape[1] > D_MODEL

    def body(a_ref, b_ref, g_ref):
        @pl.when(pl.program_id(1) == 0)
        def _():
            g_ref[...] = jnp.zeros_like(g_ref)

        av = a_ref[...]
        if square_a:
            av = (av.astype(F32) * av.astype(F32)).astype(BF16)
        g_ref[...] += _dot_tn(av, b_ref[...]).reshape(g_ref.shape)

    a_spec = pl.BlockSpec((bt, a_cols), (lambda j, k: (k, j)) if a_wide else (lambda j, k: (k, 0)))
    b_spec = pl.BlockSpec((bt, D_MODEL), (lambda j, k: (k, j)) if b_wide else (lambda j, k: (k, 0)))
    g_spec = pl.BlockSpec((chips_per_step, 2, h_rows, D_MODEL), lambda j, k: (j, 0, 0, 0),
                          pipeline_mode=None if per_chip else pl.Buffered(1))
    return _pallas(
        body, name=name, grid=(N_CHIPS if per_chip else 1, n_k),
        in_specs=[a_spec, b_spec], out_specs=[g_spec], out_shape=[SDS((N_CHIPS, 2, h_rows, D_MODEL), F32)],
        operands=(a, b), comm=comm, rider=rider)


def _adamw_math(w, g, m, v):
    m = ADAM_B1 * m + (1.0 - ADAM_B1) * g
    v = ADAM_B2 * v + (1.0 - ADAM_B2) * (g * g)
    m_hat = m / (1.0 - ADAM_B1 ** ADAM_STEP)
    v_hat = v / (1.0 - ADAM_B2 ** ADAM_STEP)
    delta = -ADAM_LR * (m_hat / (jnp.sqrt(v_hat) + ADAM_EPS) + ADAM_WD * w)
    return delta, m, v


ADAMW_STEPS_PER_HALF = 4


def _adamw_rows(items):
    n = len(items)
    per_half = ADAMW_STEPS_PER_HALF

    def body(*refs):
        for k in range(n):
            r_ref, w_ref, m_ref, v_ref = refs[4 * k:4 * k + 4]
            g_out, d_out, m_out, v_out = refs[4 * (n + k):4 * (n + k) + 4]
            g = r_ref[0]
            g_out[...] = g
            d_out[...], m_out[...], v_out[...] = _adamw_math(w_ref[...], g, m_ref[...], v_ref[...])

    in_specs, out_specs, out_shape, operands = [], [], [], []
    for reduced, w, m, v in items:
        rt = reduced.shape[1] // per_half
        blk = pl.BlockSpec((rt, D_MODEL), lambda h, r: (h * per_half + r, 0))
        in_specs += [pl.BlockSpec((1, rt, D_MODEL), lambda h, r: (h, r, 0)), blk, blk, blk]
        out_specs += [blk] * 4
        out_shape += [SDS(w.shape, F32)] * 4
        operands += [reduced, w, m, v]
    res = _pallas(body, name="adamw_rows", grid=(2, per_half), in_specs=in_specs, out_specs=out_specs,
                  out_shape=out_shape, operands=tuple(operands))
    return [res[4 * k:4 * k + 4] for k in range(n)]


def _adamw_small(packed_grads, w, m, v):
    names = SMALL_NAMES
    n = len(names)
    conv_local = w["conv_w"].shape[-1]

    def body(*refs):
        gp = refs[0]
        w_refs, m_refs, v_refs = refs[1:1 + n], refs[1 + n:1 + 2 * n], refs[1 + 2 * n:1 + 3 * n]
        outs = refs[1 + 3 * n:]
        g_out, d_out, m_out, v_out = outs[0:n], outs[n:2 * n], outs[2 * n:3 * n], outs[3 * n:4 * n]
        chip = 2 * lax.axis_index("x") + lax.axis_index("y")

        def step(k, g, index=None):
            pick = (lambda r: r[...]) if index is None else (lambda r: r[index])
            d, new_m, new_v = _adamw_math(pick(w_refs[k]), g, pick(m_refs[k]), pick(v_refs[k]))
            for ref, val in ((g_out[k], g), (d_out[k], d), (m_out[k], new_m), (v_out[k], new_v)):
                if index is None:
                    ref[...] = val
                else:
                    ref[index] = val

        for k, name in enumerate(names):
            if name in SMALL_VECTORS:
                step(k, gp[SMALL_VECTORS.index(name):SMALL_VECTORS.index(name) + 1, :])
            elif name == "attn_group_norm":
                step(k, gp[4:5, 0:Q_WIDTH])
            elif name == "conv_group_norm":
                step(k, gp[4:5, Q_WIDTH:])
            elif name == "attn_sinks":
                step(k, gp[7:8, 0:8])
            else:
                for t in range(CONV_K):
                    row, base = 5 + t // 2, CONV_WIDTH * (t % 2)
                    g = gp[row:row + 1, base:base + conv_local]
                    for j in range(1, CONV_WIDTH // conv_local):
                        g = jnp.where(chip == j, gp[row:row + 1, base + conv_local * j:base + conv_local * (j + 1)], g)
                    step(k, g, index=(0, slice(t, t + 1), slice(None)))

    shapes = [SDS(w[name].shape, F32) for name in names]
    res = pl.pallas_call(
        body, name="adamw_small", in_specs=[VMEM_WHOLE] * (1 + 3 * n), out_specs=[VMEM_WHOLE] * (4 * n),
        out_shape=shapes * 4,
    )(packed_grads, *[w[k] for k in names], *[m[k] for k in names], *[v[k] for k in names])
    return [dict(zip(names, res[i * n:(i + 1) * n])) for i in range(4)]


SMALL_VECTORS = ("pre_mix_norm", "post_mix_norm", "pre_mlp_norm", "post_mlp_norm")
SMALL_NAMES = SMALL_VECTORS + ("attn_group_norm", "conv_group_norm", "conv_w", "attn_sinks")


def _pack_small(p):
    rows = [p[n].reshape(1, D_MODEL) for n in SMALL_VECTORS]
    rows.append(jnp.concatenate([p["attn_group_norm"].reshape(1, -1), p["conv_group_norm"].reshape(1, -1)], axis=1))
    cw = p["conv_w"].reshape(CONV_K, -1)
    rows.append(jnp.pad(cw, ((0, 1), (0, CONV_WIDTH - cw.shape[1]))).reshape(2, D_MODEL))
    last = jnp.concatenate([p["attn_sinks"].reshape(1, 8), p.get("loss_sum", jnp.zeros((1, 1), F32))], axis=1)
    rows.append(jnp.pad(last, ((0, 0), (0, D_MODEL - 9))))
    return jnp.concatenate(rows, axis=0)


WEIGHT_ORDER = ("pre_mix_norm", "w_in", "conv_w", "attn_sinks", "attn_group_norm", "conv_group_norm", "w_out",
                "post_mix_norm", "pre_mlp_norm", "w_up", "w_down", "post_mlp_norm")


def kernel(x, pre_mix_norm, w_in, conv_w, attn_sinks, attn_group_norm, conv_group_norm, w_out, post_mix_norm, pre_mlp_norm, w_up, w_down, post_mlp_norm, loss_target, m_pre_mix_norm, m_w_in, m_conv_w, m_attn_sinks, m_attn_group_norm, m_conv_group_norm, m_w_out, m_post_mix_norm, m_pre_mlp_norm, m_w_up, m_w_down, m_post_mlp_norm, v_pre_mix_norm, v_w_in, v_conv_w, v_attn_sinks, v_attn_group_norm, v_conv_group_norm, v_w_out, v_post_mix_norm, v_pre_mlp_norm, v_w_up, v_w_down, v_post_mlp_norm):
    w = dict(pre_mix_norm=pre_mix_norm, w_in=w_in, conv_w=conv_w, attn_sinks=attn_sinks, attn_group_norm=attn_group_norm,
             conv_group_norm=conv_group_norm, w_out=w_out, post_mix_norm=post_mix_norm, pre_mlp_norm=pre_mlp_norm,
             w_up=w_up, w_down=w_down, post_mlp_norm=post_mlp_norm)
    m = dict(pre_mix_norm=m_pre_mix_norm, w_in=m_w_in, conv_w=m_conv_w, attn_sinks=m_attn_sinks,
             attn_group_norm=m_attn_group_norm, conv_group_norm=m_conv_group_norm, w_out=m_w_out,
             post_mix_norm=m_post_mix_norm, pre_mlp_norm=m_pre_mlp_norm, w_up=m_w_up, w_down=m_w_down,
             post_mlp_norm=m_post_mlp_norm)
    v = dict(pre_mix_norm=v_pre_mix_norm, w_in=v_w_in, conv_w=v_conv_w, attn_sinks=v_attn_sinks,
             attn_group_norm=v_attn_group_norm, conv_group_norm=v_conv_group_norm, w_out=v_w_out,
             post_mix_norm=v_post_mix_norm, pre_mlp_norm=v_pre_mlp_norm, w_up=v_w_up, w_down=v_w_down,
             post_mlp_norm=v_post_mlp_norm)
    core = lax.axis_index("c").astype(jnp.int32).reshape(1)
    xs, target = x[0], loss_target[0]
    rope = _rope_inputs(xs.shape[0])

    conv_pad = jnp.pad(conv_w[0], ((0, 8 - CONV_K), (0, 0)))
    wf_in, conv_all, hb_up, hb_down, hb_out = _gather_whole(w_in[0].T, (w_up[0], w_down[0], w_out[0]), conv_pad)
    conv_full = conv_all[:, :CONV_K, :].transpose(1, 0, 2).reshape(CONV_K, CONV_WIDTH)

    whole_up, early, late = (0, H_UP), (0, DOWN_EARLY_ROWS), (DOWN_EARLY_ROWS, H_DOWN - DOWN_EARLY_ROWS)
    *proj, wf_up, wf_out, wf_down = _in_proj(
        xs, pre_mix_norm, wf_in, rope,
        comm=_merge(_relay(hb_up, None, first=whole_up), _gather_first(hb_out), _relay(hb_down, None, first=early)))
    q, kd0, kd1, vd0, vd1, gb, gc, xin, hn = proj
    attn, wf_up, wf_out, wf_down = _attention_fwd(
        q, kd0, kd1, vd0, vd1, attn_sinks,
        comm=_merge(_relay(None, wf_up, second=whole_up), _gather_second(wf_out),
                    _relay(hb_down, wf_down, first=late, second=early)))
    mix, mixed, wf_up, wf_down = _mix_out(
        attn, gb, gc, xin, conv_full, attn_group_norm, conv_group_norm, wf_out,
        comm=_merge(_relay(None, wf_up, third=whole_up), _relay(None, wf_down, second=late, third=early, third_after=late)))
    up, hn2, dmlp, dup, dh, dmix, loss_sum, dg_post_mlp, dg_pre_mlp, dg_post_mix = _mlp_fwd_bwd(
        xs, mix, target, post_mix_norm, pre_mlp_norm, post_mlp_norm, wf_up, wf_down)

    n_k = _wgrad_grid(xs.shape[0], True, H_DOWN, with_rider=True)[3]
    g_down, dattn, dgb, dy, dg_attn, dg_conv, dconv_w = _wgrad(
        "wgrad_down", up, dmlp, per_chip=True, h_rows=H_DOWN, square_a=True,
        rider=_mix_bwd(dmix, attn, gb, gc, xin, conv_full, attn_group_norm, conv_group_norm, wf_out, n_k))
    g_up, got_down = _wgrad("wgrad_up", hn2, dup, per_chip=True, h_rows=H_UP, comm=_pair_send(g_down))
    p_down = _pair_sum("pair_sum_down", core, g_down, got_down)
    g_out, got_up = _wgrad("wgrad_out", mixed, dmix, per_chip=False, h_rows=H_OUT, comm=_pair_send(g_up))
    p_up = _pair_sum("pair_sum_up", core, g_up, got_up)
    dq, dk0, dk1, dv0, dv1, dsink, ex_down, ex_up, got_out = _attention_bwd(
        q, dattn, attn, kd0, kd1, vd0, vd1, attn_sinks,
        comm=_merge(_chip_exchange(p_down), _chip_exchange(p_up), _pair_send(g_out)))
    p_out = _pair_sum("pair_sum_out", core, g_out, got_out)
    dproj, grad_x, dg_pre_mix = _in_proj_bwd(dq, dk0, dk1, dv0, dv1, dgb, dy, gc, xin, conv_full, xs, dh, pre_mix_norm,
                                             wf_in, rope)
    g_in, ex_out = _wgrad("wgrad_in", dproj, hn, per_chip=False, h_rows=H_IN, comm=_chip_exchange(p_out))
    small = dict(pre_mix_norm=dg_pre_mix, conv_w=dconv_w, attn_sinks=dsink[:, :8], attn_group_norm=dg_attn,
                 conv_group_norm=dg_conv, post_mix_norm=dg_post_mix, pre_mlp_norm=dg_pre_mlp, post_mlp_norm=dg_post_mlp,
                 loss_sum=loss_sum)
    r_down, r_up, r_out, r_in, small_total = _tail_reduce(g_in, [ex_down, ex_up, ex_out], _pack_small(small))

    out_g, out_d, out_m, out_v = {}, {}, {}, {}
    res_up, res_down, res_out, res_in_t = _adamw_rows([
        (r_up, w_up[0], m_w_up[0], v_w_up[0]), (r_down, w_down[0], m_w_down[0], v_w_down[0]),
        (r_out, w_out[0], m_w_out[0], v_w_out[0]), (r_in, w_in[0].T, m_w_in[0].T, v_w_in[0].T)])
    for name, res in (("w_up", res_up), ("w_down", res_down), ("w_out", res_out), ("w_in", [t.T for t in res_in_t])):
        out_g[name], out_d[name], out_m[name], out_v[name] = res

    loss = small_total[7, 8] * (0.5 / D_MODEL)
    for out, part in zip((out_g, out_d, out_m, out_v), _adamw_small(small_total, w, m, v)):
        out.update(part)

    def shaped(d):
        return [d[n].reshape(w[n].shape) for n in WEIGHT_ORDER]

    return (loss, grad_x[None], *shaped(out_g), *shaped(out_d), *shaped(out_m), *shaped(out_v))
```

```python
import math
from typing import Callable, NamedTuple

import jax
import jax.numpy as jnp
import numpy as np
from jax import lax
from jax.experimental import pallas as pl
from jax.experimental.pallas import tpu as pltpu

F32 = jnp.float32
BF16 = jnp.bfloat16

D_MODEL = 1024
HEAD_DIM = 64
Q_WIDTH = 512
KV_WIDTH = 128
CONV_WIDTH = 512
CONV_K = 3
D_FF = 4096
IN_COLS = 2304
QBLOCK = 128
ROT_DIM = 16
ROPE_THETA = 500000.0
NORM_EPS = 1e-6
NEG_INF = -1e30
N_CHIPS = 4

ADAM_LR = 0.001
ADAM_B1 = 0.9
ADAM_B2 = 0.999
ADAM_EPS = 1e-08
ADAM_WD = 0.01
ADAM_STEP = 10

H_UP, H_DOWN, H_OUT, H_IN = 512, 512, 128, 288
DOWN_EARLY_ROWS = 224

TOKEN_TILE = 512
WIDE_TOKEN_TILE = 1024
ATTN_FWD_BLOCKS = 16
ATTN_BWD_BLOCKS = 2
WGRAD_TOKEN_TILE = 4096
VMEM_LIMIT_V7X = 60 * 1024 * 1024

MESH = pl.DeviceIdType.MESH
ANY = pl.BlockSpec(memory_space=pl.ANY)
VMEM_WHOLE = pl.BlockSpec(memory_space=pltpu.VMEM)
SDS = jax.ShapeDtypeStruct


def _resident(shape):
    zeros = (0,) * len(shape)
    return pl.BlockSpec(shape, lambda *_: zeros, pipeline_mode=pl.Buffered(1))


def _rms(v):
    return lax.rsqrt(jnp.mean(v * v, axis=-1, keepdims=True) + NORM_EPS)


def _norm_bwd(dy, gain, vhat, rstd):
    t = dy * gain
    return rstd * (t - vhat * jnp.mean(t * vhat, axis=-1, keepdims=True))


def _colsum(v):
    return jnp.sum(v, axis=0, keepdims=True)


def _dot_nt(a, b):
    return lax.dot_general(a, b, (((1,), (1,)), ((), ())), preferred_element_type=F32)


def _dot_tn(a, b):
    return lax.dot_general(a, b, (((0,), (0,)), ((), ())), preferred_element_type=F32)


def _dot(a, b):
    return jnp.dot(a, b, preferred_element_type=F32)


def _chip_block(w_ref, chip):
    both = w_ref[pl.ds(2 * chip, 2)]
    return both.reshape(2 * both.shape[1], both.shape[2])


def _lane_lt64(shape):
    return lax.broadcasted_iota(jnp.int32, shape, 1) < HEAD_DIM


class _Comm(NamedTuple):
    operands: tuple
    out_shapes: tuple
    aliases: dict
    n_remote: int
    n_local: int
    plan: Callable
    after: Callable = None


def _merge(*comms):
    operands, out_shapes, aliases, parts = [], [], {}, []
    n_remote = n_local = 0
    for cm in comms:
        parts.append((len(operands), len(out_shapes), n_remote, n_local, cm))
        for k, v in cm.aliases.items():
            aliases[len(operands) + k] = len(out_shapes) + v
        operands += cm.operands
        out_shapes += cm.out_shapes
        n_remote += cm.n_remote
        n_local += cm.n_local

    def run(which, ins, outs, send, recv, loc):
        sends, recvs, locs = [], [], []
        for i0, o0, r0, l0, cm in parts:
            stage = getattr(cm, which)
            if stage is not None:
                s, r, l = stage(ins[i0:i0 + len(cm.operands)], outs[o0:o0 + len(cm.out_shapes)],
                                lambda k, r0=r0: send(r0 + k), lambda k, r0=r0: recv(r0 + k), lambda k, l0=l0: loc(l0 + k))
                sends, recvs, locs = sends + s, recvs + r, locs + l
        return sends, recvs, locs

    def plan(*args):
        return run("plan", *args)

    def after(*args):
        return run("after", *args)

    return _Comm(tuple(operands), tuple(out_shapes), aliases, n_remote, n_local, plan,
                 after if any(cm.after is not None for cm in comms) else None)


def _sem_scratch(comm):
    return [pltpu.SemaphoreType.DMA((max(comm.n_remote, 1),)), pltpu.SemaphoreType.DMA((max(comm.n_remote, 1),)),
            pltpu.SemaphoreType.DMA((max(comm.n_local, 1),))]


class _Rider(NamedTuple):
    body: Callable
    in_specs: list
    out_specs: list
    out_shape: list
    operands: tuple


def _pallas(body, *, name, grid, in_specs, out_specs, out_shape, operands, scratch=(), comm=None, rider=None):
    params = pltpu.CompilerParams(dimension_semantics=("arbitrary",) * len(grid), vmem_limit_bytes=VMEM_LIMIT_V7X)
    if rider is not None:
        own_in, own_out, ride_in, ride_out = len(in_specs), len(out_specs), len(rider.in_specs), len(rider.out_specs)
        own_body = body

        def body(*refs):
            o0 = own_in + ride_in
            s0 = o0 + own_out + ride_out
            own_body(*refs[:own_in], *refs[o0:o0 + own_out], *refs[s0:])
            first = None
            for axis in range(len(grid)):
                at_start = pl.program_id(axis) == 0
                first = at_start if first is None else jnp.logical_and(first, at_start)
            rider.body(first, *refs[own_in:o0], *refs[o0 + own_out:s0])

        in_specs, out_specs = list(in_specs) + rider.in_specs, list(out_specs) + rider.out_specs
        out_shape, operands = list(out_shape) + rider.out_shape, tuple(operands) + tuple(rider.operands)
    if comm is None:
        return pl.pallas_call(body, name=name, grid=grid, in_specs=in_specs, out_specs=out_specs, out_shape=out_shape,
                              scratch_shapes=list(scratch), compiler_params=params)(*operands)
    n_in, n_out, n_scr = len(in_specs), len(out_specs), len(scratch)
    c_in, c_out = len(comm.operands), len(comm.out_shapes)

    def with_comm(*refs):
        ins, c_ins = refs[:n_in], refs[n_in:n_in + c_in]
        o0 = n_in + c_in
        outs, c_outs = refs[o0:o0 + n_out], refs[o0 + n_out:o0 + n_out + c_out]
        s0 = o0 + n_out + c_out
        scr = refs[s0:s0 + n_scr]
        send_sems, recv_sems, local_sems = refs[s0 + n_scr:]
        first = last = None
        for axis, size in enumerate(grid):
            at_start, at_end = pl.program_id(axis) == 0, pl.program_id(axis) == size - 1
            first = at_start if first is None else jnp.logical_and(first, at_start)
            last = at_end if last is None else jnp.logical_and(last, at_end)

        def copies():
            return comm.plan(c_ins, c_outs, lambda k: send_sems.at[k], lambda k: recv_sems.at[k],
                             lambda k: local_sems.at[k])

        @pl.when(first)
        def _():
            sends, _, locs = copies()
            for cp in sends + locs:
                cp.start()

        body(*ins, *outs, *scr)

        @pl.when(last)
        def _():
            sends, recvs, locs = copies()
            for cp in recvs:
                cp.wait_recv()
            for cp in sends:
                cp.wait_send()
            for cp in locs:
                cp.wait()
            if comm.after is not None:
                sends, recvs, _ = comm.after(c_ins, c_outs, lambda k: send_sems.at[k], lambda k: recv_sems.at[k],
                                             lambda k: local_sems.at[k])
                for cp in sends:
                    cp.start()
                for cp in recvs:
                    cp.wait_recv()
                for cp in sends:
                    cp.wait_send()

    return pl.pallas_call(
        with_comm, name=name, grid=grid,
        in_specs=list(in_specs) + [ANY] * c_in, out_specs=list(out_specs) + [ANY] * c_out,
        out_shape=list(out_shape) + list(comm.out_shapes),
        scratch_shapes=list(scratch) + _sem_scratch(comm),
        input_output_aliases={n_in + k: n_out + v for k, v in comm.aliases.items()},
        compiler_params=params)(*operands, *comm.operands)


def _place():
    return lax.axis_index("x"), lax.axis_index("y"), lax.axis_index("c")


def _other_chips(x, y):
    return [(1 - x, y), (x, 1 - y), (1 - x, 1 - y)]


def _slot(px, py, pc):
    return 4 * px + 2 * py + pc


def _remote(src, dst, send_sem, recv_sem, to):
    return pltpu.make_async_remote_copy(src_ref=src, dst_ref=dst, send_sem=send_sem, recv_sem=recv_sem,
                                        device_id=to, device_id_type=MESH)


def _gather_first(half_block):
    def plan(ins, outs, send, recv, loc):
        (blk,), (full,) = ins, outs
        x, y, c = _place()
        chips = _other_chips(x, y)
        mine = full.at[_slot(x, y, c)]
        sends = [_remote(blk, mine, send(0), recv(0), (x, y, 1 - c))]
        sends += [_remote(blk, mine, send(1 + j), recv(1 + j), (*chip, c)) for j, chip in enumerate(chips)]
        recvs = [_remote(blk, full.at[_slot(x, y, 1 - c)], send(0), recv(0), (x, y, 1 - c))]
        recvs += [_remote(blk, full.at[_slot(*chip, c)], send(1 + j), recv(1 + j), (*chip, c))
                  for j, chip in enumerate(chips)]
        return sends, recvs, [pltpu.make_async_copy(blk, mine, loc(0))]

    return _Comm((half_block,), (SDS((2 * N_CHIPS,) + half_block.shape, half_block.dtype),), {}, 4, 1, plan)


def _gather_second(partly_gathered):
    def plan(ins, outs, send, recv, loc):
        (src,), (full,) = ins, outs
        x, y, c = _place()
        chips = _other_chips(x, y)
        sends = [_remote(src.at[_slot(*chip, c)], full.at[_slot(*chip, c)], send(j), recv(j), (x, y, 1 - c))
                 for j, chip in enumerate(chips)]
        recvs = [_remote(src.at[_slot(*chip, 1 - c)], full.at[_slot(*chip, 1 - c)], send(j), recv(j), (x, y, 1 - c))
                 for j, chip in enumerate(chips)]
        return sends, recvs, []

    return _Comm((partly_gathered,), (SDS(partly_gathered.shape, partly_gathered.dtype),), {0: 0}, 3, 0, plan)


def _relay_pieces(full, rows, x, y, c):
    start, half = rows[0], rows[1] // 2
    upper, lower = pl.ds(start, half), pl.ds(start + half, half)
    diagonal = full.at[_slot(1 - x, 1 - y, c)]
    return [(full.at[_slot(1 - x, y, c), upper], diagonal.at[upper], (x, 1 - y, c)),
            (full.at[_slot(x, 1 - y, c), lower], diagonal.at[lower], (1 - x, y, c))]


def _relay(half_block, so_far, first=None, second=None, third=None, third_after=None):
    has_block, has_buffer = half_block is not None, so_far is not None
    shape = so_far.shape if has_buffer else (2 * N_CHIPS,) + half_block.shape
    dtype = so_far.dtype if has_buffer else half_block.dtype

    def third_leg(rows, k, ins, outs, send, recv):
        src, full = (ins[-1] if has_buffer else outs[0]), outs[0]
        x, y, c = _place()
        span, sibling = pl.ds(*rows), (x, y, 1 - c)
        here, there = _slot(1 - x, 1 - y, c), _slot(1 - x, 1 - y, 1 - c)
        return ([_remote(src.at[here, span], full.at[here, span], send(k), recv(k), sibling)],
                [_remote(src.at[there, span], full.at[there, span], send(k), recv(k), sibling)])

    def plan(ins, outs, send, recv, loc):
        src, full = (ins[-1] if has_buffer else outs[0]), outs[0]
        x, y, c = _place()
        sibling = (x, y, 1 - c)
        sends, recvs, locs = [], [], []
        if first is not None:
            span = pl.ds(*first)
            blk, mine = ins[0].at[span], full.at[_slot(x, y, c), span]
            for k, peer in enumerate([sibling, (1 - x, y, c), (x, 1 - y, c)]):
                sends.append(_remote(blk, mine, send(k), recv(k), peer))
                recvs.append(_remote(blk, full.at[_slot(*peer), span], send(k), recv(k), peer))
            locs.append(pltpu.make_async_copy(blk, mine, loc(0)))
        if second is not None:
            span = pl.ds(*second)
            for k, chip in enumerate([(1 - x, y), (x, 1 - y)]):
                sends.append(_remote(src.at[_slot(*chip, c), span], full.at[_slot(*chip, c), span], send(3 + k), recv(3 + k),
                                     sibling))
                recvs.append(_remote(src.at[_slot(*chip, 1 - c), span], full.at[_slot(*chip, 1 - c), span], send(3 + k),
                                     recv(3 + k), sibling))
            for k, (piece, lands, peer) in enumerate(_relay_pieces(full, second, x, y, c)):
                sends.append(_remote(piece, piece, send(5 + k), recv(5 + k), peer))
                recvs.append(_remote(lands, lands, send(5 + k), recv(5 + k), peer))
        if third is not None:
            s, r = third_leg(third, 7, ins, outs, send, recv)
            sends, recvs = sends + s, recvs + r
        return sends, recvs, locs

    def after(ins, outs, send, recv, loc):
        s, r = third_leg(third_after, 8, ins, outs, send, recv)
        return s, r, []

    operands = ((half_block,) if has_block else ()) + ((so_far,) if has_buffer else ())
    return _Comm(operands, (SDS(shape, dtype),), {len(operands) - 1: 0} if has_buffer else {}, 9, 1, plan,
                 after if third_after is not None else None)


def _gather_whole(first, others, small_block):
    shards = (first, *others)
    n = len(shards)
    hs = [s.shape[0] // 2 for s in shards]
    rows = hs[0]

    def body(*refs):
        src, small_ref = refs[:n], refs[n]
        out_ref, small_out_ref, half_out = refs[n + 1], refs[n + 2], refs[n + 3:2 * n + 2]
        stage, half = refs[2 * n + 2:3 * n + 2], refs[3 * n + 2:4 * n + 2]
        send_sems, recv_sems, local_sems = refs[4 * n + 2:]
        x, y, c = _place()
        me, sibling = (x, y, c), (x, y, 1 - c)
        neighbours, diagonal = [(1 - x, y), (x, 1 - y)], (1 - x, 1 - y)
        loads = [pltpu.make_async_copy(src[k].at[pl.ds(c * hs[k], hs[k])], stage[k], local_sems.at[2 + k]) for k in range(n)]
        loads[0].start()
        loads[0].wait()
        for cp in loads[1:]:
            cp.start()
        blk_ref = half[0]
        blk_ref[...] = stage[0][...].astype(BF16)

        def copy(k, block, to, src=None):
            return _remote(out_ref.at[_slot(*block)] if src is None else src, out_ref.at[_slot(*block)],
                           send_sems.at[k], recv_sems.at[k], to)

        def small_copy(k, chip, to):
            return _remote(small_ref, small_out_ref.at[2 * chip[0] + chip[1]], send_sems.at[8 + k], recv_sems.at[8 + k], to)

        mine = pltpu.make_async_copy(blk_ref, out_ref.at[_slot(*me)], local_sems.at[0])
        mine_small = pltpu.make_async_copy(small_ref, small_out_ref.at[2 * x + y], local_sems.at[1])
        mine.start()
        mine_small.start()
        started = [copy(0, me, sibling, src=blk_ref)]
        started += [copy(1 + k, me, (*chip, c), src=blk_ref) for k, chip in enumerate(neighbours)]
        started += [small_copy(k, (x, y), (*chip, c)) for k, chip in enumerate(neighbours + [diagonal])]
        for cp in started:
            cp.start()
        stores = []
        for k in range(1, n):
            loads[k].wait()
            half[k][...] = stage[k][...].astype(BF16)
            stores.append(pltpu.make_async_copy(half[k], half_out[k - 1], local_sems.at[2 + n + k]))
            stores[-1].start()
        pieces = _relay_pieces(out_ref, (0, rows), x, y, c)
        for k, chip in enumerate(neighbours):
            copy(1 + k, (*chip, c), me).wait_recv()
            piece, _, peer = pieces[k]
            started += [copy(3 + k, (*chip, c), sibling), _remote(piece, piece, send_sems.at[5 + k], recv_sems.at[5 + k], peer)]
            started[-2].start()
            started[-1].start()
        for k, (_, lands, peer) in enumerate(pieces):
            _remote(lands, lands, send_sems.at[5 + k], recv_sems.at[5 + k], peer).wait_recv()
        started.append(copy(7, (*diagonal, c), sibling))
        started[-1].start()
        copy(0, sibling, me).wait_recv()
        for k, chip in enumerate(neighbours):
            copy(3 + k, (*chip, 1 - c), me).wait_recv()
        copy(7, (*diagonal, 1 - c), me).wait_recv()
        for k, chip in enumerate(neighbours + [diagonal]):
            small_copy(k, chip, me).wait_recv()
        for cp in started:
            cp.wait_send()
        mine.wait()
        mine_small.wait()
        for cp in stores:
            cp.wait()

    return pl.pallas_call(
        body, name="gather_whole", in_specs=[ANY] * (n + 1), out_specs=[ANY] * (n + 1),
        out_shape=[SDS((2 * N_CHIPS, rows, D_MODEL), BF16), SDS((N_CHIPS,) + small_block.shape, small_block.dtype)]
                  + [SDS((h, D_MODEL), BF16) for h in hs[1:]],
        scratch_shapes=[pltpu.VMEM((h, D_MODEL), F32) for h in hs] + [pltpu.VMEM((h, D_MODEL), BF16) for h in hs]
                       + [pltpu.SemaphoreType.DMA((11,)), pltpu.SemaphoreType.DMA((11,)), pltpu.SemaphoreType.DMA((2 + 2 * n,))],
        compiler_params=pltpu.CompilerParams(vmem_limit_bytes=VMEM_LIMIT_V7X),
    )(*shards, small_block)


def _pair_send(grads):
    def plan(ins, outs, send, recv, loc):
        (g,), (got,) = ins, outs
        x, y, c = _place()
        copies = [_remote(g.at[j, 1 - c], got.at[j], send(j), recv(j), (x, y, 1 - c)) for j in range(N_CHIPS)]
        return copies, copies, []

    shape = (grads.shape[0],) + grads.shape[2:]
    return _Comm((grads,), (SDS(shape, grads.dtype),), {}, N_CHIPS, 0, plan)


def _chip_exchange(partial):
    def plan(ins, outs, send, recv, loc):
        (p,), (got,) = ins, outs
        x, y, c = _place()
        my_chip = 2 * x + y
        chips = _other_chips(x, y)
        sends = [_remote(p.at[2 * chip[0] + chip[1]], got.at[my_chip], send(j), recv(j), (*chip, c))
                 for j, chip in enumerate(chips)]
        recvs = [_remote(p.at[my_chip], got.at[2 * chip[0] + chip[1]], send(j), recv(j), (*chip, c))
                 for j, chip in enumerate(chips)]
        return sends, recvs, [pltpu.make_async_copy(p.at[my_chip], got.at[my_chip], loc(0))]

    return _Comm((partial,), (SDS(partial.shape, partial.dtype),), {}, 3, 1, plan)


def _pair_sum(name, core, grads, received):
    h = grads.shape[2]

    def body(core_ref, g_ref, r_ref, o_ref):
        o_ref[...] = (g_ref[0] + r_ref[...]).astype(BF16)

    return pl.pallas_call(
        body, name=name,
        grid_spec=pltpu.PrefetchScalarGridSpec(
            num_scalar_prefetch=1, grid=(N_CHIPS,),
            in_specs=[pl.BlockSpec((1, 1, h, D_MODEL), lambda j, core_ref: (j, core_ref[0], 0, 0)),
                      pl.BlockSpec((1, h, D_MODEL), lambda j, core_ref: (j, 0, 0))],
            out_specs=pl.BlockSpec((1, h, D_MODEL), lambda j, core_ref: (j, 0, 0))),
        out_shape=SDS((N_CHIPS, h, D_MODEL), BF16),
        compiler_params=pltpu.CompilerParams(dimension_semantics=("arbitrary",), vmem_limit_bytes=VMEM_LIMIT_V7X),
    )(core, grads, received)


SMALL_ROWS = 8


def _sum_blocks(ref):
    return (ref[0].astype(F32) + ref[1].astype(F32)) + (ref[2].astype(F32) + ref[3].astype(F32))


def _tail_reduce(last_grads, exchanged, small):
    n = len(exchanged)
    h = last_grads.shape[2]

    def body(*refs):
        g_ref, ex, small_ref = refs[0], refs[1:1 + n], refs[1 + n]
        o0 = 2 + n
        out, out_last, small_out = refs[o0:o0 + n], refs[o0 + n], refs[o0 + n + 1]
        s0 = o0 + n + 2
        halves, half_last = refs[s0:s0 + n], refs[s0 + n]
        own, got, part, exch, small_buf = refs[s0 + n + 1:s0 + n + 6]
        ex_buf = refs[s0 + n + 6:s0 + 2 * n + 6]
        pair_send, pair_recv, chip_send, chip_recv, share_send, share_recv, small_send, small_recv, local_sems = refs[s0 + 2 * n + 6:]
        x, y, c = _place()
        sibling = (x, y, 1 - c)
        my_chip, me = 2 * x + y, _slot(x, y, c)
        chips = _other_chips(x, y)[::-1]

        order = [2 * chip[0] + chip[1] for chip in chips] + [my_chip]
        to_sibling = [_remote(g_ref.at[j, 1 - c], got.at[j], pair_send.at[j], pair_recv.at[j], sibling) for j in order]
        load_own = [pltpu.make_async_copy(g_ref.at[j, c], own.at[j], local_sems.at[j]) for j in order]
        load_ex = [pltpu.make_async_copy(ex[k], ex_buf[k], local_sems.at[N_CHIPS + n + 1 + k]) for k in range(n)]
        for give, keep in zip(to_sibling, load_own):
            give.start()
            keep.start()
        for cp in load_ex:
            cp.start()

        small_buf[me] = small_ref[...]
        small_copies = []
        for mask in range(1, 8):
            peer = (x ^ (mask >> 2), y ^ ((mask >> 1) & 1), c ^ (mask & 1))
            small_copies.append(_remote(small_ref, small_buf.at[me], small_send.at[mask - 1], small_recv.at[mask - 1], peer))
        for cp in small_copies:
            cp.start()

        def share(k, half_ref, out_ref):
            keep = pltpu.make_async_copy(half_ref, out_ref.at[c], local_sems.at[N_CHIPS + k])
            give = _remote(half_ref, out_ref.at[c], share_send.at[k], share_recv.at[k], sibling)
            take = _remote(half_ref, out_ref.at[1 - c], share_send.at[k], share_recv.at[k], sibling)
            keep.start()
            give.start()
            return keep, give, take

        def pair_sum(block):
            _remote(g_ref.at[block, 1 - c], got.at[block], pair_send.at[block], pair_recv.at[block], sibling).wait_recv()
            pltpu.make_async_copy(g_ref.at[block, c], own.at[block], local_sems.at[block]).wait()
            part[block] = (own[block] + got[block]).astype(BF16)

        to_chips = []
        for j, chip in enumerate(chips):
            block = 2 * chip[0] + chip[1]
            pair_sum(block)
            to_chips.append(_remote(part.at[block], exch.at[my_chip], chip_send.at[j], chip_recv.at[j], (*chip, c)))
            to_chips[-1].start()
        pair_sum(my_chip)
        exch[my_chip] = part[my_chip]
        from_chips = [_remote(part.at[my_chip], exch.at[2 * chip[0] + chip[1]], chip_send.at[j], chip_recv.at[j], (*chip, c))
                      for j, chip in enumerate(chips)]

        shares = []
        for k in range(n):
            load_ex[k].wait()
            halves[k][...] = _sum_blocks(ex_buf[k])
            shares.append(share(k, halves[k], out[k]))

        for cp in small_copies:
            cp.wait_recv()
        total = small_buf[0]
        for d in range(1, 8):
            total = total + small_buf[d]
        small_out[...] = total

        for cp in from_chips:
            cp.wait_recv()
        half_last[...] = _sum_blocks(exch)
        shares.append(share(n, half_last, out_last))

        for keep, give, take in shares:
            take.wait_recv()
            give.wait_send()
            keep.wait()
        for cp in to_sibling + to_chips + small_copies:
            cp.wait_send()

    blocks = (N_CHIPS, h, D_MODEL)
    return pl.pallas_call(
        body, name="tail_reduce",
        in_specs=[ANY] * (n + 1) + [VMEM_WHOLE], out_specs=[ANY] * (n + 1) + [VMEM_WHOLE],
        out_shape=[SDS((2,) + e.shape[1:], F32) for e in exchanged] + [SDS((2, h, D_MODEL), F32), SDS(small.shape, F32)],
        scratch_shapes=[pltpu.VMEM(e.shape[1:], F32) for e in exchanged] + [pltpu.VMEM((h, D_MODEL), F32)]
                       + [pltpu.VMEM(blocks, F32), pltpu.VMEM(blocks, F32), pltpu.VMEM(blocks, BF16), pltpu.VMEM(blocks, BF16),
                          pltpu.VMEM((8,) + small.shape, F32)]
                       + [pltpu.VMEM(e.shape, BF16) for e in exchanged]
                       + [pltpu.SemaphoreType.DMA((N_CHIPS,)), pltpu.SemaphoreType.DMA((N_CHIPS,)),
                          pltpu.SemaphoreType.DMA((3,)), pltpu.SemaphoreType.DMA((3,)),
                          pltpu.SemaphoreType.DMA((n + 1,)), pltpu.SemaphoreType.DMA((n + 1,)),
                          pltpu.SemaphoreType.DMA((7,)), pltpu.SemaphoreType.DMA((7,)),
                          pltpu.SemaphoreType.DMA((N_CHIPS + 2 * n + 1,))],
        compiler_params=pltpu.CompilerParams(vmem_limit_bytes=VMEM_LIMIT_V7X),
    )(last_grads, *exchanged, small)


def _rope_expansion():
    half = ROT_DIM // 2
    expand = np.zeros((2 * half, 3 * 128), np.float32)
    const = np.zeros((1, 3 * 128), np.float32)
    for lane in range(128):
        d = lane % HEAD_DIM
        if d < ROT_DIM:
            expand[d % half, lane] = 1.0
        else:
            const[0, lane] = 1.0
        if d < half:
            expand[half + d, 128 + lane] = -1.0
        elif d < ROT_DIM:
            expand[half + d - half, 256 + lane] = 1.0
    return expand, const


ROPE_PIECES = 3 * ROT_DIM


def _rope_inputs(seq):
    pos = jnp.arange(seq, dtype=F32)
    inv_freq = ROPE_THETA ** (-jnp.arange(0, ROT_DIM, 2, dtype=F32) / ROT_DIM)
    ang = pos[:, None] * inv_freq[None, :]
    cs = jnp.concatenate([jnp.cos(ang), jnp.sin(ang)], axis=1)
    hi = lax.reduce_precision(cs, 8, 7)
    mid = lax.reduce_precision(cs - hi, 8, 7)
    low = cs - hi - mid
    expand, const = _rope_expansion()
    pieces = jnp.concatenate([hi, mid, low], axis=1).astype(BF16)
    return pieces, jnp.asarray(np.concatenate([expand] * 3, axis=0), BF16), jnp.asarray(const)


def _rope_specs(tb):
    return [pl.BlockSpec((tb, ROPE_PIECES), lambda i: (i, 0)), _resident((ROPE_PIECES, 3 * 128)), _resident((1, 3 * 128))]


def _rope_tile(pieces_ref, expand_ref, const_ref):
    tables = _dot(pieces_ref[...], expand_ref[...]) + const_ref[...]
    return tables[:, 0:128], tables[:, 128:256], tables[:, 256:384]


def _rope(t, c, sa, sb):
    half = ROT_DIM // 2
    return t * c + pltpu.roll(t, 128 - half, 1) * sa + pltpu.roll(t, half, 1) * sb


def _rope_transposed(dt, c, sa, sb):
    half = ROT_DIM // 2
    return dt * c + pltpu.roll(dt * sa, half, 1) + pltpu.roll(dt * sb, 128 - half, 1)


def _in_proj(x, g_pre, w_in_t, rope, comm=None):
    seq = x.shape[0]
    tb = min(seq, WIDE_TOKEN_TILE)

    def body(x_ref, g_ref, w_ref, c_ref, sa_ref, sb_ref,
             q_ref, kd0_ref, kd1_ref, vd0_ref, vd1_ref, gb_ref, gc_ref, xin_ref, hn_ref):
        xv = x_ref[...]
        hn = (xv * _rms(xv) * g_ref[...]).astype(BF16)
        hn_ref[...] = hn
        proj = _dot_nt(hn, w_ref[...].reshape(IN_COLS, D_MODEL))
        c, sa, sb = _rope_tile(c_ref, sa_ref, sb_ref)
        scale = 1.0 / math.sqrt(HEAD_DIM)
        for p in range(Q_WIDTH // 128):
            q_ref[:, 128 * p:128 * (p + 1)] = (_rope(proj[:, 128 * p:128 * (p + 1)], c, sa, sb) * scale).astype(BF16)
        k = _rope(proj[:, Q_WIDTH:Q_WIDTH + KV_WIDTH], c, sa, sb)
        v = proj[:, Q_WIDTH + KV_WIDTH:Q_WIDTH + 2 * KV_WIDTH]
        low = _lane_lt64(k.shape)
        k_sw, v_sw = pltpu.roll(k, HEAD_DIM, 1), pltpu.roll(v, HEAD_DIM, 1)
        kd0_ref[...] = jnp.where(low, k, k_sw).astype(BF16)
        kd1_ref[...] = jnp.where(low, k_sw, k).astype(BF16)
        vd0_ref[...] = jnp.where(low, v, v_sw).astype(BF16)
        vd1_ref[...] = jnp.where(low, v_sw, v).astype(BF16)
        base = Q_WIDTH + 2 * KV_WIDTH
        gb_ref[...] = proj[:, base:base + CONV_WIDTH].astype(BF16)
        gc_ref[...] = proj[:, base + CONV_WIDTH:base + 2 * CONV_WIDTH].astype(BF16)
        xin_ref[...] = proj[:, base + 2 * CONV_WIDTH:base + 3 * CONV_WIDTH].astype(BF16)

    tile = lambda w: pl.BlockSpec((tb, w), lambda i: (i, 0))
    return _pallas(
        body, name="in_proj", grid=(seq // tb,),
        in_specs=[tile(D_MODEL), _resident((1, D_MODEL)), _resident(w_in_t.shape), *_rope_specs(tb)],
        out_specs=[tile(Q_WIDTH), tile(128), tile(128), tile(128), tile(128),
                   tile(CONV_WIDTH), tile(CONV_WIDTH), tile(CONV_WIDTH), tile(D_MODEL)],
        out_shape=[SDS((seq, Q_WIDTH), BF16)] + [SDS((seq, 128), BF16)] * 4
                  + [SDS((seq, CONV_WIDTH), BF16)] * 3 + [SDS((seq, D_MODEL), BF16)],
        operands=(x, g_pre, w_in_t, *rope), comm=comm)


def _attn_valid(i):
    shape = (4 * QBLOCK, 2 * QBLOCK)
    row = lax.broadcasted_iota(jnp.int32, shape, 0)
    col = lax.broadcasted_iota(jnp.int32, shape, 1)
    qi = row & (QBLOCK - 1)
    return (col > qi) & (col <= qi + QBLOCK) & ((col >= QBLOCK) | (i > 0))


def _stack_heads(pair0, pair1):
    low = _lane_lt64(pair0.shape)
    zero = jnp.zeros_like(pair0)
    return jnp.concatenate([jnp.where(low, pair0, zero), jnp.where(low, zero, pair0),
                            jnp.where(low, pair1, zero), jnp.where(low, zero, pair1)], axis=0)


def _unstack_heads(stacked):
    low = _lane_lt64((QBLOCK, 128))
    pair0 = jnp.where(low, stacked[0:QBLOCK], stacked[QBLOCK:2 * QBLOCK])
    pair1 = jnp.where(low, stacked[2 * QBLOCK:3 * QBLOCK], stacked[3 * QBLOCK:4 * QBLOCK])
    return pair0, pair1


def _sink_column(sink_ref, kv_head):
    row = lax.broadcasted_iota(jnp.int32, (4 * QBLOCK, 1), 0)
    s = [sink_ref[0, 4 * kv_head + j] for j in range(4)]
    return jnp.where(row < QBLOCK, s[0], jnp.where(row < 2 * QBLOCK, s[1], jnp.where(row < 3 * QBLOCK, s[2], s[3])))


def _band(ref, i):
    prev = pl.multiple_of(jnp.maximum(i - 1, 0) * QBLOCK, QBLOCK)
    own = pl.multiple_of(i * QBLOCK, QBLOCK)
    return jnp.concatenate([ref[pl.ds(prev, QBLOCK), :], ref[pl.ds(own, QBLOCK), :]], axis=0), prev, own


def _softmax_with_sink(s, sink_col):
    m = jnp.maximum(jnp.max(s, axis=-1, keepdims=True), sink_col)
    p = jnp.exp(s - m)
    e_sink = jnp.exp(sink_col - m)
    inv_l = 1.0 / (jnp.sum(p, axis=-1, keepdims=True) + e_sink)
    return p, e_sink, inv_l


def _attention_fwd(q, kd0, kd1, vd0, vd1, sinks, comm=None):
    seq = q.shape[0]

    nb = ATTN_FWD_BLOCKS

    def body(sink_ref, q_ref, kd0_ref, kd1_ref, vd0_ref, vd1_ref, o_ref):
        for b in range(nb):
            i = pl.program_id(0) * nb + b
            rows = slice(QBLOCK * b, QBLOCK * (b + 1))
            valid = _attn_valid(i)
            for kv_head, (k_ref, v_ref) in enumerate(((kd0_ref, vd0_ref), (kd1_ref, vd1_ref))):
                kband, _, _ = _band(k_ref, i)
                vband, _, _ = _band(v_ref, i)
                base = 256 * kv_head
                qm = _stack_heads(q_ref[rows, base:base + 128], q_ref[rows, base + 128:base + 256])
                s = jnp.where(valid, _dot_nt(qm, kband), NEG_INF)
                p, _, inv_l = _softmax_with_sink(s, _sink_column(sink_ref, kv_head))
                o = _dot(p.astype(BF16), vband) * inv_l
                pair0, pair1 = _unstack_heads(o)
                o_ref[rows, base:base + 128] = pair0.astype(BF16)
                o_ref[rows, base + 128:base + 256] = pair1.astype(BF16)

    blk = pl.BlockSpec((nb * QBLOCK, Q_WIDTH), lambda i: (i, 0))
    full = _resident((seq, 128))
    return _pallas(
        body, name="attention_fwd", grid=(seq // (nb * QBLOCK),),
        in_specs=[pl.BlockSpec(memory_space=pltpu.SMEM), blk, full, full, full, full],
        out_specs=[blk], out_shape=[SDS((seq, Q_WIDTH), BF16)],
        operands=(sinks, q, kd0, kd1, vd0, vd1), comm=comm)


HALO = 16


def _conv_parts(gc, xin, gc_halo, xin_halo, conv_w, first):
    tb = gc.shape[0]
    u = gc.astype(F32) * xin.astype(F32)
    u_halo = jnp.where(first, 0.0, gc_halo.astype(F32) * xin_halo.astype(F32))
    ext = jnp.concatenate([u_halo, u], axis=0)
    u1 = pltpu.roll(ext, 1, 0)[HALO:HALO + tb]
    u2 = pltpu.roll(ext, 2, 0)[HALO:HALO + tb]
    y = conv_w[0:1, :] * u2 + conv_w[1:2, :] * u1 + conv_w[2:3, :] * u
    return u, u1, u2, y


def _halo_prev(tb, w):
    return pl.BlockSpec((HALO, w), lambda i: (jnp.maximum(i * (tb // HALO) - 1, 0), 0))


def _residual_mid(x, mix, g_post_mix):
    mix_f = mix.astype(F32)
    return x + mix_f * _rms(mix_f) * g_post_mix


def _mix_out(attn, gb, gc, xin, conv_w, g_attn, g_conv, w_out, comm=None):
    seq = attn.shape[0]
    tb = min(seq, WIDE_TOKEN_TILE)

    def body(a_ref, gb_ref, gc_ref, xin_ref, gch_ref, xinh_ref, cw_ref, ga_ref, gcn_ref, w_ref, mix_ref, mixed_ref):
        first = pl.program_id(0) == 0
        _, _, _, y = _conv_parts(gc_ref[...], xin_ref[...], gch_ref[...], xinh_ref[...], cw_ref[...], first)
        conv = gb_ref[...].astype(F32) * y
        a = a_ref[...].astype(F32)
        mixed_ref[:, 0:Q_WIDTH] = (a * _rms(a) * ga_ref[...]).astype(BF16)
        mixed_ref[:, Q_WIDTH:] = (conv * _rms(conv) * gcn_ref[...]).astype(BF16)
        mix_ref[...] = _dot(mixed_ref[...], w_ref[...].reshape(D_MODEL, D_MODEL)).astype(BF16)

    tile = lambda w: pl.BlockSpec((tb, w), lambda i: (i, 0))
    return _pallas(
        body, name="mix_out", grid=(seq // tb,),
        in_specs=[tile(Q_WIDTH), tile(CONV_WIDTH), tile(CONV_WIDTH), tile(CONV_WIDTH),
                  _halo_prev(tb, CONV_WIDTH), _halo_prev(tb, CONV_WIDTH),
                  _resident((CONV_K, CONV_WIDTH)), _resident((1, Q_WIDTH)), _resident((1, CONV_WIDTH)),
                  _resident(w_out.shape)],
        out_specs=[tile(D_MODEL), tile(D_MODEL)],
        out_shape=[SDS((seq, D_MODEL), BF16), SDS((seq, D_MODEL), BF16)],
        operands=(attn, gb, gc, xin, gc, xin, conv_w, g_attn, g_conv, w_out), comm=comm)


def _mlp_fwd_bwd(x, mix, target, g_post_mix, g_pre_mlp, g_post_mlp, w_up, w_down):
    seq = x.shape[0]
    tb = TOKEN_TILE

    def body(x_ref, mix_ref, t_ref, gpm_ref, g2_ref, g4_ref, wup_ref, wdown_ref,
             up_ref, hn2_ref, dmlp_ref, dup_ref, dh_ref, dmix_ref, loss_ref, dg4_ref, dg2_ref, dgpm_ref):
        @pl.when(pl.program_id(0) == 0)
        def _():
            for ref in (loss_ref, dg4_ref, dg2_ref, dgpm_ref):
                ref[...] = jnp.zeros_like(ref)

        halves = [slice(0, tb // 2), slice(tb // 2, tb)]
        chunks = [slice(1024 * j, 1024 * (j + 1)) for j in range(N_CHIPS)]
        hv, hn2, mlp, dout, dmlp, dhn2 = [], [], [], [], [], []
        for rows in halves:
            hv.append(_residual_mid(x_ref[rows, :], mix_ref[rows, :], gpm_ref[...]))
            hn2.append((hv[-1] * _rms(hv[-1]) * g2_ref[...]).astype(BF16))
            hn2_ref[rows, :] = hn2[-1]
        for k, rows in enumerate(halves):
            acc = None
            for j, cols in enumerate(chunks):
                up = jnp.maximum(_dot(hn2[k], _chip_block(wup_ref, j)), 0.0)
                up_ref[rows, cols] = up.astype(BF16)
                part = _dot((up * up).astype(BF16), _chip_block(wdown_ref, j))
                acc = part if acc is None else acc + part
            mlp.append(acc)
        loss = jnp.zeros((1, 1), F32)
        dg4 = jnp.zeros((1, D_MODEL), F32)
        for k, rows in enumerate(halves):
            rstd = _rms(mlp[k])
            zhat = mlp[k] * rstd
            diff = hv[k] + zhat * g4_ref[...] - t_ref[rows, :]
            loss = loss + jnp.sum(jnp.sum(diff * diff, axis=1, keepdims=True), axis=0, keepdims=True)
            dout.append(diff * (1.0 / D_MODEL))
            dg4 = dg4 + _colsum(dout[k] * zhat)
            dmlp.append(_norm_bwd(dout[k], g4_ref[...], zhat, rstd).astype(BF16))
            dmlp_ref[rows, :] = dmlp[k]
        for k, rows in enumerate(halves):
            acc = None
            for j, cols in enumerate(chunks):
                dact = _dot_nt(dmlp[k], _chip_block(wdown_ref, j))
                dup = (dact * (2.0 * up_ref[rows, cols].astype(F32))).astype(BF16)
                dup_ref[rows, cols] = dup
                part = _dot_nt(dup, _chip_block(wup_ref, j))
                acc = part if acc is None else acc + part
            dhn2.append(acc)
        dg2 = jnp.zeros((1, D_MODEL), F32)
        dgpm = jnp.zeros((1, D_MODEL), F32)
        for k, rows in enumerate(halves):
            r2 = _rms(hv[k])
            hhat = hv[k] * r2
            dg2 = dg2 + _colsum(dhn2[k] * hhat)
            dh = dout[k] + _norm_bwd(dhn2[k], g2_ref[...], hhat, r2)
            dh_ref[rows, :] = dh.astype(BF16)
            mix_v = mix_ref[rows, :].astype(F32)
            rz = _rms(mix_v)
            zhat = mix_v * rz
            dgpm = dgpm + _colsum(dh * zhat)
            dmix_ref[rows, :] = _norm_bwd(dh, gpm_ref[...], zhat, rz).astype(BF16)
        loss_ref[...] += loss
        dg4_ref[...] += dg4
        dg2_ref[...] += dg2
        dgpm_ref[...] += dgpm

    tile = lambda w: pl.BlockSpec((tb, w), lambda i: (i, 0))
    vec = pl.BlockSpec((1, D_MODEL), lambda i: (0, 0))
    return _pallas(
        body, name="mlp_fwd_bwd", grid=(seq // tb,),
        in_specs=[tile(D_MODEL), tile(D_MODEL), tile(D_MODEL), _resident((1, D_MODEL)), _resident((1, D_MODEL)),
                  _resident((1, D_MODEL)), _resident(w_up.shape), _resident(w_down.shape)],
        out_specs=[tile(D_FF), tile(D_MODEL), tile(D_MODEL), tile(D_FF), tile(D_MODEL), tile(D_MODEL),
                   pl.BlockSpec((1, 1), lambda i: (0, 0)), vec, vec, vec],
        out_shape=[SDS((seq, D_FF), BF16), SDS((seq, D_MODEL), BF16), SDS((seq, D_MODEL), BF16), SDS((seq, D_FF), BF16),
                   SDS((seq, D_MODEL), BF16), SDS((seq, D_MODEL), BF16),
                   SDS((1, 1), F32), SDS((1, D_MODEL), F32), SDS((1, D_MODEL), F32), SDS((1, D_MODEL), F32)],
        operands=(x, mix, target, g_post_mix, g_pre_mlp, g_post_mlp, w_up, w_down))


def _mix_bwd(dmix, attn, gb, gc, xin, conv_w, g_attn, g_conv, w_out, n_k):
    seq = attn.shape[0]
    tb = seq // (N_CHIPS * n_k)

    def body(first, dmix_ref, a_ref, gb_ref, gc_ref, xin_ref, gch_ref, xinh_ref, cw_ref, ga_ref, gcn_ref, w_ref,
             dattn_ref, dgb_ref, dy_ref, dga_ref, dgcn_ref, dcw_ref):
        @pl.when(first)
        def _():
            dga_ref[...] = jnp.zeros_like(dga_ref)
            dgcn_ref[...] = jnp.zeros_like(dgcn_ref)
            dcw_ref[...] = jnp.zeros_like(dcw_ref)

        dmixed = _dot_nt(dmix_ref[...], w_ref[...].reshape(D_MODEL, D_MODEL))
        a = a_ref[...].astype(F32)
        ra = _rms(a)
        ahat = a * ra
        dan = dmixed[:, 0:Q_WIDTH]
        dga_ref[...] += _colsum(dan * ahat)
        dattn_ref[...] = _norm_bwd(dan, ga_ref[...], ahat, ra).astype(BF16)
        gbv = gb_ref[...].astype(F32)
        u, u1, u2, y = _conv_parts(gc_ref[...], xin_ref[...], gch_ref[...], xinh_ref[...], cw_ref[...], first)
        conv = gbv * y
        rc = _rms(conv)
        chat = conv * rc
        dcn = dmixed[:, Q_WIDTH:]
        dgcn_ref[...] += _colsum(dcn * chat)
        dconv = _norm_bwd(dcn, gcn_ref[...], chat, rc)
        dgb_ref[...] = (dconv * y).astype(BF16)
        dy = dconv * gbv
        dy_ref[...] = dy.astype(BF16)
        dcw_ref[0:1, :] += _colsum(dy * u2)
        dcw_ref[1:2, :] += _colsum(dy * u1)
        dcw_ref[2:3, :] += _colsum(dy * u)

    tile = lambda w: pl.BlockSpec((tb, w), lambda j, k: (j * n_k + k, 0))
    halo = lambda w: pl.BlockSpec((HALO, w), lambda j, k: (jnp.maximum((j * n_k + k) * (tb // HALO) - 1, 0), 0))
    whole = lambda shape: pl.BlockSpec(shape, lambda j, k: (0,) * len(shape))
    return _Rider(
        body,
        in_specs=[tile(D_MODEL), tile(Q_WIDTH), tile(CONV_WIDTH), tile(CONV_WIDTH), tile(CONV_WIDTH),
                  halo(CONV_WIDTH), halo(CONV_WIDTH),
                  _resident((CONV_K, CONV_WIDTH)), _resident((1, Q_WIDTH)), _resident((1, CONV_WIDTH)),
                  _resident(w_out.shape)],
        out_specs=[tile(Q_WIDTH), tile(CONV_WIDTH), tile(CONV_WIDTH),
                   whole((1, Q_WIDTH)), whole((1, CONV_WIDTH)), whole((CONV_K, CONV_WIDTH))],
        out_shape=[SDS((seq, Q_WIDTH), BF16), SDS((seq, CONV_WIDTH), BF16), SDS((seq, CONV_WIDTH), BF16),
                   SDS((1, Q_WIDTH), F32), SDS((1, CONV_WIDTH), F32), SDS((CONV_K, CONV_WIDTH), F32)],
        operands=(dmix, attn, gb, gc, xin, gc, xin, conv_w, g_attn, g_conv, w_out))


def _attention_bwd(q, dattn, attn, kd0, kd1, vd0, vd1, sinks, comm=None):
    seq = q.shape[0]
    nb = ATTN_BWD_BLOCKS

    def body(sink_ref, q_ref, do_ref, o_ref, kd0_ref, kd1_ref, vd0_ref, vd1_ref,
             dq_ref, dk0_ref, dk1_ref, dv0_ref, dv1_ref, dsink_ref):
        @pl.when(pl.program_id(0) == 0)
        def _():
            for r in (dk0_ref, dk1_ref, dv0_ref, dv1_ref, dsink_ref):
                r[...] = jnp.zeros_like(r)

        lane = lax.broadcasted_iota(jnp.int32, (1, 128), 1)
        dsink = jnp.zeros((1, 128), F32)
        for b in range(nb):
            i = pl.program_id(0) * nb + b
            rows = slice(QBLOCK * b, QBLOCK * (b + 1))
            valid = _attn_valid(i)
            for kv_head, (k_ref, v_ref, dk_ref, dv_ref) in enumerate(
                    ((kd0_ref, vd0_ref, dk0_ref, dv0_ref), (kd1_ref, vd1_ref, dk1_ref, dv1_ref))):
                kband, prev, own = _band(k_ref, i)
                vband, _, _ = _band(v_ref, i)
                base = 256 * kv_head
                qm = _stack_heads(q_ref[rows, base:base + 128], q_ref[rows, base + 128:base + 256])
                dom = _stack_heads(do_ref[rows, base:base + 128], do_ref[rows, base + 128:base + 256])
                om = _stack_heads(o_ref[rows, base:base + 128], o_ref[rows, base + 128:base + 256])
                s = jnp.where(valid, _dot_nt(qm, kband), NEG_INF)
                p, e_sink, inv_l = _softmax_with_sink(s, _sink_column(sink_ref, kv_head))
                p = p * inv_l
                delta = jnp.sum(dom.astype(F32) * om.astype(F32), axis=-1, keepdims=True)
                ds = (p * (_dot_nt(dom, vband) - delta)).astype(BF16)
                sink_term = -(e_sink * inv_l) * delta
                for j in range(4):
                    part = jnp.sum(sink_term[QBLOCK * j:QBLOCK * (j + 1)], axis=0, keepdims=True)
                    dsink = dsink + jnp.where(lane == 4 * kv_head + j, part, 0.0)
                pair0, pair1 = _unstack_heads(_dot(ds, kband))
                dq_ref[rows, base:base + 128] = pair0.astype(BF16)
                dq_ref[rows, base + 128:base + 256] = pair1.astype(BF16)
                dkd = _dot_tn(ds, qm)
                dkd = dkd + pltpu.roll(dkd, HEAD_DIM, 1)
                dvd = _dot_tn(p.astype(BF16), dom)
                dvd = dvd + pltpu.roll(dvd, HEAD_DIM, 1)
                dk_ref[pl.ds(prev, QBLOCK), :] += dkd[0:QBLOCK]
                dk_ref[pl.ds(own, QBLOCK), :] += dkd[QBLOCK:]
                dv_ref[pl.ds(prev, QBLOCK), :] += dvd[0:QBLOCK]
                dv_ref[pl.ds(own, QBLOCK), :] += dvd[QBLOCK:]
        dsink_ref[...] += dsink

    blk = pl.BlockSpec((nb * QBLOCK, Q_WIDTH), lambda i: (i, 0))
    full = _resident((seq, 128))
    acc = pl.BlockSpec((seq, 128), lambda i: (0, 0))
    return _pallas(
        body, name="attention_bwd", grid=(seq // (nb * QBLOCK),),
        in_specs=[pl.BlockSpec(memory_space=pltpu.SMEM), blk, blk, blk, full, full, full, full],
        out_specs=[blk, acc, acc, acc, acc, pl.BlockSpec((1, 128), lambda i: (0, 0))],
        out_shape=[SDS((seq, Q_WIDTH), BF16)] + [SDS((seq, 128), F32)] * 4 + [SDS((1, 128), F32)],
        operands=(sinks, q, dattn, attn, kd0, kd1, vd0, vd1), comm=comm)


def _in_proj_bwd(dq, dk0, dk1, dv0, dv1, dgb, dy, gc, xin, conv_w, x, dh, g_pre, w_in_t, rope):
    seq = x.shape[0]
    tb = min(seq, WIDE_TOKEN_TILE)
    n_tiles = seq // tb

    def body(dq_ref, dk0_ref, dk1_ref, dv0_ref, dv1_ref, dgb_ref, dy_ref, dyh_ref, gc_ref, xin_ref, cw_ref,
             x_ref, dh_ref, g_ref, w_ref, c_ref, sa_ref, sb_ref,
             dproj_ref, gx_ref, dg_ref):
        i = pl.program_id(0)

        @pl.when(i == 0)
        def _():
            dg_ref[...] = jnp.zeros_like(dg_ref)

        dy = dy_ref[...].astype(F32)
        ext = jnp.concatenate([dy, jnp.where(i == n_tiles - 1, 0.0, dyh_ref[...].astype(F32))], axis=0)
        dy1 = pltpu.roll(ext, tb + HALO - 1, 0)[0:tb]
        dy2 = pltpu.roll(ext, tb + HALO - 2, 0)[0:tb]
        cw = cw_ref[...]
        du = cw[2:3, :] * dy + cw[1:2, :] * dy1 + cw[0:1, :] * dy2
        scale = 1.0 / math.sqrt(HEAD_DIM)
        base = Q_WIDTH + 2 * KV_WIDTH
        halves = [slice(0, tb // 2), slice(tb // 2, tb)]
        low = _lane_lt64((tb // 2, 128))
        for rows in halves:
            c, sa, sb = _rope_tile(c_ref.at[rows, :], sa_ref, sb_ref)
            for p in range(Q_WIDTH // 128):
                dproj_ref[rows, 128 * p:128 * (p + 1)] = _rope_transposed(
                    dq_ref[rows, 128 * p:128 * (p + 1)].astype(F32) * scale, c, sa, sb).astype(BF16)
            dk = jnp.where(low, dk0_ref[rows, :], dk1_ref[rows, :])
            dproj_ref[rows, Q_WIDTH:Q_WIDTH + KV_WIDTH] = _rope_transposed(dk, c, sa, sb).astype(BF16)
            dproj_ref[rows, Q_WIDTH + KV_WIDTH:base] = jnp.where(low, dv0_ref[rows, :], dv1_ref[rows, :]).astype(BF16)
            dproj_ref[rows, base:base + CONV_WIDTH] = dgb_ref[rows, :]
            dproj_ref[rows, base + CONV_WIDTH:base + 2 * CONV_WIDTH] = (du[rows] * xin_ref[rows, :].astype(F32)).astype(BF16)
            dproj_ref[rows, base + 2 * CONV_WIDTH:] = (du[rows] * gc_ref[rows, :].astype(F32)).astype(BF16)
        w_all = w_ref[...].reshape(IN_COLS, D_MODEL)
        dhn = [_dot(dproj_ref[rows, :], w_all) for rows in halves]
        dg = jnp.zeros((1, D_MODEL), F32)
        for k, rows in enumerate(halves):
            xv = x_ref[rows, :]
            r = _rms(xv)
            xhat = xv * r
            dg = dg + _colsum(dhn[k] * xhat)
            gx_ref[rows, :] = dh_ref[rows, :].astype(F32) + _norm_bwd(dhn[k], g_ref[...], xhat, r)
        dg_ref[...] += dg

    tile = lambda w: pl.BlockSpec((tb, w), lambda i: (i, 0))
    halo_next = pl.BlockSpec((HALO, CONV_WIDTH), lambda i: (jnp.minimum((i + 1) * (tb // HALO), seq // HALO - 1), 0))
    return _pallas(
        body, name="in_proj_bwd", grid=(n_tiles,),
        in_specs=[tile(Q_WIDTH), tile(128), tile(128), tile(128), tile(128), tile(CONV_WIDTH), tile(CONV_WIDTH), halo_next,
                  tile(CONV_WIDTH), tile(CONV_WIDTH), _resident((CONV_K, CONV_WIDTH)),
                  tile(D_MODEL), tile(D_MODEL), _resident((1, D_MODEL)), _resident(w_in_t.shape), *_rope_specs(tb)],
        out_specs=[tile(IN_COLS), tile(D_MODEL), pl.BlockSpec((1, D_MODEL), lambda i: (0, 0))],
        out_shape=[SDS((seq, IN_COLS), BF16), SDS((seq, D_MODEL), F32), SDS((1, D_MODEL), F32)],
        operands=(dq, dk0, dk1, dv0, dv1, dgb, dy, dy, gc, xin, conv_w, x, dh, g_pre, w_in_t, *rope))


def _wgrad_grid(seq, per_chip, h_rows, with_rider=False):
    chips_per_step = 1 if per_chip else N_CHIPS
    m = chips_per_step * 2 * h_rows
    bt = min(seq, WGRAD_TOKEN_TILE if m <= D_MODEL and not with_rider else WGRAD_TOKEN_TILE // 2)
    return chips_per_step, m, bt, seq // bt


def _wgrad(name, a, b, *, per_chip, h_rows, square_a=False, comm=None, rider=None):
    seq = a.shape[0]
    chips_per_step, m, bt, n_k = _wgrad_grid(seq, per_chip, h_rows, rider is not None)
    a_cols = m if per_chip else a.shape[1]
    a_wide = a.shape[1] > a_cols
    b_wide = b.shape[1] > D_MODEL

    def body(a_ref, b_ref, g_ref):
        @pl.when(pl.program_id(1) == 0)
        def _():
            g_ref[...] = jnp.zeros_like(g_ref)

        av = a_ref[...]
        if square_a:
            av = (av.astype(F32) * av.astype(F32)).astype(BF16)
        g_ref[...] += _dot_tn(av, b_ref[...]).reshape(g_ref.shape)

    a_spec = pl.BlockSpec((bt, a_cols), (lambda j, k: (k, j)) if a_wide else (lambda j, k: (k, 0)))
    b_spec = pl.BlockSpec((bt, D_MODEL), (lambda j, k: (k, j)) if b_wide else (lambda j, k: (k, 0)))
    g_spec = pl.BlockSpec((chips_per_step, 2, h_rows, D_MODEL), lambda j, k: (j, 0, 0, 0),
                          pipeline_mode=None if per_chip else pl.Buffered(1))
    return _pallas(
        body, name=name, grid=(N_CHIPS if per_chip else 1, n_k),
        in_specs=[a_spec, b_spec], out_specs=[g_spec], out_shape=[SDS((N_CHIPS, 2, h_rows, D_MODEL), F32)],
        operands=(a, b), comm=comm, rider=rider)


def _adamw_math(w, g, m, v):
    m = ADAM_B1 * m + (1.0 - ADAM_B1) * g
    v = ADAM_B2 * v + (1.0 - ADAM_B2) * (g * g)
    m_hat = m / (1.0 - ADAM_B1 ** ADAM_STEP)
    v_hat = v / (1.0 - ADAM_B2 ** ADAM_STEP)
    delta = -ADAM_LR * (m_hat / (jnp.sqrt(v_hat) + ADAM_EPS) + ADAM_WD * w)
    return delta, m, v


ADAMW_STEPS_PER_HALF = 4


def _adamw_rows(items):
    n = len(items)
    per_half = ADAMW_STEPS_PER_HALF

    def body(*refs):
        for k in range(n):
            r_ref, w_ref, m_ref, v_ref = refs[4 * k:4 * k + 4]
            g_out, d_out, m_out, v_out = refs[4 * (n + k):4 * (n + k) + 4]
            g = r_ref[0]
            g_out[...] = g
            d_out[...], m_out[...], v_out[...] = _adamw_math(w_ref[...], g, m_ref[...], v_ref[...])

    in_specs, out_specs, out_shape, operands = [], [], [], []
    for reduced, w, m, v in items:
        rt = reduced.shape[1] // per_half
        blk = pl.BlockSpec((rt, D_MODEL), lambda h, r: (h * per_half + r, 0))
        in_specs += [pl.BlockSpec((1, rt, D_MODEL), lambda h, r: (h, r, 0)), blk, blk, blk]
        out_specs += [blk] * 4
        out_shape += [SDS(w.shape, F32)] * 4
        operands += [reduced, w, m, v]
    res = _pallas(body, name="adamw_rows", grid=(2, per_half), in_specs=in_specs, out_specs=out_specs,
                  out_shape=out_shape, operands=tuple(operands))
    return [res[4 * k:4 * k + 4] for k in range(n)]


def _adamw_small(packed_grads, w, m, v):
    names = SMALL_NAMES
    n = len(names)
    conv_local = w["conv_w"].shape[-1]

    def body(*refs):
        gp = refs[0]
        w_refs, m_refs, v_refs = refs[1:1 + n], refs[1 + n:1 + 2 * n], refs[1 + 2 * n:1 + 3 * n]
        outs = refs[1 + 3 * n:]
        g_out, d_out, m_out, v_out = outs[0:n], outs[n:2 * n], outs[2 * n:3 * n], outs[3 * n:4 * n]
        chip = 2 * lax.axis_index("x") + lax.axis_index("y")

        def step(k, g, index=None):
            pick = (lambda r: r[...]) if index is None else (lambda r: r[index])
            d, new_m, new_v = _adamw_math(pick(w_refs[k]), g, pick(m_refs[k]), pick(v_refs[k]))
            for ref, val in ((g_out[k], g), (d_out[k], d), (m_out[k], new_m), (v_out[k], new_v)):
                if index is None:
                    ref[...] = val
                else:
                    ref[index] = val

        for k, name in enumerate(names):
            if name in SMALL_VECTORS:
                step(k, gp[SMALL_VECTORS.index(name):SMALL_VECTORS.index(name) + 1, :])
            elif name == "attn_group_norm":
                step(k, gp[4:5, 0:Q_WIDTH])
            elif name == "conv_group_norm":
                step(k, gp[4:5, Q_WIDTH:])
            elif name == "attn_sinks":
                step(k, gp[7:8, 0:8])
            else:
                for t in range(CONV_K):
                    row, base = 5 + t // 2, CONV_WIDTH * (t % 2)
                    g = gp[row:row + 1, base:base + conv_local]
                    for j in range(1, CONV_WIDTH // conv_local):
                        g = jnp.where(chip == j, gp[row:row + 1, base + conv_local * j:base + conv_local * (j + 1)], g)
                    step(k, g, index=(0, slice(t, t + 1), slice(None)))

    shapes = [SDS(w[name].shape, F32) for name in names]
    res = pl.pallas_call(
        body, name="adamw_small", in_specs=[VMEM_WHOLE] * (1 + 3 * n), out_specs=[VMEM_WHOLE] * (4 * n),
        out_shape=shapes * 4,
    )(packed_grads, *[w[k] for k in names], *[m[k] for k in names], *[v[k] for k in names])
    return [dict(zip(names, res[i * n:(i + 1) * n])) for i in range(4)]


SMALL_VECTORS = ("pre_mix_norm", "post_mix_norm", "pre_mlp_norm", "post_mlp_norm")
SMALL_NAMES = SMALL_VECTORS + ("attn_group_norm", "conv_group_norm", "conv_w", "attn_sinks")


def _pack_small(p):
    rows = [p[n].reshape(1, D_MODEL) for n in SMALL_VECTORS]
    rows.append(jnp.concatenate([p["attn_group_norm"].reshape(1, -1), p["conv_group_norm"].reshape(1, -1)], axis=1))
    cw = p["conv_w"].reshape(CONV_K, -1)
    rows.append(jnp.pad(cw, ((0, 1), (0, CONV_WIDTH - cw.shape[1]))).reshape(2, D_MODEL))
    last = jnp.concatenate([p["attn_sinks"].reshape(1, 8), p.get("loss_sum", jnp.zeros((1, 1), F32))], axis=1)
    rows.append(jnp.pad(last, ((0, 0), (0, D_MODEL - 9))))
    return jnp.concatenate(rows, axis=0)


WEIGHT_ORDER = ("pre_mix_norm", "w_in", "conv_w", "attn_sinks", "attn_group_norm", "conv_group_norm", "w_out",
                "post_mix_norm", "pre_mlp_norm", "w_up", "w_down", "post_mlp_norm")


def kernel(x, pre_mix_norm, w_in, conv_w, attn_sinks, attn_group_norm, conv_group_norm, w_out, post_mix_norm, pre_mlp_norm, w_up, w_down, post_mlp_norm, loss_target, m_pre_mix_norm, m_w_in, m_conv_w, m_attn_sinks, m_attn_group_norm, m_conv_group_norm, m_w_out, m_post_mix_norm, m_pre_mlp_norm, m_w_up, m_w_down, m_post_mlp_norm, v_pre_mix_norm, v_w_in, v_conv_w, v_attn_sinks, v_attn_group_norm, v_conv_group_norm, v_w_out, v_post_mix_norm, v_pre_mlp_norm, v_w_up, v_w_down, v_post_mlp_norm):
    w = dict(pre_mix_norm=pre_mix_norm, w_in=w_in, conv_w=conv_w, attn_sinks=attn_sinks, attn_group_norm=attn_group_norm,
             conv_group_norm=conv_group_norm, w_out=w_out, post_mix_norm=post_mix_norm, pre_mlp_norm=pre_mlp_norm,
             w_up=w_up, w_down=w_down, post_mlp_norm=post_mlp_norm)
    m = dict(pre_mix_norm=m_pre_mix_norm, w_in=m_w_in, conv_w=m_conv_w, attn_sinks=m_attn_sinks,
             attn_group_norm=m_attn_group_norm, conv_group_norm=m_conv_group_norm, w_out=m_w_out,
             post_mix_norm=m_post_mix_norm, pre_mlp_norm=m_pre_mlp_norm, w_up=m_w_up, w_down=m_w_down,
             post_mlp_norm=m_post_mlp_norm)
    v = dict(pre_mix_norm=v_pre_mix_norm, w_in=v_w_in, conv_w=v_conv_w, attn_sinks=v_attn_sinks,
             attn_group_norm=v_attn_group_norm, conv_group_norm=v_conv_group_norm, w_out=v_w_out,
             post_mix_norm=v_post_mix_norm, pre_mlp_norm=v_pre_mlp_norm, w_up=v_w_up, w_down=v_w_down,
             post_mlp_norm=v_post_mlp_norm)
    core = lax.axis_index("c").astype(jnp.int32).reshape(1)
    xs, target = x[0], loss_target[0]
    rope = _rope_inputs(xs.shape[0])

    conv_pad = jnp.pad(conv_w[0], ((0, 8 - CONV_K), (0, 0)))
    wf_in, conv_all, hb_up, hb_down, hb_out = _gather_whole(w_in[0].T, (w_up[0], w_down[0], w_out[0]), conv_pad)
    conv_full = conv_all[:, :CONV_K, :].transpose(1, 0, 2).reshape(CONV_K, CONV_WIDTH)

    whole_up, early, late = (0, H_UP), (0, DOWN_EARLY_ROWS), (DOWN_EARLY_ROWS, H_DOWN - DOWN_EARLY_ROWS)
    *proj, wf_up, wf_out, wf_down = _in_proj(
        xs, pre_mix_norm, wf_in, rope,
        comm=_merge(_relay(hb_up, None, first=whole_up), _gather_first(hb_out), _relay(hb_down, None, first=early)))
    q, kd0, kd1, vd0, vd1, gb, gc, xin, hn = proj
    attn, wf_up, wf_out, wf_down = _attention_fwd(
        q, kd0, kd1, vd0, vd1, attn_sinks,
        comm=_merge(_relay(None, wf_up, second=whole_up), _gather_second(wf_out),
                    _relay(hb_down, wf_down, first=late, second=early)))
    mix, mixed, wf_up, wf_down = _mix_out(
        attn, gb, gc, xin, conv_full, attn_group_norm, conv_group_norm, wf_out,
        comm=_merge(_relay(None, wf_up, third=whole_up), _relay(None, wf_down, second=late, third=early, third_after=late)))
    up, hn2, dmlp, dup, dh, dmix, loss_sum, dg_post_mlp, dg_pre_mlp, dg_post_mix = _mlp_fwd_bwd(
        xs, mix, target, post_mix_norm, pre_mlp_norm, post_mlp_norm, wf_up, wf_down)

    n_k = _wgrad_grid(xs.shape[0], True, H_DOWN, with_rider=True)[3]
    g_down, dattn, dgb, dy, dg_attn, dg_conv, dconv_w = _wgrad(
        "wgrad_down", up, dmlp, per_chip=True, h_rows=H_DOWN, square_a=True,
        rider=_mix_bwd(dmix, attn, gb, gc, xin, conv_full, attn_group_norm, conv_group_norm, wf_out, n_k))
    g_up, got_down = _wgrad("wgrad_up", hn2, dup, per_chip=True, h_rows=H_UP, comm=_pair_send(g_down))
    p_down = _pair_sum("pair_sum_down", core, g_down, got_down)
    g_out, got_up = _wgrad("wgrad_out", mixed, dmix, per_chip=False, h_rows=H_OUT, comm=_pair_send(g_up))
    p_up = _pair_sum("pair_sum_up", core, g_up, got_up)
    dq, dk0, dk1, dv0, dv1, dsink, ex_down, ex_up, got_out = _attention_bwd(
        q, dattn, attn, kd0, kd1, vd0, vd1, attn_sinks,
        comm=_merge(_chip_exchange(p_down), _chip_exchange(p_up), _pair_send(g_out)))
    p_out = _pair_sum("pair_sum_out", core, g_out, got_out)
    dproj, grad_x, dg_pre_mix = _in_proj_bwd(dq, dk0, dk1, dv0, dv1, dgb, dy, gc, xin, conv_full, xs, dh, pre_mix_norm,
                                             wf_in, rope)
    g_in, ex_out = _wgrad("wgrad_in", dproj, hn, per_chip=False, h_rows=H_IN, comm=_chip_exchange(p_out))
    small = dict(pre_mix_norm=dg_pre_mix, conv_w=dconv_w, attn_sinks=dsink[:, :8], attn_group_norm=dg_attn,
                 conv_group_norm=dg_conv, post_mix_norm=dg_post_mix, pre_mlp_norm=dg_pre_mlp, post_mlp_norm=dg_post_mlp,
                 loss_sum=loss_sum)
    r_down, r_up, r_out, r_in, small_total = _tail_reduce(g_in, [ex_down, ex_up, ex_out], _pack_small(small))

    out_g, out_d, out_m, out_v = {}, {}, {}, {}
    res_up, res_down, res_out, res_in_t = _adamw_rows([
        (r_up, w_up[0], m_w_up[0], v_w_up[0]), (r_down, w_down[0], m_w_down[0], v_w_down[0]),
        (r_out, w_out[0], m_w_out[0], v_w_out[0]), (r_in, w_in[0].T, m_w_in[0].T, v_w_in[0].T)])
    for name, res in (("w_up", res_up), ("w_down", res_down), ("w_out", res_out), ("w_in", [t.T for t in res_in_t])):
        out_g[name], out_d[name], out_m[name], out_v[name] = res

    loss = small_total[7, 8] * (0.5 / D_MODEL)
    for out, part in zip((out_g, out_d, out_m, out_v), _adamw_small(small_total, w, m, v)):
        out.update(part)

    def shaped(d):
        return [d[n].reshape(w[n].shape) for n in WEIGHT_ORDER]

    return (loss, grad_x[None], *shaped(out_g), *shaped(out_d), *shaped(out_m), *shaped(out_v))
```
